```python
import jax, jax.numpy as jnp
from jax import lax
import numpy as np

D_MODEL = 1024
BATCH = 8
SEQ = 4096
DEPTH = 2

D_MIX = D_MODEL
A_WIDTH = D_MIX // 4
A_GROUPS = 4
A_GROUP_DIM = A_WIDTH // A_GROUPS
A_CHUNK = 128
B_WIDTH = D_MIX // 4
B_EXPAND = 64
B_HEADS = B_WIDTH // B_EXPAND
B_KDIM = B_EXPAND
B_VDIM = B_WIDTH // B_HEADS
B_FDIM = B_HEADS * B_KDIM
B_CHUNK = 128
C_WIDTH = D_MIX - A_WIDTH - B_WIDTH
C_HEAD_DIM = 64
C_HEADS = C_WIDTH // C_HEAD_DIM
C_BLOCK = 128
COL_WIDTHS = (A_WIDTH, A_WIDTH, A_WIDTH,
              B_FDIM, B_FDIM, B_WIDTH, B_WIDTH,
              C_WIDTH, C_WIDTH, C_WIDTH, C_WIDTH, C_HEADS)
D_IN = 3 * A_WIDTH + 2 * B_FDIM + 2 * B_WIDTH + 4 * C_WIDTH + C_HEADS
NORM_EPS = 1e-6
F_FLOOR = 1e-30

kernel_name = "hybrid_gmlp_hgrn2_fox_parallel_heads"


def _rmsnorm(x, g):
    xf = x.astype(jnp.float32)
    y = xf * lax.rsqrt(jnp.mean(xf * xf, axis=-1, keepdims=True) + NORM_EPS)
    return (y * g.astype(jnp.float32)).astype(x.dtype)


def _split_cols(proj):
    offsets = []
    acc = 0
    for w in COL_WIDTHS[:-1]:
        acc += w
        offsets.append(acc)
    return jnp.split(proj, offsets, axis=-1)


def _gmlp_mixer(u, v, ln_g, ln_b, w_s, b_s):
    bsz, seq, _ = u.shape
    nc = seq // A_CHUNK
    u = jax.nn.gelu(u)
    v = jax.nn.gelu(v).reshape(bsz, nc, A_CHUNK, A_GROUPS, A_GROUP_DIM)
    vf = v.astype(jnp.float32)
    mu = jnp.mean(vf, axis=-1, keepdims=True)
    var = jnp.mean(jnp.square(vf - mu), axis=-1, keepdims=True)
    vn = (vf - mu) * lax.rsqrt(var + NORM_EPS) * ln_g.astype(jnp.float32) + ln_b.astype(jnp.float32)
    causal = jnp.tril(jnp.ones((A_CHUNK, A_CHUNK), dtype=bool))
    w = jnp.where(causal[None], w_s.astype(jnp.float32), 0.0)
    mixed = jnp.einsum('gts,bnsgc->bntgc', w, vn)
    mixed = mixed + jnp.transpose(b_s.astype(jnp.float32))[None, None, :, :, None]
    return u * mixed.reshape(bsz, seq, A_WIDTH).astype(u.dtype)


def _hgrn2_mixer(q, f_logit, i, lb, onorm_g):
    bsz, seq, _ = q.shape
    nc = seq // B_CHUNK
    qf = jax.nn.silu(q.astype(jnp.float32)) * (B_KDIM ** -0.5)
    z = f_logit.astype(jnp.float32)
    f = lb + (1.0 - lb) * jax.nn.sigmoid(z)
    log_f = jnp.log(jnp.maximum(f, F_FLOOR))
    kf = (1.0 - lb) * jax.nn.sigmoid(-z)
    vf = i.astype(jnp.float32)

    def to_chunks(t, d):
        return t.reshape(bsz, nc, B_CHUNK, B_HEADS, d).transpose(1, 0, 3, 2, 4)

    qc, kc, gc = to_chunks(qf, B_KDIM), to_chunks(kf, B_KDIM), to_chunks(log_f, B_KDIM)
    vc = to_chunks(vf, B_VDIM)
    causal = jnp.tril(jnp.ones((B_CHUNK, B_CHUNK), dtype=bool))[None, None, :, :, None]

    def step(state, inp):
        qx, kx, vx, gx = inp
        b = jnp.cumsum(gx, axis=2)
        o_inter = jnp.einsum('bhtk,bhkv->bhtv', qx * jnp.exp(b), state)
        diff = b[:, :, :, None, :] - b[:, :, None, :, :]
        decay = jnp.exp(jnp.where(causal, diff, -jnp.inf))
        scores = jnp.einsum('bhtk,bhsk,bhtsk->bhts', qx, kx, decay)
        o_intra = jnp.einsum('bhts,bhsv->bhtv', scores, vx)
        b_last = b[:, :, -1:, :]
        new_state = (jnp.exp(b_last[:, :, 0, :])[..., None] * state
                     + jnp.einsum('bhsk,bhsv->bhkv', kx * jnp.exp(b_last - b), vx))
        return new_state, o_inter + o_intra

    state0 = jnp.zeros((bsz, B_HEADS, B_KDIM, B_VDIM), jnp.float32)
    _, ys = lax.scan(step, state0, (qc, kc, vc, gc))
    o = ys.transpose(1, 0, 3, 2, 4).reshape(bsz, seq, B_HEADS, B_VDIM)
    o = o * lax.rsqrt(jnp.mean(o * o, axis=-1, keepdims=True) + NORM_EPS) * onorm_g.astype(jnp.float32)
    return o.reshape(bsz, seq, B_WIDTH).astype(q.dtype)


def _fox_mixer(q, k, v, f_logit, b_f):
    bsz, seq, _ = q.shape

    def heads(t):
        return t.reshape(bsz, seq, C_HEADS, C_HEAD_DIM).transpose(0, 2, 1, 3)

    qh, kh, vh = heads(q), heads(k), heads(v)
    log_f = jax.nn.log_sigmoid(f_logit.astype(jnp.float32) + b_f.astype(jnp.float32))
    c = jnp.cumsum(jnp.transpose(log_f, (0, 2, 1)), axis=-1)
    scale = C_HEAD_DIM ** -0.5
    diag_mask = jnp.tril(jnp.ones((C_BLOCK, C_BLOCK), dtype=bool))
    outs = []
    for blk in range(seq // C_BLOCK):
        q0 = blk * C_BLOCK
        q1 = q0 + C_BLOCK
        s = jnp.einsum('bhqd,bhkd->bhqk', qh[:, :, q0:q1], kh[:, :, :q1]).astype(jnp.float32) * scale
        s = s + c[:, :, q0:q1, None] - c[:, :, None, :q1]
        mask = jnp.concatenate([jnp.ones((C_BLOCK, q0), dtype=bool), diag_mask], axis=1)
        s = jnp.where(mask[None, None], s, -jnp.inf)
        p = jax.nn.softmax(s, axis=-1)
        outs.append(jnp.einsum('bhqk,bhkd->bhqd', p.astype(vh.dtype), vh[:, :, :q1]))
    o = jnp.concatenate(outs, axis=2)
    return o.transpose(0, 2, 1, 3).reshape(bsz, seq, C_WIDTH)


def _fwd_setup_inputs(seed: int = 0) -> dict:
    key = jax.random.key(seed)
    ks = jax.random.split(key, 13)
    f32 = jnp.float32
    x = jax.random.normal(ks[0], (BATCH, SEQ, D_MODEL), f32)
    norm_g = 1.0 + 0.05 * jax.random.normal(ks[1], (DEPTH, D_MODEL), f32)
    w_in = jax.random.normal(ks[2], (DEPTH, D_MODEL, D_IN), f32) * D_MODEL ** -0.5
    w_out = jax.random.normal(ks[3], (DEPTH, D_MIX, D_MODEL), f32) * (D_MIX ** -0.5) * (2 * DEPTH) ** -0.5
    gmlp_ln_g = 1.0 + 0.05 * jax.random.normal(ks[4], (DEPTH, A_GROUPS, A_GROUP_DIM), f32)
    gmlp_ln_b = 0.02 * jax.random.normal(ks[5], (DEPTH, A_GROUPS, A_GROUP_DIM), f32)
    gmlp_w_s = jax.random.normal(ks[6], (DEPTH, A_GROUPS, A_CHUNK, A_CHUNK), f32) * A_CHUNK ** -0.5
    gmlp_b_s = 1.0 + 0.1 * jax.random.normal(ks[7], (DEPTH, A_GROUPS, A_CHUNK), f32)
    hgrn_lb = 0.1 * jax.random.normal(ks[8], (DEPTH, B_FDIM), f32)
    hgrn_onorm_g = 1.0 + 0.05 * jax.random.normal(ks[9], (DEPTH, B_VDIM), f32)
    fox_b_f = jax.random.uniform(ks[10], (DEPTH, C_HEADS), f32, 0.0, 3.0)
    final_norm_g = 1.0 + 0.05 * jax.random.normal(ks[11], (D_MODEL,), f32)
    return {"x": x, "norm_g": norm_g, "w_in": w_in, "w_out": w_out,
            "gmlp_ln_g": gmlp_ln_g, "gmlp_ln_b": gmlp_ln_b, "gmlp_w_s": gmlp_w_s,
            "gmlp_b_s": gmlp_b_s, "hgrn_lb": hgrn_lb, "hgrn_onorm_g": hgrn_onorm_g,
            "fox_b_f": fox_b_f, "final_norm_g": final_norm_g}


def _fwd_reference(x, norm_g, w_in, w_out, gmlp_ln_g, gmlp_ln_b, gmlp_w_s, gmlp_b_s,
              hgrn_lb, hgrn_onorm_g, fox_b_f, final_norm_g):
    p = jax.nn.softmax(hgrn_lb.astype(jnp.float32), axis=0)
    lb_all = jnp.clip(jnp.cumsum(p, axis=0) - p[0:1], 0.0, 1.0 - 1e-6)
    for layer in range(DEPTH):
        h = _rmsnorm(x, norm_g[layer])
        proj = jnp.einsum('bsd,de->bse', h, w_in[layer])
        (a_u, a_v, a_z, b_q, b_fl, b_i, b_z,
         c_q, c_k, c_v, c_z, c_fl) = _split_cols(proj)
        y_a = _gmlp_mixer(a_u, a_v, gmlp_ln_g[layer], gmlp_ln_b[layer],
                          gmlp_w_s[layer], gmlp_b_s[layer]) * jax.nn.silu(a_z)
        y_b = _hgrn2_mixer(b_q, b_fl, b_i, lb_all[layer], hgrn_onorm_g[layer]) * jax.nn.silu(b_z)
        y_c = _fox_mixer(c_q, c_k, c_v, c_fl, fox_b_f[layer]) * jax.nn.silu(c_z)
        y = jnp.concatenate([y_a, y_b, y_c], axis=-1)
        x = x + jnp.einsum('bse,ed->bsd', y, w_out[layer])
    return _rmsnorm(x, final_norm_g)


import jax as _jax
import jax.numpy as _jnp

TWIN_FORMAT = 'train_step'
FWD_PARAMS = ['x', 'norm_g', 'w_in', 'w_out', 'gmlp_ln_g', 'gmlp_ln_b', 'gmlp_w_s', 'gmlp_b_s', 'hgrn_lb', 'hgrn_onorm_g', 'fox_b_f', 'final_norm_g']
TWIN_WEIGHTS = ['norm_g', 'w_in', 'w_out', 'gmlp_ln_g', 'gmlp_ln_b', 'gmlp_w_s', 'gmlp_b_s', 'hgrn_lb', 'hgrn_onorm_g', 'fox_b_f', 'final_norm_g']
TWIN_DIFF_INPUT = 'x'
TWIN_INPUTS = ['x', 'norm_g', 'w_in', 'w_out', 'gmlp_ln_g', 'gmlp_ln_b', 'gmlp_w_s', 'gmlp_b_s', 'hgrn_lb', 'hgrn_onorm_g', 'fox_b_f', 'final_norm_g', 'loss_target', 'm_norm_g', 'm_w_in', 'm_w_out', 'm_gmlp_ln_g', 'm_gmlp_ln_b', 'm_gmlp_w_s', 'm_gmlp_b_s', 'm_hgrn_lb', 'm_hgrn_onorm_g', 'm_fox_b_f', 'm_final_norm_g', 'v_norm_g', 'v_w_in', 'v_w_out', 'v_gmlp_ln_g', 'v_gmlp_ln_b', 'v_gmlp_w_s', 'v_gmlp_b_s', 'v_hgrn_lb', 'v_hgrn_onorm_g', 'v_fox_b_f', 'v_final_norm_g']
TWIN_OUTPUTS = ['loss', 'grad_x', 'grad_norm_g', 'grad_w_in', 'grad_w_out', 'grad_gmlp_ln_g', 'grad_gmlp_ln_b', 'grad_gmlp_w_s', 'grad_gmlp_b_s', 'grad_hgrn_lb', 'grad_hgrn_onorm_g', 'grad_fox_b_f', 'grad_final_norm_g', 'delta_norm_g', 'delta_w_in', 'delta_w_out', 'delta_gmlp_ln_g', 'delta_gmlp_ln_b', 'delta_gmlp_w_s', 'delta_gmlp_b_s', 'delta_hgrn_lb', 'delta_hgrn_onorm_g', 'delta_fox_b_f', 'delta_final_norm_g', 'new_m_norm_g', 'new_m_w_in', 'new_m_w_out', 'new_m_gmlp_ln_g', 'new_m_gmlp_ln_b', 'new_m_gmlp_w_s', 'new_m_gmlp_b_s', 'new_m_hgrn_lb', 'new_m_hgrn_onorm_g', 'new_m_fox_b_f', 'new_m_final_norm_g', 'new_v_norm_g', 'new_v_w_in', 'new_v_w_out', 'new_v_gmlp_ln_g', 'new_v_gmlp_ln_b', 'new_v_gmlp_w_s', 'new_v_gmlp_b_s', 'new_v_hgrn_lb', 'new_v_hgrn_onorm_g', 'new_v_fox_b_f', 'new_v_final_norm_g']
TWIN_LEAF_KINDS = {'loss': 'loss', 'grad_x': 'grad_x', 'grad_norm_g': 'grad_w', 'grad_w_in': 'grad_w', 'grad_w_out': 'grad_w', 'grad_gmlp_ln_g': 'grad_w', 'grad_gmlp_ln_b': 'grad_w', 'grad_gmlp_w_s': 'grad_w', 'grad_gmlp_b_s': 'grad_w', 'grad_hgrn_lb': 'grad_w', 'grad_hgrn_onorm_g': 'grad_w', 'grad_fox_b_f': 'grad_w', 'grad_final_norm_g': 'grad_w', 'delta_norm_g': 'delta_w', 'delta_w_in': 'delta_w', 'delta_w_out': 'delta_w', 'delta_gmlp_ln_g': 'delta_w', 'delta_gmlp_ln_b': 'delta_w', 'delta_gmlp_w_s': 'delta_w', 'delta_gmlp_b_s': 'delta_w', 'delta_hgrn_lb': 'delta_w', 'delta_hgrn_onorm_g': 'delta_w', 'delta_fox_b_f': 'delta_w', 'delta_final_norm_g': 'delta_w', 'new_m_norm_g': 'new_m', 'new_m_w_in': 'new_m', 'new_m_w_out': 'new_m', 'new_m_gmlp_ln_g': 'new_m', 'new_m_gmlp_ln_b': 'new_m', 'new_m_gmlp_w_s': 'new_m', 'new_m_gmlp_b_s': 'new_m', 'new_m_hgrn_lb': 'new_m', 'new_m_hgrn_onorm_g': 'new_m', 'new_m_fox_b_f': 'new_m', 'new_m_final_norm_g': 'new_m', 'new_v_norm_g': 'new_v', 'new_v_w_in': 'new_v', 'new_v_w_out': 'new_v', 'new_v_gmlp_ln_g': 'new_v', 'new_v_gmlp_ln_b': 'new_v', 'new_v_gmlp_w_s': 'new_v', 'new_v_gmlp_b_s': 'new_v', 'new_v_hgrn_lb': 'new_v', 'new_v_hgrn_onorm_g': 'new_v', 'new_v_fox_b_f': 'new_v', 'new_v_final_norm_g': 'new_v'}


def _forward(args):
    return _fwd_reference(*[args[k] for k in FWD_PARAMS])


def _output_shape():
    def fwd():
        inp = _fwd_setup_inputs(0)
        return _fwd_reference(*[inp[k] for k in FWD_PARAMS])
    out = _jax.eval_shape(fwd)
    return out.shape, out.dtype

N_MICROBATCH = 1
ADAM_LR = 0.001
ADAM_B1 = 0.9
ADAM_B2 = 0.999
ADAM_EPS = 1e-08
ADAM_WD = 0.01
ADAM_STEP = 10
PER_EXAMPLE_BATCH_AXIS = {'x': 0, 'loss_target': 0}
SHARED_INPUTS = []
_WEIGHT_DTYPES = {'norm_g': _jnp.float32, 'w_in': _jnp.float32, 'w_out': _jnp.float32, 'gmlp_ln_g': _jnp.float32, 'gmlp_ln_b': _jnp.float32, 'gmlp_w_s': _jnp.float32, 'gmlp_b_s': _jnp.float32, 'hgrn_lb': _jnp.float32, 'hgrn_onorm_g': _jnp.float32, 'fox_b_f': _jnp.float32, 'final_norm_g': _jnp.float32}
MOMENT_SCALE = {'norm_g': 5.944983e-02, 'w_in': 3.106874e-02, 'w_out': 7.612770e-02, 'gmlp_ln_g': 2.375198e-02, 'gmlp_ln_b': 2.635840e-02, 'gmlp_w_s': 1.713072e-02, 'gmlp_b_s': 2.452079e-02, 'hgrn_lb': 7.425219e-03, 'hgrn_onorm_g': 1.084526e-01, 'fox_b_f': 1.398347e-01, 'final_norm_g': 3.206456e+01}


def _to_microbatches(a, axis):
    t = _jnp.moveaxis(a, axis, 0)
    t = t.reshape((N_MICROBATCH, t.shape[0] // N_MICROBATCH) + t.shape[1:])
    return _jnp.moveaxis(t, 1, axis + 1)


def setup_inputs(seed: int = 0) -> dict:
    inp = _fwd_setup_inputs(seed)
    key = _jax.random.fold_in(_jax.random.key(seed), 7919)
    shape, _ = _output_shape()
    out = dict(inp)
    out["loss_target"] = _jax.random.normal(_jax.random.fold_in(key, 0), shape, _jnp.float32)
    for i, name in enumerate(TWIN_WEIGHTS):
        w = inp[name].astype(_jnp.float32)
        if MOMENT_SCALE is None:
            s = _jnp.sqrt(_jnp.mean(_jnp.square(w)) + 1e-30)
        else:
            s = MOMENT_SCALE[name]
        km, kv = _jax.random.split(_jax.random.fold_in(key, i + 1))
        out[name] = w
        out["m_" + name] = s * _jax.random.normal(km, w.shape, _jnp.float32)
        out["v_" + name] = (s * s) * _jax.random.uniform(kv, w.shape, _jnp.float32, 0.5, 1.5)
    if N_MICROBATCH > 1:
        for name, axis in PER_EXAMPLE_BATCH_AXIS.items():
            out[name] = _to_microbatches(out[name], axis)
    return {'x': out['x'], 'norm_g': out['norm_g'], 'w_in': out['w_in'], 'w_out': out['w_out'], 'gmlp_ln_g': out['gmlp_ln_g'], 'gmlp_ln_b': out['gmlp_ln_b'], 'gmlp_w_s': out['gmlp_w_s'], 'gmlp_b_s': out['gmlp_b_s'], 'hgrn_lb': out['hgrn_lb'], 'hgrn_onorm_g': out['hgrn_onorm_g'], 'fox_b_f': out['fox_b_f'], 'final_norm_g': out['final_norm_g'], 'loss_target': out['loss_target'], 'm_norm_g': out['m_norm_g'], 'm_w_in': out['m_w_in'], 'm_w_out': out['m_w_out'], 'm_gmlp_ln_g': out['m_gmlp_ln_g'], 'm_gmlp_ln_b': out['m_gmlp_ln_b'], 'm_gmlp_w_s': out['m_gmlp_w_s'], 'm_gmlp_b_s': out['m_gmlp_b_s'], 'm_hgrn_lb': out['m_hgrn_lb'], 'm_hgrn_onorm_g': out['m_hgrn_onorm_g'], 'm_fox_b_f': out['m_fox_b_f'], 'm_final_norm_g': out['m_final_norm_g'], 'v_norm_g': out['v_norm_g'], 'v_w_in': out['v_w_in'], 'v_w_out': out['v_w_out'], 'v_gmlp_ln_g': out['v_gmlp_ln_g'], 'v_gmlp_ln_b': out['v_gmlp_ln_b'], 'v_gmlp_w_s': out['v_gmlp_w_s'], 'v_gmlp_b_s': out['v_gmlp_b_s'], 'v_hgrn_lb': out['v_hgrn_lb'], 'v_hgrn_onorm_g': out['v_hgrn_onorm_g'], 'v_fox_b_f': out['v_fox_b_f'], 'v_final_norm_g': out['v_final_norm_g']}


def _loss(weights, diff, rest, loss_target):
    with _jax.named_scope("forward"):
        args = {**rest, TWIN_DIFF_INPUT: diff, **{k: w.astype(_WEIGHT_DTYPES[k]) for k, w in weights.items()}}
        y = _forward(args)
    with _jax.named_scope("loss_head"):
        err = _jnp.square(y.astype(_jnp.float32) - loss_target)
        return 0.5 * _jnp.sum(_jnp.mean(err, axis=-1)) if err.ndim else 0.5 * err


def _adamw(w, g, m, v):
    m = ADAM_B1 * m + (1.0 - ADAM_B1) * g
    v = ADAM_B2 * v + (1.0 - ADAM_B2) * _jnp.square(g)
    m_hat = m / (1.0 - ADAM_B1 ** ADAM_STEP)
    v_hat = v / (1.0 - ADAM_B2 ** ADAM_STEP)
    delta = -ADAM_LR * (m_hat / (_jnp.sqrt(v_hat) + ADAM_EPS) + ADAM_WD * w)
    return delta, m, v


def reference(x, norm_g, w_in, w_out, gmlp_ln_g, gmlp_ln_b, gmlp_w_s, gmlp_b_s, hgrn_lb, hgrn_onorm_g, fox_b_f, final_norm_g, loss_target, m_norm_g, m_w_in, m_w_out, m_gmlp_ln_g, m_gmlp_ln_b, m_gmlp_w_s, m_gmlp_b_s, m_hgrn_lb, m_hgrn_onorm_g, m_fox_b_f, m_final_norm_g, v_norm_g, v_w_in, v_w_out, v_gmlp_ln_g, v_gmlp_ln_b, v_gmlp_w_s, v_gmlp_b_s, v_hgrn_lb, v_hgrn_onorm_g, v_fox_b_f, v_final_norm_g):
    given = dict(x=x, norm_g=norm_g, w_in=w_in, w_out=w_out, gmlp_ln_g=gmlp_ln_g, gmlp_ln_b=gmlp_ln_b, gmlp_w_s=gmlp_w_s, gmlp_b_s=gmlp_b_s, hgrn_lb=hgrn_lb, hgrn_onorm_g=hgrn_onorm_g, fox_b_f=fox_b_f, final_norm_g=final_norm_g, loss_target=loss_target, m_norm_g=m_norm_g, m_w_in=m_w_in, m_w_out=m_w_out, m_gmlp_ln_g=m_gmlp_ln_g, m_gmlp_ln_b=m_gmlp_ln_b, m_gmlp_w_s=m_gmlp_w_s, m_gmlp_b_s=m_gmlp_b_s, m_hgrn_lb=m_hgrn_lb, m_hgrn_onorm_g=m_hgrn_onorm_g, m_fox_b_f=m_fox_b_f, m_final_norm_g=m_final_norm_g, v_norm_g=v_norm_g, v_w_in=v_w_in, v_w_out=v_w_out, v_gmlp_ln_g=v_gmlp_ln_g, v_gmlp_ln_b=v_gmlp_ln_b, v_gmlp_w_s=v_gmlp_w_s, v_gmlp_b_s=v_gmlp_b_s, v_hgrn_lb=v_hgrn_lb, v_hgrn_onorm_g=v_hgrn_onorm_g, v_fox_b_f=v_fox_b_f, v_final_norm_g=v_final_norm_g)
    weights = {n: given[n] for n in TWIN_WEIGHTS}
    shared = {n: given[n] for n in SHARED_INPUTS}
    per_example = {n: given[n] for n in ['x']}
    grad_fn = _jax.value_and_grad(_loss, argnums=(0, 1))

    def one_microbatch(ex, loss_target):
        ex = dict(ex)
        diff = ex.pop(TWIN_DIFF_INPUT)
        return grad_fn(weights, diff, {**shared, **ex}, loss_target)

    if N_MICROBATCH == 1:
        loss, (grad_w, grad_x) = one_microbatch(per_example, given["loss_target"])
    else:
        def body(carry, xs):
            loss_sum, grad_sum = carry
            l_k, (gw_k, gx_k) = one_microbatch(xs[0], xs[1])
            with _jax.named_scope("update"):
                return (loss_sum + l_k, _jax.tree.map(_jnp.add, grad_sum, gw_k)), gx_k

        init = (_jnp.zeros((), _jnp.float32), _jax.tree.map(_jnp.zeros_like, weights))
        (loss, grad_w), grad_x = _jax.lax.scan(body, init, (per_example, given["loss_target"]))
    with _jax.named_scope("update"):
        delta_w, new_m, new_v = {}, {}, {}
        for n in TWIN_WEIGHTS:
            delta_w[n], new_m[n], new_v[n] = _adamw(weights[n], grad_w[n], given["m_" + n], given["v_" + n])
    return (loss, grad_x, *[grad_w[n] for n in TWIN_WEIGHTS], *[delta_w[n] for n in TWIN_WEIGHTS],
            *[new_m[n] for n in TWIN_WEIGHTS], *[new_v[n] for n in TWIN_WEIGHTS])
```

```python
import functools
import math

import jax
import jax.numpy as jnp
from jax import lax
from jax.experimental import pallas as pl
from jax.experimental.pallas import tpu as pltpu

F32 = jnp.float32
BF16 = jnp.bfloat16
SDS = jax.ShapeDtypeStruct
MESH_ID = pl.DeviceIdType.MESH

D_MODEL = 1024
DEPTH = 2
A_WIDTH = 256
A_GROUPS = 4
B_WIDTH = 256
C_WIDTH = 512
C_HEADS = 8
D_IN = 3848
D_IN_PAD = 4096
CHUNK = 128
SUB = 16
SUB_SHIFT = 4
NORM_EPS = 1e-6
F_FLOOR = 1e-30
COL_AU, COL_AV, COL_AZ = 0, 256, 512
COL_BQ, COL_BF, COL_BI, COL_BZ = 768, 1024, 1280, 1536
COL_CQ, COL_CK, COL_CV, COL_CZ, COL_CF = 1792, 2304, 2816, 3328, 3840
HEAD_LANES = 128
Q_SCALE = 0.125
ADAM_LR, ADAM_B1, ADAM_B2, ADAM_EPS, ADAM_WD, ADAM_STEP = 0.001, 0.9, 0.999, 1e-08, 0.01, 10
ADAM_C1 = 1.0 - ADAM_B1 ** ADAM_STEP
ADAM_C2 = 1.0 - ADAM_B2 ** ADAM_STEP
VMEM_LIMIT = 56 * 1024 * 1024
N_CHIPS = 4
N_DEV = 8

SMALL_PARAMS = (
    ("norm_g", (DEPTH, D_MODEL)), ("gmlp_ln_g", (DEPTH, 4, 64)), ("gmlp_ln_b", (DEPTH, 4, 64)),
    ("gmlp_w_s", (DEPTH, 4, 128, 128)), ("gmlp_b_s", (DEPTH, 4, 128)), ("hgrn_lb", (DEPTH, 256)),
    ("hgrn_onorm_g", (DEPTH, 64)), ("fox_b_f", (DEPTH, 8)), ("final_norm_g", (D_MODEL,)),
)


def _tile(n, pref):
    t = min(n, pref)
    assert n % t == 0, (n, pref)
    return t


def _params(*sem):
    return pltpu.CompilerParams(dimension_semantics=sem, vmem_limit_bytes=VMEM_LIMIT)


def _dot(a, b):
    return jnp.dot(a, b, preferred_element_type=F32)


def _dot_nt(a, b):
    return lax.dot_general(a, b, (((1,), (1,)), ((), ())), preferred_element_type=F32)


def _dot_tn(a, b):
    return lax.dot_general(a, b, (((0,), (0,)), ((), ())), preferred_element_type=F32)


def _split3(x):
    hi = x.astype(BF16)
    r = x - hi.astype(F32)
    mid = r.astype(BF16)
    lo = (r - mid.astype(F32)).astype(BF16)
    return hi, mid, lo


def _dot3_left(c, x):
    hi, mid, lo = _split3(x)
    return _dot(c, hi) + _dot(c, mid) + _dot(c, lo)


def _sigmoid(x):
    return jax.nn.sigmoid(x)


def _silu_and_grad(x):
    s = _sigmoid(x)
    return x * s, s * (1.0 + x * (1.0 - s))


_GELU_C = math.sqrt(2.0 / math.pi)


def _gelu_and_grad(x):
    inner = _GELU_C * (x + 0.044715 * x * x * x)
    t = jnp.tanh(inner)
    y = 0.5 * x * (1.0 + t)
    dy = 0.5 * (1.0 + t) + 0.5 * x * (1.0 - t * t) * _GELU_C * (1.0 + 3.0 * 0.044715 * x * x)
    return y, dy


def _lane(shape):
    return lax.broadcasted_iota(jnp.int32, shape, 1)


def _row(shape):
    return lax.broadcasted_iota(jnp.int32, shape, 0)


def _gsum64(x):
    lo = _lane(x.shape) < 64
    s0 = jnp.sum(jnp.where(lo, x, 0.0), axis=-1, keepdims=True)
    s1 = jnp.sum(jnp.where(lo, 0.0, x), axis=-1, keepdims=True)
    return jnp.where(lo, s0, s1)


def _block_diag64(dtype=BF16):
    r, c = _row((128, 128)), _lane((128, 128))
    return jnp.where((r >> 6) == (c >> 6), 1.0, 0.0).astype(dtype)


def _inproj(x, g, w, tag):
    T, D = x.shape
    DP = w.shape[1]
    tm, tn = _tile(T, 512), _tile(DP, 1024)

    def body(x_ref, g_ref, w_ref, h_ref, p_ref):
        @pl.when(pl.program_id(1) == 0)
        def _():
            xv = x_ref[...]
            r = lax.rsqrt(jnp.mean(xv * xv, axis=-1, keepdims=True) + NORM_EPS)
            h_ref[...] = (xv * r * g_ref[...]).astype(BF16)

        p_ref[...] = _dot(h_ref[...], w_ref[...])

    return pl.pallas_call(
        body, name=f"inproj_{tag}", grid=(T // tm, DP // tn),
        in_specs=[pl.BlockSpec((tm, D), lambda i, j: (i, 0)), pl.BlockSpec((1, D), lambda i, j: (0, 0)),
                  pl.BlockSpec((D, tn), lambda i, j: (0, j))],
        out_specs=[pl.BlockSpec((tm, D), lambda i, j: (i, 0)), pl.BlockSpec((tm, tn), lambda i, j: (i, j))],
        out_shape=[SDS((T, D), BF16), SDS((T, DP), F32)],
        compiler_params=_params("parallel", "arbitrary"),
    )(x, g, w)


def _outproj(x, ya, yb, yc, wo, tag):
    T, D = x.shape
    tm = _tile(T, 512)

    def body(x_ref, ya_ref, yb_ref, yc_ref, wo_ref, o_ref):
        acc = x_ref[...] + _dot(ya_ref[...], wo_ref[0:A_WIDTH, :])
        acc = acc + _dot(yb_ref[...], wo_ref[A_WIDTH:A_WIDTH + B_WIDTH, :])
        o_ref[...] = acc + _dot(yc_ref[...], wo_ref[A_WIDTH + B_WIDTH:, :])

    row = lambda w: pl.BlockSpec((tm, w), lambda i: (i, 0))
    return pl.pallas_call(
        body, name=f"outproj_{tag}", grid=(T // tm,),
        in_specs=[row(D), row(A_WIDTH), row(B_WIDTH), row(C_WIDTH), pl.BlockSpec(wo.shape, lambda i: (0, 0))],
        out_specs=row(D), out_shape=SDS((T, D), F32), compiler_params=_params("parallel"),
    )(x, ya, yb, yc, wo)


def _outproj_bwd(dx, ya, yb, yc, wo, tag):
    T, D = dx.shape
    DM = wo.shape[0]
    tm = _tile(T, 512)

    def body(dx_ref, ya_ref, yb_ref, yc_ref, wo_ref, dy_ref, dwo_ref):
        @pl.when(pl.program_id(0) == 0)
        def _():
            dwo_ref[...] = jnp.zeros_like(dwo_ref)

        dxb = dx_ref[...].astype(BF16)
        dy_ref[...] = _dot_nt(dxb, wo_ref[...])
        dwo_ref[0:A_WIDTH, :] += _dot_tn(ya_ref[...], dxb)
        dwo_ref[A_WIDTH:A_WIDTH + B_WIDTH, :] += _dot_tn(yb_ref[...], dxb)
        dwo_ref[A_WIDTH + B_WIDTH:, :] += _dot_tn(yc_ref[...], dxb)

    row = lambda w: pl.BlockSpec((tm, w), lambda i: (i, 0))
    return pl.pallas_call(
        body, name=f"outproj_bwd_{tag}", grid=(T // tm,),
        in_specs=[row(D), row(A_WIDTH), row(B_WIDTH), row(C_WIDTH), pl.BlockSpec(wo.shape, lambda i: (0, 0))],
        out_specs=[row(DM), pl.BlockSpec((DM, D), lambda i: (0, 0))],
        out_shape=[SDS((T, DM), F32), SDS((DM, D), F32)], compiler_params=_params("arbitrary"),
    )(dx, ya, yb, yc, wo)


def _dw_in(h, dproj, tag):
    T, D = h.shape
    DP = dproj.shape[1]
    tm, tn = _tile(T, 512), _tile(DP, 1024)

    def body(h_ref, dp_ref, dw_ref):
        @pl.when(pl.program_id(1) == 0)
        def _():
            dw_ref[...] = jnp.zeros_like(dw_ref)

        dw_ref[...] += _dot_tn(h_ref[...], dp_ref[...])

    return pl.pallas_call(
        body, name=f"dw_in_{tag}", grid=(DP // tn, T // tm),
        in_specs=[pl.BlockSpec((tm, D), lambda j, i: (i, 0)), pl.BlockSpec((tm, tn), lambda j, i: (i, j))],
        out_specs=pl.BlockSpec((D, tn), lambda j, i: (0, j)), out_shape=SDS((D, DP), F32),
        compiler_params=_params("parallel", "arbitrary"),
    )(h, dproj)


def _dx_in(x, g, dres, dproj, w, tag):
    T, D = x.shape
    DP = w.shape[1]
    tm, tk = _tile(T, 512), _tile(DP, 1024)
    nk = DP // tk

    def body(x_ref, g_ref, dres_ref, dp_ref, w_ref, dx_ref, dg_ref, acc_ref):
        i, k = pl.program_id(0), pl.program_id(1)

        @pl.when((i == 0) & (k == 0))
        def _():
            dg_ref[...] = jnp.zeros_like(dg_ref)

        @pl.when(k == 0)
        def _():
            acc_ref[...] = jnp.zeros_like(acc_ref)

        acc_ref[...] += _dot_nt(dp_ref[...], w_ref[...])

        @pl.when(k == nk - 1)
        def _():
            xv = x_ref[...]
            r = lax.rsqrt(jnp.mean(xv * xv, axis=-1, keepdims=True) + NORM_EPS)
            xh = xv * r
            dh = acc_ref[...]
            dg_ref[...] += jnp.sum(dh * xh, axis=0, keepdims=True)
            dxh = dh * g_ref[...]
            dx_ref[...] = dres_ref[...] + r * (dxh - xh * jnp.mean(dxh * xh, axis=-1, keepdims=True))

    return pl.pallas_call(
        body, name=f"dx_in_{tag}", grid=(T // tm, nk),
        in_specs=[pl.BlockSpec((tm, D), lambda i, k: (i, 0)), pl.BlockSpec((1, D), lambda i, k: (0, 0)),
                  pl.BlockSpec((tm, D), lambda i, k: (i, 0)), pl.BlockSpec((tm, tk), lambda i, k: (i, k)),
                  pl.BlockSpec((D, tk), lambda i, k: (0, k))],
        out_specs=[pl.BlockSpec((tm, D), lambda i, k: (i, 0)), pl.BlockSpec((1, D), lambda i, k: (0, 0))],
        out_shape=[SDS((T, D), F32), SDS((1, D), F32)],
        scratch_shapes=[pltpu.VMEM((tm, D), F32)], compiler_params=_params("arbitrary", "arbitrary"),
    )(x, g, dres, dproj, w)


def _loss_head(x, g, tgt):
    T, D = x.shape
    tm = _tile(T, 512)

    def body(x_ref, g_ref, t_ref, dx_ref, loss_ref, dg_ref):
        @pl.when(pl.program_id(0) == 0)
        def _():
            loss_ref[...] = jnp.zeros_like(loss_ref)
            dg_ref[...] = jnp.zeros_like(dg_ref)

        xv = x_ref[...]
        r = lax.rsqrt(jnp.mean(xv * xv, axis=-1, keepdims=True) + NORM_EPS)
        xh = xv * r
        gv = g_ref[...]
        err = xh * gv - t_ref[...]
        tok = jnp.mean(err * err, axis=-1, keepdims=True)
        loss_ref[...] += 0.5 * jnp.sum(tok, axis=0, keepdims=True)
        dy = err * (1.0 / D)
        dg_ref[...] += jnp.sum(dy * xh, axis=0, keepdims=True)
        dxh = dy * gv
        dx_ref[...] = r * (dxh - xh * jnp.mean(dxh * xh, axis=-1, keepdims=True))

    row = pl.BlockSpec((tm, D), lambda i: (i, 0))
    return pl.pallas_call(
        body, name="loss_head", grid=(T // tm,),
        in_specs=[row, pl.BlockSpec((1, D), lambda i: (0, 0)), row],
        out_specs=[row, pl.BlockSpec((1, 128), lambda i: (0, 0)), pl.BlockSpec((1, D), lambda i: (0, 0))],
        out_shape=[SDS((T, D), F32), SDS((1, 128), F32), SDS((1, D), F32)], compiler_params=_params("arbitrary"),
    )(x, g, tgt)


def _gmlp_core(u, v, lng, lnb, wm_ref, bst_ref, pair):
    ug, dug = _gelu_and_grad(u)
    vg, dvg = _gelu_and_grad(v)
    mu = _gsum64(vg) * (1.0 / 64)
    d = vg - mu
    var = _gsum64(d * d) * (1.0 / 64)
    rstd = lax.rsqrt(var + NORM_EPS)
    xh = d * rstd
    vn = xh * lng + lnb
    vnb = vn.astype(BF16)
    lo = _lane(u.shape) < 64
    g0, g1 = 2 * pair, 2 * pair + 1
    mixed = jnp.where(lo, _dot(wm_ref[g0], vnb) + bst_ref[:, g0:g0 + 1], _dot(wm_ref[g1], vnb) + bst_ref[:, g1:g1 + 1])
    return ug, dug, dvg, rstd, xh, vnb, mixed, lo


def _gmlp_fwd(proj, lng, lnb, wm, bst, tag):
    T = proj.shape[0]

    def body(u_ref, v_ref, z_ref, lng_ref, lnb_ref, wm_ref, bst_ref, y_ref):
        for pair in range(2):
            sl = slice(128 * pair, 128 * pair + 128)
            ug, _, _, _, _, _, mixed, _ = _gmlp_core(u_ref[:, sl], v_ref[:, sl], lng_ref[:, sl], lnb_ref[:, sl],
                                                     wm_ref, bst_ref, pair)
            sz, _ = _silu_and_grad(z_ref[:, sl])
            y_ref[:, sl] = (ug * mixed * sz).astype(BF16)

    col = lambda c: pl.BlockSpec((CHUNK, A_WIDTH), lambda i, c=c: (i, c // A_WIDTH))
    full = lambda a: pl.BlockSpec(a.shape, lambda i, n=a.ndim: (0,) * n)
    return pl.pallas_call(
        body, name=f"gmlp_fwd_{tag}", grid=(T // CHUNK,),
        in_specs=[col(COL_AU), col(COL_AV), col(COL_AZ), full(lng), full(lnb), full(wm), full(bst)],
        out_specs=pl.BlockSpec((CHUNK, A_WIDTH), lambda i: (i, 0)), out_shape=SDS((T, A_WIDTH), BF16),
        compiler_params=_params("parallel"),
    )(proj, proj, proj, lng, lnb, wm, bst)


def _gmlp_bwd(proj, dy, lng, lnb, wm, wmt, bst, tag):
    T = proj.shape[0]
    n = T // CHUNK

    def body(u_ref, v_ref, z_ref, dy_ref, lng_ref, lnb_ref, wm_ref, wmt_ref, bst_ref,
             da_ref, dwm_ref, dbst_ref, dlng_ref, dlnb_ref):
        @pl.when(pl.program_id(0) == 0)
        def _():
            dwm_ref[...] = jnp.zeros_like(dwm_ref)
            dbst_ref[...] = jnp.zeros_like(dbst_ref)
            dlng_ref[...] = jnp.zeros_like(dlng_ref)
            dlnb_ref[...] = jnp.zeros_like(dlnb_ref)

        lane = _lane((CHUNK, 128))
        dbst = dbst_ref[...]
        for pair in range(2):
            sl = slice(128 * pair, 128 * pair + 128)
            lng_p = lng_ref[:, sl]
            ug, dug, dvg, rstd, xh, vnb, mixed, lo = _gmlp_core(u_ref[:, sl], v_ref[:, sl], lng_p, lnb_ref[:, sl],
                                                                wm_ref, bst_ref, pair)
            sz, dsz = _silu_and_grad(z_ref[:, sl])
            dyv = dy_ref[:, sl]
            out = ug * mixed
            dz = dyv * out * dsz
            dout = dyv * sz
            du = dout * mixed * dug
            dmix = dout * ug
            g0, g1 = 2 * pair, 2 * pair + 1
            dm0 = jnp.where(lo, dmix, 0.0)
            dm1 = jnp.where(lo, 0.0, dmix)
            dbst = dbst + jnp.where(lane == g0, jnp.sum(dm0, axis=-1, keepdims=True), 0.0)
            dbst = dbst + jnp.where(lane == g1, jnp.sum(dm1, axis=-1, keepdims=True), 0.0)
            dwm_ref[g0] += _dot_nt(dm0.astype(BF16), vnb)
            dwm_ref[g1] += _dot_nt(dm1.astype(BF16), vnb)
            dmb = dmix.astype(BF16)
            dvn = jnp.where(lo, _dot(wmt_ref[g0], dmb), _dot(wmt_ref[g1], dmb))
            dlng_ref[:, sl] += jnp.sum(dvn * xh, axis=0, keepdims=True)
            dlnb_ref[:, sl] += jnp.sum(dvn, axis=0, keepdims=True)
            dxh = dvn * lng_p
            m1 = _gsum64(dxh) * (1.0 / 64)
            m2 = _gsum64(dxh * xh) * (1.0 / 64)
            dv = rstd * (dxh - m1 - xh * m2) * dvg
            da_ref[:, COL_AU + 128 * pair:COL_AU + 128 * pair + 128] = du.astype(BF16)
            da_ref[:, COL_AV + 128 * pair:COL_AV + 128 * pair + 128] = dv.astype(BF16)
            da_ref[:, COL_AZ + 128 * pair:COL_AZ + 128 * pair + 128] = dz.astype(BF16)
        dbst_ref[...] = dbst

        @pl.when(pl.program_id(0) == n - 1)
        def _():
            causal = _lane((CHUNK, CHUNK)) <= _row((CHUNK, CHUNK))
            for g in range(A_GROUPS):
                dwm_ref[g] = jnp.where(causal, dwm_ref[g], 0.0)

    col = lambda c: pl.BlockSpec((CHUNK, A_WIDTH), lambda i, c=c: (i, c // A_WIDTH))
    full = lambda a: pl.BlockSpec(a.shape, lambda i, n=a.ndim: (0,) * n)
    acc = lambda s: pl.BlockSpec(s, lambda i, n=len(s): (0,) * n)
    return pl.pallas_call(
        body, name=f"gmlp_bwd_{tag}", grid=(n,),
        in_specs=[col(COL_AU), col(COL_AV), col(COL_AZ), pl.BlockSpec((CHUNK, A_WIDTH), lambda i: (i, 0)),
                  full(lng), full(lnb), full(wm), full(wmt), full(bst)],
        out_specs=[pl.BlockSpec((CHUNK, 3 * A_WIDTH), lambda i: (i, 0)), acc((A_GROUPS, CHUNK, CHUNK)),
                   acc((CHUNK, 128)), acc((1, A_WIDTH)), acc((1, A_WIDTH))],
        out_shape=[SDS((T, 3 * A_WIDTH), BF16), SDS((A_GROUPS, CHUNK, CHUNK), F32), SDS((CHUNK, 128), F32),
                   SDS((1, A_WIDTH), F32), SDS((1, A_WIDTH), F32)],
        compiler_params=_params("arbitrary"),
    )(proj, proj, proj, dy, lng, lnb, wm, wmt, bst)


def _hgrn_consts():
    r, c = _row((CHUNK, CHUNK)), _lane((CHUNK, CHUNK))
    same = (r >> SUB_SHIFT) == (c >> SUB_SHIFT)
    lsub = jnp.where(same & (c <= r), 1.0, 0.0).astype(BF16)
    usub = jnp.where(same & (c >= r), 1.0, 0.0).astype(BF16)
    bsub = jnp.where(same, 1.0, 0.0).astype(BF16)
    return lsub, usub, bsub


def _hgrn_gates(qv, zf, lbp):
    sq, dsq = _silu_and_grad(qv)
    qt = sq * Q_SCALE
    sg = _sigmoid(zf)
    sgn = _sigmoid(-zf)
    f = lbp + (1.0 - lbp) * sg
    g = jnp.log(jnp.maximum(f, F_FLOOR))
    kf = (1.0 - lbp) * sgn
    return qt, dsq, sg, sgn, f, g, kf


def _hgrn_intra_fwd(qt, kf, b, v, mbd):
    rid = _row((SUB, 128))
    parts = []
    for s in range(SUB):
        e = jnp.exp(jnp.minimum(b - b[s:s + 1, :], 0.0))
        parts.append(jnp.where(rid >= s, qt * kf[s:s + 1, :] * e, 0.0))
    a = _dot(jnp.concatenate(parts, axis=0).astype(BF16), mbd)
    o = jnp.zeros((SUB, 128), F32)
    for s in range(SUB):
        o = o + a[SUB * s:SUB * s + SUB, :] * v[s:s + 1, :]
    return o


def _hgrn_intra_bwd(qt, kf, b, v, do, mbd, rsum):
    rid = _row((SUB, 128))
    ps, das, kes, es = [], [], [], []
    for s in range(SUB):
        e = jnp.where(rid >= s, jnp.exp(jnp.minimum(b - b[s:s + 1, :], 0.0)), 0.0)
        ke = kf[s:s + 1, :] * e
        es.append(e)
        kes.append(ke)
        ps.append(qt * ke)
        das.append(do * v[s:s + 1, :])
    a = _dot(jnp.concatenate(ps, axis=0).astype(BF16), mbd)
    da = _dot(jnp.concatenate(das, axis=0).astype(BF16), mbd)
    dqt = jnp.zeros((SUB, 128), F32)
    xs, ys = [], []
    for s in range(SUB):
        da_s = da[SUB * s:SUB * s + SUB, :]
        dqt = dqt + da_s * kes[s]
        xs.append(a[SUB * s:SUB * s + SUB, :] * do)
        ys.append(da_s * qt * es[s])
    xh = jnp.concatenate(xs, axis=0)
    yh = jnp.concatenate(ys, axis=0)
    xhi = xh.astype(BF16)
    yhi = yh.astype(BF16)
    dv = _dot(rsum, xhi) + _dot(rsum, (xh - xhi.astype(F32)).astype(BF16))
    dkf = _dot(rsum, yhi) + _dot(rsum, (yh - yhi.astype(F32)).astype(BF16))
    return dqt, dkf, dv


def _hgrn_norm_gate(o, z, onorm):
    ms = _gsum64(o * o) * (1.0 / 64)
    r = lax.rsqrt(ms + NORM_EPS)
    xh = o * r
    sz, dsz = _silu_and_grad(z)
    return xh, r, sz, dsz, xh * onorm


def _hgrn_fwd(proj, lb, onorm, tag):
    T = proj.shape[0]
    n = T // CHUNK
    nsub = CHUNK // SUB

    def body(q_ref, f_ref, i_ref, z_ref, lb_ref, on_ref, y_ref, o_ref, s0_ref, st_ref):
        @pl.when(pl.program_id(0) == 0)
        def _():
            st_ref[...] = jnp.zeros_like(st_ref)

        lsub, _, bsub = _hgrn_consts()
        mbd = _block_diag64()
        bdmask = mbd > 0
        rid = _row((CHUNK, 128))
        for pair in range(2):
            sl = slice(128 * pair, 128 * pair + 128)
            qt, _, _, _, _, g, kf = _hgrn_gates(q_ref[:, sl], f_ref[:, sl], lb_ref[:, sl])
            v = i_ref[:, sl]
            b = _dot3_left(lsub, g)
            bl = _dot3_left(bsub, g)
            qh = (qt * jnp.exp(b)).astype(BF16)
            kh = kf * jnp.exp(bl - b)
            dec = jnp.exp(bl)
            vtb = v.T.astype(BF16)
            st = st_ref[pair]
            s0_ref[0, pair] = st
            outs = []
            for sub in range(nsub):
                rs = slice(SUB * sub, SUB * sub + SUB)
                o_inter = _dot_nt(qh[rs], st.astype(BF16))
                outs.append(o_inter + _hgrn_intra_fwd(qt[rs], kf[rs], b[rs], v[rs], mbd))
                khm = jnp.where((rid >> SUB_SHIFT) == sub, kh, 0.0).astype(BF16)
                st = jnp.where(bdmask, st * dec[SUB * sub:SUB * sub + 1, :] + _dot(vtb, khm), 0.0)
            st_ref[pair] = st
            o = jnp.concatenate(outs, axis=0)
            o_ref[:, sl] = o
            _, _, sz, _, on = _hgrn_norm_gate(o, z_ref[:, sl], on_ref[:, sl])
            y_ref[:, sl] = (on * sz).astype(BF16)

    col = lambda c: pl.BlockSpec((CHUNK, B_WIDTH), lambda i, c=c: (i, c // B_WIDTH))
    full = lambda a: pl.BlockSpec(a.shape, lambda i, n=a.ndim: (0,) * n)
    return pl.pallas_call(
        body, name=f"hgrn_fwd_{tag}", grid=(n,),
        in_specs=[col(COL_BQ), col(COL_BF), col(COL_BI), col(COL_BZ), full(lb), full(onorm)],
        out_specs=[pl.BlockSpec((CHUNK, B_WIDTH), lambda i: (i, 0)), pl.BlockSpec((CHUNK, B_WIDTH), lambda i: (i, 0)),
                   pl.BlockSpec((1, 2, 128, 128), lambda i: (i, 0, 0, 0))],
        out_shape=[SDS((T, B_WIDTH), BF16), SDS((T, B_WIDTH), F32), SDS((n, 2, 128, 128), F32)],
        scratch_shapes=[pltpu.VMEM((2, 128, 128), F32)], compiler_params=_params("arbitrary"),
    )(proj, proj, proj, proj, lb, onorm)


def _hgrn_bwd(proj, dy, o_saved, s0, lb, onorm, tag):
    T = proj.shape[0]
    n = T // CHUNK
    nsub = CHUNK // SUB

    def body(q_ref, f_ref, i_ref, z_ref, dy_ref, o_ref, s0_ref, lb_ref, on_ref,
             db_ref, dlb_ref, don_ref, dst_ref, sts_ref):
        @pl.when(pl.program_id(0) == 0)
        def _():
            dst_ref[...] = jnp.zeros_like(dst_ref)
            dlb_ref[...] = jnp.zeros_like(dlb_ref)
            don_ref[...] = jnp.zeros_like(don_ref)

        lsub, usub, bsub = _hgrn_consts()
        mbd = _block_diag64()
        bdmask = mbd > 0
        rid = _row((CHUNK, 128))
        rsum = jnp.where((_lane((SUB, SUB * SUB)) >> SUB_SHIFT) == _row((SUB, SUB * SUB)), 1.0, 0.0).astype(BF16)
        for pair in range(2):
            sl = slice(128 * pair, 128 * pair + 128)
            lbp = lb_ref[:, sl]
            qv, zf = q_ref[:, sl], f_ref[:, sl]
            qt, dsq, sg, sgn, f, g, kf = _hgrn_gates(qv, zf, lbp)
            v = i_ref[:, sl]
            b = _dot3_left(lsub, g)
            bl = _dot3_left(bsub, g)
            eb = jnp.exp(b)
            ekb = jnp.exp(bl - b)
            qh = qt * eb
            kh = kf * ekb
            dec = jnp.exp(bl)
            vtb = v.T.astype(BF16)
            onp = on_ref[:, sl]
            ov = o_ref[:, sl]
            xh, r, sz, dsz, on = _hgrn_norm_gate(ov, z_ref[:, sl], onp)
            dyv = dy_ref[:, sl]
            dz = dyv * on * dsz
            don = dyv * sz
            cn = jnp.sum(don * xh, axis=0, keepdims=True)
            don_ref[...] += cn + pltpu.roll(cn, 64, axis=1)
            dxo = don * onp
            do = r * (dxo - xh * (_gsum64(dxo * xh) * (1.0 / 64)))
            dotb = do.T.astype(BF16)
            st = s0_ref[0, pair]
            for sub in range(nsub):
                sts_ref[sub] = st
                khm = jnp.where((rid >> SUB_SHIFT) == sub, kh, 0.0).astype(BF16)
                st = jnp.where(bdmask, st * dec[SUB * sub:SUB * sub + 1, :] + _dot(vtb, khm), 0.0)
            gst = dst_ref[pair]
            dqt_p, dkf_p, dv_p, dbl_p = [None] * nsub, [None] * nsub, [None] * nsub, [None] * nsub
            for sub in reversed(range(nsub)):
                rs = slice(SUB * sub, SUB * sub + SUB)
                st_in = sts_ref[sub]
                gb = gst.astype(BF16)
                dob = do[rs].astype(BF16)
                dqh = _dot(dob, st_in.astype(BF16))
                dkh = _dot(v[rs].astype(BF16), gb)
                dv_inter = _dot_nt(kh[rs].astype(BF16), gb)
                ddec = jnp.sum(gst * st_in, axis=0, keepdims=True)
                dec_row = dec[SUB * sub:SUB * sub + 1, :]
                qhm = jnp.where((rid >> SUB_SHIFT) == sub, qh, 0.0).astype(BF16)
                gst = jnp.where(bdmask, gst * dec_row + _dot(dotb, qhm), 0.0)
                dqt_i, dkf_i, dv_i = _hgrn_intra_bwd(qt[rs], kf[rs], b[rs], v[rs], do[rs], mbd, rsum)
                dkf_inter = dkh * ekb[rs]
                dqt_p[sub] = dqh * eb[rs] + dqt_i
                dkf_p[sub] = dkf_inter + dkf_i
                dv_p[sub] = dv_inter + dv_i
                row = jnp.sum(kf[rs] * dkf_inter, axis=0, keepdims=True) + ddec * dec_row
                dbl_p[sub] = jnp.broadcast_to(row, (SUB, 128))
            dst_ref[pair] = gst
            dqt = jnp.concatenate(dqt_p, axis=0)
            dkf = jnp.concatenate(dkf_p, axis=0)
            dv = jnp.concatenate(dv_p, axis=0)
            dg = _dot3_left(usub, qt * dqt - kf * dkf) + jnp.concatenate(dbl_p, axis=0)
            df = jnp.where(f > F_FLOOR, dg / f, 0.0)
            dlb_ref[:, sl] += jnp.sum(df * (1.0 - sg) - dkf * sgn, axis=0, keepdims=True)
            dfl = (1.0 - lbp) * sg * sgn * (df - dkf)
            dq = dqt * Q_SCALE * dsq
            db_ref[:, 0 * B_WIDTH + 128 * pair:0 * B_WIDTH + 128 * pair + 128] = dq.astype(BF16)
            db_ref[:, 1 * B_WIDTH + 128 * pair:1 * B_WIDTH + 128 * pair + 128] = dfl.astype(BF16)
            db_ref[:, 2 * B_WIDTH + 128 * pair:2 * B_WIDTH + 128 * pair + 128] = dv.astype(BF16)
            db_ref[:, 3 * B_WIDTH + 128 * pair:3 * B_WIDTH + 128 * pair + 128] = dz.astype(BF16)

    rev = lambda c: pl.BlockSpec((CHUNK, B_WIDTH), lambda i, c=c: (n - 1 - i, c // B_WIDTH))
    full = lambda a: pl.BlockSpec(a.shape, lambda i, n_=a.ndim: (0,) * n_)
    acc = lambda s: pl.BlockSpec(s, lambda i, n_=len(s): (0,) * n_)
    return pl.pallas_call(
        body, name=f"hgrn_bwd_{tag}", grid=(n,),
        in_specs=[rev(COL_BQ), rev(COL_BF), rev(COL_BI), rev(COL_BZ),
                  pl.BlockSpec((CHUNK, B_WIDTH), lambda i: (n - 1 - i, 1)),
                  pl.BlockSpec((CHUNK, B_WIDTH), lambda i: (n - 1 - i, 0)),
                  pl.BlockSpec((1, 2, 128, 128), lambda i: (n - 1 - i, 0, 0, 0)), full(lb), full(onorm)],
        out_specs=[pl.BlockSpec((CHUNK, 4 * B_WIDTH), lambda i: (n - 1 - i, 0)), acc((1, B_WIDTH)), acc((1, 128))],
        out_shape=[SDS((T, 4 * B_WIDTH), BF16), SDS((1, B_WIDTH), F32), SDS((1, 128), F32)],
        scratch_shapes=[pltpu.VMEM((2, 128, 128), F32), pltpu.VMEM((nsub, 128, 128), F32)],
        compiler_params=_params("arbitrary"),
    )(proj, proj, proj, proj, dy, o_saved, s0, lb, onorm)


def _lb_fwd(hgrn_lb):
    assert hgrn_lb.shape[0] == 2

    def body(x_ref, o_ref):
        x0, x1 = x_ref[0:1, :], x_ref[1:2, :]
        m = jnp.maximum(x0, x1)
        e0, e1 = jnp.exp(x0 - m), jnp.exp(x1 - m)
        p0, p1 = e0 / (e0 + e1), e1 / (e0 + e1)
        o_ref[0:1, :] = jnp.clip(p0 - p0, 0.0, 1.0 - 1e-6)
        o_ref[1:2, :] = jnp.clip((p0 + p1) - p0, 0.0, 1.0 - 1e-6)

    return pl.pallas_call(body, name="lb_fwd", out_shape=SDS(hgrn_lb.shape, F32))(hgrn_lb)


def _lb_bwd(hgrn_lb, dlb):
    def body(x_ref, d_ref, o_ref):
        x0, x1 = x_ref[0:1, :], x_ref[1:2, :]
        m = jnp.maximum(x0, x1)
        e0, e1 = jnp.exp(x0 - m), jnp.exp(x1 - m)
        p0, p1 = e0 / (e0 + e1), e1 / (e0 + e1)
        val = (p0 + p1) - p0
        dp1 = jnp.where((val > 0.0) & (val < 1.0 - 1e-6), d_ref[1:2, :], 0.0)
        inner = p1 * dp1
        o_ref[0:1, :] = p0 * (0.0 - inner)
        o_ref[1:2, :] = p1 * (dp1 - inner)

    return pl.pallas_call(body, name="lb_bwd", out_shape=SDS(hgrn_lb.shape, F32))(hgrn_lb, dlb)


def _fox_prep(proj, bf, tag):
    T = proj.shape[0]
    n = T // CHUNK

    def body(q0_ref, q1_ref, k0_ref, k1_ref, fl_ref, bf_ref, qo_ref, ko_ref, carry_ref):
        @pl.when(pl.program_id(0) == 0)
        def _():
            carry_ref[...] = jnp.zeros_like(carry_ref)

        ltri = jnp.where(_lane((CHUNK, CHUNK)) <= _row((CHUNK, CHUNK)), 1.0, 0.0).astype(BF16)
        lf = jax.nn.log_sigmoid(fl_ref[...] + bf_ref[...])
        c = _dot3_left(ltri, lf) + carry_ref[...]
        carry_ref[...] = c[CHUNK - 1:CHUNK, :]
        lane = _lane((CHUNK, 128))
        feat = lane < 64
        ones_q = (lane >= 67) & (lane <= 69)
        ones_k = (lane >= 64) & (lane <= 66)
        qrefs, krefs = (q0_ref, q1_ref), (k0_ref, k1_ref)
        for h in range(C_HEADS):
            blk = slice(128 * ((h // 2) % 2), 128 * ((h // 2) % 2) + 128)
            qp, kp = qrefs[h // 4][:, blk], krefs[h // 4][:, blk]
            if h % 2:
                qp, kp = pltpu.roll(qp, 64, axis=1), pltpu.roll(kp, 64, axis=1)
            ch = jnp.broadcast_to(c[:, h:h + 1], (CHUNK, 128))
            hi = ch.astype(BF16).astype(F32)
            r1 = ch - hi
            mid = r1.astype(BF16).astype(F32)
            lo = r1 - mid
            aq = jnp.where(lane == 64, hi, jnp.where(lane == 65, mid, jnp.where(lane == 66, lo,
                           jnp.where(ones_q, 1.0, 0.0))))
            ak = jnp.where(lane == 67, -hi, jnp.where(lane == 68, -mid, jnp.where(lane == 69, -lo,
                           jnp.where(ones_k, 1.0, 0.0))))
            qo_ref[:, 128 * h:128 * h + 128] = jnp.where(feat, qp * Q_SCALE, aq).astype(BF16)
            ko_ref[:, 128 * h:128 * h + 128] = jnp.where(feat, kp, ak).astype(BF16)

    w = 256
    col = lambda c: pl.BlockSpec((CHUNK, w), lambda i, c=c: (i, c // w))
    return pl.pallas_call(
        body, name=f"fox_prep_{tag}", grid=(n,),
        in_specs=[col(COL_CQ), col(COL_CQ + w), col(COL_CK), col(COL_CK + w),
                  pl.BlockSpec((CHUNK, 128), lambda i: (i, COL_CF // 128)), pl.BlockSpec((1, 128), lambda i: (0, 0))],
        out_specs=[pl.BlockSpec((CHUNK, C_HEADS * 128), lambda i: (i, 0))] * 2,
        out_shape=[SDS((T, C_HEADS * 128), BF16)] * 2,
        scratch_shapes=[pltpu.VMEM((1, 128), F32)], compiler_params=_params("arbitrary"),
    )(proj, proj, proj, proj, proj, bf)


def _causal(tq, tk, q0, k0):
    return (k0 + _lane((tq, tk))) <= (q0 + _row((tq, tk)))


def _fox_fwd(qt, kt, proj, tag):
    T = proj.shape[0]
    tq = tk = _tile(T, 256)

    def body(q_ref, k_ref, v_ref, z_ref, o_ref, lse_ref, y_ref):
        i = pl.program_id(1)
        qs = (q_ref[:, 0:128], q_ref[:, 128:256])

        def step(j, carry):
            k0 = pl.multiple_of(j * tk, tk)
            kb = k_ref[pl.ds(k0, tk), :]
            vb = v_ref[pl.ds(k0, tk), :].astype(BF16)
            mask = _causal(tq, tk, i * tq, k0)
            new = []
            for h in range(2):
                m, l, acc = carry[3 * h:3 * h + 3]
                s = jnp.where(mask, _dot_nt(qs[h], kb[:, 128 * h:128 * h + 128]), -jnp.inf)
                m_new = jnp.maximum(m, jnp.max(s, axis=-1, keepdims=True))
                p = jnp.exp(s - m_new)
                alpha = jnp.exp(m - m_new)
                new += [m_new, alpha * l + jnp.sum(p, axis=-1, keepdims=True), alpha * acc + _dot(p.astype(BF16), vb)]
            return tuple(new)

        init = (jnp.full((tq, 1), -jnp.inf, F32), jnp.zeros((tq, 1), F32), jnp.zeros((tq, 128), F32)) * 2
        m0, l0, a0, m1, l1, a1 = lax.fori_loop(0, (i * tq) // tk + 1, step, init)
        lo = _lane((tq, 128)) < 64
        o = jnp.where(lo, a0 / l0, a1 / l1)
        o_ref[...] = o
        lse_ref[...] = jnp.where(lo, m0 + jnp.log(l0), m1 + jnp.log(l1))
        sz, _ = _silu_and_grad(z_ref[...])
        y_ref[...] = (o * sz).astype(BF16)

    blk = pl.BlockSpec((tq, 128), lambda p, i: (i, p))
    return pl.pallas_call(
        body, name=f"fox_fwd_{tag}", grid=(C_HEADS // 2, T // tq),
        in_specs=[pl.BlockSpec((tq, 256), lambda p, i: (i, p)), pl.BlockSpec((T, 256), lambda p, i: (0, p)),
                  pl.BlockSpec((T, 128), lambda p, i: (0, COL_CV // 128 + p)),
                  pl.BlockSpec((tq, 128), lambda p, i: (i, COL_CZ // 128 + p))],
        out_specs=[blk, blk, blk], out_shape=[SDS((T, C_WIDTH), F32), SDS((T, C_WIDTH), F32), SDS((T, C_WIDTH), BF16)],
        compiler_params=_params("parallel", "parallel"),
    )(qt, kt, proj, proj)


def _fox_bwd_prep(proj, dy, o, tag):
    T = proj.shape[0]
    tm = _tile(T, 512)

    def body(z_ref, dy_ref, o_ref, do_ref, dl_ref, dz_ref):
        sz, dsz = _silu_and_grad(z_ref[...])
        dyv, ov = dy_ref[...], o_ref[...]
        do = dyv * sz
        do_ref[...] = do.astype(BF16)
        dl_ref[...] = _gsum64(do * ov)
        dz_ref[...] = (dyv * ov * dsz).astype(BF16)

    blk = pl.BlockSpec((tm, 128), lambda i, p: (i, p))
    return pl.pallas_call(
        body, name=f"fox_bwd_prep_{tag}", grid=(T // tm, C_WIDTH // 128),
        in_specs=[pl.BlockSpec((tm, 128), lambda i, p: (i, COL_CZ // 128 + p)),
                  pl.BlockSpec((tm, 128), lambda i, p: (i, (A_WIDTH + B_WIDTH) // 128 + p)), blk],
        out_specs=[blk, blk, blk],
        out_shape=[SDS((T, C_WIDTH), BF16), SDS((T, C_WIDTH), F32), SDS((T, C_WIDTH), BF16)],
        compiler_params=_params("parallel", "parallel"),
    )(proj, dy, o)


def _fox_bwd(qt, kt, proj, do, lse, delta, tag):
    T = proj.shape[0]
    tq = tk = _tile(T, 256)
    nq = T // tq

    def body(q_ref, k_ref, v_ref, do_ref, lse_ref, dl_ref, dq_ref, dk_ref, dv_ref):
        j = pl.program_id(1)

        @pl.when(j == 0)
        def _():
            dq_ref[...] = jnp.zeros_like(dq_ref)

        ks = (k_ref[:, 0:128], k_ref[:, 128:256])
        vb = v_ref[...].astype(BF16)
        lo = _lane((tq, 128)) < 64

        def step(i, carry):
            q0 = pl.multiple_of(i * tq, tq)
            qb = q_ref[pl.ds(q0, tq), :]
            dob = do_ref[pl.ds(q0, tq), :]
            lsev = lse_ref[pl.ds(q0, tq), :]
            dlv = dl_ref[pl.ds(q0, tq), :]
            mask = _causal(tq, tk, q0, j * tk)
            dk0, dk1, dv = carry
            dks = [dk0, dk1]
            for h in range(2):
                qh = qb[:, 128 * h:128 * h + 128]
                s = _dot_nt(qh, ks[h])
                p = jnp.where(mask, jnp.exp(s - lsev[:, 64 * h:64 * h + 1]), 0.0)
                doh = jnp.where(lo if h == 0 else ~lo, dob, jnp.zeros_like(dob))
                dp = _dot_nt(doh, vb)
                ds = (p * (dp - dlv[:, 64 * h:64 * h + 1])).astype(BF16)
                dv = dv + _dot_tn(p.astype(BF16), doh)
                dks[h] = dks[h] + _dot_tn(ds, qh)
                dq_ref[pl.ds(q0, tq), 128 * h:128 * h + 128] += _dot(ds, ks[h])
            return dks[0], dks[1], dv

        z = jnp.zeros((tk, 128), F32)
        dk0, dk1, dv = lax.fori_loop((j * tk) // tq, nq, step, (z, z, z))
        dk_ref[:, 0:128] = dk0
        dk_ref[:, 128:256] = dk1
        dv_ref[...] = dv.astype(BF16)

    full = lambda w: pl.BlockSpec((T, w), lambda p, j: (0, p))
    return pl.pallas_call(
        body, name=f"fox_bwd_{tag}", grid=(C_HEADS // 2, T // tk),
        in_specs=[full(256), pl.BlockSpec((tk, 256), lambda p, j: (j, p)),
                  pl.BlockSpec((tk, 128), lambda p, j: (j, COL_CV // 128 + p)), full(128), full(128), full(128)],
        out_specs=[full(256), pl.BlockSpec((tk, 256), lambda p, j: (j, p)), pl.BlockSpec((tk, 128), lambda p, j: (j, p))],
        out_shape=[SDS((T, C_HEADS * 128), F32), SDS((T, C_HEADS * 128), F32), SDS((T, C_WIDTH), BF16)],
        compiler_params=_params("parallel", "arbitrary"),
    )(qt, kt, proj, do, lse, delta)


def _fox_bwd_post(dqt, dkt, proj, bf, tag):
    T = proj.shape[0]
    n = T // CHUNK

    def body(dq_ref, dk_ref, fl_ref, bf_ref, oq_ref, ok_ref, ofl_ref, dbf_ref, carry_ref):
        @pl.when(pl.program_id(0) == 0)
        def _():
            carry_ref[...] = jnp.zeros_like(carry_ref)
            dbf_ref[...] = jnp.zeros_like(dbf_ref)

        lane = _lane((CHUNK, 128))
        lo = lane < 64
        dc = jnp.zeros((CHUNK, 128), F32)
        for h in range(C_HEADS):
            base = 128 * h
            dc = dc + jnp.where(lane == h, dq_ref[:, base + 64:base + 65] - dk_ref[:, base + 67:base + 68], 0.0)
        utri = jnp.where(_lane((CHUNK, CHUNK)) >= _row((CHUNK, CHUNK)), 1.0, 0.0).astype(BF16)
        dlf = _dot3_left(utri, dc) + carry_ref[...]
        carry_ref[...] = dlf[0:1, :]
        dfl = jnp.where(lane < C_HEADS, dlf * _sigmoid(-(fl_ref[...] + bf_ref[...])), 0.0)
        ofl_ref[...] = dfl.astype(BF16)
        dbf_ref[...] += jnp.sum(dfl, axis=0, keepdims=True)
        for p in range(C_HEADS // 2):
            a, b = 128 * (2 * p), 128 * (2 * p + 1)
            oq_ref[:, 128 * p:128 * p + 128] = (
                jnp.where(lo, dq_ref[:, a:a + 128], pltpu.roll(dq_ref[:, b:b + 128], 64, axis=1)) * Q_SCALE).astype(BF16)
            ok_ref[:, 128 * p:128 * p + 128] = jnp.where(
                lo, dk_ref[:, a:a + 128], pltpu.roll(dk_ref[:, b:b + 128], 64, axis=1)).astype(BF16)

    rev = lambda w: pl.BlockSpec((CHUNK, w), lambda i: (n - 1 - i, 0))
    return pl.pallas_call(
        body, name=f"fox_bwd_post_{tag}", grid=(n,),
        in_specs=[rev(C_HEADS * 128), rev(C_HEADS * 128),
                  pl.BlockSpec((CHUNK, 128), lambda i: (n - 1 - i, COL_CF // 128)), pl.BlockSpec((1, 128), lambda i: (0, 0))],
        out_specs=[rev(C_WIDTH), rev(C_WIDTH), rev(128), pl.BlockSpec((1, 128), lambda i: (0, 0))],
        out_shape=[SDS((T, C_WIDTH), BF16), SDS((T, C_WIDTH), BF16), SDS((T, 128), BF16), SDS((1, 128), F32)],
        scratch_shapes=[pltpu.VMEM((1, 128), F32)], compiler_params=_params("arbitrary"),
    )(dqt, dkt, proj, bf)


def _adamw_math(w, g, m, v):
    m = ADAM_B1 * m + (1.0 - ADAM_B1) * g
    v = ADAM_B2 * v + (1.0 - ADAM_B2) * (g * g)
    delta = -ADAM_LR * ((m / ADAM_C1) / (jnp.sqrt(v / ADAM_C2) + ADAM_EPS) + ADAM_WD * w)
    return delta, m, v


def _adamw_pair(w, m, v, ga, gb, name):
    L, R, C = w.shape
    tr = _tile(R, 256)

    def body(w_ref, m_ref, v_ref, ga_ref, gb_ref, g_ref, d_ref, nm_ref, nv_ref):
        g = ga_ref[...] + gb_ref[...]
        g_ref[...] = g
        d_ref[...], nm_ref[...], nv_ref[...] = _adamw_math(w_ref[...], g, m_ref[...], v_ref[...])

    blk = pl.BlockSpec((1, tr, C), lambda l, i: (l, i, 0))
    return pl.pallas_call(
        body, name=name, grid=(L, R // tr), in_specs=[blk] * 5, out_specs=[blk] * 4,
        out_shape=[SDS(w.shape, F32)] * 4, compiler_params=_params("parallel", "parallel"),
    )(w, m, v, ga, gb)


def _adamw_small(w, m, v, gall):
    R = w.shape[0]

    def body(w_ref, m_ref, v_ref, g_ref, go_ref, d_ref, nm_ref, nv_ref):
        g = g_ref[0]
        for k in range(1, N_DEV):
            g = g + g_ref[k]
        go_ref[...] = g
        d_ref[...], nm_ref[...], nv_ref[...] = _adamw_math(w_ref[...], g, m_ref[...], v_ref[...])

    return pl.pallas_call(body, name="adamw_small", out_shape=[SDS((R, 128), F32)] * 4,
                          compiler_params=pltpu.CompilerParams(vmem_limit_bytes=VMEM_LIMIT))(w, m, v, gall)


def _sum_chips(parts, name):
    _, L, R, C = parts.shape
    tr = _tile(R, 256)

    def body(p_ref, o_ref):
        o_ref[...] = ((p_ref[0] + p_ref[1]) + p_ref[2]) + p_ref[3]

    return pl.pallas_call(
        body, name=name, grid=(L, R // tr),
        in_specs=[pl.BlockSpec((N_CHIPS, 1, tr, C), lambda l, i: (0, l, i, 0))],
        out_specs=pl.BlockSpec((1, tr, C), lambda l, i: (l, i, 0)), out_shape=SDS((L, R, C), F32),
        compiler_params=_params("parallel", "parallel"),
    )(parts)


ANY = pl.BlockSpec(memory_space=pl.ANY)


def _mesh_pos():
    return lax.axis_index("x"), lax.axis_index("y"), lax.axis_index("c")


def _other_chips(x, y):
    return [(1 - x, y), (x, 1 - y), (1 - x, 1 - y)]


def _allgather_weights(win, wout):
    def body(win_ref, wout_ref, gin_ref, gout_ref, send_sems, recv_sems, local_sems):
        x, y, c = _mesh_pos()
        me = 2 * x + y
        pairs = ((win_ref, gin_ref), (wout_ref, gout_ref))
        local = [pltpu.make_async_copy(src, dst.at[me], local_sems.at[a]) for a, (src, dst) in enumerate(pairs)]
        for cp in local:
            cp.start()
        sends = []
        for j, (px, py) in enumerate(_other_chips(x, y)):
            for a, (src, dst) in enumerate(pairs):
                cp = pltpu.make_async_remote_copy(src_ref=src, dst_ref=dst.at[me], send_sem=send_sems.at[2 * j + a],
                                                  recv_sem=recv_sems.at[2 * j + a], device_id=(px, py, c),
                                                  device_id_type=MESH_ID)
                cp.start()
                sends.append(cp)
        for j, (px, py) in enumerate(_other_chips(x, y)):
            for a, (src, dst) in enumerate(pairs):
                pltpu.make_async_remote_copy(src_ref=src, dst_ref=dst.at[2 * px + py], send_sem=send_sems.at[2 * j + a],
                                             recv_sem=recv_sems.at[2 * j + a], device_id=(px, py, c),
                                             device_id_type=MESH_ID).wait_recv()
        for cp in sends:
            cp.wait_send()
        for cp in local:
            cp.wait()

    return pl.pallas_call(
        body, name="allgather_weights", in_specs=[ANY, ANY], out_specs=[ANY, ANY],
        out_shape=[SDS((N_CHIPS,) + win.shape, win.dtype), SDS((N_CHIPS,) + wout.shape, wout.dtype)],
        scratch_shapes=[pltpu.SemaphoreType.DMA((6,)), pltpu.SemaphoreType.DMA((6,)), pltpu.SemaphoreType.DMA((2,))],
        compiler_params=pltpu.CompilerParams(has_side_effects=True),
    )(win, wout)


def _exchange_grads(gin, gout, small):
    def body(gin_ref, gout_ref, small_ref, rin_ref, rout_ref, rsmall_ref, send_sems, recv_sems, local_sems):
        x, y, c = _mesh_pos()
        me_chip = 2 * x + y
        me_dev = 4 * x + 2 * y + c
        pairs = ((gin_ref, rin_ref), (gout_ref, rout_ref))
        local = [pltpu.make_async_copy(src.at[me_chip], dst.at[me_chip], local_sems.at[a])
                 for a, (src, dst) in enumerate(pairs)]
        local.append(pltpu.make_async_copy(small_ref, rsmall_ref.at[me_dev], local_sems.at[2]))
        for cp in local:
            cp.start()
        sends = []
        for j, (px, py) in enumerate(_other_chips(x, y)):
            for a, (src, dst) in enumerate(pairs):
                cp = pltpu.make_async_remote_copy(src_ref=src.at[2 * px + py], dst_ref=dst.at[me_chip],
                                                  send_sem=send_sems.at[2 * j + a], recv_sem=recv_sems.at[2 * j + a],
                                                  device_id=(px, py, c), device_id_type=MESH_ID)
                cp.start()
                sends.append(cp)
        peers = [(k, ((1 - x) if k & 4 else x, (1 - y) if k & 2 else y, (1 - c) if k & 1 else c)) for k in range(1, N_DEV)]
        for k, peer in peers:
            cp = pltpu.make_async_remote_copy(src_ref=small_ref, dst_ref=rsmall_ref.at[me_dev], send_sem=send_sems.at[5 + k],
                                              recv_sem=recv_sems.at[5 + k], device_id=peer, device_id_type=MESH_ID)
            cp.start()
            sends.append(cp)
        for j, (px, py) in enumerate(_other_chips(x, y)):
            for a, (src, dst) in enumerate(pairs):
                pltpu.make_async_remote_copy(src_ref=src.at[me_chip], dst_ref=dst.at[2 * px + py],
                                             send_sem=send_sems.at[2 * j + a], recv_sem=recv_sems.at[2 * j + a],
                                             device_id=(px, py, c), device_id_type=MESH_ID).wait_recv()
        for k, (px, py, pc) in peers:
            pltpu.make_async_remote_copy(src_ref=small_ref, dst_ref=rsmall_ref.at[4 * px + 2 * py + pc],
                                         send_sem=send_sems.at[5 + k], recv_sem=recv_sems.at[5 + k],
                                         device_id=(px, py, pc), device_id_type=MESH_ID).wait_recv()
        for cp in sends:
            cp.wait_send()
        for cp in local:
            cp.wait()

    return pl.pallas_call(
        body, name="exchange_grads", in_specs=[ANY, ANY, ANY], out_specs=[ANY, ANY, ANY],
        out_shape=[SDS(gin.shape, F32), SDS(gout.shape, F32), SDS((N_DEV,) + small.shape, F32)],
        scratch_shapes=[pltpu.SemaphoreType.DMA((13,)), pltpu.SemaphoreType.DMA((13,)), pltpu.SemaphoreType.DMA((3,))],
        compiler_params=pltpu.CompilerParams(has_side_effects=True),
    )(gin, gout, small)


def _swap_cores(pin, pout):
    def body(pin_ref, pout_ref, oin_ref, oout_ref, send_sems, recv_sems):
        x, y, c = _mesh_pos()
        cps = [pltpu.make_async_remote_copy(src_ref=src, dst_ref=dst, send_sem=send_sems.at[a], recv_sem=recv_sems.at[a],
                                            device_id=(x, y, 1 - c), device_id_type=MESH_ID)
               for a, (src, dst) in enumerate(((pin_ref, oin_ref), (pout_ref, oout_ref)))]
        for cp in cps:
            cp.start()
        for cp in cps:
            cp.wait()

    return pl.pallas_call(
        body, name="swap_cores", in_specs=[ANY, ANY], out_specs=[ANY, ANY],
        out_shape=[SDS(pin.shape, F32), SDS(pout.shape, F32)],
        scratch_shapes=[pltpu.SemaphoreType.DMA((2,)), pltpu.SemaphoreType.DMA((2,))],
        compiler_params=pltpu.CompilerParams(has_side_effects=True),
    )(pin, pout)


def _pack_small(parts):
    flat = [jnp.pad(p.reshape(-1), (0, (-p.size) % 128)) for p in parts]
    v = jnp.concatenate(flat)
    return jnp.pad(v, (0, (-v.size) % 1024)).reshape(-1, 128)


def _unpack_small(packed):
    flat = packed.reshape(-1)
    out, off = [], 0
    for _, shape in SMALL_PARAMS:
        size = math.prod(shape)
        out.append(flat[off:off + size].reshape(shape))
        off += size + (-size) % 128
    return out


def _layer_consts(l, gmlp_ln_g, gmlp_ln_b, gmlp_w_s, gmlp_b_s, hgrn_onorm_g, fox_b_f):
    causal = jnp.tril(jnp.ones((CHUNK, CHUNK), bool))
    wm = jnp.where(causal[None], gmlp_w_s[l], 0.0)
    return dict(
        lng=gmlp_ln_g[l].reshape(1, A_WIDTH), lnb=gmlp_ln_b[l].reshape(1, A_WIDTH),
        wm=wm.astype(BF16), wmt=jnp.swapaxes(wm, 1, 2).astype(BF16),
        bst=jnp.pad(gmlp_b_s[l].T, ((0, 0), (0, 128 - A_GROUPS))),
        onorm=jnp.tile(hgrn_onorm_g[l], 4).reshape(1, B_WIDTH),
        bf=jnp.pad(fox_b_f[l], (0, 128 - C_HEADS)).reshape(1, 128),
    )


def kernel(x, norm_g, w_in, w_out, gmlp_ln_g, gmlp_ln_b, gmlp_w_s, gmlp_b_s, hgrn_lb, hgrn_onorm_g, fox_b_f, final_norm_g, loss_target, m_norm_g, m_w_in, m_w_out, m_gmlp_ln_g, m_gmlp_ln_b, m_gmlp_w_s, m_gmlp_b_s, m_hgrn_lb, m_hgrn_onorm_g, m_fox_b_f, m_final_norm_g, v_norm_g, v_w_in, v_w_out, v_gmlp_ln_g, v_gmlp_ln_b, v_gmlp_w_s, v_gmlp_b_s, v_hgrn_lb, v_hgrn_onorm_g, v_fox_b_f, v_final_norm_g):
    T = x.shape[1]
    shard_in = w_in.shape[2]
    shard_out = w_out.shape[1]
    xs = x.reshape(T, D_MODEL)
    tgt = loss_target.reshape(T, D_MODEL)

    gin_w, gout_w = _allgather_weights(w_in.astype(BF16), w_out.astype(BF16))
    w_in_full = jnp.concatenate([gin_w[k] for k in range(N_CHIPS)], axis=-1)
    w_in_full = jnp.pad(w_in_full, ((0, 0), (0, 0), (0, D_IN_PAD - D_IN)))
    w_out_full = jnp.concatenate([gout_w[k] for k in range(N_CHIPS)], axis=1)

    lb_all = _lb_fwd(hgrn_lb)
    consts = [_layer_consts(l, gmlp_ln_g, gmlp_ln_b, gmlp_w_s, gmlp_b_s, hgrn_onorm_g, fox_b_f) for l in range(DEPTH)]

    saved = []
    xl = xs
    for l in range(DEPTH):
        cs = consts[l]
        tag = f"l{l}"
        h, proj = _inproj(xl, norm_g[l].reshape(1, D_MODEL), w_in_full[l], tag)
        ya = _gmlp_fwd(proj, cs["lng"], cs["lnb"], cs["wm"], cs["bst"], tag)
        yb, ob, s0 = _hgrn_fwd(proj, lb_all[l].reshape(1, B_WIDTH), cs["onorm"], tag)
        qt, kt = _fox_prep(proj, cs["bf"], tag)
        oc, lse, yc = _fox_fwd(qt, kt, proj, tag)
        saved.append(dict(x=xl, h=h, proj=proj, ya=ya, yb=yb, yc=yc, ob=ob, s0=s0, qt=qt, kt=kt, oc=oc, lse=lse))
        xl = _outproj(xl, ya, yb, yc, w_out_full[l], tag)

    dx, loss_part, d_final = _loss_head(xl, final_norm_g.reshape(1, D_MODEL), tgt)
    loss = lax.psum(loss_part[0, 0], ("x", "y", "c"))

    g_small = {}
    dw_in, dw_out, dlb_rows = [None] * DEPTH, [None] * DEPTH, [None] * DEPTH
    for l in reversed(range(DEPTH)):
        cs, sv = consts[l], saved[l]
        tag = f"l{l}"
        proj = sv["proj"]
        dy, dw_out[l] = _outproj_bwd(dx, sv["ya"], sv["yb"], sv["yc"], w_out_full[l], tag)
        da, dwm, dbst, dlng, dlnb = _gmlp_bwd(proj, dy, cs["lng"], cs["lnb"], cs["wm"], cs["wmt"], cs["bst"], tag)
        db, dlb_rows[l], donorm = _hgrn_bwd(proj, dy, sv["ob"], sv["s0"], lb_all[l].reshape(1, B_WIDTH), cs["onorm"], tag)
        do, delta, dzc = _fox_bwd_prep(proj, dy, sv["oc"], tag)
        dqt, dkt, dvc = _fox_bwd(sv["qt"], sv["kt"], proj, do, sv["lse"], delta, tag)
        dqc, dkc, dflc, dbf = _fox_bwd_post(dqt, dkt, proj, cs["bf"], tag)
        dproj = jnp.concatenate([da, db, dqc, dkc, dvc, dzc, dflc, jnp.zeros((T, 128), BF16)], axis=1)
        dw_in[l] = _dw_in(sv["h"], dproj, tag)
        dx, dng = _dx_in(sv["x"], norm_g[l].reshape(1, D_MODEL), dx, dproj, w_in_full[l], tag)
        g_small[l] = dict(norm_g=dng.reshape(D_MODEL), ln_g=dlng.reshape(4, 64), ln_b=dlnb.reshape(4, 64), w_s=dwm,
                          b_s=dbst[:, :A_GROUPS].T, onorm=donorm[0, :64], bf=dbf[0, :C_HEADS])
    grad_x = dx.reshape(x.shape)
    d_hgrn_lb = _lb_bwd(hgrn_lb, jnp.concatenate(dlb_rows, axis=0))

    stack = lambda key: jnp.stack([g_small[l][key] for l in range(DEPTH)])
    small_g = _pack_small([stack("norm_g"), stack("ln_g"), stack("ln_b"), stack("w_s"), stack("b_s"), d_hgrn_lb,
                           stack("onorm"), stack("bf"), d_final.reshape(D_MODEL)])

    dwi = jnp.stack(dw_in)
    gin = jnp.stack([dwi[:, :, k * shard_in:(k + 1) * shard_in] for k in range(N_CHIPS)])
    gout = jnp.stack(dw_out).reshape(DEPTH, N_CHIPS, shard_out, D_MODEL).transpose(1, 0, 2, 3)
    rin, rout, rsmall = _exchange_grads(gin, gout, small_g)
    pin, pout = _sum_chips(rin, "sum_chips_w_in"), _sum_chips(rout, "sum_chips_w_out")
    oin, oout = _swap_cores(pin, pout)
    g_w_in, d_w_in, nm_w_in, nv_w_in = _adamw_pair(w_in, m_w_in, v_w_in, pin, oin, "adamw_w_in")
    g_w_out, d_w_out, nm_w_out, nv_w_out = _adamw_pair(w_out, m_w_out, v_w_out, pout, oout, "adamw_w_out")

    small_w = [norm_g, gmlp_ln_g, gmlp_ln_b, gmlp_w_s, gmlp_b_s, hgrn_lb, hgrn_onorm_g, fox_b_f, final_norm_g]
    small_m = [m_norm_g, m_gmlp_ln_g, m_gmlp_ln_b, m_gmlp_w_s, m_gmlp_b_s, m_hgrn_lb, m_hgrn_onorm_g, m_fox_b_f, m_final_norm_g]
    small_v = [v_norm_g, v_gmlp_ln_g, v_gmlp_ln_b, v_gmlp_w_s, v_gmlp_b_s, v_hgrn_lb, v_hgrn_onorm_g, v_fox_b_f, v_final_norm_g]
    outs = _adamw_small(_pack_small(small_w), _pack_small(small_m), _pack_small(small_v), rsmall)
    sg, sd, sm, sv_ = [_unpack_small(o) for o in outs]

    def order(big_in, big_out, small):
        return [small[0], big_in, big_out] + small[1:]

    return (loss, grad_x, *order(g_w_in, g_w_out, sg), *order(d_w_in, d_w_out, sd), *order(nm_w_in, nm_w_out, sm),
            *order(nv_w_in, nv_w_out, sv_))
```

```python
import functools
import math

import jax
import jax.numpy as jnp
from jax import lax
from jax.experimental import pallas as pl
from jax.experimental.pallas import tpu as pltpu

F32 = jnp.float32
BF16 = jnp.bfloat16
SDS = jax.ShapeDtypeStruct
MESH_ID = pl.DeviceIdType.MESH

D_MODEL = 1024
DEPTH = 2
A_WIDTH = 256
A_GROUPS = 4
B_WIDTH = 256
C_WIDTH = 512
C_HEADS = 8
D_IN = 3848
D_IN_PAD = 4096
CHUNK = 128
SUB = 16
SUB_SHIFT = 4
NORM_EPS = 1e-6
F_FLOOR = 1e-30
COL_AU, COL_AV, COL_AZ = 0, 256, 512
COL_BQ, COL_BF, COL_BI, COL_BZ = 768, 1024, 1280, 1536
COL_CQ, COL_CK, COL_CV, COL_CZ, COL_CF = 1792, 2304, 2816, 3328, 3840
HEAD_LANES = 128
Q_SCALE = 0.125
ADAM_LR, ADAM_B1, ADAM_B2, ADAM_EPS, ADAM_WD, ADAM_STEP = 0.001, 0.9, 0.999, 1e-08, 0.01, 10
ADAM_C1 = 1.0 - ADAM_B1 ** ADAM_STEP
ADAM_C2 = 1.0 - ADAM_B2 ** ADAM_STEP
VMEM_LIMIT = 56 * 1024 * 1024
N_CHIPS = 4
N_DEV = 8

SMALL_PARAMS = (
    ("norm_g", (DEPTH, D_MODEL)), ("gmlp_ln_g", (DEPTH, 4, 64)), ("gmlp_ln_b", (DEPTH, 4, 64)),
    ("gmlp_w_s", (DEPTH, 4, 128, 128)), ("gmlp_b_s", (DEPTH, 4, 128)), ("hgrn_lb", (DEPTH, 256)),
    ("hgrn_onorm_g", (DEPTH, 64)), ("fox_b_f", (DEPTH, 8)), ("final_norm_g", (D_MODEL,)),
)


def _tile(n, pref):
    t = min(n, pref)
    assert n % t == 0, (n, pref)
    return t


def _params(*sem):
    return pltpu.CompilerParams(dimension_semantics=sem, vmem_limit_bytes=VMEM_LIMIT)


def _dot(a, b):
    return jnp.dot(a, b, preferred_element_type=F32)


def _dot_nt(a, b):
    return lax.dot_general(a, b, (((1,), (1,)), ((), ())), preferred_element_type=F32)


def _dot_tn(a, b):
    return lax.dot_general(a, b, (((0,), (0,)), ((), ())), preferred_element_type=F32)


def _split3(x):
    hi = x.astype(BF16)
    r = x - hi.astype(F32)
    mid = r.astype(BF16)
    lo = (r - mid.astype(F32)).astype(BF16)
    return hi, mid, lo


def _dot3_left(c, x):
    hi, mid, lo = _split3(x)
    return _dot(c, hi) + _dot(c, mid) + _dot(c, lo)


def _sigmoid(x):
    return jax.nn.sigmoid(x)


def _silu_and_grad(x):
    s = _sigmoid(x)
    return x * s, s * (1.0 + x * (1.0 - s))


_GELU_C = math.sqrt(2.0 / math.pi)


def _gelu_and_grad(x):
    inner = _GELU_C * (x + 0.044715 * x * x * x)
    t = jnp.tanh(inner)
    y = 0.5 * x * (1.0 + t)
    dy = 0.5 * (1.0 + t) + 0.5 * x * (1.0 - t * t) * _GELU_C * (1.0 + 3.0 * 0.044715 * x * x)
    return y, dy


def _lane(shape):
    return lax.broadcasted_iota(jnp.int32, shape, 1)


def _row(shape):
    return lax.broadcasted_iota(jnp.int32, shape, 0)


def _gsum64(x):
    lo = _lane(x.shape) < 64
    s0 = jnp.sum(jnp.where(lo, x, 0.0), axis=-1, keepdims=True)
    s1 = jnp.sum(jnp.where(lo, 0.0, x), axis=-1, keepdims=True)
    return jnp.where(lo, s0, s1)


def _colreduce(x, op):
    parts = [x[r:r + 8, :] for r in range(0, x.shape[0], 8)]
    while len(parts) > 1:
        pairs = [op(parts[k], parts[k + 1]) for k in range(0, len(parts) - 1, 2)]
        parts = pairs + ([parts[-1]] if len(parts) % 2 else [])
    red = jnp.max if op is jnp.maximum else jnp.sum
    return red(parts[0], axis=0, keepdims=True)


def _block_diag64(dtype=BF16):
    r, c = _row((128, 128)), _lane((128, 128))
    return jnp.where((r >> 6) == (c >> 6), 1.0, 0.0).astype(dtype)


def _inproj(x, g, w, tag):
    T, D = x.shape
    DP = w.shape[1]
    tm, tn = _tile(T, 512), _tile(DP, 1024)

    def body(x_ref, g_ref, w_ref, h_ref, p_ref):
        @pl.when(pl.program_id(1) == 0)
        def _():
            xv = x_ref[...]
            r = lax.rsqrt(jnp.mean(xv * xv, axis=-1, keepdims=True) + NORM_EPS)
            h_ref[...] = (xv * r * g_ref[...]).astype(BF16)

        p_ref[...] = _dot(h_ref[...], w_ref[...])

    return pl.pallas_call(
        body, name=f"inproj_{tag}", grid=(T // tm, DP // tn),
        in_specs=[pl.BlockSpec((tm, D), lambda i, j: (i, 0)), pl.BlockSpec((1, D), lambda i, j: (0, 0)),
                  pl.BlockSpec((D, tn), lambda i, j: (0, j))],
        out_specs=[pl.BlockSpec((tm, D), lambda i, j: (i, 0)), pl.BlockSpec((tm, tn), lambda i, j: (i, j))],
        out_shape=[SDS((T, D), BF16), SDS((T, DP), F32)],
        compiler_params=_params("parallel", "arbitrary"),
    )(x, g, w)


def _outproj(x, ya, yb, yc, wo, tag):
    T, D = x.shape
    tm = _tile(T, 512)

    def body(x_ref, ya_ref, yb_ref, yc_ref, wo_ref, o_ref):
        acc = x_ref[...] + _dot(ya_ref[...], wo_ref[0:A_WIDTH, :])
        acc = acc + _dot(yb_ref[...], wo_ref[A_WIDTH:A_WIDTH + B_WIDTH, :])
        o_ref[...] = acc + _dot(yc_ref[...], wo_ref[A_WIDTH + B_WIDTH:, :])

    row = lambda w: pl.BlockSpec((tm, w), lambda i: (i, 0))
    return pl.pallas_call(
        body, name=f"outproj_{tag}", grid=(T // tm,),
        in_specs=[row(D), row(A_WIDTH), row(B_WIDTH), row(C_WIDTH), pl.BlockSpec(wo.shape, lambda i: (0, 0))],
        out_specs=row(D), out_shape=SDS((T, D), F32), compiler_params=_params("parallel"),
    )(x, ya, yb, yc, wo)


def _outproj_bwd(dx, ya, yb, yc, wo, tag):
    T, D = dx.shape
    DM = wo.shape[0]
    tm = _tile(T, 512)

    def body(dx_ref, ya_ref, yb_ref, yc_ref, wo_ref, dy_ref, dwo_ref):
        @pl.when(pl.program_id(0) == 0)
        def _():
            dwo_ref[...] = jnp.zeros_like(dwo_ref)

        dxb = dx_ref[...].astype(BF16)
        dy_ref[...] = _dot_nt(dxb, wo_ref[...])
        dwo_ref[0:A_WIDTH, :] += _dot_tn(ya_ref[...], dxb)
        dwo_ref[A_WIDTH:A_WIDTH + B_WIDTH, :] += _dot_tn(yb_ref[...], dxb)
        dwo_ref[A_WIDTH + B_WIDTH:, :] += _dot_tn(yc_ref[...], dxb)

    row = lambda w: pl.BlockSpec((tm, w), lambda i: (i, 0))
    return pl.pallas_call(
        body, name=f"outproj_bwd_{tag}", grid=(T // tm,),
        in_specs=[row(D), row(A_WIDTH), row(B_WIDTH), row(C_WIDTH), pl.BlockSpec(wo.shape, lambda i: (0, 0))],
        out_specs=[row(DM), pl.BlockSpec((DM, D), lambda i: (0, 0))],
        out_shape=[SDS((T, DM), F32), SDS((DM, D), F32)], compiler_params=_params("arbitrary"),
    )(dx, ya, yb, yc, wo)


def _dw_in(h, dproj, tag):
    T, D = h.shape
    DP = dproj.shape[1]
    tm, tn = _tile(T, 512), _tile(DP, 1024)

    def body(h_ref, dp_ref, dw_ref):
        @pl.when(pl.program_id(1) == 0)
        def _():
            dw_ref[...] = jnp.zeros_like(dw_ref)

        dw_ref[...] += _dot_tn(h_ref[...], dp_ref[...])

    return pl.pallas_call(
        body, name=f"dw_in_{tag}", grid=(DP // tn, T // tm),
        in_specs=[pl.BlockSpec((tm, D), lambda j, i: (i, 0)), pl.BlockSpec((tm, tn), lambda j, i: (i, j))],
        out_specs=pl.BlockSpec((D, tn), lambda j, i: (0, j)), out_shape=SDS((D, DP), F32),
        compiler_params=_params("parallel", "arbitrary"),
    )(h, dproj)


def _dx_in(x, g, dres, dproj, w, tag):
    T, D = x.shape
    DP = w.shape[1]
    tm, tk = _tile(T, 512), _tile(DP, 1024)
    nk = DP // tk

    def body(x_ref, g_ref, dres_ref, dp_ref, w_ref, dx_ref, dg_ref, acc_ref):
        i, k = pl.program_id(0), pl.program_id(1)

        @pl.when((i == 0) & (k == 0))
        def _():
            dg_ref[...] = jnp.zeros_like(dg_ref)

        @pl.when(k == 0)
        def _():
            acc_ref[...] = jnp.zeros_like(acc_ref)

        acc_ref[...] += _dot_nt(dp_ref[...], w_ref[...])

        @pl.when(k == nk - 1)
        def _():
            xv = x_ref[...]
            r = lax.rsqrt(jnp.mean(xv * xv, axis=-1, keepdims=True) + NORM_EPS)
            xh = xv * r
            dh = acc_ref[...]
            dg_ref[...] += jnp.sum(dh * xh, axis=0, keepdims=True)
            dxh = dh * g_ref[...]
            dx_ref[...] = dres_ref[...] + r * (dxh - xh * jnp.mean(dxh * xh, axis=-1, keepdims=True))

    return pl.pallas_call(
        body, name=f"dx_in_{tag}", grid=(T // tm, nk),
        in_specs=[pl.BlockSpec((tm, D), lambda i, k: (i, 0)), pl.BlockSpec((1, D), lambda i, k: (0, 0)),
                  pl.BlockSpec((tm, D), lambda i, k: (i, 0)), pl.BlockSpec((tm, tk), lambda i, k: (i, k)),
                  pl.BlockSpec((D, tk), lambda i, k: (0, k))],
        out_specs=[pl.BlockSpec((tm, D), lambda i, k: (i, 0)), pl.BlockSpec((1, D), lambda i, k: (0, 0))],
        out_shape=[SDS((T, D), F32), SDS((1, D), F32)],
        scratch_shapes=[pltpu.VMEM((tm, D), F32)], compiler_params=_params("arbitrary", "arbitrary"),
    )(x, g, dres, dproj, w)


def _loss_head(x, g, tgt):
    T, D = x.shape
    tm = _tile(T, 512)

    def body(x_ref, g_ref, t_ref, dx_ref, loss_ref, dg_ref):
        @pl.when(pl.program_id(0) == 0)
        def _():
            loss_ref[...] = jnp.zeros_like(loss_ref)
            dg_ref[...] = jnp.zeros_like(dg_ref)

        xv = x_ref[...]
        r = lax.rsqrt(jnp.mean(xv * xv, axis=-1, keepdims=True) + NORM_EPS)
        xh = xv * r
        gv = g_ref[...]
        err = xh * gv - t_ref[...]
        tok = jnp.mean(err * err, axis=-1, keepdims=True)
        loss_ref[...] += 0.5 * jnp.sum(tok, axis=0, keepdims=True)
        dy = err * (1.0 / D)
        dg_ref[...] += jnp.sum(dy * xh, axis=0, keepdims=True)
        dxh = dy * gv
        dx_ref[...] = r * (dxh - xh * jnp.mean(dxh * xh, axis=-1, keepdims=True))

    row = pl.BlockSpec((tm, D), lambda i: (i, 0))
    return pl.pallas_call(
        body, name="loss_head", grid=(T // tm,),
        in_specs=[row, pl.BlockSpec((1, D), lambda i: (0, 0)), row],
        out_specs=[row, pl.BlockSpec((1, 128), lambda i: (0, 0)), pl.BlockSpec((1, D), lambda i: (0, 0))],
        out_shape=[SDS((T, D), F32), SDS((1, 128), F32), SDS((1, D), F32)], compiler_params=_params("arbitrary"),
    )(x, g, tgt)


def _gmlp_core(u, v, lng, lnb, wm_ref, bst_ref, pair):
    ug, dug = _gelu_and_grad(u)
    vg, dvg = _gelu_and_grad(v)
    mu = _gsum64(vg) * (1.0 / 64)
    d = vg - mu
    var = _gsum64(d * d) * (1.0 / 64)
    rstd = lax.rsqrt(var + NORM_EPS)
    xh = d * rstd
    vn = xh * lng + lnb
    vnb = vn.astype(BF16)
    lo = _lane(u.shape) < 64
    g0, g1 = 2 * pair, 2 * pair + 1
    mixed = jnp.where(lo, _dot(wm_ref[g0], vnb) + bst_ref[:, g0:g0 + 1], _dot(wm_ref[g1], vnb) + bst_ref[:, g1:g1 + 1])
    return ug, dug, dvg, rstd, xh, vnb, mixed, lo


def _gmlp_fwd(proj, lng, lnb, wm, bst, tag):
    T = proj.shape[0]

    def body(u_ref, v_ref, z_ref, lng_ref, lnb_ref, wm_ref, bst_ref, y_ref):
        for pair in range(2):
            sl = slice(128 * pair, 128 * pair + 128)
            ug, _, _, _, _, _, mixed, _ = _gmlp_core(u_ref[:, sl], v_ref[:, sl], lng_ref[:, sl], lnb_ref[:, sl],
                                                     wm_ref, bst_ref, pair)
            sz, _ = _silu_and_grad(z_ref[:, sl])
            y_ref[:, sl] = (ug * mixed * sz).astype(BF16)

    col = lambda c: pl.BlockSpec((CHUNK, A_WIDTH), lambda i, c=c: (i, c // A_WIDTH))
    full = lambda a: pl.BlockSpec(a.shape, lambda i, n=a.ndim: (0,) * n)
    return pl.pallas_call(
        body, name=f"gmlp_fwd_{tag}", grid=(T // CHUNK,),
        in_specs=[col(COL_AU), col(COL_AV), col(COL_AZ), full(lng), full(lnb), full(wm), full(bst)],
        out_specs=pl.BlockSpec((CHUNK, A_WIDTH), lambda i: (i, 0)), out_shape=SDS((T, A_WIDTH), BF16),
        compiler_params=_params("parallel"),
    )(proj, proj, proj, lng, lnb, wm, bst)


def _gmlp_bwd(proj, dy, lng, lnb, wm, wmt, bst, tag):
    T = proj.shape[0]
    n = T // CHUNK

    def body(u_ref, v_ref, z_ref, dy_ref, lng_ref, lnb_ref, wm_ref, wmt_ref, bst_ref,
             da_ref, dwm_ref, dbst_ref, dlng_ref, dlnb_ref):
        @pl.when(pl.program_id(0) == 0)
        def _():
            dwm_ref[...] = jnp.zeros_like(dwm_ref)
            dbst_ref[...] = jnp.zeros_like(dbst_ref)
            dlng_ref[...] = jnp.zeros_like(dlng_ref)
            dlnb_ref[...] = jnp.zeros_like(dlnb_ref)

        lane = _lane((CHUNK, 128))
        dbst = dbst_ref[...]
        for pair in range(2):
            sl = slice(128 * pair, 128 * pair + 128)
            lng_p = lng_ref[:, sl]
            ug, dug, dvg, rstd, xh, vnb, mixed, lo = _gmlp_core(u_ref[:, sl], v_ref[:, sl], lng_p, lnb_ref[:, sl],
                                                                wm_ref, bst_ref, pair)
            sz, dsz = _silu_and_grad(z_ref[:, sl])
            dyv = dy_ref[:, sl]
            out = ug * mixed
            dz = dyv * out * dsz
            dout = dyv * sz
            du = dout * mixed * dug
            dmix = dout * ug
            g0, g1 = 2 * pair, 2 * pair + 1
            dm0 = jnp.where(lo, dmix, 0.0)
            dm1 = jnp.where(lo, 0.0, dmix)
            dbst = dbst + jnp.where(lane == g0, jnp.sum(dm0, axis=-1, keepdims=True), 0.0)
            dbst = dbst + jnp.where(lane == g1, jnp.sum(dm1, axis=-1, keepdims=True), 0.0)
            dwm_ref[g0] += _dot_nt(dm0.astype(BF16), vnb)
            dwm_ref[g1] += _dot_nt(dm1.astype(BF16), vnb)
            dmb = dmix.astype(BF16)
            dvn = jnp.where(lo, _dot(wmt_ref[g0], dmb), _dot(wmt_ref[g1], dmb))
            dlng_ref[:, sl] += jnp.sum(dvn * xh, axis=0, keepdims=True)
            dlnb_ref[:, sl] += jnp.sum(dvn, axis=0, keepdims=True)
            dxh = dvn * lng_p
            m1 = _gsum64(dxh) * (1.0 / 64)
            m2 = _gsum64(dxh * xh) * (1.0 / 64)
            dv = rstd * (dxh - m1 - xh * m2) * dvg
            da_ref[:, COL_AU + 128 * pair:COL_AU + 128 * pair + 128] = du.astype(BF16)
            da_ref[:, COL_AV + 128 * pair:COL_AV + 128 * pair + 128] = dv.astype(BF16)
            da_ref[:, COL_AZ + 128 * pair:COL_AZ + 128 * pair + 128] = dz.astype(BF16)
        dbst_ref[...] = dbst

        @pl.when(pl.program_id(0) == n - 1)
        def _():
            causal = _lane((CHUNK, CHUNK)) <= _row((CHUNK, CHUNK))
            for g in range(A_GROUPS):
                dwm_ref[g] = jnp.where(causal, dwm_ref[g], 0.0)

    col = lambda c: pl.BlockSpec((CHUNK, A_WIDTH), lambda i, c=c: (i, c // A_WIDTH))
    full = lambda a: pl.BlockSpec(a.shape, lambda i, n=a.ndim: (0,) * n)
    acc = lambda s: pl.BlockSpec(s, lambda i, n=len(s): (0,) * n)
    return pl.pallas_call(
        body, name=f"gmlp_bwd_{tag}", grid=(n,),
        in_specs=[col(COL_AU), col(COL_AV), col(COL_AZ), pl.BlockSpec((CHUNK, A_WIDTH), lambda i: (i, 0)),
                  full(lng), full(lnb), full(wm), full(wmt), full(bst)],
        out_specs=[pl.BlockSpec((CHUNK, 3 * A_WIDTH), lambda i: (i, 0)), acc((A_GROUPS, CHUNK, CHUNK)),
                   acc((CHUNK, 128)), acc((1, A_WIDTH)), acc((1, A_WIDTH))],
        out_shape=[SDS((T, 3 * A_WIDTH), BF16), SDS((A_GROUPS, CHUNK, CHUNK), F32), SDS((CHUNK, 128), F32),
                   SDS((1, A_WIDTH), F32), SDS((1, A_WIDTH), F32)],
        compiler_params=_params("arbitrary"),
    )(proj, proj, proj, dy, lng, lnb, wm, wmt, bst)


def _hgrn_consts():
    r, c = _row((CHUNK, CHUNK)), _lane((CHUNK, CHUNK))
    same = (r >> SUB_SHIFT) == (c >> SUB_SHIFT)
    lsub = jnp.where(same & (c <= r), 1.0, 0.0).astype(BF16)
    usub = jnp.where(same & (c >= r), 1.0, 0.0).astype(BF16)
    bsub = jnp.where(same, 1.0, 0.0).astype(BF16)
    return lsub, usub, bsub


def _hgrn_gates(qv, zf, lbp):
    sq, dsq = _silu_and_grad(qv)
    qt = sq * Q_SCALE
    sg = _sigmoid(zf)
    sgn = _sigmoid(-zf)
    f = lbp + (1.0 - lbp) * sg
    g = jnp.log(jnp.maximum(f, F_FLOOR))
    kf = (1.0 - lbp) * sgn
    return qt, dsq, sg, sgn, f, g, kf


def _hgrn_intra_fwd(qt, kf, b, v, mbd):
    rid = _row((SUB, 128))
    parts = []
    for s in range(SUB):
        e = jnp.exp(jnp.minimum(b - b[s:s + 1, :], 0.0))
        parts.append(jnp.where(rid >= s, qt * kf[s:s + 1, :] * e, 0.0))
    a = _dot(jnp.concatenate(parts, axis=0).astype(BF16), mbd)
    o = jnp.zeros((SUB, 128), F32)
    for s in range(SUB):
        o = o + a[SUB * s:SUB * s + SUB, :] * v[s:s + 1, :]
    return o


def _hgrn_intra_bwd(qt, kf, b, v, do, mbd, rsum):
    rid = _row((SUB, 128))
    ps, das, kes, es = [], [], [], []
    for s in range(SUB):
        e = jnp.where(rid >= s, jnp.exp(jnp.minimum(b - b[s:s + 1, :], 0.0)), 0.0)
        ke = kf[s:s + 1, :] * e
        es.append(e)
        kes.append(ke)
        ps.append(qt * ke)
        das.append(do * v[s:s + 1, :])
    a = _dot(jnp.concatenate(ps, axis=0).astype(BF16), mbd)
    da = _dot(jnp.concatenate(das, axis=0).astype(BF16), mbd)
    dqt = jnp.zeros((SUB, 128), F32)
    xs, ys = [], []
    for s in range(SUB):
        da_s = da[SUB * s:SUB * s + SUB, :]
        dqt = dqt + da_s * kes[s]
        xs.append(a[SUB * s:SUB * s + SUB, :] * do)
        ys.append(da_s * qt * es[s])
    xh = jnp.concatenate(xs, axis=0)
    yh = jnp.concatenate(ys, axis=0)
    xhi = xh.astype(BF16)
    yhi = yh.astype(BF16)
    dv = _dot(rsum, xhi) + _dot(rsum, (xh - xhi.astype(F32)).astype(BF16))
    dkf = _dot(rsum, yhi) + _dot(rsum, (yh - yhi.astype(F32)).astype(BF16))
    return dqt, dkf, dv


def _hgrn_norm_gate(o, z, onorm):
    ms = _gsum64(o * o) * (1.0 / 64)
    r = lax.rsqrt(ms + NORM_EPS)
    xh = o * r
    sz, dsz = _silu_and_grad(z)
    return xh, r, sz, dsz, xh * onorm


def _hgrn_fwd(proj, lb, onorm, tag):
    T = proj.shape[0]
    n = T // CHUNK
    nsub = CHUNK // SUB

    def body(q_ref, f_ref, i_ref, z_ref, lb_ref, on_ref, y_ref, o_ref, s0_ref, st_ref):
        @pl.when(pl.program_id(0) == 0)
        def _():
            st_ref[...] = jnp.zeros_like(st_ref)

        lsub, _, bsub = _hgrn_consts()
        mbd = _block_diag64()
        bdmask = mbd > 0
        rid = _row((CHUNK, 128))
        for pair in range(2):
            sl = slice(128 * pair, 128 * pair + 128)
            qt, _, _, _, _, g, kf = _hgrn_gates(q_ref[:, sl], f_ref[:, sl], lb_ref[:, sl])
            v = i_ref[:, sl]
            b = _dot3_left(lsub, g)
            bl = _dot3_left(bsub, g)
            qh = (qt * jnp.exp(b)).astype(BF16)
            kh = kf * jnp.exp(bl - b)
            dec = jnp.exp(bl)
            vtb = v.T.astype(BF16)
            st = st_ref[pair]
            s0_ref[0, pair] = st
            outs = []
            for sub in range(nsub):
                rs = slice(SUB * sub, SUB * sub + SUB)
                o_inter = _dot_nt(qh[rs], st.astype(BF16))
                outs.append(o_inter + _hgrn_intra_fwd(qt[rs], kf[rs], b[rs], v[rs], mbd))
                khm = jnp.where((rid >> SUB_SHIFT) == sub, kh, 0.0).astype(BF16)
                st = jnp.where(bdmask, st * dec[SUB * sub:SUB * sub + 1, :] + _dot(vtb, khm), 0.0)
            st_ref[pair] = st
            o = jnp.concatenate(outs, axis=0)
            o_ref[:, sl] = o
            _, _, sz, _, on = _hgrn_norm_gate(o, z_ref[:, sl], on_ref[:, sl])
            y_ref[:, sl] = (on * sz).astype(BF16)

    col = lambda c: pl.BlockSpec((CHUNK, B_WIDTH), lambda i, c=c: (i, c // B_WIDTH))
    full = lambda a: pl.BlockSpec(a.shape, lambda i, n=a.ndim: (0,) * n)
    return pl.pallas_call(
        body, name=f"hgrn_fwd_{tag}", grid=(n,),
        in_specs=[col(COL_BQ), col(COL_BF), col(COL_BI), col(COL_BZ), full(lb), full(onorm)],
        out_specs=[pl.BlockSpec((CHUNK, B_WIDTH), lambda i: (i, 0)), pl.BlockSpec((CHUNK, B_WIDTH), lambda i: (i, 0)),
                   pl.BlockSpec((1, 2, 128, 128), lambda i: (i, 0, 0, 0))],
        out_shape=[SDS((T, B_WIDTH), BF16), SDS((T, B_WIDTH), F32), SDS((n, 2, 128, 128), F32)],
        scratch_shapes=[pltpu.VMEM((2, 128, 128), F32)], compiler_params=_params("arbitrary"),
    )(proj, proj, proj, proj, lb, onorm)


def _hgrn_bwd(proj, dy, o_saved, s0, lb, onorm, tag):
    T = proj.shape[0]
    n = T // CHUNK
    nsub = CHUNK // SUB

    def body(q_ref, f_ref, i_ref, z_ref, dy_ref, o_ref, s0_ref, lb_ref, on_ref,
             db_ref, dlb_ref, don_ref, dst_ref, sts_ref):
        @pl.when(pl.program_id(0) == 0)
        def _():
            dst_ref[...] = jnp.zeros_like(dst_ref)
            dlb_ref[...] = jnp.zeros_like(dlb_ref)
            don_ref[...] = jnp.zeros_like(don_ref)

        lsub, usub, bsub = _hgrn_consts()
        mbd = _block_diag64()
        bdmask = mbd > 0
        rid = _row((CHUNK, 128))
        rsum = jnp.where((_lane((SUB, SUB * SUB)) >> SUB_SHIFT) == _row((SUB, SUB * SUB)), 1.0, 0.0).astype(BF16)
        for pair in range(2):
            sl = slice(128 * pair, 128 * pair + 128)
            lbp = lb_ref[:, sl]
            qv, zf = q_ref[:, sl], f_ref[:, sl]
            qt, dsq, sg, sgn, f, g, kf = _hgrn_gates(qv, zf, lbp)
            v = i_ref[:, sl]
            b = _dot3_left(lsub, g)
            bl = _dot3_left(bsub, g)
            eb = jnp.exp(b)
            ekb = jnp.exp(bl - b)
            qh = qt * eb
            kh = kf * ekb
            dec = jnp.exp(bl)
            vtb = v.T.astype(BF16)
            onp = on_ref[:, sl]
            ov = o_ref[:, sl]
            xh, r, sz, dsz, on = _hgrn_norm_gate(ov, z_ref[:, sl], onp)
            dyv = dy_ref[:, sl]
            dz = dyv * on * dsz
            don = dyv * sz
            cn = jnp.sum(don * xh, axis=0, keepdims=True)
            don_ref[...] += cn + pltpu.roll(cn, 64, axis=1)
            dxo = don * onp
            do = r * (dxo - xh * (_gsum64(dxo * xh) * (1.0 / 64)))
            dotb = do.T.astype(BF16)
            st = s0_ref[0, pair]
            for sub in range(nsub):
                sts_ref[sub] = st
                khm = jnp.where((rid >> SUB_SHIFT) == sub, kh, 0.0).astype(BF16)
                st = jnp.where(bdmask, st * dec[SUB * sub:SUB * sub + 1, :] + _dot(vtb, khm), 0.0)
            gst = dst_ref[pair]
            dqt_p, dkf_p, dv_p, dbl_p = [None] * nsub, [None] * nsub, [None] * nsub, [None] * nsub
            for sub in reversed(range(nsub)):
                rs = slice(SUB * sub, SUB * sub + SUB)
                st_in = sts_ref[sub]
                gb = gst.astype(BF16)
                dob = do[rs].astype(BF16)
                dqh = _dot(dob, st_in.astype(BF16))
                dkh = _dot(v[rs].astype(BF16), gb)
                dv_inter = _dot_nt(kh[rs].astype(BF16), gb)
                ddec = jnp.sum(gst * st_in, axis=0, keepdims=True)
                dec_row = dec[SUB * sub:SUB * sub + 1, :]
                qhm = jnp.where((rid >> SUB_SHIFT) == sub, qh, 0.0).astype(BF16)
                gst = jnp.where(bdmask, gst * dec_row + _dot(dotb, qhm), 0.0)
                dqt_i, dkf_i, dv_i = _hgrn_intra_bwd(qt[rs], kf[rs], b[rs], v[rs], do[rs], mbd, rsum)
                dkf_inter = dkh * ekb[rs]
                dqt_p[sub] = dqh * eb[rs] + dqt_i
                dkf_p[sub] = dkf_inter + dkf_i
                dv_p[sub] = dv_inter + dv_i
                row = jnp.sum(kf[rs] * dkf_inter, axis=0, keepdims=True) + ddec * dec_row
                dbl_p[sub] = jnp.broadcast_to(row, (SUB, 128))
            dst_ref[pair] = gst
            dqt = jnp.concatenate(dqt_p, axis=0)
            dkf = jnp.concatenate(dkf_p, axis=0)
            dv = jnp.concatenate(dv_p, axis=0)
            dg = _dot3_left(usub, qt * dqt - kf * dkf) + jnp.concatenate(dbl_p, axis=0)
            df = jnp.where(f > F_FLOOR, dg / f, 0.0)
            dlb_ref[:, sl] += jnp.sum(df * (1.0 - sg) - dkf * sgn, axis=0, keepdims=True)
            dfl = (1.0 - lbp) * sg * sgn * (df - dkf)
            dq = dqt * Q_SCALE * dsq
            db_ref[:, 0 * B_WIDTH + 128 * pair:0 * B_WIDTH + 128 * pair + 128] = dq.astype(BF16)
            db_ref[:, 1 * B_WIDTH + 128 * pair:1 * B_WIDTH + 128 * pair + 128] = dfl.astype(BF16)
            db_ref[:, 2 * B_WIDTH + 128 * pair:2 * B_WIDTH + 128 * pair + 128] = dv.astype(BF16)
            db_ref[:, 3 * B_WIDTH + 128 * pair:3 * B_WIDTH + 128 * pair + 128] = dz.astype(BF16)

    rev = lambda c: pl.BlockSpec((CHUNK, B_WIDTH), lambda i, c=c: (n - 1 - i, c // B_WIDTH))
    full = lambda a: pl.BlockSpec(a.shape, lambda i, n_=a.ndim: (0,) * n_)
    acc = lambda s: pl.BlockSpec(s, lambda i, n_=len(s): (0,) * n_)
    return pl.pallas_call(
        body, name=f"hgrn_bwd_{tag}", grid=(n,),
        in_specs=[rev(COL_BQ), rev(COL_BF), rev(COL_BI), rev(COL_BZ),
                  pl.BlockSpec((CHUNK, B_WIDTH), lambda i: (n - 1 - i, 1)),
                  pl.BlockSpec((CHUNK, B_WIDTH), lambda i: (n - 1 - i, 0)),
                  pl.BlockSpec((1, 2, 128, 128), lambda i: (n - 1 - i, 0, 0, 0)), full(lb), full(onorm)],
        out_specs=[pl.BlockSpec((CHUNK, 4 * B_WIDTH), lambda i: (n - 1 - i, 0)), acc((1, B_WIDTH)), acc((1, 128))],
        out_shape=[SDS((T, 4 * B_WIDTH), BF16), SDS((1, B_WIDTH), F32), SDS((1, 128), F32)],
        scratch_shapes=[pltpu.VMEM((2, 128, 128), F32), pltpu.VMEM((nsub, 128, 128), F32)],
        compiler_params=_params("arbitrary"),
    )(proj, proj, proj, proj, dy, o_saved, s0, lb, onorm)


def _lb_fwd(hgrn_lb):
    assert hgrn_lb.shape[0] == 2

    def body(x_ref, o_ref):
        x0, x1 = x_ref[0:1, :], x_ref[1:2, :]
        m = jnp.maximum(x0, x1)
        e0, e1 = jnp.exp(x0 - m), jnp.exp(x1 - m)
        p0, p1 = e0 / (e0 + e1), e1 / (e0 + e1)
        o_ref[0:1, :] = jnp.clip(p0 - p0, 0.0, 1.0 - 1e-6)
        o_ref[1:2, :] = jnp.clip((p0 + p1) - p0, 0.0, 1.0 - 1e-6)

    return pl.pallas_call(body, name="lb_fwd", out_shape=SDS(hgrn_lb.shape, F32))(hgrn_lb)


def _lb_bwd(hgrn_lb, dlb):
    def body(x_ref, d_ref, o_ref):
        x0, x1 = x_ref[0:1, :], x_ref[1:2, :]
        m = jnp.maximum(x0, x1)
        e0, e1 = jnp.exp(x0 - m), jnp.exp(x1 - m)
        p0, p1 = e0 / (e0 + e1), e1 / (e0 + e1)
        val = (p0 + p1) - p0
        dp1 = jnp.where((val > 0.0) & (val < 1.0 - 1e-6), d_ref[1:2, :], 0.0)
        inner = p1 * dp1
        o_ref[0:1, :] = p0 * (0.0 - inner)
        o_ref[1:2, :] = p1 * (dp1 - inner)

    return pl.pallas_call(body, name="lb_bwd", out_shape=SDS(hgrn_lb.shape, F32))(hgrn_lb, dlb)


def _fox_prep(proj, bf, tag):
    T = proj.shape[0]
    n = T // CHUNK

    def body(q0_ref, q1_ref, k0_ref, k1_ref, v0_ref, v1_ref, fl_ref, bf_ref, qo_ref, ko_ref, vt_ref, carry_ref):
        for p, v_ref in enumerate((v0_ref, v0_ref, v1_ref, v1_ref)):
            vt_ref[p, 0] = v_ref[:, 128 * (p % 2):128 * (p % 2) + 128].T.astype(BF16)

        @pl.when(pl.program_id(0) == 0)
        def _():
            carry_ref[...] = jnp.zeros_like(carry_ref)

        ltri = jnp.where(_lane((CHUNK, CHUNK)) <= _row((CHUNK, CHUNK)), 1.0, 0.0).astype(BF16)
        lf = jax.nn.log_sigmoid(fl_ref[...] + bf_ref[...])
        c = _dot3_left(ltri, lf) + carry_ref[...]
        carry_ref[...] = c[CHUNK - 1:CHUNK, :]
        lane = _lane((CHUNK, 128))
        feat = lane < 64
        ones_q = (lane >= 67) & (lane <= 69)
        ones_k = (lane >= 64) & (lane <= 66)
        qrefs, krefs = (q0_ref, q1_ref), (k0_ref, k1_ref)
        for h in range(C_HEADS):
            blk = slice(128 * ((h // 2) % 2), 128 * ((h // 2) % 2) + 128)
            qp, kp = qrefs[h // 4][:, blk], krefs[h // 4][:, blk]
            if h % 2:
                qp, kp = pltpu.roll(qp, 64, axis=1), pltpu.roll(kp, 64, axis=1)
            ch = jnp.broadcast_to(c[:, h:h + 1], (CHUNK, 128))
            hi = ch.astype(BF16).astype(F32)
            r1 = ch - hi
            mid = r1.astype(BF16).astype(F32)
            lo = r1 - mid
            aq = jnp.where(lane == 64, hi, jnp.where(lane == 65, mid, jnp.where(lane == 66, lo,
                           jnp.where(ones_q, 1.0, 0.0))))
            ak = jnp.where(lane == 67, -hi, jnp.where(lane == 68, -mid, jnp.where(lane == 69, -lo,
                           jnp.where(ones_k, 1.0, 0.0))))
            qo_ref[:, 128 * h:128 * h + 128] = jnp.where(feat, qp * Q_SCALE, aq).astype(BF16)
            ko_ref[:, 128 * h:128 * h + 128] = jnp.where(feat, kp, ak).astype(BF16)

    w = 256
    col = lambda c: pl.BlockSpec((CHUNK, w), lambda i, c=c: (i, c // w))
    return pl.pallas_call(
        body, name=f"fox_prep_{tag}", grid=(n,),
        in_specs=[col(COL_CQ), col(COL_CQ + w), col(COL_CK), col(COL_CK + w), col(COL_CV), col(COL_CV + w),
                  pl.BlockSpec((CHUNK, 128), lambda i: (i, COL_CF // 128)), pl.BlockSpec((1, 128), lambda i: (0, 0))],
        out_specs=[pl.BlockSpec((CHUNK, C_HEADS * 128), lambda i: (i, 0))] * 2
        + [pl.BlockSpec((C_HEADS // 2, 1, 128, CHUNK), lambda i: (0, i, 0, 0))],
        out_shape=[SDS((T, C_HEADS * 128), BF16)] * 2 + [SDS((C_HEADS // 2, n, 128, CHUNK), BF16)],
        scratch_shapes=[pltpu.VMEM((1, 128), F32)], compiler_params=_params("arbitrary"),
    )(proj, proj, proj, proj, proj, proj, proj, bf)


FOX_TILE = 256


def _fox_fwd(qt, kt, vt, proj, tag):
    T = proj.shape[0]
    tq = _tile(T, FOX_TILE)
    nq, nsub = T // tq, tq // CHUNK

    def body(q_ref, k_ref, vt_ref, z_ref, o_ref, lse_ref, y_ref, acc_ref):
        i = pl.program_id(1)
        qs = (q_ref[:, 0:128], q_ref[:, 128:256])
        acc_ref[...] = jnp.zeros_like(acc_ref)

        def scores(j):
            kb = k_ref[pl.ds(pl.multiple_of(j * tq, tq), tq), :]
            return tuple(_dot_nt(kb[:, 128 * h:128 * h + 128], qs[h]) for h in range(2))

        def block(j, carry, diagonal):
            sts = carry[4:6]
            nxt = () if diagonal else scores(j + 1)
            new = []
            for h in range(2):
                m, l = carry[2 * h], carry[2 * h + 1]
                st = sts[h]
                if diagonal:
                    st = jnp.where(_row((tq, tq)) <= _lane((tq, tq)), st, -jnp.inf)
                m_new = jnp.maximum(m, _colreduce(st, jnp.maximum))
                pt = jnp.exp(st - m_new)
                alpha = jnp.exp(m - m_new)
                ptb = pt.astype(BF16)
                rows = slice(64 * h, 64 * h + 64)
                pv = _dot(vt_ref[0, nsub * j, rows, :], ptb[0:CHUNK, :])
                for c in range(1, nsub):
                    pv = pv + _dot(vt_ref[0, nsub * j + c, rows, :], ptb[CHUNK * c:CHUNK * c + CHUNK, :])
                acc_ref[rows, :] = alpha * acc_ref[rows, :] + pv
                new += [m_new, alpha * l + _colreduce(pt, jnp.add)]
            return tuple(new) + nxt

        init = (jnp.full((1, tq), -jnp.inf, F32), jnp.zeros((1, tq), F32)) * 2 + scores(0)
        carry = lax.fori_loop(0, i, lambda j, c: block(j, c, False), init)
        m0, l0, m1, l1 = block(i, carry, True)
        inv = jnp.where(_row((128, tq)) < 64, 1.0 / l0, 1.0 / l1)
        o = (acc_ref[...] * inv).T
        o_ref[...] = o
        r8 = _row((8, tq))
        lse_ref[0, 0] = jnp.where(r8 == 0, m0 + jnp.log(l0), jnp.where(r8 == 1, m1 + jnp.log(l1), 0.0))
        sz, _ = _silu_and_grad(z_ref[...])
        y_ref[...] = (o * sz).astype(BF16)

    blk = pl.BlockSpec((tq, 128), lambda p, i: (i, p))
    return pl.pallas_call(
        body, name=f"fox_fwd_{tag}", grid=(C_HEADS // 2, nq),
        in_specs=[pl.BlockSpec((tq, 256), lambda p, i: (i, p)), pl.BlockSpec((T, 256), lambda p, i: (0, p)),
                  pl.BlockSpec((1, T // CHUNK, 128, CHUNK), lambda p, i: (p, 0, 0, 0)),
                  pl.BlockSpec((tq, 128), lambda p, i: (i, COL_CZ // 128 + p))],
        out_specs=[blk, pl.BlockSpec((1, 1, 8, tq), lambda p, i: (p, i, 0, 0)), blk],
        out_shape=[SDS((T, C_WIDTH), F32), SDS((C_HEADS // 2, nq, 8, tq), F32), SDS((T, C_WIDTH), BF16)],
        scratch_shapes=[pltpu.VMEM((128, tq), F32)], compiler_params=_params("parallel", "arbitrary"),
    )(qt, kt, vt, proj)


def _fox_bwd_prep(proj, dy, o, tag):
    T = proj.shape[0]
    tq = _tile(T, FOX_TILE)

    def body(z_ref, dy_ref, o_ref, do_ref, dl_ref, dz_ref):
        sz, dsz = _silu_and_grad(z_ref[...])
        dyv, ov = dy_ref[...], o_ref[...]
        do = dyv * sz
        do_ref[...] = do.astype(BF16)
        dz_ref[...] = (dyv * ov * dsz).astype(BF16)
        sel = jnp.where((_lane((16, 128)) >> 6) == _row((16, 128)), 1.0, 0.0).astype(BF16)
        hi, mid, lo = _split3(do * ov)
        dl_ref[0, 0] = (_dot_nt(sel, hi) + _dot_nt(sel, mid) + _dot_nt(sel, lo))[0:8, :]

    blk = pl.BlockSpec((tq, 128), lambda i, p: (i, p))
    return pl.pallas_call(
        body, name=f"fox_bwd_prep_{tag}", grid=(T // tq, C_WIDTH // 128),
        in_specs=[pl.BlockSpec((tq, 128), lambda i, p: (i, COL_CZ // 128 + p)),
                  pl.BlockSpec((tq, 128), lambda i, p: (i, (A_WIDTH + B_WIDTH) // 128 + p)), blk],
        out_specs=[blk, pl.BlockSpec((1, 1, 8, tq), lambda i, p: (p, i, 0, 0)), blk],
        out_shape=[SDS((T, C_WIDTH), BF16), SDS((C_HEADS // 2, T // tq, 8, tq), F32), SDS((T, C_WIDTH), BF16)],
        compiler_params=_params("parallel", "parallel"),
    )(proj, dy, o)


def _fox_bwd(qt, kt, proj, do, lse, delta, tag):
    T = proj.shape[0]
    tq = _tile(T, FOX_TILE)
    nq = T // tq

    def body(q_ref, k_ref, v_ref, do_ref, lse_ref, dl_ref, dq_ref, dk_ref, dv_ref, dvacc_ref):
        j = pl.program_id(1)

        @pl.when(j == 0)
        def _():
            dq_ref[...] = jnp.zeros_like(dq_ref)

        dk_ref[...] = jnp.zeros_like(dk_ref)
        dvacc_ref[...] = jnp.zeros_like(dvacc_ref)
        ks = (k_ref[:, 0:128], k_ref[:, 128:256])
        kts = tuple(k.astype(F32).T.astype(BF16) for k in ks)
        vb = v_ref[...].astype(BF16)
        lo = _lane((tq, 128)) < 64

        def operands(i):
            q0 = pl.multiple_of(i * tq, tq)
            qb = q_ref[pl.ds(q0, tq), :]
            dob = do_ref[pl.ds(q0, tq), :]
            qhs = (qb[:, 0:128], qb[:, 128:256])
            dohs = (jnp.where(lo, dob, jnp.zeros_like(dob)), jnp.where(lo, jnp.zeros_like(dob), dob))
            return qhs, dohs

        def scores(i):
            qhs, dohs = operands(i)
            return tuple(_dot_nt(ks[h], qhs[h]) for h in range(2)) + tuple(_dot_nt(vb, dohs[h]) for h in range(2))

        def block(i, sc, diagonal):
            nxt = scores(jnp.minimum(i + 1, nq - 1))
            qhs, dohs = operands(i)
            lsev = lse_ref[0, i]
            dlv = dl_ref[0, i]
            pts, dsts = [], []
            for h in range(2):
                pt = jnp.exp(sc[h] - lsev[h:h + 1, :])
                if diagonal:
                    pt = jnp.where(_row((tq, tq)) <= _lane((tq, tq)), pt, 0.0)
                dsts.append((pt * (sc[2 + h] - dlv[h:h + 1, :])).astype(BF16))
                pts.append(pt.astype(BF16))
            dvacc_ref[...] += _dot(jnp.concatenate(pts, axis=1), jnp.concatenate(dohs, axis=0))
            for h in range(2):
                dk_ref[:, 128 * h:128 * h + 128] += _dot(dsts[h], qhs[h])
                dq_ref[h, i] += _dot(kts[h], dsts[h])
            return nxt

        sc = block(j, scores(j), True)
        lax.fori_loop(j + 1, nq, lambda i, c: block(i, c, False), sc)
        dv_ref[...] = dvacc_ref[...].astype(BF16)

    full = lambda w: pl.BlockSpec((T, w), lambda p, j: (0, p))
    stat = pl.BlockSpec((1, nq, 8, tq), lambda p, j: (p, 0, 0, 0))
    return pl.pallas_call(
        body, name=f"fox_bwd_{tag}", grid=(C_HEADS // 2, nq),
        in_specs=[full(256), pl.BlockSpec((tq, 256), lambda p, j: (j, p)),
                  pl.BlockSpec((tq, 128), lambda p, j: (j, COL_CV // 128 + p)), full(128), stat, stat],
        out_specs=[pl.BlockSpec((2, nq, 128, tq), lambda p, j: (p, 0, 0, 0)), pl.BlockSpec((tq, 256), lambda p, j: (j, p)),
                   pl.BlockSpec((tq, 128), lambda p, j: (j, p))],
        out_shape=[SDS((C_HEADS, nq, 128, tq), F32), SDS((T, C_HEADS * 128), F32), SDS((T, C_WIDTH), BF16)],
        scratch_shapes=[pltpu.VMEM((tq, 128), F32)], compiler_params=_params("parallel", "arbitrary"),
    )(qt, kt, proj, do, lse, delta)


def _fox_bwd_post(dqt, dkt, proj, bf, tag):
    T = proj.shape[0]
    tq = _tile(T, FOX_TILE)
    n = T // tq

    def body(dq_ref, dk_ref, fl_ref, bf_ref, oq_ref, ok_ref, ofl_ref, dbf_ref, carry_ref):
        @pl.when(pl.program_id(0) == 0)
        def _():
            carry_ref[...] = jnp.zeros_like(carry_ref)
            dbf_ref[...] = jnp.zeros_like(dbf_ref)

        lane = _lane((tq, 128))
        lo = lane < 64
        dqs = [dq_ref[h, 0].T for h in range(C_HEADS)]
        dc = jnp.zeros((tq, 128), F32)
        for h in range(C_HEADS):
            dc = dc + jnp.where(lane == h, dqs[h][:, 64:65] - dk_ref[:, 128 * h + 67:128 * h + 68], 0.0)
        utri = jnp.where(_lane((tq, tq)) >= _row((tq, tq)), 1.0, 0.0).astype(BF16)
        dlf = _dot3_left(utri, dc) + carry_ref[...]
        carry_ref[...] = dlf[0:1, :]
        dfl = jnp.where(lane < C_HEADS, dlf * _sigmoid(-(fl_ref[...] + bf_ref[...])), 0.0)
        ofl_ref[...] = dfl.astype(BF16)
        dbf_ref[...] += jnp.sum(dfl, axis=0, keepdims=True)
        for p in range(C_HEADS // 2):
            a, b = 128 * (2 * p), 128 * (2 * p + 1)
            oq_ref[:, 128 * p:128 * p + 128] = (
                jnp.where(lo, dqs[2 * p], pltpu.roll(dqs[2 * p + 1], 64, axis=1)) * Q_SCALE).astype(BF16)
            ok_ref[:, 128 * p:128 * p + 128] = jnp.where(
                lo, dk_ref[:, a:a + 128], pltpu.roll(dk_ref[:, b:b + 128], 64, axis=1)).astype(BF16)

    rev = lambda w: pl.BlockSpec((tq, w), lambda i: (n - 1 - i, 0))
    return pl.pallas_call(
        body, name=f"fox_bwd_post_{tag}", grid=(n,),
        in_specs=[pl.BlockSpec((C_HEADS, 1, 128, tq), lambda i: (0, n - 1 - i, 0, 0)), rev(C_HEADS * 128),
                  pl.BlockSpec((tq, 128), lambda i: (n - 1 - i, COL_CF // 128)), pl.BlockSpec((1, 128), lambda i: (0, 0))],
        out_specs=[rev(C_WIDTH), rev(C_WIDTH), rev(128), pl.BlockSpec((1, 128), lambda i: (0, 0))],
        out_shape=[SDS((T, C_WIDTH), BF16), SDS((T, C_WIDTH), BF16), SDS((T, 128), BF16), SDS((1, 128), F32)],
        scratch_shapes=[pltpu.VMEM((1, 128), F32)], compiler_params=_params("arbitrary"),
    )(dqt, dkt, proj, bf)


def _adamw_math(w, g, m, v):
    m = ADAM_B1 * m + (1.0 - ADAM_B1) * g
    v = ADAM_B2 * v + (1.0 - ADAM_B2) * (g * g)
    delta = -ADAM_LR * ((m / ADAM_C1) / (jnp.sqrt(v / ADAM_C2) + ADAM_EPS) + ADAM_WD * w)
    return delta, m, v


def _adamw_pair(w, m, v, ga, gb, name):
    L, R, C = w.shape
    tr = _tile(R, 256)

    def body(w_ref, m_ref, v_ref, ga_ref, gb_ref, g_ref, d_ref, nm_ref, nv_ref):
        g = ga_ref[...] + gb_ref[...]
        g_ref[...] = g
        d_ref[...], nm_ref[...], nv_ref[...] = _adamw_math(w_ref[...], g, m_ref[...], v_ref[...])

    blk = pl.BlockSpec((1, tr, C), lambda l, i: (l, i, 0))
    return pl.pallas_call(
        body, name=name, grid=(L, R // tr), in_specs=[blk] * 5, out_specs=[blk] * 4,
        out_shape=[SDS(w.shape, F32)] * 4, compiler_params=_params("parallel", "parallel"),
    )(w, m, v, ga, gb)


def _adamw_small(w, m, v, gall):
    R = w.shape[0]

    def body(w_ref, m_ref, v_ref, g_ref, go_ref, d_ref, nm_ref, nv_ref):
        g = g_ref[0]
        for k in range(1, N_DEV):
            g = g + g_ref[k]
        go_ref[...] = g
        d_ref[...], nm_ref[...], nv_ref[...] = _adamw_math(w_ref[...], g, m_ref[...], v_ref[...])

    return pl.pallas_call(body, name="adamw_small", out_shape=[SDS((R, 128), F32)] * 4,
                          compiler_params=pltpu.CompilerParams(vmem_limit_bytes=VMEM_LIMIT))(w, m, v, gall)


def _sum_chips(parts, name):
    _, L, R, C = parts.shape
    tr = _tile(R, 256)

    def body(p_ref, o_ref):
        o_ref[...] = ((p_ref[0] + p_ref[1]) + p_ref[2]) + p_ref[3]

    return pl.pallas_call(
        body, name=name, grid=(L, R // tr),
        in_specs=[pl.BlockSpec((N_CHIPS, 1, tr, C), lambda l, i: (0, l, i, 0))],
        out_specs=pl.BlockSpec((1, tr, C), lambda l, i: (l, i, 0)), out_shape=SDS((L, R, C), F32),
        compiler_params=_params("parallel", "parallel"),
    )(parts)


ANY = pl.BlockSpec(memory_space=pl.ANY)


def _mesh_pos():
    return lax.axis_index("x"), lax.axis_index("y"), lax.axis_index("c")


def _other_chips(x, y):
    return [(1 - x, y), (x, 1 - y), (1 - x, 1 - y)]


def _allgather_weights(win, wout):
    def body(win_ref, wout_ref, gin_ref, gout_ref, send_sems, recv_sems, local_sems):
        x, y, c = _mesh_pos()
        me = 2 * x + y
        pairs = ((win_ref, gin_ref), (wout_ref, gout_ref))
        local = [pltpu.make_async_copy(src, dst.at[me], local_sems.at[a]) for a, (src, dst) in enumerate(pairs)]
        for cp in local:
            cp.start()
        sends = []
        for j, (px, py) in enumerate(_other_chips(x, y)):
            for a, (src, dst) in enumerate(pairs):
                cp = pltpu.make_async_remote_copy(src_ref=src, dst_ref=dst.at[me], send_sem=send_sems.at[2 * j + a],
                                                  recv_sem=recv_sems.at[2 * j + a], device_id=(px, py, c),
                                                  device_id_type=MESH_ID)
                cp.start()
                sends.append(cp)
        for j, (px, py) in enumerate(_other_chips(x, y)):
            for a, (src, dst) in enumerate(pairs):
                pltpu.make_async_remote_copy(src_ref=src, dst_ref=dst.at[2 * px + py], send_sem=send_sems.at[2 * j + a],
                                             recv_sem=recv_sems.at[2 * j + a], device_id=(px, py, c),
                                             device_id_type=MESH_ID).wait_recv()
        for cp in sends:
            cp.wait_send()
        for cp in local:
            cp.wait()

    return pl.pallas_call(
        body, name="allgather_weights", in_specs=[ANY, ANY], out_specs=[ANY, ANY],
        out_shape=[SDS((N_CHIPS,) + win.shape, win.dtype), SDS((N_CHIPS,) + wout.shape, wout.dtype)],
        scratch_shapes=[pltpu.SemaphoreType.DMA((6,)), pltpu.SemaphoreType.DMA((6,)), pltpu.SemaphoreType.DMA((2,))],
        compiler_params=pltpu.CompilerParams(has_side_effects=True),
    )(win, wout)


def _exchange_grads(gin, gout, small):
    def body(gin_ref, gout_ref, small_ref, rin_ref, rout_ref, rsmall_ref, send_sems, recv_sems, local_sems):
        x, y, c = _mesh_pos()
        me_chip = 2 * x + y
        me_dev = 4 * x + 2 * y + c
        pairs = ((gin_ref, rin_ref), (gout_ref, rout_ref))
        local = [pltpu.make_async_copy(src.at[me_chip], dst.at[me_chip], local_sems.at[a])
                 for a, (src, dst) in enumerate(pairs)]
        local.append(pltpu.make_async_copy(small_ref, rsmall_ref.at[me_dev], local_sems.at[2]))
        for cp in local:
            cp.start()
        sends = []
        for j, (px, py) in enumerate(_other_chips(x, y)):
            for a, (src, dst) in enumerate(pairs):
                cp = pltpu.make_async_remote_copy(src_ref=src.at[2 * px + py], dst_ref=dst.at[me_chip],
                                                  send_sem=send_sems.at[2 * j + a], recv_sem=recv_sems.at[2 * j + a],
                                                  device_id=(px, py, c), device_id_type=MESH_ID)
                cp.start()
                sends.append(cp)
        peers = [(k, ((1 - x) if k & 4 else x, (1 - y) if k & 2 else y, (1 - c) if k & 1 else c)) for k in range(1, N_DEV)]
        for k, peer in peers:
            cp = pltpu.make_async_remote_copy(src_ref=small_ref, dst_ref=rsmall_ref.at[me_dev], send_sem=send_sems.at[5 + k],
                                              recv_sem=recv_sems.at[5 + k], device_id=peer, device_id_type=MESH_ID)
            cp.start()
            sends.append(cp)
        for j, (px, py) in enumerate(_other_chips(x, y)):
            for a, (src, dst) in enumerate(pairs):
                pltpu.make_async_remote_copy(src_ref=src.at[me_chip], dst_ref=dst.at[2 * px + py],
                                             send_sem=send_sems.at[2 * j + a], recv_sem=recv_sems.at[2 * j + a],
                                             device_id=(px, py, c), device_id_type=MESH_ID).wait_recv()
        for k, (px, py, pc) in peers:
            pltpu.make_async_remote_copy(src_ref=small_ref, dst_ref=rsmall_ref.at[4 * px + 2 * py + pc],
                                         send_sem=send_sems.at[5 + k], recv_sem=recv_sems.at[5 + k],
                                         device_id=(px, py, pc), device_id_type=MESH_ID).wait_recv()
        for cp in sends:
            cp.wait_send()
        for cp in local:
            cp.wait()

    return pl.pallas_call(
        body, name="exchange_grads", in_specs=[ANY, ANY, ANY], out_specs=[ANY, ANY, ANY],
        out_shape=[SDS(gin.shape, F32), SDS(gout.shape, F32), SDS((N_DEV,) + small.shape, F32)],
        scratch_shapes=[pltpu.SemaphoreType.DMA((13,)), pltpu.SemaphoreType.DMA((13,)), pltpu.SemaphoreType.DMA((3,))],
        compiler_params=pltpu.CompilerParams(has_side_effects=True),
    )(gin, gout, small)


def _swap_cores(pin, pout):
    def body(pin_ref, pout_ref, oin_ref, oout_ref, send_sems, recv_sems):
        x, y, c = _mesh_pos()
        cps = [pltpu.make_async_remote_copy(src_ref=src, dst_ref=dst, send_sem=send_sems.at[a], recv_sem=recv_sems.at[a],
                                            device_id=(x, y, 1 - c), device_id_type=MESH_ID)
               for a, (src, dst) in enumerate(((pin_ref, oin_ref), (pout_ref, oout_ref)))]
        for cp in cps:
            cp.start()
        for cp in cps:
            cp.wait()

    return pl.pallas_call(
        body, name="swap_cores", in_specs=[ANY, ANY], out_specs=[ANY, ANY],
        out_shape=[SDS(pin.shape, F32), SDS(pout.shape, F32)],
        scratch_shapes=[pltpu.SemaphoreType.DMA((2,)), pltpu.SemaphoreType.DMA((2,))],
        compiler_params=pltpu.CompilerParams(has_side_effects=True),
    )(pin, pout)


def _pack_small(parts):
    flat = [jnp.pad(p.reshape(-1), (0, (-p.size) % 128)) for p in parts]
    v = jnp.concatenate(flat)
    return jnp.pad(v, (0, (-v.size) % 1024)).reshape(-1, 128)


def _unpack_small(packed):
    flat = packed.reshape(-1)
    out, off = [], 0
    for _, shape in SMALL_PARAMS:
        size = math.prod(shape)
        out.append(flat[off:off + size].reshape(shape))
        off += size + (-size) % 128
    return out


def _layer_consts(l, gmlp_ln_g, gmlp_ln_b, gmlp_w_s, gmlp_b_s, hgrn_onorm_g, fox_b_f):
    causal = jnp.tril(jnp.ones((CHUNK, CHUNK), bool))
    wm = jnp.where(causal[None], gmlp_w_s[l], 0.0)
    return dict(
        lng=gmlp_ln_g[l].reshape(1, A_WIDTH), lnb=gmlp_ln_b[l].reshape(1, A_WIDTH),
        wm=wm.astype(BF16), wmt=jnp.swapaxes(wm, 1, 2).astype(BF16),
        bst=jnp.pad(gmlp_b_s[l].T, ((0, 0), (0, 128 - A_GROUPS))),
        onorm=jnp.tile(hgrn_onorm_g[l], 4).reshape(1, B_WIDTH),
        bf=jnp.pad(fox_b_f[l], (0, 128 - C_HEADS)).reshape(1, 128),
    )


def kernel(x, norm_g, w_in, w_out, gmlp_ln_g, gmlp_ln_b, gmlp_w_s, gmlp_b_s, hgrn_lb, hgrn_onorm_g, fox_b_f, final_norm_g, loss_target, m_norm_g, m_w_in, m_w_out, m_gmlp_ln_g, m_gmlp_ln_b, m_gmlp_w_s, m_gmlp_b_s, m_hgrn_lb, m_hgrn_onorm_g, m_fox_b_f, m_final_norm_g, v_norm_g, v_w_in, v_w_out, v_gmlp_ln_g, v_gmlp_ln_b, v_gmlp_w_s, v_gmlp_b_s, v_hgrn_lb, v_hgrn_onorm_g, v_fox_b_f, v_final_norm_g):
    T = x.shape[1]
    shard_in = w_in.shape[2]
    shard_out = w_out.shape[1]
    xs = x.reshape(T, D_MODEL)
    tgt = loss_target.reshape(T, D_MODEL)

    gin_w, gout_w = _allgather_weights(w_in.astype(BF16), w_out.astype(BF16))
    w_in_full = jnp.concatenate([gin_w[k] for k in range(N_CHIPS)], axis=-1)
    w_in_full = jnp.pad(w_in_full, ((0, 0), (0, 0), (0, D_IN_PAD - D_IN)))
    w_out_full = jnp.concatenate([gout_w[k] for k in range(N_CHIPS)], axis=1)

    lb_all = _lb_fwd(hgrn_lb)
    consts = [_layer_consts(l, gmlp_ln_g, gmlp_ln_b, gmlp_w_s, gmlp_b_s, hgrn_onorm_g, fox_b_f) for l in range(DEPTH)]

    saved = []
    xl = xs
    for l in range(DEPTH):
        cs = consts[l]
        tag = f"l{l}"
        h, proj = _inproj(xl, norm_g[l].reshape(1, D_MODEL), w_in_full[l], tag)
        ya = _gmlp_fwd(proj, cs["lng"], cs["lnb"], cs["wm"], cs["bst"], tag)
        yb, ob, s0 = _hgrn_fwd(proj, lb_all[l].reshape(1, B_WIDTH), cs["onorm"], tag)
        qt, kt, vt = _fox_prep(proj, cs["bf"], tag)
        oc, lse, yc = _fox_fwd(qt, kt, vt, proj, tag)
        saved.append(dict(x=xl, h=h, proj=proj, ya=ya, yb=yb, yc=yc, ob=ob, s0=s0, qt=qt, kt=kt, oc=oc, lse=lse))
        xl = _outproj(xl, ya, yb, yc, w_out_full[l], tag)

    dx, loss_part, d_final = _loss_head(xl, final_norm_g.reshape(1, D_MODEL), tgt)
    loss = lax.psum(loss_part[0, 0], ("x", "y", "c"))

    g_small = {}
    dw_in, dw_out, dlb_rows = [None] * DEPTH, [None] * DEPTH, [None] * DEPTH
    for l in reversed(range(DEPTH)):
        cs, sv = consts[l], saved[l]
        tag = f"l{l}"
        proj = sv["proj"]
        dy, dw_out[l] = _outproj_bwd(dx, sv["ya"], sv["yb"], sv["yc"], w_out_full[l], tag)
        da, dwm, dbst, dlng, dlnb = _gmlp_bwd(proj, dy, cs["lng"], cs["lnb"], cs["wm"], cs["wmt"], cs["bst"], tag)
        db, dlb_rows[l], donorm = _hgrn_bwd(proj, dy, sv["ob"], sv["s0"], lb_all[l].reshape(1, B_WIDTH), cs["onorm"], tag)
        do, delta, dzc = _fox_bwd_prep(proj, dy, sv["oc"], tag)
        dqt, dkt, dvc = _fox_bwd(sv["qt"], sv["kt"], proj, do, sv["lse"], delta, tag)
        dqc, dkc, dflc, dbf = _fox_bwd_post(dqt, dkt, proj, cs["bf"], tag)
        dproj = jnp.concatenate([da, db, dqc, dkc, dvc, dzc, dflc, jnp.zeros((T, 128), BF16)], axis=1)
        dw_in[l] = _dw_in(sv["h"], dproj, tag)
        dx, dng = _dx_in(sv["x"], norm_g[l].reshape(1, D_MODEL), dx, dproj, w_in_full[l], tag)
        g_small[l] = dict(norm_g=dng.reshape(D_MODEL), ln_g=dlng.reshape(4, 64), ln_b=dlnb.reshape(4, 64), w_s=dwm,
                          b_s=dbst[:, :A_GROUPS].T, onorm=donorm[0, :64], bf=dbf[0, :C_HEADS])
    grad_x = dx.reshape(x.shape)
    d_hgrn_lb = _lb_bwd(hgrn_lb, jnp.concatenate(dlb_rows, axis=0))

    stack = lambda key: jnp.stack([g_small[l][key] for l in range(DEPTH)])
    small_g = _pack_small([stack("norm_g"), stack("ln_g"), stack("ln_b"), stack("w_s"), stack("b_s"), d_hgrn_lb,
                           stack("onorm"), stack("bf"), d_final.reshape(D_MODEL)])

    dwi = jnp.stack(dw_in)
    gin = jnp.stack([dwi[:, :, k * shard_in:(k + 1) * shard_in] for k in range(N_CHIPS)])
    gout = jnp.stack(dw_out).reshape(DEPTH, N_CHIPS, shard_out, D_MODEL).transpose(1, 0, 2, 3)
    rin, rout, rsmall = _exchange_grads(gin, gout, small_g)
    pin, pout = _sum_chips(rin, "sum_chips_w_in"), _sum_chips(rout, "sum_chips_w_out")
    oin, oout = _swap_cores(pin, pout)
    g_w_in, d_w_in, nm_w_in, nv_w_in = _adamw_pair(w_in, m_w_in, v_w_in, pin, oin, "adamw_w_in")
    g_w_out, d_w_out, nm_w_out, nv_w_out = _adamw_pair(w_out, m_w_out, v_w_out, pout, oout, "adamw_w_out")

    small_w = [norm_g, gmlp_ln_g, gmlp_ln_b, gmlp_w_s, gmlp_b_s, hgrn_lb, hgrn_onorm_g, fox_b_f, final_norm_g]
    small_m = [m_norm_g, m_gmlp_ln_g, m_gmlp_ln_b, m_gmlp_w_s, m_gmlp_b_s, m_hgrn_lb, m_hgrn_onorm_g, m_fox_b_f, m_final_norm_g]
    small_v = [v_norm_g, v_gmlp_ln_g, v_gmlp_ln_b, v_gmlp_w_s, v_gmlp_b_s, v_hgrn_lb, v_hgrn_onorm_g, v_fox_b_f, v_final_norm_g]
    outs = _adamw_small(_pack_small(small_w), _pack_small(small_m), _pack_small(small_v), rsmall)
    sg, sd, sm, sv_ = [_unpack_small(o) for o in outs]

    def order(big_in, big_out, small):
        return [small[0], big_in, big_out] + small[1:]

    return (loss, grad_x, *order(g_w_in, g_w_out, sg), *order(d_w_in, d_w_out, sd), *order(nm_w_in, nm_w_out, sm),
            *order(nv_w_in, nv_w_out, sv_))
```

```python
import functools
import math

import jax
import jax.numpy as jnp
from jax import lax
from jax.experimental import pallas as pl
from jax.experimental.pallas import tpu as pltpu

F32 = jnp.float32
BF16 = jnp.bfloat16
SDS = jax.ShapeDtypeStruct
MESH_ID = pl.DeviceIdType.MESH

D_MODEL = 1024
DEPTH = 2
A_WIDTH = 256
A_GROUPS = 4
B_WIDTH = 256
C_WIDTH = 512
C_HEADS = 8
D_IN = 3848
D_IN_PAD = 4096
CHUNK = 128
SUB = 16
SUB_SHIFT = 4
NORM_EPS = 1e-6
F_FLOOR = 1e-30
COL_AU, COL_AV, COL_AZ = 0, 256, 512
COL_BQ, COL_BF, COL_BI, COL_BZ = 768, 1024, 1280, 1536
COL_CQ, COL_CK, COL_CV, COL_CZ, COL_CF = 1792, 2304, 2816, 3328, 3840
HEAD_LANES = 128
Q_SCALE = 0.125
ADAM_LR, ADAM_B1, ADAM_B2, ADAM_EPS, ADAM_WD, ADAM_STEP = 0.001, 0.9, 0.999, 1e-08, 0.01, 10
ADAM_C1 = 1.0 - ADAM_B1 ** ADAM_STEP
ADAM_C2 = 1.0 - ADAM_B2 ** ADAM_STEP
VMEM_LIMIT = 56 * 1024 * 1024
N_CHIPS = 4
N_DEV = 8

SMALL_PARAMS = (
    ("norm_g", (DEPTH, D_MODEL)), ("gmlp_ln_g", (DEPTH, 4, 64)), ("gmlp_ln_b", (DEPTH, 4, 64)),
    ("gmlp_w_s", (DEPTH, 4, 128, 128)), ("gmlp_b_s", (DEPTH, 4, 128)), ("hgrn_lb", (DEPTH, 256)),
    ("hgrn_onorm_g", (DEPTH, 64)), ("fox_b_f", (DEPTH, 8)), ("final_norm_g", (D_MODEL,)),
)


def _tile(n, pref):
    t = min(n, pref)
    assert n % t == 0, (n, pref)
    return t


def _params(*sem):
    return pltpu.CompilerParams(dimension_semantics=sem, vmem_limit_bytes=VMEM_LIMIT)


def _dot(a, b):
    return jnp.dot(a, b, preferred_element_type=F32)


def _dot_nt(a, b):
    return lax.dot_general(a, b, (((1,), (1,)), ((), ())), preferred_element_type=F32)


def _dot_tn(a, b):
    return lax.dot_general(a, b, (((0,), (0,)), ((), ())), preferred_element_type=F32)


def _split3(x):
    hi = x.astype(BF16)
    r = x - hi.astype(F32)
    mid = r.astype(BF16)
    lo = (r - mid.astype(F32)).astype(BF16)
    return hi, mid, lo


def _dot3_left(c, x):
    hi, mid, lo = _split3(x)
    return _dot(c, hi) + _dot(c, mid) + _dot(c, lo)


def _sigmoid(x):
    return jax.nn.sigmoid(x)


def _silu_and_grad(x):
    s = _sigmoid(x)
    return x * s, s * (1.0 + x * (1.0 - s))


_GELU_C = math.sqrt(2.0 / math.pi)


def _gelu_and_grad(x):
    inner = _GELU_C * (x + 0.044715 * x * x * x)
    t = jnp.tanh(inner)
    y = 0.5 * x * (1.0 + t)
    dy = 0.5 * (1.0 + t) + 0.5 * x * (1.0 - t * t) * _GELU_C * (1.0 + 3.0 * 0.044715 * x * x)
    return y, dy


def _lane(shape):
    return lax.broadcasted_iota(jnp.int32, shape, 1)


def _row(shape):
    return lax.broadcasted_iota(jnp.int32, shape, 0)


def _gsum64(x):
    lo = _lane(x.shape) < 64
    s0 = jnp.sum(jnp.where(lo, x, 0.0), axis=-1, keepdims=True)
    s1 = jnp.sum(jnp.where(lo, 0.0, x), axis=-1, keepdims=True)
    return jnp.where(lo, s0, s1)


def _colreduce(x, op):
    parts = [x[r:r + 8, :] for r in range(0, x.shape[0], 8)]
    while len(parts) > 1:
        pairs = [op(parts[k], parts[k + 1]) for k in range(0, len(parts) - 1, 2)]
        parts = pairs + ([parts[-1]] if len(parts) % 2 else [])
    red = jnp.max if op is jnp.maximum else jnp.sum
    return red(parts[0], axis=0, keepdims=True)


def _block_diag64(dtype=BF16):
    r, c = _row((128, 128)), _lane((128, 128))
    return jnp.where((r >> 6) == (c >> 6), 1.0, 0.0).astype(dtype)


def _inproj(x, g, w, tag):
    T, D = x.shape
    DP = w.shape[1]
    tm, tn = _tile(T, 512), _tile(DP, 1024)

    def body(x_ref, g_ref, w_ref, h_ref, p_ref):
        @pl.when(pl.program_id(1) == 0)
        def _():
            xv = x_ref[...]
            r = lax.rsqrt(jnp.mean(xv * xv, axis=-1, keepdims=True) + NORM_EPS)
            h_ref[...] = (xv * r * g_ref[...]).astype(BF16)

        p_ref[...] = _dot(h_ref[...], w_ref[...])

    return pl.pallas_call(
        body, name=f"inproj_{tag}", grid=(T // tm, DP // tn),
        in_specs=[pl.BlockSpec((tm, D), lambda i, j: (i, 0)), pl.BlockSpec((1, D), lambda i, j: (0, 0)),
                  pl.BlockSpec((D, tn), lambda i, j: (0, j))],
        out_specs=[pl.BlockSpec((tm, D), lambda i, j: (i, 0)), pl.BlockSpec((tm, tn), lambda i, j: (i, j))],
        out_shape=[SDS((T, D), BF16), SDS((T, DP), F32)],
        compiler_params=_params("parallel", "arbitrary"),
    )(x, g, w)


def _outproj(x, ya, yb, yc, wo, tag):
    T, D = x.shape
    tm = _tile(T, 512)

    def body(x_ref, ya_ref, yb_ref, yc_ref, wo_ref, o_ref):
        acc = x_ref[...] + _dot(ya_ref[...], wo_ref[0:A_WIDTH, :])
        acc = acc + _dot(yb_ref[...], wo_ref[A_WIDTH:A_WIDTH + B_WIDTH, :])
        o_ref[...] = acc + _dot(yc_ref[...], wo_ref[A_WIDTH + B_WIDTH:, :])

    row = lambda w: pl.BlockSpec((tm, w), lambda i: (i, 0))
    return pl.pallas_call(
        body, name=f"outproj_{tag}", grid=(T // tm,),
        in_specs=[row(D), row(A_WIDTH), row(B_WIDTH), row(C_WIDTH), pl.BlockSpec(wo.shape, lambda i: (0, 0))],
        out_specs=row(D), out_shape=SDS((T, D), F32), compiler_params=_params("parallel"),
    )(x, ya, yb, yc, wo)


def _outproj_bwd(dx, ya, yb, yc, wo, tag):
    T, D = dx.shape
    DM = wo.shape[0]
    tm = _tile(T, 512)

    def body(dx_ref, ya_ref, yb_ref, yc_ref, wo_ref, dy_ref, dwo_ref):
        @pl.when(pl.program_id(0) == 0)
        def _():
            dwo_ref[...] = jnp.zeros_like(dwo_ref)

        dxb = dx_ref[...].astype(BF16)
        dy_ref[...] = _dot_nt(dxb, wo_ref[...])
        dwo_ref[0:A_WIDTH, :] += _dot_tn(ya_ref[...], dxb)
        dwo_ref[A_WIDTH:A_WIDTH + B_WIDTH, :] += _dot_tn(yb_ref[...], dxb)
        dwo_ref[A_WIDTH + B_WIDTH:, :] += _dot_tn(yc_ref[...], dxb)

    row = lambda w: pl.BlockSpec((tm, w), lambda i: (i, 0))
    return pl.pallas_call(
        body, name=f"outproj_bwd_{tag}", grid=(T // tm,),
        in_specs=[row(D), row(A_WIDTH), row(B_WIDTH), row(C_WIDTH), pl.BlockSpec(wo.shape, lambda i: (0, 0))],
        out_specs=[row(DM), pl.BlockSpec((DM, D), lambda i: (0, 0))],
        out_shape=[SDS((T, DM), F32), SDS((DM, D), F32)], compiler_params=_params("arbitrary"),
    )(dx, ya, yb, yc, wo)


def _dw_in(h, dproj, tag):
    T, D = h.shape
    DP = dproj.shape[1]
    tm, tn = _tile(T, 512), _tile(DP, 1024)

    def body(h_ref, dp_ref, dw_ref):
        @pl.when(pl.program_id(1) == 0)
        def _():
            dw_ref[...] = jnp.zeros_like(dw_ref)

        dw_ref[...] += _dot_tn(h_ref[...], dp_ref[...])

    return pl.pallas_call(
        body, name=f"dw_in_{tag}", grid=(DP // tn, T // tm),
        in_specs=[pl.BlockSpec((tm, D), lambda j, i: (i, 0)), pl.BlockSpec((tm, tn), lambda j, i: (i, j))],
        out_specs=pl.BlockSpec((D, tn), lambda j, i: (0, j)), out_shape=SDS((D, DP), F32),
        compiler_params=_params("parallel", "arbitrary"),
    )(h, dproj)


def _dx_in(x, g, dres, dproj, w, tag):
    T, D = x.shape
    DP = w.shape[1]
    tm, tk = _tile(T, 512), _tile(DP, 1024)
    nk = DP // tk

    def body(x_ref, g_ref, dres_ref, dp_ref, w_ref, dx_ref, dg_ref, acc_ref):
        i, k = pl.program_id(0), pl.program_id(1)

        @pl.when((i == 0) & (k == 0))
        def _():
            dg_ref[...] = jnp.zeros_like(dg_ref)

        @pl.when(k == 0)
        def _():
            acc_ref[...] = jnp.zeros_like(acc_ref)

        acc_ref[...] += _dot_nt(dp_ref[...], w_ref[...])

        @pl.when(k == nk - 1)
        def _():
            xv = x_ref[...]
            r = lax.rsqrt(jnp.mean(xv * xv, axis=-1, keepdims=True) + NORM_EPS)
            xh = xv * r
            dh = acc_ref[...]
            dg_ref[...] += jnp.sum(dh * xh, axis=0, keepdims=True)
            dxh = dh * g_ref[...]
            dx_ref[...] = dres_ref[...] + r * (dxh - xh * jnp.mean(dxh * xh, axis=-1, keepdims=True))

    return pl.pallas_call(
        body, name=f"dx_in_{tag}", grid=(T // tm, nk),
        in_specs=[pl.BlockSpec((tm, D), lambda i, k: (i, 0)), pl.BlockSpec((1, D), lambda i, k: (0, 0)),
                  pl.BlockSpec((tm, D), lambda i, k: (i, 0)), pl.BlockSpec((tm, tk), lambda i, k: (i, k)),
                  pl.BlockSpec((D, tk), lambda i, k: (0, k))],
        out_specs=[pl.BlockSpec((tm, D), lambda i, k: (i, 0)), pl.BlockSpec((1, D), lambda i, k: (0, 0))],
        out_shape=[SDS((T, D), F32), SDS((1, D), F32)],
        scratch_shapes=[pltpu.VMEM((tm, D), F32)], compiler_params=_params("arbitrary", "arbitrary"),
    )(x, g, dres, dproj, w)


def _loss_head(x, g, tgt):
    T, D = x.shape
    tm = _tile(T, 512)

    def body(x_ref, g_ref, t_ref, dx_ref, loss_ref, dg_ref):
        @pl.when(pl.program_id(0) == 0)
        def _():
            loss_ref[...] = jnp.zeros_like(loss_ref)
            dg_ref[...] = jnp.zeros_like(dg_ref)

        xv = x_ref[...]
        r = lax.rsqrt(jnp.mean(xv * xv, axis=-1, keepdims=True) + NORM_EPS)
        xh = xv * r
        gv = g_ref[...]
        err = xh * gv - t_ref[...]
        tok = jnp.mean(err * err, axis=-1, keepdims=True)
        loss_ref[...] += 0.5 * jnp.sum(tok, axis=0, keepdims=True)
        dy = err * (1.0 / D)
        dg_ref[...] += jnp.sum(dy * xh, axis=0, keepdims=True)
        dxh = dy * gv
        dx_ref[...] = r * (dxh - xh * jnp.mean(dxh * xh, axis=-1, keepdims=True))

    row = pl.BlockSpec((tm, D), lambda i: (i, 0))
    return pl.pallas_call(
        body, name="loss_head", grid=(T // tm,),
        in_specs=[row, pl.BlockSpec((1, D), lambda i: (0, 0)), row],
        out_specs=[row, pl.BlockSpec((1, 128), lambda i: (0, 0)), pl.BlockSpec((1, D), lambda i: (0, 0))],
        out_shape=[SDS((T, D), F32), SDS((1, 128), F32), SDS((1, D), F32)], compiler_params=_params("arbitrary"),
    )(x, g, tgt)


def _gmlp_core(u, v, lng, lnb, wm_ref, bst_ref, pair):
    ug, dug = _gelu_and_grad(u)
    vg, dvg = _gelu_and_grad(v)
    mu = _gsum64(vg) * (1.0 / 64)
    d = vg - mu
    var = _gsum64(d * d) * (1.0 / 64)
    rstd = lax.rsqrt(var + NORM_EPS)
    xh = d * rstd
    vn = xh * lng + lnb
    vnb = vn.astype(BF16)
    lo = _lane(u.shape) < 64
    g0, g1 = 2 * pair, 2 * pair + 1
    mixed = jnp.where(lo, _dot(wm_ref[g0], vnb) + bst_ref[:, g0:g0 + 1], _dot(wm_ref[g1], vnb) + bst_ref[:, g1:g1 + 1])
    return ug, dug, dvg, rstd, xh, vnb, mixed, lo


def _gmlp_fwd(proj, lng, lnb, wm, bst, tag):
    T = proj.shape[0]

    def body(u_ref, v_ref, z_ref, lng_ref, lnb_ref, wm_ref, bst_ref, y_ref):
        for pair in range(2):
            sl = slice(128 * pair, 128 * pair + 128)
            ug, _, _, _, _, _, mixed, _ = _gmlp_core(u_ref[:, sl], v_ref[:, sl], lng_ref[:, sl], lnb_ref[:, sl],
                                                     wm_ref, bst_ref, pair)
            sz, _ = _silu_and_grad(z_ref[:, sl])
            y_ref[:, sl] = (ug * mixed * sz).astype(BF16)

    col = lambda c: pl.BlockSpec((CHUNK, A_WIDTH), lambda i, c=c: (i, c // A_WIDTH))
    full = lambda a: pl.BlockSpec(a.shape, lambda i, n=a.ndim: (0,) * n)
    return pl.pallas_call(
        body, name=f"gmlp_fwd_{tag}", grid=(T // CHUNK,),
        in_specs=[col(COL_AU), col(COL_AV), col(COL_AZ), full(lng), full(lnb), full(wm), full(bst)],
        out_specs=pl.BlockSpec((CHUNK, A_WIDTH), lambda i: (i, 0)), out_shape=SDS((T, A_WIDTH), BF16),
        compiler_params=_params("parallel"),
    )(proj, proj, proj, lng, lnb, wm, bst)


def _gmlp_bwd(proj, dy, lng, lnb, wm, wmt, bst, tag):
    T = proj.shape[0]
    n = T // CHUNK

    def body(u_ref, v_ref, z_ref, dy_ref, lng_ref, lnb_ref, wm_ref, wmt_ref, bst_ref,
             da_ref, dwm_ref, dbst_ref, dlng_ref, dlnb_ref):
        @pl.when(pl.program_id(0) == 0)
        def _():
            dwm_ref[...] = jnp.zeros_like(dwm_ref)
            dbst_ref[...] = jnp.zeros_like(dbst_ref)
            dlng_ref[...] = jnp.zeros_like(dlng_ref)
            dlnb_ref[...] = jnp.zeros_like(dlnb_ref)

        lane = _lane((CHUNK, 128))
        dbst = dbst_ref[...]
        for pair in range(2):
            sl = slice(128 * pair, 128 * pair + 128)
            lng_p = lng_ref[:, sl]
            ug, dug, dvg, rstd, xh, vnb, mixed, lo = _gmlp_core(u_ref[:, sl], v_ref[:, sl], lng_p, lnb_ref[:, sl],
                                                                wm_ref, bst_ref, pair)
            sz, dsz = _silu_and_grad(z_ref[:, sl])
            dyv = dy_ref[:, sl]
            out = ug * mixed
            dz = dyv * out * dsz
            dout = dyv * sz
            du = dout * mixed * dug
            dmix = dout * ug
            g0, g1 = 2 * pair, 2 * pair + 1
            dm0 = jnp.where(lo, dmix, 0.0)
            dm1 = jnp.where(lo, 0.0, dmix)
            dbst = dbst + jnp.where(lane == g0, jnp.sum(dm0, axis=-1, keepdims=True), 0.0)
            dbst = dbst + jnp.where(lane == g1, jnp.sum(dm1, axis=-1, keepdims=True), 0.0)
            dwm_ref[g0] += _dot_nt(dm0.astype(BF16), vnb)
            dwm_ref[g1] += _dot_nt(dm1.astype(BF16), vnb)
            dmb = dmix.astype(BF16)
            dvn = jnp.where(lo, _dot(wmt_ref[g0], dmb), _dot(wmt_ref[g1], dmb))
            dlng_ref[:, sl] += jnp.sum(dvn * xh, axis=0, keepdims=True)
            dlnb_ref[:, sl] += jnp.sum(dvn, axis=0, keepdims=True)
            dxh = dvn * lng_p
            m1 = _gsum64(dxh) * (1.0 / 64)
            m2 = _gsum64(dxh * xh) * (1.0 / 64)
            dv = rstd * (dxh - m1 - xh * m2) * dvg
            da_ref[:, COL_AU + 128 * pair:COL_AU + 128 * pair + 128] = du.astype(BF16)
            da_ref[:, COL_AV + 128 * pair:COL_AV + 128 * pair + 128] = dv.astype(BF16)
            da_ref[:, COL_AZ + 128 * pair:COL_AZ + 128 * pair + 128] = dz.astype(BF16)
        dbst_ref[...] = dbst

        @pl.when(pl.program_id(0) == n - 1)
        def _():
            causal = _lane((CHUNK, CHUNK)) <= _row((CHUNK, CHUNK))
            for g in range(A_GROUPS):
                dwm_ref[g] = jnp.where(causal, dwm_ref[g], 0.0)

    col = lambda c: pl.BlockSpec((CHUNK, A_WIDTH), lambda i, c=c: (i, c // A_WIDTH))
    full = lambda a: pl.BlockSpec(a.shape, lambda i, n=a.ndim: (0,) * n)
    acc = lambda s: pl.BlockSpec(s, lambda i, n=len(s): (0,) * n)
    return pl.pallas_call(
        body, name=f"gmlp_bwd_{tag}", grid=(n,),
        in_specs=[col(COL_AU), col(COL_AV), col(COL_AZ), pl.BlockSpec((CHUNK, A_WIDTH), lambda i: (i, 0)),
                  full(lng), full(lnb), full(wm), full(wmt), full(bst)],
        out_specs=[pl.BlockSpec((CHUNK, 3 * A_WIDTH), lambda i: (i, 0)), acc((A_GROUPS, CHUNK, CHUNK)),
                   acc((CHUNK, 128)), acc((1, A_WIDTH)), acc((1, A_WIDTH))],
        out_shape=[SDS((T, 3 * A_WIDTH), BF16), SDS((A_GROUPS, CHUNK, CHUNK), F32), SDS((CHUNK, 128), F32),
                   SDS((1, A_WIDTH), F32), SDS((1, A_WIDTH), F32)],
        compiler_params=_params("arbitrary"),
    )(proj, proj, proj, dy, lng, lnb, wm, wmt, bst)


def _hgrn_consts():
    r, c = _row((CHUNK, CHUNK)), _lane((CHUNK, CHUNK))
    same = (r >> SUB_SHIFT) == (c >> SUB_SHIFT)
    lsub = jnp.where(same & (c <= r), 1.0, 0.0).astype(BF16)
    usub = jnp.where(same & (c >= r), 1.0, 0.0).astype(BF16)
    bsub = jnp.where(same, 1.0, 0.0).astype(BF16)
    return lsub, usub, bsub


def _hgrn_gates(qv, zf, lbp):
    sq, dsq = _silu_and_grad(qv)
    qt = sq * Q_SCALE
    sg = _sigmoid(zf)
    sgn = _sigmoid(-zf)
    f = lbp + (1.0 - lbp) * sg
    g = jnp.log(jnp.maximum(f, F_FLOOR))
    kf = (1.0 - lbp) * sgn
    return qt, dsq, sg, sgn, f, g, kf


def _hgrn_intra_fwd(qt, kf, b, v, mbd):
    rid = _row((SUB, 128))
    parts = []
    for s in range(SUB):
        e = jnp.exp(jnp.minimum(b - b[s:s + 1, :], 0.0))
        parts.append(jnp.where(rid >= s, qt * kf[s:s + 1, :] * e, 0.0))
    a = _dot(jnp.concatenate(parts, axis=0).astype(BF16), mbd)
    o = jnp.zeros((SUB, 128), F32)
    for s in range(SUB):
        o = o + a[SUB * s:SUB * s + SUB, :] * v[s:s + 1, :]
    return o


def _hgrn_intra_bwd(qt, kf, b, v, do, mbd, rsum):
    rid = _row((SUB, 128))
    ps, das, kes, es = [], [], [], []
    for s in range(SUB):
        e = jnp.where(rid >= s, jnp.exp(jnp.minimum(b - b[s:s + 1, :], 0.0)), 0.0)
        ke = kf[s:s + 1, :] * e
        es.append(e)
        kes.append(ke)
        ps.append(qt * ke)
        das.append(do * v[s:s + 1, :])
    a = _dot(jnp.concatenate(ps, axis=0).astype(BF16), mbd)
    da = _dot(jnp.concatenate(das, axis=0).astype(BF16), mbd)
    dqt = jnp.zeros((SUB, 128), F32)
    xs, ys = [], []
    for s in range(SUB):
        da_s = da[SUB * s:SUB * s + SUB, :]
        dqt = dqt + da_s * kes[s]
        xs.append(a[SUB * s:SUB * s + SUB, :] * do)
        ys.append(da_s * qt * es[s])
    xh = jnp.concatenate(xs, axis=0)
    yh = jnp.concatenate(ys, axis=0)
    xhi = xh.astype(BF16)
    yhi = yh.astype(BF16)
    dv = _dot(rsum, xhi) + _dot(rsum, (xh - xhi.astype(F32)).astype(BF16))
    dkf = _dot(rsum, yhi) + _dot(rsum, (yh - yhi.astype(F32)).astype(BF16))
    return dqt, dkf, dv


def _hgrn_norm_gate(o, z, onorm):
    ms = _gsum64(o * o) * (1.0 / 64)
    r = lax.rsqrt(ms + NORM_EPS)
    xh = o * r
    sz, dsz = _silu_and_grad(z)
    return xh, r, sz, dsz, xh * onorm


def _hgrn_fwd(proj, lb, onorm, tag):
    T = proj.shape[0]
    n = T // CHUNK
    nsub = CHUNK // SUB

    def body(q_ref, f_ref, i_ref, z_ref, lb_ref, on_ref, y_ref, o_ref, s0_ref, st_ref):
        @pl.when(pl.program_id(0) == 0)
        def _():
            st_ref[...] = jnp.zeros_like(st_ref)

        lsub, _, bsub = _hgrn_consts()
        mbd = _block_diag64()
        bdmask = mbd > 0
        rid = _row((CHUNK, 128))
        for pair in range(2):
            sl = slice(128 * pair, 128 * pair + 128)
            qt, _, _, _, _, g, kf = _hgrn_gates(q_ref[:, sl], f_ref[:, sl], lb_ref[:, sl])
            v = i_ref[:, sl]
            b = _dot3_left(lsub, g)
            bl = _dot3_left(bsub, g)
            qh = (qt * jnp.exp(b)).astype(BF16)
            kh = kf * jnp.exp(bl - b)
            dec = jnp.exp(bl)
            vtb = v.T.astype(BF16)
            st = st_ref[pair]
            s0_ref[0, pair] = st
            outs = []
            for sub in range(nsub):
                rs = slice(SUB * sub, SUB * sub + SUB)
                o_inter = _dot_nt(qh[rs], st.astype(BF16))
                outs.append(o_inter + _hgrn_intra_fwd(qt[rs], kf[rs], b[rs], v[rs], mbd))
                khm = jnp.where((rid >> SUB_SHIFT) == sub, kh, 0.0).astype(BF16)
                st = jnp.where(bdmask, st * dec[SUB * sub:SUB * sub + 1, :] + _dot(vtb, khm), 0.0)
            st_ref[pair] = st
            o = jnp.concatenate(outs, axis=0)
            o_ref[:, sl] = o
            _, _, sz, _, on = _hgrn_norm_gate(o, z_ref[:, sl], on_ref[:, sl])
            y_ref[:, sl] = (on * sz).astype(BF16)

    col = lambda c: pl.BlockSpec((CHUNK, B_WIDTH), lambda i, c=c: (i, c // B_WIDTH))
    full = lambda a: pl.BlockSpec(a.shape, lambda i, n=a.ndim: (0,) * n)
    return pl.pallas_call(
        body, name=f"hgrn_fwd_{tag}", grid=(n,),
        in_specs=[col(COL_BQ), col(COL_BF), col(COL_BI), col(COL_BZ), full(lb), full(onorm)],
        out_specs=[pl.BlockSpec((CHUNK, B_WIDTH), lambda i: (i, 0)), pl.BlockSpec((CHUNK, B_WIDTH), lambda i: (i, 0)),
                   pl.BlockSpec((1, 2, 128, 128), lambda i: (i, 0, 0, 0))],
        out_shape=[SDS((T, B_WIDTH), BF16), SDS((T, B_WIDTH), F32), SDS((n, 2, 128, 128), F32)],
        scratch_shapes=[pltpu.VMEM((2, 128, 128), F32)], compiler_params=_params("arbitrary"),
    )(proj, proj, proj, proj, lb, onorm)


def _hgrn_bwd(proj, dy, o_saved, s0, lb, onorm, tag):
    T = proj.shape[0]
    n = T // CHUNK
    nsub = CHUNK // SUB

    def body(q_ref, f_ref, i_ref, z_ref, dy_ref, o_ref, s0_ref, lb_ref, on_ref,
             db_ref, dlb_ref, don_ref, dst_ref, sts_ref):
        @pl.when(pl.program_id(0) == 0)
        def _():
            dst_ref[...] = jnp.zeros_like(dst_ref)
            dlb_ref[...] = jnp.zeros_like(dlb_ref)
            don_ref[...] = jnp.zeros_like(don_ref)

        lsub, usub, bsub = _hgrn_consts()
        mbd = _block_diag64()
        bdmask = mbd > 0
        rid = _row((CHUNK, 128))
        rsum = jnp.where((_lane((SUB, SUB * SUB)) >> SUB_SHIFT) == _row((SUB, SUB * SUB)), 1.0, 0.0).astype(BF16)
        for pair in range(2):
            sl = slice(128 * pair, 128 * pair + 128)
            lbp = lb_ref[:, sl]
            qv, zf = q_ref[:, sl], f_ref[:, sl]
            qt, dsq, sg, sgn, f, g, kf = _hgrn_gates(qv, zf, lbp)
            v = i_ref[:, sl]
            b = _dot3_left(lsub, g)
            bl = _dot3_left(bsub, g)
            eb = jnp.exp(b)
            ekb = jnp.exp(bl - b)
            qh = qt * eb
            kh = kf * ekb
            dec = jnp.exp(bl)
            vtb = v.T.astype(BF16)
            onp = on_ref[:, sl]
            ov = o_ref[:, sl]
            xh, r, sz, dsz, on = _hgrn_norm_gate(ov, z_ref[:, sl], onp)
            dyv = dy_ref[:, sl]
            dz = dyv * on * dsz
            don = dyv * sz
            cn = jnp.sum(don * xh, axis=0, keepdims=True)
            don_ref[...] += cn + pltpu.roll(cn, 64, axis=1)
            dxo = don * onp
            do = r * (dxo - xh * (_gsum64(dxo * xh) * (1.0 / 64)))
            dotb = do.T.astype(BF16)
            st = s0_ref[0, pair]
            for sub in range(nsub):
                sts_ref[sub] = st
                khm = jnp.where((rid >> SUB_SHIFT) == sub, kh, 0.0).astype(BF16)
                st = jnp.where(bdmask, st * dec[SUB * sub:SUB * sub + 1, :] + _dot(vtb, khm), 0.0)
            gst = dst_ref[pair]
            dqt_p, dkf_p, dv_p, dbl_p = [None] * nsub, [None] * nsub, [None] * nsub, [None] * nsub
            for sub in reversed(range(nsub)):
                rs = slice(SUB * sub, SUB * sub + SUB)
                st_in = sts_ref[sub]
                gb = gst.astype(BF16)
                dob = do[rs].astype(BF16)
                dqh = _dot(dob, st_in.astype(BF16))
                dkh = _dot(v[rs].astype(BF16), gb)
                dv_inter = _dot_nt(kh[rs].astype(BF16), gb)
                ddec = jnp.sum(gst * st_in, axis=0, keepdims=True)
                dec_row = dec[SUB * sub:SUB * sub + 1, :]
                qhm = jnp.where((rid >> SUB_SHIFT) == sub, qh, 0.0).astype(BF16)
                gst = jnp.where(bdmask, gst * dec_row + _dot(dotb, qhm), 0.0)
                dqt_i, dkf_i, dv_i = _hgrn_intra_bwd(qt[rs], kf[rs], b[rs], v[rs], do[rs], mbd, rsum)
                dkf_inter = dkh * ekb[rs]
                dqt_p[sub] = dqh * eb[rs] + dqt_i
                dkf_p[sub] = dkf_inter + dkf_i
                dv_p[sub] = dv_inter + dv_i
                row = jnp.sum(kf[rs] * dkf_inter, axis=0, keepdims=True) + ddec * dec_row
                dbl_p[sub] = jnp.broadcast_to(row, (SUB, 128))
            dst_ref[pair] = gst
            dqt = jnp.concatenate(dqt_p, axis=0)
            dkf = jnp.concatenate(dkf_p, axis=0)
            dv = jnp.concatenate(dv_p, axis=0)
            dg = _dot3_left(usub, qt * dqt - kf * dkf) + jnp.concatenate(dbl_p, axis=0)
            df = jnp.where(f > F_FLOOR, dg / f, 0.0)
            dlb_ref[:, sl] += jnp.sum(df * (1.0 - sg) - dkf * sgn, axis=0, keepdims=True)
            dfl = (1.0 - lbp) * sg * sgn * (df - dkf)
            dq = dqt * Q_SCALE * dsq
            db_ref[:, 0 * B_WIDTH + 128 * pair:0 * B_WIDTH + 128 * pair + 128] = dq.astype(BF16)
            db_ref[:, 1 * B_WIDTH + 128 * pair:1 * B_WIDTH + 128 * pair + 128] = dfl.astype(BF16)
            db_ref[:, 2 * B_WIDTH + 128 * pair:2 * B_WIDTH + 128 * pair + 128] = dv.astype(BF16)
            db_ref[:, 3 * B_WIDTH + 128 * pair:3 * B_WIDTH + 128 * pair + 128] = dz.astype(BF16)

    rev = lambda c: pl.BlockSpec((CHUNK, B_WIDTH), lambda i, c=c: (n - 1 - i, c // B_WIDTH))
    full = lambda a: pl.BlockSpec(a.shape, lambda i, n_=a.ndim: (0,) * n_)
    acc = lambda s: pl.BlockSpec(s, lambda i, n_=len(s): (0,) * n_)
    return pl.pallas_call(
        body, name=f"hgrn_bwd_{tag}", grid=(n,),
        in_specs=[rev(COL_BQ), rev(COL_BF), rev(COL_BI), rev(COL_BZ),
                  pl.BlockSpec((CHUNK, B_WIDTH), lambda i: (n - 1 - i, 1)),
                  pl.BlockSpec((CHUNK, B_WIDTH), lambda i: (n - 1 - i, 0)),
                  pl.BlockSpec((1, 2, 128, 128), lambda i: (n - 1 - i, 0, 0, 0)), full(lb), full(onorm)],
        out_specs=[pl.BlockSpec((CHUNK, 4 * B_WIDTH), lambda i: (n - 1 - i, 0)), acc((1, B_WIDTH)), acc((1, 128))],
        out_shape=[SDS((T, 4 * B_WIDTH), BF16), SDS((1, B_WIDTH), F32), SDS((1, 128), F32)],
        scratch_shapes=[pltpu.VMEM((2, 128, 128), F32), pltpu.VMEM((nsub, 128, 128), F32)],
        compiler_params=_params("arbitrary"),
    )(proj, proj, proj, proj, dy, o_saved, s0, lb, onorm)


def _lb_fwd(hgrn_lb):
    assert hgrn_lb.shape[0] == 2

    def body(x_ref, o_ref):
        x0, x1 = x_ref[0:1, :], x_ref[1:2, :]
        m = jnp.maximum(x0, x1)
        e0, e1 = jnp.exp(x0 - m), jnp.exp(x1 - m)
        p0, p1 = e0 / (e0 + e1), e1 / (e0 + e1)
        o_ref[0:1, :] = jnp.clip(p0 - p0, 0.0, 1.0 - 1e-6)
        o_ref[1:2, :] = jnp.clip((p0 + p1) - p0, 0.0, 1.0 - 1e-6)

    return pl.pallas_call(body, name="lb_fwd", out_shape=SDS(hgrn_lb.shape, F32))(hgrn_lb)


def _lb_bwd(hgrn_lb, dlb):
    def body(x_ref, d_ref, o_ref):
        x0, x1 = x_ref[0:1, :], x_ref[1:2, :]
        m = jnp.maximum(x0, x1)
        e0, e1 = jnp.exp(x0 - m), jnp.exp(x1 - m)
        p0, p1 = e0 / (e0 + e1), e1 / (e0 + e1)
        val = (p0 + p1) - p0
        dp1 = jnp.where((val > 0.0) & (val < 1.0 - 1e-6), d_ref[1:2, :], 0.0)
        inner = p1 * dp1
        o_ref[0:1, :] = p0 * (0.0 - inner)
        o_ref[1:2, :] = p1 * (dp1 - inner)

    return pl.pallas_call(body, name="lb_bwd", out_shape=SDS(hgrn_lb.shape, F32))(hgrn_lb, dlb)


def _fox_prep(proj, bf, tag):
    T = proj.shape[0]
    n = T // CHUNK

    def body(q0_ref, q1_ref, k0_ref, k1_ref, v0_ref, v1_ref, fl_ref, bf_ref, qo_ref, ko_ref, vt_ref, carry_ref):
        for p, v_ref in enumerate((v0_ref, v0_ref, v1_ref, v1_ref)):
            vt_ref[p, 0] = v_ref[:, 128 * (p % 2):128 * (p % 2) + 128].T.astype(BF16)

        @pl.when(pl.program_id(0) == 0)
        def _():
            carry_ref[...] = jnp.zeros_like(carry_ref)

        ltri = jnp.where(_lane((CHUNK, CHUNK)) <= _row((CHUNK, CHUNK)), 1.0, 0.0).astype(BF16)
        lf = jax.nn.log_sigmoid(fl_ref[...] + bf_ref[...])
        c = _dot3_left(ltri, lf) + carry_ref[...]
        carry_ref[...] = c[CHUNK - 1:CHUNK, :]
        lane = _lane((CHUNK, 128))
        feat = lane < 64
        ones_q = (lane >= 67) & (lane <= 69)
        ones_k = (lane >= 64) & (lane <= 66)
        qrefs, krefs = (q0_ref, q1_ref), (k0_ref, k1_ref)
        for h in range(C_HEADS):
            blk = slice(128 * ((h // 2) % 2), 128 * ((h // 2) % 2) + 128)
            qp, kp = qrefs[h // 4][:, blk], krefs[h // 4][:, blk]
            if h % 2:
                qp, kp = pltpu.roll(qp, 64, axis=1), pltpu.roll(kp, 64, axis=1)
            ch = jnp.broadcast_to(c[:, h:h + 1], (CHUNK, 128))
            hi = ch.astype(BF16).astype(F32)
            r1 = ch - hi
            mid = r1.astype(BF16).astype(F32)
            lo = r1 - mid
            aq = jnp.where(lane == 64, hi, jnp.where(lane == 65, mid, jnp.where(lane == 66, lo,
                           jnp.where(ones_q, 1.0, 0.0))))
            ak = jnp.where(lane == 67, -hi, jnp.where(lane == 68, -mid, jnp.where(lane == 69, -lo,
                           jnp.where(ones_k, 1.0, 0.0))))
            qo_ref[:, 128 * h:128 * h + 128] = jnp.where(feat, qp * Q_SCALE, aq).astype(BF16)
            ko_ref[:, 128 * h:128 * h + 128] = jnp.where(feat, kp, ak).astype(BF16)

    w = 256
    col = lambda c: pl.BlockSpec((CHUNK, w), lambda i, c=c: (i, c // w))
    return pl.pallas_call(
        body, name=f"fox_prep_{tag}", grid=(n,),
        in_specs=[col(COL_CQ), col(COL_CQ + w), col(COL_CK), col(COL_CK + w), col(COL_CV), col(COL_CV + w),
                  pl.BlockSpec((CHUNK, 128), lambda i: (i, COL_CF // 128)), pl.BlockSpec((1, 128), lambda i: (0, 0))],
        out_specs=[pl.BlockSpec((CHUNK, C_HEADS * 128), lambda i: (i, 0))] * 2
        + [pl.BlockSpec((C_HEADS // 2, 1, 128, CHUNK), lambda i: (0, i, 0, 0))],
        out_shape=[SDS((T, C_HEADS * 128), BF16)] * 2 + [SDS((C_HEADS // 2, n, 128, CHUNK), BF16)],
        scratch_shapes=[pltpu.VMEM((1, 128), F32)], compiler_params=_params("arbitrary"),
    )(proj, proj, proj, proj, proj, proj, proj, bf)


FOX_TILE = 256


def _ride_refs(ride, rest, n_out, n_scratch):
    n = ride.n if ride else 0
    srcs, rest = rest[:n], rest[n:]
    outs, rest = rest[:n_out], rest[n_out:]
    dsts, rest = rest[:n], rest[n:]
    return srcs, outs, dsts, rest[:n_scratch], rest[n_scratch:]


def _fox_fwd(qt, kt, vt, proj, tag, ride=None):
    T = proj.shape[0]
    tq = _tile(T, FOX_TILE)
    nq, nsub = T // tq, tq // CHUNK
    npair = C_HEADS // 2

    def body(q_ref, k_ref, vt_ref, z_ref, *rest):
        ride_srcs, (o_ref, lse_ref, y_ref), ride_dsts, (acc_ref,), ride_sems = _ride_refs(ride, rest, 3, 1)
        i = pl.program_id(1)
        if ride:
            @pl.when((pl.program_id(0) == 0) & (i == 0))
            def _():
                ride.start(ride_srcs, ride_dsts, ride_sems)

        qs = (q_ref[:, 0:128], q_ref[:, 128:256])
        acc_ref[...] = jnp.zeros_like(acc_ref)

        def scores(j):
            kb = k_ref[pl.ds(pl.multiple_of(j * tq, tq), tq), :]
            return tuple(_dot_nt(kb[:, 128 * h:128 * h + 128], qs[h]) for h in range(2))

        def block(j, carry, diagonal):
            sts = carry[4:6]
            nxt = () if diagonal else scores(j + 1)
            new = []
            for h in range(2):
                m, l = carry[2 * h], carry[2 * h + 1]
                st = sts[h]
                if diagonal:
                    st = jnp.where(_row((tq, tq)) <= _lane((tq, tq)), st, -jnp.inf)
                m_new = jnp.maximum(m, _colreduce(st, jnp.maximum))
                pt = jnp.exp(st - m_new)
                alpha = jnp.exp(m - m_new)
                ptb = pt.astype(BF16)
                rows = slice(64 * h, 64 * h + 64)
                pv = _dot(vt_ref[0, nsub * j, rows, :], ptb[0:CHUNK, :])
                for c in range(1, nsub):
                    pv = pv + _dot(vt_ref[0, nsub * j + c, rows, :], ptb[CHUNK * c:CHUNK * c + CHUNK, :])
                acc_ref[rows, :] = alpha * acc_ref[rows, :] + pv
                new += [m_new, alpha * l + _colreduce(pt, jnp.add)]
            return tuple(new) + nxt

        init = (jnp.full((1, tq), -jnp.inf, F32), jnp.zeros((1, tq), F32)) * 2 + scores(0)
        carry = lax.fori_loop(0, i, lambda j, c: block(j, c, False), init)
        m0, l0, m1, l1 = block(i, carry, True)
        inv = jnp.where(_row((128, tq)) < 64, 1.0 / l0, 1.0 / l1)
        o = (acc_ref[...] * inv).T
        o_ref[...] = o
        r8 = _row((8, tq))
        lse_ref[0, 0] = jnp.where(r8 == 0, m0 + jnp.log(l0), jnp.where(r8 == 1, m1 + jnp.log(l1), 0.0))
        sz, _ = _silu_and_grad(z_ref[...])
        y_ref[...] = (o * sz).astype(BF16)
        if ride:
            @pl.when((pl.program_id(0) == npair - 1) & (i == nq - 1))
            def _():
                ride.wait(ride_srcs, ride_dsts, ride_sems)

    blk = pl.BlockSpec((tq, 128), lambda p, i: (i, p))
    extra = ride or _ChipExchange("gather", ())
    return pl.pallas_call(
        body, name=f"fox_fwd_{tag}", grid=(npair, nq),
        in_specs=[pl.BlockSpec((tq, 256), lambda p, i: (i, p)), pl.BlockSpec((T, 256), lambda p, i: (0, p)),
                  pl.BlockSpec((1, T // CHUNK, 128, CHUNK), lambda p, i: (p, 0, 0, 0)),
                  pl.BlockSpec((tq, 128), lambda p, i: (i, COL_CZ // 128 + p))] + extra.in_specs,
        out_specs=[blk, pl.BlockSpec((1, 1, 8, tq), lambda p, i: (p, i, 0, 0)), blk] + extra.out_specs,
        out_shape=[SDS((T, C_WIDTH), F32), SDS((npair, nq, 8, tq), F32), SDS((T, C_WIDTH), BF16)] + extra.out_shape,
        scratch_shapes=[pltpu.VMEM((128, tq), F32)] + (extra.scratch if ride else []),
        compiler_params=pltpu.CompilerParams(dimension_semantics=("arbitrary", "arbitrary"), vmem_limit_bytes=VMEM_LIMIT,
                                             has_side_effects=bool(ride)),
    )(qt, kt, vt, proj, *extra.sources)


def _fox_bwd_prep(proj, dy, o, tag):
    T = proj.shape[0]
    tq = _tile(T, FOX_TILE)

    def body(z_ref, dy_ref, o_ref, do_ref, dl_ref, dz_ref):
        sz, dsz = _silu_and_grad(z_ref[...])
        dyv, ov = dy_ref[...], o_ref[...]
        do = dyv * sz
        do_ref[...] = do.astype(BF16)
        dz_ref[...] = (dyv * ov * dsz).astype(BF16)
        sel = jnp.where((_lane((16, 128)) >> 6) == _row((16, 128)), 1.0, 0.0).astype(BF16)
        hi, mid, lo = _split3(do * ov)
        dl_ref[0, 0] = (_dot_nt(sel, hi) + _dot_nt(sel, mid) + _dot_nt(sel, lo))[0:8, :]

    blk = pl.BlockSpec((tq, 128), lambda i, p: (i, p))
    return pl.pallas_call(
        body, name=f"fox_bwd_prep_{tag}", grid=(T // tq, C_WIDTH // 128),
        in_specs=[pl.BlockSpec((tq, 128), lambda i, p: (i, COL_CZ // 128 + p)),
                  pl.BlockSpec((tq, 128), lambda i, p: (i, (A_WIDTH + B_WIDTH) // 128 + p)), blk],
        out_specs=[blk, pl.BlockSpec((1, 1, 8, tq), lambda i, p: (p, i, 0, 0)), blk],
        out_shape=[SDS((T, C_WIDTH), BF16), SDS((C_HEADS // 2, T // tq, 8, tq), F32), SDS((T, C_WIDTH), BF16)],
        compiler_params=_params("parallel", "parallel"),
    )(proj, dy, o)


def _fox_bwd(qt, kt, proj, do, lse, delta, tag, ride=None):
    T = proj.shape[0]
    tq = _tile(T, FOX_TILE)
    nq = T // tq
    npair = C_HEADS // 2

    def body(q_ref, k_ref, v_ref, do_ref, lse_ref, dl_ref, *rest):
        ride_srcs, (dq_ref, dk_ref, dv_ref), ride_dsts, (dvacc_ref,), ride_sems = _ride_refs(ride, rest, 3, 1)
        j = pl.program_id(1)
        if ride:
            @pl.when((pl.program_id(0) == 0) & (j == 0))
            def _():
                ride.start(ride_srcs, ride_dsts, ride_sems)

        @pl.when(j == 0)
        def _():
            dq_ref[...] = jnp.zeros_like(dq_ref)

        dk_ref[...] = jnp.zeros_like(dk_ref)
        dvacc_ref[...] = jnp.zeros_like(dvacc_ref)
        ks = (k_ref[:, 0:128], k_ref[:, 128:256])
        kts = tuple(k.astype(F32).T.astype(BF16) for k in ks)
        vb = v_ref[...].astype(BF16)
        lo = _lane((tq, 128)) < 64

        def operands(i):
            q0 = pl.multiple_of(i * tq, tq)
            qb = q_ref[pl.ds(q0, tq), :]
            dob = do_ref[pl.ds(q0, tq), :]
            qhs = (qb[:, 0:128], qb[:, 128:256])
            dohs = (jnp.where(lo, dob, jnp.zeros_like(dob)), jnp.where(lo, jnp.zeros_like(dob), dob))
            return qhs, dohs

        def scores(i):
            qhs, dohs = operands(i)
            return tuple(_dot_nt(ks[h], qhs[h]) for h in range(2)) + tuple(_dot_nt(vb, dohs[h]) for h in range(2))

        def block(i, sc, diagonal):
            nxt = scores(jnp.minimum(i + 1, nq - 1))
            qhs, dohs = operands(i)
            lsev = lse_ref[0, i]
            dlv = dl_ref[0, i]
            pts, dsts = [], []
            for h in range(2):
                pt = jnp.exp(sc[h] - lsev[h:h + 1, :])
                if diagonal:
                    pt = jnp.where(_row((tq, tq)) <= _lane((tq, tq)), pt, 0.0)
                dsts.append((pt * (sc[2 + h] - dlv[h:h + 1, :])).astype(BF16))
                pts.append(pt.astype(BF16))
            dvacc_ref[...] += _dot(jnp.concatenate(pts, axis=1), jnp.concatenate(dohs, axis=0))
            for h in range(2):
                dk_ref[:, 128 * h:128 * h + 128] += _dot(dsts[h], qhs[h])
                dq_ref[h, i] += _dot(kts[h], dsts[h])
            return nxt

        sc = block(j, scores(j), True)
        lax.fori_loop(j + 1, nq, lambda i, c: block(i, c, False), sc)
        dv_ref[...] = dvacc_ref[...].astype(BF16)
        if ride:
            @pl.when((pl.program_id(0) == npair - 1) & (j == nq - 1))
            def _():
                ride.wait(ride_srcs, ride_dsts, ride_sems)

    full = lambda w: pl.BlockSpec((T, w), lambda p, j: (0, p))
    stat = pl.BlockSpec((1, nq, 8, tq), lambda p, j: (p, 0, 0, 0))
    extra = ride or _ChipExchange("gather", ())
    return pl.pallas_call(
        body, name=f"fox_bwd_{tag}", grid=(npair, nq),
        in_specs=[full(256), pl.BlockSpec((tq, 256), lambda p, j: (j, p)),
                  pl.BlockSpec((tq, 128), lambda p, j: (j, COL_CV // 128 + p)), full(128), stat, stat] + extra.in_specs,
        out_specs=[pl.BlockSpec((2, nq, 128, tq), lambda p, j: (p, 0, 0, 0)), pl.BlockSpec((tq, 256), lambda p, j: (j, p)),
                   pl.BlockSpec((tq, 128), lambda p, j: (j, p))] + extra.out_specs,
        out_shape=[SDS((C_HEADS, nq, 128, tq), F32), SDS((T, C_HEADS * 128), F32), SDS((T, C_WIDTH), BF16)]
        + extra.out_shape,
        scratch_shapes=[pltpu.VMEM((tq, 128), F32)] + (extra.scratch if ride else []),
        compiler_params=pltpu.CompilerParams(dimension_semantics=("arbitrary", "arbitrary"), vmem_limit_bytes=VMEM_LIMIT,
                                             has_side_effects=bool(ride)),
    )(qt, kt, proj, do, lse, delta, *extra.sources)


def _fox_bwd_post(dqt, dkt, proj, bf, tag):
    T = proj.shape[0]
    tq = _tile(T, FOX_TILE)
    n = T // tq

    def body(dq_ref, dk_ref, fl_ref, bf_ref, oq_ref, ok_ref, ofl_ref, dbf_ref, carry_ref):
        @pl.when(pl.program_id(0) == 0)
        def _():
            carry_ref[...] = jnp.zeros_like(carry_ref)
            dbf_ref[...] = jnp.zeros_like(dbf_ref)

        lane = _lane((tq, 128))
        lo = lane < 64
        dqs = [dq_ref[h, 0].T for h in range(C_HEADS)]
        dc = jnp.zeros((tq, 128), F32)
        for h in range(C_HEADS):
            dc = dc + jnp.where(lane == h, dqs[h][:, 64:65] - dk_ref[:, 128 * h + 67:128 * h + 68], 0.0)
        utri = jnp.where(_lane((tq, tq)) >= _row((tq, tq)), 1.0, 0.0).astype(BF16)
        dlf = _dot3_left(utri, dc) + carry_ref[...]
        carry_ref[...] = dlf[0:1, :]
        dfl = jnp.where(lane < C_HEADS, dlf * _sigmoid(-(fl_ref[...] + bf_ref[...])), 0.0)
        ofl_ref[...] = dfl.astype(BF16)
        dbf_ref[...] += jnp.sum(dfl, axis=0, keepdims=True)
        for p in range(C_HEADS // 2):
            a, b = 128 * (2 * p), 128 * (2 * p + 1)
            oq_ref[:, 128 * p:128 * p + 128] = (
                jnp.where(lo, dqs[2 * p], pltpu.roll(dqs[2 * p + 1], 64, axis=1)) * Q_SCALE).astype(BF16)
            ok_ref[:, 128 * p:128 * p + 128] = jnp.where(
                lo, dk_ref[:, a:a + 128], pltpu.roll(dk_ref[:, b:b + 128], 64, axis=1)).astype(BF16)

    rev = lambda w: pl.BlockSpec((tq, w), lambda i: (n - 1 - i, 0))
    return pl.pallas_call(
        body, name=f"fox_bwd_post_{tag}", grid=(n,),
        in_specs=[pl.BlockSpec((C_HEADS, 1, 128, tq), lambda i: (0, n - 1 - i, 0, 0)), rev(C_HEADS * 128),
                  pl.BlockSpec((tq, 128), lambda i: (n - 1 - i, COL_CF // 128)), pl.BlockSpec((1, 128), lambda i: (0, 0))],
        out_specs=[rev(C_WIDTH), rev(C_WIDTH), rev(128), pl.BlockSpec((1, 128), lambda i: (0, 0))],
        out_shape=[SDS((T, C_WIDTH), BF16), SDS((T, C_WIDTH), BF16), SDS((T, 128), BF16), SDS((1, 128), F32)],
        scratch_shapes=[pltpu.VMEM((1, 128), F32)], compiler_params=_params("arbitrary"),
    )(dqt, dkt, proj, bf)


def _adamw_math(w, g, m, v):
    m = ADAM_B1 * m + (1.0 - ADAM_B1) * g
    v = ADAM_B2 * v + (1.0 - ADAM_B2) * (g * g)
    delta = -ADAM_LR * ((m / ADAM_C1) / (jnp.sqrt(v / ADAM_C2) + ADAM_EPS) + ADAM_WD * w)
    return delta, m, v


def _adamw_pair(w, m, v, ga, gb, name):
    L, R, C = w.shape
    tr = _tile(R, 256)

    def body(w_ref, m_ref, v_ref, ga_ref, gb_ref, g_ref, d_ref, nm_ref, nv_ref):
        g = ga_ref[...] + gb_ref[...]
        g_ref[...] = g
        d_ref[...], nm_ref[...], nv_ref[...] = _adamw_math(w_ref[...], g, m_ref[...], v_ref[...])

    blk = pl.BlockSpec((1, tr, C), lambda l, i: (l, i, 0))
    return pl.pallas_call(
        body, name=name, grid=(L, R // tr), in_specs=[blk] * 5, out_specs=[blk] * 4,
        out_shape=[SDS(w.shape, F32)] * 4, compiler_params=_params("parallel", "parallel"),
    )(w, m, v, ga, gb)


def _adamw_small(w, m, v, gall):
    R = w.shape[0]

    def body(w_ref, m_ref, v_ref, g_ref, go_ref, d_ref, nm_ref, nv_ref):
        g = g_ref[0]
        for k in range(1, N_DEV):
            g = g + g_ref[k]
        go_ref[...] = g
        d_ref[...], nm_ref[...], nv_ref[...] = _adamw_math(w_ref[...], g, m_ref[...], v_ref[...])

    return pl.pallas_call(body, name="adamw_small", out_shape=[SDS((R, 128), F32)] * 4,
                          compiler_params=pltpu.CompilerParams(vmem_limit_bytes=VMEM_LIMIT))(w, m, v, gall)


def _sum_chips(layers, name):
    _, R, C = layers[0].shape
    tr = _tile(R, 256)

    def body(*refs):
        o_ref = refs[-1]
        for l, p_ref in enumerate(refs[:-1]):
            p = [p_ref[k].astype(F32) for k in range(N_CHIPS)]
            o_ref[l] = ((p[0] + p[1]) + p[2]) + p[3]

    return pl.pallas_call(
        body, name=name, grid=(R // tr,),
        in_specs=[pl.BlockSpec((N_CHIPS, tr, C), lambda i: (0, i, 0))] * len(layers),
        out_specs=pl.BlockSpec((len(layers), tr, C), lambda i: (0, i, 0)), out_shape=SDS((len(layers), R, C), F32),
        compiler_params=_params("parallel"),
    )(*layers)


ANY = pl.BlockSpec(memory_space=pl.ANY)


def _mesh_pos():
    return lax.axis_index("x"), lax.axis_index("y"), lax.axis_index("c")


def _other_chips(x, y):
    return [(1 - x, y), (x, 1 - y), (1 - x, 1 - y)]


class _ChipExchange:
    def __init__(self, mode, sources):
        assert mode in ("gather", "scatter")
        self.mode, self.sources = mode, tuple(sources)
        self.n = len(self.sources)
        self.in_specs = [ANY] * self.n
        self.out_specs = [ANY] * self.n
        self.out_shape = [SDS(((N_CHIPS,) + s.shape) if mode == "gather" else s.shape, s.dtype) for s in self.sources]
        self.scratch = [pltpu.SemaphoreType.DMA((3 * self.n,)), pltpu.SemaphoreType.DMA((3 * self.n,)),
                        pltpu.SemaphoreType.DMA((self.n,))]

    def _copies(self, srcs, dsts, send_sems, recv_sems, local_sems):
        x, y, c = _mesh_pos()
        me = 2 * x + y
        view = (lambda r, chip: r) if self.mode == "gather" else (lambda r, chip: r.at[chip])
        local = [pltpu.make_async_copy(view(s, me), d.at[me], local_sems.at[a]) for a, (s, d) in enumerate(zip(srcs, dsts))]
        sends, recvs = [], []
        for j, (px, py) in enumerate(_other_chips(x, y)):
            peer = 2 * px + py
            for a, (s, d) in enumerate(zip(srcs, dsts)):
                sems = dict(send_sem=send_sems.at[self.n * j + a], recv_sem=recv_sems.at[self.n * j + a],
                            device_id=(px, py, c), device_id_type=MESH_ID)
                sends.append(pltpu.make_async_remote_copy(src_ref=view(s, peer), dst_ref=d.at[me], **sems))
                recvs.append(pltpu.make_async_remote_copy(src_ref=view(s, me), dst_ref=d.at[peer], **sems))
        return local, sends, recvs

    def start(self, srcs, dsts, sems):
        local, sends, _ = self._copies(srcs, dsts, *sems)
        for cp in local + sends:
            cp.start()

    def wait(self, srcs, dsts, sems):
        local, sends, recvs = self._copies(srcs, dsts, *sems)
        for cp in recvs:
            cp.wait_recv()
        for cp in sends:
            cp.wait_send()
        for cp in local:
            cp.wait()


def _gather_weights(win, wout, tag):
    ex = _ChipExchange("gather", (win, wout))

    def body(win_ref, wout_ref, gin_ref, gout_ref, *sems):
        ex.start((win_ref, wout_ref), (gin_ref, gout_ref), sems)
        ex.wait((win_ref, wout_ref), (gin_ref, gout_ref), sems)

    return pl.pallas_call(
        body, name=f"gather_weights_{tag}", in_specs=ex.in_specs, out_specs=ex.out_specs, out_shape=ex.out_shape,
        scratch_shapes=ex.scratch, compiler_params=pltpu.CompilerParams(has_side_effects=True),
    )(win, wout)


def _exchange_grads(gin, gout, small):
    ex = _ChipExchange("scatter", (gin, gout))

    def body(gin_ref, gout_ref, small_ref, rin_ref, rout_ref, rsmall_ref, send_sems, recv_sems, local_sems,
             small_send, small_recv, small_local):
        x, y, c = _mesh_pos()
        me_dev = 4 * x + 2 * y + c
        sems = (send_sems, recv_sems, local_sems)
        ex.start((gin_ref, gout_ref), (rin_ref, rout_ref), sems)
        own = pltpu.make_async_copy(small_ref, rsmall_ref.at[me_dev], small_local)
        own.start()
        peers = [(k, ((1 - x) if k & 4 else x, (1 - y) if k & 2 else y, (1 - c) if k & 1 else c)) for k in range(1, N_DEV)]
        sends = []
        for k, peer in peers:
            cp = pltpu.make_async_remote_copy(src_ref=small_ref, dst_ref=rsmall_ref.at[me_dev], send_sem=small_send.at[k - 1],
                                              recv_sem=small_recv.at[k - 1], device_id=peer, device_id_type=MESH_ID)
            cp.start()
            sends.append(cp)
        ex.wait((gin_ref, gout_ref), (rin_ref, rout_ref), sems)
        for k, (px, py, pc) in peers:
            pltpu.make_async_remote_copy(src_ref=small_ref, dst_ref=rsmall_ref.at[4 * px + 2 * py + pc],
                                         send_sem=small_send.at[k - 1], recv_sem=small_recv.at[k - 1],
                                         device_id=(px, py, pc), device_id_type=MESH_ID).wait_recv()
        for cp in sends:
            cp.wait_send()
        own.wait()

    return pl.pallas_call(
        body, name="exchange_grads", in_specs=ex.in_specs + [ANY], out_specs=ex.out_specs + [ANY],
        out_shape=ex.out_shape + [SDS((N_DEV,) + small.shape, F32)],
        scratch_shapes=ex.scratch + [pltpu.SemaphoreType.DMA((N_DEV - 1,)), pltpu.SemaphoreType.DMA((N_DEV - 1,)),
                                     pltpu.SemaphoreType.DMA],
        compiler_params=pltpu.CompilerParams(has_side_effects=True),
    )(gin, gout, small)


def _swap_cores(pin, pout):
    def body(pin_ref, pout_ref, oin_ref, oout_ref, send_sems, recv_sems):
        x, y, c = _mesh_pos()
        cps = [pltpu.make_async_remote_copy(src_ref=src, dst_ref=dst, send_sem=send_sems.at[a], recv_sem=recv_sems.at[a],
                                            device_id=(x, y, 1 - c), device_id_type=MESH_ID)
               for a, (src, dst) in enumerate(((pin_ref, oin_ref), (pout_ref, oout_ref)))]
        for cp in cps:
            cp.start()
        for cp in cps:
            cp.wait()

    return pl.pallas_call(
        body, name="swap_cores", in_specs=[ANY, ANY], out_specs=[ANY, ANY],
        out_shape=[SDS(pin.shape, F32), SDS(pout.shape, F32)],
        scratch_shapes=[pltpu.SemaphoreType.DMA((2,)), pltpu.SemaphoreType.DMA((2,))],
        compiler_params=pltpu.CompilerParams(has_side_effects=True),
    )(pin, pout)


def _pack_small(parts):
    flat = [jnp.pad(p.reshape(-1), (0, (-p.size) % 128)) for p in parts]
    v = jnp.concatenate(flat)
    return jnp.pad(v, (0, (-v.size) % 1024)).reshape(-1, 128)


def _unpack_small(packed):
    flat = packed.reshape(-1)
    out, off = [], 0
    for _, shape in SMALL_PARAMS:
        size = math.prod(shape)
        out.append(flat[off:off + size].reshape(shape))
        off += size + (-size) % 128
    return out


def _layer_consts(l, gmlp_ln_g, gmlp_ln_b, gmlp_w_s, gmlp_b_s, hgrn_onorm_g, fox_b_f):
    causal = jnp.tril(jnp.ones((CHUNK, CHUNK), bool))
    wm = jnp.where(causal[None], gmlp_w_s[l], 0.0)
    return dict(
        lng=gmlp_ln_g[l].reshape(1, A_WIDTH), lnb=gmlp_ln_b[l].reshape(1, A_WIDTH),
        wm=wm.astype(BF16), wmt=jnp.swapaxes(wm, 1, 2).astype(BF16),
        bst=jnp.pad(gmlp_b_s[l].T, ((0, 0), (0, 128 - A_GROUPS))),
        onorm=jnp.tile(hgrn_onorm_g[l], 4).reshape(1, B_WIDTH),
        bf=jnp.pad(fox_b_f[l], (0, 128 - C_HEADS)).reshape(1, 128),
    )


def kernel(x, norm_g, w_in, w_out, gmlp_ln_g, gmlp_ln_b, gmlp_w_s, gmlp_b_s, hgrn_lb, hgrn_onorm_g, fox_b_f, final_norm_g, loss_target, m_norm_g, m_w_in, m_w_out, m_gmlp_ln_g, m_gmlp_ln_b, m_gmlp_w_s, m_gmlp_b_s, m_hgrn_lb, m_hgrn_onorm_g, m_fox_b_f, m_final_norm_g, v_norm_g, v_w_in, v_w_out, v_gmlp_ln_g, v_gmlp_ln_b, v_gmlp_w_s, v_gmlp_b_s, v_hgrn_lb, v_hgrn_onorm_g, v_fox_b_f, v_final_norm_g):
    T = x.shape[1]
    shard_in = w_in.shape[2]
    shard_out = w_out.shape[1]
    xs = x.reshape(T, D_MODEL)
    tgt = loss_target.reshape(T, D_MODEL)

    w_in_b, w_out_b = w_in.astype(BF16), w_out.astype(BF16)

    def full_weights(gathered_in, gathered_out):
        wi = jnp.concatenate([gathered_in[k] for k in range(N_CHIPS)], axis=-1)
        return jnp.pad(wi, ((0, 0), (0, D_IN_PAD - D_IN))), gathered_out.reshape(N_CHIPS * shard_out, D_MODEL)

    lb_all = _lb_fwd(hgrn_lb)
    consts = [_layer_consts(l, gmlp_ln_g, gmlp_ln_b, gmlp_w_s, gmlp_b_s, hgrn_onorm_g, fox_b_f) for l in range(DEPTH)]

    saved = []
    xl = xs
    weights = full_weights(*_gather_weights(w_in_b[0], w_out_b[0], "l0"))
    for l in range(DEPTH):
        cs = consts[l]
        tag = f"l{l}"
        w_in_l, w_out_l = weights
        h, proj = _inproj(xl, norm_g[l].reshape(1, D_MODEL), w_in_l, tag)
        ya = _gmlp_fwd(proj, cs["lng"], cs["lnb"], cs["wm"], cs["bst"], tag)
        yb, ob, s0 = _hgrn_fwd(proj, lb_all[l].reshape(1, B_WIDTH), cs["onorm"], tag)
        qt, kt, vt = _fox_prep(proj, cs["bf"], tag)
        ride = _ChipExchange("gather", (w_in_b[l + 1], w_out_b[l + 1])) if l + 1 < DEPTH else None
        oc, lse, yc, *gathered = _fox_fwd(qt, kt, vt, proj, tag, ride)
        saved.append(dict(x=xl, h=h, proj=proj, ya=ya, yb=yb, yc=yc, ob=ob, s0=s0, qt=qt, kt=kt, oc=oc, lse=lse,
                          w_in=w_in_l, w_out=w_out_l))
        xl = _outproj(xl, ya, yb, yc, w_out_l, tag)
        if ride:
            weights = full_weights(*gathered)

    dx, loss_part, d_final = _loss_head(xl, final_norm_g.reshape(1, D_MODEL), tgt)
    loss = lax.psum(loss_part[0, 0], ("x", "y", "c"))

    g_small = {}
    dlb_rows, rin, rout = [None] * DEPTH, [None] * DEPTH, [None] * DEPTH
    slabs = None
    for l in reversed(range(DEPTH)):
        cs, sv = consts[l], saved[l]
        tag = f"l{l}"
        proj = sv["proj"]
        dy, dw_out = _outproj_bwd(dx, sv["ya"], sv["yb"], sv["yc"], sv["w_out"], tag)
        da, dwm, dbst, dlng, dlnb = _gmlp_bwd(proj, dy, cs["lng"], cs["lnb"], cs["wm"], cs["wmt"], cs["bst"], tag)
        db, dlb_rows[l], donorm = _hgrn_bwd(proj, dy, sv["ob"], sv["s0"], lb_all[l].reshape(1, B_WIDTH), cs["onorm"], tag)
        do, delta, dzc = _fox_bwd_prep(proj, dy, sv["oc"], tag)
        ride = _ChipExchange("scatter", slabs) if slabs else None
        dqt, dkt, dvc, *received = _fox_bwd(sv["qt"], sv["kt"], proj, do, sv["lse"], delta, tag, ride)
        if ride:
            rin[l + 1], rout[l + 1] = received
        dqc, dkc, dflc, dbf = _fox_bwd_post(dqt, dkt, proj, cs["bf"], tag)
        dproj = jnp.concatenate([da, db, dqc, dkc, dvc, dzc, dflc, jnp.zeros((T, 128), BF16)], axis=1)
        dw_in = _dw_in(sv["h"], dproj, tag)
        dx, dng = _dx_in(sv["x"], norm_g[l].reshape(1, D_MODEL), dx, dproj, sv["w_in"], tag)
        slabs = (jnp.stack([dw_in[:, k * shard_in:(k + 1) * shard_in] for k in range(N_CHIPS)]).astype(BF16),
                 dw_out.reshape(N_CHIPS, shard_out, D_MODEL).astype(BF16))
        g_small[l] = dict(norm_g=dng.reshape(D_MODEL), ln_g=dlng.reshape(4, 64), ln_b=dlnb.reshape(4, 64), w_s=dwm,
                          b_s=dbst[:, :A_GROUPS].T, onorm=donorm[0, :64], bf=dbf[0, :C_HEADS])
    grad_x = dx.reshape(x.shape)
    d_hgrn_lb = _lb_bwd(hgrn_lb, jnp.concatenate(dlb_rows, axis=0))

    stack = lambda key: jnp.stack([g_small[l][key] for l in range(DEPTH)])
    small_g = _pack_small([stack("norm_g"), stack("ln_g"), stack("ln_b"), stack("w_s"), stack("b_s"), d_hgrn_lb,
                           stack("onorm"), stack("bf"), d_final.reshape(D_MODEL)])

    rin[0], rout[0], rsmall = _exchange_grads(*slabs, small_g)
    pin, pout = _sum_chips(rin, "sum_chips_w_in"), _sum_chips(rout, "sum_chips_w_out")
    oin, oout = _swap_cores(pin, pout)
    g_w_in, d_w_in, nm_w_in, nv_w_in = _adamw_pair(w_in, m_w_in, v_w_in, pin, oin, "adamw_w_in")
    g_w_out, d_w_out, nm_w_out, nv_w_out = _adamw_pair(w_out, m_w_out, v_w_out, pout, oout, "adamw_w_out")

    small_w = [norm_g, gmlp_ln_g, gmlp_ln_b, gmlp_w_s, gmlp_b_s, hgrn_lb, hgrn_onorm_g, fox_b_f, final_norm_g]
    small_m = [m_norm_g, m_gmlp_ln_g, m_gmlp_ln_b, m_gmlp_w_s, m_gmlp_b_s, m_hgrn_lb, m_hgrn_onorm_g, m_fox_b_f, m_final_norm_g]
    small_v = [v_norm_g, v_gmlp_ln_g, v_gmlp_ln_b, v_gmlp_w_s, v_gmlp_b_s, v_hgrn_lb, v_hgrn_onorm_g, v_fox_b_f, v_final_norm_g]
    outs = _adamw_small(_pack_small(small_w), _pack_small(small_m), _pack_small(small_v), rsmall)
    sg, sd, sm, sv_ = [_unpack_small(o) for o in outs]

    def order(big_in, big_out, small):
        return [small[0], big_in, big_out] + small[1:]

    return (loss, grad_x, *order(g_w_in, g_w_out, sg), *order(d_w_in, d_w_out, sd), *order(nm_w_in, nm_w_out, sm),
            *order(nv_w_in, nv_w_out, sv_))
```

```python
import functools
import math

import jax
import jax.numpy as jnp
from jax import lax
from jax.experimental import pallas as pl
from jax.experimental.pallas import tpu as pltpu

F32 = jnp.float32
BF16 = jnp.bfloat16
SDS = jax.ShapeDtypeStruct
MESH_ID = pl.DeviceIdType.MESH

D_MODEL = 1024
DEPTH = 2
A_WIDTH = 256
A_GROUPS = 4
B_WIDTH = 256
C_WIDTH = 512
C_HEADS = 8
D_IN = 3848
D_IN_PAD = 4096
CHUNK = 128
SUB = 16
SUB_SHIFT = 4
NORM_EPS = 1e-6
F_FLOOR = 1e-30
COL_AU, COL_AV, COL_AZ = 0, 256, 512
COL_BQ, COL_BF, COL_BI, COL_BZ = 768, 1024, 1280, 1536
COL_CQ, COL_CK, COL_CV, COL_CZ, COL_CF = 1792, 2304, 2816, 3328, 3840
HEAD_LANES = 128
Q_SCALE = 0.125
ADAM_LR, ADAM_B1, ADAM_B2, ADAM_EPS, ADAM_WD, ADAM_STEP = 0.001, 0.9, 0.999, 1e-08, 0.01, 10
ADAM_C1 = 1.0 - ADAM_B1 ** ADAM_STEP
ADAM_C2 = 1.0 - ADAM_B2 ** ADAM_STEP
VMEM_LIMIT = 56 * 1024 * 1024
ADAMW_BLOCK_BYTES = 1 << 20
N_CHIPS = 4
N_DEV = 8

SMALL_PARAMS = (
    ("norm_g", (DEPTH, D_MODEL)), ("gmlp_ln_g", (DEPTH, 4, 64)), ("gmlp_ln_b", (DEPTH, 4, 64)),
    ("gmlp_w_s", (DEPTH, 4, 128, 128)), ("gmlp_b_s", (DEPTH, 4, 128)), ("hgrn_lb", (DEPTH, 256)),
    ("hgrn_onorm_g", (DEPTH, 64)), ("fox_b_f", (DEPTH, 8)), ("final_norm_g", (D_MODEL,)),
)


def _tile(n, pref):
    t = min(n, pref)
    assert n % t == 0, (n, pref)
    return t


def _params(*sem):
    return pltpu.CompilerParams(dimension_semantics=sem, vmem_limit_bytes=VMEM_LIMIT)


def _dot(a, b):
    return jnp.dot(a, b, preferred_element_type=F32)


def _dot_nt(a, b):
    return lax.dot_general(a, b, (((1,), (1,)), ((), ())), preferred_element_type=F32)


def _dot_tn(a, b):
    return lax.dot_general(a, b, (((0,), (0,)), ((), ())), preferred_element_type=F32)


def _split3(x):
    hi = x.astype(BF16)
    r = x - hi.astype(F32)
    mid = r.astype(BF16)
    lo = (r - mid.astype(F32)).astype(BF16)
    return hi, mid, lo


def _dot3_left(c, x):
    hi, mid, lo = _split3(x)
    return _dot(c, hi) + _dot(c, mid) + _dot(c, lo)


def _sigmoid(x):
    return jax.nn.sigmoid(x)


def _silu_and_grad(x):
    s = _sigmoid(x)
    return x * s, s * (1.0 + x * (1.0 - s))


_GELU_C = math.sqrt(2.0 / math.pi)


def _gelu_and_grad(x):
    inner = _GELU_C * (x + 0.044715 * x * x * x)
    t = jnp.tanh(inner)
    y = 0.5 * x * (1.0 + t)
    dy = 0.5 * (1.0 + t) + 0.5 * x * (1.0 - t * t) * _GELU_C * (1.0 + 3.0 * 0.044715 * x * x)
    return y, dy


def _lane(shape):
    return lax.broadcasted_iota(jnp.int32, shape, 1)


def _row(shape):
    return lax.broadcasted_iota(jnp.int32, shape, 0)


def _gsum64(x):
    lo = _lane(x.shape) < 64
    s0 = jnp.sum(jnp.where(lo, x, 0.0), axis=-1, keepdims=True)
    s1 = jnp.sum(jnp.where(lo, 0.0, x), axis=-1, keepdims=True)
    return jnp.where(lo, s0, s1)


def _colreduce(x, op):
    parts = [x[r:r + 8, :] for r in range(0, x.shape[0], 8)]
    while len(parts) > 1:
        pairs = [op(parts[k], parts[k + 1]) for k in range(0, len(parts) - 1, 2)]
        parts = pairs + ([parts[-1]] if len(parts) % 2 else [])
    red = jnp.max if op is jnp.maximum else jnp.sum
    return red(parts[0], axis=0, keepdims=True)


def _block_diag64(dtype=BF16):
    r, c = _row((128, 128)), _lane((128, 128))
    return jnp.where((r >> 6) == (c >> 6), 1.0, 0.0).astype(dtype)


def _inproj(x, g, w, tag):
    T, D = x.shape
    DP = w.shape[1]
    tm, tn = _tile(T, 512), _tile(DP, 1024)

    def body(x_ref, g_ref, w_ref, h_ref, p_ref):
        @pl.when(pl.program_id(1) == 0)
        def _():
            xv = x_ref[...]
            r = lax.rsqrt(jnp.mean(xv * xv, axis=-1, keepdims=True) + NORM_EPS)
            h_ref[...] = (xv * r * g_ref[...]).astype(BF16)

        p_ref[...] = _dot(h_ref[...], w_ref[...])

    return pl.pallas_call(
        body, name=f"inproj_{tag}", grid=(T // tm, DP // tn),
        in_specs=[pl.BlockSpec((tm, D), lambda i, j: (i, 0)), pl.BlockSpec((1, D), lambda i, j: (0, 0)),
                  pl.BlockSpec((D, tn), lambda i, j: (0, j))],
        out_specs=[pl.BlockSpec((tm, D), lambda i, j: (i, 0)), pl.BlockSpec((tm, tn), lambda i, j: (i, j))],
        out_shape=[SDS((T, D), BF16), SDS((T, DP), F32)],
        compiler_params=_params("parallel", "arbitrary"),
    )(x, g, w)


def _outproj(x, ya, yb, yc, wo, tag):
    T, D = x.shape
    tm = _tile(T, 512)

    def body(x_ref, ya_ref, yb_ref, yc_ref, wo_ref, o_ref):
        acc = x_ref[...] + _dot(ya_ref[...], wo_ref[0:A_WIDTH, :])
        acc = acc + _dot(yb_ref[...], wo_ref[A_WIDTH:A_WIDTH + B_WIDTH, :])
        o_ref[...] = acc + _dot(yc_ref[...], wo_ref[A_WIDTH + B_WIDTH:, :])

    row = lambda w: pl.BlockSpec((tm, w), lambda i: (i, 0))
    return pl.pallas_call(
        body, name=f"outproj_{tag}", grid=(T // tm,),
        in_specs=[row(D), row(A_WIDTH), row(B_WIDTH), row(C_WIDTH), pl.BlockSpec(wo.shape, lambda i: (0, 0))],
        out_specs=row(D), out_shape=SDS((T, D), F32), compiler_params=_params("parallel"),
    )(x, ya, yb, yc, wo)


def _outproj_bwd(dx, ya, yb, yc, wo, tag):
    T, D = dx.shape
    DM = wo.shape[0]
    tm = _tile(T, 512)

    def body(dx_ref, ya_ref, yb_ref, yc_ref, wo_ref, dy_ref, dwo_ref):
        @pl.when(pl.program_id(0) == 0)
        def _():
            dwo_ref[...] = jnp.zeros_like(dwo_ref)

        dxb = dx_ref[...].astype(BF16)
        dy_ref[...] = _dot_nt(dxb, wo_ref[...])
        dwo_ref[0:A_WIDTH, :] += _dot_tn(ya_ref[...], dxb)
        dwo_ref[A_WIDTH:A_WIDTH + B_WIDTH, :] += _dot_tn(yb_ref[...], dxb)
        dwo_ref[A_WIDTH + B_WIDTH:, :] += _dot_tn(yc_ref[...], dxb)

    row = lambda w: pl.BlockSpec((tm, w), lambda i: (i, 0))
    return pl.pallas_call(
        body, name=f"outproj_bwd_{tag}", grid=(T // tm,),
        in_specs=[row(D), row(A_WIDTH), row(B_WIDTH), row(C_WIDTH), pl.BlockSpec(wo.shape, lambda i: (0, 0))],
        out_specs=[row(DM), pl.BlockSpec((DM, D), lambda i: (0, 0))],
        out_shape=[SDS((T, DM), F32), SDS((DM, D), F32)], compiler_params=_params("arbitrary"),
    )(dx, ya, yb, yc, wo)


def _dw_in(h, dproj, tag):
    T, D = h.shape
    DP = dproj.shape[1]
    tm, tn = _tile(T, 512), _tile(DP, 1024)

    def body(h_ref, dp_ref, dw_ref):
        @pl.when(pl.program_id(1) == 0)
        def _():
            dw_ref[...] = jnp.zeros_like(dw_ref)

        dw_ref[...] += _dot_tn(dp_ref[...], h_ref[...])

    return pl.pallas_call(
        body, name=f"dw_in_{tag}", grid=(DP // tn, T // tm),
        in_specs=[pl.BlockSpec((tm, D), lambda j, i: (i, 0)), pl.BlockSpec((tm, tn), lambda j, i: (i, j))],
        out_specs=pl.BlockSpec((tn, D), lambda j, i: (j, 0)), out_shape=SDS((DP, D), F32),
        compiler_params=_params("parallel", "arbitrary"),
    )(h, dproj)


def _dx_in(x, g, dres, dproj, w, tag):
    T, D = x.shape
    DP = w.shape[1]
    tm, tk = _tile(T, 512), _tile(DP, 1024)
    nk = DP // tk

    def body(x_ref, g_ref, dres_ref, dp_ref, w_ref, dx_ref, dg_ref, acc_ref):
        i, k = pl.program_id(0), pl.program_id(1)

        @pl.when((i == 0) & (k == 0))
        def _():
            dg_ref[...] = jnp.zeros_like(dg_ref)

        @pl.when(k == 0)
        def _():
            acc_ref[...] = jnp.zeros_like(acc_ref)

        acc_ref[...] += _dot_nt(dp_ref[...], w_ref[...])

        @pl.when(k == nk - 1)
        def _():
            xv = x_ref[...]
            r = lax.rsqrt(jnp.mean(xv * xv, axis=-1, keepdims=True) + NORM_EPS)
            xh = xv * r
            dh = acc_ref[...]
            dg_ref[...] += jnp.sum(dh * xh, axis=0, keepdims=True)
            dxh = dh * g_ref[...]
            dx_ref[...] = dres_ref[...] + r * (dxh - xh * jnp.mean(dxh * xh, axis=-1, keepdims=True))

    return pl.pallas_call(
        body, name=f"dx_in_{tag}", grid=(T // tm, nk),
        in_specs=[pl.BlockSpec((tm, D), lambda i, k: (i, 0)), pl.BlockSpec((1, D), lambda i, k: (0, 0)),
                  pl.BlockSpec((tm, D), lambda i, k: (i, 0)), pl.BlockSpec((tm, tk), lambda i, k: (i, k)),
                  pl.BlockSpec((D, tk), lambda i, k: (0, k))],
        out_specs=[pl.BlockSpec((tm, D), lambda i, k: (i, 0)), pl.BlockSpec((1, D), lambda i, k: (0, 0))],
        out_shape=[SDS((T, D), F32), SDS((1, D), F32)],
        scratch_shapes=[pltpu.VMEM((tm, D), F32)], compiler_params=_params("arbitrary", "arbitrary"),
    )(x, g, dres, dproj, w)


def _loss_head(x, g, tgt):
    T, D = x.shape
    tm = _tile(T, 512)

    def body(x_ref, g_ref, t_ref, dx_ref, loss_ref, dg_ref):
        @pl.when(pl.program_id(0) == 0)
        def _():
            loss_ref[...] = jnp.zeros_like(loss_ref)
            dg_ref[...] = jnp.zeros_like(dg_ref)

        xv = x_ref[...]
        r = lax.rsqrt(jnp.mean(xv * xv, axis=-1, keepdims=True) + NORM_EPS)
        xh = xv * r
        gv = g_ref[...]
        err = xh * gv - t_ref[...]
        tok = jnp.mean(err * err, axis=-1, keepdims=True)
        loss_ref[...] += 0.5 * jnp.sum(tok, axis=0, keepdims=True)
        dy = err * (1.0 / D)
        dg_ref[...] += jnp.sum(dy * xh, axis=0, keepdims=True)
        dxh = dy * gv
        dx_ref[...] = r * (dxh - xh * jnp.mean(dxh * xh, axis=-1, keepdims=True))

    row = pl.BlockSpec((tm, D), lambda i: (i, 0))
    return pl.pallas_call(
        body, name="loss_head", grid=(T // tm,),
        in_specs=[row, pl.BlockSpec((1, D), lambda i: (0, 0)), row],
        out_specs=[row, pl.BlockSpec((1, 128), lambda i: (0, 0)), pl.BlockSpec((1, D), lambda i: (0, 0))],
        out_shape=[SDS((T, D), F32), SDS((1, 128), F32), SDS((1, D), F32)], compiler_params=_params("arbitrary"),
    )(x, g, tgt)


def _gmlp_core(u, v, lng, lnb, wm_ref, bst_ref, pair):
    ug, dug = _gelu_and_grad(u)
    vg, dvg = _gelu_and_grad(v)
    mu = _gsum64(vg) * (1.0 / 64)
    d = vg - mu
    var = _gsum64(d * d) * (1.0 / 64)
    rstd = lax.rsqrt(var + NORM_EPS)
    xh = d * rstd
    vn = xh * lng + lnb
    vnb = vn.astype(BF16)
    lo = _lane(u.shape) < 64
    g0, g1 = 2 * pair, 2 * pair + 1
    mixed = jnp.where(lo, _dot(wm_ref[g0], vnb) + bst_ref[:, g0:g0 + 1], _dot(wm_ref[g1], vnb) + bst_ref[:, g1:g1 + 1])
    return ug, dug, dvg, rstd, xh, vnb, mixed, lo


def _gmlp_fwd(proj, lng, lnb, wm, bst, tag):
    T = proj.shape[0]

    def body(u_ref, v_ref, z_ref, lng_ref, lnb_ref, wm_ref, bst_ref, y_ref):
        for pair in range(2):
            sl = slice(128 * pair, 128 * pair + 128)
            ug, _, _, _, _, _, mixed, _ = _gmlp_core(u_ref[:, sl], v_ref[:, sl], lng_ref[:, sl], lnb_ref[:, sl],
                                                     wm_ref, bst_ref, pair)
            sz, _ = _silu_and_grad(z_ref[:, sl])
            y_ref[:, sl] = (ug * mixed * sz).astype(BF16)

    col = lambda c: pl.BlockSpec((CHUNK, A_WIDTH), lambda i, c=c: (i, c // A_WIDTH))
    full = lambda a: pl.BlockSpec(a.shape, lambda i, n=a.ndim: (0,) * n)
    return pl.pallas_call(
        body, name=f"gmlp_fwd_{tag}", grid=(T // CHUNK,),
        in_specs=[col(COL_AU), col(COL_AV), col(COL_AZ), full(lng), full(lnb), full(wm), full(bst)],
        out_specs=pl.BlockSpec((CHUNK, A_WIDTH), lambda i: (i, 0)), out_shape=SDS((T, A_WIDTH), BF16),
        compiler_params=_params("parallel"),
    )(proj, proj, proj, lng, lnb, wm, bst)


def _gmlp_bwd(proj, dy, lng, lnb, wm, wmt, bst, tag):
    T = proj.shape[0]
    n = T // CHUNK

    def body(u_ref, v_ref, z_ref, dy_ref, lng_ref, lnb_ref, wm_ref, wmt_ref, bst_ref,
             da_ref, dwm_ref, dbst_ref, dlng_ref, dlnb_ref):
        @pl.when(pl.program_id(0) == 0)
        def _():
            dwm_ref[...] = jnp.zeros_like(dwm_ref)
            dbst_ref[...] = jnp.zeros_like(dbst_ref)
            dlng_ref[...] = jnp.zeros_like(dlng_ref)
            dlnb_ref[...] = jnp.zeros_like(dlnb_ref)

        lane = _lane((CHUNK, 128))
        dbst = dbst_ref[...]
        for pair in range(2):
            sl = slice(128 * pair, 128 * pair + 128)
            lng_p = lng_ref[:, sl]
            ug, dug, dvg, rstd, xh, vnb, mixed, lo = _gmlp_core(u_ref[:, sl], v_ref[:, sl], lng_p, lnb_ref[:, sl],
                                                                wm_ref, bst_ref, pair)
            sz, dsz = _silu_and_grad(z_ref[:, sl])
            dyv = dy_ref[:, sl]
            out = ug * mixed
            dz = dyv * out * dsz
            dout = dyv * sz
            du = dout * mixed * dug
            dmix = dout * ug
            g0, g1 = 2 * pair, 2 * pair + 1
            dm0 = jnp.where(lo, dmix, 0.0)
            dm1 = jnp.where(lo, 0.0, dmix)
            dbst = dbst + jnp.where(lane == g0, jnp.sum(dm0, axis=-1, keepdims=True), 0.0)
            dbst = dbst + jnp.where(lane == g1, jnp.sum(dm1, axis=-1, keepdims=True), 0.0)
            dwm_ref[g0] += _dot_nt(dm0.astype(BF16), vnb)
            dwm_ref[g1] += _dot_nt(dm1.astype(BF16), vnb)
            dmb = dmix.astype(BF16)
            dvn = jnp.where(lo, _dot(wmt_ref[g0], dmb), _dot(wmt_ref[g1], dmb))
            dlng_ref[:, sl] += jnp.sum(dvn * xh, axis=0, keepdims=True)
            dlnb_ref[:, sl] += jnp.sum(dvn, axis=0, keepdims=True)
            dxh = dvn * lng_p
            m1 = _gsum64(dxh) * (1.0 / 64)
            m2 = _gsum64(dxh * xh) * (1.0 / 64)
            dv = rstd * (dxh - m1 - xh * m2) * dvg
            da_ref[:, COL_AU + 128 * pair:COL_AU + 128 * pair + 128] = du.astype(BF16)
            da_ref[:, COL_AV + 128 * pair:COL_AV + 128 * pair + 128] = dv.astype(BF16)
            da_ref[:, COL_AZ + 128 * pair:COL_AZ + 128 * pair + 128] = dz.astype(BF16)
        dbst_ref[...] = dbst

        @pl.when(pl.program_id(0) == n - 1)
        def _():
            causal = _lane((CHUNK, CHUNK)) <= _row((CHUNK, CHUNK))
            for g in range(A_GROUPS):
                dwm_ref[g] = jnp.where(causal, dwm_ref[g], 0.0)

    col = lambda c: pl.BlockSpec((CHUNK, A_WIDTH), lambda i, c=c: (i, c // A_WIDTH))
    full = lambda a: pl.BlockSpec(a.shape, lambda i, n=a.ndim: (0,) * n)
    acc = lambda s: pl.BlockSpec(s, lambda i, n=len(s): (0,) * n)
    return pl.pallas_call(
        body, name=f"gmlp_bwd_{tag}", grid=(n,),
        in_specs=[col(COL_AU), col(COL_AV), col(COL_AZ), pl.BlockSpec((CHUNK, A_WIDTH), lambda i: (i, 0)),
                  full(lng), full(lnb), full(wm), full(wmt), full(bst)],
        out_specs=[pl.BlockSpec((CHUNK, 3 * A_WIDTH), lambda i: (i, 0)), acc((A_GROUPS, CHUNK, CHUNK)),
                   acc((CHUNK, 128)), acc((1, A_WIDTH)), acc((1, A_WIDTH))],
        out_shape=[SDS((T, 3 * A_WIDTH), BF16), SDS((A_GROUPS, CHUNK, CHUNK), F32), SDS((CHUNK, 128), F32),
                   SDS((1, A_WIDTH), F32), SDS((1, A_WIDTH), F32)],
        compiler_params=_params("arbitrary"),
    )(proj, proj, proj, dy, lng, lnb, wm, wmt, bst)


def _hgrn_consts():
    r, c = _row((CHUNK, CHUNK)), _lane((CHUNK, CHUNK))
    same = (r >> SUB_SHIFT) == (c >> SUB_SHIFT)
    lsub = jnp.where(same & (c <= r), 1.0, 0.0).astype(BF16)
    usub = jnp.where(same & (c >= r), 1.0, 0.0).astype(BF16)
    bsub = jnp.where(same, 1.0, 0.0).astype(BF16)
    return lsub, usub, bsub


def _hgrn_gates(qv, zf, lbp):
    sq, dsq = _silu_and_grad(qv)
    qt = sq * Q_SCALE
    sg = _sigmoid(zf)
    sgn = _sigmoid(-zf)
    f = lbp + (1.0 - lbp) * sg
    g = jnp.log(jnp.maximum(f, F_FLOOR))
    kf = (1.0 - lbp) * sgn
    return qt, dsq, sg, sgn, f, g, kf


def _hgrn_intra_fwd(qt, kf, b, v, mbd):
    rid = _row((SUB, 128))
    parts = []
    for s in range(SUB):
        e = jnp.exp(jnp.minimum(b - b[s:s + 1, :], 0.0))
        parts.append(jnp.where(rid >= s, qt * kf[s:s + 1, :] * e, 0.0))
    a = _dot(jnp.concatenate(parts, axis=0).astype(BF16), mbd)
    o = jnp.zeros((SUB, 128), F32)
    for s in range(SUB):
        o = o + a[SUB * s:SUB * s + SUB, :] * v[s:s + 1, :]
    return o


def _hgrn_intra_bwd(qt, kf, b, v, do, mbd, rsum):
    rid = _row((SUB, 128))
    ps, das, kes, es = [], [], [], []
    for s in range(SUB):
        e = jnp.where(rid >= s, jnp.exp(jnp.minimum(b - b[s:s + 1, :], 0.0)), 0.0)
        ke = kf[s:s + 1, :] * e
        es.append(e)
        kes.append(ke)
        ps.append(qt * ke)
        das.append(do * v[s:s + 1, :])
    a = _dot(jnp.concatenate(ps, axis=0).astype(BF16), mbd)
    da = _dot(jnp.concatenate(das, axis=0).astype(BF16), mbd)
    dqt = jnp.zeros((SUB, 128), F32)
    xs, ys = [], []
    for s in range(SUB):
        da_s = da[SUB * s:SUB * s + SUB, :]
        dqt = dqt + da_s * kes[s]
        xs.append(a[SUB * s:SUB * s + SUB, :] * do)
        ys.append(da_s * qt * es[s])
    xh = jnp.concatenate(xs, axis=0)
    yh = jnp.concatenate(ys, axis=0)
    xhi = xh.astype(BF16)
    yhi = yh.astype(BF16)
    dv = _dot(rsum, xhi) + _dot(rsum, (xh - xhi.astype(F32)).astype(BF16))
    dkf = _dot(rsum, yhi) + _dot(rsum, (yh - yhi.astype(F32)).astype(BF16))
    return dqt, dkf, dv


def _hgrn_norm_gate(o, z, onorm):
    ms = _gsum64(o * o) * (1.0 / 64)
    r = lax.rsqrt(ms + NORM_EPS)
    xh = o * r
    sz, dsz = _silu_and_grad(z)
    return xh, r, sz, dsz, xh * onorm


def _hgrn_fwd(proj, lb, onorm, tag):
    T = proj.shape[0]
    n = T // CHUNK
    nsub = CHUNK // SUB

    def body(q_ref, f_ref, i_ref, z_ref, lb_ref, on_ref, y_ref, o_ref, s0_ref, st_ref):
        @pl.when(pl.program_id(0) == 0)
        def _():
            st_ref[...] = jnp.zeros_like(st_ref)

        lsub, _, bsub = _hgrn_consts()
        mbd = _block_diag64()
        bdmask = mbd > 0
        rid = _row((CHUNK, 128))
        for pair in range(2):
            sl = slice(128 * pair, 128 * pair + 128)
            qt, _, _, _, _, g, kf = _hgrn_gates(q_ref[:, sl], f_ref[:, sl], lb_ref[:, sl])
            v = i_ref[:, sl]
            b = _dot3_left(lsub, g)
            bl = _dot3_left(bsub, g)
            qh = (qt * jnp.exp(b)).astype(BF16)
            kh = kf * jnp.exp(bl - b)
            dec = jnp.exp(bl)
            vtb = v.T.astype(BF16)
            st = st_ref[pair]
            s0_ref[0, pair] = st
            outs = []
            for sub in range(nsub):
                rs = slice(SUB * sub, SUB * sub + SUB)
                o_inter = _dot_nt(qh[rs], st.astype(BF16))
                outs.append(o_inter + _hgrn_intra_fwd(qt[rs], kf[rs], b[rs], v[rs], mbd))
                khm = jnp.where((rid >> SUB_SHIFT) == sub, kh, 0.0).astype(BF16)
                st = jnp.where(bdmask, st * dec[SUB * sub:SUB * sub + 1, :] + _dot(vtb, khm), 0.0)
            st_ref[pair] = st
            o = jnp.concatenate(outs, axis=0)
            o_ref[:, sl] = o
            _, _, sz, _, on = _hgrn_norm_gate(o, z_ref[:, sl], on_ref[:, sl])
            y_ref[:, sl] = (on * sz).astype(BF16)

    col = lambda c: pl.BlockSpec((CHUNK, B_WIDTH), lambda i, c=c: (i, c // B_WIDTH))
    full = lambda a: pl.BlockSpec(a.shape, lambda i, n=a.ndim: (0,) * n)
    return pl.pallas_call(
        body, name=f"hgrn_fwd_{tag}", grid=(n,),
        in_specs=[col(COL_BQ), col(COL_BF), col(COL_BI), col(COL_BZ), full(lb), full(onorm)],
        out_specs=[pl.BlockSpec((CHUNK, B_WIDTH), lambda i: (i, 0)), pl.BlockSpec((CHUNK, B_WIDTH), lambda i: (i, 0)),
                   pl.BlockSpec((1, 2, 128, 128), lambda i: (i, 0, 0, 0))],
        out_shape=[SDS((T, B_WIDTH), BF16), SDS((T, B_WIDTH), F32), SDS((n, 2, 128, 128), F32)],
        scratch_shapes=[pltpu.VMEM((2, 128, 128), F32)], compiler_params=_params("arbitrary"),
    )(proj, proj, proj, proj, lb, onorm)


def _hgrn_bwd(proj, dy, o_saved, s0, lb, onorm, tag):
    T = proj.shape[0]
    n = T // CHUNK
    nsub = CHUNK // SUB

    def body(q_ref, f_ref, i_ref, z_ref, dy_ref, o_ref, s0_ref, lb_ref, on_ref,
             db_ref, dlb_ref, don_ref, dst_ref, sts_ref):
        @pl.when(pl.program_id(0) == 0)
        def _():
            dst_ref[...] = jnp.zeros_like(dst_ref)
            dlb_ref[...] = jnp.zeros_like(dlb_ref)
            don_ref[...] = jnp.zeros_like(don_ref)

        lsub, usub, bsub = _hgrn_consts()
        mbd = _block_diag64()
        bdmask = mbd > 0
        rid = _row((CHUNK, 128))
        rsum = jnp.where((_lane((SUB, SUB * SUB)) >> SUB_SHIFT) == _row((SUB, SUB * SUB)), 1.0, 0.0).astype(BF16)
        for pair in range(2):
            sl = slice(128 * pair, 128 * pair + 128)
            lbp = lb_ref[:, sl]
            qv, zf = q_ref[:, sl], f_ref[:, sl]
            qt, dsq, sg, sgn, f, g, kf = _hgrn_gates(qv, zf, lbp)
            v = i_ref[:, sl]
            b = _dot3_left(lsub, g)
            bl = _dot3_left(bsub, g)
            eb = jnp.exp(b)
            ekb = jnp.exp(bl - b)
            qh = qt * eb
            kh = kf * ekb
            dec = jnp.exp(bl)
            vtb = v.T.astype(BF16)
            onp = on_ref[:, sl]
            ov = o_ref[:, sl]
            xh, r, sz, dsz, on = _hgrn_norm_gate(ov, z_ref[:, sl], onp)
            dyv = dy_ref[:, sl]
            dz = dyv * on * dsz
            don = dyv * sz
            cn = jnp.sum(don * xh, axis=0, keepdims=True)
            don_ref[...] += cn + pltpu.roll(cn, 64, axis=1)
            dxo = don * onp
            do = r * (dxo - xh * (_gsum64(dxo * xh) * (1.0 / 64)))
            dotb = do.T.astype(BF16)
            st = s0_ref[0, pair]
            for sub in range(nsub):
                sts_ref[sub] = st
                khm = jnp.where((rid >> SUB_SHIFT) == sub, kh, 0.0).astype(BF16)
                st = jnp.where(bdmask, st * dec[SUB * sub:SUB * sub + 1, :] + _dot(vtb, khm), 0.0)
            gst = dst_ref[pair]
            dqt_p, dkf_p, dv_p, dbl_p = [None] * nsub, [None] * nsub, [None] * nsub, [None] * nsub
            for sub in reversed(range(nsub)):
                rs = slice(SUB * sub, SUB * sub + SUB)
                st_in = sts_ref[sub]
                gb = gst.astype(BF16)
                dob = do[rs].astype(BF16)
                dqh = _dot(dob, st_in.astype(BF16))
                dkh = _dot(v[rs].astype(BF16), gb)
                dv_inter = _dot_nt(kh[rs].astype(BF16), gb)
                ddec = jnp.sum(gst * st_in, axis=0, keepdims=True)
                dec_row = dec[SUB * sub:SUB * sub + 1, :]
                qhm = jnp.where((rid >> SUB_SHIFT) == sub, qh, 0.0).astype(BF16)
                gst = jnp.where(bdmask, gst * dec_row + _dot(dotb, qhm), 0.0)
                dqt_i, dkf_i, dv_i = _hgrn_intra_bwd(qt[rs], kf[rs], b[rs], v[rs], do[rs], mbd, rsum)
                dkf_inter = dkh * ekb[rs]
                dqt_p[sub] = dqh * eb[rs] + dqt_i
                dkf_p[sub] = dkf_inter + dkf_i
                dv_p[sub] = dv_inter + dv_i
                row = jnp.sum(kf[rs] * dkf_inter, axis=0, keepdims=True) + ddec * dec_row
                dbl_p[sub] = jnp.broadcast_to(row, (SUB, 128))
            dst_ref[pair] = gst
            dqt = jnp.concatenate(dqt_p, axis=0)
            dkf = jnp.concatenate(dkf_p, axis=0)
            dv = jnp.concatenate(dv_p, axis=0)
            dg = _dot3_left(usub, qt * dqt - kf * dkf) + jnp.concatenate(dbl_p, axis=0)
            df = jnp.where(f > F_FLOOR, dg / f, 0.0)
            dlb_ref[:, sl] += jnp.sum(df * (1.0 - sg) - dkf * sgn, axis=0, keepdims=True)
            dfl = (1.0 - lbp) * sg * sgn * (df - dkf)
            dq = dqt * Q_SCALE * dsq
            db_ref[:, 0 * B_WIDTH + 128 * pair:0 * B_WIDTH + 128 * pair + 128] = dq.astype(BF16)
            db_ref[:, 1 * B_WIDTH + 128 * pair:1 * B_WIDTH + 128 * pair + 128] = dfl.astype(BF16)
            db_ref[:, 2 * B_WIDTH + 128 * pair:2 * B_WIDTH + 128 * pair + 128] = dv.astype(BF16)
            db_ref[:, 3 * B_WIDTH + 128 * pair:3 * B_WIDTH + 128 * pair + 128] = dz.astype(BF16)

    rev = lambda c: pl.BlockSpec((CHUNK, B_WIDTH), lambda i, c=c: (n - 1 - i, c // B_WIDTH))
    full = lambda a: pl.BlockSpec(a.shape, lambda i, n_=a.ndim: (0,) * n_)
    acc = lambda s: pl.BlockSpec(s, lambda i, n_=len(s): (0,) * n_)
    return pl.pallas_call(
        body, name=f"hgrn_bwd_{tag}", grid=(n,),
        in_specs=[rev(COL_BQ), rev(COL_BF), rev(COL_BI), rev(COL_BZ),
                  pl.BlockSpec((CHUNK, B_WIDTH), lambda i: (n - 1 - i, 1)),
                  pl.BlockSpec((CHUNK, B_WIDTH), lambda i: (n - 1 - i, 0)),
                  pl.BlockSpec((1, 2, 128, 128), lambda i: (n - 1 - i, 0, 0, 0)), full(lb), full(onorm)],
        out_specs=[pl.BlockSpec((CHUNK, 4 * B_WIDTH), lambda i: (n - 1 - i, 0)), acc((1, B_WIDTH)), acc((1, 128))],
        out_shape=[SDS((T, 4 * B_WIDTH), BF16), SDS((1, B_WIDTH), F32), SDS((1, 128), F32)],
        scratch_shapes=[pltpu.VMEM((2, 128, 128), F32), pltpu.VMEM((nsub, 128, 128), F32)],
        compiler_params=_params("arbitrary"),
    )(proj, proj, proj, proj, dy, o_saved, s0, lb, onorm)


def _lb_fwd(hgrn_lb):
    assert hgrn_lb.shape[0] == 2

    def body(x_ref, o_ref):
        x0, x1 = x_ref[0:1, :], x_ref[1:2, :]
        m = jnp.maximum(x0, x1)
        e0, e1 = jnp.exp(x0 - m), jnp.exp(x1 - m)
        p0, p1 = e0 / (e0 + e1), e1 / (e0 + e1)
        o_ref[0:1, :] = jnp.clip(p0 - p0, 0.0, 1.0 - 1e-6)
        o_ref[1:2, :] = jnp.clip((p0 + p1) - p0, 0.0, 1.0 - 1e-6)

    return pl.pallas_call(body, name="lb_fwd", out_shape=SDS(hgrn_lb.shape, F32))(hgrn_lb)


def _lb_bwd(hgrn_lb, dlb):
    def body(x_ref, d_ref, o_ref):
        x0, x1 = x_ref[0:1, :], x_ref[1:2, :]
        m = jnp.maximum(x0, x1)
        e0, e1 = jnp.exp(x0 - m), jnp.exp(x1 - m)
        p0, p1 = e0 / (e0 + e1), e1 / (e0 + e1)
        val = (p0 + p1) - p0
        dp1 = jnp.where((val > 0.0) & (val < 1.0 - 1e-6), d_ref[1:2, :], 0.0)
        inner = p1 * dp1
        o_ref[0:1, :] = p0 * (0.0 - inner)
        o_ref[1:2, :] = p1 * (dp1 - inner)

    return pl.pallas_call(body, name="lb_bwd", out_shape=SDS(hgrn_lb.shape, F32))(hgrn_lb, dlb)


def _fox_prep(proj, bf, tag):
    T = proj.shape[0]
    n = T // CHUNK

    def body(q0_ref, q1_ref, k0_ref, k1_ref, v0_ref, v1_ref, fl_ref, bf_ref, qo_ref, ko_ref, vt_ref, carry_ref):
        for p, v_ref in enumerate((v0_ref, v0_ref, v1_ref, v1_ref)):
            vt_ref[p, 0] = v_ref[:, 128 * (p % 2):128 * (p % 2) + 128].T.astype(BF16)

        @pl.when(pl.program_id(0) == 0)
        def _():
            carry_ref[...] = jnp.zeros_like(carry_ref)

        ltri = jnp.where(_lane((CHUNK, CHUNK)) <= _row((CHUNK, CHUNK)), 1.0, 0.0).astype(BF16)
        lf = jax.nn.log_sigmoid(fl_ref[...] + bf_ref[...])
        c = _dot3_left(ltri, lf) + carry_ref[...]
        carry_ref[...] = c[CHUNK - 1:CHUNK, :]
        lane = _lane((CHUNK, 128))
        feat = lane < 64
        ones_q = (lane >= 67) & (lane <= 69)
        ones_k = (lane >= 64) & (lane <= 66)
        qrefs, krefs = (q0_ref, q1_ref), (k0_ref, k1_ref)
        for h in range(C_HEADS):
            blk = slice(128 * ((h // 2) % 2), 128 * ((h // 2) % 2) + 128)
            qp, kp = qrefs[h // 4][:, blk], krefs[h // 4][:, blk]
            if h % 2:
                qp, kp = pltpu.roll(qp, 64, axis=1), pltpu.roll(kp, 64, axis=1)
            ch = jnp.broadcast_to(c[:, h:h + 1], (CHUNK, 128))
            hi = ch.astype(BF16).astype(F32)
            r1 = ch - hi
            mid = r1.astype(BF16).astype(F32)
            lo = r1 - mid
            aq = jnp.where(lane == 64, hi, jnp.where(lane == 65, mid, jnp.where(lane == 66, lo,
                           jnp.where(ones_q, 1.0, 0.0))))
            ak = jnp.where(lane == 67, -hi, jnp.where(lane == 68, -mid, jnp.where(lane == 69, -lo,
                           jnp.where(ones_k, 1.0, 0.0))))
            qo_ref[:, 128 * h:128 * h + 128] = jnp.where(feat, qp * Q_SCALE, aq).astype(BF16)
            ko_ref[:, 128 * h:128 * h + 128] = jnp.where(feat, kp, ak).astype(BF16)

    w = 256
    col = lambda c: pl.BlockSpec((CHUNK, w), lambda i, c=c: (i, c // w))
    return pl.pallas_call(
        body, name=f"fox_prep_{tag}", grid=(n,),
        in_specs=[col(COL_CQ), col(COL_CQ + w), col(COL_CK), col(COL_CK + w), col(COL_CV), col(COL_CV + w),
                  pl.BlockSpec((CHUNK, 128), lambda i: (i, COL_CF // 128)), pl.BlockSpec((1, 128), lambda i: (0, 0))],
        out_specs=[pl.BlockSpec((CHUNK, C_HEADS * 128), lambda i: (i, 0))] * 2
        + [pl.BlockSpec((C_HEADS // 2, 1, 128, CHUNK), lambda i: (0, i, 0, 0))],
        out_shape=[SDS((T, C_HEADS * 128), BF16)] * 2 + [SDS((C_HEADS // 2, n, 128, CHUNK), BF16)],
        scratch_shapes=[pltpu.VMEM((1, 128), F32)], compiler_params=_params("arbitrary"),
    )(proj, proj, proj, proj, proj, proj, proj, bf)


FOX_TILE = 256


def _ride_refs(ride, rest, n_out, n_scratch):
    n = ride.n if ride else 0
    srcs, rest = rest[:n], rest[n:]
    outs, rest = rest[:n_out], rest[n_out:]
    dsts, rest = rest[:n], rest[n:]
    return srcs, outs, dsts, rest[:n_scratch], rest[n_scratch:]


def _fox_fwd(qt, kt, vt, proj, tag, ride=None):
    T = proj.shape[0]
    tq = _tile(T, FOX_TILE)
    nq, nsub = T // tq, tq // CHUNK
    npair = C_HEADS // 2

    def body(q_ref, k_ref, vt_ref, z_ref, *rest):
        ride_srcs, (o_ref, lse_ref, y_ref), ride_dsts, (acc_ref,), ride_sems = _ride_refs(ride, rest, 3, 1)
        i = pl.program_id(1)
        if ride:
            @pl.when((pl.program_id(0) == 0) & (i == 0))
            def _():
                ride.start(ride_srcs, ride_dsts, ride_sems)

        qs = (q_ref[:, 0:128], q_ref[:, 128:256])
        acc_ref[...] = jnp.zeros_like(acc_ref)

        def scores(j):
            kb = k_ref[pl.ds(pl.multiple_of(j * tq, tq), tq), :]
            return tuple(_dot_nt(kb[:, 128 * h:128 * h + 128], qs[h]) for h in range(2))

        def block(j, carry, diagonal):
            sts = carry[4:6]
            nxt = () if diagonal else scores(j + 1)
            new = []
            for h in range(2):
                m, l = carry[2 * h], carry[2 * h + 1]
                st = sts[h]
                if diagonal:
                    st = jnp.where(_row((tq, tq)) <= _lane((tq, tq)), st, -jnp.inf)
                m_new = jnp.maximum(m, _colreduce(st, jnp.maximum))
                pt = jnp.exp(st - m_new)
                alpha = jnp.exp(m - m_new)
                ptb = pt.astype(BF16)
                rows = slice(64 * h, 64 * h + 64)
                pv = _dot(vt_ref[0, nsub * j, rows, :], ptb[0:CHUNK, :])
                for c in range(1, nsub):
                    pv = pv + _dot(vt_ref[0, nsub * j + c, rows, :], ptb[CHUNK * c:CHUNK * c + CHUNK, :])
                acc_ref[rows, :] = alpha * acc_ref[rows, :] + pv
                new += [m_new, alpha * l + _colreduce(pt, jnp.add)]
            return tuple(new) + nxt

        init = (jnp.full((1, tq), -jnp.inf, F32), jnp.zeros((1, tq), F32)) * 2 + scores(0)
        carry = lax.fori_loop(0, i, lambda j, c: block(j, c, False), init)
        m0, l0, m1, l1 = block(i, carry, True)
        inv = jnp.where(_row((128, tq)) < 64, 1.0 / l0, 1.0 / l1)
        o = (acc_ref[...] * inv).T
        o_ref[...] = o
        r8 = _row((8, tq))
        lse_ref[0, 0] = jnp.where(r8 == 0, m0 + jnp.log(l0), jnp.where(r8 == 1, m1 + jnp.log(l1), 0.0))
        sz, _ = _silu_and_grad(z_ref[...])
        y_ref[...] = (o * sz).astype(BF16)
        if ride:
            @pl.when((pl.program_id(0) == npair - 1) & (i == nq - 1))
            def _():
                ride.wait(ride_srcs, ride_dsts, ride_sems)

    blk = pl.BlockSpec((tq, 128), lambda p, i: (i, p))
    extra = ride or _ChipExchange("gather", ())
    return pl.pallas_call(
        body, name=f"fox_fwd_{tag}", grid=(npair, nq),
        in_specs=[pl.BlockSpec((tq, 256), lambda p, i: (i, p)), pl.BlockSpec((T, 256), lambda p, i: (0, p)),
                  pl.BlockSpec((1, T // CHUNK, 128, CHUNK), lambda p, i: (p, 0, 0, 0)),
                  pl.BlockSpec((tq, 128), lambda p, i: (i, COL_CZ // 128 + p))] + extra.in_specs,
        out_specs=[blk, pl.BlockSpec((1, 1, 8, tq), lambda p, i: (p, i, 0, 0)), blk] + extra.out_specs,
        out_shape=[SDS((T, C_WIDTH), F32), SDS((npair, nq, 8, tq), F32), SDS((T, C_WIDTH), BF16)] + extra.out_shape,
        scratch_shapes=[pltpu.VMEM((128, tq), F32)] + (extra.scratch if ride else []),
        compiler_params=pltpu.CompilerParams(dimension_semantics=("arbitrary", "arbitrary"), vmem_limit_bytes=VMEM_LIMIT,
                                             has_side_effects=bool(ride)),
    )(qt, kt, vt, proj, *extra.sources)


def _fox_bwd_prep(proj, dy, o, tag):
    T = proj.shape[0]
    tq = _tile(T, FOX_TILE)

    def body(z_ref, dy_ref, o_ref, do_ref, dl_ref, dz_ref):
        sz, dsz = _silu_and_grad(z_ref[...])
        dyv, ov = dy_ref[...], o_ref[...]
        do = dyv * sz
        do_ref[...] = do.astype(BF16)
        dz_ref[...] = (dyv * ov * dsz).astype(BF16)
        sel = jnp.where((_lane((16, 128)) >> 6) == _row((16, 128)), 1.0, 0.0).astype(BF16)
        hi, mid, lo = _split3(do * ov)
        dl_ref[0, 0] = (_dot_nt(sel, hi) + _dot_nt(sel, mid) + _dot_nt(sel, lo))[0:8, :]

    blk = pl.BlockSpec((tq, 128), lambda i, p: (i, p))
    return pl.pallas_call(
        body, name=f"fox_bwd_prep_{tag}", grid=(T // tq, C_WIDTH // 128),
        in_specs=[pl.BlockSpec((tq, 128), lambda i, p: (i, COL_CZ // 128 + p)),
                  pl.BlockSpec((tq, 128), lambda i, p: (i, (A_WIDTH + B_WIDTH) // 128 + p)), blk],
        out_specs=[blk, pl.BlockSpec((1, 1, 8, tq), lambda i, p: (p, i, 0, 0)), blk],
        out_shape=[SDS((T, C_WIDTH), BF16), SDS((C_HEADS // 2, T // tq, 8, tq), F32), SDS((T, C_WIDTH), BF16)],
        compiler_params=_params("parallel", "parallel"),
    )(proj, dy, o)


def _fox_bwd(qt, kt, proj, do, lse, delta, tag, ride=None):
    T = proj.shape[0]
    tq = _tile(T, FOX_TILE)
    nq = T // tq
    npair = C_HEADS // 2

    def body(q_ref, k_ref, v_ref, do_ref, lse_ref, dl_ref, *rest):
        ride_srcs, (dq_ref, dk_ref, dv_ref), ride_dsts, (dvacc_ref,), ride_sems = _ride_refs(ride, rest, 3, 1)
        j = pl.program_id(1)
        if ride:
            @pl.when((pl.program_id(0) == 0) & (j == 0))
            def _():
                ride.start(ride_srcs, ride_dsts, ride_sems)

        @pl.when(j == 0)
        def _():
            dq_ref[...] = jnp.zeros_like(dq_ref)

        dk_ref[...] = jnp.zeros_like(dk_ref)
        dvacc_ref[...] = jnp.zeros_like(dvacc_ref)
        ks = (k_ref[:, 0:128], k_ref[:, 128:256])
        kts = tuple(k.astype(F32).T.astype(BF16) for k in ks)
        vb = v_ref[...].astype(BF16)
        lo = _lane((tq, 128)) < 64

        def operands(i):
            q0 = pl.multiple_of(i * tq, tq)
            qb = q_ref[pl.ds(q0, tq), :]
            dob = do_ref[pl.ds(q0, tq), :]
            qhs = (qb[:, 0:128], qb[:, 128:256])
            dohs = (jnp.where(lo, dob, jnp.zeros_like(dob)), jnp.where(lo, jnp.zeros_like(dob), dob))
            return qhs, dohs

        def scores(i):
            qhs, dohs = operands(i)
            return tuple(_dot_nt(ks[h], qhs[h]) for h in range(2)) + tuple(_dot_nt(vb, dohs[h]) for h in range(2))

        def block(i, sc, diagonal):
            nxt = scores(jnp.minimum(i + 1, nq - 1))
            qhs, dohs = operands(i)
            lsev = lse_ref[0, i]
            dlv = dl_ref[0, i]
            pts, dsts = [], []
            for h in range(2):
                pt = jnp.exp(sc[h] - lsev[h:h + 1, :])
                if diagonal:
                    pt = jnp.where(_row((tq, tq)) <= _lane((tq, tq)), pt, 0.0)
                dsts.append((pt * (sc[2 + h] - dlv[h:h + 1, :])).astype(BF16))
                pts.append(pt.astype(BF16))
            dvacc_ref[...] += _dot(jnp.concatenate(pts, axis=1), jnp.concatenate(dohs, axis=0))
            for h in range(2):
                dk_ref[:, 128 * h:128 * h + 128] += _dot(dsts[h], qhs[h])
                dq_ref[h, i] += _dot(kts[h], dsts[h])
            return nxt

        sc = block(j, scores(j), True)
        lax.fori_loop(j + 1, nq, lambda i, c: block(i, c, False), sc)
        dv_ref[...] = dvacc_ref[...].astype(BF16)
        if ride:
            @pl.when((pl.program_id(0) == npair - 1) & (j == nq - 1))
            def _():
                ride.wait(ride_srcs, ride_dsts, ride_sems)

    full = lambda w: pl.BlockSpec((T, w), lambda p, j: (0, p))
    stat = pl.BlockSpec((1, nq, 8, tq), lambda p, j: (p, 0, 0, 0))
    extra = ride or _ChipExchange("gather", ())
    return pl.pallas_call(
        body, name=f"fox_bwd_{tag}", grid=(npair, nq),
        in_specs=[full(256), pl.BlockSpec((tq, 256), lambda p, j: (j, p)),
                  pl.BlockSpec((tq, 128), lambda p, j: (j, COL_CV // 128 + p)), full(128), stat, stat] + extra.in_specs,
        out_specs=[pl.BlockSpec((2, nq, 128, tq), lambda p, j: (p, 0, 0, 0)), pl.BlockSpec((tq, 256), lambda p, j: (j, p)),
                   pl.BlockSpec((tq, 128), lambda p, j: (j, p))] + extra.out_specs,
        out_shape=[SDS((C_HEADS, nq, 128, tq), F32), SDS((T, C_HEADS * 128), F32), SDS((T, C_WIDTH), BF16)]
        + extra.out_shape,
        scratch_shapes=[pltpu.VMEM((tq, 128), F32)] + (extra.scratch if ride else []),
        compiler_params=pltpu.CompilerParams(dimension_semantics=("arbitrary", "arbitrary"), vmem_limit_bytes=VMEM_LIMIT,
                                             has_side_effects=bool(ride)),
    )(qt, kt, proj, do, lse, delta, *extra.sources)


def _fox_bwd_post(dqt, dkt, proj, bf, tag):
    T = proj.shape[0]
    tq = _tile(T, FOX_TILE)
    n = T // tq

    def body(dq_ref, dk_ref, fl_ref, bf_ref, oq_ref, ok_ref, ofl_ref, dbf_ref, carry_ref):
        @pl.when(pl.program_id(0) == 0)
        def _():
            carry_ref[...] = jnp.zeros_like(carry_ref)
            dbf_ref[...] = jnp.zeros_like(dbf_ref)

        lane = _lane((tq, 128))
        lo = lane < 64
        dqs = [dq_ref[h, 0].T for h in range(C_HEADS)]
        dc = jnp.zeros((tq, 128), F32)
        for h in range(C_HEADS):
            dc = dc + jnp.where(lane == h, dqs[h][:, 64:65] - dk_ref[:, 128 * h + 67:128 * h + 68], 0.0)
        utri = jnp.where(_lane((tq, tq)) >= _row((tq, tq)), 1.0, 0.0).astype(BF16)
        dlf = _dot3_left(utri, dc) + carry_ref[...]
        carry_ref[...] = dlf[0:1, :]
        dfl = jnp.where(lane < C_HEADS, dlf * _sigmoid(-(fl_ref[...] + bf_ref[...])), 0.0)
        ofl_ref[...] = dfl.astype(BF16)
        dbf_ref[...] += jnp.sum(dfl, axis=0, keepdims=True)
        for p in range(C_HEADS // 2):
            a, b = 128 * (2 * p), 128 * (2 * p + 1)
            oq_ref[:, 128 * p:128 * p + 128] = (
                jnp.where(lo, dqs[2 * p], pltpu.roll(dqs[2 * p + 1], 64, axis=1)) * Q_SCALE).astype(BF16)
            ok_ref[:, 128 * p:128 * p + 128] = jnp.where(
                lo, dk_ref[:, a:a + 128], pltpu.roll(dk_ref[:, b:b + 128], 64, axis=1)).astype(BF16)

    rev = lambda w: pl.BlockSpec((tq, w), lambda i: (n - 1 - i, 0))
    return pl.pallas_call(
        body, name=f"fox_bwd_post_{tag}", grid=(n,),
        in_specs=[pl.BlockSpec((C_HEADS, 1, 128, tq), lambda i: (0, n - 1 - i, 0, 0)), rev(C_HEADS * 128),
                  pl.BlockSpec((tq, 128), lambda i: (n - 1 - i, COL_CF // 128)), pl.BlockSpec((1, 128), lambda i: (0, 0))],
        out_specs=[rev(C_WIDTH), rev(C_WIDTH), rev(128), pl.BlockSpec((1, 128), lambda i: (0, 0))],
        out_shape=[SDS((T, C_WIDTH), BF16), SDS((T, C_WIDTH), BF16), SDS((T, 128), BF16), SDS((1, 128), F32)],
        scratch_shapes=[pltpu.VMEM((1, 128), F32)], compiler_params=_params("arbitrary"),
    )(dqt, dkt, proj, bf)


def _adamw_math(w, g, m, v):
    m = ADAM_B1 * m + (1.0 - ADAM_B1) * g
    v = ADAM_B2 * v + (1.0 - ADAM_B2) * (g * g)
    delta = -ADAM_LR * ((m / ADAM_C1) / (jnp.sqrt(v / ADAM_C2) + ADAM_EPS) + ADAM_WD * w)
    return delta, m, v


def _adamw_pair(w, m, v, ga, gb, name):
    n0 = w.shape[0]
    most = max(1, ADAMW_BLOCK_BYTES // (4 * math.prod(w.shape[1:])))
    t0 = max(t for t in range(1, min(n0, most) + 1) if n0 % t == 0)

    def body(w_ref, m_ref, v_ref, ga_ref, gb_ref, g_ref, d_ref, nm_ref, nv_ref):
        g = ga_ref[...] + gb_ref[...]
        g_ref[...] = g
        d_ref[...], nm_ref[...], nv_ref[...] = _adamw_math(w_ref[...], g, m_ref[...], v_ref[...])

    blk = pl.BlockSpec((t0,) + w.shape[1:], lambda i: (i, 0, 0))
    return pl.pallas_call(
        body, name=name, grid=(n0 // t0,), in_specs=[blk] * 5, out_specs=[blk] * 4,
        out_shape=[SDS(w.shape, F32)] * 4, compiler_params=_params("parallel"),
    )(w, m, v, ga, gb)


def _adamw_small(w, m, v, gall):
    R = w.shape[0]

    def body(w_ref, m_ref, v_ref, g_ref, go_ref, d_ref, nm_ref, nv_ref):
        g = g_ref[0]
        for k in range(1, N_DEV):
            g = g + g_ref[k]
        go_ref[...] = g
        d_ref[...], nm_ref[...], nv_ref[...] = _adamw_math(w_ref[...], g, m_ref[...], v_ref[...])

    return pl.pallas_call(body, name="adamw_small", out_shape=[SDS((R, 128), F32)] * 4,
                          compiler_params=pltpu.CompilerParams(vmem_limit_bytes=VMEM_LIMIT))(w, m, v, gall)


def _sum_chips(layers, name, layer_major):
    _, R, C = layers[0].shape
    L = len(layers)
    tc = _tile(C, 256)

    def body(*refs):
        o_ref = refs[-1]
        for l, p_ref in enumerate(refs[:-1]):
            p = [p_ref[k].astype(F32) for k in range(N_CHIPS)]
            s = ((p[0] + p[1]) + p[2]) + p[3]
            if layer_major:
                o_ref[l] = s
            else:
                o_ref[:, l, :] = s

    out = (L, R, C) if layer_major else (R, L, C)
    out_blk = (L, R, tc) if layer_major else (R, L, tc)
    return pl.pallas_call(
        body, name=name, grid=(C // tc,),
        in_specs=[pl.BlockSpec((N_CHIPS, R, tc), lambda i: (0, 0, i))] * L,
        out_specs=pl.BlockSpec(out_blk, lambda i: (0, 0, i)), out_shape=SDS(out, F32),
        compiler_params=_params("parallel"),
    )(*layers)


ANY = pl.BlockSpec(memory_space=pl.ANY)


def _mesh_pos():
    return lax.axis_index("x"), lax.axis_index("y"), lax.axis_index("c")


def _other_chips(x, y):
    return [(1 - x, y), (x, 1 - y), (1 - x, 1 - y)]


class _ChipExchange:
    def __init__(self, mode, sources):
        assert mode in ("gather", "scatter")
        self.mode, self.sources = mode, tuple(sources)
        self.n = len(self.sources)
        self.in_specs = [ANY] * self.n
        self.out_specs = [ANY] * self.n
        self.out_shape = [SDS(((N_CHIPS,) + s.shape) if mode == "gather" else s.shape, s.dtype) for s in self.sources]
        self.scratch = [pltpu.SemaphoreType.DMA((3 * self.n,)), pltpu.SemaphoreType.DMA((3 * self.n,)),
                        pltpu.SemaphoreType.DMA((self.n,))]

    def _copies(self, srcs, dsts, send_sems, recv_sems, local_sems):
        x, y, c = _mesh_pos()
        me = 2 * x + y
        view = (lambda r, chip: r) if self.mode == "gather" else (lambda r, chip: r.at[chip])
        local = [pltpu.make_async_copy(view(s, me), d.at[me], local_sems.at[a]) for a, (s, d) in enumerate(zip(srcs, dsts))]
        sends, recvs = [], []
        for j, (px, py) in enumerate(_other_chips(x, y)):
            peer = 2 * px + py
            for a, (s, d) in enumerate(zip(srcs, dsts)):
                sems = dict(send_sem=send_sems.at[self.n * j + a], recv_sem=recv_sems.at[self.n * j + a],
                            device_id=(px, py, c), device_id_type=MESH_ID)
                sends.append(pltpu.make_async_remote_copy(src_ref=view(s, peer), dst_ref=d.at[me], **sems))
                recvs.append(pltpu.make_async_remote_copy(src_ref=view(s, me), dst_ref=d.at[peer], **sems))
        return local, sends, recvs

    def start(self, srcs, dsts, sems):
        local, sends, _ = self._copies(srcs, dsts, *sems)
        for cp in local + sends:
            cp.start()

    def wait(self, srcs, dsts, sems):
        local, sends, recvs = self._copies(srcs, dsts, *sems)
        for cp in recvs:
            cp.wait_recv()
        for cp in sends:
            cp.wait_send()
        for cp in local:
            cp.wait()


def _gather_weights(win, wout, tag):
    ex = _ChipExchange("gather", (win, wout))

    def body(win_ref, wout_ref, gin_ref, gout_ref, *sems):
        ex.start((win_ref, wout_ref), (gin_ref, gout_ref), sems)
        ex.wait((win_ref, wout_ref), (gin_ref, gout_ref), sems)

    return pl.pallas_call(
        body, name=f"gather_weights_{tag}", in_specs=ex.in_specs, out_specs=ex.out_specs, out_shape=ex.out_shape,
        scratch_shapes=ex.scratch, compiler_params=pltpu.CompilerParams(has_side_effects=True),
    )(win, wout)


def _exchange_grads(gin, gout, small):
    ex = _ChipExchange("scatter", (gin, gout))

    def body(gin_ref, gout_ref, small_ref, rin_ref, rout_ref, rsmall_ref, send_sems, recv_sems, local_sems,
             small_send, small_recv, small_local):
        x, y, c = _mesh_pos()
        me_dev = 4 * x + 2 * y + c
        sems = (send_sems, recv_sems, local_sems)
        ex.start((gin_ref, gout_ref), (rin_ref, rout_ref), sems)
        own = pltpu.make_async_copy(small_ref, rsmall_ref.at[me_dev], small_local)
        own.start()
        peers = [(k, ((1 - x) if k & 4 else x, (1 - y) if k & 2 else y, (1 - c) if k & 1 else c)) for k in range(1, N_DEV)]
        sends = []
        for k, peer in peers:
            cp = pltpu.make_async_remote_copy(src_ref=small_ref, dst_ref=rsmall_ref.at[me_dev], send_sem=small_send.at[k - 1],
                                              recv_sem=small_recv.at[k - 1], device_id=peer, device_id_type=MESH_ID)
            cp.start()
            sends.append(cp)
        ex.wait((gin_ref, gout_ref), (rin_ref, rout_ref), sems)
        for k, (px, py, pc) in peers:
            pltpu.make_async_remote_copy(src_ref=small_ref, dst_ref=rsmall_ref.at[4 * px + 2 * py + pc],
                                         send_sem=small_send.at[k - 1], recv_sem=small_recv.at[k - 1],
                                         device_id=(px, py, pc), device_id_type=MESH_ID).wait_recv()
        for cp in sends:
            cp.wait_send()
        own.wait()

    return pl.pallas_call(
        body, name="exchange_grads", in_specs=ex.in_specs + [ANY], out_specs=ex.out_specs + [ANY],
        out_shape=ex.out_shape + [SDS((N_DEV,) + small.shape, F32)],
        scratch_shapes=ex.scratch + [pltpu.SemaphoreType.DMA((N_DEV - 1,)), pltpu.SemaphoreType.DMA((N_DEV - 1,)),
                                     pltpu.SemaphoreType.DMA],
        compiler_params=pltpu.CompilerParams(has_side_effects=True),
    )(gin, gout, small)


def _swap_cores(pin, pout):
    def body(pin_ref, pout_ref, oin_ref, oout_ref, send_sems, recv_sems):
        x, y, c = _mesh_pos()
        cps = [pltpu.make_async_remote_copy(src_ref=src, dst_ref=dst, send_sem=send_sems.at[a], recv_sem=recv_sems.at[a],
                                            device_id=(x, y, 1 - c), device_id_type=MESH_ID)
               for a, (src, dst) in enumerate(((pin_ref, oin_ref), (pout_ref, oout_ref)))]
        for cp in cps:
            cp.start()
        for cp in cps:
            cp.wait()

    return pl.pallas_call(
        body, name="swap_cores", in_specs=[ANY, ANY], out_specs=[ANY, ANY],
        out_shape=[SDS(pin.shape, F32), SDS(pout.shape, F32)],
        scratch_shapes=[pltpu.SemaphoreType.DMA((2,)), pltpu.SemaphoreType.DMA((2,))],
        compiler_params=pltpu.CompilerParams(has_side_effects=True),
    )(pin, pout)


def _pack_small(parts):
    flat = [jnp.pad(p.reshape(-1), (0, (-p.size) % 128)) for p in parts]
    v = jnp.concatenate(flat)
    return jnp.pad(v, (0, (-v.size) % 1024)).reshape(-1, 128)


def _unpack_small(packed):
    flat = packed.reshape(-1)
    out, off = [], 0
    for _, shape in SMALL_PARAMS:
        size = math.prod(shape)
        out.append(flat[off:off + size].reshape(shape))
        off += size + (-size) % 128
    return out


def _layer_consts(l, gmlp_ln_g, gmlp_ln_b, gmlp_w_s, gmlp_b_s, hgrn_onorm_g, fox_b_f):
    causal = jnp.tril(jnp.ones((CHUNK, CHUNK), bool))
    wm = jnp.where(causal[None], gmlp_w_s[l], 0.0)
    return dict(
        lng=gmlp_ln_g[l].reshape(1, A_WIDTH), lnb=gmlp_ln_b[l].reshape(1, A_WIDTH),
        wm=wm.astype(BF16), wmt=jnp.swapaxes(wm, 1, 2).astype(BF16),
        bst=jnp.pad(gmlp_b_s[l].T, ((0, 0), (0, 128 - A_GROUPS))),
        onorm=jnp.tile(hgrn_onorm_g[l], 4).reshape(1, B_WIDTH),
        bf=jnp.pad(fox_b_f[l], (0, 128 - C_HEADS)).reshape(1, 128),
    )


def kernel(x, norm_g, w_in, w_out, gmlp_ln_g, gmlp_ln_b, gmlp_w_s, gmlp_b_s, hgrn_lb, hgrn_onorm_g, fox_b_f, final_norm_g, loss_target, m_norm_g, m_w_in, m_w_out, m_gmlp_ln_g, m_gmlp_ln_b, m_gmlp_w_s, m_gmlp_b_s, m_hgrn_lb, m_hgrn_onorm_g, m_fox_b_f, m_final_norm_g, v_norm_g, v_w_in, v_w_out, v_gmlp_ln_g, v_gmlp_ln_b, v_gmlp_w_s, v_gmlp_b_s, v_hgrn_lb, v_hgrn_onorm_g, v_fox_b_f, v_final_norm_g):
    T = x.shape[1]
    shard_in = w_in.shape[2]
    shard_out = w_out.shape[1]
    xs = x.reshape(T, D_MODEL)
    tgt = loss_target.reshape(T, D_MODEL)

    w_in_b, w_out_b = w_in.astype(BF16), w_out.astype(BF16)

    def full_weights(gathered_in, gathered_out):
        wi = jnp.concatenate([gathered_in[k] for k in range(N_CHIPS)], axis=-1)
        return jnp.pad(wi, ((0, 0), (0, D_IN_PAD - D_IN))), gathered_out.reshape(N_CHIPS * shard_out, D_MODEL)

    lb_all = _lb_fwd(hgrn_lb)
    consts = [_layer_consts(l, gmlp_ln_g, gmlp_ln_b, gmlp_w_s, gmlp_b_s, hgrn_onorm_g, fox_b_f) for l in range(DEPTH)]

    saved = []
    xl = xs
    weights = full_weights(*_gather_weights(w_in_b[0], w_out_b[0], "l0"))
    for l in range(DEPTH):
        cs = consts[l]
        tag = f"l{l}"
        w_in_l, w_out_l = weights
        h, proj = _inproj(xl, norm_g[l].reshape(1, D_MODEL), w_in_l, tag)
        ya = _gmlp_fwd(proj, cs["lng"], cs["lnb"], cs["wm"], cs["bst"], tag)
        yb, ob, s0 = _hgrn_fwd(proj, lb_all[l].reshape(1, B_WIDTH), cs["onorm"], tag)
        qt, kt, vt = _fox_prep(proj, cs["bf"], tag)
        ride = _ChipExchange("gather", (w_in_b[l + 1], w_out_b[l + 1])) if l + 1 < DEPTH else None
        oc, lse, yc, *gathered = _fox_fwd(qt, kt, vt, proj, tag, ride)
        saved.append(dict(x=xl, h=h, proj=proj, ya=ya, yb=yb, yc=yc, ob=ob, s0=s0, qt=qt, kt=kt, oc=oc, lse=lse,
                          w_in=w_in_l, w_out=w_out_l))
        xl = _outproj(xl, ya, yb, yc, w_out_l, tag)
        if ride:
            weights = full_weights(*gathered)

    dx, loss_part, d_final = _loss_head(xl, final_norm_g.reshape(1, D_MODEL), tgt)
    loss = lax.psum(loss_part[0, 0], ("x", "y", "c"))

    g_small = {}
    dlb_rows, rin, rout = [None] * DEPTH, [None] * DEPTH, [None] * DEPTH
    slabs = None
    for l in reversed(range(DEPTH)):
        cs, sv = consts[l], saved[l]
        tag = f"l{l}"
        proj = sv["proj"]
        dy, dw_out = _outproj_bwd(dx, sv["ya"], sv["yb"], sv["yc"], sv["w_out"], tag)
        da, dwm, dbst, dlng, dlnb = _gmlp_bwd(proj, dy, cs["lng"], cs["lnb"], cs["wm"], cs["wmt"], cs["bst"], tag)
        db, dlb_rows[l], donorm = _hgrn_bwd(proj, dy, sv["ob"], sv["s0"], lb_all[l].reshape(1, B_WIDTH), cs["onorm"], tag)
        do, delta, dzc = _fox_bwd_prep(proj, dy, sv["oc"], tag)
        ride = _ChipExchange("scatter", slabs) if slabs else None
        dqt, dkt, dvc, *received = _fox_bwd(sv["qt"], sv["kt"], proj, do, sv["lse"], delta, tag, ride)
        if ride:
            rin[l + 1], rout[l + 1] = received
        dqc, dkc, dflc, dbf = _fox_bwd_post(dqt, dkt, proj, cs["bf"], tag)
        dproj = jnp.concatenate([da, db, dqc, dkc, dvc, dzc, dflc, jnp.zeros((T, 128), BF16)], axis=1)
        dw_in = _dw_in(sv["h"], dproj, tag)
        dx, dng = _dx_in(sv["x"], norm_g[l].reshape(1, D_MODEL), dx, dproj, sv["w_in"], tag)
        slabs = (jnp.stack([dw_in[k * shard_in:(k + 1) * shard_in] for k in range(N_CHIPS)]).astype(BF16),
                 dw_out.reshape(N_CHIPS, shard_out, D_MODEL).astype(BF16))
        g_small[l] = dict(norm_g=dng.reshape(D_MODEL), ln_g=dlng.reshape(4, 64), ln_b=dlnb.reshape(4, 64), w_s=dwm,
                          b_s=dbst[:, :A_GROUPS].T, onorm=donorm[0, :64], bf=dbf[0, :C_HEADS])
    grad_x = dx.reshape(x.shape)
    d_hgrn_lb = _lb_bwd(hgrn_lb, jnp.concatenate(dlb_rows, axis=0))

    stack = lambda key: jnp.stack([g_small[l][key] for l in range(DEPTH)])
    small_g = _pack_small([stack("norm_g"), stack("ln_g"), stack("ln_b"), stack("w_s"), stack("b_s"), d_hgrn_lb,
                           stack("onorm"), stack("bf"), d_final.reshape(D_MODEL)])

    rin[0], rout[0], rsmall = _exchange_grads(*slabs, small_g)
    pin, pout = _sum_chips(rin, "sum_chips_w_in", False), _sum_chips(rout, "sum_chips_w_out", True)
    oin, oout = _swap_cores(pin, pout)
    to_view = lambda a: jnp.transpose(a, (2, 0, 1))
    g_w_in, d_w_in, nm_w_in, nv_w_in = [
        jnp.transpose(o, (1, 2, 0))
        for o in _adamw_pair(to_view(w_in), to_view(m_w_in), to_view(v_w_in), pin, oin, "adamw_w_in")]
    g_w_out, d_w_out, nm_w_out, nv_w_out = _adamw_pair(w_out, m_w_out, v_w_out, pout, oout, "adamw_w_out")

    small_w = [norm_g, gmlp_ln_g, gmlp_ln_b, gmlp_w_s, gmlp_b_s, hgrn_lb, hgrn_onorm_g, fox_b_f, final_norm_g]
    small_m = [m_norm_g, m_gmlp_ln_g, m_gmlp_ln_b, m_gmlp_w_s, m_gmlp_b_s, m_hgrn_lb, m_hgrn_onorm_g, m_fox_b_f, m_final_norm_g]
    small_v = [v_norm_g, v_gmlp_ln_g, v_gmlp_ln_b, v_gmlp_w_s, v_gmlp_b_s, v_hgrn_lb, v_hgrn_onorm_g, v_fox_b_f, v_final_norm_g]
    outs = _adamw_small(_pack_small(small_w), _pack_small(small_m), _pack_small(small_v), rsmall)
    sg, sd, sm, sv_ = [_unpack_small(o) for o in outs]

    def order(big_in, big_out, small):
        return [small[0], big_in, big_out] + small[1:]

    return (loss, grad_x, *order(g_w_in, g_w_out, sg), *order(d_w_in, d_w_out, sd), *order(nm_w_in, nm_w_out, sm),
            *order(nv_w_in, nv_w_out, sv_))
```

```python
import functools
import math

import jax
import jax.numpy as jnp
from jax import lax
from jax.experimental import pallas as pl
from jax.experimental.pallas import tpu as pltpu

F32 = jnp.float32
BF16 = jnp.bfloat16
SDS = jax.ShapeDtypeStruct
MESH_ID = pl.DeviceIdType.MESH

D_MODEL = 1024
DEPTH = 2
A_WIDTH = 256
A_GROUPS = 4
B_WIDTH = 256
C_WIDTH = 512
C_HEADS = 8
D_IN = 3848
D_IN_PAD = 4096
CHUNK = 128
SUB = 16
SUB_SHIFT = 4
NORM_EPS = 1e-6
F_FLOOR = 1e-30
COL_AU, COL_AV, COL_AZ = 0, 256, 512
COL_BQ, COL_BF, COL_BI, COL_BZ = 768, 1024, 1280, 1536
COL_CQ, COL_CK, COL_CV, COL_CZ, COL_CF = 1792, 2304, 2816, 3328, 3840
HEAD_LANES = 128
Q_SCALE = 0.125
ADAM_LR, ADAM_B1, ADAM_B2, ADAM_EPS, ADAM_WD, ADAM_STEP = 0.001, 0.9, 0.999, 1e-08, 0.01, 10
ADAM_C1 = 1.0 - ADAM_B1 ** ADAM_STEP
ADAM_C2 = 1.0 - ADAM_B2 ** ADAM_STEP
VMEM_LIMIT = 56 * 1024 * 1024
ADAMW_BLOCK_BYTES = 1 << 20
N_CHIPS = 4
N_DEV = 8

SMALL_PARAMS = (
    ("norm_g", (DEPTH, D_MODEL)), ("gmlp_ln_g", (DEPTH, 4, 64)), ("gmlp_ln_b", (DEPTH, 4, 64)),
    ("gmlp_w_s", (DEPTH, 4, 128, 128)), ("gmlp_b_s", (DEPTH, 4, 128)), ("hgrn_lb", (DEPTH, 256)),
    ("hgrn_onorm_g", (DEPTH, 64)), ("fox_b_f", (DEPTH, 8)), ("final_norm_g", (D_MODEL,)),
)


def _tile(n, pref):
    t = min(n, pref)
    assert n % t == 0, (n, pref)
    return t


def _params(*sem):
    return pltpu.CompilerParams(dimension_semantics=sem, vmem_limit_bytes=VMEM_LIMIT)


def _dot(a, b):
    return jnp.dot(a, b, preferred_element_type=F32)


def _dot_nt(a, b):
    return lax.dot_general(a, b, (((1,), (1,)), ((), ())), preferred_element_type=F32)


def _dot_tn(a, b):
    return lax.dot_general(a, b, (((0,), (0,)), ((), ())), preferred_element_type=F32)


def _split3(x):
    hi = x.astype(BF16)
    r = x - hi.astype(F32)
    mid = r.astype(BF16)
    lo = (r - mid.astype(F32)).astype(BF16)
    return hi, mid, lo


def _dot3_left(c, x):
    hi, mid, lo = _split3(x)
    return _dot(c, hi) + _dot(c, mid) + _dot(c, lo)


def _sigmoid(x):
    return jax.nn.sigmoid(x)


def _silu_and_grad(x):
    s = _sigmoid(x)
    return x * s, s * (1.0 + x * (1.0 - s))


_GELU_C = math.sqrt(2.0 / math.pi)


def _gelu_and_grad(x):
    inner = _GELU_C * (x + 0.044715 * x * x * x)
    t = jnp.tanh(inner)
    y = 0.5 * x * (1.0 + t)
    dy = 0.5 * (1.0 + t) + 0.5 * x * (1.0 - t * t) * _GELU_C * (1.0 + 3.0 * 0.044715 * x * x)
    return y, dy


def _lane(shape):
    return lax.broadcasted_iota(jnp.int32, shape, 1)


def _row(shape):
    return lax.broadcasted_iota(jnp.int32, shape, 0)


def _gsum64(x):
    lo = _lane(x.shape) < 64
    s0 = jnp.sum(jnp.where(lo, x, 0.0), axis=-1, keepdims=True)
    s1 = jnp.sum(jnp.where(lo, 0.0, x), axis=-1, keepdims=True)
    return jnp.where(lo, s0, s1)


def _colreduce(x, op):
    parts = [x[r:r + 8, :] for r in range(0, x.shape[0], 8)]
    while len(parts) > 1:
        pairs = [op(parts[k], parts[k + 1]) for k in range(0, len(parts) - 1, 2)]
        parts = pairs + ([parts[-1]] if len(parts) % 2 else [])
    red = jnp.max if op is jnp.maximum else jnp.sum
    return red(parts[0], axis=0, keepdims=True)


def _block_diag64(dtype=BF16):
    r, c = _row((128, 128)), _lane((128, 128))
    return jnp.where((r >> 6) == (c >> 6), 1.0, 0.0).astype(dtype)


def _inproj(x, g, w, tag):
    T, D = x.shape
    DP = w.shape[1]
    tm, tn = _tile(T, 512), _tile(DP, 1024)

    def body(x_ref, g_ref, w_ref, h_ref, p_ref):
        @pl.when(pl.program_id(1) == 0)
        def _():
            xv = x_ref[...]
            r = lax.rsqrt(jnp.mean(xv * xv, axis=-1, keepdims=True) + NORM_EPS)
            h_ref[...] = (xv * r * g_ref[...]).astype(BF16)

        p_ref[...] = _dot(h_ref[...], w_ref[...])

    return pl.pallas_call(
        body, name=f"inproj_{tag}", grid=(T // tm, DP // tn),
        in_specs=[pl.BlockSpec((tm, D), lambda i, j: (i, 0)), pl.BlockSpec((1, D), lambda i, j: (0, 0)),
                  pl.BlockSpec((D, tn), lambda i, j: (0, j))],
        out_specs=[pl.BlockSpec((tm, D), lambda i, j: (i, 0)), pl.BlockSpec((tm, tn), lambda i, j: (i, j))],
        out_shape=[SDS((T, D), BF16), SDS((T, DP), F32)],
        compiler_params=_params("parallel", "arbitrary"),
    )(x, g, w)


def _outproj(x, ya, yb, yc, wo, tag):
    T, D = x.shape
    tm = _tile(T, 512)

    def body(x_ref, ya_ref, yb_ref, yc_ref, wo_ref, o_ref):
        acc = x_ref[...] + _dot(ya_ref[...], wo_ref[0:A_WIDTH, :])
        acc = acc + _dot(yb_ref[...], wo_ref[A_WIDTH:A_WIDTH + B_WIDTH, :])
        o_ref[...] = acc + _dot(yc_ref[...], wo_ref[A_WIDTH + B_WIDTH:, :])

    row = lambda w: pl.BlockSpec((tm, w), lambda i: (i, 0))
    return pl.pallas_call(
        body, name=f"outproj_{tag}", grid=(T // tm,),
        in_specs=[row(D), row(A_WIDTH), row(B_WIDTH), row(C_WIDTH), pl.BlockSpec(wo.shape, lambda i: (0, 0))],
        out_specs=row(D), out_shape=SDS((T, D), F32), compiler_params=_params("parallel"),
    )(x, ya, yb, yc, wo)


def _outproj_bwd(dx, ya, yb, yc, wo, tag):
    T, D = dx.shape
    DM = wo.shape[0]
    tm = _tile(T, 512)

    def body(dx_ref, ya_ref, yb_ref, yc_ref, wo_ref, dy_ref, dwo_ref):
        @pl.when(pl.program_id(0) == 0)
        def _():
            dwo_ref[...] = jnp.zeros_like(dwo_ref)

        dxb = dx_ref[...].astype(BF16)
        dy_ref[...] = _dot_nt(dxb, wo_ref[...])
        dwo_ref[0:A_WIDTH, :] += _dot_tn(ya_ref[...], dxb)
        dwo_ref[A_WIDTH:A_WIDTH + B_WIDTH, :] += _dot_tn(yb_ref[...], dxb)
        dwo_ref[A_WIDTH + B_WIDTH:, :] += _dot_tn(yc_ref[...], dxb)

    row = lambda w: pl.BlockSpec((tm, w), lambda i: (i, 0))
    return pl.pallas_call(
        body, name=f"outproj_bwd_{tag}", grid=(T // tm,),
        in_specs=[row(D), row(A_WIDTH), row(B_WIDTH), row(C_WIDTH), pl.BlockSpec(wo.shape, lambda i: (0, 0))],
        out_specs=[row(DM), pl.BlockSpec((DM, D), lambda i: (0, 0))],
        out_shape=[SDS((T, DM), F32), SDS((DM, D), F32)], compiler_params=_params("arbitrary"),
    )(dx, ya, yb, yc, wo)


def _dw_in(h, dproj, tag):
    T, D = h.shape
    DP = dproj.shape[1]
    tm, tn = _tile(T, 512), _tile(DP, 1024)

    def body(h_ref, dp_ref, dw_ref):
        @pl.when(pl.program_id(1) == 0)
        def _():
            dw_ref[...] = jnp.zeros_like(dw_ref)

        dw_ref[...] += _dot_tn(dp_ref[...], h_ref[...])

    return pl.pallas_call(
        body, name=f"dw_in_{tag}", grid=(DP // tn, T // tm),
        in_specs=[pl.BlockSpec((tm, D), lambda j, i: (i, 0)), pl.BlockSpec((tm, tn), lambda j, i: (i, j))],
        out_specs=pl.BlockSpec((tn, D), lambda j, i: (j, 0)), out_shape=SDS((DP, D), F32),
        compiler_params=_params("parallel", "arbitrary"),
    )(h, dproj)


def _dx_in(x, g, dres, dproj, w, tag):
    T, D = x.shape
    DP = w.shape[1]
    tm, tk = _tile(T, 512), _tile(DP, 1024)
    nk = DP // tk

    def body(x_ref, g_ref, dres_ref, dp_ref, w_ref, dx_ref, dg_ref, acc_ref):
        i, k = pl.program_id(0), pl.program_id(1)

        @pl.when((i == 0) & (k == 0))
        def _():
            dg_ref[...] = jnp.zeros_like(dg_ref)

        @pl.when(k == 0)
        def _():
            acc_ref[...] = jnp.zeros_like(acc_ref)

        acc_ref[...] += _dot_nt(dp_ref[...], w_ref[...])

        @pl.when(k == nk - 1)
        def _():
            xv = x_ref[...]
            r = lax.rsqrt(jnp.mean(xv * xv, axis=-1, keepdims=True) + NORM_EPS)
            xh = xv * r
            dh = acc_ref[...]
            dg_ref[...] += jnp.sum(dh * xh, axis=0, keepdims=True)
            dxh = dh * g_ref[...]
            dx_ref[...] = dres_ref[...] + r * (dxh - xh * jnp.mean(dxh * xh, axis=-1, keepdims=True))

    return pl.pallas_call(
        body, name=f"dx_in_{tag}", grid=(T // tm, nk),
        in_specs=[pl.BlockSpec((tm, D), lambda i, k: (i, 0)), pl.BlockSpec((1, D), lambda i, k: (0, 0)),
                  pl.BlockSpec((tm, D), lambda i, k: (i, 0)), pl.BlockSpec((tm, tk), lambda i, k: (i, k)),
                  pl.BlockSpec((D, tk), lambda i, k: (0, k))],
        out_specs=[pl.BlockSpec((tm, D), lambda i, k: (i, 0)), pl.BlockSpec((1, D), lambda i, k: (0, 0))],
        out_shape=[SDS((T, D), F32), SDS((1, D), F32)],
        scratch_shapes=[pltpu.VMEM((tm, D), F32)], compiler_params=_params("arbitrary", "arbitrary"),
    )(x, g, dres, dproj, w)


def _loss_head(x, g, tgt):
    T, D = x.shape
    tm = _tile(T, 512)

    def body(x_ref, g_ref, t_ref, dx_ref, loss_ref, dg_ref):
        @pl.when(pl.program_id(0) == 0)
        def _():
            loss_ref[...] = jnp.zeros_like(loss_ref)
            dg_ref[...] = jnp.zeros_like(dg_ref)

        xv = x_ref[...]
        r = lax.rsqrt(jnp.mean(xv * xv, axis=-1, keepdims=True) + NORM_EPS)
        xh = xv * r
        gv = g_ref[...]
        err = xh * gv - t_ref[...]
        tok = jnp.mean(err * err, axis=-1, keepdims=True)
        loss_ref[...] += 0.5 * jnp.sum(tok, axis=0, keepdims=True)
        dy = err * (1.0 / D)
        dg_ref[...] += jnp.sum(dy * xh, axis=0, keepdims=True)
        dxh = dy * gv
        dx_ref[...] = r * (dxh - xh * jnp.mean(dxh * xh, axis=-1, keepdims=True))

    row = pl.BlockSpec((tm, D), lambda i: (i, 0))
    return pl.pallas_call(
        body, name="loss_head", grid=(T // tm,),
        in_specs=[row, pl.BlockSpec((1, D), lambda i: (0, 0)), row],
        out_specs=[row, pl.BlockSpec((1, 128), lambda i: (0, 0)), pl.BlockSpec((1, D), lambda i: (0, 0))],
        out_shape=[SDS((T, D), F32), SDS((1, 128), F32), SDS((1, D), F32)], compiler_params=_params("arbitrary"),
    )(x, g, tgt)


def _gmlp_core(u, v, lng, lnb, wm_ref, bst_ref, pair):
    ug, dug = _gelu_and_grad(u)
    vg, dvg = _gelu_and_grad(v)
    mu = _gsum64(vg) * (1.0 / 64)
    d = vg - mu
    var = _gsum64(d * d) * (1.0 / 64)
    rstd = lax.rsqrt(var + NORM_EPS)
    xh = d * rstd
    vn = xh * lng + lnb
    vnb = vn.astype(BF16)
    lo = _lane(u.shape) < 64
    g0, g1 = 2 * pair, 2 * pair + 1
    mixed = jnp.where(lo, _dot(wm_ref[g0], vnb) + bst_ref[:, g0:g0 + 1], _dot(wm_ref[g1], vnb) + bst_ref[:, g1:g1 + 1])
    return ug, dug, dvg, rstd, xh, vnb, mixed, lo


def _gmlp_fwd(proj, lng, lnb, wm, bst, tag):
    T = proj.shape[0]

    def body(u_ref, v_ref, z_ref, lng_ref, lnb_ref, wm_ref, bst_ref, y_ref):
        for pair in range(2):
            sl = slice(128 * pair, 128 * pair + 128)
            ug, _, _, _, _, _, mixed, _ = _gmlp_core(u_ref[:, sl], v_ref[:, sl], lng_ref[:, sl], lnb_ref[:, sl],
                                                     wm_ref, bst_ref, pair)
            sz, _ = _silu_and_grad(z_ref[:, sl])
            y_ref[:, sl] = (ug * mixed * sz).astype(BF16)

    col = lambda c: pl.BlockSpec((CHUNK, A_WIDTH), lambda i, c=c: (i, c // A_WIDTH))
    full = lambda a: pl.BlockSpec(a.shape, lambda i, n=a.ndim: (0,) * n)
    return pl.pallas_call(
        body, name=f"gmlp_fwd_{tag}", grid=(T // CHUNK,),
        in_specs=[col(COL_AU), col(COL_AV), col(COL_AZ), full(lng), full(lnb), full(wm), full(bst)],
        out_specs=pl.BlockSpec((CHUNK, A_WIDTH), lambda i: (i, 0)), out_shape=SDS((T, A_WIDTH), BF16),
        compiler_params=_params("parallel"),
    )(proj, proj, proj, lng, lnb, wm, bst)


def _gmlp_bwd(proj, dy, lng, lnb, wm, wmt, bst, tag):
    T = proj.shape[0]
    n = T // CHUNK

    def body(u_ref, v_ref, z_ref, dy_ref, lng_ref, lnb_ref, wm_ref, wmt_ref, bst_ref,
             da_ref, dwm_ref, dbst_ref, dlng_ref, dlnb_ref):
        @pl.when(pl.program_id(0) == 0)
        def _():
            dwm_ref[...] = jnp.zeros_like(dwm_ref)
            dbst_ref[...] = jnp.zeros_like(dbst_ref)
            dlng_ref[...] = jnp.zeros_like(dlng_ref)
            dlnb_ref[...] = jnp.zeros_like(dlnb_ref)

        lane = _lane((CHUNK, 128))
        dbst = dbst_ref[...]
        for pair in range(2):
            sl = slice(128 * pair, 128 * pair + 128)
            lng_p = lng_ref[:, sl]
            ug, dug, dvg, rstd, xh, vnb, mixed, lo = _gmlp_core(u_ref[:, sl], v_ref[:, sl], lng_p, lnb_ref[:, sl],
                                                                wm_ref, bst_ref, pair)
            sz, dsz = _silu_and_grad(z_ref[:, sl])
            dyv = dy_ref[:, sl]
            out = ug * mixed
            dz = dyv * out * dsz
            dout = dyv * sz
            du = dout * mixed * dug
            dmix = dout * ug
            g0, g1 = 2 * pair, 2 * pair + 1
            dm0 = jnp.where(lo, dmix, 0.0)
            dm1 = jnp.where(lo, 0.0, dmix)
            dbst = dbst + jnp.where(lane == g0, jnp.sum(dm0, axis=-1, keepdims=True), 0.0)
            dbst = dbst + jnp.where(lane == g1, jnp.sum(dm1, axis=-1, keepdims=True), 0.0)
            dwm_ref[g0] += _dot_nt(dm0.astype(BF16), vnb)
            dwm_ref[g1] += _dot_nt(dm1.astype(BF16), vnb)
            dmb = dmix.astype(BF16)
            dvn = jnp.where(lo, _dot(wmt_ref[g0], dmb), _dot(wmt_ref[g1], dmb))
            dlng_ref[:, sl] += jnp.sum(dvn * xh, axis=0, keepdims=True)
            dlnb_ref[:, sl] += jnp.sum(dvn, axis=0, keepdims=True)
            dxh = dvn * lng_p
            m1 = _gsum64(dxh) * (1.0 / 64)
            m2 = _gsum64(dxh * xh) * (1.0 / 64)
            dv = rstd * (dxh - m1 - xh * m2) * dvg
            da_ref[:, COL_AU + 128 * pair:COL_AU + 128 * pair + 128] = du.astype(BF16)
            da_ref[:, COL_AV + 128 * pair:COL_AV + 128 * pair + 128] = dv.astype(BF16)
            da_ref[:, COL_AZ + 128 * pair:COL_AZ + 128 * pair + 128] = dz.astype(BF16)
        dbst_ref[...] = dbst

        @pl.when(pl.program_id(0) == n - 1)
        def _():
            causal = _lane((CHUNK, CHUNK)) <= _row((CHUNK, CHUNK))
            for g in range(A_GROUPS):
                dwm_ref[g] = jnp.where(causal, dwm_ref[g], 0.0)

    col = lambda c: pl.BlockSpec((CHUNK, A_WIDTH), lambda i, c=c: (i, c // A_WIDTH))
    full = lambda a: pl.BlockSpec(a.shape, lambda i, n=a.ndim: (0,) * n)
    acc = lambda s: pl.BlockSpec(s, lambda i, n=len(s): (0,) * n)
    return pl.pallas_call(
        body, name=f"gmlp_bwd_{tag}", grid=(n,),
        in_specs=[col(COL_AU), col(COL_AV), col(COL_AZ), pl.BlockSpec((CHUNK, A_WIDTH), lambda i: (i, 0)),
                  full(lng), full(lnb), full(wm), full(wmt), full(bst)],
        out_specs=[pl.BlockSpec((CHUNK, 3 * A_WIDTH), lambda i: (i, 0)), acc((A_GROUPS, CHUNK, CHUNK)),
                   acc((CHUNK, 128)), acc((1, A_WIDTH)), acc((1, A_WIDTH))],
        out_shape=[SDS((T, 3 * A_WIDTH), BF16), SDS((A_GROUPS, CHUNK, CHUNK), F32), SDS((CHUNK, 128), F32),
                   SDS((1, A_WIDTH), F32), SDS((1, A_WIDTH), F32)],
        compiler_params=_params("arbitrary"),
    )(proj, proj, proj, dy, lng, lnb, wm, wmt, bst)


def _hgrn_consts():
    r, c = _row((CHUNK, CHUNK)), _lane((CHUNK, CHUNK))
    same = (r >> SUB_SHIFT) == (c >> SUB_SHIFT)
    lsub = jnp.where(same & (c <= r), 1.0, 0.0).astype(BF16)
    usub = jnp.where(same & (c >= r), 1.0, 0.0).astype(BF16)
    bsub = jnp.where(same, 1.0, 0.0).astype(BF16)
    return lsub, usub, bsub


def _hgrn_gates(qv, zf, lbp):
    sq, dsq = _silu_and_grad(qv)
    qt = sq * Q_SCALE
    sg = _sigmoid(zf)
    sgn = _sigmoid(-zf)
    f = lbp + (1.0 - lbp) * sg
    g = jnp.log(jnp.maximum(f, F_FLOOR))
    kf = (1.0 - lbp) * sgn
    return qt, dsq, sg, sgn, f, g, kf


def _hgrn_intra_fwd(qt, kf, b, v, mbd):
    rid = _row((SUB, 128))
    parts = []
    for s in range(SUB):
        e = jnp.exp(jnp.minimum(b - b[s:s + 1, :], 0.0))
        parts.append(jnp.where(rid >= s, qt * kf[s:s + 1, :] * e, 0.0))
    a = _dot(jnp.concatenate(parts, axis=0).astype(BF16), mbd)
    o = jnp.zeros((SUB, 128), F32)
    for s in range(SUB):
        o = o + a[SUB * s:SUB * s + SUB, :] * v[s:s + 1, :]
    return o


def _hgrn_intra_bwd(qt, kf, b, v, do, mbd, rsum):
    rid = _row((SUB, 128))
    ps, das, kes, es = [], [], [], []
    for s in range(SUB):
        e = jnp.where(rid >= s, jnp.exp(jnp.minimum(b - b[s:s + 1, :], 0.0)), 0.0)
        ke = kf[s:s + 1, :] * e
        es.append(e)
        kes.append(ke)
        ps.append(qt * ke)
        das.append(do * v[s:s + 1, :])
    a = _dot(jnp.concatenate(ps, axis=0).astype(BF16), mbd)
    da = _dot(jnp.concatenate(das, axis=0).astype(BF16), mbd)
    dqt = jnp.zeros((SUB, 128), F32)
    xs, ys = [], []
    for s in range(SUB):
        da_s = da[SUB * s:SUB * s + SUB, :]
        dqt = dqt + da_s * kes[s]
        xs.append(a[SUB * s:SUB * s + SUB, :] * do)
        ys.append(da_s * qt * es[s])
    xh = jnp.concatenate(xs, axis=0)
    yh = jnp.concatenate(ys, axis=0)
    xhi = xh.astype(BF16)
    yhi = yh.astype(BF16)
    dv = _dot(rsum, xhi) + _dot(rsum, (xh - xhi.astype(F32)).astype(BF16))
    dkf = _dot(rsum, yhi) + _dot(rsum, (yh - yhi.astype(F32)).astype(BF16))
    return dqt, dkf, dv


def _hgrn_norm_gate(o, z, onorm):
    ms = _gsum64(o * o) * (1.0 / 64)
    r = lax.rsqrt(ms + NORM_EPS)
    xh = o * r
    sz, dsz = _silu_and_grad(z)
    return xh, r, sz, dsz, xh * onorm


def _hgrn_fwd(proj, lb, onorm, tag):
    T = proj.shape[0]
    n = T // CHUNK
    nsub = CHUNK // SUB

    def body(q_ref, f_ref, i_ref, z_ref, lb_ref, on_ref, y_ref, o_ref, s0_ref, st_ref):
        @pl.when(pl.program_id(0) == 0)
        def _():
            st_ref[...] = jnp.zeros_like(st_ref)

        lsub, _, bsub = _hgrn_consts()
        mbd = _block_diag64()
        bdmask = mbd > 0
        rid = _row((CHUNK, 128))
        for pair in range(2):
            sl = slice(128 * pair, 128 * pair + 128)
            qt, _, _, _, _, g, kf = _hgrn_gates(q_ref[:, sl], f_ref[:, sl], lb_ref[:, sl])
            v = i_ref[:, sl]
            b = _dot3_left(lsub, g)
            bl = _dot3_left(bsub, g)
            qh = (qt * jnp.exp(b)).astype(BF16)
            kh = kf * jnp.exp(bl - b)
            dec = jnp.exp(bl)
            vtb = v.T.astype(BF16)
            st = st_ref[pair]
            s0_ref[0, pair] = st
            outs = []
            for sub in range(nsub):
                rs = slice(SUB * sub, SUB * sub + SUB)
                o_inter = _dot_nt(qh[rs], st.astype(BF16))
                outs.append(o_inter + _hgrn_intra_fwd(qt[rs], kf[rs], b[rs], v[rs], mbd))
                khm = jnp.where((rid >> SUB_SHIFT) == sub, kh, 0.0).astype(BF16)
                st = jnp.where(bdmask, st * dec[SUB * sub:SUB * sub + 1, :] + _dot(vtb, khm), 0.0)
            st_ref[pair] = st
            o = jnp.concatenate(outs, axis=0)
            o_ref[:, sl] = o
            _, _, sz, _, on = _hgrn_norm_gate(o, z_ref[:, sl], on_ref[:, sl])
            y_ref[:, sl] = (on * sz).astype(BF16)

    col = lambda c: pl.BlockSpec((CHUNK, B_WIDTH), lambda i, c=c: (i, c // B_WIDTH))
    full = lambda a: pl.BlockSpec(a.shape, lambda i, n=a.ndim: (0,) * n)
    return pl.pallas_call(
        body, name=f"hgrn_fwd_{tag}", grid=(n,),
        in_specs=[col(COL_BQ), col(COL_BF), col(COL_BI), col(COL_BZ), full(lb), full(onorm)],
        out_specs=[pl.BlockSpec((CHUNK, B_WIDTH), lambda i: (i, 0)), pl.BlockSpec((CHUNK, B_WIDTH), lambda i: (i, 0)),
                   pl.BlockSpec((1, 2, 128, 128), lambda i: (i, 0, 0, 0))],
        out_shape=[SDS((T, B_WIDTH), BF16), SDS((T, B_WIDTH), F32), SDS((n, 2, 128, 128), F32)],
        scratch_shapes=[pltpu.VMEM((2, 128, 128), F32)], compiler_params=_params("arbitrary"),
    )(proj, proj, proj, proj, lb, onorm)


def _hgrn_bwd(proj, dy, o_saved, s0, lb, onorm, tag):
    T = proj.shape[0]
    n = T // CHUNK
    nsub = CHUNK // SUB

    def body(q_ref, f_ref, i_ref, z_ref, dy_ref, o_ref, s0_ref, lb_ref, on_ref,
             db_ref, dlb_ref, don_ref, dst_ref, sts_ref):
        @pl.when(pl.program_id(0) == 0)
        def _():
            dst_ref[...] = jnp.zeros_like(dst_ref)
            dlb_ref[...] = jnp.zeros_like(dlb_ref)
            don_ref[...] = jnp.zeros_like(don_ref)

        lsub, usub, bsub = _hgrn_consts()
        mbd = _block_diag64()
        bdmask = mbd > 0
        rid = _row((CHUNK, 128))
        rsum = jnp.where((_lane((SUB, SUB * SUB)) >> SUB_SHIFT) == _row((SUB, SUB * SUB)), 1.0, 0.0).astype(BF16)
        for pair in range(2):
            sl = slice(128 * pair, 128 * pair + 128)
            lbp = lb_ref[:, sl]
            qv, zf = q_ref[:, sl], f_ref[:, sl]
            qt, dsq, sg, sgn, f, g, kf = _hgrn_gates(qv, zf, lbp)
            v = i_ref[:, sl]
            b = _dot3_left(lsub, g)
            bl = _dot3_left(bsub, g)
            eb = jnp.exp(b)
            ekb = jnp.exp(bl - b)
            qh = qt * eb
            kh = kf * ekb
            dec = jnp.exp(bl)
            vtb = v.T.astype(BF16)
            onp = on_ref[:, sl]
            ov = o_ref[:, sl]
            xh, r, sz, dsz, on = _hgrn_norm_gate(ov, z_ref[:, sl], onp)
            dyv = dy_ref[:, sl]
            dz = dyv * on * dsz
            don = dyv * sz
            cn = jnp.sum(don * xh, axis=0, keepdims=True)
            don_ref[...] += cn + pltpu.roll(cn, 64, axis=1)
            dxo = don * onp
            do = r * (dxo - xh * (_gsum64(dxo * xh) * (1.0 / 64)))
            dotb = do.T.astype(BF16)
            st = s0_ref[0, pair]
            for sub in range(nsub):
                sts_ref[sub] = st
                khm = jnp.where((rid >> SUB_SHIFT) == sub, kh, 0.0).astype(BF16)
                st = jnp.where(bdmask, st * dec[SUB * sub:SUB * sub + 1, :] + _dot(vtb, khm), 0.0)
            gst = dst_ref[pair]
            dqt_p, dkf_p, dv_p, dbl_p = [None] * nsub, [None] * nsub, [None] * nsub, [None] * nsub
            for sub in reversed(range(nsub)):
                rs = slice(SUB * sub, SUB * sub + SUB)
                st_in = sts_ref[sub]
                gb = gst.astype(BF16)
                dob = do[rs].astype(BF16)
                dqh = _dot(dob, st_in.astype(BF16))
                dkh = _dot(v[rs].astype(BF16), gb)
                dv_inter = _dot_nt(kh[rs].astype(BF16), gb)
                ddec = jnp.sum(gst * st_in, axis=0, keepdims=True)
                dec_row = dec[SUB * sub:SUB * sub + 1, :]
                qhm = jnp.where((rid >> SUB_SHIFT) == sub, qh, 0.0).astype(BF16)
                gst = jnp.where(bdmask, gst * dec_row + _dot(dotb, qhm), 0.0)
                dqt_i, dkf_i, dv_i = _hgrn_intra_bwd(qt[rs], kf[rs], b[rs], v[rs], do[rs], mbd, rsum)
                dkf_inter = dkh * ekb[rs]
                dqt_p[sub] = dqh * eb[rs] + dqt_i
                dkf_p[sub] = dkf_inter + dkf_i
                dv_p[sub] = dv_inter + dv_i
                row = jnp.sum(kf[rs] * dkf_inter, axis=0, keepdims=True) + ddec * dec_row
                dbl_p[sub] = jnp.broadcast_to(row, (SUB, 128))
            dst_ref[pair] = gst
            dqt = jnp.concatenate(dqt_p, axis=0)
            dkf = jnp.concatenate(dkf_p, axis=0)
            dv = jnp.concatenate(dv_p, axis=0)
            dg = _dot3_left(usub, qt * dqt - kf * dkf) + jnp.concatenate(dbl_p, axis=0)
            df = jnp.where(f > F_FLOOR, dg / f, 0.0)
            dlb_ref[:, sl] += jnp.sum(df * (1.0 - sg) - dkf * sgn, axis=0, keepdims=True)
            dfl = (1.0 - lbp) * sg * sgn * (df - dkf)
            dq = dqt * Q_SCALE * dsq
            db_ref[:, 0 * B_WIDTH + 128 * pair:0 * B_WIDTH + 128 * pair + 128] = dq.astype(BF16)
            db_ref[:, 1 * B_WIDTH + 128 * pair:1 * B_WIDTH + 128 * pair + 128] = dfl.astype(BF16)
            db_ref[:, 2 * B_WIDTH + 128 * pair:2 * B_WIDTH + 128 * pair + 128] = dv.astype(BF16)
            db_ref[:, 3 * B_WIDTH + 128 * pair:3 * B_WIDTH + 128 * pair + 128] = dz.astype(BF16)

    rev = lambda c: pl.BlockSpec((CHUNK, B_WIDTH), lambda i, c=c: (n - 1 - i, c // B_WIDTH))
    full = lambda a: pl.BlockSpec(a.shape, lambda i, n_=a.ndim: (0,) * n_)
    acc = lambda s: pl.BlockSpec(s, lambda i, n_=len(s): (0,) * n_)
    return pl.pallas_call(
        body, name=f"hgrn_bwd_{tag}", grid=(n,),
        in_specs=[rev(COL_BQ), rev(COL_BF), rev(COL_BI), rev(COL_BZ),
                  pl.BlockSpec((CHUNK, B_WIDTH), lambda i: (n - 1 - i, 1)),
                  pl.BlockSpec((CHUNK, B_WIDTH), lambda i: (n - 1 - i, 0)),
                  pl.BlockSpec((1, 2, 128, 128), lambda i: (n - 1 - i, 0, 0, 0)), full(lb), full(onorm)],
        out_specs=[pl.BlockSpec((CHUNK, 4 * B_WIDTH), lambda i: (n - 1 - i, 0)), acc((1, B_WIDTH)), acc((1, 128))],
        out_shape=[SDS((T, 4 * B_WIDTH), BF16), SDS((1, B_WIDTH), F32), SDS((1, 128), F32)],
        scratch_shapes=[pltpu.VMEM((2, 128, 128), F32), pltpu.VMEM((nsub, 128, 128), F32)],
        compiler_params=_params("arbitrary"),
    )(proj, proj, proj, proj, dy, o_saved, s0, lb, onorm)


def _lb_fwd(hgrn_lb):
    assert hgrn_lb.shape[0] == 2

    def body(x_ref, o_ref):
        x0, x1 = x_ref[0:1, :], x_ref[1:2, :]
        m = jnp.maximum(x0, x1)
        e0, e1 = jnp.exp(x0 - m), jnp.exp(x1 - m)
        p0, p1 = e0 / (e0 + e1), e1 / (e0 + e1)
        o_ref[0:1, :] = jnp.clip(p0 - p0, 0.0, 1.0 - 1e-6)
        o_ref[1:2, :] = jnp.clip((p0 + p1) - p0, 0.0, 1.0 - 1e-6)

    return pl.pallas_call(body, name="lb_fwd", out_shape=SDS(hgrn_lb.shape, F32))(hgrn_lb)


def _lb_bwd(hgrn_lb, dlb):
    def body(x_ref, d_ref, o_ref):
        x0, x1 = x_ref[0:1, :], x_ref[1:2, :]
        m = jnp.maximum(x0, x1)
        e0, e1 = jnp.exp(x0 - m), jnp.exp(x1 - m)
        p0, p1 = e0 / (e0 + e1), e1 / (e0 + e1)
        val = (p0 + p1) - p0
        dp1 = jnp.where((val > 0.0) & (val < 1.0 - 1e-6), d_ref[1:2, :], 0.0)
        inner = p1 * dp1
        o_ref[0:1, :] = p0 * (0.0 - inner)
        o_ref[1:2, :] = p1 * (dp1 - inner)

    return pl.pallas_call(body, name="lb_bwd", out_shape=SDS(hgrn_lb.shape, F32))(hgrn_lb, dlb)


def _fox_prep(proj, bf, tag):
    T = proj.shape[0]
    n = T // CHUNK

    def body(q0_ref, q1_ref, k0_ref, k1_ref, v0_ref, v1_ref, fl_ref, bf_ref, qo_ref, ko_ref, vt_ref, carry_ref):
        for p, v_ref in enumerate((v0_ref, v0_ref, v1_ref, v1_ref)):
            vt_ref[p, 0] = v_ref[:, 128 * (p % 2):128 * (p % 2) + 128].T.astype(BF16)

        @pl.when(pl.program_id(0) == 0)
        def _():
            carry_ref[...] = jnp.zeros_like(carry_ref)

        ltri = jnp.where(_lane((CHUNK, CHUNK)) <= _row((CHUNK, CHUNK)), 1.0, 0.0).astype(BF16)
        lf = jax.nn.log_sigmoid(fl_ref[...] + bf_ref[...])
        c = _dot3_left(ltri, lf) + carry_ref[...]
        carry_ref[...] = c[CHUNK - 1:CHUNK, :]
        lane = _lane((CHUNK, 128))
        feat = lane < 64
        ones_q = (lane >= 67) & (lane <= 69)
        ones_k = (lane >= 64) & (lane <= 66)
        qrefs, krefs = (q0_ref, q1_ref), (k0_ref, k1_ref)
        for h in range(C_HEADS):
            blk = slice(128 * ((h // 2) % 2), 128 * ((h // 2) % 2) + 128)
            qp, kp = qrefs[h // 4][:, blk], krefs[h // 4][:, blk]
            if h % 2:
                qp, kp = pltpu.roll(qp, 64, axis=1), pltpu.roll(kp, 64, axis=1)
            ch = jnp.broadcast_to(c[:, h:h + 1], (CHUNK, 128))
            hi = ch.astype(BF16).astype(F32)
            r1 = ch - hi
            mid = r1.astype(BF16).astype(F32)
            lo = r1 - mid
            aq = jnp.where(lane == 64, hi, jnp.where(lane == 65, mid, jnp.where(lane == 66, lo,
                           jnp.where(ones_q, 1.0, 0.0))))
            ak = jnp.where(lane == 67, -hi, jnp.where(lane == 68, -mid, jnp.where(lane == 69, -lo,
                           jnp.where(ones_k, 1.0, 0.0))))
            qo_ref[:, 128 * h:128 * h + 128] = jnp.where(feat, qp * Q_SCALE, aq).astype(BF16)
            ko_ref[:, 128 * h:128 * h + 128] = jnp.where(feat, kp, ak).astype(BF16)

    w = 256
    col = lambda c: pl.BlockSpec((CHUNK, w), lambda i, c=c: (i, c // w))
    return pl.pallas_call(
        body, name=f"fox_prep_{tag}", grid=(n,),
        in_specs=[col(COL_CQ), col(COL_CQ + w), col(COL_CK), col(COL_CK + w), col(COL_CV), col(COL_CV + w),
                  pl.BlockSpec((CHUNK, 128), lambda i: (i, COL_CF // 128)), pl.BlockSpec((1, 128), lambda i: (0, 0))],
        out_specs=[pl.BlockSpec((CHUNK, C_HEADS * 128), lambda i: (i, 0))] * 2
        + [pl.BlockSpec((C_HEADS // 2, 1, 128, CHUNK), lambda i: (0, i, 0, 0))],
        out_shape=[SDS((T, C_HEADS * 128), BF16)] * 2 + [SDS((C_HEADS // 2, n, 128, CHUNK), BF16)],
        scratch_shapes=[pltpu.VMEM((1, 128), F32)], compiler_params=_params("arbitrary"),
    )(proj, proj, proj, proj, proj, proj, proj, bf)


FOX_TILE = 256
FOX_KEYS = 512


def _fox_mask(tk, tq, k0, q0):
    return (_row((tk, tq)) + (k0 - q0)) <= _lane((tk, tq))


def _ride_refs(ride, rest, n_out, n_scratch):
    n = ride.n if ride else 0
    srcs, rest = rest[:n], rest[n:]
    outs, rest = rest[:n_out], rest[n_out:]
    dsts, rest = rest[:n], rest[n:]
    return srcs, outs, dsts, rest[:n_scratch], rest[n_scratch:]


def _fox_fwd(qt, kt, vt, proj, tag, ride=None):
    T = proj.shape[0]
    tq, tk = _tile(T, FOX_TILE), _tile(T, FOX_KEYS)
    nq, nsub = T // tq, tk // CHUNK
    npair = C_HEADS // 2

    def body(q_ref, k_ref, vt_ref, z_ref, *rest):
        ride_srcs, (o_ref, lse_ref, y_ref), ride_dsts, (acc_ref, st_ref, pt_ref), ride_sems = _ride_refs(ride, rest, 3, 3)
        i = pl.program_id(1)
        if ride:
            @pl.when((pl.program_id(0) == 0) & (i == 0))
            def _():
                ride.start(ride_srcs, ride_dsts, ride_sems)

        qs = (q_ref[:, 0:128], q_ref[:, 128:256])
        acc_ref[...] = jnp.zeros_like(acc_ref)
        pt_ref[...] = jnp.zeros_like(pt_ref)
        nfull = (i * tq) // tk

        def scores(j):
            kb = k_ref[pl.ds(pl.multiple_of(j * tk, tk), tk), :]
            return tuple(_dot_nt(kb[:, 128 * h:128 * h + 128], qs[h]) for h in range(2))

        def weigh(j, h):
            rows = slice(64 * h, 64 * h + 64)
            pv = _dot(vt_ref[0, nsub * j, rows, :], pt_ref[h, 0:CHUNK, :])
            for c in range(1, nsub):
                pv = pv + _dot(vt_ref[0, nsub * j + c, rows, :], pt_ref[h, CHUNK * c:CHUNK * c + CHUNK, :])
            return pv

        def block(j, carry, diagonal):
            nxt = () if diagonal else scores(j + 1)
            pvs = [weigh(jnp.maximum(j - 1, 0), h) for h in range(2)]
            new = []
            for h in range(2):
                m, l, alpha_prev = carry[3 * h:3 * h + 3]
                st = st_ref[h]
                if diagonal:
                    st = jnp.where(_fox_mask(tk, tq, j * tk, i * tq), st, -jnp.inf)
                m_new = jnp.maximum(m, _colreduce(st, jnp.maximum))
                pt = jnp.exp(st - m_new)
                alpha = jnp.exp(m - m_new)
                rows = slice(64 * h, 64 * h + 64)
                acc_ref[rows, :] = alpha_prev * acc_ref[rows, :] + pvs[h]
                pt_ref[h] = pt.astype(BF16)
                new += [m_new, alpha * l + _colreduce(pt, jnp.add), alpha]
            for h, st in enumerate(nxt):
                st_ref[h] = st
            return tuple(new)

        for h, st in enumerate(scores(0)):
            st_ref[h] = st
        init = (jnp.full((1, tq), -jnp.inf, F32), jnp.zeros((1, tq), F32), jnp.ones((1, tq), F32)) * 2
        carry = lax.fori_loop(0, nfull, lambda j, c: block(j, c, False), init)
        m0, l0, a0, m1, l1, a1 = block(nfull, carry, True)
        for h, alpha in enumerate((a0, a1)):
            rows = slice(64 * h, 64 * h + 64)
            acc_ref[rows, :] = alpha * acc_ref[rows, :] + weigh(nfull, h)
        inv = jnp.where(_row((128, tq)) < 64, 1.0 / l0, 1.0 / l1)
        o = (acc_ref[...] * inv).T
        o_ref[...] = o
        r8 = _row((8, tq))
        lse_ref[0, 0] = jnp.where(r8 == 0, m0 + jnp.log(l0), jnp.where(r8 == 1, m1 + jnp.log(l1), 0.0))
        sz, _ = _silu_and_grad(z_ref[...])
        y_ref[...] = (o * sz).astype(BF16)
        if ride:
            @pl.when((pl.program_id(0) == npair - 1) & (i == nq - 1))
            def _():
                ride.wait(ride_srcs, ride_dsts, ride_sems)

    blk = pl.BlockSpec((tq, 128), lambda p, i: (i, p))
    extra = ride or _ChipExchange("gather", ())
    return pl.pallas_call(
        body, name=f"fox_fwd_{tag}", grid=(npair, nq),
        in_specs=[pl.BlockSpec((tq, 256), lambda p, i: (i, p)), pl.BlockSpec((T, 256), lambda p, i: (0, p)),
                  pl.BlockSpec((1, T // CHUNK, 128, CHUNK), lambda p, i: (p, 0, 0, 0)),
                  pl.BlockSpec((tq, 128), lambda p, i: (i, COL_CZ // 128 + p))] + extra.in_specs,
        out_specs=[blk, pl.BlockSpec((1, 1, 8, tq), lambda p, i: (p, i, 0, 0)), blk] + extra.out_specs,
        out_shape=[SDS((T, C_WIDTH), F32), SDS((npair, nq, 8, tq), F32), SDS((T, C_WIDTH), BF16)] + extra.out_shape,
        scratch_shapes=[pltpu.VMEM((128, tq), F32), pltpu.VMEM((2, tk, tq), F32), pltpu.VMEM((2, tk, tq), BF16)]
        + (extra.scratch if ride else []),
        compiler_params=pltpu.CompilerParams(dimension_semantics=("arbitrary", "arbitrary"), vmem_limit_bytes=VMEM_LIMIT,
                                             has_side_effects=bool(ride)),
    )(qt, kt, vt, proj, *extra.sources)


def _fox_bwd_prep(proj, dy, o, tag):
    T = proj.shape[0]
    tq = _tile(T, FOX_TILE)

    def body(z_ref, dy_ref, o_ref, do_ref, dl_ref, dz_ref):
        sz, dsz = _silu_and_grad(z_ref[...])
        dyv, ov = dy_ref[...], o_ref[...]
        do = dyv * sz
        do_ref[...] = do.astype(BF16)
        dz_ref[...] = (dyv * ov * dsz).astype(BF16)
        sel = jnp.where((_lane((16, 128)) >> 6) == _row((16, 128)), 1.0, 0.0).astype(BF16)
        hi, mid, lo = _split3(do * ov)
        dl_ref[0, 0] = (_dot_nt(sel, hi) + _dot_nt(sel, mid) + _dot_nt(sel, lo))[0:8, :]

    blk = pl.BlockSpec((tq, 128), lambda i, p: (i, p))
    return pl.pallas_call(
        body, name=f"fox_bwd_prep_{tag}", grid=(T // tq, C_WIDTH // 128),
        in_specs=[pl.BlockSpec((tq, 128), lambda i, p: (i, COL_CZ // 128 + p)),
                  pl.BlockSpec((tq, 128), lambda i, p: (i, (A_WIDTH + B_WIDTH) // 128 + p)), blk],
        out_specs=[blk, pl.BlockSpec((1, 1, 8, tq), lambda i, p: (p, i, 0, 0)), blk],
        out_shape=[SDS((T, C_WIDTH), BF16), SDS((C_HEADS // 2, T // tq, 8, tq), F32), SDS((T, C_WIDTH), BF16)],
        compiler_params=_params("parallel", "parallel"),
    )(proj, dy, o)


def _fox_bwd(qt, kt, proj, do, lse, delta, tag, ride=None):
    T = proj.shape[0]
    tq, tk = _tile(T, FOX_TILE), _tile(T, FOX_KEYS)
    nq, nk = T // tq, T // tk
    ndiag = tk // tq
    npair = C_HEADS // 2

    def body(q_ref, k_ref, v_ref, do_ref, lse_ref, dl_ref, *rest):
        ride_srcs, (dq_ref, dk_ref, dv_ref), ride_dsts, scratch, ride_sems = _ride_refs(ride, rest, 3, 4)
        dvacc_ref, sc_ref, pt_ref, ds_ref = scratch
        j = pl.program_id(1)
        first = (j * tk) // tq
        if ride:
            @pl.when((pl.program_id(0) == 0) & (j == 0))
            def _():
                ride.start(ride_srcs, ride_dsts, ride_sems)

        @pl.when(j == 0)
        def _():
            dq_ref[...] = jnp.zeros_like(dq_ref)

        dk_ref[...] = jnp.zeros_like(dk_ref)
        dvacc_ref[...] = jnp.zeros_like(dvacc_ref)
        ks = (k_ref[:, 0:128], k_ref[:, 128:256])
        kts = tuple(k.astype(F32).T.astype(BF16) for k in ks)
        vb = v_ref[...].astype(BF16)
        lo = _lane((tq, 128)) < 64

        def operands(i):
            q0 = pl.multiple_of(i * tq, tq)
            qb = q_ref[pl.ds(q0, tq), :]
            dob = do_ref[pl.ds(q0, tq), :]
            qhs = (qb[:, 0:128], qb[:, 128:256])
            dohs = (jnp.where(lo, dob, jnp.zeros_like(dob)), jnp.where(lo, jnp.zeros_like(dob), dob))
            return qhs, dohs

        def scores(i):
            qhs, dohs = operands(i)
            return tuple(_dot_nt(ks[h], qhs[h]) for h in range(2)) + tuple(_dot_nt(vb, dohs[h]) for h in range(2))

        def park(sc):
            for a, s in enumerate(sc):
                sc_ref[a] = s

        def grads(i):
            qhs, dohs = operands(i)
            dvacc_ref[...] += _dot(jnp.concatenate([pt_ref[0], pt_ref[1]], axis=1), jnp.concatenate(dohs, axis=0))
            for h in range(2):
                dk_ref[:, 128 * h:128 * h + 128] += _dot(ds_ref[h], qhs[h])
                dq_ref[h, i] += _dot(kts[h], ds_ref[h])

        def block(i, diagonal, opening):
            nxt = scores(jnp.minimum(i + 1, nq - 1))
            if not opening:
                grads(i - 1)
            lsev = lse_ref[0, i]
            dlv = dl_ref[0, i]
            for h in range(2):
                pt = jnp.exp(sc_ref[h] - lsev[h:h + 1, :])
                if diagonal:
                    pt = jnp.where(_fox_mask(tk, tq, j * tk, i * tq), pt, 0.0)
                ds_ref[h] = (pt * (sc_ref[2 + h] - dlv[h:h + 1, :])).astype(BF16)
                pt_ref[h] = pt.astype(BF16)
            park(nxt)

        park(scores(first))
        for d in range(ndiag):
            block(first + d, True, d == 0)

        def step(i, carry):
            block(i, False, False)
            return carry

        lax.fori_loop(first + ndiag, nq, step, 0)
        grads(nq - 1)
        dv_ref[...] = dvacc_ref[...].astype(BF16)
        if ride:
            @pl.when((pl.program_id(0) == npair - 1) & (j == nk - 1))
            def _():
                ride.wait(ride_srcs, ride_dsts, ride_sems)

    full = lambda w: pl.BlockSpec((T, w), lambda p, j: (0, p))
    stat = pl.BlockSpec((1, nq, 8, tq), lambda p, j: (p, 0, 0, 0))
    extra = ride or _ChipExchange("gather", ())
    return pl.pallas_call(
        body, name=f"fox_bwd_{tag}", grid=(npair, nk),
        in_specs=[full(256), pl.BlockSpec((tk, 256), lambda p, j: (j, p)),
                  pl.BlockSpec((tk, 128), lambda p, j: (j, COL_CV // 128 + p)), full(128), stat, stat] + extra.in_specs,
        out_specs=[pl.BlockSpec((2, nq, 128, tq), lambda p, j: (p, 0, 0, 0)), pl.BlockSpec((tk, 256), lambda p, j: (j, p)),
                   pl.BlockSpec((tk, 128), lambda p, j: (j, p))] + extra.out_specs,
        out_shape=[SDS((C_HEADS, nq, 128, tq), F32), SDS((T, C_HEADS * 128), F32), SDS((T, C_WIDTH), BF16)]
        + extra.out_shape,
        scratch_shapes=[pltpu.VMEM((tk, 128), F32), pltpu.VMEM((4, tk, tq), F32), pltpu.VMEM((2, tk, tq), BF16),
                        pltpu.VMEM((2, tk, tq), BF16)] + (extra.scratch if ride else []),
        compiler_params=pltpu.CompilerParams(dimension_semantics=("arbitrary", "arbitrary"), vmem_limit_bytes=VMEM_LIMIT,
                                             has_side_effects=bool(ride)),
    )(qt, kt, proj, do, lse, delta, *extra.sources)


def _fox_bwd_post(dqt, dkt, proj, bf, tag):
    T = proj.shape[0]
    tq = _tile(T, FOX_TILE)
    n = T // tq

    def body(dq_ref, dk_ref, fl_ref, bf_ref, oq_ref, ok_ref, ofl_ref, dbf_ref, carry_ref):
        @pl.when(pl.program_id(0) == 0)
        def _():
            carry_ref[...] = jnp.zeros_like(carry_ref)
            dbf_ref[...] = jnp.zeros_like(dbf_ref)

        lane = _lane((tq, 128))
        lo = lane < 64
        dqs = [dq_ref[h, 0].T for h in range(C_HEADS)]
        dc = jnp.zeros((tq, 128), F32)
        for h in range(C_HEADS):
            dc = dc + jnp.where(lane == h, dqs[h][:, 64:65] - dk_ref[:, 128 * h + 67:128 * h + 68], 0.0)
        utri = jnp.where(_lane((tq, tq)) >= _row((tq, tq)), 1.0, 0.0).astype(BF16)
        dlf = _dot3_left(utri, dc) + carry_ref[...]
        carry_ref[...] = dlf[0:1, :]
        dfl = jnp.where(lane < C_HEADS, dlf * _sigmoid(-(fl_ref[...] + bf_ref[...])), 0.0)
        ofl_ref[...] = dfl.astype(BF16)
        dbf_ref[...] += jnp.sum(dfl, axis=0, keepdims=True)
        for p in range(C_HEADS // 2):
            a, b = 128 * (2 * p), 128 * (2 * p + 1)
            oq_ref[:, 128 * p:128 * p + 128] = (
                jnp.where(lo, dqs[2 * p], pltpu.roll(dqs[2 * p + 1], 64, axis=1)) * Q_SCALE).astype(BF16)
            ok_ref[:, 128 * p:128 * p + 128] = jnp.where(
                lo, dk_ref[:, a:a + 128], pltpu.roll(dk_ref[:, b:b + 128], 64, axis=1)).astype(BF16)

    rev = lambda w: pl.BlockSpec((tq, w), lambda i: (n - 1 - i, 0))
    return pl.pallas_call(
        body, name=f"fox_bwd_post_{tag}", grid=(n,),
        in_specs=[pl.BlockSpec((C_HEADS, 1, 128, tq), lambda i: (0, n - 1 - i, 0, 0)), rev(C_HEADS * 128),
                  pl.BlockSpec((tq, 128), lambda i: (n - 1 - i, COL_CF // 128)), pl.BlockSpec((1, 128), lambda i: (0, 0))],
        out_specs=[rev(C_WIDTH), rev(C_WIDTH), rev(128), pl.BlockSpec((1, 128), lambda i: (0, 0))],
        out_shape=[SDS((T, C_WIDTH), BF16), SDS((T, C_WIDTH), BF16), SDS((T, 128), BF16), SDS((1, 128), F32)],
        scratch_shapes=[pltpu.VMEM((1, 128), F32)], compiler_params=_params("arbitrary"),
    )(dqt, dkt, proj, bf)


def _adamw_math(w, g, m, v):
    m = ADAM_B1 * m + (1.0 - ADAM_B1) * g
    v = ADAM_B2 * v + (1.0 - ADAM_B2) * (g * g)
    delta = -ADAM_LR * ((m / ADAM_C1) / (jnp.sqrt(v / ADAM_C2) + ADAM_EPS) + ADAM_WD * w)
    return delta, m, v


def _adamw_pair(w, m, v, ga, gb, name):
    n0 = w.shape[0]
    most = max(1, ADAMW_BLOCK_BYTES // (4 * math.prod(w.shape[1:])))
    t0 = max(t for t in range(1, min(n0, most) + 1) if n0 % t == 0)

    def body(w_ref, m_ref, v_ref, ga_ref, gb_ref, g_ref, d_ref, nm_ref, nv_ref):
        g = ga_ref[...] + gb_ref[...]
        g_ref[...] = g
        d_ref[...], nm_ref[...], nv_ref[...] = _adamw_math(w_ref[...], g, m_ref[...], v_ref[...])

    blk = pl.BlockSpec((t0,) + w.shape[1:], lambda i: (i, 0, 0))
    return pl.pallas_call(
        body, name=name, grid=(n0 // t0,), in_specs=[blk] * 5, out_specs=[blk] * 4,
        out_shape=[SDS(w.shape, F32)] * 4, compiler_params=_params("parallel"),
    )(w, m, v, ga, gb)


def _adamw_small(w, m, v, gall):
    R = w.shape[0]

    def body(w_ref, m_ref, v_ref, g_ref, go_ref, d_ref, nm_ref, nv_ref):
        g = g_ref[0]
        for k in range(1, N_DEV):
            g = g + g_ref[k]
        go_ref[...] = g
        d_ref[...], nm_ref[...], nv_ref[...] = _adamw_math(w_ref[...], g, m_ref[...], v_ref[...])

    return pl.pallas_call(body, name="adamw_small", out_shape=[SDS((R, 128), F32)] * 4,
                          compiler_params=pltpu.CompilerParams(vmem_limit_bytes=VMEM_LIMIT))(w, m, v, gall)


def _sum_chips(layers, name, layer_major):
    _, R, C = layers[0].shape
    L = len(layers)
    tc = _tile(C, 256)

    def body(*refs):
        o_ref = refs[-1]
        for l, p_ref in enumerate(refs[:-1]):
            p = [p_ref[k].astype(F32) for k in range(N_CHIPS)]
            s = ((p[0] + p[1]) + p[2]) + p[3]
            if layer_major:
                o_ref[l] = s
            else:
                o_ref[:, l, :] = s

    out = (L, R, C) if layer_major else (R, L, C)
    out_blk = (L, R, tc) if layer_major else (R, L, tc)
    return pl.pallas_call(
        body, name=name, grid=(C // tc,),
        in_specs=[pl.BlockSpec((N_CHIPS, R, tc), lambda i: (0, 0, i))] * L,
        out_specs=pl.BlockSpec(out_blk, lambda i: (0, 0, i)), out_shape=SDS(out, F32),
        compiler_params=_params("parallel"),
    )(*layers)


ANY = pl.BlockSpec(memory_space=pl.ANY)


def _mesh_pos():
    return lax.axis_index("x"), lax.axis_index("y"), lax.axis_index("c")


def _other_chips(x, y):
    return [(1 - x, y), (x, 1 - y), (1 - x, 1 - y)]


class _ChipExchange:
    def __init__(self, mode, sources):
        assert mode in ("gather", "scatter")
        self.mode, self.sources = mode, tuple(sources)
        self.n = len(self.sources)
        self.in_specs = [ANY] * self.n
        self.out_specs = [ANY] * self.n
        self.out_shape = [SDS(((N_CHIPS,) + s.shape) if mode == "gather" else s.shape, s.dtype) for s in self.sources]
        self.scratch = [pltpu.SemaphoreType.DMA((3 * self.n,)), pltpu.SemaphoreType.DMA((3 * self.n,)),
                        pltpu.SemaphoreType.DMA((self.n,))]

    def _copies(self, srcs, dsts, send_sems, recv_sems, local_sems):
        x, y, c = _mesh_pos()
        me = 2 * x + y
        view = (lambda r, chip: r) if self.mode == "gather" else (lambda r, chip: r.at[chip])
        local = [pltpu.make_async_copy(view(s, me), d.at[me], local_sems.at[a]) for a, (s, d) in enumerate(zip(srcs, dsts))]
        sends, recvs = [], []
        for j, (px, py) in enumerate(_other_chips(x, y)):
            peer = 2 * px + py
            for a, (s, d) in enumerate(zip(srcs, dsts)):
                sems = dict(send_sem=send_sems.at[self.n * j + a], recv_sem=recv_sems.at[self.n * j + a],
                            device_id=(px, py, c), device_id_type=MESH_ID)
                sends.append(pltpu.make_async_remote_copy(src_ref=view(s, peer), dst_ref=d.at[me], **sems))
                recvs.append(pltpu.make_async_remote_copy(src_ref=view(s, me), dst_ref=d.at[peer], **sems))
        return local, sends, recvs

    def start(self, srcs, dsts, sems):
        local, sends, _ = self._copies(srcs, dsts, *sems)
        for cp in local + sends:
            cp.start()

    def wait(self, srcs, dsts, sems):
        local, sends, recvs = self._copies(srcs, dsts, *sems)
        for cp in recvs:
            cp.wait_recv()
        for cp in sends:
            cp.wait_send()
        for cp in local:
            cp.wait()


def _gather_weights(win, wout, tag):
    ex = _ChipExchange("gather", (win, wout))

    def body(win_ref, wout_ref, gin_ref, gout_ref, *sems):
        ex.start((win_ref, wout_ref), (gin_ref, gout_ref), sems)
        ex.wait((win_ref, wout_ref), (gin_ref, gout_ref), sems)

    return pl.pallas_call(
        body, name=f"gather_weights_{tag}", in_specs=ex.in_specs, out_specs=ex.out_specs, out_shape=ex.out_shape,
        scratch_shapes=ex.scratch, compiler_params=pltpu.CompilerParams(has_side_effects=True),
    )(win, wout)


def _exchange_grads(gin, gout, small):
    ex = _ChipExchange("scatter", (gin, gout))

    def body(gin_ref, gout_ref, small_ref, rin_ref, rout_ref, rsmall_ref, send_sems, recv_sems, local_sems,
             small_send, small_recv, small_local):
        x, y, c = _mesh_pos()
        me_dev = 4 * x + 2 * y + c
        sems = (send_sems, recv_sems, local_sems)
        ex.start((gin_ref, gout_ref), (rin_ref, rout_ref), sems)
        own = pltpu.make_async_copy(small_ref, rsmall_ref.at[me_dev], small_local)
        own.start()
        peers = [(k, ((1 - x) if k & 4 else x, (1 - y) if k & 2 else y, (1 - c) if k & 1 else c)) for k in range(1, N_DEV)]
        sends = []
        for k, peer in peers:
            cp = pltpu.make_async_remote_copy(src_ref=small_ref, dst_ref=rsmall_ref.at[me_dev], send_sem=small_send.at[k - 1],
                                              recv_sem=small_recv.at[k - 1], device_id=peer, device_id_type=MESH_ID)
            cp.start()
            sends.append(cp)
        ex.wait((gin_ref, gout_ref), (rin_ref, rout_ref), sems)
        for k, (px, py, pc) in peers:
            pltpu.make_async_remote_copy(src_ref=small_ref, dst_ref=rsmall_ref.at[4 * px + 2 * py + pc],
                                         send_sem=small_send.at[k - 1], recv_sem=small_recv.at[k - 1],
                                         device_id=(px, py, pc), device_id_type=MESH_ID).wait_recv()
        for cp in sends:
            cp.wait_send()
        own.wait()

    return pl.pallas_call(
        body, name="exchange_grads", in_specs=ex.in_specs + [ANY], out_specs=ex.out_specs + [ANY],
        out_shape=ex.out_shape + [SDS((N_DEV,) + small.shape, F32)],
        scratch_shapes=ex.scratch + [pltpu.SemaphoreType.DMA((N_DEV - 1,)), pltpu.SemaphoreType.DMA((N_DEV - 1,)),
                                     pltpu.SemaphoreType.DMA],
        compiler_params=pltpu.CompilerParams(has_side_effects=True),
    )(gin, gout, small)


def _swap_cores(pin, pout):
    def body(pin_ref, pout_ref, oin_ref, oout_ref, send_sems, recv_sems):
        x, y, c = _mesh_pos()
        cps = [pltpu.make_async_remote_copy(src_ref=src, dst_ref=dst, send_sem=send_sems.at[a], recv_sem=recv_sems.at[a],
                                            device_id=(x, y, 1 - c), device_id_type=MESH_ID)
               for a, (src, dst) in enumerate(((pin_ref, oin_ref), (pout_ref, oout_ref)))]
        for cp in cps:
            cp.start()
        for cp in cps:
            cp.wait()

    return pl.pallas_call(
        body, name="swap_cores", in_specs=[ANY, ANY], out_specs=[ANY, ANY],
        out_shape=[SDS(pin.shape, F32), SDS(pout.shape, F32)],
        scratch_shapes=[pltpu.SemaphoreType.DMA((2,)), pltpu.SemaphoreType.DMA((2,))],
        compiler_params=pltpu.CompilerParams(has_side_effects=True),
    )(pin, pout)


def _pack_small(parts):
    flat = [jnp.pad(p.reshape(-1), (0, (-p.size) % 128)) for p in parts]
    v = jnp.concatenate(flat)
    return jnp.pad(v, (0, (-v.size) % 1024)).reshape(-1, 128)


def _unpack_small(packed):
    flat = packed.reshape(-1)
    out, off = [], 0
    for _, shape in SMALL_PARAMS:
        size = math.prod(shape)
        out.append(flat[off:off + size].reshape(shape))
        off += size + (-size) % 128
    return out


def _layer_consts(l, gmlp_ln_g, gmlp_ln_b, gmlp_w_s, gmlp_b_s, hgrn_onorm_g, fox_b_f):
    causal = jnp.tril(jnp.ones((CHUNK, CHUNK), bool))
    wm = jnp.where(causal[None], gmlp_w_s[l], 0.0)
    return dict(
        lng=gmlp_ln_g[l].reshape(1, A_WIDTH), lnb=gmlp_ln_b[l].reshape(1, A_WIDTH),
        wm=wm.astype(BF16), wmt=jnp.swapaxes(wm, 1, 2).astype(BF16),
        bst=jnp.pad(gmlp_b_s[l].T, ((0, 0), (0, 128 - A_GROUPS))),
        onorm=jnp.tile(hgrn_onorm_g[l], 4).reshape(1, B_WIDTH),
        bf=jnp.pad(fox_b_f[l], (0, 128 - C_HEADS)).reshape(1, 128),
    )


def kernel(x, norm_g, w_in, w_out, gmlp_ln_g, gmlp_ln_b, gmlp_w_s, gmlp_b_s, hgrn_lb, hgrn_onorm_g, fox_b_f, final_norm_g, loss_target, m_norm_g, m_w_in, m_w_out, m_gmlp_ln_g, m_gmlp_ln_b, m_gmlp_w_s, m_gmlp_b_s, m_hgrn_lb, m_hgrn_onorm_g, m_fox_b_f, m_final_norm_g, v_norm_g, v_w_in, v_w_out, v_gmlp_ln_g, v_gmlp_ln_b, v_gmlp_w_s, v_gmlp_b_s, v_hgrn_lb, v_hgrn_onorm_g, v_fox_b_f, v_final_norm_g):
    T = x.shape[1]
    shard_in = w_in.shape[2]
    shard_out = w_out.shape[1]
    xs = x.reshape(T, D_MODEL)
    tgt = loss_target.reshape(T, D_MODEL)

    w_in_b, w_out_b = w_in.astype(BF16), w_out.astype(BF16)

    def full_weights(gathered_in, gathered_out):
        wi = jnp.concatenate([gathered_in[k] for k in range(N_CHIPS)], axis=-1)
        return jnp.pad(wi, ((0, 0), (0, D_IN_PAD - D_IN))), gathered_out.reshape(N_CHIPS * shard_out, D_MODEL)

    lb_all = _lb_fwd(hgrn_lb)
    consts = [_layer_consts(l, gmlp_ln_g, gmlp_ln_b, gmlp_w_s, gmlp_b_s, hgrn_onorm_g, fox_b_f) for l in range(DEPTH)]

    saved = []
    xl = xs
    weights = full_weights(*_gather_weights(w_in_b[0], w_out_b[0], "l0"))
    for l in range(DEPTH):
        cs = consts[l]
        tag = f"l{l}"
        w_in_l, w_out_l = weights
        h, proj = _inproj(xl, norm_g[l].reshape(1, D_MODEL), w_in_l, tag)
        ya = _gmlp_fwd(proj, cs["lng"], cs["lnb"], cs["wm"], cs["bst"], tag)
        yb, ob, s0 = _hgrn_fwd(proj, lb_all[l].reshape(1, B_WIDTH), cs["onorm"], tag)
        qt, kt, vt = _fox_prep(proj, cs["bf"], tag)
        ride = _ChipExchange("gather", (w_in_b[l + 1], w_out_b[l + 1])) if l + 1 < DEPTH else None
        oc, lse, yc, *gathered = _fox_fwd(qt, kt, vt, proj, tag, ride)
        saved.append(dict(x=xl, h=h, proj=proj, ya=ya, yb=yb, yc=yc, ob=ob, s0=s0, qt=qt, kt=kt, oc=oc, lse=lse,
                          w_in=w_in_l, w_out=w_out_l))
        xl = _outproj(xl, ya, yb, yc, w_out_l, tag)
        if ride:
            weights = full_weights(*gathered)

    dx, loss_part, d_final = _loss_head(xl, final_norm_g.reshape(1, D_MODEL), tgt)
    loss = lax.psum(loss_part[0, 0], ("x", "y", "c"))

    g_small = {}
    dlb_rows, rin, rout = [None] * DEPTH, [None] * DEPTH, [None] * DEPTH
    slabs = None
    for l in reversed(range(DEPTH)):
        cs, sv = consts[l], saved[l]
        tag = f"l{l}"
        proj = sv["proj"]
        dy, dw_out = _outproj_bwd(dx, sv["ya"], sv["yb"], sv["yc"], sv["w_out"], tag)
        da, dwm, dbst, dlng, dlnb = _gmlp_bwd(proj, dy, cs["lng"], cs["lnb"], cs["wm"], cs["wmt"], cs["bst"], tag)
        db, dlb_rows[l], donorm = _hgrn_bwd(proj, dy, sv["ob"], sv["s0"], lb_all[l].reshape(1, B_WIDTH), cs["onorm"], tag)
        do, delta, dzc = _fox_bwd_prep(proj, dy, sv["oc"], tag)
        ride = _ChipExchange("scatter", slabs) if slabs else None
        dqt, dkt, dvc, *received = _fox_bwd(sv["qt"], sv["kt"], proj, do, sv["lse"], delta, tag, ride)
        if ride:
            rin[l + 1], rout[l + 1] = received
        dqc, dkc, dflc, dbf = _fox_bwd_post(dqt, dkt, proj, cs["bf"], tag)
        dproj = jnp.concatenate([da, db, dqc, dkc, dvc, dzc, dflc, jnp.zeros((T, 128), BF16)], axis=1)
        dw_in = _dw_in(sv["h"], dproj, tag)
        dx, dng = _dx_in(sv["x"], norm_g[l].reshape(1, D_MODEL), dx, dproj, sv["w_in"], tag)
        slabs = (jnp.stack([dw_in[k * shard_in:(k + 1) * shard_in] for k in range(N_CHIPS)]).astype(BF16),
                 dw_out.reshape(N_CHIPS, shard_out, D_MODEL).astype(BF16))
        g_small[l] = dict(norm_g=dng.reshape(D_MODEL), ln_g=dlng.reshape(4, 64), ln_b=dlnb.reshape(4, 64), w_s=dwm,
                          b_s=dbst[:, :A_GROUPS].T, onorm=donorm[0, :64], bf=dbf[0, :C_HEADS])
    grad_x = dx.reshape(x.shape)
    d_hgrn_lb = _lb_bwd(hgrn_lb, jnp.concatenate(dlb_rows, axis=0))

    stack = lambda key: jnp.stack([g_small[l][key] for l in range(DEPTH)])
    small_g = _pack_small([stack("norm_g"), stack("ln_g"), stack("ln_b"), stack("w_s"), stack("b_s"), d_hgrn_lb,
                           stack("onorm"), stack("bf"), d_final.reshape(D_MODEL)])

    rin[0], rout[0], rsmall = _exchange_grads(*slabs, small_g)
    pin, pout = _sum_chips(rin, "sum_chips_w_in", False), _sum_chips(rout, "sum_chips_w_out", True)
    oin, oout = _swap_cores(pin, pout)
    to_view = lambda a: jnp.transpose(a, (2, 0, 1))
    g_w_in, d_w_in, nm_w_in, nv_w_in = [
        jnp.transpose(o, (1, 2, 0))
        for o in _adamw_pair(to_view(w_in), to_view(m_w_in), to_view(v_w_in), pin, oin, "adamw_w_in")]
    g_w_out, d_w_out, nm_w_out, nv_w_out = _adamw_pair(w_out, m_w_out, v_w_out, pout, oout, "adamw_w_out")

    small_w = [norm_g, gmlp_ln_g, gmlp_ln_b, gmlp_w_s, gmlp_b_s, hgrn_lb, hgrn_onorm_g, fox_b_f, final_norm_g]
    small_m = [m_norm_g, m_gmlp_ln_g, m_gmlp_ln_b, m_gmlp_w_s, m_gmlp_b_s, m_hgrn_lb, m_hgrn_onorm_g, m_fox_b_f, m_final_norm_g]
    small_v = [v_norm_g, v_gmlp_ln_g, v_gmlp_ln_b, v_gmlp_w_s, v_gmlp_b_s, v_hgrn_lb, v_hgrn_onorm_g, v_fox_b_f, v_final_norm_g]
    outs = _adamw_small(_pack_small(small_w), _pack_small(small_m), _pack_small(small_v), rsmall)
    sg, sd, sm, sv_ = [_unpack_small(o) for o in outs]

    def order(big_in, big_out, small):
        return [small[0], big_in, big_out] + small[1:]

    return (loss, grad_x, *order(g_w_in, g_w_out, sg), *order(d_w_in, d_w_out, sd), *order(nm_w_in, nm_w_out, sm),
            *order(nv_w_in, nv_w_out, sv_))
```

```python
import functools
import math

import jax
import jax.numpy as jnp
from jax import lax
from jax.experimental import pallas as pl
from jax.experimental.pallas import tpu as pltpu

F32 = jnp.float32
BF16 = jnp.bfloat16
SDS = jax.ShapeDtypeStruct
MESH_ID = pl.DeviceIdType.MESH

D_MODEL = 1024
DEPTH = 2
A_WIDTH = 256
A_GROUPS = 4
B_WIDTH = 256
C_WIDTH = 512
C_HEADS = 8
D_IN = 3848
D_IN_PAD = 4096
CHUNK = 128
SUB = 16
SUB_SHIFT = 4
NORM_EPS = 1e-6
F_FLOOR = 1e-30
COL_AU, COL_AV, COL_AZ = 0, 256, 512
COL_BQ, COL_BF, COL_BI, COL_BZ = 768, 1024, 1280, 1536
COL_CQ, COL_CK, COL_CV, COL_CZ, COL_CF = 1792, 2304, 2816, 3328, 3840
HEAD_LANES = 128
Q_SCALE = 0.125
ADAM_LR, ADAM_B1, ADAM_B2, ADAM_EPS, ADAM_WD, ADAM_STEP = 0.001, 0.9, 0.999, 1e-08, 0.01, 10
ADAM_C1 = 1.0 - ADAM_B1 ** ADAM_STEP
ADAM_C2 = 1.0 - ADAM_B2 ** ADAM_STEP
VMEM_LIMIT = 56 * 1024 * 1024
ADAMW_BLOCK_BYTES = 1 << 20
N_CHIPS = 4
N_DEV = 8

SMALL_PARAMS = (
    ("norm_g", (DEPTH, D_MODEL)), ("gmlp_ln_g", (DEPTH, 4, 64)), ("gmlp_ln_b", (DEPTH, 4, 64)),
    ("gmlp_w_s", (DEPTH, 4, 128, 128)), ("gmlp_b_s", (DEPTH, 4, 128)), ("hgrn_lb", (DEPTH, 256)),
    ("hgrn_onorm_g", (DEPTH, 64)), ("fox_b_f", (DEPTH, 8)), ("final_norm_g", (D_MODEL,)),
)


def _tile(n, pref):
    t = min(n, pref)
    assert n % t == 0, (n, pref)
    return t


def _params(*sem):
    return pltpu.CompilerParams(dimension_semantics=sem, vmem_limit_bytes=VMEM_LIMIT)


def _dot(a, b):
    return jnp.dot(a, b, preferred_element_type=F32)


def _dot_nt(a, b):
    return lax.dot_general(a, b, (((1,), (1,)), ((), ())), preferred_element_type=F32)


def _dot_tn(a, b):
    return lax.dot_general(a, b, (((0,), (0,)), ((), ())), preferred_element_type=F32)


def _split3(x):
    hi = x.astype(BF16)
    r = x - hi.astype(F32)
    mid = r.astype(BF16)
    lo = (r - mid.astype(F32)).astype(BF16)
    return hi, mid, lo


def _dot3_left(c, x):
    hi, mid, lo = _split3(x)
    return _dot(c, hi) + _dot(c, mid) + _dot(c, lo)


def _sigmoid(x):
    return jax.nn.sigmoid(x)


def _silu_and_grad(x):
    s = _sigmoid(x)
    return x * s, s * (1.0 + x * (1.0 - s))


_GELU_C = math.sqrt(2.0 / math.pi)


def _gelu_and_grad(x):
    inner = _GELU_C * (x + 0.044715 * x * x * x)
    t = jnp.tanh(inner)
    y = 0.5 * x * (1.0 + t)
    dy = 0.5 * (1.0 + t) + 0.5 * x * (1.0 - t * t) * _GELU_C * (1.0 + 3.0 * 0.044715 * x * x)
    return y, dy


def _lane(shape):
    return lax.broadcasted_iota(jnp.int32, shape, 1)


def _row(shape):
    return lax.broadcasted_iota(jnp.int32, shape, 0)


def _gsum64(x):
    lo = _lane(x.shape) < 64
    s0 = jnp.sum(jnp.where(lo, x, 0.0), axis=-1, keepdims=True)
    s1 = jnp.sum(jnp.where(lo, 0.0, x), axis=-1, keepdims=True)
    return jnp.where(lo, s0, s1)


def _colreduce(x, op):
    parts = [x[r:r + 8, :] for r in range(0, x.shape[0], 8)]
    while len(parts) > 1:
        pairs = [op(parts[k], parts[k + 1]) for k in range(0, len(parts) - 1, 2)]
        parts = pairs + ([parts[-1]] if len(parts) % 2 else [])
    red = jnp.max if op is jnp.maximum else jnp.sum
    return red(parts[0], axis=0, keepdims=True)


def _block_diag64(dtype=BF16):
    r, c = _row((128, 128)), _lane((128, 128))
    return jnp.where((r >> 6) == (c >> 6), 1.0, 0.0).astype(dtype)


def _inproj(x, g, w, tag):
    T, D = x.shape
    DP = w.shape[1]
    tm, tn = _tile(T, 512), _tile(DP, 1024)

    def body(x_ref, g_ref, w_ref, h_ref, p_ref):
        @pl.when(pl.program_id(1) == 0)
        def _():
            xv = x_ref[...]
            r = lax.rsqrt(jnp.mean(xv * xv, axis=-1, keepdims=True) + NORM_EPS)
            h_ref[...] = (xv * r * g_ref[...]).astype(BF16)

        p_ref[...] = _dot(h_ref[...], w_ref[...])

    return pl.pallas_call(
        body, name=f"inproj_{tag}", grid=(T // tm, DP // tn),
        in_specs=[pl.BlockSpec((tm, D), lambda i, j: (i, 0)), pl.BlockSpec((1, D), lambda i, j: (0, 0)),
                  pl.BlockSpec((D, tn), lambda i, j: (0, j))],
        out_specs=[pl.BlockSpec((tm, D), lambda i, j: (i, 0)), pl.BlockSpec((tm, tn), lambda i, j: (i, j))],
        out_shape=[SDS((T, D), BF16), SDS((T, DP), F32)],
        compiler_params=_params("parallel", "arbitrary"),
    )(x, g, w)


def _outproj(x, ya, yb, yc, wo, tag):
    T, D = x.shape
    tm = _tile(T, 512)

    def body(x_ref, ya_ref, yb_ref, yc_ref, wo_ref, o_ref):
        acc = x_ref[...] + _dot(ya_ref[...], wo_ref[0:A_WIDTH, :])
        acc = acc + _dot(yb_ref[...], wo_ref[A_WIDTH:A_WIDTH + B_WIDTH, :])
        o_ref[...] = acc + _dot(yc_ref[...], wo_ref[A_WIDTH + B_WIDTH:, :])

    row = lambda w: pl.BlockSpec((tm, w), lambda i: (i, 0))
    return pl.pallas_call(
        body, name=f"outproj_{tag}", grid=(T // tm,),
        in_specs=[row(D), row(A_WIDTH), row(B_WIDTH), row(C_WIDTH), pl.BlockSpec(wo.shape, lambda i: (0, 0))],
        out_specs=row(D), out_shape=SDS((T, D), F32), compiler_params=_params("parallel"),
    )(x, ya, yb, yc, wo)


def _outproj_bwd(dx, ya, yb, yc, wo, tag):
    T, D = dx.shape
    DM = wo.shape[0]
    tm = _tile(T, 512)

    def body(dx_ref, ya_ref, yb_ref, yc_ref, wo_ref, dy_ref, dwo_ref):
        @pl.when(pl.program_id(0) == 0)
        def _():
            dwo_ref[...] = jnp.zeros_like(dwo_ref)

        dxb = dx_ref[...].astype(BF16)
        dy_ref[...] = _dot_nt(dxb, wo_ref[...])
        dwo_ref[0:A_WIDTH, :] += _dot_tn(ya_ref[...], dxb)
        dwo_ref[A_WIDTH:A_WIDTH + B_WIDTH, :] += _dot_tn(yb_ref[...], dxb)
        dwo_ref[A_WIDTH + B_WIDTH:, :] += _dot_tn(yc_ref[...], dxb)

    row = lambda w: pl.BlockSpec((tm, w), lambda i: (i, 0))
    return pl.pallas_call(
        body, name=f"outproj_bwd_{tag}", grid=(T // tm,),
        in_specs=[row(D), row(A_WIDTH), row(B_WIDTH), row(C_WIDTH), pl.BlockSpec(wo.shape, lambda i: (0, 0))],
        out_specs=[row(DM), pl.BlockSpec((DM, D), lambda i: (0, 0))],
        out_shape=[SDS((T, DM), F32), SDS((DM, D), F32)], compiler_params=_params("arbitrary"),
    )(dx, ya, yb, yc, wo)


def _dw_in(h, dproj, tag, ride=None):
    T, D = h.shape
    DP = dproj.shape[1]
    tm, tn = _tile(T, 512), _tile(DP, 1024)
    grid = (DP // tn, T // tm)

    def body(h_ref, dp_ref, *rest):
        ride_srcs, (dw_ref,), ride_dsts, _, ride_sems = _ride_refs(ride, rest, 1, 0)
        _ride_start(ride, grid, ride_srcs, ride_dsts, ride_sems)

        @pl.when(pl.program_id(1) == 0)
        def _():
            dw_ref[...] = jnp.zeros_like(dw_ref)

        dw_ref[...] += _dot_tn(dp_ref[...], h_ref[...])
        _ride_wait(ride, grid, ride_srcs, ride_dsts, ride_sems)

    extra = ride or _ChipExchange("gather", ())
    out = pl.pallas_call(
        body, name=f"dw_in_{tag}", grid=grid,
        in_specs=[pl.BlockSpec((tm, D), lambda j, i: (i, 0)), pl.BlockSpec((tm, tn), lambda j, i: (i, j))] + extra.in_specs,
        out_specs=[pl.BlockSpec((tn, D), lambda j, i: (j, 0))] + extra.out_specs,
        out_shape=[SDS((DP, D), F32)] + extra.out_shape, scratch_shapes=extra.scratch if ride else [],
        compiler_params=pltpu.CompilerParams(dimension_semantics=("arbitrary", "arbitrary"), vmem_limit_bytes=VMEM_LIMIT,
                                             has_side_effects=bool(ride)),
    )(h, dproj, *extra.sources)
    return out if ride else out[0]


def _dx_in(x, g, dres, dproj, w, tag, ride=None):
    T, D = x.shape
    DP = w.shape[1]
    tm, tk = _tile(T, 512), _tile(DP, 1024)
    nk = DP // tk

    grid = (T // tm, nk)

    def body(x_ref, g_ref, dres_ref, dp_ref, w_ref, *rest):
        ride_srcs, (dx_ref, dg_ref), ride_dsts, (acc_ref,), ride_sems = _ride_refs(ride, rest, 2, 1)
        i, k = pl.program_id(0), pl.program_id(1)
        _ride_start(ride, grid, ride_srcs, ride_dsts, ride_sems)

        @pl.when((i == 0) & (k == 0))
        def _():
            dg_ref[...] = jnp.zeros_like(dg_ref)

        @pl.when(k == 0)
        def _():
            acc_ref[...] = jnp.zeros_like(acc_ref)

        acc_ref[...] += _dot_nt(dp_ref[...], w_ref[...])

        @pl.when(k == nk - 1)
        def _():
            xv = x_ref[...]
            r = lax.rsqrt(jnp.mean(xv * xv, axis=-1, keepdims=True) + NORM_EPS)
            xh = xv * r
            dh = acc_ref[...]
            dg_ref[...] += jnp.sum(dh * xh, axis=0, keepdims=True)
            dxh = dh * g_ref[...]
            dx_ref[...] = dres_ref[...] + r * (dxh - xh * jnp.mean(dxh * xh, axis=-1, keepdims=True))

        _ride_wait(ride, grid, ride_srcs, ride_dsts, ride_sems)

    extra = ride or _ChipExchange("gather", ())
    return pl.pallas_call(
        body, name=f"dx_in_{tag}", grid=grid,
        in_specs=[pl.BlockSpec((tm, D), lambda i, k: (i, 0)), pl.BlockSpec((1, D), lambda i, k: (0, 0)),
                  pl.BlockSpec((tm, D), lambda i, k: (i, 0)), pl.BlockSpec((tm, tk), lambda i, k: (i, k)),
                  pl.BlockSpec((D, tk), lambda i, k: (0, k))] + extra.in_specs,
        out_specs=[pl.BlockSpec((tm, D), lambda i, k: (i, 0)), pl.BlockSpec((1, D), lambda i, k: (0, 0))] + extra.out_specs,
        out_shape=[SDS((T, D), F32), SDS((1, D), F32)] + extra.out_shape,
        scratch_shapes=[pltpu.VMEM((tm, D), F32)] + (extra.scratch if ride else []),
        compiler_params=pltpu.CompilerParams(dimension_semantics=("arbitrary", "arbitrary"), vmem_limit_bytes=VMEM_LIMIT,
                                             has_side_effects=bool(ride)),
    )(x, g, dres, dproj, w, *extra.sources)


def _loss_head(x, g, tgt):
    T, D = x.shape
    tm = _tile(T, 512)

    def body(x_ref, g_ref, t_ref, dx_ref, loss_ref, dg_ref):
        @pl.when(pl.program_id(0) == 0)
        def _():
            loss_ref[...] = jnp.zeros_like(loss_ref)
            dg_ref[...] = jnp.zeros_like(dg_ref)

        xv = x_ref[...]
        r = lax.rsqrt(jnp.mean(xv * xv, axis=-1, keepdims=True) + NORM_EPS)
        xh = xv * r
        gv = g_ref[...]
        err = xh * gv - t_ref[...]
        tok = jnp.mean(err * err, axis=-1, keepdims=True)
        loss_ref[...] += 0.5 * jnp.sum(tok, axis=0, keepdims=True)
        dy = err * (1.0 / D)
        dg_ref[...] += jnp.sum(dy * xh, axis=0, keepdims=True)
        dxh = dy * gv
        dx_ref[...] = r * (dxh - xh * jnp.mean(dxh * xh, axis=-1, keepdims=True))

    row = pl.BlockSpec((tm, D), lambda i: (i, 0))
    return pl.pallas_call(
        body, name="loss_head", grid=(T // tm,),
        in_specs=[row, pl.BlockSpec((1, D), lambda i: (0, 0)), row],
        out_specs=[row, pl.BlockSpec((1, 128), lambda i: (0, 0)), pl.BlockSpec((1, D), lambda i: (0, 0))],
        out_shape=[SDS((T, D), F32), SDS((1, 128), F32), SDS((1, D), F32)], compiler_params=_params("arbitrary"),
    )(x, g, tgt)


def _gmlp_core(u, v, lng, lnb, wm_ref, bst_ref, pair):
    ug, dug = _gelu_and_grad(u)
    vg, dvg = _gelu_and_grad(v)
    mu = _gsum64(vg) * (1.0 / 64)
    d = vg - mu
    var = _gsum64(d * d) * (1.0 / 64)
    rstd = lax.rsqrt(var + NORM_EPS)
    xh = d * rstd
    vn = xh * lng + lnb
    vnb = vn.astype(BF16)
    lo = _lane(u.shape) < 64
    g0, g1 = 2 * pair, 2 * pair + 1
    mixed = jnp.where(lo, _dot(wm_ref[g0], vnb) + bst_ref[:, g0:g0 + 1], _dot(wm_ref[g1], vnb) + bst_ref[:, g1:g1 + 1])
    return ug, dug, dvg, rstd, xh, vnb, mixed, lo


def _gmlp_fwd(proj, lng, lnb, wm, bst, tag):
    T = proj.shape[0]

    def body(u_ref, v_ref, z_ref, lng_ref, lnb_ref, wm_ref, bst_ref, y_ref):
        for pair in range(2):
            sl = slice(128 * pair, 128 * pair + 128)
            ug, _, _, _, _, _, mixed, _ = _gmlp_core(u_ref[:, sl], v_ref[:, sl], lng_ref[:, sl], lnb_ref[:, sl],
                                                     wm_ref, bst_ref, pair)
            sz, _ = _silu_and_grad(z_ref[:, sl])
            y_ref[:, sl] = (ug * mixed * sz).astype(BF16)

    col = lambda c: pl.BlockSpec((CHUNK, A_WIDTH), lambda i, c=c: (i, c // A_WIDTH))
    full = lambda a: pl.BlockSpec(a.shape, lambda i, n=a.ndim: (0,) * n)
    return pl.pallas_call(
        body, name=f"gmlp_fwd_{tag}", grid=(T // CHUNK,),
        in_specs=[col(COL_AU), col(COL_AV), col(COL_AZ), full(lng), full(lnb), full(wm), full(bst)],
        out_specs=pl.BlockSpec((CHUNK, A_WIDTH), lambda i: (i, 0)), out_shape=SDS((T, A_WIDTH), BF16),
        compiler_params=_params("parallel"),
    )(proj, proj, proj, lng, lnb, wm, bst)


def _gmlp_bwd(proj, dy, lng, lnb, wm, wmt, bst, tag):
    T = proj.shape[0]
    n = T // CHUNK

    def body(u_ref, v_ref, z_ref, dy_ref, lng_ref, lnb_ref, wm_ref, wmt_ref, bst_ref,
             da_ref, dwm_ref, dbst_ref, dlng_ref, dlnb_ref):
        @pl.when(pl.program_id(0) == 0)
        def _():
            dwm_ref[...] = jnp.zeros_like(dwm_ref)
            dbst_ref[...] = jnp.zeros_like(dbst_ref)
            dlng_ref[...] = jnp.zeros_like(dlng_ref)
            dlnb_ref[...] = jnp.zeros_like(dlnb_ref)

        lane = _lane((CHUNK, 128))
        dbst = dbst_ref[...]
        for pair in range(2):
            sl = slice(128 * pair, 128 * pair + 128)
            lng_p = lng_ref[:, sl]
            ug, dug, dvg, rstd, xh, vnb, mixed, lo = _gmlp_core(u_ref[:, sl], v_ref[:, sl], lng_p, lnb_ref[:, sl],
                                                                wm_ref, bst_ref, pair)
            sz, dsz = _silu_and_grad(z_ref[:, sl])
            dyv = dy_ref[:, sl]
            out = ug * mixed
            dz = dyv * out * dsz
            dout = dyv * sz
            du = dout * mixed * dug
            dmix = dout * ug
            g0, g1 = 2 * pair, 2 * pair + 1
            dm0 = jnp.where(lo, dmix, 0.0)
            dm1 = jnp.where(lo, 0.0, dmix)
            dbst = dbst + jnp.where(lane == g0, jnp.sum(dm0, axis=-1, keepdims=True), 0.0)
            dbst = dbst + jnp.where(lane == g1, jnp.sum(dm1, axis=-1, keepdims=True), 0.0)
            dwm_ref[g0] += _dot_nt(dm0.astype(BF16), vnb)
            dwm_ref[g1] += _dot_nt(dm1.astype(BF16), vnb)
            dmb = dmix.astype(BF16)
            dvn = jnp.where(lo, _dot(wmt_ref[g0], dmb), _dot(wmt_ref[g1], dmb))
            dlng_ref[:, sl] += jnp.sum(dvn * xh, axis=0, keepdims=True)
            dlnb_ref[:, sl] += jnp.sum(dvn, axis=0, keepdims=True)
            dxh = dvn * lng_p
            m1 = _gsum64(dxh) * (1.0 / 64)
            m2 = _gsum64(dxh * xh) * (1.0 / 64)
            dv = rstd * (dxh - m1 - xh * m2) * dvg
            da_ref[:, COL_AU + 128 * pair:COL_AU + 128 * pair + 128] = du.astype(BF16)
            da_ref[:, COL_AV + 128 * pair:COL_AV + 128 * pair + 128] = dv.astype(BF16)
            da_ref[:, COL_AZ + 128 * pair:COL_AZ + 128 * pair + 128] = dz.astype(BF16)
        dbst_ref[...] = dbst

        @pl.when(pl.program_id(0) == n - 1)
        def _():
            causal = _lane((CHUNK, CHUNK)) <= _row((CHUNK, CHUNK))
            for g in range(A_GROUPS):
                dwm_ref[g] = jnp.where(causal, dwm_ref[g], 0.0)

    col = lambda c: pl.BlockSpec((CHUNK, A_WIDTH), lambda i, c=c: (i, c // A_WIDTH))
    full = lambda a: pl.BlockSpec(a.shape, lambda i, n=a.ndim: (0,) * n)
    acc = lambda s: pl.BlockSpec(s, lambda i, n=len(s): (0,) * n)
    return pl.pallas_call(
        body, name=f"gmlp_bwd_{tag}", grid=(n,),
        in_specs=[col(COL_AU), col(COL_AV), col(COL_AZ), pl.BlockSpec((CHUNK, A_WIDTH), lambda i: (i, 0)),
                  full(lng), full(lnb), full(wm), full(wmt), full(bst)],
        out_specs=[pl.BlockSpec((CHUNK, 3 * A_WIDTH), lambda i: (i, 0)), acc((A_GROUPS, CHUNK, CHUNK)),
                   acc((CHUNK, 128)), acc((1, A_WIDTH)), acc((1, A_WIDTH))],
        out_shape=[SDS((T, 3 * A_WIDTH), BF16), SDS((A_GROUPS, CHUNK, CHUNK), F32), SDS((CHUNK, 128), F32),
                   SDS((1, A_WIDTH), F32), SDS((1, A_WIDTH), F32)],
        compiler_params=_params("arbitrary"),
    )(proj, proj, proj, dy, lng, lnb, wm, wmt, bst)


def _hgrn_consts():
    r, c = _row((CHUNK, CHUNK)), _lane((CHUNK, CHUNK))
    same = (r >> SUB_SHIFT) == (c >> SUB_SHIFT)
    lsub = jnp.where(same & (c <= r), 1.0, 0.0).astype(BF16)
    usub = jnp.where(same & (c >= r), 1.0, 0.0).astype(BF16)
    bsub = jnp.where(same, 1.0, 0.0).astype(BF16)
    return lsub, usub, bsub


def _hgrn_gates(qv, zf, lbp):
    sq, dsq = _silu_and_grad(qv)
    qt = sq * Q_SCALE
    sg = _sigmoid(zf)
    sgn = _sigmoid(-zf)
    f = lbp + (1.0 - lbp) * sg
    g = jnp.log(jnp.maximum(f, F_FLOOR))
    kf = (1.0 - lbp) * sgn
    return qt, dsq, sg, sgn, f, g, kf


def _hgrn_intra_fwd(qt, kf, b, v, mbd):
    rid = _row((SUB, 128))
    parts = []
    for s in range(SUB):
        e = jnp.exp(jnp.minimum(b - b[s:s + 1, :], 0.0))
        parts.append(jnp.where(rid >= s, qt * kf[s:s + 1, :] * e, 0.0))
    a = _dot(jnp.concatenate(parts, axis=0).astype(BF16), mbd)
    o = jnp.zeros((SUB, 128), F32)
    for s in range(SUB):
        o = o + a[SUB * s:SUB * s + SUB, :] * v[s:s + 1, :]
    return o


def _hgrn_intra_bwd(qt, kf, b, v, do, mbd, rsum):
    rid = _row((SUB, 128))
    ps, das, kes, es = [], [], [], []
    for s in range(SUB):
        e = jnp.where(rid >= s, jnp.exp(jnp.minimum(b - b[s:s + 1, :], 0.0)), 0.0)
        ke = kf[s:s + 1, :] * e
        es.append(e)
        kes.append(ke)
        ps.append(qt * ke)
        das.append(do * v[s:s + 1, :])
    a = _dot(jnp.concatenate(ps, axis=0).astype(BF16), mbd)
    da = _dot(jnp.concatenate(das, axis=0).astype(BF16), mbd)
    dqt = jnp.zeros((SUB, 128), F32)
    xs, ys = [], []
    for s in range(SUB):
        da_s = da[SUB * s:SUB * s + SUB, :]
        dqt = dqt + da_s * kes[s]
        xs.append(a[SUB * s:SUB * s + SUB, :] * do)
        ys.append(da_s * qt * es[s])
    xh = jnp.concatenate(xs, axis=0)
    yh = jnp.concatenate(ys, axis=0)
    xhi = xh.astype(BF16)
    yhi = yh.astype(BF16)
    dv = _dot(rsum, xhi) + _dot(rsum, (xh - xhi.astype(F32)).astype(BF16))
    dkf = _dot(rsum, yhi) + _dot(rsum, (yh - yhi.astype(F32)).astype(BF16))
    return dqt, dkf, dv


def _hgrn_norm_gate(o, z, onorm):
    ms = _gsum64(o * o) * (1.0 / 64)
    r = lax.rsqrt(ms + NORM_EPS)
    xh = o * r
    sz, dsz = _silu_and_grad(z)
    return xh, r, sz, dsz, xh * onorm


def _hgrn_fwd(proj, lb, onorm, tag):
    T = proj.shape[0]
    n = T // CHUNK
    nsub = CHUNK // SUB

    def body(q_ref, f_ref, i_ref, z_ref, lb_ref, on_ref, y_ref, o_ref, s0_ref, st_ref):
        @pl.when(pl.program_id(0) == 0)
        def _():
            st_ref[...] = jnp.zeros_like(st_ref)

        lsub, _, bsub = _hgrn_consts()
        mbd = _block_diag64()
        bdmask = mbd > 0
        rid = _row((CHUNK, 128))
        for pair in range(2):
            sl = slice(128 * pair, 128 * pair + 128)
            qt, _, _, _, _, g, kf = _hgrn_gates(q_ref[:, sl], f_ref[:, sl], lb_ref[:, sl])
            v = i_ref[:, sl]
            b = _dot3_left(lsub, g)
            bl = _dot3_left(bsub, g)
            qh = (qt * jnp.exp(b)).astype(BF16)
            kh = kf * jnp.exp(bl - b)
            dec = jnp.exp(bl)
            vtb = v.T.astype(BF16)
            st = st_ref[pair]
            s0_ref[0, pair] = st
            outs = []
            for sub in range(nsub):
                rs = slice(SUB * sub, SUB * sub + SUB)
                o_inter = _dot_nt(qh[rs], st.astype(BF16))
                outs.append(o_inter + _hgrn_intra_fwd(qt[rs], kf[rs], b[rs], v[rs], mbd))
                khm = jnp.where((rid >> SUB_SHIFT) == sub, kh, 0.0).astype(BF16)
                st = jnp.where(bdmask, st * dec[SUB * sub:SUB * sub + 1, :] + _dot(vtb, khm), 0.0)
            st_ref[pair] = st
            o = jnp.concatenate(outs, axis=0)
            o_ref[:, sl] = o
            _, _, sz, _, on = _hgrn_norm_gate(o, z_ref[:, sl], on_ref[:, sl])
            y_ref[:, sl] = (on * sz).astype(BF16)

    col = lambda c: pl.BlockSpec((CHUNK, B_WIDTH), lambda i, c=c: (i, c // B_WIDTH))
    full = lambda a: pl.BlockSpec(a.shape, lambda i, n=a.ndim: (0,) * n)
    return pl.pallas_call(
        body, name=f"hgrn_fwd_{tag}", grid=(n,),
        in_specs=[col(COL_BQ), col(COL_BF), col(COL_BI), col(COL_BZ), full(lb), full(onorm)],
        out_specs=[pl.BlockSpec((CHUNK, B_WIDTH), lambda i: (i, 0)), pl.BlockSpec((CHUNK, B_WIDTH), lambda i: (i, 0)),
                   pl.BlockSpec((1, 2, 128, 128), lambda i: (i, 0, 0, 0))],
        out_shape=[SDS((T, B_WIDTH), BF16), SDS((T, B_WIDTH), F32), SDS((n, 2, 128, 128), F32)],
        scratch_shapes=[pltpu.VMEM((2, 128, 128), F32)], compiler_params=_params("arbitrary"),
    )(proj, proj, proj, proj, lb, onorm)


def _hgrn_bwd(proj, dy, o_saved, s0, lb, onorm, tag):
    T = proj.shape[0]
    n = T // CHUNK
    nsub = CHUNK // SUB

    def body(q_ref, f_ref, i_ref, z_ref, dy_ref, o_ref, s0_ref, lb_ref, on_ref,
             db_ref, dlb_ref, don_ref, dst_ref, sts_ref):
        @pl.when(pl.program_id(0) == 0)
        def _():
            dst_ref[...] = jnp.zeros_like(dst_ref)
            dlb_ref[...] = jnp.zeros_like(dlb_ref)
            don_ref[...] = jnp.zeros_like(don_ref)

        lsub, usub, bsub = _hgrn_consts()
        mbd = _block_diag64()
        bdmask = mbd > 0
        rid = _row((CHUNK, 128))
        rsum = jnp.where((_lane((SUB, SUB * SUB)) >> SUB_SHIFT) == _row((SUB, SUB * SUB)), 1.0, 0.0).astype(BF16)
        for pair in range(2):
            sl = slice(128 * pair, 128 * pair + 128)
            lbp = lb_ref[:, sl]
            qv, zf = q_ref[:, sl], f_ref[:, sl]
            qt, dsq, sg, sgn, f, g, kf = _hgrn_gates(qv, zf, lbp)
            v = i_ref[:, sl]
            b = _dot3_left(lsub, g)
            bl = _dot3_left(bsub, g)
            eb = jnp.exp(b)
            ekb = jnp.exp(bl - b)
            qh = qt * eb
            kh = kf * ekb
            dec = jnp.exp(bl)
            vtb = v.T.astype(BF16)
            onp = on_ref[:, sl]
            ov = o_ref[:, sl]
            xh, r, sz, dsz, on = _hgrn_norm_gate(ov, z_ref[:, sl], onp)
            dyv = dy_ref[:, sl]
            dz = dyv * on * dsz
            don = dyv * sz
            cn = jnp.sum(don * xh, axis=0, keepdims=True)
            don_ref[...] += cn + pltpu.roll(cn, 64, axis=1)
            dxo = don * onp
            do = r * (dxo - xh * (_gsum64(dxo * xh) * (1.0 / 64)))
            dotb = do.T.astype(BF16)
            st = s0_ref[0, pair]
            for sub in range(nsub):
                sts_ref[sub] = st
                khm = jnp.where((rid >> SUB_SHIFT) == sub, kh, 0.0).astype(BF16)
                st = jnp.where(bdmask, st * dec[SUB * sub:SUB * sub + 1, :] + _dot(vtb, khm), 0.0)
            gst = dst_ref[pair]
            dqt_p, dkf_p, dv_p, dbl_p = [None] * nsub, [None] * nsub, [None] * nsub, [None] * nsub
            for sub in reversed(range(nsub)):
                rs = slice(SUB * sub, SUB * sub + SUB)
                st_in = sts_ref[sub]
                gb = gst.astype(BF16)
                dob = do[rs].astype(BF16)
                dqh = _dot(dob, st_in.astype(BF16))
                dkh = _dot(v[rs].astype(BF16), gb)
                dv_inter = _dot_nt(kh[rs].astype(BF16), gb)
                ddec = jnp.sum(gst * st_in, axis=0, keepdims=True)
                dec_row = dec[SUB * sub:SUB * sub + 1, :]
                qhm = jnp.where((rid >> SUB_SHIFT) == sub, qh, 0.0).astype(BF16)
                gst = jnp.where(bdmask, gst * dec_row + _dot(dotb, qhm), 0.0)
                dqt_i, dkf_i, dv_i = _hgrn_intra_bwd(qt[rs], kf[rs], b[rs], v[rs], do[rs], mbd, rsum)
                dkf_inter = dkh * ekb[rs]
                dqt_p[sub] = dqh * eb[rs] + dqt_i
                dkf_p[sub] = dkf_inter + dkf_i
                dv_p[sub] = dv_inter + dv_i
                row = jnp.sum(kf[rs] * dkf_inter, axis=0, keepdims=True) + ddec * dec_row
                dbl_p[sub] = jnp.broadcast_to(row, (SUB, 128))
            dst_ref[pair] = gst
            dqt = jnp.concatenate(dqt_p, axis=0)
            dkf = jnp.concatenate(dkf_p, axis=0)
            dv = jnp.concatenate(dv_p, axis=0)
            dg = _dot3_left(usub, qt * dqt - kf * dkf) + jnp.concatenate(dbl_p, axis=0)
            df = jnp.where(f > F_FLOOR, dg / f, 0.0)
            dlb_ref[:, sl] += jnp.sum(df * (1.0 - sg) - dkf * sgn, axis=0, keepdims=True)
            dfl = (1.0 - lbp) * sg * sgn * (df - dkf)
            dq = dqt * Q_SCALE * dsq
            db_ref[:, 0 * B_WIDTH + 128 * pair:0 * B_WIDTH + 128 * pair + 128] = dq.astype(BF16)
            db_ref[:, 1 * B_WIDTH + 128 * pair:1 * B_WIDTH + 128 * pair + 128] = dfl.astype(BF16)
            db_ref[:, 2 * B_WIDTH + 128 * pair:2 * B_WIDTH + 128 * pair + 128] = dv.astype(BF16)
            db_ref[:, 3 * B_WIDTH + 128 * pair:3 * B_WIDTH + 128 * pair + 128] = dz.astype(BF16)

    rev = lambda c: pl.BlockSpec((CHUNK, B_WIDTH), lambda i, c=c: (n - 1 - i, c // B_WIDTH))
    full = lambda a: pl.BlockSpec(a.shape, lambda i, n_=a.ndim: (0,) * n_)
    acc = lambda s: pl.BlockSpec(s, lambda i, n_=len(s): (0,) * n_)
    return pl.pallas_call(
        body, name=f"hgrn_bwd_{tag}", grid=(n,),
        in_specs=[rev(COL_BQ), rev(COL_BF), rev(COL_BI), rev(COL_BZ),
                  pl.BlockSpec((CHUNK, B_WIDTH), lambda i: (n - 1 - i, 1)),
                  pl.BlockSpec((CHUNK, B_WIDTH), lambda i: (n - 1 - i, 0)),
                  pl.BlockSpec((1, 2, 128, 128), lambda i: (n - 1 - i, 0, 0, 0)), full(lb), full(onorm)],
        out_specs=[pl.BlockSpec((CHUNK, 4 * B_WIDTH), lambda i: (n - 1 - i, 0)), acc((1, B_WIDTH)), acc((1, 128))],
        out_shape=[SDS((T, 4 * B_WIDTH), BF16), SDS((1, B_WIDTH), F32), SDS((1, 128), F32)],
        scratch_shapes=[pltpu.VMEM((2, 128, 128), F32), pltpu.VMEM((nsub, 128, 128), F32)],
        compiler_params=_params("arbitrary"),
    )(proj, proj, proj, proj, dy, o_saved, s0, lb, onorm)


def _lb_fwd(hgrn_lb):
    assert hgrn_lb.shape[0] == 2

    def body(x_ref, o_ref):
        x0, x1 = x_ref[0:1, :], x_ref[1:2, :]
        m = jnp.maximum(x0, x1)
        e0, e1 = jnp.exp(x0 - m), jnp.exp(x1 - m)
        p0, p1 = e0 / (e0 + e1), e1 / (e0 + e1)
        o_ref[0:1, :] = jnp.clip(p0 - p0, 0.0, 1.0 - 1e-6)
        o_ref[1:2, :] = jnp.clip((p0 + p1) - p0, 0.0, 1.0 - 1e-6)

    return pl.pallas_call(body, name="lb_fwd", out_shape=SDS(hgrn_lb.shape, F32))(hgrn_lb)


def _lb_bwd(hgrn_lb, dlb):
    def body(x_ref, d_ref, o_ref):
        x0, x1 = x_ref[0:1, :], x_ref[1:2, :]
        m = jnp.maximum(x0, x1)
        e0, e1 = jnp.exp(x0 - m), jnp.exp(x1 - m)
        p0, p1 = e0 / (e0 + e1), e1 / (e0 + e1)
        val = (p0 + p1) - p0
        dp1 = jnp.where((val > 0.0) & (val < 1.0 - 1e-6), d_ref[1:2, :], 0.0)
        inner = p1 * dp1
        o_ref[0:1, :] = p0 * (0.0 - inner)
        o_ref[1:2, :] = p1 * (dp1 - inner)

    return pl.pallas_call(body, name="lb_bwd", out_shape=SDS(hgrn_lb.shape, F32))(hgrn_lb, dlb)


def _fox_prep(proj, bf, tag):
    T = proj.shape[0]
    n = T // CHUNK

    def body(q0_ref, q1_ref, k0_ref, k1_ref, v0_ref, v1_ref, fl_ref, bf_ref, qo_ref, ko_ref, vt_ref, carry_ref):
        for p, v_ref in enumerate((v0_ref, v0_ref, v1_ref, v1_ref)):
            vt_ref[p, 0] = v_ref[:, 128 * (p % 2):128 * (p % 2) + 128].T.astype(BF16)

        @pl.when(pl.program_id(0) == 0)
        def _():
            carry_ref[...] = jnp.zeros_like(carry_ref)

        ltri = jnp.where(_lane((CHUNK, CHUNK)) <= _row((CHUNK, CHUNK)), 1.0, 0.0).astype(BF16)
        lf = jax.nn.log_sigmoid(fl_ref[...] + bf_ref[...])
        c = _dot3_left(ltri, lf) + carry_ref[...]
        carry_ref[...] = c[CHUNK - 1:CHUNK, :]
        lane = _lane((CHUNK, 128))
        feat = lane < 64
        ones_q = (lane >= 67) & (lane <= 69)
        ones_k = (lane >= 64) & (lane <= 66)
        qrefs, krefs = (q0_ref, q1_ref), (k0_ref, k1_ref)
        for h in range(C_HEADS):
            blk = slice(128 * ((h // 2) % 2), 128 * ((h // 2) % 2) + 128)
            qp, kp = qrefs[h // 4][:, blk], krefs[h // 4][:, blk]
            if h % 2:
                qp, kp = pltpu.roll(qp, 64, axis=1), pltpu.roll(kp, 64, axis=1)
            ch = jnp.broadcast_to(c[:, h:h + 1], (CHUNK, 128))
            hi = ch.astype(BF16).astype(F32)
            r1 = ch - hi
            mid = r1.astype(BF16).astype(F32)
            lo = r1 - mid
            aq = jnp.where(lane == 64, hi, jnp.where(lane == 65, mid, jnp.where(lane == 66, lo,
                           jnp.where(ones_q, 1.0, 0.0))))
            ak = jnp.where(lane == 67, -hi, jnp.where(lane == 68, -mid, jnp.where(lane == 69, -lo,
                           jnp.where(ones_k, 1.0, 0.0))))
            qo_ref[:, 128 * h:128 * h + 128] = jnp.where(feat, qp * Q_SCALE, aq).astype(BF16)
            ko_ref[:, 128 * h:128 * h + 128] = jnp.where(feat, kp, ak).astype(BF16)

    w = 256
    col = lambda c: pl.BlockSpec((CHUNK, w), lambda i, c=c: (i, c // w))
    return pl.pallas_call(
        body, name=f"fox_prep_{tag}", grid=(n,),
        in_specs=[col(COL_CQ), col(COL_CQ + w), col(COL_CK), col(COL_CK + w), col(COL_CV), col(COL_CV + w),
                  pl.BlockSpec((CHUNK, 128), lambda i: (i, COL_CF // 128)), pl.BlockSpec((1, 128), lambda i: (0, 0))],
        out_specs=[pl.BlockSpec((CHUNK, C_HEADS * 128), lambda i: (i, 0))] * 2
        + [pl.BlockSpec((C_HEADS // 2, 1, 128, CHUNK), lambda i: (0, i, 0, 0))],
        out_shape=[SDS((T, C_HEADS * 128), BF16)] * 2 + [SDS((C_HEADS // 2, n, 128, CHUNK), BF16)],
        scratch_shapes=[pltpu.VMEM((1, 128), F32)], compiler_params=_params("arbitrary"),
    )(proj, proj, proj, proj, proj, proj, proj, bf)


FOX_TILE = 256
FOX_KEYS = 512


def _fox_mask(tk, tq, k0, q0):
    return (_row((tk, tq)) + (k0 - q0)) <= _lane((tk, tq))


def _ride_refs(ride, rest, n_out, n_scratch):
    n = ride.n if ride else 0
    srcs, rest = rest[:n], rest[n:]
    outs, rest = rest[:n_out], rest[n_out:]
    dsts, rest = rest[:n], rest[n:]
    return srcs, outs, dsts, rest[:n_scratch], rest[n_scratch:]


def _ride_start(ride, grid, srcs, dsts, sems):
    if ride:
        first = functools.reduce(lambda a, b: a & b, [pl.program_id(d) == 0 for d in range(len(grid))])
        pl.when(first)(lambda: ride.start(srcs, dsts, sems))


def _ride_wait(ride, grid, srcs, dsts, sems):
    if ride:
        last = functools.reduce(lambda a, b: a & b, [pl.program_id(d) == n - 1 for d, n in enumerate(grid)])
        pl.when(last)(lambda: ride.wait(srcs, dsts, sems))


def _fox_fwd(qt, kt, vt, proj, tag, ride=None):
    T = proj.shape[0]
    tq, tk = _tile(T, FOX_TILE), _tile(T, FOX_KEYS)
    nq, nsub = T // tq, tk // CHUNK
    npair = C_HEADS // 2

    def body(q_ref, k_ref, vt_ref, z_ref, *rest):
        ride_srcs, (o_ref, lse_ref, y_ref), ride_dsts, (acc_ref, st_ref, pt_ref), ride_sems = _ride_refs(ride, rest, 3, 3)
        i = pl.program_id(1)
        _ride_start(ride, (npair, nq), ride_srcs, ride_dsts, ride_sems)

        qs = (q_ref[:, 0:128], q_ref[:, 128:256])
        acc_ref[...] = jnp.zeros_like(acc_ref)
        pt_ref[...] = jnp.zeros_like(pt_ref)
        nfull = (i * tq) // tk

        def scores(j):
            kb = k_ref[pl.ds(pl.multiple_of(j * tk, tk), tk), :]
            return tuple(_dot_nt(kb[:, 128 * h:128 * h + 128], qs[h]) for h in range(2))

        def weigh(j, h):
            rows = slice(64 * h, 64 * h + 64)
            pv = _dot(vt_ref[0, nsub * j, rows, :], pt_ref[h, 0:CHUNK, :])
            for c in range(1, nsub):
                pv = pv + _dot(vt_ref[0, nsub * j + c, rows, :], pt_ref[h, CHUNK * c:CHUNK * c + CHUNK, :])
            return pv

        def block(j, carry, diagonal):
            nxt = () if diagonal else scores(j + 1)
            pvs = [weigh(jnp.maximum(j - 1, 0), h) for h in range(2)]
            new = []
            for h in range(2):
                m, l, alpha_prev = carry[3 * h:3 * h + 3]
                st = st_ref[h]
                if diagonal:
                    st = jnp.where(_fox_mask(tk, tq, j * tk, i * tq), st, -jnp.inf)
                m_new = jnp.maximum(m, _colreduce(st, jnp.maximum))
                pt = jnp.exp(st - m_new)
                alpha = jnp.exp(m - m_new)
                rows = slice(64 * h, 64 * h + 64)
                acc_ref[rows, :] = alpha_prev * acc_ref[rows, :] + pvs[h]
                pt_ref[h] = pt.astype(BF16)
                new += [m_new, alpha * l + _colreduce(pt, jnp.add), alpha]
            for h, st in enumerate(nxt):
                st_ref[h] = st
            return tuple(new)

        for h, st in enumerate(scores(0)):
            st_ref[h] = st
        init = (jnp.full((1, tq), -jnp.inf, F32), jnp.zeros((1, tq), F32), jnp.ones((1, tq), F32)) * 2
        carry = lax.fori_loop(0, nfull, lambda j, c: block(j, c, False), init)
        m0, l0, a0, m1, l1, a1 = block(nfull, carry, True)
        for h, alpha in enumerate((a0, a1)):
            rows = slice(64 * h, 64 * h + 64)
            acc_ref[rows, :] = alpha * acc_ref[rows, :] + weigh(nfull, h)
        inv = jnp.where(_row((128, tq)) < 64, 1.0 / l0, 1.0 / l1)
        o = (acc_ref[...] * inv).T
        o_ref[...] = o
        r8 = _row((8, tq))
        lse_ref[0, 0] = jnp.where(r8 == 0, m0 + jnp.log(l0), jnp.where(r8 == 1, m1 + jnp.log(l1), 0.0))
        sz, _ = _silu_and_grad(z_ref[...])
        y_ref[...] = (o * sz).astype(BF16)
        _ride_wait(ride, (npair, nq), ride_srcs, ride_dsts, ride_sems)

    blk = pl.BlockSpec((tq, 128), lambda p, i: (i, p))
    extra = ride or _ChipExchange("gather", ())
    return pl.pallas_call(
        body, name=f"fox_fwd_{tag}", grid=(npair, nq),
        in_specs=[pl.BlockSpec((tq, 256), lambda p, i: (i, p)), pl.BlockSpec((T, 256), lambda p, i: (0, p)),
                  pl.BlockSpec((1, T // CHUNK, 128, CHUNK), lambda p, i: (p, 0, 0, 0)),
                  pl.BlockSpec((tq, 128), lambda p, i: (i, COL_CZ // 128 + p))] + extra.in_specs,
        out_specs=[blk, pl.BlockSpec((1, 1, 8, tq), lambda p, i: (p, i, 0, 0)), blk] + extra.out_specs,
        out_shape=[SDS((T, C_WIDTH), F32), SDS((npair, nq, 8, tq), F32), SDS((T, C_WIDTH), BF16)] + extra.out_shape,
        scratch_shapes=[pltpu.VMEM((128, tq), F32), pltpu.VMEM((2, tk, tq), F32), pltpu.VMEM((2, tk, tq), BF16)]
        + (extra.scratch if ride else []),
        compiler_params=pltpu.CompilerParams(dimension_semantics=("arbitrary", "arbitrary"), vmem_limit_bytes=VMEM_LIMIT,
                                             has_side_effects=bool(ride)),
    )(qt, kt, vt, proj, *extra.sources)


def _fox_bwd_prep(proj, dy, o, tag):
    T = proj.shape[0]
    tq = _tile(T, FOX_TILE)

    def body(z_ref, dy_ref, o_ref, do_ref, dl_ref, dz_ref):
        sz, dsz = _silu_and_grad(z_ref[...])
        dyv, ov = dy_ref[...], o_ref[...]
        do = dyv * sz
        do_ref[...] = do.astype(BF16)
        dz_ref[...] = (dyv * ov * dsz).astype(BF16)
        sel = jnp.where((_lane((16, 128)) >> 6) == _row((16, 128)), 1.0, 0.0).astype(BF16)
        hi, mid, lo = _split3(do * ov)
        dl_ref[0, 0] = (_dot_nt(sel, hi) + _dot_nt(sel, mid) + _dot_nt(sel, lo))[0:8, :]

    blk = pl.BlockSpec((tq, 128), lambda i, p: (i, p))
    return pl.pallas_call(
        body, name=f"fox_bwd_prep_{tag}", grid=(T // tq, C_WIDTH // 128),
        in_specs=[pl.BlockSpec((tq, 128), lambda i, p: (i, COL_CZ // 128 + p)),
                  pl.BlockSpec((tq, 128), lambda i, p: (i, (A_WIDTH + B_WIDTH) // 128 + p)), blk],
        out_specs=[blk, pl.BlockSpec((1, 1, 8, tq), lambda i, p: (p, i, 0, 0)), blk],
        out_shape=[SDS((T, C_WIDTH), BF16), SDS((C_HEADS // 2, T // tq, 8, tq), F32), SDS((T, C_WIDTH), BF16)],
        compiler_params=_params("parallel", "parallel"),
    )(proj, dy, o)


def _fox_bwd(qt, kt, proj, do, lse, delta, tag, ride=None):
    T = proj.shape[0]
    tq, tk = _tile(T, FOX_TILE), _tile(T, FOX_KEYS)
    nq, nk = T // tq, T // tk
    ndiag = tk // tq
    npair = C_HEADS // 2

    def body(q_ref, k_ref, v_ref, do_ref, lse_ref, dl_ref, *rest):
        ride_srcs, (dq_ref, dk_ref, dv_ref), ride_dsts, scratch, ride_sems = _ride_refs(ride, rest, 3, 4)
        dvacc_ref, sc_ref, pt_ref, ds_ref = scratch
        j = pl.program_id(1)
        first = (j * tk) // tq
        _ride_start(ride, (npair, nk), ride_srcs, ride_dsts, ride_sems)

        @pl.when(j == 0)
        def _():
            dq_ref[...] = jnp.zeros_like(dq_ref)

        dk_ref[...] = jnp.zeros_like(dk_ref)
        dvacc_ref[...] = jnp.zeros_like(dvacc_ref)
        ks = (k_ref[:, 0:128], k_ref[:, 128:256])
        kts = tuple(k.astype(F32).T.astype(BF16) for k in ks)
        vb = v_ref[...].astype(BF16)
        lo = _lane((tq, 128)) < 64

        def operands(i):
            q0 = pl.multiple_of(i * tq, tq)
            qb = q_ref[pl.ds(q0, tq), :]
            dob = do_ref[pl.ds(q0, tq), :]
            qhs = (qb[:, 0:128], qb[:, 128:256])
            dohs = (jnp.where(lo, dob, jnp.zeros_like(dob)), jnp.where(lo, jnp.zeros_like(dob), dob))
            return qhs, dohs

        def scores(i):
            qhs, dohs = operands(i)
            return tuple(_dot_nt(ks[h], qhs[h]) for h in range(2)) + tuple(_dot_nt(vb, dohs[h]) for h in range(2))

        def park(sc):
            for a, s in enumerate(sc):
                sc_ref[a] = s

        def grads(i):
            qhs, dohs = operands(i)
            dvacc_ref[...] += _dot(jnp.concatenate([pt_ref[0], pt_ref[1]], axis=1), jnp.concatenate(dohs, axis=0))
            for h in range(2):
                dk_ref[:, 128 * h:128 * h + 128] += _dot(ds_ref[h], qhs[h])
                dq_ref[h, i] += _dot(kts[h], ds_ref[h])

        def block(i, diagonal, opening):
            nxt = scores(jnp.minimum(i + 1, nq - 1))
            if not opening:
                grads(i - 1)
            lsev = lse_ref[0, i]
            dlv = dl_ref[0, i]
            for h in range(2):
                pt = jnp.exp(sc_ref[h] - lsev[h:h + 1, :])
                if diagonal:
                    pt = jnp.where(_fox_mask(tk, tq, j * tk, i * tq), pt, 0.0)
                ds_ref[h] = (pt * (sc_ref[2 + h] - dlv[h:h + 1, :])).astype(BF16)
                pt_ref[h] = pt.astype(BF16)
            park(nxt)

        park(scores(first))
        for d in range(ndiag):
            block(first + d, True, d == 0)

        def step(i, carry):
            block(i, False, False)
            return carry

        lax.fori_loop(first + ndiag, nq, step, 0)
        grads(nq - 1)
        dv_ref[...] = dvacc_ref[...].astype(BF16)
        _ride_wait(ride, (npair, nk), ride_srcs, ride_dsts, ride_sems)

    full = lambda w: pl.BlockSpec((T, w), lambda p, j: (0, p))
    stat = pl.BlockSpec((1, nq, 8, tq), lambda p, j: (p, 0, 0, 0))
    extra = ride or _ChipExchange("gather", ())
    return pl.pallas_call(
        body, name=f"fox_bwd_{tag}", grid=(npair, nk),
        in_specs=[full(256), pl.BlockSpec((tk, 256), lambda p, j: (j, p)),
                  pl.BlockSpec((tk, 128), lambda p, j: (j, COL_CV // 128 + p)), full(128), stat, stat] + extra.in_specs,
        out_specs=[pl.BlockSpec((2, nq, 128, tq), lambda p, j: (p, 0, 0, 0)), pl.BlockSpec((tk, 256), lambda p, j: (j, p)),
                   pl.BlockSpec((tk, 128), lambda p, j: (j, p))] + extra.out_specs,
        out_shape=[SDS((C_HEADS, nq, 128, tq), F32), SDS((T, C_HEADS * 128), F32), SDS((T, C_WIDTH), BF16)]
        + extra.out_shape,
        scratch_shapes=[pltpu.VMEM((tk, 128), F32), pltpu.VMEM((4, tk, tq), F32), pltpu.VMEM((2, tk, tq), BF16),
                        pltpu.VMEM((2, tk, tq), BF16)] + (extra.scratch if ride else []),
        compiler_params=pltpu.CompilerParams(dimension_semantics=("arbitrary", "arbitrary"), vmem_limit_bytes=VMEM_LIMIT,
                                             has_side_effects=bool(ride)),
    )(qt, kt, proj, do, lse, delta, *extra.sources)


def _fox_bwd_post(dqt, dkt, proj, bf, tag):
    T = proj.shape[0]
    tq = _tile(T, FOX_TILE)
    n = T // tq

    def body(dq_ref, dk_ref, fl_ref, bf_ref, oq_ref, ok_ref, ofl_ref, dbf_ref, carry_ref):
        @pl.when(pl.program_id(0) == 0)
        def _():
            carry_ref[...] = jnp.zeros_like(carry_ref)
            dbf_ref[...] = jnp.zeros_like(dbf_ref)

        lane = _lane((tq, 128))
        lo = lane < 64
        dqs = [dq_ref[h, 0].T for h in range(C_HEADS)]
        dc = jnp.zeros((tq, 128), F32)
        for h in range(C_HEADS):
            dc = dc + jnp.where(lane == h, dqs[h][:, 64:65] - dk_ref[:, 128 * h + 67:128 * h + 68], 0.0)
        utri = jnp.where(_lane((tq, tq)) >= _row((tq, tq)), 1.0, 0.0).astype(BF16)
        dlf = _dot3_left(utri, dc) + carry_ref[...]
        carry_ref[...] = dlf[0:1, :]
        dfl = jnp.where(lane < C_HEADS, dlf * _sigmoid(-(fl_ref[...] + bf_ref[...])), 0.0)
        ofl_ref[...] = dfl.astype(BF16)
        dbf_ref[...] += jnp.sum(dfl, axis=0, keepdims=True)
        for p in range(C_HEADS // 2):
            a, b = 128 * (2 * p), 128 * (2 * p + 1)
            oq_ref[:, 128 * p:128 * p + 128] = (
                jnp.where(lo, dqs[2 * p], pltpu.roll(dqs[2 * p + 1], 64, axis=1)) * Q_SCALE).astype(BF16)
            ok_ref[:, 128 * p:128 * p + 128] = jnp.where(
                lo, dk_ref[:, a:a + 128], pltpu.roll(dk_ref[:, b:b + 128], 64, axis=1)).astype(BF16)

    rev = lambda w: pl.BlockSpec((tq, w), lambda i: (n - 1 - i, 0))
    return pl.pallas_call(
        body, name=f"fox_bwd_post_{tag}", grid=(n,),
        in_specs=[pl.BlockSpec((C_HEADS, 1, 128, tq), lambda i: (0, n - 1 - i, 0, 0)), rev(C_HEADS * 128),
                  pl.BlockSpec((tq, 128), lambda i: (n - 1 - i, COL_CF // 128)), pl.BlockSpec((1, 128), lambda i: (0, 0))],
        out_specs=[rev(C_WIDTH), rev(C_WIDTH), rev(128), pl.BlockSpec((1, 128), lambda i: (0, 0))],
        out_shape=[SDS((T, C_WIDTH), BF16), SDS((T, C_WIDTH), BF16), SDS((T, 128), BF16), SDS((1, 128), F32)],
        scratch_shapes=[pltpu.VMEM((1, 128), F32)], compiler_params=_params("arbitrary"),
    )(dqt, dkt, proj, bf)


def _adamw_math(w, g, m, v):
    m = ADAM_B1 * m + (1.0 - ADAM_B1) * g
    v = ADAM_B2 * v + (1.0 - ADAM_B2) * (g * g)
    delta = -ADAM_LR * ((m / ADAM_C1) / (jnp.sqrt(v / ADAM_C2) + ADAM_EPS) + ADAM_WD * w)
    return delta, m, v


def _adamw_pair(w, m, v, ga, gb, name):
    n0 = w.shape[0]
    most = max(1, ADAMW_BLOCK_BYTES // (4 * math.prod(w.shape[1:])))
    t0 = max(t for t in range(1, min(n0, most) + 1) if n0 % t == 0)

    def body(w_ref, m_ref, v_ref, ga_ref, gb_ref, g_ref, d_ref, nm_ref, nv_ref):
        g = ga_ref[...] + gb_ref[...]
        g_ref[...] = g
        d_ref[...], nm_ref[...], nv_ref[...] = _adamw_math(w_ref[...], g, m_ref[...], v_ref[...])

    blk = pl.BlockSpec((t0,) + w.shape[1:], lambda i: (i, 0, 0))
    return pl.pallas_call(
        body, name=name, grid=(n0 // t0,), in_specs=[blk] * 5, out_specs=[blk] * 4,
        out_shape=[SDS(w.shape, F32)] * 4, compiler_params=_params("parallel"),
    )(w, m, v, ga, gb)


def _adamw_small(w, m, v, gall):
    R = w.shape[0]

    def body(w_ref, m_ref, v_ref, g_ref, go_ref, d_ref, nm_ref, nv_ref):
        g = g_ref[0]
        for k in range(1, N_DEV):
            g = g + g_ref[k]
        go_ref[...] = g
        d_ref[...], nm_ref[...], nv_ref[...] = _adamw_math(w_ref[...], g, m_ref[...], v_ref[...])

    return pl.pallas_call(body, name="adamw_small", out_shape=[SDS((R, 128), F32)] * 4,
                          compiler_params=pltpu.CompilerParams(vmem_limit_bytes=VMEM_LIMIT))(w, m, v, gall)


def _sum_chips(layers, name, layer_major):
    _, R, C = layers[0].shape
    L = len(layers)
    tc = _tile(C, 256)

    def body(*refs):
        o_ref = refs[-1]
        for l, p_ref in enumerate(refs[:-1]):
            p = [p_ref[k].astype(F32) for k in range(N_CHIPS)]
            s = ((p[0] + p[1]) + p[2]) + p[3]
            if layer_major:
                o_ref[l] = s
            else:
                o_ref[:, l, :] = s

    out = (L, R, C) if layer_major else (R, L, C)
    out_blk = (L, R, tc) if layer_major else (R, L, tc)
    return pl.pallas_call(
        body, name=name, grid=(C // tc,),
        in_specs=[pl.BlockSpec((N_CHIPS, R, tc), lambda i: (0, 0, i))] * L,
        out_specs=pl.BlockSpec(out_blk, lambda i: (0, 0, i)), out_shape=SDS(out, F32),
        compiler_params=_params("parallel"),
    )(*layers)


ANY = pl.BlockSpec(memory_space=pl.ANY)


def _mesh_pos():
    return lax.axis_index("x"), lax.axis_index("y"), lax.axis_index("c")


def _other_chips(x, y):
    return [(1 - x, y), (x, 1 - y), (1 - x, 1 - y)]


class _ChipExchange:
    def __init__(self, mode, sources):
        assert mode in ("gather", "scatter")
        self.mode, self.sources = mode, tuple(sources)
        self.n = len(self.sources)
        self.in_specs = [ANY] * self.n
        self.out_specs = [ANY] * self.n
        self.out_shape = [SDS(((N_CHIPS,) + s.shape) if mode == "gather" else s.shape, s.dtype) for s in self.sources]
        self.scratch = [pltpu.SemaphoreType.DMA((3 * self.n,)), pltpu.SemaphoreType.DMA((3 * self.n,)),
                        pltpu.SemaphoreType.DMA((self.n,))]

    def _copies(self, srcs, dsts, send_sems, recv_sems, local_sems):
        x, y, c = _mesh_pos()
        me = 2 * x + y
        view = (lambda r, chip: r) if self.mode == "gather" else (lambda r, chip: r.at[chip])
        local = [pltpu.make_async_copy(view(s, me), d.at[me], local_sems.at[a]) for a, (s, d) in enumerate(zip(srcs, dsts))]
        sends, recvs = [], []
        for j, (px, py) in enumerate(_other_chips(x, y)):
            peer = 2 * px + py
            for a, (s, d) in enumerate(zip(srcs, dsts)):
                sems = dict(send_sem=send_sems.at[self.n * j + a], recv_sem=recv_sems.at[self.n * j + a],
                            device_id=(px, py, c), device_id_type=MESH_ID)
                sends.append(pltpu.make_async_remote_copy(src_ref=view(s, peer), dst_ref=d.at[me], **sems))
                recvs.append(pltpu.make_async_remote_copy(src_ref=view(s, me), dst_ref=d.at[peer], **sems))
        return local, sends, recvs

    def start(self, srcs, dsts, sems):
        local, sends, _ = self._copies(srcs, dsts, *sems)
        for cp in local + sends:
            cp.start()

    def wait(self, srcs, dsts, sems):
        local, sends, recvs = self._copies(srcs, dsts, *sems)
        for cp in recvs:
            cp.wait_recv()
        for cp in sends:
            cp.wait_send()
        for cp in local:
            cp.wait()


def _gather_halves(w, tag):
    R, C = w.shape
    H = R // 2

    def body(w_ref, g_ref, send_sems, recv_sems, pass_send, pass_recv, local_sem):
        x, y, c = _mesh_pos()
        me = 2 * x + y
        mine, theirs = pl.ds(c * H, H), pl.ds((1 - c) * H, H)
        own = pltpu.make_async_copy(w_ref, g_ref.at[me], local_sem)
        own.start()

        def fetch(j, px, py, src, dst):
            return pltpu.make_async_remote_copy(src_ref=src, dst_ref=dst, send_sem=send_sems.at[j], recv_sem=recv_sems.at[j],
                                                device_id=(px, py, c), device_id_type=MESH_ID)

        def hand(j, rows, peer):
            return pltpu.make_async_remote_copy(src_ref=g_ref.at[peer, rows], dst_ref=g_ref.at[peer, rows],
                                                send_sem=pass_send.at[j], recv_sem=pass_recv.at[j],
                                                device_id=(x, y, 1 - c), device_id_type=MESH_ID)

        chips = _other_chips(x, y)
        sends = [fetch(j, px, py, w_ref.at[mine], g_ref.at[me, mine]) for j, (px, py) in enumerate(chips)]
        for cp in sends:
            cp.start()
        passed = []
        for j, (px, py) in enumerate(chips):
            peer = 2 * px + py
            fetch(j, px, py, w_ref.at[mine], g_ref.at[peer, mine]).wait_recv()
            passed.append(hand(j, mine, peer))
            passed[-1].start()
        for j, (px, py) in enumerate(chips):
            hand(j, theirs, 2 * px + py).wait_recv()
        for cp in sends + passed:
            cp.wait_send()
        own.wait()

    return pl.pallas_call(
        body, name=f"gather_halves_{tag}", in_specs=[ANY], out_specs=ANY, out_shape=SDS((N_CHIPS, R, C), w.dtype),
        scratch_shapes=[pltpu.SemaphoreType.DMA((3,)), pltpu.SemaphoreType.DMA((3,)), pltpu.SemaphoreType.DMA((3,)),
                        pltpu.SemaphoreType.DMA((3,)), pltpu.SemaphoreType.DMA],
        compiler_params=pltpu.CompilerParams(has_side_effects=True),
    )(w)


class _DeviceGather:
    def __init__(self, source):
        self.sources, self.n = (source,), 1
        self.in_specs, self.out_specs = [ANY], [ANY]
        self.out_shape = [SDS((N_DEV,) + source.shape, source.dtype)]
        self.scratch = [pltpu.SemaphoreType.DMA((N_DEV - 1,)), pltpu.SemaphoreType.DMA((N_DEV - 1,)),
                        pltpu.SemaphoreType.DMA((1,))]

    def _copies(self, srcs, dsts, send_sems, recv_sems, local_sems):
        (src,), (dst,) = srcs, dsts
        x, y, c = _mesh_pos()
        me = 4 * x + 2 * y + c
        local = [pltpu.make_async_copy(src, dst.at[me], local_sems.at[0])]
        sends, recvs = [], []
        for k in range(1, N_DEV):
            px, py, pc = (1 - x) if k & 4 else x, (1 - y) if k & 2 else y, (1 - c) if k & 1 else c
            sems = dict(send_sem=send_sems.at[k - 1], recv_sem=recv_sems.at[k - 1], device_id=(px, py, pc),
                        device_id_type=MESH_ID)
            sends.append(pltpu.make_async_remote_copy(src_ref=src, dst_ref=dst.at[me], **sems))
            recvs.append(pltpu.make_async_remote_copy(src_ref=src, dst_ref=dst.at[4 * px + 2 * py + pc], **sems))
        return local, sends, recvs

    start = _ChipExchange.start
    wait = _ChipExchange.wait


def _gather_devices(a, name):
    ex = _DeviceGather(a)

    def body(a_ref, g_ref, *sems):
        ex.start((a_ref,), (g_ref,), sems)
        ex.wait((a_ref,), (g_ref,), sems)

    return pl.pallas_call(
        body, name=name, in_specs=ex.in_specs, out_specs=ex.out_specs[0], out_shape=ex.out_shape[0],
        scratch_shapes=ex.scratch, compiler_params=pltpu.CompilerParams(has_side_effects=True),
    )(a)


def _swap_cores(pin, pout):
    def body(pin_ref, pout_ref, oin_ref, oout_ref, send_sems, recv_sems):
        x, y, c = _mesh_pos()
        cps = [pltpu.make_async_remote_copy(src_ref=src, dst_ref=dst, send_sem=send_sems.at[a], recv_sem=recv_sems.at[a],
                                            device_id=(x, y, 1 - c), device_id_type=MESH_ID)
               for a, (src, dst) in enumerate(((pin_ref, oin_ref), (pout_ref, oout_ref)))]
        for cp in cps:
            cp.start()
        for cp in cps:
            cp.wait()

    return pl.pallas_call(
        body, name="swap_cores", in_specs=[ANY, ANY], out_specs=[ANY, ANY],
        out_shape=[SDS(pin.shape, F32), SDS(pout.shape, F32)],
        scratch_shapes=[pltpu.SemaphoreType.DMA((2,)), pltpu.SemaphoreType.DMA((2,))],
        compiler_params=pltpu.CompilerParams(has_side_effects=True),
    )(pin, pout)


def _pack_small(parts):
    flat = [jnp.pad(p.reshape(-1), (0, (-p.size) % 128)) for p in parts]
    v = jnp.concatenate(flat)
    return jnp.pad(v, (0, (-v.size) % 1024)).reshape(-1, 128)


def _unpack_small(packed):
    flat = packed.reshape(-1)
    out, off = [], 0
    for _, shape in SMALL_PARAMS:
        size = math.prod(shape)
        out.append(flat[off:off + size].reshape(shape))
        off += size + (-size) % 128
    return out


def _layer_consts(l, gmlp_ln_g, gmlp_ln_b, gmlp_w_s, gmlp_b_s, hgrn_onorm_g, fox_b_f):
    causal = jnp.tril(jnp.ones((CHUNK, CHUNK), bool))
    wm = jnp.where(causal[None], gmlp_w_s[l], 0.0)
    return dict(
        lng=gmlp_ln_g[l].reshape(1, A_WIDTH), lnb=gmlp_ln_b[l].reshape(1, A_WIDTH),
        wm=wm.astype(BF16), wmt=jnp.swapaxes(wm, 1, 2).astype(BF16),
        bst=jnp.pad(gmlp_b_s[l].T, ((0, 0), (0, 128 - A_GROUPS))),
        onorm=jnp.tile(hgrn_onorm_g[l], 4).reshape(1, B_WIDTH),
        bf=jnp.pad(fox_b_f[l], (0, 128 - C_HEADS)).reshape(1, 128),
    )


def kernel(x, norm_g, w_in, w_out, gmlp_ln_g, gmlp_ln_b, gmlp_w_s, gmlp_b_s, hgrn_lb, hgrn_onorm_g, fox_b_f, final_norm_g, loss_target, m_norm_g, m_w_in, m_w_out, m_gmlp_ln_g, m_gmlp_ln_b, m_gmlp_w_s, m_gmlp_b_s, m_hgrn_lb, m_hgrn_onorm_g, m_fox_b_f, m_final_norm_g, v_norm_g, v_w_in, v_w_out, v_gmlp_ln_g, v_gmlp_ln_b, v_gmlp_w_s, v_gmlp_b_s, v_hgrn_lb, v_hgrn_onorm_g, v_fox_b_f, v_final_norm_g):
    T = x.shape[1]
    shard_in = w_in.shape[2]
    shard_out = w_out.shape[1]
    xs = x.reshape(T, D_MODEL)
    tgt = loss_target.reshape(T, D_MODEL)

    w_in_b, w_out_b = w_in.astype(BF16), w_out.astype(BF16)

    def full_w_in(gathered):
        wi = jnp.concatenate([gathered[k] for k in range(N_CHIPS)], axis=-1)
        return jnp.pad(wi, ((0, 0), (0, D_IN_PAD - D_IN)))

    lb_all = _lb_fwd(hgrn_lb)
    consts = [_layer_consts(l, gmlp_ln_g, gmlp_ln_b, gmlp_w_s, gmlp_b_s, hgrn_onorm_g, fox_b_f) for l in range(DEPTH)]

    saved = []
    xl = xs
    w_in_l = full_w_in(_gather_halves(w_in_b[0], "w_in_l0"))
    for l in range(DEPTH):
        cs = consts[l]
        tag = f"l{l}"
        h, proj = _inproj(xl, norm_g[l].reshape(1, D_MODEL), w_in_l, tag)
        ya = _gmlp_fwd(proj, cs["lng"], cs["lnb"], cs["wm"], cs["bst"], tag)
        yb, ob, s0 = _hgrn_fwd(proj, lb_all[l].reshape(1, B_WIDTH), cs["onorm"], tag)
        qt, kt, vt = _fox_prep(proj, cs["bf"], tag)
        ride = _ChipExchange("gather", (w_out_b[l],) + ((w_in_b[l + 1],) if l + 1 < DEPTH else ()))
        oc, lse, yc, *gathered = _fox_fwd(qt, kt, vt, proj, tag, ride)
        w_out_l = gathered[0].reshape(N_CHIPS * shard_out, D_MODEL)
        saved.append(dict(x=xl, h=h, proj=proj, ya=ya, yb=yb, yc=yc, ob=ob, s0=s0, qt=qt, kt=kt, oc=oc, lse=lse,
                          w_in=w_in_l, w_out=w_out_l))
        xl = _outproj(xl, ya, yb, yc, w_out_l, tag)
        if l + 1 < DEPTH:
            w_in_l = full_w_in(gathered[1])

    dx, loss_part, d_final = _loss_head(xl, final_norm_g.reshape(1, D_MODEL), tgt)
    loss = lax.psum(loss_part[0, 0], ("x", "y", "c"))

    g_small = {}
    dlb_rows, rin, rout = [None] * DEPTH, [None] * DEPTH, [None] * DEPTH
    slabs_in = None
    stack = lambda key: jnp.stack([g_small[l][key] for l in range(DEPTH)])
    for l in reversed(range(DEPTH)):
        cs, sv = consts[l], saved[l]
        tag = f"l{l}"
        proj = sv["proj"]
        dy, dw_out = _outproj_bwd(dx, sv["ya"], sv["yb"], sv["yc"], sv["w_out"], tag)
        da, dwm, dbst, dlng, dlnb = _gmlp_bwd(proj, dy, cs["lng"], cs["lnb"], cs["wm"], cs["wmt"], cs["bst"], tag)
        db, dlb_rows[l], donorm = _hgrn_bwd(proj, dy, sv["ob"], sv["s0"], lb_all[l].reshape(1, B_WIDTH), cs["onorm"], tag)
        do, delta, dzc = _fox_bwd_prep(proj, dy, sv["oc"], tag)
        slabs_out = dw_out.reshape(N_CHIPS, shard_out, D_MODEL).astype(BF16)
        ride = _ChipExchange("scatter", (slabs_out,) + ((slabs_in,) if slabs_in is not None else ()))
        dqt, dkt, dvc, *received = _fox_bwd(sv["qt"], sv["kt"], proj, do, sv["lse"], delta, tag, ride)
        rout[l] = received[0]
        if slabs_in is not None:
            rin[l + 1] = received[1]
        dqc, dkc, dflc, dbf = _fox_bwd_post(dqt, dkt, proj, cs["bf"], tag)
        g_small[l] = dict(ln_g=dlng.reshape(4, 64), ln_b=dlnb.reshape(4, 64), w_s=dwm, b_s=dbst[:, :A_GROUPS].T,
                          onorm=donorm[0, :64], bf=dbf[0, :C_HEADS])
        dproj = jnp.concatenate([da, db, dqc, dkc, dvc, dzc, dflc, jnp.zeros((T, 128), BF16)], axis=1)
        if l == 0:
            d_hgrn_lb = _lb_bwd(hgrn_lb, jnp.concatenate(dlb_rows, axis=0))
            early = _pack_small([stack("ln_g"), stack("ln_b"), stack("w_s"), stack("b_s"), d_hgrn_lb, stack("onorm"),
                                 stack("bf"), d_final.reshape(D_MODEL)])
            dw_in, rearly = _dw_in(sv["h"], dproj, tag, _DeviceGather(early))
        else:
            dw_in = _dw_in(sv["h"], dproj, tag)
        slabs_in = jnp.stack([dw_in[k * shard_in:(k + 1) * shard_in] for k in range(N_CHIPS)]).astype(BF16)
        ride = _ChipExchange("scatter", (slabs_in,)) if l == 0 else None
        dx, dng, *received = _dx_in(sv["x"], norm_g[l].reshape(1, D_MODEL), dx, dproj, sv["w_in"], tag, ride)
        if l == 0:
            rin[0] = received[0]
        g_small[l]["norm_g"] = dng.reshape(D_MODEL)
    grad_x = dx.reshape(x.shape)
    rlate = _gather_devices(_pack_small([stack("norm_g")]), "gather_norm_grads")
    rsmall = jnp.concatenate([rlate, rearly], axis=1)

    pin, pout = _sum_chips(rin, "sum_chips_w_in", False), _sum_chips(rout, "sum_chips_w_out", True)
    oin, oout = _swap_cores(pin, pout)
    to_view = lambda a: jnp.transpose(a, (2, 0, 1))
    g_w_in, d_w_in, nm_w_in, nv_w_in = [
        jnp.transpose(o, (1, 2, 0))
        for o in _adamw_pair(to_view(w_in), to_view(m_w_in), to_view(v_w_in), pin, oin, "adamw_w_in")]
    g_w_out, d_w_out, nm_w_out, nv_w_out = _adamw_pair(w_out, m_w_out, v_w_out, pout, oout, "adamw_w_out")

    small_w = [norm_g, gmlp_ln_g, gmlp_ln_b, gmlp_w_s, gmlp_b_s, hgrn_lb, hgrn_onorm_g, fox_b_f, final_norm_g]
    small_m = [m_norm_g, m_gmlp_ln_g, m_gmlp_ln_b, m_gmlp_w_s, m_gmlp_b_s, m_hgrn_lb, m_hgrn_onorm_g, m_fox_b_f, m_final_norm_g]
    small_v = [v_norm_g, v_gmlp_ln_g, v_gmlp_ln_b, v_gmlp_w_s, v_gmlp_b_s, v_hgrn_lb, v_hgrn_onorm_g, v_fox_b_f, v_final_norm_g]
    outs = _adamw_small(_pack_small(small_w), _pack_small(small_m), _pack_small(small_v), rsmall)
    sg, sd, sm, sv_ = [_unpack_small(o) for o in outs]

    def order(big_in, big_out, small):
        return [small[0], big_in, big_out] + small[1:]

    return (loss, grad_x, *order(g_w_in, g_w_out, sg), *order(d_w_in, d_w_out, sd), *order(nm_w_in, nm_w_out, sm),
            *order(nv_w_in, nv_w_out, sv_))
```

```python
import functools
import math

import jax
import jax.numpy as jnp
from jax import lax
from jax.experimental import pallas as pl
from jax.experimental.pallas import tpu as pltpu

F32 = jnp.float32
BF16 = jnp.bfloat16
SDS = jax.ShapeDtypeStruct
MESH_ID = pl.DeviceIdType.MESH

D_MODEL = 1024
DEPTH = 2
A_WIDTH = 256
A_GROUPS = 4
B_WIDTH = 256
C_WIDTH = 512
C_HEADS = 8
D_IN = 3848
D_IN_PAD = 4096
CHUNK = 128
SUB = 16
SUB_SHIFT = 4
NORM_EPS = 1e-6
F_FLOOR = 1e-30
COL_AU, COL_AV, COL_AZ = 0, 256, 512
COL_BQ, COL_BF, COL_BI, COL_BZ = 768, 1024, 1280, 1536
COL_CQ, COL_CK, COL_CV, COL_CZ, COL_CF = 1792, 2304, 2816, 3328, 3840
HEAD_LANES = 128
Q_SCALE = 0.125
ADAM_LR, ADAM_B1, ADAM_B2, ADAM_EPS, ADAM_WD, ADAM_STEP = 0.001, 0.9, 0.999, 1e-08, 0.01, 10
ADAM_C1 = 1.0 - ADAM_B1 ** ADAM_STEP
ADAM_C2 = 1.0 - ADAM_B2 ** ADAM_STEP
VMEM_LIMIT = 56 * 1024 * 1024
ADAMW_BLOCK_BYTES = 1 << 20
N_CHIPS = 4
N_DEV = 8

SMALL_PARAMS = (
    ("norm_g", (DEPTH, D_MODEL)), ("gmlp_ln_g", (DEPTH, 4, 64)), ("gmlp_ln_b", (DEPTH, 4, 64)),
    ("gmlp_w_s", (DEPTH, 4, 128, 128)), ("gmlp_b_s", (DEPTH, 4, 128)), ("hgrn_lb", (DEPTH, 256)),
    ("hgrn_onorm_g", (DEPTH, 64)), ("fox_b_f", (DEPTH, 8)), ("final_norm_g", (D_MODEL,)),
)


def _tile(n, pref):
    t = min(n, pref)
    assert n % t == 0, (n, pref)
    return t


def _params(*sem):
    return pltpu.CompilerParams(dimension_semantics=sem, vmem_limit_bytes=VMEM_LIMIT)


def _dot(a, b):
    return jnp.dot(a, b, preferred_element_type=F32)


def _dot_nt(a, b):
    return lax.dot_general(a, b, (((1,), (1,)), ((), ())), preferred_element_type=F32)


def _dot_tn(a, b):
    return lax.dot_general(a, b, (((0,), (0,)), ((), ())), preferred_element_type=F32)


def _split3(x):
    hi = x.astype(BF16)
    r = x - hi.astype(F32)
    mid = r.astype(BF16)
    lo = (r - mid.astype(F32)).astype(BF16)
    return hi, mid, lo


def _dot3_left(c, x):
    hi, mid, lo = _split3(x)
    return _dot(c, hi) + _dot(c, mid) + _dot(c, lo)


def _sigmoid(x):
    return jax.nn.sigmoid(x)


def _silu_and_grad(x):
    s = _sigmoid(x)
    return x * s, s * (1.0 + x * (1.0 - s))


_GELU_C = math.sqrt(2.0 / math.pi)


def _gelu_and_grad(x):
    inner = _GELU_C * (x + 0.044715 * x * x * x)
    t = jnp.tanh(inner)
    y = 0.5 * x * (1.0 + t)
    dy = 0.5 * (1.0 + t) + 0.5 * x * (1.0 - t * t) * _GELU_C * (1.0 + 3.0 * 0.044715 * x * x)
    return y, dy


def _lane(shape):
    return lax.broadcasted_iota(jnp.int32, shape, 1)


def _row(shape):
    return lax.broadcasted_iota(jnp.int32, shape, 0)


def _gsum64(x):
    lo = _lane(x.shape) < 64
    s0 = jnp.sum(jnp.where(lo, x, 0.0), axis=-1, keepdims=True)
    s1 = jnp.sum(jnp.where(lo, 0.0, x), axis=-1, keepdims=True)
    return jnp.where(lo, s0, s1)


def _colreduce(x, op):
    parts = [x[r:r + 8, :] for r in range(0, x.shape[0], 8)]
    while len(parts) > 1:
        pairs = [op(parts[k], parts[k + 1]) for k in range(0, len(parts) - 1, 2)]
        parts = pairs + ([parts[-1]] if len(parts) % 2 else [])
    red = jnp.max if op is jnp.maximum else jnp.sum
    return red(parts[0], axis=0, keepdims=True)


def _block_diag64(dtype=BF16):
    r, c = _row((128, 128)), _lane((128, 128))
    return jnp.where((r >> 6) == (c >> 6), 1.0, 0.0).astype(dtype)


def _inproj(x, g, w, tag):
    T, D = x.shape
    DP = w.shape[1]
    tm = _tile(T, 512)

    def body(x_ref, g_ref, w_ref, h_ref, p_ref):
        xv = x_ref[...]
        r = lax.rsqrt(jnp.mean(xv * xv, axis=-1, keepdims=True) + NORM_EPS)
        h = (xv * r * g_ref[...]).astype(BF16)
        h_ref[...] = h
        p_ref[...] = _dot(h, w_ref[...])

    return pl.pallas_call(
        body, name=f"inproj_{tag}", grid=(T // tm,),
        in_specs=[pl.BlockSpec((tm, D), lambda i: (i, 0)), pl.BlockSpec((1, D), lambda i: (0, 0)),
                  pl.BlockSpec((D, DP), lambda i: (0, 0))],
        out_specs=[pl.BlockSpec((tm, D), lambda i: (i, 0)), pl.BlockSpec((tm, DP), lambda i: (i, 0))],
        out_shape=[SDS((T, D), BF16), SDS((T, DP), F32)],
        compiler_params=_params("parallel"),
    )(x, g, w)


def _outproj(x, ya, yb, yc, wo, tag):
    T, D = x.shape
    tm = _tile(T, 512)

    def body(x_ref, ya_ref, yb_ref, yc_ref, wo_ref, o_ref):
        acc = x_ref[...] + _dot(ya_ref[...], wo_ref[0:A_WIDTH, :])
        acc = acc + _dot(yb_ref[...], wo_ref[A_WIDTH:A_WIDTH + B_WIDTH, :])
        o_ref[...] = acc + _dot(yc_ref[...], wo_ref[A_WIDTH + B_WIDTH:, :])

    row = lambda w: pl.BlockSpec((tm, w), lambda i: (i, 0))
    return pl.pallas_call(
        body, name=f"outproj_{tag}", grid=(T // tm,),
        in_specs=[row(D), row(A_WIDTH), row(B_WIDTH), row(C_WIDTH), pl.BlockSpec(wo.shape, lambda i: (0, 0))],
        out_specs=row(D), out_shape=SDS((T, D), F32), compiler_params=_params("parallel"),
    )(x, ya, yb, yc, wo)


def _outproj_bwd(dx, ya, yb, yc, wo, tag):
    T, D = dx.shape
    DM = wo.shape[0]
    tm = _tile(T, 512)

    def body(dx_ref, ya_ref, yb_ref, yc_ref, wo_ref, dy_ref, dwo_ref):
        @pl.when(pl.program_id(0) == 0)
        def _():
            dwo_ref[...] = jnp.zeros_like(dwo_ref)

        dxb = dx_ref[...].astype(BF16)
        dy_ref[...] = _dot_nt(dxb, wo_ref[...])
        dwo_ref[0:A_WIDTH, :] += _dot_tn(ya_ref[...], dxb)
        dwo_ref[A_WIDTH:A_WIDTH + B_WIDTH, :] += _dot_tn(yb_ref[...], dxb)
        dwo_ref[A_WIDTH + B_WIDTH:, :] += _dot_tn(yc_ref[...], dxb)

    row = lambda w: pl.BlockSpec((tm, w), lambda i: (i, 0))
    return pl.pallas_call(
        body, name=f"outproj_bwd_{tag}", grid=(T // tm,),
        in_specs=[row(D), row(A_WIDTH), row(B_WIDTH), row(C_WIDTH), pl.BlockSpec(wo.shape, lambda i: (0, 0))],
        out_specs=[row(DM), pl.BlockSpec((DM, D), lambda i: (0, 0))],
        out_shape=[SDS((T, DM), F32), SDS((DM, D), F32)], compiler_params=_params("arbitrary"),
    )(dx, ya, yb, yc, wo)


def _dw_in(h, dproj, tag, ride=None):
    T, D = h.shape
    DP = dproj.shape[1]
    tm, tn = _tile(T, 512), _tile(DP, 1024)
    grid = (DP // tn, T // tm)

    def body(h_ref, dp_ref, *rest):
        ride_srcs, (dw_ref,), ride_dsts, _, ride_sems = _ride_refs(ride, rest, 1, 0)
        _ride_start(ride, grid, ride_srcs, ride_dsts, ride_sems)

        @pl.when(pl.program_id(1) == 0)
        def _():
            dw_ref[...] = jnp.zeros_like(dw_ref)

        dw_ref[...] += _dot_tn(dp_ref[...], h_ref[...])
        _ride_wait(ride, grid, ride_srcs, ride_dsts, ride_sems)

    extra = ride or _ChipExchange("gather", ())
    out = pl.pallas_call(
        body, name=f"dw_in_{tag}", grid=grid,
        in_specs=[pl.BlockSpec((tm, D), lambda j, i: (i, 0)), pl.BlockSpec((tm, tn), lambda j, i: (i, j))] + extra.in_specs,
        out_specs=[pl.BlockSpec((tn, D), lambda j, i: (j, 0))] + extra.out_specs,
        out_shape=[SDS((DP, D), F32)] + extra.out_shape, scratch_shapes=extra.scratch if ride else [],
        compiler_params=pltpu.CompilerParams(dimension_semantics=("arbitrary", "arbitrary"), vmem_limit_bytes=VMEM_LIMIT,
                                             has_side_effects=bool(ride)),
    )(h, dproj, *extra.sources)
    return out if ride else out[0]


def _dx_in(x, g, dres, dproj, w, tag, ride=None):
    T, D = x.shape
    DP = w.shape[1]
    tm = _tile(T, 512)
    grid = (T // tm,)

    def body(x_ref, g_ref, dres_ref, dp_ref, w_ref, *rest):
        ride_srcs, (dx_ref, dg_ref), ride_dsts, _, ride_sems = _ride_refs(ride, rest, 2, 0)
        _ride_start(ride, grid, ride_srcs, ride_dsts, ride_sems)

        @pl.when(pl.program_id(0) == 0)
        def _():
            dg_ref[...] = jnp.zeros_like(dg_ref)

        dh = _dot_nt(dp_ref[...], w_ref[...])
        xv = x_ref[...]
        r = lax.rsqrt(jnp.mean(xv * xv, axis=-1, keepdims=True) + NORM_EPS)
        xh = xv * r
        dg_ref[...] += jnp.sum(dh * xh, axis=0, keepdims=True)
        dxh = dh * g_ref[...]
        dx_ref[...] = dres_ref[...] + r * (dxh - xh * jnp.mean(dxh * xh, axis=-1, keepdims=True))
        _ride_wait(ride, grid, ride_srcs, ride_dsts, ride_sems)

    extra = ride or _ChipExchange("gather", ())
    row = pl.BlockSpec((tm, D), lambda i: (i, 0))
    return pl.pallas_call(
        body, name=f"dx_in_{tag}", grid=grid,
        in_specs=[row, pl.BlockSpec((1, D), lambda i: (0, 0)), row, pl.BlockSpec((tm, DP), lambda i: (i, 0)),
                  pl.BlockSpec((D, DP), lambda i: (0, 0))] + extra.in_specs,
        out_specs=[row, pl.BlockSpec((1, D), lambda i: (0, 0))] + extra.out_specs,
        out_shape=[SDS((T, D), F32), SDS((1, D), F32)] + extra.out_shape,
        scratch_shapes=extra.scratch if ride else [],
        compiler_params=pltpu.CompilerParams(dimension_semantics=("arbitrary",), vmem_limit_bytes=VMEM_LIMIT,
                                             has_side_effects=bool(ride)),
    )(x, g, dres, dproj, w, *extra.sources)


def _loss_head(x, g, tgt):
    T, D = x.shape
    tm = _tile(T, 512)

    def body(x_ref, g_ref, t_ref, dx_ref, loss_ref, dg_ref):
        @pl.when(pl.program_id(0) == 0)
        def _():
            loss_ref[...] = jnp.zeros_like(loss_ref)
            dg_ref[...] = jnp.zeros_like(dg_ref)

        xv = x_ref[...]
        r = lax.rsqrt(jnp.mean(xv * xv, axis=-1, keepdims=True) + NORM_EPS)
        xh = xv * r
        gv = g_ref[...]
        err = xh * gv - t_ref[...]
        tok = jnp.mean(err * err, axis=-1, keepdims=True)
        loss_ref[...] += 0.5 * jnp.sum(tok, axis=0, keepdims=True)
        dy = err * (1.0 / D)
        dg_ref[...] += jnp.sum(dy * xh, axis=0, keepdims=True)
        dxh = dy * gv
        dx_ref[...] = r * (dxh - xh * jnp.mean(dxh * xh, axis=-1, keepdims=True))

    row = pl.BlockSpec((tm, D), lambda i: (i, 0))
    return pl.pallas_call(
        body, name="loss_head", grid=(T // tm,),
        in_specs=[row, pl.BlockSpec((1, D), lambda i: (0, 0)), row],
        out_specs=[row, pl.BlockSpec((1, 128), lambda i: (0, 0)), pl.BlockSpec((1, D), lambda i: (0, 0))],
        out_shape=[SDS((T, D), F32), SDS((1, 128), F32), SDS((1, D), F32)], compiler_params=_params("arbitrary"),
    )(x, g, tgt)


def _gmlp_core(u, v, lng, lnb, wm_ref, bst_ref, pair):
    ug, dug = _gelu_and_grad(u)
    vg, dvg = _gelu_and_grad(v)
    mu = _gsum64(vg) * (1.0 / 64)
    d = vg - mu
    var = _gsum64(d * d) * (1.0 / 64)
    rstd = lax.rsqrt(var + NORM_EPS)
    xh = d * rstd
    vn = xh * lng + lnb
    vnb = vn.astype(BF16)
    lo = _lane(u.shape) < 64
    g0, g1 = 2 * pair, 2 * pair + 1
    mixed = jnp.where(lo, _dot(wm_ref[g0], vnb) + bst_ref[:, g0:g0 + 1], _dot(wm_ref[g1], vnb) + bst_ref[:, g1:g1 + 1])
    return ug, dug, dvg, rstd, xh, vnb, mixed, lo


def _gmlp_fwd(proj, lng, lnb, wm, bst, tag):
    T = proj.shape[0]

    def body(u_ref, v_ref, z_ref, lng_ref, lnb_ref, wm_ref, bst_ref, y_ref):
        for pair in range(2):
            sl = slice(128 * pair, 128 * pair + 128)
            ug, _, _, _, _, _, mixed, _ = _gmlp_core(u_ref[:, sl], v_ref[:, sl], lng_ref[:, sl], lnb_ref[:, sl],
                                                     wm_ref, bst_ref, pair)
            sz, _ = _silu_and_grad(z_ref[:, sl])
            y_ref[:, sl] = (ug * mixed * sz).astype(BF16)

    col = lambda c: pl.BlockSpec((CHUNK, A_WIDTH), lambda i, c=c: (i, c // A_WIDTH))
    full = lambda a: pl.BlockSpec(a.shape, lambda i, n=a.ndim: (0,) * n)
    return pl.pallas_call(
        body, name=f"gmlp_fwd_{tag}", grid=(T // CHUNK,),
        in_specs=[col(COL_AU), col(COL_AV), col(COL_AZ), full(lng), full(lnb), full(wm), full(bst)],
        out_specs=pl.BlockSpec((CHUNK, A_WIDTH), lambda i: (i, 0)), out_shape=SDS((T, A_WIDTH), BF16),
        compiler_params=_params("parallel"),
    )(proj, proj, proj, lng, lnb, wm, bst)


def _gmlp_bwd(proj, dy, lng, lnb, wm, wmt, bst, tag):
    T = proj.shape[0]
    n = T // CHUNK

    def body(u_ref, v_ref, z_ref, dy_ref, lng_ref, lnb_ref, wm_ref, wmt_ref, bst_ref,
             da_ref, dwm_ref, dbst_ref, dlng_ref, dlnb_ref):
        @pl.when(pl.program_id(0) == 0)
        def _():
            dwm_ref[...] = jnp.zeros_like(dwm_ref)
            dbst_ref[...] = jnp.zeros_like(dbst_ref)
            dlng_ref[...] = jnp.zeros_like(dlng_ref)
            dlnb_ref[...] = jnp.zeros_like(dlnb_ref)

        lane = _lane((CHUNK, 128))
        dbst = dbst_ref[...]
        for pair in range(2):
            sl = slice(128 * pair, 128 * pair + 128)
            lng_p = lng_ref[:, sl]
            ug, dug, dvg, rstd, xh, vnb, mixed, lo = _gmlp_core(u_ref[:, sl], v_ref[:, sl], lng_p, lnb_ref[:, sl],
                                                                wm_ref, bst_ref, pair)
            sz, dsz = _silu_and_grad(z_ref[:, sl])
            dyv = dy_ref[:, sl]
            out = ug * mixed
            dz = dyv * out * dsz
            dout = dyv * sz
            du = dout * mixed * dug
            dmix = dout * ug
            g0, g1 = 2 * pair, 2 * pair + 1
            dm0 = jnp.where(lo, dmix, 0.0)
            dm1 = jnp.where(lo, 0.0, dmix)
            dbst = dbst + jnp.where(lane == g0, jnp.sum(dm0, axis=-1, keepdims=True), 0.0)
            dbst = dbst + jnp.where(lane == g1, jnp.sum(dm1, axis=-1, keepdims=True), 0.0)
            dwm_ref[g0] += _dot_nt(dm0.astype(BF16), vnb)
            dwm_ref[g1] += _dot_nt(dm1.astype(BF16), vnb)
            dmb = dmix.astype(BF16)
            dvn = jnp.where(lo, _dot(wmt_ref[g0], dmb), _dot(wmt_ref[g1], dmb))
            dlng_ref[:, sl] += jnp.sum(dvn * xh, axis=0, keepdims=True)
            dlnb_ref[:, sl] += jnp.sum(dvn, axis=0, keepdims=True)
            dxh = dvn * lng_p
            m1 = _gsum64(dxh) * (1.0 / 64)
            m2 = _gsum64(dxh * xh) * (1.0 / 64)
            dv = rstd * (dxh - m1 - xh * m2) * dvg
            da_ref[:, COL_AU + 128 * pair:COL_AU + 128 * pair + 128] = du.astype(BF16)
            da_ref[:, COL_AV + 128 * pair:COL_AV + 128 * pair + 128] = dv.astype(BF16)
            da_ref[:, COL_AZ + 128 * pair:COL_AZ + 128 * pair + 128] = dz.astype(BF16)
        dbst_ref[...] = dbst

        @pl.when(pl.program_id(0) == n - 1)
        def _():
            causal = _lane((CHUNK, CHUNK)) <= _row((CHUNK, CHUNK))
            for g in range(A_GROUPS):
                dwm_ref[g] = jnp.where(causal, dwm_ref[g], 0.0)

    col = lambda c: pl.BlockSpec((CHUNK, A_WIDTH), lambda i, c=c: (i, c // A_WIDTH))
    full = lambda a: pl.BlockSpec(a.shape, lambda i, n=a.ndim: (0,) * n)
    acc = lambda s: pl.BlockSpec(s, lambda i, n=len(s): (0,) * n)
    return pl.pallas_call(
        body, name=f"gmlp_bwd_{tag}", grid=(n,),
        in_specs=[col(COL_AU), col(COL_AV), col(COL_AZ), pl.BlockSpec((CHUNK, A_WIDTH), lambda i: (i, 0)),
                  full(lng), full(lnb), full(wm), full(wmt), full(bst)],
        out_specs=[pl.BlockSpec((CHUNK, 3 * A_WIDTH), lambda i: (i, 0)), acc((A_GROUPS, CHUNK, CHUNK)),
                   acc((CHUNK, 128)), acc((1, A_WIDTH)), acc((1, A_WIDTH))],
        out_shape=[SDS((T, 3 * A_WIDTH), BF16), SDS((A_GROUPS, CHUNK, CHUNK), F32), SDS((CHUNK, 128), F32),
                   SDS((1, A_WIDTH), F32), SDS((1, A_WIDTH), F32)],
        compiler_params=_params("arbitrary"),
    )(proj, proj, proj, dy, lng, lnb, wm, wmt, bst)


def _hgrn_consts():
    r, c = _row((CHUNK, CHUNK)), _lane((CHUNK, CHUNK))
    same = (r >> SUB_SHIFT) == (c >> SUB_SHIFT)
    lsub = jnp.where(same & (c <= r), 1.0, 0.0).astype(BF16)
    usub = jnp.where(same & (c >= r), 1.0, 0.0).astype(BF16)
    bsub = jnp.where(same, 1.0, 0.0).astype(BF16)
    return lsub, usub, bsub


def _hgrn_gates(qv, zf, lbp):
    sq, dsq = _silu_and_grad(qv)
    qt = sq * Q_SCALE
    sg = _sigmoid(zf)
    sgn = _sigmoid(-zf)
    f = lbp + (1.0 - lbp) * sg
    g = jnp.log(jnp.maximum(f, F_FLOOR))
    kf = (1.0 - lbp) * sgn
    return qt, dsq, sg, sgn, f, g, kf


def _hgrn_intra_fwd(qt, kf, b, v, mbd):
    rid = _row((SUB, 128))
    parts = []
    for s in range(SUB):
        e = jnp.exp(b - b[s:s + 1, :])
        parts.append(jnp.where(rid >= s, qt * kf[s:s + 1, :] * e, 0.0))
    a = _dot(jnp.concatenate(parts, axis=0).astype(BF16), mbd)
    o = jnp.zeros((SUB, 128), F32)
    for s in range(SUB):
        o = o + a[SUB * s:SUB * s + SUB, :] * v[s:s + 1, :]
    return o


def _hgrn_intra_bwd(qt, kf, b, v, do, mbd, rsum):
    rid = _row((SUB, 128))
    ps, das, kes, es = [], [], [], []
    for s in range(SUB):
        e = jnp.where(rid >= s, jnp.exp(b - b[s:s + 1, :]), 0.0)
        ke = kf[s:s + 1, :] * e
        es.append(e)
        kes.append(ke)
        ps.append(qt * ke)
        das.append(do * v[s:s + 1, :])
    a = _dot(jnp.concatenate(ps, axis=0).astype(BF16), mbd)
    da = _dot(jnp.concatenate(das, axis=0).astype(BF16), mbd)
    dqt = jnp.zeros((SUB, 128), F32)
    xs, ys = [], []
    for s in range(SUB):
        da_s = da[SUB * s:SUB * s + SUB, :]
        dqt = dqt + da_s * kes[s]
        xs.append(a[SUB * s:SUB * s + SUB, :] * do)
        ys.append(da_s * qt * es[s])
    dv = _dot(rsum, jnp.concatenate(xs, axis=0).astype(BF16))
    dkf = _dot(rsum, jnp.concatenate(ys, axis=0).astype(BF16))
    return dqt, dkf, dv


def _hgrn_norm_gate(o, z, onorm):
    ms = _gsum64(o * o) * (1.0 / 64)
    r = lax.rsqrt(ms + NORM_EPS)
    xh = o * r
    sz, dsz = _silu_and_grad(z)
    return xh, r, sz, dsz, xh * onorm


def _hgrn_fwd(proj, lb, onorm, tag):
    T = proj.shape[0]
    n = T // CHUNK
    nsub = CHUNK // SUB

    def body(q_ref, f_ref, i_ref, z_ref, lb_ref, on_ref, y_ref, o_ref, s0_ref, st_ref):
        @pl.when(pl.program_id(0) == 0)
        def _():
            st_ref[...] = jnp.zeros_like(st_ref)

        lsub, _, bsub = _hgrn_consts()
        mbd = _block_diag64()
        bdmask = mbd > 0
        rid = _row((CHUNK, 128))
        for pair in range(2):
            sl = slice(128 * pair, 128 * pair + 128)
            qt, _, _, _, _, g, kf = _hgrn_gates(q_ref[:, sl], f_ref[:, sl], lb_ref[:, sl])
            v = i_ref[:, sl]
            b = _dot3_left(lsub, g)
            bl = _dot3_left(bsub, g)
            qh = (qt * jnp.exp(b)).astype(BF16)
            kh = kf * jnp.exp(bl - b)
            dec = jnp.exp(bl)
            vtb = v.T.astype(BF16)
            st = st_ref[pair]
            s0_ref[0, pair] = st
            outs = []
            for sub in range(nsub):
                rs = slice(SUB * sub, SUB * sub + SUB)
                o_inter = _dot_nt(qh[rs], st.astype(BF16))
                outs.append(o_inter + _hgrn_intra_fwd(qt[rs], kf[rs], b[rs], v[rs], mbd))
                khm = jnp.where((rid >> SUB_SHIFT) == sub, kh, 0.0).astype(BF16)
                st = jnp.where(bdmask, st * dec[SUB * sub:SUB * sub + 1, :] + _dot(vtb, khm), 0.0)
            st_ref[pair] = st
            o = jnp.concatenate(outs, axis=0)
            o_ref[:, sl] = o
            _, _, sz, _, on = _hgrn_norm_gate(o, z_ref[:, sl], on_ref[:, sl])
            y_ref[:, sl] = (on * sz).astype(BF16)

    col = lambda c: pl.BlockSpec((CHUNK, B_WIDTH), lambda i, c=c: (i, c // B_WIDTH))
    full = lambda a: pl.BlockSpec(a.shape, lambda i, n=a.ndim: (0,) * n)
    return pl.pallas_call(
        body, name=f"hgrn_fwd_{tag}", grid=(n,),
        in_specs=[col(COL_BQ), col(COL_BF), col(COL_BI), col(COL_BZ), full(lb), full(onorm)],
        out_specs=[pl.BlockSpec((CHUNK, B_WIDTH), lambda i: (i, 0)), pl.BlockSpec((CHUNK, B_WIDTH), lambda i: (i, 0)),
                   pl.BlockSpec((1, 2, 128, 128), lambda i: (i, 0, 0, 0))],
        out_shape=[SDS((T, B_WIDTH), BF16), SDS((T, B_WIDTH), F32), SDS((n, 2, 128, 128), F32)],
        scratch_shapes=[pltpu.VMEM((2, 128, 128), F32)], compiler_params=_params("arbitrary"),
    )(proj, proj, proj, proj, lb, onorm)


def _hgrn_bwd(proj, dy, o_saved, s0, lb, onorm, tag):
    T = proj.shape[0]
    n = T // CHUNK
    nsub = CHUNK // SUB

    def body(q_ref, f_ref, i_ref, z_ref, dy_ref, o_ref, s0_ref, lb_ref, on_ref,
             db_ref, dlb_ref, don_ref, dst_ref, sts_ref):
        @pl.when(pl.program_id(0) == 0)
        def _():
            dst_ref[...] = jnp.zeros_like(dst_ref)
            dlb_ref[...] = jnp.zeros_like(dlb_ref)
            don_ref[...] = jnp.zeros_like(don_ref)

        lsub, usub, bsub = _hgrn_consts()
        mbd = _block_diag64()
        bdmask = mbd > 0
        rsum = jnp.where((_lane((SUB, SUB * SUB)) >> SUB_SHIFT) == _row((SUB, SUB * SUB)), 1.0, 0.0).astype(BF16)
        for pair in range(2):
            sl = slice(128 * pair, 128 * pair + 128)
            lbp = lb_ref[:, sl]
            qv, zf = q_ref[:, sl], f_ref[:, sl]
            qt, dsq, sg, sgn, f, g, kf = _hgrn_gates(qv, zf, lbp)
            v = i_ref[:, sl]
            b = _dot3_left(lsub, g)
            bl = _dot3_left(bsub, g)
            eb = jnp.exp(b)
            ekb = jnp.exp(bl - b)
            qhb = (qt * eb).astype(BF16)
            khb = (kf * ekb).astype(BF16)
            dec = jnp.exp(bl)
            vb = v.astype(BF16)
            onp = on_ref[:, sl]
            ov = o_ref[:, sl]
            xh, r, sz, dsz, on = _hgrn_norm_gate(ov, z_ref[:, sl], onp)
            dyv = dy_ref[:, sl]
            dz = dyv * on * dsz
            don = dyv * sz
            cn = jnp.sum(don * xh, axis=0, keepdims=True)
            don_ref[...] += cn + pltpu.roll(cn, 64, axis=1)
            dxo = don * onp
            do = r * (dxo - xh * (_gsum64(dxo * xh) * (1.0 / 64)))
            dob = do.astype(BF16)
            st = s0_ref[0, pair]
            for sub in range(nsub):
                rs = slice(SUB * sub, SUB * sub + SUB)
                sts_ref[sub] = st
                st = jnp.where(bdmask, st * dec[SUB * sub:SUB * sub + 1, :] + _dot_tn(vb[rs], khb[rs]), 0.0)
            gst = dst_ref[pair]
            dqt_p, dkf_p, dv_p, dbl_p = [None] * nsub, [None] * nsub, [None] * nsub, [None] * nsub
            for sub in reversed(range(nsub)):
                rs = slice(SUB * sub, SUB * sub + SUB)
                st_in = sts_ref[sub]
                gb = gst.astype(BF16)
                dqh = _dot(dob[rs], st_in.astype(BF16))
                dkh = _dot(vb[rs], gb)
                dv_inter = _dot_nt(khb[rs], gb)
                ddec = jnp.sum(gst * st_in, axis=0, keepdims=True)
                dec_row = dec[SUB * sub:SUB * sub + 1, :]
                gst = jnp.where(bdmask, gst * dec_row + _dot_tn(dob[rs], qhb[rs]), 0.0)
                dqt_i, dkf_i, dv_i = _hgrn_intra_bwd(qt[rs], kf[rs], b[rs], v[rs], do[rs], mbd, rsum)
                dkf_inter = dkh * ekb[rs]
                dqt_p[sub] = dqh * eb[rs] + dqt_i
                dkf_p[sub] = dkf_inter + dkf_i
                dv_p[sub] = dv_inter + dv_i
                row = jnp.sum(kf[rs] * dkf_inter, axis=0, keepdims=True) + ddec * dec_row
                dbl_p[sub] = jnp.broadcast_to(row, (SUB, 128))
            dst_ref[pair] = gst
            dqt = jnp.concatenate(dqt_p, axis=0)
            dkf = jnp.concatenate(dkf_p, axis=0)
            dv = jnp.concatenate(dv_p, axis=0)
            dg = _dot3_left(usub, qt * dqt - kf * dkf) + jnp.concatenate(dbl_p, axis=0)
            df = jnp.where(f > F_FLOOR, dg / f, 0.0)
            dlb_ref[:, sl] += jnp.sum(df * (1.0 - sg) - dkf * sgn, axis=0, keepdims=True)
            dfl = (1.0 - lbp) * sg * sgn * (df - dkf)
            dq = dqt * Q_SCALE * dsq
            db_ref[:, 0 * B_WIDTH + 128 * pair:0 * B_WIDTH + 128 * pair + 128] = dq.astype(BF16)
            db_ref[:, 1 * B_WIDTH + 128 * pair:1 * B_WIDTH + 128 * pair + 128] = dfl.astype(BF16)
            db_ref[:, 2 * B_WIDTH + 128 * pair:2 * B_WIDTH + 128 * pair + 128] = dv.astype(BF16)
            db_ref[:, 3 * B_WIDTH + 128 * pair:3 * B_WIDTH + 128 * pair + 128] = dz.astype(BF16)

    rev = lambda c: pl.BlockSpec((CHUNK, B_WIDTH), lambda i, c=c: (n - 1 - i, c // B_WIDTH))
    full = lambda a: pl.BlockSpec(a.shape, lambda i, n_=a.ndim: (0,) * n_)
    acc = lambda s: pl.BlockSpec(s, lambda i, n_=len(s): (0,) * n_)
    return pl.pallas_call(
        body, name=f"hgrn_bwd_{tag}", grid=(n,),
        in_specs=[rev(COL_BQ), rev(COL_BF), rev(COL_BI), rev(COL_BZ),
                  pl.BlockSpec((CHUNK, B_WIDTH), lambda i: (n - 1 - i, 1)),
                  pl.BlockSpec((CHUNK, B_WIDTH), lambda i: (n - 1 - i, 0)),
                  pl.BlockSpec((1, 2, 128, 128), lambda i: (n - 1 - i, 0, 0, 0)), full(lb), full(onorm)],
        out_specs=[pl.BlockSpec((CHUNK, 4 * B_WIDTH), lambda i: (n - 1 - i, 0)), acc((1, B_WIDTH)), acc((1, 128))],
        out_shape=[SDS((T, 4 * B_WIDTH), BF16), SDS((1, B_WIDTH), F32), SDS((1, 128), F32)],
        scratch_shapes=[pltpu.VMEM((2, 128, 128), F32), pltpu.VMEM((nsub, 128, 128), F32)],
        compiler_params=_params("arbitrary"),
    )(proj, proj, proj, proj, dy, o_saved, s0, lb, onorm)


def _lb_fwd(hgrn_lb):
    assert hgrn_lb.shape[0] == 2

    def body(x_ref, o_ref):
        x0, x1 = x_ref[0:1, :], x_ref[1:2, :]
        m = jnp.maximum(x0, x1)
        e0, e1 = jnp.exp(x0 - m), jnp.exp(x1 - m)
        p0, p1 = e0 / (e0 + e1), e1 / (e0 + e1)
        o_ref[0:1, :] = jnp.clip(p0 - p0, 0.0, 1.0 - 1e-6)
        o_ref[1:2, :] = jnp.clip((p0 + p1) - p0, 0.0, 1.0 - 1e-6)

    return pl.pallas_call(body, name="lb_fwd", out_shape=SDS(hgrn_lb.shape, F32))(hgrn_lb)


def _lb_bwd(hgrn_lb, dlb):
    def body(x_ref, d_ref, o_ref):
        x0, x1 = x_ref[0:1, :], x_ref[1:2, :]
        m = jnp.maximum(x0, x1)
        e0, e1 = jnp.exp(x0 - m), jnp.exp(x1 - m)
        p0, p1 = e0 / (e0 + e1), e1 / (e0 + e1)
        val = (p0 + p1) - p0
        dp1 = jnp.where((val > 0.0) & (val < 1.0 - 1e-6), d_ref[1:2, :], 0.0)
        inner = p1 * dp1
        o_ref[0:1, :] = p0 * (0.0 - inner)
        o_ref[1:2, :] = p1 * (dp1 - inner)

    return pl.pallas_call(body, name="lb_bwd", out_shape=SDS(hgrn_lb.shape, F32))(hgrn_lb, dlb)


def _fox_prep(proj, bf, tag):
    T = proj.shape[0]
    n = T // CHUNK

    def body(q0_ref, q1_ref, k0_ref, k1_ref, v0_ref, v1_ref, fl_ref, bf_ref, qo_ref, ko_ref, vt_ref, carry_ref):
        for p, v_ref in enumerate((v0_ref, v0_ref, v1_ref, v1_ref)):
            vt_ref[p, 0] = v_ref[:, 128 * (p % 2):128 * (p % 2) + 128].T.astype(BF16)

        @pl.when(pl.program_id(0) == 0)
        def _():
            carry_ref[...] = jnp.zeros_like(carry_ref)

        ltri = jnp.where(_lane((CHUNK, CHUNK)) <= _row((CHUNK, CHUNK)), 1.0, 0.0).astype(BF16)
        lf = jax.nn.log_sigmoid(fl_ref[...] + bf_ref[...])
        c = _dot3_left(ltri, lf) + carry_ref[...]
        carry_ref[...] = c[CHUNK - 1:CHUNK, :]
        lane = _lane((CHUNK, 128))
        feat = lane < 64
        ones_q = (lane >= 67) & (lane <= 69)
        ones_k = (lane >= 64) & (lane <= 66)
        qrefs, krefs = (q0_ref, q1_ref), (k0_ref, k1_ref)
        for h in range(C_HEADS):
            blk = slice(128 * ((h // 2) % 2), 128 * ((h // 2) % 2) + 128)
            qp, kp = qrefs[h // 4][:, blk], krefs[h // 4][:, blk]
            if h % 2:
                qp, kp = pltpu.roll(qp, 64, axis=1), pltpu.roll(kp, 64, axis=1)
            ch = jnp.broadcast_to(c[:, h:h + 1], (CHUNK, 128))
            hi = ch.astype(BF16).astype(F32)
            r1 = ch - hi
            mid = r1.astype(BF16).astype(F32)
            lo = r1 - mid
            aq = jnp.where(lane == 64, hi, jnp.where(lane == 65, mid, jnp.where(lane == 66, lo,
                           jnp.where(ones_q, 1.0, 0.0))))
            ak = jnp.where(lane == 67, -hi, jnp.where(lane == 68, -mid, jnp.where(lane == 69, -lo,
                           jnp.where(ones_k, 1.0, 0.0))))
            qo_ref[:, 128 * h:128 * h + 128] = jnp.where(feat, qp * Q_SCALE, aq).astype(BF16)
            ko_ref[:, 128 * h:128 * h + 128] = jnp.where(feat, kp, ak).astype(BF16)

    w = 256
    col = lambda c: pl.BlockSpec((CHUNK, w), lambda i, c=c: (i, c // w))
    return pl.pallas_call(
        body, name=f"fox_prep_{tag}", grid=(n,),
        in_specs=[col(COL_CQ), col(COL_CQ + w), col(COL_CK), col(COL_CK + w), col(COL_CV), col(COL_CV + w),
                  pl.BlockSpec((CHUNK, 128), lambda i: (i, COL_CF // 128)), pl.BlockSpec((1, 128), lambda i: (0, 0))],
        out_specs=[pl.BlockSpec((CHUNK, C_HEADS * 128), lambda i: (i, 0))] * 2
        + [pl.BlockSpec((C_HEADS // 2, 1, 128, CHUNK), lambda i: (0, i, 0, 0))],
        out_shape=[SDS((T, C_HEADS * 128), BF16)] * 2 + [SDS((C_HEADS // 2, n, 128, CHUNK), BF16)],
        scratch_shapes=[pltpu.VMEM((1, 128), F32)], compiler_params=_params("arbitrary"),
    )(proj, proj, proj, proj, proj, proj, proj, bf)


FOX_TILE = 256
FOX_KEYS = 512


def _fox_mask(tk, tq, k0, q0):
    return (_row((tk, tq)) + (k0 - q0)) <= _lane((tk, tq))


def _ride_refs(ride, rest, n_out, n_scratch):
    n = ride.n if ride else 0
    srcs, rest = rest[:n], rest[n:]
    outs, rest = rest[:n_out], rest[n_out:]
    dsts, rest = rest[:n], rest[n:]
    return srcs, outs, dsts, rest[:n_scratch], rest[n_scratch:]


def _ride_start(ride, grid, srcs, dsts, sems):
    if ride:
        first = functools.reduce(lambda a, b: a & b, [pl.program_id(d) == 0 for d in range(len(grid))])
        pl.when(first)(lambda: ride.start(srcs, dsts, sems))


def _ride_wait(ride, grid, srcs, dsts, sems):
    if ride:
        last = functools.reduce(lambda a, b: a & b, [pl.program_id(d) == n - 1 for d, n in enumerate(grid)])
        pl.when(last)(lambda: ride.wait(srcs, dsts, sems))


def _fox_fwd(qt, kt, vt, proj, tag, ride=None):
    T = proj.shape[0]
    tq, tk = _tile(T, FOX_TILE), _tile(T, FOX_KEYS)
    nq, nsub = T // tq, tk // CHUNK
    npair = C_HEADS // 2

    def body(q_ref, k_ref, vt_ref, z_ref, *rest):
        ride_srcs, (o_ref, lse_ref, y_ref), ride_dsts, (acc_ref, st_ref, pt_ref), ride_sems = _ride_refs(ride, rest, 3, 3)
        i = pl.program_id(1)
        _ride_start(ride, (npair, nq), ride_srcs, ride_dsts, ride_sems)

        qs = (q_ref[:, 0:128], q_ref[:, 128:256])
        acc_ref[...] = jnp.zeros_like(acc_ref)
        pt_ref[...] = jnp.zeros_like(pt_ref)
        nfull = (i * tq) // tk

        def scores(j):
            kb = k_ref[pl.ds(pl.multiple_of(j * tk, tk), tk), :]
            return tuple(_dot_nt(kb[:, 128 * h:128 * h + 128], qs[h]) for h in range(2))

        def weigh(j, h):
            rows = slice(64 * h, 64 * h + 64)
            pv = _dot(vt_ref[0, nsub * j, rows, :], pt_ref[h, 0:CHUNK, :])
            for c in range(1, nsub):
                pv = pv + _dot(vt_ref[0, nsub * j + c, rows, :], pt_ref[h, CHUNK * c:CHUNK * c + CHUNK, :])
            return pv

        def block(j, carry, diagonal):
            nxt = () if diagonal else scores(j + 1)
            pvs = [weigh(jnp.maximum(j - 1, 0), h) for h in range(2)]
            new = []
            for h in range(2):
                m, l, alpha_prev = carry[3 * h:3 * h + 3]
                st = st_ref[h]
                if diagonal:
                    st = jnp.where(_fox_mask(tk, tq, j * tk, i * tq), st, -jnp.inf)
                m_new = jnp.maximum(m, _colreduce(st, jnp.maximum))
                pt = jnp.exp(st - m_new)
                alpha = jnp.exp(m - m_new)
                rows = slice(64 * h, 64 * h + 64)
                acc_ref[rows, :] = alpha_prev * acc_ref[rows, :] + pvs[h]
                pt_ref[h] = pt.astype(BF16)
                new += [m_new, alpha * l + _colreduce(pt, jnp.add), alpha]
            for h, st in enumerate(nxt):
                st_ref[h] = st
            return tuple(new)

        for h, st in enumerate(scores(0)):
            st_ref[h] = st
        init = (jnp.full((1, tq), -jnp.inf, F32), jnp.zeros((1, tq), F32), jnp.ones((1, tq), F32)) * 2
        carry = lax.fori_loop(0, nfull, lambda j, c: block(j, c, False), init)
        m0, l0, a0, m1, l1, a1 = block(nfull, carry, True)
        for h, alpha in enumerate((a0, a1)):
            rows = slice(64 * h, 64 * h + 64)
            acc_ref[rows, :] = alpha * acc_ref[rows, :] + weigh(nfull, h)
        inv = jnp.where(_row((128, tq)) < 64, 1.0 / l0, 1.0 / l1)
        o = (acc_ref[...] * inv).T
        o_ref[...] = o
        r8 = _row((8, tq))
        lse_ref[0, 0] = jnp.where(r8 == 0, m0 + jnp.log(l0), jnp.where(r8 == 1, m1 + jnp.log(l1), 0.0))
        sz, _ = _silu_and_grad(z_ref[...])
        y_ref[...] = (o * sz).astype(BF16)
        _ride_wait(ride, (npair, nq), ride_srcs, ride_dsts, ride_sems)

    blk = pl.BlockSpec((tq, 128), lambda p, i: (i, p))
    extra = ride or _ChipExchange("gather", ())
    return pl.pallas_call(
        body, name=f"fox_fwd_{tag}", grid=(npair, nq),
        in_specs=[pl.BlockSpec((tq, 256), lambda p, i: (i, p)), pl.BlockSpec((T, 256), lambda p, i: (0, p)),
                  pl.BlockSpec((1, T // CHUNK, 128, CHUNK), lambda p, i: (p, 0, 0, 0)),
                  pl.BlockSpec((tq, 128), lambda p, i: (i, COL_CZ // 128 + p))] + extra.in_specs,
        out_specs=[blk, pl.BlockSpec((1, 1, 8, tq), lambda p, i: (p, i, 0, 0)), blk] + extra.out_specs,
        out_shape=[SDS((T, C_WIDTH), F32), SDS((npair, nq, 8, tq), F32), SDS((T, C_WIDTH), BF16)] + extra.out_shape,
        scratch_shapes=[pltpu.VMEM((128, tq), F32), pltpu.VMEM((2, tk, tq), F32), pltpu.VMEM((2, tk, tq), BF16)]
        + (extra.scratch if ride else []),
        compiler_params=pltpu.CompilerParams(dimension_semantics=("arbitrary", "arbitrary"), vmem_limit_bytes=VMEM_LIMIT,
                                             has_side_effects=bool(ride)),
    )(qt, kt, vt, proj, *extra.sources)


def _fox_bwd_prep(proj, dy, o, tag):
    T = proj.shape[0]
    tq = _tile(T, FOX_TILE)

    def body(z0_ref, z1_ref, dy_ref, o_ref, do_ref, dl_ref, dz_ref):
        sel = jnp.where((_lane((16, 128)) >> 6) == _row((16, 128)), 1.0, 0.0).astype(BF16)
        for p, z_ref in enumerate((z0_ref, z0_ref, z1_ref, z1_ref)):
            sl = slice(128 * p, 128 * p + 128)
            sz, dsz = _silu_and_grad(z_ref[:, 128 * (p % 2):128 * (p % 2) + 128])
            dyv, ov = dy_ref[:, sl], o_ref[:, sl]
            do = dyv * sz
            do_ref[:, sl] = do.astype(BF16)
            dz_ref[:, sl] = (dyv * ov * dsz).astype(BF16)
            hi, mid, lo = _split3(do * ov)
            dl_ref[p, 0] = (_dot_nt(sel, hi) + _dot_nt(sel, mid) + _dot_nt(sel, lo))[0:8, :]

    w = 256
    blk = pl.BlockSpec((tq, C_WIDTH), lambda i: (i, 0))
    return pl.pallas_call(
        body, name=f"fox_bwd_prep_{tag}", grid=(T // tq,),
        in_specs=[pl.BlockSpec((tq, w), lambda i: (i, COL_CZ // w)), pl.BlockSpec((tq, w), lambda i: (i, COL_CZ // w + 1)),
                  pl.BlockSpec((tq, C_WIDTH), lambda i: (i, (A_WIDTH + B_WIDTH) // C_WIDTH)), blk],
        out_specs=[blk, pl.BlockSpec((C_HEADS // 2, 1, 8, tq), lambda i: (0, i, 0, 0)), blk],
        out_shape=[SDS((T, C_WIDTH), BF16), SDS((C_HEADS // 2, T // tq, 8, tq), F32), SDS((T, C_WIDTH), BF16)],
        compiler_params=_params("parallel"),
    )(proj, proj, dy, o)


def _fox_bwd(qt, kt, proj, do, lse, delta, tag, ride=None):
    T = proj.shape[0]
    tq, tk = _tile(T, FOX_TILE), _tile(T, FOX_KEYS)
    nq, nk = T // tq, T // tk
    ndiag = tk // tq
    npair = C_HEADS // 2

    def body(q_ref, k_ref, v_ref, do_ref, lse_ref, dl_ref, *rest):
        ride_srcs, (dq_ref, dk_ref, dv_ref), ride_dsts, scratch, ride_sems = _ride_refs(ride, rest, 3, 4)
        dvacc_ref, sc_ref, pt_ref, ds_ref = scratch
        j = pl.program_id(1)
        first = (j * tk) // tq
        _ride_start(ride, (npair, nk), ride_srcs, ride_dsts, ride_sems)

        @pl.when(j == 0)
        def _():
            dq_ref[...] = jnp.zeros_like(dq_ref)

        dk_ref[...] = jnp.zeros_like(dk_ref)
        dvacc_ref[...] = jnp.zeros_like(dvacc_ref)
        ks = (k_ref[:, 0:128], k_ref[:, 128:256])
        kts = tuple(k.astype(F32).T.astype(BF16) for k in ks)
        vb = v_ref[...].astype(BF16)
        lo = _lane((tq, 128)) < 64

        def operands(i):
            q0 = pl.multiple_of(i * tq, tq)
            qb = q_ref[pl.ds(q0, tq), :]
            dob = do_ref[pl.ds(q0, tq), :]
            qhs = (qb[:, 0:128], qb[:, 128:256])
            dohs = (jnp.where(lo, dob, jnp.zeros_like(dob)), jnp.where(lo, jnp.zeros_like(dob), dob))
            return qhs, dohs

        def scores(i):
            qhs, dohs = operands(i)
            return tuple(_dot_nt(ks[h], qhs[h]) for h in range(2)) + tuple(_dot_nt(vb, dohs[h]) for h in range(2))

        def park(sc):
            for a, s in enumerate(sc):
                sc_ref[a] = s

        def grads(i):
            qhs, dohs = operands(i)
            dvacc_ref[...] += _dot(jnp.concatenate([pt_ref[0], pt_ref[1]], axis=1), jnp.concatenate(dohs, axis=0))
            for h in range(2):
                dk_ref[:, 128 * h:128 * h + 128] += _dot(ds_ref[h], qhs[h])
                dq_ref[h, i] += _dot(kts[h], ds_ref[h])

        def block(i, diagonal, opening):
            nxt = scores(jnp.minimum(i + 1, nq - 1))
            if not opening:
                grads(i - 1)
            lsev = lse_ref[0, i]
            dlv = dl_ref[0, i]
            for h in range(2):
                pt = jnp.exp(sc_ref[h] - lsev[h:h + 1, :])
                if diagonal:
                    pt = jnp.where(_fox_mask(tk, tq, j * tk, i * tq), pt, 0.0)
                ds_ref[h] = (pt * (sc_ref[2 + h] - dlv[h:h + 1, :])).astype(BF16)
                pt_ref[h] = pt.astype(BF16)
            park(nxt)

        park(scores(first))
        for d in range(ndiag):
            block(first + d, True, d == 0)

        def step(i, carry):
            block(i, False, False)
            return carry

        lax.fori_loop(first + ndiag, nq, step, 0)
        grads(nq - 1)
        dv_ref[...] = dvacc_ref[...].astype(BF16)
        _ride_wait(ride, (npair, nk), ride_srcs, ride_dsts, ride_sems)

    full = lambda w: pl.BlockSpec((T, w), lambda p, j: (0, p))
    stat = pl.BlockSpec((1, nq, 8, tq), lambda p, j: (p, 0, 0, 0))
    extra = ride or _ChipExchange("gather", ())
    return pl.pallas_call(
        body, name=f"fox_bwd_{tag}", grid=(npair, nk),
        in_specs=[full(256), pl.BlockSpec((tk, 256), lambda p, j: (j, p)),
                  pl.BlockSpec((tk, 128), lambda p, j: (j, COL_CV // 128 + p)), full(128), stat, stat] + extra.in_specs,
        out_specs=[pl.BlockSpec((2, nq, 128, tq), lambda p, j: (p, 0, 0, 0)), pl.BlockSpec((tk, 256), lambda p, j: (j, p)),
                   pl.BlockSpec((tk, 128), lambda p, j: (j, p))] + extra.out_specs,
        out_shape=[SDS((C_HEADS, nq, 128, tq), F32), SDS((T, C_HEADS * 128), F32), SDS((T, C_WIDTH), BF16)]
        + extra.out_shape,
        scratch_shapes=[pltpu.VMEM((tk, 128), F32), pltpu.VMEM((4, tk, tq), F32), pltpu.VMEM((2, tk, tq), BF16),
                        pltpu.VMEM((2, tk, tq), BF16)] + (extra.scratch if ride else []),
        compiler_params=pltpu.CompilerParams(dimension_semantics=("arbitrary", "arbitrary"), vmem_limit_bytes=VMEM_LIMIT,
                                             has_side_effects=bool(ride)),
    )(qt, kt, proj, do, lse, delta, *extra.sources)


def _fox_bwd_post(dqt, dkt, proj, bf, tag):
    T = proj.shape[0]
    tq = _tile(T, FOX_TILE)
    n = T // tq

    def body(dq_ref, dk_ref, fl_ref, bf_ref, oq_ref, ok_ref, ofl_ref, dbf_ref, carry_ref):
        @pl.when(pl.program_id(0) == 0)
        def _():
            carry_ref[...] = jnp.zeros_like(carry_ref)
            dbf_ref[...] = jnp.zeros_like(dbf_ref)

        lane = _lane((tq, 128))
        lo = lane < 64
        dqs = [dq_ref[h, 0].T for h in range(C_HEADS)]
        dc = jnp.zeros((tq, 128), F32)
        for h in range(C_HEADS):
            dc = dc + jnp.where(lane == h, dqs[h][:, 64:65] - dk_ref[:, 128 * h + 67:128 * h + 68], 0.0)
        utri = jnp.where(_lane((tq, tq)) >= _row((tq, tq)), 1.0, 0.0).astype(BF16)
        dlf = _dot3_left(utri, dc) + carry_ref[...]
        carry_ref[...] = dlf[0:1, :]
        dfl = jnp.where(lane < C_HEADS, dlf * _sigmoid(-(fl_ref[...] + bf_ref[...])), 0.0)
        ofl_ref[...] = dfl.astype(BF16)
        dbf_ref[...] += jnp.sum(dfl, axis=0, keepdims=True)
        for p in range(C_HEADS // 2):
            a, b = 128 * (2 * p), 128 * (2 * p + 1)
            oq_ref[:, 128 * p:128 * p + 128] = (
                jnp.where(lo, dqs[2 * p], pltpu.roll(dqs[2 * p + 1], 64, axis=1)) * Q_SCALE).astype(BF16)
            ok_ref[:, 128 * p:128 * p + 128] = jnp.where(
                lo, dk_ref[:, a:a + 128], pltpu.roll(dk_ref[:, b:b + 128], 64, axis=1)).astype(BF16)

    rev = lambda w: pl.BlockSpec((tq, w), lambda i: (n - 1 - i, 0))
    return pl.pallas_call(
        body, name=f"fox_bwd_post_{tag}", grid=(n,),
        in_specs=[pl.BlockSpec((C_HEADS, 1, 128, tq), lambda i: (0, n - 1 - i, 0, 0)), rev(C_HEADS * 128),
                  pl.BlockSpec((tq, 128), lambda i: (n - 1 - i, COL_CF // 128)), pl.BlockSpec((1, 128), lambda i: (0, 0))],
        out_specs=[rev(C_WIDTH), rev(C_WIDTH), rev(128), pl.BlockSpec((1, 128), lambda i: (0, 0))],
        out_shape=[SDS((T, C_WIDTH), BF16), SDS((T, C_WIDTH), BF16), SDS((T, 128), BF16), SDS((1, 128), F32)],
        scratch_shapes=[pltpu.VMEM((1, 128), F32)], compiler_params=_params("arbitrary"),
    )(dqt, dkt, proj, bf)


def _adamw_math(w, g, m, v):
    m = ADAM_B1 * m + (1.0 - ADAM_B1) * g
    v = ADAM_B2 * v + (1.0 - ADAM_B2) * (g * g)
    delta = -ADAM_LR * ((m / ADAM_C1) / (jnp.sqrt(v / ADAM_C2) + ADAM_EPS) + ADAM_WD * w)
    return delta, m, v


def _adamw_pair(w, m, v, ga, gb, name):
    n0 = w.shape[0]
    most = max(1, ADAMW_BLOCK_BYTES // (4 * math.prod(w.shape[1:])))
    t0 = max(t for t in range(1, min(n0, most) + 1) if n0 % t == 0)

    def body(w_ref, m_ref, v_ref, ga_ref, gb_ref, g_ref, d_ref, nm_ref, nv_ref):
        g = ga_ref[...] + gb_ref[...]
        g_ref[...] = g
        d_ref[...], nm_ref[...], nv_ref[...] = _adamw_math(w_ref[...], g, m_ref[...], v_ref[...])

    blk = pl.BlockSpec((t0,) + w.shape[1:], lambda i: (i, 0, 0))
    return pl.pallas_call(
        body, name=name, grid=(n0 // t0,), in_specs=[blk] * 5, out_specs=[blk] * 4,
        out_shape=[SDS(w.shape, F32)] * 4, compiler_params=_params("parallel"),
    )(w, m, v, ga, gb)


def _adamw_small(w, m, v, gall):
    R = w.shape[0]

    def body(w_ref, m_ref, v_ref, g_ref, go_ref, d_ref, nm_ref, nv_ref):
        g = g_ref[0]
        for k in range(1, N_DEV):
            g = g + g_ref[k]
        go_ref[...] = g
        d_ref[...], nm_ref[...], nv_ref[...] = _adamw_math(w_ref[...], g, m_ref[...], v_ref[...])

    return pl.pallas_call(body, name="adamw_small", out_shape=[SDS((R, 128), F32)] * 4,
                          compiler_params=pltpu.CompilerParams(vmem_limit_bytes=VMEM_LIMIT))(w, m, v, gall)


def _sum_chips(layers, name, layer_major):
    _, R, C = layers[0].shape
    L = len(layers)
    tc = _tile(C, 256)

    def body(*refs):
        o_ref = refs[-1]
        for l, p_ref in enumerate(refs[:-1]):
            p = [p_ref[k].astype(F32) for k in range(N_CHIPS)]
            s = ((p[0] + p[1]) + p[2]) + p[3]
            if layer_major:
                o_ref[l] = s
            else:
                o_ref[:, l, :] = s

    out = (L, R, C) if layer_major else (R, L, C)
    out_blk = (L, R, tc) if layer_major else (R, L, tc)
    return pl.pallas_call(
        body, name=name, grid=(C // tc,),
        in_specs=[pl.BlockSpec((N_CHIPS, R, tc), lambda i: (0, 0, i))] * L,
        out_specs=pl.BlockSpec(out_blk, lambda i: (0, 0, i)), out_shape=SDS(out, F32),
        compiler_params=_params("parallel"),
    )(*layers)


ANY = pl.BlockSpec(memory_space=pl.ANY)


def _mesh_pos():
    return lax.axis_index("x"), lax.axis_index("y"), lax.axis_index("c")


def _other_chips(x, y):
    return [(1 - x, y), (x, 1 - y), (1 - x, 1 - y)]


class _ChipExchange:
    def __init__(self, mode, sources):
        assert mode in ("gather", "scatter")
        self.mode, self.sources = mode, tuple(sources)
        self.n = len(self.sources)
        self.in_specs = [ANY] * self.n
        self.out_specs = [ANY] * self.n
        self.out_shape = [SDS(((N_CHIPS,) + s.shape) if mode == "gather" else s.shape, s.dtype) for s in self.sources]
        self.scratch = [pltpu.SemaphoreType.DMA((3 * self.n,)), pltpu.SemaphoreType.DMA((3 * self.n,)),
                        pltpu.SemaphoreType.DMA((self.n,))]

    def _copies(self, srcs, dsts, send_sems, recv_sems, local_sems):
        x, y, c = _mesh_pos()
        me = 2 * x + y
        view = (lambda r, chip: r) if self.mode == "gather" else (lambda r, chip: r.at[chip])
        local = [pltpu.make_async_copy(view(s, me), d.at[me], local_sems.at[a]) for a, (s, d) in enumerate(zip(srcs, dsts))]
        sends, recvs = [], []
        for j, (px, py) in enumerate(_other_chips(x, y)):
            peer = 2 * px + py
            for a, (s, d) in enumerate(zip(srcs, dsts)):
                sems = dict(send_sem=send_sems.at[self.n * j + a], recv_sem=recv_sems.at[self.n * j + a],
                            device_id=(px, py, c), device_id_type=MESH_ID)
                sends.append(pltpu.make_async_remote_copy(src_ref=view(s, peer), dst_ref=d.at[me], **sems))
                recvs.append(pltpu.make_async_remote_copy(src_ref=view(s, me), dst_ref=d.at[peer], **sems))
        return local, sends, recvs

    def start(self, srcs, dsts, sems):
        local, sends, _ = self._copies(srcs, dsts, *sems)
        for cp in local + sends:
            cp.start()

    def wait(self, srcs, dsts, sems):
        local, sends, recvs = self._copies(srcs, dsts, *sems)
        for cp in recvs:
            cp.wait_recv()
        for cp in sends:
            cp.wait_send()
        for cp in local:
            cp.wait()


def _gather_halves(w, tag):
    R, C = w.shape
    H = R // 2

    def body(w_ref, g_ref, send_sems, recv_sems, pass_send, pass_recv, local_sem):
        x, y, c = _mesh_pos()
        me = 2 * x + y
        mine, theirs = pl.ds(c * H, H), pl.ds((1 - c) * H, H)
        own = pltpu.make_async_copy(w_ref, g_ref.at[me], local_sem)
        own.start()

        def fetch(j, px, py, src, dst):
            return pltpu.make_async_remote_copy(src_ref=src, dst_ref=dst, send_sem=send_sems.at[j], recv_sem=recv_sems.at[j],
                                                device_id=(px, py, c), device_id_type=MESH_ID)

        def hand(j, rows, peer):
            return pltpu.make_async_remote_copy(src_ref=g_ref.at[peer, rows], dst_ref=g_ref.at[peer, rows],
                                                send_sem=pass_send.at[j], recv_sem=pass_recv.at[j],
                                                device_id=(x, y, 1 - c), device_id_type=MESH_ID)

        chips = _other_chips(x, y)
        sends = [fetch(j, px, py, w_ref.at[mine], g_ref.at[me, mine]) for j, (px, py) in enumerate(chips)]
        for cp in sends:
            cp.start()
        passed = []
        for j, (px, py) in enumerate(chips):
            peer = 2 * px + py
            fetch(j, px, py, w_ref.at[mine], g_ref.at[peer, mine]).wait_recv()
            passed.append(hand(j, mine, peer))
            passed[-1].start()
        for j, (px, py) in enumerate(chips):
            hand(j, theirs, 2 * px + py).wait_recv()
        for cp in sends + passed:
            cp.wait_send()
        own.wait()

    return pl.pallas_call(
        body, name=f"gather_halves_{tag}", in_specs=[ANY], out_specs=ANY, out_shape=SDS((N_CHIPS, R, C), w.dtype),
        scratch_shapes=[pltpu.SemaphoreType.DMA((3,)), pltpu.SemaphoreType.DMA((3,)), pltpu.SemaphoreType.DMA((3,)),
                        pltpu.SemaphoreType.DMA((3,)), pltpu.SemaphoreType.DMA],
        compiler_params=pltpu.CompilerParams(has_side_effects=True),
    )(w)


class _DeviceGather:
    def __init__(self, source):
        self.sources, self.n = (source,), 1
        self.in_specs, self.out_specs = [ANY], [ANY]
        self.out_shape = [SDS((N_DEV,) + source.shape, source.dtype)]
        self.scratch = [pltpu.SemaphoreType.DMA((N_DEV - 1,)), pltpu.SemaphoreType.DMA((N_DEV - 1,)),
                        pltpu.SemaphoreType.DMA((1,))]

    def _copies(self, srcs, dsts, send_sems, recv_sems, local_sems):
        (src,), (dst,) = srcs, dsts
        x, y, c = _mesh_pos()
        me = 4 * x + 2 * y + c
        local = [pltpu.make_async_copy(src, dst.at[me], local_sems.at[0])]
        sends, recvs = [], []
        for k in range(1, N_DEV):
            px, py, pc = (1 - x) if k & 4 else x, (1 - y) if k & 2 else y, (1 - c) if k & 1 else c
            sems = dict(send_sem=send_sems.at[k - 1], recv_sem=recv_sems.at[k - 1], device_id=(px, py, pc),
                        device_id_type=MESH_ID)
            sends.append(pltpu.make_async_remote_copy(src_ref=src, dst_ref=dst.at[me], **sems))
            recvs.append(pltpu.make_async_remote_copy(src_ref=src, dst_ref=dst.at[4 * px + 2 * py + pc], **sems))
        return local, sends, recvs

    start = _ChipExchange.start
    wait = _ChipExchange.wait


def _gather_devices(a, name):
    ex = _DeviceGather(a)

    def body(a_ref, g_ref, *sems):
        ex.start((a_ref,), (g_ref,), sems)
        ex.wait((a_ref,), (g_ref,), sems)

    return pl.pallas_call(
        body, name=name, in_specs=ex.in_specs, out_specs=ex.out_specs[0], out_shape=ex.out_shape[0],
        scratch_shapes=ex.scratch, compiler_params=pltpu.CompilerParams(has_side_effects=True),
    )(a)


def _swap_cores(pin, pout):
    def body(pin_ref, pout_ref, oin_ref, oout_ref, send_sems, recv_sems):
        x, y, c = _mesh_pos()
        cps = [pltpu.make_async_remote_copy(src_ref=src, dst_ref=dst, send_sem=send_sems.at[a], recv_sem=recv_sems.at[a],
                                            device_id=(x, y, 1 - c), device_id_type=MESH_ID)
               for a, (src, dst) in enumerate(((pin_ref, oin_ref), (pout_ref, oout_ref)))]
        for cp in cps:
            cp.start()
        for cp in cps:
            cp.wait()

    return pl.pallas_call(
        body, name="swap_cores", in_specs=[ANY, ANY], out_specs=[ANY, ANY],
        out_shape=[SDS(pin.shape, F32), SDS(pout.shape, F32)],
        scratch_shapes=[pltpu.SemaphoreType.DMA((2,)), pltpu.SemaphoreType.DMA((2,))],
        compiler_params=pltpu.CompilerParams(has_side_effects=True),
    )(pin, pout)


def _pack_small(parts):
    flat = [jnp.pad(p.reshape(-1), (0, (-p.size) % 128)) for p in parts]
    v = jnp.concatenate(flat)
    return jnp.pad(v, (0, (-v.size) % 1024)).reshape(-1, 128)


def _unpack_small(packed):
    flat = packed.reshape(-1)
    out, off = [], 0
    for _, shape in SMALL_PARAMS:
        size = math.prod(shape)
        out.append(flat[off:off + size].reshape(shape))
        off += size + (-size) % 128
    return out


def _layer_consts(l, gmlp_ln_g, gmlp_ln_b, gmlp_w_s, gmlp_b_s, hgrn_onorm_g, fox_b_f):
    causal = jnp.tril(jnp.ones((CHUNK, CHUNK), bool))
    wm = jnp.where(causal[None], gmlp_w_s[l], 0.0)
    return dict(
        lng=gmlp_ln_g[l].reshape(1, A_WIDTH), lnb=gmlp_ln_b[l].reshape(1, A_WIDTH),
        wm=wm.astype(BF16), wmt=jnp.swapaxes(wm, 1, 2).astype(BF16),
        bst=jnp.pad(gmlp_b_s[l].T, ((0, 0), (0, 128 - A_GROUPS))),
        onorm=jnp.tile(hgrn_onorm_g[l], 4).reshape(1, B_WIDTH),
        bf=jnp.pad(fox_b_f[l], (0, 128 - C_HEADS)).reshape(1, 128),
    )


def kernel(x, norm_g, w_in, w_out, gmlp_ln_g, gmlp_ln_b, gmlp_w_s, gmlp_b_s, hgrn_lb, hgrn_onorm_g, fox_b_f, final_norm_g, loss_target, m_norm_g, m_w_in, m_w_out, m_gmlp_ln_g, m_gmlp_ln_b, m_gmlp_w_s, m_gmlp_b_s, m_hgrn_lb, m_hgrn_onorm_g, m_fox_b_f, m_final_norm_g, v_norm_g, v_w_in, v_w_out, v_gmlp_ln_g, v_gmlp_ln_b, v_gmlp_w_s, v_gmlp_b_s, v_hgrn_lb, v_hgrn_onorm_g, v_fox_b_f, v_final_norm_g):
    T = x.shape[1]
    shard_in = w_in.shape[2]
    shard_out = w_out.shape[1]
    xs = x.reshape(T, D_MODEL)
    tgt = loss_target.reshape(T, D_MODEL)

    w_in_b, w_out_b = w_in.astype(BF16), w_out.astype(BF16)

    def full_w_in(gathered):
        wi = jnp.concatenate([gathered[k] for k in range(N_CHIPS)], axis=-1)
        return jnp.pad(wi, ((0, 0), (0, D_IN_PAD - D_IN)))

    lb_all = _lb_fwd(hgrn_lb)
    consts = [_layer_consts(l, gmlp_ln_g, gmlp_ln_b, gmlp_w_s, gmlp_b_s, hgrn_onorm_g, fox_b_f) for l in range(DEPTH)]

    saved = []
    xl = xs
    w_in_l = full_w_in(_gather_halves(w_in_b[0], "w_in_l0"))
    for l in range(DEPTH):
        cs = consts[l]
        tag = f"l{l}"
        h, proj = _inproj(xl, norm_g[l].reshape(1, D_MODEL), w_in_l, tag)
        ya = _gmlp_fwd(proj, cs["lng"], cs["lnb"], cs["wm"], cs["bst"], tag)
        yb, ob, s0 = _hgrn_fwd(proj, lb_all[l].reshape(1, B_WIDTH), cs["onorm"], tag)
        qt, kt, vt = _fox_prep(proj, cs["bf"], tag)
        ride = _ChipExchange("gather", (w_out_b[l],) + ((w_in_b[l + 1],) if l + 1 < DEPTH else ()))
        oc, lse, yc, *gathered = _fox_fwd(qt, kt, vt, proj, tag, ride)
        w_out_l = gathered[0].reshape(N_CHIPS * shard_out, D_MODEL)
        saved.append(dict(x=xl, h=h, proj=proj, ya=ya, yb=yb, yc=yc, ob=ob, s0=s0, qt=qt, kt=kt, oc=oc, lse=lse,
                          w_in=w_in_l, w_out=w_out_l))
        xl = _outproj(xl, ya, yb, yc, w_out_l, tag)
        if l + 1 < DEPTH:
            w_in_l = full_w_in(gathered[1])

    dx, loss_part, d_final = _loss_head(xl, final_norm_g.reshape(1, D_MODEL), tgt)
    loss = lax.psum(loss_part[0, 0], ("x", "y", "c"))

    g_small = {}
    dlb_rows, rin, rout = [None] * DEPTH, [None] * DEPTH, [None] * DEPTH
    slabs_in = None
    stack = lambda key: jnp.stack([g_small[l][key] for l in range(DEPTH)])
    for l in reversed(range(DEPTH)):
        cs, sv = consts[l], saved[l]
        tag = f"l{l}"
        proj = sv["proj"]
        dy, dw_out = _outproj_bwd(dx, sv["ya"], sv["yb"], sv["yc"], sv["w_out"], tag)
        da, dwm, dbst, dlng, dlnb = _gmlp_bwd(proj, dy, cs["lng"], cs["lnb"], cs["wm"], cs["wmt"], cs["bst"], tag)
        db, dlb_rows[l], donorm = _hgrn_bwd(proj, dy, sv["ob"], sv["s0"], lb_all[l].reshape(1, B_WIDTH), cs["onorm"], tag)
        do, delta, dzc = _fox_bwd_prep(proj, dy, sv["oc"], tag)
        slabs_out = dw_out.reshape(N_CHIPS, shard_out, D_MODEL).astype(BF16)
        ride = _ChipExchange("scatter", (slabs_out,) + ((slabs_in,) if slabs_in is not None else ()))
        dqt, dkt, dvc, *received = _fox_bwd(sv["qt"], sv["kt"], proj, do, sv["lse"], delta, tag, ride)
        rout[l] = received[0]
        if slabs_in is not None:
            rin[l + 1] = received[1]
        dqc, dkc, dflc, dbf = _fox_bwd_post(dqt, dkt, proj, cs["bf"], tag)
        g_small[l] = dict(ln_g=dlng.reshape(4, 64), ln_b=dlnb.reshape(4, 64), w_s=dwm, b_s=dbst[:, :A_GROUPS].T,
                          onorm=donorm[0, :64], bf=dbf[0, :C_HEADS])
        dproj = jnp.concatenate([da, db, dqc, dkc, dvc, dzc, dflc, jnp.zeros((T, 128), BF16)], axis=1)
        if l == 0:
            d_hgrn_lb = _lb_bwd(hgrn_lb, jnp.concatenate(dlb_rows, axis=0))
            early = _pack_small([stack("ln_g"), stack("ln_b"), stack("w_s"), stack("b_s"), d_hgrn_lb, stack("onorm"),
                                 stack("bf"), d_final.reshape(D_MODEL)])
            dw_in, rearly = _dw_in(sv["h"], dproj, tag, _DeviceGather(early))
        else:
            dw_in = _dw_in(sv["h"], dproj, tag)
        slabs_in = jnp.stack([dw_in[k * shard_in:(k + 1) * shard_in] for k in range(N_CHIPS)]).astype(BF16)
        ride = _ChipExchange("scatter", (slabs_in,)) if l == 0 else None
        dx, dng, *received = _dx_in(sv["x"], norm_g[l].reshape(1, D_MODEL), dx, dproj, sv["w_in"], tag, ride)
        if l == 0:
            rin[0] = received[0]
        g_small[l]["norm_g"] = dng.reshape(D_MODEL)
    grad_x = dx.reshape(x.shape)
    rlate = _gather_devices(_pack_small([stack("norm_g")]), "gather_norm_grads")
    rsmall = jnp.concatenate([rlate, rearly], axis=1)

    pin, pout = _sum_chips(rin, "sum_chips_w_in", False), _sum_chips(rout, "sum_chips_w_out", True)
    oin, oout = _swap_cores(pin, pout)
    to_view = lambda a: jnp.transpose(a, (2, 0, 1))
    g_w_in, d_w_in, nm_w_in, nv_w_in = [
        jnp.transpose(o, (1, 2, 0))
        for o in _adamw_pair(to_view(w_in), to_view(m_w_in), to_view(v_w_in), pin, oin, "adamw_w_in")]
    g_w_out, d_w_out, nm_w_out, nv_w_out = _adamw_pair(w_out, m_w_out, v_w_out, pout, oout, "adamw_w_out")

    small_w = [norm_g, gmlp_ln_g, gmlp_ln_b, gmlp_w_s, gmlp_b_s, hgrn_lb, hgrn_onorm_g, fox_b_f, final_norm_g]
    small_m = [m_norm_g, m_gmlp_ln_g, m_gmlp_ln_b, m_gmlp_w_s, m_gmlp_b_s, m_hgrn_lb, m_hgrn_onorm_g, m_fox_b_f, m_final_norm_g]
    small_v = [v_norm_g, v_gmlp_ln_g, v_gmlp_ln_b, v_gmlp_w_s, v_gmlp_b_s, v_hgrn_lb, v_hgrn_onorm_g, v_fox_b_f, v_final_norm_g]
    outs = _adamw_small(_pack_small(small_w), _pack_small(small_m), _pack_small(small_v), rsmall)
    sg, sd, sm, sv_ = [_unpack_small(o) for o in outs]

    def order(big_in, big_out, small):
        return [small[0], big_in, big_out] + small[1:]

    return (loss, grad_x, *order(g_w_in, g_w_out, sg), *order(d_w_in, d_w_out, sd), *order(nm_w_in, nm_w_out, sm),
            *order(nv_w_in, nv_w_out, sv_))
```

```python
import functools
import math

import jax
import jax.numpy as jnp
from jax import lax
from jax.experimental import pallas as pl
from jax.experimental.pallas import tpu as pltpu

F32 = jnp.float32
BF16 = jnp.bfloat16
SDS = jax.ShapeDtypeStruct
MESH_ID = pl.DeviceIdType.MESH

D_MODEL = 1024
DEPTH = 2
A_WIDTH = 256
A_GROUPS = 4
B_WIDTH = 256
C_WIDTH = 512
C_HEADS = 8
D_IN = 3848
D_IN_PAD = 4096
CHUNK = 128
SUB = 16
SUB_SHIFT = 4
NORM_EPS = 1e-6
F_FLOOR = 1e-30
COL_AU, COL_AV, COL_AZ = 0, 256, 512
COL_BQ, COL_BF, COL_BI, COL_BZ = 768, 1024, 1280, 1536
COL_CQ, COL_CK, COL_CV, COL_CZ, COL_CF = 1792, 2304, 2816, 3328, 3840
HEAD_LANES = 128
Q_SCALE = 0.125
ADAM_LR, ADAM_B1, ADAM_B2, ADAM_EPS, ADAM_WD, ADAM_STEP = 0.001, 0.9, 0.999, 1e-08, 0.01, 10
ADAM_C1 = 1.0 - ADAM_B1 ** ADAM_STEP
ADAM_C2 = 1.0 - ADAM_B2 ** ADAM_STEP
VMEM_LIMIT = 56 * 1024 * 1024
ADAMW_BLOCK_BYTES = 1 << 20
N_CHIPS = 4
N_DEV = 8

SMALL_PARAMS = (
    ("norm_g", (DEPTH, D_MODEL)), ("gmlp_ln_g", (DEPTH, 4, 64)), ("gmlp_ln_b", (DEPTH, 4, 64)),
    ("gmlp_w_s", (DEPTH, 4, 128, 128)), ("gmlp_b_s", (DEPTH, 4, 128)), ("hgrn_lb", (DEPTH, 256)),
    ("hgrn_onorm_g", (DEPTH, 64)), ("fox_b_f", (DEPTH, 8)), ("final_norm_g", (D_MODEL,)),
)


def _tile(n, pref):
    t = min(n, pref)
    assert n % t == 0, (n, pref)
    return t


def _params(*sem):
    return pltpu.CompilerParams(dimension_semantics=sem, vmem_limit_bytes=VMEM_LIMIT)


def _dot(a, b):
    return jnp.dot(a, b, preferred_element_type=F32)


def _dot_nt(a, b):
    return lax.dot_general(a, b, (((1,), (1,)), ((), ())), preferred_element_type=F32)


def _dot_tn(a, b):
    return lax.dot_general(a, b, (((0,), (0,)), ((), ())), preferred_element_type=F32)


def _split3(x):
    hi = x.astype(BF16)
    r = x - hi.astype(F32)
    mid = r.astype(BF16)
    lo = (r - mid.astype(F32)).astype(BF16)
    return hi, mid, lo


def _dot3_left(c, x):
    hi, mid, lo = _split3(x)
    return _dot(c, hi) + _dot(c, mid) + _dot(c, lo)


def _sigmoid(x):
    return jax.nn.sigmoid(x)


def _silu_and_grad(x):
    s = _sigmoid(x)
    return x * s, s * (1.0 + x * (1.0 - s))


_GELU_C = math.sqrt(2.0 / math.pi)


def _gelu_and_grad(x):
    inner = _GELU_C * (x + 0.044715 * x * x * x)
    t = jnp.tanh(inner)
    y = 0.5 * x * (1.0 + t)
    dy = 0.5 * (1.0 + t) + 0.5 * x * (1.0 - t * t) * _GELU_C * (1.0 + 3.0 * 0.044715 * x * x)
    return y, dy


def _lane(shape):
    return lax.broadcasted_iota(jnp.int32, shape, 1)


def _row(shape):
    return lax.broadcasted_iota(jnp.int32, shape, 0)


def _gsum64(x):
    lo = _lane(x.shape) < 64
    s0 = jnp.sum(jnp.where(lo, x, 0.0), axis=-1, keepdims=True)
    s1 = jnp.sum(jnp.where(lo, 0.0, x), axis=-1, keepdims=True)
    return jnp.where(lo, s0, s1)


def _colreduce(x, op):
    parts = [x[r:r + 8, :] for r in range(0, x.shape[0], 8)]
    while len(parts) > 1:
        pairs = [op(parts[k], parts[k + 1]) for k in range(0, len(parts) - 1, 2)]
        parts = pairs + ([parts[-1]] if len(parts) % 2 else [])
    red = jnp.max if op is jnp.maximum else jnp.sum
    return red(parts[0], axis=0, keepdims=True)


def _block_diag64(dtype=BF16):
    r, c = _row((128, 128)), _lane((128, 128))
    return jnp.where((r >> 6) == (c >> 6), 1.0, 0.0).astype(dtype)


def _inproj(x, g, w, tag):
    T, D = x.shape
    DP = w.shape[1]
    tm = _tile(T, 512)

    def body(x_ref, g_ref, w_ref, h_ref, p_ref):
        xv = x_ref[...]
        r = lax.rsqrt(jnp.mean(xv * xv, axis=-1, keepdims=True) + NORM_EPS)
        h = (xv * r * g_ref[...]).astype(BF16)
        h_ref[...] = h
        p_ref[...] = _dot(h, w_ref[...])

    return pl.pallas_call(
        body, name=f"inproj_{tag}", grid=(T // tm,),
        in_specs=[pl.BlockSpec((tm, D), lambda i: (i, 0)), pl.BlockSpec((1, D), lambda i: (0, 0)),
                  pl.BlockSpec((D, DP), lambda i: (0, 0))],
        out_specs=[pl.BlockSpec((tm, D), lambda i: (i, 0)), pl.BlockSpec((tm, DP), lambda i: (i, 0))],
        out_shape=[SDS((T, D), BF16), SDS((T, DP), F32)],
        compiler_params=_params("parallel"),
    )(x, g, w)


def _outproj(x, ya, yb, yc, wo, tag):
    T, D = x.shape
    tm = _tile(T, 512)

    def body(x_ref, ya_ref, yb_ref, yc_ref, wo_ref, o_ref):
        acc = x_ref[...] + _dot(ya_ref[...], wo_ref[0:A_WIDTH, :])
        acc = acc + _dot(yb_ref[...], wo_ref[A_WIDTH:A_WIDTH + B_WIDTH, :])
        o_ref[...] = acc + _dot(yc_ref[...], wo_ref[A_WIDTH + B_WIDTH:, :])

    row = lambda w: pl.BlockSpec((tm, w), lambda i: (i, 0))
    return pl.pallas_call(
        body, name=f"outproj_{tag}", grid=(T // tm,),
        in_specs=[row(D), row(A_WIDTH), row(B_WIDTH), row(C_WIDTH), pl.BlockSpec(wo.shape, lambda i: (0, 0))],
        out_specs=row(D), out_shape=SDS((T, D), F32), compiler_params=_params("parallel"),
    )(x, ya, yb, yc, wo)


def _outproj_bwd(dx, ya, yb, yc, wo, tag):
    T, D = dx.shape
    DM = wo.shape[0]
    tm = _tile(T, 512)

    def body(dx_ref, ya_ref, yb_ref, yc_ref, wo_ref, dy_ref, dwo_ref):
        @pl.when(pl.program_id(0) == 0)
        def _():
            dwo_ref[...] = jnp.zeros_like(dwo_ref)

        dxb = dx_ref[...].astype(BF16)
        dy_ref[...] = _dot_nt(dxb, wo_ref[...])
        dwo_ref[0:A_WIDTH, :] += _dot_tn(ya_ref[...], dxb)
        dwo_ref[A_WIDTH:A_WIDTH + B_WIDTH, :] += _dot_tn(yb_ref[...], dxb)
        dwo_ref[A_WIDTH + B_WIDTH:, :] += _dot_tn(yc_ref[...], dxb)

    row = lambda w: pl.BlockSpec((tm, w), lambda i: (i, 0))
    return pl.pallas_call(
        body, name=f"outproj_bwd_{tag}", grid=(T // tm,),
        in_specs=[row(D), row(A_WIDTH), row(B_WIDTH), row(C_WIDTH), pl.BlockSpec(wo.shape, lambda i: (0, 0))],
        out_specs=[row(DM), pl.BlockSpec((DM, D), lambda i: (0, 0))],
        out_shape=[SDS((T, DM), F32), SDS((DM, D), F32)], compiler_params=_params("arbitrary"),
    )(dx, ya, yb, yc, wo)


def _dw_in(h, dproj, tag, ride=None):
    T, D = h.shape
    DP = dproj.shape[1]
    tm, tn = _tile(T, 512), _tile(DP, 1024)
    grid = (DP // tn, T // tm)

    def body(h_ref, dp_ref, *rest):
        ride_srcs, (dw_ref,), ride_dsts, _, ride_sems = _ride_refs(ride, rest, 1, 0)
        _ride_start(ride, grid, ride_srcs, ride_dsts, ride_sems)

        @pl.when(pl.program_id(1) == 0)
        def _():
            dw_ref[...] = jnp.zeros_like(dw_ref)

        dw_ref[...] += _dot_tn(dp_ref[...], h_ref[...])
        _ride_wait(ride, grid, ride_srcs, ride_dsts, ride_sems)

    extra = ride or _ChipExchange("gather", ())
    out = pl.pallas_call(
        body, name=f"dw_in_{tag}", grid=grid,
        in_specs=[pl.BlockSpec((tm, D), lambda j, i: (i, 0)), pl.BlockSpec((tm, tn), lambda j, i: (i, j))] + extra.in_specs,
        out_specs=[pl.BlockSpec((tn, D), lambda j, i: (j, 0))] + extra.out_specs,
        out_shape=[SDS((DP, D), F32)] + extra.out_shape, scratch_shapes=extra.scratch if ride else [],
        compiler_params=pltpu.CompilerParams(dimension_semantics=("arbitrary", "arbitrary"), vmem_limit_bytes=VMEM_LIMIT,
                                             has_side_effects=bool(ride)),
    )(h, dproj, *extra.sources)
    return out if ride else out[0]


def _dx_in(x, g, dres, dproj, w, tag, ride=None):
    T, D = x.shape
    DP = w.shape[1]
    tm = _tile(T, 512)
    grid = (T // tm,)

    def body(x_ref, g_ref, dres_ref, dp_ref, w_ref, *rest):
        ride_srcs, (dx_ref, dg_ref), ride_dsts, _, ride_sems = _ride_refs(ride, rest, 2, 0)
        _ride_start(ride, grid, ride_srcs, ride_dsts, ride_sems)

        @pl.when(pl.program_id(0) == 0)
        def _():
            dg_ref[...] = jnp.zeros_like(dg_ref)

        dh = _dot_nt(dp_ref[...], w_ref[...])
        xv = x_ref[...]
        r = lax.rsqrt(jnp.mean(xv * xv, axis=-1, keepdims=True) + NORM_EPS)
        xh = xv * r
        dg_ref[...] += jnp.sum(dh * xh, axis=0, keepdims=True)
        dxh = dh * g_ref[...]
        dx_ref[...] = dres_ref[...] + r * (dxh - xh * jnp.mean(dxh * xh, axis=-1, keepdims=True))
        _ride_wait(ride, grid, ride_srcs, ride_dsts, ride_sems)

    extra = ride or _ChipExchange("gather", ())
    row = pl.BlockSpec((tm, D), lambda i: (i, 0))
    return pl.pallas_call(
        body, name=f"dx_in_{tag}", grid=grid,
        in_specs=[row, pl.BlockSpec((1, D), lambda i: (0, 0)), row, pl.BlockSpec((tm, DP), lambda i: (i, 0)),
                  pl.BlockSpec((D, DP), lambda i: (0, 0))] + extra.in_specs,
        out_specs=[row, pl.BlockSpec((1, D), lambda i: (0, 0))] + extra.out_specs,
        out_shape=[SDS((T, D), F32), SDS((1, D), F32)] + extra.out_shape,
        scratch_shapes=extra.scratch if ride else [],
        compiler_params=pltpu.CompilerParams(dimension_semantics=("arbitrary",), vmem_limit_bytes=VMEM_LIMIT,
                                             has_side_effects=bool(ride)),
    )(x, g, dres, dproj, w, *extra.sources)


def _loss_head(x, g, tgt):
    T, D = x.shape
    tm = _tile(T, 512)

    def body(x_ref, g_ref, t_ref, dx_ref, loss_ref, dg_ref):
        @pl.when(pl.program_id(0) == 0)
        def _():
            loss_ref[...] = jnp.zeros_like(loss_ref)
            dg_ref[...] = jnp.zeros_like(dg_ref)

        xv = x_ref[...]
        r = lax.rsqrt(jnp.mean(xv * xv, axis=-1, keepdims=True) + NORM_EPS)
        xh = xv * r
        gv = g_ref[...]
        err = xh * gv - t_ref[...]
        tok = jnp.mean(err * err, axis=-1, keepdims=True)
        loss_ref[...] += 0.5 * jnp.sum(tok, axis=0, keepdims=True)
        dy = err * (1.0 / D)
        dg_ref[...] += jnp.sum(dy * xh, axis=0, keepdims=True)
        dxh = dy * gv
        dx_ref[...] = r * (dxh - xh * jnp.mean(dxh * xh, axis=-1, keepdims=True))

    row = pl.BlockSpec((tm, D), lambda i: (i, 0))
    return pl.pallas_call(
        body, name="loss_head", grid=(T // tm,),
        in_specs=[row, pl.BlockSpec((1, D), lambda i: (0, 0)), row],
        out_specs=[row, pl.BlockSpec((1, 128), lambda i: (0, 0)), pl.BlockSpec((1, D), lambda i: (0, 0))],
        out_shape=[SDS((T, D), F32), SDS((1, 128), F32), SDS((1, D), F32)], compiler_params=_params("arbitrary"),
    )(x, g, tgt)


def _gmlp_core(u, v, lng, lnb, wm_ref, bst_ref, pair):
    ug, dug = _gelu_and_grad(u)
    vg, dvg = _gelu_and_grad(v)
    mu = _gsum64(vg) * (1.0 / 64)
    d = vg - mu
    var = _gsum64(d * d) * (1.0 / 64)
    rstd = lax.rsqrt(var + NORM_EPS)
    xh = d * rstd
    vn = xh * lng + lnb
    vnb = vn.astype(BF16)
    lo = _lane(u.shape) < 64
    g0, g1 = 2 * pair, 2 * pair + 1
    mixed = jnp.where(lo, _dot(wm_ref[g0], vnb) + bst_ref[:, g0:g0 + 1], _dot(wm_ref[g1], vnb) + bst_ref[:, g1:g1 + 1])
    return ug, dug, dvg, rstd, xh, vnb, mixed, lo


def _gmlp_fwd(proj, lng, lnb, wm, bst, tag):
    T = proj.shape[0]

    def body(u_ref, v_ref, z_ref, lng_ref, lnb_ref, wm_ref, bst_ref, y_ref):
        for pair in range(2):
            sl = slice(128 * pair, 128 * pair + 128)
            ug, _, _, _, _, _, mixed, _ = _gmlp_core(u_ref[:, sl], v_ref[:, sl], lng_ref[:, sl], lnb_ref[:, sl],
                                                     wm_ref, bst_ref, pair)
            sz, _ = _silu_and_grad(z_ref[:, sl])
            y_ref[:, sl] = (ug * mixed * sz).astype(BF16)

    col = lambda c: pl.BlockSpec((CHUNK, A_WIDTH), lambda i, c=c: (i, c // A_WIDTH))
    full = lambda a: pl.BlockSpec(a.shape, lambda i, n=a.ndim: (0,) * n)
    return pl.pallas_call(
        body, name=f"gmlp_fwd_{tag}", grid=(T // CHUNK,),
        in_specs=[col(COL_AU), col(COL_AV), col(COL_AZ), full(lng), full(lnb), full(wm), full(bst)],
        out_specs=pl.BlockSpec((CHUNK, A_WIDTH), lambda i: (i, 0)), out_shape=SDS((T, A_WIDTH), BF16),
        compiler_params=_params("parallel"),
    )(proj, proj, proj, lng, lnb, wm, bst)


def _gmlp_bwd(proj, dy, lng, lnb, wm, wmt, bst, tag):
    T = proj.shape[0]
    n = T // CHUNK

    def body(u_ref, v_ref, z_ref, dy_ref, lng_ref, lnb_ref, wm_ref, wmt_ref, bst_ref,
             da_ref, dwm_ref, dbst_ref, dlng_ref, dlnb_ref):
        @pl.when(pl.program_id(0) == 0)
        def _():
            dwm_ref[...] = jnp.zeros_like(dwm_ref)
            dbst_ref[...] = jnp.zeros_like(dbst_ref)
            dlng_ref[...] = jnp.zeros_like(dlng_ref)
            dlnb_ref[...] = jnp.zeros_like(dlnb_ref)

        lane = _lane((CHUNK, 128))
        dbst = dbst_ref[...]
        for pair in range(2):
            sl = slice(128 * pair, 128 * pair + 128)
            lng_p = lng_ref[:, sl]
            ug, dug, dvg, rstd, xh, vnb, mixed, lo = _gmlp_core(u_ref[:, sl], v_ref[:, sl], lng_p, lnb_ref[:, sl],
                                                                wm_ref, bst_ref, pair)
            sz, dsz = _silu_and_grad(z_ref[:, sl])
            dyv = dy_ref[:, sl]
            out = ug * mixed
            dz = dyv * out * dsz
            dout = dyv * sz
            du = dout * mixed * dug
            dmix = dout * ug
            g0, g1 = 2 * pair, 2 * pair + 1
            dm0 = jnp.where(lo, dmix, 0.0)
            dm1 = jnp.where(lo, 0.0, dmix)
            dbst = dbst + jnp.where(lane == g0, jnp.sum(dm0, axis=-1, keepdims=True), 0.0)
            dbst = dbst + jnp.where(lane == g1, jnp.sum(dm1, axis=-1, keepdims=True), 0.0)
            dwm_ref[g0] += _dot_nt(dm0.astype(BF16), vnb)
            dwm_ref[g1] += _dot_nt(dm1.astype(BF16), vnb)
            dmb = dmix.astype(BF16)
            dvn = jnp.where(lo, _dot(wmt_ref[g0], dmb), _dot(wmt_ref[g1], dmb))
            dlng_ref[:, sl] += jnp.sum(dvn * xh, axis=0, keepdims=True)
            dlnb_ref[:, sl] += jnp.sum(dvn, axis=0, keepdims=True)
            dxh = dvn * lng_p
            m1 = _gsum64(dxh) * (1.0 / 64)
            m2 = _gsum64(dxh * xh) * (1.0 / 64)
            dv = rstd * (dxh - m1 - xh * m2) * dvg
            da_ref[:, COL_AU + 128 * pair:COL_AU + 128 * pair + 128] = du.astype(BF16)
            da_ref[:, COL_AV + 128 * pair:COL_AV + 128 * pair + 128] = dv.astype(BF16)
            da_ref[:, COL_AZ + 128 * pair:COL_AZ + 128 * pair + 128] = dz.astype(BF16)
        dbst_ref[...] = dbst

        @pl.when(pl.program_id(0) == n - 1)
        def _():
            causal = _lane((CHUNK, CHUNK)) <= _row((CHUNK, CHUNK))
            for g in range(A_GROUPS):
                dwm_ref[g] = jnp.where(causal, dwm_ref[g], 0.0)

    col = lambda c: pl.BlockSpec((CHUNK, A_WIDTH), lambda i, c=c: (i, c // A_WIDTH))
    full = lambda a: pl.BlockSpec(a.shape, lambda i, n=a.ndim: (0,) * n)
    acc = lambda s: pl.BlockSpec(s, lambda i, n=len(s): (0,) * n)
    return pl.pallas_call(
        body, name=f"gmlp_bwd_{tag}", grid=(n,),
        in_specs=[col(COL_AU), col(COL_AV), col(COL_AZ), pl.BlockSpec((CHUNK, A_WIDTH), lambda i: (i, 0)),
                  full(lng), full(lnb), full(wm), full(wmt), full(bst)],
        out_specs=[pl.BlockSpec((CHUNK, 3 * A_WIDTH), lambda i: (i, 0)), acc((A_GROUPS, CHUNK, CHUNK)),
                   acc((CHUNK, 128)), acc((1, A_WIDTH)), acc((1, A_WIDTH))],
        out_shape=[SDS((T, 3 * A_WIDTH), BF16), SDS((A_GROUPS, CHUNK, CHUNK), F32), SDS((CHUNK, 128), F32),
                   SDS((1, A_WIDTH), F32), SDS((1, A_WIDTH), F32)],
        compiler_params=_params("arbitrary"),
    )(proj, proj, proj, dy, lng, lnb, wm, wmt, bst)


def _hgrn_consts():
    r, c = _row((CHUNK, CHUNK)), _lane((CHUNK, CHUNK))
    same = (r >> SUB_SHIFT) == (c >> SUB_SHIFT)
    lsub = jnp.where(same & (c <= r), 1.0, 0.0).astype(BF16)
    usub = jnp.where(same & (c >= r), 1.0, 0.0).astype(BF16)
    bsub = jnp.where(same, 1.0, 0.0).astype(BF16)
    return lsub, usub, bsub


def _hgrn_gates(qv, zf, lbp):
    sq, dsq = _silu_and_grad(qv)
    qt = sq * Q_SCALE
    sg = _sigmoid(zf)
    sgn = _sigmoid(-zf)
    f = lbp + (1.0 - lbp) * sg
    g = jnp.log(jnp.maximum(f, F_FLOOR))
    kf = (1.0 - lbp) * sgn
    return qt, dsq, sg, sgn, f, g, kf


def _hgrn_intra_fwd(qt, kf, b, v, mbd):
    rid = _row((SUB, 128))
    parts = []
    for s in range(SUB):
        e = jnp.exp(b - b[s:s + 1, :])
        parts.append(jnp.where(rid >= s, qt * kf[s:s + 1, :] * e, 0.0))
    a = _dot(jnp.concatenate(parts, axis=0).astype(BF16), mbd)
    o = jnp.zeros((SUB, 128), F32)
    for s in range(SUB):
        o = o + a[SUB * s:SUB * s + SUB, :] * v[s:s + 1, :]
    return o


def _hgrn_intra_bwd(qt, kf, b, v, do, mbd, rsum):
    rid = _row((SUB, 128))
    ps, das, kes, es = [], [], [], []
    for s in range(SUB):
        e = jnp.where(rid >= s, jnp.exp(b - b[s:s + 1, :]), 0.0)
        ke = kf[s:s + 1, :] * e
        es.append(e)
        kes.append(ke)
        ps.append(qt * ke)
        das.append(do * v[s:s + 1, :])
    a = _dot(jnp.concatenate(ps, axis=0).astype(BF16), mbd)
    da = _dot(jnp.concatenate(das, axis=0).astype(BF16), mbd)
    dqt = jnp.zeros((SUB, 128), F32)
    xs, ys = [], []
    for s in range(SUB):
        da_s = da[SUB * s:SUB * s + SUB, :]
        dqt = dqt + da_s * kes[s]
        xs.append(a[SUB * s:SUB * s + SUB, :] * do)
        ys.append(da_s * qt * es[s])
    dv = _dot(rsum, jnp.concatenate(xs, axis=0).astype(BF16))
    dkf = _dot(rsum, jnp.concatenate(ys, axis=0).astype(BF16))
    return dqt, dkf, dv


def _hgrn_norm_gate(o, z, onorm):
    ms = _gsum64(o * o) * (1.0 / 64)
    r = lax.rsqrt(ms + NORM_EPS)
    xh = o * r
    sz, dsz = _silu_and_grad(z)
    return xh, r, sz, dsz, xh * onorm


def _hgrn_fwd(proj, lb, onorm, tag):
    T = proj.shape[0]
    n = T // CHUNK
    nsub = CHUNK // SUB

    def body(q_ref, f_ref, i_ref, z_ref, lb_ref, on_ref, y_ref, o_ref, s0_ref, st_ref):
        @pl.when(pl.program_id(0) == 0)
        def _():
            st_ref[...] = jnp.zeros_like(st_ref)

        lsub, _, bsub = _hgrn_consts()
        mbd = _block_diag64()
        bdmask = mbd > 0
        rid = _row((CHUNK, 128))
        for pair in range(2):
            sl = slice(128 * pair, 128 * pair + 128)
            qt, _, _, _, _, g, kf = _hgrn_gates(q_ref[:, sl], f_ref[:, sl], lb_ref[:, sl])
            v = i_ref[:, sl]
            b = _dot3_left(lsub, g)
            bl = _dot3_left(bsub, g)
            qh = (qt * jnp.exp(b)).astype(BF16)
            kh = kf * jnp.exp(bl - b)
            dec = jnp.exp(bl)
            vtb = v.T.astype(BF16)
            st = st_ref[pair]
            s0_ref[0, pair] = st
            outs = []
            for sub in range(nsub):
                rs = slice(SUB * sub, SUB * sub + SUB)
                o_inter = _dot_nt(qh[rs], st.astype(BF16))
                outs.append(o_inter + _hgrn_intra_fwd(qt[rs], kf[rs], b[rs], v[rs], mbd))
                khm = jnp.where((rid >> SUB_SHIFT) == sub, kh, 0.0).astype(BF16)
                st = jnp.where(bdmask, st * dec[SUB * sub:SUB * sub + 1, :] + _dot(vtb, khm), 0.0)
            st_ref[pair] = st
            o = jnp.concatenate(outs, axis=0)
            o_ref[:, sl] = o
            _, _, sz, _, on = _hgrn_norm_gate(o, z_ref[:, sl], on_ref[:, sl])
            y_ref[:, sl] = (on * sz).astype(BF16)

    col = lambda c: pl.BlockSpec((CHUNK, B_WIDTH), lambda i, c=c: (i, c // B_WIDTH))
    full = lambda a: pl.BlockSpec(a.shape, lambda i, n=a.ndim: (0,) * n)
    return pl.pallas_call(
        body, name=f"hgrn_fwd_{tag}", grid=(n,),
        in_specs=[col(COL_BQ), col(COL_BF), col(COL_BI), col(COL_BZ), full(lb), full(onorm)],
        out_specs=[pl.BlockSpec((CHUNK, B_WIDTH), lambda i: (i, 0)), pl.BlockSpec((CHUNK, B_WIDTH), lambda i: (i, 0)),
                   pl.BlockSpec((1, 2, 128, 128), lambda i: (i, 0, 0, 0))],
        out_shape=[SDS((T, B_WIDTH), BF16), SDS((T, B_WIDTH), F32), SDS((n, 2, 128, 128), F32)],
        scratch_shapes=[pltpu.VMEM((2, 128, 128), F32)], compiler_params=_params("arbitrary"),
    )(proj, proj, proj, proj, lb, onorm)


def _hgrn_bwd(proj, dy, o_saved, s0, lb, onorm, tag):
    T = proj.shape[0]
    n = T // CHUNK
    nsub = CHUNK // SUB

    def body(q_ref, f_ref, i_ref, z_ref, dy_ref, o_ref, s0_ref, lb_ref, on_ref,
             db_ref, dlb_ref, don_ref, dst_ref, sts_ref):
        @pl.when(pl.program_id(0) == 0)
        def _():
            dst_ref[...] = jnp.zeros_like(dst_ref)
            dlb_ref[...] = jnp.zeros_like(dlb_ref)
            don_ref[...] = jnp.zeros_like(don_ref)

        lsub, usub, bsub = _hgrn_consts()
        mbd = _block_diag64()
        bdmask = mbd > 0
        rsum = jnp.where((_lane((SUB, SUB * SUB)) >> SUB_SHIFT) == _row((SUB, SUB * SUB)), 1.0, 0.0).astype(BF16)
        for pair in range(2):
            sl = slice(128 * pair, 128 * pair + 128)
            lbp = lb_ref[:, sl]
            qv, zf = q_ref[:, sl], f_ref[:, sl]
            qt, dsq, sg, sgn, f, g, kf = _hgrn_gates(qv, zf, lbp)
            v = i_ref[:, sl]
            b = _dot3_left(lsub, g)
            bl = _dot3_left(bsub, g)
            eb = jnp.exp(b)
            ekb = jnp.exp(bl - b)
            qhb = (qt * eb).astype(BF16)
            khb = (kf * ekb).astype(BF16)
            dec = jnp.exp(bl)
            vb = v.astype(BF16)
            onp = on_ref[:, sl]
            ov = o_ref[:, sl]
            xh, r, sz, dsz, on = _hgrn_norm_gate(ov, z_ref[:, sl], onp)
            dyv = dy_ref[:, sl]
            dz = dyv * on * dsz
            don = dyv * sz
            cn = jnp.sum(don * xh, axis=0, keepdims=True)
            don_ref[...] += cn + pltpu.roll(cn, 64, axis=1)
            dxo = don * onp
            do = r * (dxo - xh * (_gsum64(dxo * xh) * (1.0 / 64)))
            dob = do.astype(BF16)
            st = s0_ref[0, pair]
            for sub in range(nsub):
                rs = slice(SUB * sub, SUB * sub + SUB)
                sts_ref[sub] = st
                st = jnp.where(bdmask, st * dec[SUB * sub:SUB * sub + 1, :] + _dot_tn(vb[rs], khb[rs]), 0.0)
            gst = dst_ref[pair]
            dqt_p, dkf_p, dv_p, dbl_p = [None] * nsub, [None] * nsub, [None] * nsub, [None] * nsub
            for sub in reversed(range(nsub)):
                rs = slice(SUB * sub, SUB * sub + SUB)
                st_in = sts_ref[sub]
                gb = gst.astype(BF16)
                dqh = _dot(dob[rs], st_in.astype(BF16))
                dkh = _dot(vb[rs], gb)
                dv_inter = _dot_nt(khb[rs], gb)
                ddec = jnp.sum(gst * st_in, axis=0, keepdims=True)
                dec_row = dec[SUB * sub:SUB * sub + 1, :]
                gst = jnp.where(bdmask, gst * dec_row + _dot_tn(dob[rs], qhb[rs]), 0.0)
                dqt_i, dkf_i, dv_i = _hgrn_intra_bwd(qt[rs], kf[rs], b[rs], v[rs], do[rs], mbd, rsum)
                dkf_inter = dkh * ekb[rs]
                dqt_p[sub] = dqh * eb[rs] + dqt_i
                dkf_p[sub] = dkf_inter + dkf_i
                dv_p[sub] = dv_inter + dv_i
                row = jnp.sum(kf[rs] * dkf_inter, axis=0, keepdims=True) + ddec * dec_row
                dbl_p[sub] = jnp.broadcast_to(row, (SUB, 128))
            dst_ref[pair] = gst
            dqt = jnp.concatenate(dqt_p, axis=0)
            dkf = jnp.concatenate(dkf_p, axis=0)
            dv = jnp.concatenate(dv_p, axis=0)
            dg = _dot3_left(usub, qt * dqt - kf * dkf) + jnp.concatenate(dbl_p, axis=0)
            df = jnp.where(f > F_FLOOR, dg / f, 0.0)
            dlb_ref[:, sl] += jnp.sum(df * (1.0 - sg) - dkf * sgn, axis=0, keepdims=True)
            dfl = (1.0 - lbp) * sg * sgn * (df - dkf)
            dq = dqt * Q_SCALE * dsq
            db_ref[:, 0 * B_WIDTH + 128 * pair:0 * B_WIDTH + 128 * pair + 128] = dq.astype(BF16)
            db_ref[:, 1 * B_WIDTH + 128 * pair:1 * B_WIDTH + 128 * pair + 128] = dfl.astype(BF16)
            db_ref[:, 2 * B_WIDTH + 128 * pair:2 * B_WIDTH + 128 * pair + 128] = dv.astype(BF16)
            db_ref[:, 3 * B_WIDTH + 128 * pair:3 * B_WIDTH + 128 * pair + 128] = dz.astype(BF16)

    rev = lambda c: pl.BlockSpec((CHUNK, B_WIDTH), lambda i, c=c: (n - 1 - i, c // B_WIDTH))
    full = lambda a: pl.BlockSpec(a.shape, lambda i, n_=a.ndim: (0,) * n_)
    acc = lambda s: pl.BlockSpec(s, lambda i, n_=len(s): (0,) * n_)
    return pl.pallas_call(
        body, name=f"hgrn_bwd_{tag}", grid=(n,),
        in_specs=[rev(COL_BQ), rev(COL_BF), rev(COL_BI), rev(COL_BZ),
                  pl.BlockSpec((CHUNK, B_WIDTH), lambda i: (n - 1 - i, 1)),
                  pl.BlockSpec((CHUNK, B_WIDTH), lambda i: (n - 1 - i, 0)),
                  pl.BlockSpec((1, 2, 128, 128), lambda i: (n - 1 - i, 0, 0, 0)), full(lb), full(onorm)],
        out_specs=[pl.BlockSpec((CHUNK, 4 * B_WIDTH), lambda i: (n - 1 - i, 0)), acc((1, B_WIDTH)), acc((1, 128))],
        out_shape=[SDS((T, 4 * B_WIDTH), BF16), SDS((1, B_WIDTH), F32), SDS((1, 128), F32)],
        scratch_shapes=[pltpu.VMEM((2, 128, 128), F32), pltpu.VMEM((nsub, 128, 128), F32)],
        compiler_params=_params("arbitrary"),
    )(proj, proj, proj, proj, dy, o_saved, s0, lb, onorm)


def _lb_fwd(hgrn_lb):
    assert hgrn_lb.shape[0] == 2

    def body(x_ref, o_ref):
        x0, x1 = x_ref[0:1, :], x_ref[1:2, :]
        m = jnp.maximum(x0, x1)
        e0, e1 = jnp.exp(x0 - m), jnp.exp(x1 - m)
        p0, p1 = e0 / (e0 + e1), e1 / (e0 + e1)
        o_ref[0:1, :] = jnp.clip(p0 - p0, 0.0, 1.0 - 1e-6)
        o_ref[1:2, :] = jnp.clip((p0 + p1) - p0, 0.0, 1.0 - 1e-6)

    return pl.pallas_call(body, name="lb_fwd", out_shape=SDS(hgrn_lb.shape, F32))(hgrn_lb)


def _lb_bwd(hgrn_lb, dlb):
    def body(x_ref, d_ref, o_ref):
        x0, x1 = x_ref[0:1, :], x_ref[1:2, :]
        m = jnp.maximum(x0, x1)
        e0, e1 = jnp.exp(x0 - m), jnp.exp(x1 - m)
        p0, p1 = e0 / (e0 + e1), e1 / (e0 + e1)
        val = (p0 + p1) - p0
        dp1 = jnp.where((val > 0.0) & (val < 1.0 - 1e-6), d_ref[1:2, :], 0.0)
        inner = p1 * dp1
        o_ref[0:1, :] = p0 * (0.0 - inner)
        o_ref[1:2, :] = p1 * (dp1 - inner)

    return pl.pallas_call(body, name="lb_bwd", out_shape=SDS(hgrn_lb.shape, F32))(hgrn_lb, dlb)


def _fox_prep(proj, bf, tag):
    T = proj.shape[0]
    n = T // CHUNK

    def body(q0_ref, q1_ref, k0_ref, k1_ref, v0_ref, v1_ref, fl_ref, bf_ref, qo_ref, ko_ref, vt_ref, carry_ref):
        for p, v_ref in enumerate((v0_ref, v0_ref, v1_ref, v1_ref)):
            vt_ref[p, 0] = v_ref[:, 128 * (p % 2):128 * (p % 2) + 128].T.astype(BF16)

        @pl.when(pl.program_id(0) == 0)
        def _():
            carry_ref[...] = jnp.zeros_like(carry_ref)

        ltri = jnp.where(_lane((CHUNK, CHUNK)) <= _row((CHUNK, CHUNK)), 1.0, 0.0).astype(BF16)
        lf = jax.nn.log_sigmoid(fl_ref[...] + bf_ref[...])
        c = _dot3_left(ltri, lf) + carry_ref[...]
        carry_ref[...] = c[CHUNK - 1:CHUNK, :]
        lane = _lane((CHUNK, 128))
        feat = lane < 64
        ones_q = (lane >= 67) & (lane <= 69)
        ones_k = (lane >= 64) & (lane <= 66)
        qrefs, krefs = (q0_ref, q1_ref), (k0_ref, k1_ref)
        for h in range(C_HEADS):
            blk = slice(128 * ((h // 2) % 2), 128 * ((h // 2) % 2) + 128)
            qp, kp = qrefs[h // 4][:, blk], krefs[h // 4][:, blk]
            if h % 2:
                qp, kp = pltpu.roll(qp, 64, axis=1), pltpu.roll(kp, 64, axis=1)
            ch = jnp.broadcast_to(c[:, h:h + 1], (CHUNK, 128))
            hi = ch.astype(BF16).astype(F32)
            r1 = ch - hi
            mid = r1.astype(BF16).astype(F32)
            lo = r1 - mid
            aq = jnp.where(lane == 64, hi, jnp.where(lane == 65, mid, jnp.where(lane == 66, lo,
                           jnp.where(ones_q, 1.0, 0.0))))
            ak = jnp.where(lane == 67, -hi, jnp.where(lane == 68, -mid, jnp.where(lane == 69, -lo,
                           jnp.where(ones_k, 1.0, 0.0))))
            qo_ref[:, 128 * h:128 * h + 128] = jnp.where(feat, qp * Q_SCALE, aq).astype(BF16)
            ko_ref[:, 128 * h:128 * h + 128] = jnp.where(feat, kp, ak).astype(BF16)

    w = 256
    col = lambda c: pl.BlockSpec((CHUNK, w), lambda i, c=c: (i, c // w))
    return pl.pallas_call(
        body, name=f"fox_prep_{tag}", grid=(n,),
        in_specs=[col(COL_CQ), col(COL_CQ + w), col(COL_CK), col(COL_CK + w), col(COL_CV), col(COL_CV + w),
                  pl.BlockSpec((CHUNK, 128), lambda i: (i, COL_CF // 128)), pl.BlockSpec((1, 128), lambda i: (0, 0))],
        out_specs=[pl.BlockSpec((CHUNK, C_HEADS * 128), lambda i: (i, 0))] * 2
        + [pl.BlockSpec((C_HEADS // 2, 1, 128, CHUNK), lambda i: (0, i, 0, 0))],
        out_shape=[SDS((T, C_HEADS * 128), BF16)] * 2 + [SDS((C_HEADS // 2, n, 128, CHUNK), BF16)],
        scratch_shapes=[pltpu.VMEM((1, 128), F32)], compiler_params=_params("arbitrary"),
    )(proj, proj, proj, proj, proj, proj, proj, bf)


FOX_TILE = 512
FOX_KEYS = 512


def _fox_mask(tk, tq, k0, q0):
    return (_row((tk, tq)) + (k0 - q0)) <= _lane((tk, tq))


def _ride_refs(ride, rest, n_out, n_scratch):
    n = ride.n if ride else 0
    srcs, rest = rest[:n], rest[n:]
    outs, rest = rest[:n_out], rest[n_out:]
    dsts, rest = rest[:n], rest[n:]
    return srcs, outs, dsts, rest[:n_scratch], rest[n_scratch:]


def _ride_start(ride, grid, srcs, dsts, sems):
    if ride:
        first = functools.reduce(lambda a, b: a & b, [pl.program_id(d) == 0 for d in range(len(grid))])
        pl.when(first)(lambda: ride.start(srcs, dsts, sems))


def _ride_wait(ride, grid, srcs, dsts, sems):
    if ride:
        last = functools.reduce(lambda a, b: a & b, [pl.program_id(d) == n - 1 for d, n in enumerate(grid)])
        pl.when(last)(lambda: ride.wait(srcs, dsts, sems))


def _fox_fwd(qt, kt, vt, proj, tag, ride=None):
    T = proj.shape[0]
    tq, tk = _tile(T, FOX_TILE), _tile(T, FOX_KEYS)
    nq, nsub = T // tq, tk // CHUNK
    npair = C_HEADS // 2

    def body(q_ref, k_ref, vt_ref, z_ref, *rest):
        ride_srcs, (o_ref, lse_ref, y_ref), ride_dsts, (acc_ref, st_ref, pt_ref), ride_sems = _ride_refs(ride, rest, 3, 3)
        i = pl.program_id(1)
        _ride_start(ride, (npair, nq), ride_srcs, ride_dsts, ride_sems)

        qs = (q_ref[:, 0:128], q_ref[:, 128:256])
        acc_ref[...] = jnp.zeros_like(acc_ref)
        pt_ref[...] = jnp.zeros_like(pt_ref)
        nfull = (i * tq) // tk

        def scores(j):
            kb = k_ref[pl.ds(pl.multiple_of(j * tk, tk), tk), :]
            return tuple(_dot_nt(kb[:, 128 * h:128 * h + 128], qs[h]) for h in range(2))

        def weigh(j, h):
            rows = slice(64 * h, 64 * h + 64)
            pv = _dot(vt_ref[0, nsub * j, rows, :], pt_ref[h, 0:CHUNK, :])
            for c in range(1, nsub):
                pv = pv + _dot(vt_ref[0, nsub * j + c, rows, :], pt_ref[h, CHUNK * c:CHUNK * c + CHUNK, :])
            return pv

        def block(j, carry, diagonal):
            nxt = () if diagonal else scores(j + 1)
            pvs = [weigh(jnp.maximum(j - 1, 0), h) for h in range(2)]
            new = []
            for h in range(2):
                m, l, alpha_prev = carry[3 * h:3 * h + 3]
                st = st_ref[h]
                if diagonal:
                    st = jnp.where(_fox_mask(tk, tq, j * tk, i * tq), st, -jnp.inf)
                m_new = jnp.maximum(m, _colreduce(st, jnp.maximum))
                pt = jnp.exp(st - m_new)
                alpha = jnp.exp(m - m_new)
                rows = slice(64 * h, 64 * h + 64)
                acc_ref[rows, :] = alpha_prev * acc_ref[rows, :] + pvs[h]
                pt_ref[h] = pt.astype(BF16)
                new += [m_new, alpha * l + _colreduce(pt, jnp.add), alpha]
            for h, st in enumerate(nxt):
                st_ref[h] = st
            return tuple(new)

        for h, st in enumerate(scores(0)):
            st_ref[h] = st
        init = (jnp.full((1, tq), -jnp.inf, F32), jnp.zeros((1, tq), F32), jnp.ones((1, tq), F32)) * 2
        carry = lax.fori_loop(0, nfull, lambda j, c: block(j, c, False), init)
        m0, l0, a0, m1, l1, a1 = block(nfull, carry, True)
        for h, alpha in enumerate((a0, a1)):
            rows = slice(64 * h, 64 * h + 64)
            acc_ref[rows, :] = alpha * acc_ref[rows, :] + weigh(nfull, h)
        inv = jnp.where(_row((128, tq)) < 64, 1.0 / l0, 1.0 / l1)
        o = (acc_ref[...] * inv).T
        o_ref[...] = o
        r8 = _row((8, tq))
        lse_ref[0, 0] = jnp.where(r8 == 0, m0 + jnp.log(l0), jnp.where(r8 == 1, m1 + jnp.log(l1), 0.0))
        sz, _ = _silu_and_grad(z_ref[...])
        y_ref[...] = (o * sz).astype(BF16)
        _ride_wait(ride, (npair, nq), ride_srcs, ride_dsts, ride_sems)

    blk = pl.BlockSpec((tq, 128), lambda p, i: (i, p))
    extra = ride or _ChipExchange("gather", ())
    return pl.pallas_call(
        body, name=f"fox_fwd_{tag}", grid=(npair, nq),
        in_specs=[pl.BlockSpec((tq, 256), lambda p, i: (i, p)), pl.BlockSpec((T, 256), lambda p, i: (0, p)),
                  pl.BlockSpec((1, T // CHUNK, 128, CHUNK), lambda p, i: (p, 0, 0, 0)),
                  pl.BlockSpec((tq, 128), lambda p, i: (i, COL_CZ // 128 + p))] + extra.in_specs,
        out_specs=[blk, pl.BlockSpec((1, 1, 8, tq), lambda p, i: (p, i, 0, 0)), blk] + extra.out_specs,
        out_shape=[SDS((T, C_WIDTH), F32), SDS((npair, nq, 8, tq), F32), SDS((T, C_WIDTH), BF16)] + extra.out_shape,
        scratch_shapes=[pltpu.VMEM((128, tq), F32), pltpu.VMEM((2, tk, tq), F32), pltpu.VMEM((2, tk, tq), BF16)]
        + (extra.scratch if ride else []),
        compiler_params=pltpu.CompilerParams(dimension_semantics=("arbitrary", "arbitrary"), vmem_limit_bytes=VMEM_LIMIT,
                                             has_side_effects=bool(ride)),
    )(qt, kt, vt, proj, *extra.sources)


def _fox_bwd_prep(proj, dy, o, tag):
    T = proj.shape[0]
    tq = _tile(T, FOX_TILE)

    def body(z0_ref, z1_ref, dy_ref, o_ref, do_ref, dl_ref, dz_ref):
        sel = jnp.where((_lane((16, 128)) >> 6) == _row((16, 128)), 1.0, 0.0).astype(BF16)
        for p, z_ref in enumerate((z0_ref, z0_ref, z1_ref, z1_ref)):
            sl = slice(128 * p, 128 * p + 128)
            sz, dsz = _silu_and_grad(z_ref[:, 128 * (p % 2):128 * (p % 2) + 128])
            dyv, ov = dy_ref[:, sl], o_ref[:, sl]
            do = dyv * sz
            do_ref[:, sl] = do.astype(BF16)
            dz_ref[:, sl] = (dyv * ov * dsz).astype(BF16)
            hi, mid, lo = _split3(do * ov)
            dl_ref[p, 0] = (_dot_nt(sel, hi) + _dot_nt(sel, mid) + _dot_nt(sel, lo))[0:8, :]

    w = 256
    blk = pl.BlockSpec((tq, C_WIDTH), lambda i: (i, 0))
    return pl.pallas_call(
        body, name=f"fox_bwd_prep_{tag}", grid=(T // tq,),
        in_specs=[pl.BlockSpec((tq, w), lambda i: (i, COL_CZ // w)), pl.BlockSpec((tq, w), lambda i: (i, COL_CZ // w + 1)),
                  pl.BlockSpec((tq, C_WIDTH), lambda i: (i, (A_WIDTH + B_WIDTH) // C_WIDTH)), blk],
        out_specs=[blk, pl.BlockSpec((C_HEADS // 2, 1, 8, tq), lambda i: (0, i, 0, 0)), blk],
        out_shape=[SDS((T, C_WIDTH), BF16), SDS((C_HEADS // 2, T // tq, 8, tq), F32), SDS((T, C_WIDTH), BF16)],
        compiler_params=_params("parallel"),
    )(proj, proj, dy, o)


def _fox_bwd(qt, kt, proj, do, lse, delta, tag, ride=None):
    T = proj.shape[0]
    tq, tk = _tile(T, FOX_TILE), _tile(T, FOX_KEYS)
    nq, nk = T // tq, T // tk
    ndiag = tk // tq
    npair = C_HEADS // 2

    def body(q_ref, k_ref, v_ref, do_ref, lse_ref, dl_ref, *rest):
        ride_srcs, (dq_ref, dk_ref, dv_ref), ride_dsts, scratch, ride_sems = _ride_refs(ride, rest, 3, 4)
        dvacc_ref, sc_ref, pt_ref, ds_ref = scratch
        j = pl.program_id(1)
        first = (j * tk) // tq
        _ride_start(ride, (npair, nk), ride_srcs, ride_dsts, ride_sems)

        @pl.when(j == 0)
        def _():
            dq_ref[...] = jnp.zeros_like(dq_ref)

        dk_ref[...] = jnp.zeros_like(dk_ref)
        dvacc_ref[...] = jnp.zeros_like(dvacc_ref)
        ks = (k_ref[:, 0:128], k_ref[:, 128:256])
        kts = tuple(k.astype(F32).T.astype(BF16) for k in ks)
        vb = v_ref[...].astype(BF16)
        lo = _lane((tq, 128)) < 64

        def operands(i):
            q0 = pl.multiple_of(i * tq, tq)
            qb = q_ref[pl.ds(q0, tq), :]
            dob = do_ref[pl.ds(q0, tq), :]
            qhs = (qb[:, 0:128], qb[:, 128:256])
            dohs = (jnp.where(lo, dob, jnp.zeros_like(dob)), jnp.where(lo, jnp.zeros_like(dob), dob))
            return qhs, dohs

        def scores(i):
            qhs, dohs = operands(i)
            return tuple(_dot_nt(ks[h], qhs[h]) for h in range(2)) + tuple(_dot_nt(vb, dohs[h]) for h in range(2))

        def park(sc):
            for a, s in enumerate(sc):
                sc_ref[a] = s

        def grads(i):
            qhs, dohs = operands(i)
            dvacc_ref[...] += _dot(jnp.concatenate([pt_ref[0], pt_ref[1]], axis=1), jnp.concatenate(dohs, axis=0))
            for h in range(2):
                dk_ref[:, 128 * h:128 * h + 128] += _dot(ds_ref[h], qhs[h])
                dq_ref[h, i] += _dot(kts[h], ds_ref[h])

        def block(i, diagonal, opening):
            nxt = scores(jnp.minimum(i + 1, nq - 1))
            if not opening:
                grads(i - 1)
            lsev = lse_ref[0, i]
            dlv = dl_ref[0, i]
            for h in range(2):
                pt = jnp.exp(sc_ref[h] - lsev[h:h + 1, :])
                if diagonal:
                    pt = jnp.where(_fox_mask(tk, tq, j * tk, i * tq), pt, 0.0)
                ds_ref[h] = (pt * (sc_ref[2 + h] - dlv[h:h + 1, :])).astype(BF16)
                pt_ref[h] = pt.astype(BF16)
            park(nxt)

        park(scores(first))
        for d in range(ndiag):
            block(first + d, True, d == 0)

        def step(i, carry):
            block(i, False, False)
            return carry

        lax.fori_loop(first + ndiag, nq, step, 0)
        grads(nq - 1)
        dv_ref[...] = dvacc_ref[...].astype(BF16)
        _ride_wait(ride, (npair, nk), ride_srcs, ride_dsts, ride_sems)

    full = lambda w: pl.BlockSpec((T, w), lambda p, j: (0, p))
    stat = pl.BlockSpec((1, nq, 8, tq), lambda p, j: (p, 0, 0, 0))
    extra = ride or _ChipExchange("gather", ())
    return pl.pallas_call(
        body, name=f"fox_bwd_{tag}", grid=(npair, nk),
        in_specs=[full(256), pl.BlockSpec((tk, 256), lambda p, j: (j, p)),
                  pl.BlockSpec((tk, 128), lambda p, j: (j, COL_CV // 128 + p)), full(128), stat, stat] + extra.in_specs,
        out_specs=[pl.BlockSpec((2, nq, 128, tq), lambda p, j: (p, 0, 0, 0)), pl.BlockSpec((tk, 256), lambda p, j: (j, p)),
                   pl.BlockSpec((tk, 128), lambda p, j: (j, p))] + extra.out_specs,
        out_shape=[SDS((C_HEADS, nq, 128, tq), F32), SDS((T, C_HEADS * 128), F32), SDS((T, C_WIDTH), BF16)]
        + extra.out_shape,
        scratch_shapes=[pltpu.VMEM((tk, 128), F32), pltpu.VMEM((4, tk, tq), F32), pltpu.VMEM((2, tk, tq), BF16),
                        pltpu.VMEM((2, tk, tq), BF16)] + (extra.scratch if ride else []),
        compiler_params=pltpu.CompilerParams(dimension_semantics=("arbitrary", "arbitrary"), vmem_limit_bytes=VMEM_LIMIT,
                                             has_side_effects=bool(ride)),
    )(qt, kt, proj, do, lse, delta, *extra.sources)


def _fox_bwd_post(dqt, dkt, proj, bf, tag):
    T = proj.shape[0]
    tq = _tile(T, FOX_TILE)
    n = T // tq

    def body(dq_ref, dk_ref, fl_ref, bf_ref, oq_ref, ok_ref, ofl_ref, dbf_ref, carry_ref):
        @pl.when(pl.program_id(0) == 0)
        def _():
            carry_ref[...] = jnp.zeros_like(carry_ref)
            dbf_ref[...] = jnp.zeros_like(dbf_ref)

        lane = _lane((tq, 128))
        lo = lane < 64
        dqs = [dq_ref[h, 0].T for h in range(C_HEADS)]
        dc = jnp.zeros((tq, 128), F32)
        for h in range(C_HEADS):
            dc = dc + jnp.where(lane == h, dqs[h][:, 64:65] - dk_ref[:, 128 * h + 67:128 * h + 68], 0.0)
        utri = jnp.where(_lane((tq, tq)) >= _row((tq, tq)), 1.0, 0.0).astype(BF16)
        dlf = _dot3_left(utri, dc) + carry_ref[...]
        carry_ref[...] = dlf[0:1, :]
        dfl = jnp.where(lane < C_HEADS, dlf * _sigmoid(-(fl_ref[...] + bf_ref[...])), 0.0)
        ofl_ref[...] = dfl.astype(BF16)
        dbf_ref[...] += jnp.sum(dfl, axis=0, keepdims=True)
        for p in range(C_HEADS // 2):
            a, b = 128 * (2 * p), 128 * (2 * p + 1)
            oq_ref[:, 128 * p:128 * p + 128] = (
                jnp.where(lo, dqs[2 * p], pltpu.roll(dqs[2 * p + 1], 64, axis=1)) * Q_SCALE).astype(BF16)
            ok_ref[:, 128 * p:128 * p + 128] = jnp.where(
                lo, dk_ref[:, a:a + 128], pltpu.roll(dk_ref[:, b:b + 128], 64, axis=1)).astype(BF16)

    rev = lambda w: pl.BlockSpec((tq, w), lambda i: (n - 1 - i, 0))
    return pl.pallas_call(
        body, name=f"fox_bwd_post_{tag}", grid=(n,),
        in_specs=[pl.BlockSpec((C_HEADS, 1, 128, tq), lambda i: (0, n - 1 - i, 0, 0)), rev(C_HEADS * 128),
                  pl.BlockSpec((tq, 128), lambda i: (n - 1 - i, COL_CF // 128)), pl.BlockSpec((1, 128), lambda i: (0, 0))],
        out_specs=[rev(C_WIDTH), rev(C_WIDTH), rev(128), pl.BlockSpec((1, 128), lambda i: (0, 0))],
        out_shape=[SDS((T, C_WIDTH), BF16), SDS((T, C_WIDTH), BF16), SDS((T, 128), BF16), SDS((1, 128), F32)],
        scratch_shapes=[pltpu.VMEM((1, 128), F32)], compiler_params=_params("arbitrary"),
    )(dqt, dkt, proj, bf)


def _adamw_math(w, g, m, v):
    m = ADAM_B1 * m + (1.0 - ADAM_B1) * g
    v = ADAM_B2 * v + (1.0 - ADAM_B2) * (g * g)
    delta = -ADAM_LR * ((m / ADAM_C1) / (jnp.sqrt(v / ADAM_C2) + ADAM_EPS) + ADAM_WD * w)
    return delta, m, v


def _adamw_pair(w, m, v, ga, gb, name):
    n0 = w.shape[0]
    most = max(1, ADAMW_BLOCK_BYTES // (4 * math.prod(w.shape[1:])))
    t0 = max(t for t in range(1, min(n0, most) + 1) if n0 % t == 0)

    def body(w_ref, m_ref, v_ref, ga_ref, gb_ref, g_ref, d_ref, nm_ref, nv_ref):
        g = ga_ref[...] + gb_ref[...]
        g_ref[...] = g
        d_ref[...], nm_ref[...], nv_ref[...] = _adamw_math(w_ref[...], g, m_ref[...], v_ref[...])

    blk = pl.BlockSpec((t0,) + w.shape[1:], lambda i: (i, 0, 0))
    return pl.pallas_call(
        body, name=name, grid=(n0 // t0,), in_specs=[blk] * 5, out_specs=[blk] * 4,
        out_shape=[SDS(w.shape, F32)] * 4, compiler_params=_params("parallel"),
    )(w, m, v, ga, gb)


def _adamw_small(w, m, v, gall):
    R = w.shape[0]

    def body(w_ref, m_ref, v_ref, g_ref, go_ref, d_ref, nm_ref, nv_ref):
        g = g_ref[0]
        for k in range(1, N_DEV):
            g = g + g_ref[k]
        go_ref[...] = g
        d_ref[...], nm_ref[...], nv_ref[...] = _adamw_math(w_ref[...], g, m_ref[...], v_ref[...])

    return pl.pallas_call(body, name="adamw_small", out_shape=[SDS((R, 128), F32)] * 4,
                          compiler_params=pltpu.CompilerParams(vmem_limit_bytes=VMEM_LIMIT))(w, m, v, gall)


def _sum_chips(layers, name, layer_major):
    _, R, C = layers[0].shape
    L = len(layers)
    tc = _tile(C, 256)

    def body(*refs):
        o_ref = refs[-1]
        for l, p_ref in enumerate(refs[:-1]):
            p = [p_ref[k].astype(F32) for k in range(N_CHIPS)]
            s = ((p[0] + p[1]) + p[2]) + p[3]
            if layer_major:
                o_ref[l] = s
            else:
                o_ref[:, l, :] = s

    out = (L, R, C) if layer_major else (R, L, C)
    out_blk = (L, R, tc) if layer_major else (R, L, tc)
    return pl.pallas_call(
        body, name=name, grid=(C // tc,),
        in_specs=[pl.BlockSpec((N_CHIPS, R, tc), lambda i: (0, 0, i))] * L,
        out_specs=pl.BlockSpec(out_blk, lambda i: (0, 0, i)), out_shape=SDS(out, F32),
        compiler_params=_params("parallel"),
    )(*layers)


ANY = pl.BlockSpec(memory_space=pl.ANY)


def _mesh_pos():
    return lax.axis_index("x"), lax.axis_index("y"), lax.axis_index("c")


def _other_chips(x, y):
    return [(1 - x, y), (x, 1 - y), (1 - x, 1 - y)]


class _ChipExchange:
    def __init__(self, mode, sources):
        assert mode in ("gather", "scatter")
        self.mode, self.sources = mode, tuple(sources)
        self.n = len(self.sources)
        self.in_specs = [ANY] * self.n
        self.out_specs = [ANY] * self.n
        self.out_shape = [SDS(((N_CHIPS,) + s.shape) if mode == "gather" else s.shape, s.dtype) for s in self.sources]
        self.scratch = [pltpu.SemaphoreType.DMA((3 * self.n,)), pltpu.SemaphoreType.DMA((3 * self.n,)),
                        pltpu.SemaphoreType.DMA((self.n,))]

    def _copies(self, srcs, dsts, send_sems, recv_sems, local_sems):
        x, y, c = _mesh_pos()
        me = 2 * x + y
        view = (lambda r, chip: r) if self.mode == "gather" else (lambda r, chip: r.at[chip])
        local = [pltpu.make_async_copy(view(s, me), d.at[me], local_sems.at[a]) for a, (s, d) in enumerate(zip(srcs, dsts))]
        sends, recvs = [], []
        for j, (px, py) in enumerate(_other_chips(x, y)):
            peer = 2 * px + py
            for a, (s, d) in enumerate(zip(srcs, dsts)):
                sems = dict(send_sem=send_sems.at[self.n * j + a], recv_sem=recv_sems.at[self.n * j + a],
                            device_id=(px, py, c), device_id_type=MESH_ID)
                sends.append(pltpu.make_async_remote_copy(src_ref=view(s, peer), dst_ref=d.at[me], **sems))
                recvs.append(pltpu.make_async_remote_copy(src_ref=view(s, me), dst_ref=d.at[peer], **sems))
        return local, sends, recvs

    def start(self, srcs, dsts, sems):
        local, sends, _ = self._copies(srcs, dsts, *sems)
        for cp in local + sends:
            cp.start()

    def wait(self, srcs, dsts, sems):
        local, sends, recvs = self._copies(srcs, dsts, *sems)
        for cp in recvs:
            cp.wait_recv()
        for cp in sends:
            cp.wait_send()
        for cp in local:
            cp.wait()


def _gather_halves(w, tag):
    R, C = w.shape
    H = R // 2

    def body(w_ref, g_ref, send_sems, recv_sems, pass_send, pass_recv, local_sem):
        x, y, c = _mesh_pos()
        me = 2 * x + y
        mine, theirs = pl.ds(c * H, H), pl.ds((1 - c) * H, H)
        own = pltpu.make_async_copy(w_ref, g_ref.at[me], local_sem)
        own.start()

        def fetch(j, px, py, src, dst):
            return pltpu.make_async_remote_copy(src_ref=src, dst_ref=dst, send_sem=send_sems.at[j], recv_sem=recv_sems.at[j],
                                                device_id=(px, py, c), device_id_type=MESH_ID)

        def hand(j, rows, peer):
            return pltpu.make_async_remote_copy(src_ref=g_ref.at[peer, rows], dst_ref=g_ref.at[peer, rows],
                                                send_sem=pass_send.at[j], recv_sem=pass_recv.at[j],
                                                device_id=(x, y, 1 - c), device_id_type=MESH_ID)

        chips = _other_chips(x, y)
        sends = [fetch(j, px, py, w_ref.at[mine], g_ref.at[me, mine]) for j, (px, py) in enumerate(chips)]
        for cp in sends:
            cp.start()
        passed = []
        for j, (px, py) in enumerate(chips):
            peer = 2 * px + py
            fetch(j, px, py, w_ref.at[mine], g_ref.at[peer, mine]).wait_recv()
            passed.append(hand(j, mine, peer))
            passed[-1].start()
        for j, (px, py) in enumerate(chips):
            hand(j, theirs, 2 * px + py).wait_recv()
        for cp in sends + passed:
            cp.wait_send()
        own.wait()

    return pl.pallas_call(
        body, name=f"gather_halves_{tag}", in_specs=[ANY], out_specs=ANY, out_shape=SDS((N_CHIPS, R, C), w.dtype),
        scratch_shapes=[pltpu.SemaphoreType.DMA((3,)), pltpu.SemaphoreType.DMA((3,)), pltpu.SemaphoreType.DMA((3,)),
                        pltpu.SemaphoreType.DMA((3,)), pltpu.SemaphoreType.DMA],
        compiler_params=pltpu.CompilerParams(has_side_effects=True),
    )(w)


class _DeviceGather:
    def __init__(self, source):
        self.sources, self.n = (source,), 1
        self.in_specs, self.out_specs = [ANY], [ANY]
        self.out_shape = [SDS((N_DEV,) + source.shape, source.dtype)]
        self.scratch = [pltpu.SemaphoreType.DMA((N_DEV - 1,)), pltpu.SemaphoreType.DMA((N_DEV - 1,)),
                        pltpu.SemaphoreType.DMA((1,))]

    def _copies(self, srcs, dsts, send_sems, recv_sems, local_sems):
        (src,), (dst,) = srcs, dsts
        x, y, c = _mesh_pos()
        me = 4 * x + 2 * y + c
        local = [pltpu.make_async_copy(src, dst.at[me], local_sems.at[0])]
        sends, recvs = [], []
        for k in range(1, N_DEV):
            px, py, pc = (1 - x) if k & 4 else x, (1 - y) if k & 2 else y, (1 - c) if k & 1 else c
            sems = dict(send_sem=send_sems.at[k - 1], recv_sem=recv_sems.at[k - 1], device_id=(px, py, pc),
                        device_id_type=MESH_ID)
            sends.append(pltpu.make_async_remote_copy(src_ref=src, dst_ref=dst.at[me], **sems))
            recvs.append(pltpu.make_async_remote_copy(src_ref=src, dst_ref=dst.at[4 * px + 2 * py + pc], **sems))
        return local, sends, recvs

    start = _ChipExchange.start
    wait = _ChipExchange.wait


def _gather_devices(a, name):
    ex = _DeviceGather(a)

    def body(a_ref, g_ref, *sems):
        ex.start((a_ref,), (g_ref,), sems)
        ex.wait((a_ref,), (g_ref,), sems)

    return pl.pallas_call(
        body, name=name, in_specs=ex.in_specs, out_specs=ex.out_specs[0], out_shape=ex.out_shape[0],
        scratch_shapes=ex.scratch, compiler_params=pltpu.CompilerParams(has_side_effects=True),
    )(a)


def _swap_cores(pin, pout):
    def body(pin_ref, pout_ref, oin_ref, oout_ref, send_sems, recv_sems):
        x, y, c = _mesh_pos()
        cps = [pltpu.make_async_remote_copy(src_ref=src, dst_ref=dst, send_sem=send_sems.at[a], recv_sem=recv_sems.at[a],
                                            device_id=(x, y, 1 - c), device_id_type=MESH_ID)
               for a, (src, dst) in enumerate(((pin_ref, oin_ref), (pout_ref, oout_ref)))]
        for cp in cps:
            cp.start()
        for cp in cps:
            cp.wait()

    return pl.pallas_call(
        body, name="swap_cores", in_specs=[ANY, ANY], out_specs=[ANY, ANY],
        out_shape=[SDS(pin.shape, F32), SDS(pout.shape, F32)],
        scratch_shapes=[pltpu.SemaphoreType.DMA((2,)), pltpu.SemaphoreType.DMA((2,))],
        compiler_params=pltpu.CompilerParams(has_side_effects=True),
    )(pin, pout)


def _pack_small(parts):
    flat = [jnp.pad(p.reshape(-1), (0, (-p.size) % 128)) for p in parts]
    v = jnp.concatenate(flat)
    return jnp.pad(v, (0, (-v.size) % 1024)).reshape(-1, 128)


def _unpack_small(packed):
    flat = packed.reshape(-1)
    out, off = [], 0
    for _, shape in SMALL_PARAMS:
        size = math.prod(shape)
        out.append(flat[off:off + size].reshape(shape))
        off += size + (-size) % 128
    return out


def _layer_consts(l, gmlp_ln_g, gmlp_ln_b, gmlp_w_s, gmlp_b_s, hgrn_onorm_g, fox_b_f):
    causal = jnp.tril(jnp.ones((CHUNK, CHUNK), bool))
    wm = jnp.where(causal[None], gmlp_w_s[l], 0.0)
    return dict(
        lng=gmlp_ln_g[l].reshape(1, A_WIDTH), lnb=gmlp_ln_b[l].reshape(1, A_WIDTH),
        wm=wm.astype(BF16), wmt=jnp.swapaxes(wm, 1, 2).astype(BF16),
        bst=jnp.pad(gmlp_b_s[l].T, ((0, 0), (0, 128 - A_GROUPS))),
        onorm=jnp.tile(hgrn_onorm_g[l], 4).reshape(1, B_WIDTH),
        bf=jnp.pad(fox_b_f[l], (0, 128 - C_HEADS)).reshape(1, 128),
    )


def kernel(x, norm_g, w_in, w_out, gmlp_ln_g, gmlp_ln_b, gmlp_w_s, gmlp_b_s, hgrn_lb, hgrn_onorm_g, fox_b_f, final_norm_g, loss_target, m_norm_g, m_w_in, m_w_out, m_gmlp_ln_g, m_gmlp_ln_b, m_gmlp_w_s, m_gmlp_b_s, m_hgrn_lb, m_hgrn_onorm_g, m_fox_b_f, m_final_norm_g, v_norm_g, v_w_in, v_w_out, v_gmlp_ln_g, v_gmlp_ln_b, v_gmlp_w_s, v_gmlp_b_s, v_hgrn_lb, v_hgrn_onorm_g, v_fox_b_f, v_final_norm_g):
    T = x.shape[1]
    shard_in = w_in.shape[2]
    shard_out = w_out.shape[1]
    xs = x.reshape(T, D_MODEL)
    tgt = loss_target.reshape(T, D_MODEL)

    w_in_b, w_out_b = w_in.astype(BF16), w_out.astype(BF16)

    def full_w_in(gathered):
        wi = jnp.concatenate([gathered[k] for k in range(N_CHIPS)], axis=-1)
        return jnp.pad(wi, ((0, 0), (0, D_IN_PAD - D_IN)))

    lb_all = _lb_fwd(hgrn_lb)
    consts = [_layer_consts(l, gmlp_ln_g, gmlp_ln_b, gmlp_w_s, gmlp_b_s, hgrn_onorm_g, fox_b_f) for l in range(DEPTH)]

    saved = []
    xl = xs
    w_in_l = full_w_in(_gather_halves(w_in_b[0], "w_in_l0"))
    for l in range(DEPTH):
        cs = consts[l]
        tag = f"l{l}"
        h, proj = _inproj(xl, norm_g[l].reshape(1, D_MODEL), w_in_l, tag)
        ya = _gmlp_fwd(proj, cs["lng"], cs["lnb"], cs["wm"], cs["bst"], tag)
        yb, ob, s0 = _hgrn_fwd(proj, lb_all[l].reshape(1, B_WIDTH), cs["onorm"], tag)
        qt, kt, vt = _fox_prep(proj, cs["bf"], tag)
        ride = _ChipExchange("gather", (w_out_b[l],) + ((w_in_b[l + 1],) if l + 1 < DEPTH else ()))
        oc, lse, yc, *gathered = _fox_fwd(qt, kt, vt, proj, tag, ride)
        w_out_l = gathered[0].reshape(N_CHIPS * shard_out, D_MODEL)
        saved.append(dict(x=xl, h=h, proj=proj, ya=ya, yb=yb, yc=yc, ob=ob, s0=s0, qt=qt, kt=kt, oc=oc, lse=lse,
                          w_in=w_in_l, w_out=w_out_l))
        xl = _outproj(xl, ya, yb, yc, w_out_l, tag)
        if l + 1 < DEPTH:
            w_in_l = full_w_in(gathered[1])

    dx, loss_part, d_final = _loss_head(xl, final_norm_g.reshape(1, D_MODEL), tgt)
    loss = lax.psum(loss_part[0, 0], ("x", "y", "c"))

    g_small = {}
    dlb_rows, rin, rout = [None] * DEPTH, [None] * DEPTH, [None] * DEPTH
    slabs_in = None
    stack = lambda key: jnp.stack([g_small[l][key] for l in range(DEPTH)])
    for l in reversed(range(DEPTH)):
        cs, sv = consts[l], saved[l]
        tag = f"l{l}"
        proj = sv["proj"]
        dy, dw_out = _outproj_bwd(dx, sv["ya"], sv["yb"], sv["yc"], sv["w_out"], tag)
        da, dwm, dbst, dlng, dlnb = _gmlp_bwd(proj, dy, cs["lng"], cs["lnb"], cs["wm"], cs["wmt"], cs["bst"], tag)
        db, dlb_rows[l], donorm = _hgrn_bwd(proj, dy, sv["ob"], sv["s0"], lb_all[l].reshape(1, B_WIDTH), cs["onorm"], tag)
        do, delta, dzc = _fox_bwd_prep(proj, dy, sv["oc"], tag)
        slabs_out = dw_out.reshape(N_CHIPS, shard_out, D_MODEL).astype(BF16)
        ride = _ChipExchange("scatter", (slabs_out,) + ((slabs_in,) if slabs_in is not None else ()))
        dqt, dkt, dvc, *received = _fox_bwd(sv["qt"], sv["kt"], proj, do, sv["lse"], delta, tag, ride)
        rout[l] = received[0]
        if slabs_in is not None:
            rin[l + 1] = received[1]
        dqc, dkc, dflc, dbf = _fox_bwd_post(dqt, dkt, proj, cs["bf"], tag)
        g_small[l] = dict(ln_g=dlng.reshape(4, 64), ln_b=dlnb.reshape(4, 64), w_s=dwm, b_s=dbst[:, :A_GROUPS].T,
                          onorm=donorm[0, :64], bf=dbf[0, :C_HEADS])
        dproj = jnp.concatenate([da, db, dqc, dkc, dvc, dzc, dflc, jnp.zeros((T, 128), BF16)], axis=1)
        if l == 0:
            d_hgrn_lb = _lb_bwd(hgrn_lb, jnp.concatenate(dlb_rows, axis=0))
            early = _pack_small([stack("ln_g"), stack("ln_b"), stack("w_s"), stack("b_s"), d_hgrn_lb, stack("onorm"),
                                 stack("bf"), d_final.reshape(D_MODEL)])
            dw_in, rearly = _dw_in(sv["h"], dproj, tag, _DeviceGather(early))
        else:
            dw_in = _dw_in(sv["h"], dproj, tag)
        slabs_in = jnp.stack([dw_in[k * shard_in:(k + 1) * shard_in] for k in range(N_CHIPS)]).astype(BF16)
        ride = _ChipExchange("scatter", (slabs_in,)) if l == 0 else None
        dx, dng, *received = _dx_in(sv["x"], norm_g[l].reshape(1, D_MODEL), dx, dproj, sv["w_in"], tag, ride)
        if l == 0:
            rin[0] = received[0]
        g_small[l]["norm_g"] = dng.reshape(D_MODEL)
    grad_x = dx.reshape(x.shape)
    rlate = _gather_devices(_pack_small([stack("norm_g")]), "gather_norm_grads")
    rsmall = jnp.concatenate([rlate, rearly], axis=1)

    pin, pout = _sum_chips(rin, "sum_chips_w_in", False), _sum_chips(rout, "sum_chips_w_out", True)
    oin, oout = _swap_cores(pin, pout)
    to_view = lambda a: jnp.transpose(a, (2, 0, 1))
    g_w_in, d_w_in, nm_w_in, nv_w_in = [
        jnp.transpose(o, (1, 2, 0))
        for o in _adamw_pair(to_view(w_in), to_view(m_w_in), to_view(v_w_in), pin, oin, "adamw_w_in")]
    g_w_out, d_w_out, nm_w_out, nv_w_out = _adamw_pair(w_out, m_w_out, v_w_out, pout, oout, "adamw_w_out")

    small_w = [norm_g, gmlp_ln_g, gmlp_ln_b, gmlp_w_s, gmlp_b_s, hgrn_lb, hgrn_onorm_g, fox_b_f, final_norm_g]
    small_m = [m_norm_g, m_gmlp_ln_g, m_gmlp_ln_b, m_gmlp_w_s, m_gmlp_b_s, m_hgrn_lb, m_hgrn_onorm_g, m_fox_b_f, m_final_norm_g]
    small_v = [v_norm_g, v_gmlp_ln_g, v_gmlp_ln_b, v_gmlp_w_s, v_gmlp_b_s, v_hgrn_lb, v_hgrn_onorm_g, v_fox_b_f, v_final_norm_g]
    outs = _adamw_small(_pack_small(small_w), _pack_small(small_m), _pack_small(small_v), rsmall)
    sg, sd, sm, sv_ = [_unpack_small(o) for o in outs]

    def order(big_in, big_out, small):
        return [small[0], big_in, big_out] + small[1:]

    return (loss, grad_x, *order(g_w_in, g_w_out, sg), *order(d_w_in, d_w_out, sd), *order(nm_w_in, nm_w_out, sm),
            *order(nv_w_in, nv_w_out, sv_))
```

```python
import functools
import math

import jax
import jax.numpy as jnp
from jax import lax
from jax.experimental import pallas as pl
from jax.experimental.pallas import tpu as pltpu

F32 = jnp.float32
BF16 = jnp.bfloat16
SDS = jax.ShapeDtypeStruct
MESH_ID = pl.DeviceIdType.MESH

D_MODEL = 1024
DEPTH = 2
A_WIDTH = 256
A_GROUPS = 4
B_WIDTH = 256
C_WIDTH = 512
C_HEADS = 8
D_IN = 3848
D_IN_PAD = 4096
CHUNK = 128
SUB = 16
SUB_SHIFT = 4
NORM_EPS = 1e-6
F_FLOOR = 1e-30
COL_AU, COL_AV, COL_AZ = 0, 256, 512
COL_BQ, COL_BF, COL_BI, COL_BZ = 768, 1024, 1280, 1536
COL_CQ, COL_CK, COL_CV, COL_CZ, COL_CF = 1792, 2304, 2816, 3328, 3840
HEAD_LANES = 128
Q_SCALE = 0.125
ADAM_LR, ADAM_B1, ADAM_B2, ADAM_EPS, ADAM_WD, ADAM_STEP = 0.001, 0.9, 0.999, 1e-08, 0.01, 10
ADAM_C1 = 1.0 - ADAM_B1 ** ADAM_STEP
ADAM_C2 = 1.0 - ADAM_B2 ** ADAM_STEP
VMEM_LIMIT = 56 * 1024 * 1024
ADAMW_BLOCK_BYTES = 1 << 20
N_CHIPS = 4
N_DEV = 8

SMALL_PARAMS = (
    ("norm_g", (DEPTH, D_MODEL)), ("gmlp_ln_g", (DEPTH, 4, 64)), ("gmlp_ln_b", (DEPTH, 4, 64)),
    ("gmlp_w_s", (DEPTH, 4, 128, 128)), ("gmlp_b_s", (DEPTH, 4, 128)), ("hgrn_lb", (DEPTH, 256)),
    ("hgrn_onorm_g", (DEPTH, 64)), ("fox_b_f", (DEPTH, 8)), ("final_norm_g", (D_MODEL,)),
)


def _tile(n, pref):
    t = min(n, pref)
    assert n % t == 0, (n, pref)
    return t


def _params(*sem):
    return pltpu.CompilerParams(dimension_semantics=sem, vmem_limit_bytes=VMEM_LIMIT)


def _dot(a, b):
    return jnp.dot(a, b, preferred_element_type=F32)


def _dot_nt(a, b):
    return lax.dot_general(a, b, (((1,), (1,)), ((), ())), preferred_element_type=F32)


def _dot_tn(a, b):
    return lax.dot_general(a, b, (((0,), (0,)), ((), ())), preferred_element_type=F32)


def _split3(x):
    hi = x.astype(BF16)
    r = x - hi.astype(F32)
    mid = r.astype(BF16)
    lo = (r - mid.astype(F32)).astype(BF16)
    return hi, mid, lo


def _dot3_left(c, x):
    hi, mid, lo = _split3(x)
    return _dot(c, hi) + _dot(c, mid) + _dot(c, lo)


def _sigmoid(x):
    return jax.nn.sigmoid(x)


def _silu_and_grad(x):
    s = _sigmoid(x)
    return x * s, s * (1.0 + x * (1.0 - s))


_GELU_C = math.sqrt(2.0 / math.pi)


def _gelu_and_grad(x):
    inner = _GELU_C * (x + 0.044715 * x * x * x)
    t = jnp.tanh(inner)
    y = 0.5 * x * (1.0 + t)
    dy = 0.5 * (1.0 + t) + 0.5 * x * (1.0 - t * t) * _GELU_C * (1.0 + 3.0 * 0.044715 * x * x)
    return y, dy


def _lane(shape):
    return lax.broadcasted_iota(jnp.int32, shape, 1)


def _row(shape):
    return lax.broadcasted_iota(jnp.int32, shape, 0)


def _gsum64(x):
    lo = _lane(x.shape) < 64
    s0 = jnp.sum(jnp.where(lo, x, 0.0), axis=-1, keepdims=True)
    s1 = jnp.sum(jnp.where(lo, 0.0, x), axis=-1, keepdims=True)
    return jnp.where(lo, s0, s1)


def _colreduce(x, op):
    parts = [x[r:r + 8, :] for r in range(0, x.shape[0], 8)]
    while len(parts) > 1:
        pairs = [op(parts[k], parts[k + 1]) for k in range(0, len(parts) - 1, 2)]
        parts = pairs + ([parts[-1]] if len(parts) % 2 else [])
    red = jnp.max if op is jnp.maximum else jnp.sum
    return red(parts[0], axis=0, keepdims=True)


def _block_diag64(dtype=BF16):
    r, c = _row((128, 128)), _lane((128, 128))
    return jnp.where((r >> 6) == (c >> 6), 1.0, 0.0).astype(dtype)


def _inproj(x, g, w, tag):
    T, D = x.shape
    DP = w.shape[1]
    tm = _tile(T, 512)

    def body(x_ref, g_ref, w_ref, h_ref, p_ref):
        xv = x_ref[...]
        r = lax.rsqrt(jnp.mean(xv * xv, axis=-1, keepdims=True) + NORM_EPS)
        h = (xv * r * g_ref[...]).astype(BF16)
        h_ref[...] = h
        p_ref[...] = _dot(h, w_ref[...])

    return pl.pallas_call(
        body, name=f"inproj_{tag}", grid=(T // tm,),
        in_specs=[pl.BlockSpec((tm, D), lambda i: (i, 0)), pl.BlockSpec((1, D), lambda i: (0, 0)),
                  pl.BlockSpec((D, DP), lambda i: (0, 0))],
        out_specs=[pl.BlockSpec((tm, D), lambda i: (i, 0)), pl.BlockSpec((tm, DP), lambda i: (i, 0))],
        out_shape=[SDS((T, D), BF16), SDS((T, DP), F32)],
        compiler_params=_params("parallel"),
    )(x, g, w)


def _outproj(x, ya, yb, yc, wo, tag):
    T, D = x.shape
    tm = _tile(T, 512)

    def body(x_ref, ya_ref, yb_ref, yc_ref, wo_ref, o_ref):
        acc = x_ref[...] + _dot(ya_ref[...], wo_ref[0:A_WIDTH, :])
        acc = acc + _dot(yb_ref[...], wo_ref[A_WIDTH:A_WIDTH + B_WIDTH, :])
        o_ref[...] = acc + _dot(yc_ref[...], wo_ref[A_WIDTH + B_WIDTH:, :])

    row = lambda w: pl.BlockSpec((tm, w), lambda i: (i, 0))
    return pl.pallas_call(
        body, name=f"outproj_{tag}", grid=(T // tm,),
        in_specs=[row(D), row(A_WIDTH), row(B_WIDTH), row(C_WIDTH), pl.BlockSpec(wo.shape, lambda i: (0, 0))],
        out_specs=row(D), out_shape=SDS((T, D), F32), compiler_params=_params("parallel"),
    )(x, ya, yb, yc, wo)


def _outproj_bwd(dx, ya, yb, yc, wo, tag):
    T, D = dx.shape
    DM = wo.shape[0]
    tm = _tile(T, 512)

    def body(dx_ref, ya_ref, yb_ref, yc_ref, wo_ref, dy_ref, dwo_ref):
        @pl.when(pl.program_id(0) == 0)
        def _():
            dwo_ref[...] = jnp.zeros_like(dwo_ref)

        dxb = dx_ref[...].astype(BF16)
        dy_ref[...] = _dot_nt(dxb, wo_ref[...])
        dwo_ref[0:A_WIDTH, :] += _dot_tn(ya_ref[...], dxb)
        dwo_ref[A_WIDTH:A_WIDTH + B_WIDTH, :] += _dot_tn(yb_ref[...], dxb)
        dwo_ref[A_WIDTH + B_WIDTH:, :] += _dot_tn(yc_ref[...], dxb)

    row = lambda w: pl.BlockSpec((tm, w), lambda i: (i, 0))
    return pl.pallas_call(
        body, name=f"outproj_bwd_{tag}", grid=(T // tm,),
        in_specs=[row(D), row(A_WIDTH), row(B_WIDTH), row(C_WIDTH), pl.BlockSpec(wo.shape, lambda i: (0, 0))],
        out_specs=[row(DM), pl.BlockSpec((DM, D), lambda i: (0, 0))],
        out_shape=[SDS((T, DM), F32), SDS((DM, D), F32)], compiler_params=_params("arbitrary"),
    )(dx, ya, yb, yc, wo)


def _dw_in(h, dproj, tag, ride=None):
    T, D = h.shape
    DP = dproj.shape[1]
    tm, tn = _tile(T, 1024), _tile(DP, 1024)
    grid = (DP // tn, T // tm)

    def body(h_ref, dp_ref, *rest):
        ride_srcs, (dw_ref,), ride_dsts, _, ride_sems = _ride_refs(ride, rest, 1, 0)
        _ride_start(ride, grid, ride_srcs, ride_dsts, ride_sems)

        @pl.when(pl.program_id(1) == 0)
        def _():
            dw_ref[...] = jnp.zeros_like(dw_ref)

        dw_ref[...] += _dot_tn(dp_ref[...], h_ref[...])
        _ride_wait(ride, grid, ride_srcs, ride_dsts, ride_sems)

    extra = ride or _ChipExchange("gather", ())
    out = pl.pallas_call(
        body, name=f"dw_in_{tag}", grid=grid,
        in_specs=[pl.BlockSpec((tm, D), lambda j, i: (i, 0)), pl.BlockSpec((tm, tn), lambda j, i: (i, j))] + extra.in_specs,
        out_specs=[pl.BlockSpec((tn, D), lambda j, i: (j, 0))] + extra.out_specs,
        out_shape=[SDS((DP, D), F32)] + extra.out_shape, scratch_shapes=extra.scratch if ride else [],
        compiler_params=pltpu.CompilerParams(dimension_semantics=("arbitrary", "arbitrary"), vmem_limit_bytes=VMEM_LIMIT,
                                             has_side_effects=bool(ride)),
    )(h, dproj, *extra.sources)
    return out if ride else out[0]


def _dx_in(x, g, dres, dproj, w, tag, ride=None):
    T, D = x.shape
    DP = w.shape[1]
    tm = _tile(T, 512)
    grid = (T // tm,)

    def body(x_ref, g_ref, dres_ref, dp_ref, w_ref, *rest):
        ride_srcs, (dx_ref, dg_ref), ride_dsts, _, ride_sems = _ride_refs(ride, rest, 2, 0)
        _ride_start(ride, grid, ride_srcs, ride_dsts, ride_sems)

        @pl.when(pl.program_id(0) == 0)
        def _():
            dg_ref[...] = jnp.zeros_like(dg_ref)

        dh = _dot_nt(dp_ref[...], w_ref[...])
        xv = x_ref[...]
        r = lax.rsqrt(jnp.mean(xv * xv, axis=-1, keepdims=True) + NORM_EPS)
        xh = xv * r
        dg_ref[...] += jnp.sum(dh * xh, axis=0, keepdims=True)
        dxh = dh * g_ref[...]
        dx_ref[...] = dres_ref[...] + r * (dxh - xh * jnp.mean(dxh * xh, axis=-1, keepdims=True))
        _ride_wait(ride, grid, ride_srcs, ride_dsts, ride_sems)

    extra = ride or _ChipExchange("gather", ())
    row = pl.BlockSpec((tm, D), lambda i: (i, 0))
    return pl.pallas_call(
        body, name=f"dx_in_{tag}", grid=grid,
        in_specs=[row, pl.BlockSpec((1, D), lambda i: (0, 0)), row, pl.BlockSpec((tm, DP), lambda i: (i, 0)),
                  pl.BlockSpec((D, DP), lambda i: (0, 0))] + extra.in_specs,
        out_specs=[row, pl.BlockSpec((1, D), lambda i: (0, 0))] + extra.out_specs,
        out_shape=[SDS((T, D), F32), SDS((1, D), F32)] + extra.out_shape,
        scratch_shapes=extra.scratch if ride else [],
        compiler_params=pltpu.CompilerParams(dimension_semantics=("arbitrary",), vmem_limit_bytes=VMEM_LIMIT,
                                             has_side_effects=bool(ride)),
    )(x, g, dres, dproj, w, *extra.sources)


def _loss_head(x, g, tgt):
    T, D = x.shape
    tm = _tile(T, 512)

    def body(x_ref, g_ref, t_ref, dx_ref, loss_ref, dg_ref):
        @pl.when(pl.program_id(0) == 0)
        def _():
            loss_ref[...] = jnp.zeros_like(loss_ref)
            dg_ref[...] = jnp.zeros_like(dg_ref)

        xv = x_ref[...]
        r = lax.rsqrt(jnp.mean(xv * xv, axis=-1, keepdims=True) + NORM_EPS)
        xh = xv * r
        gv = g_ref[...]
        err = xh * gv - t_ref[...]
        tok = jnp.mean(err * err, axis=-1, keepdims=True)
        loss_ref[...] += 0.5 * jnp.sum(tok, axis=0, keepdims=True)
        dy = err * (1.0 / D)
        dg_ref[...] += jnp.sum(dy * xh, axis=0, keepdims=True)
        dxh = dy * gv
        dx_ref[...] = r * (dxh - xh * jnp.mean(dxh * xh, axis=-1, keepdims=True))

    row = pl.BlockSpec((tm, D), lambda i: (i, 0))
    return pl.pallas_call(
        body, name="loss_head", grid=(T // tm,),
        in_specs=[row, pl.BlockSpec((1, D), lambda i: (0, 0)), row],
        out_specs=[row, pl.BlockSpec((1, 128), lambda i: (0, 0)), pl.BlockSpec((1, D), lambda i: (0, 0))],
        out_shape=[SDS((T, D), F32), SDS((1, 128), F32), SDS((1, D), F32)], compiler_params=_params("arbitrary"),
    )(x, g, tgt)


def _gmlp_core(u, v, lng, lnb, wm_ref, bst_ref, pair):
    ug, dug = _gelu_and_grad(u)
    vg, dvg = _gelu_and_grad(v)
    mu = _gsum64(vg) * (1.0 / 64)
    d = vg - mu
    var = _gsum64(d * d) * (1.0 / 64)
    rstd = lax.rsqrt(var + NORM_EPS)
    xh = d * rstd
    vn = xh * lng + lnb
    vnb = vn.astype(BF16)
    lo = _lane(u.shape) < 64
    g0, g1 = 2 * pair, 2 * pair + 1
    mixed = jnp.where(lo, _dot(wm_ref[g0], vnb) + bst_ref[:, g0:g0 + 1], _dot(wm_ref[g1], vnb) + bst_ref[:, g1:g1 + 1])
    return ug, dug, dvg, rstd, xh, vnb, mixed, lo


def _gmlp_fwd(proj, lng, lnb, wm, bst, tag):
    T = proj.shape[0]

    def body(u_ref, v_ref, z_ref, lng_ref, lnb_ref, wm_ref, bst_ref, y_ref):
        for pair in range(2):
            sl = slice(128 * pair, 128 * pair + 128)
            ug, _, _, _, _, _, mixed, _ = _gmlp_core(u_ref[:, sl], v_ref[:, sl], lng_ref[:, sl], lnb_ref[:, sl],
                                                     wm_ref, bst_ref, pair)
            sz, _ = _silu_and_grad(z_ref[:, sl])
            y_ref[:, sl] = (ug * mixed * sz).astype(BF16)

    col = lambda c: pl.BlockSpec((CHUNK, A_WIDTH), lambda i, c=c: (i, c // A_WIDTH))
    full = lambda a: pl.BlockSpec(a.shape, lambda i, n=a.ndim: (0,) * n)
    return pl.pallas_call(
        body, name=f"gmlp_fwd_{tag}", grid=(T // CHUNK,),
        in_specs=[col(COL_AU), col(COL_AV), col(COL_AZ), full(lng), full(lnb), full(wm), full(bst)],
        out_specs=pl.BlockSpec((CHUNK, A_WIDTH), lambda i: (i, 0)), out_shape=SDS((T, A_WIDTH), BF16),
        compiler_params=_params("parallel"),
    )(proj, proj, proj, lng, lnb, wm, bst)


def _gmlp_bwd(proj, dy, lng, lnb, wm, wmt, bst, tag):
    T = proj.shape[0]
    n = T // CHUNK

    def body(u_ref, v_ref, z_ref, dy_ref, lng_ref, lnb_ref, wm_ref, wmt_ref, bst_ref,
             da_ref, dwm_ref, dbst_ref, dlng_ref, dlnb_ref):
        @pl.when(pl.program_id(0) == 0)
        def _():
            dwm_ref[...] = jnp.zeros_like(dwm_ref)
            dbst_ref[...] = jnp.zeros_like(dbst_ref)
            dlng_ref[...] = jnp.zeros_like(dlng_ref)
            dlnb_ref[...] = jnp.zeros_like(dlnb_ref)

        lane = _lane((CHUNK, 128))
        dbst = dbst_ref[...]
        for pair in range(2):
            sl = slice(128 * pair, 128 * pair + 128)
            lng_p = lng_ref[:, sl]
            ug, dug, dvg, rstd, xh, vnb, mixed, lo = _gmlp_core(u_ref[:, sl], v_ref[:, sl], lng_p, lnb_ref[:, sl],
                                                                wm_ref, bst_ref, pair)
            sz, dsz = _silu_and_grad(z_ref[:, sl])
            dyv = dy_ref[:, sl]
            out = ug * mixed
            dz = dyv * out * dsz
            dout = dyv * sz
            du = dout * mixed * dug
            dmix = dout * ug
            g0, g1 = 2 * pair, 2 * pair + 1
            dm0 = jnp.where(lo, dmix, 0.0)
            dm1 = jnp.where(lo, 0.0, dmix)
            dbst = dbst + jnp.where(lane == g0, jnp.sum(dm0, axis=-1, keepdims=True), 0.0)
            dbst = dbst + jnp.where(lane == g1, jnp.sum(dm1, axis=-1, keepdims=True), 0.0)
            dwm_ref[g0] += _dot_nt(dm0.astype(BF16), vnb)
            dwm_ref[g1] += _dot_nt(dm1.astype(BF16), vnb)
            dmb = dmix.astype(BF16)
            dvn = jnp.where(lo, _dot(wmt_ref[g0], dmb), _dot(wmt_ref[g1], dmb))
            dlng_ref[:, sl] += jnp.sum(dvn * xh, axis=0, keepdims=True)
            dlnb_ref[:, sl] += jnp.sum(dvn, axis=0, keepdims=True)
            dxh = dvn * lng_p
            m1 = _gsum64(dxh) * (1.0 / 64)
            m2 = _gsum64(dxh * xh) * (1.0 / 64)
            dv = rstd * (dxh - m1 - xh * m2) * dvg
            da_ref[:, COL_AU + 128 * pair:COL_AU + 128 * pair + 128] = du.astype(BF16)
            da_ref[:, COL_AV + 128 * pair:COL_AV + 128 * pair + 128] = dv.astype(BF16)
            da_ref[:, COL_AZ + 128 * pair:COL_AZ + 128 * pair + 128] = dz.astype(BF16)
        dbst_ref[...] = dbst

        @pl.when(pl.program_id(0) == n - 1)
        def _():
            causal = _lane((CHUNK, CHUNK)) <= _row((CHUNK, CHUNK))
            for g in range(A_GROUPS):
                dwm_ref[g] = jnp.where(causal, dwm_ref[g], 0.0)

    col = lambda c: pl.BlockSpec((CHUNK, A_WIDTH), lambda i, c=c: (i, c // A_WIDTH))
    full = lambda a: pl.BlockSpec(a.shape, lambda i, n=a.ndim: (0,) * n)
    acc = lambda s: pl.BlockSpec(s, lambda i, n=len(s): (0,) * n)
    return pl.pallas_call(
        body, name=f"gmlp_bwd_{tag}", grid=(n,),
        in_specs=[col(COL_AU), col(COL_AV), col(COL_AZ), pl.BlockSpec((CHUNK, A_WIDTH), lambda i: (i, 0)),
                  full(lng), full(lnb), full(wm), full(wmt), full(bst)],
        out_specs=[pl.BlockSpec((CHUNK, 3 * A_WIDTH), lambda i: (i, 0)), acc((A_GROUPS, CHUNK, CHUNK)),
                   acc((CHUNK, 128)), acc((1, A_WIDTH)), acc((1, A_WIDTH))],
        out_shape=[SDS((T, 3 * A_WIDTH), BF16), SDS((A_GROUPS, CHUNK, CHUNK), F32), SDS((CHUNK, 128), F32),
                   SDS((1, A_WIDTH), F32), SDS((1, A_WIDTH), F32)],
        compiler_params=_params("arbitrary"),
    )(proj, proj, proj, dy, lng, lnb, wm, wmt, bst)


def _hgrn_consts():
    r, c = _row((CHUNK, CHUNK)), _lane((CHUNK, CHUNK))
    same = (r >> SUB_SHIFT) == (c >> SUB_SHIFT)
    lsub = jnp.where(same & (c <= r), 1.0, 0.0).astype(BF16)
    usub = jnp.where(same & (c >= r), 1.0, 0.0).astype(BF16)
    bsub = jnp.where(same, 1.0, 0.0).astype(BF16)
    return lsub, usub, bsub


def _hgrn_gates(qv, zf, lbp):
    sq, dsq = _silu_and_grad(qv)
    qt = sq * Q_SCALE
    sg = _sigmoid(zf)
    sgn = _sigmoid(-zf)
    f = lbp + (1.0 - lbp) * sg
    g = jnp.log(jnp.maximum(f, F_FLOOR))
    kf = (1.0 - lbp) * sgn
    return qt, dsq, sg, sgn, f, g, kf


def _hgrn_intra_fwd(qt, kf, b, v, mbd):
    rid = _row((SUB, 128))
    parts = []
    for s in range(SUB):
        e = jnp.exp(b - b[s:s + 1, :])
        parts.append(jnp.where(rid >= s, qt * kf[s:s + 1, :] * e, 0.0))
    a = _dot(jnp.concatenate(parts, axis=0).astype(BF16), mbd)
    o = jnp.zeros((SUB, 128), F32)
    for s in range(SUB):
        o = o + a[SUB * s:SUB * s + SUB, :] * v[s:s + 1, :]
    return o


def _hgrn_intra_bwd(qt, kf, b, v, do, mbd, rsum):
    rid = _row((SUB, 128))
    ps, das, kes, es = [], [], [], []
    for s in range(SUB):
        e = jnp.where(rid >= s, jnp.exp(b - b[s:s + 1, :]), 0.0)
        ke = kf[s:s + 1, :] * e
        es.append(e)
        kes.append(ke)
        ps.append(qt * ke)
        das.append(do * v[s:s + 1, :])
    a = _dot(jnp.concatenate(ps, axis=0).astype(BF16), mbd)
    da = _dot(jnp.concatenate(das, axis=0).astype(BF16), mbd)
    dqt = jnp.zeros((SUB, 128), F32)
    xs, ys = [], []
    for s in range(SUB):
        da_s = da[SUB * s:SUB * s + SUB, :]
        dqt = dqt + da_s * kes[s]
        xs.append(a[SUB * s:SUB * s + SUB, :] * do)
        ys.append(da_s * qt * es[s])
    dv = _dot(rsum, jnp.concatenate(xs, axis=0).astype(BF16))
    dkf = _dot(rsum, jnp.concatenate(ys, axis=0).astype(BF16))
    return dqt, dkf, dv


def _hgrn_norm_gate(o, z, onorm):
    ms = _gsum64(o * o) * (1.0 / 64)
    r = lax.rsqrt(ms + NORM_EPS)
    xh = o * r
    sz, dsz = _silu_and_grad(z)
    return xh, r, sz, dsz, xh * onorm


def _hgrn_fwd(proj, lb, onorm, tag):
    T = proj.shape[0]
    n = T // CHUNK
    nsub = CHUNK // SUB

    def body(q_ref, f_ref, i_ref, z_ref, lb_ref, on_ref, y_ref, o_ref, s0_ref, st_ref):
        @pl.when(pl.program_id(0) == 0)
        def _():
            st_ref[...] = jnp.zeros_like(st_ref)

        lsub, _, bsub = _hgrn_consts()
        mbd = _block_diag64()
        bdmask = mbd > 0
        rid = _row((CHUNK, 128))
        for pair in range(2):
            sl = slice(128 * pair, 128 * pair + 128)
            qt, _, _, _, _, g, kf = _hgrn_gates(q_ref[:, sl], f_ref[:, sl], lb_ref[:, sl])
            v = i_ref[:, sl]
            b = _dot3_left(lsub, g)
            bl = _dot3_left(bsub, g)
            qh = (qt * jnp.exp(b)).astype(BF16)
            kh = kf * jnp.exp(bl - b)
            dec = jnp.exp(bl)
            vtb = v.T.astype(BF16)
            st = st_ref[pair]
            s0_ref[0, pair] = st
            outs = []
            for sub in range(nsub):
                rs = slice(SUB * sub, SUB * sub + SUB)
                o_inter = _dot_nt(qh[rs], st.astype(BF16))
                outs.append(o_inter + _hgrn_intra_fwd(qt[rs], kf[rs], b[rs], v[rs], mbd))
                khm = jnp.where((rid >> SUB_SHIFT) == sub, kh, 0.0).astype(BF16)
                st = jnp.where(bdmask, st * dec[SUB * sub:SUB * sub + 1, :] + _dot(vtb, khm), 0.0)
            st_ref[pair] = st
            o = jnp.concatenate(outs, axis=0)
            o_ref[:, sl] = o
            _, _, sz, _, on = _hgrn_norm_gate(o, z_ref[:, sl], on_ref[:, sl])
            y_ref[:, sl] = (on * sz).astype(BF16)

    col = lambda c: pl.BlockSpec((CHUNK, B_WIDTH), lambda i, c=c: (i, c // B_WIDTH))
    full = lambda a: pl.BlockSpec(a.shape, lambda i, n=a.ndim: (0,) * n)
    return pl.pallas_call(
        body, name=f"hgrn_fwd_{tag}", grid=(n,),
        in_specs=[col(COL_BQ), col(COL_BF), col(COL_BI), col(COL_BZ), full(lb), full(onorm)],
        out_specs=[pl.BlockSpec((CHUNK, B_WIDTH), lambda i: (i, 0)), pl.BlockSpec((CHUNK, B_WIDTH), lambda i: (i, 0)),
                   pl.BlockSpec((1, 2, 128, 128), lambda i: (i, 0, 0, 0))],
        out_shape=[SDS((T, B_WIDTH), BF16), SDS((T, B_WIDTH), F32), SDS((n, 2, 128, 128), F32)],
        scratch_shapes=[pltpu.VMEM((2, 128, 128), F32)], compiler_params=_params("arbitrary"),
    )(proj, proj, proj, proj, lb, onorm)


def _hgrn_bwd(proj, dy, o_saved, s0, lb, onorm, tag):
    T = proj.shape[0]
    n = T // CHUNK
    nsub = CHUNK // SUB

    def body(q_ref, f_ref, i_ref, z_ref, dy_ref, o_ref, s0_ref, lb_ref, on_ref,
             db_ref, dlb_ref, don_ref, dst_ref, sts_ref):
        @pl.when(pl.program_id(0) == 0)
        def _():
            dst_ref[...] = jnp.zeros_like(dst_ref)
            dlb_ref[...] = jnp.zeros_like(dlb_ref)
            don_ref[...] = jnp.zeros_like(don_ref)

        lsub, usub, bsub = _hgrn_consts()
        mbd = _block_diag64()
        bdmask = mbd > 0
        rsum = jnp.where((_lane((SUB, SUB * SUB)) >> SUB_SHIFT) == _row((SUB, SUB * SUB)), 1.0, 0.0).astype(BF16)
        for pair in range(2):
            sl = slice(128 * pair, 128 * pair + 128)
            lbp = lb_ref[:, sl]
            qv, zf = q_ref[:, sl], f_ref[:, sl]
            qt, dsq, sg, sgn, f, g, kf = _hgrn_gates(qv, zf, lbp)
            v = i_ref[:, sl]
            b = _dot3_left(lsub, g)
            bl = _dot3_left(bsub, g)
            eb = jnp.exp(b)
            ekb = jnp.exp(bl - b)
            qhb = (qt * eb).astype(BF16)
            khb = (kf * ekb).astype(BF16)
            dec = jnp.exp(bl)
            vb = v.astype(BF16)
            onp = on_ref[:, sl]
            ov = o_ref[:, sl]
            xh, r, sz, dsz, on = _hgrn_norm_gate(ov, z_ref[:, sl], onp)
            dyv = dy_ref[:, sl]
            dz = dyv * on * dsz
            don = dyv * sz
            cn = jnp.sum(don * xh, axis=0, keepdims=True)
            don_ref[...] += cn + pltpu.roll(cn, 64, axis=1)
            dxo = don * onp
            do = r * (dxo - xh * (_gsum64(dxo * xh) * (1.0 / 64)))
            dob = do.astype(BF16)
            st = s0_ref[0, pair]
            for sub in range(nsub):
                rs = slice(SUB * sub, SUB * sub + SUB)
                sts_ref[sub] = st
                st = jnp.where(bdmask, st * dec[SUB * sub:SUB * sub + 1, :] + _dot_tn(vb[rs], khb[rs]), 0.0)
            gst = dst_ref[pair]
            dqt_p, dkf_p, dv_p, dbl_p = [None] * nsub, [None] * nsub, [None] * nsub, [None] * nsub
            for sub in reversed(range(nsub)):
                rs = slice(SUB * sub, SUB * sub + SUB)
                st_in = sts_ref[sub]
                gb = gst.astype(BF16)
                dqh = _dot(dob[rs], st_in.astype(BF16))
                dkh = _dot(vb[rs], gb)
                dv_inter = _dot_nt(khb[rs], gb)
                ddec = jnp.sum(gst * st_in, axis=0, keepdims=True)
                dec_row = dec[SUB * sub:SUB * sub + 1, :]
                gst = jnp.where(bdmask, gst * dec_row + _dot_tn(dob[rs], qhb[rs]), 0.0)
                dqt_i, dkf_i, dv_i = _hgrn_intra_bwd(qt[rs], kf[rs], b[rs], v[rs], do[rs], mbd, rsum)
                dkf_inter = dkh * ekb[rs]
                dqt_p[sub] = dqh * eb[rs] + dqt_i
                dkf_p[sub] = dkf_inter + dkf_i
                dv_p[sub] = dv_inter + dv_i
                row = jnp.sum(kf[rs] * dkf_inter, axis=0, keepdims=True) + ddec * dec_row
                dbl_p[sub] = jnp.broadcast_to(row, (SUB, 128))
            dst_ref[pair] = gst
            dqt = jnp.concatenate(dqt_p, axis=0)
            dkf = jnp.concatenate(dkf_p, axis=0)
            dv = jnp.concatenate(dv_p, axis=0)
            dg = _dot3_left(usub, qt * dqt - kf * dkf) + jnp.concatenate(dbl_p, axis=0)
            df = jnp.where(f > F_FLOOR, dg / f, 0.0)
            dlb_ref[:, sl] += jnp.sum(df * (1.0 - sg) - dkf * sgn, axis=0, keepdims=True)
            dfl = (1.0 - lbp) * sg * sgn * (df - dkf)
            dq = dqt * Q_SCALE * dsq
            db_ref[:, 0 * B_WIDTH + 128 * pair:0 * B_WIDTH + 128 * pair + 128] = dq.astype(BF16)
            db_ref[:, 1 * B_WIDTH + 128 * pair:1 * B_WIDTH + 128 * pair + 128] = dfl.astype(BF16)
            db_ref[:, 2 * B_WIDTH + 128 * pair:2 * B_WIDTH + 128 * pair + 128] = dv.astype(BF16)
            db_ref[:, 3 * B_WIDTH + 128 * pair:3 * B_WIDTH + 128 * pair + 128] = dz.astype(BF16)

    rev = lambda c: pl.BlockSpec((CHUNK, B_WIDTH), lambda i, c=c: (n - 1 - i, c // B_WIDTH))
    full = lambda a: pl.BlockSpec(a.shape, lambda i, n_=a.ndim: (0,) * n_)
    acc = lambda s: pl.BlockSpec(s, lambda i, n_=len(s): (0,) * n_)
    return pl.pallas_call(
        body, name=f"hgrn_bwd_{tag}", grid=(n,),
        in_specs=[rev(COL_BQ), rev(COL_BF), rev(COL_BI), rev(COL_BZ),
                  pl.BlockSpec((CHUNK, B_WIDTH), lambda i: (n - 1 - i, 1)),
                  pl.BlockSpec((CHUNK, B_WIDTH), lambda i: (n - 1 - i, 0)),
                  pl.BlockSpec((1, 2, 128, 128), lambda i: (n - 1 - i, 0, 0, 0)), full(lb), full(onorm)],
        out_specs=[pl.BlockSpec((CHUNK, 4 * B_WIDTH), lambda i: (n - 1 - i, 0)), acc((1, B_WIDTH)), acc((1, 128))],
        out_shape=[SDS((T, 4 * B_WIDTH), BF16), SDS((1, B_WIDTH), F32), SDS((1, 128), F32)],
        scratch_shapes=[pltpu.VMEM((2, 128, 128), F32), pltpu.VMEM((nsub, 128, 128), F32)],
        compiler_params=_params("arbitrary"),
    )(proj, proj, proj, proj, dy, o_saved, s0, lb, onorm)


def _lb_fwd(hgrn_lb):
    assert hgrn_lb.shape[0] == 2

    def body(x_ref, o_ref):
        x0, x1 = x_ref[0:1, :], x_ref[1:2, :]
        m = jnp.maximum(x0, x1)
        e0, e1 = jnp.exp(x0 - m), jnp.exp(x1 - m)
        p0, p1 = e0 / (e0 + e1), e1 / (e0 + e1)
        o_ref[0:1, :] = jnp.clip(p0 - p0, 0.0, 1.0 - 1e-6)
        o_ref[1:2, :] = jnp.clip((p0 + p1) - p0, 0.0, 1.0 - 1e-6)

    return pl.pallas_call(body, name="lb_fwd", out_shape=SDS(hgrn_lb.shape, F32))(hgrn_lb)


def _lb_bwd(hgrn_lb, dlb):
    def body(x_ref, d_ref, o_ref):
        x0, x1 = x_ref[0:1, :], x_ref[1:2, :]
        m = jnp.maximum(x0, x1)
        e0, e1 = jnp.exp(x0 - m), jnp.exp(x1 - m)
        p0, p1 = e0 / (e0 + e1), e1 / (e0 + e1)
        val = (p0 + p1) - p0
        dp1 = jnp.where((val > 0.0) & (val < 1.0 - 1e-6), d_ref[1:2, :], 0.0)
        inner = p1 * dp1
        o_ref[0:1, :] = p0 * (0.0 - inner)
        o_ref[1:2, :] = p1 * (dp1 - inner)

    return pl.pallas_call(body, name="lb_bwd", out_shape=SDS(hgrn_lb.shape, F32))(hgrn_lb, dlb)


def _fox_prep(proj, bf, tag):
    T = proj.shape[0]
    n = T // CHUNK

    def body(q0_ref, q1_ref, k0_ref, k1_ref, v0_ref, v1_ref, fl_ref, bf_ref, qo_ref, ko_ref, vt_ref, carry_ref):
        for p, v_ref in enumerate((v0_ref, v0_ref, v1_ref, v1_ref)):
            vt_ref[p, 0] = v_ref[:, 128 * (p % 2):128 * (p % 2) + 128].T.astype(BF16)

        @pl.when(pl.program_id(0) == 0)
        def _():
            carry_ref[...] = jnp.zeros_like(carry_ref)

        ltri = jnp.where(_lane((CHUNK, CHUNK)) <= _row((CHUNK, CHUNK)), 1.0, 0.0).astype(BF16)
        lf = jax.nn.log_sigmoid(fl_ref[...] + bf_ref[...])
        c = _dot3_left(ltri, lf) + carry_ref[...]
        carry_ref[...] = c[CHUNK - 1:CHUNK, :]
        lane = _lane((CHUNK, 128))
        feat = lane < 64
        ones_q = (lane >= 67) & (lane <= 69)
        ones_k = (lane >= 64) & (lane <= 66)
        qrefs, krefs = (q0_ref, q1_ref), (k0_ref, k1_ref)
        for h in range(C_HEADS):
            blk = slice(128 * ((h // 2) % 2), 128 * ((h // 2) % 2) + 128)
            qp, kp = qrefs[h // 4][:, blk], krefs[h // 4][:, blk]
            if h % 2:
                qp, kp = pltpu.roll(qp, 64, axis=1), pltpu.roll(kp, 64, axis=1)
            ch = jnp.broadcast_to(c[:, h:h + 1], (CHUNK, 128))
            hi = ch.astype(BF16).astype(F32)
            r1 = ch - hi
            mid = r1.astype(BF16).astype(F32)
            lo = r1 - mid
            aq = jnp.where(lane == 64, hi, jnp.where(lane == 65, mid, jnp.where(lane == 66, lo,
                           jnp.where(ones_q, 1.0, 0.0))))
            ak = jnp.where(lane == 67, -hi, jnp.where(lane == 68, -mid, jnp.where(lane == 69, -lo,
                           jnp.where(ones_k, 1.0, 0.0))))
            qo_ref[:, 128 * h:128 * h + 128] = jnp.where(feat, qp * Q_SCALE, aq).astype(BF16)
            ko_ref[:, 128 * h:128 * h + 128] = jnp.where(feat, kp, ak).astype(BF16)

    w = 256
    col = lambda c: pl.BlockSpec((CHUNK, w), lambda i, c=c: (i, c // w))
    return pl.pallas_call(
        body, name=f"fox_prep_{tag}", grid=(n,),
        in_specs=[col(COL_CQ), col(COL_CQ + w), col(COL_CK), col(COL_CK + w), col(COL_CV), col(COL_CV + w),
                  pl.BlockSpec((CHUNK, 128), lambda i: (i, COL_CF // 128)), pl.BlockSpec((1, 128), lambda i: (0, 0))],
        out_specs=[pl.BlockSpec((CHUNK, C_HEADS * 128), lambda i: (i, 0))] * 2
        + [pl.BlockSpec((C_HEADS // 2, 1, 128, CHUNK), lambda i: (0, i, 0, 0))],
        out_shape=[SDS((T, C_HEADS * 128), BF16)] * 2 + [SDS((C_HEADS // 2, n, 128, CHUNK), BF16)],
        scratch_shapes=[pltpu.VMEM((1, 128), F32)], compiler_params=_params("arbitrary"),
    )(proj, proj, proj, proj, proj, proj, proj, bf)


FOX_TILE = 512
FOX_KEYS = 512


def _fox_mask(tk, tq, k0, q0):
    return (_row((tk, tq)) + (k0 - q0)) <= _lane((tk, tq))


def _ride_refs(ride, rest, n_out, n_scratch):
    n = ride.n if ride else 0
    srcs, rest = rest[:n], rest[n:]
    outs, rest = rest[:n_out], rest[n_out:]
    dsts, rest = rest[:n], rest[n:]
    return srcs, outs, dsts, rest[:n_scratch], rest[n_scratch:]


def _ride_start(ride, grid, srcs, dsts, sems):
    if ride:
        first = functools.reduce(lambda a, b: a & b, [pl.program_id(d) == 0 for d in range(len(grid))])
        pl.when(first)(lambda: ride.start(srcs, dsts, sems))


def _ride_wait(ride, grid, srcs, dsts, sems):
    if ride:
        last = functools.reduce(lambda a, b: a & b, [pl.program_id(d) == n - 1 for d, n in enumerate(grid)])
        pl.when(last)(lambda: ride.wait(srcs, dsts, sems))


def _fox_fwd(qt, kt, vt, proj, tag, ride=None):
    T = proj.shape[0]
    tq, tk = _tile(T, FOX_TILE), _tile(T, FOX_KEYS)
    nq, nsub = T // tq, tk // CHUNK
    npair = C_HEADS // 2

    def body(q_ref, k_ref, vt_ref, z_ref, *rest):
        ride_srcs, (o_ref, lse_ref, y_ref), ride_dsts, (acc_ref, st_ref, pt_ref), ride_sems = _ride_refs(ride, rest, 3, 3)
        i = pl.program_id(1)
        _ride_start(ride, (npair, nq), ride_srcs, ride_dsts, ride_sems)

        qs = (q_ref[:, 0:128], q_ref[:, 128:256])
        acc_ref[...] = jnp.zeros_like(acc_ref)
        pt_ref[...] = jnp.zeros_like(pt_ref)
        nfull = (i * tq) // tk

        def scores(j):
            kb = k_ref[pl.ds(pl.multiple_of(j * tk, tk), tk), :]
            return tuple(_dot_nt(kb[:, 128 * h:128 * h + 128], qs[h]) for h in range(2))

        def weigh(j, h):
            rows = slice(64 * h, 64 * h + 64)
            pv = _dot(vt_ref[0, nsub * j, rows, :], pt_ref[h, 0:CHUNK, :])
            for c in range(1, nsub):
                pv = pv + _dot(vt_ref[0, nsub * j + c, rows, :], pt_ref[h, CHUNK * c:CHUNK * c + CHUNK, :])
            return pv

        def block(j, carry, diagonal):
            nxt = () if diagonal else scores(j + 1)
            pvs = [weigh(jnp.maximum(j - 1, 0), h) for h in range(2)]
            new = []
            for h in range(2):
                m, l, alpha_prev = carry[3 * h:3 * h + 3]
                st = st_ref[h]
                if diagonal:
                    st = jnp.where(_fox_mask(tk, tq, j * tk, i * tq), st, -jnp.inf)
                m_new = jnp.maximum(m, _colreduce(st, jnp.maximum))
                pt = jnp.exp(st - m_new)
                alpha = jnp.exp(m - m_new)
                rows = slice(64 * h, 64 * h + 64)
                acc_ref[rows, :] = alpha_prev * acc_ref[rows, :] + pvs[h]
                pt_ref[h] = pt.astype(BF16)
                new += [m_new, alpha * l + _colreduce(pt, jnp.add), alpha]
            for h, st in enumerate(nxt):
                st_ref[h] = st
            return tuple(new)

        for h, st in enumerate(scores(0)):
            st_ref[h] = st
        init = (jnp.full((1, tq), -jnp.inf, F32), jnp.zeros((1, tq), F32), jnp.ones((1, tq), F32)) * 2
        carry = lax.fori_loop(0, nfull, lambda j, c: block(j, c, False), init)
        m0, l0, a0, m1, l1, a1 = block(nfull, carry, True)
        for h, alpha in enumerate((a0, a1)):
            rows = slice(64 * h, 64 * h + 64)
            acc_ref[rows, :] = alpha * acc_ref[rows, :] + weigh(nfull, h)
        inv = jnp.where(_row((128, tq)) < 64, 1.0 / l0, 1.0 / l1)
        o = (acc_ref[...] * inv).T
        o_ref[...] = o
        r8 = _row((8, tq))
        lse_ref[0, 0] = jnp.where(r8 == 0, m0 + jnp.log(l0), jnp.where(r8 == 1, m1 + jnp.log(l1), 0.0))
        sz, _ = _silu_and_grad(z_ref[...])
        y_ref[...] = (o * sz).astype(BF16)
        _ride_wait(ride, (npair, nq), ride_srcs, ride_dsts, ride_sems)

    blk = pl.BlockSpec((tq, 128), lambda p, i: (i, p))
    extra = ride or _ChipExchange("gather", ())
    return pl.pallas_call(
        body, name=f"fox_fwd_{tag}", grid=(npair, nq),
        in_specs=[pl.BlockSpec((tq, 256), lambda p, i: (i, p)), pl.BlockSpec((T, 256), lambda p, i: (0, p)),
                  pl.BlockSpec((1, T // CHUNK, 128, CHUNK), lambda p, i: (p, 0, 0, 0)),
                  pl.BlockSpec((tq, 128), lambda p, i: (i, COL_CZ // 128 + p))] + extra.in_specs,
        out_specs=[blk, pl.BlockSpec((1, 1, 8, tq), lambda p, i: (p, i, 0, 0)), blk] + extra.out_specs,
        out_shape=[SDS((T, C_WIDTH), F32), SDS((npair, nq, 8, tq), F32), SDS((T, C_WIDTH), BF16)] + extra.out_shape,
        scratch_shapes=[pltpu.VMEM((128, tq), F32), pltpu.VMEM((2, tk, tq), F32), pltpu.VMEM((2, tk, tq), BF16)]
        + (extra.scratch if ride else []),
        compiler_params=pltpu.CompilerParams(dimension_semantics=("arbitrary", "arbitrary"), vmem_limit_bytes=VMEM_LIMIT,
                                             has_side_effects=bool(ride)),
    )(qt, kt, vt, proj, *extra.sources)


def _fox_bwd_prep(proj, dy, o, tag):
    T = proj.shape[0]
    tq = _tile(T, FOX_TILE)

    def body(z0_ref, z1_ref, dy_ref, o_ref, do_ref, dl_ref, dz_ref):
        sel = jnp.where((_lane((16, 128)) >> 6) == _row((16, 128)), 1.0, 0.0).astype(BF16)
        for p, z_ref in enumerate((z0_ref, z0_ref, z1_ref, z1_ref)):
            sl = slice(128 * p, 128 * p + 128)
            sz, dsz = _silu_and_grad(z_ref[:, 128 * (p % 2):128 * (p % 2) + 128])
            dyv, ov = dy_ref[:, sl], o_ref[:, sl]
            do = dyv * sz
            do_ref[:, sl] = do.astype(BF16)
            dz_ref[:, sl] = (dyv * ov * dsz).astype(BF16)
            hi, mid, lo = _split3(do * ov)
            dl_ref[p, 0] = (_dot_nt(sel, hi) + _dot_nt(sel, mid) + _dot_nt(sel, lo))[0:8, :]

    w = 256
    blk = pl.BlockSpec((tq, C_WIDTH), lambda i: (i, 0))
    return pl.pallas_call(
        body, name=f"fox_bwd_prep_{tag}", grid=(T // tq,),
        in_specs=[pl.BlockSpec((tq, w), lambda i: (i, COL_CZ // w)), pl.BlockSpec((tq, w), lambda i: (i, COL_CZ // w + 1)),
                  pl.BlockSpec((tq, C_WIDTH), lambda i: (i, (A_WIDTH + B_WIDTH) // C_WIDTH)), blk],
        out_specs=[blk, pl.BlockSpec((C_HEADS // 2, 1, 8, tq), lambda i: (0, i, 0, 0)), blk],
        out_shape=[SDS((T, C_WIDTH), BF16), SDS((C_HEADS // 2, T // tq, 8, tq), F32), SDS((T, C_WIDTH), BF16)],
        compiler_params=_params("parallel"),
    )(proj, proj, dy, o)


def _fox_bwd(qt, kt, proj, do, lse, delta, tag, ride=None):
    T = proj.shape[0]
    tq, tk = _tile(T, FOX_TILE), _tile(T, FOX_KEYS)
    nq, nk = T // tq, T // tk
    ndiag = tk // tq
    npair = C_HEADS // 2

    def body(q_ref, k_ref, v_ref, do_ref, lse_ref, dl_ref, *rest):
        ride_srcs, (dq_ref, dk_ref, dv_ref), ride_dsts, scratch, ride_sems = _ride_refs(ride, rest, 3, 4)
        dvacc_ref, sc_ref, pt_ref, ds_ref = scratch
        j = pl.program_id(1)
        first = (j * tk) // tq
        _ride_start(ride, (npair, nk), ride_srcs, ride_dsts, ride_sems)

        @pl.when(j == 0)
        def _():
            dq_ref[...] = jnp.zeros_like(dq_ref)

        dk_ref[...] = jnp.zeros_like(dk_ref)
        dvacc_ref[...] = jnp.zeros_like(dvacc_ref)
        ks = (k_ref[:, 0:128], k_ref[:, 128:256])
        kts = tuple(k.astype(F32).T.astype(BF16) for k in ks)
        vb = v_ref[...].astype(BF16)
        lo = _lane((tq, 128)) < 64

        def operands(i):
            q0 = pl.multiple_of(i * tq, tq)
            qb = q_ref[pl.ds(q0, tq), :]
            dob = do_ref[pl.ds(q0, tq), :]
            qhs = (qb[:, 0:128], qb[:, 128:256])
            dohs = (jnp.where(lo, dob, jnp.zeros_like(dob)), jnp.where(lo, jnp.zeros_like(dob), dob))
            return qhs, dohs

        def scores(i):
            qhs, dohs = operands(i)
            return tuple(_dot_nt(ks[h], qhs[h]) for h in range(2)) + tuple(_dot_nt(vb, dohs[h]) for h in range(2))

        def park(sc):
            for a, s in enumerate(sc):
                sc_ref[a] = s

        def grads(i):
            qhs, dohs = operands(i)
            dvacc_ref[...] += _dot(jnp.concatenate([pt_ref[0], pt_ref[1]], axis=1), jnp.concatenate(dohs, axis=0))
            for h in range(2):
                dk_ref[:, 128 * h:128 * h + 128] += _dot(ds_ref[h], qhs[h])
                dq_ref[h, i] += _dot(kts[h], ds_ref[h])

        def block(i, diagonal, opening):
            nxt = scores(jnp.minimum(i + 1, nq - 1))
            if not opening:
                grads(i - 1)
            lsev = lse_ref[0, i]
            dlv = dl_ref[0, i]
            for h in range(2):
                pt = jnp.exp(sc_ref[h] - lsev[h:h + 1, :])
                if diagonal:
                    pt = jnp.where(_fox_mask(tk, tq, j * tk, i * tq), pt, 0.0)
                ds_ref[h] = (pt * (sc_ref[2 + h] - dlv[h:h + 1, :])).astype(BF16)
                pt_ref[h] = pt.astype(BF16)
            park(nxt)

        park(scores(first))
        for d in range(ndiag):
            block(first + d, True, d == 0)

        def step(i, carry):
            block(i, False, False)
            return carry

        lax.fori_loop(first + ndiag, nq, step, 0)
        grads(nq - 1)
        dv_ref[...] = dvacc_ref[...].astype(BF16)
        _ride_wait(ride, (npair, nk), ride_srcs, ride_dsts, ride_sems)

    full = lambda w: pl.BlockSpec((T, w), lambda p, j: (0, p))
    stat = pl.BlockSpec((1, nq, 8, tq), lambda p, j: (p, 0, 0, 0))
    extra = ride or _ChipExchange("gather", ())
    return pl.pallas_call(
        body, name=f"fox_bwd_{tag}", grid=(npair, nk),
        in_specs=[full(256), pl.BlockSpec((tk, 256), lambda p, j: (j, p)),
                  pl.BlockSpec((tk, 128), lambda p, j: (j, COL_CV // 128 + p)), full(128), stat, stat] + extra.in_specs,
        out_specs=[pl.BlockSpec((2, nq, 128, tq), lambda p, j: (p, 0, 0, 0)), pl.BlockSpec((tk, 256), lambda p, j: (j, p)),
                   pl.BlockSpec((tk, 128), lambda p, j: (j, p))] + extra.out_specs,
        out_shape=[SDS((C_HEADS, nq, 128, tq), F32), SDS((T, C_HEADS * 128), F32), SDS((T, C_WIDTH), BF16)]
        + extra.out_shape,
        scratch_shapes=[pltpu.VMEM((tk, 128), F32), pltpu.VMEM((4, tk, tq), F32), pltpu.VMEM((2, tk, tq), BF16),
                        pltpu.VMEM((2, tk, tq), BF16)] + (extra.scratch if ride else []),
        compiler_params=pltpu.CompilerParams(dimension_semantics=("arbitrary", "arbitrary"), vmem_limit_bytes=VMEM_LIMIT,
                                             has_side_effects=bool(ride)),
    )(qt, kt, proj, do, lse, delta, *extra.sources)


def _fox_bwd_post(dqt, dkt, proj, bf, tag):
    T = proj.shape[0]
    tq = _tile(T, FOX_TILE)
    n = T // tq

    def body(dq_ref, dk_ref, fl_ref, bf_ref, oq_ref, ok_ref, ofl_ref, dbf_ref, carry_ref):
        @pl.when(pl.program_id(0) == 0)
        def _():
            carry_ref[...] = jnp.zeros_like(carry_ref)
            dbf_ref[...] = jnp.zeros_like(dbf_ref)

        lane = _lane((tq, 128))
        lo = lane < 64
        dqs = [dq_ref[h, 0].T for h in range(C_HEADS)]
        dc = jnp.zeros((tq, 128), F32)
        for h in range(C_HEADS):
            dc = dc + jnp.where(lane == h, dqs[h][:, 64:65] - dk_ref[:, 128 * h + 67:128 * h + 68], 0.0)
        utri = jnp.where(_lane((tq, tq)) >= _row((tq, tq)), 1.0, 0.0).astype(BF16)
        dlf = _dot3_left(utri, dc) + carry_ref[...]
        carry_ref[...] = dlf[0:1, :]
        dfl = jnp.where(lane < C_HEADS, dlf * _sigmoid(-(fl_ref[...] + bf_ref[...])), 0.0)
        ofl_ref[...] = dfl.astype(BF16)
        dbf_ref[...] += jnp.sum(dfl, axis=0, keepdims=True)
        for p in range(C_HEADS // 2):
            a, b = 128 * (2 * p), 128 * (2 * p + 1)
            oq_ref[:, 128 * p:128 * p + 128] = (
                jnp.where(lo, dqs[2 * p], pltpu.roll(dqs[2 * p + 1], 64, axis=1)) * Q_SCALE).astype(BF16)
            ok_ref[:, 128 * p:128 * p + 128] = jnp.where(
                lo, dk_ref[:, a:a + 128], pltpu.roll(dk_ref[:, b:b + 128], 64, axis=1)).astype(BF16)

    rev = lambda w: pl.BlockSpec((tq, w), lambda i: (n - 1 - i, 0))
    return pl.pallas_call(
        body, name=f"fox_bwd_post_{tag}", grid=(n,),
        in_specs=[pl.BlockSpec((C_HEADS, 1, 128, tq), lambda i: (0, n - 1 - i, 0, 0)), rev(C_HEADS * 128),
                  pl.BlockSpec((tq, 128), lambda i: (n - 1 - i, COL_CF // 128)), pl.BlockSpec((1, 128), lambda i: (0, 0))],
        out_specs=[rev(C_WIDTH), rev(C_WIDTH), rev(128), pl.BlockSpec((1, 128), lambda i: (0, 0))],
        out_shape=[SDS((T, C_WIDTH), BF16), SDS((T, C_WIDTH), BF16), SDS((T, 128), BF16), SDS((1, 128), F32)],
        scratch_shapes=[pltpu.VMEM((1, 128), F32)], compiler_params=_params("arbitrary"),
    )(dqt, dkt, proj, bf)


def _adamw_math(w, g, m, v):
    m = ADAM_B1 * m + (1.0 - ADAM_B1) * g
    v = ADAM_B2 * v + (1.0 - ADAM_B2) * (g * g)
    delta = -ADAM_LR * ((m / ADAM_C1) / (jnp.sqrt(v / ADAM_C2) + ADAM_EPS) + ADAM_WD * w)
    return delta, m, v


def _adamw_pair(w, m, v, ga, gb, name):
    n0 = w.shape[0]
    most = max(1, ADAMW_BLOCK_BYTES // (4 * math.prod(w.shape[1:])))
    t0 = max(t for t in range(1, min(n0, most) + 1) if n0 % t == 0)

    def body(w_ref, m_ref, v_ref, ga_ref, gb_ref, g_ref, d_ref, nm_ref, nv_ref):
        g = ga_ref[...] + gb_ref[...]
        g_ref[...] = g
        d_ref[...], nm_ref[...], nv_ref[...] = _adamw_math(w_ref[...], g, m_ref[...], v_ref[...])

    blk = pl.BlockSpec((t0,) + w.shape[1:], lambda i: (i, 0, 0))
    return pl.pallas_call(
        body, name=name, grid=(n0 // t0,), in_specs=[blk] * 5, out_specs=[blk] * 4,
        out_shape=[SDS(w.shape, F32)] * 4, compiler_params=_params("parallel"),
    )(w, m, v, ga, gb)


def _adamw_small(w, m, v, gall):
    R = w.shape[0]

    def body(w_ref, m_ref, v_ref, g_ref, go_ref, d_ref, nm_ref, nv_ref):
        g = g_ref[0]
        for k in range(1, N_DEV):
            g = g + g_ref[k]
        go_ref[...] = g
        d_ref[...], nm_ref[...], nv_ref[...] = _adamw_math(w_ref[...], g, m_ref[...], v_ref[...])

    return pl.pallas_call(body, name="adamw_small", out_shape=[SDS((R, 128), F32)] * 4,
                          compiler_params=pltpu.CompilerParams(vmem_limit_bytes=VMEM_LIMIT))(w, m, v, gall)


def _sum_chips(layers, name, layer_major):
    _, R, C = layers[0].shape
    L = len(layers)
    tc = _tile(C, 256)

    def body(*refs):
        o_ref = refs[-1]
        for l, p_ref in enumerate(refs[:-1]):
            p = [p_ref[k].astype(F32) for k in range(N_CHIPS)]
            s = ((p[0] + p[1]) + p[2]) + p[3]
            if layer_major:
                o_ref[l] = s
            else:
                o_ref[:, l, :] = s

    out = (L, R, C) if layer_major else (R, L, C)
    out_blk = (L, R, tc) if layer_major else (R, L, tc)
    return pl.pallas_call(
        body, name=name, grid=(C // tc,),
        in_specs=[pl.BlockSpec((N_CHIPS, R, tc), lambda i: (0, 0, i))] * L,
        out_specs=pl.BlockSpec(out_blk, lambda i: (0, 0, i)), out_shape=SDS(out, F32),
        compiler_params=_params("parallel"),
    )(*layers)


ANY = pl.BlockSpec(memory_space=pl.ANY)


def _mesh_pos():
    return lax.axis_index("x"), lax.axis_index("y"), lax.axis_index("c")


def _other_chips(x, y):
    return [(1 - x, y), (x, 1 - y), (1 - x, 1 - y)]


class _ChipExchange:
    def __init__(self, mode, sources):
        assert mode in ("gather", "scatter")
        self.mode, self.sources = mode, tuple(sources)
        self.n = len(self.sources)
        self.in_specs = [ANY] * self.n
        self.out_specs = [ANY] * self.n
        self.out_shape = [SDS(((N_CHIPS,) + s.shape) if mode == "gather" else s.shape, s.dtype) for s in self.sources]
        self.scratch = [pltpu.SemaphoreType.DMA((3 * self.n,)), pltpu.SemaphoreType.DMA((3 * self.n,)),
                        pltpu.SemaphoreType.DMA((self.n,))]

    def _copies(self, srcs, dsts, send_sems, recv_sems, local_sems):
        x, y, c = _mesh_pos()
        me = 2 * x + y
        view = (lambda r, chip: r) if self.mode == "gather" else (lambda r, chip: r.at[chip])
        local = [pltpu.make_async_copy(view(s, me), d.at[me], local_sems.at[a]) for a, (s, d) in enumerate(zip(srcs, dsts))]
        sends, recvs = [], []
        for j, (px, py) in enumerate(_other_chips(x, y)):
            peer = 2 * px + py
            for a, (s, d) in enumerate(zip(srcs, dsts)):
                sems = dict(send_sem=send_sems.at[self.n * j + a], recv_sem=recv_sems.at[self.n * j + a],
                            device_id=(px, py, c), device_id_type=MESH_ID)
                sends.append(pltpu.make_async_remote_copy(src_ref=view(s, peer), dst_ref=d.at[me], **sems))
                recvs.append(pltpu.make_async_remote_copy(src_ref=view(s, me), dst_ref=d.at[peer], **sems))
        return local, sends, recvs

    def start(self, srcs, dsts, sems):
        local, sends, _ = self._copies(srcs, dsts, *sems)
        for cp in local + sends:
            cp.start()

    def wait(self, srcs, dsts, sems):
        local, sends, recvs = self._copies(srcs, dsts, *sems)
        for cp in recvs:
            cp.wait_recv()
        for cp in sends:
            cp.wait_send()
        for cp in local:
            cp.wait()


def _gather_halves(w, tag):
    R, C = w.shape
    H = R // 2

    def body(w_ref, g_ref, send_sems, recv_sems, pass_send, pass_recv, local_sem):
        x, y, c = _mesh_pos()
        me = 2 * x + y
        mine, theirs = pl.ds(c * H, H), pl.ds((1 - c) * H, H)
        own = pltpu.make_async_copy(w_ref, g_ref.at[me], local_sem)
        own.start()

        def fetch(j, px, py, src, dst):
            return pltpu.make_async_remote_copy(src_ref=src, dst_ref=dst, send_sem=send_sems.at[j], recv_sem=recv_sems.at[j],
                                                device_id=(px, py, c), device_id_type=MESH_ID)

        def hand(j, rows, peer):
            return pltpu.make_async_remote_copy(src_ref=g_ref.at[peer, rows], dst_ref=g_ref.at[peer, rows],
                                                send_sem=pass_send.at[j], recv_sem=pass_recv.at[j],
                                                device_id=(x, y, 1 - c), device_id_type=MESH_ID)

        chips = _other_chips(x, y)
        sends = [fetch(j, px, py, w_ref.at[mine], g_ref.at[me, mine]) for j, (px, py) in enumerate(chips)]
        for cp in sends:
            cp.start()
        passed = []
        for j, (px, py) in enumerate(chips):
            peer = 2 * px + py
            fetch(j, px, py, w_ref.at[mine], g_ref.at[peer, mine]).wait_recv()
            passed.append(hand(j, mine, peer))
            passed[-1].start()
        for j, (px, py) in enumerate(chips):
            hand(j, theirs, 2 * px + py).wait_recv()
        for cp in sends + passed:
            cp.wait_send()
        own.wait()

    return pl.pallas_call(
        body, name=f"gather_halves_{tag}", in_specs=[ANY], out_specs=ANY, out_shape=SDS((N_CHIPS, R, C), w.dtype),
        scratch_shapes=[pltpu.SemaphoreType.DMA((3,)), pltpu.SemaphoreType.DMA((3,)), pltpu.SemaphoreType.DMA((3,)),
                        pltpu.SemaphoreType.DMA((3,)), pltpu.SemaphoreType.DMA],
        compiler_params=pltpu.CompilerParams(has_side_effects=True),
    )(w)


class _DeviceGather:
    def __init__(self, source):
        self.sources, self.n = (source,), 1
        self.in_specs, self.out_specs = [ANY], [ANY]
        self.out_shape = [SDS((N_DEV,) + source.shape, source.dtype)]
        self.scratch = [pltpu.SemaphoreType.DMA((N_DEV - 1,)), pltpu.SemaphoreType.DMA((N_DEV - 1,)),
                        pltpu.SemaphoreType.DMA((1,))]

    def _copies(self, srcs, dsts, send_sems, recv_sems, local_sems):
        (src,), (dst,) = srcs, dsts
        x, y, c = _mesh_pos()
        me = 4 * x + 2 * y + c
        local = [pltpu.make_async_copy(src, dst.at[me], local_sems.at[0])]
        sends, recvs = [], []
        for k in range(1, N_DEV):
            px, py, pc = (1 - x) if k & 4 else x, (1 - y) if k & 2 else y, (1 - c) if k & 1 else c
            sems = dict(send_sem=send_sems.at[k - 1], recv_sem=recv_sems.at[k - 1], device_id=(px, py, pc),
                        device_id_type=MESH_ID)
            sends.append(pltpu.make_async_remote_copy(src_ref=src, dst_ref=dst.at[me], **sems))
            recvs.append(pltpu.make_async_remote_copy(src_ref=src, dst_ref=dst.at[4 * px + 2 * py + pc], **sems))
        return local, sends, recvs

    start = _ChipExchange.start
    wait = _ChipExchange.wait


def _gather_devices(a, name):
    ex = _DeviceGather(a)

    def body(a_ref, g_ref, *sems):
        ex.start((a_ref,), (g_ref,), sems)
        ex.wait((a_ref,), (g_ref,), sems)

    return pl.pallas_call(
        body, name=name, in_specs=ex.in_specs, out_specs=ex.out_specs[0], out_shape=ex.out_shape[0],
        scratch_shapes=ex.scratch, compiler_params=pltpu.CompilerParams(has_side_effects=True),
    )(a)


def _swap_cores(pin, pout):
    def body(pin_ref, pout_ref, oin_ref, oout_ref, send_sems, recv_sems):
        x, y, c = _mesh_pos()
        cps = [pltpu.make_async_remote_copy(src_ref=src, dst_ref=dst, send_sem=send_sems.at[a], recv_sem=recv_sems.at[a],
                                            device_id=(x, y, 1 - c), device_id_type=MESH_ID)
               for a, (src, dst) in enumerate(((pin_ref, oin_ref), (pout_ref, oout_ref)))]
        for cp in cps:
            cp.start()
        for cp in cps:
            cp.wait()

    return pl.pallas_call(
        body, name="swap_cores", in_specs=[ANY, ANY], out_specs=[ANY, ANY],
        out_shape=[SDS(pin.shape, F32), SDS(pout.shape, F32)],
        scratch_shapes=[pltpu.SemaphoreType.DMA((2,)), pltpu.SemaphoreType.DMA((2,))],
        compiler_params=pltpu.CompilerParams(has_side_effects=True),
    )(pin, pout)


def _pack_small(parts):
    flat = [jnp.pad(p.reshape(-1), (0, (-p.size) % 128)) for p in parts]
    v = jnp.concatenate(flat)
    return jnp.pad(v, (0, (-v.size) % 1024)).reshape(-1, 128)


def _unpack_small(packed):
    flat = packed.reshape(-1)
    out, off = [], 0
    for _, shape in SMALL_PARAMS:
        size = math.prod(shape)
        out.append(flat[off:off + size].reshape(shape))
        off += size + (-size) % 128
    return out


def _layer_consts(l, gmlp_ln_g, gmlp_ln_b, gmlp_w_s, gmlp_b_s, hgrn_onorm_g, fox_b_f):
    causal = jnp.tril(jnp.ones((CHUNK, CHUNK), bool))
    wm = jnp.where(causal[None], gmlp_w_s[l], 0.0)
    return dict(
        lng=gmlp_ln_g[l].reshape(1, A_WIDTH), lnb=gmlp_ln_b[l].reshape(1, A_WIDTH),
        wm=wm.astype(BF16), wmt=jnp.swapaxes(wm, 1, 2).astype(BF16),
        bst=jnp.pad(gmlp_b_s[l].T, ((0, 0), (0, 128 - A_GROUPS))),
        onorm=jnp.tile(hgrn_onorm_g[l], 4).reshape(1, B_WIDTH),
        bf=jnp.pad(fox_b_f[l], (0, 128 - C_HEADS)).reshape(1, 128),
    )


def kernel(x, norm_g, w_in, w_out, gmlp_ln_g, gmlp_ln_b, gmlp_w_s, gmlp_b_s, hgrn_lb, hgrn_onorm_g, fox_b_f, final_norm_g, loss_target, m_norm_g, m_w_in, m_w_out, m_gmlp_ln_g, m_gmlp_ln_b, m_gmlp_w_s, m_gmlp_b_s, m_hgrn_lb, m_hgrn_onorm_g, m_fox_b_f, m_final_norm_g, v_norm_g, v_w_in, v_w_out, v_gmlp_ln_g, v_gmlp_ln_b, v_gmlp_w_s, v_gmlp_b_s, v_hgrn_lb, v_hgrn_onorm_g, v_fox_b_f, v_final_norm_g):
    T = x.shape[1]
    shard_in = w_in.shape[2]
    shard_out = w_out.shape[1]
    xs = x.reshape(T, D_MODEL)
    tgt = loss_target.reshape(T, D_MODEL)

    w_in_b, w_out_b = w_in.astype(BF16), w_out.astype(BF16)

    def full_w_in(gathered):
        wi = jnp.concatenate([gathered[k] for k in range(N_CHIPS)], axis=-1)
        return jnp.pad(wi, ((0, 0), (0, D_IN_PAD - D_IN)))

    lb_all = _lb_fwd(hgrn_lb)
    consts = [_layer_consts(l, gmlp_ln_g, gmlp_ln_b, gmlp_w_s, gmlp_b_s, hgrn_onorm_g, fox_b_f) for l in range(DEPTH)]

    saved = []
    xl = xs
    w_in_l = full_w_in(_gather_halves(w_in_b[0], "w_in_l0"))
    for l in range(DEPTH):
        cs = consts[l]
        tag = f"l{l}"
        h, proj = _inproj(xl, norm_g[l].reshape(1, D_MODEL), w_in_l, tag)
        ya = _gmlp_fwd(proj, cs["lng"], cs["lnb"], cs["wm"], cs["bst"], tag)
        yb, ob, s0 = _hgrn_fwd(proj, lb_all[l].reshape(1, B_WIDTH), cs["onorm"], tag)
        qt, kt, vt = _fox_prep(proj, cs["bf"], tag)
        ride = _ChipExchange("gather", (w_out_b[l],) + ((w_in_b[l + 1],) if l + 1 < DEPTH else ()))
        oc, lse, yc, *gathered = _fox_fwd(qt, kt, vt, proj, tag, ride)
        w_out_l = gathered[0].reshape(N_CHIPS * shard_out, D_MODEL)
        saved.append(dict(x=xl, h=h, proj=proj, ya=ya, yb=yb, yc=yc, ob=ob, s0=s0, qt=qt, kt=kt, oc=oc, lse=lse,
                          w_in=w_in_l, w_out=w_out_l))
        xl = _outproj(xl, ya, yb, yc, w_out_l, tag)
        if l + 1 < DEPTH:
            w_in_l = full_w_in(gathered[1])

    dx, loss_part, d_final = _loss_head(xl, final_norm_g.reshape(1, D_MODEL), tgt)

    g_small = {}
    dlb_rows, rin, rout = [None] * DEPTH, [None] * DEPTH, [None] * DEPTH
    slabs_in = None
    stack = lambda key: jnp.stack([g_small[l][key] for l in range(DEPTH)])
    for l in reversed(range(DEPTH)):
        cs, sv = consts[l], saved[l]
        tag = f"l{l}"
        proj = sv["proj"]
        dy, dw_out = _outproj_bwd(dx, sv["ya"], sv["yb"], sv["yc"], sv["w_out"], tag)
        da, dwm, dbst, dlng, dlnb = _gmlp_bwd(proj, dy, cs["lng"], cs["lnb"], cs["wm"], cs["wmt"], cs["bst"], tag)
        db, dlb_rows[l], donorm = _hgrn_bwd(proj, dy, sv["ob"], sv["s0"], lb_all[l].reshape(1, B_WIDTH), cs["onorm"], tag)
        do, delta, dzc = _fox_bwd_prep(proj, dy, sv["oc"], tag)
        slabs_out = dw_out.reshape(N_CHIPS, shard_out, D_MODEL).astype(BF16)
        ride = _ChipExchange("scatter", (slabs_out,) + ((slabs_in,) if slabs_in is not None else ()))
        dqt, dkt, dvc, *received = _fox_bwd(sv["qt"], sv["kt"], proj, do, sv["lse"], delta, tag, ride)
        rout[l] = received[0]
        if slabs_in is not None:
            rin[l + 1] = received[1]
        dqc, dkc, dflc, dbf = _fox_bwd_post(dqt, dkt, proj, cs["bf"], tag)
        g_small[l] = dict(ln_g=dlng.reshape(4, 64), ln_b=dlnb.reshape(4, 64), w_s=dwm, b_s=dbst[:, :A_GROUPS].T,
                          onorm=donorm[0, :64], bf=dbf[0, :C_HEADS])
        dproj = jnp.concatenate([da, db, dqc, dkc, dvc, dzc, dflc, jnp.zeros((T, 128), BF16)], axis=1)
        if l == 0:
            d_hgrn_lb = _lb_bwd(hgrn_lb, jnp.concatenate(dlb_rows, axis=0))
            early = _pack_small([stack("ln_g"), stack("ln_b"), stack("w_s"), stack("b_s"), d_hgrn_lb, stack("onorm"),
                                 stack("bf"), d_final.reshape(D_MODEL), loss_part.reshape(128)])
            dw_in, rearly = _dw_in(sv["h"], dproj, tag, _DeviceGather(early))
        else:
            dw_in = _dw_in(sv["h"], dproj, tag)
        slabs_in = dw_in[:N_CHIPS * shard_in].reshape(N_CHIPS, shard_in, D_MODEL).astype(BF16)
        ride = _ChipExchange("scatter", (slabs_in,)) if l == 0 else None
        dx, dng, *received = _dx_in(sv["x"], norm_g[l].reshape(1, D_MODEL), dx, dproj, sv["w_in"], tag, ride)
        if l == 0:
            rin[0] = received[0]
        g_small[l]["norm_g"] = dng.reshape(D_MODEL)
    grad_x = dx.reshape(x.shape)
    rlate = _gather_devices(_pack_small([stack("norm_g")]), "gather_norm_grads")
    rsmall = jnp.concatenate([rlate, rearly], axis=1)

    pin, pout = _sum_chips(rin, "sum_chips_w_in", False), _sum_chips(rout, "sum_chips_w_out", True)
    oin, oout = _swap_cores(pin, pout)
    to_view = lambda a: jnp.transpose(a, (2, 0, 1))
    g_w_in, d_w_in, nm_w_in, nv_w_in = [
        jnp.transpose(o, (1, 2, 0))
        for o in _adamw_pair(to_view(w_in), to_view(m_w_in), to_view(v_w_in), pin, oin, "adamw_w_in")]
    g_w_out, d_w_out, nm_w_out, nv_w_out = _adamw_pair(w_out, m_w_out, v_w_out, pout, oout, "adamw_w_out")

    small_w = [norm_g, gmlp_ln_g, gmlp_ln_b, gmlp_w_s, gmlp_b_s, hgrn_lb, hgrn_onorm_g, fox_b_f, final_norm_g]
    small_m = [m_norm_g, m_gmlp_ln_g, m_gmlp_ln_b, m_gmlp_w_s, m_gmlp_b_s, m_hgrn_lb, m_hgrn_onorm_g, m_fox_b_f, m_final_norm_g]
    small_v = [v_norm_g, v_gmlp_ln_g, v_gmlp_ln_b, v_gmlp_w_s, v_gmlp_b_s, v_hgrn_lb, v_hgrn_onorm_g, v_fox_b_f, v_final_norm_g]
    slot = [jnp.zeros((128,), F32)]
    outs = _adamw_small(_pack_small(small_w + slot), _pack_small(small_m + slot), _pack_small(small_v + slot), rsmall)
    sg, sd, sm, sv_ = [_unpack_small(o) for o in outs]
    loss = outs[0].reshape(-1)[sum(math.prod(s) + (-math.prod(s)) % 128 for _, s in SMALL_PARAMS)]

    def order(big_in, big_out, small):
        return [small[0], big_in, big_out] + small[1:]

    return (loss, grad_x, *order(g_w_in, g_w_out, sg), *order(d_w_in, d_w_out, sd), *order(nm_w_in, nm_w_out, sm),
            *order(nv_w_in, nv_w_out, sv_))
```

```python
import functools
import math

import jax
import jax.numpy as jnp
from jax import lax
from jax.experimental import pallas as pl
from jax.experimental.pallas import tpu as pltpu

F32 = jnp.float32
BF16 = jnp.bfloat16
SDS = jax.ShapeDtypeStruct
MESH_ID = pl.DeviceIdType.MESH

D_MODEL = 1024
DEPTH = 2
A_WIDTH = 256
A_GROUPS = 4
B_WIDTH = 256
C_WIDTH = 512
C_HEADS = 8
D_IN = 3848
D_IN_PAD = 4096
CHUNK = 128
SUB = 16
SUB_SHIFT = 4
NORM_EPS = 1e-6
F_FLOOR = 1e-30
COL_AU, COL_AV, COL_AZ = 0, 256, 512
COL_BQ, COL_BF, COL_BI, COL_BZ = 768, 1024, 1280, 1536
COL_CQ, COL_CK, COL_CV, COL_CZ, COL_CF = 1792, 2304, 2816, 3328, 3840
HEAD_LANES = 128
Q_SCALE = 0.125
ADAM_LR, ADAM_B1, ADAM_B2, ADAM_EPS, ADAM_WD, ADAM_STEP = 0.001, 0.9, 0.999, 1e-08, 0.01, 10
ADAM_C1 = 1.0 - ADAM_B1 ** ADAM_STEP
ADAM_C2 = 1.0 - ADAM_B2 ** ADAM_STEP
VMEM_LIMIT = 56 * 1024 * 1024
ADAMW_BLOCK_BYTES = 1 << 20
N_CHIPS = 4
N_DEV = 8

SMALL_PARAMS = (
    ("norm_g", (DEPTH, D_MODEL)), ("gmlp_ln_g", (DEPTH, 4, 64)), ("gmlp_ln_b", (DEPTH, 4, 64)),
    ("gmlp_w_s", (DEPTH, 4, 128, 128)), ("gmlp_b_s", (DEPTH, 4, 128)), ("hgrn_lb", (DEPTH, 256)),
    ("hgrn_onorm_g", (DEPTH, 64)), ("fox_b_f", (DEPTH, 8)), ("final_norm_g", (D_MODEL,)),
)


def _tile(n, pref):
    t = min(n, pref)
    assert n % t == 0, (n, pref)
    return t


def _params(*sem):
    return pltpu.CompilerParams(dimension_semantics=sem, vmem_limit_bytes=VMEM_LIMIT)


def _dot(a, b):
    return jnp.dot(a, b, preferred_element_type=F32)


def _dot_nt(a, b):
    return lax.dot_general(a, b, (((1,), (1,)), ((), ())), preferred_element_type=F32)


def _dot_tn(a, b):
    return lax.dot_general(a, b, (((0,), (0,)), ((), ())), preferred_element_type=F32)


def _split3(x):
    hi = x.astype(BF16)
    r = x - hi.astype(F32)
    mid = r.astype(BF16)
    lo = (r - mid.astype(F32)).astype(BF16)
    return hi, mid, lo


def _dot3_left(c, x):
    hi, mid, lo = _split3(x)
    return _dot(c, hi) + _dot(c, mid) + _dot(c, lo)


def _sigmoid(x):
    return jax.nn.sigmoid(x)


def _silu_and_grad(x):
    s = _sigmoid(x)
    return x * s, s * (1.0 + x * (1.0 - s))


_GELU_C = math.sqrt(2.0 / math.pi)


def _gelu_and_grad(x):
    inner = _GELU_C * (x + 0.044715 * x * x * x)
    t = jnp.tanh(inner)
    y = 0.5 * x * (1.0 + t)
    dy = 0.5 * (1.0 + t) + 0.5 * x * (1.0 - t * t) * _GELU_C * (1.0 + 3.0 * 0.044715 * x * x)
    return y, dy


def _lane(shape):
    return lax.broadcasted_iota(jnp.int32, shape, 1)


def _row(shape):
    return lax.broadcasted_iota(jnp.int32, shape, 0)


def _gsum64(x):
    lo = _lane(x.shape) < 64
    s0 = jnp.sum(jnp.where(lo, x, 0.0), axis=-1, keepdims=True)
    s1 = jnp.sum(jnp.where(lo, 0.0, x), axis=-1, keepdims=True)
    return jnp.where(lo, s0, s1)


def _colreduce(x, op):
    parts = [x[r:r + 8, :] for r in range(0, x.shape[0], 8)]
    while len(parts) > 1:
        pairs = [op(parts[k], parts[k + 1]) for k in range(0, len(parts) - 1, 2)]
        parts = pairs + ([parts[-1]] if len(parts) % 2 else [])
    red = jnp.max if op is jnp.maximum else jnp.sum
    return red(parts[0], axis=0, keepdims=True)


def _block_diag64(dtype=BF16):
    r, c = _row((128, 128)), _lane((128, 128))
    return jnp.where((r >> 6) == (c >> 6), 1.0, 0.0).astype(dtype)


def _inproj(x, g, w, tag):
    T, D = x.shape
    DP = w.shape[1]
    tm = _tile(T, 512)

    def body(x_ref, g_ref, w_ref, h_ref, p_ref):
        xv = x_ref[...]
        r = lax.rsqrt(jnp.mean(xv * xv, axis=-1, keepdims=True) + NORM_EPS)
        h = (xv * r * g_ref[...]).astype(BF16)
        h_ref[...] = h
        p_ref[...] = _dot(h, w_ref[...])

    return pl.pallas_call(
        body, name=f"inproj_{tag}", grid=(T // tm,),
        in_specs=[pl.BlockSpec((tm, D), lambda i: (i, 0)), pl.BlockSpec((1, D), lambda i: (0, 0)),
                  pl.BlockSpec((D, DP), lambda i: (0, 0))],
        out_specs=[pl.BlockSpec((tm, D), lambda i: (i, 0)), pl.BlockSpec((tm, DP), lambda i: (i, 0))],
        out_shape=[SDS((T, D), BF16), SDS((T, DP), F32)],
        compiler_params=_params("parallel"),
    )(x, g, w)


def _outproj(x, ya, yb, yc, wo, tag):
    T, D = x.shape
    tm = _tile(T, 512)

    def body(x_ref, ya_ref, yb_ref, yc_ref, wo_ref, o_ref):
        acc = x_ref[...] + _dot(ya_ref[...], wo_ref[0:A_WIDTH, :])
        acc = acc + _dot(yb_ref[...], wo_ref[A_WIDTH:A_WIDTH + B_WIDTH, :])
        o_ref[...] = acc + _dot(yc_ref[...], wo_ref[A_WIDTH + B_WIDTH:, :])

    row = lambda w: pl.BlockSpec((tm, w), lambda i: (i, 0))
    return pl.pallas_call(
        body, name=f"outproj_{tag}", grid=(T // tm,),
        in_specs=[row(D), row(A_WIDTH), row(B_WIDTH), row(C_WIDTH), pl.BlockSpec(wo.shape, lambda i: (0, 0))],
        out_specs=row(D), out_shape=SDS((T, D), F32), compiler_params=_params("parallel"),
    )(x, ya, yb, yc, wo)


def _outproj_bwd(dx, ya, yb, yc, wo, tag):
    T, D = dx.shape
    DM = wo.shape[0]
    tm = _tile(T, 512)

    def body(dx_ref, ya_ref, yb_ref, yc_ref, wo_ref, dy_ref, dwo_ref):
        @pl.when(pl.program_id(0) == 0)
        def _():
            dwo_ref[...] = jnp.zeros_like(dwo_ref)

        dxb = dx_ref[...].astype(BF16)
        dy_ref[...] = _dot_nt(dxb, wo_ref[...])
        dwo_ref[0:A_WIDTH, :] += _dot_tn(ya_ref[...], dxb)
        dwo_ref[A_WIDTH:A_WIDTH + B_WIDTH, :] += _dot_tn(yb_ref[...], dxb)
        dwo_ref[A_WIDTH + B_WIDTH:, :] += _dot_tn(yc_ref[...], dxb)

    row = lambda w: pl.BlockSpec((tm, w), lambda i: (i, 0))
    return pl.pallas_call(
        body, name=f"outproj_bwd_{tag}", grid=(T // tm,),
        in_specs=[row(D), row(A_WIDTH), row(B_WIDTH), row(C_WIDTH), pl.BlockSpec(wo.shape, lambda i: (0, 0))],
        out_specs=[row(DM), pl.BlockSpec((DM, D), lambda i: (0, 0))],
        out_shape=[SDS((T, DM), F32), SDS((DM, D), F32)], compiler_params=_params("arbitrary"),
    )(dx, ya, yb, yc, wo)


def _dw_in(h, dproj, tag, ride=None):
    T, D = h.shape
    DP = dproj.shape[1]
    tm, tn = _tile(T, 1024), _tile(DP, 1024)
    grid = (DP // tn, T // tm)

    def body(h_ref, dp_ref, *rest):
        ride_srcs, (dw_ref,), ride_dsts, _, ride_sems = _ride_refs(ride, rest, 1, 0)
        _ride_start(ride, grid, ride_srcs, ride_dsts, ride_sems)

        @pl.when(pl.program_id(1) == 0)
        def _():
            dw_ref[...] = jnp.zeros_like(dw_ref)

        dw_ref[...] += _dot_tn(dp_ref[...], h_ref[...])
        _ride_wait(ride, grid, ride_srcs, ride_dsts, ride_sems)

    extra = ride or _ChipExchange("gather", ())
    out = pl.pallas_call(
        body, name=f"dw_in_{tag}", grid=grid,
        in_specs=[pl.BlockSpec((tm, D), lambda j, i: (i, 0)), pl.BlockSpec((tm, tn), lambda j, i: (i, j))] + extra.in_specs,
        out_specs=[pl.BlockSpec((tn, D), lambda j, i: (j, 0))] + extra.out_specs,
        out_shape=[SDS((DP, D), F32)] + extra.out_shape, scratch_shapes=extra.scratch if ride else [],
        compiler_params=pltpu.CompilerParams(dimension_semantics=("arbitrary", "arbitrary"), vmem_limit_bytes=VMEM_LIMIT,
                                             has_side_effects=bool(ride)),
    )(h, dproj, *extra.sources)
    return out if ride else out[0]


def _dx_in(x, g, dres, dproj, w, tag, ride=None):
    T, D = x.shape
    DP = w.shape[1]
    tm = _tile(T, 512)
    grid = (T // tm,)

    def body(x_ref, g_ref, dres_ref, dp_ref, w_ref, *rest):
        ride_srcs, (dx_ref, dg_ref), ride_dsts, _, ride_sems = _ride_refs(ride, rest, 2, 0)
        _ride_start(ride, grid, ride_srcs, ride_dsts, ride_sems)

        @pl.when(pl.program_id(0) == 0)
        def _():
            dg_ref[...] = jnp.zeros_like(dg_ref)

        dh = _dot_nt(dp_ref[...], w_ref[...])
        xv = x_ref[...]
        r = lax.rsqrt(jnp.mean(xv * xv, axis=-1, keepdims=True) + NORM_EPS)
        xh = xv * r
        dg_ref[...] += jnp.sum(dh * xh, axis=0, keepdims=True)
        dxh = dh * g_ref[...]
        dx_ref[...] = dres_ref[...] + r * (dxh - xh * jnp.mean(dxh * xh, axis=-1, keepdims=True))
        _ride_wait(ride, grid, ride_srcs, ride_dsts, ride_sems)

    extra = ride or _ChipExchange("gather", ())
    row = pl.BlockSpec((tm, D), lambda i: (i, 0))
    return pl.pallas_call(
        body, name=f"dx_in_{tag}", grid=grid,
        in_specs=[row, pl.BlockSpec((1, D), lambda i: (0, 0)), row, pl.BlockSpec((tm, DP), lambda i: (i, 0)),
                  pl.BlockSpec((D, DP), lambda i: (0, 0))] + extra.in_specs,
        out_specs=[row, pl.BlockSpec((1, D), lambda i: (0, 0))] + extra.out_specs,
        out_shape=[SDS((T, D), F32), SDS((1, D), F32)] + extra.out_shape,
        scratch_shapes=extra.scratch if ride else [],
        compiler_params=pltpu.CompilerParams(dimension_semantics=("arbitrary",), vmem_limit_bytes=VMEM_LIMIT,
                                             has_side_effects=bool(ride)),
    )(x, g, dres, dproj, w, *extra.sources)


def _loss_head(x, g, tgt):
    T, D = x.shape
    tm = _tile(T, 512)

    def body(x_ref, g_ref, t_ref, dx_ref, loss_ref, dg_ref):
        @pl.when(pl.program_id(0) == 0)
        def _():
            loss_ref[...] = jnp.zeros_like(loss_ref)
            dg_ref[...] = jnp.zeros_like(dg_ref)

        xv = x_ref[...]
        r = lax.rsqrt(jnp.mean(xv * xv, axis=-1, keepdims=True) + NORM_EPS)
        xh = xv * r
        gv = g_ref[...]
        err = xh * gv - t_ref[...]
        tok = jnp.mean(err * err, axis=-1, keepdims=True)
        loss_ref[...] += 0.5 * jnp.sum(tok, axis=0, keepdims=True)
        dy = err * (1.0 / D)
        dg_ref[...] += jnp.sum(dy * xh, axis=0, keepdims=True)
        dxh = dy * gv
        dx_ref[...] = r * (dxh - xh * jnp.mean(dxh * xh, axis=-1, keepdims=True))

    row = pl.BlockSpec((tm, D), lambda i: (i, 0))
    return pl.pallas_call(
        body, name="loss_head", grid=(T // tm,),
        in_specs=[row, pl.BlockSpec((1, D), lambda i: (0, 0)), row],
        out_specs=[row, pl.BlockSpec((1, 128), lambda i: (0, 0)), pl.BlockSpec((1, D), lambda i: (0, 0))],
        out_shape=[SDS((T, D), F32), SDS((1, 128), F32), SDS((1, D), F32)], compiler_params=_params("arbitrary"),
    )(x, g, tgt)


def _gmlp_core(u, v, lng, lnb, wm_ref, bst_ref, pair):
    ug, dug = _gelu_and_grad(u)
    vg, dvg = _gelu_and_grad(v)
    mu = _gsum64(vg) * (1.0 / 64)
    d = vg - mu
    var = _gsum64(d * d) * (1.0 / 64)
    rstd = lax.rsqrt(var + NORM_EPS)
    xh = d * rstd
    vn = xh * lng + lnb
    vnb = vn.astype(BF16)
    lo = _lane(u.shape) < 64
    g0, g1 = 2 * pair, 2 * pair + 1
    mixed = jnp.where(lo, _dot(wm_ref[g0], vnb) + bst_ref[:, g0:g0 + 1], _dot(wm_ref[g1], vnb) + bst_ref[:, g1:g1 + 1])
    return ug, dug, dvg, rstd, xh, vnb, mixed, lo


def _gmlp_fwd(proj, lng, lnb, wm, bst, tag):
    T = proj.shape[0]

    def body(u_ref, v_ref, z_ref, lng_ref, lnb_ref, wm_ref, bst_ref, y_ref):
        for pair in range(2):
            sl = slice(128 * pair, 128 * pair + 128)
            ug, _, _, _, _, _, mixed, _ = _gmlp_core(u_ref[:, sl], v_ref[:, sl], lng_ref[:, sl], lnb_ref[:, sl],
                                                     wm_ref, bst_ref, pair)
            sz, _ = _silu_and_grad(z_ref[:, sl])
            y_ref[:, sl] = (ug * mixed * sz).astype(BF16)

    col = lambda c: pl.BlockSpec((CHUNK, A_WIDTH), lambda i, c=c: (i, c // A_WIDTH))
    full = lambda a: pl.BlockSpec(a.shape, lambda i, n=a.ndim: (0,) * n)
    return pl.pallas_call(
        body, name=f"gmlp_fwd_{tag}", grid=(T // CHUNK,),
        in_specs=[col(COL_AU), col(COL_AV), col(COL_AZ), full(lng), full(lnb), full(wm), full(bst)],
        out_specs=pl.BlockSpec((CHUNK, A_WIDTH), lambda i: (i, 0)), out_shape=SDS((T, A_WIDTH), BF16),
        compiler_params=_params("parallel"),
    )(proj, proj, proj, lng, lnb, wm, bst)


def _gmlp_bwd(proj, dy, lng, lnb, wm, wmt, bst, tag):
    T = proj.shape[0]
    n = T // CHUNK

    def body(u_ref, v_ref, z_ref, dy_ref, lng_ref, lnb_ref, wm_ref, wmt_ref, bst_ref,
             da_ref, dwm_ref, dbst_ref, dlng_ref, dlnb_ref):
        @pl.when(pl.program_id(0) == 0)
        def _():
            dwm_ref[...] = jnp.zeros_like(dwm_ref)
            dbst_ref[...] = jnp.zeros_like(dbst_ref)
            dlng_ref[...] = jnp.zeros_like(dlng_ref)
            dlnb_ref[...] = jnp.zeros_like(dlnb_ref)

        lane = _lane((CHUNK, 128))
        dbst = dbst_ref[...]
        for pair in range(2):
            sl = slice(128 * pair, 128 * pair + 128)
            lng_p = lng_ref[:, sl]
            ug, dug, dvg, rstd, xh, vnb, mixed, lo = _gmlp_core(u_ref[:, sl], v_ref[:, sl], lng_p, lnb_ref[:, sl],
                                                                wm_ref, bst_ref, pair)
            sz, dsz = _silu_and_grad(z_ref[:, sl])
            dyv = dy_ref[:, sl]
            out = ug * mixed
            dz = dyv * out * dsz
            dout = dyv * sz
            du = dout * mixed * dug
            dmix = dout * ug
            g0, g1 = 2 * pair, 2 * pair + 1
            dm0 = jnp.where(lo, dmix, 0.0)
            dm1 = jnp.where(lo, 0.0, dmix)
            dbst = dbst + jnp.where(lane == g0, jnp.sum(dm0, axis=-1, keepdims=True), 0.0)
            dbst = dbst + jnp.where(lane == g1, jnp.sum(dm1, axis=-1, keepdims=True), 0.0)
            dwm_ref[g0] += _dot_nt(dm0.astype(BF16), vnb)
            dwm_ref[g1] += _dot_nt(dm1.astype(BF16), vnb)
            dmb = dmix.astype(BF16)
            dvn = jnp.where(lo, _dot(wmt_ref[g0], dmb), _dot(wmt_ref[g1], dmb))
            dlng_ref[:, sl] += jnp.sum(dvn * xh, axis=0, keepdims=True)
            dlnb_ref[:, sl] += jnp.sum(dvn, axis=0, keepdims=True)
            dxh = dvn * lng_p
            m1 = _gsum64(dxh) * (1.0 / 64)
            m2 = _gsum64(dxh * xh) * (1.0 / 64)
            dv = rstd * (dxh - m1 - xh * m2) * dvg
            da_ref[:, COL_AU + 128 * pair:COL_AU + 128 * pair + 128] = du.astype(BF16)
            da_ref[:, COL_AV + 128 * pair:COL_AV + 128 * pair + 128] = dv.astype(BF16)
            da_ref[:, COL_AZ + 128 * pair:COL_AZ + 128 * pair + 128] = dz.astype(BF16)
        dbst_ref[...] = dbst

        @pl.when(pl.program_id(0) == n - 1)
        def _():
            causal = _lane((CHUNK, CHUNK)) <= _row((CHUNK, CHUNK))
            for g in range(A_GROUPS):
                dwm_ref[g] = jnp.where(causal, dwm_ref[g], 0.0)

    col = lambda c: pl.BlockSpec((CHUNK, A_WIDTH), lambda i, c=c: (i, c // A_WIDTH))
    full = lambda a: pl.BlockSpec(a.shape, lambda i, n=a.ndim: (0,) * n)
    acc = lambda s: pl.BlockSpec(s, lambda i, n=len(s): (0,) * n)
    return pl.pallas_call(
        body, name=f"gmlp_bwd_{tag}", grid=(n,),
        in_specs=[col(COL_AU), col(COL_AV), col(COL_AZ), pl.BlockSpec((CHUNK, A_WIDTH), lambda i: (i, 0)),
                  full(lng), full(lnb), full(wm), full(wmt), full(bst)],
        out_specs=[pl.BlockSpec((CHUNK, 3 * A_WIDTH), lambda i: (i, 0)), acc((A_GROUPS, CHUNK, CHUNK)),
                   acc((CHUNK, 128)), acc((1, A_WIDTH)), acc((1, A_WIDTH))],
        out_shape=[SDS((T, 3 * A_WIDTH), BF16), SDS((A_GROUPS, CHUNK, CHUNK), F32), SDS((CHUNK, 128), F32),
                   SDS((1, A_WIDTH), F32), SDS((1, A_WIDTH), F32)],
        compiler_params=_params("arbitrary"),
    )(proj, proj, proj, dy, lng, lnb, wm, wmt, bst)


def _hgrn_consts():
    r, c = _row((CHUNK, CHUNK)), _lane((CHUNK, CHUNK))
    same = (r >> SUB_SHIFT) == (c >> SUB_SHIFT)
    lsub = jnp.where(same & (c <= r), 1.0, 0.0).astype(BF16)
    usub = jnp.where(same & (c >= r), 1.0, 0.0).astype(BF16)
    bsub = jnp.where(same, 1.0, 0.0).astype(BF16)
    return lsub, usub, bsub


def _hgrn_gates(qv, zf, lbp):
    sq, dsq = _silu_and_grad(qv)
    qt = sq * Q_SCALE
    sg = _sigmoid(zf)
    sgn = _sigmoid(-zf)
    f = lbp + (1.0 - lbp) * sg
    g = jnp.log(jnp.maximum(f, F_FLOOR))
    kf = (1.0 - lbp) * sgn
    return qt, dsq, sg, sgn, f, g, kf


def _hgrn_intra_fwd(qt, kf, b, v, mbd):
    rid = _row((SUB, 128))
    parts = []
    for s in range(SUB):
        e = jnp.exp(b - b[s:s + 1, :])
        parts.append(jnp.where(rid >= s, qt * kf[s:s + 1, :] * e, 0.0))
    a = _dot(jnp.concatenate(parts, axis=0).astype(BF16), mbd)
    o = jnp.zeros((SUB, 128), F32)
    for s in range(SUB):
        o = o + a[SUB * s:SUB * s + SUB, :] * v[s:s + 1, :]
    return o


def _hgrn_intra_bwd(qt, kf, b, v, do, mbd, rsum):
    rid = _row((SUB, 128))
    ps, das, kes, es = [], [], [], []
    for s in range(SUB):
        e = jnp.where(rid >= s, jnp.exp(b - b[s:s + 1, :]), 0.0)
        ke = kf[s:s + 1, :] * e
        es.append(e)
        kes.append(ke)
        ps.append(qt * ke)
        das.append(do * v[s:s + 1, :])
    a = _dot(jnp.concatenate(ps, axis=0).astype(BF16), mbd)
    da = _dot(jnp.concatenate(das, axis=0).astype(BF16), mbd)
    dqt = jnp.zeros((SUB, 128), F32)
    xs, ys = [], []
    for s in range(SUB):
        da_s = da[SUB * s:SUB * s + SUB, :]
        dqt = dqt + da_s * kes[s]
        xs.append(a[SUB * s:SUB * s + SUB, :] * do)
        ys.append(da_s * qt * es[s])
    dv = _dot(rsum, jnp.concatenate(xs, axis=0).astype(BF16))
    dkf = _dot(rsum, jnp.concatenate(ys, axis=0).astype(BF16))
    return dqt, dkf, dv


def _hgrn_norm_gate(o, z, onorm):
    ms = _gsum64(o * o) * (1.0 / 64)
    r = lax.rsqrt(ms + NORM_EPS)
    xh = o * r
    sz, dsz = _silu_and_grad(z)
    return xh, r, sz, dsz, xh * onorm


def _hgrn_fwd(proj, lb, onorm, tag):
    T = proj.shape[0]
    n = T // CHUNK
    nsub = CHUNK // SUB

    def body(q_ref, f_ref, i_ref, z_ref, lb_ref, on_ref, y_ref, o_ref, s0_ref, st_ref):
        @pl.when(pl.program_id(0) == 0)
        def _():
            st_ref[...] = jnp.zeros_like(st_ref)

        lsub, _, bsub = _hgrn_consts()
        mbd = _block_diag64()
        bdmask = mbd > 0
        rid = _row((CHUNK, 128))
        for pair in range(2):
            sl = slice(128 * pair, 128 * pair + 128)
            qt, _, _, _, _, g, kf = _hgrn_gates(q_ref[:, sl], f_ref[:, sl], lb_ref[:, sl])
            v = i_ref[:, sl]
            b = _dot3_left(lsub, g)
            bl = _dot3_left(bsub, g)
            qh = (qt * jnp.exp(b)).astype(BF16)
            kh = kf * jnp.exp(bl - b)
            dec = jnp.exp(bl)
            vtb = v.T.astype(BF16)
            st = st_ref[pair]
            s0_ref[0, pair] = st
            outs = []
            for sub in range(nsub):
                rs = slice(SUB * sub, SUB * sub + SUB)
                o_inter = _dot_nt(qh[rs], st.astype(BF16))
                outs.append(o_inter + _hgrn_intra_fwd(qt[rs], kf[rs], b[rs], v[rs], mbd))
                khm = jnp.where((rid >> SUB_SHIFT) == sub, kh, 0.0).astype(BF16)
                st = jnp.where(bdmask, st * dec[SUB * sub:SUB * sub + 1, :] + _dot(vtb, khm), 0.0)
            st_ref[pair] = st
            o = jnp.concatenate(outs, axis=0)
            o_ref[:, sl] = o
            _, _, sz, _, on = _hgrn_norm_gate(o, z_ref[:, sl], on_ref[:, sl])
            y_ref[:, sl] = (on * sz).astype(BF16)

    col = lambda c: pl.BlockSpec((CHUNK, B_WIDTH), lambda i, c=c: (i, c // B_WIDTH))
    full = lambda a: pl.BlockSpec(a.shape, lambda i, n=a.ndim: (0,) * n)
    return pl.pallas_call(
        body, name=f"hgrn_fwd_{tag}", grid=(n,),
        in_specs=[col(COL_BQ), col(COL_BF), col(COL_BI), col(COL_BZ), full(lb), full(onorm)],
        out_specs=[pl.BlockSpec((CHUNK, B_WIDTH), lambda i: (i, 0)), pl.BlockSpec((CHUNK, B_WIDTH), lambda i: (i, 0)),
                   pl.BlockSpec((1, 2, 128, 128), lambda i: (i, 0, 0, 0))],
        out_shape=[SDS((T, B_WIDTH), BF16), SDS((T, B_WIDTH), F32), SDS((n, 2, 128, 128), F32)],
        scratch_shapes=[pltpu.VMEM((2, 128, 128), F32)], compiler_params=_params("arbitrary"),
    )(proj, proj, proj, proj, lb, onorm)


def _hgrn_bwd(proj, dy, o_saved, s0, lb, onorm, tag):
    T = proj.shape[0]
    n = T // CHUNK
    nsub = CHUNK // SUB

    def body(q_ref, f_ref, i_ref, z_ref, dy_ref, o_ref, s0_ref, lb_ref, on_ref,
             db_ref, dlb_ref, don_ref, dst_ref, sts_ref):
        @pl.when(pl.program_id(0) == 0)
        def _():
            dst_ref[...] = jnp.zeros_like(dst_ref)
            dlb_ref[...] = jnp.zeros_like(dlb_ref)
            don_ref[...] = jnp.zeros_like(don_ref)

        lsub, usub, bsub = _hgrn_consts()
        mbd = _block_diag64()
        bdmask = mbd > 0
        rsum = jnp.where((_lane((SUB, SUB * SUB)) >> SUB_SHIFT) == _row((SUB, SUB * SUB)), 1.0, 0.0).astype(BF16)
        for pair in range(2):
            sl = slice(128 * pair, 128 * pair + 128)
            lbp = lb_ref[:, sl]
            qv, zf = q_ref[:, sl], f_ref[:, sl]
            qt, dsq, sg, sgn, f, g, kf = _hgrn_gates(qv, zf, lbp)
            v = i_ref[:, sl]
            b = _dot3_left(lsub, g)
            bl = _dot3_left(bsub, g)
            eb = jnp.exp(b)
            ekb = jnp.exp(bl - b)
            qhb = (qt * eb).astype(BF16)
            khb = (kf * ekb).astype(BF16)
            dec = jnp.exp(bl)
            vb = v.astype(BF16)
            onp = on_ref[:, sl]
            ov = o_ref[:, sl]
            xh, r, sz, dsz, on = _hgrn_norm_gate(ov, z_ref[:, sl], onp)
            dyv = dy_ref[:, sl]
            dz = dyv * on * dsz
            don = dyv * sz
            cn = jnp.sum(don * xh, axis=0, keepdims=True)
            don_ref[...] += cn + pltpu.roll(cn, 64, axis=1)
            dxo = don * onp
            do = r * (dxo - xh * (_gsum64(dxo * xh) * (1.0 / 64)))
            dob = do.astype(BF16)
            st = s0_ref[0, pair]
            for sub in range(nsub):
                rs = slice(SUB * sub, SUB * sub + SUB)
                sts_ref[sub] = st
                st = jnp.where(bdmask, st * dec[SUB * sub:SUB * sub + 1, :] + _dot_tn(vb[rs], khb[rs]), 0.0)
            gst = dst_ref[pair]
            dqt_p, dkf_p, dv_p, dbl_p = [None] * nsub, [None] * nsub, [None] * nsub, [None] * nsub
            for sub in reversed(range(nsub)):
                rs = slice(SUB * sub, SUB * sub + SUB)
                st_in = sts_ref[sub]
                gb = gst.astype(BF16)
                dqh = _dot(dob[rs], st_in.astype(BF16))
                dkh = _dot(vb[rs], gb)
                dv_inter = _dot_nt(khb[rs], gb)
                ddec = jnp.sum(gst * st_in, axis=0, keepdims=True)
                dec_row = dec[SUB * sub:SUB * sub + 1, :]
                gst = jnp.where(bdmask, gst * dec_row + _dot_tn(dob[rs], qhb[rs]), 0.0)
                dqt_i, dkf_i, dv_i = _hgrn_intra_bwd(qt[rs], kf[rs], b[rs], v[rs], do[rs], mbd, rsum)
                dkf_inter = dkh * ekb[rs]
                dqt_p[sub] = dqh * eb[rs] + dqt_i
                dkf_p[sub] = dkf_inter + dkf_i
                dv_p[sub] = dv_inter + dv_i
                row = jnp.sum(kf[rs] * dkf_inter, axis=0, keepdims=True) + ddec * dec_row
                dbl_p[sub] = jnp.broadcast_to(row, (SUB, 128))
            dst_ref[pair] = gst
            dqt = jnp.concatenate(dqt_p, axis=0)
            dkf = jnp.concatenate(dkf_p, axis=0)
            dv = jnp.concatenate(dv_p, axis=0)
            dg = _dot3_left(usub, qt * dqt - kf * dkf) + jnp.concatenate(dbl_p, axis=0)
            df = jnp.where(f > F_FLOOR, dg / f, 0.0)
            dlb_ref[:, sl] += jnp.sum(df * (1.0 - sg) - dkf * sgn, axis=0, keepdims=True)
            dfl = (1.0 - lbp) * sg * sgn * (df - dkf)
            dq = dqt * Q_SCALE * dsq
            db_ref[:, 0 * B_WIDTH + 128 * pair:0 * B_WIDTH + 128 * pair + 128] = dq.astype(BF16)
            db_ref[:, 1 * B_WIDTH + 128 * pair:1 * B_WIDTH + 128 * pair + 128] = dfl.astype(BF16)
            db_ref[:, 2 * B_WIDTH + 128 * pair:2 * B_WIDTH + 128 * pair + 128] = dv.astype(BF16)
            db_ref[:, 3 * B_WIDTH + 128 * pair:3 * B_WIDTH + 128 * pair + 128] = dz.astype(BF16)

    rev = lambda c: pl.BlockSpec((CHUNK, B_WIDTH), lambda i, c=c: (n - 1 - i, c // B_WIDTH))
    full = lambda a: pl.BlockSpec(a.shape, lambda i, n_=a.ndim: (0,) * n_)
    acc = lambda s: pl.BlockSpec(s, lambda i, n_=len(s): (0,) * n_)
    return pl.pallas_call(
        body, name=f"hgrn_bwd_{tag}", grid=(n,),
        in_specs=[rev(COL_BQ), rev(COL_BF), rev(COL_BI), rev(COL_BZ),
                  pl.BlockSpec((CHUNK, B_WIDTH), lambda i: (n - 1 - i, 1)),
                  pl.BlockSpec((CHUNK, B_WIDTH), lambda i: (n - 1 - i, 0)),
                  pl.BlockSpec((1, 2, 128, 128), lambda i: (n - 1 - i, 0, 0, 0)), full(lb), full(onorm)],
        out_specs=[pl.BlockSpec((CHUNK, 4 * B_WIDTH), lambda i: (n - 1 - i, 0)), acc((1, B_WIDTH)), acc((1, 128))],
        out_shape=[SDS((T, 4 * B_WIDTH), BF16), SDS((1, B_WIDTH), F32), SDS((1, 128), F32)],
        scratch_shapes=[pltpu.VMEM((2, 128, 128), F32), pltpu.VMEM((nsub, 128, 128), F32)],
        compiler_params=_params("arbitrary"),
    )(proj, proj, proj, proj, dy, o_saved, s0, lb, onorm)


def _lb_fwd(hgrn_lb):
    assert hgrn_lb.shape[0] == 2

    def body(x_ref, o_ref):
        x0, x1 = x_ref[0:1, :], x_ref[1:2, :]
        m = jnp.maximum(x0, x1)
        e0, e1 = jnp.exp(x0 - m), jnp.exp(x1 - m)
        p0, p1 = e0 / (e0 + e1), e1 / (e0 + e1)
        o_ref[0:1, :] = jnp.clip(p0 - p0, 0.0, 1.0 - 1e-6)
        o_ref[1:2, :] = jnp.clip((p0 + p1) - p0, 0.0, 1.0 - 1e-6)

    return pl.pallas_call(body, name="lb_fwd", out_shape=SDS(hgrn_lb.shape, F32))(hgrn_lb)


def _lb_bwd(hgrn_lb, dlb):
    def body(x_ref, d_ref, o_ref):
        x0, x1 = x_ref[0:1, :], x_ref[1:2, :]
        m = jnp.maximum(x0, x1)
        e0, e1 = jnp.exp(x0 - m), jnp.exp(x1 - m)
        p0, p1 = e0 / (e0 + e1), e1 / (e0 + e1)
        val = (p0 + p1) - p0
        dp1 = jnp.where((val > 0.0) & (val < 1.0 - 1e-6), d_ref[1:2, :], 0.0)
        inner = p1 * dp1
        o_ref[0:1, :] = p0 * (0.0 - inner)
        o_ref[1:2, :] = p1 * (dp1 - inner)

    return pl.pallas_call(body, name="lb_bwd", out_shape=SDS(hgrn_lb.shape, F32))(hgrn_lb, dlb)


def _fox_prep(proj, bf, tag):
    T = proj.shape[0]
    n = T // CHUNK

    def body(q0_ref, q1_ref, k0_ref, k1_ref, v0_ref, v1_ref, fl_ref, bf_ref, qo_ref, ko_ref, vt_ref, carry_ref):
        for p, v_ref in enumerate((v0_ref, v0_ref, v1_ref, v1_ref)):
            vt_ref[p, 0] = v_ref[:, 128 * (p % 2):128 * (p % 2) + 128].T.astype(BF16)

        @pl.when(pl.program_id(0) == 0)
        def _():
            carry_ref[...] = jnp.zeros_like(carry_ref)

        ltri = jnp.where(_lane((CHUNK, CHUNK)) <= _row((CHUNK, CHUNK)), 1.0, 0.0).astype(BF16)
        lf = jax.nn.log_sigmoid(fl_ref[...] + bf_ref[...])
        c = _dot3_left(ltri, lf) + carry_ref[...]
        carry_ref[...] = c[CHUNK - 1:CHUNK, :]
        lane = _lane((CHUNK, 128))
        feat = lane < 64
        ones_q = (lane >= 67) & (lane <= 69)
        ones_k = (lane >= 64) & (lane <= 66)
        qrefs, krefs = (q0_ref, q1_ref), (k0_ref, k1_ref)
        for h in range(C_HEADS):
            blk = slice(128 * ((h // 2) % 2), 128 * ((h // 2) % 2) + 128)
            qp, kp = qrefs[h // 4][:, blk], krefs[h // 4][:, blk]
            if h % 2:
                qp, kp = pltpu.roll(qp, 64, axis=1), pltpu.roll(kp, 64, axis=1)
            ch = jnp.broadcast_to(c[:, h:h + 1], (CHUNK, 128))
            hi = ch.astype(BF16).astype(F32)
            r1 = ch - hi
            mid = r1.astype(BF16).astype(F32)
            lo = r1 - mid
            aq = jnp.where(lane == 64, hi, jnp.where(lane == 65, mid, jnp.where(lane == 66, lo,
                           jnp.where(ones_q, 1.0, 0.0))))
            ak = jnp.where(lane == 67, -hi, jnp.where(lane == 68, -mid, jnp.where(lane == 69, -lo,
                           jnp.where(ones_k, 1.0, 0.0))))
            qo_ref[:, 128 * h:128 * h + 128] = jnp.where(feat, qp * Q_SCALE, aq).astype(BF16)
            ko_ref[:, 128 * h:128 * h + 128] = jnp.where(feat, kp, ak).astype(BF16)

    w = 256
    col = lambda c: pl.BlockSpec((CHUNK, w), lambda i, c=c: (i, c // w))
    return pl.pallas_call(
        body, name=f"fox_prep_{tag}", grid=(n,),
        in_specs=[col(COL_CQ), col(COL_CQ + w), col(COL_CK), col(COL_CK + w), col(COL_CV), col(COL_CV + w),
                  pl.BlockSpec((CHUNK, 128), lambda i: (i, COL_CF // 128)), pl.BlockSpec((1, 128), lambda i: (0, 0))],
        out_specs=[pl.BlockSpec((CHUNK, C_HEADS * 128), lambda i: (i, 0))] * 2
        + [pl.BlockSpec((C_HEADS // 2, 1, 128, CHUNK), lambda i: (0, i, 0, 0))],
        out_shape=[SDS((T, C_HEADS * 128), BF16)] * 2 + [SDS((C_HEADS // 2, n, 128, CHUNK), BF16)],
        scratch_shapes=[pltpu.VMEM((1, 128), F32)], compiler_params=_params("arbitrary"),
    )(proj, proj, proj, proj, proj, proj, proj, bf)


FOX_TILE = 512
FOX_KEYS = 512


def _fox_mask(tk, tq, k0, q0):
    return (_row((tk, tq)) + (k0 - q0)) <= _lane((tk, tq))


def _ride_refs(ride, rest, n_out, n_scratch):
    n = ride.n if ride else 0
    srcs, rest = rest[:n], rest[n:]
    outs, rest = rest[:n_out], rest[n_out:]
    dsts, rest = rest[:n], rest[n:]
    return srcs, outs, dsts, rest[:n_scratch], rest[n_scratch:]


def _ride_start(ride, grid, srcs, dsts, sems):
    if ride:
        first = functools.reduce(lambda a, b: a & b, [pl.program_id(d) == 0 for d in range(len(grid))])
        pl.when(first)(lambda: ride.start(srcs, dsts, sems))


def _ride_wait(ride, grid, srcs, dsts, sems):
    if ride:
        last = functools.reduce(lambda a, b: a & b, [pl.program_id(d) == n - 1 for d, n in enumerate(grid)])
        pl.when(last)(lambda: ride.wait(srcs, dsts, sems))


def _fox_fwd(qt, kt, vt, proj, tag, ride=None):
    T = proj.shape[0]
    tq, tk = _tile(T, FOX_TILE), _tile(T, FOX_KEYS)
    nq, nsub = T // tq, tk // CHUNK
    npair = C_HEADS // 2

    def body(q_ref, k_ref, vt_ref, z_ref, *rest):
        ride_srcs, (o_ref, lse_ref, y_ref), ride_dsts, (acc_ref, st_ref, pt_ref), ride_sems = _ride_refs(ride, rest, 3, 3)
        i = pl.program_id(1)
        _ride_start(ride, (npair, nq), ride_srcs, ride_dsts, ride_sems)

        qs = (q_ref[:, 0:128], q_ref[:, 128:256])
        acc_ref[...] = jnp.zeros_like(acc_ref)
        pt_ref[...] = jnp.zeros_like(pt_ref)
        nfull = (i * tq) // tk

        def scores(j):
            kb = k_ref[pl.ds(pl.multiple_of(j * tk, tk), tk), :]
            return tuple(_dot_nt(kb[:, 128 * h:128 * h + 128], qs[h]) for h in range(2))

        def weigh(j, h):
            rows = slice(64 * h, 64 * h + 64)
            pv = _dot(vt_ref[0, nsub * j, rows, :], pt_ref[h, 0:CHUNK, :])
            for c in range(1, nsub):
                pv = pv + _dot(vt_ref[0, nsub * j + c, rows, :], pt_ref[h, CHUNK * c:CHUNK * c + CHUNK, :])
            return pv

        def block(j, carry, diagonal):
            nxt = () if diagonal else scores(j + 1)
            pvs = [weigh(jnp.maximum(j - 1, 0), h) for h in range(2)]
            new = []
            for h in range(2):
                m, l, alpha_prev = carry[3 * h:3 * h + 3]
                st = st_ref[h]
                if diagonal:
                    st = jnp.where(_fox_mask(tk, tq, j * tk, i * tq), st, -jnp.inf)
                m_new = jnp.maximum(m, _colreduce(st, jnp.maximum))
                pt = jnp.exp(st - m_new)
                alpha = jnp.exp(m - m_new)
                rows = slice(64 * h, 64 * h + 64)
                acc_ref[rows, :] = alpha_prev * acc_ref[rows, :] + pvs[h]
                pt_ref[h] = pt.astype(BF16)
                new += [m_new, alpha * l + _colreduce(pt, jnp.add), alpha]
            for h, st in enumerate(nxt):
                st_ref[h] = st
            return tuple(new)

        for h, st in enumerate(scores(0)):
            st_ref[h] = st
        init = (jnp.full((1, tq), -jnp.inf, F32), jnp.zeros((1, tq), F32), jnp.ones((1, tq), F32)) * 2
        carry = lax.fori_loop(0, nfull, lambda j, c: block(j, c, False), init)
        m0, l0, a0, m1, l1, a1 = block(nfull, carry, True)
        for h, alpha in enumerate((a0, a1)):
            rows = slice(64 * h, 64 * h + 64)
            acc_ref[rows, :] = alpha * acc_ref[rows, :] + weigh(nfull, h)
        inv = jnp.where(_row((128, tq)) < 64, 1.0 / l0, 1.0 / l1)
        o = (acc_ref[...] * inv).T
        o_ref[...] = o
        r8 = _row((8, tq))
        lse_ref[0, 0] = jnp.where(r8 == 0, m0 + jnp.log(l0), jnp.where(r8 == 1, m1 + jnp.log(l1), 0.0))
        sz, _ = _silu_and_grad(z_ref[...])
        y_ref[...] = (o * sz).astype(BF16)
        _ride_wait(ride, (npair, nq), ride_srcs, ride_dsts, ride_sems)

    blk = pl.BlockSpec((tq, 128), lambda p, i: (i, p))
    extra = ride or _ChipExchange("gather", ())
    return pl.pallas_call(
        body, name=f"fox_fwd_{tag}", grid=(npair, nq),
        in_specs=[pl.BlockSpec((tq, 256), lambda p, i: (i, p)), pl.BlockSpec((T, 256), lambda p, i: (0, p)),
                  pl.BlockSpec((1, T // CHUNK, 128, CHUNK), lambda p, i: (p, 0, 0, 0)),
                  pl.BlockSpec((tq, 128), lambda p, i: (i, COL_CZ // 128 + p))] + extra.in_specs,
        out_specs=[blk, pl.BlockSpec((1, 1, 8, tq), lambda p, i: (p, i, 0, 0)), blk] + extra.out_specs,
        out_shape=[SDS((T, C_WIDTH), F32), SDS((npair, nq, 8, tq), F32), SDS((T, C_WIDTH), BF16)] + extra.out_shape,
        scratch_shapes=[pltpu.VMEM((128, tq), F32), pltpu.VMEM((2, tk, tq), F32), pltpu.VMEM((2, tk, tq), BF16)]
        + (extra.scratch if ride else []),
        compiler_params=pltpu.CompilerParams(dimension_semantics=("arbitrary", "arbitrary"), vmem_limit_bytes=VMEM_LIMIT,
                                             has_side_effects=bool(ride)),
    )(qt, kt, vt, proj, *extra.sources)


def _fox_bwd_prep(proj, dy, o, tag):
    T = proj.shape[0]
    tq = _tile(T, FOX_TILE)

    def body(z0_ref, z1_ref, dy_ref, o_ref, do_ref, dl_ref, dz_ref):
        sel = jnp.where((_lane((16, 128)) >> 6) == _row((16, 128)), 1.0, 0.0).astype(BF16)
        for p, z_ref in enumerate((z0_ref, z0_ref, z1_ref, z1_ref)):
            sl = slice(128 * p, 128 * p + 128)
            sz, dsz = _silu_and_grad(z_ref[:, 128 * (p % 2):128 * (p % 2) + 128])
            dyv, ov = dy_ref[:, sl], o_ref[:, sl]
            do = dyv * sz
            do_ref[:, sl] = do.astype(BF16)
            dz_ref[:, sl] = (dyv * ov * dsz).astype(BF16)
            hi, mid, lo = _split3(do * ov)
            dl_ref[p, 0] = (_dot_nt(sel, hi) + _dot_nt(sel, mid) + _dot_nt(sel, lo))[0:8, :]

    w = 256
    blk = pl.BlockSpec((tq, C_WIDTH), lambda i: (i, 0))
    return pl.pallas_call(
        body, name=f"fox_bwd_prep_{tag}", grid=(T // tq,),
        in_specs=[pl.BlockSpec((tq, w), lambda i: (i, COL_CZ // w)), pl.BlockSpec((tq, w), lambda i: (i, COL_CZ // w + 1)),
                  pl.BlockSpec((tq, C_WIDTH), lambda i: (i, (A_WIDTH + B_WIDTH) // C_WIDTH)), blk],
        out_specs=[blk, pl.BlockSpec((C_HEADS // 2, 1, 8, tq), lambda i: (0, i, 0, 0)), blk],
        out_shape=[SDS((T, C_WIDTH), BF16), SDS((C_HEADS // 2, T // tq, 8, tq), F32), SDS((T, C_WIDTH), BF16)],
        compiler_params=_params("parallel"),
    )(proj, proj, dy, o)


def _fox_bwd(qt, kt, proj, do, lse, delta, tag, ride=None):
    T = proj.shape[0]
    tq, tk = _tile(T, FOX_TILE), _tile(T, FOX_KEYS)
    nq, nk = T // tq, T // tk
    ndiag = tk // tq
    npair = C_HEADS // 2

    def body(q_ref, k_ref, v_ref, do_ref, lse_ref, dl_ref, *rest):
        ride_srcs, (dq_ref, dk_ref, dv_ref), ride_dsts, scratch, ride_sems = _ride_refs(ride, rest, 3, 4)
        dvacc_ref, sc_ref, pt_ref, ds_ref = scratch
        j = pl.program_id(1)
        first = (j * tk) // tq
        _ride_start(ride, (npair, nk), ride_srcs, ride_dsts, ride_sems)

        @pl.when(j == 0)
        def _():
            dq_ref[...] = jnp.zeros_like(dq_ref)

        dk_ref[...] = jnp.zeros_like(dk_ref)
        dvacc_ref[...] = jnp.zeros_like(dvacc_ref)
        ks = (k_ref[:, 0:128], k_ref[:, 128:256])
        kts = tuple(k.astype(F32).T.astype(BF16) for k in ks)
        vb = v_ref[...].astype(BF16)
        lo = _lane((tq, 128)) < 64

        def operands(i):
            q0 = pl.multiple_of(i * tq, tq)
            qb = q_ref[pl.ds(q0, tq), :]
            dob = do_ref[pl.ds(q0, tq), :]
            qhs = (qb[:, 0:128], qb[:, 128:256])
            dohs = (jnp.where(lo, dob, jnp.zeros_like(dob)), jnp.where(lo, jnp.zeros_like(dob), dob))
            return qhs, dohs

        def scores(i):
            qhs, dohs = operands(i)
            return tuple(_dot_nt(ks[h], qhs[h]) for h in range(2)) + tuple(_dot_nt(vb, dohs[h]) for h in range(2))

        def park(sc):
            for a, s in enumerate(sc):
                sc_ref[a] = s

        def grads(i):
            qhs, dohs = operands(i)
            dvacc_ref[...] += _dot(jnp.concatenate([pt_ref[0], pt_ref[1]], axis=1), jnp.concatenate(dohs, axis=0))
            for h in range(2):
                dk_ref[:, 128 * h:128 * h + 128] += _dot(ds_ref[h], qhs[h])
                dq_ref[h, i] += _dot(kts[h], ds_ref[h])

        def block(i, diagonal, opening):
            nxt = scores(jnp.minimum(i + 1, nq - 1))
            if not opening:
                grads(i - 1)
            lsev = lse_ref[0, i]
            dlv = dl_ref[0, i]
            for h in range(2):
                pt = jnp.exp(sc_ref[h] - lsev[h:h + 1, :])
                if diagonal:
                    pt = jnp.where(_fox_mask(tk, tq, j * tk, i * tq), pt, 0.0)
                ds_ref[h] = (pt * (sc_ref[2 + h] - dlv[h:h + 1, :])).astype(BF16)
                pt_ref[h] = pt.astype(BF16)
            park(nxt)

        park(scores(first))
        for d in range(ndiag):
            block(first + d, True, d == 0)

        def step(i, carry):
            block(i, False, False)
            return carry

        lax.fori_loop(first + ndiag, nq, step, 0)
        grads(nq - 1)
        dv_ref[...] = dvacc_ref[...].astype(BF16)
        _ride_wait(ride, (npair, nk), ride_srcs, ride_dsts, ride_sems)

    full = lambda w: pl.BlockSpec((T, w), lambda p, j: (0, p))
    stat = pl.BlockSpec((1, nq, 8, tq), lambda p, j: (p, 0, 0, 0))
    extra = ride or _ChipExchange("gather", ())
    return pl.pallas_call(
        body, name=f"fox_bwd_{tag}", grid=(npair, nk),
        in_specs=[full(256), pl.BlockSpec((tk, 256), lambda p, j: (j, p)),
                  pl.BlockSpec((tk, 128), lambda p, j: (j, COL_CV // 128 + p)), full(128), stat, stat] + extra.in_specs,
        out_specs=[pl.BlockSpec((2, nq, 128, tq), lambda p, j: (p, 0, 0, 0)), pl.BlockSpec((tk, 256), lambda p, j: (j, p)),
                   pl.BlockSpec((tk, 128), lambda p, j: (j, p))] + extra.out_specs,
        out_shape=[SDS((C_HEADS, nq, 128, tq), F32), SDS((T, C_HEADS * 128), F32), SDS((T, C_WIDTH), BF16)]
        + extra.out_shape,
        scratch_shapes=[pltpu.VMEM((tk, 128), F32), pltpu.VMEM((4, tk, tq), F32), pltpu.VMEM((2, tk, tq), BF16),
                        pltpu.VMEM((2, tk, tq), BF16)] + (extra.scratch if ride else []),
        compiler_params=pltpu.CompilerParams(dimension_semantics=("arbitrary", "arbitrary"), vmem_limit_bytes=VMEM_LIMIT,
                                             has_side_effects=bool(ride)),
    )(qt, kt, proj, do, lse, delta, *extra.sources)


def _fox_bwd_post(dqt, dkt, proj, bf, tag):
    T = proj.shape[0]
    tq = _tile(T, FOX_TILE)
    n = T // tq

    def body(dq_ref, dk_ref, fl_ref, bf_ref, oq_ref, ok_ref, ofl_ref, dbf_ref, carry_ref):
        @pl.when(pl.program_id(0) == 0)
        def _():
            carry_ref[...] = jnp.zeros_like(carry_ref)
            dbf_ref[...] = jnp.zeros_like(dbf_ref)

        lane = _lane((tq, 128))
        lo = lane < 64
        dqs = [dq_ref[h, 0].T for h in range(C_HEADS)]
        dc = jnp.zeros((tq, 128), F32)
        for h in range(C_HEADS):
            dc = dc + jnp.where(lane == h, dqs[h][:, 64:65] - dk_ref[:, 128 * h + 67:128 * h + 68], 0.0)
        utri = jnp.where(_lane((tq, tq)) >= _row((tq, tq)), 1.0, 0.0).astype(BF16)
        dlf = _dot3_left(utri, dc) + carry_ref[...]
        carry_ref[...] = dlf[0:1, :]
        dfl = jnp.where(lane < C_HEADS, dlf * _sigmoid(-(fl_ref[...] + bf_ref[...])), 0.0)
        ofl_ref[...] = dfl.astype(BF16)
        dbf_ref[...] += jnp.sum(dfl, axis=0, keepdims=True)
        for p in range(C_HEADS // 2):
            a, b = 128 * (2 * p), 128 * (2 * p + 1)
            oq_ref[:, 128 * p:128 * p + 128] = (
                jnp.where(lo, dqs[2 * p], pltpu.roll(dqs[2 * p + 1], 64, axis=1)) * Q_SCALE).astype(BF16)
            ok_ref[:, 128 * p:128 * p + 128] = jnp.where(
                lo, dk_ref[:, a:a + 128], pltpu.roll(dk_ref[:, b:b + 128], 64, axis=1)).astype(BF16)

    rev = lambda w: pl.BlockSpec((tq, w), lambda i: (n - 1 - i, 0))
    return pl.pallas_call(
        body, name=f"fox_bwd_post_{tag}", grid=(n,),
        in_specs=[pl.BlockSpec((C_HEADS, 1, 128, tq), lambda i: (0, n - 1 - i, 0, 0)), rev(C_HEADS * 128),
                  pl.BlockSpec((tq, 128), lambda i: (n - 1 - i, COL_CF // 128)), pl.BlockSpec((1, 128), lambda i: (0, 0))],
        out_specs=[rev(C_WIDTH), rev(C_WIDTH), rev(128), pl.BlockSpec((1, 128), lambda i: (0, 0))],
        out_shape=[SDS((T, C_WIDTH), BF16), SDS((T, C_WIDTH), BF16), SDS((T, 128), BF16), SDS((1, 128), F32)],
        scratch_shapes=[pltpu.VMEM((1, 128), F32)], compiler_params=_params("arbitrary"),
    )(dqt, dkt, proj, bf)


def _adamw_math(w, g, m, v):
    m = ADAM_B1 * m + (1.0 - ADAM_B1) * g
    v = ADAM_B2 * v + (1.0 - ADAM_B2) * (g * g)
    delta = -ADAM_LR * ((m / ADAM_C1) / (jnp.sqrt(v / ADAM_C2) + ADAM_EPS) + ADAM_WD * w)
    return delta, m, v


def _adamw_pair(w, m, v, ga, gb, name):
    n0 = w.shape[0]
    most = max(1, ADAMW_BLOCK_BYTES // (4 * math.prod(w.shape[1:])))
    t0 = max(t for t in range(1, min(n0, most) + 1) if n0 % t == 0)

    def body(w_ref, m_ref, v_ref, ga_ref, gb_ref, g_ref, d_ref, nm_ref, nv_ref):
        g = ga_ref[...] + gb_ref[...]
        g_ref[...] = g
        d_ref[...], nm_ref[...], nv_ref[...] = _adamw_math(w_ref[...], g, m_ref[...], v_ref[...])

    blk = pl.BlockSpec((t0,) + w.shape[1:], lambda i: (i, 0, 0))
    return pl.pallas_call(
        body, name=name, grid=(n0 // t0,), in_specs=[blk] * 5, out_specs=[blk] * 4,
        out_shape=[SDS(w.shape, F32)] * 4, compiler_params=_params("parallel"),
    )(w, m, v, ga, gb)


def _adamw_small(w, m, v, gall):
    R = w.shape[0]

    def body(w_ref, m_ref, v_ref, g_ref, go_ref, d_ref, nm_ref, nv_ref):
        g = g_ref[0]
        for k in range(1, N_DEV):
            g = g + g_ref[k]
        go_ref[...] = g
        d_ref[...], nm_ref[...], nv_ref[...] = _adamw_math(w_ref[...], g, m_ref[...], v_ref[...])

    return pl.pallas_call(body, name="adamw_small", out_shape=[SDS((R, 128), F32)] * 4,
                          compiler_params=pltpu.CompilerParams(vmem_limit_bytes=VMEM_LIMIT))(w, m, v, gall)


def _sum_chips(layers, name, layer_major):
    _, R, C = layers[0].shape
    L = len(layers)
    tc = _tile(C, 256)

    def body(*refs):
        o_ref = refs[-1]
        for l, p_ref in enumerate(refs[:-1]):
            p = [p_ref[k].astype(F32) for k in range(N_CHIPS)]
            s = ((p[0] + p[1]) + p[2]) + p[3]
            if layer_major:
                o_ref[l] = s
            else:
                o_ref[:, l, :] = s

    out = (L, R, C) if layer_major else (R, L, C)
    out_blk = (L, R, tc) if layer_major else (R, L, tc)
    return pl.pallas_call(
        body, name=name, grid=(C // tc,),
        in_specs=[pl.BlockSpec((N_CHIPS, R, tc), lambda i: (0, 0, i))] * L,
        out_specs=pl.BlockSpec(out_blk, lambda i: (0, 0, i)), out_shape=SDS(out, F32),
        compiler_params=_params("parallel"),
    )(*layers)


ANY = pl.BlockSpec(memory_space=pl.ANY)


def _mesh_pos():
    return lax.axis_index("x"), lax.axis_index("y"), lax.axis_index("c")


def _other_chips(x, y):
    return [(1 - x, y), (x, 1 - y), (1 - x, 1 - y)]


class _ChipExchange:
    def __init__(self, mode, sources):
        assert mode in ("gather", "scatter")
        self.mode, self.sources = mode, tuple(sources)
        self.n = len(self.sources)
        self.in_specs = [ANY] * self.n
        self.out_specs = [ANY] * self.n
        self.out_shape = [SDS(((N_CHIPS,) + s.shape) if mode == "gather" else s.shape, s.dtype) for s in self.sources]
        self.scratch = [pltpu.SemaphoreType.DMA((3 * self.n,)), pltpu.SemaphoreType.DMA((3 * self.n,)),
                        pltpu.SemaphoreType.DMA((self.n,))]

    def _copies(self, srcs, dsts, send_sems, recv_sems, local_sems):
        x, y, c = _mesh_pos()
        me = 2 * x + y
        view = (lambda r, chip: r) if self.mode == "gather" else (lambda r, chip: r.at[chip])
        local = [pltpu.make_async_copy(view(s, me), d.at[me], local_sems.at[a]) for a, (s, d) in enumerate(zip(srcs, dsts))]
        sends, recvs = [], []
        for j, (px, py) in enumerate(_other_chips(x, y)):
            peer = 2 * px + py
            for a, (s, d) in enumerate(zip(srcs, dsts)):
                sems = dict(send_sem=send_sems.at[self.n * j + a], recv_sem=recv_sems.at[self.n * j + a],
                            device_id=(px, py, c), device_id_type=MESH_ID)
                sends.append(pltpu.make_async_remote_copy(src_ref=view(s, peer), dst_ref=d.at[me], **sems))
                recvs.append(pltpu.make_async_remote_copy(src_ref=view(s, me), dst_ref=d.at[peer], **sems))
        return local, sends, recvs

    def start(self, srcs, dsts, sems):
        local, sends, _ = self._copies(srcs, dsts, *sems)
        for cp in local + sends:
            cp.start()

    def wait(self, srcs, dsts, sems):
        local, sends, recvs = self._copies(srcs, dsts, *sems)
        for cp in recvs:
            cp.wait_recv()
        for cp in sends:
            cp.wait_send()
        for cp in local:
            cp.wait()


def _gather_halves(w, tag):
    R, C = w.shape
    H = R // 2

    def body(w_ref, g_ref, send_sems, recv_sems, pass_send, pass_recv, local_sem):
        x, y, c = _mesh_pos()
        me = 2 * x + y
        mine, theirs = pl.ds(c * H, H), pl.ds((1 - c) * H, H)
        own = pltpu.make_async_copy(w_ref, g_ref.at[me], local_sem)
        own.start()

        def fetch(j, px, py, src, dst):
            return pltpu.make_async_remote_copy(src_ref=src, dst_ref=dst, send_sem=send_sems.at[j], recv_sem=recv_sems.at[j],
                                                device_id=(px, py, c), device_id_type=MESH_ID)

        def hand(j, rows, peer):
            return pltpu.make_async_remote_copy(src_ref=g_ref.at[peer, rows], dst_ref=g_ref.at[peer, rows],
                                                send_sem=pass_send.at[j], recv_sem=pass_recv.at[j],
                                                device_id=(x, y, 1 - c), device_id_type=MESH_ID)

        chips = _other_chips(x, y)
        sends = [fetch(j, px, py, w_ref.at[mine], g_ref.at[me, mine]) for j, (px, py) in enumerate(chips)]
        for cp in sends:
            cp.start()
        passed = []
        for j, (px, py) in enumerate(chips):
            peer = 2 * px + py
            fetch(j, px, py, w_ref.at[mine], g_ref.at[peer, mine]).wait_recv()
            passed.append(hand(j, mine, peer))
            passed[-1].start()
        for j, (px, py) in enumerate(chips):
            hand(j, theirs, 2 * px + py).wait_recv()
        for cp in sends + passed:
            cp.wait_send()
        own.wait()

    return pl.pallas_call(
        body, name=f"gather_halves_{tag}", in_specs=[ANY], out_specs=ANY, out_shape=SDS((N_CHIPS, R, C), w.dtype),
        scratch_shapes=[pltpu.SemaphoreType.DMA((3,)), pltpu.SemaphoreType.DMA((3,)), pltpu.SemaphoreType.DMA((3,)),
                        pltpu.SemaphoreType.DMA((3,)), pltpu.SemaphoreType.DMA],
        compiler_params=pltpu.CompilerParams(has_side_effects=True),
    )(w)


class _DeviceGather:
    def __init__(self, source):
        self.sources, self.n = (source,), 1
        self.in_specs, self.out_specs = [ANY], [ANY]
        self.out_shape = [SDS((N_DEV,) + source.shape, source.dtype)]
        self.scratch = [pltpu.SemaphoreType.DMA((N_DEV - 1,)), pltpu.SemaphoreType.DMA((N_DEV - 1,)),
                        pltpu.SemaphoreType.DMA((1,))]

    def _copies(self, srcs, dsts, send_sems, recv_sems, local_sems):
        (src,), (dst,) = srcs, dsts
        x, y, c = _mesh_pos()
        me = 4 * x + 2 * y + c
        local = [pltpu.make_async_copy(src, dst.at[me], local_sems.at[0])]
        sends, recvs = [], []
        for k in range(1, N_DEV):
            px, py, pc = (1 - x) if k & 4 else x, (1 - y) if k & 2 else y, (1 - c) if k & 1 else c
            sems = dict(send_sem=send_sems.at[k - 1], recv_sem=recv_sems.at[k - 1], device_id=(px, py, pc),
                        device_id_type=MESH_ID)
            sends.append(pltpu.make_async_remote_copy(src_ref=src, dst_ref=dst.at[me], **sems))
            recvs.append(pltpu.make_async_remote_copy(src_ref=src, dst_ref=dst.at[4 * px + 2 * py + pc], **sems))
        return local, sends, recvs

    start = _ChipExchange.start
    wait = _ChipExchange.wait


def _gather_devices(a, name):
    ex = _DeviceGather(a)

    def body(a_ref, g_ref, *sems):
        ex.start((a_ref,), (g_ref,), sems)
        ex.wait((a_ref,), (g_ref,), sems)

    return pl.pallas_call(
        body, name=name, in_specs=ex.in_specs, out_specs=ex.out_specs[0], out_shape=ex.out_shape[0],
        scratch_shapes=ex.scratch, compiler_params=pltpu.CompilerParams(has_side_effects=True),
    )(a)


def _swap_cores(pin, pout):
    def body(pin_ref, pout_ref, oin_ref, oout_ref, send_sems, recv_sems):
        x, y, c = _mesh_pos()
        cps = [pltpu.make_async_remote_copy(src_ref=src, dst_ref=dst, send_sem=send_sems.at[a], recv_sem=recv_sems.at[a],
                                            device_id=(x, y, 1 - c), device_id_type=MESH_ID)
               for a, (src, dst) in enumerate(((pin_ref, oin_ref), (pout_ref, oout_ref)))]
        for cp in cps:
            cp.start()
        for cp in cps:
            cp.wait()

    return pl.pallas_call(
        body, name="swap_cores", in_specs=[ANY, ANY], out_specs=[ANY, ANY],
        out_shape=[SDS(pin.shape, F32), SDS(pout.shape, F32)],
        scratch_shapes=[pltpu.SemaphoreType.DMA((2,)), pltpu.SemaphoreType.DMA((2,))],
        compiler_params=pltpu.CompilerParams(has_side_effects=True),
    )(pin, pout)


PACK_TILE = 8 * 128


def _pack_rows(size):
    return (size + PACK_TILE - 1) // PACK_TILE * 8


def _pack_small(parts):
    return jnp.concatenate([jnp.pad(p.reshape(-1), (0, (-p.size) % PACK_TILE)).reshape(-1, 128) for p in parts])


def _unpack_small(packed):
    out, row = [], 0
    for _, shape in SMALL_PARAMS:
        size = math.prod(shape)
        rows = packed[row:row + _pack_rows(size)]
        out.append(rows.reshape(-1)[:size].reshape(shape))
        row += _pack_rows(size)
    return out


def _layer_consts(l, gmlp_ln_g, gmlp_ln_b, gmlp_w_s, gmlp_b_s, hgrn_onorm_g, fox_b_f):
    causal = jnp.tril(jnp.ones((CHUNK, CHUNK), bool))
    wm = jnp.where(causal[None], gmlp_w_s[l], 0.0)
    return dict(
        lng=gmlp_ln_g[l].reshape(1, A_WIDTH), lnb=gmlp_ln_b[l].reshape(1, A_WIDTH),
        wm=wm.astype(BF16), wmt=jnp.swapaxes(wm, 1, 2).astype(BF16),
        bst=jnp.pad(gmlp_b_s[l].T, ((0, 0), (0, 128 - A_GROUPS))),
        onorm=jnp.tile(hgrn_onorm_g[l], 4).reshape(1, B_WIDTH),
        bf=jnp.pad(fox_b_f[l], (0, 128 - C_HEADS)).reshape(1, 128),
    )


def kernel(x, norm_g, w_in, w_out, gmlp_ln_g, gmlp_ln_b, gmlp_w_s, gmlp_b_s, hgrn_lb, hgrn_onorm_g, fox_b_f, final_norm_g, loss_target, m_norm_g, m_w_in, m_w_out, m_gmlp_ln_g, m_gmlp_ln_b, m_gmlp_w_s, m_gmlp_b_s, m_hgrn_lb, m_hgrn_onorm_g, m_fox_b_f, m_final_norm_g, v_norm_g, v_w_in, v_w_out, v_gmlp_ln_g, v_gmlp_ln_b, v_gmlp_w_s, v_gmlp_b_s, v_hgrn_lb, v_hgrn_onorm_g, v_fox_b_f, v_final_norm_g):
    T = x.shape[1]
    shard_in = w_in.shape[2]
    shard_out = w_out.shape[1]
    xs = x.reshape(T, D_MODEL)
    tgt = loss_target.reshape(T, D_MODEL)

    w_in_b, w_out_b = w_in.astype(BF16), w_out.astype(BF16)

    def full_w_in(gathered):
        return jnp.concatenate([gathered[k] for k in range(N_CHIPS)] + [jnp.zeros((D_MODEL, D_IN_PAD - D_IN), BF16)], axis=-1)

    lb_all = _lb_fwd(hgrn_lb)
    consts = [_layer_consts(l, gmlp_ln_g, gmlp_ln_b, gmlp_w_s, gmlp_b_s, hgrn_onorm_g, fox_b_f) for l in range(DEPTH)]

    saved = []
    xl = xs
    w_in_l = full_w_in(_gather_halves(w_in_b[0], "w_in_l0"))
    for l in range(DEPTH):
        cs = consts[l]
        tag = f"l{l}"
        h, proj = _inproj(xl, norm_g[l].reshape(1, D_MODEL), w_in_l, tag)
        ya = _gmlp_fwd(proj, cs["lng"], cs["lnb"], cs["wm"], cs["bst"], tag)
        yb, ob, s0 = _hgrn_fwd(proj, lb_all[l].reshape(1, B_WIDTH), cs["onorm"], tag)
        qt, kt, vt = _fox_prep(proj, cs["bf"], tag)
        ride = _ChipExchange("gather", (w_out_b[l],) + ((w_in_b[l + 1],) if l + 1 < DEPTH else ()))
        oc, lse, yc, *gathered = _fox_fwd(qt, kt, vt, proj, tag, ride)
        w_out_l = gathered[0].reshape(N_CHIPS * shard_out, D_MODEL)
        saved.append(dict(x=xl, h=h, proj=proj, ya=ya, yb=yb, yc=yc, ob=ob, s0=s0, qt=qt, kt=kt, oc=oc, lse=lse,
                          w_in=w_in_l, w_out=w_out_l))
        xl = _outproj(xl, ya, yb, yc, w_out_l, tag)
        if l + 1 < DEPTH:
            w_in_l = full_w_in(gathered[1])

    dx, loss_part, d_final = _loss_head(xl, final_norm_g.reshape(1, D_MODEL), tgt)

    g_small = {}
    dlb_rows, rin, rout = [None] * DEPTH, [None] * DEPTH, [None] * DEPTH
    slabs_in = None
    stack = lambda key: jnp.stack([g_small[l][key] for l in range(DEPTH)])
    for l in reversed(range(DEPTH)):
        cs, sv = consts[l], saved[l]
        tag = f"l{l}"
        proj = sv["proj"]
        dy, dw_out = _outproj_bwd(dx, sv["ya"], sv["yb"], sv["yc"], sv["w_out"], tag)
        da, dwm, dbst, dlng, dlnb = _gmlp_bwd(proj, dy, cs["lng"], cs["lnb"], cs["wm"], cs["wmt"], cs["bst"], tag)
        db, dlb_rows[l], donorm = _hgrn_bwd(proj, dy, sv["ob"], sv["s0"], lb_all[l].reshape(1, B_WIDTH), cs["onorm"], tag)
        do, delta, dzc = _fox_bwd_prep(proj, dy, sv["oc"], tag)
        slabs_out = dw_out.reshape(N_CHIPS, shard_out, D_MODEL).astype(BF16)
        ride = _ChipExchange("scatter", (slabs_out,) + ((slabs_in,) if slabs_in is not None else ()))
        dqt, dkt, dvc, *received = _fox_bwd(sv["qt"], sv["kt"], proj, do, sv["lse"], delta, tag, ride)
        rout[l] = received[0]
        if slabs_in is not None:
            rin[l + 1] = received[1]
        dqc, dkc, dflc, dbf = _fox_bwd_post(dqt, dkt, proj, cs["bf"], tag)
        g_small[l] = dict(ln_g=dlng.reshape(4, 64), ln_b=dlnb.reshape(4, 64), w_s=dwm, b_s=dbst[:, :A_GROUPS].T,
                          onorm=donorm[0, :64], bf=dbf[0, :C_HEADS])
        dproj = jnp.concatenate([da, db, dqc, dkc, dvc, dzc, dflc, jnp.zeros((T, 128), BF16)], axis=1)
        if l == 0:
            d_hgrn_lb = _lb_bwd(hgrn_lb, jnp.concatenate(dlb_rows, axis=0))
            early = _pack_small([stack("ln_g"), stack("ln_b"), stack("w_s"), stack("b_s"), d_hgrn_lb, stack("onorm"),
                                 stack("bf"), d_final.reshape(D_MODEL), loss_part.reshape(128)])
            dw_in, rearly = _dw_in(sv["h"], dproj, tag, _DeviceGather(early))
        else:
            dw_in = _dw_in(sv["h"], dproj, tag)
        slabs_in = dw_in[:N_CHIPS * shard_in].reshape(N_CHIPS, shard_in, D_MODEL).astype(BF16)
        ride = _ChipExchange("scatter", (slabs_in,)) if l == 0 else None
        dx, dng, *received = _dx_in(sv["x"], norm_g[l].reshape(1, D_MODEL), dx, dproj, sv["w_in"], tag, ride)
        if l == 0:
            rin[0] = received[0]
        g_small[l]["norm_g"] = dng.reshape(D_MODEL)
    grad_x = dx.reshape(x.shape)
    rlate = _gather_devices(_pack_small([stack("norm_g")]), "gather_norm_grads")
    rsmall = jnp.concatenate([rlate, rearly], axis=1)

    pin, pout = _sum_chips(rin, "sum_chips_w_in", False), _sum_chips(rout, "sum_chips_w_out", True)
    oin, oout = _swap_cores(pin, pout)
    to_view = lambda a: jnp.transpose(a, (2, 0, 1))
    g_w_in, d_w_in, nm_w_in, nv_w_in = [
        jnp.transpose(o, (1, 2, 0))
        for o in _adamw_pair(to_view(w_in), to_view(m_w_in), to_view(v_w_in), pin, oin, "adamw_w_in")]
    g_w_out, d_w_out, nm_w_out, nv_w_out = _adamw_pair(w_out, m_w_out, v_w_out, pout, oout, "adamw_w_out")

    small_w = [norm_g, gmlp_ln_g, gmlp_ln_b, gmlp_w_s, gmlp_b_s, hgrn_lb, hgrn_onorm_g, fox_b_f, final_norm_g]
    small_m = [m_norm_g, m_gmlp_ln_g, m_gmlp_ln_b, m_gmlp_w_s, m_gmlp_b_s, m_hgrn_lb, m_hgrn_onorm_g, m_fox_b_f, m_final_norm_g]
    small_v = [v_norm_g, v_gmlp_ln_g, v_gmlp_ln_b, v_gmlp_w_s, v_gmlp_b_s, v_hgrn_lb, v_hgrn_onorm_g, v_fox_b_f, v_final_norm_g]
    slot = [jnp.zeros((128,), F32)]
    outs = _adamw_small(_pack_small(small_w + slot), _pack_small(small_m + slot), _pack_small(small_v + slot), rsmall)
    sg, sd, sm, sv_ = [_unpack_small(o) for o in outs]
    loss = outs[0][sum(_pack_rows(math.prod(s)) for _, s in SMALL_PARAMS), 0]

    def order(big_in, big_out, small):
        return [small[0], big_in, big_out] + small[1:]

    return (loss, grad_x, *order(g_w_in, g_w_out, sg), *order(d_w_in, d_w_out, sd), *order(nm_w_in, nm_w_out, sm),
            *order(nv_w_in, nv_w_out, sv_))
```

```python
import functools
import math

import jax
import jax.numpy as jnp
from jax import lax
from jax.experimental import pallas as pl
from jax.experimental.pallas import tpu as pltpu

F32 = jnp.float32
BF16 = jnp.bfloat16
SDS = jax.ShapeDtypeStruct
MESH_ID = pl.DeviceIdType.MESH

D_MODEL = 1024
DEPTH = 2
A_WIDTH = 256
A_GROUPS = 4
B_WIDTH = 256
C_WIDTH = 512
C_HEADS = 8
D_IN = 3848
D_IN_PAD = 4096
CHUNK = 128
SUB = 16
SUB_SHIFT = 4
NORM_EPS = 1e-6
F_FLOOR = 1e-30
COL_AU, COL_AV, COL_AZ = 0, 256, 512
COL_BQ, COL_BF, COL_BI, COL_BZ = 768, 1024, 1280, 1536
COL_CQ, COL_CK, COL_CV, COL_CZ, COL_CF = 1792, 2304, 2816, 3328, 3840
HEAD_LANES = 128
Q_SCALE = 0.125
ADAM_LR, ADAM_B1, ADAM_B2, ADAM_EPS, ADAM_WD, ADAM_STEP = 0.001, 0.9, 0.999, 1e-08, 0.01, 10
ADAM_C1 = 1.0 - ADAM_B1 ** ADAM_STEP
ADAM_C2 = 1.0 - ADAM_B2 ** ADAM_STEP
VMEM_LIMIT = 56 * 1024 * 1024
ADAMW_BLOCK_BYTES = 1 << 20
N_CHIPS = 4
N_DEV = 8

SMALL_PARAMS = (
    ("norm_g", (DEPTH, D_MODEL)), ("gmlp_ln_g", (DEPTH, 4, 64)), ("gmlp_ln_b", (DEPTH, 4, 64)),
    ("gmlp_w_s", (DEPTH, 4, 128, 128)), ("gmlp_b_s", (DEPTH, 4, 128)), ("hgrn_lb", (DEPTH, 256)),
    ("hgrn_onorm_g", (DEPTH, 64)), ("fox_b_f", (DEPTH, 8)), ("final_norm_g", (D_MODEL,)),
)


def _tile(n, pref):
    t = min(n, pref)
    assert n % t == 0, (n, pref)
    return t


def _params(*sem):
    return pltpu.CompilerParams(dimension_semantics=sem, vmem_limit_bytes=VMEM_LIMIT)


def _dot(a, b):
    return jnp.dot(a, b, preferred_element_type=F32)


def _dot_nt(a, b):
    return lax.dot_general(a, b, (((1,), (1,)), ((), ())), preferred_element_type=F32)


def _dot_tn(a, b):
    return lax.dot_general(a, b, (((0,), (0,)), ((), ())), preferred_element_type=F32)


def _split3(x):
    hi = x.astype(BF16)
    r = x - hi.astype(F32)
    mid = r.astype(BF16)
    lo = (r - mid.astype(F32)).astype(BF16)
    return hi, mid, lo


def _dot3_left(c, x):
    hi, mid, lo = _split3(x)
    return _dot(c, hi) + _dot(c, mid) + _dot(c, lo)


def _sigmoid(x):
    return jax.nn.sigmoid(x)


def _silu_and_grad(x):
    s = _sigmoid(x)
    return x * s, s * (1.0 + x * (1.0 - s))


_GELU_C = math.sqrt(2.0 / math.pi)


def _gelu_and_grad(x):
    inner = _GELU_C * (x + 0.044715 * x * x * x)
    t = jnp.tanh(inner)
    y = 0.5 * x * (1.0 + t)
    dy = 0.5 * (1.0 + t) + 0.5 * x * (1.0 - t * t) * _GELU_C * (1.0 + 3.0 * 0.044715 * x * x)
    return y, dy


def _lane(shape):
    return lax.broadcasted_iota(jnp.int32, shape, 1)


def _row(shape):
    return lax.broadcasted_iota(jnp.int32, shape, 0)


def _gsum64(x):
    lo = _lane(x.shape) < 64
    s0 = jnp.sum(jnp.where(lo, x, 0.0), axis=-1, keepdims=True)
    s1 = jnp.sum(jnp.where(lo, 0.0, x), axis=-1, keepdims=True)
    return jnp.where(lo, s0, s1)


def _colreduce(x, op):
    parts = [x[r:r + 8, :] for r in range(0, x.shape[0], 8)]
    while len(parts) > 1:
        pairs = [op(parts[k], parts[k + 1]) for k in range(0, len(parts) - 1, 2)]
        parts = pairs + ([parts[-1]] if len(parts) % 2 else [])
    red = jnp.max if op is jnp.maximum else jnp.sum
    return red(parts[0], axis=0, keepdims=True)


def _block_diag64(dtype=BF16):
    r, c = _row((128, 128)), _lane((128, 128))
    return jnp.where((r >> 6) == (c >> 6), 1.0, 0.0).astype(dtype)


def _inproj(x, g, w, tag):
    T, D = x.shape
    DP = w.shape[1]
    tm = _tile(T, 512)

    def body(x_ref, g_ref, w_ref, h_ref, p_ref):
        xv = x_ref[...]
        r = lax.rsqrt(jnp.mean(xv * xv, axis=-1, keepdims=True) + NORM_EPS)
        h = (xv * r * g_ref[...]).astype(BF16)
        h_ref[...] = h
        p_ref[...] = _dot(h, w_ref[...])

    return pl.pallas_call(
        body, name=f"inproj_{tag}", grid=(T // tm,),
        in_specs=[pl.BlockSpec((tm, D), lambda i: (i, 0)), pl.BlockSpec((1, D), lambda i: (0, 0)),
                  pl.BlockSpec((D, DP), lambda i: (0, 0))],
        out_specs=[pl.BlockSpec((tm, D), lambda i: (i, 0)), pl.BlockSpec((tm, DP), lambda i: (i, 0))],
        out_shape=[SDS((T, D), BF16), SDS((T, DP), F32)],
        compiler_params=_params("parallel"),
    )(x, g, w)


def _outproj(x, ya, yb, yc, wo, tag):
    T, D = x.shape
    tm = _tile(T, 512)

    def body(x_ref, ya_ref, yb_ref, yc_ref, wo_ref, o_ref):
        acc = x_ref[...] + _dot(ya_ref[...], wo_ref[0:A_WIDTH, :])
        acc = acc + _dot(yb_ref[...], wo_ref[A_WIDTH:A_WIDTH + B_WIDTH, :])
        o_ref[...] = acc + _dot(yc_ref[...], wo_ref[A_WIDTH + B_WIDTH:, :])

    row = lambda w: pl.BlockSpec((tm, w), lambda i: (i, 0))
    return pl.pallas_call(
        body, name=f"outproj_{tag}", grid=(T // tm,),
        in_specs=[row(D), row(A_WIDTH), row(B_WIDTH), row(C_WIDTH), pl.BlockSpec(wo.shape, lambda i: (0, 0))],
        out_specs=row(D), out_shape=SDS((T, D), F32), compiler_params=_params("parallel"),
    )(x, ya, yb, yc, wo)


def _outproj_bwd(dx, ya, yb, yc, wo, tag):
    T, D = dx.shape
    DM = wo.shape[0]
    tm = _tile(T, 512)

    def body(dx_ref, ya_ref, yb_ref, yc_ref, wo_ref, dy_ref, dwo_ref):
        @pl.when(pl.program_id(0) == 0)
        def _():
            dwo_ref[...] = jnp.zeros_like(dwo_ref)

        dxb = dx_ref[...].astype(BF16)
        dy_ref[...] = _dot_nt(dxb, wo_ref[...])
        dwo_ref[0:A_WIDTH, :] += _dot_tn(ya_ref[...], dxb)
        dwo_ref[A_WIDTH:A_WIDTH + B_WIDTH, :] += _dot_tn(yb_ref[...], dxb)
        dwo_ref[A_WIDTH + B_WIDTH:, :] += _dot_tn(yc_ref[...], dxb)

    row = lambda w: pl.BlockSpec((tm, w), lambda i: (i, 0))
    return pl.pallas_call(
        body, name=f"outproj_bwd_{tag}", grid=(T // tm,),
        in_specs=[row(D), row(A_WIDTH), row(B_WIDTH), row(C_WIDTH), pl.BlockSpec(wo.shape, lambda i: (0, 0))],
        out_specs=[row(DM), pl.BlockSpec((DM, D), lambda i: (0, 0))],
        out_shape=[SDS((T, DM), F32), SDS((DM, D), F32)], compiler_params=_params("arbitrary"),
    )(dx, ya, yb, yc, wo)


def _dw_in(h, dproj, tag, ride=None):
    T, D = h.shape
    DP = dproj.shape[1]
    tm, tn = _tile(T, 1024), _tile(DP, 1024)
    grid = (DP // tn, T // tm)

    def body(h_ref, dp_ref, *rest):
        ride_srcs, (dw_ref,), ride_dsts, _, ride_sems = _ride_refs(ride, rest, 1, 0)
        _ride_start(ride, grid, ride_srcs, ride_dsts, ride_sems)

        @pl.when(pl.program_id(1) == 0)
        def _():
            dw_ref[...] = jnp.zeros_like(dw_ref)

        dw_ref[...] += _dot_tn(dp_ref[...], h_ref[...])
        _ride_wait(ride, grid, ride_srcs, ride_dsts, ride_sems)

    extra = ride or _ChipExchange("gather", ())
    out = pl.pallas_call(
        body, name=f"dw_in_{tag}", grid=grid,
        in_specs=[pl.BlockSpec((tm, D), lambda j, i: (i, 0)), pl.BlockSpec((tm, tn), lambda j, i: (i, j))] + extra.in_specs,
        out_specs=[pl.BlockSpec((tn, D), lambda j, i: (j, 0))] + extra.out_specs,
        out_shape=[SDS((DP, D), F32)] + extra.out_shape, scratch_shapes=extra.scratch if ride else [],
        compiler_params=pltpu.CompilerParams(dimension_semantics=("arbitrary", "arbitrary"), vmem_limit_bytes=VMEM_LIMIT,
                                             has_side_effects=bool(ride)),
    )(h, dproj, *extra.sources)
    return out if ride else out[0]


def _dx_in(x, g, dres, dproj, w, tag, ride=None):
    T, D = x.shape
    DP = w.shape[1]
    tm = _tile(T, 512)
    grid = (T // tm,)

    def body(x_ref, g_ref, dres_ref, dp_ref, w_ref, *rest):
        ride_srcs, (dx_ref, dg_ref), ride_dsts, _, ride_sems = _ride_refs(ride, rest, 2, 0)
        _ride_start(ride, grid, ride_srcs, ride_dsts, ride_sems)

        @pl.when(pl.program_id(0) == 0)
        def _():
            dg_ref[...] = jnp.zeros_like(dg_ref)

        dh = _dot_nt(dp_ref[...], w_ref[...])
        xv = x_ref[...]
        r = lax.rsqrt(jnp.mean(xv * xv, axis=-1, keepdims=True) + NORM_EPS)
        xh = xv * r
        dg_ref[...] += jnp.sum(dh * xh, axis=0, keepdims=True)
        dxh = dh * g_ref[...]
        dx_ref[...] = dres_ref[...] + r * (dxh - xh * jnp.mean(dxh * xh, axis=-1, keepdims=True))
        _ride_wait(ride, grid, ride_srcs, ride_dsts, ride_sems)

    extra = ride or _ChipExchange("gather", ())
    row = pl.BlockSpec((tm, D), lambda i: (i, 0))
    return pl.pallas_call(
        body, name=f"dx_in_{tag}", grid=grid,
        in_specs=[row, pl.BlockSpec((1, D), lambda i: (0, 0)), row, pl.BlockSpec((tm, DP), lambda i: (i, 0)),
                  pl.BlockSpec((D, DP), lambda i: (0, 0))] + extra.in_specs,
        out_specs=[row, pl.BlockSpec((1, D), lambda i: (0, 0))] + extra.out_specs,
        out_shape=[SDS((T, D), F32), SDS((1, D), F32)] + extra.out_shape,
        scratch_shapes=extra.scratch if ride else [],
        compiler_params=pltpu.CompilerParams(dimension_semantics=("arbitrary",), vmem_limit_bytes=VMEM_LIMIT,
                                             has_side_effects=bool(ride)),
    )(x, g, dres, dproj, w, *extra.sources)


def _loss_head(x, g, tgt):
    T, D = x.shape
    tm = _tile(T, 512)

    def body(x_ref, g_ref, t_ref, dx_ref, loss_ref, dg_ref):
        @pl.when(pl.program_id(0) == 0)
        def _():
            loss_ref[...] = jnp.zeros_like(loss_ref)
            dg_ref[...] = jnp.zeros_like(dg_ref)

        xv = x_ref[...]
        r = lax.rsqrt(jnp.mean(xv * xv, axis=-1, keepdims=True) + NORM_EPS)
        xh = xv * r
        gv = g_ref[...]
        err = xh * gv - t_ref[...]
        tok = jnp.mean(err * err, axis=-1, keepdims=True)
        loss_ref[...] += 0.5 * jnp.sum(tok, axis=0, keepdims=True)
        dy = err * (1.0 / D)
        dg_ref[...] += jnp.sum(dy * xh, axis=0, keepdims=True)
        dxh = dy * gv
        dx_ref[...] = r * (dxh - xh * jnp.mean(dxh * xh, axis=-1, keepdims=True))

    row = pl.BlockSpec((tm, D), lambda i: (i, 0))
    return pl.pallas_call(
        body, name="loss_head", grid=(T // tm,),
        in_specs=[row, pl.BlockSpec((1, D), lambda i: (0, 0)), row],
        out_specs=[row, pl.BlockSpec((1, 128), lambda i: (0, 0)), pl.BlockSpec((1, D), lambda i: (0, 0))],
        out_shape=[SDS((T, D), F32), SDS((1, 128), F32), SDS((1, D), F32)], compiler_params=_params("arbitrary"),
    )(x, g, tgt)


def _gmlp_core(u, v, lng, lnb, wm_ref, bst_ref, pair):
    ug, dug = _gelu_and_grad(u)
    vg, dvg = _gelu_and_grad(v)
    mu = _gsum64(vg) * (1.0 / 64)
    d = vg - mu
    var = _gsum64(d * d) * (1.0 / 64)
    rstd = lax.rsqrt(var + NORM_EPS)
    xh = d * rstd
    vn = xh * lng + lnb
    vnb = vn.astype(BF16)
    lo = _lane(u.shape) < 64
    g0, g1 = 2 * pair, 2 * pair + 1
    mixed = jnp.where(lo, _dot(wm_ref[g0], vnb) + bst_ref[:, g0:g0 + 1], _dot(wm_ref[g1], vnb) + bst_ref[:, g1:g1 + 1])
    return ug, dug, dvg, rstd, xh, vnb, mixed, lo


def _gmlp_fwd(proj, lng, lnb, wm, bst, tag):
    T = proj.shape[0]

    def body(u_ref, v_ref, z_ref, lng_ref, lnb_ref, wm_ref, bst_ref, y_ref):
        for pair in range(2):
            sl = slice(128 * pair, 128 * pair + 128)
            ug, _, _, _, _, _, mixed, _ = _gmlp_core(u_ref[:, sl], v_ref[:, sl], lng_ref[:, sl], lnb_ref[:, sl],
                                                     wm_ref, bst_ref, pair)
            sz, _ = _silu_and_grad(z_ref[:, sl])
            y_ref[:, sl] = (ug * mixed * sz).astype(BF16)

    col = lambda c: pl.BlockSpec((CHUNK, A_WIDTH), lambda i, c=c: (i, c // A_WIDTH))
    full = lambda a: pl.BlockSpec(a.shape, lambda i, n=a.ndim: (0,) * n)
    return pl.pallas_call(
        body, name=f"gmlp_fwd_{tag}", grid=(T // CHUNK,),
        in_specs=[col(COL_AU), col(COL_AV), col(COL_AZ), full(lng), full(lnb), full(wm), full(bst)],
        out_specs=pl.BlockSpec((CHUNK, A_WIDTH), lambda i: (i, 0)), out_shape=SDS((T, A_WIDTH), BF16),
        compiler_params=_params("parallel"),
    )(proj, proj, proj, lng, lnb, wm, bst)


def _gmlp_bwd(proj, dy, lng, lnb, wm, wmt, bst, tag):
    T = proj.shape[0]
    n = T // CHUNK

    def body(u_ref, v_ref, z_ref, dy_ref, lng_ref, lnb_ref, wm_ref, wmt_ref, bst_ref,
             da_ref, dwm_ref, dbst_ref, dlng_ref, dlnb_ref):
        @pl.when(pl.program_id(0) == 0)
        def _():
            dwm_ref[...] = jnp.zeros_like(dwm_ref)
            dbst_ref[...] = jnp.zeros_like(dbst_ref)
            dlng_ref[...] = jnp.zeros_like(dlng_ref)
            dlnb_ref[...] = jnp.zeros_like(dlnb_ref)

        lane = _lane((CHUNK, 128))
        dbst = dbst_ref[...]
        for pair in range(2):
            sl = slice(128 * pair, 128 * pair + 128)
            lng_p = lng_ref[:, sl]
            ug, dug, dvg, rstd, xh, vnb, mixed, lo = _gmlp_core(u_ref[:, sl], v_ref[:, sl], lng_p, lnb_ref[:, sl],
                                                                wm_ref, bst_ref, pair)
            sz, dsz = _silu_and_grad(z_ref[:, sl])
            dyv = dy_ref[:, sl]
            out = ug * mixed
            dz = dyv * out * dsz
            dout = dyv * sz
            du = dout * mixed * dug
            dmix = dout * ug
            g0, g1 = 2 * pair, 2 * pair + 1
            dm0 = jnp.where(lo, dmix, 0.0)
            dm1 = jnp.where(lo, 0.0, dmix)
            dbst = dbst + jnp.where(lane == g0, jnp.sum(dm0, axis=-1, keepdims=True), 0.0)
            dbst = dbst + jnp.where(lane == g1, jnp.sum(dm1, axis=-1, keepdims=True), 0.0)
            dwm_ref[g0] += _dot_nt(dm0.astype(BF16), vnb)
            dwm_ref[g1] += _dot_nt(dm1.astype(BF16), vnb)
            dmb = dmix.astype(BF16)
            dvn = jnp.where(lo, _dot(wmt_ref[g0], dmb), _dot(wmt_ref[g1], dmb))
            dlng_ref[:, sl] += jnp.sum(dvn * xh, axis=0, keepdims=True)
            dlnb_ref[:, sl] += jnp.sum(dvn, axis=0, keepdims=True)
            dxh = dvn * lng_p
            m1 = _gsum64(dxh) * (1.0 / 64)
            m2 = _gsum64(dxh * xh) * (1.0 / 64)
            dv = rstd * (dxh - m1 - xh * m2) * dvg
            da_ref[:, COL_AU + 128 * pair:COL_AU + 128 * pair + 128] = du.astype(BF16)
            da_ref[:, COL_AV + 128 * pair:COL_AV + 128 * pair + 128] = dv.astype(BF16)
            da_ref[:, COL_AZ + 128 * pair:COL_AZ + 128 * pair + 128] = dz.astype(BF16)
        dbst_ref[...] = dbst

        @pl.when(pl.program_id(0) == n - 1)
        def _():
            causal = _lane((CHUNK, CHUNK)) <= _row((CHUNK, CHUNK))
            for g in range(A_GROUPS):
                dwm_ref[g] = jnp.where(causal, dwm_ref[g], 0.0)

    col = lambda c: pl.BlockSpec((CHUNK, A_WIDTH), lambda i, c=c: (i, c // A_WIDTH))
    full = lambda a: pl.BlockSpec(a.shape, lambda i, n=a.ndim: (0,) * n)
    acc = lambda s: pl.BlockSpec(s, lambda i, n=len(s): (0,) * n)
    return pl.pallas_call(
        body, name=f"gmlp_bwd_{tag}", grid=(n,),
        in_specs=[col(COL_AU), col(COL_AV), col(COL_AZ), pl.BlockSpec((CHUNK, A_WIDTH), lambda i: (i, 0)),
                  full(lng), full(lnb), full(wm), full(wmt), full(bst)],
        out_specs=[pl.BlockSpec((CHUNK, 3 * A_WIDTH), lambda i: (i, 0)), acc((A_GROUPS, CHUNK, CHUNK)),
                   acc((CHUNK, 128)), acc((1, A_WIDTH)), acc((1, A_WIDTH))],
        out_shape=[SDS((T, 3 * A_WIDTH), BF16), SDS((A_GROUPS, CHUNK, CHUNK), F32), SDS((CHUNK, 128), F32),
                   SDS((1, A_WIDTH), F32), SDS((1, A_WIDTH), F32)],
        compiler_params=_params("arbitrary"),
    )(proj, proj, proj, dy, lng, lnb, wm, wmt, bst)


def _hgrn_consts():
    r, c = _row((CHUNK, CHUNK)), _lane((CHUNK, CHUNK))
    same = (r >> SUB_SHIFT) == (c >> SUB_SHIFT)
    lsub = jnp.where(same & (c <= r), 1.0, 0.0).astype(BF16)
    usub = jnp.where(same & (c >= r), 1.0, 0.0).astype(BF16)
    bsub = jnp.where(same, 1.0, 0.0).astype(BF16)
    return lsub, usub, bsub


def _hgrn_gates(qv, zf, lbp):
    sq, dsq = _silu_and_grad(qv)
    qt = sq * Q_SCALE
    sg = _sigmoid(zf)
    sgn = _sigmoid(-zf)
    f = lbp + (1.0 - lbp) * sg
    g = jnp.log(jnp.maximum(f, F_FLOOR))
    kf = (1.0 - lbp) * sgn
    return qt, dsq, sg, sgn, f, g, kf


def _hgrn_intra_fwd(qt, kf, b, v, mbd):
    rid = _row((SUB, 128))
    parts = []
    for s in range(SUB):
        e = jnp.exp(b - b[s:s + 1, :])
        parts.append(jnp.where(rid >= s, qt * kf[s:s + 1, :] * e, 0.0))
    a = _dot(jnp.concatenate(parts, axis=0).astype(BF16), mbd)
    o = jnp.zeros((SUB, 128), F32)
    for s in range(SUB):
        o = o + a[SUB * s:SUB * s + SUB, :] * v[s:s + 1, :]
    return o


def _hgrn_intra_bwd(qt, kf, b, v, do, mbd, rsum):
    rid = _row((SUB, 128))
    ps, das, kes, es = [], [], [], []
    for s in range(SUB):
        e = jnp.where(rid >= s, jnp.exp(b - b[s:s + 1, :]), 0.0)
        ke = kf[s:s + 1, :] * e
        es.append(e)
        kes.append(ke)
        ps.append(qt * ke)
        das.append(do * v[s:s + 1, :])
    a = _dot(jnp.concatenate(ps, axis=0).astype(BF16), mbd)
    da = _dot(jnp.concatenate(das, axis=0).astype(BF16), mbd)
    dqt = jnp.zeros((SUB, 128), F32)
    xs, ys = [], []
    for s in range(SUB):
        da_s = da[SUB * s:SUB * s + SUB, :]
        dqt = dqt + da_s * kes[s]
        xs.append(a[SUB * s:SUB * s + SUB, :] * do)
        ys.append(da_s * qt * es[s])
    dv = _dot(rsum, jnp.concatenate(xs, axis=0).astype(BF16))
    dkf = _dot(rsum, jnp.concatenate(ys, axis=0).astype(BF16))
    return dqt, dkf, dv


def _hgrn_norm_gate(o, z, onorm):
    ms = _gsum64(o * o) * (1.0 / 64)
    r = lax.rsqrt(ms + NORM_EPS)
    xh = o * r
    sz, dsz = _silu_and_grad(z)
    return xh, r, sz, dsz, xh * onorm


def _hgrn_fwd(proj, lb, onorm, tag):
    T = proj.shape[0]
    n = T // CHUNK
    nsub = CHUNK // SUB

    def body(q_ref, f_ref, i_ref, z_ref, lb_ref, on_ref, y_ref, o_ref, s0_ref, st_ref):
        @pl.when(pl.program_id(0) == 0)
        def _():
            st_ref[...] = jnp.zeros_like(st_ref)

        lsub, _, bsub = _hgrn_consts()
        mbd = _block_diag64()
        bdmask = mbd > 0
        rid = _row((CHUNK, 128))
        for pair in range(2):
            sl = slice(128 * pair, 128 * pair + 128)
            qt, _, _, _, _, g, kf = _hgrn_gates(q_ref[:, sl], f_ref[:, sl], lb_ref[:, sl])
            v = i_ref[:, sl]
            b = _dot3_left(lsub, g)
            bl = _dot3_left(bsub, g)
            qh = (qt * jnp.exp(b)).astype(BF16)
            kh = kf * jnp.exp(bl - b)
            dec = jnp.exp(bl)
            vtb = v.T.astype(BF16)
            st = st_ref[pair]
            s0_ref[0, pair] = st
            outs = []
            for sub in range(nsub):
                rs = slice(SUB * sub, SUB * sub + SUB)
                o_inter = _dot_nt(qh[rs], st.astype(BF16))
                outs.append(o_inter + _hgrn_intra_fwd(qt[rs], kf[rs], b[rs], v[rs], mbd))
                khm = jnp.where((rid >> SUB_SHIFT) == sub, kh, 0.0).astype(BF16)
                st = jnp.where(bdmask, st * dec[SUB * sub:SUB * sub + 1, :] + _dot(vtb, khm), 0.0)
            st_ref[pair] = st
            o = jnp.concatenate(outs, axis=0)
            o_ref[:, sl] = o
            _, _, sz, _, on = _hgrn_norm_gate(o, z_ref[:, sl], on_ref[:, sl])
            y_ref[:, sl] = (on * sz).astype(BF16)

    col = lambda c: pl.BlockSpec((CHUNK, B_WIDTH), lambda i, c=c: (i, c // B_WIDTH))
    full = lambda a: pl.BlockSpec(a.shape, lambda i, n=a.ndim: (0,) * n)
    return pl.pallas_call(
        body, name=f"hgrn_fwd_{tag}", grid=(n,),
        in_specs=[col(COL_BQ), col(COL_BF), col(COL_BI), col(COL_BZ), full(lb), full(onorm)],
        out_specs=[pl.BlockSpec((CHUNK, B_WIDTH), lambda i: (i, 0)), pl.BlockSpec((CHUNK, B_WIDTH), lambda i: (i, 0)),
                   pl.BlockSpec((1, 2, 128, 128), lambda i: (i, 0, 0, 0))],
        out_shape=[SDS((T, B_WIDTH), BF16), SDS((T, B_WIDTH), F32), SDS((n, 2, 128, 128), F32)],
        scratch_shapes=[pltpu.VMEM((2, 128, 128), F32)], compiler_params=_params("arbitrary"),
    )(proj, proj, proj, proj, lb, onorm)


def _hgrn_bwd(proj, dy, o_saved, s0, lb, onorm, tag):
    T = proj.shape[0]
    n = T // CHUNK
    nsub = CHUNK // SUB

    def body(q_ref, f_ref, i_ref, z_ref, dy_ref, o_ref, s0_ref, lb_ref, on_ref,
             db_ref, dlb_ref, don_ref, dst_ref, sts_ref):
        @pl.when(pl.program_id(0) == 0)
        def _():
            dst_ref[...] = jnp.zeros_like(dst_ref)
            dlb_ref[...] = jnp.zeros_like(dlb_ref)
            don_ref[...] = jnp.zeros_like(don_ref)

        lsub, usub, bsub = _hgrn_consts()
        mbd = _block_diag64()
        bdmask = mbd > 0
        rsum = jnp.where((_lane((SUB, SUB * SUB)) >> SUB_SHIFT) == _row((SUB, SUB * SUB)), 1.0, 0.0).astype(BF16)
        for pair in range(2):
            sl = slice(128 * pair, 128 * pair + 128)
            lbp = lb_ref[:, sl]
            qv, zf = q_ref[:, sl], f_ref[:, sl]
            qt, dsq, sg, sgn, f, g, kf = _hgrn_gates(qv, zf, lbp)
            v = i_ref[:, sl]
            b = _dot3_left(lsub, g)
            bl = _dot3_left(bsub, g)
            eb = jnp.exp(b)
            ekb = jnp.exp(bl - b)
            qhb = (qt * eb).astype(BF16)
            khb = (kf * ekb).astype(BF16)
            dec = jnp.exp(bl)
            vb = v.astype(BF16)
            onp = on_ref[:, sl]
            ov = o_ref[:, sl]
            xh, r, sz, dsz, on = _hgrn_norm_gate(ov, z_ref[:, sl], onp)
            dyv = dy_ref[:, sl]
            dz = dyv * on * dsz
            don = dyv * sz
            cn = jnp.sum(don * xh, axis=0, keepdims=True)
            don_ref[...] += cn + pltpu.roll(cn, 64, axis=1)
            dxo = don * onp
            do = r * (dxo - xh * (_gsum64(dxo * xh) * (1.0 / 64)))
            dob = do.astype(BF16)
            st = s0_ref[0, pair]
            for sub in range(nsub):
                rs = slice(SUB * sub, SUB * sub + SUB)
                sts_ref[sub] = st
                st = jnp.where(bdmask, st * dec[SUB * sub:SUB * sub + 1, :] + _dot_tn(vb[rs], khb[rs]), 0.0)
            gst = dst_ref[pair]
            dqt_p, dkf_p, dv_p, dbl_p = [None] * nsub, [None] * nsub, [None] * nsub, [None] * nsub
            for sub in reversed(range(nsub)):
                rs = slice(SUB * sub, SUB * sub + SUB)
                st_in = sts_ref[sub]
                gb = gst.astype(BF16)
                dqh = _dot(dob[rs], st_in.astype(BF16))
                dkh = _dot(vb[rs], gb)
                dv_inter = _dot_nt(khb[rs], gb)
                ddec = jnp.sum(gst * st_in, axis=0, keepdims=True)
                dec_row = dec[SUB * sub:SUB * sub + 1, :]
                gst = jnp.where(bdmask, gst * dec_row + _dot_tn(dob[rs], qhb[rs]), 0.0)
                dqt_i, dkf_i, dv_i = _hgrn_intra_bwd(qt[rs], kf[rs], b[rs], v[rs], do[rs], mbd, rsum)
                dkf_inter = dkh * ekb[rs]
                dqt_p[sub] = dqh * eb[rs] + dqt_i
                dkf_p[sub] = dkf_inter + dkf_i
                dv_p[sub] = dv_inter + dv_i
                row = jnp.sum(kf[rs] * dkf_inter, axis=0, keepdims=True) + ddec * dec_row
                dbl_p[sub] = jnp.broadcast_to(row, (SUB, 128))
            dst_ref[pair] = gst
            dqt = jnp.concatenate(dqt_p, axis=0)
            dkf = jnp.concatenate(dkf_p, axis=0)
            dv = jnp.concatenate(dv_p, axis=0)
            dg = _dot3_left(usub, qt * dqt - kf * dkf) + jnp.concatenate(dbl_p, axis=0)
            df = jnp.where(f > F_FLOOR, dg / f, 0.0)
            dlb_ref[:, sl] += jnp.sum(df * (1.0 - sg) - dkf * sgn, axis=0, keepdims=True)
            dfl = (1.0 - lbp) * sg * sgn * (df - dkf)
            dq = dqt * Q_SCALE * dsq
            db_ref[:, 0 * B_WIDTH + 128 * pair:0 * B_WIDTH + 128 * pair + 128] = dq.astype(BF16)
            db_ref[:, 1 * B_WIDTH + 128 * pair:1 * B_WIDTH + 128 * pair + 128] = dfl.astype(BF16)
            db_ref[:, 2 * B_WIDTH + 128 * pair:2 * B_WIDTH + 128 * pair + 128] = dv.astype(BF16)
            db_ref[:, 3 * B_WIDTH + 128 * pair:3 * B_WIDTH + 128 * pair + 128] = dz.astype(BF16)

    rev = lambda c: pl.BlockSpec((CHUNK, B_WIDTH), lambda i, c=c: (n - 1 - i, c // B_WIDTH))
    full = lambda a: pl.BlockSpec(a.shape, lambda i, n_=a.ndim: (0,) * n_)
    acc = lambda s: pl.BlockSpec(s, lambda i, n_=len(s): (0,) * n_)
    return pl.pallas_call(
        body, name=f"hgrn_bwd_{tag}", grid=(n,),
        in_specs=[rev(COL_BQ), rev(COL_BF), rev(COL_BI), rev(COL_BZ),
                  pl.BlockSpec((CHUNK, B_WIDTH), lambda i: (n - 1 - i, 1)),
                  pl.BlockSpec((CHUNK, B_WIDTH), lambda i: (n - 1 - i, 0)),
                  pl.BlockSpec((1, 2, 128, 128), lambda i: (n - 1 - i, 0, 0, 0)), full(lb), full(onorm)],
        out_specs=[pl.BlockSpec((CHUNK, 4 * B_WIDTH), lambda i: (n - 1 - i, 0)), acc((1, B_WIDTH)), acc((1, 128))],
        out_shape=[SDS((T, 4 * B_WIDTH), BF16), SDS((1, B_WIDTH), F32), SDS((1, 128), F32)],
        scratch_shapes=[pltpu.VMEM((2, 128, 128), F32), pltpu.VMEM((nsub, 128, 128), F32)],
        compiler_params=_params("arbitrary"),
    )(proj, proj, proj, proj, dy, o_saved, s0, lb, onorm)


def _lb_fwd(hgrn_lb):
    assert hgrn_lb.shape[0] == 2

    def body(x_ref, o_ref):
        x0, x1 = x_ref[0:1, :], x_ref[1:2, :]
        m = jnp.maximum(x0, x1)
        e0, e1 = jnp.exp(x0 - m), jnp.exp(x1 - m)
        p0, p1 = e0 / (e0 + e1), e1 / (e0 + e1)
        o_ref[0:1, :] = jnp.clip(p0 - p0, 0.0, 1.0 - 1e-6)
        o_ref[1:2, :] = jnp.clip((p0 + p1) - p0, 0.0, 1.0 - 1e-6)

    return pl.pallas_call(body, name="lb_fwd", out_shape=SDS(hgrn_lb.shape, F32))(hgrn_lb)


def _lb_bwd(hgrn_lb, dlb):
    def body(x_ref, d_ref, o_ref):
        x0, x1 = x_ref[0:1, :], x_ref[1:2, :]
        m = jnp.maximum(x0, x1)
        e0, e1 = jnp.exp(x0 - m), jnp.exp(x1 - m)
        p0, p1 = e0 / (e0 + e1), e1 / (e0 + e1)
        val = (p0 + p1) - p0
        dp1 = jnp.where((val > 0.0) & (val < 1.0 - 1e-6), d_ref[1:2, :], 0.0)
        inner = p1 * dp1
        o_ref[0:1, :] = p0 * (0.0 - inner)
        o_ref[1:2, :] = p1 * (dp1 - inner)

    return pl.pallas_call(body, name="lb_bwd", out_shape=SDS(hgrn_lb.shape, F32))(hgrn_lb, dlb)


def _fox_prep(proj, bf, tag):
    T = proj.shape[0]
    n = T // CHUNK

    def body(q0_ref, q1_ref, k0_ref, k1_ref, v0_ref, v1_ref, fl_ref, bf_ref, qo_ref, ko_ref, vt_ref, carry_ref):
        for p, v_ref in enumerate((v0_ref, v0_ref, v1_ref, v1_ref)):
            vt_ref[p, 0] = v_ref[:, 128 * (p % 2):128 * (p % 2) + 128].T.astype(BF16)

        @pl.when(pl.program_id(0) == 0)
        def _():
            carry_ref[...] = jnp.zeros_like(carry_ref)

        ltri = jnp.where(_lane((CHUNK, CHUNK)) <= _row((CHUNK, CHUNK)), 1.0, 0.0).astype(BF16)
        lf = jax.nn.log_sigmoid(fl_ref[...] + bf_ref[...])
        c = _dot3_left(ltri, lf) + carry_ref[...]
        carry_ref[...] = c[CHUNK - 1:CHUNK, :]
        lane = _lane((CHUNK, 128))
        feat = lane < 64
        ones_q = (lane >= 67) & (lane <= 69)
        ones_k = (lane >= 64) & (lane <= 66)
        qrefs, krefs = (q0_ref, q1_ref), (k0_ref, k1_ref)
        for h in range(C_HEADS):
            blk = slice(128 * ((h // 2) % 2), 128 * ((h // 2) % 2) + 128)
            qp, kp = qrefs[h // 4][:, blk], krefs[h // 4][:, blk]
            if h % 2:
                qp, kp = pltpu.roll(qp, 64, axis=1), pltpu.roll(kp, 64, axis=1)
            ch = jnp.broadcast_to(c[:, h:h + 1], (CHUNK, 128))
            hi = ch.astype(BF16).astype(F32)
            r1 = ch - hi
            mid = r1.astype(BF16).astype(F32)
            lo = r1 - mid
            aq = jnp.where(lane == 64, hi, jnp.where(lane == 65, mid, jnp.where(lane == 66, lo,
                           jnp.where(ones_q, 1.0, 0.0))))
            ak = jnp.where(lane == 67, -hi, jnp.where(lane == 68, -mid, jnp.where(lane == 69, -lo,
                           jnp.where(ones_k, 1.0, 0.0))))
            qo_ref[:, 128 * h:128 * h + 128] = jnp.where(feat, qp * Q_SCALE, aq).astype(BF16)
            ko_ref[:, 128 * h:128 * h + 128] = jnp.where(feat, kp, ak).astype(BF16)

    w = 256
    col = lambda c: pl.BlockSpec((CHUNK, w), lambda i, c=c: (i, c // w))
    return pl.pallas_call(
        body, name=f"fox_prep_{tag}", grid=(n,),
        in_specs=[col(COL_CQ), col(COL_CQ + w), col(COL_CK), col(COL_CK + w), col(COL_CV), col(COL_CV + w),
                  pl.BlockSpec((CHUNK, 128), lambda i: (i, COL_CF // 128)), pl.BlockSpec((1, 128), lambda i: (0, 0))],
        out_specs=[pl.BlockSpec((CHUNK, C_HEADS * 128), lambda i: (i, 0))] * 2
        + [pl.BlockSpec((C_HEADS // 2, 1, 128, CHUNK), lambda i: (0, i, 0, 0))],
        out_shape=[SDS((T, C_HEADS * 128), BF16)] * 2 + [SDS((C_HEADS // 2, n, 128, CHUNK), BF16)],
        scratch_shapes=[pltpu.VMEM((1, 128), F32)], compiler_params=_params("arbitrary"),
    )(proj, proj, proj, proj, proj, proj, proj, bf)


FOX_TILE = 512
FOX_KEYS = 512


def _fox_mask(tk, tq, k0, q0):
    return (_row((tk, tq)) + (k0 - q0)) <= _lane((tk, tq))


def _ride_refs(ride, rest, n_out, n_scratch):
    n = ride.n if ride else 0
    srcs, rest = rest[:n], rest[n:]
    outs, rest = rest[:n_out], rest[n_out:]
    dsts, rest = rest[:n], rest[n:]
    return srcs, outs, dsts, rest[:n_scratch], rest[n_scratch:]


def _ride_start(ride, grid, srcs, dsts, sems):
    if ride:
        first = functools.reduce(lambda a, b: a & b, [pl.program_id(d) == 0 for d in range(len(grid))])
        pl.when(first)(lambda: ride.start(srcs, dsts, sems))


def _ride_wait(ride, grid, srcs, dsts, sems):
    if ride:
        last = functools.reduce(lambda a, b: a & b, [pl.program_id(d) == n - 1 for d, n in enumerate(grid)])
        pl.when(last)(lambda: ride.wait(srcs, dsts, sems))


def _fox_fwd(qt, kt, vt, proj, tag, ride=None):
    T = proj.shape[0]
    tq, tk = _tile(T, FOX_TILE), _tile(T, FOX_KEYS)
    nq, nsub = T // tq, tk // CHUNK
    npair = C_HEADS // 2

    def body(q_ref, k_ref, vt_ref, z_ref, *rest):
        ride_srcs, (o_ref, lse_ref, y_ref), ride_dsts, (acc_ref, st_ref, pt_ref), ride_sems = _ride_refs(ride, rest, 3, 3)
        i = pl.program_id(1)
        _ride_start(ride, (npair, nq), ride_srcs, ride_dsts, ride_sems)

        qs = (q_ref[:, 0:128], q_ref[:, 128:256])
        acc_ref[...] = jnp.zeros_like(acc_ref)
        pt_ref[...] = jnp.zeros_like(pt_ref)
        nfull = (i * tq) // tk

        def scores(j):
            kb = k_ref[pl.ds(pl.multiple_of(j * tk, tk), tk), :]
            return tuple(_dot_nt(kb[:, 128 * h:128 * h + 128], qs[h]) for h in range(2))

        def weigh(j, h):
            rows = slice(64 * h, 64 * h + 64)
            pv = _dot(vt_ref[0, nsub * j, rows, :], pt_ref[h, 0:CHUNK, :])
            for c in range(1, nsub):
                pv = pv + _dot(vt_ref[0, nsub * j + c, rows, :], pt_ref[h, CHUNK * c:CHUNK * c + CHUNK, :])
            return pv

        def block(j, carry, diagonal):
            nxt = () if diagonal else scores(j + 1)
            pvs = [weigh(jnp.maximum(j - 1, 0), h) for h in range(2)]
            new = []
            for h in range(2):
                m, l, alpha_prev = carry[3 * h:3 * h + 3]
                st = st_ref[h]
                if diagonal:
                    st = jnp.where(_fox_mask(tk, tq, j * tk, i * tq), st, -jnp.inf)
                m_new = jnp.maximum(m, _colreduce(st, jnp.maximum))
                pt = jnp.exp(st - m_new)
                alpha = jnp.exp(m - m_new)
                rows = slice(64 * h, 64 * h + 64)
                acc_ref[rows, :] = alpha_prev * acc_ref[rows, :] + pvs[h]
                pt_ref[h] = pt.astype(BF16)
                new += [m_new, alpha * l + _colreduce(pt, jnp.add), alpha]
            for h, st in enumerate(nxt):
                st_ref[h] = st
            return tuple(new)

        for h, st in enumerate(scores(0)):
            st_ref[h] = st
        init = (jnp.full((1, tq), -jnp.inf, F32), jnp.zeros((1, tq), F32), jnp.ones((1, tq), F32)) * 2
        carry = lax.fori_loop(0, nfull, lambda j, c: block(j, c, False), init)
        m0, l0, a0, m1, l1, a1 = block(nfull, carry, True)
        for h, alpha in enumerate((a0, a1)):
            rows = slice(64 * h, 64 * h + 64)
            acc_ref[rows, :] = alpha * acc_ref[rows, :] + weigh(nfull, h)
        inv = jnp.where(_row((128, tq)) < 64, 1.0 / l0, 1.0 / l1)
        o = (acc_ref[...] * inv).T
        o_ref[...] = o
        r8 = _row((8, tq))
        lse_ref[0, 0] = jnp.where(r8 == 0, m0 + jnp.log(l0), jnp.where(r8 == 1, m1 + jnp.log(l1), 0.0))
        sz, _ = _silu_and_grad(z_ref[...])
        y_ref[...] = (o * sz).astype(BF16)
        _ride_wait(ride, (npair, nq), ride_srcs, ride_dsts, ride_sems)

    blk = pl.BlockSpec((tq, 128), lambda p, i: (i, p))
    extra = ride or _ChipExchange("gather", ())
    return pl.pallas_call(
        body, name=f"fox_fwd_{tag}", grid=(npair, nq),
        in_specs=[pl.BlockSpec((tq, 256), lambda p, i: (i, p)), pl.BlockSpec((T, 256), lambda p, i: (0, p)),
                  pl.BlockSpec((1, T // CHUNK, 128, CHUNK), lambda p, i: (p, 0, 0, 0)),
                  pl.BlockSpec((tq, 128), lambda p, i: (i, COL_CZ // 128 + p))] + extra.in_specs,
        out_specs=[blk, pl.BlockSpec((1, 1, 8, tq), lambda p, i: (p, i, 0, 0)), blk] + extra.out_specs,
        out_shape=[SDS((T, C_WIDTH), F32), SDS((npair, nq, 8, tq), F32), SDS((T, C_WIDTH), BF16)] + extra.out_shape,
        scratch_shapes=[pltpu.VMEM((128, tq), F32), pltpu.VMEM((2, tk, tq), F32), pltpu.VMEM((2, tk, tq), BF16)]
        + (extra.scratch if ride else []),
        compiler_params=pltpu.CompilerParams(dimension_semantics=("arbitrary", "arbitrary"), vmem_limit_bytes=VMEM_LIMIT,
                                             has_side_effects=bool(ride)),
    )(qt, kt, vt, proj, *extra.sources)


def _fox_bwd_prep(proj, dy, o, qt, tag):
    T = proj.shape[0]
    tq = _tile(T, FOX_TILE)
    nq = T // tq

    def body(z0_ref, z1_ref, dy_ref, o_ref, q_ref, do_ref, dl_ref, dz_ref, dot_ref, qt_ref):
        sel = jnp.where((_lane((16, 128)) >> 6) == _row((16, 128)), 1.0, 0.0).astype(BF16)
        for p, z_ref in enumerate((z0_ref, z0_ref, z1_ref, z1_ref)):
            sl = slice(128 * p, 128 * p + 128)
            sz, dsz = _silu_and_grad(z_ref[:, 128 * (p % 2):128 * (p % 2) + 128])
            dyv, ov = dy_ref[:, sl], o_ref[:, sl]
            do = dyv * sz
            do_ref[:, sl] = do.astype(BF16)
            dot_ref[p, 0] = do.T.astype(BF16)
            dz_ref[:, sl] = (dyv * ov * dsz).astype(BF16)
            hi, mid, lo = _split3(do * ov)
            dl_ref[p, 0] = (_dot_nt(sel, hi) + _dot_nt(sel, mid) + _dot_nt(sel, lo))[0:8, :]
        for h in range(C_HEADS):
            qt_ref[h, 0] = q_ref[:, 128 * h:128 * h + 128].astype(F32).T.astype(BF16)

    w = 256
    blk = pl.BlockSpec((tq, C_WIDTH), lambda i: (i, 0))
    return pl.pallas_call(
        body, name=f"fox_bwd_prep_{tag}", grid=(nq,),
        in_specs=[pl.BlockSpec((tq, w), lambda i: (i, COL_CZ // w)), pl.BlockSpec((tq, w), lambda i: (i, COL_CZ // w + 1)),
                  pl.BlockSpec((tq, C_WIDTH), lambda i: (i, (A_WIDTH + B_WIDTH) // C_WIDTH)), blk,
                  pl.BlockSpec((tq, C_HEADS * 128), lambda i: (i, 0))],
        out_specs=[blk, pl.BlockSpec((C_HEADS // 2, 1, 8, tq), lambda i: (0, i, 0, 0)), blk,
                   pl.BlockSpec((C_HEADS // 2, 1, 128, tq), lambda i: (0, i, 0, 0)),
                   pl.BlockSpec((C_HEADS, 1, 128, tq), lambda i: (0, i, 0, 0))],
        out_shape=[SDS((T, C_WIDTH), BF16), SDS((C_HEADS // 2, nq, 8, tq), F32), SDS((T, C_WIDTH), BF16),
                   SDS((C_HEADS // 2, nq, 128, tq), BF16), SDS((C_HEADS, nq, 128, tq), BF16)],
        compiler_params=_params("parallel"),
    )(proj, proj, dy, o, qt)


def _fox_bwd(qt, kt, proj, do, lse, delta, dot, qtr, tag, ride=None):
    T = proj.shape[0]
    tq, tk = _tile(T, FOX_TILE), _tile(T, FOX_KEYS)
    nq, nk = T // tq, T // tk
    assert tq == tk
    npair = C_HEADS // 2

    def body(q_ref, k_ref, v_ref, do_ref, lse_ref, dl_ref, dot_ref, qtr_ref, *rest):
        ride_srcs, (dq_ref, dk_ref, dv_ref), ride_dsts, scratch, ride_sems = _ride_refs(ride, rest, 3, 5)
        dvt_ref, dkt_ref, sc_ref, pt_ref, ds_ref = scratch
        j = pl.program_id(1)
        first = (j * tk) // tq
        _ride_start(ride, (npair, nk), ride_srcs, ride_dsts, ride_sems)

        @pl.when(j == 0)
        def _():
            dq_ref[...] = jnp.zeros_like(dq_ref)

        dkt_ref[...] = jnp.zeros_like(dkt_ref)
        dvt_ref[...] = jnp.zeros_like(dvt_ref)
        ks = (k_ref[:, 0:128], k_ref[:, 128:256])
        kts = tuple(k.astype(F32).T.astype(BF16) for k in ks)
        vb = v_ref[...].astype(BF16)
        lo = _lane((tq, 128)) < 64

        def operands(i):
            q0 = pl.multiple_of(i * tq, tq)
            qb = q_ref[pl.ds(q0, tq), :]
            dob = do_ref[pl.ds(q0, tq), :]
            qhs = (qb[:, 0:128], qb[:, 128:256])
            dohs = (jnp.where(lo, dob, jnp.zeros_like(dob)), jnp.where(lo, jnp.zeros_like(dob), dob))
            return qhs, dohs

        def scores(i):
            qhs, dohs = operands(i)
            return tuple(_dot_nt(ks[h], qhs[h]) for h in range(2)) + tuple(_dot_nt(vb, dohs[h]) for h in range(2))

        def park(sc, slot):
            for a, s in enumerate(sc):
                sc_ref[slot, a] = s

        def grads(i):
            for h in range(2):
                rows = slice(64 * h, 64 * h + 64)
                dvt_ref[rows, :] += _dot_nt(dot_ref[0, i, rows, :], pt_ref[h])
                dkt_ref[h] += _dot_nt(qtr_ref[h, i], ds_ref[h])
                dq_ref[h, i] += _dot(kts[h], ds_ref[h])

        def block(i, slot, diagonal, opening):
            park(scores(jnp.minimum(i + 1, nq - 1)), 1 - slot)
            if not opening:
                grads(i - 1)
            lsev = lse_ref[0, i]
            dlv = dl_ref[0, i]
            for h in range(2):
                pt = jnp.exp(sc_ref[slot, h] - lsev[h:h + 1, :])
                if diagonal:
                    pt = jnp.where(_fox_mask(tk, tq, j * tk, i * tq), pt, 0.0)
                ds_ref[h] = (pt * (sc_ref[slot, 2 + h] - dlv[h:h + 1, :])).astype(BF16)
                pt_ref[h] = pt.astype(BF16)

        park(scores(first), 0)
        block(first, 0, True, True)
        rest = nq - 1 - first

        def two_steps(t, carry):
            block(first + 1 + 2 * t, 1, False, False)
            block(first + 2 + 2 * t, 0, False, False)
            return carry

        lax.fori_loop(0, rest // 2, two_steps, 0)
        pl.when(rest % 2 == 1)(lambda: block(nq - 1, 1, False, False))
        grads(nq - 1)
        dv_ref[...] = dvt_ref[...].T.astype(BF16)
        for h in range(2):
            dk_ref[:, 128 * h:128 * h + 128] = dkt_ref[h].T
        _ride_wait(ride, (npair, nk), ride_srcs, ride_dsts, ride_sems)

    full = lambda w: pl.BlockSpec((T, w), lambda p, j: (0, p))
    stat = pl.BlockSpec((1, nq, 8, tq), lambda p, j: (p, 0, 0, 0))
    extra = ride or _ChipExchange("gather", ())
    return pl.pallas_call(
        body, name=f"fox_bwd_{tag}", grid=(npair, nk),
        in_specs=[full(256), pl.BlockSpec((tk, 256), lambda p, j: (j, p)),
                  pl.BlockSpec((tk, 128), lambda p, j: (j, COL_CV // 128 + p)), full(128), stat, stat,
                  pl.BlockSpec((1, nq, 128, tq), lambda p, j: (p, 0, 0, 0)),
                  pl.BlockSpec((2, nq, 128, tq), lambda p, j: (p, 0, 0, 0))] + extra.in_specs,
        out_specs=[pl.BlockSpec((2, nq, 128, tq), lambda p, j: (p, 0, 0, 0)), pl.BlockSpec((tk, 256), lambda p, j: (j, p)),
                   pl.BlockSpec((tk, 128), lambda p, j: (j, p))] + extra.out_specs,
        out_shape=[SDS((C_HEADS, nq, 128, tq), F32), SDS((T, C_HEADS * 128), F32), SDS((T, C_WIDTH), BF16)]
        + extra.out_shape,
        scratch_shapes=[pltpu.VMEM((128, tk), F32), pltpu.VMEM((2, 128, tk), F32), pltpu.VMEM((2, 4, tk, tq), F32),
                        pltpu.VMEM((2, tk, tq), BF16), pltpu.VMEM((2, tk, tq), BF16)] + (extra.scratch if ride else []),
        compiler_params=pltpu.CompilerParams(dimension_semantics=("arbitrary", "arbitrary"), vmem_limit_bytes=VMEM_LIMIT,
                                             has_side_effects=bool(ride)),
    )(qt, kt, proj, do, lse, delta, dot, qtr, *extra.sources)


def _fox_bwd_post(dqt, dkt, proj, bf, tag):
    T = proj.shape[0]
    tq = _tile(T, FOX_TILE)
    n = T // tq

    def body(dq_ref, dk_ref, fl_ref, bf_ref, oq_ref, ok_ref, ofl_ref, dbf_ref, carry_ref):
        @pl.when(pl.program_id(0) == 0)
        def _():
            carry_ref[...] = jnp.zeros_like(carry_ref)
            dbf_ref[...] = jnp.zeros_like(dbf_ref)

        lane = _lane((tq, 128))
        lo = lane < 64
        dqs = [dq_ref[h, 0].T for h in range(C_HEADS)]
        dc = jnp.zeros((tq, 128), F32)
        for h in range(C_HEADS):
            dc = dc + jnp.where(lane == h, dqs[h][:, 64:65] - dk_ref[:, 128 * h + 67:128 * h + 68], 0.0)
        utri = jnp.where(_lane((tq, tq)) >= _row((tq, tq)), 1.0, 0.0).astype(BF16)
        dlf = _dot3_left(utri, dc) + carry_ref[...]
        carry_ref[...] = dlf[0:1, :]
        dfl = jnp.where(lane < C_HEADS, dlf * _sigmoid(-(fl_ref[...] + bf_ref[...])), 0.0)
        ofl_ref[...] = dfl.astype(BF16)
        dbf_ref[...] += jnp.sum(dfl, axis=0, keepdims=True)
        for p in range(C_HEADS // 2):
            a, b = 128 * (2 * p), 128 * (2 * p + 1)
            oq_ref[:, 128 * p:128 * p + 128] = (
                jnp.where(lo, dqs[2 * p], pltpu.roll(dqs[2 * p + 1], 64, axis=1)) * Q_SCALE).astype(BF16)
            ok_ref[:, 128 * p:128 * p + 128] = jnp.where(
                lo, dk_ref[:, a:a + 128], pltpu.roll(dk_ref[:, b:b + 128], 64, axis=1)).astype(BF16)

    rev = lambda w: pl.BlockSpec((tq, w), lambda i: (n - 1 - i, 0))
    return pl.pallas_call(
        body, name=f"fox_bwd_post_{tag}", grid=(n,),
        in_specs=[pl.BlockSpec((C_HEADS, 1, 128, tq), lambda i: (0, n - 1 - i, 0, 0)), rev(C_HEADS * 128),
                  pl.BlockSpec((tq, 128), lambda i: (n - 1 - i, COL_CF // 128)), pl.BlockSpec((1, 128), lambda i: (0, 0))],
        out_specs=[rev(C_WIDTH), rev(C_WIDTH), rev(128), pl.BlockSpec((1, 128), lambda i: (0, 0))],
        out_shape=[SDS((T, C_WIDTH), BF16), SDS((T, C_WIDTH), BF16), SDS((T, 128), BF16), SDS((1, 128), F32)],
        scratch_shapes=[pltpu.VMEM((1, 128), F32)], compiler_params=_params("arbitrary"),
    )(dqt, dkt, proj, bf)


def _adamw_math(w, g, m, v):
    m = ADAM_B1 * m + (1.0 - ADAM_B1) * g
    v = ADAM_B2 * v + (1.0 - ADAM_B2) * (g * g)
    delta = -ADAM_LR * ((m / ADAM_C1) / (jnp.sqrt(v / ADAM_C2) + ADAM_EPS) + ADAM_WD * w)
    return delta, m, v


def _adamw_pair(w, m, v, ga, gb, name):
    n0 = w.shape[0]
    most = max(1, ADAMW_BLOCK_BYTES // (4 * math.prod(w.shape[1:])))
    t0 = max(t for t in range(1, min(n0, most) + 1) if n0 % t == 0)

    def body(w_ref, m_ref, v_ref, ga_ref, gb_ref, g_ref, d_ref, nm_ref, nv_ref):
        g = ga_ref[...] + gb_ref[...]
        g_ref[...] = g
        d_ref[...], nm_ref[...], nv_ref[...] = _adamw_math(w_ref[...], g, m_ref[...], v_ref[...])

    blk = pl.BlockSpec((t0,) + w.shape[1:], lambda i: (i, 0, 0))
    return pl.pallas_call(
        body, name=name, grid=(n0 // t0,), in_specs=[blk] * 5, out_specs=[blk] * 4,
        out_shape=[SDS(w.shape, F32)] * 4, compiler_params=_params("parallel"),
    )(w, m, v, ga, gb)


def _adamw_small(w, m, v, gall):
    R = w.shape[0]

    def body(w_ref, m_ref, v_ref, g_ref, go_ref, d_ref, nm_ref, nv_ref):
        g = g_ref[0]
        for k in range(1, N_DEV):
            g = g + g_ref[k]
        go_ref[...] = g
        d_ref[...], nm_ref[...], nv_ref[...] = _adamw_math(w_ref[...], g, m_ref[...], v_ref[...])

    return pl.pallas_call(body, name="adamw_small", out_shape=[SDS((R, 128), F32)] * 4,
                          compiler_params=pltpu.CompilerParams(vmem_limit_bytes=VMEM_LIMIT))(w, m, v, gall)


def _sum_chips(layers, name, layer_major):
    _, R, C = layers[0].shape
    L = len(layers)
    tc = _tile(C, 256)

    def body(*refs):
        o_ref = refs[-1]
        for l, p_ref in enumerate(refs[:-1]):
            p = [p_ref[k].astype(F32) for k in range(N_CHIPS)]
            s = ((p[0] + p[1]) + p[2]) + p[3]
            if layer_major:
                o_ref[l] = s
            else:
                o_ref[:, l, :] = s

    out = (L, R, C) if layer_major else (R, L, C)
    out_blk = (L, R, tc) if layer_major else (R, L, tc)
    return pl.pallas_call(
        body, name=name, grid=(C // tc,),
        in_specs=[pl.BlockSpec((N_CHIPS, R, tc), lambda i: (0, 0, i))] * L,
        out_specs=pl.BlockSpec(out_blk, lambda i: (0, 0, i)), out_shape=SDS(out, F32),
        compiler_params=_params("parallel"),
    )(*layers)


ANY = pl.BlockSpec(memory_space=pl.ANY)


def _mesh_pos():
    return lax.axis_index("x"), lax.axis_index("y"), lax.axis_index("c")


def _other_chips(x, y):
    return [(1 - x, y), (x, 1 - y), (1 - x, 1 - y)]


class _ChipExchange:
    def __init__(self, mode, sources):
        assert mode in ("gather", "scatter")
        self.mode, self.sources = mode, tuple(sources)
        self.n = len(self.sources)
        self.in_specs = [ANY] * self.n
        self.out_specs = [ANY] * self.n
        self.out_shape = [SDS(((N_CHIPS,) + s.shape) if mode == "gather" else s.shape, s.dtype) for s in self.sources]
        self.scratch = [pltpu.SemaphoreType.DMA((3 * self.n,)), pltpu.SemaphoreType.DMA((3 * self.n,)),
                        pltpu.SemaphoreType.DMA((self.n,))]

    def _copies(self, srcs, dsts, send_sems, recv_sems, local_sems):
        x, y, c = _mesh_pos()
        me = 2 * x + y
        view = (lambda r, chip: r) if self.mode == "gather" else (lambda r, chip: r.at[chip])
        local = [pltpu.make_async_copy(view(s, me), d.at[me], local_sems.at[a]) for a, (s, d) in enumerate(zip(srcs, dsts))]
        sends, recvs = [], []
        for j, (px, py) in enumerate(_other_chips(x, y)):
            peer = 2 * px + py
            for a, (s, d) in enumerate(zip(srcs, dsts)):
                sems = dict(send_sem=send_sems.at[self.n * j + a], recv_sem=recv_sems.at[self.n * j + a],
                            device_id=(px, py, c), device_id_type=MESH_ID)
                sends.append(pltpu.make_async_remote_copy(src_ref=view(s, peer), dst_ref=d.at[me], **sems))
                recvs.append(pltpu.make_async_remote_copy(src_ref=view(s, me), dst_ref=d.at[peer], **sems))
        return local, sends, recvs

    def start(self, srcs, dsts, sems):
        local, sends, _ = self._copies(srcs, dsts, *sems)
        for cp in local + sends:
            cp.start()

    def wait(self, srcs, dsts, sems):
        local, sends, recvs = self._copies(srcs, dsts, *sems)
        for cp in recvs:
            cp.wait_recv()
        for cp in sends:
            cp.wait_send()
        for cp in local:
            cp.wait()


def _gather_halves(w, tag):
    R, C = w.shape
    H = R // 2

    def body(w_ref, g_ref, send_sems, recv_sems, pass_send, pass_recv, local_sem):
        x, y, c = _mesh_pos()
        me = 2 * x + y
        mine, theirs = pl.ds(c * H, H), pl.ds((1 - c) * H, H)
        own = pltpu.make_async_copy(w_ref, g_ref.at[me], local_sem)
        own.start()

        def fetch(j, px, py, src, dst):
            return pltpu.make_async_remote_copy(src_ref=src, dst_ref=dst, send_sem=send_sems.at[j], recv_sem=recv_sems.at[j],
                                                device_id=(px, py, c), device_id_type=MESH_ID)

        def hand(j, rows, peer):
            return pltpu.make_async_remote_copy(src_ref=g_ref.at[peer, rows], dst_ref=g_ref.at[peer, rows],
                                                send_sem=pass_send.at[j], recv_sem=pass_recv.at[j],
                                                device_id=(x, y, 1 - c), device_id_type=MESH_ID)

        chips = _other_chips(x, y)
        sends = [fetch(j, px, py, w_ref.at[mine], g_ref.at[me, mine]) for j, (px, py) in enumerate(chips)]
        for cp in sends:
            cp.start()
        passed = []
        for j, (px, py) in enumerate(chips):
            peer = 2 * px + py
            fetch(j, px, py, w_ref.at[mine], g_ref.at[peer, mine]).wait_recv()
            passed.append(hand(j, mine, peer))
            passed[-1].start()
        for j, (px, py) in enumerate(chips):
            hand(j, theirs, 2 * px + py).wait_recv()
        for cp in sends + passed:
            cp.wait_send()
        own.wait()

    return pl.pallas_call(
        body, name=f"gather_halves_{tag}", in_specs=[ANY], out_specs=ANY, out_shape=SDS((N_CHIPS, R, C), w.dtype),
        scratch_shapes=[pltpu.SemaphoreType.DMA((3,)), pltpu.SemaphoreType.DMA((3,)), pltpu.SemaphoreType.DMA((3,)),
                        pltpu.SemaphoreType.DMA((3,)), pltpu.SemaphoreType.DMA],
        compiler_params=pltpu.CompilerParams(has_side_effects=True),
    )(w)


class _DeviceGather:
    def __init__(self, source):
        self.sources, self.n = (source,), 1
        self.in_specs, self.out_specs = [ANY], [ANY]
        self.out_shape = [SDS((N_DEV,) + source.shape, source.dtype)]
        self.scratch = [pltpu.SemaphoreType.DMA((N_DEV - 1,)), pltpu.SemaphoreType.DMA((N_DEV - 1,)),
                        pltpu.SemaphoreType.DMA((1,))]

    def _copies(self, srcs, dsts, send_sems, recv_sems, local_sems):
        (src,), (dst,) = srcs, dsts
        x, y, c = _mesh_pos()
        me = 4 * x + 2 * y + c
        local = [pltpu.make_async_copy(src, dst.at[me], local_sems.at[0])]
        sends, recvs = [], []
        for k in range(1, N_DEV):
            px, py, pc = (1 - x) if k & 4 else x, (1 - y) if k & 2 else y, (1 - c) if k & 1 else c
            sems = dict(send_sem=send_sems.at[k - 1], recv_sem=recv_sems.at[k - 1], device_id=(px, py, pc),
                        device_id_type=MESH_ID)
            sends.append(pltpu.make_async_remote_copy(src_ref=src, dst_ref=dst.at[me], **sems))
            recvs.append(pltpu.make_async_remote_copy(src_ref=src, dst_ref=dst.at[4 * px + 2 * py + pc], **sems))
        return local, sends, recvs

    start = _ChipExchange.start
    wait = _ChipExchange.wait


def _gather_devices(a, name):
    ex = _DeviceGather(a)

    def body(a_ref, g_ref, *sems):
        ex.start((a_ref,), (g_ref,), sems)
        ex.wait((a_ref,), (g_ref,), sems)

    return pl.pallas_call(
        body, name=name, in_specs=ex.in_specs, out_specs=ex.out_specs[0], out_shape=ex.out_shape[0],
        scratch_shapes=ex.scratch, compiler_params=pltpu.CompilerParams(has_side_effects=True),
    )(a)


def _swap_cores(pin, pout):
    def body(pin_ref, pout_ref, oin_ref, oout_ref, send_sems, recv_sems):
        x, y, c = _mesh_pos()
        cps = [pltpu.make_async_remote_copy(src_ref=src, dst_ref=dst, send_sem=send_sems.at[a], recv_sem=recv_sems.at[a],
                                            device_id=(x, y, 1 - c), device_id_type=MESH_ID)
               for a, (src, dst) in enumerate(((pin_ref, oin_ref), (pout_ref, oout_ref)))]
        for cp in cps:
            cp.start()
        for cp in cps:
            cp.wait()

    return pl.pallas_call(
        body, name="swap_cores", in_specs=[ANY, ANY], out_specs=[ANY, ANY],
        out_shape=[SDS(pin.shape, F32), SDS(pout.shape, F32)],
        scratch_shapes=[pltpu.SemaphoreType.DMA((2,)), pltpu.SemaphoreType.DMA((2,))],
        compiler_params=pltpu.CompilerParams(has_side_effects=True),
    )(pin, pout)


PACK_TILE = 8 * 128


def _pack_rows(size):
    return (size + PACK_TILE - 1) // PACK_TILE * 8


def _pack_small(parts):
    return jnp.concatenate([jnp.pad(p.reshape(-1), (0, (-p.size) % PACK_TILE)).reshape(-1, 128) for p in parts])


def _unpack_small(packed):
    out, row = [], 0
    for _, shape in SMALL_PARAMS:
        size = math.prod(shape)
        rows = packed[row:row + _pack_rows(size)]
        out.append(rows.reshape(-1)[:size].reshape(shape))
        row += _pack_rows(size)
    return out


def _layer_consts(l, gmlp_ln_g, gmlp_ln_b, gmlp_w_s, gmlp_b_s, hgrn_onorm_g, fox_b_f):
    causal = jnp.tril(jnp.ones((CHUNK, CHUNK), bool))
    wm = jnp.where(causal[None], gmlp_w_s[l], 0.0)
    return dict(
        lng=gmlp_ln_g[l].reshape(1, A_WIDTH), lnb=gmlp_ln_b[l].reshape(1, A_WIDTH),
        wm=wm.astype(BF16), wmt=jnp.swapaxes(wm, 1, 2).astype(BF16),
        bst=jnp.pad(gmlp_b_s[l].T, ((0, 0), (0, 128 - A_GROUPS))),
        onorm=jnp.tile(hgrn_onorm_g[l], 4).reshape(1, B_WIDTH),
        bf=jnp.pad(fox_b_f[l], (0, 128 - C_HEADS)).reshape(1, 128),
    )


def kernel(x, norm_g, w_in, w_out, gmlp_ln_g, gmlp_ln_b, gmlp_w_s, gmlp_b_s, hgrn_lb, hgrn_onorm_g, fox_b_f, final_norm_g, loss_target, m_norm_g, m_w_in, m_w_out, m_gmlp_ln_g, m_gmlp_ln_b, m_gmlp_w_s, m_gmlp_b_s, m_hgrn_lb, m_hgrn_onorm_g, m_fox_b_f, m_final_norm_g, v_norm_g, v_w_in, v_w_out, v_gmlp_ln_g, v_gmlp_ln_b, v_gmlp_w_s, v_gmlp_b_s, v_hgrn_lb, v_hgrn_onorm_g, v_fox_b_f, v_final_norm_g):
    T = x.shape[1]
    shard_in = w_in.shape[2]
    shard_out = w_out.shape[1]
    xs = x.reshape(T, D_MODEL)
    tgt = loss_target.reshape(T, D_MODEL)

    w_in_b, w_out_b = w_in.astype(BF16), w_out.astype(BF16)

    def full_w_in(gathered):
        return jnp.concatenate([gathered[k] for k in range(N_CHIPS)] + [jnp.zeros((D_MODEL, D_IN_PAD - D_IN), BF16)], axis=-1)

    lb_all = _lb_fwd(hgrn_lb)
    consts = [_layer_consts(l, gmlp_ln_g, gmlp_ln_b, gmlp_w_s, gmlp_b_s, hgrn_onorm_g, fox_b_f) for l in range(DEPTH)]

    saved = []
    xl = xs
    w_in_l = full_w_in(_gather_halves(w_in_b[0], "w_in_l0"))
    for l in range(DEPTH):
        cs = consts[l]
        tag = f"l{l}"
        h, proj = _inproj(xl, norm_g[l].reshape(1, D_MODEL), w_in_l, tag)
        ya = _gmlp_fwd(proj, cs["lng"], cs["lnb"], cs["wm"], cs["bst"], tag)
        yb, ob, s0 = _hgrn_fwd(proj, lb_all[l].reshape(1, B_WIDTH), cs["onorm"], tag)
        qt, kt, vt = _fox_prep(proj, cs["bf"], tag)
        ride = _ChipExchange("gather", (w_out_b[l],) + ((w_in_b[l + 1],) if l + 1 < DEPTH else ()))
        oc, lse, yc, *gathered = _fox_fwd(qt, kt, vt, proj, tag, ride)
        w_out_l = gathered[0].reshape(N_CHIPS * shard_out, D_MODEL)
        saved.append(dict(x=xl, h=h, proj=proj, ya=ya, yb=yb, yc=yc, ob=ob, s0=s0, qt=qt, kt=kt, oc=oc, lse=lse,
                          w_in=w_in_l, w_out=w_out_l))
        xl = _outproj(xl, ya, yb, yc, w_out_l, tag)
        if l + 1 < DEPTH:
            w_in_l = full_w_in(gathered[1])

    dx, loss_part, d_final = _loss_head(xl, final_norm_g.reshape(1, D_MODEL), tgt)

    g_small = {}
    dlb_rows, rin, rout = [None] * DEPTH, [None] * DEPTH, [None] * DEPTH
    slabs_in = None
    stack = lambda key: jnp.stack([g_small[l][key] for l in range(DEPTH)])
    for l in reversed(range(DEPTH)):
        cs, sv = consts[l], saved[l]
        tag = f"l{l}"
        proj = sv["proj"]
        dy, dw_out = _outproj_bwd(dx, sv["ya"], sv["yb"], sv["yc"], sv["w_out"], tag)
        da, dwm, dbst, dlng, dlnb = _gmlp_bwd(proj, dy, cs["lng"], cs["lnb"], cs["wm"], cs["wmt"], cs["bst"], tag)
        db, dlb_rows[l], donorm = _hgrn_bwd(proj, dy, sv["ob"], sv["s0"], lb_all[l].reshape(1, B_WIDTH), cs["onorm"], tag)
        do, delta, dzc, dot, qtr = _fox_bwd_prep(proj, dy, sv["oc"], sv["qt"], tag)
        slabs_out = dw_out.reshape(N_CHIPS, shard_out, D_MODEL).astype(BF16)
        ride = _ChipExchange("scatter", (slabs_out,) + ((slabs_in,) if slabs_in is not None else ()))
        dqt, dkt, dvc, *received = _fox_bwd(sv["qt"], sv["kt"], proj, do, sv["lse"], delta, dot, qtr, tag, ride)
        rout[l] = received[0]
        if slabs_in is not None:
            rin[l + 1] = received[1]
        dqc, dkc, dflc, dbf = _fox_bwd_post(dqt, dkt, proj, cs["bf"], tag)
        g_small[l] = dict(ln_g=dlng.reshape(4, 64), ln_b=dlnb.reshape(4, 64), w_s=dwm, b_s=dbst[:, :A_GROUPS].T,
                          onorm=donorm[0, :64], bf=dbf[0, :C_HEADS])
        dproj = jnp.concatenate([da, db, dqc, dkc, dvc, dzc, dflc, jnp.zeros((T, 128), BF16)], axis=1)
        if l == 0:
            d_hgrn_lb = _lb_bwd(hgrn_lb, jnp.concatenate(dlb_rows, axis=0))
            early = _pack_small([stack("ln_g"), stack("ln_b"), stack("w_s"), stack("b_s"), d_hgrn_lb, stack("onorm"),
                                 stack("bf"), d_final.reshape(D_MODEL), loss_part.reshape(128)])
            dw_in, rearly = _dw_in(sv["h"], dproj, tag, _DeviceGather(early))
        else:
            dw_in = _dw_in(sv["h"], dproj, tag)
        slabs_in = dw_in[:N_CHIPS * shard_in].reshape(N_CHIPS, shard_in, D_MODEL).astype(BF16)
        ride = _ChipExchange("scatter", (slabs_in,)) if l == 0 else None
        dx, dng, *received = _dx_in(sv["x"], norm_g[l].reshape(1, D_MODEL), dx, dproj, sv["w_in"], tag, ride)
        if l == 0:
            rin[0] = received[0]
        g_small[l]["norm_g"] = dng.reshape(D_MODEL)
    grad_x = dx.reshape(x.shape)
    rlate = _gather_devices(_pack_small([stack("norm_g")]), "gather_norm_grads")
    rsmall = jnp.concatenate([rlate, rearly], axis=1)

    pin, pout = _sum_chips(rin, "sum_chips_w_in", False), _sum_chips(rout, "sum_chips_w_out", True)
    oin, oout = _swap_cores(pin, pout)
    to_view = lambda a: jnp.transpose(a, (2, 0, 1))
    g_w_in, d_w_in, nm_w_in, nv_w_in = [
        jnp.transpose(o, (1, 2, 0))
        for o in _adamw_pair(to_view(w_in), to_view(m_w_in), to_view(v_w_in), pin, oin, "adamw_w_in")]
    g_w_out, d_w_out, nm_w_out, nv_w_out = _adamw_pair(w_out, m_w_out, v_w_out, pout, oout, "adamw_w_out")

    small_w = [norm_g, gmlp_ln_g, gmlp_ln_b, gmlp_w_s, gmlp_b_s, hgrn_lb, hgrn_onorm_g, fox_b_f, final_norm_g]
    small_m = [m_norm_g, m_gmlp_ln_g, m_gmlp_ln_b, m_gmlp_w_s, m_gmlp_b_s, m_hgrn_lb, m_hgrn_onorm_g, m_fox_b_f, m_final_norm_g]
    small_v = [v_norm_g, v_gmlp_ln_g, v_gmlp_ln_b, v_gmlp_w_s, v_gmlp_b_s, v_hgrn_lb, v_hgrn_onorm_g, v_fox_b_f, v_final_norm_g]
    slot = [jnp.zeros((128,), F32)]
    outs = _adamw_small(_pack_small(small_w + slot), _pack_small(small_m + slot), _pack_small(small_v + slot), rsmall)
    sg, sd, sm, sv_ = [_unpack_small(o) for o in outs]
    loss = outs[0][sum(_pack_rows(math.prod(s)) for _, s in SMALL_PARAMS), 0]

    def order(big_in, big_out, small):
        return [small[0], big_in, big_out] + small[1:]

    return (loss, grad_x, *order(g_w_in, g_w_out, sg), *order(d_w_in, d_w_out, sd), *order(nm_w_in, nm_w_out, sm),
            *order(nv_w_in, nv_w_out, sv_))
```

```python
import collections
import functools
import math

import jax
import jax.numpy as jnp
from jax import lax
from jax.experimental import pallas as pl
from jax.experimental.pallas import tpu as pltpu

F32 = jnp.float32
BF16 = jnp.bfloat16
SDS = jax.ShapeDtypeStruct
MESH_ID = pl.DeviceIdType.MESH

D_MODEL = 1024
DEPTH = 2
A_WIDTH = 256
A_GROUPS = 4
B_WIDTH = 256
C_WIDTH = 512
C_HEADS = 8
D_IN = 3848
D_IN_PAD = 4096
CHUNK = 128
SUB = 16
SUB_SHIFT = 4
NORM_EPS = 1e-6
F_FLOOR = 1e-30
COL_AU, COL_AV, COL_AZ = 0, 256, 512
COL_BQ, COL_BF, COL_BI, COL_BZ = 768, 1024, 1280, 1536
COL_CQ, COL_CK, COL_CV, COL_CZ, COL_CF = 1792, 2304, 2816, 3328, 3840
HEAD_LANES = 128
Q_SCALE = 0.125
ADAM_LR, ADAM_B1, ADAM_B2, ADAM_EPS, ADAM_WD, ADAM_STEP = 0.001, 0.9, 0.999, 1e-08, 0.01, 10
ADAM_C1 = 1.0 - ADAM_B1 ** ADAM_STEP
ADAM_C2 = 1.0 - ADAM_B2 ** ADAM_STEP
VMEM_LIMIT = 56 * 1024 * 1024
ADAMW_BLOCK_BYTES = 1 << 20
N_CHIPS = 4
N_DEV = 8

SMALL_PARAMS = (
    ("norm_g", (DEPTH, D_MODEL)), ("gmlp_ln_g", (DEPTH, 4, 64)), ("gmlp_ln_b", (DEPTH, 4, 64)),
    ("gmlp_w_s", (DEPTH, 4, 128, 128)), ("gmlp_b_s", (DEPTH, 4, 128)), ("hgrn_lb", (DEPTH, 256)),
    ("hgrn_onorm_g", (DEPTH, 64)), ("fox_b_f", (DEPTH, 8)), ("final_norm_g", (D_MODEL,)),
)


def _tile(n, pref):
    t = min(n, pref)
    assert n % t == 0, (n, pref)
    return t


def _params(*sem):
    return pltpu.CompilerParams(dimension_semantics=sem, vmem_limit_bytes=VMEM_LIMIT)


_Part = collections.namedtuple("_Part", "body operands in_specs out_specs out_shape scratch")


def _run_parts(parts, grid, name):
    counts = [(len(p.operands), len(p.out_shape), len(p.scratch)) for p in parts]

    def body(*refs):
        ins, outs, scr = [], [], []
        pos = 0
        for group, k in ((ins, 0), (outs, 1), (scr, 2)):
            for c in counts:
                group.append(refs[pos:pos + c[k]])
                pos += c[k]
        for p, i, o, s in zip(parts, ins, outs, scr):
            p.body(*i, *o, *s)

    flat = lambda key: [x for p in parts for x in getattr(p, key)]
    res = pl.pallas_call(
        body, name=name, grid=grid, in_specs=flat("in_specs"), out_specs=flat("out_specs"), out_shape=flat("out_shape"),
        scratch_shapes=flat("scratch"), compiler_params=_params(*(("arbitrary",) * len(grid))),
    )(*flat("operands"))
    out, pos = [], 0
    for c in counts:
        out.append(list(res[pos:pos + c[1]]))
        pos += c[1]
    return out


def _dot(a, b):
    return jnp.dot(a, b, preferred_element_type=F32)


def _dot_nt(a, b):
    return lax.dot_general(a, b, (((1,), (1,)), ((), ())), preferred_element_type=F32)


def _dot_tn(a, b):
    return lax.dot_general(a, b, (((0,), (0,)), ((), ())), preferred_element_type=F32)


def _split3(x):
    hi = x.astype(BF16)
    r = x - hi.astype(F32)
    mid = r.astype(BF16)
    lo = (r - mid.astype(F32)).astype(BF16)
    return hi, mid, lo


def _dot3_left(c, x):
    hi, mid, lo = _split3(x)
    return _dot(c, hi) + _dot(c, mid) + _dot(c, lo)


def _sigmoid(x):
    return jax.nn.sigmoid(x)


def _silu_and_grad(x):
    s = _sigmoid(x)
    return x * s, s * (1.0 + x * (1.0 - s))


_GELU_C = math.sqrt(2.0 / math.pi)


def _gelu_and_grad(x):
    inner = _GELU_C * (x + 0.044715 * x * x * x)
    t = jnp.tanh(inner)
    y = 0.5 * x * (1.0 + t)
    dy = 0.5 * (1.0 + t) + 0.5 * x * (1.0 - t * t) * _GELU_C * (1.0 + 3.0 * 0.044715 * x * x)
    return y, dy


def _lane(shape):
    return lax.broadcasted_iota(jnp.int32, shape, 1)


def _row(shape):
    return lax.broadcasted_iota(jnp.int32, shape, 0)


def _gsum64(x):
    lo = _lane(x.shape) < 64
    s0 = jnp.sum(jnp.where(lo, x, 0.0), axis=-1, keepdims=True)
    s1 = jnp.sum(jnp.where(lo, 0.0, x), axis=-1, keepdims=True)
    return jnp.where(lo, s0, s1)


def _colreduce(x, op):
    parts = [x[r:r + 8, :] for r in range(0, x.shape[0], 8)]
    while len(parts) > 1:
        pairs = [op(parts[k], parts[k + 1]) for k in range(0, len(parts) - 1, 2)]
        parts = pairs + ([parts[-1]] if len(parts) % 2 else [])
    red = jnp.max if op is jnp.maximum else jnp.sum
    return red(parts[0], axis=0, keepdims=True)


def _block_diag64(dtype=BF16):
    r, c = _row((128, 128)), _lane((128, 128))
    return jnp.where((r >> 6) == (c >> 6), 1.0, 0.0).astype(dtype)


def _inproj(x, g, w, tag):
    T, D = x.shape
    DP = w.shape[1]
    tm = _tile(T, 512)

    def body(x_ref, g_ref, w_ref, h_ref, p_ref):
        xv = x_ref[...]
        r = lax.rsqrt(jnp.mean(xv * xv, axis=-1, keepdims=True) + NORM_EPS)
        h = (xv * r * g_ref[...]).astype(BF16)
        h_ref[...] = h
        p_ref[...] = _dot(h, w_ref[...])

    return pl.pallas_call(
        body, name=f"inproj_{tag}", grid=(T // tm,),
        in_specs=[pl.BlockSpec((tm, D), lambda i: (i, 0)), pl.BlockSpec((1, D), lambda i: (0, 0)),
                  pl.BlockSpec((D, DP), lambda i: (0, 0))],
        out_specs=[pl.BlockSpec((tm, D), lambda i: (i, 0)), pl.BlockSpec((tm, DP), lambda i: (i, 0))],
        out_shape=[SDS((T, D), BF16), SDS((T, DP), F32)],
        compiler_params=_params("parallel"),
    )(x, g, w)


def _outproj(x, ya, yb, yc, wo, tag):
    T, D = x.shape
    tm = _tile(T, 512)

    def body(x_ref, ya_ref, yb_ref, yc_ref, wo_ref, o_ref):
        acc = x_ref[...] + _dot(ya_ref[...], wo_ref[0:A_WIDTH, :])
        acc = acc + _dot(yb_ref[...], wo_ref[A_WIDTH:A_WIDTH + B_WIDTH, :])
        o_ref[...] = acc + _dot(yc_ref[...], wo_ref[A_WIDTH + B_WIDTH:, :])

    row = lambda w: pl.BlockSpec((tm, w), lambda i: (i, 0))
    return pl.pallas_call(
        body, name=f"outproj_{tag}", grid=(T // tm,),
        in_specs=[row(D), row(A_WIDTH), row(B_WIDTH), row(C_WIDTH), pl.BlockSpec(wo.shape, lambda i: (0, 0))],
        out_specs=row(D), out_shape=SDS((T, D), F32), compiler_params=_params("parallel"),
    )(x, ya, yb, yc, wo)


def _outproj_bwd(dx, ya, yb, yc, wo, tag):
    T, D = dx.shape
    DM = wo.shape[0]
    tm = _tile(T, 512)

    def body(dx_ref, ya_ref, yb_ref, yc_ref, wo_ref, dy_ref, dwo_ref):
        @pl.when(pl.program_id(0) == 0)
        def _():
            dwo_ref[...] = jnp.zeros_like(dwo_ref)

        dxb = dx_ref[...].astype(BF16)
        dy_ref[...] = _dot_nt(dxb, wo_ref[...])
        dwo_ref[0:A_WIDTH, :] += _dot_tn(ya_ref[...], dxb)
        dwo_ref[A_WIDTH:A_WIDTH + B_WIDTH, :] += _dot_tn(yb_ref[...], dxb)
        dwo_ref[A_WIDTH + B_WIDTH:, :] += _dot_tn(yc_ref[...], dxb)

    row = lambda w: pl.BlockSpec((tm, w), lambda i: (i, 0))
    return pl.pallas_call(
        body, name=f"outproj_bwd_{tag}", grid=(T // tm,),
        in_specs=[row(D), row(A_WIDTH), row(B_WIDTH), row(C_WIDTH), pl.BlockSpec(wo.shape, lambda i: (0, 0))],
        out_specs=[row(DM), pl.BlockSpec((DM, D), lambda i: (0, 0))],
        out_shape=[SDS((T, DM), F32), SDS((DM, D), F32)], compiler_params=_params("arbitrary"),
    )(dx, ya, yb, yc, wo)


def _dw_in(h, dproj, tag, ride=None):
    T, D = h.shape
    DP = dproj.shape[1]
    tm, tn = _tile(T, 1024), _tile(DP, 1024)
    grid = (DP // tn, T // tm)

    def body(h_ref, dp_ref, *rest):
        ride_srcs, (dw_ref,), ride_dsts, _, ride_sems = _ride_refs(ride, rest, 1, 0)
        _ride_start(ride, grid, ride_srcs, ride_dsts, ride_sems)

        @pl.when(pl.program_id(1) == 0)
        def _():
            dw_ref[...] = jnp.zeros_like(dw_ref)

        dw_ref[...] += _dot_tn(dp_ref[...], h_ref[...])
        _ride_wait(ride, grid, ride_srcs, ride_dsts, ride_sems)

    extra = ride or _ChipExchange("gather", ())
    out = pl.pallas_call(
        body, name=f"dw_in_{tag}", grid=grid,
        in_specs=[pl.BlockSpec((tm, D), lambda j, i: (i, 0)), pl.BlockSpec((tm, tn), lambda j, i: (i, j))] + extra.in_specs,
        out_specs=[pl.BlockSpec((tn, D), lambda j, i: (j, 0))] + extra.out_specs,
        out_shape=[SDS((DP, D), F32)] + extra.out_shape, scratch_shapes=extra.scratch if ride else [],
        compiler_params=pltpu.CompilerParams(dimension_semantics=("arbitrary", "arbitrary"), vmem_limit_bytes=VMEM_LIMIT,
                                             has_side_effects=bool(ride)),
    )(h, dproj, *extra.sources)
    return out if ride else out[0]


def _dx_in(x, g, dres, dproj, w, tag, ride=None):
    T, D = x.shape
    DP = w.shape[1]
    tm = _tile(T, 512)
    grid = (T // tm,)

    def body(x_ref, g_ref, dres_ref, dp_ref, w_ref, *rest):
        ride_srcs, (dx_ref, dg_ref), ride_dsts, _, ride_sems = _ride_refs(ride, rest, 2, 0)
        _ride_start(ride, grid, ride_srcs, ride_dsts, ride_sems)

        @pl.when(pl.program_id(0) == 0)
        def _():
            dg_ref[...] = jnp.zeros_like(dg_ref)

        dh = _dot_nt(dp_ref[...], w_ref[...])
        xv = x_ref[...]
        r = lax.rsqrt(jnp.mean(xv * xv, axis=-1, keepdims=True) + NORM_EPS)
        xh = xv * r
        dg_ref[...] += jnp.sum(dh * xh, axis=0, keepdims=True)
        dxh = dh * g_ref[...]
        dx_ref[...] = dres_ref[...] + r * (dxh - xh * jnp.mean(dxh * xh, axis=-1, keepdims=True))
        _ride_wait(ride, grid, ride_srcs, ride_dsts, ride_sems)

    extra = ride or _ChipExchange("gather", ())
    row = pl.BlockSpec((tm, D), lambda i: (i, 0))
    return pl.pallas_call(
        body, name=f"dx_in_{tag}", grid=grid,
        in_specs=[row, pl.BlockSpec((1, D), lambda i: (0, 0)), row, pl.BlockSpec((tm, DP), lambda i: (i, 0)),
                  pl.BlockSpec((D, DP), lambda i: (0, 0))] + extra.in_specs,
        out_specs=[row, pl.BlockSpec((1, D), lambda i: (0, 0))] + extra.out_specs,
        out_shape=[SDS((T, D), F32), SDS((1, D), F32)] + extra.out_shape,
        scratch_shapes=extra.scratch if ride else [],
        compiler_params=pltpu.CompilerParams(dimension_semantics=("arbitrary",), vmem_limit_bytes=VMEM_LIMIT,
                                             has_side_effects=bool(ride)),
    )(x, g, dres, dproj, w, *extra.sources)


def _loss_head(x, g, tgt):
    T, D = x.shape
    tm = _tile(T, 512)

    def body(x_ref, g_ref, t_ref, dx_ref, loss_ref, dg_ref):
        @pl.when(pl.program_id(0) == 0)
        def _():
            loss_ref[...] = jnp.zeros_like(loss_ref)
            dg_ref[...] = jnp.zeros_like(dg_ref)

        xv = x_ref[...]
        r = lax.rsqrt(jnp.mean(xv * xv, axis=-1, keepdims=True) + NORM_EPS)
        xh = xv * r
        gv = g_ref[...]
        err = xh * gv - t_ref[...]
        tok = jnp.mean(err * err, axis=-1, keepdims=True)
        loss_ref[...] += 0.5 * jnp.sum(tok, axis=0, keepdims=True)
        dy = err * (1.0 / D)
        dg_ref[...] += jnp.sum(dy * xh, axis=0, keepdims=True)
        dxh = dy * gv
        dx_ref[...] = r * (dxh - xh * jnp.mean(dxh * xh, axis=-1, keepdims=True))

    row = pl.BlockSpec((tm, D), lambda i: (i, 0))
    return pl.pallas_call(
        body, name="loss_head", grid=(T // tm,),
        in_specs=[row, pl.BlockSpec((1, D), lambda i: (0, 0)), row],
        out_specs=[row, pl.BlockSpec((1, 128), lambda i: (0, 0)), pl.BlockSpec((1, D), lambda i: (0, 0))],
        out_shape=[SDS((T, D), F32), SDS((1, 128), F32), SDS((1, D), F32)], compiler_params=_params("arbitrary"),
    )(x, g, tgt)


def _gmlp_core(u, v, lng, lnb, wm_ref, bst_ref, pair):
    ug, dug = _gelu_and_grad(u)
    vg, dvg = _gelu_and_grad(v)
    mu = _gsum64(vg) * (1.0 / 64)
    d = vg - mu
    var = _gsum64(d * d) * (1.0 / 64)
    rstd = lax.rsqrt(var + NORM_EPS)
    xh = d * rstd
    vn = xh * lng + lnb
    vnb = vn.astype(BF16)
    lo = _lane(u.shape) < 64
    g0, g1 = 2 * pair, 2 * pair + 1
    mixed = jnp.where(lo, _dot(wm_ref[g0], vnb) + bst_ref[:, g0:g0 + 1], _dot(wm_ref[g1], vnb) + bst_ref[:, g1:g1 + 1])
    return ug, dug, dvg, rstd, xh, vnb, mixed, lo


def _gmlp_fwd(proj, lng, lnb, wm, bst):
    T = proj.shape[0]

    def body(u_ref, v_ref, z_ref, lng_ref, lnb_ref, wm_ref, bst_ref, y_ref):
        for pair in range(2):
            sl = slice(128 * pair, 128 * pair + 128)
            ug, _, _, _, _, _, mixed, _ = _gmlp_core(u_ref[:, sl], v_ref[:, sl], lng_ref[:, sl], lnb_ref[:, sl],
                                                     wm_ref, bst_ref, pair)
            sz, _ = _silu_and_grad(z_ref[:, sl])
            y_ref[:, sl] = (ug * mixed * sz).astype(BF16)

    col = lambda c: pl.BlockSpec((CHUNK, A_WIDTH), lambda i, c=c: (i, c // A_WIDTH))
    full = lambda a: pl.BlockSpec(a.shape, lambda i, n=a.ndim: (0,) * n)
    return _Part(body, (proj, proj, proj, lng, lnb, wm, bst),
                 [col(COL_AU), col(COL_AV), col(COL_AZ), full(lng), full(lnb), full(wm), full(bst)],
                 [pl.BlockSpec((CHUNK, A_WIDTH), lambda i: (i, 0))], [SDS((T, A_WIDTH), BF16)], [])


def _gmlp_bwd(proj, dy, lng, lnb, wm, wmt, bst):
    T = proj.shape[0]
    n = T // CHUNK

    def body(u_ref, v_ref, z_ref, dy_ref, lng_ref, lnb_ref, wm_ref, wmt_ref, bst_ref,
             da_ref, dwm_ref, dbst_ref, dlng_ref, dlnb_ref):
        @pl.when(pl.program_id(0) == 0)
        def _():
            dwm_ref[...] = jnp.zeros_like(dwm_ref)
            dbst_ref[...] = jnp.zeros_like(dbst_ref)
            dlng_ref[...] = jnp.zeros_like(dlng_ref)
            dlnb_ref[...] = jnp.zeros_like(dlnb_ref)

        lane = _lane((CHUNK, 128))
        dbst = dbst_ref[...]
        for pair in range(2):
            sl = slice(128 * pair, 128 * pair + 128)
            lng_p = lng_ref[:, sl]
            ug, dug, dvg, rstd, xh, vnb, mixed, lo = _gmlp_core(u_ref[:, sl], v_ref[:, sl], lng_p, lnb_ref[:, sl],
                                                                wm_ref, bst_ref, pair)
            sz, dsz = _silu_and_grad(z_ref[:, sl])
            dyv = dy_ref[:, sl]
            out = ug * mixed
            dz = dyv * out * dsz
            dout = dyv * sz
            du = dout * mixed * dug
            dmix = dout * ug
            g0, g1 = 2 * pair, 2 * pair + 1
            dm0 = jnp.where(lo, dmix, 0.0)
            dm1 = jnp.where(lo, 0.0, dmix)
            dbst = dbst + jnp.where(lane == g0, jnp.sum(dm0, axis=-1, keepdims=True), 0.0)
            dbst = dbst + jnp.where(lane == g1, jnp.sum(dm1, axis=-1, keepdims=True), 0.0)
            dwm_ref[g0] += _dot_nt(dm0.astype(BF16), vnb)
            dwm_ref[g1] += _dot_nt(dm1.astype(BF16), vnb)
            dmb = dmix.astype(BF16)
            dvn = jnp.where(lo, _dot(wmt_ref[g0], dmb), _dot(wmt_ref[g1], dmb))
            dlng_ref[:, sl] += jnp.sum(dvn * xh, axis=0, keepdims=True)
            dlnb_ref[:, sl] += jnp.sum(dvn, axis=0, keepdims=True)
            dxh = dvn * lng_p
            m1 = _gsum64(dxh) * (1.0 / 64)
            m2 = _gsum64(dxh * xh) * (1.0 / 64)
            dv = rstd * (dxh - m1 - xh * m2) * dvg
            da_ref[:, COL_AU + 128 * pair:COL_AU + 128 * pair + 128] = du.astype(BF16)
            da_ref[:, COL_AV + 128 * pair:COL_AV + 128 * pair + 128] = dv.astype(BF16)
            da_ref[:, COL_AZ + 128 * pair:COL_AZ + 128 * pair + 128] = dz.astype(BF16)
        dbst_ref[...] = dbst

        @pl.when(pl.program_id(0) == n - 1)
        def _():
            causal = _lane((CHUNK, CHUNK)) <= _row((CHUNK, CHUNK))
            for g in range(A_GROUPS):
                dwm_ref[g] = jnp.where(causal, dwm_ref[g], 0.0)

    col = lambda c: pl.BlockSpec((CHUNK, A_WIDTH), lambda i, c=c: (i, c // A_WIDTH))
    full = lambda a: pl.BlockSpec(a.shape, lambda i, n=a.ndim: (0,) * n)
    acc = lambda s: pl.BlockSpec(s, lambda i, n=len(s): (0,) * n)
    return _Part(body, (proj, proj, proj, dy, lng, lnb, wm, wmt, bst),
                 [col(COL_AU), col(COL_AV), col(COL_AZ), pl.BlockSpec((CHUNK, A_WIDTH), lambda i: (i, 0)),
                  full(lng), full(lnb), full(wm), full(wmt), full(bst)],
                 [pl.BlockSpec((CHUNK, 3 * A_WIDTH), lambda i: (i, 0)), acc((A_GROUPS, CHUNK, CHUNK)),
                  acc((CHUNK, 128)), acc((1, A_WIDTH)), acc((1, A_WIDTH))],
                 [SDS((T, 3 * A_WIDTH), BF16), SDS((A_GROUPS, CHUNK, CHUNK), F32), SDS((CHUNK, 128), F32),
                  SDS((1, A_WIDTH), F32), SDS((1, A_WIDTH), F32)], [])


def _hgrn_consts():
    r, c = _row((CHUNK, CHUNK)), _lane((CHUNK, CHUNK))
    same = (r >> SUB_SHIFT) == (c >> SUB_SHIFT)
    lsub = jnp.where(same & (c <= r), 1.0, 0.0).astype(BF16)
    usub = jnp.where(same & (c >= r), 1.0, 0.0).astype(BF16)
    bsub = jnp.where(same, 1.0, 0.0).astype(BF16)
    return lsub, usub, bsub


def _hgrn_gates(qv, zf, lbp):
    sq, dsq = _silu_and_grad(qv)
    qt = sq * Q_SCALE
    sg = _sigmoid(zf)
    sgn = _sigmoid(-zf)
    f = lbp + (1.0 - lbp) * sg
    g = jnp.log(jnp.maximum(f, F_FLOOR))
    kf = (1.0 - lbp) * sgn
    return qt, dsq, sg, sgn, f, g, kf


def _hgrn_intra_fwd(qt, kf, b, v, mbd):
    rid = _row((SUB, 128))
    parts = []
    for s in range(SUB):
        e = jnp.exp(b - b[s:s + 1, :])
        parts.append(jnp.where(rid >= s, qt * kf[s:s + 1, :] * e, 0.0))
    a = _dot(jnp.concatenate(parts, axis=0).astype(BF16), mbd)
    o = jnp.zeros((SUB, 128), F32)
    for s in range(SUB):
        o = o + a[SUB * s:SUB * s + SUB, :] * v[s:s + 1, :]
    return o


def _hgrn_intra_bwd(qt, kf, b, v, do, mbd, rsum):
    rid = _row((SUB, 128))
    ps, das, kes, es = [], [], [], []
    for s in range(SUB):
        e = jnp.where(rid >= s, jnp.exp(b - b[s:s + 1, :]), 0.0)
        ke = kf[s:s + 1, :] * e
        es.append(e)
        kes.append(ke)
        ps.append(qt * ke)
        das.append(do * v[s:s + 1, :])
    a = _dot(jnp.concatenate(ps, axis=0).astype(BF16), mbd)
    da = _dot(jnp.concatenate(das, axis=0).astype(BF16), mbd)
    dqt = jnp.zeros((SUB, 128), F32)
    xs, ys = [], []
    for s in range(SUB):
        da_s = da[SUB * s:SUB * s + SUB, :]
        dqt = dqt + da_s * kes[s]
        xs.append(a[SUB * s:SUB * s + SUB, :] * do)
        ys.append(da_s * qt * es[s])
    dv = _dot(rsum, jnp.concatenate(xs, axis=0).astype(BF16))
    dkf = _dot(rsum, jnp.concatenate(ys, axis=0).astype(BF16))
    return dqt, dkf, dv


def _hgrn_norm_gate(o, z, onorm):
    ms = _gsum64(o * o) * (1.0 / 64)
    r = lax.rsqrt(ms + NORM_EPS)
    xh = o * r
    sz, dsz = _silu_and_grad(z)
    return xh, r, sz, dsz, xh * onorm


def _hgrn_fwd(proj, lb, onorm):
    T = proj.shape[0]
    n = T // CHUNK
    nsub = CHUNK // SUB

    def body(q_ref, f_ref, i_ref, z_ref, lb_ref, on_ref, y_ref, o_ref, s0_ref, st_ref):
        @pl.when(pl.program_id(0) == 0)
        def _():
            st_ref[...] = jnp.zeros_like(st_ref)

        lsub, _, bsub = _hgrn_consts()
        mbd = _block_diag64()
        bdmask = mbd > 0
        rid = _row((CHUNK, 128))
        for pair in range(2):
            sl = slice(128 * pair, 128 * pair + 128)
            qt, _, _, _, _, g, kf = _hgrn_gates(q_ref[:, sl], f_ref[:, sl], lb_ref[:, sl])
            v = i_ref[:, sl]
            b = _dot3_left(lsub, g)
            bl = _dot3_left(bsub, g)
            qh = (qt * jnp.exp(b)).astype(BF16)
            kh = kf * jnp.exp(bl - b)
            dec = jnp.exp(bl)
            vtb = v.T.astype(BF16)
            st = st_ref[pair]
            s0_ref[0, pair] = st
            outs = []
            for sub in range(nsub):
                rs = slice(SUB * sub, SUB * sub + SUB)
                o_inter = _dot_nt(qh[rs], st.astype(BF16))
                outs.append(o_inter + _hgrn_intra_fwd(qt[rs], kf[rs], b[rs], v[rs], mbd))
                khm = jnp.where((rid >> SUB_SHIFT) == sub, kh, 0.0).astype(BF16)
                st = jnp.where(bdmask, st * dec[SUB * sub:SUB * sub + 1, :] + _dot(vtb, khm), 0.0)
            st_ref[pair] = st
            o = jnp.concatenate(outs, axis=0)
            o_ref[:, sl] = o
            _, _, sz, _, on = _hgrn_norm_gate(o, z_ref[:, sl], on_ref[:, sl])
            y_ref[:, sl] = (on * sz).astype(BF16)

    col = lambda c: pl.BlockSpec((CHUNK, B_WIDTH), lambda i, c=c: (i, c // B_WIDTH))
    full = lambda a: pl.BlockSpec(a.shape, lambda i, n=a.ndim: (0,) * n)
    return _Part(body, (proj, proj, proj, proj, lb, onorm),
                 [col(COL_BQ), col(COL_BF), col(COL_BI), col(COL_BZ), full(lb), full(onorm)],
                 [pl.BlockSpec((CHUNK, B_WIDTH), lambda i: (i, 0)), pl.BlockSpec((CHUNK, B_WIDTH), lambda i: (i, 0)),
                  pl.BlockSpec((1, 2, 128, 128), lambda i: (i, 0, 0, 0))],
                 [SDS((T, B_WIDTH), BF16), SDS((T, B_WIDTH), F32), SDS((n, 2, 128, 128), F32)],
                 [pltpu.VMEM((2, 128, 128), F32)])


def _hgrn_bwd(proj, dy, o_saved, s0, lb, onorm):
    T = proj.shape[0]
    n = T // CHUNK
    nsub = CHUNK // SUB

    def body(q_ref, f_ref, i_ref, z_ref, dy_ref, o_ref, s0_ref, lb_ref, on_ref,
             db_ref, dlb_ref, don_ref, dst_ref, sts_ref):
        @pl.when(pl.program_id(0) == 0)
        def _():
            dst_ref[...] = jnp.zeros_like(dst_ref)
            dlb_ref[...] = jnp.zeros_like(dlb_ref)
            don_ref[...] = jnp.zeros_like(don_ref)

        lsub, usub, bsub = _hgrn_consts()
        mbd = _block_diag64()
        bdmask = mbd > 0
        rsum = jnp.where((_lane((SUB, SUB * SUB)) >> SUB_SHIFT) == _row((SUB, SUB * SUB)), 1.0, 0.0).astype(BF16)
        for pair in range(2):
            sl = slice(128 * pair, 128 * pair + 128)
            lbp = lb_ref[:, sl]
            qv, zf = q_ref[:, sl], f_ref[:, sl]
            qt, dsq, sg, sgn, f, g, kf = _hgrn_gates(qv, zf, lbp)
            v = i_ref[:, sl]
            b = _dot3_left(lsub, g)
            bl = _dot3_left(bsub, g)
            eb = jnp.exp(b)
            ekb = jnp.exp(bl - b)
            qhb = (qt * eb).astype(BF16)
            khb = (kf * ekb).astype(BF16)
            dec = jnp.exp(bl)
            vb = v.astype(BF16)
            onp = on_ref[:, sl]
            ov = o_ref[:, sl]
            xh, r, sz, dsz, on = _hgrn_norm_gate(ov, z_ref[:, sl], onp)
            dyv = dy_ref[:, sl]
            dz = dyv * on * dsz
            don = dyv * sz
            cn = jnp.sum(don * xh, axis=0, keepdims=True)
            don_ref[...] += cn + pltpu.roll(cn, 64, axis=1)
            dxo = don * onp
            do = r * (dxo - xh * (_gsum64(dxo * xh) * (1.0 / 64)))
            dob = do.astype(BF16)
            st = s0_ref[0, pair]
            for sub in range(nsub):
                rs = slice(SUB * sub, SUB * sub + SUB)
                sts_ref[sub] = st
                st = jnp.where(bdmask, st * dec[SUB * sub:SUB * sub + 1, :] + _dot_tn(vb[rs], khb[rs]), 0.0)
            gst = dst_ref[pair]
            dqt_p, dkf_p, dv_p, dbl_p = [None] * nsub, [None] * nsub, [None] * nsub, [None] * nsub
            for sub in reversed(range(nsub)):
                rs = slice(SUB * sub, SUB * sub + SUB)
                st_in = sts_ref[sub]
                gb = gst.astype(BF16)
                dqh = _dot(dob[rs], st_in.astype(BF16))
                dkh = _dot(vb[rs], gb)
                dv_inter = _dot_nt(khb[rs], gb)
                ddec = jnp.sum(gst * st_in, axis=0, keepdims=True)
                dec_row = dec[SUB * sub:SUB * sub + 1, :]
                gst = jnp.where(bdmask, gst * dec_row + _dot_tn(dob[rs], qhb[rs]), 0.0)
                dqt_i, dkf_i, dv_i = _hgrn_intra_bwd(qt[rs], kf[rs], b[rs], v[rs], do[rs], mbd, rsum)
                dkf_inter = dkh * ekb[rs]
                dqt_p[sub] = dqh * eb[rs] + dqt_i
                dkf_p[sub] = dkf_inter + dkf_i
                dv_p[sub] = dv_inter + dv_i
                row = jnp.sum(kf[rs] * dkf_inter, axis=0, keepdims=True) + ddec * dec_row
                dbl_p[sub] = jnp.broadcast_to(row, (SUB, 128))
            dst_ref[pair] = gst
            dqt = jnp.concatenate(dqt_p, axis=0)
            dkf = jnp.concatenate(dkf_p, axis=0)
            dv = jnp.concatenate(dv_p, axis=0)
            dg = _dot3_left(usub, qt * dqt - kf * dkf) + jnp.concatenate(dbl_p, axis=0)
            df = jnp.where(f > F_FLOOR, dg / f, 0.0)
            dlb_ref[:, sl] += jnp.sum(df * (1.0 - sg) - dkf * sgn, axis=0, keepdims=True)
            dfl = (1.0 - lbp) * sg * sgn * (df - dkf)
            dq = dqt * Q_SCALE * dsq
            db_ref[:, 0 * B_WIDTH + 128 * pair:0 * B_WIDTH + 128 * pair + 128] = dq.astype(BF16)
            db_ref[:, 1 * B_WIDTH + 128 * pair:1 * B_WIDTH + 128 * pair + 128] = dfl.astype(BF16)
            db_ref[:, 2 * B_WIDTH + 128 * pair:2 * B_WIDTH + 128 * pair + 128] = dv.astype(BF16)
            db_ref[:, 3 * B_WIDTH + 128 * pair:3 * B_WIDTH + 128 * pair + 128] = dz.astype(BF16)

    rev = lambda c: pl.BlockSpec((CHUNK, B_WIDTH), lambda i, c=c: (n - 1 - i, c // B_WIDTH))
    full = lambda a: pl.BlockSpec(a.shape, lambda i, n_=a.ndim: (0,) * n_)
    acc = lambda s: pl.BlockSpec(s, lambda i, n_=len(s): (0,) * n_)
    return _Part(body, (proj, proj, proj, proj, dy, o_saved, s0, lb, onorm),
                 [rev(COL_BQ), rev(COL_BF), rev(COL_BI), rev(COL_BZ),
                  pl.BlockSpec((CHUNK, B_WIDTH), lambda i: (n - 1 - i, 1)),
                  pl.BlockSpec((CHUNK, B_WIDTH), lambda i: (n - 1 - i, 0)),
                  pl.BlockSpec((1, 2, 128, 128), lambda i: (n - 1 - i, 0, 0, 0)), full(lb), full(onorm)],
                 [pl.BlockSpec((CHUNK, 4 * B_WIDTH), lambda i: (n - 1 - i, 0)), acc((1, B_WIDTH)), acc((1, 128))],
                 [SDS((T, 4 * B_WIDTH), BF16), SDS((1, B_WIDTH), F32), SDS((1, 128), F32)],
                 [pltpu.VMEM((2, 128, 128), F32), pltpu.VMEM((nsub, 128, 128), F32)])


def _lb_fwd(hgrn_lb):
    assert hgrn_lb.shape[0] == 2

    def body(x_ref, o_ref):
        x0, x1 = x_ref[0:1, :], x_ref[1:2, :]
        m = jnp.maximum(x0, x1)
        e0, e1 = jnp.exp(x0 - m), jnp.exp(x1 - m)
        p0, p1 = e0 / (e0 + e1), e1 / (e0 + e1)
        o_ref[0:1, :] = jnp.clip(p0 - p0, 0.0, 1.0 - 1e-6)
        o_ref[1:2, :] = jnp.clip((p0 + p1) - p0, 0.0, 1.0 - 1e-6)

    return pl.pallas_call(body, name="lb_fwd", out_shape=SDS(hgrn_lb.shape, F32))(hgrn_lb)


def _lb_bwd(hgrn_lb, dlb):
    def body(x_ref, d_ref, o_ref):
        x0, x1 = x_ref[0:1, :], x_ref[1:2, :]
        m = jnp.maximum(x0, x1)
        e0, e1 = jnp.exp(x0 - m), jnp.exp(x1 - m)
        p0, p1 = e0 / (e0 + e1), e1 / (e0 + e1)
        val = (p0 + p1) - p0
        dp1 = jnp.where((val > 0.0) & (val < 1.0 - 1e-6), d_ref[1:2, :], 0.0)
        inner = p1 * dp1
        o_ref[0:1, :] = p0 * (0.0 - inner)
        o_ref[1:2, :] = p1 * (dp1 - inner)

    return pl.pallas_call(body, name="lb_bwd", out_shape=SDS(hgrn_lb.shape, F32))(hgrn_lb, dlb)


def _fox_prep(proj, bf):
    T = proj.shape[0]
    n = T // CHUNK

    def body(q0_ref, q1_ref, k0_ref, k1_ref, v0_ref, v1_ref, fl_ref, bf_ref, qo_ref, ko_ref, vt_ref, carry_ref):
        for p, v_ref in enumerate((v0_ref, v0_ref, v1_ref, v1_ref)):
            vt_ref[p, 0] = v_ref[:, 128 * (p % 2):128 * (p % 2) + 128].T.astype(BF16)

        @pl.when(pl.program_id(0) == 0)
        def _():
            carry_ref[...] = jnp.zeros_like(carry_ref)

        ltri = jnp.where(_lane((CHUNK, CHUNK)) <= _row((CHUNK, CHUNK)), 1.0, 0.0).astype(BF16)
        lf = jax.nn.log_sigmoid(fl_ref[...] + bf_ref[...])
        c = _dot3_left(ltri, lf) + carry_ref[...]
        carry_ref[...] = c[CHUNK - 1:CHUNK, :]
        lane = _lane((CHUNK, 128))
        feat = lane < 64
        ones_q = (lane >= 67) & (lane <= 69)
        ones_k = (lane >= 64) & (lane <= 66)
        qrefs, krefs = (q0_ref, q1_ref), (k0_ref, k1_ref)
        for h in range(C_HEADS):
            blk = slice(128 * ((h // 2) % 2), 128 * ((h // 2) % 2) + 128)
            qp, kp = qrefs[h // 4][:, blk], krefs[h // 4][:, blk]
            if h % 2:
                qp, kp = pltpu.roll(qp, 64, axis=1), pltpu.roll(kp, 64, axis=1)
            ch = jnp.broadcast_to(c[:, h:h + 1], (CHUNK, 128))
            hi = ch.astype(BF16).astype(F32)
            r1 = ch - hi
            mid = r1.astype(BF16).astype(F32)
            lo = r1 - mid
            aq = jnp.where(lane == 64, hi, jnp.where(lane == 65, mid, jnp.where(lane == 66, lo,
                           jnp.where(ones_q, 1.0, 0.0))))
            ak = jnp.where(lane == 67, -hi, jnp.where(lane == 68, -mid, jnp.where(lane == 69, -lo,
                           jnp.where(ones_k, 1.0, 0.0))))
            qo_ref[:, 128 * h:128 * h + 128] = jnp.where(feat, qp * Q_SCALE, aq).astype(BF16)
            ko_ref[:, 128 * h:128 * h + 128] = jnp.where(feat, kp, ak).astype(BF16)

    w = 256
    col = lambda c: pl.BlockSpec((CHUNK, w), lambda i, c=c: (i, c // w))
    return _Part(body, (proj, proj, proj, proj, proj, proj, proj, bf),
                 [col(COL_CQ), col(COL_CQ + w), col(COL_CK), col(COL_CK + w), col(COL_CV), col(COL_CV + w),
                  pl.BlockSpec((CHUNK, 128), lambda i: (i, COL_CF // 128)), pl.BlockSpec((1, 128), lambda i: (0, 0))],
                 [pl.BlockSpec((CHUNK, C_HEADS * 128), lambda i: (i, 0))] * 2
                 + [pl.BlockSpec((C_HEADS // 2, 1, 128, CHUNK), lambda i: (0, i, 0, 0))],
                 [SDS((T, C_HEADS * 128), BF16)] * 2 + [SDS((C_HEADS // 2, n, 128, CHUNK), BF16)],
                 [pltpu.VMEM((1, 128), F32)])


FOX_TILE = 512
FOX_KEYS = 512


def _fox_mask(tk, tq, k0, q0):
    return (_row((tk, tq)) + (k0 - q0)) <= _lane((tk, tq))


def _ride_refs(ride, rest, n_out, n_scratch):
    n = ride.n if ride else 0
    srcs, rest = rest[:n], rest[n:]
    outs, rest = rest[:n_out], rest[n_out:]
    dsts, rest = rest[:n], rest[n:]
    return srcs, outs, dsts, rest[:n_scratch], rest[n_scratch:]


def _ride_start(ride, grid, srcs, dsts, sems):
    if ride:
        first = functools.reduce(lambda a, b: a & b, [pl.program_id(d) == 0 for d in range(len(grid))])
        pl.when(first)(lambda: ride.start(srcs, dsts, sems))


def _ride_wait(ride, grid, srcs, dsts, sems):
    if ride:
        last = functools.reduce(lambda a, b: a & b, [pl.program_id(d) == n - 1 for d, n in enumerate(grid)])
        pl.when(last)(lambda: ride.wait(srcs, dsts, sems))


def _fox_fwd(qt, kt, vt, proj, tag, ride=None):
    T = proj.shape[0]
    tq, tk = _tile(T, FOX_TILE), _tile(T, FOX_KEYS)
    nq, nsub = T // tq, tk // CHUNK
    npair = C_HEADS // 2

    def body(q_ref, k_ref, vt_ref, z_ref, *rest):
        ride_srcs, (o_ref, lse_ref, y_ref), ride_dsts, (acc_ref, st_ref, pt_ref), ride_sems = _ride_refs(ride, rest, 3, 3)
        i = pl.program_id(1)
        _ride_start(ride, (npair, nq), ride_srcs, ride_dsts, ride_sems)

        qs = (q_ref[:, 0:128], q_ref[:, 128:256])
        acc_ref[...] = jnp.zeros_like(acc_ref)
        pt_ref[...] = jnp.zeros_like(pt_ref)
        nfull = (i * tq) // tk

        def scores(j):
            kb = k_ref[pl.ds(pl.multiple_of(j * tk, tk), tk), :]
            return tuple(_dot_nt(kb[:, 128 * h:128 * h + 128], qs[h]) for h in range(2))

        def weigh(j, h):
            rows = slice(64 * h, 64 * h + 64)
            pv = _dot(vt_ref[0, nsub * j, rows, :], pt_ref[h, 0:CHUNK, :])
            for c in range(1, nsub):
                pv = pv + _dot(vt_ref[0, nsub * j + c, rows, :], pt_ref[h, CHUNK * c:CHUNK * c + CHUNK, :])
            return pv

        def block(j, carry, diagonal):
            nxt = () if diagonal else scores(j + 1)
            pvs = [weigh(jnp.maximum(j - 1, 0), h) for h in range(2)]
            new = []
            for h in range(2):
                m, l, alpha_prev = carry[3 * h:3 * h + 3]
                st = st_ref[h]
                if diagonal:
                    st = jnp.where(_fox_mask(tk, tq, j * tk, i * tq), st, -jnp.inf)
                m_new = jnp.maximum(m, _colreduce(st, jnp.maximum))
                pt = jnp.exp(st - m_new)
                alpha = jnp.exp(m - m_new)
                rows = slice(64 * h, 64 * h + 64)
                acc_ref[rows, :] = alpha_prev * acc_ref[rows, :] + pvs[h]
                pt_ref[h] = pt.astype(BF16)
                new += [m_new, alpha * l + _colreduce(pt, jnp.add), alpha]
            for h, st in enumerate(nxt):
                st_ref[h] = st
            return tuple(new)

        for h, st in enumerate(scores(0)):
            st_ref[h] = st
        init = (jnp.full((1, tq), -jnp.inf, F32), jnp.zeros((1, tq), F32), jnp.ones((1, tq), F32)) * 2
        carry = lax.fori_loop(0, nfull, lambda j, c: block(j, c, False), init)
        m0, l0, a0, m1, l1, a1 = block(nfull, carry, True)
        for h, alpha in enumerate((a0, a1)):
            rows = slice(64 * h, 64 * h + 64)
            acc_ref[rows, :] = alpha * acc_ref[rows, :] + weigh(nfull, h)
        inv = jnp.where(_row((128, tq)) < 64, 1.0 / l0, 1.0 / l1)
        o = (acc_ref[...] * inv).T
        o_ref[...] = o
        r8 = _row((8, tq))
        lse_ref[0, 0] = jnp.where(r8 == 0, m0 + jnp.log(l0), jnp.where(r8 == 1, m1 + jnp.log(l1), 0.0))
        sz, _ = _silu_and_grad(z_ref[...])
        y_ref[...] = (o * sz).astype(BF16)
        _ride_wait(ride, (npair, nq), ride_srcs, ride_dsts, ride_sems)

    blk = pl.BlockSpec((tq, 128), lambda p, i: (i, p))
    extra = ride or _ChipExchange("gather", ())
    return pl.pallas_call(
        body, name=f"fox_fwd_{tag}", grid=(npair, nq),
        in_specs=[pl.BlockSpec((tq, 256), lambda p, i: (i, p)), pl.BlockSpec((T, 256), lambda p, i: (0, p)),
                  pl.BlockSpec((1, T // CHUNK, 128, CHUNK), lambda p, i: (p, 0, 0, 0)),
                  pl.BlockSpec((tq, 128), lambda p, i: (i, COL_CZ // 128 + p))] + extra.in_specs,
        out_specs=[blk, pl.BlockSpec((1, 1, 8, tq), lambda p, i: (p, i, 0, 0)), blk] + extra.out_specs,
        out_shape=[SDS((T, C_WIDTH), F32), SDS((npair, nq, 8, tq), F32), SDS((T, C_WIDTH), BF16)] + extra.out_shape,
        scratch_shapes=[pltpu.VMEM((128, tq), F32), pltpu.VMEM((2, tk, tq), F32), pltpu.VMEM((2, tk, tq), BF16)]
        + (extra.scratch if ride else []),
        compiler_params=pltpu.CompilerParams(dimension_semantics=("arbitrary", "arbitrary"), vmem_limit_bytes=VMEM_LIMIT,
                                             has_side_effects=bool(ride)),
    )(qt, kt, vt, proj, *extra.sources)


def _fox_bwd_prep(proj, dy, o, qt, tag):
    T = proj.shape[0]
    tq = _tile(T, FOX_TILE)
    nq = T // tq

    def body(z0_ref, z1_ref, dy_ref, o_ref, q_ref, do_ref, dl_ref, dz_ref, dot_ref, qt_ref):
        sel = jnp.where((_lane((16, 128)) >> 6) == _row((16, 128)), 1.0, 0.0).astype(BF16)
        for p, z_ref in enumerate((z0_ref, z0_ref, z1_ref, z1_ref)):
            sl = slice(128 * p, 128 * p + 128)
            sz, dsz = _silu_and_grad(z_ref[:, 128 * (p % 2):128 * (p % 2) + 128])
            dyv, ov = dy_ref[:, sl], o_ref[:, sl]
            do = dyv * sz
            do_ref[:, sl] = do.astype(BF16)
            dot_ref[p, 0] = do.T.astype(BF16)
            dz_ref[:, sl] = (dyv * ov * dsz).astype(BF16)
            hi, mid, lo = _split3(do * ov)
            dl_ref[p, 0] = (_dot_nt(sel, hi) + _dot_nt(sel, mid) + _dot_nt(sel, lo))[0:8, :]
        for h in range(C_HEADS):
            qt_ref[h, 0] = q_ref[:, 128 * h:128 * h + 128].astype(F32).T.astype(BF16)

    w = 256
    blk = pl.BlockSpec((tq, C_WIDTH), lambda i: (i, 0))
    return pl.pallas_call(
        body, name=f"fox_bwd_prep_{tag}", grid=(nq,),
        in_specs=[pl.BlockSpec((tq, w), lambda i: (i, COL_CZ // w)), pl.BlockSpec((tq, w), lambda i: (i, COL_CZ // w + 1)),
                  pl.BlockSpec((tq, C_WIDTH), lambda i: (i, (A_WIDTH + B_WIDTH) // C_WIDTH)), blk,
                  pl.BlockSpec((tq, C_HEADS * 128), lambda i: (i, 0))],
        out_specs=[blk, pl.BlockSpec((C_HEADS // 2, 1, 8, tq), lambda i: (0, i, 0, 0)), blk,
                   pl.BlockSpec((C_HEADS // 2, 1, 128, tq), lambda i: (0, i, 0, 0)),
                   pl.BlockSpec((C_HEADS, 1, 128, tq), lambda i: (0, i, 0, 0))],
        out_shape=[SDS((T, C_WIDTH), BF16), SDS((C_HEADS // 2, nq, 8, tq), F32), SDS((T, C_WIDTH), BF16),
                   SDS((C_HEADS // 2, nq, 128, tq), BF16), SDS((C_HEADS, nq, 128, tq), BF16)],
        compiler_params=_params("parallel"),
    )(proj, proj, dy, o, qt)


def _fox_bwd(qt, kt, proj, do, lse, delta, dot, qtr, tag, ride=None):
    T = proj.shape[0]
    tq, tk = _tile(T, FOX_TILE), _tile(T, FOX_KEYS)
    nq, nk = T // tq, T // tk
    assert tq == tk
    npair = C_HEADS // 2

    def body(q_ref, k_ref, v_ref, do_ref, lse_ref, dl_ref, dot_ref, qtr_ref, *rest):
        ride_srcs, (dq_ref, dk_ref, dv_ref), ride_dsts, scratch, ride_sems = _ride_refs(ride, rest, 3, 5)
        dvt_ref, dkt_ref, sc_ref, pt_ref, ds_ref = scratch
        j = pl.program_id(1)
        first = (j * tk) // tq
        _ride_start(ride, (npair, nk), ride_srcs, ride_dsts, ride_sems)

        @pl.when(j == 0)
        def _():
            dq_ref[...] = jnp.zeros_like(dq_ref)

        dkt_ref[...] = jnp.zeros_like(dkt_ref)
        dvt_ref[...] = jnp.zeros_like(dvt_ref)
        ks = (k_ref[:, 0:128], k_ref[:, 128:256])
        kts = tuple(k.astype(F32).T.astype(BF16) for k in ks)
        vb = v_ref[...].astype(BF16)
        lo = _lane((tq, 128)) < 64

        def operands(i):
            q0 = pl.multiple_of(i * tq, tq)
            qb = q_ref[pl.ds(q0, tq), :]
            dob = do_ref[pl.ds(q0, tq), :]
            qhs = (qb[:, 0:128], qb[:, 128:256])
            dohs = (jnp.where(lo, dob, jnp.zeros_like(dob)), jnp.where(lo, jnp.zeros_like(dob), dob))
            return qhs, dohs

        def scores(i):
            qhs, dohs = operands(i)
            return tuple(_dot_nt(ks[h], qhs[h]) for h in range(2)) + tuple(_dot_nt(vb, dohs[h]) for h in range(2))

        def park(sc, slot):
            for a, s in enumerate(sc):
                sc_ref[slot, a] = s

        def grads(i):
            for h in range(2):
                rows = slice(64 * h, 64 * h + 64)
                dvt_ref[rows, :] += _dot_nt(dot_ref[0, i, rows, :], pt_ref[h])
                dkt_ref[h] += _dot_nt(qtr_ref[h, i], ds_ref[h])
                dq_ref[h, i] += _dot(kts[h], ds_ref[h])

        def block(i, slot, diagonal, opening):
            park(scores(jnp.minimum(i + 1, nq - 1)), 1 - slot)
            if not opening:
                grads(i - 1)
            lsev = lse_ref[0, i]
            dlv = dl_ref[0, i]
            for h in range(2):
                pt = jnp.exp(sc_ref[slot, h] - lsev[h:h + 1, :])
                if diagonal:
                    pt = jnp.where(_fox_mask(tk, tq, j * tk, i * tq), pt, 0.0)
                ds_ref[h] = (pt * (sc_ref[slot, 2 + h] - dlv[h:h + 1, :])).astype(BF16)
                pt_ref[h] = pt.astype(BF16)

        park(scores(first), 0)
        block(first, 0, True, True)
        rest = nq - 1 - first

        def two_steps(t, carry):
            block(first + 1 + 2 * t, 1, False, False)
            block(first + 2 + 2 * t, 0, False, False)
            return carry

        lax.fori_loop(0, rest // 2, two_steps, 0)
        pl.when(rest % 2 == 1)(lambda: block(nq - 1, 1, False, False))
        grads(nq - 1)
        dv_ref[...] = dvt_ref[...].T.astype(BF16)
        for h in range(2):
            dk_ref[:, 128 * h:128 * h + 128] = dkt_ref[h].T
        _ride_wait(ride, (npair, nk), ride_srcs, ride_dsts, ride_sems)

    full = lambda w: pl.BlockSpec((T, w), lambda p, j: (0, p))
    stat = pl.BlockSpec((1, nq, 8, tq), lambda p, j: (p, 0, 0, 0))
    extra = ride or _ChipExchange("gather", ())
    return pl.pallas_call(
        body, name=f"fox_bwd_{tag}", grid=(npair, nk),
        in_specs=[full(256), pl.BlockSpec((tk, 256), lambda p, j: (j, p)),
                  pl.BlockSpec((tk, 128), lambda p, j: (j, COL_CV // 128 + p)), full(128), stat, stat,
                  pl.BlockSpec((1, nq, 128, tq), lambda p, j: (p, 0, 0, 0)),
                  pl.BlockSpec((2, nq, 128, tq), lambda p, j: (p, 0, 0, 0))] + extra.in_specs,
        out_specs=[pl.BlockSpec((2, nq, 128, tq), lambda p, j: (p, 0, 0, 0)), pl.BlockSpec((tk, 256), lambda p, j: (j, p)),
                   pl.BlockSpec((tk, 128), lambda p, j: (j, p))] + extra.out_specs,
        out_shape=[SDS((C_HEADS, nq, 128, tq), F32), SDS((T, C_HEADS * 128), F32), SDS((T, C_WIDTH), BF16)]
        + extra.out_shape,
        scratch_shapes=[pltpu.VMEM((128, tk), F32), pltpu.VMEM((2, 128, tk), F32), pltpu.VMEM((2, 4, tk, tq), F32),
                        pltpu.VMEM((2, tk, tq), BF16), pltpu.VMEM((2, tk, tq), BF16)] + (extra.scratch if ride else []),
        compiler_params=pltpu.CompilerParams(dimension_semantics=("arbitrary", "arbitrary"), vmem_limit_bytes=VMEM_LIMIT,
                                             has_side_effects=bool(ride)),
    )(qt, kt, proj, do, lse, delta, dot, qtr, *extra.sources)


def _fox_bwd_post(dqt, dkt, proj, bf, tag):
    T = proj.shape[0]
    tq = _tile(T, FOX_TILE)
    n = T // tq

    def body(dq_ref, dk_ref, fl_ref, bf_ref, oq_ref, ok_ref, ofl_ref, dbf_ref, carry_ref):
        @pl.when(pl.program_id(0) == 0)
        def _():
            carry_ref[...] = jnp.zeros_like(carry_ref)
            dbf_ref[...] = jnp.zeros_like(dbf_ref)

        lane = _lane((tq, 128))
        lo = lane < 64
        dqs = [dq_ref[h, 0].T for h in range(C_HEADS)]
        dc = jnp.zeros((tq, 128), F32)
        for h in range(C_HEADS):
            dc = dc + jnp.where(lane == h, dqs[h][:, 64:65] - dk_ref[:, 128 * h + 67:128 * h + 68], 0.0)
        utri = jnp.where(_lane((tq, tq)) >= _row((tq, tq)), 1.0, 0.0).astype(BF16)
        dlf = _dot3_left(utri, dc) + carry_ref[...]
        carry_ref[...] = dlf[0:1, :]
        dfl = jnp.where(lane < C_HEADS, dlf * _sigmoid(-(fl_ref[...] + bf_ref[...])), 0.0)
        ofl_ref[...] = dfl.astype(BF16)
        dbf_ref[...] += jnp.sum(dfl, axis=0, keepdims=True)
        for p in range(C_HEADS // 2):
            a, b = 128 * (2 * p), 128 * (2 * p + 1)
            oq_ref[:, 128 * p:128 * p + 128] = (
                jnp.where(lo, dqs[2 * p], pltpu.roll(dqs[2 * p + 1], 64, axis=1)) * Q_SCALE).astype(BF16)
            ok_ref[:, 128 * p:128 * p + 128] = jnp.where(
                lo, dk_ref[:, a:a + 128], pltpu.roll(dk_ref[:, b:b + 128], 64, axis=1)).astype(BF16)

    rev = lambda w: pl.BlockSpec((tq, w), lambda i: (n - 1 - i, 0))
    return pl.pallas_call(
        body, name=f"fox_bwd_post_{tag}", grid=(n,),
        in_specs=[pl.BlockSpec((C_HEADS, 1, 128, tq), lambda i: (0, n - 1 - i, 0, 0)), rev(C_HEADS * 128),
                  pl.BlockSpec((tq, 128), lambda i: (n - 1 - i, COL_CF // 128)), pl.BlockSpec((1, 128), lambda i: (0, 0))],
        out_specs=[rev(C_WIDTH), rev(C_WIDTH), rev(128), pl.BlockSpec((1, 128), lambda i: (0, 0))],
        out_shape=[SDS((T, C_WIDTH), BF16), SDS((T, C_WIDTH), BF16), SDS((T, 128), BF16), SDS((1, 128), F32)],
        scratch_shapes=[pltpu.VMEM((1, 128), F32)], compiler_params=_params("arbitrary"),
    )(dqt, dkt, proj, bf)


def _adamw_math(w, g, m, v):
    m = ADAM_B1 * m + (1.0 - ADAM_B1) * g
    v = ADAM_B2 * v + (1.0 - ADAM_B2) * (g * g)
    delta = -ADAM_LR * ((m / ADAM_C1) / (jnp.sqrt(v / ADAM_C2) + ADAM_EPS) + ADAM_WD * w)
    return delta, m, v


def _adamw_pair(w, m, v, ga, gb, name):
    n0 = w.shape[0]
    most = max(1, ADAMW_BLOCK_BYTES // (4 * math.prod(w.shape[1:])))
    t0 = max(t for t in range(1, min(n0, most) + 1) if n0 % t == 0)

    def body(w_ref, m_ref, v_ref, ga_ref, gb_ref, g_ref, d_ref, nm_ref, nv_ref):
        g = ga_ref[...] + gb_ref[...]
        g_ref[...] = g
        d_ref[...], nm_ref[...], nv_ref[...] = _adamw_math(w_ref[...], g, m_ref[...], v_ref[...])

    blk = pl.BlockSpec((t0,) + w.shape[1:], lambda i: (i, 0, 0))
    return pl.pallas_call(
        body, name=name, grid=(n0 // t0,), in_specs=[blk] * 5, out_specs=[blk] * 4,
        out_shape=[SDS(w.shape, F32)] * 4, compiler_params=_params("parallel"),
    )(w, m, v, ga, gb)


def _adamw_small(w, m, v, gall):
    R = w.shape[0]

    def body(w_ref, m_ref, v_ref, g_ref, go_ref, d_ref, nm_ref, nv_ref):
        g = g_ref[0]
        for k in range(1, N_DEV):
            g = g + g_ref[k]
        go_ref[...] = g
        d_ref[...], nm_ref[...], nv_ref[...] = _adamw_math(w_ref[...], g, m_ref[...], v_ref[...])

    return pl.pallas_call(body, name="adamw_small", out_shape=[SDS((R, 128), F32)] * 4,
                          compiler_params=pltpu.CompilerParams(vmem_limit_bytes=VMEM_LIMIT))(w, m, v, gall)


def _sum_chips(layers, name, layer_major):
    _, R, C = layers[0].shape
    L = len(layers)
    tc = _tile(C, 256)

    def body(*refs):
        o_ref = refs[-1]
        for l, p_ref in enumerate(refs[:-1]):
            p = [p_ref[k].astype(F32) for k in range(N_CHIPS)]
            s = ((p[0] + p[1]) + p[2]) + p[3]
            if layer_major:
                o_ref[l] = s
            else:
                o_ref[:, l, :] = s

    out = (L, R, C) if layer_major else (R, L, C)
    out_blk = (L, R, tc) if layer_major else (R, L, tc)
    return pl.pallas_call(
        body, name=name, grid=(C // tc,),
        in_specs=[pl.BlockSpec((N_CHIPS, R, tc), lambda i: (0, 0, i))] * L,
        out_specs=pl.BlockSpec(out_blk, lambda i: (0, 0, i)), out_shape=SDS(out, F32),
        compiler_params=_params("parallel"),
    )(*layers)


ANY = pl.BlockSpec(memory_space=pl.ANY)


def _mesh_pos():
    return lax.axis_index("x"), lax.axis_index("y"), lax.axis_index("c")


def _other_chips(x, y):
    return [(1 - x, y), (x, 1 - y), (1 - x, 1 - y)]


class _ChipExchange:
    def __init__(self, mode, sources):
        assert mode in ("gather", "scatter")
        self.mode, self.sources = mode, tuple(sources)
        self.n = len(self.sources)
        self.in_specs = [ANY] * self.n
        self.out_specs = [ANY] * self.n
        self.out_shape = [SDS(((N_CHIPS,) + s.shape) if mode == "gather" else s.shape, s.dtype) for s in self.sources]
        self.scratch = [pltpu.SemaphoreType.DMA((3 * self.n,)), pltpu.SemaphoreType.DMA((3 * self.n,)),
                        pltpu.SemaphoreType.DMA((self.n,))]

    def _copies(self, srcs, dsts, send_sems, recv_sems, local_sems):
        x, y, c = _mesh_pos()
        me = 2 * x + y
        view = (lambda r, chip: r) if self.mode == "gather" else (lambda r, chip: r.at[chip])
        local = [pltpu.make_async_copy(view(s, me), d.at[me], local_sems.at[a]) for a, (s, d) in enumerate(zip(srcs, dsts))]
        sends, recvs = [], []
        for j, (px, py) in enumerate(_other_chips(x, y)):
            peer = 2 * px + py
            for a, (s, d) in enumerate(zip(srcs, dsts)):
                sems = dict(send_sem=send_sems.at[self.n * j + a], recv_sem=recv_sems.at[self.n * j + a],
                            device_id=(px, py, c), device_id_type=MESH_ID)
                sends.append(pltpu.make_async_remote_copy(src_ref=view(s, peer), dst_ref=d.at[me], **sems))
                recvs.append(pltpu.make_async_remote_copy(src_ref=view(s, me), dst_ref=d.at[peer], **sems))
        return local, sends, recvs

    def start(self, srcs, dsts, sems):
        local, sends, _ = self._copies(srcs, dsts, *sems)
        for cp in local + sends:
            cp.start()

    def wait(self, srcs, dsts, sems):
        local, sends, recvs = self._copies(srcs, dsts, *sems)
        for cp in recvs:
            cp.wait_recv()
        for cp in sends:
            cp.wait_send()
        for cp in local:
            cp.wait()


def _gather_halves(w, tag):
    R, C = w.shape
    H = R // 2

    def body(w_ref, g_ref, send_sems, recv_sems, pass_send, pass_recv, local_sem):
        x, y, c = _mesh_pos()
        me = 2 * x + y
        mine, theirs = pl.ds(c * H, H), pl.ds((1 - c) * H, H)
        own = pltpu.make_async_copy(w_ref, g_ref.at[me], local_sem)
        own.start()

        def fetch(j, px, py, src, dst):
            return pltpu.make_async_remote_copy(src_ref=src, dst_ref=dst, send_sem=send_sems.at[j], recv_sem=recv_sems.at[j],
                                                device_id=(px, py, c), device_id_type=MESH_ID)

        def hand(j, rows, peer):
            return pltpu.make_async_remote_copy(src_ref=g_ref.at[peer, rows], dst_ref=g_ref.at[peer, rows],
                                                send_sem=pass_send.at[j], recv_sem=pass_recv.at[j],
                                                device_id=(x, y, 1 - c), device_id_type=MESH_ID)

        chips = _other_chips(x, y)
        sends = [fetch(j, px, py, w_ref.at[mine], g_ref.at[me, mine]) for j, (px, py) in enumerate(chips)]
        for cp in sends:
            cp.start()
        passed = []
        for j, (px, py) in enumerate(chips):
            peer = 2 * px + py
            fetch(j, px, py, w_ref.at[mine], g_ref.at[peer, mine]).wait_recv()
            passed.append(hand(j, mine, peer))
            passed[-1].start()
        for j, (px, py) in enumerate(chips):
            hand(j, theirs, 2 * px + py).wait_recv()
        for cp in sends + passed:
            cp.wait_send()
        own.wait()

    return pl.pallas_call(
        body, name=f"gather_halves_{tag}", in_specs=[ANY], out_specs=ANY, out_shape=SDS((N_CHIPS, R, C), w.dtype),
        scratch_shapes=[pltpu.SemaphoreType.DMA((3,)), pltpu.SemaphoreType.DMA((3,)), pltpu.SemaphoreType.DMA((3,)),
                        pltpu.SemaphoreType.DMA((3,)), pltpu.SemaphoreType.DMA],
        compiler_params=pltpu.CompilerParams(has_side_effects=True),
    )(w)


class _DeviceGather:
    def __init__(self, source):
        self.sources, self.n = (source,), 1
        self.in_specs, self.out_specs = [ANY], [ANY]
        self.out_shape = [SDS((N_DEV,) + source.shape, source.dtype)]
        self.scratch = [pltpu.SemaphoreType.DMA((N_DEV - 1,)), pltpu.SemaphoreType.DMA((N_DEV - 1,)),
                        pltpu.SemaphoreType.DMA((1,))]

    def _copies(self, srcs, dsts, send_sems, recv_sems, local_sems):
        (src,), (dst,) = srcs, dsts
        x, y, c = _mesh_pos()
        me = 4 * x + 2 * y + c
        local = [pltpu.make_async_copy(src, dst.at[me], local_sems.at[0])]
        sends, recvs = [], []
        for k in range(1, N_DEV):
            px, py, pc = (1 - x) if k & 4 else x, (1 - y) if k & 2 else y, (1 - c) if k & 1 else c
            sems = dict(send_sem=send_sems.at[k - 1], recv_sem=recv_sems.at[k - 1], device_id=(px, py, pc),
                        device_id_type=MESH_ID)
            sends.append(pltpu.make_async_remote_copy(src_ref=src, dst_ref=dst.at[me], **sems))
            recvs.append(pltpu.make_async_remote_copy(src_ref=src, dst_ref=dst.at[4 * px + 2 * py + pc], **sems))
        return local, sends, recvs

    start = _ChipExchange.start
    wait = _ChipExchange.wait


def _gather_devices(a, name):
    ex = _DeviceGather(a)

    def body(a_ref, g_ref, *sems):
        ex.start((a_ref,), (g_ref,), sems)
        ex.wait((a_ref,), (g_ref,), sems)

    return pl.pallas_call(
        body, name=name, in_specs=ex.in_specs, out_specs=ex.out_specs[0], out_shape=ex.out_shape[0],
        scratch_shapes=ex.scratch, compiler_params=pltpu.CompilerParams(has_side_effects=True),
    )(a)


def _swap_cores(pin, pout):
    def body(pin_ref, pout_ref, oin_ref, oout_ref, send_sems, recv_sems):
        x, y, c = _mesh_pos()
        cps = [pltpu.make_async_remote_copy(src_ref=src, dst_ref=dst, send_sem=send_sems.at[a], recv_sem=recv_sems.at[a],
                                            device_id=(x, y, 1 - c), device_id_type=MESH_ID)
               for a, (src, dst) in enumerate(((pin_ref, oin_ref), (pout_ref, oout_ref)))]
        for cp in cps:
            cp.start()
        for cp in cps:
            cp.wait()

    return pl.pallas_call(
        body, name="swap_cores", in_specs=[ANY, ANY], out_specs=[ANY, ANY],
        out_shape=[SDS(pin.shape, F32), SDS(pout.shape, F32)],
        scratch_shapes=[pltpu.SemaphoreType.DMA((2,)), pltpu.SemaphoreType.DMA((2,))],
        compiler_params=pltpu.CompilerParams(has_side_effects=True),
    )(pin, pout)


PACK_TILE = 8 * 128


def _pack_rows(size):
    return (size + PACK_TILE - 1) // PACK_TILE * 8


def _pack_small(parts):
    return jnp.concatenate([jnp.pad(p.reshape(-1), (0, (-p.size) % PACK_TILE)).reshape(-1, 128) for p in parts])


def _unpack_small(packed):
    out, row = [], 0
    for _, shape in SMALL_PARAMS:
        size = math.prod(shape)
        rows = packed[row:row + _pack_rows(size)]
        out.append(rows.reshape(-1)[:size].reshape(shape))
        row += _pack_rows(size)
    return out


def _layer_consts(l, gmlp_ln_g, gmlp_ln_b, gmlp_w_s, gmlp_b_s, hgrn_onorm_g, fox_b_f):
    causal = jnp.tril(jnp.ones((CHUNK, CHUNK), bool))
    wm = jnp.where(causal[None], gmlp_w_s[l], 0.0)
    return dict(
        lng=gmlp_ln_g[l].reshape(1, A_WIDTH), lnb=gmlp_ln_b[l].reshape(1, A_WIDTH),
        wm=wm.astype(BF16), wmt=jnp.swapaxes(wm, 1, 2).astype(BF16),
        bst=jnp.pad(gmlp_b_s[l].T, ((0, 0), (0, 128 - A_GROUPS))),
        onorm=jnp.tile(hgrn_onorm_g[l], 4).reshape(1, B_WIDTH),
        bf=jnp.pad(fox_b_f[l], (0, 128 - C_HEADS)).reshape(1, 128),
    )


def kernel(x, norm_g, w_in, w_out, gmlp_ln_g, gmlp_ln_b, gmlp_w_s, gmlp_b_s, hgrn_lb, hgrn_onorm_g, fox_b_f, final_norm_g, loss_target, m_norm_g, m_w_in, m_w_out, m_gmlp_ln_g, m_gmlp_ln_b, m_gmlp_w_s, m_gmlp_b_s, m_hgrn_lb, m_hgrn_onorm_g, m_fox_b_f, m_final_norm_g, v_norm_g, v_w_in, v_w_out, v_gmlp_ln_g, v_gmlp_ln_b, v_gmlp_w_s, v_gmlp_b_s, v_hgrn_lb, v_hgrn_onorm_g, v_fox_b_f, v_final_norm_g):
    T = x.shape[1]
    shard_in = w_in.shape[2]
    shard_out = w_out.shape[1]
    xs = x.reshape(T, D_MODEL)
    tgt = loss_target.reshape(T, D_MODEL)

    w_in_b, w_out_b = w_in.astype(BF16), w_out.astype(BF16)

    def full_w_in(gathered):
        return jnp.concatenate([gathered[k] for k in range(N_CHIPS)] + [jnp.zeros((D_MODEL, D_IN_PAD - D_IN), BF16)], axis=-1)

    lb_all = _lb_fwd(hgrn_lb)
    consts = [_layer_consts(l, gmlp_ln_g, gmlp_ln_b, gmlp_w_s, gmlp_b_s, hgrn_onorm_g, fox_b_f) for l in range(DEPTH)]

    saved = []
    xl = xs
    w_in_l = full_w_in(_gather_halves(w_in_b[0], "w_in_l0"))
    for l in range(DEPTH):
        cs = consts[l]
        tag = f"l{l}"
        h, proj = _inproj(xl, norm_g[l].reshape(1, D_MODEL), w_in_l, tag)
        (ya,), (yb, ob, s0), (qt, kt, vt) = _run_parts(
            [_gmlp_fwd(proj, cs["lng"], cs["lnb"], cs["wm"], cs["bst"]),
             _hgrn_fwd(proj, lb_all[l].reshape(1, B_WIDTH), cs["onorm"]), _fox_prep(proj, cs["bf"])],
            (T // CHUNK,), f"mix_fwd_{tag}")
        ride = _ChipExchange("gather", (w_out_b[l],) + ((w_in_b[l + 1],) if l + 1 < DEPTH else ()))
        oc, lse, yc, *gathered = _fox_fwd(qt, kt, vt, proj, tag, ride)
        w_out_l = gathered[0].reshape(N_CHIPS * shard_out, D_MODEL)
        saved.append(dict(x=xl, h=h, proj=proj, ya=ya, yb=yb, yc=yc, ob=ob, s0=s0, qt=qt, kt=kt, oc=oc, lse=lse,
                          w_in=w_in_l, w_out=w_out_l))
        xl = _outproj(xl, ya, yb, yc, w_out_l, tag)
        if l + 1 < DEPTH:
            w_in_l = full_w_in(gathered[1])

    dx, loss_part, d_final = _loss_head(xl, final_norm_g.reshape(1, D_MODEL), tgt)

    g_small = {}
    dlb_rows, rin, rout = [None] * DEPTH, [None] * DEPTH, [None] * DEPTH
    slabs_in = None
    stack = lambda key: jnp.stack([g_small[l][key] for l in range(DEPTH)])
    for l in reversed(range(DEPTH)):
        cs, sv = consts[l], saved[l]
        tag = f"l{l}"
        proj = sv["proj"]
        dy, dw_out = _outproj_bwd(dx, sv["ya"], sv["yb"], sv["yc"], sv["w_out"], tag)
        (da, dwm, dbst, dlng, dlnb), (db, dlb_rows[l], donorm) = _run_parts(
            [_gmlp_bwd(proj, dy, cs["lng"], cs["lnb"], cs["wm"], cs["wmt"], cs["bst"]),
             _hgrn_bwd(proj, dy, sv["ob"], sv["s0"], lb_all[l].reshape(1, B_WIDTH), cs["onorm"])],
            (T // CHUNK,), f"mix_bwd_{tag}")
        do, delta, dzc, dot, qtr = _fox_bwd_prep(proj, dy, sv["oc"], sv["qt"], tag)
        slabs_out = dw_out.reshape(N_CHIPS, shard_out, D_MODEL).astype(BF16)
        ride = _ChipExchange("scatter", (slabs_out,) + ((slabs_in,) if slabs_in is not None else ()))
        dqt, dkt, dvc, *received = _fox_bwd(sv["qt"], sv["kt"], proj, do, sv["lse"], delta, dot, qtr, tag, ride)
        rout[l] = received[0]
        if slabs_in is not None:
            rin[l + 1] = received[1]
        dqc, dkc, dflc, dbf = _fox_bwd_post(dqt, dkt, proj, cs["bf"], tag)
        g_small[l] = dict(ln_g=dlng.reshape(4, 64), ln_b=dlnb.reshape(4, 64), w_s=dwm, b_s=dbst[:, :A_GROUPS].T,
                          onorm=donorm[0, :64], bf=dbf[0, :C_HEADS])
        dproj = jnp.concatenate([da, db, dqc, dkc, dvc, dzc, dflc, jnp.zeros((T, 128), BF16)], axis=1)
        if l == 0:
            d_hgrn_lb = _lb_bwd(hgrn_lb, jnp.concatenate(dlb_rows, axis=0))
            early = _pack_small([stack("ln_g"), stack("ln_b"), stack("w_s"), stack("b_s"), d_hgrn_lb, stack("onorm"),
                                 stack("bf"), d_final.reshape(D_MODEL), loss_part.reshape(128)])
            dw_in, rearly = _dw_in(sv["h"], dproj, tag, _DeviceGather(early))
        else:
            dw_in = _dw_in(sv["h"], dproj, tag)
        slabs_in = dw_in[:N_CHIPS * shard_in].reshape(N_CHIPS, shard_in, D_MODEL).astype(BF16)
        ride = _ChipExchange("scatter", (slabs_in,)) if l == 0 else None
        dx, dng, *received = _dx_in(sv["x"], norm_g[l].reshape(1, D_MODEL), dx, dproj, sv["w_in"], tag, ride)
        if l == 0:
            rin[0] = received[0]
        g_small[l]["norm_g"] = dng.reshape(D_MODEL)
    grad_x = dx.reshape(x.shape)
    rlate = _gather_devices(_pack_small([stack("norm_g")]), "gather_norm_grads")
    rsmall = jnp.concatenate([rlate, rearly], axis=1)

    pin, pout = _sum_chips(rin, "sum_chips_w_in", False), _sum_chips(rout, "sum_chips_w_out", True)
    oin, oout = _swap_cores(pin, pout)
    to_view = lambda a: jnp.transpose(a, (2, 0, 1))
    g_w_in, d_w_in, nm_w_in, nv_w_in = [
        jnp.transpose(o, (1, 2, 0))
        for o in _adamw_pair(to_view(w_in), to_view(m_w_in), to_view(v_w_in), pin, oin, "adamw_w_in")]
    g_w_out, d_w_out, nm_w_out, nv_w_out = _adamw_pair(w_out, m_w_out, v_w_out, pout, oout, "adamw_w_out")

    small_w = [norm_g, gmlp_ln_g, gmlp_ln_b, gmlp_w_s, gmlp_b_s, hgrn_lb, hgrn_onorm_g, fox_b_f, final_norm_g]
    small_m = [m_norm_g, m_gmlp_ln_g, m_gmlp_ln_b, m_gmlp_w_s, m_gmlp_b_s, m_hgrn_lb, m_hgrn_onorm_g, m_fox_b_f, m_final_norm_g]
    small_v = [v_norm_g, v_gmlp_ln_g, v_gmlp_ln_b, v_gmlp_w_s, v_gmlp_b_s, v_hgrn_lb, v_hgrn_onorm_g, v_fox_b_f, v_final_norm_g]
    slot = [jnp.zeros((128,), F32)]
    outs = _adamw_small(_pack_small(small_w + slot), _pack_small(small_m + slot), _pack_small(small_v + slot), rsmall)
    sg, sd, sm, sv_ = [_unpack_small(o) for o in outs]
    loss = outs[0][sum(_pack_rows(math.prod(s)) for _, s in SMALL_PARAMS), 0]

    def order(big_in, big_out, small):
        return [small[0], big_in, big_out] + small[1:]

    return (loss, grad_x, *order(g_w_in, g_w_out, sg), *order(d_w_in, d_w_out, sd), *order(nm_w_in, nm_w_out, sm),
            *order(nv_w_in, nv_w_out, sv_))
```

```python
import collections
import functools
import math

import jax
import jax.numpy as jnp
from jax import lax
from jax.experimental import pallas as pl
from jax.experimental.pallas import tpu as pltpu

F32 = jnp.float32
BF16 = jnp.bfloat16
SDS = jax.ShapeDtypeStruct
MESH_ID = pl.DeviceIdType.MESH

D_MODEL = 1024
DEPTH = 2
A_WIDTH = 256
A_GROUPS = 4
B_WIDTH = 256
C_WIDTH = 512
C_HEADS = 8
D_IN = 3848
D_IN_PAD = 4096
CHUNK = 128
SUB = 16
SUB_SHIFT = 4
NORM_EPS = 1e-6
F_FLOOR = 1e-30
COL_AU, COL_AV, COL_AZ = 0, 256, 512
COL_BQ, COL_BF, COL_BI, COL_BZ = 768, 1024, 1280, 1536
COL_CQ, COL_CK, COL_CV, COL_CZ, COL_CF = 1792, 2304, 2816, 3328, 3840
HEAD_LANES = 128
Q_SCALE = 0.125
ADAM_LR, ADAM_B1, ADAM_B2, ADAM_EPS, ADAM_WD, ADAM_STEP = 0.001, 0.9, 0.999, 1e-08, 0.01, 10
ADAM_C1 = 1.0 - ADAM_B1 ** ADAM_STEP
ADAM_C2 = 1.0 - ADAM_B2 ** ADAM_STEP
VMEM_LIMIT = 56 * 1024 * 1024
ADAMW_BLOCK_BYTES = 1 << 20
N_CHIPS = 4
N_DEV = 8

SMALL_PARAMS = (
    ("norm_g", (DEPTH, D_MODEL)), ("gmlp_ln_g", (DEPTH, 4, 64)), ("gmlp_ln_b", (DEPTH, 4, 64)),
    ("gmlp_w_s", (DEPTH, 4, 128, 128)), ("gmlp_b_s", (DEPTH, 4, 128)), ("hgrn_lb", (DEPTH, 256)),
    ("hgrn_onorm_g", (DEPTH, 64)), ("fox_b_f", (DEPTH, 8)), ("final_norm_g", (D_MODEL,)),
)


def _tile(n, pref):
    t = min(n, pref)
    assert n % t == 0, (n, pref)
    return t


def _params(*sem):
    return pltpu.CompilerParams(dimension_semantics=sem, vmem_limit_bytes=VMEM_LIMIT)


_Part = collections.namedtuple("_Part", "body operands in_specs out_specs out_shape scratch")


def _run_parts(parts, grid, name):
    counts = [(len(p.operands), len(p.out_shape), len(p.scratch)) for p in parts]

    def body(*refs):
        ins, outs, scr = [], [], []
        pos = 0
        for group, k in ((ins, 0), (outs, 1), (scr, 2)):
            for c in counts:
                group.append(refs[pos:pos + c[k]])
                pos += c[k]
        for p, i, o, s in zip(parts, ins, outs, scr):
            p.body(*i, *o, *s)

    flat = lambda key: [x for p in parts for x in getattr(p, key)]
    res = pl.pallas_call(
        body, name=name, grid=grid, in_specs=flat("in_specs"), out_specs=flat("out_specs"), out_shape=flat("out_shape"),
        scratch_shapes=flat("scratch"), compiler_params=_params(*(("arbitrary",) * len(grid))),
    )(*flat("operands"))
    out, pos = [], 0
    for c in counts:
        out.append(list(res[pos:pos + c[1]]))
        pos += c[1]
    return out


def _dot(a, b):
    return jnp.dot(a, b, preferred_element_type=F32)


def _dot_nt(a, b):
    return lax.dot_general(a, b, (((1,), (1,)), ((), ())), preferred_element_type=F32)


def _dot_tn(a, b):
    return lax.dot_general(a, b, (((0,), (0,)), ((), ())), preferred_element_type=F32)


def _split3(x):
    hi = x.astype(BF16)
    r = x - hi.astype(F32)
    mid = r.astype(BF16)
    lo = (r - mid.astype(F32)).astype(BF16)
    return hi, mid, lo


def _dot3_left(c, x):
    hi, mid, lo = _split3(x)
    return _dot(c, hi) + _dot(c, mid) + _dot(c, lo)


def _sigmoid(x):
    return jax.nn.sigmoid(x)


def _silu_and_grad(x):
    s = _sigmoid(x)
    return x * s, s * (1.0 + x * (1.0 - s))


_GELU_C = math.sqrt(2.0 / math.pi)


def _gelu_and_grad(x):
    inner = _GELU_C * (x + 0.044715 * x * x * x)
    t = jnp.tanh(inner)
    y = 0.5 * x * (1.0 + t)
    dy = 0.5 * (1.0 + t) + 0.5 * x * (1.0 - t * t) * _GELU_C * (1.0 + 3.0 * 0.044715 * x * x)
    return y, dy


def _lane(shape):
    return lax.broadcasted_iota(jnp.int32, shape, 1)


def _row(shape):
    return lax.broadcasted_iota(jnp.int32, shape, 0)


def _gsum64(x):
    lo = _lane(x.shape) < 64
    s0 = jnp.sum(jnp.where(lo, x, 0.0), axis=-1, keepdims=True)
    s1 = jnp.sum(jnp.where(lo, 0.0, x), axis=-1, keepdims=True)
    return jnp.where(lo, s0, s1)


def _colreduce(x, op):
    parts = [x[r:r + 8, :] for r in range(0, x.shape[0], 8)]
    while len(parts) > 1:
        pairs = [op(parts[k], parts[k + 1]) for k in range(0, len(parts) - 1, 2)]
        parts = pairs + ([parts[-1]] if len(parts) % 2 else [])
    red = jnp.max if op is jnp.maximum else jnp.sum
    return red(parts[0], axis=0, keepdims=True)


def _block_diag64(dtype=BF16):
    r, c = _row((128, 128)), _lane((128, 128))
    return jnp.where((r >> 6) == (c >> 6), 1.0, 0.0).astype(dtype)


def _inproj(x, g, w, tag):
    T, D = x.shape
    DP = w.shape[1]
    tm = _tile(T, 512)

    def body(x_ref, g_ref, w_ref, h_ref, p_ref):
        xv = x_ref[...]
        r = lax.rsqrt(jnp.mean(xv * xv, axis=-1, keepdims=True) + NORM_EPS)
        h = (xv * r * g_ref[...]).astype(BF16)
        h_ref[...] = h
        p_ref[...] = _dot(h, w_ref[...])

    return pl.pallas_call(
        body, name=f"inproj_{tag}", grid=(T // tm,),
        in_specs=[pl.BlockSpec((tm, D), lambda i: (i, 0)), pl.BlockSpec((1, D), lambda i: (0, 0)),
                  pl.BlockSpec((D, DP), lambda i: (0, 0))],
        out_specs=[pl.BlockSpec((tm, D), lambda i: (i, 0)), pl.BlockSpec((tm, DP), lambda i: (i, 0))],
        out_shape=[SDS((T, D), BF16), SDS((T, DP), F32)],
        compiler_params=_params("parallel"),
    )(x, g, w)


def _outproj(x, ya, yb, yc, wo, tag):
    T, D = x.shape
    tm = _tile(T, 512)

    def body(x_ref, ya_ref, yb_ref, yc_ref, wo_ref, o_ref):
        acc = x_ref[...] + _dot(ya_ref[...], wo_ref[0:A_WIDTH, :])
        acc = acc + _dot(yb_ref[...], wo_ref[A_WIDTH:A_WIDTH + B_WIDTH, :])
        o_ref[...] = acc + _dot(yc_ref[...], wo_ref[A_WIDTH + B_WIDTH:, :])

    row = lambda w: pl.BlockSpec((tm, w), lambda i: (i, 0))
    return pl.pallas_call(
        body, name=f"outproj_{tag}", grid=(T // tm,),
        in_specs=[row(D), row(A_WIDTH), row(B_WIDTH), row(C_WIDTH), pl.BlockSpec(wo.shape, lambda i: (0, 0))],
        out_specs=row(D), out_shape=SDS((T, D), F32), compiler_params=_params("parallel"),
    )(x, ya, yb, yc, wo)


def _outproj_bwd(dx, ya, yb, yc, wo, tag):
    T, D = dx.shape
    DM = wo.shape[0]
    tm = _tile(T, 512)

    def body(dx_ref, ya_ref, yb_ref, yc_ref, wo_ref, dy_ref, dwo_ref):
        @pl.when(pl.program_id(0) == 0)
        def _():
            dwo_ref[...] = jnp.zeros_like(dwo_ref)

        dxb = dx_ref[...].astype(BF16)
        dy_ref[...] = _dot_nt(dxb, wo_ref[...])
        dwo_ref[0:A_WIDTH, :] += _dot_tn(ya_ref[...], dxb)
        dwo_ref[A_WIDTH:A_WIDTH + B_WIDTH, :] += _dot_tn(yb_ref[...], dxb)
        dwo_ref[A_WIDTH + B_WIDTH:, :] += _dot_tn(yc_ref[...], dxb)

    row = lambda w: pl.BlockSpec((tm, w), lambda i: (i, 0))
    return pl.pallas_call(
        body, name=f"outproj_bwd_{tag}", grid=(T // tm,),
        in_specs=[row(D), row(A_WIDTH), row(B_WIDTH), row(C_WIDTH), pl.BlockSpec(wo.shape, lambda i: (0, 0))],
        out_specs=[row(DM), pl.BlockSpec((DM, D), lambda i: (0, 0))],
        out_shape=[SDS((T, DM), F32), SDS((DM, D), F32)], compiler_params=_params("arbitrary"),
    )(dx, ya, yb, yc, wo)


def _dw_in(h, dproj, tag, ride=None):
    T, D = h.shape
    DP = dproj.shape[1]
    tm, tn = _tile(T, 1024), _tile(DP, 1024)
    grid = (DP // tn, T // tm)

    def body(h_ref, dp_ref, *rest):
        ride_srcs, (dw_ref,), ride_dsts, _, ride_sems = _ride_refs(ride, rest, 1, 0)
        _ride_start(ride, grid, ride_srcs, ride_dsts, ride_sems)

        @pl.when(pl.program_id(1) == 0)
        def _():
            dw_ref[...] = jnp.zeros_like(dw_ref)

        dw_ref[...] += _dot_tn(dp_ref[...], h_ref[...])
        _ride_wait(ride, grid, ride_srcs, ride_dsts, ride_sems)

    extra = ride or _ChipExchange("gather", ())
    out = pl.pallas_call(
        body, name=f"dw_in_{tag}", grid=grid,
        in_specs=[pl.BlockSpec((tm, D), lambda j, i: (i, 0)), pl.BlockSpec((tm, tn), lambda j, i: (i, j))] + extra.in_specs,
        out_specs=[pl.BlockSpec((tn, D), lambda j, i: (j, 0))] + extra.out_specs,
        out_shape=[SDS((DP, D), F32)] + extra.out_shape, scratch_shapes=extra.scratch if ride else [],
        compiler_params=pltpu.CompilerParams(dimension_semantics=("arbitrary", "arbitrary"), vmem_limit_bytes=VMEM_LIMIT,
                                             has_side_effects=bool(ride)),
    )(h, dproj, *extra.sources)
    return out if ride else out[0]


def _dx_in(x, g, dres, dproj, w, tag, ride=None):
    T, D = x.shape
    DP = w.shape[1]
    tm = _tile(T, 512)
    grid = (T // tm,)

    def body(x_ref, g_ref, dres_ref, dp_ref, w_ref, *rest):
        ride_srcs, (dx_ref, dg_ref), ride_dsts, _, ride_sems = _ride_refs(ride, rest, 2, 0)
        _ride_start(ride, grid, ride_srcs, ride_dsts, ride_sems)

        @pl.when(pl.program_id(0) == 0)
        def _():
            dg_ref[...] = jnp.zeros_like(dg_ref)

        dh = _dot_nt(dp_ref[...], w_ref[...])
        xv = x_ref[...]
        r = lax.rsqrt(jnp.mean(xv * xv, axis=-1, keepdims=True) + NORM_EPS)
        xh = xv * r
        dg_ref[...] += jnp.sum(dh * xh, axis=0, keepdims=True)
        dxh = dh * g_ref[...]
        dx_ref[...] = dres_ref[...] + r * (dxh - xh * jnp.mean(dxh * xh, axis=-1, keepdims=True))
        _ride_wait(ride, grid, ride_srcs, ride_dsts, ride_sems)

    extra = ride or _ChipExchange("gather", ())
    row = pl.BlockSpec((tm, D), lambda i: (i, 0))
    return pl.pallas_call(
        body, name=f"dx_in_{tag}", grid=grid,
        in_specs=[row, pl.BlockSpec((1, D), lambda i: (0, 0)), row, pl.BlockSpec((tm, DP), lambda i: (i, 0)),
                  pl.BlockSpec((D, DP), lambda i: (0, 0))] + extra.in_specs,
        out_specs=[row, pl.BlockSpec((1, D), lambda i: (0, 0))] + extra.out_specs,
        out_shape=[SDS((T, D), F32), SDS((1, D), F32)] + extra.out_shape,
        scratch_shapes=extra.scratch if ride else [],
        compiler_params=pltpu.CompilerParams(dimension_semantics=("arbitrary",), vmem_limit_bytes=VMEM_LIMIT,
                                             has_side_effects=bool(ride)),
    )(x, g, dres, dproj, w, *extra.sources)


def _loss_head(x, g, tgt):
    T, D = x.shape
    tm = _tile(T, 512)

    def body(x_ref, g_ref, t_ref, dx_ref, loss_ref, dg_ref):
        @pl.when(pl.program_id(0) == 0)
        def _():
            loss_ref[...] = jnp.zeros_like(loss_ref)
            dg_ref[...] = jnp.zeros_like(dg_ref)

        xv = x_ref[...]
        r = lax.rsqrt(jnp.mean(xv * xv, axis=-1, keepdims=True) + NORM_EPS)
        xh = xv * r
        gv = g_ref[...]
        err = xh * gv - t_ref[...]
        tok = jnp.mean(err * err, axis=-1, keepdims=True)
        loss_ref[...] += 0.5 * jnp.sum(tok, axis=0, keepdims=True)
        dy = err * (1.0 / D)
        dg_ref[...] += jnp.sum(dy * xh, axis=0, keepdims=True)
        dxh = dy * gv
        dx_ref[...] = r * (dxh - xh * jnp.mean(dxh * xh, axis=-1, keepdims=True))

    row = pl.BlockSpec((tm, D), lambda i: (i, 0))
    return pl.pallas_call(
        body, name="loss_head", grid=(T // tm,),
        in_specs=[row, pl.BlockSpec((1, D), lambda i: (0, 0)), row],
        out_specs=[row, pl.BlockSpec((1, 128), lambda i: (0, 0)), pl.BlockSpec((1, D), lambda i: (0, 0))],
        out_shape=[SDS((T, D), F32), SDS((1, 128), F32), SDS((1, D), F32)], compiler_params=_params("arbitrary"),
    )(x, g, tgt)


def _gmlp_core(u, v, lng, lnb, wm_ref, bst_ref, pair):
    ug, dug = _gelu_and_grad(u)
    vg, dvg = _gelu_and_grad(v)
    mu = _gsum64(vg) * (1.0 / 64)
    d = vg - mu
    var = _gsum64(d * d) * (1.0 / 64)
    rstd = lax.rsqrt(var + NORM_EPS)
    xh = d * rstd
    vn = xh * lng + lnb
    vnb = vn.astype(BF16)
    lo = _lane(u.shape) < 64
    g0, g1 = 2 * pair, 2 * pair + 1
    mixed = jnp.where(lo, _dot(wm_ref[g0], vnb) + bst_ref[:, g0:g0 + 1], _dot(wm_ref[g1], vnb) + bst_ref[:, g1:g1 + 1])
    return ug, dug, dvg, rstd, xh, vnb, mixed, lo


def _gmlp_fwd(proj, lng, lnb, wm, bst):
    T = proj.shape[0]

    def body(u_ref, v_ref, z_ref, lng_ref, lnb_ref, wm_ref, bst_ref, y_ref):
        for pair in range(2):
            sl = slice(128 * pair, 128 * pair + 128)
            ug, _, _, _, _, _, mixed, _ = _gmlp_core(u_ref[:, sl], v_ref[:, sl], lng_ref[:, sl], lnb_ref[:, sl],
                                                     wm_ref, bst_ref, pair)
            sz, _ = _silu_and_grad(z_ref[:, sl])
            y_ref[:, sl] = (ug * mixed * sz).astype(BF16)

    col = lambda c: pl.BlockSpec((CHUNK, A_WIDTH), lambda i, c=c: (i, c // A_WIDTH))
    full = lambda a: pl.BlockSpec(a.shape, lambda i, n=a.ndim: (0,) * n)
    return _Part(body, (proj, proj, proj, lng, lnb, wm, bst),
                 [col(COL_AU), col(COL_AV), col(COL_AZ), full(lng), full(lnb), full(wm), full(bst)],
                 [pl.BlockSpec((CHUNK, A_WIDTH), lambda i: (i, 0))], [SDS((T, A_WIDTH), BF16)], [])


def _gmlp_bwd(proj, dy, lng, lnb, wm, wmt, bst):
    T = proj.shape[0]
    n = T // CHUNK

    def body(u_ref, v_ref, z_ref, dy_ref, lng_ref, lnb_ref, wm_ref, wmt_ref, bst_ref,
             da_ref, dwm_ref, dbst_ref, dlng_ref, dlnb_ref):
        @pl.when(pl.program_id(0) == 0)
        def _():
            dwm_ref[...] = jnp.zeros_like(dwm_ref)
            dbst_ref[...] = jnp.zeros_like(dbst_ref)
            dlng_ref[...] = jnp.zeros_like(dlng_ref)
            dlnb_ref[...] = jnp.zeros_like(dlnb_ref)

        lane = _lane((CHUNK, 128))
        dbst = dbst_ref[...]
        for pair in range(2):
            sl = slice(128 * pair, 128 * pair + 128)
            lng_p = lng_ref[:, sl]
            ug, dug, dvg, rstd, xh, vnb, mixed, lo = _gmlp_core(u_ref[:, sl], v_ref[:, sl], lng_p, lnb_ref[:, sl],
                                                                wm_ref, bst_ref, pair)
            sz, dsz = _silu_and_grad(z_ref[:, sl])
            dyv = dy_ref[:, sl]
            out = ug * mixed
            dz = dyv * out * dsz
            dout = dyv * sz
            du = dout * mixed * dug
            dmix = dout * ug
            g0, g1 = 2 * pair, 2 * pair + 1
            dm0 = jnp.where(lo, dmix, 0.0)
            dm1 = jnp.where(lo, 0.0, dmix)
            dbst = dbst + jnp.where(lane == g0, jnp.sum(dm0, axis=-1, keepdims=True), 0.0)
            dbst = dbst + jnp.where(lane == g1, jnp.sum(dm1, axis=-1, keepdims=True), 0.0)
            dwm_ref[g0] += _dot_nt(dm0.astype(BF16), vnb)
            dwm_ref[g1] += _dot_nt(dm1.astype(BF16), vnb)
            dmb = dmix.astype(BF16)
            dvn = jnp.where(lo, _dot(wmt_ref[g0], dmb), _dot(wmt_ref[g1], dmb))
            dlng_ref[:, sl] += jnp.sum(dvn * xh, axis=0, keepdims=True)
            dlnb_ref[:, sl] += jnp.sum(dvn, axis=0, keepdims=True)
            dxh = dvn * lng_p
            m1 = _gsum64(dxh) * (1.0 / 64)
            m2 = _gsum64(dxh * xh) * (1.0 / 64)
            dv = rstd * (dxh - m1 - xh * m2) * dvg
            da_ref[:, COL_AU + 128 * pair:COL_AU + 128 * pair + 128] = du.astype(BF16)
            da_ref[:, COL_AV + 128 * pair:COL_AV + 128 * pair + 128] = dv.astype(BF16)
            da_ref[:, COL_AZ + 128 * pair:COL_AZ + 128 * pair + 128] = dz.astype(BF16)
        dbst_ref[...] = dbst

        @pl.when(pl.program_id(0) == n - 1)
        def _():
            causal = _lane((CHUNK, CHUNK)) <= _row((CHUNK, CHUNK))
            for g in range(A_GROUPS):
                dwm_ref[g] = jnp.where(causal, dwm_ref[g], 0.0)

    col = lambda c: pl.BlockSpec((CHUNK, A_WIDTH), lambda i, c=c: (i, c // A_WIDTH))
    full = lambda a: pl.BlockSpec(a.shape, lambda i, n=a.ndim: (0,) * n)
    acc = lambda s: pl.BlockSpec(s, lambda i, n=len(s): (0,) * n)
    return _Part(body, (proj, proj, proj, dy, lng, lnb, wm, wmt, bst),
                 [col(COL_AU), col(COL_AV), col(COL_AZ), pl.BlockSpec((CHUNK, A_WIDTH), lambda i: (i, 0)),
                  full(lng), full(lnb), full(wm), full(wmt), full(bst)],
                 [pl.BlockSpec((CHUNK, 3 * A_WIDTH), lambda i: (i, 0)), acc((A_GROUPS, CHUNK, CHUNK)),
                  acc((CHUNK, 128)), acc((1, A_WIDTH)), acc((1, A_WIDTH))],
                 [SDS((T, 3 * A_WIDTH), BF16), SDS((A_GROUPS, CHUNK, CHUNK), F32), SDS((CHUNK, 128), F32),
                  SDS((1, A_WIDTH), F32), SDS((1, A_WIDTH), F32)], [])


def _hgrn_consts():
    r, c = _row((CHUNK, CHUNK)), _lane((CHUNK, CHUNK))
    same = (r >> SUB_SHIFT) == (c >> SUB_SHIFT)
    lsub = jnp.where(same & (c <= r), 1.0, 0.0).astype(BF16)
    usub = jnp.where(same & (c >= r), 1.0, 0.0).astype(BF16)
    bsub = jnp.where(same, 1.0, 0.0).astype(BF16)
    return lsub, usub, bsub


def _hgrn_gates(qv, zf, lbp):
    sq, dsq = _silu_and_grad(qv)
    qt = sq * Q_SCALE
    sg = _sigmoid(zf)
    sgn = _sigmoid(-zf)
    f = lbp + (1.0 - lbp) * sg
    g = jnp.log(jnp.maximum(f, F_FLOOR))
    kf = (1.0 - lbp) * sgn
    return qt, dsq, sg, sgn, f, g, kf


def _hgrn_intra_fwd(qt, kf, b, v, mbd):
    rid = _row((SUB, 128))
    parts = []
    for s in range(SUB):
        e = jnp.exp(b - b[s:s + 1, :])
        parts.append(jnp.where(rid >= s, qt * kf[s:s + 1, :] * e, 0.0))
    a = _dot(jnp.concatenate(parts, axis=0).astype(BF16), mbd)
    o = jnp.zeros((SUB, 128), F32)
    for s in range(SUB):
        o = o + a[SUB * s:SUB * s + SUB, :] * v[s:s + 1, :]
    return o


def _hgrn_intra_bwd(qt, kf, b, v, do, mbd, rsum):
    rid = _row((SUB, 128))
    ps, das, kes, es = [], [], [], []
    for s in range(SUB):
        e = jnp.where(rid >= s, jnp.exp(b - b[s:s + 1, :]), 0.0)
        ke = kf[s:s + 1, :] * e
        es.append(e)
        kes.append(ke)
        ps.append(qt * ke)
        das.append(do * v[s:s + 1, :])
    a = _dot(jnp.concatenate(ps, axis=0).astype(BF16), mbd)
    da = _dot(jnp.concatenate(das, axis=0).astype(BF16), mbd)
    dqt = jnp.zeros((SUB, 128), F32)
    xs, ys = [], []
    for s in range(SUB):
        da_s = da[SUB * s:SUB * s + SUB, :]
        dqt = dqt + da_s * kes[s]
        xs.append(a[SUB * s:SUB * s + SUB, :] * do)
        ys.append(da_s * qt * es[s])
    dv = _dot(rsum, jnp.concatenate(xs, axis=0).astype(BF16))
    dkf = _dot(rsum, jnp.concatenate(ys, axis=0).astype(BF16))
    return dqt, dkf, dv


def _hgrn_norm_gate(o, z, onorm):
    ms = _gsum64(o * o) * (1.0 / 64)
    r = lax.rsqrt(ms + NORM_EPS)
    xh = o * r
    sz, dsz = _silu_and_grad(z)
    return xh, r, sz, dsz, xh * onorm


def _hgrn_fwd(proj, lb, onorm):
    T = proj.shape[0]
    n = T // CHUNK
    nsub = CHUNK // SUB

    def body(q_ref, f_ref, i_ref, z_ref, lb_ref, on_ref, y_ref, o_ref, s0_ref, st_ref):
        @pl.when(pl.program_id(0) == 0)
        def _():
            st_ref[...] = jnp.zeros_like(st_ref)

        lsub, _, bsub = _hgrn_consts()
        mbd = _block_diag64()
        bdmask = mbd > 0
        rid = _row((CHUNK, 128))
        for pair in range(2):
            sl = slice(128 * pair, 128 * pair + 128)
            qt, _, _, _, _, g, kf = _hgrn_gates(q_ref[:, sl], f_ref[:, sl], lb_ref[:, sl])
            v = i_ref[:, sl]
            b = _dot3_left(lsub, g)
            bl = _dot3_left(bsub, g)
            qh = (qt * jnp.exp(b)).astype(BF16)
            kh = kf * jnp.exp(bl - b)
            dec = jnp.exp(bl)
            vtb = v.T.astype(BF16)
            st = st_ref[pair]
            s0_ref[0, pair] = st
            outs = []
            for sub in range(nsub):
                rs = slice(SUB * sub, SUB * sub + SUB)
                o_inter = _dot_nt(qh[rs], st.astype(BF16))
                outs.append(o_inter + _hgrn_intra_fwd(qt[rs], kf[rs], b[rs], v[rs], mbd))
                khm = jnp.where((rid >> SUB_SHIFT) == sub, kh, 0.0).astype(BF16)
                st = jnp.where(bdmask, st * dec[SUB * sub:SUB * sub + 1, :] + _dot(vtb, khm), 0.0)
            st_ref[pair] = st
            o = jnp.concatenate(outs, axis=0)
            o_ref[:, sl] = o
            _, _, sz, _, on = _hgrn_norm_gate(o, z_ref[:, sl], on_ref[:, sl])
            y_ref[:, sl] = (on * sz).astype(BF16)

    col = lambda c: pl.BlockSpec((CHUNK, B_WIDTH), lambda i, c=c: (i, c // B_WIDTH))
    full = lambda a: pl.BlockSpec(a.shape, lambda i, n=a.ndim: (0,) * n)
    return _Part(body, (proj, proj, proj, proj, lb, onorm),
                 [col(COL_BQ), col(COL_BF), col(COL_BI), col(COL_BZ), full(lb), full(onorm)],
                 [pl.BlockSpec((CHUNK, B_WIDTH), lambda i: (i, 0)), pl.BlockSpec((CHUNK, B_WIDTH), lambda i: (i, 0)),
                  pl.BlockSpec((1, 2, 128, 128), lambda i: (i, 0, 0, 0))],
                 [SDS((T, B_WIDTH), BF16), SDS((T, B_WIDTH), F32), SDS((n, 2, 128, 128), F32)],
                 [pltpu.VMEM((2, 128, 128), F32)])


def _hgrn_bwd(proj, dy, o_saved, s0, lb, onorm):
    T = proj.shape[0]
    n = T // CHUNK
    nsub = CHUNK // SUB

    def body(q_ref, f_ref, i_ref, z_ref, dy_ref, o_ref, s0_ref, lb_ref, on_ref,
             db_ref, dlb_ref, don_ref, dst_ref, sts_ref):
        @pl.when(pl.program_id(0) == 0)
        def _():
            dst_ref[...] = jnp.zeros_like(dst_ref)
            dlb_ref[...] = jnp.zeros_like(dlb_ref)
            don_ref[...] = jnp.zeros_like(don_ref)

        lsub, usub, bsub = _hgrn_consts()
        mbd = _block_diag64()
        bdmask = mbd > 0
        rsum = jnp.where((_lane((SUB, SUB * SUB)) >> SUB_SHIFT) == _row((SUB, SUB * SUB)), 1.0, 0.0).astype(BF16)
        for pair in range(2):
            sl = slice(128 * pair, 128 * pair + 128)
            lbp = lb_ref[:, sl]
            qv, zf = q_ref[:, sl], f_ref[:, sl]
            qt, dsq, sg, sgn, f, g, kf = _hgrn_gates(qv, zf, lbp)
            v = i_ref[:, sl]
            b = _dot3_left(lsub, g)
            bl = _dot3_left(bsub, g)
            eb = jnp.exp(b)
            ekb = jnp.exp(bl - b)
            qhb = (qt * eb).astype(BF16)
            khb = (kf * ekb).astype(BF16)
            dec = jnp.exp(bl)
            vb = v.astype(BF16)
            onp = on_ref[:, sl]
            ov = o_ref[:, sl]
            xh, r, sz, dsz, on = _hgrn_norm_gate(ov, z_ref[:, sl], onp)
            dyv = dy_ref[:, sl]
            dz = dyv * on * dsz
            don = dyv * sz
            cn = jnp.sum(don * xh, axis=0, keepdims=True)
            don_ref[...] += cn + pltpu.roll(cn, 64, axis=1)
            dxo = don * onp
            do = r * (dxo - xh * (_gsum64(dxo * xh) * (1.0 / 64)))
            dob = do.astype(BF16)
            st = s0_ref[0, pair]
            for sub in range(nsub):
                rs = slice(SUB * sub, SUB * sub + SUB)
                sts_ref[sub] = st
                st = jnp.where(bdmask, st * dec[SUB * sub:SUB * sub + 1, :] + _dot_tn(vb[rs], khb[rs]), 0.0)
            gst = dst_ref[pair]
            dqt_p, dkf_p, dv_p, dbl_p = [None] * nsub, [None] * nsub, [None] * nsub, [None] * nsub
            for sub in reversed(range(nsub)):
                rs = slice(SUB * sub, SUB * sub + SUB)
                st_in = sts_ref[sub]
                gb = gst.astype(BF16)
                dqh = _dot(dob[rs], st_in.astype(BF16))
                dkh = _dot(vb[rs], gb)
                dv_inter = _dot_nt(khb[rs], gb)
                ddec = jnp.sum(gst * st_in, axis=0, keepdims=True)
                dec_row = dec[SUB * sub:SUB * sub + 1, :]
                gst = jnp.where(bdmask, gst * dec_row + _dot_tn(dob[rs], qhb[rs]), 0.0)
                dqt_i, dkf_i, dv_i = _hgrn_intra_bwd(qt[rs], kf[rs], b[rs], v[rs], do[rs], mbd, rsum)
                dkf_inter = dkh * ekb[rs]
                dqt_p[sub] = dqh * eb[rs] + dqt_i
                dkf_p[sub] = dkf_inter + dkf_i
                dv_p[sub] = dv_inter + dv_i
                row = jnp.sum(kf[rs] * dkf_inter, axis=0, keepdims=True) + ddec * dec_row
                dbl_p[sub] = jnp.broadcast_to(row, (SUB, 128))
            dst_ref[pair] = gst
            dqt = jnp.concatenate(dqt_p, axis=0)
            dkf = jnp.concatenate(dkf_p, axis=0)
            dv = jnp.concatenate(dv_p, axis=0)
            dg = _dot3_left(usub, qt * dqt - kf * dkf) + jnp.concatenate(dbl_p, axis=0)
            df = jnp.where(f > F_FLOOR, dg / f, 0.0)
            dlb_ref[:, sl] += jnp.sum(df * (1.0 - sg) - dkf * sgn, axis=0, keepdims=True)
            dfl = (1.0 - lbp) * sg * sgn * (df - dkf)
            dq = dqt * Q_SCALE * dsq
            db_ref[:, 0 * B_WIDTH + 128 * pair:0 * B_WIDTH + 128 * pair + 128] = dq.astype(BF16)
            db_ref[:, 1 * B_WIDTH + 128 * pair:1 * B_WIDTH + 128 * pair + 128] = dfl.astype(BF16)
            db_ref[:, 2 * B_WIDTH + 128 * pair:2 * B_WIDTH + 128 * pair + 128] = dv.astype(BF16)
            db_ref[:, 3 * B_WIDTH + 128 * pair:3 * B_WIDTH + 128 * pair + 128] = dz.astype(BF16)

    rev = lambda c: pl.BlockSpec((CHUNK, B_WIDTH), lambda i, c=c: (n - 1 - i, c // B_WIDTH))
    full = lambda a: pl.BlockSpec(a.shape, lambda i, n_=a.ndim: (0,) * n_)
    acc = lambda s: pl.BlockSpec(s, lambda i, n_=len(s): (0,) * n_)
    return _Part(body, (proj, proj, proj, proj, dy, o_saved, s0, lb, onorm),
                 [rev(COL_BQ), rev(COL_BF), rev(COL_BI), rev(COL_BZ),
                  pl.BlockSpec((CHUNK, B_WIDTH), lambda i: (n - 1 - i, 1)),
                  pl.BlockSpec((CHUNK, B_WIDTH), lambda i: (n - 1 - i, 0)),
                  pl.BlockSpec((1, 2, 128, 128), lambda i: (n - 1 - i, 0, 0, 0)), full(lb), full(onorm)],
                 [pl.BlockSpec((CHUNK, 4 * B_WIDTH), lambda i: (n - 1 - i, 0)), acc((1, B_WIDTH)), acc((1, 128))],
                 [SDS((T, 4 * B_WIDTH), BF16), SDS((1, B_WIDTH), F32), SDS((1, 128), F32)],
                 [pltpu.VMEM((2, 128, 128), F32), pltpu.VMEM((nsub, 128, 128), F32)])


def _lb_fwd(hgrn_lb):
    assert hgrn_lb.shape[0] == 2

    def body(x_ref, o_ref):
        x0, x1 = x_ref[0:1, :], x_ref[1:2, :]
        m = jnp.maximum(x0, x1)
        e0, e1 = jnp.exp(x0 - m), jnp.exp(x1 - m)
        p0, p1 = e0 / (e0 + e1), e1 / (e0 + e1)
        o_ref[0:1, :] = jnp.clip(p0 - p0, 0.0, 1.0 - 1e-6)
        o_ref[1:2, :] = jnp.clip((p0 + p1) - p0, 0.0, 1.0 - 1e-6)

    return pl.pallas_call(body, name="lb_fwd", out_shape=SDS(hgrn_lb.shape, F32))(hgrn_lb)


def _lb_bwd(hgrn_lb, dlb):
    def body(x_ref, d_ref, o_ref):
        x0, x1 = x_ref[0:1, :], x_ref[1:2, :]
        m = jnp.maximum(x0, x1)
        e0, e1 = jnp.exp(x0 - m), jnp.exp(x1 - m)
        p0, p1 = e0 / (e0 + e1), e1 / (e0 + e1)
        val = (p0 + p1) - p0
        dp1 = jnp.where((val > 0.0) & (val < 1.0 - 1e-6), d_ref[1:2, :], 0.0)
        inner = p1 * dp1
        o_ref[0:1, :] = p0 * (0.0 - inner)
        o_ref[1:2, :] = p1 * (dp1 - inner)

    return pl.pallas_call(body, name="lb_bwd", out_shape=SDS(hgrn_lb.shape, F32))(hgrn_lb, dlb)


def _fox_prep(proj, bf):
    T = proj.shape[0]
    n = T // CHUNK

    def body(q0_ref, q1_ref, k0_ref, k1_ref, v0_ref, v1_ref, fl_ref, bf_ref, qo_ref, ko_ref, vt_ref, carry_ref):
        for p, v_ref in enumerate((v0_ref, v0_ref, v1_ref, v1_ref)):
            vt_ref[p, 0] = v_ref[:, 128 * (p % 2):128 * (p % 2) + 128].T.astype(BF16)

        @pl.when(pl.program_id(0) == 0)
        def _():
            carry_ref[...] = jnp.zeros_like(carry_ref)

        ltri = jnp.where(_lane((CHUNK, CHUNK)) <= _row((CHUNK, CHUNK)), 1.0, 0.0).astype(BF16)
        lf = jax.nn.log_sigmoid(fl_ref[...] + bf_ref[...])
        c = _dot3_left(ltri, lf) + carry_ref[...]
        carry_ref[...] = c[CHUNK - 1:CHUNK, :]
        lane = _lane((CHUNK, 128))
        feat = lane < 64
        ones_q = (lane >= 67) & (lane <= 69)
        ones_k = (lane >= 64) & (lane <= 66)
        qrefs, krefs = (q0_ref, q1_ref), (k0_ref, k1_ref)
        for h in range(C_HEADS):
            blk = slice(128 * ((h // 2) % 2), 128 * ((h // 2) % 2) + 128)
            qp, kp = qrefs[h // 4][:, blk], krefs[h // 4][:, blk]
            if h % 2:
                qp, kp = pltpu.roll(qp, 64, axis=1), pltpu.roll(kp, 64, axis=1)
            ch = jnp.broadcast_to(c[:, h:h + 1], (CHUNK, 128))
            hi = ch.astype(BF16).astype(F32)
            r1 = ch - hi
            mid = r1.astype(BF16).astype(F32)
            lo = r1 - mid
            aq = jnp.where(lane == 64, hi, jnp.where(lane == 65, mid, jnp.where(lane == 66, lo,
                           jnp.where(ones_q, 1.0, 0.0))))
            ak = jnp.where(lane == 67, -hi, jnp.where(lane == 68, -mid, jnp.where(lane == 69, -lo,
                           jnp.where(ones_k, 1.0, 0.0))))
            qo_ref[:, 128 * h:128 * h + 128] = jnp.where(feat, qp * Q_SCALE, aq).astype(BF16)
            ko_ref[:, 128 * h:128 * h + 128] = jnp.where(feat, kp, ak).astype(BF16)

    w = 256
    col = lambda c: pl.BlockSpec((CHUNK, w), lambda i, c=c: (i, c // w))
    return _Part(body, (proj, proj, proj, proj, proj, proj, proj, bf),
                 [col(COL_CQ), col(COL_CQ + w), col(COL_CK), col(COL_CK + w), col(COL_CV), col(COL_CV + w),
                  pl.BlockSpec((CHUNK, 128), lambda i: (i, COL_CF // 128)), pl.BlockSpec((1, 128), lambda i: (0, 0))],
                 [pl.BlockSpec((CHUNK, C_HEADS * 128), lambda i: (i, 0))] * 2
                 + [pl.BlockSpec((C_HEADS // 2, 1, 128, CHUNK), lambda i: (0, i, 0, 0))],
                 [SDS((T, C_HEADS * 128), BF16)] * 2 + [SDS((C_HEADS // 2, n, 128, CHUNK), BF16)],
                 [pltpu.VMEM((1, 128), F32)])


FOX_TILE = 512
FOX_KEYS = 512


def _fox_mask(tk, tq, k0, q0):
    return (_row((tk, tq)) + (k0 - q0)) <= _lane((tk, tq))


def _ride_refs(ride, rest, n_out, n_scratch):
    n = ride.n if ride else 0
    srcs, rest = rest[:n], rest[n:]
    outs, rest = rest[:n_out], rest[n_out:]
    dsts, rest = rest[:n], rest[n:]
    return srcs, outs, dsts, rest[:n_scratch], rest[n_scratch:]


def _ride_start(ride, grid, srcs, dsts, sems):
    if ride:
        first = functools.reduce(lambda a, b: a & b, [pl.program_id(d) == 0 for d in range(len(grid))])
        pl.when(first)(lambda: ride.start(srcs, dsts, sems))


def _ride_wait(ride, grid, srcs, dsts, sems):
    if ride:
        last = functools.reduce(lambda a, b: a & b, [pl.program_id(d) == n - 1 for d, n in enumerate(grid)])
        pl.when(last)(lambda: ride.wait(srcs, dsts, sems))


def _fox_fwd(qt, kt, vt, proj, tag, ride=None):
    T = proj.shape[0]
    tq, tk = _tile(T, FOX_TILE), _tile(T, FOX_KEYS)
    nq, nsub = T // tq, tk // CHUNK
    npair = C_HEADS // 2

    def body(q_ref, k_ref, vt_ref, z_ref, *rest):
        ride_srcs, (o_ref, lse_ref, y_ref), ride_dsts, (acc_ref, st_ref, pt_ref), ride_sems = _ride_refs(ride, rest, 3, 3)
        i = pl.program_id(1)
        _ride_start(ride, (npair, nq), ride_srcs, ride_dsts, ride_sems)

        qs = (q_ref[:, 0:128], q_ref[:, 128:256])
        acc_ref[...] = jnp.zeros_like(acc_ref)
        pt_ref[...] = jnp.zeros_like(pt_ref)
        nfull = (i * tq) // tk

        def scores(j):
            kb = k_ref[pl.ds(pl.multiple_of(j * tk, tk), tk), :]
            return tuple(_dot_nt(kb[:, 128 * h:128 * h + 128], qs[h]) for h in range(2))

        def weigh(j, h):
            rows = slice(64 * h, 64 * h + 64)
            pv = _dot(vt_ref[0, nsub * j, rows, :], pt_ref[h, 0:CHUNK, :])
            for c in range(1, nsub):
                pv = pv + _dot(vt_ref[0, nsub * j + c, rows, :], pt_ref[h, CHUNK * c:CHUNK * c + CHUNK, :])
            return pv

        def block(j, carry, diagonal):
            nxt = () if diagonal else scores(j + 1)
            pvs = [weigh(jnp.maximum(j - 1, 0), h) for h in range(2)]
            new = []
            for h in range(2):
                m, l, alpha_prev = carry[3 * h:3 * h + 3]
                st = st_ref[h]
                if diagonal:
                    st = jnp.where(_fox_mask(tk, tq, j * tk, i * tq), st, -jnp.inf)
                m_new = jnp.maximum(m, _colreduce(st, jnp.maximum))
                pt = jnp.exp(st - m_new)
                alpha = jnp.exp(m - m_new)
                rows = slice(64 * h, 64 * h + 64)
                acc_ref[rows, :] = alpha_prev * acc_ref[rows, :] + pvs[h]
                pt_ref[h] = pt.astype(BF16)
                new += [m_new, alpha * l + _colreduce(pt, jnp.add), alpha]
            for h, st in enumerate(nxt):
                st_ref[h] = st
            return tuple(new)

        for h, st in enumerate(scores(0)):
            st_ref[h] = st
        init = (jnp.full((1, tq), -jnp.inf, F32), jnp.zeros((1, tq), F32), jnp.ones((1, tq), F32)) * 2
        carry = lax.fori_loop(0, nfull, lambda j, c: block(j, c, False), init)
        m0, l0, a0, m1, l1, a1 = block(nfull, carry, True)
        for h, alpha in enumerate((a0, a1)):
            rows = slice(64 * h, 64 * h + 64)
            acc_ref[rows, :] = alpha * acc_ref[rows, :] + weigh(nfull, h)
        inv = jnp.where(_row((128, tq)) < 64, 1.0 / l0, 1.0 / l1)
        o = (acc_ref[...] * inv).T
        o_ref[...] = o
        r8 = _row((8, tq))
        lse_ref[0, 0] = jnp.where(r8 == 0, m0 + jnp.log(l0), jnp.where(r8 == 1, m1 + jnp.log(l1), 0.0))
        sz, _ = _silu_and_grad(z_ref[...])
        y_ref[...] = (o * sz).astype(BF16)
        _ride_wait(ride, (npair, nq), ride_srcs, ride_dsts, ride_sems)

    blk = pl.BlockSpec((tq, 128), lambda p, i: (i, p))
    extra = ride or _ChipExchange("gather", ())
    return pl.pallas_call(
        body, name=f"fox_fwd_{tag}", grid=(npair, nq),
        in_specs=[pl.BlockSpec((tq, 256), lambda p, i: (i, p)), pl.BlockSpec((T, 256), lambda p, i: (0, p)),
                  pl.BlockSpec((1, T // CHUNK, 128, CHUNK), lambda p, i: (p, 0, 0, 0)),
                  pl.BlockSpec((tq, 128), lambda p, i: (i, COL_CZ // 128 + p))] + extra.in_specs,
        out_specs=[blk, pl.BlockSpec((1, 1, 8, tq), lambda p, i: (p, i, 0, 0)), blk] + extra.out_specs,
        out_shape=[SDS((T, C_WIDTH), F32), SDS((npair, nq, 8, tq), F32), SDS((T, C_WIDTH), BF16)] + extra.out_shape,
        scratch_shapes=[pltpu.VMEM((128, tq), F32), pltpu.VMEM((2, tk, tq), F32), pltpu.VMEM((2, tk, tq), BF16)]
        + (extra.scratch if ride else []),
        compiler_params=pltpu.CompilerParams(dimension_semantics=("arbitrary", "arbitrary"), vmem_limit_bytes=VMEM_LIMIT,
                                             has_side_effects=bool(ride)),
    )(qt, kt, vt, proj, *extra.sources)


def _fox_bwd_prep(proj, dy, o, qt, tag):
    T = proj.shape[0]
    tq = _tile(T, FOX_TILE)
    nq = T // tq

    def body(z0_ref, z1_ref, dy_ref, o_ref, q_ref, do_ref, dl_ref, dz_ref, dot_ref, qt_ref):
        sel = jnp.where((_lane((16, 128)) >> 6) == _row((16, 128)), 1.0, 0.0).astype(BF16)
        for p, z_ref in enumerate((z0_ref, z0_ref, z1_ref, z1_ref)):
            sl = slice(128 * p, 128 * p + 128)
            sz, dsz = _silu_and_grad(z_ref[:, 128 * (p % 2):128 * (p % 2) + 128])
            dyv, ov = dy_ref[:, sl], o_ref[:, sl]
            do = dyv * sz
            do_ref[:, sl] = do.astype(BF16)
            dot_ref[p, 0] = do.T.astype(BF16)
            dz_ref[:, sl] = (dyv * ov * dsz).astype(BF16)
            hi, mid, lo = _split3(do * ov)
            dl_ref[p, 0] = (_dot_nt(sel, hi) + _dot_nt(sel, mid) + _dot_nt(sel, lo))[0:8, :]
        for h in range(C_HEADS):
            qt_ref[h, 0] = q_ref[:, 128 * h:128 * h + 128].astype(F32).T.astype(BF16)

    w = 256
    blk = pl.BlockSpec((tq, C_WIDTH), lambda i: (i, 0))
    return pl.pallas_call(
        body, name=f"fox_bwd_prep_{tag}", grid=(nq,),
        in_specs=[pl.BlockSpec((tq, w), lambda i: (i, COL_CZ // w)), pl.BlockSpec((tq, w), lambda i: (i, COL_CZ // w + 1)),
                  pl.BlockSpec((tq, C_WIDTH), lambda i: (i, (A_WIDTH + B_WIDTH) // C_WIDTH)), blk,
                  pl.BlockSpec((tq, C_HEADS * 128), lambda i: (i, 0))],
        out_specs=[blk, pl.BlockSpec((C_HEADS // 2, 1, 8, tq), lambda i: (0, i, 0, 0)), blk,
                   pl.BlockSpec((C_HEADS // 2, 1, 128, tq), lambda i: (0, i, 0, 0)),
                   pl.BlockSpec((C_HEADS, 1, 128, tq), lambda i: (0, i, 0, 0))],
        out_shape=[SDS((T, C_WIDTH), BF16), SDS((C_HEADS // 2, nq, 8, tq), F32), SDS((T, C_WIDTH), BF16),
                   SDS((C_HEADS // 2, nq, 128, tq), BF16), SDS((C_HEADS, nq, 128, tq), BF16)],
        compiler_params=_params("parallel"),
    )(proj, proj, dy, o, qt)


def _fox_bwd(qt, kt, proj, do, lse, delta, dot, qtr, tag, ride=None):
    T = proj.shape[0]
    tq, tk = _tile(T, FOX_TILE), _tile(T, FOX_KEYS)
    nq, nk = T // tq, T // tk
    assert tq == tk
    npair = C_HEADS // 2

    def body(q_ref, k_ref, v_ref, do_ref, lse_ref, dl_ref, dot_ref, qtr_ref, *rest):
        ride_srcs, (dq_ref, dk_ref, dv_ref), ride_dsts, scratch, ride_sems = _ride_refs(ride, rest, 3, 5)
        dvt_ref, dkt_ref, sc_ref, pt_ref, ds_ref = scratch
        j = pl.program_id(1)
        first = (j * tk) // tq
        _ride_start(ride, (npair, nk), ride_srcs, ride_dsts, ride_sems)

        @pl.when(j == 0)
        def _():
            dq_ref[...] = jnp.zeros_like(dq_ref)

        dkt_ref[...] = jnp.zeros_like(dkt_ref)
        dvt_ref[...] = jnp.zeros_like(dvt_ref)
        ks = (k_ref[:, 0:128], k_ref[:, 128:256])
        kts = tuple(k.astype(F32).T.astype(BF16) for k in ks)
        vb = v_ref[...].astype(BF16)
        lo = _lane((tq, 128)) < 64

        def operands(i):
            q0 = pl.multiple_of(i * tq, tq)
            qb = q_ref[pl.ds(q0, tq), :]
            dob = do_ref[pl.ds(q0, tq), :]
            qhs = (qb[:, 0:128], qb[:, 128:256])
            dohs = (jnp.where(lo, dob, jnp.zeros_like(dob)), jnp.where(lo, jnp.zeros_like(dob), dob))
            return qhs, dohs

        def scores(i):
            qhs, dohs = operands(i)
            return tuple(_dot_nt(ks[h], qhs[h]) for h in range(2)) + tuple(_dot_nt(vb, dohs[h]) for h in range(2))

        def park(sc, slot):
            for a, s in enumerate(sc):
                sc_ref[slot, a] = s

        def grads(i):
            for h in range(2):
                rows = slice(64 * h, 64 * h + 64)
                dvt_ref[rows, :] += _dot_nt(dot_ref[0, i, rows, :], pt_ref[h])
                dkt_ref[h] += _dot_nt(qtr_ref[h, i], ds_ref[h])
                dq_ref[h, i] += _dot(kts[h], ds_ref[h])

        def block(i, slot, diagonal, opening):
            park(scores(jnp.minimum(i + 1, nq - 1)), 1 - slot)
            if not opening:
                grads(i - 1)
            lsev = lse_ref[0, i]
            dlv = dl_ref[0, i]
            for h in range(2):
                pt = jnp.exp(sc_ref[slot, h] - lsev[h:h + 1, :])
                if diagonal:
                    pt = jnp.where(_fox_mask(tk, tq, j * tk, i * tq), pt, 0.0)
                ds_ref[h] = (pt * (sc_ref[slot, 2 + h] - dlv[h:h + 1, :])).astype(BF16)
                pt_ref[h] = pt.astype(BF16)

        park(scores(first), 0)
        block(first, 0, True, True)
        rest = nq - 1 - first

        def two_steps(t, carry):
            block(first + 1 + 2 * t, 1, False, False)
            block(first + 2 + 2 * t, 0, False, False)
            return carry

        lax.fori_loop(0, rest // 2, two_steps, 0)
        pl.when(rest % 2 == 1)(lambda: block(nq - 1, 1, False, False))
        grads(nq - 1)
        dv_ref[...] = dvt_ref[...].T.astype(BF16)
        for h in range(2):
            dk_ref[:, 128 * h:128 * h + 128] = dkt_ref[h].T
        _ride_wait(ride, (npair, nk), ride_srcs, ride_dsts, ride_sems)

    full = lambda w: pl.BlockSpec((T, w), lambda p, j: (0, p))
    stat = pl.BlockSpec((1, nq, 8, tq), lambda p, j: (p, 0, 0, 0))
    extra = ride or _ChipExchange("gather", ())
    return pl.pallas_call(
        body, name=f"fox_bwd_{tag}", grid=(npair, nk),
        in_specs=[full(256), pl.BlockSpec((tk, 256), lambda p, j: (j, p)),
                  pl.BlockSpec((tk, 128), lambda p, j: (j, COL_CV // 128 + p)), full(128), stat, stat,
                  pl.BlockSpec((1, nq, 128, tq), lambda p, j: (p, 0, 0, 0)),
                  pl.BlockSpec((2, nq, 128, tq), lambda p, j: (p, 0, 0, 0))] + extra.in_specs,
        out_specs=[pl.BlockSpec((2, nq, 128, tq), lambda p, j: (p, 0, 0, 0)), pl.BlockSpec((tk, 256), lambda p, j: (j, p)),
                   pl.BlockSpec((tk, 128), lambda p, j: (j, p))] + extra.out_specs,
        out_shape=[SDS((C_HEADS, nq, 128, tq), F32), SDS((T, C_HEADS * 128), F32), SDS((T, C_WIDTH), BF16)]
        + extra.out_shape,
        scratch_shapes=[pltpu.VMEM((128, tk), F32), pltpu.VMEM((2, 128, tk), F32), pltpu.VMEM((2, 4, tk, tq), F32),
                        pltpu.VMEM((2, tk, tq), BF16), pltpu.VMEM((2, tk, tq), BF16)] + (extra.scratch if ride else []),
        compiler_params=pltpu.CompilerParams(dimension_semantics=("arbitrary", "arbitrary"), vmem_limit_bytes=VMEM_LIMIT,
                                             has_side_effects=bool(ride)),
    )(qt, kt, proj, do, lse, delta, dot, qtr, *extra.sources)


def _fox_bwd_post(dqt, dkt, proj, bf, tag):
    T = proj.shape[0]
    tq = _tile(T, FOX_TILE)
    n = T // tq

    def body(dq_ref, dk_ref, fl_ref, bf_ref, oq_ref, ok_ref, ofl_ref, dbf_ref, carry_ref):
        @pl.when(pl.program_id(0) == 0)
        def _():
            carry_ref[...] = jnp.zeros_like(carry_ref)
            dbf_ref[...] = jnp.zeros_like(dbf_ref)

        lane = _lane((tq, 128))
        lo = lane < 64
        dqs = [dq_ref[h, 0].T for h in range(C_HEADS)]
        dc = jnp.zeros((tq, 128), F32)
        for h in range(C_HEADS):
            dc = dc + jnp.where(lane == h, dqs[h][:, 64:65] - dk_ref[:, 128 * h + 67:128 * h + 68], 0.0)
        utri = jnp.where(_lane((tq, tq)) >= _row((tq, tq)), 1.0, 0.0).astype(BF16)
        dlf = _dot3_left(utri, dc) + carry_ref[...]
        carry_ref[...] = dlf[0:1, :]
        dfl = jnp.where(lane < C_HEADS, dlf * _sigmoid(-(fl_ref[...] + bf_ref[...])), 0.0)
        ofl_ref[...] = dfl.astype(BF16)
        dbf_ref[...] += jnp.sum(dfl, axis=0, keepdims=True)
        for p in range(C_HEADS // 2):
            a, b = 128 * (2 * p), 128 * (2 * p + 1)
            oq_ref[:, 128 * p:128 * p + 128] = (
                jnp.where(lo, dqs[2 * p], pltpu.roll(dqs[2 * p + 1], 64, axis=1)) * Q_SCALE).astype(BF16)
            ok_ref[:, 128 * p:128 * p + 128] = jnp.where(
                lo, dk_ref[:, a:a + 128], pltpu.roll(dk_ref[:, b:b + 128], 64, axis=1)).astype(BF16)

    rev = lambda w: pl.BlockSpec((tq, w), lambda i: (n - 1 - i, 0))
    return pl.pallas_call(
        body, name=f"fox_bwd_post_{tag}", grid=(n,),
        in_specs=[pl.BlockSpec((C_HEADS, 1, 128, tq), lambda i: (0, n - 1 - i, 0, 0)), rev(C_HEADS * 128),
                  pl.BlockSpec((tq, 128), lambda i: (n - 1 - i, COL_CF // 128)), pl.BlockSpec((1, 128), lambda i: (0, 0))],
        out_specs=[rev(C_WIDTH), rev(C_WIDTH), rev(128), pl.BlockSpec((1, 128), lambda i: (0, 0))],
        out_shape=[SDS((T, C_WIDTH), BF16), SDS((T, C_WIDTH), BF16), SDS((T, 128), BF16), SDS((1, 128), F32)],
        scratch_shapes=[pltpu.VMEM((1, 128), F32)], compiler_params=_params("arbitrary"),
    )(dqt, dkt, proj, bf)


def _adamw_math(w, g, m, v):
    m = ADAM_B1 * m + (1.0 - ADAM_B1) * g
    v = ADAM_B2 * v + (1.0 - ADAM_B2) * (g * g)
    delta = -ADAM_LR * ((m / ADAM_C1) / (jnp.sqrt(v / ADAM_C2) + ADAM_EPS) + ADAM_WD * w)
    return delta, m, v


def _adamw_pair(w, m, v, ga, gb, name):
    n0 = w.shape[0]
    most = max(1, ADAMW_BLOCK_BYTES // (4 * math.prod(w.shape[1:])))
    t0 = max(t for t in range(1, min(n0, most) + 1) if n0 % t == 0)

    def body(w_ref, m_ref, v_ref, ga_ref, gb_ref, g_ref, d_ref, nm_ref, nv_ref):
        g = ga_ref[...] + gb_ref[...]
        g_ref[...] = g
        d_ref[...], nm_ref[...], nv_ref[...] = _adamw_math(w_ref[...], g, m_ref[...], v_ref[...])

    blk = pl.BlockSpec((t0,) + w.shape[1:], lambda i: (i, 0, 0))
    return pl.pallas_call(
        body, name=name, grid=(n0 // t0,), in_specs=[blk] * 5, out_specs=[blk] * 4,
        out_shape=[SDS(w.shape, F32)] * 4, compiler_params=_params("parallel"),
    )(w, m, v, ga, gb)


def _adamw_small(ws, ms, vs, gall):
    offs = _small_offsets()
    n = len(ws)

    def body(*refs):
        w_refs, m_refs, v_refs, g_ref = refs[:n], refs[n:2 * n], refs[2 * n:3 * n], refs[3 * n]
        outs = refs[3 * n + 1:]

        def total(off, rows):
            g = g_ref[0, off:off + rows, :]
            for dev in range(1, N_DEV):
                g = g + g_ref[dev, off:off + rows, :]
            return g

        for k in range(n):
            g = total(offs[k], ws[k].shape[0])
            go_ref, d_ref, nm_ref, nv_ref = outs[4 * k:4 * k + 4]
            go_ref[...] = g
            d_ref[...], nm_ref[...], nv_ref[...] = _adamw_math(w_refs[k][...], g, m_refs[k][...], v_refs[k][...])
        outs[4 * n][...] = total(offs[n], 1)

    shapes = [SDS(w.shape, F32) for w in ws for _ in range(4)] + [SDS((1, 128), F32)]
    res = pl.pallas_call(body, name="adamw_small", out_shape=shapes,
                         compiler_params=pltpu.CompilerParams(vmem_limit_bytes=VMEM_LIMIT))(*ws, *ms, *vs, gall)
    return [res[4 * k:4 * k + 4] for k in range(n)], res[4 * n]


def _pack_grads(dlng, dlnb, dwm, dbst, dlb, donorm, dbf, dfinal, loss_part):
    offs = _small_offsets()
    base = offs[1]
    L = len(dwm)
    assert L == 2

    def body(*refs):
        lng, lnb, wm, bst, on, bf = (refs[L * a:L * a + L] for a in range(6))
        lb_ref, fin_ref, loss_ref, o_ref = refs[6 * L:]
        o_ref[...] = jnp.zeros_like(o_ref)
        lane = _lane((1, 128))
        for l in range(L):
            for j in range(2):
                o_ref[offs[1] - base + 2 * l + j:offs[1] - base + 2 * l + j + 1, :] = lng[l][:, 128 * j:128 * j + 128]
                o_ref[offs[2] - base + 2 * l + j:offs[2] - base + 2 * l + j + 1, :] = lnb[l][:, 128 * j:128 * j + 128]
                o_ref[offs[5] - base + 2 * l + j:offs[5] - base + 2 * l + j + 1, :] = lb_ref[l:l + 1, 128 * j:128 * j + 128]
            for g in range(A_GROUPS):
                row = offs[3] - base + (A_GROUPS * l + g) * CHUNK
                o_ref[row:row + CHUNK, :] = wm[l][g]
            o_ref[offs[4] - base + A_GROUPS * l:offs[4] - base + A_GROUPS * (l + 1), :] = bst[l][...].T[0:A_GROUPS, :]
        o_ref[offs[6] - base:offs[6] - base + 1, :] = jnp.where(lane < 64, on[0][...], pltpu.roll(on[1][...], 64, axis=1))
        o_ref[offs[7] - base:offs[7] - base + 1, :] = jnp.where(
            lane < C_HEADS, bf[0][...], jnp.where(lane < 2 * C_HEADS, pltpu.roll(bf[1][...], C_HEADS, axis=1), 0.0))
        for j in range(D_MODEL // 128):
            o_ref[offs[8] - base + j:offs[8] - base + j + 1, :] = fin_ref[:, 128 * j:128 * j + 128]
        o_ref[offs[9] - base:offs[9] - base + 1, :] = loss_ref[...]

    rows = offs[9] + 8 - base
    return pl.pallas_call(body, name="pack_grads", out_shape=SDS((rows, 128), F32))(
        *dlng, *dlnb, *dwm, *dbst, *donorm, *dbf, dlb, dfinal, loss_part)


def _sum_chips(layers, name, layer_major):
    _, R, C = layers[0].shape
    L = len(layers)
    tc = _tile(C, 256)

    def body(*refs):
        o_ref = refs[-1]
        for l, p_ref in enumerate(refs[:-1]):
            p = [p_ref[k].astype(F32) for k in range(N_CHIPS)]
            s = ((p[0] + p[1]) + p[2]) + p[3]
            if layer_major:
                o_ref[l] = s
            else:
                o_ref[:, l, :] = s

    out = (L, R, C) if layer_major else (R, L, C)
    out_blk = (L, R, tc) if layer_major else (R, L, tc)
    return pl.pallas_call(
        body, name=name, grid=(C // tc,),
        in_specs=[pl.BlockSpec((N_CHIPS, R, tc), lambda i: (0, 0, i))] * L,
        out_specs=pl.BlockSpec(out_blk, lambda i: (0, 0, i)), out_shape=SDS(out, F32),
        compiler_params=_params("parallel"),
    )(*layers)


ANY = pl.BlockSpec(memory_space=pl.ANY)


def _mesh_pos():
    return lax.axis_index("x"), lax.axis_index("y"), lax.axis_index("c")


def _other_chips(x, y):
    return [(1 - x, y), (x, 1 - y), (1 - x, 1 - y)]


class _ChipExchange:
    def __init__(self, mode, sources):
        assert mode in ("gather", "scatter")
        self.mode, self.sources = mode, tuple(sources)
        self.n = len(self.sources)
        self.in_specs = [ANY] * self.n
        self.out_specs = [ANY] * self.n
        self.out_shape = [SDS(((N_CHIPS,) + s.shape) if mode == "gather" else s.shape, s.dtype) for s in self.sources]
        self.scratch = [pltpu.SemaphoreType.DMA((3 * self.n,)), pltpu.SemaphoreType.DMA((3 * self.n,)),
                        pltpu.SemaphoreType.DMA((self.n,))]

    def _copies(self, srcs, dsts, send_sems, recv_sems, local_sems):
        x, y, c = _mesh_pos()
        me = 2 * x + y
        view = (lambda r, chip: r) if self.mode == "gather" else (lambda r, chip: r.at[chip])
        local = [pltpu.make_async_copy(view(s, me), d.at[me], local_sems.at[a]) for a, (s, d) in enumerate(zip(srcs, dsts))]
        sends, recvs = [], []
        for j, (px, py) in enumerate(_other_chips(x, y)):
            peer = 2 * px + py
            for a, (s, d) in enumerate(zip(srcs, dsts)):
                sems = dict(send_sem=send_sems.at[self.n * j + a], recv_sem=recv_sems.at[self.n * j + a],
                            device_id=(px, py, c), device_id_type=MESH_ID)
                sends.append(pltpu.make_async_remote_copy(src_ref=view(s, peer), dst_ref=d.at[me], **sems))
                recvs.append(pltpu.make_async_remote_copy(src_ref=view(s, me), dst_ref=d.at[peer], **sems))
        return local, sends, recvs

    def start(self, srcs, dsts, sems):
        local, sends, _ = self._copies(srcs, dsts, *sems)
        for cp in local + sends:
            cp.start()

    def wait(self, srcs, dsts, sems):
        local, sends, recvs = self._copies(srcs, dsts, *sems)
        for cp in recvs:
            cp.wait_recv()
        for cp in sends:
            cp.wait_send()
        for cp in local:
            cp.wait()


def _gather_halves(w, tag):
    R, C = w.shape
    H = R // 2

    def body(w_ref, g_ref, send_sems, recv_sems, pass_send, pass_recv, local_sem):
        x, y, c = _mesh_pos()
        me = 2 * x + y
        mine, theirs = pl.ds(c * H, H), pl.ds((1 - c) * H, H)
        own = pltpu.make_async_copy(w_ref, g_ref.at[me], local_sem)
        own.start()

        def fetch(j, px, py, src, dst):
            return pltpu.make_async_remote_copy(src_ref=src, dst_ref=dst, send_sem=send_sems.at[j], recv_sem=recv_sems.at[j],
                                                device_id=(px, py, c), device_id_type=MESH_ID)

        def hand(j, rows, peer):
            return pltpu.make_async_remote_copy(src_ref=g_ref.at[peer, rows], dst_ref=g_ref.at[peer, rows],
                                                send_sem=pass_send.at[j], recv_sem=pass_recv.at[j],
                                                device_id=(x, y, 1 - c), device_id_type=MESH_ID)

        chips = _other_chips(x, y)
        sends = [fetch(j, px, py, w_ref.at[mine], g_ref.at[me, mine]) for j, (px, py) in enumerate(chips)]
        for cp in sends:
            cp.start()
        passed = []
        for j, (px, py) in enumerate(chips):
            peer = 2 * px + py
            fetch(j, px, py, w_ref.at[mine], g_ref.at[peer, mine]).wait_recv()
            passed.append(hand(j, mine, peer))
            passed[-1].start()
        for j, (px, py) in enumerate(chips):
            hand(j, theirs, 2 * px + py).wait_recv()
        for cp in sends + passed:
            cp.wait_send()
        own.wait()

    return pl.pallas_call(
        body, name=f"gather_halves_{tag}", in_specs=[ANY], out_specs=ANY, out_shape=SDS((N_CHIPS, R, C), w.dtype),
        scratch_shapes=[pltpu.SemaphoreType.DMA((3,)), pltpu.SemaphoreType.DMA((3,)), pltpu.SemaphoreType.DMA((3,)),
                        pltpu.SemaphoreType.DMA((3,)), pltpu.SemaphoreType.DMA],
        compiler_params=pltpu.CompilerParams(has_side_effects=True),
    )(w)


class _DeviceGather:
    def __init__(self, source):
        self.sources, self.n = (source,), 1
        self.in_specs, self.out_specs = [ANY], [ANY]
        self.out_shape = [SDS((N_DEV,) + source.shape, source.dtype)]
        self.scratch = [pltpu.SemaphoreType.DMA((N_DEV - 1,)), pltpu.SemaphoreType.DMA((N_DEV - 1,)),
                        pltpu.SemaphoreType.DMA((1,))]

    def _copies(self, srcs, dsts, send_sems, recv_sems, local_sems):
        (src,), (dst,) = srcs, dsts
        x, y, c = _mesh_pos()
        me = 4 * x + 2 * y + c
        local = [pltpu.make_async_copy(src, dst.at[me], local_sems.at[0])]
        sends, recvs = [], []
        for k in range(1, N_DEV):
            px, py, pc = (1 - x) if k & 4 else x, (1 - y) if k & 2 else y, (1 - c) if k & 1 else c
            sems = dict(send_sem=send_sems.at[k - 1], recv_sem=recv_sems.at[k - 1], device_id=(px, py, pc),
                        device_id_type=MESH_ID)
            sends.append(pltpu.make_async_remote_copy(src_ref=src, dst_ref=dst.at[me], **sems))
            recvs.append(pltpu.make_async_remote_copy(src_ref=src, dst_ref=dst.at[4 * px + 2 * py + pc], **sems))
        return local, sends, recvs

    start = _ChipExchange.start
    wait = _ChipExchange.wait


def _gather_devices(a, name):
    ex = _DeviceGather(a)

    def body(a_ref, g_ref, *sems):
        ex.start((a_ref,), (g_ref,), sems)
        ex.wait((a_ref,), (g_ref,), sems)

    return pl.pallas_call(
        body, name=name, in_specs=ex.in_specs, out_specs=ex.out_specs[0], out_shape=ex.out_shape[0],
        scratch_shapes=ex.scratch, compiler_params=pltpu.CompilerParams(has_side_effects=True),
    )(a)


def _swap_cores(pin, pout):
    def body(pin_ref, pout_ref, oin_ref, oout_ref, send_sems, recv_sems):
        x, y, c = _mesh_pos()
        cps = [pltpu.make_async_remote_copy(src_ref=src, dst_ref=dst, send_sem=send_sems.at[a], recv_sem=recv_sems.at[a],
                                            device_id=(x, y, 1 - c), device_id_type=MESH_ID)
               for a, (src, dst) in enumerate(((pin_ref, oin_ref), (pout_ref, oout_ref)))]
        for cp in cps:
            cp.start()
        for cp in cps:
            cp.wait()

    return pl.pallas_call(
        body, name="swap_cores", in_specs=[ANY, ANY], out_specs=[ANY, ANY],
        out_shape=[SDS(pin.shape, F32), SDS(pout.shape, F32)],
        scratch_shapes=[pltpu.SemaphoreType.DMA((2,)), pltpu.SemaphoreType.DMA((2,))],
        compiler_params=pltpu.CompilerParams(has_side_effects=True),
    )(pin, pout)


PACK_TILE = 8 * 128


def _pack_rows(size):
    return (size + PACK_TILE - 1) // PACK_TILE * 8


def _small_offsets():
    offs = [0]
    for _, shape in SMALL_PARAMS:
        offs.append(offs[-1] + _pack_rows(math.prod(shape)))
    return offs


def _rows_view(a):
    flat = a.reshape(-1)
    return jnp.pad(flat, (0, (-flat.size) % 128)).reshape(-1, 128)


def _from_rows(rows, shape):
    return rows.reshape(-1)[:math.prod(shape)].reshape(shape)


def _layer_consts(l, gmlp_ln_g, gmlp_ln_b, gmlp_w_s, gmlp_b_s, hgrn_onorm_g, fox_b_f):
    causal = jnp.tril(jnp.ones((CHUNK, CHUNK), bool))
    wm = jnp.where(causal[None], gmlp_w_s[l], 0.0)
    return dict(
        lng=gmlp_ln_g[l].reshape(1, A_WIDTH), lnb=gmlp_ln_b[l].reshape(1, A_WIDTH),
        wm=wm.astype(BF16), wmt=jnp.swapaxes(wm, 1, 2).astype(BF16),
        bst=jnp.pad(gmlp_b_s[l].T, ((0, 0), (0, 128 - A_GROUPS))),
        onorm=jnp.tile(hgrn_onorm_g[l], 4).reshape(1, B_WIDTH),
        bf=jnp.pad(fox_b_f[l], (0, 128 - C_HEADS)).reshape(1, 128),
    )


def kernel(x, norm_g, w_in, w_out, gmlp_ln_g, gmlp_ln_b, gmlp_w_s, gmlp_b_s, hgrn_lb, hgrn_onorm_g, fox_b_f, final_norm_g, loss_target, m_norm_g, m_w_in, m_w_out, m_gmlp_ln_g, m_gmlp_ln_b, m_gmlp_w_s, m_gmlp_b_s, m_hgrn_lb, m_hgrn_onorm_g, m_fox_b_f, m_final_norm_g, v_norm_g, v_w_in, v_w_out, v_gmlp_ln_g, v_gmlp_ln_b, v_gmlp_w_s, v_gmlp_b_s, v_hgrn_lb, v_hgrn_onorm_g, v_fox_b_f, v_final_norm_g):
    T = x.shape[1]
    shard_in = w_in.shape[2]
    shard_out = w_out.shape[1]
    xs = x.reshape(T, D_MODEL)
    tgt = loss_target.reshape(T, D_MODEL)

    w_in_b, w_out_b = w_in.astype(BF16), w_out.astype(BF16)

    def full_w_in(gathered):
        return jnp.concatenate([gathered[k] for k in range(N_CHIPS)] + [jnp.zeros((D_MODEL, D_IN_PAD - D_IN), BF16)], axis=-1)

    lb_all = _lb_fwd(hgrn_lb)
    consts = [_layer_consts(l, gmlp_ln_g, gmlp_ln_b, gmlp_w_s, gmlp_b_s, hgrn_onorm_g, fox_b_f) for l in range(DEPTH)]

    saved = []
    xl = xs
    w_in_l = full_w_in(_gather_halves(w_in_b[0], "w_in_l0"))
    for l in range(DEPTH):
        cs = consts[l]
        tag = f"l{l}"
        h, proj = _inproj(xl, norm_g[l].reshape(1, D_MODEL), w_in_l, tag)
        (ya,), (yb, ob, s0), (qt, kt, vt) = _run_parts(
            [_gmlp_fwd(proj, cs["lng"], cs["lnb"], cs["wm"], cs["bst"]),
             _hgrn_fwd(proj, lb_all[l].reshape(1, B_WIDTH), cs["onorm"]), _fox_prep(proj, cs["bf"])],
            (T // CHUNK,), f"mix_fwd_{tag}")
        ride = _ChipExchange("gather", (w_out_b[l],) + ((w_in_b[l + 1],) if l + 1 < DEPTH else ()))
        oc, lse, yc, *gathered = _fox_fwd(qt, kt, vt, proj, tag, ride)
        w_out_l = gathered[0].reshape(N_CHIPS * shard_out, D_MODEL)
        saved.append(dict(x=xl, h=h, proj=proj, ya=ya, yb=yb, yc=yc, ob=ob, s0=s0, qt=qt, kt=kt, oc=oc, lse=lse,
                          w_in=w_in_l, w_out=w_out_l))
        xl = _outproj(xl, ya, yb, yc, w_out_l, tag)
        if l + 1 < DEPTH:
            w_in_l = full_w_in(gathered[1])

    dx, loss_part, d_final = _loss_head(xl, final_norm_g.reshape(1, D_MODEL), tgt)

    g_small = {}
    dlb_rows, rin, rout = [None] * DEPTH, [None] * DEPTH, [None] * DEPTH
    slabs_in = None
    for l in reversed(range(DEPTH)):
        cs, sv = consts[l], saved[l]
        tag = f"l{l}"
        proj = sv["proj"]
        dy, dw_out = _outproj_bwd(dx, sv["ya"], sv["yb"], sv["yc"], sv["w_out"], tag)
        (da, dwm, dbst, dlng, dlnb), (db, dlb_rows[l], donorm) = _run_parts(
            [_gmlp_bwd(proj, dy, cs["lng"], cs["lnb"], cs["wm"], cs["wmt"], cs["bst"]),
             _hgrn_bwd(proj, dy, sv["ob"], sv["s0"], lb_all[l].reshape(1, B_WIDTH), cs["onorm"])],
            (T // CHUNK,), f"mix_bwd_{tag}")
        do, delta, dzc, dot, qtr = _fox_bwd_prep(proj, dy, sv["oc"], sv["qt"], tag)
        slabs_out = dw_out.reshape(N_CHIPS, shard_out, D_MODEL).astype(BF16)
        ride = _ChipExchange("scatter", (slabs_out,) + ((slabs_in,) if slabs_in is not None else ()))
        dqt, dkt, dvc, *received = _fox_bwd(sv["qt"], sv["kt"], proj, do, sv["lse"], delta, dot, qtr, tag, ride)
        rout[l] = received[0]
        if slabs_in is not None:
            rin[l + 1] = received[1]
        dqc, dkc, dflc, dbf = _fox_bwd_post(dqt, dkt, proj, cs["bf"], tag)
        g_small[l] = dict(ln_g=dlng, ln_b=dlnb, w_s=dwm, b_s=dbst, onorm=donorm, bf=dbf)
        dproj = jnp.concatenate([da, db, dqc, dkc, dvc, dzc, dflc, jnp.zeros((T, 128), BF16)], axis=1)
        if l == 0:
            d_hgrn_lb = _lb_bwd(hgrn_lb, jnp.concatenate(dlb_rows, axis=0))
            per_layer = lambda key: [g_small[k][key] for k in range(DEPTH)]
            early = _pack_grads(per_layer("ln_g"), per_layer("ln_b"), per_layer("w_s"), per_layer("b_s"), d_hgrn_lb,
                                per_layer("onorm"), per_layer("bf"), d_final, loss_part)
            dw_in, rearly = _dw_in(sv["h"], dproj, tag, _DeviceGather(early))
        else:
            dw_in = _dw_in(sv["h"], dproj, tag)
        slabs_in = dw_in[:N_CHIPS * shard_in].reshape(N_CHIPS, shard_in, D_MODEL).astype(BF16)
        ride = _ChipExchange("scatter", (slabs_in,)) if l == 0 else None
        dx, dng, *received = _dx_in(sv["x"], norm_g[l].reshape(1, D_MODEL), dx, dproj, sv["w_in"], tag, ride)
        if l == 0:
            rin[0] = received[0]
        g_small[l]["norm_g"] = dng.reshape(D_MODEL // 128, 128)
    grad_x = dx.reshape(x.shape)
    rlate = _gather_devices(jnp.concatenate([g_small[l]["norm_g"] for l in range(DEPTH)]), "gather_norm_grads")
    rsmall = jnp.concatenate([rlate, rearly], axis=1)

    pin, pout = _sum_chips(rin, "sum_chips_w_in", False), _sum_chips(rout, "sum_chips_w_out", True)
    oin, oout = _swap_cores(pin, pout)
    to_view = lambda a: jnp.transpose(a, (2, 0, 1))
    g_w_in, d_w_in, nm_w_in, nv_w_in = [
        jnp.transpose(o, (1, 2, 0))
        for o in _adamw_pair(to_view(w_in), to_view(m_w_in), to_view(v_w_in), pin, oin, "adamw_w_in")]
    g_w_out, d_w_out, nm_w_out, nv_w_out = _adamw_pair(w_out, m_w_out, v_w_out, pout, oout, "adamw_w_out")

    small_w = [norm_g, gmlp_ln_g, gmlp_ln_b, gmlp_w_s, gmlp_b_s, hgrn_lb, hgrn_onorm_g, fox_b_f, final_norm_g]
    small_m = [m_norm_g, m_gmlp_ln_g, m_gmlp_ln_b, m_gmlp_w_s, m_gmlp_b_s, m_hgrn_lb, m_hgrn_onorm_g, m_fox_b_f, m_final_norm_g]
    small_v = [v_norm_g, v_gmlp_ln_g, v_gmlp_ln_b, v_gmlp_w_s, v_gmlp_b_s, v_hgrn_lb, v_hgrn_onorm_g, v_fox_b_f, v_final_norm_g]
    views = lambda ps: [_rows_view(p) for p in ps]
    per_param, loss_row = _adamw_small(views(small_w), views(small_m), views(small_v), rsmall)
    sg, sd, sm, sv_ = [[_from_rows(per_param[k][a], shape) for k, (_, shape) in enumerate(SMALL_PARAMS)] for a in range(4)]
    loss = loss_row[0, 0]

    def order(big_in, big_out, small):
        return [small[0], big_in, big_out] + small[1:]

    return (loss, grad_x, *order(g_w_in, g_w_out, sg), *order(d_w_in, d_w_out, sd), *order(nm_w_in, nm_w_out, sm),
            *order(nv_w_in, nv_w_out, sv_))
```

```python
import collections
import functools
import math

import jax
import jax.numpy as jnp
from jax import lax
from jax.experimental import pallas as pl
from jax.experimental.pallas import tpu as pltpu

F32 = jnp.float32
BF16 = jnp.bfloat16
SDS = jax.ShapeDtypeStruct
MESH_ID = pl.DeviceIdType.MESH

D_MODEL = 1024
DEPTH = 2
A_WIDTH = 256
A_GROUPS = 4
B_WIDTH = 256
C_WIDTH = 512
C_HEADS = 8
D_IN = 3848
D_IN_PAD = 4096
CHUNK = 128
SUB = 16
SUB_SHIFT = 4
NORM_EPS = 1e-6
F_FLOOR = 1e-30
COL_AU, COL_AV, COL_AZ = 0, 256, 512
COL_BQ, COL_BF, COL_BI, COL_BZ = 768, 1024, 1280, 1536
COL_CQ, COL_CK, COL_CV, COL_CZ, COL_CF = 1792, 2304, 2816, 3328, 3840
HEAD_LANES = 128
Q_SCALE = 0.125
ADAM_LR, ADAM_B1, ADAM_B2, ADAM_EPS, ADAM_WD, ADAM_STEP = 0.001, 0.9, 0.999, 1e-08, 0.01, 10
ADAM_C1 = 1.0 - ADAM_B1 ** ADAM_STEP
ADAM_C2 = 1.0 - ADAM_B2 ** ADAM_STEP
VMEM_LIMIT = 56 * 1024 * 1024
ADAMW_BLOCK_BYTES = 1 << 20
N_CHIPS = 4
N_DEV = 8

SMALL_PARAMS = (
    ("norm_g", (DEPTH, D_MODEL)), ("gmlp_ln_g", (DEPTH, 4, 64)), ("gmlp_ln_b", (DEPTH, 4, 64)),
    ("gmlp_w_s", (DEPTH, 4, 128, 128)), ("gmlp_b_s", (DEPTH, 4, 128)), ("hgrn_lb", (DEPTH, 256)),
    ("hgrn_onorm_g", (DEPTH, 64)), ("fox_b_f", (DEPTH, 8)), ("final_norm_g", (D_MODEL,)),
)


def _tile(n, pref):
    t = min(n, pref)
    assert n % t == 0, (n, pref)
    return t


def _params(*sem):
    return pltpu.CompilerParams(dimension_semantics=sem, vmem_limit_bytes=VMEM_LIMIT)


_Part = collections.namedtuple("_Part", "body operands in_specs out_specs out_shape scratch")


def _run_parts(parts, grid, name):
    counts = [(len(p.operands), len(p.out_shape), len(p.scratch)) for p in parts]

    def body(*refs):
        ins, outs, scr = [], [], []
        pos = 0
        for group, k in ((ins, 0), (outs, 1), (scr, 2)):
            for c in counts:
                group.append(refs[pos:pos + c[k]])
                pos += c[k]
        for p, i, o, s in zip(parts, ins, outs, scr):
            p.body(*i, *o, *s)

    flat = lambda key: [x for p in parts for x in getattr(p, key)]
    res = pl.pallas_call(
        body, name=name, grid=grid, in_specs=flat("in_specs"), out_specs=flat("out_specs"), out_shape=flat("out_shape"),
        scratch_shapes=flat("scratch"), compiler_params=_params(*(("arbitrary",) * len(grid))),
    )(*flat("operands"))
    out, pos = [], 0
    for c in counts:
        out.append(list(res[pos:pos + c[1]]))
        pos += c[1]
    return out


def _dot(a, b):
    return jnp.dot(a, b, preferred_element_type=F32)


def _dot_nt(a, b):
    return lax.dot_general(a, b, (((1,), (1,)), ((), ())), preferred_element_type=F32)


def _dot_tn(a, b):
    return lax.dot_general(a, b, (((0,), (0,)), ((), ())), preferred_element_type=F32)


def _split3(x):
    hi = x.astype(BF16)
    r = x - hi.astype(F32)
    mid = r.astype(BF16)
    lo = (r - mid.astype(F32)).astype(BF16)
    return hi, mid, lo


def _dot3_left(c, x):
    hi, mid, lo = _split3(x)
    return _dot(c, hi) + _dot(c, mid) + _dot(c, lo)


def _sigmoid(x):
    return jax.nn.sigmoid(x)


def _silu_and_grad(x):
    s = _sigmoid(x)
    return x * s, s * (1.0 + x * (1.0 - s))


_GELU_C = math.sqrt(2.0 / math.pi)


def _gelu_and_grad(x):
    inner = _GELU_C * (x + 0.044715 * x * x * x)
    t = jnp.tanh(inner)
    y = 0.5 * x * (1.0 + t)
    dy = 0.5 * (1.0 + t) + 0.5 * x * (1.0 - t * t) * _GELU_C * (1.0 + 3.0 * 0.044715 * x * x)
    return y, dy


def _lane(shape):
    return lax.broadcasted_iota(jnp.int32, shape, 1)


def _row(shape):
    return lax.broadcasted_iota(jnp.int32, shape, 0)


def _gsum64(x):
    lo = _lane(x.shape) < 64
    s0 = jnp.sum(jnp.where(lo, x, 0.0), axis=-1, keepdims=True)
    s1 = jnp.sum(jnp.where(lo, 0.0, x), axis=-1, keepdims=True)
    return jnp.where(lo, s0, s1)


def _colreduce(x, op):
    parts = [x[r:r + 8, :] for r in range(0, x.shape[0], 8)]
    while len(parts) > 1:
        pairs = [op(parts[k], parts[k + 1]) for k in range(0, len(parts) - 1, 2)]
        parts = pairs + ([parts[-1]] if len(parts) % 2 else [])
    red = jnp.max if op is jnp.maximum else jnp.sum
    return red(parts[0], axis=0, keepdims=True)


def _block_diag64(dtype=BF16):
    r, c = _row((128, 128)), _lane((128, 128))
    return jnp.where((r >> 6) == (c >> 6), 1.0, 0.0).astype(dtype)


def _inproj(x, g, w, tag):
    T, D = x.shape
    DP = w.shape[1]
    tm = _tile(T, 512)

    def body(x_ref, g_ref, w_ref, h_ref, p_ref):
        xv = x_ref[...]
        r = lax.rsqrt(jnp.mean(xv * xv, axis=-1, keepdims=True) + NORM_EPS)
        h = (xv * r * g_ref[...]).astype(BF16)
        h_ref[...] = h
        p_ref[...] = _dot(h, w_ref[...])

    return pl.pallas_call(
        body, name=f"inproj_{tag}", grid=(T // tm,),
        in_specs=[pl.BlockSpec((tm, D), lambda i: (i, 0)), pl.BlockSpec((1, D), lambda i: (0, 0)),
                  pl.BlockSpec((D, DP), lambda i: (0, 0))],
        out_specs=[pl.BlockSpec((tm, D), lambda i: (i, 0)), pl.BlockSpec((tm, DP), lambda i: (i, 0))],
        out_shape=[SDS((T, D), BF16), SDS((T, DP), F32)],
        compiler_params=_params("parallel"),
    )(x, g, w)


def _outproj(x, ya, yb, yc, wo, tag):
    T, D = x.shape
    tm = _tile(T, 512)

    def body(x_ref, ya_ref, yb_ref, yc_ref, wo_ref, o_ref):
        acc = x_ref[...] + _dot(ya_ref[...], wo_ref[0:A_WIDTH, :])
        acc = acc + _dot(yb_ref[...], wo_ref[A_WIDTH:A_WIDTH + B_WIDTH, :])
        o_ref[...] = acc + _dot(yc_ref[...], wo_ref[A_WIDTH + B_WIDTH:, :])

    row = lambda w: pl.BlockSpec((tm, w), lambda i: (i, 0))
    return pl.pallas_call(
        body, name=f"outproj_{tag}", grid=(T // tm,),
        in_specs=[row(D), row(A_WIDTH), row(B_WIDTH), row(C_WIDTH), pl.BlockSpec(wo.shape, lambda i: (0, 0))],
        out_specs=row(D), out_shape=SDS((T, D), F32), compiler_params=_params("parallel"),
    )(x, ya, yb, yc, wo)


def _outproj_bwd(dx, ya, yb, yc, wo, tag):
    T, D = dx.shape
    DM = wo.shape[0]
    tm = _tile(T, 512)

    def body(dx_ref, ya_ref, yb_ref, yc_ref, wo_ref, dy_ref, dwo_ref):
        @pl.when(pl.program_id(0) == 0)
        def _():
            dwo_ref[...] = jnp.zeros_like(dwo_ref)

        dxb = dx_ref[...].astype(BF16)
        dy_ref[...] = _dot_nt(dxb, wo_ref[...])
        dwo_ref[0:A_WIDTH, :] += _dot_tn(ya_ref[...], dxb)
        dwo_ref[A_WIDTH:A_WIDTH + B_WIDTH, :] += _dot_tn(yb_ref[...], dxb)
        dwo_ref[A_WIDTH + B_WIDTH:, :] += _dot_tn(yc_ref[...], dxb)

    row = lambda w: pl.BlockSpec((tm, w), lambda i: (i, 0))
    return pl.pallas_call(
        body, name=f"outproj_bwd_{tag}", grid=(T // tm,),
        in_specs=[row(D), row(A_WIDTH), row(B_WIDTH), row(C_WIDTH), pl.BlockSpec(wo.shape, lambda i: (0, 0))],
        out_specs=[row(DM), pl.BlockSpec((DM, D), lambda i: (0, 0))],
        out_shape=[SDS((T, DM), F32), SDS((DM, D), F32)], compiler_params=_params("arbitrary"),
    )(dx, ya, yb, yc, wo)


def _dw_in(h, dproj, tag, ride=None):
    T, D = h.shape
    DP = dproj.shape[1]
    tm, tn = _tile(T, 1024), _tile(DP, 1024)
    grid = (DP // tn, T // tm)

    def body(h_ref, dp_ref, *rest):
        ride_srcs, (dw_ref,), ride_dsts, _, ride_sems = _ride_refs(ride, rest, 1, 0)
        _ride_start(ride, grid, ride_srcs, ride_dsts, ride_sems)

        @pl.when(pl.program_id(1) == 0)
        def _():
            dw_ref[...] = jnp.zeros_like(dw_ref)

        dw_ref[...] += _dot_tn(dp_ref[...], h_ref[...])
        _ride_wait(ride, grid, ride_srcs, ride_dsts, ride_sems)

    extra = ride or _ChipExchange("gather", ())
    out = pl.pallas_call(
        body, name=f"dw_in_{tag}", grid=grid,
        in_specs=[pl.BlockSpec((tm, D), lambda j, i: (i, 0)), pl.BlockSpec((tm, tn), lambda j, i: (i, j))] + extra.in_specs,
        out_specs=[pl.BlockSpec((tn, D), lambda j, i: (j, 0))] + extra.out_specs,
        out_shape=[SDS((DP, D), F32)] + extra.out_shape, scratch_shapes=extra.scratch if ride else [],
        compiler_params=pltpu.CompilerParams(dimension_semantics=("arbitrary", "arbitrary"), vmem_limit_bytes=VMEM_LIMIT,
                                             has_side_effects=bool(ride)),
    )(h, dproj, *extra.sources)
    return out if ride else out[0]


def _dx_in(x, g, dres, dproj, w, tag, ride=None):
    T, D = x.shape
    DP = w.shape[1]
    tm = _tile(T, 512)
    grid = (T // tm,)

    def body(x_ref, g_ref, dres_ref, dp_ref, w_ref, *rest):
        ride_srcs, (dx_ref, dg_ref), ride_dsts, _, ride_sems = _ride_refs(ride, rest, 2, 0)
        _ride_start(ride, grid, ride_srcs, ride_dsts, ride_sems)

        @pl.when(pl.program_id(0) == 0)
        def _():
            dg_ref[...] = jnp.zeros_like(dg_ref)

        dh = _dot_nt(dp_ref[...], w_ref[...])
        xv = x_ref[...]
        r = lax.rsqrt(jnp.mean(xv * xv, axis=-1, keepdims=True) + NORM_EPS)
        xh = xv * r
        dg_ref[...] += jnp.sum(dh * xh, axis=0, keepdims=True)
        dxh = dh * g_ref[...]
        dx_ref[...] = dres_ref[...] + r * (dxh - xh * jnp.mean(dxh * xh, axis=-1, keepdims=True))
        _ride_wait(ride, grid, ride_srcs, ride_dsts, ride_sems)

    extra = ride or _ChipExchange("gather", ())
    row = pl.BlockSpec((tm, D), lambda i: (i, 0))
    return pl.pallas_call(
        body, name=f"dx_in_{tag}", grid=grid,
        in_specs=[row, pl.BlockSpec((1, D), lambda i: (0, 0)), row, pl.BlockSpec((tm, DP), lambda i: (i, 0)),
                  pl.BlockSpec((D, DP), lambda i: (0, 0))] + extra.in_specs,
        out_specs=[row, pl.BlockSpec((1, D), lambda i: (0, 0))] + extra.out_specs,
        out_shape=[SDS((T, D), F32), SDS((1, D), F32)] + extra.out_shape,
        scratch_shapes=extra.scratch if ride else [],
        compiler_params=pltpu.CompilerParams(dimension_semantics=("arbitrary",), vmem_limit_bytes=VMEM_LIMIT,
                                             has_side_effects=bool(ride)),
    )(x, g, dres, dproj, w, *extra.sources)


def _loss_head(x, g, tgt):
    T, D = x.shape
    tm = _tile(T, 512)

    def body(x_ref, g_ref, t_ref, dx_ref, loss_ref, dg_ref):
        @pl.when(pl.program_id(0) == 0)
        def _():
            loss_ref[...] = jnp.zeros_like(loss_ref)
            dg_ref[...] = jnp.zeros_like(dg_ref)

        xv = x_ref[...]
        r = lax.rsqrt(jnp.mean(xv * xv, axis=-1, keepdims=True) + NORM_EPS)
        xh = xv * r
        gv = g_ref[...]
        err = xh * gv - t_ref[...]
        tok = jnp.mean(err * err, axis=-1, keepdims=True)
        loss_ref[...] += 0.5 * jnp.sum(tok, axis=0, keepdims=True)
        dy = err * (1.0 / D)
        dg_ref[...] += jnp.sum(dy * xh, axis=0, keepdims=True)
        dxh = dy * gv
        dx_ref[...] = r * (dxh - xh * jnp.mean(dxh * xh, axis=-1, keepdims=True))

    row = pl.BlockSpec((tm, D), lambda i: (i, 0))
    return pl.pallas_call(
        body, name="loss_head", grid=(T // tm,),
        in_specs=[row, pl.BlockSpec((1, D), lambda i: (0, 0)), row],
        out_specs=[row, pl.BlockSpec((1, 128), lambda i: (0, 0)), pl.BlockSpec((1, D), lambda i: (0, 0))],
        out_shape=[SDS((T, D), F32), SDS((1, 128), F32), SDS((1, D), F32)], compiler_params=_params("arbitrary"),
    )(x, g, tgt)


def _gmlp_core(u, v, lng, lnb, wm_ref, bst_ref, pair):
    ug, dug = _gelu_and_grad(u)
    vg, dvg = _gelu_and_grad(v)
    mu = _gsum64(vg) * (1.0 / 64)
    d = vg - mu
    var = _gsum64(d * d) * (1.0 / 64)
    rstd = lax.rsqrt(var + NORM_EPS)
    xh = d * rstd
    vn = xh * lng + lnb
    vnb = vn.astype(BF16)
    lo = _lane(u.shape) < 64
    g0, g1 = 2 * pair, 2 * pair + 1
    mixed = jnp.where(lo, _dot(wm_ref[g0], vnb) + bst_ref[:, g0:g0 + 1], _dot(wm_ref[g1], vnb) + bst_ref[:, g1:g1 + 1])
    return ug, dug, dvg, rstd, xh, vnb, mixed, lo


def _gmlp_fwd(proj, lng, lnb, wm, bst):
    T = proj.shape[0]

    def body(u_ref, v_ref, z_ref, lng_ref, lnb_ref, wm_ref, bst_ref, y_ref):
        for pair in range(2):
            sl = slice(128 * pair, 128 * pair + 128)
            ug, _, _, _, _, _, mixed, _ = _gmlp_core(u_ref[:, sl], v_ref[:, sl], lng_ref[:, sl], lnb_ref[:, sl],
                                                     wm_ref, bst_ref, pair)
            sz, _ = _silu_and_grad(z_ref[:, sl])
            y_ref[:, sl] = (ug * mixed * sz).astype(BF16)

    col = lambda c: pl.BlockSpec((CHUNK, A_WIDTH), lambda i, c=c: (i, c // A_WIDTH))
    full = lambda a: pl.BlockSpec(a.shape, lambda i, n=a.ndim: (0,) * n)
    return _Part(body, (proj, proj, proj, lng, lnb, wm, bst),
                 [col(COL_AU), col(COL_AV), col(COL_AZ), full(lng), full(lnb), full(wm), full(bst)],
                 [pl.BlockSpec((CHUNK, A_WIDTH), lambda i: (i, 0))], [SDS((T, A_WIDTH), BF16)], [])


def _gmlp_bwd(proj, dy, lng, lnb, wm, wmt, bst):
    T = proj.shape[0]
    n = T // CHUNK

    def body(u_ref, v_ref, z_ref, dy_ref, lng_ref, lnb_ref, wm_ref, wmt_ref, bst_ref,
             da_ref, dwm_ref, dbst_ref, dlng_ref, dlnb_ref):
        @pl.when(pl.program_id(0) == 0)
        def _():
            dwm_ref[...] = jnp.zeros_like(dwm_ref)
            dbst_ref[...] = jnp.zeros_like(dbst_ref)
            dlng_ref[...] = jnp.zeros_like(dlng_ref)
            dlnb_ref[...] = jnp.zeros_like(dlnb_ref)

        lane = _lane((CHUNK, 128))
        dbst = dbst_ref[...]
        for pair in range(2):
            sl = slice(128 * pair, 128 * pair + 128)
            lng_p = lng_ref[:, sl]
            ug, dug, dvg, rstd, xh, vnb, mixed, lo = _gmlp_core(u_ref[:, sl], v_ref[:, sl], lng_p, lnb_ref[:, sl],
                                                                wm_ref, bst_ref, pair)
            sz, dsz = _silu_and_grad(z_ref[:, sl])
            dyv = dy_ref[:, sl]
            out = ug * mixed
            dz = dyv * out * dsz
            dout = dyv * sz
            du = dout * mixed * dug
            dmix = dout * ug
            g0, g1 = 2 * pair, 2 * pair + 1
            dm0 = jnp.where(lo, dmix, 0.0)
            dm1 = jnp.where(lo, 0.0, dmix)
            dbst = dbst + jnp.where(lane == g0, jnp.sum(dm0, axis=-1, keepdims=True), 0.0)
            dbst = dbst + jnp.where(lane == g1, jnp.sum(dm1, axis=-1, keepdims=True), 0.0)
            dwm_ref[g0] += _dot_nt(dm0.astype(BF16), vnb)
            dwm_ref[g1] += _dot_nt(dm1.astype(BF16), vnb)
            dmb = dmix.astype(BF16)
            dvn = jnp.where(lo, _dot(wmt_ref[g0], dmb), _dot(wmt_ref[g1], dmb))
            dlng_ref[:, sl] += jnp.sum(dvn * xh, axis=0, keepdims=True)
            dlnb_ref[:, sl] += jnp.sum(dvn, axis=0, keepdims=True)
            dxh = dvn * lng_p
            m1 = _gsum64(dxh) * (1.0 / 64)
            m2 = _gsum64(dxh * xh) * (1.0 / 64)
            dv = rstd * (dxh - m1 - xh * m2) * dvg
            da_ref[:, COL_AU + 128 * pair:COL_AU + 128 * pair + 128] = du.astype(BF16)
            da_ref[:, COL_AV + 128 * pair:COL_AV + 128 * pair + 128] = dv.astype(BF16)
            da_ref[:, COL_AZ + 128 * pair:COL_AZ + 128 * pair + 128] = dz.astype(BF16)
        dbst_ref[...] = dbst

        @pl.when(pl.program_id(0) == n - 1)
        def _():
            causal = _lane((CHUNK, CHUNK)) <= _row((CHUNK, CHUNK))
            for g in range(A_GROUPS):
                dwm_ref[g] = jnp.where(causal, dwm_ref[g], 0.0)

    col = lambda c: pl.BlockSpec((CHUNK, A_WIDTH), lambda i, c=c: (i, c // A_WIDTH))
    full = lambda a: pl.BlockSpec(a.shape, lambda i, n=a.ndim: (0,) * n)
    acc = lambda s: pl.BlockSpec(s, lambda i, n=len(s): (0,) * n)
    return _Part(body, (proj, proj, proj, dy, lng, lnb, wm, wmt, bst),
                 [col(COL_AU), col(COL_AV), col(COL_AZ), pl.BlockSpec((CHUNK, A_WIDTH), lambda i: (i, 0)),
                  full(lng), full(lnb), full(wm), full(wmt), full(bst)],
                 [pl.BlockSpec((CHUNK, 3 * A_WIDTH), lambda i: (i, 0)), acc((A_GROUPS, CHUNK, CHUNK)),
                  acc((CHUNK, 128)), acc((1, A_WIDTH)), acc((1, A_WIDTH))],
                 [SDS((T, 3 * A_WIDTH), BF16), SDS((A_GROUPS, CHUNK, CHUNK), F32), SDS((CHUNK, 128), F32),
                  SDS((1, A_WIDTH), F32), SDS((1, A_WIDTH), F32)], [])


def _hgrn_consts():
    r, c = _row((CHUNK, CHUNK)), _lane((CHUNK, CHUNK))
    same = (r >> SUB_SHIFT) == (c >> SUB_SHIFT)
    lsub = jnp.where(same & (c <= r), 1.0, 0.0).astype(BF16)
    usub = jnp.where(same & (c >= r), 1.0, 0.0).astype(BF16)
    bsub = jnp.where(same, 1.0, 0.0).astype(BF16)
    return lsub, usub, bsub


def _hgrn_gates(qv, zf, lbp):
    sq, dsq = _silu_and_grad(qv)
    qt = sq * Q_SCALE
    sg = _sigmoid(zf)
    sgn = _sigmoid(-zf)
    f = lbp + (1.0 - lbp) * sg
    g = jnp.log(jnp.maximum(f, F_FLOOR))
    kf = (1.0 - lbp) * sgn
    return qt, dsq, sg, sgn, f, g, kf


def _hgrn_intra_scores(qt, kf, b, mbd):
    rid = _row((SUB, 128))
    parts = []
    for s in range(SUB):
        e = jnp.exp(b - b[s:s + 1, :])
        parts.append(jnp.where(rid >= s, qt * kf[s:s + 1, :] * e, 0.0))
    return _dot(jnp.concatenate(parts, axis=0).astype(BF16), mbd)


def _hgrn_intra_out(a, v):
    o = jnp.zeros((SUB, 128), F32)
    for s in range(SUB):
        o = o + a[SUB * s:SUB * s + SUB, :] * v[s:s + 1, :]
    return o


def _hgrn_intra_bwd_scores(qt, kf, b, v, do, mbd):
    rid = _row((SUB, 128))
    ps, das, kes, es = [], [], [], []
    for s in range(SUB):
        e = jnp.where(rid >= s, jnp.exp(b - b[s:s + 1, :]), 0.0)
        ke = kf[s:s + 1, :] * e
        es.append(e)
        kes.append(ke)
        ps.append(qt * ke)
        das.append(do * v[s:s + 1, :])
    a = _dot(jnp.concatenate(ps, axis=0).astype(BF16), mbd)
    da = _dot(jnp.concatenate(das, axis=0).astype(BF16), mbd)
    return a, da, kes, es


def _hgrn_intra_bwd_grads(scores, qt, do, rsum):
    a, da, kes, es = scores
    dqt = jnp.zeros((SUB, 128), F32)
    xs, ys = [], []
    for s in range(SUB):
        da_s = da[SUB * s:SUB * s + SUB, :]
        dqt = dqt + da_s * kes[s]
        xs.append(a[SUB * s:SUB * s + SUB, :] * do)
        ys.append(da_s * qt * es[s])
    dv = _dot(rsum, jnp.concatenate(xs, axis=0).astype(BF16))
    dkf = _dot(rsum, jnp.concatenate(ys, axis=0).astype(BF16))
    return dqt, dkf, dv


def _hgrn_norm_gate(o, z, onorm):
    ms = _gsum64(o * o) * (1.0 / 64)
    r = lax.rsqrt(ms + NORM_EPS)
    xh = o * r
    sz, dsz = _silu_and_grad(z)
    return xh, r, sz, dsz, xh * onorm


def _hgrn_fwd(proj, lb, onorm):
    T = proj.shape[0]
    n = T // CHUNK
    nsub = CHUNK // SUB

    def body(q_ref, f_ref, i_ref, z_ref, lb_ref, on_ref, y_ref, o_ref, s0_ref, st_ref):
        @pl.when(pl.program_id(0) == 0)
        def _():
            st_ref[...] = jnp.zeros_like(st_ref)

        lsub, _, bsub = _hgrn_consts()
        mbd = _block_diag64()
        bdmask = mbd > 0
        rid = _row((CHUNK, 128))
        for pair in range(2):
            sl = slice(128 * pair, 128 * pair + 128)
            qt, _, _, _, _, g, kf = _hgrn_gates(q_ref[:, sl], f_ref[:, sl], lb_ref[:, sl])
            v = i_ref[:, sl]
            b = _dot3_left(lsub, g)
            bl = _dot3_left(bsub, g)
            qh = (qt * jnp.exp(b)).astype(BF16)
            kh = kf * jnp.exp(bl - b)
            dec = jnp.exp(bl)
            vtb = v.T.astype(BF16)
            st = st_ref[pair]
            s0_ref[0, pair] = st
            subs = [slice(SUB * sub, SUB * sub + SUB) for sub in range(nsub)]
            scores = [_hgrn_intra_scores(qt[rs], kf[rs], b[rs], mbd) for rs in subs]
            adds = [_dot(vtb, jnp.where((rid >> SUB_SHIFT) == sub, kh, 0.0).astype(BF16)) for sub in range(nsub)]
            outs = []
            for sub, rs in enumerate(subs):
                outs.append(_dot_nt(qh[rs], st.astype(BF16)) + _hgrn_intra_out(scores[sub], v[rs]))
                st = jnp.where(bdmask, st * dec[SUB * sub:SUB * sub + 1, :] + adds[sub], 0.0)
            st_ref[pair] = st
            o = jnp.concatenate(outs, axis=0)
            o_ref[:, sl] = o
            _, _, sz, _, on = _hgrn_norm_gate(o, z_ref[:, sl], on_ref[:, sl])
            y_ref[:, sl] = (on * sz).astype(BF16)

    col = lambda c: pl.BlockSpec((CHUNK, B_WIDTH), lambda i, c=c: (i, c // B_WIDTH))
    full = lambda a: pl.BlockSpec(a.shape, lambda i, n=a.ndim: (0,) * n)
    return _Part(body, (proj, proj, proj, proj, lb, onorm),
                 [col(COL_BQ), col(COL_BF), col(COL_BI), col(COL_BZ), full(lb), full(onorm)],
                 [pl.BlockSpec((CHUNK, B_WIDTH), lambda i: (i, 0)), pl.BlockSpec((CHUNK, B_WIDTH), lambda i: (i, 0)),
                  pl.BlockSpec((1, 2, 128, 128), lambda i: (i, 0, 0, 0))],
                 [SDS((T, B_WIDTH), BF16), SDS((T, B_WIDTH), F32), SDS((n, 2, 128, 128), F32)],
                 [pltpu.VMEM((2, 128, 128), F32)])


def _hgrn_bwd(proj, dy, o_saved, s0, lb, onorm):
    T = proj.shape[0]
    n = T // CHUNK
    nsub = CHUNK // SUB

    def body(q_ref, f_ref, i_ref, z_ref, dy_ref, o_ref, s0_ref, lb_ref, on_ref,
             db_ref, dlb_ref, don_ref, dst_ref, sts_ref):
        @pl.when(pl.program_id(0) == 0)
        def _():
            dst_ref[...] = jnp.zeros_like(dst_ref)
            dlb_ref[...] = jnp.zeros_like(dlb_ref)
            don_ref[...] = jnp.zeros_like(don_ref)

        lsub, usub, bsub = _hgrn_consts()
        mbd = _block_diag64()
        bdmask = mbd > 0
        rsum = jnp.where((_lane((SUB, SUB * SUB)) >> SUB_SHIFT) == _row((SUB, SUB * SUB)), 1.0, 0.0).astype(BF16)
        for pair in range(2):
            sl = slice(128 * pair, 128 * pair + 128)
            lbp = lb_ref[:, sl]
            qv, zf = q_ref[:, sl], f_ref[:, sl]
            qt, dsq, sg, sgn, f, g, kf = _hgrn_gates(qv, zf, lbp)
            v = i_ref[:, sl]
            b = _dot3_left(lsub, g)
            bl = _dot3_left(bsub, g)
            eb = jnp.exp(b)
            ekb = jnp.exp(bl - b)
            qhb = (qt * eb).astype(BF16)
            khb = (kf * ekb).astype(BF16)
            dec = jnp.exp(bl)
            vb = v.astype(BF16)
            onp = on_ref[:, sl]
            ov = o_ref[:, sl]
            xh, r, sz, dsz, on = _hgrn_norm_gate(ov, z_ref[:, sl], onp)
            dyv = dy_ref[:, sl]
            dz = dyv * on * dsz
            don = dyv * sz
            cn = jnp.sum(don * xh, axis=0, keepdims=True)
            don_ref[...] += cn + pltpu.roll(cn, 64, axis=1)
            dxo = don * onp
            do = r * (dxo - xh * (_gsum64(dxo * xh) * (1.0 / 64)))
            dob = do.astype(BF16)
            subs = [slice(SUB * sub, SUB * sub + SUB) for sub in range(nsub)]
            scores = [_hgrn_intra_bwd_scores(qt[rs], kf[rs], b[rs], v[rs], do[rs], mbd) for rs in subs]
            st_adds = [_dot_tn(vb[rs], khb[rs]) for rs in subs]
            gst_adds = [_dot_tn(dob[rs], qhb[rs]) for rs in subs]
            st = s0_ref[0, pair]
            for sub in range(nsub):
                sts_ref[sub] = st
                st = jnp.where(bdmask, st * dec[SUB * sub:SUB * sub + 1, :] + st_adds[sub], 0.0)
            gst = dst_ref[pair]
            dqt_p, dkf_p, dv_p, dbl_p = [None] * nsub, [None] * nsub, [None] * nsub, [None] * nsub
            for sub in reversed(range(nsub)):
                rs = subs[sub]
                st_in = sts_ref[sub]
                gb = gst.astype(BF16)
                dqh = _dot(dob[rs], st_in.astype(BF16))
                dkh = _dot(vb[rs], gb)
                dv_inter = _dot_nt(khb[rs], gb)
                ddec = jnp.sum(gst * st_in, axis=0, keepdims=True)
                dec_row = dec[SUB * sub:SUB * sub + 1, :]
                gst = jnp.where(bdmask, gst * dec_row + gst_adds[sub], 0.0)
                dqt_i, dkf_i, dv_i = _hgrn_intra_bwd_grads(scores[sub], qt[rs], do[rs], rsum)
                dkf_inter = dkh * ekb[rs]
                dqt_p[sub] = dqh * eb[rs] + dqt_i
                dkf_p[sub] = dkf_inter + dkf_i
                dv_p[sub] = dv_inter + dv_i
                row = jnp.sum(kf[rs] * dkf_inter, axis=0, keepdims=True) + ddec * dec_row
                dbl_p[sub] = jnp.broadcast_to(row, (SUB, 128))
            dst_ref[pair] = gst
            dqt = jnp.concatenate(dqt_p, axis=0)
            dkf = jnp.concatenate(dkf_p, axis=0)
            dv = jnp.concatenate(dv_p, axis=0)
            dg = _dot3_left(usub, qt * dqt - kf * dkf) + jnp.concatenate(dbl_p, axis=0)
            df = jnp.where(f > F_FLOOR, dg / f, 0.0)
            dlb_ref[:, sl] += jnp.sum(df * (1.0 - sg) - dkf * sgn, axis=0, keepdims=True)
            dfl = (1.0 - lbp) * sg * sgn * (df - dkf)
            dq = dqt * Q_SCALE * dsq
            db_ref[:, 0 * B_WIDTH + 128 * pair:0 * B_WIDTH + 128 * pair + 128] = dq.astype(BF16)
            db_ref[:, 1 * B_WIDTH + 128 * pair:1 * B_WIDTH + 128 * pair + 128] = dfl.astype(BF16)
            db_ref[:, 2 * B_WIDTH + 128 * pair:2 * B_WIDTH + 128 * pair + 128] = dv.astype(BF16)
            db_ref[:, 3 * B_WIDTH + 128 * pair:3 * B_WIDTH + 128 * pair + 128] = dz.astype(BF16)

    rev = lambda c: pl.BlockSpec((CHUNK, B_WIDTH), lambda i, c=c: (n - 1 - i, c // B_WIDTH))
    full = lambda a: pl.BlockSpec(a.shape, lambda i, n_=a.ndim: (0,) * n_)
    acc = lambda s: pl.BlockSpec(s, lambda i, n_=len(s): (0,) * n_)
    return _Part(body, (proj, proj, proj, proj, dy, o_saved, s0, lb, onorm),
                 [rev(COL_BQ), rev(COL_BF), rev(COL_BI), rev(COL_BZ),
                  pl.BlockSpec((CHUNK, B_WIDTH), lambda i: (n - 1 - i, 1)),
                  pl.BlockSpec((CHUNK, B_WIDTH), lambda i: (n - 1 - i, 0)),
                  pl.BlockSpec((1, 2, 128, 128), lambda i: (n - 1 - i, 0, 0, 0)), full(lb), full(onorm)],
                 [pl.BlockSpec((CHUNK, 4 * B_WIDTH), lambda i: (n - 1 - i, 0)), acc((1, B_WIDTH)), acc((1, 128))],
                 [SDS((T, 4 * B_WIDTH), BF16), SDS((1, B_WIDTH), F32), SDS((1, 128), F32)],
                 [pltpu.VMEM((2, 128, 128), F32), pltpu.VMEM((nsub, 128, 128), F32)])


def _lb_fwd(hgrn_lb):
    assert hgrn_lb.shape[0] == 2

    def body(x_ref, o_ref):
        x0, x1 = x_ref[0:1, :], x_ref[1:2, :]
        m = jnp.maximum(x0, x1)
        e0, e1 = jnp.exp(x0 - m), jnp.exp(x1 - m)
        p0, p1 = e0 / (e0 + e1), e1 / (e0 + e1)
        o_ref[0:1, :] = jnp.clip(p0 - p0, 0.0, 1.0 - 1e-6)
        o_ref[1:2, :] = jnp.clip((p0 + p1) - p0, 0.0, 1.0 - 1e-6)

    return pl.pallas_call(body, name="lb_fwd", out_shape=SDS(hgrn_lb.shape, F32))(hgrn_lb)


def _lb_bwd(hgrn_lb, dlb):
    def body(x_ref, d_ref, o_ref):
        x0, x1 = x_ref[0:1, :], x_ref[1:2, :]
        m = jnp.maximum(x0, x1)
        e0, e1 = jnp.exp(x0 - m), jnp.exp(x1 - m)
        p0, p1 = e0 / (e0 + e1), e1 / (e0 + e1)
        val = (p0 + p1) - p0
        dp1 = jnp.where((val > 0.0) & (val < 1.0 - 1e-6), d_ref[1:2, :], 0.0)
        inner = p1 * dp1
        o_ref[0:1, :] = p0 * (0.0 - inner)
        o_ref[1:2, :] = p1 * (dp1 - inner)

    return pl.pallas_call(body, name="lb_bwd", out_shape=SDS(hgrn_lb.shape, F32))(hgrn_lb, dlb)


def _fox_prep(proj, bf):
    T = proj.shape[0]
    n = T // CHUNK

    def body(q0_ref, q1_ref, k0_ref, k1_ref, v0_ref, v1_ref, fl_ref, bf_ref, qo_ref, ko_ref, vt_ref, carry_ref):
        for p, v_ref in enumerate((v0_ref, v0_ref, v1_ref, v1_ref)):
            vt_ref[p, 0] = v_ref[:, 128 * (p % 2):128 * (p % 2) + 128].T.astype(BF16)

        @pl.when(pl.program_id(0) == 0)
        def _():
            carry_ref[...] = jnp.zeros_like(carry_ref)

        ltri = jnp.where(_lane((CHUNK, CHUNK)) <= _row((CHUNK, CHUNK)), 1.0, 0.0).astype(BF16)
        lf = jax.nn.log_sigmoid(fl_ref[...] + bf_ref[...])
        c = _dot3_left(ltri, lf) + carry_ref[...]
        carry_ref[...] = c[CHUNK - 1:CHUNK, :]
        lane = _lane((CHUNK, 128))
        feat = lane < 64
        ones_q = (lane >= 67) & (lane <= 69)
        ones_k = (lane >= 64) & (lane <= 66)
        qrefs, krefs = (q0_ref, q1_ref), (k0_ref, k1_ref)
        for h in range(C_HEADS):
            blk = slice(128 * ((h // 2) % 2), 128 * ((h // 2) % 2) + 128)
            qp, kp = qrefs[h // 4][:, blk], krefs[h // 4][:, blk]
            if h % 2:
                qp, kp = pltpu.roll(qp, 64, axis=1), pltpu.roll(kp, 64, axis=1)
            ch = jnp.broadcast_to(c[:, h:h + 1], (CHUNK, 128))
            hi = ch.astype(BF16).astype(F32)
            r1 = ch - hi
            mid = r1.astype(BF16).astype(F32)
            lo = r1 - mid
            aq = jnp.where(lane == 64, hi, jnp.where(lane == 65, mid, jnp.where(lane == 66, lo,
                           jnp.where(ones_q, 1.0, 0.0))))
            ak = jnp.where(lane == 67, -hi, jnp.where(lane == 68, -mid, jnp.where(lane == 69, -lo,
                           jnp.where(ones_k, 1.0, 0.0))))
            qo_ref[:, 128 * h:128 * h + 128] = jnp.where(feat, qp * Q_SCALE, aq).astype(BF16)
            ko_ref[:, 128 * h:128 * h + 128] = jnp.where(feat, kp, ak).astype(BF16)

    w = 256
    col = lambda c: pl.BlockSpec((CHUNK, w), lambda i, c=c: (i, c // w))
    return _Part(body, (proj, proj, proj, proj, proj, proj, proj, bf),
                 [col(COL_CQ), col(COL_CQ + w), col(COL_CK), col(COL_CK + w), col(COL_CV), col(COL_CV + w),
                  pl.BlockSpec((CHUNK, 128), lambda i: (i, COL_CF // 128)), pl.BlockSpec((1, 128), lambda i: (0, 0))],
                 [pl.BlockSpec((CHUNK, C_HEADS * 128), lambda i: (i, 0))] * 2
                 + [pl.BlockSpec((C_HEADS // 2, 1, 128, CHUNK), lambda i: (0, i, 0, 0))],
                 [SDS((T, C_HEADS * 128), BF16)] * 2 + [SDS((C_HEADS // 2, n, 128, CHUNK), BF16)],
                 [pltpu.VMEM((1, 128), F32)])


FOX_TILE = 512
FOX_KEYS = 512


def _fox_mask(tk, tq, k0, q0):
    return (_row((tk, tq)) + (k0 - q0)) <= _lane((tk, tq))


def _ride_refs(ride, rest, n_out, n_scratch):
    n = ride.n if ride else 0
    srcs, rest = rest[:n], rest[n:]
    outs, rest = rest[:n_out], rest[n_out:]
    dsts, rest = rest[:n], rest[n:]
    return srcs, outs, dsts, rest[:n_scratch], rest[n_scratch:]


def _ride_start(ride, grid, srcs, dsts, sems):
    if ride:
        first = functools.reduce(lambda a, b: a & b, [pl.program_id(d) == 0 for d in range(len(grid))])
        pl.when(first)(lambda: ride.start(srcs, dsts, sems))


def _ride_wait(ride, grid, srcs, dsts, sems):
    if ride:
        last = functools.reduce(lambda a, b: a & b, [pl.program_id(d) == n - 1 for d, n in enumerate(grid)])
        pl.when(last)(lambda: ride.wait(srcs, dsts, sems))


def _fox_fwd(qt, kt, vt, proj, tag, ride=None):
    T = proj.shape[0]
    tq, tk = _tile(T, FOX_TILE), _tile(T, FOX_KEYS)
    nq, nsub = T // tq, tk // CHUNK
    npair = C_HEADS // 2

    def body(q_ref, k_ref, vt_ref, z_ref, *rest):
        ride_srcs, (o_ref, lse_ref, y_ref), ride_dsts, (acc_ref, st_ref, pt_ref), ride_sems = _ride_refs(ride, rest, 3, 3)
        i = pl.program_id(1)
        _ride_start(ride, (npair, nq), ride_srcs, ride_dsts, ride_sems)

        qs = (q_ref[:, 0:128], q_ref[:, 128:256])
        acc_ref[...] = jnp.zeros_like(acc_ref)
        pt_ref[...] = jnp.zeros_like(pt_ref)
        nfull = (i * tq) // tk

        def scores(j):
            kb = k_ref[pl.ds(pl.multiple_of(j * tk, tk), tk), :]
            return tuple(_dot_nt(kb[:, 128 * h:128 * h + 128], qs[h]) for h in range(2))

        def weigh(j, h):
            rows = slice(64 * h, 64 * h + 64)
            pv = _dot(vt_ref[0, nsub * j, rows, :], pt_ref[h, 0:CHUNK, :])
            for c in range(1, nsub):
                pv = pv + _dot(vt_ref[0, nsub * j + c, rows, :], pt_ref[h, CHUNK * c:CHUNK * c + CHUNK, :])
            return pv

        def block(j, carry, diagonal):
            nxt = () if diagonal else scores(j + 1)
            pvs = [weigh(jnp.maximum(j - 1, 0), h) for h in range(2)]
            new = []
            for h in range(2):
                m, l, alpha_prev = carry[3 * h:3 * h + 3]
                st = st_ref[h]
                if diagonal:
                    st = jnp.where(_fox_mask(tk, tq, j * tk, i * tq), st, -jnp.inf)
                m_new = jnp.maximum(m, _colreduce(st, jnp.maximum))
                pt = jnp.exp(st - m_new)
                alpha = jnp.exp(m - m_new)
                rows = slice(64 * h, 64 * h + 64)
                acc_ref[rows, :] = alpha_prev * acc_ref[rows, :] + pvs[h]
                pt_ref[h] = pt.astype(BF16)
                new += [m_new, alpha * l + _colreduce(pt, jnp.add), alpha]
            for h, st in enumerate(nxt):
                st_ref[h] = st
            return tuple(new)

        for h, st in enumerate(scores(0)):
            st_ref[h] = st
        init = (jnp.full((1, tq), -jnp.inf, F32), jnp.zeros((1, tq), F32), jnp.ones((1, tq), F32)) * 2
        carry = lax.fori_loop(0, nfull, lambda j, c: block(j, c, False), init)
        m0, l0, a0, m1, l1, a1 = block(nfull, carry, True)
        for h, alpha in enumerate((a0, a1)):
            rows = slice(64 * h, 64 * h + 64)
            acc_ref[rows, :] = alpha * acc_ref[rows, :] + weigh(nfull, h)
        inv = jnp.where(_row((128, tq)) < 64, 1.0 / l0, 1.0 / l1)
        o = (acc_ref[...] * inv).T
        o_ref[...] = o
        r8 = _row((8, tq))
        lse_ref[0, 0] = jnp.where(r8 == 0, m0 + jnp.log(l0), jnp.where(r8 == 1, m1 + jnp.log(l1), 0.0))
        sz, _ = _silu_and_grad(z_ref[...])
        y_ref[...] = (o * sz).astype(BF16)
        _ride_wait(ride, (npair, nq), ride_srcs, ride_dsts, ride_sems)

    blk = pl.BlockSpec((tq, 128), lambda p, i: (i, p))
    extra = ride or _ChipExchange("gather", ())
    return pl.pallas_call(
        body, name=f"fox_fwd_{tag}", grid=(npair, nq),
        in_specs=[pl.BlockSpec((tq, 256), lambda p, i: (i, p)), pl.BlockSpec((T, 256), lambda p, i: (0, p)),
                  pl.BlockSpec((1, T // CHUNK, 128, CHUNK), lambda p, i: (p, 0, 0, 0)),
                  pl.BlockSpec((tq, 128), lambda p, i: (i, COL_CZ // 128 + p))] + extra.in_specs,
        out_specs=[blk, pl.BlockSpec((1, 1, 8, tq), lambda p, i: (p, i, 0, 0)), blk] + extra.out_specs,
        out_shape=[SDS((T, C_WIDTH), F32), SDS((npair, nq, 8, tq), F32), SDS((T, C_WIDTH), BF16)] + extra.out_shape,
        scratch_shapes=[pltpu.VMEM((128, tq), F32), pltpu.VMEM((2, tk, tq), F32), pltpu.VMEM((2, tk, tq), BF16)]
        + (extra.scratch if ride else []),
        compiler_params=pltpu.CompilerParams(dimension_semantics=("arbitrary", "arbitrary"), vmem_limit_bytes=VMEM_LIMIT,
                                             has_side_effects=bool(ride)),
    )(qt, kt, vt, proj, *extra.sources)


def _fox_bwd_prep(proj, dy, o, qt, tag):
    T = proj.shape[0]
    tq = _tile(T, FOX_TILE)
    nq = T // tq

    def body(z0_ref, z1_ref, dy_ref, o_ref, q_ref, do_ref, dl_ref, dz_ref, dot_ref, qt_ref):
        sel = jnp.where((_lane((16, 128)) >> 6) == _row((16, 128)), 1.0, 0.0).astype(BF16)
        for p, z_ref in enumerate((z0_ref, z0_ref, z1_ref, z1_ref)):
            sl = slice(128 * p, 128 * p + 128)
            sz, dsz = _silu_and_grad(z_ref[:, 128 * (p % 2):128 * (p % 2) + 128])
            dyv, ov = dy_ref[:, sl], o_ref[:, sl]
            do = dyv * sz
            do_ref[:, sl] = do.astype(BF16)
            dot_ref[p, 0] = do.T.astype(BF16)
            dz_ref[:, sl] = (dyv * ov * dsz).astype(BF16)
            hi, mid, lo = _split3(do * ov)
            dl_ref[p, 0] = (_dot_nt(sel, hi) + _dot_nt(sel, mid) + _dot_nt(sel, lo))[0:8, :]
        for h in range(C_HEADS):
            qt_ref[h, 0] = q_ref[:, 128 * h:128 * h + 128].astype(F32).T.astype(BF16)

    w = 256
    blk = pl.BlockSpec((tq, C_WIDTH), lambda i: (i, 0))
    return pl.pallas_call(
        body, name=f"fox_bwd_prep_{tag}", grid=(nq,),
        in_specs=[pl.BlockSpec((tq, w), lambda i: (i, COL_CZ // w)), pl.BlockSpec((tq, w), lambda i: (i, COL_CZ // w + 1)),
                  pl.BlockSpec((tq, C_WIDTH), lambda i: (i, (A_WIDTH + B_WIDTH) // C_WIDTH)), blk,
                  pl.BlockSpec((tq, C_HEADS * 128), lambda i: (i, 0))],
        out_specs=[blk, pl.BlockSpec((C_HEADS // 2, 1, 8, tq), lambda i: (0, i, 0, 0)), blk,
                   pl.BlockSpec((C_HEADS // 2, 1, 128, tq), lambda i: (0, i, 0, 0)),
                   pl.BlockSpec((C_HEADS, 1, 128, tq), lambda i: (0, i, 0, 0))],
        out_shape=[SDS((T, C_WIDTH), BF16), SDS((C_HEADS // 2, nq, 8, tq), F32), SDS((T, C_WIDTH), BF16),
                   SDS((C_HEADS // 2, nq, 128, tq), BF16), SDS((C_HEADS, nq, 128, tq), BF16)],
        compiler_params=_params("parallel"),
    )(proj, proj, dy, o, qt)


def _fox_bwd(qt, kt, proj, do, lse, delta, dot, qtr, tag, ride=None):
    T = proj.shape[0]
    tq, tk = _tile(T, FOX_TILE), _tile(T, FOX_KEYS)
    nq, nk = T // tq, T // tk
    assert tq == tk
    npair = C_HEADS // 2

    def body(q_ref, k_ref, v_ref, do_ref, lse_ref, dl_ref, dot_ref, qtr_ref, *rest):
        ride_srcs, (dq_ref, dk_ref, dv_ref), ride_dsts, scratch, ride_sems = _ride_refs(ride, rest, 3, 5)
        dvt_ref, dkt_ref, sc_ref, pt_ref, ds_ref = scratch
        j = pl.program_id(1)
        first = (j * tk) // tq
        _ride_start(ride, (npair, nk), ride_srcs, ride_dsts, ride_sems)

        @pl.when(j == 0)
        def _():
            dq_ref[...] = jnp.zeros_like(dq_ref)

        dkt_ref[...] = jnp.zeros_like(dkt_ref)
        dvt_ref[...] = jnp.zeros_like(dvt_ref)
        ks = (k_ref[:, 0:128], k_ref[:, 128:256])
        kts = tuple(k.astype(F32).T.astype(BF16) for k in ks)
        vb = v_ref[...].astype(BF16)
        lo = _lane((tq, 128)) < 64

        def operands(i):
            q0 = pl.multiple_of(i * tq, tq)
            qb = q_ref[pl.ds(q0, tq), :]
            dob = do_ref[pl.ds(q0, tq), :]
            qhs = (qb[:, 0:128], qb[:, 128:256])
            dohs = (jnp.where(lo, dob, jnp.zeros_like(dob)), jnp.where(lo, jnp.zeros_like(dob), dob))
            return qhs, dohs

        def scores(i):
            qhs, dohs = operands(i)
            return tuple(_dot_nt(ks[h], qhs[h]) for h in range(2)) + tuple(_dot_nt(vb, dohs[h]) for h in range(2))

        def park(sc, slot):
            for a, s in enumerate(sc):
                sc_ref[slot, a] = s

        def grads(i):
            for h in range(2):
                rows = slice(64 * h, 64 * h + 64)
                dvt_ref[rows, :] += _dot_nt(dot_ref[0, i, rows, :], pt_ref[h])
                dkt_ref[h] += _dot_nt(qtr_ref[h, i], ds_ref[h])
                dq_ref[h, i] += _dot(kts[h], ds_ref[h])

        def block(i, slot, diagonal, opening):
            park(scores(jnp.minimum(i + 1, nq - 1)), 1 - slot)
            if not opening:
                grads(i - 1)
            lsev = lse_ref[0, i]
            dlv = dl_ref[0, i]
            for h in range(2):
                pt = jnp.exp(sc_ref[slot, h] - lsev[h:h + 1, :])
                if diagonal:
                    pt = jnp.where(_fox_mask(tk, tq, j * tk, i * tq), pt, 0.0)
                ds_ref[h] = (pt * (sc_ref[slot, 2 + h] - dlv[h:h + 1, :])).astype(BF16)
                pt_ref[h] = pt.astype(BF16)

        park(scores(first), 0)
        block(first, 0, True, True)
        rest = nq - 1 - first

        def two_steps(t, carry):
            block(first + 1 + 2 * t, 1, False, False)
            block(first + 2 + 2 * t, 0, False, False)
            return carry

        lax.fori_loop(0, rest // 2, two_steps, 0)
        pl.when(rest % 2 == 1)(lambda: block(nq - 1, 1, False, False))
        grads(nq - 1)
        dv_ref[...] = dvt_ref[...].T.astype(BF16)
        for h in range(2):
            dk_ref[:, 128 * h:128 * h + 128] = dkt_ref[h].T
        _ride_wait(ride, (npair, nk), ride_srcs, ride_dsts, ride_sems)

    full = lambda w: pl.BlockSpec((T, w), lambda p, j: (0, p))
    stat = pl.BlockSpec((1, nq, 8, tq), lambda p, j: (p, 0, 0, 0))
    extra = ride or _ChipExchange("gather", ())
    return pl.pallas_call(
        body, name=f"fox_bwd_{tag}", grid=(npair, nk),
        in_specs=[full(256), pl.BlockSpec((tk, 256), lambda p, j: (j, p)),
                  pl.BlockSpec((tk, 128), lambda p, j: (j, COL_CV // 128 + p)), full(128), stat, stat,
                  pl.BlockSpec((1, nq, 128, tq), lambda p, j: (p, 0, 0, 0)),
                  pl.BlockSpec((2, nq, 128, tq), lambda p, j: (p, 0, 0, 0))] + extra.in_specs,
        out_specs=[pl.BlockSpec((2, nq, 128, tq), lambda p, j: (p, 0, 0, 0)), pl.BlockSpec((tk, 256), lambda p, j: (j, p)),
                   pl.BlockSpec((tk, 128), lambda p, j: (j, p))] + extra.out_specs,
        out_shape=[SDS((C_HEADS, nq, 128, tq), F32), SDS((T, C_HEADS * 128), F32), SDS((T, C_WIDTH), BF16)]
        + extra.out_shape,
        scratch_shapes=[pltpu.VMEM((128, tk), F32), pltpu.VMEM((2, 128, tk), F32), pltpu.VMEM((2, 4, tk, tq), F32),
                        pltpu.VMEM((2, tk, tq), BF16), pltpu.VMEM((2, tk, tq), BF16)] + (extra.scratch if ride else []),
        compiler_params=pltpu.CompilerParams(dimension_semantics=("arbitrary", "arbitrary"), vmem_limit_bytes=VMEM_LIMIT,
                                             has_side_effects=bool(ride)),
    )(qt, kt, proj, do, lse, delta, dot, qtr, *extra.sources)


def _fox_bwd_post(dqt, dkt, proj, bf, tag):
    T = proj.shape[0]
    tq = _tile(T, FOX_TILE)
    n = T // tq

    def body(dq_ref, dk_ref, fl_ref, bf_ref, oq_ref, ok_ref, ofl_ref, dbf_ref, carry_ref):
        @pl.when(pl.program_id(0) == 0)
        def _():
            carry_ref[...] = jnp.zeros_like(carry_ref)
            dbf_ref[...] = jnp.zeros_like(dbf_ref)

        lane = _lane((tq, 128))
        lo = lane < 64
        dqs = [dq_ref[h, 0].T for h in range(C_HEADS)]
        dc = jnp.zeros((tq, 128), F32)
        for h in range(C_HEADS):
            dc = dc + jnp.where(lane == h, dqs[h][:, 64:65] - dk_ref[:, 128 * h + 67:128 * h + 68], 0.0)
        utri = jnp.where(_lane((tq, tq)) >= _row((tq, tq)), 1.0, 0.0).astype(BF16)
        dlf = _dot3_left(utri, dc) + carry_ref[...]
        carry_ref[...] = dlf[0:1, :]
        dfl = jnp.where(lane < C_HEADS, dlf * _sigmoid(-(fl_ref[...] + bf_ref[...])), 0.0)
        ofl_ref[...] = dfl.astype(BF16)
        dbf_ref[...] += jnp.sum(dfl, axis=0, keepdims=True)
        for p in range(C_HEADS // 2):
            a, b = 128 * (2 * p), 128 * (2 * p + 1)
            oq_ref[:, 128 * p:128 * p + 128] = (
                jnp.where(lo, dqs[2 * p], pltpu.roll(dqs[2 * p + 1], 64, axis=1)) * Q_SCALE).astype(BF16)
            ok_ref[:, 128 * p:128 * p + 128] = jnp.where(
                lo, dk_ref[:, a:a + 128], pltpu.roll(dk_ref[:, b:b + 128], 64, axis=1)).astype(BF16)

    rev = lambda w: pl.BlockSpec((tq, w), lambda i: (n - 1 - i, 0))
    return pl.pallas_call(
        body, name=f"fox_bwd_post_{tag}", grid=(n,),
        in_specs=[pl.BlockSpec((C_HEADS, 1, 128, tq), lambda i: (0, n - 1 - i, 0, 0)), rev(C_HEADS * 128),
                  pl.BlockSpec((tq, 128), lambda i: (n - 1 - i, COL_CF // 128)), pl.BlockSpec((1, 128), lambda i: (0, 0))],
        out_specs=[rev(C_WIDTH), rev(C_WIDTH), rev(128), pl.BlockSpec((1, 128), lambda i: (0, 0))],
        out_shape=[SDS((T, C_WIDTH), BF16), SDS((T, C_WIDTH), BF16), SDS((T, 128), BF16), SDS((1, 128), F32)],
        scratch_shapes=[pltpu.VMEM((1, 128), F32)], compiler_params=_params("arbitrary"),
    )(dqt, dkt, proj, bf)


def _adamw_math(w, g, m, v):
    m = ADAM_B1 * m + (1.0 - ADAM_B1) * g
    v = ADAM_B2 * v + (1.0 - ADAM_B2) * (g * g)
    delta = -ADAM_LR * ((m / ADAM_C1) / (jnp.sqrt(v / ADAM_C2) + ADAM_EPS) + ADAM_WD * w)
    return delta, m, v


def _adamw_pair(w, m, v, ga, gb, name):
    n0 = w.shape[0]
    most = max(1, ADAMW_BLOCK_BYTES // (4 * math.prod(w.shape[1:])))
    t0 = max(t for t in range(1, min(n0, most) + 1) if n0 % t == 0)

    def body(w_ref, m_ref, v_ref, ga_ref, gb_ref, g_ref, d_ref, nm_ref, nv_ref):
        g = ga_ref[...] + gb_ref[...]
        g_ref[...] = g
        d_ref[...], nm_ref[...], nv_ref[...] = _adamw_math(w_ref[...], g, m_ref[...], v_ref[...])

    blk = pl.BlockSpec((t0,) + w.shape[1:], lambda i: (i, 0, 0))
    return pl.pallas_call(
        body, name=name, grid=(n0 // t0,), in_specs=[blk] * 5, out_specs=[blk] * 4,
        out_shape=[SDS(w.shape, F32)] * 4, compiler_params=_params("parallel"),
    )(w, m, v, ga, gb)


def _adamw_small(ws, ms, vs, gall):
    offs = _small_offsets()
    n = len(ws)

    def body(*refs):
        w_refs, m_refs, v_refs, g_ref = refs[:n], refs[n:2 * n], refs[2 * n:3 * n], refs[3 * n]
        outs = refs[3 * n + 1:]

        def total(off, rows):
            g = g_ref[0, off:off + rows, :]
            for dev in range(1, N_DEV):
                g = g + g_ref[dev, off:off + rows, :]
            return g

        for k in range(n):
            g = total(offs[k], ws[k].shape[0])
            go_ref, d_ref, nm_ref, nv_ref = outs[4 * k:4 * k + 4]
            go_ref[...] = g
            d_ref[...], nm_ref[...], nv_ref[...] = _adamw_math(w_refs[k][...], g, m_refs[k][...], v_refs[k][...])
        outs[4 * n][...] = total(offs[n], 1)

    shapes = [SDS(w.shape, F32) for w in ws for _ in range(4)] + [SDS((1, 128), F32)]
    res = pl.pallas_call(body, name="adamw_small", out_shape=shapes,
                         compiler_params=pltpu.CompilerParams(vmem_limit_bytes=VMEM_LIMIT))(*ws, *ms, *vs, gall)
    return [res[4 * k:4 * k + 4] for k in range(n)], res[4 * n]


def _pack_grads(dlng, dlnb, dwm, dbst, dlb, donorm, dbf, dfinal, loss_part):
    offs = _small_offsets()
    base = offs[1]
    L = len(dwm)
    assert L == 2

    def body(*refs):
        lng, lnb, wm, bst, on, bf = (refs[L * a:L * a + L] for a in range(6))
        lb_ref, fin_ref, loss_ref, o_ref = refs[6 * L:]
        o_ref[...] = jnp.zeros_like(o_ref)
        lane = _lane((1, 128))
        for l in range(L):
            for j in range(2):
                o_ref[offs[1] - base + 2 * l + j:offs[1] - base + 2 * l + j + 1, :] = lng[l][:, 128 * j:128 * j + 128]
                o_ref[offs[2] - base + 2 * l + j:offs[2] - base + 2 * l + j + 1, :] = lnb[l][:, 128 * j:128 * j + 128]
                o_ref[offs[5] - base + 2 * l + j:offs[5] - base + 2 * l + j + 1, :] = lb_ref[l:l + 1, 128 * j:128 * j + 128]
            for g in range(A_GROUPS):
                row = offs[3] - base + (A_GROUPS * l + g) * CHUNK
                o_ref[row:row + CHUNK, :] = wm[l][g]
            o_ref[offs[4] - base + A_GROUPS * l:offs[4] - base + A_GROUPS * (l + 1), :] = bst[l][...].T[0:A_GROUPS, :]
        o_ref[offs[6] - base:offs[6] - base + 1, :] = jnp.where(lane < 64, on[0][...], pltpu.roll(on[1][...], 64, axis=1))
        o_ref[offs[7] - base:offs[7] - base + 1, :] = jnp.where(
            lane < C_HEADS, bf[0][...], jnp.where(lane < 2 * C_HEADS, pltpu.roll(bf[1][...], C_HEADS, axis=1), 0.0))
        for j in range(D_MODEL // 128):
            o_ref[offs[8] - base + j:offs[8] - base + j + 1, :] = fin_ref[:, 128 * j:128 * j + 128]
        o_ref[offs[9] - base:offs[9] - base + 1, :] = loss_ref[...]

    rows = offs[9] + 8 - base
    return pl.pallas_call(body, name="pack_grads", out_shape=SDS((rows, 128), F32))(
        *dlng, *dlnb, *dwm, *dbst, *donorm, *dbf, dlb, dfinal, loss_part)


def _sum_chips(layers, name, layer_major):
    _, R, C = layers[0].shape
    L = len(layers)
    tc = _tile(C, 256)

    def body(*refs):
        o_ref = refs[-1]
        for l, p_ref in enumerate(refs[:-1]):
            p = [p_ref[k].astype(F32) for k in range(N_CHIPS)]
            s = ((p[0] + p[1]) + p[2]) + p[3]
            if layer_major:
                o_ref[l] = s
            else:
                o_ref[:, l, :] = s

    out = (L, R, C) if layer_major else (R, L, C)
    out_blk = (L, R, tc) if layer_major else (R, L, tc)
    return pl.pallas_call(
        body, name=name, grid=(C // tc,),
        in_specs=[pl.BlockSpec((N_CHIPS, R, tc), lambda i: (0, 0, i))] * L,
        out_specs=pl.BlockSpec(out_blk, lambda i: (0, 0, i)), out_shape=SDS(out, F32),
        compiler_params=_params("parallel"),
    )(*layers)


ANY = pl.BlockSpec(memory_space=pl.ANY)


def _mesh_pos():
    return lax.axis_index("x"), lax.axis_index("y"), lax.axis_index("c")


def _other_chips(x, y):
    return [(1 - x, y), (x, 1 - y), (1 - x, 1 - y)]


class _ChipExchange:
    def __init__(self, mode, sources):
        assert mode in ("gather", "scatter")
        self.mode, self.sources = mode, tuple(sources)
        self.n = len(self.sources)
        self.in_specs = [ANY] * self.n
        self.out_specs = [ANY] * self.n
        self.out_shape = [SDS(((N_CHIPS,) + s.shape) if mode == "gather" else s.shape, s.dtype) for s in self.sources]
        self.scratch = [pltpu.SemaphoreType.DMA((3 * self.n,)), pltpu.SemaphoreType.DMA((3 * self.n,)),
                        pltpu.SemaphoreType.DMA((self.n,))]

    def _copies(self, srcs, dsts, send_sems, recv_sems, local_sems):
        x, y, c = _mesh_pos()
        me = 2 * x + y
        view = (lambda r, chip: r) if self.mode == "gather" else (lambda r, chip: r.at[chip])
        local = [pltpu.make_async_copy(view(s, me), d.at[me], local_sems.at[a]) for a, (s, d) in enumerate(zip(srcs, dsts))]
        sends, recvs = [], []
        for j, (px, py) in enumerate(_other_chips(x, y)):
            peer = 2 * px + py
            for a, (s, d) in enumerate(zip(srcs, dsts)):
                sems = dict(send_sem=send_sems.at[self.n * j + a], recv_sem=recv_sems.at[self.n * j + a],
                            device_id=(px, py, c), device_id_type=MESH_ID)
                sends.append(pltpu.make_async_remote_copy(src_ref=view(s, peer), dst_ref=d.at[me], **sems))
                recvs.append(pltpu.make_async_remote_copy(src_ref=view(s, me), dst_ref=d.at[peer], **sems))
        return local, sends, recvs

    def start(self, srcs, dsts, sems):
        local, sends, _ = self._copies(srcs, dsts, *sems)
        for cp in local + sends:
            cp.start()

    def wait(self, srcs, dsts, sems):
        local, sends, recvs = self._copies(srcs, dsts, *sems)
        for cp in recvs:
            cp.wait_recv()
        for cp in sends:
            cp.wait_send()
        for cp in local:
            cp.wait()


def _gather_halves(w, tag):
    R, C = w.shape
    H = R // 2

    def body(w_ref, g_ref, send_sems, recv_sems, pass_send, pass_recv, local_sem):
        x, y, c = _mesh_pos()
        me = 2 * x + y
        mine, theirs = pl.ds(c * H, H), pl.ds((1 - c) * H, H)
        own = pltpu.make_async_copy(w_ref, g_ref.at[me], local_sem)
        own.start()

        def fetch(j, px, py, src, dst):
            return pltpu.make_async_remote_copy(src_ref=src, dst_ref=dst, send_sem=send_sems.at[j], recv_sem=recv_sems.at[j],
                                                device_id=(px, py, c), device_id_type=MESH_ID)

        def hand(j, rows, peer):
            return pltpu.make_async_remote_copy(src_ref=g_ref.at[peer, rows], dst_ref=g_ref.at[peer, rows],
                                                send_sem=pass_send.at[j], recv_sem=pass_recv.at[j],
                                                device_id=(x, y, 1 - c), device_id_type=MESH_ID)

        chips = _other_chips(x, y)
        sends = [fetch(j, px, py, w_ref.at[mine], g_ref.at[me, mine]) for j, (px, py) in enumerate(chips)]
        for cp in sends:
            cp.start()
        passed = []
        for j, (px, py) in enumerate(chips):
            peer = 2 * px + py
            fetch(j, px, py, w_ref.at[mine], g_ref.at[peer, mine]).wait_recv()
            passed.append(hand(j, mine, peer))
            passed[-1].start()
        for j, (px, py) in enumerate(chips):
            hand(j, theirs, 2 * px + py).wait_recv()
        for cp in sends + passed:
            cp.wait_send()
        own.wait()

    return pl.pallas_call(
        body, name=f"gather_halves_{tag}", in_specs=[ANY], out_specs=ANY, out_shape=SDS((N_CHIPS, R, C), w.dtype),
        scratch_shapes=[pltpu.SemaphoreType.DMA((3,)), pltpu.SemaphoreType.DMA((3,)), pltpu.SemaphoreType.DMA((3,)),
                        pltpu.SemaphoreType.DMA((3,)), pltpu.SemaphoreType.DMA],
        compiler_params=pltpu.CompilerParams(has_side_effects=True),
    )(w)


class _DeviceGather:
    def __init__(self, source):
        self.sources, self.n = (source,), 1
        self.in_specs, self.out_specs = [ANY], [ANY]
        self.out_shape = [SDS((N_DEV,) + source.shape, source.dtype)]
        self.scratch = [pltpu.SemaphoreType.DMA((N_DEV - 1,)), pltpu.SemaphoreType.DMA((N_DEV - 1,)),
                        pltpu.SemaphoreType.DMA((1,))]

    def _copies(self, srcs, dsts, send_sems, recv_sems, local_sems):
        (src,), (dst,) = srcs, dsts
        x, y, c = _mesh_pos()
        me = 4 * x + 2 * y + c
        local = [pltpu.make_async_copy(src, dst.at[me], local_sems.at[0])]
        sends, recvs = [], []
        for k in range(1, N_DEV):
            px, py, pc = (1 - x) if k & 4 else x, (1 - y) if k & 2 else y, (1 - c) if k & 1 else c
            sems = dict(send_sem=send_sems.at[k - 1], recv_sem=recv_sems.at[k - 1], device_id=(px, py, pc),
                        device_id_type=MESH_ID)
            sends.append(pltpu.make_async_remote_copy(src_ref=src, dst_ref=dst.at[me], **sems))
            recvs.append(pltpu.make_async_remote_copy(src_ref=src, dst_ref=dst.at[4 * px + 2 * py + pc], **sems))
        return local, sends, recvs

    start = _ChipExchange.start
    wait = _ChipExchange.wait


def _gather_devices(a, name):
    ex = _DeviceGather(a)

    def body(a_ref, g_ref, *sems):
        ex.start((a_ref,), (g_ref,), sems)
        ex.wait((a_ref,), (g_ref,), sems)

    return pl.pallas_call(
        body, name=name, in_specs=ex.in_specs, out_specs=ex.out_specs[0], out_shape=ex.out_shape[0],
        scratch_shapes=ex.scratch, compiler_params=pltpu.CompilerParams(has_side_effects=True),
    )(a)


def _swap_cores(pin, pout):
    def body(pin_ref, pout_ref, oin_ref, oout_ref, send_sems, recv_sems):
        x, y, c = _mesh_pos()
        cps = [pltpu.make_async_remote_copy(src_ref=src, dst_ref=dst, send_sem=send_sems.at[a], recv_sem=recv_sems.at[a],
                                            device_id=(x, y, 1 - c), device_id_type=MESH_ID)
               for a, (src, dst) in enumerate(((pin_ref, oin_ref), (pout_ref, oout_ref)))]
        for cp in cps:
            cp.start()
        for cp in cps:
            cp.wait()

    return pl.pallas_call(
        body, name="swap_cores", in_specs=[ANY, ANY], out_specs=[ANY, ANY],
        out_shape=[SDS(pin.shape, F32), SDS(pout.shape, F32)],
        scratch_shapes=[pltpu.SemaphoreType.DMA((2,)), pltpu.SemaphoreType.DMA((2,))],
        compiler_params=pltpu.CompilerParams(has_side_effects=True),
    )(pin, pout)


PACK_TILE = 8 * 128


def _pack_rows(size):
    return (size + PACK_TILE - 1) // PACK_TILE * 8


def _small_offsets():
    offs = [0]
    for _, shape in SMALL_PARAMS:
        offs.append(offs[-1] + _pack_rows(math.prod(shape)))
    return offs


def _rows_view(a):
    flat = a.reshape(-1)
    return jnp.pad(flat, (0, (-flat.size) % 128)).reshape(-1, 128)


def _from_rows(rows, shape):
    return rows.reshape(-1)[:math.prod(shape)].reshape(shape)


def _layer_consts(l, gmlp_ln_g, gmlp_ln_b, gmlp_w_s, gmlp_b_s, hgrn_onorm_g, fox_b_f):
    causal = jnp.tril(jnp.ones((CHUNK, CHUNK), bool))
    wm = jnp.where(causal[None], gmlp_w_s[l], 0.0)
    return dict(
        lng=gmlp_ln_g[l].reshape(1, A_WIDTH), lnb=gmlp_ln_b[l].reshape(1, A_WIDTH),
        wm=wm.astype(BF16), wmt=jnp.swapaxes(wm, 1, 2).astype(BF16),
        bst=jnp.pad(gmlp_b_s[l].T, ((0, 0), (0, 128 - A_GROUPS))),
        onorm=jnp.tile(hgrn_onorm_g[l], 4).reshape(1, B_WIDTH),
        bf=jnp.pad(fox_b_f[l], (0, 128 - C_HEADS)).reshape(1, 128),
    )


def kernel(x, norm_g, w_in, w_out, gmlp_ln_g, gmlp_ln_b, gmlp_w_s, gmlp_b_s, hgrn_lb, hgrn_onorm_g, fox_b_f, final_norm_g, loss_target, m_norm_g, m_w_in, m_w_out, m_gmlp_ln_g, m_gmlp_ln_b, m_gmlp_w_s, m_gmlp_b_s, m_hgrn_lb, m_hgrn_onorm_g, m_fox_b_f, m_final_norm_g, v_norm_g, v_w_in, v_w_out, v_gmlp_ln_g, v_gmlp_ln_b, v_gmlp_w_s, v_gmlp_b_s, v_hgrn_lb, v_hgrn_onorm_g, v_fox_b_f, v_final_norm_g):
    T = x.shape[1]
    shard_in = w_in.shape[2]
    shard_out = w_out.shape[1]
    xs = x.reshape(T, D_MODEL)
    tgt = loss_target.reshape(T, D_MODEL)

    w_in_b, w_out_b = w_in.astype(BF16), w_out.astype(BF16)

    def full_w_in(gathered):
        return jnp.concatenate([gathered[k] for k in range(N_CHIPS)] + [jnp.zeros((D_MODEL, D_IN_PAD - D_IN), BF16)], axis=-1)

    lb_all = _lb_fwd(hgrn_lb)
    consts = [_layer_consts(l, gmlp_ln_g, gmlp_ln_b, gmlp_w_s, gmlp_b_s, hgrn_onorm_g, fox_b_f) for l in range(DEPTH)]

    saved = []
    xl = xs
    w_in_l = full_w_in(_gather_halves(w_in_b[0], "w_in_l0"))
    for l in range(DEPTH):
        cs = consts[l]
        tag = f"l{l}"
        h, proj = _inproj(xl, norm_g[l].reshape(1, D_MODEL), w_in_l, tag)
        (ya,), (yb, ob, s0), (qt, kt, vt) = _run_parts(
            [_gmlp_fwd(proj, cs["lng"], cs["lnb"], cs["wm"], cs["bst"]),
             _hgrn_fwd(proj, lb_all[l].reshape(1, B_WIDTH), cs["onorm"]), _fox_prep(proj, cs["bf"])],
            (T // CHUNK,), f"mix_fwd_{tag}")
        ride = _ChipExchange("gather", (w_out_b[l],) + ((w_in_b[l + 1],) if l + 1 < DEPTH else ()))
        oc, lse, yc, *gathered = _fox_fwd(qt, kt, vt, proj, tag, ride)
        w_out_l = gathered[0].reshape(N_CHIPS * shard_out, D_MODEL)
        saved.append(dict(x=xl, h=h, proj=proj, ya=ya, yb=yb, yc=yc, ob=ob, s0=s0, qt=qt, kt=kt, oc=oc, lse=lse,
                          w_in=w_in_l, w_out=w_out_l))
        xl = _outproj(xl, ya, yb, yc, w_out_l, tag)
        if l + 1 < DEPTH:
            w_in_l = full_w_in(gathered[1])

    dx, loss_part, d_final = _loss_head(xl, final_norm_g.reshape(1, D_MODEL), tgt)

    g_small = {}
    dlb_rows, rin, rout = [None] * DEPTH, [None] * DEPTH, [None] * DEPTH
    slabs_in = None
    for l in reversed(range(DEPTH)):
        cs, sv = consts[l], saved[l]
        tag = f"l{l}"
        proj = sv["proj"]
        dy, dw_out = _outproj_bwd(dx, sv["ya"], sv["yb"], sv["yc"], sv["w_out"], tag)
        (da, dwm, dbst, dlng, dlnb), (db, dlb_rows[l], donorm) = _run_parts(
            [_gmlp_bwd(proj, dy, cs["lng"], cs["lnb"], cs["wm"], cs["wmt"], cs["bst"]),
             _hgrn_bwd(proj, dy, sv["ob"], sv["s0"], lb_all[l].reshape(1, B_WIDTH), cs["onorm"])],
            (T // CHUNK,), f"mix_bwd_{tag}")
        do, delta, dzc, dot, qtr = _fox_bwd_prep(proj, dy, sv["oc"], sv["qt"], tag)
        slabs_out = dw_out.reshape(N_CHIPS, shard_out, D_MODEL).astype(BF16)
        ride = _ChipExchange("scatter", (slabs_out,) + ((slabs_in,) if slabs_in is not None else ()))
        dqt, dkt, dvc, *received = _fox_bwd(sv["qt"], sv["kt"], proj, do, sv["lse"], delta, dot, qtr, tag, ride)
        rout[l] = received[0]
        if slabs_in is not None:
            rin[l + 1] = received[1]
        dqc, dkc, dflc, dbf = _fox_bwd_post(dqt, dkt, proj, cs["bf"], tag)
        g_small[l] = dict(ln_g=dlng, ln_b=dlnb, w_s=dwm, b_s=dbst, onorm=donorm, bf=dbf)
        dproj = jnp.concatenate([da, db, dqc, dkc, dvc, dzc, dflc, jnp.zeros((T, 128), BF16)], axis=1)
        if l == 0:
            d_hgrn_lb = _lb_bwd(hgrn_lb, jnp.concatenate(dlb_rows, axis=0))
            per_layer = lambda key: [g_small[k][key] for k in range(DEPTH)]
            early = _pack_grads(per_layer("ln_g"), per_layer("ln_b"), per_layer("w_s"), per_layer("b_s"), d_hgrn_lb,
                                per_layer("onorm"), per_layer("bf"), d_final, loss_part)
            dw_in, rearly = _dw_in(sv["h"], dproj, tag, _DeviceGather(early))
        else:
            dw_in = _dw_in(sv["h"], dproj, tag)
        slabs_in = dw_in[:N_CHIPS * shard_in].reshape(N_CHIPS, shard_in, D_MODEL).astype(BF16)
        ride = _ChipExchange("scatter", (slabs_in,)) if l == 0 else None
        dx, dng, *received = _dx_in(sv["x"], norm_g[l].reshape(1, D_MODEL), dx, dproj, sv["w_in"], tag, ride)
        if l == 0:
            rin[0] = received[0]
        g_small[l]["norm_g"] = dng.reshape(D_MODEL // 128, 128)
    grad_x = dx.reshape(x.shape)
    rlate = _gather_devices(jnp.concatenate([g_small[l]["norm_g"] for l in range(DEPTH)]), "gather_norm_grads")
    rsmall = jnp.concatenate([rlate, rearly], axis=1)

    pin, pout = _sum_chips(rin, "sum_chips_w_in", False), _sum_chips(rout, "sum_chips_w_out", True)
    oin, oout = _swap_cores(pin, pout)
    to_view = lambda a: jnp.transpose(a, (2, 0, 1))
    g_w_in, d_w_in, nm_w_in, nv_w_in = [
        jnp.transpose(o, (1, 2, 0))
        for o in _adamw_pair(to_view(w_in), to_view(m_w_in), to_view(v_w_in), pin, oin, "adamw_w_in")]
    g_w_out, d_w_out, nm_w_out, nv_w_out = _adamw_pair(w_out, m_w_out, v_w_out, pout, oout, "adamw_w_out")

    small_w = [norm_g, gmlp_ln_g, gmlp_ln_b, gmlp_w_s, gmlp_b_s, hgrn_lb, hgrn_onorm_g, fox_b_f, final_norm_g]
    small_m = [m_norm_g, m_gmlp_ln_g, m_gmlp_ln_b, m_gmlp_w_s, m_gmlp_b_s, m_hgrn_lb, m_hgrn_onorm_g, m_fox_b_f, m_final_norm_g]
    small_v = [v_norm_g, v_gmlp_ln_g, v_gmlp_ln_b, v_gmlp_w_s, v_gmlp_b_s, v_hgrn_lb, v_hgrn_onorm_g, v_fox_b_f, v_final_norm_g]
    views = lambda ps: [_rows_view(p) for p in ps]
    per_param, loss_row = _adamw_small(views(small_w), views(small_m), views(small_v), rsmall)
    sg, sd, sm, sv_ = [[_from_rows(per_param[k][a], shape) for k, (_, shape) in enumerate(SMALL_PARAMS)] for a in range(4)]
    loss = loss_row[0, 0]

    def order(big_in, big_out, small):
        return [small[0], big_in, big_out] + small[1:]

    return (loss, grad_x, *order(g_w_in, g_w_out, sg), *order(d_w_in, d_w_out, sd), *order(nm_w_in, nm_w_out, sm),
            *order(nv_w_in, nv_w_out, sv_))
```

```python
import collections
import functools
import math

import jax
import jax.numpy as jnp
from jax import lax
from jax.experimental import pallas as pl
from jax.experimental.pallas import tpu as pltpu

F32 = jnp.float32
BF16 = jnp.bfloat16
SDS = jax.ShapeDtypeStruct
MESH_ID = pl.DeviceIdType.MESH

D_MODEL = 1024
DEPTH = 2
A_WIDTH = 256
A_GROUPS = 4
B_WIDTH = 256
C_WIDTH = 512
C_HEADS = 8
D_IN = 3848
D_IN_PAD = 4096
CHUNK = 128
SUB = 16
SUB_SHIFT = 4
NORM_EPS = 1e-6
F_FLOOR = 1e-30
COL_AU, COL_AV, COL_AZ = 0, 256, 512
COL_BQ, COL_BF, COL_BI, COL_BZ = 768, 1024, 1280, 1536
COL_CQ, COL_CK, COL_CV, COL_CZ, COL_CF = 1792, 2304, 2816, 3328, 3840
HEAD_LANES = 128
Q_SCALE = 0.125
ADAM_LR, ADAM_B1, ADAM_B2, ADAM_EPS, ADAM_WD, ADAM_STEP = 0.001, 0.9, 0.999, 1e-08, 0.01, 10
ADAM_C1 = 1.0 - ADAM_B1 ** ADAM_STEP
ADAM_C2 = 1.0 - ADAM_B2 ** ADAM_STEP
VMEM_LIMIT = 56 * 1024 * 1024
ADAMW_BLOCK_BYTES = 1 << 20
N_CHIPS = 4
N_DEV = 8

SMALL_PARAMS = (
    ("norm_g", (DEPTH, D_MODEL)), ("gmlp_ln_g", (DEPTH, 4, 64)), ("gmlp_ln_b", (DEPTH, 4, 64)),
    ("gmlp_w_s", (DEPTH, 4, 128, 128)), ("gmlp_b_s", (DEPTH, 4, 128)), ("hgrn_lb", (DEPTH, 256)),
    ("hgrn_onorm_g", (DEPTH, 64)), ("fox_b_f", (DEPTH, 8)), ("final_norm_g", (D_MODEL,)),
)


def _tile(n, pref):
    t = min(n, pref)
    assert n % t == 0, (n, pref)
    return t


def _params(*sem):
    return pltpu.CompilerParams(dimension_semantics=sem, vmem_limit_bytes=VMEM_LIMIT)


_Part = collections.namedtuple("_Part", "body operands in_specs out_specs out_shape scratch")


def _run_parts(parts, grid, name):
    counts = [(len(p.operands), len(p.out_shape), len(p.scratch)) for p in parts]

    def body(*refs):
        ins, outs, scr = [], [], []
        pos = 0
        for group, k in ((ins, 0), (outs, 1), (scr, 2)):
            for c in counts:
                group.append(refs[pos:pos + c[k]])
                pos += c[k]
        for p, i, o, s in zip(parts, ins, outs, scr):
            p.body(*i, *o, *s)

    flat = lambda key: [x for p in parts for x in getattr(p, key)]
    res = pl.pallas_call(
        body, name=name, grid=grid, in_specs=flat("in_specs"), out_specs=flat("out_specs"), out_shape=flat("out_shape"),
        scratch_shapes=flat("scratch"), compiler_params=_params(*(("arbitrary",) * len(grid))),
    )(*flat("operands"))
    out, pos = [], 0
    for c in counts:
        out.append(list(res[pos:pos + c[1]]))
        pos += c[1]
    return out


def _dot(a, b):
    return jnp.dot(a, b, preferred_element_type=F32)


def _dot_nt(a, b):
    return lax.dot_general(a, b, (((1,), (1,)), ((), ())), preferred_element_type=F32)


def _dot_tn(a, b):
    return lax.dot_general(a, b, (((0,), (0,)), ((), ())), preferred_element_type=F32)


def _split3(x):
    hi = x.astype(BF16)
    r = x - hi.astype(F32)
    mid = r.astype(BF16)
    lo = (r - mid.astype(F32)).astype(BF16)
    return hi, mid, lo


def _dot3_left(c, x):
    hi, mid, lo = _split3(x)
    return _dot(c, hi) + _dot(c, mid) + _dot(c, lo)


def _sigmoid(x):
    return jax.nn.sigmoid(x)


def _silu_and_grad(x):
    s = _sigmoid(x)
    return x * s, s * (1.0 + x * (1.0 - s))


_GELU_C = math.sqrt(2.0 / math.pi)


def _gelu_and_grad(x):
    inner = _GELU_C * (x + 0.044715 * x * x * x)
    t = jnp.tanh(inner)
    y = 0.5 * x * (1.0 + t)
    dy = 0.5 * (1.0 + t) + 0.5 * x * (1.0 - t * t) * _GELU_C * (1.0 + 3.0 * 0.044715 * x * x)
    return y, dy


def _lane(shape):
    return lax.broadcasted_iota(jnp.int32, shape, 1)


def _row(shape):
    return lax.broadcasted_iota(jnp.int32, shape, 0)


def _gsum64(x):
    lo = _lane(x.shape) < 64
    s0 = jnp.sum(jnp.where(lo, x, 0.0), axis=-1, keepdims=True)
    s1 = jnp.sum(jnp.where(lo, 0.0, x), axis=-1, keepdims=True)
    return jnp.where(lo, s0, s1)


def _colreduce(x, op):
    parts = [x[r:r + 8, :] for r in range(0, x.shape[0], 8)]
    while len(parts) > 1:
        pairs = [op(parts[k], parts[k + 1]) for k in range(0, len(parts) - 1, 2)]
        parts = pairs + ([parts[-1]] if len(parts) % 2 else [])
    red = jnp.max if op is jnp.maximum else jnp.sum
    return red(parts[0], axis=0, keepdims=True)


def _block_diag64(dtype=BF16):
    r, c = _row((128, 128)), _lane((128, 128))
    return jnp.where((r >> 6) == (c >> 6), 1.0, 0.0).astype(dtype)


def _inproj(x, g, w, tag):
    T, D = x.shape
    DP = w.shape[1]
    tm = _tile(T, 512)

    def body(x_ref, g_ref, w_ref, h_ref, p_ref):
        xv = x_ref[...]
        r = lax.rsqrt(jnp.mean(xv * xv, axis=-1, keepdims=True) + NORM_EPS)
        h = (xv * r * g_ref[...]).astype(BF16)
        h_ref[...] = h
        p_ref[...] = _dot(h, w_ref[...])

    return pl.pallas_call(
        body, name=f"inproj_{tag}", grid=(T // tm,),
        in_specs=[pl.BlockSpec((tm, D), lambda i: (i, 0)), pl.BlockSpec((1, D), lambda i: (0, 0)),
                  pl.BlockSpec((D, DP), lambda i: (0, 0))],
        out_specs=[pl.BlockSpec((tm, D), lambda i: (i, 0)), pl.BlockSpec((tm, DP), lambda i: (i, 0))],
        out_shape=[SDS((T, D), BF16), SDS((T, DP), F32)],
        compiler_params=_params("parallel"),
    )(x, g, w)


def _outproj(x, ya, yb, yc, wo, tag):
    T, D = x.shape
    tm = _tile(T, 512)

    def body(x_ref, ya_ref, yb_ref, yc_ref, wo_ref, o_ref):
        acc = x_ref[...] + _dot(ya_ref[...], wo_ref[0:A_WIDTH, :])
        acc = acc + _dot(yb_ref[...], wo_ref[A_WIDTH:A_WIDTH + B_WIDTH, :])
        o_ref[...] = acc + _dot(yc_ref[...], wo_ref[A_WIDTH + B_WIDTH:, :])

    row = lambda w: pl.BlockSpec((tm, w), lambda i: (i, 0))
    return pl.pallas_call(
        body, name=f"outproj_{tag}", grid=(T // tm,),
        in_specs=[row(D), row(A_WIDTH), row(B_WIDTH), row(C_WIDTH), pl.BlockSpec(wo.shape, lambda i: (0, 0))],
        out_specs=row(D), out_shape=SDS((T, D), F32), compiler_params=_params("parallel"),
    )(x, ya, yb, yc, wo)


def _outproj_bwd(dx, ya, yb, yc, wo, tag):
    T, D = dx.shape
    DM = wo.shape[0]
    tm = _tile(T, 512)

    def body(dx_ref, ya_ref, yb_ref, yc_ref, wo_ref, dy_ref, dwo_ref):
        @pl.when(pl.program_id(0) == 0)
        def _():
            dwo_ref[...] = jnp.zeros_like(dwo_ref)

        dxb = dx_ref[...].astype(BF16)
        dy_ref[...] = _dot_nt(dxb, wo_ref[...])
        dwo_ref[0:A_WIDTH, :] += _dot_tn(ya_ref[...], dxb)
        dwo_ref[A_WIDTH:A_WIDTH + B_WIDTH, :] += _dot_tn(yb_ref[...], dxb)
        dwo_ref[A_WIDTH + B_WIDTH:, :] += _dot_tn(yc_ref[...], dxb)

    row = lambda w: pl.BlockSpec((tm, w), lambda i: (i, 0))
    return pl.pallas_call(
        body, name=f"outproj_bwd_{tag}", grid=(T // tm,),
        in_specs=[row(D), row(A_WIDTH), row(B_WIDTH), row(C_WIDTH), pl.BlockSpec(wo.shape, lambda i: (0, 0))],
        out_specs=[row(DM), pl.BlockSpec((DM, D), lambda i: (0, 0))],
        out_shape=[SDS((T, DM), F32), SDS((DM, D), F32)], compiler_params=_params("arbitrary"),
    )(dx, ya, yb, yc, wo)


def _dw_in(h, dproj, tag, ride=None):
    T, D = h.shape
    DP = dproj.shape[1]
    tm, tn = _tile(T, 1024), _tile(DP, 1024)
    grid = (DP // tn, T // tm)

    def body(h_ref, dp_ref, *rest):
        ride_srcs, (dw_ref,), ride_dsts, _, ride_sems = _ride_refs(ride, rest, 1, 0)
        _ride_start(ride, grid, ride_srcs, ride_dsts, ride_sems)

        @pl.when(pl.program_id(1) == 0)
        def _():
            dw_ref[...] = jnp.zeros_like(dw_ref)

        dw_ref[...] += _dot_tn(dp_ref[...], h_ref[...])
        _ride_wait(ride, grid, ride_srcs, ride_dsts, ride_sems)

    extra = ride or _ChipExchange("gather", ())
    out = pl.pallas_call(
        body, name=f"dw_in_{tag}", grid=grid,
        in_specs=[pl.BlockSpec((tm, D), lambda j, i: (i, 0)), pl.BlockSpec((tm, tn), lambda j, i: (i, j))] + extra.in_specs,
        out_specs=[pl.BlockSpec((tn, D), lambda j, i: (j, 0))] + extra.out_specs,
        out_shape=[SDS((DP, D), F32)] + extra.out_shape, scratch_shapes=extra.scratch if ride else [],
        compiler_params=pltpu.CompilerParams(dimension_semantics=("arbitrary", "arbitrary"), vmem_limit_bytes=VMEM_LIMIT,
                                             has_side_effects=bool(ride)),
    )(h, dproj, *extra.sources)
    return out if ride else out[0]


def _dx_in(x, g, dres, dproj, w, tag, ride=None):
    T, D = x.shape
    DP = w.shape[1]
    tm = _tile(T, 512)
    grid = (T // tm,)

    def body(x_ref, g_ref, dres_ref, dp_ref, w_ref, *rest):
        ride_srcs, (dx_ref, dg_ref), ride_dsts, _, ride_sems = _ride_refs(ride, rest, 2, 0)
        _ride_start(ride, grid, ride_srcs, ride_dsts, ride_sems)

        @pl.when(pl.program_id(0) == 0)
        def _():
            dg_ref[...] = jnp.zeros_like(dg_ref)

        dh = _dot_nt(dp_ref[...], w_ref[...])
        xv = x_ref[...]
        r = lax.rsqrt(jnp.mean(xv * xv, axis=-1, keepdims=True) + NORM_EPS)
        xh = xv * r
        dg_ref[...] += jnp.sum(dh * xh, axis=0, keepdims=True)
        dxh = dh * g_ref[...]
        dx_ref[...] = dres_ref[...] + r * (dxh - xh * jnp.mean(dxh * xh, axis=-1, keepdims=True))
        _ride_wait(ride, grid, ride_srcs, ride_dsts, ride_sems)

    extra = ride or _ChipExchange("gather", ())
    row = pl.BlockSpec((tm, D), lambda i: (i, 0))
    return pl.pallas_call(
        body, name=f"dx_in_{tag}", grid=grid,
        in_specs=[row, pl.BlockSpec((1, D), lambda i: (0, 0)), row, pl.BlockSpec((tm, DP), lambda i: (i, 0)),
                  pl.BlockSpec((D, DP), lambda i: (0, 0))] + extra.in_specs,
        out_specs=[row, pl.BlockSpec((1, D), lambda i: (0, 0))] + extra.out_specs,
        out_shape=[SDS((T, D), F32), SDS((1, D), F32)] + extra.out_shape,
        scratch_shapes=extra.scratch if ride else [],
        compiler_params=pltpu.CompilerParams(dimension_semantics=("arbitrary",), vmem_limit_bytes=VMEM_LIMIT,
                                             has_side_effects=bool(ride)),
    )(x, g, dres, dproj, w, *extra.sources)


def _loss_head(x, g, tgt):
    T, D = x.shape
    tm = _tile(T, 512)

    def body(x_ref, g_ref, t_ref, dx_ref, loss_ref, dg_ref):
        @pl.when(pl.program_id(0) == 0)
        def _():
            loss_ref[...] = jnp.zeros_like(loss_ref)
            dg_ref[...] = jnp.zeros_like(dg_ref)

        xv = x_ref[...]
        r = lax.rsqrt(jnp.mean(xv * xv, axis=-1, keepdims=True) + NORM_EPS)
        xh = xv * r
        gv = g_ref[...]
        err = xh * gv - t_ref[...]
        tok = jnp.mean(err * err, axis=-1, keepdims=True)
        loss_ref[...] += 0.5 * jnp.sum(tok, axis=0, keepdims=True)
        dy = err * (1.0 / D)
        dg_ref[...] += jnp.sum(dy * xh, axis=0, keepdims=True)
        dxh = dy * gv
        dx_ref[...] = r * (dxh - xh * jnp.mean(dxh * xh, axis=-1, keepdims=True))

    row = pl.BlockSpec((tm, D), lambda i: (i, 0))
    return pl.pallas_call(
        body, name="loss_head", grid=(T // tm,),
        in_specs=[row, pl.BlockSpec((1, D), lambda i: (0, 0)), row],
        out_specs=[row, pl.BlockSpec((1, 128), lambda i: (0, 0)), pl.BlockSpec((1, D), lambda i: (0, 0))],
        out_shape=[SDS((T, D), F32), SDS((1, 128), F32), SDS((1, D), F32)], compiler_params=_params("arbitrary"),
    )(x, g, tgt)


def _gmlp_core(u, v, lng, lnb, wm_ref, bst_ref, pair):
    ug, dug = _gelu_and_grad(u)
    vg, dvg = _gelu_and_grad(v)
    mu = _gsum64(vg) * (1.0 / 64)
    d = vg - mu
    var = _gsum64(d * d) * (1.0 / 64)
    rstd = lax.rsqrt(var + NORM_EPS)
    xh = d * rstd
    vn = xh * lng + lnb
    vnb = vn.astype(BF16)
    lo = _lane(u.shape) < 64
    g0, g1 = 2 * pair, 2 * pair + 1
    mixed = jnp.where(lo, _dot(wm_ref[g0], vnb) + bst_ref[:, g0:g0 + 1], _dot(wm_ref[g1], vnb) + bst_ref[:, g1:g1 + 1])
    return ug, dug, dvg, rstd, xh, vnb, mixed, lo


def _gmlp_fwd(proj, lng, lnb, wm, bst):
    T = proj.shape[0]

    def body(u_ref, v_ref, z_ref, lng_ref, lnb_ref, wm_ref, bst_ref, y_ref):
        for pair in range(2):
            sl = slice(128 * pair, 128 * pair + 128)
            ug, _, _, _, _, _, mixed, _ = _gmlp_core(u_ref[:, sl], v_ref[:, sl], lng_ref[:, sl], lnb_ref[:, sl],
                                                     wm_ref, bst_ref, pair)
            sz, _ = _silu_and_grad(z_ref[:, sl])
            y_ref[:, sl] = (ug * mixed * sz).astype(BF16)

    col = lambda c: pl.BlockSpec((CHUNK, A_WIDTH), lambda i, c=c: (i, c // A_WIDTH))
    full = lambda a: pl.BlockSpec(a.shape, lambda i, n=a.ndim: (0,) * n)
    return _Part(body, (proj, proj, proj, lng, lnb, wm, bst),
                 [col(COL_AU), col(COL_AV), col(COL_AZ), full(lng), full(lnb), full(wm), full(bst)],
                 [pl.BlockSpec((CHUNK, A_WIDTH), lambda i: (i, 0))], [SDS((T, A_WIDTH), BF16)], [])


def _gmlp_bwd(proj, dy, lng, lnb, wm, wmt, bst):
    T = proj.shape[0]
    n = T // CHUNK

    def body(u_ref, v_ref, z_ref, dy_ref, lng_ref, lnb_ref, wm_ref, wmt_ref, bst_ref,
             da_ref, dwm_ref, dbst_ref, dlng_ref, dlnb_ref):
        @pl.when(pl.program_id(0) == 0)
        def _():
            dwm_ref[...] = jnp.zeros_like(dwm_ref)
            dbst_ref[...] = jnp.zeros_like(dbst_ref)
            dlng_ref[...] = jnp.zeros_like(dlng_ref)
            dlnb_ref[...] = jnp.zeros_like(dlnb_ref)

        lane = _lane((CHUNK, 128))
        dbst = dbst_ref[...]
        for pair in range(2):
            sl = slice(128 * pair, 128 * pair + 128)
            lng_p = lng_ref[:, sl]
            ug, dug, dvg, rstd, xh, vnb, mixed, lo = _gmlp_core(u_ref[:, sl], v_ref[:, sl], lng_p, lnb_ref[:, sl],
                                                                wm_ref, bst_ref, pair)
            sz, dsz = _silu_and_grad(z_ref[:, sl])
            dyv = dy_ref[:, sl]
            out = ug * mixed
            dz = dyv * out * dsz
            dout = dyv * sz
            du = dout * mixed * dug
            dmix = dout * ug
            g0, g1 = 2 * pair, 2 * pair + 1
            dm0 = jnp.where(lo, dmix, 0.0)
            dm1 = jnp.where(lo, 0.0, dmix)
            dbst = dbst + jnp.where(lane == g0, jnp.sum(dm0, axis=-1, keepdims=True), 0.0)
            dbst = dbst + jnp.where(lane == g1, jnp.sum(dm1, axis=-1, keepdims=True), 0.0)
            dwm_ref[g0] += _dot_nt(dm0.astype(BF16), vnb)
            dwm_ref[g1] += _dot_nt(dm1.astype(BF16), vnb)
            dmb = dmix.astype(BF16)
            dvn = jnp.where(lo, _dot(wmt_ref[g0], dmb), _dot(wmt_ref[g1], dmb))
            dlng_ref[:, sl] += jnp.sum(dvn * xh, axis=0, keepdims=True)
            dlnb_ref[:, sl] += jnp.sum(dvn, axis=0, keepdims=True)
            dxh = dvn * lng_p
            m1 = _gsum64(dxh) * (1.0 / 64)
            m2 = _gsum64(dxh * xh) * (1.0 / 64)
            dv = rstd * (dxh - m1 - xh * m2) * dvg
            da_ref[:, COL_AU + 128 * pair:COL_AU + 128 * pair + 128] = du.astype(BF16)
            da_ref[:, COL_AV + 128 * pair:COL_AV + 128 * pair + 128] = dv.astype(BF16)
            da_ref[:, COL_AZ + 128 * pair:COL_AZ + 128 * pair + 128] = dz.astype(BF16)
        dbst_ref[...] = dbst

        @pl.when(pl.program_id(0) == n - 1)
        def _():
            causal = _lane((CHUNK, CHUNK)) <= _row((CHUNK, CHUNK))
            for g in range(A_GROUPS):
                dwm_ref[g] = jnp.where(causal, dwm_ref[g], 0.0)

    col = lambda c: pl.BlockSpec((CHUNK, A_WIDTH), lambda i, c=c: (i, c // A_WIDTH))
    full = lambda a: pl.BlockSpec(a.shape, lambda i, n=a.ndim: (0,) * n)
    acc = lambda s: pl.BlockSpec(s, lambda i, n=len(s): (0,) * n)
    return _Part(body, (proj, proj, proj, dy, lng, lnb, wm, wmt, bst),
                 [col(COL_AU), col(COL_AV), col(COL_AZ), pl.BlockSpec((CHUNK, A_WIDTH), lambda i: (i, 0)),
                  full(lng), full(lnb), full(wm), full(wmt), full(bst)],
                 [pl.BlockSpec((CHUNK, 3 * A_WIDTH), lambda i: (i, 0)), acc((A_GROUPS, CHUNK, CHUNK)),
                  acc((CHUNK, 128)), acc((1, A_WIDTH)), acc((1, A_WIDTH))],
                 [SDS((T, 3 * A_WIDTH), BF16), SDS((A_GROUPS, CHUNK, CHUNK), F32), SDS((CHUNK, 128), F32),
                  SDS((1, A_WIDTH), F32), SDS((1, A_WIDTH), F32)], [])


def _hgrn_consts():
    r, c = _row((CHUNK, CHUNK)), _lane((CHUNK, CHUNK))
    same = (r >> SUB_SHIFT) == (c >> SUB_SHIFT)
    lsub = jnp.where(same & (c <= r), 1.0, 0.0).astype(BF16)
    usub = jnp.where(same & (c >= r), 1.0, 0.0).astype(BF16)
    bsub = jnp.where(same, 1.0, 0.0).astype(BF16)
    return lsub, usub, bsub


def _hgrn_gates(qv, zf, lbp):
    sq, dsq = _silu_and_grad(qv)
    qt = sq * Q_SCALE
    sg = _sigmoid(zf)
    sgn = _sigmoid(-zf)
    f = lbp + (1.0 - lbp) * sg
    g = jnp.log(jnp.maximum(f, F_FLOOR))
    kf = (1.0 - lbp) * sgn
    return qt, dsq, sg, sgn, f, g, kf


def _hgrn_intra_scores(qt, kf, b, mbd):
    rid = _row((SUB, 128))
    parts = []
    for s in range(SUB):
        e = jnp.exp(b - b[s:s + 1, :])
        parts.append(jnp.where(rid >= s, qt * kf[s:s + 1, :] * e, 0.0))
    return _dot(jnp.concatenate(parts, axis=0).astype(BF16), mbd)


def _hgrn_intra_out(a, v):
    o = jnp.zeros((SUB, 128), F32)
    for s in range(SUB):
        o = o + a[SUB * s:SUB * s + SUB, :] * v[s:s + 1, :]
    return o


def _hgrn_intra_bwd_scores(qt, kf, b, v, do, mbd):
    rid = _row((SUB, 128))
    ps, das, kes, es = [], [], [], []
    for s in range(SUB):
        e = jnp.where(rid >= s, jnp.exp(b - b[s:s + 1, :]), 0.0)
        ke = kf[s:s + 1, :] * e
        es.append(e)
        kes.append(ke)
        ps.append(qt * ke)
        das.append(do * v[s:s + 1, :])
    a = _dot(jnp.concatenate(ps, axis=0).astype(BF16), mbd)
    da = _dot(jnp.concatenate(das, axis=0).astype(BF16), mbd)
    return a, da, kes, es


def _hgrn_intra_bwd_grads(scores, qt, do, rsum):
    a, da, kes, es = scores
    dqt = jnp.zeros((SUB, 128), F32)
    xs, ys = [], []
    for s in range(SUB):
        da_s = da[SUB * s:SUB * s + SUB, :]
        dqt = dqt + da_s * kes[s]
        xs.append(a[SUB * s:SUB * s + SUB, :] * do)
        ys.append(da_s * qt * es[s])
    dv = _dot(rsum, jnp.concatenate(xs, axis=0).astype(BF16))
    dkf = _dot(rsum, jnp.concatenate(ys, axis=0).astype(BF16))
    return dqt, dkf, dv


def _hgrn_norm_gate(o, z, onorm):
    ms = _gsum64(o * o) * (1.0 / 64)
    r = lax.rsqrt(ms + NORM_EPS)
    xh = o * r
    sz, dsz = _silu_and_grad(z)
    return xh, r, sz, dsz, xh * onorm


def _hgrn_fwd(proj, lb, onorm):
    T = proj.shape[0]
    n = T // CHUNK
    nsub = CHUNK // SUB

    def body(q_ref, f_ref, i_ref, z_ref, lb_ref, on_ref, y_ref, o_ref, s0_ref, st_ref):
        @pl.when(pl.program_id(0) == 0)
        def _():
            st_ref[...] = jnp.zeros_like(st_ref)

        lsub, _, bsub = _hgrn_consts()
        mbd = _block_diag64()
        bdmask = mbd > 0
        rid = _row((CHUNK, 128))
        subs = [slice(SUB * sub, SUB * sub + SUB) for sub in range(nsub)]
        work = []
        for pair in range(2):
            sl = slice(128 * pair, 128 * pair + 128)
            qt, _, _, _, _, g, kf = _hgrn_gates(q_ref[:, sl], f_ref[:, sl], lb_ref[:, sl])
            work.append(dict(sl=sl, qt=qt, kf=kf, v=i_ref[:, sl], b=_dot3_left(lsub, g), bl=_dot3_left(bsub, g)))
        for w in work:
            qt, kf, v, b, bl = w["qt"], w["kf"], w["v"], w["b"], w["bl"]
            w["qh"] = (qt * jnp.exp(b)).astype(BF16)
            kh = kf * jnp.exp(bl - b)
            w["dec"] = jnp.exp(bl)
            vtb = v.T.astype(BF16)
            w["scores"] = [_hgrn_intra_scores(qt[rs], kf[rs], b[rs], mbd) for rs in subs]
            w["adds"] = [_dot(vtb, jnp.where((rid >> SUB_SHIFT) == sub, kh, 0.0).astype(BF16)) for sub in range(nsub)]
        for pair, w in enumerate(work):
            w["st"] = st_ref[pair]
            s0_ref[0, pair] = w["st"]
            w["outs"] = []
        for sub, rs in enumerate(subs):
            for w in work:
                w["outs"].append(_dot_nt(w["qh"][rs], w["st"].astype(BF16)) + _hgrn_intra_out(w["scores"][sub], w["v"][rs]))
                w["st"] = jnp.where(bdmask, w["st"] * w["dec"][SUB * sub:SUB * sub + 1, :] + w["adds"][sub], 0.0)
        for pair, w in enumerate(work):
            sl = w["sl"]
            st_ref[pair] = w["st"]
            o = jnp.concatenate(w["outs"], axis=0)
            o_ref[:, sl] = o
            _, _, sz, _, on = _hgrn_norm_gate(o, z_ref[:, sl], on_ref[:, sl])
            y_ref[:, sl] = (on * sz).astype(BF16)

    col = lambda c: pl.BlockSpec((CHUNK, B_WIDTH), lambda i, c=c: (i, c // B_WIDTH))
    full = lambda a: pl.BlockSpec(a.shape, lambda i, n=a.ndim: (0,) * n)
    return _Part(body, (proj, proj, proj, proj, lb, onorm),
                 [col(COL_BQ), col(COL_BF), col(COL_BI), col(COL_BZ), full(lb), full(onorm)],
                 [pl.BlockSpec((CHUNK, B_WIDTH), lambda i: (i, 0)), pl.BlockSpec((CHUNK, B_WIDTH), lambda i: (i, 0)),
                  pl.BlockSpec((1, 2, 128, 128), lambda i: (i, 0, 0, 0))],
                 [SDS((T, B_WIDTH), BF16), SDS((T, B_WIDTH), F32), SDS((n, 2, 128, 128), F32)],
                 [pltpu.VMEM((2, 128, 128), F32)])


def _hgrn_bwd(proj, dy, o_saved, s0, lb, onorm):
    T = proj.shape[0]
    n = T // CHUNK
    nsub = CHUNK // SUB

    def body(q_ref, f_ref, i_ref, z_ref, dy_ref, o_ref, s0_ref, lb_ref, on_ref,
             db_ref, dlb_ref, don_ref, dst_ref, sts_ref):
        @pl.when(pl.program_id(0) == 0)
        def _():
            dst_ref[...] = jnp.zeros_like(dst_ref)
            dlb_ref[...] = jnp.zeros_like(dlb_ref)
            don_ref[...] = jnp.zeros_like(don_ref)

        lsub, usub, bsub = _hgrn_consts()
        mbd = _block_diag64()
        bdmask = mbd > 0
        rsum = jnp.where((_lane((SUB, SUB * SUB)) >> SUB_SHIFT) == _row((SUB, SUB * SUB)), 1.0, 0.0).astype(BF16)
        subs = [slice(SUB * sub, SUB * sub + SUB) for sub in range(nsub)]
        work = []
        for pair in range(2):
            sl = slice(128 * pair, 128 * pair + 128)
            lbp = lb_ref[:, sl]
            qt, dsq, sg, sgn, f, g, kf = _hgrn_gates(q_ref[:, sl], f_ref[:, sl], lbp)
            w = dict(sl=sl, lbp=lbp, qt=qt, dsq=dsq, sg=sg, sgn=sgn, f=f, kf=kf, v=i_ref[:, sl],
                     b=_dot3_left(lsub, g), bl=_dot3_left(bsub, g))
            onp = on_ref[:, sl]
            xh, r, sz, dsz, on = _hgrn_norm_gate(o_ref[:, sl], z_ref[:, sl], onp)
            dyv = dy_ref[:, sl]
            w["dz"] = dyv * on * dsz
            don = dyv * sz
            cn = jnp.sum(don * xh, axis=0, keepdims=True)
            don_ref[...] += cn + pltpu.roll(cn, 64, axis=1)
            dxo = don * onp
            w["do"] = r * (dxo - xh * (_gsum64(dxo * xh) * (1.0 / 64)))
            work.append(w)
        for w in work:
            qt, kf, v, b, bl, do = w["qt"], w["kf"], w["v"], w["b"], w["bl"], w["do"]
            w["eb"] = jnp.exp(b)
            w["ekb"] = jnp.exp(bl - b)
            w["qhb"] = (qt * w["eb"]).astype(BF16)
            w["khb"] = (kf * w["ekb"]).astype(BF16)
            w["dec"] = jnp.exp(bl)
            w["vb"] = v.astype(BF16)
            w["dob"] = do.astype(BF16)
            w["scores"] = [_hgrn_intra_bwd_scores(qt[rs], kf[rs], b[rs], v[rs], do[rs], mbd) for rs in subs]
            w["st_adds"] = [_dot_tn(w["vb"][rs], w["khb"][rs]) for rs in subs]
            w["gst_adds"] = [_dot_tn(w["dob"][rs], w["qhb"][rs]) for rs in subs]
        for pair, w in enumerate(work):
            w["st"] = s0_ref[0, pair]
        for sub in range(nsub):
            for pair, w in enumerate(work):
                sts_ref[pair, sub] = w["st"]
                w["st"] = jnp.where(bdmask, w["st"] * w["dec"][SUB * sub:SUB * sub + 1, :] + w["st_adds"][sub], 0.0)
        for pair, w in enumerate(work):
            w["gst"] = dst_ref[pair]
            w["dqt_p"], w["dkf_p"], w["dv_p"], w["dbl_p"] = ([None] * nsub for _ in range(4))
        for sub in reversed(range(nsub)):
            rs = subs[sub]
            for pair, w in enumerate(work):
                gst = w["gst"]
                st_in = sts_ref[pair, sub]
                gb = gst.astype(BF16)
                dqh = _dot(w["dob"][rs], st_in.astype(BF16))
                dkh = _dot(w["vb"][rs], gb)
                dv_inter = _dot_nt(w["khb"][rs], gb)
                ddec = jnp.sum(gst * st_in, axis=0, keepdims=True)
                dec_row = w["dec"][SUB * sub:SUB * sub + 1, :]
                w["gst"] = jnp.where(bdmask, gst * dec_row + w["gst_adds"][sub], 0.0)
                dqt_i, dkf_i, dv_i = _hgrn_intra_bwd_grads(w["scores"][sub], w["qt"][rs], w["do"][rs], rsum)
                dkf_inter = dkh * w["ekb"][rs]
                w["dqt_p"][sub] = dqh * w["eb"][rs] + dqt_i
                w["dkf_p"][sub] = dkf_inter + dkf_i
                w["dv_p"][sub] = dv_inter + dv_i
                row = jnp.sum(w["kf"][rs] * dkf_inter, axis=0, keepdims=True) + ddec * dec_row
                w["dbl_p"][sub] = jnp.broadcast_to(row, (SUB, 128))
        for pair, w in enumerate(work):
            sl, lbp, sg, sgn, f = w["sl"], w["lbp"], w["sg"], w["sgn"], w["f"]
            dst_ref[pair] = w["gst"]
            dqt = jnp.concatenate(w["dqt_p"], axis=0)
            dkf = jnp.concatenate(w["dkf_p"], axis=0)
            dv = jnp.concatenate(w["dv_p"], axis=0)
            dg = _dot3_left(usub, w["qt"] * dqt - w["kf"] * dkf) + jnp.concatenate(w["dbl_p"], axis=0)
            df = jnp.where(f > F_FLOOR, dg / f, 0.0)
            dlb_ref[:, sl] += jnp.sum(df * (1.0 - sg) - dkf * sgn, axis=0, keepdims=True)
            dfl = (1.0 - lbp) * sg * sgn * (df - dkf)
            dq = dqt * Q_SCALE * w["dsq"]
            db_ref[:, 0 * B_WIDTH + 128 * pair:0 * B_WIDTH + 128 * pair + 128] = dq.astype(BF16)
            db_ref[:, 1 * B_WIDTH + 128 * pair:1 * B_WIDTH + 128 * pair + 128] = dfl.astype(BF16)
            db_ref[:, 2 * B_WIDTH + 128 * pair:2 * B_WIDTH + 128 * pair + 128] = dv.astype(BF16)
            db_ref[:, 3 * B_WIDTH + 128 * pair:3 * B_WIDTH + 128 * pair + 128] = w["dz"].astype(BF16)

    rev = lambda c: pl.BlockSpec((CHUNK, B_WIDTH), lambda i, c=c: (n - 1 - i, c // B_WIDTH))
    full = lambda a: pl.BlockSpec(a.shape, lambda i, n_=a.ndim: (0,) * n_)
    acc = lambda s: pl.BlockSpec(s, lambda i, n_=len(s): (0,) * n_)
    return _Part(body, (proj, proj, proj, proj, dy, o_saved, s0, lb, onorm),
                 [rev(COL_BQ), rev(COL_BF), rev(COL_BI), rev(COL_BZ),
                  pl.BlockSpec((CHUNK, B_WIDTH), lambda i: (n - 1 - i, 1)),
                  pl.BlockSpec((CHUNK, B_WIDTH), lambda i: (n - 1 - i, 0)),
                  pl.BlockSpec((1, 2, 128, 128), lambda i: (n - 1 - i, 0, 0, 0)), full(lb), full(onorm)],
                 [pl.BlockSpec((CHUNK, 4 * B_WIDTH), lambda i: (n - 1 - i, 0)), acc((1, B_WIDTH)), acc((1, 128))],
                 [SDS((T, 4 * B_WIDTH), BF16), SDS((1, B_WIDTH), F32), SDS((1, 128), F32)],
                 [pltpu.VMEM((2, 128, 128), F32), pltpu.VMEM((2, nsub, 128, 128), F32)])


def _lb_fwd(hgrn_lb):
    assert hgrn_lb.shape[0] == 2

    def body(x_ref, o_ref):
        x0, x1 = x_ref[0:1, :], x_ref[1:2, :]
        m = jnp.maximum(x0, x1)
        e0, e1 = jnp.exp(x0 - m), jnp.exp(x1 - m)
        p0, p1 = e0 / (e0 + e1), e1 / (e0 + e1)
        o_ref[0:1, :] = jnp.clip(p0 - p0, 0.0, 1.0 - 1e-6)
        o_ref[1:2, :] = jnp.clip((p0 + p1) - p0, 0.0, 1.0 - 1e-6)

    return pl.pallas_call(body, name="lb_fwd", out_shape=SDS(hgrn_lb.shape, F32))(hgrn_lb)


def _lb_bwd(hgrn_lb, dlb):
    def body(x_ref, d_ref, o_ref):
        x0, x1 = x_ref[0:1, :], x_ref[1:2, :]
        m = jnp.maximum(x0, x1)
        e0, e1 = jnp.exp(x0 - m), jnp.exp(x1 - m)
        p0, p1 = e0 / (e0 + e1), e1 / (e0 + e1)
        val = (p0 + p1) - p0
        dp1 = jnp.where((val > 0.0) & (val < 1.0 - 1e-6), d_ref[1:2, :], 0.0)
        inner = p1 * dp1
        o_ref[0:1, :] = p0 * (0.0 - inner)
        o_ref[1:2, :] = p1 * (dp1 - inner)

    return pl.pallas_call(body, name="lb_bwd", out_shape=SDS(hgrn_lb.shape, F32))(hgrn_lb, dlb)


def _fox_prep(proj, bf):
    T = proj.shape[0]
    n = T // CHUNK

    def body(q0_ref, q1_ref, k0_ref, k1_ref, v0_ref, v1_ref, fl_ref, bf_ref, qo_ref, ko_ref, vt_ref, carry_ref):
        for p, v_ref in enumerate((v0_ref, v0_ref, v1_ref, v1_ref)):
            vt_ref[p, 0] = v_ref[:, 128 * (p % 2):128 * (p % 2) + 128].T.astype(BF16)

        @pl.when(pl.program_id(0) == 0)
        def _():
            carry_ref[...] = jnp.zeros_like(carry_ref)

        ltri = jnp.where(_lane((CHUNK, CHUNK)) <= _row((CHUNK, CHUNK)), 1.0, 0.0).astype(BF16)
        lf = jax.nn.log_sigmoid(fl_ref[...] + bf_ref[...])
        c = _dot3_left(ltri, lf) + carry_ref[...]
        carry_ref[...] = c[CHUNK - 1:CHUNK, :]
        lane = _lane((CHUNK, 128))
        feat = lane < 64
        ones_q = (lane >= 67) & (lane <= 69)
        ones_k = (lane >= 64) & (lane <= 66)
        qrefs, krefs = (q0_ref, q1_ref), (k0_ref, k1_ref)
        for h in range(C_HEADS):
            blk = slice(128 * ((h // 2) % 2), 128 * ((h // 2) % 2) + 128)
            qp, kp = qrefs[h // 4][:, blk], krefs[h // 4][:, blk]
            if h % 2:
                qp, kp = pltpu.roll(qp, 64, axis=1), pltpu.roll(kp, 64, axis=1)
            ch = jnp.broadcast_to(c[:, h:h + 1], (CHUNK, 128))
            hi = ch.astype(BF16).astype(F32)
            r1 = ch - hi
            mid = r1.astype(BF16).astype(F32)
            lo = r1 - mid
            aq = jnp.where(lane == 64, hi, jnp.where(lane == 65, mid, jnp.where(lane == 66, lo,
                           jnp.where(ones_q, 1.0, 0.0))))
            ak = jnp.where(lane == 67, -hi, jnp.where(lane == 68, -mid, jnp.where(lane == 69, -lo,
                           jnp.where(ones_k, 1.0, 0.0))))
            qo_ref[:, 128 * h:128 * h + 128] = jnp.where(feat, qp * Q_SCALE, aq).astype(BF16)
            ko_ref[:, 128 * h:128 * h + 128] = jnp.where(feat, kp, ak).astype(BF16)

    w = 256
    col = lambda c: pl.BlockSpec((CHUNK, w), lambda i, c=c: (i, c // w))
    return _Part(body, (proj, proj, proj, proj, proj, proj, proj, bf),
                 [col(COL_CQ), col(COL_CQ + w), col(COL_CK), col(COL_CK + w), col(COL_CV), col(COL_CV + w),
                  pl.BlockSpec((CHUNK, 128), lambda i: (i, COL_CF // 128)), pl.BlockSpec((1, 128), lambda i: (0, 0))],
                 [pl.BlockSpec((CHUNK, C_HEADS * 128), lambda i: (i, 0))] * 2
                 + [pl.BlockSpec((C_HEADS // 2, 1, 128, CHUNK), lambda i: (0, i, 0, 0))],
                 [SDS((T, C_HEADS * 128), BF16)] * 2 + [SDS((C_HEADS // 2, n, 128, CHUNK), BF16)],
                 [pltpu.VMEM((1, 128), F32)])


FOX_TILE = 512
FOX_KEYS = 512


def _fox_mask(tk, tq, k0, q0):
    return (_row((tk, tq)) + (k0 - q0)) <= _lane((tk, tq))


def _ride_refs(ride, rest, n_out, n_scratch):
    n = ride.n if ride else 0
    srcs, rest = rest[:n], rest[n:]
    outs, rest = rest[:n_out], rest[n_out:]
    dsts, rest = rest[:n], rest[n:]
    return srcs, outs, dsts, rest[:n_scratch], rest[n_scratch:]


def _ride_start(ride, grid, srcs, dsts, sems):
    if ride:
        first = functools.reduce(lambda a, b: a & b, [pl.program_id(d) == 0 for d in range(len(grid))])
        pl.when(first)(lambda: ride.start(srcs, dsts, sems))


def _ride_wait(ride, grid, srcs, dsts, sems):
    if ride:
        last = functools.reduce(lambda a, b: a & b, [pl.program_id(d) == n - 1 for d, n in enumerate(grid)])
        pl.when(last)(lambda: ride.wait(srcs, dsts, sems))


def _fox_fwd(qt, kt, vt, proj, tag, ride=None):
    T = proj.shape[0]
    tq, tk = _tile(T, FOX_TILE), _tile(T, FOX_KEYS)
    nq, nsub = T // tq, tk // CHUNK
    npair = C_HEADS // 2

    def body(q_ref, k_ref, vt_ref, z_ref, *rest):
        ride_srcs, (o_ref, lse_ref, y_ref), ride_dsts, (acc_ref, st_ref, pt_ref), ride_sems = _ride_refs(ride, rest, 3, 3)
        i = pl.program_id(1)
        _ride_start(ride, (npair, nq), ride_srcs, ride_dsts, ride_sems)

        qs = (q_ref[:, 0:128], q_ref[:, 128:256])
        acc_ref[...] = jnp.zeros_like(acc_ref)
        pt_ref[...] = jnp.zeros_like(pt_ref)
        nfull = (i * tq) // tk

        def scores(j):
            kb = k_ref[pl.ds(pl.multiple_of(j * tk, tk), tk), :]
            return tuple(_dot_nt(kb[:, 128 * h:128 * h + 128], qs[h]) for h in range(2))

        def weigh(j, h):
            rows = slice(64 * h, 64 * h + 64)
            pv = _dot(vt_ref[0, nsub * j, rows, :], pt_ref[h, 0:CHUNK, :])
            for c in range(1, nsub):
                pv = pv + _dot(vt_ref[0, nsub * j + c, rows, :], pt_ref[h, CHUNK * c:CHUNK * c + CHUNK, :])
            return pv

        def block(j, carry, diagonal):
            nxt = () if diagonal else scores(j + 1)
            pvs = [weigh(jnp.maximum(j - 1, 0), h) for h in range(2)]
            new = []
            for h in range(2):
                m, l, alpha_prev = carry[3 * h:3 * h + 3]
                st = st_ref[h]
                if diagonal:
                    st = jnp.where(_fox_mask(tk, tq, j * tk, i * tq), st, -jnp.inf)
                m_new = jnp.maximum(m, _colreduce(st, jnp.maximum))
                pt = jnp.exp(st - m_new)
                alpha = jnp.exp(m - m_new)
                rows = slice(64 * h, 64 * h + 64)
                acc_ref[rows, :] = alpha_prev * acc_ref[rows, :] + pvs[h]
                pt_ref[h] = pt.astype(BF16)
                new += [m_new, alpha * l + _colreduce(pt, jnp.add), alpha]
            for h, st in enumerate(nxt):
                st_ref[h] = st
            return tuple(new)

        for h, st in enumerate(scores(0)):
            st_ref[h] = st
        init = (jnp.full((1, tq), -jnp.inf, F32), jnp.zeros((1, tq), F32), jnp.ones((1, tq), F32)) * 2
        carry = lax.fori_loop(0, nfull, lambda j, c: block(j, c, False), init)
        m0, l0, a0, m1, l1, a1 = block(nfull, carry, True)
        for h, alpha in enumerate((a0, a1)):
            rows = slice(64 * h, 64 * h + 64)
            acc_ref[rows, :] = alpha * acc_ref[rows, :] + weigh(nfull, h)
        inv = jnp.where(_row((128, tq)) < 64, 1.0 / l0, 1.0 / l1)
        o = (acc_ref[...] * inv).T
        o_ref[...] = o
        r8 = _row((8, tq))
        lse_ref[0, 0] = jnp.where(r8 == 0, m0 + jnp.log(l0), jnp.where(r8 == 1, m1 + jnp.log(l1), 0.0))
        sz, _ = _silu_and_grad(z_ref[...])
        y_ref[...] = (o * sz).astype(BF16)
        _ride_wait(ride, (npair, nq), ride_srcs, ride_dsts, ride_sems)

    blk = pl.BlockSpec((tq, 128), lambda p, i: (i, p))
    extra = ride or _ChipExchange("gather", ())
    return pl.pallas_call(
        body, name=f"fox_fwd_{tag}", grid=(npair, nq),
        in_specs=[pl.BlockSpec((tq, 256), lambda p, i: (i, p)), pl.BlockSpec((T, 256), lambda p, i: (0, p)),
                  pl.BlockSpec((1, T // CHUNK, 128, CHUNK), lambda p, i: (p, 0, 0, 0)),
                  pl.BlockSpec((tq, 128), lambda p, i: (i, COL_CZ // 128 + p))] + extra.in_specs,
        out_specs=[blk, pl.BlockSpec((1, 1, 8, tq), lambda p, i: (p, i, 0, 0)), blk] + extra.out_specs,
        out_shape=[SDS((T, C_WIDTH), F32), SDS((npair, nq, 8, tq), F32), SDS((T, C_WIDTH), BF16)] + extra.out_shape,
        scratch_shapes=[pltpu.VMEM((128, tq), F32), pltpu.VMEM((2, tk, tq), F32), pltpu.VMEM((2, tk, tq), BF16)]
        + (extra.scratch if ride else []),
        compiler_params=pltpu.CompilerParams(dimension_semantics=("arbitrary", "arbitrary"), vmem_limit_bytes=VMEM_LIMIT,
                                             has_side_effects=bool(ride)),
    )(qt, kt, vt, proj, *extra.sources)


def _fox_bwd_prep(proj, dy, o, qt, tag):
    T = proj.shape[0]
    tq = _tile(T, FOX_TILE)
    nq = T // tq

    def body(z0_ref, z1_ref, dy_ref, o_ref, q_ref, do_ref, dl_ref, dz_ref, dot_ref, qt_ref):
        sel = jnp.where((_lane((16, 128)) >> 6) == _row((16, 128)), 1.0, 0.0).astype(BF16)
        for p, z_ref in enumerate((z0_ref, z0_ref, z1_ref, z1_ref)):
            sl = slice(128 * p, 128 * p + 128)
            sz, dsz = _silu_and_grad(z_ref[:, 128 * (p % 2):128 * (p % 2) + 128])
            dyv, ov = dy_ref[:, sl], o_ref[:, sl]
            do = dyv * sz
            do_ref[:, sl] = do.astype(BF16)
            dot_ref[p, 0] = do.T.astype(BF16)
            dz_ref[:, sl] = (dyv * ov * dsz).astype(BF16)
            hi, mid, lo = _split3(do * ov)
            dl_ref[p, 0] = (_dot_nt(sel, hi) + _dot_nt(sel, mid) + _dot_nt(sel, lo))[0:8, :]
        for h in range(C_HEADS):
            qt_ref[h, 0] = q_ref[:, 128 * h:128 * h + 128].astype(F32).T.astype(BF16)

    w = 256
    blk = pl.BlockSpec((tq, C_WIDTH), lambda i: (i, 0))
    return pl.pallas_call(
        body, name=f"fox_bwd_prep_{tag}", grid=(nq,),
        in_specs=[pl.BlockSpec((tq, w), lambda i: (i, COL_CZ // w)), pl.BlockSpec((tq, w), lambda i: (i, COL_CZ // w + 1)),
                  pl.BlockSpec((tq, C_WIDTH), lambda i: (i, (A_WIDTH + B_WIDTH) // C_WIDTH)), blk,
                  pl.BlockSpec((tq, C_HEADS * 128), lambda i: (i, 0))],
        out_specs=[blk, pl.BlockSpec((C_HEADS // 2, 1, 8, tq), lambda i: (0, i, 0, 0)), blk,
                   pl.BlockSpec((C_HEADS // 2, 1, 128, tq), lambda i: (0, i, 0, 0)),
                   pl.BlockSpec((C_HEADS, 1, 128, tq), lambda i: (0, i, 0, 0))],
        out_shape=[SDS((T, C_WIDTH), BF16), SDS((C_HEADS // 2, nq, 8, tq), F32), SDS((T, C_WIDTH), BF16),
                   SDS((C_HEADS // 2, nq, 128, tq), BF16), SDS((C_HEADS, nq, 128, tq), BF16)],
        compiler_params=_params("parallel"),
    )(proj, proj, dy, o, qt)


def _fox_bwd(qt, kt, proj, do, lse, delta, dot, qtr, tag, ride=None):
    T = proj.shape[0]
    tq, tk = _tile(T, FOX_TILE), _tile(T, FOX_KEYS)
    nq, nk = T // tq, T // tk
    assert tq == tk
    npair = C_HEADS // 2

    def body(q_ref, k_ref, v_ref, do_ref, lse_ref, dl_ref, dot_ref, qtr_ref, *rest):
        ride_srcs, (dq_ref, dk_ref, dv_ref), ride_dsts, scratch, ride_sems = _ride_refs(ride, rest, 3, 5)
        dvt_ref, dkt_ref, sc_ref, pt_ref, ds_ref = scratch
        j = pl.program_id(1)
        first = (j * tk) // tq
        _ride_start(ride, (npair, nk), ride_srcs, ride_dsts, ride_sems)

        @pl.when(j == 0)
        def _():
            dq_ref[...] = jnp.zeros_like(dq_ref)

        dkt_ref[...] = jnp.zeros_like(dkt_ref)
        dvt_ref[...] = jnp.zeros_like(dvt_ref)
        ks = (k_ref[:, 0:128], k_ref[:, 128:256])
        kts = tuple(k.astype(F32).T.astype(BF16) for k in ks)
        vb = v_ref[...].astype(BF16)
        lo = _lane((tq, 128)) < 64

        def operands(i):
            q0 = pl.multiple_of(i * tq, tq)
            qb = q_ref[pl.ds(q0, tq), :]
            dob = do_ref[pl.ds(q0, tq), :]
            qhs = (qb[:, 0:128], qb[:, 128:256])
            dohs = (jnp.where(lo, dob, jnp.zeros_like(dob)), jnp.where(lo, jnp.zeros_like(dob), dob))
            return qhs, dohs

        def scores(i):
            qhs, dohs = operands(i)
            return tuple(_dot_nt(ks[h], qhs[h]) for h in range(2)) + tuple(_dot_nt(vb, dohs[h]) for h in range(2))

        def park(sc, slot):
            for a, s in enumerate(sc):
                sc_ref[slot, a] = s

        def grads(i):
            for h in range(2):
                rows = slice(64 * h, 64 * h + 64)
                dvt_ref[rows, :] += _dot_nt(dot_ref[0, i, rows, :], pt_ref[h])
                dkt_ref[h] += _dot_nt(qtr_ref[h, i], ds_ref[h])
                dq_ref[h, i] += _dot(kts[h], ds_ref[h])

        def block(i, slot, diagonal, opening):
            park(scores(jnp.minimum(i + 1, nq - 1)), 1 - slot)
            if not opening:
                grads(i - 1)
            lsev = lse_ref[0, i]
            dlv = dl_ref[0, i]
            for h in range(2):
                pt = jnp.exp(sc_ref[slot, h] - lsev[h:h + 1, :])
                if diagonal:
                    pt = jnp.where(_fox_mask(tk, tq, j * tk, i * tq), pt, 0.0)
                ds_ref[h] = (pt * (sc_ref[slot, 2 + h] - dlv[h:h + 1, :])).astype(BF16)
                pt_ref[h] = pt.astype(BF16)

        park(scores(first), 0)
        block(first, 0, True, True)
        rest = nq - 1 - first

        def two_steps(t, carry):
            block(first + 1 + 2 * t, 1, False, False)
            block(first + 2 + 2 * t, 0, False, False)
            return carry

        lax.fori_loop(0, rest // 2, two_steps, 0)
        pl.when(rest % 2 == 1)(lambda: block(nq - 1, 1, False, False))
        grads(nq - 1)
        dv_ref[...] = dvt_ref[...].T.astype(BF16)
        for h in range(2):
            dk_ref[:, 128 * h:128 * h + 128] = dkt_ref[h].T
        _ride_wait(ride, (npair, nk), ride_srcs, ride_dsts, ride_sems)

    full = lambda w: pl.BlockSpec((T, w), lambda p, j: (0, p))
    stat = pl.BlockSpec((1, nq, 8, tq), lambda p, j: (p, 0, 0, 0))
    extra = ride or _ChipExchange("gather", ())
    return pl.pallas_call(
        body, name=f"fox_bwd_{tag}", grid=(npair, nk),
        in_specs=[full(256), pl.BlockSpec((tk, 256), lambda p, j: (j, p)),
                  pl.BlockSpec((tk, 128), lambda p, j: (j, COL_CV // 128 + p)), full(128), stat, stat,
                  pl.BlockSpec((1, nq, 128, tq), lambda p, j: (p, 0, 0, 0)),
                  pl.BlockSpec((2, nq, 128, tq), lambda p, j: (p, 0, 0, 0))] + extra.in_specs,
        out_specs=[pl.BlockSpec((2, nq, 128, tq), lambda p, j: (p, 0, 0, 0)), pl.BlockSpec((tk, 256), lambda p, j: (j, p)),
                   pl.BlockSpec((tk, 128), lambda p, j: (j, p))] + extra.out_specs,
        out_shape=[SDS((C_HEADS, nq, 128, tq), F32), SDS((T, C_HEADS * 128), F32), SDS((T, C_WIDTH), BF16)]
        + extra.out_shape,
        scratch_shapes=[pltpu.VMEM((128, tk), F32), pltpu.VMEM((2, 128, tk), F32), pltpu.VMEM((2, 4, tk, tq), F32),
                        pltpu.VMEM((2, tk, tq), BF16), pltpu.VMEM((2, tk, tq), BF16)] + (extra.scratch if ride else []),
        compiler_params=pltpu.CompilerParams(dimension_semantics=("arbitrary", "arbitrary"), vmem_limit_bytes=VMEM_LIMIT,
                                             has_side_effects=bool(ride)),
    )(qt, kt, proj, do, lse, delta, dot, qtr, *extra.sources)


def _fox_bwd_post(dqt, dkt, proj, bf, tag):
    T = proj.shape[0]
    tq = _tile(T, FOX_TILE)
    n = T // tq

    def body(dq_ref, dk_ref, fl_ref, bf_ref, oq_ref, ok_ref, ofl_ref, dbf_ref, carry_ref):
        @pl.when(pl.program_id(0) == 0)
        def _():
            carry_ref[...] = jnp.zeros_like(carry_ref)
            dbf_ref[...] = jnp.zeros_like(dbf_ref)

        lane = _lane((tq, 128))
        lo = lane < 64
        dqs = [dq_ref[h, 0].T for h in range(C_HEADS)]
        dc = jnp.zeros((tq, 128), F32)
        for h in range(C_HEADS):
            dc = dc + jnp.where(lane == h, dqs[h][:, 64:65] - dk_ref[:, 128 * h + 67:128 * h + 68], 0.0)
        utri = jnp.where(_lane((tq, tq)) >= _row((tq, tq)), 1.0, 0.0).astype(BF16)
        dlf = _dot3_left(utri, dc) + carry_ref[...]
        carry_ref[...] = dlf[0:1, :]
        dfl = jnp.where(lane < C_HEADS, dlf * _sigmoid(-(fl_ref[...] + bf_ref[...])), 0.0)
        ofl_ref[...] = dfl.astype(BF16)
        dbf_ref[...] += jnp.sum(dfl, axis=0, keepdims=True)
        for p in range(C_HEADS // 2):
            a, b = 128 * (2 * p), 128 * (2 * p + 1)
            oq_ref[:, 128 * p:128 * p + 128] = (
                jnp.where(lo, dqs[2 * p], pltpu.roll(dqs[2 * p + 1], 64, axis=1)) * Q_SCALE).astype(BF16)
            ok_ref[:, 128 * p:128 * p + 128] = jnp.where(
                lo, dk_ref[:, a:a + 128], pltpu.roll(dk_ref[:, b:b + 128], 64, axis=1)).astype(BF16)

    rev = lambda w: pl.BlockSpec((tq, w), lambda i: (n - 1 - i, 0))
    return pl.pallas_call(
        body, name=f"fox_bwd_post_{tag}", grid=(n,),
        in_specs=[pl.BlockSpec((C_HEADS, 1, 128, tq), lambda i: (0, n - 1 - i, 0, 0)), rev(C_HEADS * 128),
                  pl.BlockSpec((tq, 128), lambda i: (n - 1 - i, COL_CF // 128)), pl.BlockSpec((1, 128), lambda i: (0, 0))],
        out_specs=[rev(C_WIDTH), rev(C_WIDTH), rev(128), pl.BlockSpec((1, 128), lambda i: (0, 0))],
        out_shape=[SDS((T, C_WIDTH), BF16), SDS((T, C_WIDTH), BF16), SDS((T, 128), BF16), SDS((1, 128), F32)],
        scratch_shapes=[pltpu.VMEM((1, 128), F32)], compiler_params=_params("arbitrary"),
    )(dqt, dkt, proj, bf)


def _adamw_math(w, g, m, v):
    m = ADAM_B1 * m + (1.0 - ADAM_B1) * g
    v = ADAM_B2 * v + (1.0 - ADAM_B2) * (g * g)
    delta = -ADAM_LR * ((m / ADAM_C1) / (jnp.sqrt(v / ADAM_C2) + ADAM_EPS) + ADAM_WD * w)
    return delta, m, v


def _adamw_pair(w, m, v, ga, gb, name):
    n0 = w.shape[0]
    most = max(1, ADAMW_BLOCK_BYTES // (4 * math.prod(w.shape[1:])))
    t0 = max(t for t in range(1, min(n0, most) + 1) if n0 % t == 0)

    def body(w_ref, m_ref, v_ref, ga_ref, gb_ref, g_ref, d_ref, nm_ref, nv_ref):
        g = ga_ref[...] + gb_ref[...]
        g_ref[...] = g
        d_ref[...], nm_ref[...], nv_ref[...] = _adamw_math(w_ref[...], g, m_ref[...], v_ref[...])

    blk = pl.BlockSpec((t0,) + w.shape[1:], lambda i: (i, 0, 0))
    return pl.pallas_call(
        body, name=name, grid=(n0 // t0,), in_specs=[blk] * 5, out_specs=[blk] * 4,
        out_shape=[SDS(w.shape, F32)] * 4, compiler_params=_params("parallel"),
    )(w, m, v, ga, gb)


def _adamw_small(ws, ms, vs, gall):
    offs = _small_offsets()
    n = len(ws)

    def body(*refs):
        w_refs, m_refs, v_refs, g_ref = refs[:n], refs[n:2 * n], refs[2 * n:3 * n], refs[3 * n]
        outs = refs[3 * n + 1:]

        def total(off, rows):
            g = g_ref[0, off:off + rows, :]
            for dev in range(1, N_DEV):
                g = g + g_ref[dev, off:off + rows, :]
            return g

        for k in range(n):
            g = total(offs[k], ws[k].shape[0])
            go_ref, d_ref, nm_ref, nv_ref = outs[4 * k:4 * k + 4]
            go_ref[...] = g
            d_ref[...], nm_ref[...], nv_ref[...] = _adamw_math(w_refs[k][...], g, m_refs[k][...], v_refs[k][...])
        outs[4 * n][...] = total(offs[n], 1)

    shapes = [SDS(w.shape, F32) for w in ws for _ in range(4)] + [SDS((1, 128), F32)]
    res = pl.pallas_call(body, name="adamw_small", out_shape=shapes,
                         compiler_params=pltpu.CompilerParams(vmem_limit_bytes=VMEM_LIMIT))(*ws, *ms, *vs, gall)
    return [res[4 * k:4 * k + 4] for k in range(n)], res[4 * n]


def _pack_grads(dlng, dlnb, dwm, dbst, dlb, donorm, dbf, dfinal, loss_part):
    offs = _small_offsets()
    base = offs[1]
    L = len(dwm)
    assert L == 2

    def body(*refs):
        lng, lnb, wm, bst, on, bf = (refs[L * a:L * a + L] for a in range(6))
        lb_ref, fin_ref, loss_ref, o_ref = refs[6 * L:]
        o_ref[...] = jnp.zeros_like(o_ref)
        lane = _lane((1, 128))
        for l in range(L):
            for j in range(2):
                o_ref[offs[1] - base + 2 * l + j:offs[1] - base + 2 * l + j + 1, :] = lng[l][:, 128 * j:128 * j + 128]
                o_ref[offs[2] - base + 2 * l + j:offs[2] - base + 2 * l + j + 1, :] = lnb[l][:, 128 * j:128 * j + 128]
                o_ref[offs[5] - base + 2 * l + j:offs[5] - base + 2 * l + j + 1, :] = lb_ref[l:l + 1, 128 * j:128 * j + 128]
            for g in range(A_GROUPS):
                row = offs[3] - base + (A_GROUPS * l + g) * CHUNK
                o_ref[row:row + CHUNK, :] = wm[l][g]
            o_ref[offs[4] - base + A_GROUPS * l:offs[4] - base + A_GROUPS * (l + 1), :] = bst[l][...].T[0:A_GROUPS, :]
        o_ref[offs[6] - base:offs[6] - base + 1, :] = jnp.where(lane < 64, on[0][...], pltpu.roll(on[1][...], 64, axis=1))
        o_ref[offs[7] - base:offs[7] - base + 1, :] = jnp.where(
            lane < C_HEADS, bf[0][...], jnp.where(lane < 2 * C_HEADS, pltpu.roll(bf[1][...], C_HEADS, axis=1), 0.0))
        for j in range(D_MODEL // 128):
            o_ref[offs[8] - base + j:offs[8] - base + j + 1, :] = fin_ref[:, 128 * j:128 * j + 128]
        o_ref[offs[9] - base:offs[9] - base + 1, :] = loss_ref[...]

    rows = offs[9] + 8 - base
    return pl.pallas_call(body, name="pack_grads", out_shape=SDS((rows, 128), F32))(
        *dlng, *dlnb, *dwm, *dbst, *donorm, *dbf, dlb, dfinal, loss_part)


def _sum_chips(layers, name, layer_major):
    _, R, C = layers[0].shape
    L = len(layers)
    tc = _tile(C, 256)

    def body(*refs):
        o_ref = refs[-1]
        for l, p_ref in enumerate(refs[:-1]):
            p = [p_ref[k].astype(F32) for k in range(N_CHIPS)]
            s = ((p[0] + p[1]) + p[2]) + p[3]
            if layer_major:
                o_ref[l] = s
            else:
                o_ref[:, l, :] = s

    out = (L, R, C) if layer_major else (R, L, C)
    out_blk = (L, R, tc) if layer_major else (R, L, tc)
    return pl.pallas_call(
        body, name=name, grid=(C // tc,),
        in_specs=[pl.BlockSpec((N_CHIPS, R, tc), lambda i: (0, 0, i))] * L,
        out_specs=pl.BlockSpec(out_blk, lambda i: (0, 0, i)), out_shape=SDS(out, F32),
        compiler_params=_params("parallel"),
    )(*layers)


ANY = pl.BlockSpec(memory_space=pl.ANY)


def _mesh_pos():
    return lax.axis_index("x"), lax.axis_index("y"), lax.axis_index("c")


def _other_chips(x, y):
    return [(1 - x, y), (x, 1 - y), (1 - x, 1 - y)]


class _ChipExchange:
    def __init__(self, mode, sources):
        assert mode in ("gather", "scatter")
        self.mode, self.sources = mode, tuple(sources)
        self.n = len(self.sources)
        self.in_specs = [ANY] * self.n
        self.out_specs = [ANY] * self.n
        self.out_shape = [SDS(((N_CHIPS,) + s.shape) if mode == "gather" else s.shape, s.dtype) for s in self.sources]
        self.scratch = [pltpu.SemaphoreType.DMA((3 * self.n,)), pltpu.SemaphoreType.DMA((3 * self.n,)),
                        pltpu.SemaphoreType.DMA((self.n,))]

    def _copies(self, srcs, dsts, send_sems, recv_sems, local_sems):
        x, y, c = _mesh_pos()
        me = 2 * x + y
        view = (lambda r, chip: r) if self.mode == "gather" else (lambda r, chip: r.at[chip])
        local = [pltpu.make_async_copy(view(s, me), d.at[me], local_sems.at[a]) for a, (s, d) in enumerate(zip(srcs, dsts))]
        sends, recvs = [], []
        for j, (px, py) in enumerate(_other_chips(x, y)):
            peer = 2 * px + py
            for a, (s, d) in enumerate(zip(srcs, dsts)):
                sems = dict(send_sem=send_sems.at[self.n * j + a], recv_sem=recv_sems.at[self.n * j + a],
                            device_id=(px, py, c), device_id_type=MESH_ID)
                sends.append(pltpu.make_async_remote_copy(src_ref=view(s, peer), dst_ref=d.at[me], **sems))
                recvs.append(pltpu.make_async_remote_copy(src_ref=view(s, me), dst_ref=d.at[peer], **sems))
        return local, sends, recvs

    def start(self, srcs, dsts, sems):
        local, sends, _ = self._copies(srcs, dsts, *sems)
        for cp in local + sends:
            cp.start()

    def wait(self, srcs, dsts, sems):
        local, sends, recvs = self._copies(srcs, dsts, *sems)
        for cp in recvs:
            cp.wait_recv()
        for cp in sends:
            cp.wait_send()
        for cp in local:
            cp.wait()


def _gather_halves(w, tag):
    R, C = w.shape
    H = R // 2

    def body(w_ref, g_ref, send_sems, recv_sems, pass_send, pass_recv, local_sem):
        x, y, c = _mesh_pos()
        me = 2 * x + y
        mine, theirs = pl.ds(c * H, H), pl.ds((1 - c) * H, H)
        own = pltpu.make_async_copy(w_ref, g_ref.at[me], local_sem)
        own.start()

        def fetch(j, px, py, src, dst):
            return pltpu.make_async_remote_copy(src_ref=src, dst_ref=dst, send_sem=send_sems.at[j], recv_sem=recv_sems.at[j],
                                                device_id=(px, py, c), device_id_type=MESH_ID)

        def hand(j, rows, peer):
            return pltpu.make_async_remote_copy(src_ref=g_ref.at[peer, rows], dst_ref=g_ref.at[peer, rows],
                                                send_sem=pass_send.at[j], recv_sem=pass_recv.at[j],
                                                device_id=(x, y, 1 - c), device_id_type=MESH_ID)

        chips = _other_chips(x, y)
        sends = [fetch(j, px, py, w_ref.at[mine], g_ref.at[me, mine]) for j, (px, py) in enumerate(chips)]
        for cp in sends:
            cp.start()
        passed = []
        for j, (px, py) in enumerate(chips):
            peer = 2 * px + py
            fetch(j, px, py, w_ref.at[mine], g_ref.at[peer, mine]).wait_recv()
            passed.append(hand(j, mine, peer))
            passed[-1].start()
        for j, (px, py) in enumerate(chips):
            hand(j, theirs, 2 * px + py).wait_recv()
        for cp in sends + passed:
            cp.wait_send()
        own.wait()

    return pl.pallas_call(
        body, name=f"gather_halves_{tag}", in_specs=[ANY], out_specs=ANY, out_shape=SDS((N_CHIPS, R, C), w.dtype),
        scratch_shapes=[pltpu.SemaphoreType.DMA((3,)), pltpu.SemaphoreType.DMA((3,)), pltpu.SemaphoreType.DMA((3,)),
                        pltpu.SemaphoreType.DMA((3,)), pltpu.SemaphoreType.DMA],
        compiler_params=pltpu.CompilerParams(has_side_effects=True),
    )(w)


class _DeviceGather:
    def __init__(self, source):
        self.sources, self.n = (source,), 1
        self.in_specs, self.out_specs = [ANY], [ANY]
        self.out_shape = [SDS((N_DEV,) + source.shape, source.dtype)]
        self.scratch = [pltpu.SemaphoreType.DMA((N_DEV - 1,)), pltpu.SemaphoreType.DMA((N_DEV - 1,)),
                        pltpu.SemaphoreType.DMA((1,))]

    def _copies(self, srcs, dsts, send_sems, recv_sems, local_sems):
        (src,), (dst,) = srcs, dsts
        x, y, c = _mesh_pos()
        me = 4 * x + 2 * y + c
        local = [pltpu.make_async_copy(src, dst.at[me], local_sems.at[0])]
        sends, recvs = [], []
        for k in range(1, N_DEV):
            px, py, pc = (1 - x) if k & 4 else x, (1 - y) if k & 2 else y, (1 - c) if k & 1 else c
            sems = dict(send_sem=send_sems.at[k - 1], recv_sem=recv_sems.at[k - 1], device_id=(px, py, pc),
                        device_id_type=MESH_ID)
            sends.append(pltpu.make_async_remote_copy(src_ref=src, dst_ref=dst.at[me], **sems))
            recvs.append(pltpu.make_async_remote_copy(src_ref=src, dst_ref=dst.at[4 * px + 2 * py + pc], **sems))
        return local, sends, recvs

    start = _ChipExchange.start
    wait = _ChipExchange.wait


def _gather_devices(a, name):
    ex = _DeviceGather(a)

    def body(a_ref, g_ref, *sems):
        ex.start((a_ref,), (g_ref,), sems)
        ex.wait((a_ref,), (g_ref,), sems)

    return pl.pallas_call(
        body, name=name, in_specs=ex.in_specs, out_specs=ex.out_specs[0], out_shape=ex.out_shape[0],
        scratch_shapes=ex.scratch, compiler_params=pltpu.CompilerParams(has_side_effects=True),
    )(a)


def _swap_cores(pin, pout):
    def body(pin_ref, pout_ref, oin_ref, oout_ref, send_sems, recv_sems):
        x, y, c = _mesh_pos()
        cps = [pltpu.make_async_remote_copy(src_ref=src, dst_ref=dst, send_sem=send_sems.at[a], recv_sem=recv_sems.at[a],
                                            device_id=(x, y, 1 - c), device_id_type=MESH_ID)
               for a, (src, dst) in enumerate(((pin_ref, oin_ref), (pout_ref, oout_ref)))]
        for cp in cps:
            cp.start()
        for cp in cps:
            cp.wait()

    return pl.pallas_call(
        body, name="swap_cores", in_specs=[ANY, ANY], out_specs=[ANY, ANY],
        out_shape=[SDS(pin.shape, F32), SDS(pout.shape, F32)],
        scratch_shapes=[pltpu.SemaphoreType.DMA((2,)), pltpu.SemaphoreType.DMA((2,))],
        compiler_params=pltpu.CompilerParams(has_side_effects=True),
    )(pin, pout)


PACK_TILE = 8 * 128


def _pack_rows(size):
    return (size + PACK_TILE - 1) // PACK_TILE * 8


def _small_offsets():
    offs = [0]
    for _, shape in SMALL_PARAMS:
        offs.append(offs[-1] + _pack_rows(math.prod(shape)))
    return offs


def _rows_view(a):
    flat = a.reshape(-1)
    return jnp.pad(flat, (0, (-flat.size) % 128)).reshape(-1, 128)


def _from_rows(rows, shape):
    return rows.reshape(-1)[:math.prod(shape)].reshape(shape)


def _layer_consts(l, gmlp_ln_g, gmlp_ln_b, gmlp_w_s, gmlp_b_s, hgrn_onorm_g, fox_b_f):
    causal = jnp.tril(jnp.ones((CHUNK, CHUNK), bool))
    wm = jnp.where(causal[None], gmlp_w_s[l], 0.0)
    return dict(
        lng=gmlp_ln_g[l].reshape(1, A_WIDTH), lnb=gmlp_ln_b[l].reshape(1, A_WIDTH),
        wm=wm.astype(BF16), wmt=jnp.swapaxes(wm, 1, 2).astype(BF16),
        bst=jnp.pad(gmlp_b_s[l].T, ((0, 0), (0, 128 - A_GROUPS))),
        onorm=jnp.tile(hgrn_onorm_g[l], 4).reshape(1, B_WIDTH),
        bf=jnp.pad(fox_b_f[l], (0, 128 - C_HEADS)).reshape(1, 128),
    )


def kernel(x, norm_g, w_in, w_out, gmlp_ln_g, gmlp_ln_b, gmlp_w_s, gmlp_b_s, hgrn_lb, hgrn_onorm_g, fox_b_f, final_norm_g, loss_target, m_norm_g, m_w_in, m_w_out, m_gmlp_ln_g, m_gmlp_ln_b, m_gmlp_w_s, m_gmlp_b_s, m_hgrn_lb, m_hgrn_onorm_g, m_fox_b_f, m_final_norm_g, v_norm_g, v_w_in, v_w_out, v_gmlp_ln_g, v_gmlp_ln_b, v_gmlp_w_s, v_gmlp_b_s, v_hgrn_lb, v_hgrn_onorm_g, v_fox_b_f, v_final_norm_g):
    T = x.shape[1]
    shard_in = w_in.shape[2]
    shard_out = w_out.shape[1]
    xs = x.reshape(T, D_MODEL)
    tgt = loss_target.reshape(T, D_MODEL)

    w_in_b, w_out_b = w_in.astype(BF16), w_out.astype(BF16)

    def full_w_in(gathered):
        return jnp.concatenate([gathered[k] for k in range(N_CHIPS)] + [jnp.zeros((D_MODEL, D_IN_PAD - D_IN), BF16)], axis=-1)

    lb_all = _lb_fwd(hgrn_lb)
    consts = [_layer_consts(l, gmlp_ln_g, gmlp_ln_b, gmlp_w_s, gmlp_b_s, hgrn_onorm_g, fox_b_f) for l in range(DEPTH)]

    saved = []
    xl = xs
    w_in_l = full_w_in(_gather_halves(w_in_b[0], "w_in_l0"))
    for l in range(DEPTH):
        cs = consts[l]
        tag = f"l{l}"
        h, proj = _inproj(xl, norm_g[l].reshape(1, D_MODEL), w_in_l, tag)
        (ya,), (yb, ob, s0), (qt, kt, vt) = _run_parts(
            [_gmlp_fwd(proj, cs["lng"], cs["lnb"], cs["wm"], cs["bst"]),
             _hgrn_fwd(proj, lb_all[l].reshape(1, B_WIDTH), cs["onorm"]), _fox_prep(proj, cs["bf"])],
            (T // CHUNK,), f"mix_fwd_{tag}")
        ride = _ChipExchange("gather", (w_out_b[l],) + ((w_in_b[l + 1],) if l + 1 < DEPTH else ()))
        oc, lse, yc, *gathered = _fox_fwd(qt, kt, vt, proj, tag, ride)
        w_out_l = gathered[0].reshape(N_CHIPS * shard_out, D_MODEL)
        saved.append(dict(x=xl, h=h, proj=proj, ya=ya, yb=yb, yc=yc, ob=ob, s0=s0, qt=qt, kt=kt, oc=oc, lse=lse,
                          w_in=w_in_l, w_out=w_out_l))
        xl = _outproj(xl, ya, yb, yc, w_out_l, tag)
        if l + 1 < DEPTH:
            w_in_l = full_w_in(gathered[1])

    dx, loss_part, d_final = _loss_head(xl, final_norm_g.reshape(1, D_MODEL), tgt)

    g_small = {}
    dlb_rows, rin, rout = [None] * DEPTH, [None] * DEPTH, [None] * DEPTH
    slabs_in = None
    for l in reversed(range(DEPTH)):
        cs, sv = consts[l], saved[l]
        tag = f"l{l}"
        proj = sv["proj"]
        dy, dw_out = _outproj_bwd(dx, sv["ya"], sv["yb"], sv["yc"], sv["w_out"], tag)
        (da, dwm, dbst, dlng, dlnb), (db, dlb_rows[l], donorm) = _run_parts(
            [_gmlp_bwd(proj, dy, cs["lng"], cs["lnb"], cs["wm"], cs["wmt"], cs["bst"]),
             _hgrn_bwd(proj, dy, sv["ob"], sv["s0"], lb_all[l].reshape(1, B_WIDTH), cs["onorm"])],
            (T // CHUNK,), f"mix_bwd_{tag}")
        do, delta, dzc, dot, qtr = _fox_bwd_prep(proj, dy, sv["oc"], sv["qt"], tag)
        slabs_out = dw_out.reshape(N_CHIPS, shard_out, D_MODEL).astype(BF16)
        ride = _ChipExchange("scatter", (slabs_out,) + ((slabs_in,) if slabs_in is not None else ()))
        dqt, dkt, dvc, *received = _fox_bwd(sv["qt"], sv["kt"], proj, do, sv["lse"], delta, dot, qtr, tag, ride)
        rout[l] = received[0]
        if slabs_in is not None:
            rin[l + 1] = received[1]
        dqc, dkc, dflc, dbf = _fox_bwd_post(dqt, dkt, proj, cs["bf"], tag)
        g_small[l] = dict(ln_g=dlng, ln_b=dlnb, w_s=dwm, b_s=dbst, onorm=donorm, bf=dbf)
        dproj = jnp.concatenate([da, db, dqc, dkc, dvc, dzc, dflc, jnp.zeros((T, 128), BF16)], axis=1)
        if l == 0:
            d_hgrn_lb = _lb_bwd(hgrn_lb, jnp.concatenate(dlb_rows, axis=0))
            per_layer = lambda key: [g_small[k][key] for k in range(DEPTH)]
            early = _pack_grads(per_layer("ln_g"), per_layer("ln_b"), per_layer("w_s"), per_layer("b_s"), d_hgrn_lb,
                                per_layer("onorm"), per_layer("bf"), d_final, loss_part)
            dw_in, rearly = _dw_in(sv["h"], dproj, tag, _DeviceGather(early))
        else:
            dw_in = _dw_in(sv["h"], dproj, tag)
        slabs_in = dw_in[:N_CHIPS * shard_in].reshape(N_CHIPS, shard_in, D_MODEL).astype(BF16)
        ride = _ChipExchange("scatter", (slabs_in,)) if l == 0 else None
        dx, dng, *received = _dx_in(sv["x"], norm_g[l].reshape(1, D_MODEL), dx, dproj, sv["w_in"], tag, ride)
        if l == 0:
            rin[0] = received[0]
        g_small[l]["norm_g"] = dng.reshape(D_MODEL // 128, 128)
    grad_x = dx.reshape(x.shape)
    rlate = _gather_devices(jnp.concatenate([g_small[l]["norm_g"] for l in range(DEPTH)]), "gather_norm_grads")
    rsmall = jnp.concatenate([rlate, rearly], axis=1)

    pin, pout = _sum_chips(rin, "sum_chips_w_in", False), _sum_chips(rout, "sum_chips_w_out", True)
    oin, oout = _swap_cores(pin, pout)
    to_view = lambda a: jnp.transpose(a, (2, 0, 1))
    g_w_in, d_w_in, nm_w_in, nv_w_in = [
        jnp.transpose(o, (1, 2, 0))
        for o in _adamw_pair(to_view(w_in), to_view(m_w_in), to_view(v_w_in), pin, oin, "adamw_w_in")]
    g_w_out, d_w_out, nm_w_out, nv_w_out = _adamw_pair(w_out, m_w_out, v_w_out, pout, oout, "adamw_w_out")

    small_w = [norm_g, gmlp_ln_g, gmlp_ln_b, gmlp_w_s, gmlp_b_s, hgrn_lb, hgrn_onorm_g, fox_b_f, final_norm_g]
    small_m = [m_norm_g, m_gmlp_ln_g, m_gmlp_ln_b, m_gmlp_w_s, m_gmlp_b_s, m_hgrn_lb, m_hgrn_onorm_g, m_fox_b_f, m_final_norm_g]
    small_v = [v_norm_g, v_gmlp_ln_g, v_gmlp_ln_b, v_gmlp_w_s, v_gmlp_b_s, v_hgrn_lb, v_hgrn_onorm_g, v_fox_b_f, v_final_norm_g]
    views = lambda ps: [_rows_view(p) for p in ps]
    per_param, loss_row = _adamw_small(views(small_w), views(small_m), views(small_v), rsmall)
    sg, sd, sm, sv_ = [[_from_rows(per_param[k][a], shape) for k, (_, shape) in enumerate(SMALL_PARAMS)] for a in range(4)]
    loss = loss_row[0, 0]

    def order(big_in, big_out, small):
        return [small[0], big_in, big_out] + small[1:]

    return (loss, grad_x, *order(g_w_in, g_w_out, sg), *order(d_w_in, d_w_out, sd), *order(nm_w_in, nm_w_out, sm),
            *order(nv_w_in, nv_w_out, sv_))
```

```python
import collections
import functools
import math

import jax
import jax.numpy as jnp
from jax import lax
from jax.experimental import pallas as pl
from jax.experimental.pallas import tpu as pltpu

F32 = jnp.float32
BF16 = jnp.bfloat16
SDS = jax.ShapeDtypeStruct
MESH_ID = pl.DeviceIdType.MESH

D_MODEL = 1024
DEPTH = 2
A_WIDTH = 256
A_GROUPS = 4
B_WIDTH = 256
C_WIDTH = 512
C_HEADS = 8
D_IN = 3848
D_IN_PAD = 4096
CHUNK = 128
SUB = 16
SUB_SHIFT = 4
NORM_EPS = 1e-6
F_FLOOR = 1e-30
COL_AU, COL_AV, COL_AZ = 0, 256, 512
COL_BQ, COL_BF, COL_BI, COL_BZ = 768, 1024, 1280, 1536
COL_CQ, COL_CK, COL_CV, COL_CZ, COL_CF = 1792, 2304, 2816, 3328, 3840
HEAD_LANES = 128
Q_SCALE = 0.125
ADAM_LR, ADAM_B1, ADAM_B2, ADAM_EPS, ADAM_WD, ADAM_STEP = 0.001, 0.9, 0.999, 1e-08, 0.01, 10
ADAM_C1 = 1.0 - ADAM_B1 ** ADAM_STEP
ADAM_C2 = 1.0 - ADAM_B2 ** ADAM_STEP
VMEM_LIMIT = 56 * 1024 * 1024
ADAMW_BLOCK_BYTES = 1 << 20
N_CHIPS = 4
N_DEV = 8

SMALL_PARAMS = (
    ("norm_g", (DEPTH, D_MODEL)), ("gmlp_ln_g", (DEPTH, 4, 64)), ("gmlp_ln_b", (DEPTH, 4, 64)),
    ("gmlp_w_s", (DEPTH, 4, 128, 128)), ("gmlp_b_s", (DEPTH, 4, 128)), ("hgrn_lb", (DEPTH, 256)),
    ("hgrn_onorm_g", (DEPTH, 64)), ("fox_b_f", (DEPTH, 8)), ("final_norm_g", (D_MODEL,)),
)


def _tile(n, pref):
    t = min(n, pref)
    assert n % t == 0, (n, pref)
    return t


def _params(*sem):
    return pltpu.CompilerParams(dimension_semantics=sem, vmem_limit_bytes=VMEM_LIMIT)


_Part = collections.namedtuple("_Part", "body operands in_specs out_specs out_shape scratch")


def _run_parts(parts, grid, name):
    counts = [(len(p.operands), len(p.out_shape), len(p.scratch)) for p in parts]

    def body(*refs):
        ins, outs, scr = [], [], []
        pos = 0
        for group, k in ((ins, 0), (outs, 1), (scr, 2)):
            for c in counts:
                group.append(refs[pos:pos + c[k]])
                pos += c[k]
        for p, i, o, s in zip(parts, ins, outs, scr):
            p.body(*i, *o, *s)

    flat = lambda key: [x for p in parts for x in getattr(p, key)]
    res = pl.pallas_call(
        body, name=name, grid=grid, in_specs=flat("in_specs"), out_specs=flat("out_specs"), out_shape=flat("out_shape"),
        scratch_shapes=flat("scratch"), compiler_params=_params(*(("arbitrary",) * len(grid))),
    )(*flat("operands"))
    out, pos = [], 0
    for c in counts:
        out.append(list(res[pos:pos + c[1]]))
        pos += c[1]
    return out


def _dot(a, b):
    return jnp.dot(a, b, preferred_element_type=F32)


def _dot_nt(a, b):
    return lax.dot_general(a, b, (((1,), (1,)), ((), ())), preferred_element_type=F32)


def _dot_tn(a, b):
    return lax.dot_general(a, b, (((0,), (0,)), ((), ())), preferred_element_type=F32)


def _split3(x):
    hi = x.astype(BF16)
    r = x - hi.astype(F32)
    mid = r.astype(BF16)
    lo = (r - mid.astype(F32)).astype(BF16)
    return hi, mid, lo


def _dot3_left(c, x):
    hi, mid, lo = _split3(x)
    return _dot(c, hi) + _dot(c, mid) + _dot(c, lo)


def _sigmoid(x):
    return jax.nn.sigmoid(x)


def _silu_and_grad(x):
    s = _sigmoid(x)
    return x * s, s * (1.0 + x * (1.0 - s))


_GELU_C = math.sqrt(2.0 / math.pi)


def _gelu_and_grad(x):
    inner = _GELU_C * (x + 0.044715 * x * x * x)
    t = jnp.tanh(inner)
    y = 0.5 * x * (1.0 + t)
    dy = 0.5 * (1.0 + t) + 0.5 * x * (1.0 - t * t) * _GELU_C * (1.0 + 3.0 * 0.044715 * x * x)
    return y, dy


def _lane(shape):
    return lax.broadcasted_iota(jnp.int32, shape, 1)


def _row(shape):
    return lax.broadcasted_iota(jnp.int32, shape, 0)


def _gsum64(x):
    lo = _lane(x.shape) < 64
    s0 = jnp.sum(jnp.where(lo, x, 0.0), axis=-1, keepdims=True)
    s1 = jnp.sum(jnp.where(lo, 0.0, x), axis=-1, keepdims=True)
    return jnp.where(lo, s0, s1)


def _colreduce(x, op):
    parts = [x[r:r + 8, :] for r in range(0, x.shape[0], 8)]
    while len(parts) > 1:
        pairs = [op(parts[k], parts[k + 1]) for k in range(0, len(parts) - 1, 2)]
        parts = pairs + ([parts[-1]] if len(parts) % 2 else [])
    red = jnp.max if op is jnp.maximum else jnp.sum
    return red(parts[0], axis=0, keepdims=True)


def _block_diag64(dtype=BF16):
    r, c = _row((128, 128)), _lane((128, 128))
    return jnp.where((r >> 6) == (c >> 6), 1.0, 0.0).astype(dtype)


def _inproj(x, g, w, tag):
    T, D = x.shape
    DP = w.shape[1]
    tm = _tile(T, 512)

    def body(x_ref, g_ref, w_ref, h_ref, p_ref):
        xv = x_ref[...]
        r = lax.rsqrt(jnp.mean(xv * xv, axis=-1, keepdims=True) + NORM_EPS)
        h = (xv * r * g_ref[...]).astype(BF16)
        h_ref[...] = h
        p_ref[...] = _dot(h, w_ref[...])

    return pl.pallas_call(
        body, name=f"inproj_{tag}", grid=(T // tm,),
        in_specs=[pl.BlockSpec((tm, D), lambda i: (i, 0)), pl.BlockSpec((1, D), lambda i: (0, 0)),
                  pl.BlockSpec((D, DP), lambda i: (0, 0))],
        out_specs=[pl.BlockSpec((tm, D), lambda i: (i, 0)), pl.BlockSpec((tm, DP), lambda i: (i, 0))],
        out_shape=[SDS((T, D), BF16), SDS((T, DP), F32)],
        compiler_params=_params("parallel"),
    )(x, g, w)


def _outproj(x, ya, yb, yc, wo, tag):
    T, D = x.shape
    tm = _tile(T, 512)

    def body(x_ref, ya_ref, yb_ref, yc_ref, wo_ref, o_ref):
        acc = x_ref[...] + _dot(ya_ref[...], wo_ref[0:A_WIDTH, :])
        acc = acc + _dot(yb_ref[...], wo_ref[A_WIDTH:A_WIDTH + B_WIDTH, :])
        o_ref[...] = acc + _dot(yc_ref[...], wo_ref[A_WIDTH + B_WIDTH:, :])

    row = lambda w: pl.BlockSpec((tm, w), lambda i: (i, 0))
    return pl.pallas_call(
        body, name=f"outproj_{tag}", grid=(T // tm,),
        in_specs=[row(D), row(A_WIDTH), row(B_WIDTH), row(C_WIDTH), pl.BlockSpec(wo.shape, lambda i: (0, 0))],
        out_specs=row(D), out_shape=SDS((T, D), F32), compiler_params=_params("parallel"),
    )(x, ya, yb, yc, wo)


def _outproj_bwd(dx, ya, yb, yc, wo, tag):
    T, D = dx.shape
    DM = wo.shape[0]
    tm = _tile(T, 512)

    def body(dx_ref, ya_ref, yb_ref, yc_ref, wo_ref, dy_ref, dwo_ref):
        @pl.when(pl.program_id(0) == 0)
        def _():
            dwo_ref[...] = jnp.zeros_like(dwo_ref)

        dxb = dx_ref[...].astype(BF16)
        dy_ref[...] = _dot_nt(dxb, wo_ref[...])
        dwo_ref[0:A_WIDTH, :] += _dot_tn(ya_ref[...], dxb)
        dwo_ref[A_WIDTH:A_WIDTH + B_WIDTH, :] += _dot_tn(yb_ref[...], dxb)
        dwo_ref[A_WIDTH + B_WIDTH:, :] += _dot_tn(yc_ref[...], dxb)

    row = lambda w: pl.BlockSpec((tm, w), lambda i: (i, 0))
    return pl.pallas_call(
        body, name=f"outproj_bwd_{tag}", grid=(T // tm,),
        in_specs=[row(D), row(A_WIDTH), row(B_WIDTH), row(C_WIDTH), pl.BlockSpec(wo.shape, lambda i: (0, 0))],
        out_specs=[row(DM), pl.BlockSpec((DM, D), lambda i: (0, 0))],
        out_shape=[SDS((T, DM), F32), SDS((DM, D), F32)], compiler_params=_params("arbitrary"),
    )(dx, ya, yb, yc, wo)


def _piece_offsets(pieces):
    offs = [0]
    for p in pieces:
        offs.append(offs[-1] + p.shape[1])
    return offs


def _dw_in(h, pieces, dp_width, tag, ride=None):
    T, D = h.shape
    tm = _tile(T, 512)
    grid = (T // tm,)
    offs = _piece_offsets(pieces)
    n = len(pieces)

    def body(h_ref, *rest):
        p_refs, rest = rest[:n], rest[n:]
        ride_srcs, (dw_ref,), ride_dsts, (acc_ref,), ride_sems = _ride_refs(ride, rest, 1, 1)
        i = pl.program_id(0)
        _ride_start(ride, grid, ride_srcs, ride_dsts, ride_sems)

        @pl.when(i == 0)
        def _():
            acc_ref[...] = jnp.zeros_like(acc_ref)

        hv = h_ref[...]
        for k, p_ref in enumerate(p_refs):
            acc_ref[offs[k]:offs[k + 1], :] += _dot_tn(p_ref[...], hv)

        @pl.when(i == grid[0] - 1)
        def _():
            dw_ref[...] = acc_ref[...].astype(BF16)

        _ride_wait(ride, grid, ride_srcs, ride_dsts, ride_sems)

    extra = ride or _ChipExchange("gather", ())
    out = pl.pallas_call(
        body, name=f"dw_in_{tag}", grid=grid,
        in_specs=[pl.BlockSpec((tm, D), lambda i: (i, 0))] + [pl.BlockSpec((tm, p.shape[1]), lambda i: (i, 0)) for p in pieces]
        + extra.in_specs,
        out_specs=[pl.BlockSpec((dp_width, D), lambda i: (0, 0))] + extra.out_specs,
        out_shape=[SDS((dp_width, D), BF16)] + extra.out_shape,
        scratch_shapes=[pltpu.VMEM((dp_width, D), F32)] + (extra.scratch if ride else []),
        compiler_params=pltpu.CompilerParams(dimension_semantics=("arbitrary",), vmem_limit_bytes=VMEM_LIMIT,
                                             has_side_effects=bool(ride)),
    )(h, *pieces, *extra.sources)
    return out if ride else out[0]


def _dx_in(x, g, dres, pieces, w, tag, ride=None):
    T, D = x.shape
    DP = w.shape[1]
    tm = _tile(T, 512)
    grid = (T // tm,)
    offs = _piece_offsets(pieces)
    n = len(pieces)

    def body(x_ref, g_ref, dres_ref, w_ref, *rest):
        p_refs, rest = rest[:n], rest[n:]
        ride_srcs, (dx_ref, dg_ref), ride_dsts, _, ride_sems = _ride_refs(ride, rest, 2, 0)
        _ride_start(ride, grid, ride_srcs, ride_dsts, ride_sems)

        @pl.when(pl.program_id(0) == 0)
        def _():
            dg_ref[...] = jnp.zeros_like(dg_ref)

        dh = _dot_nt(p_refs[0][...], w_ref[:, offs[0]:offs[1]])
        for k in range(1, n):
            dh = dh + _dot_nt(p_refs[k][...], w_ref[:, offs[k]:offs[k + 1]])
        xv = x_ref[...]
        r = lax.rsqrt(jnp.mean(xv * xv, axis=-1, keepdims=True) + NORM_EPS)
        xh = xv * r
        dg_ref[...] += jnp.sum(dh * xh, axis=0, keepdims=True)
        dxh = dh * g_ref[...]
        dx_ref[...] = dres_ref[...] + r * (dxh - xh * jnp.mean(dxh * xh, axis=-1, keepdims=True))
        _ride_wait(ride, grid, ride_srcs, ride_dsts, ride_sems)

    extra = ride or _ChipExchange("gather", ())
    row = pl.BlockSpec((tm, D), lambda i: (i, 0))
    return pl.pallas_call(
        body, name=f"dx_in_{tag}", grid=grid,
        in_specs=[row, pl.BlockSpec((1, D), lambda i: (0, 0)), row, pl.BlockSpec((D, DP), lambda i: (0, 0))]
        + [pl.BlockSpec((tm, p.shape[1]), lambda i: (i, 0)) for p in pieces] + extra.in_specs,
        out_specs=[row, pl.BlockSpec((1, D), lambda i: (0, 0))] + extra.out_specs,
        out_shape=[SDS((T, D), F32), SDS((1, D), F32)] + extra.out_shape,
        scratch_shapes=extra.scratch if ride else [],
        compiler_params=pltpu.CompilerParams(dimension_semantics=("arbitrary",), vmem_limit_bytes=VMEM_LIMIT,
                                             has_side_effects=bool(ride)),
    )(x, g, dres, w, *pieces, *extra.sources)


def _loss_head(x, g, tgt):
    T, D = x.shape
    tm = _tile(T, 512)

    def body(x_ref, g_ref, t_ref, dx_ref, loss_ref, dg_ref):
        @pl.when(pl.program_id(0) == 0)
        def _():
            loss_ref[...] = jnp.zeros_like(loss_ref)
            dg_ref[...] = jnp.zeros_like(dg_ref)

        xv = x_ref[...]
        r = lax.rsqrt(jnp.mean(xv * xv, axis=-1, keepdims=True) + NORM_EPS)
        xh = xv * r
        gv = g_ref[...]
        err = xh * gv - t_ref[...]
        tok = jnp.mean(err * err, axis=-1, keepdims=True)
        loss_ref[...] += 0.5 * jnp.sum(tok, axis=0, keepdims=True)
        dy = err * (1.0 / D)
        dg_ref[...] += jnp.sum(dy * xh, axis=0, keepdims=True)
        dxh = dy * gv
        dx_ref[...] = r * (dxh - xh * jnp.mean(dxh * xh, axis=-1, keepdims=True))

    row = pl.BlockSpec((tm, D), lambda i: (i, 0))
    return pl.pallas_call(
        body, name="loss_head", grid=(T // tm,),
        in_specs=[row, pl.BlockSpec((1, D), lambda i: (0, 0)), row],
        out_specs=[row, pl.BlockSpec((1, 128), lambda i: (0, 0)), pl.BlockSpec((1, D), lambda i: (0, 0))],
        out_shape=[SDS((T, D), F32), SDS((1, 128), F32), SDS((1, D), F32)], compiler_params=_params("arbitrary"),
    )(x, g, tgt)


def _gmlp_core(u, v, lng, lnb, wm_ref, bst_ref, pair):
    ug, dug = _gelu_and_grad(u)
    vg, dvg = _gelu_and_grad(v)
    mu = _gsum64(vg) * (1.0 / 64)
    d = vg - mu
    var = _gsum64(d * d) * (1.0 / 64)
    rstd = lax.rsqrt(var + NORM_EPS)
    xh = d * rstd
    vn = xh * lng + lnb
    vnb = vn.astype(BF16)
    lo = _lane(u.shape) < 64
    g0, g1 = 2 * pair, 2 * pair + 1
    mixed = jnp.where(lo, _dot(wm_ref[g0], vnb) + bst_ref[:, g0:g0 + 1], _dot(wm_ref[g1], vnb) + bst_ref[:, g1:g1 + 1])
    return ug, dug, dvg, rstd, xh, vnb, mixed, lo


def _gmlp_fwd(proj, lng, lnb, wm, bst):
    T = proj.shape[0]

    def body(u_ref, v_ref, z_ref, lng_ref, lnb_ref, wm_ref, bst_ref, y_ref):
        for pair in range(2):
            sl = slice(128 * pair, 128 * pair + 128)
            ug, _, _, _, _, _, mixed, _ = _gmlp_core(u_ref[:, sl], v_ref[:, sl], lng_ref[:, sl], lnb_ref[:, sl],
                                                     wm_ref, bst_ref, pair)
            sz, _ = _silu_and_grad(z_ref[:, sl])
            y_ref[:, sl] = (ug * mixed * sz).astype(BF16)

    col = lambda c: pl.BlockSpec((CHUNK, A_WIDTH), lambda i, c=c: (i, c // A_WIDTH))
    full = lambda a: pl.BlockSpec(a.shape, lambda i, n=a.ndim: (0,) * n)
    return _Part(body, (proj, proj, proj, lng, lnb, wm, bst),
                 [col(COL_AU), col(COL_AV), col(COL_AZ), full(lng), full(lnb), full(wm), full(bst)],
                 [pl.BlockSpec((CHUNK, A_WIDTH), lambda i: (i, 0))], [SDS((T, A_WIDTH), BF16)], [])


def _gmlp_bwd(proj, dy, lng, lnb, wm, wmt, bst):
    T = proj.shape[0]
    n = T // CHUNK

    def body(u_ref, v_ref, z_ref, dy_ref, lng_ref, lnb_ref, wm_ref, wmt_ref, bst_ref,
             da_ref, dwm_ref, dbst_ref, dlng_ref, dlnb_ref):
        @pl.when(pl.program_id(0) == 0)
        def _():
            dwm_ref[...] = jnp.zeros_like(dwm_ref)
            dbst_ref[...] = jnp.zeros_like(dbst_ref)
            dlng_ref[...] = jnp.zeros_like(dlng_ref)
            dlnb_ref[...] = jnp.zeros_like(dlnb_ref)

        lane = _lane((CHUNK, 128))
        dbst = dbst_ref[...]
        for pair in range(2):
            sl = slice(128 * pair, 128 * pair + 128)
            lng_p = lng_ref[:, sl]
            ug, dug, dvg, rstd, xh, vnb, mixed, lo = _gmlp_core(u_ref[:, sl], v_ref[:, sl], lng_p, lnb_ref[:, sl],
                                                                wm_ref, bst_ref, pair)
            sz, dsz = _silu_and_grad(z_ref[:, sl])
            dyv = dy_ref[:, sl]
            out = ug * mixed
            dz = dyv * out * dsz
            dout = dyv * sz
            du = dout * mixed * dug
            dmix = dout * ug
            g0, g1 = 2 * pair, 2 * pair + 1
            dm0 = jnp.where(lo, dmix, 0.0)
            dm1 = jnp.where(lo, 0.0, dmix)
            dbst = dbst + jnp.where(lane == g0, jnp.sum(dm0, axis=-1, keepdims=True), 0.0)
            dbst = dbst + jnp.where(lane == g1, jnp.sum(dm1, axis=-1, keepdims=True), 0.0)
            dwm_ref[g0] += _dot_nt(dm0.astype(BF16), vnb)
            dwm_ref[g1] += _dot_nt(dm1.astype(BF16), vnb)
            dmb = dmix.astype(BF16)
            dvn = jnp.where(lo, _dot(wmt_ref[g0], dmb), _dot(wmt_ref[g1], dmb))
            dlng_ref[:, sl] += jnp.sum(dvn * xh, axis=0, keepdims=True)
            dlnb_ref[:, sl] += jnp.sum(dvn, axis=0, keepdims=True)
            dxh = dvn * lng_p
            m1 = _gsum64(dxh) * (1.0 / 64)
            m2 = _gsum64(dxh * xh) * (1.0 / 64)
            dv = rstd * (dxh - m1 - xh * m2) * dvg
            da_ref[:, COL_AU + 128 * pair:COL_AU + 128 * pair + 128] = du.astype(BF16)
            da_ref[:, COL_AV + 128 * pair:COL_AV + 128 * pair + 128] = dv.astype(BF16)
            da_ref[:, COL_AZ + 128 * pair:COL_AZ + 128 * pair + 128] = dz.astype(BF16)
        dbst_ref[...] = dbst

        @pl.when(pl.program_id(0) == n - 1)
        def _():
            causal = _lane((CHUNK, CHUNK)) <= _row((CHUNK, CHUNK))
            for g in range(A_GROUPS):
                dwm_ref[g] = jnp.where(causal, dwm_ref[g], 0.0)

    col = lambda c: pl.BlockSpec((CHUNK, A_WIDTH), lambda i, c=c: (i, c // A_WIDTH))
    full = lambda a: pl.BlockSpec(a.shape, lambda i, n=a.ndim: (0,) * n)
    acc = lambda s: pl.BlockSpec(s, lambda i, n=len(s): (0,) * n)
    return _Part(body, (proj, proj, proj, dy, lng, lnb, wm, wmt, bst),
                 [col(COL_AU), col(COL_AV), col(COL_AZ), pl.BlockSpec((CHUNK, A_WIDTH), lambda i: (i, 0)),
                  full(lng), full(lnb), full(wm), full(wmt), full(bst)],
                 [pl.BlockSpec((CHUNK, 3 * A_WIDTH), lambda i: (i, 0)), acc((A_GROUPS, CHUNK, CHUNK)),
                  acc((CHUNK, 128)), acc((1, A_WIDTH)), acc((1, A_WIDTH))],
                 [SDS((T, 3 * A_WIDTH), BF16), SDS((A_GROUPS, CHUNK, CHUNK), F32), SDS((CHUNK, 128), F32),
                  SDS((1, A_WIDTH), F32), SDS((1, A_WIDTH), F32)], [])


def _hgrn_consts():
    r, c = _row((CHUNK, CHUNK)), _lane((CHUNK, CHUNK))
    same = (r >> SUB_SHIFT) == (c >> SUB_SHIFT)
    lsub = jnp.where(same & (c <= r), 1.0, 0.0).astype(BF16)
    usub = jnp.where(same & (c >= r), 1.0, 0.0).astype(BF16)
    bsub = jnp.where(same, 1.0, 0.0).astype(BF16)
    return lsub, usub, bsub


def _hgrn_gates(qv, zf, lbp):
    sq, dsq = _silu_and_grad(qv)
    qt = sq * Q_SCALE
    sg = _sigmoid(zf)
    sgn = _sigmoid(-zf)
    f = lbp + (1.0 - lbp) * sg
    g = jnp.log(jnp.maximum(f, F_FLOOR))
    kf = (1.0 - lbp) * sgn
    return qt, dsq, sg, sgn, f, g, kf


def _hgrn_intra_scores(qt, kf, b, mbd):
    rid = _row((SUB, 128))
    parts = []
    for s in range(SUB):
        e = jnp.exp(b - b[s:s + 1, :])
        parts.append(jnp.where(rid >= s, qt * kf[s:s + 1, :] * e, 0.0))
    return _dot(jnp.concatenate(parts, axis=0).astype(BF16), mbd)


def _hgrn_intra_out(a, v):
    o = jnp.zeros((SUB, 128), F32)
    for s in range(SUB):
        o = o + a[SUB * s:SUB * s + SUB, :] * v[s:s + 1, :]
    return o


def _hgrn_intra_bwd_scores(qt, kf, b, v, do, mbd):
    rid = _row((SUB, 128))
    ps, das, kes, es = [], [], [], []
    for s in range(SUB):
        e = jnp.where(rid >= s, jnp.exp(b - b[s:s + 1, :]), 0.0)
        ke = kf[s:s + 1, :] * e
        es.append(e)
        kes.append(ke)
        ps.append(qt * ke)
        das.append(do * v[s:s + 1, :])
    a = _dot(jnp.concatenate(ps, axis=0).astype(BF16), mbd)
    da = _dot(jnp.concatenate(das, axis=0).astype(BF16), mbd)
    return a, da, kes, es


def _hgrn_intra_bwd_grads(scores, qt, do, rsum):
    a, da, kes, es = scores
    dqt = jnp.zeros((SUB, 128), F32)
    xs, ys = [], []
    for s in range(SUB):
        da_s = da[SUB * s:SUB * s + SUB, :]
        dqt = dqt + da_s * kes[s]
        xs.append(a[SUB * s:SUB * s + SUB, :] * do)
        ys.append(da_s * qt * es[s])
    dv = _dot(rsum, jnp.concatenate(xs, axis=0).astype(BF16))
    dkf = _dot(rsum, jnp.concatenate(ys, axis=0).astype(BF16))
    return dqt, dkf, dv


def _hgrn_norm_gate(o, z, onorm):
    ms = _gsum64(o * o) * (1.0 / 64)
    r = lax.rsqrt(ms + NORM_EPS)
    xh = o * r
    sz, dsz = _silu_and_grad(z)
    return xh, r, sz, dsz, xh * onorm


def _hgrn_fwd(proj, lb, onorm):
    T = proj.shape[0]
    n = T // CHUNK
    nsub = CHUNK // SUB

    def body(q_ref, f_ref, i_ref, z_ref, lb_ref, on_ref, y_ref, o_ref, s0_ref, st_ref):
        @pl.when(pl.program_id(0) == 0)
        def _():
            st_ref[...] = jnp.zeros_like(st_ref)

        lsub, _, bsub = _hgrn_consts()
        mbd = _block_diag64()
        bdmask = mbd > 0
        rid = _row((CHUNK, 128))
        subs = [slice(SUB * sub, SUB * sub + SUB) for sub in range(nsub)]
        work = []
        for pair in range(2):
            sl = slice(128 * pair, 128 * pair + 128)
            qt, _, _, _, _, g, kf = _hgrn_gates(q_ref[:, sl], f_ref[:, sl], lb_ref[:, sl])
            work.append(dict(sl=sl, qt=qt, kf=kf, v=i_ref[:, sl], b=_dot3_left(lsub, g), bl=_dot3_left(bsub, g)))
        for w in work:
            qt, kf, v, b, bl = w["qt"], w["kf"], w["v"], w["b"], w["bl"]
            w["qh"] = (qt * jnp.exp(b)).astype(BF16)
            kh = kf * jnp.exp(bl - b)
            w["dec"] = jnp.exp(bl)
            vtb = v.T.astype(BF16)
            w["scores"] = [_hgrn_intra_scores(qt[rs], kf[rs], b[rs], mbd) for rs in subs]
            w["adds"] = [_dot(vtb, jnp.where((rid >> SUB_SHIFT) == sub, kh, 0.0).astype(BF16)) for sub in range(nsub)]
        for pair, w in enumerate(work):
            w["st"] = st_ref[pair]
            s0_ref[0, pair] = w["st"]
            w["outs"] = []
        for sub, rs in enumerate(subs):
            for w in work:
                w["outs"].append(_dot_nt(w["qh"][rs], w["st"].astype(BF16)) + _hgrn_intra_out(w["scores"][sub], w["v"][rs]))
                w["st"] = jnp.where(bdmask, w["st"] * w["dec"][SUB * sub:SUB * sub + 1, :] + w["adds"][sub], 0.0)
        for pair, w in enumerate(work):
            sl = w["sl"]
            st_ref[pair] = w["st"]
            o = jnp.concatenate(w["outs"], axis=0)
            o_ref[:, sl] = o
            _, _, sz, _, on = _hgrn_norm_gate(o, z_ref[:, sl], on_ref[:, sl])
            y_ref[:, sl] = (on * sz).astype(BF16)

    col = lambda c: pl.BlockSpec((CHUNK, B_WIDTH), lambda i, c=c: (i, c // B_WIDTH))
    full = lambda a: pl.BlockSpec(a.shape, lambda i, n=a.ndim: (0,) * n)
    return _Part(body, (proj, proj, proj, proj, lb, onorm),
                 [col(COL_BQ), col(COL_BF), col(COL_BI), col(COL_BZ), full(lb), full(onorm)],
                 [pl.BlockSpec((CHUNK, B_WIDTH), lambda i: (i, 0)), pl.BlockSpec((CHUNK, B_WIDTH), lambda i: (i, 0)),
                  pl.BlockSpec((1, 2, 128, 128), lambda i: (i, 0, 0, 0))],
                 [SDS((T, B_WIDTH), BF16), SDS((T, B_WIDTH), F32), SDS((n, 2, 128, 128), F32)],
                 [pltpu.VMEM((2, 128, 128), F32)])


def _hgrn_bwd(proj, dy, o_saved, s0, lb, onorm):
    T = proj.shape[0]
    n = T // CHUNK
    nsub = CHUNK // SUB

    def body(q_ref, f_ref, i_ref, z_ref, dy_ref, o_ref, s0_ref, lb_ref, on_ref,
             db_ref, dlb_ref, don_ref, dst_ref, sts_ref):
        @pl.when(pl.program_id(0) == 0)
        def _():
            dst_ref[...] = jnp.zeros_like(dst_ref)
            dlb_ref[...] = jnp.zeros_like(dlb_ref)
            don_ref[...] = jnp.zeros_like(don_ref)

        lsub, usub, bsub = _hgrn_consts()
        mbd = _block_diag64()
        bdmask = mbd > 0
        rsum = jnp.where((_lane((SUB, SUB * SUB)) >> SUB_SHIFT) == _row((SUB, SUB * SUB)), 1.0, 0.0).astype(BF16)
        subs = [slice(SUB * sub, SUB * sub + SUB) for sub in range(nsub)]
        work = []
        for pair in range(2):
            sl = slice(128 * pair, 128 * pair + 128)
            lbp = lb_ref[:, sl]
            qt, dsq, sg, sgn, f, g, kf = _hgrn_gates(q_ref[:, sl], f_ref[:, sl], lbp)
            w = dict(sl=sl, lbp=lbp, qt=qt, dsq=dsq, sg=sg, sgn=sgn, f=f, kf=kf, v=i_ref[:, sl],
                     b=_dot3_left(lsub, g), bl=_dot3_left(bsub, g))
            onp = on_ref[:, sl]
            xh, r, sz, dsz, on = _hgrn_norm_gate(o_ref[:, sl], z_ref[:, sl], onp)
            dyv = dy_ref[:, sl]
            w["dz"] = dyv * on * dsz
            don = dyv * sz
            cn = jnp.sum(don * xh, axis=0, keepdims=True)
            don_ref[...] += cn + pltpu.roll(cn, 64, axis=1)
            dxo = don * onp
            w["do"] = r * (dxo - xh * (_gsum64(dxo * xh) * (1.0 / 64)))
            work.append(w)
        for w in work:
            qt, kf, v, b, bl, do = w["qt"], w["kf"], w["v"], w["b"], w["bl"], w["do"]
            w["eb"] = jnp.exp(b)
            w["ekb"] = jnp.exp(bl - b)
            w["qhb"] = (qt * w["eb"]).astype(BF16)
            w["khb"] = (kf * w["ekb"]).astype(BF16)
            w["dec"] = jnp.exp(bl)
            w["vb"] = v.astype(BF16)
            w["dob"] = do.astype(BF16)
            w["scores"] = [_hgrn_intra_bwd_scores(qt[rs], kf[rs], b[rs], v[rs], do[rs], mbd) for rs in subs]
            w["st_adds"] = [_dot_tn(w["vb"][rs], w["khb"][rs]) for rs in subs]
            w["gst_adds"] = [_dot_tn(w["dob"][rs], w["qhb"][rs]) for rs in subs]
        for pair, w in enumerate(work):
            w["st"] = s0_ref[0, pair]
        for sub in range(nsub):
            for pair, w in enumerate(work):
                sts_ref[pair, sub] = w["st"]
                w["st"] = jnp.where(bdmask, w["st"] * w["dec"][SUB * sub:SUB * sub + 1, :] + w["st_adds"][sub], 0.0)
        for pair, w in enumerate(work):
            w["gst"] = dst_ref[pair]
            w["dqt_p"], w["dkf_p"], w["dv_p"], w["dbl_p"] = ([None] * nsub for _ in range(4))
        for sub in reversed(range(nsub)):
            rs = subs[sub]
            for pair, w in enumerate(work):
                gst = w["gst"]
                st_in = sts_ref[pair, sub]
                gb = gst.astype(BF16)
                dqh = _dot(w["dob"][rs], st_in.astype(BF16))
                dkh = _dot(w["vb"][rs], gb)
                dv_inter = _dot_nt(w["khb"][rs], gb)
                ddec = jnp.sum(gst * st_in, axis=0, keepdims=True)
                dec_row = w["dec"][SUB * sub:SUB * sub + 1, :]
                w["gst"] = jnp.where(bdmask, gst * dec_row + w["gst_adds"][sub], 0.0)
                dqt_i, dkf_i, dv_i = _hgrn_intra_bwd_grads(w["scores"][sub], w["qt"][rs], w["do"][rs], rsum)
                dkf_inter = dkh * w["ekb"][rs]
                w["dqt_p"][sub] = dqh * w["eb"][rs] + dqt_i
                w["dkf_p"][sub] = dkf_inter + dkf_i
                w["dv_p"][sub] = dv_inter + dv_i
                row = jnp.sum(w["kf"][rs] * dkf_inter, axis=0, keepdims=True) + ddec * dec_row
                w["dbl_p"][sub] = jnp.broadcast_to(row, (SUB, 128))
        for pair, w in enumerate(work):
            sl, lbp, sg, sgn, f = w["sl"], w["lbp"], w["sg"], w["sgn"], w["f"]
            dst_ref[pair] = w["gst"]
            dqt = jnp.concatenate(w["dqt_p"], axis=0)
            dkf = jnp.concatenate(w["dkf_p"], axis=0)
            dv = jnp.concatenate(w["dv_p"], axis=0)
            dg = _dot3_left(usub, w["qt"] * dqt - w["kf"] * dkf) + jnp.concatenate(w["dbl_p"], axis=0)
            df = jnp.where(f > F_FLOOR, dg / f, 0.0)
            dlb_ref[:, sl] += jnp.sum(df * (1.0 - sg) - dkf * sgn, axis=0, keepdims=True)
            dfl = (1.0 - lbp) * sg * sgn * (df - dkf)
            dq = dqt * Q_SCALE * w["dsq"]
            db_ref[:, 0 * B_WIDTH + 128 * pair:0 * B_WIDTH + 128 * pair + 128] = dq.astype(BF16)
            db_ref[:, 1 * B_WIDTH + 128 * pair:1 * B_WIDTH + 128 * pair + 128] = dfl.astype(BF16)
            db_ref[:, 2 * B_WIDTH + 128 * pair:2 * B_WIDTH + 128 * pair + 128] = dv.astype(BF16)
            db_ref[:, 3 * B_WIDTH + 128 * pair:3 * B_WIDTH + 128 * pair + 128] = w["dz"].astype(BF16)

    rev = lambda c: pl.BlockSpec((CHUNK, B_WIDTH), lambda i, c=c: (n - 1 - i, c // B_WIDTH))
    full = lambda a: pl.BlockSpec(a.shape, lambda i, n_=a.ndim: (0,) * n_)
    acc = lambda s: pl.BlockSpec(s, lambda i, n_=len(s): (0,) * n_)
    return _Part(body, (proj, proj, proj, proj, dy, o_saved, s0, lb, onorm),
                 [rev(COL_BQ), rev(COL_BF), rev(COL_BI), rev(COL_BZ),
                  pl.BlockSpec((CHUNK, B_WIDTH), lambda i: (n - 1 - i, 1)),
                  pl.BlockSpec((CHUNK, B_WIDTH), lambda i: (n - 1 - i, 0)),
                  pl.BlockSpec((1, 2, 128, 128), lambda i: (n - 1 - i, 0, 0, 0)), full(lb), full(onorm)],
                 [pl.BlockSpec((CHUNK, 4 * B_WIDTH), lambda i: (n - 1 - i, 0)), acc((1, B_WIDTH)), acc((1, 128))],
                 [SDS((T, 4 * B_WIDTH), BF16), SDS((1, B_WIDTH), F32), SDS((1, 128), F32)],
                 [pltpu.VMEM((2, 128, 128), F32), pltpu.VMEM((2, nsub, 128, 128), F32)])


def _lb_fwd(hgrn_lb):
    assert hgrn_lb.shape[0] == 2

    def body(x_ref, o_ref):
        x0, x1 = x_ref[0:1, :], x_ref[1:2, :]
        m = jnp.maximum(x0, x1)
        e0, e1 = jnp.exp(x0 - m), jnp.exp(x1 - m)
        p0, p1 = e0 / (e0 + e1), e1 / (e0 + e1)
        o_ref[0:1, :] = jnp.clip(p0 - p0, 0.0, 1.0 - 1e-6)
        o_ref[1:2, :] = jnp.clip((p0 + p1) - p0, 0.0, 1.0 - 1e-6)

    return pl.pallas_call(body, name="lb_fwd", out_shape=SDS(hgrn_lb.shape, F32))(hgrn_lb)


def _lb_bwd(hgrn_lb, dlb):
    def body(x_ref, d_ref, o_ref):
        x0, x1 = x_ref[0:1, :], x_ref[1:2, :]
        m = jnp.maximum(x0, x1)
        e0, e1 = jnp.exp(x0 - m), jnp.exp(x1 - m)
        p0, p1 = e0 / (e0 + e1), e1 / (e0 + e1)
        val = (p0 + p1) - p0
        dp1 = jnp.where((val > 0.0) & (val < 1.0 - 1e-6), d_ref[1:2, :], 0.0)
        inner = p1 * dp1
        o_ref[0:1, :] = p0 * (0.0 - inner)
        o_ref[1:2, :] = p1 * (dp1 - inner)

    return pl.pallas_call(body, name="lb_bwd", out_shape=SDS(hgrn_lb.shape, F32))(hgrn_lb, dlb)


def _fox_prep(proj, bf):
    T = proj.shape[0]
    n = T // CHUNK

    def body(q0_ref, q1_ref, k0_ref, k1_ref, v0_ref, v1_ref, fl_ref, bf_ref, qo_ref, ko_ref, vt_ref, carry_ref):
        for p, v_ref in enumerate((v0_ref, v0_ref, v1_ref, v1_ref)):
            vt_ref[p, 0] = v_ref[:, 128 * (p % 2):128 * (p % 2) + 128].T.astype(BF16)

        @pl.when(pl.program_id(0) == 0)
        def _():
            carry_ref[...] = jnp.zeros_like(carry_ref)

        ltri = jnp.where(_lane((CHUNK, CHUNK)) <= _row((CHUNK, CHUNK)), 1.0, 0.0).astype(BF16)
        lf = jax.nn.log_sigmoid(fl_ref[...] + bf_ref[...])
        c = _dot3_left(ltri, lf) + carry_ref[...]
        carry_ref[...] = c[CHUNK - 1:CHUNK, :]
        lane = _lane((CHUNK, 128))
        feat = lane < 64
        ones_q = (lane >= 67) & (lane <= 69)
        ones_k = (lane >= 64) & (lane <= 66)
        qrefs, krefs = (q0_ref, q1_ref), (k0_ref, k1_ref)
        for h in range(C_HEADS):
            blk = slice(128 * ((h // 2) % 2), 128 * ((h // 2) % 2) + 128)
            qp, kp = qrefs[h // 4][:, blk], krefs[h // 4][:, blk]
            if h % 2:
                qp, kp = pltpu.roll(qp, 64, axis=1), pltpu.roll(kp, 64, axis=1)
            ch = jnp.broadcast_to(c[:, h:h + 1], (CHUNK, 128))
            hi = ch.astype(BF16).astype(F32)
            r1 = ch - hi
            mid = r1.astype(BF16).astype(F32)
            lo = r1 - mid
            aq = jnp.where(lane == 64, hi, jnp.where(lane == 65, mid, jnp.where(lane == 66, lo,
                           jnp.where(ones_q, 1.0, 0.0))))
            ak = jnp.where(lane == 67, -hi, jnp.where(lane == 68, -mid, jnp.where(lane == 69, -lo,
                           jnp.where(ones_k, 1.0, 0.0))))
            qo_ref[:, 128 * h:128 * h + 128] = jnp.where(feat, qp * Q_SCALE, aq).astype(BF16)
            ko_ref[:, 128 * h:128 * h + 128] = jnp.where(feat, kp, ak).astype(BF16)

    w = 256
    col = lambda c: pl.BlockSpec((CHUNK, w), lambda i, c=c: (i, c // w))
    return _Part(body, (proj, proj, proj, proj, proj, proj, proj, bf),
                 [col(COL_CQ), col(COL_CQ + w), col(COL_CK), col(COL_CK + w), col(COL_CV), col(COL_CV + w),
                  pl.BlockSpec((CHUNK, 128), lambda i: (i, COL_CF // 128)), pl.BlockSpec((1, 128), lambda i: (0, 0))],
                 [pl.BlockSpec((CHUNK, C_HEADS * 128), lambda i: (i, 0))] * 2
                 + [pl.BlockSpec((C_HEADS // 2, 1, 128, CHUNK), lambda i: (0, i, 0, 0))],
                 [SDS((T, C_HEADS * 128), BF16)] * 2 + [SDS((C_HEADS // 2, n, 128, CHUNK), BF16)],
                 [pltpu.VMEM((1, 128), F32)])


FOX_TILE = 512
FOX_KEYS = 512


def _fox_mask(tk, tq, k0, q0):
    return (_row((tk, tq)) + (k0 - q0)) <= _lane((tk, tq))


def _ride_refs(ride, rest, n_out, n_scratch):
    n = ride.n if ride else 0
    srcs, rest = rest[:n], rest[n:]
    outs, rest = rest[:n_out], rest[n_out:]
    dsts, rest = rest[:n], rest[n:]
    return srcs, outs, dsts, rest[:n_scratch], rest[n_scratch:]


def _ride_start(ride, grid, srcs, dsts, sems):
    if ride:
        first = functools.reduce(lambda a, b: a & b, [pl.program_id(d) == 0 for d in range(len(grid))])
        pl.when(first)(lambda: ride.start(srcs, dsts, sems))


def _ride_wait(ride, grid, srcs, dsts, sems):
    if ride:
        last = functools.reduce(lambda a, b: a & b, [pl.program_id(d) == n - 1 for d, n in enumerate(grid)])
        pl.when(last)(lambda: ride.wait(srcs, dsts, sems))


def _fox_fwd(qt, kt, vt, proj, tag, ride=None):
    T = proj.shape[0]
    tq, tk = _tile(T, FOX_TILE), _tile(T, FOX_KEYS)
    nq, nsub = T // tq, tk // CHUNK
    npair = C_HEADS // 2

    def body(q_ref, k_ref, vt_ref, z_ref, *rest):
        ride_srcs, (o_ref, lse_ref, y_ref), ride_dsts, (acc_ref, st_ref, pt_ref), ride_sems = _ride_refs(ride, rest, 3, 3)
        i = pl.program_id(1)
        _ride_start(ride, (npair, nq), ride_srcs, ride_dsts, ride_sems)

        qs = (q_ref[:, 0:128], q_ref[:, 128:256])
        acc_ref[...] = jnp.zeros_like(acc_ref)
        pt_ref[...] = jnp.zeros_like(pt_ref)
        nfull = (i * tq) // tk

        def scores(j):
            kb = k_ref[pl.ds(pl.multiple_of(j * tk, tk), tk), :]
            return tuple(_dot_nt(kb[:, 128 * h:128 * h + 128], qs[h]) for h in range(2))

        def weigh(j, h):
            rows = slice(64 * h, 64 * h + 64)
            vth = jnp.concatenate([vt_ref[0, nsub * j + c, rows, :] for c in range(nsub)], axis=1)
            return _dot(vth, pt_ref[h])

        def block(j, carry, diagonal):
            nxt = () if diagonal else scores(j + 1)
            pvs = [weigh(jnp.maximum(j - 1, 0), h) for h in range(2)]
            new = []
            for h in range(2):
                m, l, alpha_prev = carry[3 * h:3 * h + 3]
                st = st_ref[h]
                if diagonal:
                    st = jnp.where(_fox_mask(tk, tq, j * tk, i * tq), st, -jnp.inf)
                m_new = jnp.maximum(m, _colreduce(st, jnp.maximum))
                pt = jnp.exp(st - m_new)
                alpha = jnp.exp(m - m_new)
                rows = slice(64 * h, 64 * h + 64)
                acc_ref[rows, :] = alpha_prev * acc_ref[rows, :] + pvs[h]
                pt_ref[h] = pt.astype(BF16)
                new += [m_new, alpha * l + _colreduce(pt, jnp.add), alpha]
            for h, st in enumerate(nxt):
                st_ref[h] = st
            return tuple(new)

        for h, st in enumerate(scores(0)):
            st_ref[h] = st
        init = (jnp.full((1, tq), -jnp.inf, F32), jnp.zeros((1, tq), F32), jnp.ones((1, tq), F32)) * 2
        carry = lax.fori_loop(0, nfull, lambda j, c: block(j, c, False), init)
        m0, l0, a0, m1, l1, a1 = block(nfull, carry, True)
        for h, alpha in enumerate((a0, a1)):
            rows = slice(64 * h, 64 * h + 64)
            acc_ref[rows, :] = alpha * acc_ref[rows, :] + weigh(nfull, h)
        inv = jnp.where(_row((128, tq)) < 64, 1.0 / l0, 1.0 / l1)
        o = (acc_ref[...] * inv).T
        o_ref[...] = o
        r8 = _row((8, tq))
        lse_ref[0, 0] = jnp.where(r8 == 0, m0 + jnp.log(l0), jnp.where(r8 == 1, m1 + jnp.log(l1), 0.0))
        sz, _ = _silu_and_grad(z_ref[...])
        y_ref[...] = (o * sz).astype(BF16)
        _ride_wait(ride, (npair, nq), ride_srcs, ride_dsts, ride_sems)

    blk = pl.BlockSpec((tq, 128), lambda p, i: (i, p))
    extra = ride or _ChipExchange("gather", ())
    return pl.pallas_call(
        body, name=f"fox_fwd_{tag}", grid=(npair, nq),
        in_specs=[pl.BlockSpec((tq, 256), lambda p, i: (i, p)), pl.BlockSpec((T, 256), lambda p, i: (0, p)),
                  pl.BlockSpec((1, T // CHUNK, 128, CHUNK), lambda p, i: (p, 0, 0, 0)),
                  pl.BlockSpec((tq, 128), lambda p, i: (i, COL_CZ // 128 + p))] + extra.in_specs,
        out_specs=[blk, pl.BlockSpec((1, 1, 8, tq), lambda p, i: (p, i, 0, 0)), blk] + extra.out_specs,
        out_shape=[SDS((T, C_WIDTH), F32), SDS((npair, nq, 8, tq), F32), SDS((T, C_WIDTH), BF16)] + extra.out_shape,
        scratch_shapes=[pltpu.VMEM((128, tq), F32), pltpu.VMEM((2, tk, tq), F32), pltpu.VMEM((2, tk, tq), BF16)]
        + (extra.scratch if ride else []),
        compiler_params=pltpu.CompilerParams(dimension_semantics=("arbitrary", "arbitrary"), vmem_limit_bytes=VMEM_LIMIT,
                                             has_side_effects=bool(ride)),
    )(qt, kt, vt, proj, *extra.sources)


def _fox_bwd_prep(proj, dy, o, qt, tag):
    T = proj.shape[0]
    tq = _tile(T, FOX_TILE)
    nq = T // tq

    def body(z0_ref, z1_ref, dy_ref, o_ref, q_ref, do_ref, dl_ref, dz_ref, dot_ref, qt_ref):
        sel = jnp.where((_lane((16, 128)) >> 6) == _row((16, 128)), 1.0, 0.0).astype(BF16)
        for p, z_ref in enumerate((z0_ref, z0_ref, z1_ref, z1_ref)):
            sl = slice(128 * p, 128 * p + 128)
            sz, dsz = _silu_and_grad(z_ref[:, 128 * (p % 2):128 * (p % 2) + 128])
            dyv, ov = dy_ref[:, sl], o_ref[:, sl]
            do = dyv * sz
            do_ref[:, sl] = do.astype(BF16)
            dot_ref[p, 0] = do.T.astype(BF16)
            dz_ref[:, sl] = (dyv * ov * dsz).astype(BF16)
            hi, mid, lo = _split3(do * ov)
            dl_ref[p, 0] = (_dot_nt(sel, hi) + _dot_nt(sel, mid) + _dot_nt(sel, lo))[0:8, :]
        for h in range(C_HEADS):
            qt_ref[h, 0] = q_ref[:, 128 * h:128 * h + 128].astype(F32).T.astype(BF16)

    w = 256
    blk = pl.BlockSpec((tq, C_WIDTH), lambda i: (i, 0))
    return pl.pallas_call(
        body, name=f"fox_bwd_prep_{tag}", grid=(nq,),
        in_specs=[pl.BlockSpec((tq, w), lambda i: (i, COL_CZ // w)), pl.BlockSpec((tq, w), lambda i: (i, COL_CZ // w + 1)),
                  pl.BlockSpec((tq, C_WIDTH), lambda i: (i, (A_WIDTH + B_WIDTH) // C_WIDTH)), blk,
                  pl.BlockSpec((tq, C_HEADS * 128), lambda i: (i, 0))],
        out_specs=[blk, pl.BlockSpec((C_HEADS // 2, 1, 8, tq), lambda i: (0, i, 0, 0)), blk,
                   pl.BlockSpec((C_HEADS // 2, 1, 128, tq), lambda i: (0, i, 0, 0)),
                   pl.BlockSpec((C_HEADS, 1, 128, tq), lambda i: (0, i, 0, 0))],
        out_shape=[SDS((T, C_WIDTH), BF16), SDS((C_HEADS // 2, nq, 8, tq), F32), SDS((T, C_WIDTH), BF16),
                   SDS((C_HEADS // 2, nq, 128, tq), BF16), SDS((C_HEADS, nq, 128, tq), BF16)],
        compiler_params=_params("parallel"),
    )(proj, proj, dy, o, qt)


def _fox_bwd(qt, kt, proj, do, lse, delta, dot, qtr, tag, ride=None):
    T = proj.shape[0]
    tq, tk = _tile(T, FOX_TILE), _tile(T, FOX_KEYS)
    nq, nk = T // tq, T // tk
    assert tq == tk
    npair = C_HEADS // 2

    def body(q_ref, k_ref, v_ref, do_ref, lse_ref, dl_ref, dot_ref, qtr_ref, *rest):
        ride_srcs, (dq_ref, dk_ref, dv_ref), ride_dsts, scratch, ride_sems = _ride_refs(ride, rest, 3, 5)
        dvt_ref, dkt_ref, sc_ref, pt_ref, ds_ref = scratch
        j = pl.program_id(1)
        first = (j * tk) // tq
        _ride_start(ride, (npair, nk), ride_srcs, ride_dsts, ride_sems)

        @pl.when(j == 0)
        def _():
            dq_ref[...] = jnp.zeros_like(dq_ref)

        dkt_ref[...] = jnp.zeros_like(dkt_ref)
        dvt_ref[...] = jnp.zeros_like(dvt_ref)
        ks = (k_ref[:, 0:128], k_ref[:, 128:256])
        kts = tuple(k.astype(F32).T.astype(BF16) for k in ks)
        vb = v_ref[...].astype(BF16)
        lo = _lane((tq, 128)) < 64

        def operands(i):
            q0 = pl.multiple_of(i * tq, tq)
            qb = q_ref[pl.ds(q0, tq), :]
            dob = do_ref[pl.ds(q0, tq), :]
            qhs = (qb[:, 0:128], qb[:, 128:256])
            dohs = (jnp.where(lo, dob, jnp.zeros_like(dob)), jnp.where(lo, jnp.zeros_like(dob), dob))
            return qhs, dohs

        def scores(i):
            qhs, dohs = operands(i)
            return tuple(_dot_nt(ks[h], qhs[h]) for h in range(2)) + tuple(_dot_nt(vb, dohs[h]) for h in range(2))

        def park(sc, slot):
            for a, s in enumerate(sc):
                sc_ref[slot, a] = s

        def grads(i):
            for h in range(2):
                rows = slice(64 * h, 64 * h + 64)
                dvt_ref[rows, :] += _dot_nt(dot_ref[0, i, rows, :], pt_ref[h])
                dkt_ref[h] += _dot_nt(qtr_ref[h, i], ds_ref[h])
                dq_ref[h, i] += _dot(kts[h], ds_ref[h])

        def block(i, slot, diagonal, opening):
            park(scores(jnp.minimum(i + 1, nq - 1)), 1 - slot)
            if not opening:
                grads(i - 1)
            lsev = lse_ref[0, i]
            dlv = dl_ref[0, i]
            for h in range(2):
                pt = jnp.exp(sc_ref[slot, h] - lsev[h:h + 1, :])
                if diagonal:
                    pt = jnp.where(_fox_mask(tk, tq, j * tk, i * tq), pt, 0.0)
                ds_ref[h] = (pt * (sc_ref[slot, 2 + h] - dlv[h:h + 1, :])).astype(BF16)
                pt_ref[h] = pt.astype(BF16)

        park(scores(first), 0)
        block(first, 0, True, True)
        rest = nq - 1 - first

        def two_steps(t, carry):
            block(first + 1 + 2 * t, 1, False, False)
            block(first + 2 + 2 * t, 0, False, False)
            return carry

        lax.fori_loop(0, rest // 2, two_steps, 0)
        pl.when(rest % 2 == 1)(lambda: block(nq - 1, 1, False, False))
        grads(nq - 1)
        dv_ref[...] = dvt_ref[...].T.astype(BF16)
        for h in range(2):
            dk_ref[:, 128 * h:128 * h + 128] = dkt_ref[h].T
        _ride_wait(ride, (npair, nk), ride_srcs, ride_dsts, ride_sems)

    full = lambda w: pl.BlockSpec((T, w), lambda p, j: (0, p))
    stat = pl.BlockSpec((1, nq, 8, tq), lambda p, j: (p, 0, 0, 0))
    extra = ride or _ChipExchange("gather", ())
    return pl.pallas_call(
        body, name=f"fox_bwd_{tag}", grid=(npair, nk),
        in_specs=[full(256), pl.BlockSpec((tk, 256), lambda p, j: (j, p)),
                  pl.BlockSpec((tk, 128), lambda p, j: (j, COL_CV // 128 + p)), full(128), stat, stat,
                  pl.BlockSpec((1, nq, 128, tq), lambda p, j: (p, 0, 0, 0)),
                  pl.BlockSpec((2, nq, 128, tq), lambda p, j: (p, 0, 0, 0))] + extra.in_specs,
        out_specs=[pl.BlockSpec((2, nq, 128, tq), lambda p, j: (p, 0, 0, 0)), pl.BlockSpec((tk, 256), lambda p, j: (j, p)),
                   pl.BlockSpec((tk, 128), lambda p, j: (j, p))] + extra.out_specs,
        out_shape=[SDS((C_HEADS, nq, 128, tq), F32), SDS((T, C_HEADS * 128), F32), SDS((T, C_WIDTH), BF16)]
        + extra.out_shape,
        scratch_shapes=[pltpu.VMEM((128, tk), F32), pltpu.VMEM((2, 128, tk), F32), pltpu.VMEM((2, 4, tk, tq), F32),
                        pltpu.VMEM((2, tk, tq), BF16), pltpu.VMEM((2, tk, tq), BF16)] + (extra.scratch if ride else []),
        compiler_params=pltpu.CompilerParams(dimension_semantics=("arbitrary", "arbitrary"), vmem_limit_bytes=VMEM_LIMIT,
                                             has_side_effects=bool(ride)),
    )(qt, kt, proj, do, lse, delta, dot, qtr, *extra.sources)


def _fox_bwd_post(dqt, dkt, proj, bf, tag):
    T = proj.shape[0]
    tq = _tile(T, FOX_TILE)
    n = T // tq

    def body(dq_ref, dk_ref, fl_ref, bf_ref, oq_ref, ok_ref, ofl_ref, dbf_ref, carry_ref):
        @pl.when(pl.program_id(0) == 0)
        def _():
            carry_ref[...] = jnp.zeros_like(carry_ref)
            dbf_ref[...] = jnp.zeros_like(dbf_ref)

        lane = _lane((tq, 128))
        lo = lane < 64
        dqs = [dq_ref[h, 0].T for h in range(C_HEADS)]
        dc = jnp.zeros((tq, 128), F32)
        for h in range(C_HEADS):
            dc = dc + jnp.where(lane == h, dqs[h][:, 64:65] - dk_ref[:, 128 * h + 67:128 * h + 68], 0.0)
        utri = jnp.where(_lane((tq, tq)) >= _row((tq, tq)), 1.0, 0.0).astype(BF16)
        dlf = _dot3_left(utri, dc) + carry_ref[...]
        carry_ref[...] = dlf[0:1, :]
        dfl = jnp.where(lane < C_HEADS, dlf * _sigmoid(-(fl_ref[...] + bf_ref[...])), 0.0)
        ofl_ref[...] = dfl.astype(BF16)
        dbf_ref[...] += jnp.sum(dfl, axis=0, keepdims=True)
        for p in range(C_HEADS // 2):
            a, b = 128 * (2 * p), 128 * (2 * p + 1)
            oq_ref[:, 128 * p:128 * p + 128] = (
                jnp.where(lo, dqs[2 * p], pltpu.roll(dqs[2 * p + 1], 64, axis=1)) * Q_SCALE).astype(BF16)
            ok_ref[:, 128 * p:128 * p + 128] = jnp.where(
                lo, dk_ref[:, a:a + 128], pltpu.roll(dk_ref[:, b:b + 128], 64, axis=1)).astype(BF16)

    rev = lambda w: pl.BlockSpec((tq, w), lambda i: (n - 1 - i, 0))
    return pl.pallas_call(
        body, name=f"fox_bwd_post_{tag}", grid=(n,),
        in_specs=[pl.BlockSpec((C_HEADS, 1, 128, tq), lambda i: (0, n - 1 - i, 0, 0)), rev(C_HEADS * 128),
                  pl.BlockSpec((tq, 128), lambda i: (n - 1 - i, COL_CF // 128)), pl.BlockSpec((1, 128), lambda i: (0, 0))],
        out_specs=[rev(C_WIDTH), rev(C_WIDTH), rev(128), pl.BlockSpec((1, 128), lambda i: (0, 0))],
        out_shape=[SDS((T, C_WIDTH), BF16), SDS((T, C_WIDTH), BF16), SDS((T, 128), BF16), SDS((1, 128), F32)],
        scratch_shapes=[pltpu.VMEM((1, 128), F32)], compiler_params=_params("arbitrary"),
    )(dqt, dkt, proj, bf)


def _adamw_math(w, g, m, v):
    m = ADAM_B1 * m + (1.0 - ADAM_B1) * g
    v = ADAM_B2 * v + (1.0 - ADAM_B2) * (g * g)
    delta = -ADAM_LR * ((m / ADAM_C1) / (jnp.sqrt(v / ADAM_C2) + ADAM_EPS) + ADAM_WD * w)
    return delta, m, v


def _adamw_pair(w, m, v, ga, gb, name):
    n0 = w.shape[0]
    most = max(1, ADAMW_BLOCK_BYTES // (4 * math.prod(w.shape[1:])))
    t0 = max(t for t in range(1, min(n0, most) + 1) if n0 % t == 0)

    def body(w_ref, m_ref, v_ref, ga_ref, gb_ref, g_ref, d_ref, nm_ref, nv_ref):
        g = ga_ref[...] + gb_ref[...]
        g_ref[...] = g
        d_ref[...], nm_ref[...], nv_ref[...] = _adamw_math(w_ref[...], g, m_ref[...], v_ref[...])

    blk = pl.BlockSpec((t0,) + w.shape[1:], lambda i: (i, 0, 0))
    return pl.pallas_call(
        body, name=name, grid=(n0 // t0,), in_specs=[blk] * 5, out_specs=[blk] * 4,
        out_shape=[SDS(w.shape, F32)] * 4, compiler_params=_params("parallel"),
    )(w, m, v, ga, gb)


def _adamw_small(ws, ms, vs, gall):
    offs = _small_offsets()
    n = len(ws)

    def body(*refs):
        w_refs, m_refs, v_refs, g_ref = refs[:n], refs[n:2 * n], refs[2 * n:3 * n], refs[3 * n]
        outs = refs[3 * n + 1:]

        def total(off, rows):
            g = g_ref[0, off:off + rows, :]
            for dev in range(1, N_DEV):
                g = g + g_ref[dev, off:off + rows, :]
            return g

        for k in range(n):
            g = total(offs[k], ws[k].shape[0])
            go_ref, d_ref, nm_ref, nv_ref = outs[4 * k:4 * k + 4]
            go_ref[...] = g
            d_ref[...], nm_ref[...], nv_ref[...] = _adamw_math(w_refs[k][...], g, m_refs[k][...], v_refs[k][...])
        outs[4 * n][...] = total(offs[n], 1)

    shapes = [SDS(w.shape, F32) for w in ws for _ in range(4)] + [SDS((1, 128), F32)]
    res = pl.pallas_call(body, name="adamw_small", out_shape=shapes,
                         compiler_params=pltpu.CompilerParams(vmem_limit_bytes=VMEM_LIMIT))(*ws, *ms, *vs, gall)
    return [res[4 * k:4 * k + 4] for k in range(n)], res[4 * n]


def _pack_grads(dlng, dlnb, dwm, dbst, dlb, donorm, dbf, dfinal, loss_part):
    offs = _small_offsets()
    base = offs[1]
    L = len(dwm)
    assert L == 2

    def body(*refs):
        lng, lnb, wm, bst, on, bf = (refs[L * a:L * a + L] for a in range(6))
        lb_ref, fin_ref, loss_ref, o_ref = refs[6 * L:]
        o_ref[...] = jnp.zeros_like(o_ref)
        lane = _lane((1, 128))
        for l in range(L):
            for j in range(2):
                o_ref[offs[1] - base + 2 * l + j:offs[1] - base + 2 * l + j + 1, :] = lng[l][:, 128 * j:128 * j + 128]
                o_ref[offs[2] - base + 2 * l + j:offs[2] - base + 2 * l + j + 1, :] = lnb[l][:, 128 * j:128 * j + 128]
                o_ref[offs[5] - base + 2 * l + j:offs[5] - base + 2 * l + j + 1, :] = lb_ref[l:l + 1, 128 * j:128 * j + 128]
            for g in range(A_GROUPS):
                row = offs[3] - base + (A_GROUPS * l + g) * CHUNK
                o_ref[row:row + CHUNK, :] = wm[l][g]
            o_ref[offs[4] - base + A_GROUPS * l:offs[4] - base + A_GROUPS * (l + 1), :] = bst[l][...].T[0:A_GROUPS, :]
        o_ref[offs[6] - base:offs[6] - base + 1, :] = jnp.where(lane < 64, on[0][...], pltpu.roll(on[1][...], 64, axis=1))
        o_ref[offs[7] - base:offs[7] - base + 1, :] = jnp.where(
            lane < C_HEADS, bf[0][...], jnp.where(lane < 2 * C_HEADS, pltpu.roll(bf[1][...], C_HEADS, axis=1), 0.0))
        for j in range(D_MODEL // 128):
            o_ref[offs[8] - base + j:offs[8] - base + j + 1, :] = fin_ref[:, 128 * j:128 * j + 128]
        o_ref[offs[9] - base:offs[9] - base + 1, :] = loss_ref[...]

    rows = offs[9] + 8 - base
    return pl.pallas_call(body, name="pack_grads", out_shape=SDS((rows, 128), F32))(
        *dlng, *dlnb, *dwm, *dbst, *donorm, *dbf, dlb, dfinal, loss_part)


def _sum_chips(layers, name, layer_major):
    _, R, C = layers[0].shape
    L = len(layers)
    tc = _tile(C, 256)

    def body(*refs):
        o_ref = refs[-1]
        for l, p_ref in enumerate(refs[:-1]):
            p = [p_ref[k].astype(F32) for k in range(N_CHIPS)]
            s = ((p[0] + p[1]) + p[2]) + p[3]
            if layer_major:
                o_ref[l] = s
            else:
                o_ref[:, l, :] = s

    out = (L, R, C) if layer_major else (R, L, C)
    out_blk = (L, R, tc) if layer_major else (R, L, tc)
    return pl.pallas_call(
        body, name=name, grid=(C // tc,),
        in_specs=[pl.BlockSpec((N_CHIPS, R, tc), lambda i: (0, 0, i))] * L,
        out_specs=pl.BlockSpec(out_blk, lambda i: (0, 0, i)), out_shape=SDS(out, F32),
        compiler_params=_params("parallel"),
    )(*layers)


ANY = pl.BlockSpec(memory_space=pl.ANY)


def _mesh_pos():
    return lax.axis_index("x"), lax.axis_index("y"), lax.axis_index("c")


def _other_chips(x, y):
    return [(1 - x, y), (x, 1 - y), (1 - x, 1 - y)]


class _ChipExchange:
    def __init__(self, mode, sources):
        assert mode in ("gather", "scatter")
        self.mode, self.sources = mode, tuple(sources)
        self.n = len(self.sources)
        self.in_specs = [ANY] * self.n
        self.out_specs = [ANY] * self.n
        self.out_shape = [SDS(((N_CHIPS,) + s.shape) if mode == "gather" else s.shape, s.dtype) for s in self.sources]
        self.scratch = [pltpu.SemaphoreType.DMA((3 * self.n,)), pltpu.SemaphoreType.DMA((3 * self.n,)),
                        pltpu.SemaphoreType.DMA((self.n,))]

    def _copies(self, srcs, dsts, send_sems, recv_sems, local_sems):
        x, y, c = _mesh_pos()
        me = 2 * x + y
        view = (lambda r, chip: r) if self.mode == "gather" else (lambda r, chip: r.at[chip])
        local = [pltpu.make_async_copy(view(s, me), d.at[me], local_sems.at[a]) for a, (s, d) in enumerate(zip(srcs, dsts))]
        sends, recvs = [], []
        for j, (px, py) in enumerate(_other_chips(x, y)):
            peer = 2 * px + py
            for a, (s, d) in enumerate(zip(srcs, dsts)):
                sems = dict(send_sem=send_sems.at[self.n * j + a], recv_sem=recv_sems.at[self.n * j + a],
                            device_id=(px, py, c), device_id_type=MESH_ID)
                sends.append(pltpu.make_async_remote_copy(src_ref=view(s, peer), dst_ref=d.at[me], **sems))
                recvs.append(pltpu.make_async_remote_copy(src_ref=view(s, me), dst_ref=d.at[peer], **sems))
        return local, sends, recvs

    def start(self, srcs, dsts, sems):
        local, sends, _ = self._copies(srcs, dsts, *sems)
        for cp in local + sends:
            cp.start()

    def wait(self, srcs, dsts, sems):
        local, sends, recvs = self._copies(srcs, dsts, *sems)
        for cp in recvs:
            cp.wait_recv()
        for cp in sends:
            cp.wait_send()
        for cp in local:
            cp.wait()


def _gather_halves(w, tag):
    R, C = w.shape
    H = R // 2

    def body(w_ref, g_ref, send_sems, recv_sems, pass_send, pass_recv, local_sem):
        x, y, c = _mesh_pos()
        me = 2 * x + y
        mine, theirs = pl.ds(c * H, H), pl.ds((1 - c) * H, H)
        own = pltpu.make_async_copy(w_ref, g_ref.at[me], local_sem)
        own.start()

        def fetch(j, px, py, src, dst):
            return pltpu.make_async_remote_copy(src_ref=src, dst_ref=dst, send_sem=send_sems.at[j], recv_sem=recv_sems.at[j],
                                                device_id=(px, py, c), device_id_type=MESH_ID)

        def hand(j, rows, peer):
            return pltpu.make_async_remote_copy(src_ref=g_ref.at[peer, rows], dst_ref=g_ref.at[peer, rows],
                                                send_sem=pass_send.at[j], recv_sem=pass_recv.at[j],
                                                device_id=(x, y, 1 - c), device_id_type=MESH_ID)

        chips = _other_chips(x, y)
        sends = [fetch(j, px, py, w_ref.at[mine], g_ref.at[me, mine]) for j, (px, py) in enumerate(chips)]
        for cp in sends:
            cp.start()
        passed = []
        for j, (px, py) in enumerate(chips):
            peer = 2 * px + py
            fetch(j, px, py, w_ref.at[mine], g_ref.at[peer, mine]).wait_recv()
            passed.append(hand(j, mine, peer))
            passed[-1].start()
        for j, (px, py) in enumerate(chips):
            hand(j, theirs, 2 * px + py).wait_recv()
        for cp in sends + passed:
            cp.wait_send()
        own.wait()

    return pl.pallas_call(
        body, name=f"gather_halves_{tag}", in_specs=[ANY], out_specs=ANY, out_shape=SDS((N_CHIPS, R, C), w.dtype),
        scratch_shapes=[pltpu.SemaphoreType.DMA((3,)), pltpu.SemaphoreType.DMA((3,)), pltpu.SemaphoreType.DMA((3,)),
                        pltpu.SemaphoreType.DMA((3,)), pltpu.SemaphoreType.DMA],
        compiler_params=pltpu.CompilerParams(has_side_effects=True),
    )(w)


class _DeviceGather:
    def __init__(self, source):
        self.sources, self.n = (source,), 1
        self.in_specs, self.out_specs = [ANY], [ANY]
        self.out_shape = [SDS((N_DEV,) + source.shape, source.dtype)]
        self.scratch = [pltpu.SemaphoreType.DMA((N_DEV - 1,)), pltpu.SemaphoreType.DMA((N_DEV - 1,)),
                        pltpu.SemaphoreType.DMA((1,))]

    def _copies(self, srcs, dsts, send_sems, recv_sems, local_sems):
        (src,), (dst,) = srcs, dsts
        x, y, c = _mesh_pos()
        me = 4 * x + 2 * y + c
        local = [pltpu.make_async_copy(src, dst.at[me], local_sems.at[0])]
        sends, recvs = [], []
        for k in range(1, N_DEV):
            px, py, pc = (1 - x) if k & 4 else x, (1 - y) if k & 2 else y, (1 - c) if k & 1 else c
            sems = dict(send_sem=send_sems.at[k - 1], recv_sem=recv_sems.at[k - 1], device_id=(px, py, pc),
                        device_id_type=MESH_ID)
            sends.append(pltpu.make_async_remote_copy(src_ref=src, dst_ref=dst.at[me], **sems))
            recvs.append(pltpu.make_async_remote_copy(src_ref=src, dst_ref=dst.at[4 * px + 2 * py + pc], **sems))
        return local, sends, recvs

    start = _ChipExchange.start
    wait = _ChipExchange.wait


def _gather_devices(a, name):
    ex = _DeviceGather(a)

    def body(a_ref, g_ref, *sems):
        ex.start((a_ref,), (g_ref,), sems)
        ex.wait((a_ref,), (g_ref,), sems)

    return pl.pallas_call(
        body, name=name, in_specs=ex.in_specs, out_specs=ex.out_specs[0], out_shape=ex.out_shape[0],
        scratch_shapes=ex.scratch, compiler_params=pltpu.CompilerParams(has_side_effects=True),
    )(a)


def _swap_cores(pin, pout):
    def body(pin_ref, pout_ref, oin_ref, oout_ref, send_sems, recv_sems):
        x, y, c = _mesh_pos()
        cps = [pltpu.make_async_remote_copy(src_ref=src, dst_ref=dst, send_sem=send_sems.at[a], recv_sem=recv_sems.at[a],
                                            device_id=(x, y, 1 - c), device_id_type=MESH_ID)
               for a, (src, dst) in enumerate(((pin_ref, oin_ref), (pout_ref, oout_ref)))]
        for cp in cps:
            cp.start()
        for cp in cps:
            cp.wait()

    return pl.pallas_call(
        body, name="swap_cores", in_specs=[ANY, ANY], out_specs=[ANY, ANY],
        out_shape=[SDS(pin.shape, F32), SDS(pout.shape, F32)],
        scratch_shapes=[pltpu.SemaphoreType.DMA((2,)), pltpu.SemaphoreType.DMA((2,))],
        compiler_params=pltpu.CompilerParams(has_side_effects=True),
    )(pin, pout)


PACK_TILE = 8 * 128


def _pack_rows(size):
    return (size + PACK_TILE - 1) // PACK_TILE * 8


def _small_offsets():
    offs = [0]
    for _, shape in SMALL_PARAMS:
        offs.append(offs[-1] + _pack_rows(math.prod(shape)))
    return offs


def _rows_view(a):
    flat = a.reshape(-1)
    return jnp.pad(flat, (0, (-flat.size) % 128)).reshape(-1, 128)


def _from_rows(rows, shape):
    return rows.reshape(-1)[:math.prod(shape)].reshape(shape)


def _layer_consts(l, gmlp_ln_g, gmlp_ln_b, gmlp_w_s, gmlp_b_s, hgrn_onorm_g, fox_b_f):
    causal = jnp.tril(jnp.ones((CHUNK, CHUNK), bool))
    wm = jnp.where(causal[None], gmlp_w_s[l], 0.0)
    return dict(
        lng=gmlp_ln_g[l].reshape(1, A_WIDTH), lnb=gmlp_ln_b[l].reshape(1, A_WIDTH),
        wm=wm.astype(BF16), wmt=jnp.swapaxes(wm, 1, 2).astype(BF16),
        bst=jnp.pad(gmlp_b_s[l].T, ((0, 0), (0, 128 - A_GROUPS))),
        onorm=jnp.tile(hgrn_onorm_g[l], 4).reshape(1, B_WIDTH),
        bf=jnp.pad(fox_b_f[l], (0, 128 - C_HEADS)).reshape(1, 128),
    )


def kernel(x, norm_g, w_in, w_out, gmlp_ln_g, gmlp_ln_b, gmlp_w_s, gmlp_b_s, hgrn_lb, hgrn_onorm_g, fox_b_f, final_norm_g, loss_target, m_norm_g, m_w_in, m_w_out, m_gmlp_ln_g, m_gmlp_ln_b, m_gmlp_w_s, m_gmlp_b_s, m_hgrn_lb, m_hgrn_onorm_g, m_fox_b_f, m_final_norm_g, v_norm_g, v_w_in, v_w_out, v_gmlp_ln_g, v_gmlp_ln_b, v_gmlp_w_s, v_gmlp_b_s, v_hgrn_lb, v_hgrn_onorm_g, v_fox_b_f, v_final_norm_g):
    T = x.shape[1]
    shard_in = w_in.shape[2]
    shard_out = w_out.shape[1]
    xs = x.reshape(T, D_MODEL)
    tgt = loss_target.reshape(T, D_MODEL)

    w_in_b, w_out_b = w_in.astype(BF16), w_out.astype(BF16)

    def full_w_in(gathered):
        return jnp.concatenate([gathered[k] for k in range(N_CHIPS)] + [jnp.zeros((D_MODEL, D_IN_PAD - D_IN), BF16)], axis=-1)

    lb_all = _lb_fwd(hgrn_lb)
    consts = [_layer_consts(l, gmlp_ln_g, gmlp_ln_b, gmlp_w_s, gmlp_b_s, hgrn_onorm_g, fox_b_f) for l in range(DEPTH)]

    saved = []
    xl = xs
    w_in_l = full_w_in(_gather_halves(w_in_b[0], "w_in_l0"))
    for l in range(DEPTH):
        cs = consts[l]
        tag = f"l{l}"
        h, proj = _inproj(xl, norm_g[l].reshape(1, D_MODEL), w_in_l, tag)
        (ya,), (yb, ob, s0), (qt, kt, vt) = _run_parts(
            [_gmlp_fwd(proj, cs["lng"], cs["lnb"], cs["wm"], cs["bst"]),
             _hgrn_fwd(proj, lb_all[l].reshape(1, B_WIDTH), cs["onorm"]), _fox_prep(proj, cs["bf"])],
            (T // CHUNK,), f"mix_fwd_{tag}")
        ride = _ChipExchange("gather", (w_out_b[l],) + ((w_in_b[l + 1],) if l + 1 < DEPTH else ()))
        oc, lse, yc, *gathered = _fox_fwd(qt, kt, vt, proj, tag, ride)
        w_out_l = gathered[0].reshape(N_CHIPS * shard_out, D_MODEL)
        saved.append(dict(x=xl, h=h, proj=proj, ya=ya, yb=yb, yc=yc, ob=ob, s0=s0, qt=qt, kt=kt, oc=oc, lse=lse,
                          w_in=w_in_l, w_out=w_out_l))
        xl = _outproj(xl, ya, yb, yc, w_out_l, tag)
        if l + 1 < DEPTH:
            w_in_l = full_w_in(gathered[1])

    dx, loss_part, d_final = _loss_head(xl, final_norm_g.reshape(1, D_MODEL), tgt)

    g_small = {}
    dlb_rows, rin, rout = [None] * DEPTH, [None] * DEPTH, [None] * DEPTH
    slabs_in = None
    for l in reversed(range(DEPTH)):
        cs, sv = consts[l], saved[l]
        tag = f"l{l}"
        proj = sv["proj"]
        dy, dw_out = _outproj_bwd(dx, sv["ya"], sv["yb"], sv["yc"], sv["w_out"], tag)
        (da, dwm, dbst, dlng, dlnb), (db, dlb_rows[l], donorm) = _run_parts(
            [_gmlp_bwd(proj, dy, cs["lng"], cs["lnb"], cs["wm"], cs["wmt"], cs["bst"]),
             _hgrn_bwd(proj, dy, sv["ob"], sv["s0"], lb_all[l].reshape(1, B_WIDTH), cs["onorm"])],
            (T // CHUNK,), f"mix_bwd_{tag}")
        do, delta, dzc, dot, qtr = _fox_bwd_prep(proj, dy, sv["oc"], sv["qt"], tag)
        slabs_out = dw_out.reshape(N_CHIPS, shard_out, D_MODEL).astype(BF16)
        ride = _ChipExchange("scatter", (slabs_out,) + ((slabs_in,) if slabs_in is not None else ()))
        dqt, dkt, dvc, *received = _fox_bwd(sv["qt"], sv["kt"], proj, do, sv["lse"], delta, dot, qtr, tag, ride)
        rout[l] = received[0]
        if slabs_in is not None:
            rin[l + 1] = received[1]
        dqc, dkc, dflc, dbf = _fox_bwd_post(dqt, dkt, proj, cs["bf"], tag)
        g_small[l] = dict(ln_g=dlng, ln_b=dlnb, w_s=dwm, b_s=dbst, onorm=donorm, bf=dbf)
        dproj = [da, db, dqc, dkc, dvc, dzc, dflc]
        if l == 0:
            d_hgrn_lb = _lb_bwd(hgrn_lb, jnp.concatenate(dlb_rows, axis=0))
            per_layer = lambda key: [g_small[k][key] for k in range(DEPTH)]
            early = _pack_grads(per_layer("ln_g"), per_layer("ln_b"), per_layer("w_s"), per_layer("b_s"), d_hgrn_lb,
                                per_layer("onorm"), per_layer("bf"), d_final, loss_part)
            dw_in, rearly = _dw_in(sv["h"], dproj, D_IN_PAD, tag, _DeviceGather(early))
        else:
            dw_in = _dw_in(sv["h"], dproj, D_IN_PAD, tag)
        slabs_in = dw_in[:N_CHIPS * shard_in].reshape(N_CHIPS, shard_in, D_MODEL)
        ride = _ChipExchange("scatter", (slabs_in,)) if l == 0 else None
        dx, dng, *received = _dx_in(sv["x"], norm_g[l].reshape(1, D_MODEL), dx, dproj, sv["w_in"], tag, ride)
        if l == 0:
            rin[0] = received[0]
        g_small[l]["norm_g"] = dng.reshape(D_MODEL // 128, 128)
    grad_x = dx.reshape(x.shape)
    rlate = _gather_devices(jnp.concatenate([g_small[l]["norm_g"] for l in range(DEPTH)]), "gather_norm_grads")
    rsmall = jnp.concatenate([rlate, rearly], axis=1)

    pin, pout = _sum_chips(rin, "sum_chips_w_in", False), _sum_chips(rout, "sum_chips_w_out", True)
    oin, oout = _swap_cores(pin, pout)
    to_view = lambda a: jnp.transpose(a, (2, 0, 1))
    g_w_in, d_w_in, nm_w_in, nv_w_in = [
        jnp.transpose(o, (1, 2, 0))
        for o in _adamw_pair(to_view(w_in), to_view(m_w_in), to_view(v_w_in), pin, oin, "adamw_w_in")]
    g_w_out, d_w_out, nm_w_out, nv_w_out = _adamw_pair(w_out, m_w_out, v_w_out, pout, oout, "adamw_w_out")

    small_w = [norm_g, gmlp_ln_g, gmlp_ln_b, gmlp_w_s, gmlp_b_s, hgrn_lb, hgrn_onorm_g, fox_b_f, final_norm_g]
    small_m = [m_norm_g, m_gmlp_ln_g, m_gmlp_ln_b, m_gmlp_w_s, m_gmlp_b_s, m_hgrn_lb, m_hgrn_onorm_g, m_fox_b_f, m_final_norm_g]
    small_v = [v_norm_g, v_gmlp_ln_g, v_gmlp_ln_b, v_gmlp_w_s, v_gmlp_b_s, v_hgrn_lb, v_hgrn_onorm_g, v_fox_b_f, v_final_norm_g]
    views = lambda ps: [_rows_view(p) for p in ps]
    per_param, loss_row = _adamw_small(views(small_w), views(small_m), views(small_v), rsmall)
    sg, sd, sm, sv_ = [[_from_rows(per_param[k][a], shape) for k, (_, shape) in enumerate(SMALL_PARAMS)] for a in range(4)]
    loss = loss_row[0, 0]

    def order(big_in, big_out, small):
        return [small[0], big_in, big_out] + small[1:]

    return (loss, grad_x, *order(g_w_in, g_w_out, sg), *order(d_w_in, d_w_out, sd), *order(nm_w_in, nm_w_out, sm),
            *order(nv_w_in, nv_w_out, sv_))
```

```python
import collections
import functools
import math

import jax
import jax.numpy as jnp
from jax import lax
from jax.experimental import pallas as pl
from jax.experimental.pallas import tpu as pltpu

F32 = jnp.float32
BF16 = jnp.bfloat16
SDS = jax.ShapeDtypeStruct
MESH_ID = pl.DeviceIdType.MESH

D_MODEL = 1024
DEPTH = 2
A_WIDTH = 256
A_GROUPS = 4
B_WIDTH = 256
C_WIDTH = 512
C_HEADS = 8
D_IN = 3848
D_IN_PAD = 4096
CHUNK = 128
SUB = 16
SUB_SHIFT = 4
NORM_EPS = 1e-6
F_FLOOR = 1e-30
COL_AU, COL_AV, COL_AZ = 0, 256, 512
COL_BQ, COL_BF, COL_BI, COL_BZ = 768, 1024, 1280, 1536
COL_CQ, COL_CK, COL_CV, COL_CZ, COL_CF = 1792, 2304, 2816, 3328, 3840
HEAD_LANES = 128
Q_SCALE = 0.125
ADAM_LR, ADAM_B1, ADAM_B2, ADAM_EPS, ADAM_WD, ADAM_STEP = 0.001, 0.9, 0.999, 1e-08, 0.01, 10
ADAM_C1 = 1.0 - ADAM_B1 ** ADAM_STEP
ADAM_C2 = 1.0 - ADAM_B2 ** ADAM_STEP
VMEM_LIMIT = 56 * 1024 * 1024
ADAMW_BLOCK_BYTES = 1 << 20
N_CHIPS = 4
N_DEV = 8

SMALL_PARAMS = (
    ("norm_g", (DEPTH, D_MODEL)), ("gmlp_ln_g", (DEPTH, 4, 64)), ("gmlp_ln_b", (DEPTH, 4, 64)),
    ("gmlp_w_s", (DEPTH, 4, 128, 128)), ("gmlp_b_s", (DEPTH, 4, 128)), ("hgrn_lb", (DEPTH, 256)),
    ("hgrn_onorm_g", (DEPTH, 64)), ("fox_b_f", (DEPTH, 8)), ("final_norm_g", (D_MODEL,)),
)


def _tile(n, pref):
    t = min(n, pref)
    assert n % t == 0, (n, pref)
    return t


def _params(*sem):
    return pltpu.CompilerParams(dimension_semantics=sem, vmem_limit_bytes=VMEM_LIMIT)


_Part = collections.namedtuple("_Part", "body operands in_specs out_specs out_shape scratch")


def _run_parts(parts, grid, name):
    counts = [(len(p.operands), len(p.out_shape), len(p.scratch)) for p in parts]

    def body(*refs):
        ins, outs, scr = [], [], []
        pos = 0
        for group, k in ((ins, 0), (outs, 1), (scr, 2)):
            for c in counts:
                group.append(refs[pos:pos + c[k]])
                pos += c[k]
        for p, i, o, s in zip(parts, ins, outs, scr):
            p.body(*i, *o, *s)

    flat = lambda key: [x for p in parts for x in getattr(p, key)]
    res = pl.pallas_call(
        body, name=name, grid=grid, in_specs=flat("in_specs"), out_specs=flat("out_specs"), out_shape=flat("out_shape"),
        scratch_shapes=flat("scratch"), compiler_params=_params(*(("arbitrary",) * len(grid))),
    )(*flat("operands"))
    out, pos = [], 0
    for c in counts:
        out.append(list(res[pos:pos + c[1]]))
        pos += c[1]
    return out


def _dot(a, b):
    return jnp.dot(a, b, preferred_element_type=F32)


def _dot_nt(a, b):
    return lax.dot_general(a, b, (((1,), (1,)), ((), ())), preferred_element_type=F32)


def _dot_tn(a, b):
    return lax.dot_general(a, b, (((0,), (0,)), ((), ())), preferred_element_type=F32)


def _split3(x):
    hi = x.astype(BF16)
    r = x - hi.astype(F32)
    mid = r.astype(BF16)
    lo = (r - mid.astype(F32)).astype(BF16)
    return hi, mid, lo


def _dot3_left(c, x):
    hi, mid, lo = _split3(x)
    return _dot(c, hi) + _dot(c, mid) + _dot(c, lo)


def _sigmoid(x):
    return jax.nn.sigmoid(x)


def _silu_and_grad(x):
    s = _sigmoid(x)
    return x * s, s * (1.0 + x * (1.0 - s))


_GELU_C = math.sqrt(2.0 / math.pi)


def _gelu_and_grad(x):
    inner = _GELU_C * (x + 0.044715 * x * x * x)
    t = jnp.tanh(inner)
    y = 0.5 * x * (1.0 + t)
    dy = 0.5 * (1.0 + t) + 0.5 * x * (1.0 - t * t) * _GELU_C * (1.0 + 3.0 * 0.044715 * x * x)
    return y, dy


def _lane(shape):
    return lax.broadcasted_iota(jnp.int32, shape, 1)


def _row(shape):
    return lax.broadcasted_iota(jnp.int32, shape, 0)


def _gsum64(x):
    lo = _lane(x.shape) < 64
    s0 = jnp.sum(jnp.where(lo, x, 0.0), axis=-1, keepdims=True)
    s1 = jnp.sum(jnp.where(lo, 0.0, x), axis=-1, keepdims=True)
    return jnp.where(lo, s0, s1)


def _colreduce(x, op):
    parts = [x[r:r + 8, :] for r in range(0, x.shape[0], 8)]
    while len(parts) > 1:
        pairs = [op(parts[k], parts[k + 1]) for k in range(0, len(parts) - 1, 2)]
        parts = pairs + ([parts[-1]] if len(parts) % 2 else [])
    red = jnp.max if op is jnp.maximum else jnp.sum
    return red(parts[0], axis=0, keepdims=True)


def _block_diag64(dtype=BF16):
    r, c = _row((128, 128)), _lane((128, 128))
    return jnp.where((r >> 6) == (c >> 6), 1.0, 0.0).astype(dtype)


def _inproj(x, g, w, tag):
    T, D = x.shape
    DP = w.shape[1]
    tm = _tile(T, 512)

    def body(x_ref, g_ref, w_ref, h_ref, p_ref):
        xv = x_ref[...]
        r = lax.rsqrt(jnp.mean(xv * xv, axis=-1, keepdims=True) + NORM_EPS)
        h = (xv * r * g_ref[...]).astype(BF16)
        h_ref[...] = h
        p_ref[...] = _dot(h, w_ref[...])

    return pl.pallas_call(
        body, name=f"inproj_{tag}", grid=(T // tm,),
        in_specs=[pl.BlockSpec((tm, D), lambda i: (i, 0)), pl.BlockSpec((1, D), lambda i: (0, 0)),
                  pl.BlockSpec((D, DP), lambda i: (0, 0))],
        out_specs=[pl.BlockSpec((tm, D), lambda i: (i, 0)), pl.BlockSpec((tm, DP), lambda i: (i, 0))],
        out_shape=[SDS((T, D), BF16), SDS((T, DP), F32)],
        compiler_params=_params("parallel"),
    )(x, g, w)


def _outproj(x, ya, yb, yc, wo, tag):
    T, D = x.shape
    tm = _tile(T, 512)

    def body(x_ref, ya_ref, yb_ref, yc_ref, wo_ref, o_ref):
        acc = x_ref[...] + _dot(ya_ref[...], wo_ref[0:A_WIDTH, :])
        acc = acc + _dot(yb_ref[...], wo_ref[A_WIDTH:A_WIDTH + B_WIDTH, :])
        o_ref[...] = acc + _dot(yc_ref[...], wo_ref[A_WIDTH + B_WIDTH:, :])

    row = lambda w: pl.BlockSpec((tm, w), lambda i: (i, 0))
    return pl.pallas_call(
        body, name=f"outproj_{tag}", grid=(T // tm,),
        in_specs=[row(D), row(A_WIDTH), row(B_WIDTH), row(C_WIDTH), pl.BlockSpec(wo.shape, lambda i: (0, 0))],
        out_specs=row(D), out_shape=SDS((T, D), F32), compiler_params=_params("parallel"),
    )(x, ya, yb, yc, wo)


def _outproj_bwd(dx, ya, yb, yc, wo, tag):
    T, D = dx.shape
    DM = wo.shape[0]
    tm = _tile(T, 512)

    def body(dx_ref, ya_ref, yb_ref, yc_ref, wo_ref, dy_ref, dwo_ref):
        @pl.when(pl.program_id(0) == 0)
        def _():
            dwo_ref[...] = jnp.zeros_like(dwo_ref)

        dxb = dx_ref[...].astype(BF16)
        dy_ref[...] = _dot_nt(dxb, wo_ref[...])
        dwo_ref[0:A_WIDTH, :] += _dot_tn(ya_ref[...], dxb)
        dwo_ref[A_WIDTH:A_WIDTH + B_WIDTH, :] += _dot_tn(yb_ref[...], dxb)
        dwo_ref[A_WIDTH + B_WIDTH:, :] += _dot_tn(yc_ref[...], dxb)

    row = lambda w: pl.BlockSpec((tm, w), lambda i: (i, 0))
    return pl.pallas_call(
        body, name=f"outproj_bwd_{tag}", grid=(T // tm,),
        in_specs=[row(D), row(A_WIDTH), row(B_WIDTH), row(C_WIDTH), pl.BlockSpec(wo.shape, lambda i: (0, 0))],
        out_specs=[row(DM), pl.BlockSpec((DM, D), lambda i: (0, 0))],
        out_shape=[SDS((T, DM), F32), SDS((DM, D), F32)], compiler_params=_params("arbitrary"),
    )(dx, ya, yb, yc, wo)


def _piece_offsets(pieces):
    offs = [0]
    for p in pieces:
        offs.append(offs[-1] + p.shape[1])
    return offs


def _dw_in(h, pieces, dp_width, tag, ride=None):
    T, D = h.shape
    tm = _tile(T, 512)
    grid = (T // tm,)
    offs = _piece_offsets(pieces)
    n = len(pieces)

    def body(h_ref, *rest):
        p_refs, rest = rest[:n], rest[n:]
        ride_srcs, (dw_ref,), ride_dsts, (acc_ref,), ride_sems = _ride_refs(ride, rest, 1, 1)
        i = pl.program_id(0)
        _ride_start(ride, grid, ride_srcs, ride_dsts, ride_sems)

        @pl.when(i == 0)
        def _():
            acc_ref[...] = jnp.zeros_like(acc_ref)

        hv = h_ref[...]
        for k, p_ref in enumerate(p_refs):
            acc_ref[offs[k]:offs[k + 1], :] += _dot_tn(p_ref[...], hv)

        @pl.when(i == grid[0] - 1)
        def _():
            dw_ref[...] = acc_ref[...].astype(BF16)

        _ride_wait(ride, grid, ride_srcs, ride_dsts, ride_sems)

    extra = ride or _ChipExchange("gather", ())
    out = pl.pallas_call(
        body, name=f"dw_in_{tag}", grid=grid,
        in_specs=[pl.BlockSpec((tm, D), lambda i: (i, 0))] + [pl.BlockSpec((tm, p.shape[1]), lambda i: (i, 0)) for p in pieces]
        + extra.in_specs,
        out_specs=[pl.BlockSpec((dp_width, D), lambda i: (0, 0))] + extra.out_specs,
        out_shape=[SDS((dp_width, D), BF16)] + extra.out_shape,
        scratch_shapes=[pltpu.VMEM((dp_width, D), F32)] + (extra.scratch if ride else []),
        compiler_params=pltpu.CompilerParams(dimension_semantics=("arbitrary",), vmem_limit_bytes=VMEM_LIMIT,
                                             has_side_effects=bool(ride)),
    )(h, *pieces, *extra.sources)
    return out if ride else out[0]


def _dx_in(x, g, dres, pieces, w, tag, ride=None):
    T, D = x.shape
    DP = w.shape[1]
    tm = _tile(T, 512)
    grid = (T // tm,)
    offs = _piece_offsets(pieces)
    n = len(pieces)

    def body(x_ref, g_ref, dres_ref, w_ref, *rest):
        p_refs, rest = rest[:n], rest[n:]
        ride_srcs, (dx_ref, dg_ref), ride_dsts, _, ride_sems = _ride_refs(ride, rest, 2, 0)
        _ride_start(ride, grid, ride_srcs, ride_dsts, ride_sems)

        @pl.when(pl.program_id(0) == 0)
        def _():
            dg_ref[...] = jnp.zeros_like(dg_ref)

        dh = _dot_nt(p_refs[0][...], w_ref[:, offs[0]:offs[1]])
        for k in range(1, n):
            dh = dh + _dot_nt(p_refs[k][...], w_ref[:, offs[k]:offs[k + 1]])
        xv = x_ref[...]
        r = lax.rsqrt(jnp.mean(xv * xv, axis=-1, keepdims=True) + NORM_EPS)
        xh = xv * r
        dg_ref[...] += jnp.sum(dh * xh, axis=0, keepdims=True)
        dxh = dh * g_ref[...]
        dx_ref[...] = dres_ref[...] + r * (dxh - xh * jnp.mean(dxh * xh, axis=-1, keepdims=True))
        _ride_wait(ride, grid, ride_srcs, ride_dsts, ride_sems)

    extra = ride or _ChipExchange("gather", ())
    row = pl.BlockSpec((tm, D), lambda i: (i, 0))
    return pl.pallas_call(
        body, name=f"dx_in_{tag}", grid=grid,
        in_specs=[row, pl.BlockSpec((1, D), lambda i: (0, 0)), row, pl.BlockSpec((D, DP), lambda i: (0, 0))]
        + [pl.BlockSpec((tm, p.shape[1]), lambda i: (i, 0)) for p in pieces] + extra.in_specs,
        out_specs=[row, pl.BlockSpec((1, D), lambda i: (0, 0))] + extra.out_specs,
        out_shape=[SDS((T, D), F32), SDS((1, D), F32)] + extra.out_shape,
        scratch_shapes=extra.scratch if ride else [],
        compiler_params=pltpu.CompilerParams(dimension_semantics=("arbitrary",), vmem_limit_bytes=VMEM_LIMIT,
                                             has_side_effects=bool(ride)),
    )(x, g, dres, w, *pieces, *extra.sources)


def _loss_head(x, g, tgt):
    T, D = x.shape
    tm = _tile(T, 512)

    def body(x_ref, g_ref, t_ref, dx_ref, loss_ref, dg_ref):
        @pl.when(pl.program_id(0) == 0)
        def _():
            loss_ref[...] = jnp.zeros_like(loss_ref)
            dg_ref[...] = jnp.zeros_like(dg_ref)

        xv = x_ref[...]
        r = lax.rsqrt(jnp.mean(xv * xv, axis=-1, keepdims=True) + NORM_EPS)
        xh = xv * r
        gv = g_ref[...]
        err = xh * gv - t_ref[...]
        tok = jnp.mean(err * err, axis=-1, keepdims=True)
        loss_ref[...] += 0.5 * jnp.sum(tok, axis=0, keepdims=True)
        dy = err * (1.0 / D)
        dg_ref[...] += jnp.sum(dy * xh, axis=0, keepdims=True)
        dxh = dy * gv
        dx_ref[...] = r * (dxh - xh * jnp.mean(dxh * xh, axis=-1, keepdims=True))

    row = pl.BlockSpec((tm, D), lambda i: (i, 0))
    return pl.pallas_call(
        body, name="loss_head", grid=(T // tm,),
        in_specs=[row, pl.BlockSpec((1, D), lambda i: (0, 0)), row],
        out_specs=[row, pl.BlockSpec((1, 128), lambda i: (0, 0)), pl.BlockSpec((1, D), lambda i: (0, 0))],
        out_shape=[SDS((T, D), F32), SDS((1, 128), F32), SDS((1, D), F32)], compiler_params=_params("arbitrary"),
    )(x, g, tgt)


def _gmlp_core(u, v, lng, lnb, wm_ref, bst_ref, pair):
    ug, dug = _gelu_and_grad(u)
    vg, dvg = _gelu_and_grad(v)
    mu = _gsum64(vg) * (1.0 / 64)
    d = vg - mu
    var = _gsum64(d * d) * (1.0 / 64)
    rstd = lax.rsqrt(var + NORM_EPS)
    xh = d * rstd
    vn = xh * lng + lnb
    vnb = vn.astype(BF16)
    lo = _lane(u.shape) < 64
    g0, g1 = 2 * pair, 2 * pair + 1
    mixed = jnp.where(lo, _dot(wm_ref[g0], vnb) + bst_ref[:, g0:g0 + 1], _dot(wm_ref[g1], vnb) + bst_ref[:, g1:g1 + 1])
    return ug, dug, dvg, rstd, xh, vnb, mixed, lo


def _gmlp_fwd(proj, lng, lnb, wm, bst):
    T = proj.shape[0]

    def body(u_ref, v_ref, z_ref, lng_ref, lnb_ref, wm_ref, bst_ref, y_ref):
        for pair in range(2):
            sl = slice(128 * pair, 128 * pair + 128)
            ug, _, _, _, _, _, mixed, _ = _gmlp_core(u_ref[:, sl], v_ref[:, sl], lng_ref[:, sl], lnb_ref[:, sl],
                                                     wm_ref, bst_ref, pair)
            sz, _ = _silu_and_grad(z_ref[:, sl])
            y_ref[:, sl] = (ug * mixed * sz).astype(BF16)

    col = lambda c: pl.BlockSpec((CHUNK, A_WIDTH), lambda i, c=c: (i, c // A_WIDTH))
    full = lambda a: pl.BlockSpec(a.shape, lambda i, n=a.ndim: (0,) * n)
    return _Part(body, (proj, proj, proj, lng, lnb, wm, bst),
                 [col(COL_AU), col(COL_AV), col(COL_AZ), full(lng), full(lnb), full(wm), full(bst)],
                 [pl.BlockSpec((CHUNK, A_WIDTH), lambda i: (i, 0))], [SDS((T, A_WIDTH), BF16)], [])


def _gmlp_bwd(proj, dy, lng, lnb, wm, wmt, bst):
    T = proj.shape[0]
    n = T // CHUNK

    def body(u_ref, v_ref, z_ref, dy_ref, lng_ref, lnb_ref, wm_ref, wmt_ref, bst_ref,
             da_ref, dwm_ref, dbst_ref, dlng_ref, dlnb_ref):
        @pl.when(pl.program_id(0) == 0)
        def _():
            dwm_ref[...] = jnp.zeros_like(dwm_ref)
            dbst_ref[...] = jnp.zeros_like(dbst_ref)
            dlng_ref[...] = jnp.zeros_like(dlng_ref)
            dlnb_ref[...] = jnp.zeros_like(dlnb_ref)

        lane = _lane((CHUNK, 128))
        dbst = dbst_ref[...]
        for pair in range(2):
            sl = slice(128 * pair, 128 * pair + 128)
            lng_p = lng_ref[:, sl]
            ug, dug, dvg, rstd, xh, vnb, mixed, lo = _gmlp_core(u_ref[:, sl], v_ref[:, sl], lng_p, lnb_ref[:, sl],
                                                                wm_ref, bst_ref, pair)
            sz, dsz = _silu_and_grad(z_ref[:, sl])
            dyv = dy_ref[:, sl]
            out = ug * mixed
            dz = dyv * out * dsz
            dout = dyv * sz
            du = dout * mixed * dug
            dmix = dout * ug
            g0, g1 = 2 * pair, 2 * pair + 1
            dm0 = jnp.where(lo, dmix, 0.0)
            dm1 = jnp.where(lo, 0.0, dmix)
            dbst = dbst + jnp.where(lane == g0, jnp.sum(dm0, axis=-1, keepdims=True), 0.0)
            dbst = dbst + jnp.where(lane == g1, jnp.sum(dm1, axis=-1, keepdims=True), 0.0)
            dwm_ref[g0] += _dot_nt(dm0.astype(BF16), vnb)
            dwm_ref[g1] += _dot_nt(dm1.astype(BF16), vnb)
            dmb = dmix.astype(BF16)
            dvn = jnp.where(lo, _dot(wmt_ref[g0], dmb), _dot(wmt_ref[g1], dmb))
            dlng_ref[:, sl] += jnp.sum(dvn * xh, axis=0, keepdims=True)
            dlnb_ref[:, sl] += jnp.sum(dvn, axis=0, keepdims=True)
            dxh = dvn * lng_p
            m1 = _gsum64(dxh) * (1.0 / 64)
            m2 = _gsum64(dxh * xh) * (1.0 / 64)
            dv = rstd * (dxh - m1 - xh * m2) * dvg
            da_ref[:, COL_AU + 128 * pair:COL_AU + 128 * pair + 128] = du.astype(BF16)
            da_ref[:, COL_AV + 128 * pair:COL_AV + 128 * pair + 128] = dv.astype(BF16)
            da_ref[:, COL_AZ + 128 * pair:COL_AZ + 128 * pair + 128] = dz.astype(BF16)
        dbst_ref[...] = dbst

        @pl.when(pl.program_id(0) == n - 1)
        def _():
            causal = _lane((CHUNK, CHUNK)) <= _row((CHUNK, CHUNK))
            for g in range(A_GROUPS):
                dwm_ref[g] = jnp.where(causal, dwm_ref[g], 0.0)

    col = lambda c: pl.BlockSpec((CHUNK, A_WIDTH), lambda i, c=c: (i, c // A_WIDTH))
    full = lambda a: pl.BlockSpec(a.shape, lambda i, n=a.ndim: (0,) * n)
    acc = lambda s: pl.BlockSpec(s, lambda i, n=len(s): (0,) * n)
    return _Part(body, (proj, proj, proj, dy, lng, lnb, wm, wmt, bst),
                 [col(COL_AU), col(COL_AV), col(COL_AZ), pl.BlockSpec((CHUNK, A_WIDTH), lambda i: (i, 0)),
                  full(lng), full(lnb), full(wm), full(wmt), full(bst)],
                 [pl.BlockSpec((CHUNK, 3 * A_WIDTH), lambda i: (i, 0)), acc((A_GROUPS, CHUNK, CHUNK)),
                  acc((CHUNK, 128)), acc((1, A_WIDTH)), acc((1, A_WIDTH))],
                 [SDS((T, 3 * A_WIDTH), BF16), SDS((A_GROUPS, CHUNK, CHUNK), F32), SDS((CHUNK, 128), F32),
                  SDS((1, A_WIDTH), F32), SDS((1, A_WIDTH), F32)], [])


def _hgrn_consts():
    r, c = _row((CHUNK, CHUNK)), _lane((CHUNK, CHUNK))
    same = (r >> SUB_SHIFT) == (c >> SUB_SHIFT)
    lsub = jnp.where(same & (c <= r), 1.0, 0.0).astype(BF16)
    usub = jnp.where(same & (c >= r), 1.0, 0.0).astype(BF16)
    bsub = jnp.where(same, 1.0, 0.0).astype(BF16)
    return lsub, usub, bsub


def _hgrn_gates(qv, zf, lbp):
    sq, dsq = _silu_and_grad(qv)
    qt = sq * Q_SCALE
    sg = _sigmoid(zf)
    sgn = _sigmoid(-zf)
    f = lbp + (1.0 - lbp) * sg
    g = jnp.log(jnp.maximum(f, F_FLOOR))
    kf = (1.0 - lbp) * sgn
    return qt, dsq, sg, sgn, f, g, kf


def _hgrn_intra_scores(qt, kf, b, mbd):
    rid = _row((SUB, 128))
    parts = []
    for s in range(SUB):
        e = jnp.exp(b - b[s:s + 1, :])
        parts.append(jnp.where(rid >= s, qt * kf[s:s + 1, :] * e, 0.0))
    return _dot(jnp.concatenate(parts, axis=0).astype(BF16), mbd)


def _hgrn_intra_out(a, v):
    o = jnp.zeros((SUB, 128), F32)
    for s in range(SUB):
        o = o + a[SUB * s:SUB * s + SUB, :] * v[s:s + 1, :]
    return o


def _hgrn_intra_bwd_scores(qt, kf, b, v, do, mbd):
    rid = _row((SUB, 128))
    ps, das, kes, es = [], [], [], []
    for s in range(SUB):
        e = jnp.where(rid >= s, jnp.exp(b - b[s:s + 1, :]), 0.0)
        ke = kf[s:s + 1, :] * e
        es.append(e)
        kes.append(ke)
        ps.append(qt * ke)
        das.append(do * v[s:s + 1, :])
    a = _dot(jnp.concatenate(ps, axis=0).astype(BF16), mbd)
    da = _dot(jnp.concatenate(das, axis=0).astype(BF16), mbd)
    return a, da, kes, es


def _hgrn_intra_bwd_grads(scores, qt, do, rsum):
    a, da, kes, es = scores
    dqt = jnp.zeros((SUB, 128), F32)
    xs, ys = [], []
    for s in range(SUB):
        da_s = da[SUB * s:SUB * s + SUB, :]
        dqt = dqt + da_s * kes[s]
        xs.append(a[SUB * s:SUB * s + SUB, :] * do)
        ys.append(da_s * qt * es[s])
    dv = _dot(rsum, jnp.concatenate(xs, axis=0).astype(BF16))
    dkf = _dot(rsum, jnp.concatenate(ys, axis=0).astype(BF16))
    return dqt, dkf, dv


def _hgrn_norm_gate(o, z, onorm):
    ms = _gsum64(o * o) * (1.0 / 64)
    r = lax.rsqrt(ms + NORM_EPS)
    xh = o * r
    sz, dsz = _silu_and_grad(z)
    return xh, r, sz, dsz, xh * onorm


def _hgrn_fwd(proj, lb, onorm):
    T = proj.shape[0]
    n = T // CHUNK
    nsub = CHUNK // SUB

    def body(q_ref, f_ref, i_ref, z_ref, lb_ref, on_ref, y_ref, o_ref, s0_ref, st_ref):
        @pl.when(pl.program_id(0) == 0)
        def _():
            st_ref[...] = jnp.zeros_like(st_ref)

        lsub, _, bsub = _hgrn_consts()
        mbd = _block_diag64()
        bdmask = mbd > 0
        rid = _row((CHUNK, 128))
        subs = [slice(SUB * sub, SUB * sub + SUB) for sub in range(nsub)]
        work = []
        for pair in range(2):
            sl = slice(128 * pair, 128 * pair + 128)
            qt, _, _, _, _, g, kf = _hgrn_gates(q_ref[:, sl], f_ref[:, sl], lb_ref[:, sl])
            work.append(dict(sl=sl, qt=qt, kf=kf, v=i_ref[:, sl], b=_dot3_left(lsub, g), bl=_dot3_left(bsub, g)))
        for w in work:
            qt, kf, v, b, bl = w["qt"], w["kf"], w["v"], w["b"], w["bl"]
            w["qh"] = (qt * jnp.exp(b)).astype(BF16)
            kh = kf * jnp.exp(bl - b)
            w["dec"] = jnp.exp(bl)
            vtb = v.T.astype(BF16)
            w["scores"] = [_hgrn_intra_scores(qt[rs], kf[rs], b[rs], mbd) for rs in subs]
            w["adds"] = [_dot(vtb, jnp.where((rid >> SUB_SHIFT) == sub, kh, 0.0).astype(BF16)) for sub in range(nsub)]
        for pair, w in enumerate(work):
            w["st"] = st_ref[pair]
            s0_ref[0, pair] = w["st"]
            w["outs"] = []
        for sub, rs in enumerate(subs):
            for w in work:
                w["outs"].append(_dot_nt(w["qh"][rs], w["st"].astype(BF16)) + _hgrn_intra_out(w["scores"][sub], w["v"][rs]))
                w["st"] = jnp.where(bdmask, w["st"] * w["dec"][SUB * sub:SUB * sub + 1, :] + w["adds"][sub], 0.0)
        for pair, w in enumerate(work):
            sl = w["sl"]
            st_ref[pair] = w["st"]
            o = jnp.concatenate(w["outs"], axis=0)
            o_ref[:, sl] = o
            _, _, sz, _, on = _hgrn_norm_gate(o, z_ref[:, sl], on_ref[:, sl])
            y_ref[:, sl] = (on * sz).astype(BF16)

    col = lambda c: pl.BlockSpec((CHUNK, B_WIDTH), lambda i, c=c: (i, c // B_WIDTH))
    full = lambda a: pl.BlockSpec(a.shape, lambda i, n=a.ndim: (0,) * n)
    return _Part(body, (proj, proj, proj, proj, lb, onorm),
                 [col(COL_BQ), col(COL_BF), col(COL_BI), col(COL_BZ), full(lb), full(onorm)],
                 [pl.BlockSpec((CHUNK, B_WIDTH), lambda i: (i, 0)), pl.BlockSpec((CHUNK, B_WIDTH), lambda i: (i, 0)),
                  pl.BlockSpec((1, 2, 128, 128), lambda i: (i, 0, 0, 0))],
                 [SDS((T, B_WIDTH), BF16), SDS((T, B_WIDTH), F32), SDS((n, 2, 128, 128), F32)],
                 [pltpu.VMEM((2, 128, 128), F32)])


def _hgrn_bwd(proj, dy, o_saved, s0, lb, onorm):
    T = proj.shape[0]
    n = T // CHUNK
    nsub = CHUNK // SUB

    def body(q_ref, f_ref, i_ref, z_ref, dy_ref, o_ref, s0_ref, lb_ref, on_ref,
             db_ref, dlb_ref, don_ref, dst_ref, sts_ref):
        @pl.when(pl.program_id(0) == 0)
        def _():
            dst_ref[...] = jnp.zeros_like(dst_ref)
            dlb_ref[...] = jnp.zeros_like(dlb_ref)
            don_ref[...] = jnp.zeros_like(don_ref)

        lsub, usub, bsub = _hgrn_consts()
        mbd = _block_diag64()
        bdmask = mbd > 0
        rsum = jnp.where((_lane((SUB, SUB * SUB)) >> SUB_SHIFT) == _row((SUB, SUB * SUB)), 1.0, 0.0).astype(BF16)
        subs = [slice(SUB * sub, SUB * sub + SUB) for sub in range(nsub)]
        work = []
        for pair in range(2):
            sl = slice(128 * pair, 128 * pair + 128)
            lbp = lb_ref[:, sl]
            qt, dsq, sg, sgn, f, g, kf = _hgrn_gates(q_ref[:, sl], f_ref[:, sl], lbp)
            w = dict(sl=sl, lbp=lbp, qt=qt, dsq=dsq, sg=sg, sgn=sgn, f=f, kf=kf, v=i_ref[:, sl],
                     b=_dot3_left(lsub, g), bl=_dot3_left(bsub, g))
            onp = on_ref[:, sl]
            xh, r, sz, dsz, on = _hgrn_norm_gate(o_ref[:, sl], z_ref[:, sl], onp)
            dyv = dy_ref[:, sl]
            w["dz"] = dyv * on * dsz
            don = dyv * sz
            cn = jnp.sum(don * xh, axis=0, keepdims=True)
            don_ref[...] += cn + pltpu.roll(cn, 64, axis=1)
            dxo = don * onp
            w["do"] = r * (dxo - xh * (_gsum64(dxo * xh) * (1.0 / 64)))
            work.append(w)
        for w in work:
            qt, kf, v, b, bl, do = w["qt"], w["kf"], w["v"], w["b"], w["bl"], w["do"]
            w["eb"] = jnp.exp(b)
            w["ekb"] = jnp.exp(bl - b)
            w["qhb"] = (qt * w["eb"]).astype(BF16)
            w["khb"] = (kf * w["ekb"]).astype(BF16)
            w["dec"] = jnp.exp(bl)
            w["vb"] = v.astype(BF16)
            w["dob"] = do.astype(BF16)
            w["scores"] = [_hgrn_intra_bwd_scores(qt[rs], kf[rs], b[rs], v[rs], do[rs], mbd) for rs in subs]
            w["st_adds"] = [_dot_tn(w["vb"][rs], w["khb"][rs]) for rs in subs]
            w["gst_adds"] = [_dot_tn(w["dob"][rs], w["qhb"][rs]) for rs in subs]
        for pair, w in enumerate(work):
            w["st"] = s0_ref[0, pair]
        for sub in range(nsub):
            for pair, w in enumerate(work):
                sts_ref[pair, sub] = w["st"]
                w["st"] = jnp.where(bdmask, w["st"] * w["dec"][SUB * sub:SUB * sub + 1, :] + w["st_adds"][sub], 0.0)
        for pair, w in enumerate(work):
            w["gst"] = dst_ref[pair]
            w["dqt_p"], w["dkf_p"], w["dv_p"], w["dbl_p"] = ([None] * nsub for _ in range(4))
        for sub in reversed(range(nsub)):
            rs = subs[sub]
            for pair, w in enumerate(work):
                gst = w["gst"]
                st_in = sts_ref[pair, sub]
                gb = gst.astype(BF16)
                dqh = _dot(w["dob"][rs], st_in.astype(BF16))
                dkh = _dot(w["vb"][rs], gb)
                dv_inter = _dot_nt(w["khb"][rs], gb)
                ddec = jnp.sum(gst * st_in, axis=0, keepdims=True)
                dec_row = w["dec"][SUB * sub:SUB * sub + 1, :]
                w["gst"] = jnp.where(bdmask, gst * dec_row + w["gst_adds"][sub], 0.0)
                dqt_i, dkf_i, dv_i = _hgrn_intra_bwd_grads(w["scores"][sub], w["qt"][rs], w["do"][rs], rsum)
                dkf_inter = dkh * w["ekb"][rs]
                w["dqt_p"][sub] = dqh * w["eb"][rs] + dqt_i
                w["dkf_p"][sub] = dkf_inter + dkf_i
                w["dv_p"][sub] = dv_inter + dv_i
                row = jnp.sum(w["kf"][rs] * dkf_inter, axis=0, keepdims=True) + ddec * dec_row
                w["dbl_p"][sub] = jnp.broadcast_to(row, (SUB, 128))
        for pair, w in enumerate(work):
            sl, lbp, sg, sgn, f = w["sl"], w["lbp"], w["sg"], w["sgn"], w["f"]
            dst_ref[pair] = w["gst"]
            dqt = jnp.concatenate(w["dqt_p"], axis=0)
            dkf = jnp.concatenate(w["dkf_p"], axis=0)
            dv = jnp.concatenate(w["dv_p"], axis=0)
            dg = _dot3_left(usub, w["qt"] * dqt - w["kf"] * dkf) + jnp.concatenate(w["dbl_p"], axis=0)
            df = jnp.where(f > F_FLOOR, dg / f, 0.0)
            dlb_ref[:, sl] += jnp.sum(df * (1.0 - sg) - dkf * sgn, axis=0, keepdims=True)
            dfl = (1.0 - lbp) * sg * sgn * (df - dkf)
            dq = dqt * Q_SCALE * w["dsq"]
            db_ref[:, 0 * B_WIDTH + 128 * pair:0 * B_WIDTH + 128 * pair + 128] = dq.astype(BF16)
            db_ref[:, 1 * B_WIDTH + 128 * pair:1 * B_WIDTH + 128 * pair + 128] = dfl.astype(BF16)
            db_ref[:, 2 * B_WIDTH + 128 * pair:2 * B_WIDTH + 128 * pair + 128] = dv.astype(BF16)
            db_ref[:, 3 * B_WIDTH + 128 * pair:3 * B_WIDTH + 128 * pair + 128] = w["dz"].astype(BF16)

    rev = lambda c: pl.BlockSpec((CHUNK, B_WIDTH), lambda i, c=c: (n - 1 - i, c // B_WIDTH))
    full = lambda a: pl.BlockSpec(a.shape, lambda i, n_=a.ndim: (0,) * n_)
    acc = lambda s: pl.BlockSpec(s, lambda i, n_=len(s): (0,) * n_)
    return _Part(body, (proj, proj, proj, proj, dy, o_saved, s0, lb, onorm),
                 [rev(COL_BQ), rev(COL_BF), rev(COL_BI), rev(COL_BZ),
                  pl.BlockSpec((CHUNK, B_WIDTH), lambda i: (n - 1 - i, 1)),
                  pl.BlockSpec((CHUNK, B_WIDTH), lambda i: (n - 1 - i, 0)),
                  pl.BlockSpec((1, 2, 128, 128), lambda i: (n - 1 - i, 0, 0, 0)), full(lb), full(onorm)],
                 [pl.BlockSpec((CHUNK, 4 * B_WIDTH), lambda i: (n - 1 - i, 0)), acc((1, B_WIDTH)), acc((1, 128))],
                 [SDS((T, 4 * B_WIDTH), BF16), SDS((1, B_WIDTH), F32), SDS((1, 128), F32)],
                 [pltpu.VMEM((2, 128, 128), F32), pltpu.VMEM((2, nsub, 128, 128), F32)])


def _lb_fwd(hgrn_lb):
    assert hgrn_lb.shape[0] == 2

    def body(x_ref, o_ref):
        x0, x1 = x_ref[0:1, :], x_ref[1:2, :]
        m = jnp.maximum(x0, x1)
        e0, e1 = jnp.exp(x0 - m), jnp.exp(x1 - m)
        p0, p1 = e0 / (e0 + e1), e1 / (e0 + e1)
        o_ref[0:1, :] = jnp.clip(p0 - p0, 0.0, 1.0 - 1e-6)
        o_ref[1:2, :] = jnp.clip((p0 + p1) - p0, 0.0, 1.0 - 1e-6)

    return pl.pallas_call(body, name="lb_fwd", out_shape=SDS(hgrn_lb.shape, F32))(hgrn_lb)


def _lb_bwd(hgrn_lb, dlb):
    def body(x_ref, d_ref, o_ref):
        x0, x1 = x_ref[0:1, :], x_ref[1:2, :]
        m = jnp.maximum(x0, x1)
        e0, e1 = jnp.exp(x0 - m), jnp.exp(x1 - m)
        p0, p1 = e0 / (e0 + e1), e1 / (e0 + e1)
        val = (p0 + p1) - p0
        dp1 = jnp.where((val > 0.0) & (val < 1.0 - 1e-6), d_ref[1:2, :], 0.0)
        inner = p1 * dp1
        o_ref[0:1, :] = p0 * (0.0 - inner)
        o_ref[1:2, :] = p1 * (dp1 - inner)

    return pl.pallas_call(body, name="lb_bwd", out_shape=SDS(hgrn_lb.shape, F32))(hgrn_lb, dlb)


def _fox_prep(proj, bf):
    T = proj.shape[0]
    n = T // CHUNK

    def body(q0_ref, q1_ref, k0_ref, k1_ref, v0_ref, v1_ref, fl_ref, bf_ref, qo_ref, ko_ref, vt_ref, carry_ref):
        for p, v_ref in enumerate((v0_ref, v0_ref, v1_ref, v1_ref)):
            vt_ref[p, 0] = v_ref[:, 128 * (p % 2):128 * (p % 2) + 128].T.astype(BF16)

        @pl.when(pl.program_id(0) == 0)
        def _():
            carry_ref[...] = jnp.zeros_like(carry_ref)

        ltri = jnp.where(_lane((CHUNK, CHUNK)) <= _row((CHUNK, CHUNK)), 1.0, 0.0).astype(BF16)
        lf = jax.nn.log_sigmoid(fl_ref[...] + bf_ref[...])
        c = _dot3_left(ltri, lf) + carry_ref[...]
        carry_ref[...] = c[CHUNK - 1:CHUNK, :]
        lane = _lane((CHUNK, 128))
        feat = lane < 64
        ones_q = (lane >= 67) & (lane <= 69)
        ones_k = (lane >= 64) & (lane <= 66)
        qrefs, krefs = (q0_ref, q1_ref), (k0_ref, k1_ref)
        for h in range(C_HEADS):
            blk = slice(128 * ((h // 2) % 2), 128 * ((h // 2) % 2) + 128)
            qp, kp = qrefs[h // 4][:, blk], krefs[h // 4][:, blk]
            if h % 2:
                qp, kp = pltpu.roll(qp, 64, axis=1), pltpu.roll(kp, 64, axis=1)
            ch = jnp.broadcast_to(c[:, h:h + 1], (CHUNK, 128))
            hi = ch.astype(BF16).astype(F32)
            r1 = ch - hi
            mid = r1.astype(BF16).astype(F32)
            lo = r1 - mid
            aq = jnp.where(lane == 64, hi, jnp.where(lane == 65, mid, jnp.where(lane == 66, lo,
                           jnp.where(ones_q, 1.0, 0.0))))
            ak = jnp.where(lane == 67, -hi, jnp.where(lane == 68, -mid, jnp.where(lane == 69, -lo,
                           jnp.where(ones_k, 1.0, 0.0))))
            qo_ref[:, 128 * h:128 * h + 128] = jnp.where(feat, qp * Q_SCALE, aq).astype(BF16)
            ko_ref[:, 128 * h:128 * h + 128] = jnp.where(feat, kp, ak).astype(BF16)

    w = 256
    col = lambda c: pl.BlockSpec((CHUNK, w), lambda i, c=c: (i, c // w))
    return _Part(body, (proj, proj, proj, proj, proj, proj, proj, bf),
                 [col(COL_CQ), col(COL_CQ + w), col(COL_CK), col(COL_CK + w), col(COL_CV), col(COL_CV + w),
                  pl.BlockSpec((CHUNK, 128), lambda i: (i, COL_CF // 128)), pl.BlockSpec((1, 128), lambda i: (0, 0))],
                 [pl.BlockSpec((CHUNK, C_HEADS * 128), lambda i: (i, 0))] * 2
                 + [pl.BlockSpec((C_HEADS // 2, 1, 128, CHUNK), lambda i: (0, i, 0, 0))],
                 [SDS((T, C_HEADS * 128), BF16)] * 2 + [SDS((C_HEADS // 2, n, 128, CHUNK), BF16)],
                 [pltpu.VMEM((1, 128), F32)])


FOX_TILE = 512
FOX_KEYS = 512
FOX_STRIP = 16


def _fox_mask(tk, tq, k0, q0):
    return (_row((tk, tq)) + (k0 - q0)) <= _lane((tk, tq))


def _ride_refs(ride, rest, n_out, n_scratch):
    n = ride.n if ride else 0
    srcs, rest = rest[:n], rest[n:]
    outs, rest = rest[:n_out], rest[n_out:]
    dsts, rest = rest[:n], rest[n:]
    return srcs, outs, dsts, rest[:n_scratch], rest[n_scratch:]


def _ride_start(ride, grid, srcs, dsts, sems):
    if ride:
        first = functools.reduce(lambda a, b: a & b, [pl.program_id(d) == 0 for d in range(len(grid))])
        pl.when(first)(lambda: ride.start(srcs, dsts, sems))


def _ride_wait(ride, grid, srcs, dsts, sems):
    if ride:
        last = functools.reduce(lambda a, b: a & b, [pl.program_id(d) == n - 1 for d, n in enumerate(grid)])
        pl.when(last)(lambda: ride.wait(srcs, dsts, sems))


def _fox_fwd(qt, kt, vt, proj, tag, ride=None):
    T = proj.shape[0]
    tq, tk = _tile(T, FOX_TILE), _tile(T, FOX_KEYS)
    nq, nsub = T // tq, tk // CHUNK
    npair = C_HEADS // 2

    def body(q_ref, k_ref, vt_ref, z_ref, *rest):
        ride_srcs, (o_ref, lse_ref, y_ref), ride_dsts, (acc_ref, st_ref, pt_ref), ride_sems = _ride_refs(ride, rest, 3, 3)
        i = pl.program_id(1)
        _ride_start(ride, (npair, nq), ride_srcs, ride_dsts, ride_sems)

        qs = (q_ref[:, 0:128], q_ref[:, 128:256])
        acc_ref[...] = jnp.zeros_like(acc_ref)
        pt_ref[...] = jnp.zeros_like(pt_ref)
        nfull = (i * tq) // tk

        def scores(j):
            kb = k_ref[pl.ds(pl.multiple_of(j * tk, tk), tk), :]
            return tuple(_dot_nt(kb[:, 128 * h:128 * h + 128], qs[h]) for h in range(2))

        def weigh(j, h):
            rows = slice(64 * h, 64 * h + 64)
            vth = jnp.concatenate([vt_ref[0, nsub * j + c, rows, :] for c in range(nsub)], axis=1)
            return _dot(vth, pt_ref[h])

        def block(j, carry, diagonal):
            nxt = () if diagonal else scores(j + 1)
            pvs = [weigh(jnp.maximum(j - 1, 0), h) for h in range(2)]
            new = []
            for h in range(2):
                m, l, alpha_prev = carry[3 * h:3 * h + 3]
                st = st_ref[h]
                if diagonal:
                    st = jnp.where(_fox_mask(tk, tq, j * tk, i * tq), st, -jnp.inf)
                m_new = jnp.maximum(m, _colreduce(st, jnp.maximum))
                pt = jnp.exp(st - m_new)
                alpha = jnp.exp(m - m_new)
                rows = slice(64 * h, 64 * h + 64)
                acc_ref[rows, :] = alpha_prev * acc_ref[rows, :] + pvs[h]
                pt_ref[h] = pt.astype(BF16)
                new += [m_new, alpha * l + _colreduce(pt, jnp.add), alpha]
            for h, st in enumerate(nxt):
                st_ref[h] = st
            return tuple(new)

        for h, st in enumerate(scores(0)):
            st_ref[h] = st
        init = (jnp.full((1, tq), -jnp.inf, F32), jnp.zeros((1, tq), F32), jnp.ones((1, tq), F32)) * 2
        carry = lax.fori_loop(0, nfull, lambda j, c: block(j, c, False), init)
        m0, l0, a0, m1, l1, a1 = block(nfull, carry, True)
        for h, alpha in enumerate((a0, a1)):
            rows = slice(64 * h, 64 * h + 64)
            acc_ref[rows, :] = alpha * acc_ref[rows, :] + weigh(nfull, h)
        inv = jnp.where(_row((128, tq)) < 64, 1.0 / l0, 1.0 / l1)
        o = (acc_ref[...] * inv).T
        o_ref[...] = o
        r8 = _row((8, tq))
        lse_ref[0, 0] = jnp.where(r8 == 0, m0 + jnp.log(l0), jnp.where(r8 == 1, m1 + jnp.log(l1), 0.0))
        sz, _ = _silu_and_grad(z_ref[...])
        y_ref[...] = (o * sz).astype(BF16)
        _ride_wait(ride, (npair, nq), ride_srcs, ride_dsts, ride_sems)

    blk = pl.BlockSpec((tq, 128), lambda p, i: (i, p))
    extra = ride or _ChipExchange("gather", ())
    return pl.pallas_call(
        body, name=f"fox_fwd_{tag}", grid=(npair, nq),
        in_specs=[pl.BlockSpec((tq, 256), lambda p, i: (i, p)), pl.BlockSpec((T, 256), lambda p, i: (0, p)),
                  pl.BlockSpec((1, T // CHUNK, 128, CHUNK), lambda p, i: (p, 0, 0, 0)),
                  pl.BlockSpec((tq, 128), lambda p, i: (i, COL_CZ // 128 + p))] + extra.in_specs,
        out_specs=[blk, pl.BlockSpec((1, 1, 8, tq), lambda p, i: (p, i, 0, 0)), blk] + extra.out_specs,
        out_shape=[SDS((T, C_WIDTH), F32), SDS((npair, nq, 8, tq), F32), SDS((T, C_WIDTH), BF16)] + extra.out_shape,
        scratch_shapes=[pltpu.VMEM((128, tq), F32), pltpu.VMEM((2, tk, tq), F32), pltpu.VMEM((2, tk, tq), BF16)]
        + (extra.scratch if ride else []),
        compiler_params=pltpu.CompilerParams(dimension_semantics=("arbitrary", "arbitrary"), vmem_limit_bytes=VMEM_LIMIT,
                                             has_side_effects=bool(ride)),
    )(qt, kt, vt, proj, *extra.sources)


def _fox_bwd_prep(proj, dy, o, qt, tag):
    T = proj.shape[0]
    tq = _tile(T, FOX_TILE)
    nq = T // tq

    def body(z0_ref, z1_ref, dy_ref, o_ref, q_ref, do_ref, dl_ref, dz_ref, dot_ref, qt_ref):
        sel = jnp.where((_lane((16, 128)) >> 6) == _row((16, 128)), 1.0, 0.0).astype(BF16)
        for p, z_ref in enumerate((z0_ref, z0_ref, z1_ref, z1_ref)):
            sl = slice(128 * p, 128 * p + 128)
            sz, dsz = _silu_and_grad(z_ref[:, 128 * (p % 2):128 * (p % 2) + 128])
            dyv, ov = dy_ref[:, sl], o_ref[:, sl]
            do = dyv * sz
            do_ref[:, sl] = do.astype(BF16)
            dot_ref[p, 0] = do.T.astype(BF16)
            dz_ref[:, sl] = (dyv * ov * dsz).astype(BF16)
            hi, mid, lo = _split3(do * ov)
            dl_ref[p, 0] = (_dot_nt(sel, hi) + _dot_nt(sel, mid) + _dot_nt(sel, lo))[0:8, :]
        for h in range(C_HEADS):
            qt_ref[h, 0] = q_ref[:, 128 * h:128 * h + 128].astype(F32).T.astype(BF16)

    w = 256
    blk = pl.BlockSpec((tq, C_WIDTH), lambda i: (i, 0))
    return pl.pallas_call(
        body, name=f"fox_bwd_prep_{tag}", grid=(nq,),
        in_specs=[pl.BlockSpec((tq, w), lambda i: (i, COL_CZ // w)), pl.BlockSpec((tq, w), lambda i: (i, COL_CZ // w + 1)),
                  pl.BlockSpec((tq, C_WIDTH), lambda i: (i, (A_WIDTH + B_WIDTH) // C_WIDTH)), blk,
                  pl.BlockSpec((tq, C_HEADS * 128), lambda i: (i, 0))],
        out_specs=[blk, pl.BlockSpec((C_HEADS // 2, 1, 8, tq), lambda i: (0, i, 0, 0)), blk,
                   pl.BlockSpec((C_HEADS // 2, 1, 128, tq), lambda i: (0, i, 0, 0)),
                   pl.BlockSpec((C_HEADS, 1, 128, tq), lambda i: (0, i, 0, 0))],
        out_shape=[SDS((T, C_WIDTH), BF16), SDS((C_HEADS // 2, nq, 8, tq), F32), SDS((T, C_WIDTH), BF16),
                   SDS((C_HEADS // 2, nq, 128, tq), BF16), SDS((C_HEADS, nq, 128, tq), BF16)],
        compiler_params=_params("parallel"),
    )(proj, proj, dy, o, qt)


def _fox_bwd(qt, kt, proj, do, lse, delta, dot, qtr, tag, ride=None):
    T = proj.shape[0]
    tq, tk = _tile(T, FOX_TILE), _tile(T, FOX_KEYS)
    nq, nk = T // tq, T // tk
    assert tq == tk
    npair = C_HEADS // 2

    def body(q_ref, k_ref, v_ref, do_ref, lse_ref, dl_ref, dot_ref, qtr_ref, *rest):
        ride_srcs, (dq_ref, dk_ref, dv_ref), ride_dsts, scratch, ride_sems = _ride_refs(ride, rest, 3, 4)
        dvt_ref, dkt_ref, pt_ref, ds_ref = scratch
        j = pl.program_id(1)
        first = (j * tk) // tq
        _ride_start(ride, (npair, nk), ride_srcs, ride_dsts, ride_sems)

        @pl.when(j == 0)
        def _():
            dq_ref[...] = jnp.zeros_like(dq_ref)

        dkt_ref[...] = jnp.zeros_like(dkt_ref)
        dvt_ref[...] = jnp.zeros_like(dvt_ref)
        ks = (k_ref[:, 0:128], k_ref[:, 128:256])
        kts = tuple(k.astype(F32).T.astype(BF16) for k in ks)
        vb = v_ref[...].astype(BF16)
        lo = _lane((tq, 128)) < 64

        def operands(i):
            q0 = pl.multiple_of(i * tq, tq)
            qb = q_ref[pl.ds(q0, tq), :]
            dob = do_ref[pl.ds(q0, tq), :]
            qhs = (qb[:, 0:128], qb[:, 128:256])
            dohs = (jnp.where(lo, dob, jnp.zeros_like(dob)), jnp.where(lo, jnp.zeros_like(dob), dob))
            return qhs, dohs

        def scores(i):
            qhs, dohs = operands(i)
            return tuple((_dot_nt(ks[h], qhs[h]), _dot_nt(vb, dohs[h])) for h in range(2))

        def grads(i, slot):
            for h in range(2):
                rows = slice(64 * h, 64 * h + 64)
                dvt_ref[rows, :] += _dot_nt(dot_ref[0, i, rows, :], pt_ref[slot, h])
                dkt_ref[h] += _dot_nt(qtr_ref[h, i], ds_ref[slot, h])
                dq_ref[h, i] += _dot(kts[h], ds_ref[slot, h])

        def block(i, slot, diagonal, opening):
            sc = scores(i)
            if not opening:
                grads(i - 1, 1 - slot)
            lsev = lse_ref[0, i]
            dlv = dl_ref[0, i]
            for h in range(2):
                lseh = jnp.broadcast_to(lsev[h:h + 1, :], (FOX_STRIP, tq))
                dlh = jnp.broadcast_to(dlv[h:h + 1, :], (FOX_STRIP, tq))
                for r in range(0, tk, FOX_STRIP):
                    rows = slice(r, r + FOX_STRIP)
                    pt = jnp.exp(sc[h][0][rows, :] - lseh)
                    if diagonal:
                        pt = jnp.where(_fox_mask(FOX_STRIP, tq, r, 0), pt, 0.0)
                    ds_ref[slot, h, rows, :] = (pt * (sc[h][1][rows, :] - dlh)).astype(BF16)
                    pt_ref[slot, h, rows, :] = pt.astype(BF16)

        block(first, 0, True, True)
        rest = nq - 1 - first

        def two_steps(t, carry):
            block(first + 1 + 2 * t, 1, False, False)
            block(first + 2 + 2 * t, 0, False, False)
            return carry

        lax.fori_loop(0, rest // 2, two_steps, 0)
        pl.when(rest % 2 == 1)(lambda: block(nq - 1, 1, False, False))
        grads(nq - 1, rest % 2)
        dv_ref[...] = dvt_ref[...].T.astype(BF16)
        for h in range(2):
            dk_ref[:, 128 * h:128 * h + 128] = dkt_ref[h].T
        _ride_wait(ride, (npair, nk), ride_srcs, ride_dsts, ride_sems)

    full = lambda w: pl.BlockSpec((T, w), lambda p, j: (0, p))
    stat = pl.BlockSpec((1, nq, 8, tq), lambda p, j: (p, 0, 0, 0))
    extra = ride or _ChipExchange("gather", ())
    return pl.pallas_call(
        body, name=f"fox_bwd_{tag}", grid=(npair, nk),
        in_specs=[full(256), pl.BlockSpec((tk, 256), lambda p, j: (j, p)),
                  pl.BlockSpec((tk, 128), lambda p, j: (j, COL_CV // 128 + p)), full(128), stat, stat,
                  pl.BlockSpec((1, nq, 128, tq), lambda p, j: (p, 0, 0, 0)),
                  pl.BlockSpec((2, nq, 128, tq), lambda p, j: (p, 0, 0, 0))] + extra.in_specs,
        out_specs=[pl.BlockSpec((2, nq, 128, tq), lambda p, j: (p, 0, 0, 0)), pl.BlockSpec((tk, 256), lambda p, j: (j, p)),
                   pl.BlockSpec((tk, 128), lambda p, j: (j, p))] + extra.out_specs,
        out_shape=[SDS((C_HEADS, nq, 128, tq), F32), SDS((T, C_HEADS * 128), F32), SDS((T, C_WIDTH), BF16)]
        + extra.out_shape,
        scratch_shapes=[pltpu.VMEM((128, tk), F32), pltpu.VMEM((2, 128, tk), F32),
                        pltpu.VMEM((2, 2, tk, tq), BF16), pltpu.VMEM((2, 2, tk, tq), BF16)]
        + (extra.scratch if ride else []),
        compiler_params=pltpu.CompilerParams(dimension_semantics=("arbitrary", "arbitrary"), vmem_limit_bytes=VMEM_LIMIT,
                                             has_side_effects=bool(ride)),
    )(qt, kt, proj, do, lse, delta, dot, qtr, *extra.sources)


def _fox_bwd_post(dqt, dkt, proj, bf, tag):
    T = proj.shape[0]
    tq = _tile(T, FOX_TILE)
    n = T // tq

    def body(dq_ref, dk_ref, fl_ref, bf_ref, oq_ref, ok_ref, ofl_ref, dbf_ref, carry_ref):
        @pl.when(pl.program_id(0) == 0)
        def _():
            carry_ref[...] = jnp.zeros_like(carry_ref)
            dbf_ref[...] = jnp.zeros_like(dbf_ref)

        lane = _lane((tq, 128))
        lo = lane < 64
        dqs = [dq_ref[h, 0].T for h in range(C_HEADS)]
        dc = jnp.zeros((tq, 128), F32)
        for h in range(C_HEADS):
            dc = dc + jnp.where(lane == h, dqs[h][:, 64:65] - dk_ref[:, 128 * h + 67:128 * h + 68], 0.0)
        utri = jnp.where(_lane((tq, tq)) >= _row((tq, tq)), 1.0, 0.0).astype(BF16)
        dlf = _dot3_left(utri, dc) + carry_ref[...]
        carry_ref[...] = dlf[0:1, :]
        dfl = jnp.where(lane < C_HEADS, dlf * _sigmoid(-(fl_ref[...] + bf_ref[...])), 0.0)
        ofl_ref[...] = dfl.astype(BF16)
        dbf_ref[...] += jnp.sum(dfl, axis=0, keepdims=True)
        for p in range(C_HEADS // 2):
            a, b = 128 * (2 * p), 128 * (2 * p + 1)
            oq_ref[:, 128 * p:128 * p + 128] = (
                jnp.where(lo, dqs[2 * p], pltpu.roll(dqs[2 * p + 1], 64, axis=1)) * Q_SCALE).astype(BF16)
            ok_ref[:, 128 * p:128 * p + 128] = jnp.where(
                lo, dk_ref[:, a:a + 128], pltpu.roll(dk_ref[:, b:b + 128], 64, axis=1)).astype(BF16)

    rev = lambda w: pl.BlockSpec((tq, w), lambda i: (n - 1 - i, 0))
    return pl.pallas_call(
        body, name=f"fox_bwd_post_{tag}", grid=(n,),
        in_specs=[pl.BlockSpec((C_HEADS, 1, 128, tq), lambda i: (0, n - 1 - i, 0, 0)), rev(C_HEADS * 128),
                  pl.BlockSpec((tq, 128), lambda i: (n - 1 - i, COL_CF // 128)), pl.BlockSpec((1, 128), lambda i: (0, 0))],
        out_specs=[rev(C_WIDTH), rev(C_WIDTH), rev(128), pl.BlockSpec((1, 128), lambda i: (0, 0))],
        out_shape=[SDS((T, C_WIDTH), BF16), SDS((T, C_WIDTH), BF16), SDS((T, 128), BF16), SDS((1, 128), F32)],
        scratch_shapes=[pltpu.VMEM((1, 128), F32)], compiler_params=_params("arbitrary"),
    )(dqt, dkt, proj, bf)


def _adamw_math(w, g, m, v):
    m = ADAM_B1 * m + (1.0 - ADAM_B1) * g
    v = ADAM_B2 * v + (1.0 - ADAM_B2) * (g * g)
    delta = -ADAM_LR * ((m / ADAM_C1) / (jnp.sqrt(v / ADAM_C2) + ADAM_EPS) + ADAM_WD * w)
    return delta, m, v


def _adamw_pair(w, m, v, ga, gb, name):
    n0 = w.shape[0]
    most = max(1, ADAMW_BLOCK_BYTES // (4 * math.prod(w.shape[1:])))
    t0 = max(t for t in range(1, min(n0, most) + 1) if n0 % t == 0)

    def body(w_ref, m_ref, v_ref, ga_ref, gb_ref, g_ref, d_ref, nm_ref, nv_ref):
        g = ga_ref[...] + gb_ref[...]
        g_ref[...] = g
        d_ref[...], nm_ref[...], nv_ref[...] = _adamw_math(w_ref[...], g, m_ref[...], v_ref[...])

    blk = pl.BlockSpec((t0,) + w.shape[1:], lambda i: (i, 0, 0))
    return pl.pallas_call(
        body, name=name, grid=(n0 // t0,), in_specs=[blk] * 5, out_specs=[blk] * 4,
        out_shape=[SDS(w.shape, F32)] * 4, compiler_params=_params("parallel"),
    )(w, m, v, ga, gb)


def _adamw_small(ws, ms, vs, gall):
    offs = _small_offsets()
    n = len(ws)

    def body(*refs):
        w_refs, m_refs, v_refs, g_ref = refs[:n], refs[n:2 * n], refs[2 * n:3 * n], refs[3 * n]
        outs = refs[3 * n + 1:]

        def total(off, rows):
            g = g_ref[0, off:off + rows, :]
            for dev in range(1, N_DEV):
                g = g + g_ref[dev, off:off + rows, :]
            return g

        for k in range(n):
            g = total(offs[k], ws[k].shape[0])
            go_ref, d_ref, nm_ref, nv_ref = outs[4 * k:4 * k + 4]
            go_ref[...] = g
            d_ref[...], nm_ref[...], nv_ref[...] = _adamw_math(w_refs[k][...], g, m_refs[k][...], v_refs[k][...])
        outs[4 * n][...] = total(offs[n], 1)

    shapes = [SDS(w.shape, F32) for w in ws for _ in range(4)] + [SDS((1, 128), F32)]
    res = pl.pallas_call(body, name="adamw_small", out_shape=shapes,
                         compiler_params=pltpu.CompilerParams(vmem_limit_bytes=VMEM_LIMIT))(*ws, *ms, *vs, gall)
    return [res[4 * k:4 * k + 4] for k in range(n)], res[4 * n]


def _pack_grads(dlng, dlnb, dwm, dbst, dlb, donorm, dbf, dfinal, loss_part):
    offs = _small_offsets()
    base = offs[1]
    L = len(dwm)
    assert L == 2

    def body(*refs):
        lng, lnb, wm, bst, on, bf = (refs[L * a:L * a + L] for a in range(6))
        lb_ref, fin_ref, loss_ref, o_ref = refs[6 * L:]
        o_ref[...] = jnp.zeros_like(o_ref)
        lane = _lane((1, 128))
        for l in range(L):
            for j in range(2):
                o_ref[offs[1] - base + 2 * l + j:offs[1] - base + 2 * l + j + 1, :] = lng[l][:, 128 * j:128 * j + 128]
                o_ref[offs[2] - base + 2 * l + j:offs[2] - base + 2 * l + j + 1, :] = lnb[l][:, 128 * j:128 * j + 128]
                o_ref[offs[5] - base + 2 * l + j:offs[5] - base + 2 * l + j + 1, :] = lb_ref[l:l + 1, 128 * j:128 * j + 128]
            for g in range(A_GROUPS):
                row = offs[3] - base + (A_GROUPS * l + g) * CHUNK
                o_ref[row:row + CHUNK, :] = wm[l][g]
            o_ref[offs[4] - base + A_GROUPS * l:offs[4] - base + A_GROUPS * (l + 1), :] = bst[l][...].T[0:A_GROUPS, :]
        o_ref[offs[6] - base:offs[6] - base + 1, :] = jnp.where(lane < 64, on[0][...], pltpu.roll(on[1][...], 64, axis=1))
        o_ref[offs[7] - base:offs[7] - base + 1, :] = jnp.where(
            lane < C_HEADS, bf[0][...], jnp.where(lane < 2 * C_HEADS, pltpu.roll(bf[1][...], C_HEADS, axis=1), 0.0))
        for j in range(D_MODEL // 128):
            o_ref[offs[8] - base + j:offs[8] - base + j + 1, :] = fin_ref[:, 128 * j:128 * j + 128]
        o_ref[offs[9] - base:offs[9] - base + 1, :] = loss_ref[...]

    rows = offs[9] + 8 - base
    return pl.pallas_call(body, name="pack_grads", out_shape=SDS((rows, 128), F32))(
        *dlng, *dlnb, *dwm, *dbst, *donorm, *dbf, dlb, dfinal, loss_part)


def _sum_chips(layers, name, layer_major):
    _, R, C = layers[0].shape
    L = len(layers)
    tc = _tile(C, 256)

    def body(*refs):
        o_ref = refs[-1]
        for l, p_ref in enumerate(refs[:-1]):
            p = [p_ref[k].astype(F32) for k in range(N_CHIPS)]
            s = ((p[0] + p[1]) + p[2]) + p[3]
            if layer_major:
                o_ref[l] = s
            else:
                o_ref[:, l, :] = s

    out = (L, R, C) if layer_major else (R, L, C)
    out_blk = (L, R, tc) if layer_major else (R, L, tc)
    return pl.pallas_call(
        body, name=name, grid=(C // tc,),
        in_specs=[pl.BlockSpec((N_CHIPS, R, tc), lambda i: (0, 0, i))] * L,
        out_specs=pl.BlockSpec(out_blk, lambda i: (0, 0, i)), out_shape=SDS(out, F32),
        compiler_params=_params("parallel"),
    )(*layers)


ANY = pl.BlockSpec(memory_space=pl.ANY)


def _mesh_pos():
    return lax.axis_index("x"), lax.axis_index("y"), lax.axis_index("c")


def _other_chips(x, y):
    return [(1 - x, y), (x, 1 - y), (1 - x, 1 - y)]


class _ChipExchange:
    def __init__(self, mode, sources):
        assert mode in ("gather", "scatter")
        self.mode, self.sources = mode, tuple(sources)
        self.n = len(self.sources)
        self.in_specs = [ANY] * self.n
        self.out_specs = [ANY] * self.n
        self.out_shape = [SDS(((N_CHIPS,) + s.shape) if mode == "gather" else s.shape, s.dtype) for s in self.sources]
        self.scratch = [pltpu.SemaphoreType.DMA((3 * self.n,)), pltpu.SemaphoreType.DMA((3 * self.n,)),
                        pltpu.SemaphoreType.DMA((self.n,))]

    def _copies(self, srcs, dsts, send_sems, recv_sems, local_sems):
        x, y, c = _mesh_pos()
        me = 2 * x + y
        view = (lambda r, chip: r) if self.mode == "gather" else (lambda r, chip: r.at[chip])
        local = [pltpu.make_async_copy(view(s, me), d.at[me], local_sems.at[a]) for a, (s, d) in enumerate(zip(srcs, dsts))]
        sends, recvs = [], []
        for j, (px, py) in enumerate(_other_chips(x, y)):
            peer = 2 * px + py
            for a, (s, d) in enumerate(zip(srcs, dsts)):
                sems = dict(send_sem=send_sems.at[self.n * j + a], recv_sem=recv_sems.at[self.n * j + a],
                            device_id=(px, py, c), device_id_type=MESH_ID)
                sends.append(pltpu.make_async_remote_copy(src_ref=view(s, peer), dst_ref=d.at[me], **sems))
                recvs.append(pltpu.make_async_remote_copy(src_ref=view(s, me), dst_ref=d.at[peer], **sems))
        return local, sends, recvs

    def start(self, srcs, dsts, sems):
        local, sends, _ = self._copies(srcs, dsts, *sems)
        for cp in local + sends:
            cp.start()

    def wait(self, srcs, dsts, sems):
        local, sends, recvs = self._copies(srcs, dsts, *sems)
        for cp in recvs:
            cp.wait_recv()
        for cp in sends:
            cp.wait_send()
        for cp in local:
            cp.wait()


def _gather_halves(w, tag):
    R, C = w.shape
    H = R // 2

    def body(w_ref, g_ref, send_sems, recv_sems, pass_send, pass_recv, local_sem):
        x, y, c = _mesh_pos()
        me = 2 * x + y
        mine, theirs = pl.ds(c * H, H), pl.ds((1 - c) * H, H)
        own = pltpu.make_async_copy(w_ref, g_ref.at[me], local_sem)
        own.start()

        def fetch(j, px, py, src, dst):
            return pltpu.make_async_remote_copy(src_ref=src, dst_ref=dst, send_sem=send_sems.at[j], recv_sem=recv_sems.at[j],
                                                device_id=(px, py, c), device_id_type=MESH_ID)

        def hand(j, rows, peer):
            return pltpu.make_async_remote_copy(src_ref=g_ref.at[peer, rows], dst_ref=g_ref.at[peer, rows],
                                                send_sem=pass_send.at[j], recv_sem=pass_recv.at[j],
                                                device_id=(x, y, 1 - c), device_id_type=MESH_ID)

        chips = _other_chips(x, y)
        sends = [fetch(j, px, py, w_ref.at[mine], g_ref.at[me, mine]) for j, (px, py) in enumerate(chips)]
        for cp in sends:
            cp.start()
        passed = []
        for j, (px, py) in enumerate(chips):
            peer = 2 * px + py
            fetch(j, px, py, w_ref.at[mine], g_ref.at[peer, mine]).wait_recv()
            passed.append(hand(j, mine, peer))
            passed[-1].start()
        for j, (px, py) in enumerate(chips):
            hand(j, theirs, 2 * px + py).wait_recv()
        for cp in sends + passed:
            cp.wait_send()
        own.wait()

    return pl.pallas_call(
        body, name=f"gather_halves_{tag}", in_specs=[ANY], out_specs=ANY, out_shape=SDS((N_CHIPS, R, C), w.dtype),
        scratch_shapes=[pltpu.SemaphoreType.DMA((3,)), pltpu.SemaphoreType.DMA((3,)), pltpu.SemaphoreType.DMA((3,)),
                        pltpu.SemaphoreType.DMA((3,)), pltpu.SemaphoreType.DMA],
        compiler_params=pltpu.CompilerParams(has_side_effects=True),
    )(w)


class _DeviceGather:
    def __init__(self, source):
        self.sources, self.n = (source,), 1
        self.in_specs, self.out_specs = [ANY], [ANY]
        self.out_shape = [SDS((N_DEV,) + source.shape, source.dtype)]
        self.scratch = [pltpu.SemaphoreType.DMA((N_DEV - 1,)), pltpu.SemaphoreType.DMA((N_DEV - 1,)),
                        pltpu.SemaphoreType.DMA((1,))]

    def _copies(self, srcs, dsts, send_sems, recv_sems, local_sems):
        (src,), (dst,) = srcs, dsts
        x, y, c = _mesh_pos()
        me = 4 * x + 2 * y + c
        local = [pltpu.make_async_copy(src, dst.at[me], local_sems.at[0])]
        sends, recvs = [], []
        for k in range(1, N_DEV):
            px, py, pc = (1 - x) if k & 4 else x, (1 - y) if k & 2 else y, (1 - c) if k & 1 else c
            sems = dict(send_sem=send_sems.at[k - 1], recv_sem=recv_sems.at[k - 1], device_id=(px, py, pc),
                        device_id_type=MESH_ID)
            sends.append(pltpu.make_async_remote_copy(src_ref=src, dst_ref=dst.at[me], **sems))
            recvs.append(pltpu.make_async_remote_copy(src_ref=src, dst_ref=dst.at[4 * px + 2 * py + pc], **sems))
        return local, sends, recvs

    start = _ChipExchange.start
    wait = _ChipExchange.wait


def _gather_devices(a, name):
    ex = _DeviceGather(a)

    def body(a_ref, g_ref, *sems):
        ex.start((a_ref,), (g_ref,), sems)
        ex.wait((a_ref,), (g_ref,), sems)

    return pl.pallas_call(
        body, name=name, in_specs=ex.in_specs, out_specs=ex.out_specs[0], out_shape=ex.out_shape[0],
        scratch_shapes=ex.scratch, compiler_params=pltpu.CompilerParams(has_side_effects=True),
    )(a)


def _swap_cores(pin, pout):
    def body(pin_ref, pout_ref, oin_ref, oout_ref, send_sems, recv_sems):
        x, y, c = _mesh_pos()
        cps = [pltpu.make_async_remote_copy(src_ref=src, dst_ref=dst, send_sem=send_sems.at[a], recv_sem=recv_sems.at[a],
                                            device_id=(x, y, 1 - c), device_id_type=MESH_ID)
               for a, (src, dst) in enumerate(((pin_ref, oin_ref), (pout_ref, oout_ref)))]
        for cp in cps:
            cp.start()
        for cp in cps:
            cp.wait()

    return pl.pallas_call(
        body, name="swap_cores", in_specs=[ANY, ANY], out_specs=[ANY, ANY],
        out_shape=[SDS(pin.shape, F32), SDS(pout.shape, F32)],
        scratch_shapes=[pltpu.SemaphoreType.DMA((2,)), pltpu.SemaphoreType.DMA((2,))],
        compiler_params=pltpu.CompilerParams(has_side_effects=True),
    )(pin, pout)


PACK_TILE = 8 * 128


def _pack_rows(size):
    return (size + PACK_TILE - 1) // PACK_TILE * 8


def _small_offsets():
    offs = [0]
    for _, shape in SMALL_PARAMS:
        offs.append(offs[-1] + _pack_rows(math.prod(shape)))
    return offs


def _rows_view(a):
    flat = a.reshape(-1)
    return jnp.pad(flat, (0, (-flat.size) % 128)).reshape(-1, 128)


def _from_rows(rows, shape):
    return rows.reshape(-1)[:math.prod(shape)].reshape(shape)


def _layer_consts(l, gmlp_ln_g, gmlp_ln_b, gmlp_w_s, gmlp_b_s, hgrn_onorm_g, fox_b_f):
    causal = jnp.tril(jnp.ones((CHUNK, CHUNK), bool))
    wm = jnp.where(causal[None], gmlp_w_s[l], 0.0)
    return dict(
        lng=gmlp_ln_g[l].reshape(1, A_WIDTH), lnb=gmlp_ln_b[l].reshape(1, A_WIDTH),
        wm=wm.astype(BF16), wmt=jnp.swapaxes(wm, 1, 2).astype(BF16),
        bst=jnp.pad(gmlp_b_s[l].T, ((0, 0), (0, 128 - A_GROUPS))),
        onorm=jnp.tile(hgrn_onorm_g[l], 4).reshape(1, B_WIDTH),
        bf=jnp.pad(fox_b_f[l], (0, 128 - C_HEADS)).reshape(1, 128),
    )


def kernel(x, norm_g, w_in, w_out, gmlp_ln_g, gmlp_ln_b, gmlp_w_s, gmlp_b_s, hgrn_lb, hgrn_onorm_g, fox_b_f, final_norm_g, loss_target, m_norm_g, m_w_in, m_w_out, m_gmlp_ln_g, m_gmlp_ln_b, m_gmlp_w_s, m_gmlp_b_s, m_hgrn_lb, m_hgrn_onorm_g, m_fox_b_f, m_final_norm_g, v_norm_g, v_w_in, v_w_out, v_gmlp_ln_g, v_gmlp_ln_b, v_gmlp_w_s, v_gmlp_b_s, v_hgrn_lb, v_hgrn_onorm_g, v_fox_b_f, v_final_norm_g):
    T = x.shape[1]
    shard_in = w_in.shape[2]
    shard_out = w_out.shape[1]
    xs = x.reshape(T, D_MODEL)
    tgt = loss_target.reshape(T, D_MODEL)

    w_in_b, w_out_b = w_in.astype(BF16), w_out.astype(BF16)

    def full_w_in(gathered):
        return jnp.concatenate([gathered[k] for k in range(N_CHIPS)] + [jnp.zeros((D_MODEL, D_IN_PAD - D_IN), BF16)], axis=-1)

    lb_all = _lb_fwd(hgrn_lb)
    consts = [_layer_consts(l, gmlp_ln_g, gmlp_ln_b, gmlp_w_s, gmlp_b_s, hgrn_onorm_g, fox_b_f) for l in range(DEPTH)]

    saved = []
    xl = xs
    w_in_l = full_w_in(_gather_halves(w_in_b[0], "w_in_l0"))
    for l in range(DEPTH):
        cs = consts[l]
        tag = f"l{l}"
        h, proj = _inproj(xl, norm_g[l].reshape(1, D_MODEL), w_in_l, tag)
        (ya,), (yb, ob, s0), (qt, kt, vt) = _run_parts(
            [_gmlp_fwd(proj, cs["lng"], cs["lnb"], cs["wm"], cs["bst"]),
             _hgrn_fwd(proj, lb_all[l].reshape(1, B_WIDTH), cs["onorm"]), _fox_prep(proj, cs["bf"])],
            (T // CHUNK,), f"mix_fwd_{tag}")
        ride = _ChipExchange("gather", (w_out_b[l],) + ((w_in_b[l + 1],) if l + 1 < DEPTH else ()))
        oc, lse, yc, *gathered = _fox_fwd(qt, kt, vt, proj, tag, ride)
        w_out_l = gathered[0].reshape(N_CHIPS * shard_out, D_MODEL)
        saved.append(dict(x=xl, h=h, proj=proj, ya=ya, yb=yb, yc=yc, ob=ob, s0=s0, qt=qt, kt=kt, oc=oc, lse=lse,
                          w_in=w_in_l, w_out=w_out_l))
        xl = _outproj(xl, ya, yb, yc, w_out_l, tag)
        if l + 1 < DEPTH:
            w_in_l = full_w_in(gathered[1])

    dx, loss_part, d_final = _loss_head(xl, final_norm_g.reshape(1, D_MODEL), tgt)

    g_small = {}
    dlb_rows, rin, rout = [None] * DEPTH, [None] * DEPTH, [None] * DEPTH
    slabs_in = None
    for l in reversed(range(DEPTH)):
        cs, sv = consts[l], saved[l]
        tag = f"l{l}"
        proj = sv["proj"]
        dy, dw_out = _outproj_bwd(dx, sv["ya"], sv["yb"], sv["yc"], sv["w_out"], tag)
        (da, dwm, dbst, dlng, dlnb), (db, dlb_rows[l], donorm) = _run_parts(
            [_gmlp_bwd(proj, dy, cs["lng"], cs["lnb"], cs["wm"], cs["wmt"], cs["bst"]),
             _hgrn_bwd(proj, dy, sv["ob"], sv["s0"], lb_all[l].reshape(1, B_WIDTH), cs["onorm"])],
            (T // CHUNK,), f"mix_bwd_{tag}")
        do, delta, dzc, dot, qtr = _fox_bwd_prep(proj, dy, sv["oc"], sv["qt"], tag)
        slabs_out = dw_out.reshape(N_CHIPS, shard_out, D_MODEL).astype(BF16)
        ride = _ChipExchange("scatter", (slabs_out,) + ((slabs_in,) if slabs_in is not None else ()))
        dqt, dkt, dvc, *received = _fox_bwd(sv["qt"], sv["kt"], proj, do, sv["lse"], delta, dot, qtr, tag, ride)
        rout[l] = received[0]
        if slabs_in is not None:
            rin[l + 1] = received[1]
        dqc, dkc, dflc, dbf = _fox_bwd_post(dqt, dkt, proj, cs["bf"], tag)
        g_small[l] = dict(ln_g=dlng, ln_b=dlnb, w_s=dwm, b_s=dbst, onorm=donorm, bf=dbf)
        dproj = [da, db, dqc, dkc, dvc, dzc, dflc]
        if l == 0:
            d_hgrn_lb = _lb_bwd(hgrn_lb, jnp.concatenate(dlb_rows, axis=0))
            per_layer = lambda key: [g_small[k][key] for k in range(DEPTH)]
            early = _pack_grads(per_layer("ln_g"), per_layer("ln_b"), per_layer("w_s"), per_layer("b_s"), d_hgrn_lb,
                                per_layer("onorm"), per_layer("bf"), d_final, loss_part)
            dw_in, rearly = _dw_in(sv["h"], dproj, D_IN_PAD, tag, _DeviceGather(early))
        else:
            dw_in = _dw_in(sv["h"], dproj, D_IN_PAD, tag)
        slabs_in = dw_in[:N_CHIPS * shard_in].reshape(N_CHIPS, shard_in, D_MODEL)
        ride = _ChipExchange("scatter", (slabs_in,)) if l == 0 else None
        dx, dng, *received = _dx_in(sv["x"], norm_g[l].reshape(1, D_MODEL), dx, dproj, sv["w_in"], tag, ride)
        if l == 0:
            rin[0] = received[0]
        g_small[l]["norm_g"] = dng.reshape(D_MODEL // 128, 128)
    grad_x = dx.reshape(x.shape)
    rlate = _gather_devices(jnp.concatenate([g_small[l]["norm_g"] for l in range(DEPTH)]), "gather_norm_grads")
    rsmall = jnp.concatenate([rlate, rearly], axis=1)

    pin, pout = _sum_chips(rin, "sum_chips_w_in", False), _sum_chips(rout, "sum_chips_w_out", True)
    oin, oout = _swap_cores(pin, pout)
    to_view = lambda a: jnp.transpose(a, (2, 0, 1))
    g_w_in, d_w_in, nm_w_in, nv_w_in = [
        jnp.transpose(o, (1, 2, 0))
        for o in _adamw_pair(to_view(w_in), to_view(m_w_in), to_view(v_w_in), pin, oin, "adamw_w_in")]
    g_w_out, d_w_out, nm_w_out, nv_w_out = _adamw_pair(w_out, m_w_out, v_w_out, pout, oout, "adamw_w_out")

    small_w = [norm_g, gmlp_ln_g, gmlp_ln_b, gmlp_w_s, gmlp_b_s, hgrn_lb, hgrn_onorm_g, fox_b_f, final_norm_g]
    small_m = [m_norm_g, m_gmlp_ln_g, m_gmlp_ln_b, m_gmlp_w_s, m_gmlp_b_s, m_hgrn_lb, m_hgrn_onorm_g, m_fox_b_f, m_final_norm_g]
    small_v = [v_norm_g, v_gmlp_ln_g, v_gmlp_ln_b, v_gmlp_w_s, v_gmlp_b_s, v_hgrn_lb, v_hgrn_onorm_g, v_fox_b_f, v_final_norm_g]
    views = lambda ps: [_rows_view(p) for p in ps]
    per_param, loss_row = _adamw_small(views(small_w), views(small_m), views(small_v), rsmall)
    sg, sd, sm, sv_ = [[_from_rows(per_param[k][a], shape) for k, (_, shape) in enumerate(SMALL_PARAMS)] for a in range(4)]
    loss = loss_row[0, 0]

    def order(big_in, big_out, small):
        return [small[0], big_in, big_out] + small[1:]

    return (loss, grad_x, *order(g_w_in, g_w_out, sg), *order(d_w_in, d_w_out, sd), *order(nm_w_in, nm_w_out, sm),
            *order(nv_w_in, nv_w_out, sv_))
```

```python
import collections
import functools
import math

import jax
import jax.numpy as jnp
from jax import lax
from jax.experimental import pallas as pl
from jax.experimental.pallas import tpu as pltpu

F32 = jnp.float32
BF16 = jnp.bfloat16
SDS = jax.ShapeDtypeStruct
MESH_ID = pl.DeviceIdType.MESH

D_MODEL = 1024
DEPTH = 2
A_WIDTH = 256
A_GROUPS = 4
B_WIDTH = 256
C_WIDTH = 512
C_HEADS = 8
D_IN = 3848
D_IN_PAD = 4096
CHUNK = 128
SUB = 16
SUB_SHIFT = 4
NORM_EPS = 1e-6
F_FLOOR = 1e-30
COL_AU, COL_AV, COL_AZ = 0, 256, 512
COL_BQ, COL_BF, COL_BI, COL_BZ = 768, 1024, 1280, 1536
COL_CQ, COL_CK, COL_CV, COL_CZ, COL_CF = 1792, 2304, 2816, 3328, 3840
HEAD_LANES = 128
Q_SCALE = 0.125
ADAM_LR, ADAM_B1, ADAM_B2, ADAM_EPS, ADAM_WD, ADAM_STEP = 0.001, 0.9, 0.999, 1e-08, 0.01, 10
ADAM_C1 = 1.0 - ADAM_B1 ** ADAM_STEP
ADAM_C2 = 1.0 - ADAM_B2 ** ADAM_STEP
VMEM_LIMIT = 56 * 1024 * 1024
ADAMW_BLOCK_BYTES = 1 << 20
N_CHIPS = 4
N_DEV = 8

SMALL_PARAMS = (
    ("norm_g", (DEPTH, D_MODEL)), ("gmlp_ln_g", (DEPTH, 4, 64)), ("gmlp_ln_b", (DEPTH, 4, 64)),
    ("gmlp_w_s", (DEPTH, 4, 128, 128)), ("gmlp_b_s", (DEPTH, 4, 128)), ("hgrn_lb", (DEPTH, 256)),
    ("hgrn_onorm_g", (DEPTH, 64)), ("fox_b_f", (DEPTH, 8)), ("final_norm_g", (D_MODEL,)),
)


def _tile(n, pref):
    t = min(n, pref)
    assert n % t == 0, (n, pref)
    return t


def _params(*sem):
    return pltpu.CompilerParams(dimension_semantics=sem, vmem_limit_bytes=VMEM_LIMIT)


_Part = collections.namedtuple("_Part", "body operands in_specs out_specs out_shape scratch")


def _run_parts(parts, grid, name):
    counts = [(len(p.operands), len(p.out_shape), len(p.scratch)) for p in parts]

    def body(*refs):
        ins, outs, scr = [], [], []
        pos = 0
        for group, k in ((ins, 0), (outs, 1), (scr, 2)):
            for c in counts:
                group.append(refs[pos:pos + c[k]])
                pos += c[k]
        for p, i, o, s in zip(parts, ins, outs, scr):
            p.body(*i, *o, *s)

    flat = lambda key: [x for p in parts for x in getattr(p, key)]
    res = pl.pallas_call(
        body, name=name, grid=grid, in_specs=flat("in_specs"), out_specs=flat("out_specs"), out_shape=flat("out_shape"),
        scratch_shapes=flat("scratch"), compiler_params=_params(*(("arbitrary",) * len(grid))),
    )(*flat("operands"))
    out, pos = [], 0
    for c in counts:
        out.append(list(res[pos:pos + c[1]]))
        pos += c[1]
    return out


def _dot(a, b):
    return jnp.dot(a, b, preferred_element_type=F32)


def _dot_nt(a, b):
    return lax.dot_general(a, b, (((1,), (1,)), ((), ())), preferred_element_type=F32)


def _dot_tn(a, b):
    return lax.dot_general(a, b, (((0,), (0,)), ((), ())), preferred_element_type=F32)


def _split3(x):
    hi = x.astype(BF16)
    r = x - hi.astype(F32)
    mid = r.astype(BF16)
    lo = (r - mid.astype(F32)).astype(BF16)
    return hi, mid, lo


def _dot3_left(c, x):
    hi, mid, lo = _split3(x)
    return _dot(c, hi) + _dot(c, mid) + _dot(c, lo)


def _sigmoid(x):
    return jax.nn.sigmoid(x)


def _silu_and_grad(x):
    s = _sigmoid(x)
    return x * s, s * (1.0 + x * (1.0 - s))


_GELU_C = math.sqrt(2.0 / math.pi)


def _gelu_and_grad(x):
    inner = _GELU_C * (x + 0.044715 * x * x * x)
    t = jnp.tanh(inner)
    y = 0.5 * x * (1.0 + t)
    dy = 0.5 * (1.0 + t) + 0.5 * x * (1.0 - t * t) * _GELU_C * (1.0 + 3.0 * 0.044715 * x * x)
    return y, dy


def _lane(shape):
    return lax.broadcasted_iota(jnp.int32, shape, 1)


def _row(shape):
    return lax.broadcasted_iota(jnp.int32, shape, 0)


def _gsum64(x):
    lo = _lane(x.shape) < 64
    s0 = jnp.sum(jnp.where(lo, x, 0.0), axis=-1, keepdims=True)
    s1 = jnp.sum(jnp.where(lo, 0.0, x), axis=-1, keepdims=True)
    return jnp.where(lo, s0, s1)


def _colreduce(x, op):
    parts = [x[r:r + 8, :] for r in range(0, x.shape[0], 8)]
    while len(parts) > 1:
        pairs = [op(parts[k], parts[k + 1]) for k in range(0, len(parts) - 1, 2)]
        parts = pairs + ([parts[-1]] if len(parts) % 2 else [])
    red = jnp.max if op is jnp.maximum else jnp.sum
    return red(parts[0], axis=0, keepdims=True)


def _block_diag64(dtype=BF16):
    r, c = _row((128, 128)), _lane((128, 128))
    return jnp.where((r >> 6) == (c >> 6), 1.0, 0.0).astype(dtype)


def _assemble_w_in(slab_ref, wt_ref):
    shard = slab_ref.shape[1]
    top = N_CHIPS * shard // 16 * 16
    wt_ref[top:, :] = jnp.zeros((wt_ref.shape[0] - top, wt_ref.shape[1]), wt_ref.dtype)
    for k in range(N_CHIPS):
        wt_ref[shard * k:shard * (k + 1), :] = slab_ref[k]


def _inproj(x, g, w, dp_width, tag):
    T, D = x.shape
    tm = _tile(T, 512)

    def body(x_ref, g_ref, w_ref, h_ref, p_ref, wt_ref):
        pl.when(pl.program_id(0) == 0)(lambda: _assemble_w_in(w_ref, wt_ref))
        xv = x_ref[...]
        r = lax.rsqrt(jnp.mean(xv * xv, axis=-1, keepdims=True) + NORM_EPS)
        h = (xv * r * g_ref[...]).astype(BF16)
        h_ref[...] = h
        p_ref[...] = _dot_nt(h, wt_ref[...])

    return pl.pallas_call(
        body, name=f"inproj_{tag}", grid=(T // tm,),
        in_specs=[pl.BlockSpec((tm, D), lambda i: (i, 0)), pl.BlockSpec((1, D), lambda i: (0, 0)),
                  pl.BlockSpec(w.shape, lambda i: (0, 0, 0))],
        out_specs=[pl.BlockSpec((tm, D), lambda i: (i, 0)), pl.BlockSpec((tm, dp_width), lambda i: (i, 0))],
        out_shape=[SDS((T, D), BF16), SDS((T, dp_width), F32)],
        scratch_shapes=[pltpu.VMEM((dp_width, D), BF16)],
        compiler_params=_params("arbitrary"),
    )(x, g, w)


def _outproj(x, ya, yb, yc, wo, tag):
    T, D = x.shape
    tm = _tile(T, 512)

    def body(x_ref, ya_ref, yb_ref, yc_ref, wo_ref, o_ref):
        acc = x_ref[...] + _dot(ya_ref[...], wo_ref[0:A_WIDTH, :])
        acc = acc + _dot(yb_ref[...], wo_ref[A_WIDTH:A_WIDTH + B_WIDTH, :])
        o_ref[...] = acc + _dot(yc_ref[...], wo_ref[A_WIDTH + B_WIDTH:, :])

    row = lambda w: pl.BlockSpec((tm, w), lambda i: (i, 0))
    return pl.pallas_call(
        body, name=f"outproj_{tag}", grid=(T // tm,),
        in_specs=[row(D), row(A_WIDTH), row(B_WIDTH), row(C_WIDTH), pl.BlockSpec(wo.shape, lambda i: (0, 0))],
        out_specs=row(D), out_shape=SDS((T, D), F32), compiler_params=_params("parallel"),
    )(x, ya, yb, yc, wo)


def _outproj_bwd(dx, ya, yb, yc, wo, tag):
    T, D = dx.shape
    DM = wo.shape[0]
    tm = _tile(T, 512)

    def body(dx_ref, ya_ref, yb_ref, yc_ref, wo_ref, dy_ref, dwo_ref):
        @pl.when(pl.program_id(0) == 0)
        def _():
            dwo_ref[...] = jnp.zeros_like(dwo_ref)

        dxb = dx_ref[...].astype(BF16)
        dy_ref[...] = _dot_nt(dxb, wo_ref[...])
        dwo_ref[0:A_WIDTH, :] += _dot_tn(ya_ref[...], dxb)
        dwo_ref[A_WIDTH:A_WIDTH + B_WIDTH, :] += _dot_tn(yb_ref[...], dxb)
        dwo_ref[A_WIDTH + B_WIDTH:, :] += _dot_tn(yc_ref[...], dxb)

    row = lambda w: pl.BlockSpec((tm, w), lambda i: (i, 0))
    return pl.pallas_call(
        body, name=f"outproj_bwd_{tag}", grid=(T // tm,),
        in_specs=[row(D), row(A_WIDTH), row(B_WIDTH), row(C_WIDTH), pl.BlockSpec(wo.shape, lambda i: (0, 0))],
        out_specs=[row(DM), pl.BlockSpec((DM, D), lambda i: (0, 0))],
        out_shape=[SDS((T, DM), F32), SDS((DM, D), F32)], compiler_params=_params("arbitrary"),
    )(dx, ya, yb, yc, wo)


def _piece_offsets(pieces):
    offs = [0]
    for p in pieces:
        offs.append(offs[-1] + p.shape[1])
    return offs


def _dw_in(h, pieces, dp_width, shard, tag, ride=None):
    T, D = h.shape
    assert N_CHIPS * shard <= dp_width
    tm = _tile(T, 512)
    grid = (T // tm,)
    offs = _piece_offsets(pieces)
    n = len(pieces)

    def body(h_ref, *rest):
        p_refs, rest = rest[:n], rest[n:]
        ride_srcs, (dw_ref,), ride_dsts, (acc_ref,), ride_sems = _ride_refs(ride, rest, 1, 1)
        i = pl.program_id(0)
        _ride_start(ride, grid, ride_srcs, ride_dsts, ride_sems)

        @pl.when(i == 0)
        def _():
            acc_ref[...] = jnp.zeros_like(acc_ref)

        hv = h_ref[...]
        for k, p_ref in enumerate(p_refs):
            acc_ref[offs[k]:offs[k + 1], :] += _dot_tn(p_ref[...], hv)

        @pl.when(i == grid[0] - 1)
        def _():
            for k in range(N_CHIPS):
                dw_ref[k] = acc_ref[shard * k:shard * (k + 1), :].astype(BF16)

        _ride_wait(ride, grid, ride_srcs, ride_dsts, ride_sems)

    extra = ride or _ChipExchange("gather", ())
    out = pl.pallas_call(
        body, name=f"dw_in_{tag}", grid=grid,
        in_specs=[pl.BlockSpec((tm, D), lambda i: (i, 0))] + [pl.BlockSpec((tm, p.shape[1]), lambda i: (i, 0)) for p in pieces]
        + extra.in_specs,
        out_specs=[pl.BlockSpec((N_CHIPS, shard, D), lambda i: (0, 0, 0))] + extra.out_specs,
        out_shape=[SDS((N_CHIPS, shard, D), BF16)] + extra.out_shape,
        scratch_shapes=[pltpu.VMEM((dp_width, D), F32)] + (extra.scratch if ride else []),
        compiler_params=pltpu.CompilerParams(dimension_semantics=("arbitrary",), vmem_limit_bytes=VMEM_LIMIT,
                                             has_side_effects=bool(ride)),
    )(h, *pieces, *extra.sources)
    return out if ride else out[0]


def _dx_in(x, g, dres, pieces, w, tag, ride=None):
    T, D = x.shape
    tm = _tile(T, 512)
    grid = (T // tm,)
    offs = _piece_offsets(pieces)
    n = len(pieces)

    def body(x_ref, g_ref, dres_ref, w_ref, *rest):
        p_refs, rest = rest[:n], rest[n:]
        ride_srcs, (dx_ref, dg_ref), ride_dsts, (wt_ref,), ride_sems = _ride_refs(ride, rest, 2, 1)
        _ride_start(ride, grid, ride_srcs, ride_dsts, ride_sems)

        @pl.when(pl.program_id(0) == 0)
        def _():
            dg_ref[...] = jnp.zeros_like(dg_ref)
            _assemble_w_in(w_ref, wt_ref)

        dh = _dot(p_refs[0][...], wt_ref[offs[0]:offs[1], :])
        for k in range(1, n):
            dh = dh + _dot(p_refs[k][...], wt_ref[offs[k]:offs[k + 1], :])
        xv = x_ref[...]
        r = lax.rsqrt(jnp.mean(xv * xv, axis=-1, keepdims=True) + NORM_EPS)
        xh = xv * r
        dg_ref[...] += jnp.sum(dh * xh, axis=0, keepdims=True)
        dxh = dh * g_ref[...]
        dx_ref[...] = dres_ref[...] + r * (dxh - xh * jnp.mean(dxh * xh, axis=-1, keepdims=True))
        _ride_wait(ride, grid, ride_srcs, ride_dsts, ride_sems)

    extra = ride or _ChipExchange("gather", ())
    row = pl.BlockSpec((tm, D), lambda i: (i, 0))
    return pl.pallas_call(
        body, name=f"dx_in_{tag}", grid=grid,
        in_specs=[row, pl.BlockSpec((1, D), lambda i: (0, 0)), row, pl.BlockSpec(w.shape, lambda i: (0, 0, 0))]
        + [pl.BlockSpec((tm, p.shape[1]), lambda i: (i, 0)) for p in pieces] + extra.in_specs,
        out_specs=[row, pl.BlockSpec((1, D), lambda i: (0, 0))] + extra.out_specs,
        out_shape=[SDS((T, D), F32), SDS((1, D), F32)] + extra.out_shape,
        scratch_shapes=[pltpu.VMEM((offs[-1], D), BF16)] + (extra.scratch if ride else []),
        compiler_params=pltpu.CompilerParams(dimension_semantics=("arbitrary",), vmem_limit_bytes=VMEM_LIMIT,
                                             has_side_effects=bool(ride)),
    )(x, g, dres, w, *pieces, *extra.sources)


def _loss_head(x, g, tgt):
    T, D = x.shape
    tm = _tile(T, 512)

    def body(x_ref, g_ref, t_ref, dx_ref, loss_ref, dg_ref):
        @pl.when(pl.program_id(0) == 0)
        def _():
            loss_ref[...] = jnp.zeros_like(loss_ref)
            dg_ref[...] = jnp.zeros_like(dg_ref)

        xv = x_ref[...]
        r = lax.rsqrt(jnp.mean(xv * xv, axis=-1, keepdims=True) + NORM_EPS)
        xh = xv * r
        gv = g_ref[...]
        err = xh * gv - t_ref[...]
        tok = jnp.mean(err * err, axis=-1, keepdims=True)
        loss_ref[...] += 0.5 * jnp.sum(tok, axis=0, keepdims=True)
        dy = err * (1.0 / D)
        dg_ref[...] += jnp.sum(dy * xh, axis=0, keepdims=True)
        dxh = dy * gv
        dx_ref[...] = r * (dxh - xh * jnp.mean(dxh * xh, axis=-1, keepdims=True))

    row = pl.BlockSpec((tm, D), lambda i: (i, 0))
    return pl.pallas_call(
        body, name="loss_head", grid=(T // tm,),
        in_specs=[row, pl.BlockSpec((1, D), lambda i: (0, 0)), row],
        out_specs=[row, pl.BlockSpec((1, 128), lambda i: (0, 0)), pl.BlockSpec((1, D), lambda i: (0, 0))],
        out_shape=[SDS((T, D), F32), SDS((1, 128), F32), SDS((1, D), F32)], compiler_params=_params("arbitrary"),
    )(x, g, tgt)


def _gmlp_core(u, v, lng, lnb, wm_ref, bst_ref, pair):
    ug, dug = _gelu_and_grad(u)
    vg, dvg = _gelu_and_grad(v)
    mu = _gsum64(vg) * (1.0 / 64)
    d = vg - mu
    var = _gsum64(d * d) * (1.0 / 64)
    rstd = lax.rsqrt(var + NORM_EPS)
    xh = d * rstd
    vn = xh * lng + lnb
    vnb = vn.astype(BF16)
    lo = _lane(u.shape) < 64
    g0, g1 = 2 * pair, 2 * pair + 1
    mixed = jnp.where(lo, _dot(wm_ref[g0], vnb) + bst_ref[:, g0:g0 + 1], _dot(wm_ref[g1], vnb) + bst_ref[:, g1:g1 + 1])
    return ug, dug, dvg, rstd, xh, vnb, mixed, lo


def _gmlp_fwd(proj, lng, lnb, wm, bst):
    T = proj.shape[0]

    def body(u_ref, v_ref, z_ref, lng_ref, lnb_ref, wm_ref, bst_ref, y_ref):
        for pair in range(2):
            sl = slice(128 * pair, 128 * pair + 128)
            ug, _, _, _, _, _, mixed, _ = _gmlp_core(u_ref[:, sl], v_ref[:, sl], lng_ref[:, sl], lnb_ref[:, sl],
                                                     wm_ref, bst_ref, pair)
            sz, _ = _silu_and_grad(z_ref[:, sl])
            y_ref[:, sl] = (ug * mixed * sz).astype(BF16)

    col = lambda c: pl.BlockSpec((CHUNK, A_WIDTH), lambda i, c=c: (i, c // A_WIDTH))
    full = lambda a: pl.BlockSpec(a.shape, lambda i, n=a.ndim: (0,) * n)
    return _Part(body, (proj, proj, proj, lng, lnb, wm, bst),
                 [col(COL_AU), col(COL_AV), col(COL_AZ), full(lng), full(lnb), full(wm), full(bst)],
                 [pl.BlockSpec((CHUNK, A_WIDTH), lambda i: (i, 0))], [SDS((T, A_WIDTH), BF16)], [])


def _gmlp_bwd(proj, dy, lng, lnb, wm, wmt, bst):
    T = proj.shape[0]
    n = T // CHUNK

    def body(u_ref, v_ref, z_ref, dy_ref, lng_ref, lnb_ref, wm_ref, wmt_ref, bst_ref,
             da_ref, dwm_ref, dbst_ref, dlng_ref, dlnb_ref):
        @pl.when(pl.program_id(0) == 0)
        def _():
            dwm_ref[...] = jnp.zeros_like(dwm_ref)
            dbst_ref[...] = jnp.zeros_like(dbst_ref)
            dlng_ref[...] = jnp.zeros_like(dlng_ref)
            dlnb_ref[...] = jnp.zeros_like(dlnb_ref)

        lane = _lane((CHUNK, 128))
        dbst = dbst_ref[...]
        for pair in range(2):
            sl = slice(128 * pair, 128 * pair + 128)
            lng_p = lng_ref[:, sl]
            ug, dug, dvg, rstd, xh, vnb, mixed, lo = _gmlp_core(u_ref[:, sl], v_ref[:, sl], lng_p, lnb_ref[:, sl],
                                                                wm_ref, bst_ref, pair)
            sz, dsz = _silu_and_grad(z_ref[:, sl])
            dyv = dy_ref[:, sl]
            out = ug * mixed
            dz = dyv * out * dsz
            dout = dyv * sz
            du = dout * mixed * dug
            dmix = dout * ug
            g0, g1 = 2 * pair, 2 * pair + 1
            dm0 = jnp.where(lo, dmix, 0.0)
            dm1 = jnp.where(lo, 0.0, dmix)
            dbst = dbst + jnp.where(lane == g0, jnp.sum(dm0, axis=-1, keepdims=True), 0.0)
            dbst = dbst + jnp.where(lane == g1, jnp.sum(dm1, axis=-1, keepdims=True), 0.0)
            dwm_ref[g0] += _dot_nt(dm0.astype(BF16), vnb)
            dwm_ref[g1] += _dot_nt(dm1.astype(BF16), vnb)
            dmb = dmix.astype(BF16)
            dvn = jnp.where(lo, _dot(wmt_ref[g0], dmb), _dot(wmt_ref[g1], dmb))
            dlng_ref[:, sl] += jnp.sum(dvn * xh, axis=0, keepdims=True)
            dlnb_ref[:, sl] += jnp.sum(dvn, axis=0, keepdims=True)
            dxh = dvn * lng_p
            m1 = _gsum64(dxh) * (1.0 / 64)
            m2 = _gsum64(dxh * xh) * (1.0 / 64)
            dv = rstd * (dxh - m1 - xh * m2) * dvg
            da_ref[:, COL_AU + 128 * pair:COL_AU + 128 * pair + 128] = du.astype(BF16)
            da_ref[:, COL_AV + 128 * pair:COL_AV + 128 * pair + 128] = dv.astype(BF16)
            da_ref[:, COL_AZ + 128 * pair:COL_AZ + 128 * pair + 128] = dz.astype(BF16)
        dbst_ref[...] = dbst

        @pl.when(pl.program_id(0) == n - 1)
        def _():
            causal = _lane((CHUNK, CHUNK)) <= _row((CHUNK, CHUNK))
            for g in range(A_GROUPS):
                dwm_ref[g] = jnp.where(causal, dwm_ref[g], 0.0)

    col = lambda c: pl.BlockSpec((CHUNK, A_WIDTH), lambda i, c=c: (i, c // A_WIDTH))
    full = lambda a: pl.BlockSpec(a.shape, lambda i, n=a.ndim: (0,) * n)
    acc = lambda s: pl.BlockSpec(s, lambda i, n=len(s): (0,) * n)
    return _Part(body, (proj, proj, proj, dy, lng, lnb, wm, wmt, bst),
                 [col(COL_AU), col(COL_AV), col(COL_AZ), pl.BlockSpec((CHUNK, A_WIDTH), lambda i: (i, 0)),
                  full(lng), full(lnb), full(wm), full(wmt), full(bst)],
                 [pl.BlockSpec((CHUNK, 3 * A_WIDTH), lambda i: (i, 0)), acc((A_GROUPS, CHUNK, CHUNK)),
                  acc((CHUNK, 128)), acc((1, A_WIDTH)), acc((1, A_WIDTH))],
                 [SDS((T, 3 * A_WIDTH), BF16), SDS((A_GROUPS, CHUNK, CHUNK), F32), SDS((CHUNK, 128), F32),
                  SDS((1, A_WIDTH), F32), SDS((1, A_WIDTH), F32)], [])


def _hgrn_consts():
    r, c = _row((CHUNK, CHUNK)), _lane((CHUNK, CHUNK))
    same = (r >> SUB_SHIFT) == (c >> SUB_SHIFT)
    lsub = jnp.where(same & (c <= r), 1.0, 0.0).astype(BF16)
    usub = jnp.where(same & (c >= r), 1.0, 0.0).astype(BF16)
    bsub = jnp.where(same, 1.0, 0.0).astype(BF16)
    return lsub, usub, bsub


def _hgrn_gates(qv, zf, lbp):
    sq, dsq = _silu_and_grad(qv)
    qt = sq * Q_SCALE
    sg = _sigmoid(zf)
    sgn = _sigmoid(-zf)
    f = lbp + (1.0 - lbp) * sg
    g = jnp.log(jnp.maximum(f, F_FLOOR))
    kf = (1.0 - lbp) * sgn
    return qt, dsq, sg, sgn, f, g, kf


def _hgrn_intra_scores(qt, kf, b, mbd):
    rid = _row((SUB, 128))
    parts = []
    for s in range(SUB):
        e = jnp.exp(b - b[s:s + 1, :])
        parts.append(jnp.where(rid >= s, qt * kf[s:s + 1, :] * e, 0.0))
    return _dot(jnp.concatenate(parts, axis=0).astype(BF16), mbd)


def _hgrn_intra_out(a, v):
    o = jnp.zeros((SUB, 128), F32)
    for s in range(SUB):
        o = o + a[SUB * s:SUB * s + SUB, :] * v[s:s + 1, :]
    return o


def _hgrn_intra_bwd_scores(qt, kf, b, v, do, mbd):
    rid = _row((SUB, 128))
    ps, das, kes, es = [], [], [], []
    for s in range(SUB):
        e = jnp.where(rid >= s, jnp.exp(b - b[s:s + 1, :]), 0.0)
        ke = kf[s:s + 1, :] * e
        es.append(e)
        kes.append(ke)
        ps.append(qt * ke)
        das.append(do * v[s:s + 1, :])
    a = _dot(jnp.concatenate(ps, axis=0).astype(BF16), mbd)
    da = _dot(jnp.concatenate(das, axis=0).astype(BF16), mbd)
    return a, da, kes, es


def _hgrn_intra_bwd_grads(scores, qt, do, rsum):
    a, da, kes, es = scores
    dqt = jnp.zeros((SUB, 128), F32)
    xs, ys = [], []
    for s in range(SUB):
        da_s = da[SUB * s:SUB * s + SUB, :]
        dqt = dqt + da_s * kes[s]
        xs.append(a[SUB * s:SUB * s + SUB, :] * do)
        ys.append(da_s * qt * es[s])
    dv = _dot(rsum, jnp.concatenate(xs, axis=0).astype(BF16))
    dkf = _dot(rsum, jnp.concatenate(ys, axis=0).astype(BF16))
    return dqt, dkf, dv


def _hgrn_norm_gate(o, z, onorm):
    ms = _gsum64(o * o) * (1.0 / 64)
    r = lax.rsqrt(ms + NORM_EPS)
    xh = o * r
    sz, dsz = _silu_and_grad(z)
    return xh, r, sz, dsz, xh * onorm


def _hgrn_fwd(proj, lb, onorm):
    T = proj.shape[0]
    n = T // CHUNK
    nsub = CHUNK // SUB

    def body(q_ref, f_ref, i_ref, z_ref, lb_ref, on_ref, y_ref, o_ref, s0_ref, st_ref):
        @pl.when(pl.program_id(0) == 0)
        def _():
            st_ref[...] = jnp.zeros_like(st_ref)

        lsub, _, bsub = _hgrn_consts()
        mbd = _block_diag64()
        bdmask = mbd > 0
        rid = _row((CHUNK, 128))
        subs = [slice(SUB * sub, SUB * sub + SUB) for sub in range(nsub)]
        work = []
        for pair in range(2):
            sl = slice(128 * pair, 128 * pair + 128)
            qt, _, _, _, _, g, kf = _hgrn_gates(q_ref[:, sl], f_ref[:, sl], lb_ref[:, sl])
            work.append(dict(sl=sl, qt=qt, kf=kf, v=i_ref[:, sl], b=_dot3_left(lsub, g), bl=_dot3_left(bsub, g)))
        for w in work:
            qt, kf, v, b, bl = w["qt"], w["kf"], w["v"], w["b"], w["bl"]
            w["qh"] = (qt * jnp.exp(b)).astype(BF16)
            kh = kf * jnp.exp(bl - b)
            w["dec"] = jnp.exp(bl)
            vtb = v.T.astype(BF16)
            w["scores"] = [_hgrn_intra_scores(qt[rs], kf[rs], b[rs], mbd) for rs in subs]
            w["adds"] = [_dot(vtb, jnp.where((rid >> SUB_SHIFT) == sub, kh, 0.0).astype(BF16)) for sub in range(nsub)]
        for pair, w in enumerate(work):
            w["st"] = st_ref[pair]
            s0_ref[0, pair] = w["st"]
            w["outs"] = []
        for sub, rs in enumerate(subs):
            for w in work:
                w["outs"].append(_dot_nt(w["qh"][rs], w["st"].astype(BF16)) + _hgrn_intra_out(w["scores"][sub], w["v"][rs]))
                w["st"] = jnp.where(bdmask, w["st"] * w["dec"][SUB * sub:SUB * sub + 1, :] + w["adds"][sub], 0.0)
        for pair, w in enumerate(work):
            sl = w["sl"]
            st_ref[pair] = w["st"]
            o = jnp.concatenate(w["outs"], axis=0)
            o_ref[:, sl] = o
            _, _, sz, _, on = _hgrn_norm_gate(o, z_ref[:, sl], on_ref[:, sl])
            y_ref[:, sl] = (on * sz).astype(BF16)

    col = lambda c: pl.BlockSpec((CHUNK, B_WIDTH), lambda i, c=c: (i, c // B_WIDTH))
    full = lambda a: pl.BlockSpec(a.shape, lambda i, n=a.ndim: (0,) * n)
    return _Part(body, (proj, proj, proj, proj, lb, onorm),
                 [col(COL_BQ), col(COL_BF), col(COL_BI), col(COL_BZ), full(lb), full(onorm)],
                 [pl.BlockSpec((CHUNK, B_WIDTH), lambda i: (i, 0)), pl.BlockSpec((CHUNK, B_WIDTH), lambda i: (i, 0)),
                  pl.BlockSpec((1, 2, 128, 128), lambda i: (i, 0, 0, 0))],
                 [SDS((T, B_WIDTH), BF16), SDS((T, B_WIDTH), F32), SDS((n, 2, 128, 128), F32)],
                 [pltpu.VMEM((2, 128, 128), F32)])


def _hgrn_bwd(proj, dy, o_saved, s0, lb, onorm):
    T = proj.shape[0]
    n = T // CHUNK
    nsub = CHUNK // SUB

    def body(q_ref, f_ref, i_ref, z_ref, dy_ref, o_ref, s0_ref, lb_ref, on_ref,
             db_ref, dlb_ref, don_ref, dst_ref, sts_ref):
        @pl.when(pl.program_id(0) == 0)
        def _():
            dst_ref[...] = jnp.zeros_like(dst_ref)
            dlb_ref[...] = jnp.zeros_like(dlb_ref)
            don_ref[...] = jnp.zeros_like(don_ref)

        lsub, usub, bsub = _hgrn_consts()
        mbd = _block_diag64()
        bdmask = mbd > 0
        rsum = jnp.where((_lane((SUB, SUB * SUB)) >> SUB_SHIFT) == _row((SUB, SUB * SUB)), 1.0, 0.0).astype(BF16)
        subs = [slice(SUB * sub, SUB * sub + SUB) for sub in range(nsub)]
        work = []
        for pair in range(2):
            sl = slice(128 * pair, 128 * pair + 128)
            lbp = lb_ref[:, sl]
            qt, dsq, sg, sgn, f, g, kf = _hgrn_gates(q_ref[:, sl], f_ref[:, sl], lbp)
            w = dict(sl=sl, lbp=lbp, qt=qt, dsq=dsq, sg=sg, sgn=sgn, f=f, kf=kf, v=i_ref[:, sl],
                     b=_dot3_left(lsub, g), bl=_dot3_left(bsub, g))
            onp = on_ref[:, sl]
            xh, r, sz, dsz, on = _hgrn_norm_gate(o_ref[:, sl], z_ref[:, sl], onp)
            dyv = dy_ref[:, sl]
            w["dz"] = dyv * on * dsz
            don = dyv * sz
            cn = jnp.sum(don * xh, axis=0, keepdims=True)
            don_ref[...] += cn + pltpu.roll(cn, 64, axis=1)
            dxo = don * onp
            w["do"] = r * (dxo - xh * (_gsum64(dxo * xh) * (1.0 / 64)))
            work.append(w)
        for w in work:
            qt, kf, v, b, bl, do = w["qt"], w["kf"], w["v"], w["b"], w["bl"], w["do"]
            w["eb"] = jnp.exp(b)
            w["ekb"] = jnp.exp(bl - b)
            w["qhb"] = (qt * w["eb"]).astype(BF16)
            w["khb"] = (kf * w["ekb"]).astype(BF16)
            w["dec"] = jnp.exp(bl)
            w["vb"] = v.astype(BF16)
            w["dob"] = do.astype(BF16)
            w["scores"] = [_hgrn_intra_bwd_scores(qt[rs], kf[rs], b[rs], v[rs], do[rs], mbd) for rs in subs]
            w["st_adds"] = [_dot_tn(w["vb"][rs], w["khb"][rs]) for rs in subs]
            w["gst_adds"] = [_dot_tn(w["dob"][rs], w["qhb"][rs]) for rs in subs]
        for pair, w in enumerate(work):
            w["st"] = s0_ref[0, pair]
        for sub in range(nsub):
            for pair, w in enumerate(work):
                sts_ref[pair, sub] = w["st"]
                w["st"] = jnp.where(bdmask, w["st"] * w["dec"][SUB * sub:SUB * sub + 1, :] + w["st_adds"][sub], 0.0)
        for pair, w in enumerate(work):
            w["gst"] = dst_ref[pair]
            w["dqt_p"], w["dkf_p"], w["dv_p"], w["dbl_p"] = ([None] * nsub for _ in range(4))
        for sub in reversed(range(nsub)):
            rs = subs[sub]
            for pair, w in enumerate(work):
                gst = w["gst"]
                st_in = sts_ref[pair, sub]
                gb = gst.astype(BF16)
                dqh = _dot(w["dob"][rs], st_in.astype(BF16))
                dkh = _dot(w["vb"][rs], gb)
                dv_inter = _dot_nt(w["khb"][rs], gb)
                ddec = jnp.sum(gst * st_in, axis=0, keepdims=True)
                dec_row = w["dec"][SUB * sub:SUB * sub + 1, :]
                w["gst"] = jnp.where(bdmask, gst * dec_row + w["gst_adds"][sub], 0.0)
                dqt_i, dkf_i, dv_i = _hgrn_intra_bwd_grads(w["scores"][sub], w["qt"][rs], w["do"][rs], rsum)
                dkf_inter = dkh * w["ekb"][rs]
                w["dqt_p"][sub] = dqh * w["eb"][rs] + dqt_i
                w["dkf_p"][sub] = dkf_inter + dkf_i
                w["dv_p"][sub] = dv_inter + dv_i
                row = jnp.sum(w["kf"][rs] * dkf_inter, axis=0, keepdims=True) + ddec * dec_row
                w["dbl_p"][sub] = jnp.broadcast_to(row, (SUB, 128))
        for pair, w in enumerate(work):
            sl, lbp, sg, sgn, f = w["sl"], w["lbp"], w["sg"], w["sgn"], w["f"]
            dst_ref[pair] = w["gst"]
            dqt = jnp.concatenate(w["dqt_p"], axis=0)
            dkf = jnp.concatenate(w["dkf_p"], axis=0)
            dv = jnp.concatenate(w["dv_p"], axis=0)
            dg = _dot3_left(usub, w["qt"] * dqt - w["kf"] * dkf) + jnp.concatenate(w["dbl_p"], axis=0)
            df = jnp.where(f > F_FLOOR, dg / f, 0.0)
            dlb_ref[:, sl] += jnp.sum(df * (1.0 - sg) - dkf * sgn, axis=0, keepdims=True)
            dfl = (1.0 - lbp) * sg * sgn * (df - dkf)
            dq = dqt * Q_SCALE * w["dsq"]
            db_ref[:, 0 * B_WIDTH + 128 * pair:0 * B_WIDTH + 128 * pair + 128] = dq.astype(BF16)
            db_ref[:, 1 * B_WIDTH + 128 * pair:1 * B_WIDTH + 128 * pair + 128] = dfl.astype(BF16)
            db_ref[:, 2 * B_WIDTH + 128 * pair:2 * B_WIDTH + 128 * pair + 128] = dv.astype(BF16)
            db_ref[:, 3 * B_WIDTH + 128 * pair:3 * B_WIDTH + 128 * pair + 128] = w["dz"].astype(BF16)

    rev = lambda c: pl.BlockSpec((CHUNK, B_WIDTH), lambda i, c=c: (n - 1 - i, c // B_WIDTH))
    full = lambda a: pl.BlockSpec(a.shape, lambda i, n_=a.ndim: (0,) * n_)
    acc = lambda s: pl.BlockSpec(s, lambda i, n_=len(s): (0,) * n_)
    return _Part(body, (proj, proj, proj, proj, dy, o_saved, s0, lb, onorm),
                 [rev(COL_BQ), rev(COL_BF), rev(COL_BI), rev(COL_BZ),
                  pl.BlockSpec((CHUNK, B_WIDTH), lambda i: (n - 1 - i, 1)),
                  pl.BlockSpec((CHUNK, B_WIDTH), lambda i: (n - 1 - i, 0)),
                  pl.BlockSpec((1, 2, 128, 128), lambda i: (n - 1 - i, 0, 0, 0)), full(lb), full(onorm)],
                 [pl.BlockSpec((CHUNK, 4 * B_WIDTH), lambda i: (n - 1 - i, 0)), acc((1, B_WIDTH)), acc((1, 128))],
                 [SDS((T, 4 * B_WIDTH), BF16), SDS((1, B_WIDTH), F32), SDS((1, 128), F32)],
                 [pltpu.VMEM((2, 128, 128), F32), pltpu.VMEM((2, nsub, 128, 128), F32)])


def _lb_fwd(hgrn_lb):
    assert hgrn_lb.shape[0] == 2

    def body(x_ref, o_ref):
        x0, x1 = x_ref[0:1, :], x_ref[1:2, :]
        m = jnp.maximum(x0, x1)
        e0, e1 = jnp.exp(x0 - m), jnp.exp(x1 - m)
        p0, p1 = e0 / (e0 + e1), e1 / (e0 + e1)
        o_ref[0:1, :] = jnp.clip(p0 - p0, 0.0, 1.0 - 1e-6)
        o_ref[1:2, :] = jnp.clip((p0 + p1) - p0, 0.0, 1.0 - 1e-6)

    return pl.pallas_call(body, name="lb_fwd", out_shape=SDS(hgrn_lb.shape, F32))(hgrn_lb)


def _lb_bwd(hgrn_lb, dlb):
    def body(x_ref, d_ref, o_ref):
        x0, x1 = x_ref[0:1, :], x_ref[1:2, :]
        m = jnp.maximum(x0, x1)
        e0, e1 = jnp.exp(x0 - m), jnp.exp(x1 - m)
        p0, p1 = e0 / (e0 + e1), e1 / (e0 + e1)
        val = (p0 + p1) - p0
        dp1 = jnp.where((val > 0.0) & (val < 1.0 - 1e-6), d_ref[1:2, :], 0.0)
        inner = p1 * dp1
        o_ref[0:1, :] = p0 * (0.0 - inner)
        o_ref[1:2, :] = p1 * (dp1 - inner)

    return pl.pallas_call(body, name="lb_bwd", out_shape=SDS(hgrn_lb.shape, F32))(hgrn_lb, dlb)


def _fox_prep(proj, bf):
    T = proj.shape[0]
    n = T // CHUNK

    def body(q0_ref, q1_ref, k0_ref, k1_ref, v0_ref, v1_ref, fl_ref, bf_ref, qo_ref, ko_ref, vt_ref, carry_ref):
        for p, v_ref in enumerate((v0_ref, v0_ref, v1_ref, v1_ref)):
            vt_ref[p, 0] = v_ref[:, 128 * (p % 2):128 * (p % 2) + 128].T.astype(BF16)

        @pl.when(pl.program_id(0) == 0)
        def _():
            carry_ref[...] = jnp.zeros_like(carry_ref)

        ltri = jnp.where(_lane((CHUNK, CHUNK)) <= _row((CHUNK, CHUNK)), 1.0, 0.0).astype(BF16)
        lf = jax.nn.log_sigmoid(fl_ref[...] + bf_ref[...])
        c = _dot3_left(ltri, lf) + carry_ref[...]
        carry_ref[...] = c[CHUNK - 1:CHUNK, :]
        lane = _lane((CHUNK, 128))
        feat = lane < 64
        ones_q = (lane >= 67) & (lane <= 69)
        ones_k = (lane >= 64) & (lane <= 66)
        qrefs, krefs = (q0_ref, q1_ref), (k0_ref, k1_ref)
        for h in range(C_HEADS):
            blk = slice(128 * ((h // 2) % 2), 128 * ((h // 2) % 2) + 128)
            qp, kp = qrefs[h // 4][:, blk], krefs[h // 4][:, blk]
            if h % 2:
                qp, kp = pltpu.roll(qp, 64, axis=1), pltpu.roll(kp, 64, axis=1)
            ch = jnp.broadcast_to(c[:, h:h + 1], (CHUNK, 128))
            hi = ch.astype(BF16).astype(F32)
            r1 = ch - hi
            mid = r1.astype(BF16).astype(F32)
            lo = r1 - mid
            aq = jnp.where(lane == 64, hi, jnp.where(lane == 65, mid, jnp.where(lane == 66, lo,
                           jnp.where(ones_q, 1.0, 0.0))))
            ak = jnp.where(lane == 67, -hi, jnp.where(lane == 68, -mid, jnp.where(lane == 69, -lo,
                           jnp.where(ones_k, 1.0, 0.0))))
            qo_ref[:, 128 * h:128 * h + 128] = jnp.where(feat, qp * Q_SCALE, aq).astype(BF16)
            ko_ref[:, 128 * h:128 * h + 128] = jnp.where(feat, kp, ak).astype(BF16)

    w = 256
    col = lambda c: pl.BlockSpec((CHUNK, w), lambda i, c=c: (i, c // w))
    return _Part(body, (proj, proj, proj, proj, proj, proj, proj, bf),
                 [col(COL_CQ), col(COL_CQ + w), col(COL_CK), col(COL_CK + w), col(COL_CV), col(COL_CV + w),
                  pl.BlockSpec((CHUNK, 128), lambda i: (i, COL_CF // 128)), pl.BlockSpec((1, 128), lambda i: (0, 0))],
                 [pl.BlockSpec((CHUNK, C_HEADS * 128), lambda i: (i, 0))] * 2
                 + [pl.BlockSpec((C_HEADS // 2, 1, 128, CHUNK), lambda i: (0, i, 0, 0))],
                 [SDS((T, C_HEADS * 128), BF16)] * 2 + [SDS((C_HEADS // 2, n, 128, CHUNK), BF16)],
                 [pltpu.VMEM((1, 128), F32)])


FOX_TILE = 512
FOX_KEYS = 512
FOX_STRIP = 16


def _fox_mask(tk, tq, k0, q0):
    return (_row((tk, tq)) + (k0 - q0)) <= _lane((tk, tq))


def _ride_refs(ride, rest, n_out, n_scratch):
    n = ride.n if ride else 0
    srcs, rest = rest[:n], rest[n:]
    outs, rest = rest[:n_out], rest[n_out:]
    dsts, rest = rest[:n], rest[n:]
    return srcs, outs, dsts, rest[:n_scratch], rest[n_scratch:]


def _ride_start(ride, grid, srcs, dsts, sems):
    if ride:
        first = functools.reduce(lambda a, b: a & b, [pl.program_id(d) == 0 for d in range(len(grid))])
        pl.when(first)(lambda: ride.start(srcs, dsts, sems))


def _ride_wait(ride, grid, srcs, dsts, sems):
    if ride:
        last = functools.reduce(lambda a, b: a & b, [pl.program_id(d) == n - 1 for d, n in enumerate(grid)])
        pl.when(last)(lambda: ride.wait(srcs, dsts, sems))


def _fox_fwd(qt, kt, vt, proj, tag, ride=None):
    T = proj.shape[0]
    tq, tk = _tile(T, FOX_TILE), _tile(T, FOX_KEYS)
    nq, nsub = T // tq, tk // CHUNK
    npair = C_HEADS // 2

    def body(q_ref, k_ref, vt_ref, z_ref, *rest):
        ride_srcs, (o_ref, lse_ref, y_ref), ride_dsts, (acc_ref, st_ref, pt_ref), ride_sems = _ride_refs(ride, rest, 3, 3)
        i = pl.program_id(1)
        _ride_start(ride, (npair, nq), ride_srcs, ride_dsts, ride_sems)

        qs = (q_ref[:, 0:128], q_ref[:, 128:256])
        acc_ref[...] = jnp.zeros_like(acc_ref)
        pt_ref[...] = jnp.zeros_like(pt_ref)
        nfull = (i * tq) // tk

        def scores(j):
            kb = k_ref[pl.ds(pl.multiple_of(j * tk, tk), tk), :]
            return tuple(_dot_nt(kb[:, 128 * h:128 * h + 128], qs[h]) for h in range(2))

        def weigh(j, h):
            rows = slice(64 * h, 64 * h + 64)
            vth = jnp.concatenate([vt_ref[0, nsub * j + c, rows, :] for c in range(nsub)], axis=1)
            return _dot(vth, pt_ref[h])

        def block(j, carry, diagonal):
            nxt = () if diagonal else scores(j + 1)
            pvs = [weigh(jnp.maximum(j - 1, 0), h) for h in range(2)]
            new = []
            for h in range(2):
                m, l, alpha_prev = carry[3 * h:3 * h + 3]
                st = st_ref[h]
                if diagonal:
                    st = jnp.where(_fox_mask(tk, tq, j * tk, i * tq), st, -jnp.inf)
                m_new = jnp.maximum(m, _colreduce(st, jnp.maximum))
                pt = jnp.exp(st - m_new)
                alpha = jnp.exp(m - m_new)
                rows = slice(64 * h, 64 * h + 64)
                acc_ref[rows, :] = alpha_prev * acc_ref[rows, :] + pvs[h]
                pt_ref[h] = pt.astype(BF16)
                new += [m_new, alpha * l + _colreduce(pt, jnp.add), alpha]
            for h, st in enumerate(nxt):
                st_ref[h] = st
            return tuple(new)

        for h, st in enumerate(scores(0)):
            st_ref[h] = st
        init = (jnp.full((1, tq), -jnp.inf, F32), jnp.zeros((1, tq), F32), jnp.ones((1, tq), F32)) * 2
        carry = lax.fori_loop(0, nfull, lambda j, c: block(j, c, False), init)
        m0, l0, a0, m1, l1, a1 = block(nfull, carry, True)
        for h, alpha in enumerate((a0, a1)):
            rows = slice(64 * h, 64 * h + 64)
            acc_ref[rows, :] = alpha * acc_ref[rows, :] + weigh(nfull, h)
        inv = jnp.where(_row((128, tq)) < 64, 1.0 / l0, 1.0 / l1)
        o = (acc_ref[...] * inv).T
        o_ref[...] = o
        r8 = _row((8, tq))
        lse_ref[0, 0] = jnp.where(r8 == 0, m0 + jnp.log(l0), jnp.where(r8 == 1, m1 + jnp.log(l1), 0.0))
        sz, _ = _silu_and_grad(z_ref[...])
        y_ref[...] = (o * sz).astype(BF16)
        _ride_wait(ride, (npair, nq), ride_srcs, ride_dsts, ride_sems)

    blk = pl.BlockSpec((tq, 128), lambda p, i: (i, p))
    extra = ride or _ChipExchange("gather", ())
    return pl.pallas_call(
        body, name=f"fox_fwd_{tag}", grid=(npair, nq),
        in_specs=[pl.BlockSpec((tq, 256), lambda p, i: (i, p)), pl.BlockSpec((T, 256), lambda p, i: (0, p)),
                  pl.BlockSpec((1, T // CHUNK, 128, CHUNK), lambda p, i: (p, 0, 0, 0)),
                  pl.BlockSpec((tq, 128), lambda p, i: (i, COL_CZ // 128 + p))] + extra.in_specs,
        out_specs=[blk, pl.BlockSpec((1, 1, 8, tq), lambda p, i: (p, i, 0, 0)), blk] + extra.out_specs,
        out_shape=[SDS((T, C_WIDTH), F32), SDS((npair, nq, 8, tq), F32), SDS((T, C_WIDTH), BF16)] + extra.out_shape,
        scratch_shapes=[pltpu.VMEM((128, tq), F32), pltpu.VMEM((2, tk, tq), F32), pltpu.VMEM((2, tk, tq), BF16)]
        + (extra.scratch if ride else []),
        compiler_params=pltpu.CompilerParams(dimension_semantics=("arbitrary", "arbitrary"), vmem_limit_bytes=VMEM_LIMIT,
                                             has_side_effects=bool(ride)),
    )(qt, kt, vt, proj, *extra.sources)


def _fox_bwd_prep(proj, dy, o, qt, tag):
    T = proj.shape[0]
    tq = _tile(T, FOX_TILE)
    nq = T // tq

    def body(z0_ref, z1_ref, dy_ref, o_ref, q_ref, do_ref, dl_ref, dz_ref, dot_ref, qt_ref):
        sel = jnp.where((_lane((16, 128)) >> 6) == _row((16, 128)), 1.0, 0.0).astype(BF16)
        for p, z_ref in enumerate((z0_ref, z0_ref, z1_ref, z1_ref)):
            sl = slice(128 * p, 128 * p + 128)
            sz, dsz = _silu_and_grad(z_ref[:, 128 * (p % 2):128 * (p % 2) + 128])
            dyv, ov = dy_ref[:, sl], o_ref[:, sl]
            do = dyv * sz
            do_ref[:, sl] = do.astype(BF16)
            dot_ref[p, 0] = do.T.astype(BF16)
            dz_ref[:, sl] = (dyv * ov * dsz).astype(BF16)
            hi, mid, lo = _split3(do * ov)
            dl_ref[p, 0] = (_dot_nt(sel, hi) + _dot_nt(sel, mid) + _dot_nt(sel, lo))[0:8, :]
        for h in range(C_HEADS):
            qt_ref[h, 0] = q_ref[:, 128 * h:128 * h + 128].astype(F32).T.astype(BF16)

    w = 256
    blk = pl.BlockSpec((tq, C_WIDTH), lambda i: (i, 0))
    return pl.pallas_call(
        body, name=f"fox_bwd_prep_{tag}", grid=(nq,),
        in_specs=[pl.BlockSpec((tq, w), lambda i: (i, COL_CZ // w)), pl.BlockSpec((tq, w), lambda i: (i, COL_CZ // w + 1)),
                  pl.BlockSpec((tq, C_WIDTH), lambda i: (i, (A_WIDTH + B_WIDTH) // C_WIDTH)), blk,
                  pl.BlockSpec((tq, C_HEADS * 128), lambda i: (i, 0))],
        out_specs=[blk, pl.BlockSpec((C_HEADS // 2, 1, 8, tq), lambda i: (0, i, 0, 0)), blk,
                   pl.BlockSpec((C_HEADS // 2, 1, 128, tq), lambda i: (0, i, 0, 0)),
                   pl.BlockSpec((C_HEADS, 1, 128, tq), lambda i: (0, i, 0, 0))],
        out_shape=[SDS((T, C_WIDTH), BF16), SDS((C_HEADS // 2, nq, 8, tq), F32), SDS((T, C_WIDTH), BF16),
                   SDS((C_HEADS // 2, nq, 128, tq), BF16), SDS((C_HEADS, nq, 128, tq), BF16)],
        compiler_params=_params("parallel"),
    )(proj, proj, dy, o, qt)


def _fox_bwd(qt, kt, proj, do, lse, delta, dot, qtr, tag, ride=None):
    T = proj.shape[0]
    tq, tk = _tile(T, FOX_TILE), _tile(T, FOX_KEYS)
    nq, nk = T // tq, T // tk
    assert tq == tk
    npair = C_HEADS // 2

    def body(q_ref, k_ref, v_ref, do_ref, lse_ref, dl_ref, dot_ref, qtr_ref, *rest):
        ride_srcs, (dq_ref, dk_ref, dv_ref), ride_dsts, scratch, ride_sems = _ride_refs(ride, rest, 3, 4)
        dvt_ref, dkt_ref, pt_ref, ds_ref = scratch
        j = pl.program_id(1)
        first = (j * tk) // tq
        _ride_start(ride, (npair, nk), ride_srcs, ride_dsts, ride_sems)

        @pl.when(j == 0)
        def _():
            dq_ref[...] = jnp.zeros_like(dq_ref)

        dkt_ref[...] = jnp.zeros_like(dkt_ref)
        dvt_ref[...] = jnp.zeros_like(dvt_ref)
        ks = (k_ref[:, 0:128], k_ref[:, 128:256])
        kts = tuple(k.astype(F32).T.astype(BF16) for k in ks)
        vb = v_ref[...].astype(BF16)
        lo = _lane((tq, 128)) < 64

        def operands(i):
            q0 = pl.multiple_of(i * tq, tq)
            qb = q_ref[pl.ds(q0, tq), :]
            dob = do_ref[pl.ds(q0, tq), :]
            qhs = (qb[:, 0:128], qb[:, 128:256])
            dohs = (jnp.where(lo, dob, jnp.zeros_like(dob)), jnp.where(lo, jnp.zeros_like(dob), dob))
            return qhs, dohs

        def scores(i):
            qhs, dohs = operands(i)
            return tuple((_dot_nt(ks[h], qhs[h]), _dot_nt(vb, dohs[h])) for h in range(2))

        def grads(i, slot):
            for h in range(2):
                rows = slice(64 * h, 64 * h + 64)
                dvt_ref[rows, :] += _dot_nt(dot_ref[0, i, rows, :], pt_ref[slot, h])
                dkt_ref[h] += _dot_nt(qtr_ref[h, i], ds_ref[slot, h])
                dq_ref[h, i] += _dot(kts[h], ds_ref[slot, h])

        def block(i, slot, diagonal, opening):
            sc = scores(i)
            if not opening:
                grads(i - 1, 1 - slot)
            lsev = lse_ref[0, i]
            dlv = dl_ref[0, i]
            for h in range(2):
                lseh = jnp.broadcast_to(lsev[h:h + 1, :], (FOX_STRIP, tq))
                dlh = jnp.broadcast_to(dlv[h:h + 1, :], (FOX_STRIP, tq))
                for r in range(0, tk, FOX_STRIP):
                    rows = slice(r, r + FOX_STRIP)
                    pt = jnp.exp(sc[h][0][rows, :] - lseh)
                    if diagonal:
                        pt = jnp.where(_fox_mask(FOX_STRIP, tq, r, 0), pt, 0.0)
                    ds_ref[slot, h, rows, :] = (pt * (sc[h][1][rows, :] - dlh)).astype(BF16)
                    pt_ref[slot, h, rows, :] = pt.astype(BF16)

        block(first, 0, True, True)
        rest = nq - 1 - first

        def two_steps(t, carry):
            block(first + 1 + 2 * t, 1, False, False)
            block(first + 2 + 2 * t, 0, False, False)
            return carry

        lax.fori_loop(0, rest // 2, two_steps, 0)
        pl.when(rest % 2 == 1)(lambda: block(nq - 1, 1, False, False))
        grads(nq - 1, rest % 2)
        dv_ref[...] = dvt_ref[...].T.astype(BF16)
        for h in range(2):
            dk_ref[:, 128 * h:128 * h + 128] = dkt_ref[h].T
        _ride_wait(ride, (npair, nk), ride_srcs, ride_dsts, ride_sems)

    full = lambda w: pl.BlockSpec((T, w), lambda p, j: (0, p))
    stat = pl.BlockSpec((1, nq, 8, tq), lambda p, j: (p, 0, 0, 0))
    extra = ride or _ChipExchange("gather", ())
    return pl.pallas_call(
        body, name=f"fox_bwd_{tag}", grid=(npair, nk),
        in_specs=[full(256), pl.BlockSpec((tk, 256), lambda p, j: (j, p)),
                  pl.BlockSpec((tk, 128), lambda p, j: (j, COL_CV // 128 + p)), full(128), stat, stat,
                  pl.BlockSpec((1, nq, 128, tq), lambda p, j: (p, 0, 0, 0)),
                  pl.BlockSpec((2, nq, 128, tq), lambda p, j: (p, 0, 0, 0))] + extra.in_specs,
        out_specs=[pl.BlockSpec((2, nq, 128, tq), lambda p, j: (p, 0, 0, 0)), pl.BlockSpec((tk, 256), lambda p, j: (j, p)),
                   pl.BlockSpec((tk, 128), lambda p, j: (j, p))] + extra.out_specs,
        out_shape=[SDS((C_HEADS, nq, 128, tq), F32), SDS((T, C_HEADS * 128), F32), SDS((T, C_WIDTH), BF16)]
        + extra.out_shape,
        scratch_shapes=[pltpu.VMEM((128, tk), F32), pltpu.VMEM((2, 128, tk), F32),
                        pltpu.VMEM((2, 2, tk, tq), BF16), pltpu.VMEM((2, 2, tk, tq), BF16)]
        + (extra.scratch if ride else []),
        compiler_params=pltpu.CompilerParams(dimension_semantics=("arbitrary", "arbitrary"), vmem_limit_bytes=VMEM_LIMIT,
                                             has_side_effects=bool(ride)),
    )(qt, kt, proj, do, lse, delta, dot, qtr, *extra.sources)


def _fox_bwd_post(dqt, dkt, proj, bf, tag):
    T = proj.shape[0]
    tq = _tile(T, FOX_TILE)
    n = T // tq

    def body(dq_ref, dk_ref, fl_ref, bf_ref, oq_ref, ok_ref, ofl_ref, dbf_ref, carry_ref):
        @pl.when(pl.program_id(0) == 0)
        def _():
            carry_ref[...] = jnp.zeros_like(carry_ref)
            dbf_ref[...] = jnp.zeros_like(dbf_ref)

        lane = _lane((tq, 128))
        lo = lane < 64
        dqs = [dq_ref[h, 0].T for h in range(C_HEADS)]
        dc = jnp.zeros((tq, 128), F32)
        for h in range(C_HEADS):
            dc = dc + jnp.where(lane == h, dqs[h][:, 64:65] - dk_ref[:, 128 * h + 67:128 * h + 68], 0.0)
        utri = jnp.where(_lane((tq, tq)) >= _row((tq, tq)), 1.0, 0.0).astype(BF16)
        dlf = _dot3_left(utri, dc) + carry_ref[...]
        carry_ref[...] = dlf[0:1, :]
        dfl = jnp.where(lane < C_HEADS, dlf * _sigmoid(-(fl_ref[...] + bf_ref[...])), 0.0)
        ofl_ref[...] = dfl.astype(BF16)
        dbf_ref[...] += jnp.sum(dfl, axis=0, keepdims=True)
        for p in range(C_HEADS // 2):
            a, b = 128 * (2 * p), 128 * (2 * p + 1)
            oq_ref[:, 128 * p:128 * p + 128] = (
                jnp.where(lo, dqs[2 * p], pltpu.roll(dqs[2 * p + 1], 64, axis=1)) * Q_SCALE).astype(BF16)
            ok_ref[:, 128 * p:128 * p + 128] = jnp.where(
                lo, dk_ref[:, a:a + 128], pltpu.roll(dk_ref[:, b:b + 128], 64, axis=1)).astype(BF16)

    rev = lambda w: pl.BlockSpec((tq, w), lambda i: (n - 1 - i, 0))
    return pl.pallas_call(
        body, name=f"fox_bwd_post_{tag}", grid=(n,),
        in_specs=[pl.BlockSpec((C_HEADS, 1, 128, tq), lambda i: (0, n - 1 - i, 0, 0)), rev(C_HEADS * 128),
                  pl.BlockSpec((tq, 128), lambda i: (n - 1 - i, COL_CF // 128)), pl.BlockSpec((1, 128), lambda i: (0, 0))],
        out_specs=[rev(C_WIDTH), rev(C_WIDTH), rev(128), pl.BlockSpec((1, 128), lambda i: (0, 0))],
        out_shape=[SDS((T, C_WIDTH), BF16), SDS((T, C_WIDTH), BF16), SDS((T, 128), BF16), SDS((1, 128), F32)],
        scratch_shapes=[pltpu.VMEM((1, 128), F32)], compiler_params=_params("arbitrary"),
    )(dqt, dkt, proj, bf)


def _adamw_math(w, g, m, v):
    m = ADAM_B1 * m + (1.0 - ADAM_B1) * g
    v = ADAM_B2 * v + (1.0 - ADAM_B2) * (g * g)
    delta = -ADAM_LR * ((m / ADAM_C1) / (jnp.sqrt(v / ADAM_C2) + ADAM_EPS) + ADAM_WD * w)
    return delta, m, v


def _adamw_pair(w, m, v, ga, gb, name):
    n0 = w.shape[0]
    most = max(1, ADAMW_BLOCK_BYTES // (4 * math.prod(w.shape[1:])))
    t0 = max(t for t in range(1, min(n0, most) + 1) if n0 % t == 0)

    def body(w_ref, m_ref, v_ref, ga_ref, gb_ref, g_ref, d_ref, nm_ref, nv_ref):
        g = ga_ref[...] + gb_ref[...]
        g_ref[...] = g
        d_ref[...], nm_ref[...], nv_ref[...] = _adamw_math(w_ref[...], g, m_ref[...], v_ref[...])

    blk = pl.BlockSpec((t0,) + w.shape[1:], lambda i: (i, 0, 0))
    return pl.pallas_call(
        body, name=name, grid=(n0 // t0,), in_specs=[blk] * 5, out_specs=[blk] * 4,
        out_shape=[SDS(w.shape, F32)] * 4, compiler_params=_params("parallel"),
    )(w, m, v, ga, gb)


def _adamw_small(ws, ms, vs, gall):
    offs = _small_offsets()
    n = len(ws)

    def body(*refs):
        w_refs, m_refs, v_refs, g_ref = refs[:n], refs[n:2 * n], refs[2 * n:3 * n], refs[3 * n]
        outs = refs[3 * n + 1:]

        def total(off, rows):
            g = g_ref[0, off:off + rows, :]
            for dev in range(1, N_DEV):
                g = g + g_ref[dev, off:off + rows, :]
            return g

        for k in range(n):
            g = total(offs[k], ws[k].shape[0])
            go_ref, d_ref, nm_ref, nv_ref = outs[4 * k:4 * k + 4]
            go_ref[...] = g
            d_ref[...], nm_ref[...], nv_ref[...] = _adamw_math(w_refs[k][...], g, m_refs[k][...], v_refs[k][...])
        outs[4 * n][...] = total(offs[n], 1)

    shapes = [SDS(w.shape, F32) for w in ws for _ in range(4)] + [SDS((1, 128), F32)]
    res = pl.pallas_call(body, name="adamw_small", out_shape=shapes,
                         compiler_params=pltpu.CompilerParams(vmem_limit_bytes=VMEM_LIMIT))(*ws, *ms, *vs, gall)
    return [res[4 * k:4 * k + 4] for k in range(n)], res[4 * n]


def _pack_grads(dlng, dlnb, dwm, dbst, dlb, donorm, dbf, dfinal, loss_part):
    offs = _small_offsets()
    base = offs[1]
    L = len(dwm)
    assert L == 2

    def body(*refs):
        lng, lnb, wm, bst, on, bf = (refs[L * a:L * a + L] for a in range(6))
        lb_ref, fin_ref, loss_ref, o_ref = refs[6 * L:]
        o_ref[...] = jnp.zeros_like(o_ref)
        lane = _lane((1, 128))
        for l in range(L):
            for j in range(2):
                o_ref[offs[1] - base + 2 * l + j:offs[1] - base + 2 * l + j + 1, :] = lng[l][:, 128 * j:128 * j + 128]
                o_ref[offs[2] - base + 2 * l + j:offs[2] - base + 2 * l + j + 1, :] = lnb[l][:, 128 * j:128 * j + 128]
                o_ref[offs[5] - base + 2 * l + j:offs[5] - base + 2 * l + j + 1, :] = lb_ref[l:l + 1, 128 * j:128 * j + 128]
            for g in range(A_GROUPS):
                row = offs[3] - base + (A_GROUPS * l + g) * CHUNK
                o_ref[row:row + CHUNK, :] = wm[l][g]
            o_ref[offs[4] - base + A_GROUPS * l:offs[4] - base + A_GROUPS * (l + 1), :] = bst[l][...].T[0:A_GROUPS, :]
        o_ref[offs[6] - base:offs[6] - base + 1, :] = jnp.where(lane < 64, on[0][...], pltpu.roll(on[1][...], 64, axis=1))
        o_ref[offs[7] - base:offs[7] - base + 1, :] = jnp.where(
            lane < C_HEADS, bf[0][...], jnp.where(lane < 2 * C_HEADS, pltpu.roll(bf[1][...], C_HEADS, axis=1), 0.0))
        for j in range(D_MODEL // 128):
            o_ref[offs[8] - base + j:offs[8] - base + j + 1, :] = fin_ref[:, 128 * j:128 * j + 128]
        o_ref[offs[9] - base:offs[9] - base + 1, :] = loss_ref[...]

    rows = offs[9] + 8 - base
    return pl.pallas_call(body, name="pack_grads", out_shape=SDS((rows, 128), F32))(
        *dlng, *dlnb, *dwm, *dbst, *donorm, *dbf, dlb, dfinal, loss_part)


def _sum_chips(layers, name, layer_major):
    _, R, C = layers[0].shape
    L = len(layers)
    tc = _tile(C, 256)

    def body(*refs):
        o_ref = refs[-1]
        for l, p_ref in enumerate(refs[:-1]):
            p = [p_ref[k].astype(F32) for k in range(N_CHIPS)]
            s = ((p[0] + p[1]) + p[2]) + p[3]
            if layer_major:
                o_ref[l] = s
            else:
                o_ref[:, l, :] = s

    out = (L, R, C) if layer_major else (R, L, C)
    out_blk = (L, R, tc) if layer_major else (R, L, tc)
    return pl.pallas_call(
        body, name=name, grid=(C // tc,),
        in_specs=[pl.BlockSpec((N_CHIPS, R, tc), lambda i: (0, 0, i))] * L,
        out_specs=pl.BlockSpec(out_blk, lambda i: (0, 0, i)), out_shape=SDS(out, F32),
        compiler_params=_params("parallel"),
    )(*layers)


ANY = pl.BlockSpec(memory_space=pl.ANY)


def _mesh_pos():
    return lax.axis_index("x"), lax.axis_index("y"), lax.axis_index("c")


def _other_chips(x, y):
    return [(1 - x, y), (x, 1 - y), (1 - x, 1 - y)]


class _ChipExchange:
    def __init__(self, mode, sources):
        assert mode in ("gather", "scatter")
        self.mode, self.sources = mode, tuple(sources)
        self.n = len(self.sources)
        self.in_specs = [ANY] * self.n
        self.out_specs = [ANY] * self.n
        self.out_shape = [SDS(((N_CHIPS,) + s.shape) if mode == "gather" else s.shape, s.dtype) for s in self.sources]
        self.scratch = [pltpu.SemaphoreType.DMA((3 * self.n,)), pltpu.SemaphoreType.DMA((3 * self.n,)),
                        pltpu.SemaphoreType.DMA((self.n,))]

    def _copies(self, srcs, dsts, send_sems, recv_sems, local_sems):
        x, y, c = _mesh_pos()
        me = 2 * x + y
        view = (lambda r, chip: r) if self.mode == "gather" else (lambda r, chip: r.at[chip])
        local = [pltpu.make_async_copy(view(s, me), d.at[me], local_sems.at[a]) for a, (s, d) in enumerate(zip(srcs, dsts))]
        sends, recvs = [], []
        for j, (px, py) in enumerate(_other_chips(x, y)):
            peer = 2 * px + py
            for a, (s, d) in enumerate(zip(srcs, dsts)):
                sems = dict(send_sem=send_sems.at[self.n * j + a], recv_sem=recv_sems.at[self.n * j + a],
                            device_id=(px, py, c), device_id_type=MESH_ID)
                sends.append(pltpu.make_async_remote_copy(src_ref=view(s, peer), dst_ref=d.at[me], **sems))
                recvs.append(pltpu.make_async_remote_copy(src_ref=view(s, me), dst_ref=d.at[peer], **sems))
        return local, sends, recvs

    def start(self, srcs, dsts, sems):
        local, sends, _ = self._copies(srcs, dsts, *sems)
        for cp in local + sends:
            cp.start()

    def wait(self, srcs, dsts, sems):
        local, sends, recvs = self._copies(srcs, dsts, *sems)
        for cp in recvs:
            cp.wait_recv()
        for cp in sends:
            cp.wait_send()
        for cp in local:
            cp.wait()


def _gather_halves(w, tag):
    R, C = w.shape
    H = C // 2

    def body(w_ref, g_ref, send_sems, recv_sems, pass_send, pass_recv, local_sem):
        x, y, c = _mesh_pos()
        me = 2 * x + y
        mine, theirs = pl.ds(pl.multiple_of(c * H, H), H), pl.ds(pl.multiple_of((1 - c) * H, H), H)
        own = pltpu.make_async_copy(w_ref, g_ref.at[me], local_sem)
        own.start()

        def fetch(j, px, py, src, dst):
            return pltpu.make_async_remote_copy(src_ref=src, dst_ref=dst, send_sem=send_sems.at[j], recv_sem=recv_sems.at[j],
                                                device_id=(px, py, c), device_id_type=MESH_ID)

        def hand(j, cols, peer):
            return pltpu.make_async_remote_copy(src_ref=g_ref.at[peer, :, cols], dst_ref=g_ref.at[peer, :, cols],
                                                send_sem=pass_send.at[j], recv_sem=pass_recv.at[j],
                                                device_id=(x, y, 1 - c), device_id_type=MESH_ID)

        chips = _other_chips(x, y)
        sends = [fetch(j, px, py, w_ref.at[:, mine], g_ref.at[me, :, mine]) for j, (px, py) in enumerate(chips)]
        for cp in sends:
            cp.start()
        passed = []
        for j, (px, py) in enumerate(chips):
            peer = 2 * px + py
            fetch(j, px, py, w_ref.at[:, mine], g_ref.at[peer, :, mine]).wait_recv()
            passed.append(hand(j, mine, peer))
            passed[-1].start()
        for j, (px, py) in enumerate(chips):
            hand(j, theirs, 2 * px + py).wait_recv()
        for cp in sends + passed:
            cp.wait_send()
        own.wait()

    return pl.pallas_call(
        body, name=f"gather_halves_{tag}", in_specs=[ANY], out_specs=ANY, out_shape=SDS((N_CHIPS, R, C), w.dtype),
        scratch_shapes=[pltpu.SemaphoreType.DMA((3,)), pltpu.SemaphoreType.DMA((3,)), pltpu.SemaphoreType.DMA((3,)),
                        pltpu.SemaphoreType.DMA((3,)), pltpu.SemaphoreType.DMA],
        compiler_params=pltpu.CompilerParams(has_side_effects=True),
    )(w)


class _DeviceGather:
    def __init__(self, source):
        self.sources, self.n = (source,), 1
        self.in_specs, self.out_specs = [ANY], [ANY]
        self.out_shape = [SDS((N_DEV,) + source.shape, source.dtype)]
        self.scratch = [pltpu.SemaphoreType.DMA((N_DEV - 1,)), pltpu.SemaphoreType.DMA((N_DEV - 1,)),
                        pltpu.SemaphoreType.DMA((1,))]

    def _copies(self, srcs, dsts, send_sems, recv_sems, local_sems):
        (src,), (dst,) = srcs, dsts
        x, y, c = _mesh_pos()
        me = 4 * x + 2 * y + c
        local = [pltpu.make_async_copy(src, dst.at[me], local_sems.at[0])]
        sends, recvs = [], []
        for k in range(1, N_DEV):
            px, py, pc = (1 - x) if k & 4 else x, (1 - y) if k & 2 else y, (1 - c) if k & 1 else c
            sems = dict(send_sem=send_sems.at[k - 1], recv_sem=recv_sems.at[k - 1], device_id=(px, py, pc),
                        device_id_type=MESH_ID)
            sends.append(pltpu.make_async_remote_copy(src_ref=src, dst_ref=dst.at[me], **sems))
            recvs.append(pltpu.make_async_remote_copy(src_ref=src, dst_ref=dst.at[4 * px + 2 * py + pc], **sems))
        return local, sends, recvs

    start = _ChipExchange.start
    wait = _ChipExchange.wait


def _gather_devices(a, name):
    ex = _DeviceGather(a)

    def body(a_ref, g_ref, *sems):
        ex.start((a_ref,), (g_ref,), sems)
        ex.wait((a_ref,), (g_ref,), sems)

    return pl.pallas_call(
        body, name=name, in_specs=ex.in_specs, out_specs=ex.out_specs[0], out_shape=ex.out_shape[0],
        scratch_shapes=ex.scratch, compiler_params=pltpu.CompilerParams(has_side_effects=True),
    )(a)


def _swap_cores(pin, pout):
    def body(pin_ref, pout_ref, oin_ref, oout_ref, send_sems, recv_sems):
        x, y, c = _mesh_pos()
        cps = [pltpu.make_async_remote_copy(src_ref=src, dst_ref=dst, send_sem=send_sems.at[a], recv_sem=recv_sems.at[a],
                                            device_id=(x, y, 1 - c), device_id_type=MESH_ID)
               for a, (src, dst) in enumerate(((pin_ref, oin_ref), (pout_ref, oout_ref)))]
        for cp in cps:
            cp.start()
        for cp in cps:
            cp.wait()

    return pl.pallas_call(
        body, name="swap_cores", in_specs=[ANY, ANY], out_specs=[ANY, ANY],
        out_shape=[SDS(pin.shape, F32), SDS(pout.shape, F32)],
        scratch_shapes=[pltpu.SemaphoreType.DMA((2,)), pltpu.SemaphoreType.DMA((2,))],
        compiler_params=pltpu.CompilerParams(has_side_effects=True),
    )(pin, pout)


PACK_TILE = 8 * 128


def _pack_rows(size):
    return (size + PACK_TILE - 1) // PACK_TILE * 8


def _small_offsets():
    offs = [0]
    for _, shape in SMALL_PARAMS:
        offs.append(offs[-1] + _pack_rows(math.prod(shape)))
    return offs


def _rows_view(a):
    flat = a.reshape(-1)
    return jnp.pad(flat, (0, (-flat.size) % 128)).reshape(-1, 128)


def _from_rows(rows, shape):
    return rows.reshape(-1)[:math.prod(shape)].reshape(shape)


def _layer_consts(l, gmlp_ln_g, gmlp_ln_b, gmlp_w_s, gmlp_b_s, hgrn_onorm_g, fox_b_f):
    causal = jnp.tril(jnp.ones((CHUNK, CHUNK), bool))
    wm = jnp.where(causal[None], gmlp_w_s[l], 0.0)
    return dict(
        lng=gmlp_ln_g[l].reshape(1, A_WIDTH), lnb=gmlp_ln_b[l].reshape(1, A_WIDTH),
        wm=wm.astype(BF16), wmt=jnp.swapaxes(wm, 1, 2).astype(BF16),
        bst=jnp.pad(gmlp_b_s[l].T, ((0, 0), (0, 128 - A_GROUPS))),
        onorm=jnp.tile(hgrn_onorm_g[l], 4).reshape(1, B_WIDTH),
        bf=jnp.pad(fox_b_f[l], (0, 128 - C_HEADS)).reshape(1, 128),
    )


def kernel(x, norm_g, w_in, w_out, gmlp_ln_g, gmlp_ln_b, gmlp_w_s, gmlp_b_s, hgrn_lb, hgrn_onorm_g, fox_b_f, final_norm_g, loss_target, m_norm_g, m_w_in, m_w_out, m_gmlp_ln_g, m_gmlp_ln_b, m_gmlp_w_s, m_gmlp_b_s, m_hgrn_lb, m_hgrn_onorm_g, m_fox_b_f, m_final_norm_g, v_norm_g, v_w_in, v_w_out, v_gmlp_ln_g, v_gmlp_ln_b, v_gmlp_w_s, v_gmlp_b_s, v_hgrn_lb, v_hgrn_onorm_g, v_fox_b_f, v_final_norm_g):
    T = x.shape[1]
    shard_in = w_in.shape[2]
    shard_out = w_out.shape[1]
    xs = x.reshape(T, D_MODEL)
    tgt = loss_target.reshape(T, D_MODEL)

    w_in_b = [w_in[l].T.astype(BF16) for l in range(DEPTH)]
    w_out_b = w_out.astype(BF16)

    lb_all = _lb_fwd(hgrn_lb)
    consts = [_layer_consts(l, gmlp_ln_g, gmlp_ln_b, gmlp_w_s, gmlp_b_s, hgrn_onorm_g, fox_b_f) for l in range(DEPTH)]

    saved = []
    xl = xs
    w_in_l = _gather_halves(w_in_b[0], "w_in_l0")
    for l in range(DEPTH):
        cs = consts[l]
        tag = f"l{l}"
        h, proj = _inproj(xl, norm_g[l].reshape(1, D_MODEL), w_in_l, D_IN_PAD, tag)
        (ya,), (yb, ob, s0), (qt, kt, vt) = _run_parts(
            [_gmlp_fwd(proj, cs["lng"], cs["lnb"], cs["wm"], cs["bst"]),
             _hgrn_fwd(proj, lb_all[l].reshape(1, B_WIDTH), cs["onorm"]), _fox_prep(proj, cs["bf"])],
            (T // CHUNK,), f"mix_fwd_{tag}")
        ride = _ChipExchange("gather", (w_out_b[l],) + ((w_in_b[l + 1],) if l + 1 < DEPTH else ()))
        oc, lse, yc, *gathered = _fox_fwd(qt, kt, vt, proj, tag, ride)
        w_out_l = gathered[0].reshape(N_CHIPS * shard_out, D_MODEL)
        saved.append(dict(x=xl, h=h, proj=proj, ya=ya, yb=yb, yc=yc, ob=ob, s0=s0, qt=qt, kt=kt, oc=oc, lse=lse,
                          w_in=w_in_l, w_out=w_out_l))
        xl = _outproj(xl, ya, yb, yc, w_out_l, tag)
        if l + 1 < DEPTH:
            w_in_l = gathered[1]

    dx, loss_part, d_final = _loss_head(xl, final_norm_g.reshape(1, D_MODEL), tgt)

    g_small = {}
    dlb_rows, rin, rout = [None] * DEPTH, [None] * DEPTH, [None] * DEPTH
    slabs_in = None
    for l in reversed(range(DEPTH)):
        cs, sv = consts[l], saved[l]
        tag = f"l{l}"
        proj = sv["proj"]
        dy, dw_out = _outproj_bwd(dx, sv["ya"], sv["yb"], sv["yc"], sv["w_out"], tag)
        (da, dwm, dbst, dlng, dlnb), (db, dlb_rows[l], donorm) = _run_parts(
            [_gmlp_bwd(proj, dy, cs["lng"], cs["lnb"], cs["wm"], cs["wmt"], cs["bst"]),
             _hgrn_bwd(proj, dy, sv["ob"], sv["s0"], lb_all[l].reshape(1, B_WIDTH), cs["onorm"])],
            (T // CHUNK,), f"mix_bwd_{tag}")
        do, delta, dzc, dot, qtr = _fox_bwd_prep(proj, dy, sv["oc"], sv["qt"], tag)
        slabs_out = dw_out.reshape(N_CHIPS, shard_out, D_MODEL).astype(BF16)
        ride = _ChipExchange("scatter", (slabs_out,) + ((slabs_in,) if slabs_in is not None else ()))
        dqt, dkt, dvc, *received = _fox_bwd(sv["qt"], sv["kt"], proj, do, sv["lse"], delta, dot, qtr, tag, ride)
        rout[l] = received[0]
        if slabs_in is not None:
            rin[l + 1] = received[1]
        dqc, dkc, dflc, dbf = _fox_bwd_post(dqt, dkt, proj, cs["bf"], tag)
        g_small[l] = dict(ln_g=dlng, ln_b=dlnb, w_s=dwm, b_s=dbst, onorm=donorm, bf=dbf)
        dproj = [da, db, dqc, dkc, dvc, dzc, dflc]
        if l == 0:
            d_hgrn_lb = _lb_bwd(hgrn_lb, jnp.concatenate(dlb_rows, axis=0))
            per_layer = lambda key: [g_small[k][key] for k in range(DEPTH)]
            early = _pack_grads(per_layer("ln_g"), per_layer("ln_b"), per_layer("w_s"), per_layer("b_s"), d_hgrn_lb,
                                per_layer("onorm"), per_layer("bf"), d_final, loss_part)
            slabs_in, rearly = _dw_in(sv["h"], dproj, D_IN_PAD, shard_in, tag, _DeviceGather(early))
        else:
            slabs_in = _dw_in(sv["h"], dproj, D_IN_PAD, shard_in, tag)
        ride = _ChipExchange("scatter", (slabs_in,)) if l == 0 else None
        dx, dng, *received = _dx_in(sv["x"], norm_g[l].reshape(1, D_MODEL), dx, dproj, sv["w_in"], tag, ride)
        if l == 0:
            rin[0] = received[0]
        g_small[l]["norm_g"] = dng.reshape(D_MODEL // 128, 128)
    grad_x = dx.reshape(x.shape)
    rlate = _gather_devices(jnp.concatenate([g_small[l]["norm_g"] for l in range(DEPTH)]), "gather_norm_grads")
    rsmall = jnp.concatenate([rlate, rearly], axis=1)

    pin, pout = _sum_chips(rin, "sum_chips_w_in", False), _sum_chips(rout, "sum_chips_w_out", True)
    oin, oout = _swap_cores(pin, pout)
    to_view = lambda a: jnp.transpose(a, (2, 0, 1))
    g_w_in, d_w_in, nm_w_in, nv_w_in = [
        jnp.transpose(o, (1, 2, 0))
        for o in _adamw_pair(to_view(w_in), to_view(m_w_in), to_view(v_w_in), pin, oin, "adamw_w_in")]
    g_w_out, d_w_out, nm_w_out, nv_w_out = _adamw_pair(w_out, m_w_out, v_w_out, pout, oout, "adamw_w_out")

    small_w = [norm_g, gmlp_ln_g, gmlp_ln_b, gmlp_w_s, gmlp_b_s, hgrn_lb, hgrn_onorm_g, fox_b_f, final_norm_g]
    small_m = [m_norm_g, m_gmlp_ln_g, m_gmlp_ln_b, m_gmlp_w_s, m_gmlp_b_s, m_hgrn_lb, m_hgrn_onorm_g, m_fox_b_f, m_final_norm_g]
    small_v = [v_norm_g, v_gmlp_ln_g, v_gmlp_ln_b, v_gmlp_w_s, v_gmlp_b_s, v_hgrn_lb, v_hgrn_onorm_g, v_fox_b_f, v_final_norm_g]
    views = lambda ps: [_rows_view(p) for p in ps]
    per_param, loss_row = _adamw_small(views(small_w), views(small_m), views(small_v), rsmall)
    sg, sd, sm, sv_ = [[_from_rows(per_param[k][a], shape) for k, (_, shape) in enumerate(SMALL_PARAMS)] for a in range(4)]
    loss = loss_row[0, 0]

    def order(big_in, big_out, small):
        return [small[0], big_in, big_out] + small[1:]

    return (loss, grad_x, *order(g_w_in, g_w_out, sg), *order(d_w_in, d_w_out, sd), *order(nm_w_in, nm_w_out, sm),
            *order(nv_w_in, nv_w_out, sv_))
```

```python
import collections
import functools
import math

import jax
import jax.numpy as jnp
from jax import lax
from jax.experimental import pallas as pl
from jax.experimental.pallas import tpu as pltpu

F32 = jnp.float32
BF16 = jnp.bfloat16
SDS = jax.ShapeDtypeStruct
MESH_ID = pl.DeviceIdType.MESH

D_MODEL = 1024
DEPTH = 2
A_WIDTH = 256
A_GROUPS = 4
B_WIDTH = 256
C_WIDTH = 512
C_HEADS = 8
D_IN = 3848
D_IN_PAD = 4096
CHUNK = 128
SUB = 16
SUB_SHIFT = 4
NORM_EPS = 1e-6
F_FLOOR = 1e-30
COL_AU, COL_AV, COL_AZ = 0, 256, 512
COL_BQ, COL_BF, COL_BI, COL_BZ = 768, 1024, 1280, 1536
COL_CQ, COL_CK, COL_CV, COL_CZ, COL_CF = 1792, 2304, 2816, 3328, 3840
HEAD_LANES = 128
Q_SCALE = 0.125
ADAM_LR, ADAM_B1, ADAM_B2, ADAM_EPS, ADAM_WD, ADAM_STEP = 0.001, 0.9, 0.999, 1e-08, 0.01, 10
ADAM_C1 = 1.0 - ADAM_B1 ** ADAM_STEP
ADAM_C2 = 1.0 - ADAM_B2 ** ADAM_STEP
VMEM_LIMIT = 56 * 1024 * 1024
ADAMW_BLOCK_BYTES = 1 << 20
N_CHIPS = 4
N_DEV = 8

SMALL_PARAMS = (
    ("norm_g", (DEPTH, D_MODEL)), ("gmlp_ln_g", (DEPTH, 4, 64)), ("gmlp_ln_b", (DEPTH, 4, 64)),
    ("gmlp_w_s", (DEPTH, 4, 128, 128)), ("gmlp_b_s", (DEPTH, 4, 128)), ("hgrn_lb", (DEPTH, 256)),
    ("hgrn_onorm_g", (DEPTH, 64)), ("fox_b_f", (DEPTH, 8)), ("final_norm_g", (D_MODEL,)),
)


def _tile(n, pref):
    t = min(n, pref)
    assert n % t == 0, (n, pref)
    return t


def _params(*sem):
    return pltpu.CompilerParams(dimension_semantics=sem, vmem_limit_bytes=VMEM_LIMIT)


_Part = collections.namedtuple("_Part", "body operands in_specs out_specs out_shape scratch")


def _run_parts(parts, grid, name):
    counts = [(len(p.operands), len(p.out_shape), len(p.scratch)) for p in parts]

    def body(*refs):
        ins, outs, scr = [], [], []
        pos = 0
        for group, k in ((ins, 0), (outs, 1), (scr, 2)):
            for c in counts:
                group.append(refs[pos:pos + c[k]])
                pos += c[k]
        for p, i, o, s in zip(parts, ins, outs, scr):
            p.body(*i, *o, *s)

    flat = lambda key: [x for p in parts for x in getattr(p, key)]
    res = pl.pallas_call(
        body, name=name, grid=grid, in_specs=flat("in_specs"), out_specs=flat("out_specs"), out_shape=flat("out_shape"),
        scratch_shapes=flat("scratch"), compiler_params=_params(*(("arbitrary",) * len(grid))),
    )(*flat("operands"))
    out, pos = [], 0
    for c in counts:
        out.append(list(res[pos:pos + c[1]]))
        pos += c[1]
    return out


def _dot(a, b):
    return jnp.dot(a, b, preferred_element_type=F32)


def _dot_nt(a, b):
    return lax.dot_general(a, b, (((1,), (1,)), ((), ())), preferred_element_type=F32)


def _dot_tn(a, b):
    return lax.dot_general(a, b, (((0,), (0,)), ((), ())), preferred_element_type=F32)


def _split3(x):
    hi = x.astype(BF16)
    r = x - hi.astype(F32)
    mid = r.astype(BF16)
    lo = (r - mid.astype(F32)).astype(BF16)
    return hi, mid, lo


def _dot3_left(c, x):
    hi, mid, lo = _split3(x)
    return _dot(c, hi) + _dot(c, mid) + _dot(c, lo)


def _sigmoid(x):
    return jax.nn.sigmoid(x)


def _silu_and_grad(x):
    s = _sigmoid(x)
    return x * s, s * (1.0 + x * (1.0 - s))


_GELU_C = math.sqrt(2.0 / math.pi)


def _gelu_and_grad(x):
    inner = _GELU_C * (x + 0.044715 * x * x * x)
    t = jnp.tanh(inner)
    y = 0.5 * x * (1.0 + t)
    dy = 0.5 * (1.0 + t) + 0.5 * x * (1.0 - t * t) * _GELU_C * (1.0 + 3.0 * 0.044715 * x * x)
    return y, dy


def _lane(shape):
    return lax.broadcasted_iota(jnp.int32, shape, 1)


def _row(shape):
    return lax.broadcasted_iota(jnp.int32, shape, 0)


def _gsum64(x):
    lo = _lane(x.shape) < 64
    s0 = jnp.sum(jnp.where(lo, x, 0.0), axis=-1, keepdims=True)
    s1 = jnp.sum(jnp.where(lo, 0.0, x), axis=-1, keepdims=True)
    return jnp.where(lo, s0, s1)


def _colreduce(x, op):
    parts = [x[r:r + 8, :] for r in range(0, x.shape[0], 8)]
    while len(parts) > 1:
        pairs = [op(parts[k], parts[k + 1]) for k in range(0, len(parts) - 1, 2)]
        parts = pairs + ([parts[-1]] if len(parts) % 2 else [])
    red = jnp.max if op is jnp.maximum else jnp.sum
    return red(parts[0], axis=0, keepdims=True)


def _block_diag64(dtype=BF16):
    r, c = _row((128, 128)), _lane((128, 128))
    return jnp.where((r >> 6) == (c >> 6), 1.0, 0.0).astype(dtype)


def _assemble_w_in(slab_ref, wt_ref):
    shard = slab_ref.shape[1]
    top = N_CHIPS * shard // 16 * 16
    wt_ref[top:, :] = jnp.zeros((wt_ref.shape[0] - top, wt_ref.shape[1]), wt_ref.dtype)
    for k in range(N_CHIPS):
        wt_ref[shard * k:shard * (k + 1), :] = slab_ref[k]


def _inproj(x, g, w, dp_width, tag):
    T, D = x.shape
    tm = _tile(T, 512)

    def body(x_ref, g_ref, w_ref, h_ref, p_ref, wt_ref):
        pl.when(pl.program_id(0) == 0)(lambda: _assemble_w_in(w_ref, wt_ref))
        xv = x_ref[...]
        r = lax.rsqrt(jnp.mean(xv * xv, axis=-1, keepdims=True) + NORM_EPS)
        h = (xv * r * g_ref[...]).astype(BF16)
        h_ref[...] = h
        p_ref[...] = _dot_nt(h, wt_ref[...])

    return pl.pallas_call(
        body, name=f"inproj_{tag}", grid=(T // tm,),
        in_specs=[pl.BlockSpec((tm, D), lambda i: (i, 0)), pl.BlockSpec((1, D), lambda i: (0, 0)),
                  pl.BlockSpec(w.shape, lambda i: (0, 0, 0))],
        out_specs=[pl.BlockSpec((tm, D), lambda i: (i, 0)), pl.BlockSpec((tm, dp_width), lambda i: (i, 0))],
        out_shape=[SDS((T, D), BF16), SDS((T, dp_width), F32)],
        scratch_shapes=[pltpu.VMEM((dp_width, D), BF16)],
        compiler_params=_params("arbitrary"),
    )(x, g, w)


def _outproj(x, ya, yb, yc, wo, tag):
    T, D = x.shape
    tm = _tile(T, 512)

    def body(x_ref, ya_ref, yb_ref, yc_ref, wo_ref, o_ref):
        acc = x_ref[...] + _dot(ya_ref[...], wo_ref[0:A_WIDTH, :])
        acc = acc + _dot(yb_ref[...], wo_ref[A_WIDTH:A_WIDTH + B_WIDTH, :])
        o_ref[...] = acc + _dot(yc_ref[...], wo_ref[A_WIDTH + B_WIDTH:, :])

    row = lambda w: pl.BlockSpec((tm, w), lambda i: (i, 0))
    return pl.pallas_call(
        body, name=f"outproj_{tag}", grid=(T // tm,),
        in_specs=[row(D), row(A_WIDTH), row(B_WIDTH), row(C_WIDTH), pl.BlockSpec(wo.shape, lambda i: (0, 0))],
        out_specs=row(D), out_shape=SDS((T, D), F32), compiler_params=_params("parallel"),
    )(x, ya, yb, yc, wo)


def _outproj_bwd(dx, ya, yb, yc, wo, tag):
    T, D = dx.shape
    DM = wo.shape[0]
    tm = _tile(T, 512)

    def body(dx_ref, ya_ref, yb_ref, yc_ref, wo_ref, dy_ref, dwo_ref):
        @pl.when(pl.program_id(0) == 0)
        def _():
            dwo_ref[...] = jnp.zeros_like(dwo_ref)

        dxb = dx_ref[...].astype(BF16)
        dy_ref[...] = _dot_nt(dxb, wo_ref[...])
        dwo_ref[0:A_WIDTH, :] += _dot_tn(ya_ref[...], dxb)
        dwo_ref[A_WIDTH:A_WIDTH + B_WIDTH, :] += _dot_tn(yb_ref[...], dxb)
        dwo_ref[A_WIDTH + B_WIDTH:, :] += _dot_tn(yc_ref[...], dxb)

    row = lambda w: pl.BlockSpec((tm, w), lambda i: (i, 0))
    return pl.pallas_call(
        body, name=f"outproj_bwd_{tag}", grid=(T // tm,),
        in_specs=[row(D), row(A_WIDTH), row(B_WIDTH), row(C_WIDTH), pl.BlockSpec(wo.shape, lambda i: (0, 0))],
        out_specs=[row(DM), pl.BlockSpec((DM, D), lambda i: (0, 0))],
        out_shape=[SDS((T, DM), F32), SDS((DM, D), F32)], compiler_params=_params("arbitrary"),
    )(dx, ya, yb, yc, wo)


DW_IN_GROUPS = ((0, 256), (256, 256), (512, 512))


def _piece_offsets(pieces):
    offs = [0]
    for p in pieces:
        offs.append(offs[-1] + p.shape[1])
    return offs


def _dw_in(h, pieces, dp_width, shard, cols, tag, ride=None):
    T = h.shape[0]
    first, D = cols
    assert N_CHIPS * shard <= dp_width and first % D == 0
    tm = _tile(T, 512)
    grid = (T // tm,)
    offs = _piece_offsets(pieces)
    n = len(pieces)

    def body(h_ref, *rest):
        p_refs, rest = rest[:n], rest[n:]
        ride_srcs, (dw_ref,), ride_dsts, (acc_ref,), ride_sems = _ride_refs(ride, rest, 1, 1)
        i = pl.program_id(0)
        _ride_start(ride, grid, ride_srcs, ride_dsts, ride_sems)

        @pl.when(i == 0)
        def _():
            acc_ref[...] = jnp.zeros_like(acc_ref)

        hv = h_ref[...]
        for k, p_ref in enumerate(p_refs):
            acc_ref[offs[k]:offs[k + 1], :] += _dot_tn(p_ref[...], hv)

        @pl.when(i == grid[0] - 1)
        def _():
            for k in range(N_CHIPS):
                dw_ref[k] = acc_ref[shard * k:shard * (k + 1), :].astype(BF16)

        _ride_wait(ride, grid, ride_srcs, ride_dsts, ride_sems)

    extra = ride or _ChipExchange("gather", ())
    out = pl.pallas_call(
        body, name=f"dw_in_{tag}", grid=grid,
        in_specs=[pl.BlockSpec((tm, D), lambda i: (i, first // D))]
        + [pl.BlockSpec((tm, p.shape[1]), lambda i: (i, 0)) for p in pieces] + extra.in_specs,
        out_specs=[pl.BlockSpec((N_CHIPS, shard, D), lambda i: (0, 0, 0))] + extra.out_specs,
        out_shape=[SDS((N_CHIPS, shard, D), BF16)] + extra.out_shape,
        scratch_shapes=[pltpu.VMEM((dp_width, D), F32)] + (extra.scratch if ride else []),
        compiler_params=pltpu.CompilerParams(dimension_semantics=("arbitrary",), vmem_limit_bytes=VMEM_LIMIT,
                                             has_side_effects=bool(ride)),
    )(h, *pieces, *extra.sources)
    return out if ride else out[0]


def _dx_in(x, g, dres, pieces, w, tag, ride=None):
    T, D = x.shape
    tm = _tile(T, 512)
    grid = (T // tm,)
    offs = _piece_offsets(pieces)
    n = len(pieces)

    def body(x_ref, g_ref, dres_ref, w_ref, *rest):
        p_refs, rest = rest[:n], rest[n:]
        ride_srcs, (dx_ref, dg_ref), ride_dsts, (wt_ref,), ride_sems = _ride_refs(ride, rest, 2, 1)
        _ride_start(ride, grid, ride_srcs, ride_dsts, ride_sems)

        @pl.when(pl.program_id(0) == 0)
        def _():
            dg_ref[...] = jnp.zeros_like(dg_ref)
            _assemble_w_in(w_ref, wt_ref)

        dh = _dot(p_refs[0][...], wt_ref[offs[0]:offs[1], :])
        for k in range(1, n):
            dh = dh + _dot(p_refs[k][...], wt_ref[offs[k]:offs[k + 1], :])
        xv = x_ref[...]
        r = lax.rsqrt(jnp.mean(xv * xv, axis=-1, keepdims=True) + NORM_EPS)
        xh = xv * r
        dg_ref[...] += jnp.sum(dh * xh, axis=0, keepdims=True)
        dxh = dh * g_ref[...]
        dx_ref[...] = dres_ref[...] + r * (dxh - xh * jnp.mean(dxh * xh, axis=-1, keepdims=True))
        _ride_wait(ride, grid, ride_srcs, ride_dsts, ride_sems)

    extra = ride or _ChipExchange("gather", ())
    row = pl.BlockSpec((tm, D), lambda i: (i, 0))
    return pl.pallas_call(
        body, name=f"dx_in_{tag}", grid=grid,
        in_specs=[row, pl.BlockSpec((1, D), lambda i: (0, 0)), row, pl.BlockSpec(w.shape, lambda i: (0, 0, 0))]
        + [pl.BlockSpec((tm, p.shape[1]), lambda i: (i, 0)) for p in pieces] + extra.in_specs,
        out_specs=[row, pl.BlockSpec((1, D), lambda i: (0, 0))] + extra.out_specs,
        out_shape=[SDS((T, D), F32), SDS((1, D), F32)] + extra.out_shape,
        scratch_shapes=[pltpu.VMEM((offs[-1], D), BF16)] + (extra.scratch if ride else []),
        compiler_params=pltpu.CompilerParams(dimension_semantics=("arbitrary",), vmem_limit_bytes=VMEM_LIMIT,
                                             has_side_effects=bool(ride)),
    )(x, g, dres, w, *pieces, *extra.sources)


def _loss_head(x, g, tgt):
    T, D = x.shape
    tm = _tile(T, 512)

    def body(x_ref, g_ref, t_ref, dx_ref, loss_ref, dg_ref):
        @pl.when(pl.program_id(0) == 0)
        def _():
            loss_ref[...] = jnp.zeros_like(loss_ref)
            dg_ref[...] = jnp.zeros_like(dg_ref)

        xv = x_ref[...]
        r = lax.rsqrt(jnp.mean(xv * xv, axis=-1, keepdims=True) + NORM_EPS)
        xh = xv * r
        gv = g_ref[...]
        err = xh * gv - t_ref[...]
        tok = jnp.mean(err * err, axis=-1, keepdims=True)
        loss_ref[...] += 0.5 * jnp.sum(tok, axis=0, keepdims=True)
        dy = err * (1.0 / D)
        dg_ref[...] += jnp.sum(dy * xh, axis=0, keepdims=True)
        dxh = dy * gv
        dx_ref[...] = r * (dxh - xh * jnp.mean(dxh * xh, axis=-1, keepdims=True))

    row = pl.BlockSpec((tm, D), lambda i: (i, 0))
    return pl.pallas_call(
        body, name="loss_head", grid=(T // tm,),
        in_specs=[row, pl.BlockSpec((1, D), lambda i: (0, 0)), row],
        out_specs=[row, pl.BlockSpec((1, 128), lambda i: (0, 0)), pl.BlockSpec((1, D), lambda i: (0, 0))],
        out_shape=[SDS((T, D), F32), SDS((1, 128), F32), SDS((1, D), F32)], compiler_params=_params("arbitrary"),
    )(x, g, tgt)


def _gmlp_core(u, v, lng, lnb, wm_ref, bst_ref, pair):
    ug, dug = _gelu_and_grad(u)
    vg, dvg = _gelu_and_grad(v)
    mu = _gsum64(vg) * (1.0 / 64)
    d = vg - mu
    var = _gsum64(d * d) * (1.0 / 64)
    rstd = lax.rsqrt(var + NORM_EPS)
    xh = d * rstd
    vn = xh * lng + lnb
    vnb = vn.astype(BF16)
    lo = _lane(u.shape) < 64
    g0, g1 = 2 * pair, 2 * pair + 1
    mixed = jnp.where(lo, _dot(wm_ref[g0], vnb) + bst_ref[:, g0:g0 + 1], _dot(wm_ref[g1], vnb) + bst_ref[:, g1:g1 + 1])
    return ug, dug, dvg, rstd, xh, vnb, mixed, lo


def _gmlp_fwd(proj, lng, lnb, wm, bst):
    T = proj.shape[0]

    def body(u_ref, v_ref, z_ref, lng_ref, lnb_ref, wm_ref, bst_ref, y_ref):
        for pair in range(2):
            sl = slice(128 * pair, 128 * pair + 128)
            ug, _, _, _, _, _, mixed, _ = _gmlp_core(u_ref[:, sl], v_ref[:, sl], lng_ref[:, sl], lnb_ref[:, sl],
                                                     wm_ref, bst_ref, pair)
            sz, _ = _silu_and_grad(z_ref[:, sl])
            y_ref[:, sl] = (ug * mixed * sz).astype(BF16)

    col = lambda c: pl.BlockSpec((CHUNK, A_WIDTH), lambda i, c=c: (i, c // A_WIDTH))
    full = lambda a: pl.BlockSpec(a.shape, lambda i, n=a.ndim: (0,) * n)
    return _Part(body, (proj, proj, proj, lng, lnb, wm, bst),
                 [col(COL_AU), col(COL_AV), col(COL_AZ), full(lng), full(lnb), full(wm), full(bst)],
                 [pl.BlockSpec((CHUNK, A_WIDTH), lambda i: (i, 0))], [SDS((T, A_WIDTH), BF16)], [])


def _gmlp_bwd(proj, dy, lng, lnb, wm, wmt, bst):
    T = proj.shape[0]
    n = T // CHUNK

    def body(u_ref, v_ref, z_ref, dy_ref, lng_ref, lnb_ref, wm_ref, wmt_ref, bst_ref,
             da_ref, dwm_ref, dbst_ref, dlng_ref, dlnb_ref):
        @pl.when(pl.program_id(0) == 0)
        def _():
            dwm_ref[...] = jnp.zeros_like(dwm_ref)
            dbst_ref[...] = jnp.zeros_like(dbst_ref)
            dlng_ref[...] = jnp.zeros_like(dlng_ref)
            dlnb_ref[...] = jnp.zeros_like(dlnb_ref)

        lane = _lane((CHUNK, 128))
        dbst = dbst_ref[...]
        for pair in range(2):
            sl = slice(128 * pair, 128 * pair + 128)
            lng_p = lng_ref[:, sl]
            ug, dug, dvg, rstd, xh, vnb, mixed, lo = _gmlp_core(u_ref[:, sl], v_ref[:, sl], lng_p, lnb_ref[:, sl],
                                                                wm_ref, bst_ref, pair)
            sz, dsz = _silu_and_grad(z_ref[:, sl])
            dyv = dy_ref[:, sl]
            out = ug * mixed
            dz = dyv * out * dsz
            dout = dyv * sz
            du = dout * mixed * dug
            dmix = dout * ug
            g0, g1 = 2 * pair, 2 * pair + 1
            dm0 = jnp.where(lo, dmix, 0.0)
            dm1 = jnp.where(lo, 0.0, dmix)
            dbst = dbst + jnp.where(lane == g0, jnp.sum(dm0, axis=-1, keepdims=True), 0.0)
            dbst = dbst + jnp.where(lane == g1, jnp.sum(dm1, axis=-1, keepdims=True), 0.0)
            dwm_ref[g0] += _dot_nt(dm0.astype(BF16), vnb)
            dwm_ref[g1] += _dot_nt(dm1.astype(BF16), vnb)
            dmb = dmix.astype(BF16)
            dvn = jnp.where(lo, _dot(wmt_ref[g0], dmb), _dot(wmt_ref[g1], dmb))
            dlng_ref[:, sl] += jnp.sum(dvn * xh, axis=0, keepdims=True)
            dlnb_ref[:, sl] += jnp.sum(dvn, axis=0, keepdims=True)
            dxh = dvn * lng_p
            m1 = _gsum64(dxh) * (1.0 / 64)
            m2 = _gsum64(dxh * xh) * (1.0 / 64)
            dv = rstd * (dxh - m1 - xh * m2) * dvg
            da_ref[:, COL_AU + 128 * pair:COL_AU + 128 * pair + 128] = du.astype(BF16)
            da_ref[:, COL_AV + 128 * pair:COL_AV + 128 * pair + 128] = dv.astype(BF16)
            da_ref[:, COL_AZ + 128 * pair:COL_AZ + 128 * pair + 128] = dz.astype(BF16)
        dbst_ref[...] = dbst

        @pl.when(pl.program_id(0) == n - 1)
        def _():
            causal = _lane((CHUNK, CHUNK)) <= _row((CHUNK, CHUNK))
            for g in range(A_GROUPS):
                dwm_ref[g] = jnp.where(causal, dwm_ref[g], 0.0)

    col = lambda c: pl.BlockSpec((CHUNK, A_WIDTH), lambda i, c=c: (i, c // A_WIDTH))
    full = lambda a: pl.BlockSpec(a.shape, lambda i, n=a.ndim: (0,) * n)
    acc = lambda s: pl.BlockSpec(s, lambda i, n=len(s): (0,) * n)
    return _Part(body, (proj, proj, proj, dy, lng, lnb, wm, wmt, bst),
                 [col(COL_AU), col(COL_AV), col(COL_AZ), pl.BlockSpec((CHUNK, A_WIDTH), lambda i: (i, 0)),
                  full(lng), full(lnb), full(wm), full(wmt), full(bst)],
                 [pl.BlockSpec((CHUNK, 3 * A_WIDTH), lambda i: (i, 0)), acc((A_GROUPS, CHUNK, CHUNK)),
                  acc((CHUNK, 128)), acc((1, A_WIDTH)), acc((1, A_WIDTH))],
                 [SDS((T, 3 * A_WIDTH), BF16), SDS((A_GROUPS, CHUNK, CHUNK), F32), SDS((CHUNK, 128), F32),
                  SDS((1, A_WIDTH), F32), SDS((1, A_WIDTH), F32)], [])


def _hgrn_consts():
    r, c = _row((CHUNK, CHUNK)), _lane((CHUNK, CHUNK))
    same = (r >> SUB_SHIFT) == (c >> SUB_SHIFT)
    lsub = jnp.where(same & (c <= r), 1.0, 0.0).astype(BF16)
    usub = jnp.where(same & (c >= r), 1.0, 0.0).astype(BF16)
    bsub = jnp.where(same, 1.0, 0.0).astype(BF16)
    return lsub, usub, bsub


def _hgrn_gates(qv, zf, lbp):
    sq, dsq = _silu_and_grad(qv)
    qt = sq * Q_SCALE
    sg = _sigmoid(zf)
    sgn = _sigmoid(-zf)
    f = lbp + (1.0 - lbp) * sg
    g = jnp.log(jnp.maximum(f, F_FLOOR))
    kf = (1.0 - lbp) * sgn
    return qt, dsq, sg, sgn, f, g, kf


def _hgrn_intra_scores(qt, kf, b, mbd):
    rid = _row((SUB, 128))
    parts = []
    for s in range(SUB):
        e = jnp.exp(b - b[s:s + 1, :])
        parts.append(jnp.where(rid >= s, qt * kf[s:s + 1, :] * e, 0.0))
    return _dot(jnp.concatenate(parts, axis=0).astype(BF16), mbd)


def _hgrn_intra_out(a, v):
    o = jnp.zeros((SUB, 128), F32)
    for s in range(SUB):
        o = o + a[SUB * s:SUB * s + SUB, :] * v[s:s + 1, :]
    return o


def _hgrn_intra_bwd_scores(qt, kf, b, v, do, mbd):
    rid = _row((SUB, 128))
    ps, das, kes, es = [], [], [], []
    for s in range(SUB):
        e = jnp.where(rid >= s, jnp.exp(b - b[s:s + 1, :]), 0.0)
        ke = kf[s:s + 1, :] * e
        es.append(e)
        kes.append(ke)
        ps.append(qt * ke)
        das.append(do * v[s:s + 1, :])
    a = _dot(jnp.concatenate(ps, axis=0).astype(BF16), mbd)
    da = _dot(jnp.concatenate(das, axis=0).astype(BF16), mbd)
    return a, da, kes, es


def _hgrn_intra_bwd_grads(scores, qt, do, rsum):
    a, da, kes, es = scores
    dqt = jnp.zeros((SUB, 128), F32)
    xs, ys = [], []
    for s in range(SUB):
        da_s = da[SUB * s:SUB * s + SUB, :]
        dqt = dqt + da_s * kes[s]
        xs.append(a[SUB * s:SUB * s + SUB, :] * do)
        ys.append(da_s * qt * es[s])
    dv = _dot(rsum, jnp.concatenate(xs, axis=0).astype(BF16))
    dkf = _dot(rsum, jnp.concatenate(ys, axis=0).astype(BF16))
    return dqt, dkf, dv


def _hgrn_norm_gate(o, z, onorm):
    ms = _gsum64(o * o) * (1.0 / 64)
    r = lax.rsqrt(ms + NORM_EPS)
    xh = o * r
    sz, dsz = _silu_and_grad(z)
    return xh, r, sz, dsz, xh * onorm


def _hgrn_fwd(proj, lb, onorm):
    T = proj.shape[0]
    n = T // CHUNK
    nsub = CHUNK // SUB

    def body(q_ref, f_ref, i_ref, z_ref, lb_ref, on_ref, y_ref, o_ref, s0_ref, st_ref):
        @pl.when(pl.program_id(0) == 0)
        def _():
            st_ref[...] = jnp.zeros_like(st_ref)

        lsub, _, bsub = _hgrn_consts()
        mbd = _block_diag64()
        bdmask = mbd > 0
        rid = _row((CHUNK, 128))
        subs = [slice(SUB * sub, SUB * sub + SUB) for sub in range(nsub)]
        work = []
        for pair in range(2):
            sl = slice(128 * pair, 128 * pair + 128)
            qt, _, _, _, _, g, kf = _hgrn_gates(q_ref[:, sl], f_ref[:, sl], lb_ref[:, sl])
            work.append(dict(sl=sl, qt=qt, kf=kf, v=i_ref[:, sl], b=_dot3_left(lsub, g), bl=_dot3_left(bsub, g)))
        for w in work:
            qt, kf, v, b, bl = w["qt"], w["kf"], w["v"], w["b"], w["bl"]
            w["qh"] = (qt * jnp.exp(b)).astype(BF16)
            kh = kf * jnp.exp(bl - b)
            w["dec"] = jnp.exp(bl)
            vtb = v.T.astype(BF16)
            w["scores"] = [_hgrn_intra_scores(qt[rs], kf[rs], b[rs], mbd) for rs in subs]
            w["adds"] = [_dot(vtb, jnp.where((rid >> SUB_SHIFT) == sub, kh, 0.0).astype(BF16)) for sub in range(nsub)]
        for pair, w in enumerate(work):
            w["st"] = st_ref[pair]
            s0_ref[0, pair] = w["st"]
            w["outs"] = []
        for sub, rs in enumerate(subs):
            for w in work:
                w["outs"].append(_dot_nt(w["qh"][rs], w["st"].astype(BF16)) + _hgrn_intra_out(w["scores"][sub], w["v"][rs]))
                w["st"] = jnp.where(bdmask, w["st"] * w["dec"][SUB * sub:SUB * sub + 1, :] + w["adds"][sub], 0.0)
        for pair, w in enumerate(work):
            sl = w["sl"]
            st_ref[pair] = w["st"]
            o = jnp.concatenate(w["outs"], axis=0)
            o_ref[:, sl] = o
            _, _, sz, _, on = _hgrn_norm_gate(o, z_ref[:, sl], on_ref[:, sl])
            y_ref[:, sl] = (on * sz).astype(BF16)

    col = lambda c: pl.BlockSpec((CHUNK, B_WIDTH), lambda i, c=c: (i, c // B_WIDTH))
    full = lambda a: pl.BlockSpec(a.shape, lambda i, n=a.ndim: (0,) * n)
    return _Part(body, (proj, proj, proj, proj, lb, onorm),
                 [col(COL_BQ), col(COL_BF), col(COL_BI), col(COL_BZ), full(lb), full(onorm)],
                 [pl.BlockSpec((CHUNK, B_WIDTH), lambda i: (i, 0)), pl.BlockSpec((CHUNK, B_WIDTH), lambda i: (i, 0)),
                  pl.BlockSpec((1, 2, 128, 128), lambda i: (i, 0, 0, 0))],
                 [SDS((T, B_WIDTH), BF16), SDS((T, B_WIDTH), F32), SDS((n, 2, 128, 128), F32)],
                 [pltpu.VMEM((2, 128, 128), F32)])


def _hgrn_bwd(proj, dy, o_saved, s0, lb, onorm):
    T = proj.shape[0]
    n = T // CHUNK
    nsub = CHUNK // SUB

    def body(q_ref, f_ref, i_ref, z_ref, dy_ref, o_ref, s0_ref, lb_ref, on_ref,
             db_ref, dlb_ref, don_ref, dst_ref, sts_ref):
        @pl.when(pl.program_id(0) == 0)
        def _():
            dst_ref[...] = jnp.zeros_like(dst_ref)
            dlb_ref[...] = jnp.zeros_like(dlb_ref)
            don_ref[...] = jnp.zeros_like(don_ref)

        lsub, usub, bsub = _hgrn_consts()
        mbd = _block_diag64()
        bdmask = mbd > 0
        rsum = jnp.where((_lane((SUB, SUB * SUB)) >> SUB_SHIFT) == _row((SUB, SUB * SUB)), 1.0, 0.0).astype(BF16)
        subs = [slice(SUB * sub, SUB * sub + SUB) for sub in range(nsub)]
        work = []
        for pair in range(2):
            sl = slice(128 * pair, 128 * pair + 128)
            lbp = lb_ref[:, sl]
            qt, dsq, sg, sgn, f, g, kf = _hgrn_gates(q_ref[:, sl], f_ref[:, sl], lbp)
            w = dict(sl=sl, lbp=lbp, qt=qt, dsq=dsq, sg=sg, sgn=sgn, f=f, kf=kf, v=i_ref[:, sl],
                     b=_dot3_left(lsub, g), bl=_dot3_left(bsub, g))
            onp = on_ref[:, sl]
            xh, r, sz, dsz, on = _hgrn_norm_gate(o_ref[:, sl], z_ref[:, sl], onp)
            dyv = dy_ref[:, sl]
            w["dz"] = dyv * on * dsz
            don = dyv * sz
            cn = jnp.sum(don * xh, axis=0, keepdims=True)
            don_ref[...] += cn + pltpu.roll(cn, 64, axis=1)
            dxo = don * onp
            w["do"] = r * (dxo - xh * (_gsum64(dxo * xh) * (1.0 / 64)))
            work.append(w)
        for w in work:
            qt, kf, v, b, bl, do = w["qt"], w["kf"], w["v"], w["b"], w["bl"], w["do"]
            w["eb"] = jnp.exp(b)
            w["ekb"] = jnp.exp(bl - b)
            w["qhb"] = (qt * w["eb"]).astype(BF16)
            w["khb"] = (kf * w["ekb"]).astype(BF16)
            w["dec"] = jnp.exp(bl)
            w["vb"] = v.astype(BF16)
            w["dob"] = do.astype(BF16)
            w["scores"] = [_hgrn_intra_bwd_scores(qt[rs], kf[rs], b[rs], v[rs], do[rs], mbd) for rs in subs]
            w["st_adds"] = [_dot_tn(w["vb"][rs], w["khb"][rs]) for rs in subs]
            w["gst_adds"] = [_dot_tn(w["dob"][rs], w["qhb"][rs]) for rs in subs]
        for pair, w in enumerate(work):
            w["st"] = s0_ref[0, pair]
        for sub in range(nsub):
            for pair, w in enumerate(work):
                sts_ref[pair, sub] = w["st"]
                w["st"] = jnp.where(bdmask, w["st"] * w["dec"][SUB * sub:SUB * sub + 1, :] + w["st_adds"][sub], 0.0)
        for pair, w in enumerate(work):
            w["gst"] = dst_ref[pair]
            w["dqt_p"], w["dkf_p"], w["dv_p"], w["dbl_p"] = ([None] * nsub for _ in range(4))
        for sub in reversed(range(nsub)):
            rs = subs[sub]
            for pair, w in enumerate(work):
                gst = w["gst"]
                st_in = sts_ref[pair, sub]
                gb = gst.astype(BF16)
                dqh = _dot(w["dob"][rs], st_in.astype(BF16))
                dkh = _dot(w["vb"][rs], gb)
                dv_inter = _dot_nt(w["khb"][rs], gb)
                ddec = jnp.sum(gst * st_in, axis=0, keepdims=True)
                dec_row = w["dec"][SUB * sub:SUB * sub + 1, :]
                w["gst"] = jnp.where(bdmask, gst * dec_row + w["gst_adds"][sub], 0.0)
                dqt_i, dkf_i, dv_i = _hgrn_intra_bwd_grads(w["scores"][sub], w["qt"][rs], w["do"][rs], rsum)
                dkf_inter = dkh * w["ekb"][rs]
                w["dqt_p"][sub] = dqh * w["eb"][rs] + dqt_i
                w["dkf_p"][sub] = dkf_inter + dkf_i
                w["dv_p"][sub] = dv_inter + dv_i
                row = jnp.sum(w["kf"][rs] * dkf_inter, axis=0, keepdims=True) + ddec * dec_row
                w["dbl_p"][sub] = jnp.broadcast_to(row, (SUB, 128))
        for pair, w in enumerate(work):
            sl, lbp, sg, sgn, f = w["sl"], w["lbp"], w["sg"], w["sgn"], w["f"]
            dst_ref[pair] = w["gst"]
            dqt = jnp.concatenate(w["dqt_p"], axis=0)
            dkf = jnp.concatenate(w["dkf_p"], axis=0)
            dv = jnp.concatenate(w["dv_p"], axis=0)
            dg = _dot3_left(usub, w["qt"] * dqt - w["kf"] * dkf) + jnp.concatenate(w["dbl_p"], axis=0)
            df = jnp.where(f > F_FLOOR, dg / f, 0.0)
            dlb_ref[:, sl] += jnp.sum(df * (1.0 - sg) - dkf * sgn, axis=0, keepdims=True)
            dfl = (1.0 - lbp) * sg * sgn * (df - dkf)
            dq = dqt * Q_SCALE * w["dsq"]
            db_ref[:, 0 * B_WIDTH + 128 * pair:0 * B_WIDTH + 128 * pair + 128] = dq.astype(BF16)
            db_ref[:, 1 * B_WIDTH + 128 * pair:1 * B_WIDTH + 128 * pair + 128] = dfl.astype(BF16)
            db_ref[:, 2 * B_WIDTH + 128 * pair:2 * B_WIDTH + 128 * pair + 128] = dv.astype(BF16)
            db_ref[:, 3 * B_WIDTH + 128 * pair:3 * B_WIDTH + 128 * pair + 128] = w["dz"].astype(BF16)

    rev = lambda c: pl.BlockSpec((CHUNK, B_WIDTH), lambda i, c=c: (n - 1 - i, c // B_WIDTH))
    full = lambda a: pl.BlockSpec(a.shape, lambda i, n_=a.ndim: (0,) * n_)
    acc = lambda s: pl.BlockSpec(s, lambda i, n_=len(s): (0,) * n_)
    return _Part(body, (proj, proj, proj, proj, dy, o_saved, s0, lb, onorm),
                 [rev(COL_BQ), rev(COL_BF), rev(COL_BI), rev(COL_BZ),
                  pl.BlockSpec((CHUNK, B_WIDTH), lambda i: (n - 1 - i, 1)),
                  pl.BlockSpec((CHUNK, B_WIDTH), lambda i: (n - 1 - i, 0)),
                  pl.BlockSpec((1, 2, 128, 128), lambda i: (n - 1 - i, 0, 0, 0)), full(lb), full(onorm)],
                 [pl.BlockSpec((CHUNK, 4 * B_WIDTH), lambda i: (n - 1 - i, 0)), acc((1, B_WIDTH)), acc((1, 128))],
                 [SDS((T, 4 * B_WIDTH), BF16), SDS((1, B_WIDTH), F32), SDS((1, 128), F32)],
                 [pltpu.VMEM((2, 128, 128), F32), pltpu.VMEM((2, nsub, 128, 128), F32)])


def _lb_fwd(hgrn_lb):
    assert hgrn_lb.shape[0] == 2

    def body(x_ref, o_ref):
        x0, x1 = x_ref[0:1, :], x_ref[1:2, :]
        m = jnp.maximum(x0, x1)
        e0, e1 = jnp.exp(x0 - m), jnp.exp(x1 - m)
        p0, p1 = e0 / (e0 + e1), e1 / (e0 + e1)
        o_ref[0:1, :] = jnp.clip(p0 - p0, 0.0, 1.0 - 1e-6)
        o_ref[1:2, :] = jnp.clip((p0 + p1) - p0, 0.0, 1.0 - 1e-6)

    return pl.pallas_call(body, name="lb_fwd", out_shape=SDS(hgrn_lb.shape, F32))(hgrn_lb)


def _lb_bwd(hgrn_lb, dlb):
    def body(x_ref, d_ref, o_ref):
        x0, x1 = x_ref[0:1, :], x_ref[1:2, :]
        m = jnp.maximum(x0, x1)
        e0, e1 = jnp.exp(x0 - m), jnp.exp(x1 - m)
        p0, p1 = e0 / (e0 + e1), e1 / (e0 + e1)
        val = (p0 + p1) - p0
        dp1 = jnp.where((val > 0.0) & (val < 1.0 - 1e-6), d_ref[1:2, :], 0.0)
        inner = p1 * dp1
        o_ref[0:1, :] = p0 * (0.0 - inner)
        o_ref[1:2, :] = p1 * (dp1 - inner)

    return pl.pallas_call(body, name="lb_bwd", out_shape=SDS(hgrn_lb.shape, F32))(hgrn_lb, dlb)


def _fox_prep(proj, bf):
    T = proj.shape[0]
    n = T // CHUNK

    def body(q0_ref, q1_ref, k0_ref, k1_ref, v0_ref, v1_ref, fl_ref, bf_ref, qo_ref, ko_ref, vt_ref, carry_ref):
        for p, v_ref in enumerate((v0_ref, v0_ref, v1_ref, v1_ref)):
            vt_ref[p, 0] = v_ref[:, 128 * (p % 2):128 * (p % 2) + 128].T.astype(BF16)

        @pl.when(pl.program_id(0) == 0)
        def _():
            carry_ref[...] = jnp.zeros_like(carry_ref)

        ltri = jnp.where(_lane((CHUNK, CHUNK)) <= _row((CHUNK, CHUNK)), 1.0, 0.0).astype(BF16)
        lf = jax.nn.log_sigmoid(fl_ref[...] + bf_ref[...])
        c = _dot3_left(ltri, lf) + carry_ref[...]
        carry_ref[...] = c[CHUNK - 1:CHUNK, :]
        lane = _lane((CHUNK, 128))
        feat = lane < 64
        ones_q = (lane >= 67) & (lane <= 69)
        ones_k = (lane >= 64) & (lane <= 66)
        qrefs, krefs = (q0_ref, q1_ref), (k0_ref, k1_ref)
        for h in range(C_HEADS):
            blk = slice(128 * ((h // 2) % 2), 128 * ((h // 2) % 2) + 128)
            qp, kp = qrefs[h // 4][:, blk], krefs[h // 4][:, blk]
            if h % 2:
                qp, kp = pltpu.roll(qp, 64, axis=1), pltpu.roll(kp, 64, axis=1)
            ch = jnp.broadcast_to(c[:, h:h + 1], (CHUNK, 128))
            hi = ch.astype(BF16).astype(F32)
            r1 = ch - hi
            mid = r1.astype(BF16).astype(F32)
            lo = r1 - mid
            aq = jnp.where(lane == 64, hi, jnp.where(lane == 65, mid, jnp.where(lane == 66, lo,
                           jnp.where(ones_q, 1.0, 0.0))))
            ak = jnp.where(lane == 67, -hi, jnp.where(lane == 68, -mid, jnp.where(lane == 69, -lo,
                           jnp.where(ones_k, 1.0, 0.0))))
            qo_ref[:, 128 * h:128 * h + 128] = jnp.where(feat, qp * Q_SCALE, aq).astype(BF16)
            ko_ref[:, 128 * h:128 * h + 128] = jnp.where(feat, kp, ak).astype(BF16)

    w = 256
    col = lambda c: pl.BlockSpec((CHUNK, w), lambda i, c=c: (i, c // w))
    return _Part(body, (proj, proj, proj, proj, proj, proj, proj, bf),
                 [col(COL_CQ), col(COL_CQ + w), col(COL_CK), col(COL_CK + w), col(COL_CV), col(COL_CV + w),
                  pl.BlockSpec((CHUNK, 128), lambda i: (i, COL_CF // 128)), pl.BlockSpec((1, 128), lambda i: (0, 0))],
                 [pl.BlockSpec((CHUNK, C_HEADS * 128), lambda i: (i, 0))] * 2
                 + [pl.BlockSpec((C_HEADS // 2, 1, 128, CHUNK), lambda i: (0, i, 0, 0))],
                 [SDS((T, C_HEADS * 128), BF16)] * 2 + [SDS((C_HEADS // 2, n, 128, CHUNK), BF16)],
                 [pltpu.VMEM((1, 128), F32)])


FOX_TILE = 512
FOX_KEYS = 512
FOX_STRIP = 16


def _fox_mask(tk, tq, k0, q0):
    return (_row((tk, tq)) + (k0 - q0)) <= _lane((tk, tq))


def _ride_refs(ride, rest, n_out, n_scratch):
    n = ride.n if ride else 0
    srcs, rest = rest[:n], rest[n:]
    outs, rest = rest[:n_out], rest[n_out:]
    dsts, rest = rest[:n], rest[n:]
    return srcs, outs, dsts, rest[:n_scratch], rest[n_scratch:]


def _ride_start(ride, grid, srcs, dsts, sems):
    if ride:
        first = functools.reduce(lambda a, b: a & b, [pl.program_id(d) == 0 for d in range(len(grid))])
        pl.when(first)(lambda: ride.start(srcs, dsts, sems))


def _ride_wait(ride, grid, srcs, dsts, sems):
    if ride:
        last = functools.reduce(lambda a, b: a & b, [pl.program_id(d) == n - 1 for d, n in enumerate(grid)])
        pl.when(last)(lambda: ride.wait(srcs, dsts, sems))


def _fox_fwd(qt, kt, vt, proj, tag, ride=None):
    T = proj.shape[0]
    tq, tk = _tile(T, FOX_TILE), _tile(T, FOX_KEYS)
    nq, nsub = T // tq, tk // CHUNK
    npair = C_HEADS // 2

    def body(q_ref, k_ref, vt_ref, z_ref, *rest):
        ride_srcs, (o_ref, lse_ref, y_ref), ride_dsts, (acc_ref, st_ref, pt_ref), ride_sems = _ride_refs(ride, rest, 3, 3)
        i = pl.program_id(1)
        _ride_start(ride, (npair, nq), ride_srcs, ride_dsts, ride_sems)

        qs = (q_ref[:, 0:128], q_ref[:, 128:256])
        acc_ref[...] = jnp.zeros_like(acc_ref)
        pt_ref[...] = jnp.zeros_like(pt_ref)
        nfull = (i * tq) // tk

        def scores(j):
            kb = k_ref[pl.ds(pl.multiple_of(j * tk, tk), tk), :]
            return tuple(_dot_nt(kb[:, 128 * h:128 * h + 128], qs[h]) for h in range(2))

        def weigh(j, h):
            rows = slice(64 * h, 64 * h + 64)
            vth = jnp.concatenate([vt_ref[0, nsub * j + c, rows, :] for c in range(nsub)], axis=1)
            return _dot(vth, pt_ref[h])

        def block(j, carry, diagonal):
            nxt = () if diagonal else scores(j + 1)
            pvs = [weigh(jnp.maximum(j - 1, 0), h) for h in range(2)]
            new = []
            for h in range(2):
                m, l, alpha_prev = carry[3 * h:3 * h + 3]
                st = st_ref[h]
                if diagonal:
                    st = jnp.where(_fox_mask(tk, tq, j * tk, i * tq), st, -jnp.inf)
                m_new = jnp.maximum(m, _colreduce(st, jnp.maximum))
                pt = jnp.exp(st - m_new)
                alpha = jnp.exp(m - m_new)
                rows = slice(64 * h, 64 * h + 64)
                acc_ref[rows, :] = alpha_prev * acc_ref[rows, :] + pvs[h]
                pt_ref[h] = pt.astype(BF16)
                new += [m_new, alpha * l + _colreduce(pt, jnp.add), alpha]
            for h, st in enumerate(nxt):
                st_ref[h] = st
            return tuple(new)

        for h, st in enumerate(scores(0)):
            st_ref[h] = st
        init = (jnp.full((1, tq), -jnp.inf, F32), jnp.zeros((1, tq), F32), jnp.ones((1, tq), F32)) * 2
        carry = lax.fori_loop(0, nfull, lambda j, c: block(j, c, False), init)
        m0, l0, a0, m1, l1, a1 = block(nfull, carry, True)
        for h, alpha in enumerate((a0, a1)):
            rows = slice(64 * h, 64 * h + 64)
            acc_ref[rows, :] = alpha * acc_ref[rows, :] + weigh(nfull, h)
        inv = jnp.where(_row((128, tq)) < 64, 1.0 / l0, 1.0 / l1)
        o = (acc_ref[...] * inv).T
        o_ref[...] = o
        r8 = _row((8, tq))
        lse_ref[0, 0] = jnp.where(r8 == 0, m0 + jnp.log(l0), jnp.where(r8 == 1, m1 + jnp.log(l1), 0.0))
        sz, _ = _silu_and_grad(z_ref[...])
        y_ref[...] = (o * sz).astype(BF16)
        _ride_wait(ride, (npair, nq), ride_srcs, ride_dsts, ride_sems)

    blk = pl.BlockSpec((tq, 128), lambda p, i: (i, p))
    extra = ride or _ChipExchange("gather", ())
    return pl.pallas_call(
        body, name=f"fox_fwd_{tag}", grid=(npair, nq),
        in_specs=[pl.BlockSpec((tq, 256), lambda p, i: (i, p)), pl.BlockSpec((T, 256), lambda p, i: (0, p)),
                  pl.BlockSpec((1, T // CHUNK, 128, CHUNK), lambda p, i: (p, 0, 0, 0)),
                  pl.BlockSpec((tq, 128), lambda p, i: (i, COL_CZ // 128 + p))] + extra.in_specs,
        out_specs=[blk, pl.BlockSpec((1, 1, 8, tq), lambda p, i: (p, i, 0, 0)), blk] + extra.out_specs,
        out_shape=[SDS((T, C_WIDTH), F32), SDS((npair, nq, 8, tq), F32), SDS((T, C_WIDTH), BF16)] + extra.out_shape,
        scratch_shapes=[pltpu.VMEM((128, tq), F32), pltpu.VMEM((2, tk, tq), F32), pltpu.VMEM((2, tk, tq), BF16)]
        + (extra.scratch if ride else []),
        compiler_params=pltpu.CompilerParams(dimension_semantics=("arbitrary", "arbitrary"), vmem_limit_bytes=VMEM_LIMIT,
                                             has_side_effects=bool(ride)),
    )(qt, kt, vt, proj, *extra.sources)


def _fox_bwd_prep(proj, dy, o, qt, tag):
    T = proj.shape[0]
    tq = _tile(T, FOX_TILE)
    nq = T // tq

    def body(z0_ref, z1_ref, dy_ref, o_ref, q_ref, do_ref, dl_ref, dz_ref, dot_ref, qt_ref):
        sel = jnp.where((_lane((16, 128)) >> 6) == _row((16, 128)), 1.0, 0.0).astype(BF16)
        for p, z_ref in enumerate((z0_ref, z0_ref, z1_ref, z1_ref)):
            sl = slice(128 * p, 128 * p + 128)
            sz, dsz = _silu_and_grad(z_ref[:, 128 * (p % 2):128 * (p % 2) + 128])
            dyv, ov = dy_ref[:, sl], o_ref[:, sl]
            do = dyv * sz
            do_ref[:, sl] = do.astype(BF16)
            dot_ref[p, 0] = do.T.astype(BF16)
            dz_ref[:, sl] = (dyv * ov * dsz).astype(BF16)
            hi, mid, lo = _split3(do * ov)
            dl_ref[p, 0] = (_dot_nt(sel, hi) + _dot_nt(sel, mid) + _dot_nt(sel, lo))[0:8, :]
        for h in range(C_HEADS):
            qt_ref[h, 0] = q_ref[:, 128 * h:128 * h + 128].astype(F32).T.astype(BF16)

    w = 256
    blk = pl.BlockSpec((tq, C_WIDTH), lambda i: (i, 0))
    return pl.pallas_call(
        body, name=f"fox_bwd_prep_{tag}", grid=(nq,),
        in_specs=[pl.BlockSpec((tq, w), lambda i: (i, COL_CZ // w)), pl.BlockSpec((tq, w), lambda i: (i, COL_CZ // w + 1)),
                  pl.BlockSpec((tq, C_WIDTH), lambda i: (i, (A_WIDTH + B_WIDTH) // C_WIDTH)), blk,
                  pl.BlockSpec((tq, C_HEADS * 128), lambda i: (i, 0))],
        out_specs=[blk, pl.BlockSpec((C_HEADS // 2, 1, 8, tq), lambda i: (0, i, 0, 0)), blk,
                   pl.BlockSpec((C_HEADS // 2, 1, 128, tq), lambda i: (0, i, 0, 0)),
                   pl.BlockSpec((C_HEADS, 1, 128, tq), lambda i: (0, i, 0, 0))],
        out_shape=[SDS((T, C_WIDTH), BF16), SDS((C_HEADS // 2, nq, 8, tq), F32), SDS((T, C_WIDTH), BF16),
                   SDS((C_HEADS // 2, nq, 128, tq), BF16), SDS((C_HEADS, nq, 128, tq), BF16)],
        compiler_params=_params("parallel"),
    )(proj, proj, dy, o, qt)


def _fox_bwd(qt, kt, proj, do, lse, delta, dot, qtr, tag, ride=None):
    T = proj.shape[0]
    tq, tk = _tile(T, FOX_TILE), _tile(T, FOX_KEYS)
    nq, nk = T // tq, T // tk
    assert tq == tk
    npair = C_HEADS // 2

    def body(q_ref, k_ref, v_ref, do_ref, lse_ref, dl_ref, dot_ref, qtr_ref, *rest):
        ride_srcs, (dq_ref, dk_ref, dv_ref), ride_dsts, scratch, ride_sems = _ride_refs(ride, rest, 3, 4)
        dvt_ref, dkt_ref, pt_ref, ds_ref = scratch
        j = pl.program_id(1)
        first = (j * tk) // tq
        _ride_start(ride, (npair, nk), ride_srcs, ride_dsts, ride_sems)

        @pl.when(j == 0)
        def _():
            dq_ref[...] = jnp.zeros_like(dq_ref)

        dkt_ref[...] = jnp.zeros_like(dkt_ref)
        dvt_ref[...] = jnp.zeros_like(dvt_ref)
        ks = (k_ref[:, 0:128], k_ref[:, 128:256])
        kts = tuple(k.astype(F32).T.astype(BF16) for k in ks)
        vb = v_ref[...].astype(BF16)
        lo = _lane((tq, 128)) < 64

        def operands(i):
            q0 = pl.multiple_of(i * tq, tq)
            qb = q_ref[pl.ds(q0, tq), :]
            dob = do_ref[pl.ds(q0, tq), :]
            qhs = (qb[:, 0:128], qb[:, 128:256])
            dohs = (jnp.where(lo, dob, jnp.zeros_like(dob)), jnp.where(lo, jnp.zeros_like(dob), dob))
            return qhs, dohs

        def scores(i):
            qhs, dohs = operands(i)
            return tuple((_dot_nt(ks[h], qhs[h]), _dot_nt(vb, dohs[h])) for h in range(2))

        def grads(i, slot):
            for h in range(2):
                rows = slice(64 * h, 64 * h + 64)
                dvt_ref[rows, :] += _dot_nt(dot_ref[0, i, rows, :], pt_ref[slot, h])
                dkt_ref[h] += _dot_nt(qtr_ref[h, i], ds_ref[slot, h])
                dq_ref[h, i] += _dot(kts[h], ds_ref[slot, h])

        def block(i, slot, diagonal, opening):
            sc = scores(i)
            if not opening:
                grads(i - 1, 1 - slot)
            lsev = lse_ref[0, i]
            dlv = dl_ref[0, i]
            for h in range(2):
                lseh = jnp.broadcast_to(lsev[h:h + 1, :], (FOX_STRIP, tq))
                dlh = jnp.broadcast_to(dlv[h:h + 1, :], (FOX_STRIP, tq))
                for r in range(0, tk, FOX_STRIP):
                    rows = slice(r, r + FOX_STRIP)
                    pt = jnp.exp(sc[h][0][rows, :] - lseh)
                    if diagonal:
                        pt = jnp.where(_fox_mask(FOX_STRIP, tq, r, 0), pt, 0.0)
                    ds_ref[slot, h, rows, :] = (pt * (sc[h][1][rows, :] - dlh)).astype(BF16)
                    pt_ref[slot, h, rows, :] = pt.astype(BF16)

        block(first, 0, True, True)
        rest = nq - 1 - first

        def two_steps(t, carry):
            block(first + 1 + 2 * t, 1, False, False)
            block(first + 2 + 2 * t, 0, False, False)
            return carry

        lax.fori_loop(0, rest // 2, two_steps, 0)
        pl.when(rest % 2 == 1)(lambda: block(nq - 1, 1, False, False))
        grads(nq - 1, rest % 2)
        dv_ref[...] = dvt_ref[...].T.astype(BF16)
        for h in range(2):
            dk_ref[:, 128 * h:128 * h + 128] = dkt_ref[h].T
        _ride_wait(ride, (npair, nk), ride_srcs, ride_dsts, ride_sems)

    full = lambda w: pl.BlockSpec((T, w), lambda p, j: (0, p))
    stat = pl.BlockSpec((1, nq, 8, tq), lambda p, j: (p, 0, 0, 0))
    extra = ride or _ChipExchange("gather", ())
    return pl.pallas_call(
        body, name=f"fox_bwd_{tag}", grid=(npair, nk),
        in_specs=[full(256), pl.BlockSpec((tk, 256), lambda p, j: (j, p)),
                  pl.BlockSpec((tk, 128), lambda p, j: (j, COL_CV // 128 + p)), full(128), stat, stat,
                  pl.BlockSpec((1, nq, 128, tq), lambda p, j: (p, 0, 0, 0)),
                  pl.BlockSpec((2, nq, 128, tq), lambda p, j: (p, 0, 0, 0))] + extra.in_specs,
        out_specs=[pl.BlockSpec((2, nq, 128, tq), lambda p, j: (p, 0, 0, 0)), pl.BlockSpec((tk, 256), lambda p, j: (j, p)),
                   pl.BlockSpec((tk, 128), lambda p, j: (j, p))] + extra.out_specs,
        out_shape=[SDS((C_HEADS, nq, 128, tq), F32), SDS((T, C_HEADS * 128), F32), SDS((T, C_WIDTH), BF16)]
        + extra.out_shape,
        scratch_shapes=[pltpu.VMEM((128, tk), F32), pltpu.VMEM((2, 128, tk), F32),
                        pltpu.VMEM((2, 2, tk, tq), BF16), pltpu.VMEM((2, 2, tk, tq), BF16)]
        + (extra.scratch if ride else []),
        compiler_params=pltpu.CompilerParams(dimension_semantics=("arbitrary", "arbitrary"), vmem_limit_bytes=VMEM_LIMIT,
                                             has_side_effects=bool(ride)),
    )(qt, kt, proj, do, lse, delta, dot, qtr, *extra.sources)


def _fox_bwd_post(dqt, dkt, proj, bf, tag):
    T = proj.shape[0]
    tq = _tile(T, FOX_TILE)
    n = T // tq

    def body(dq_ref, dk_ref, fl_ref, bf_ref, oq_ref, ok_ref, ofl_ref, dbf_ref, carry_ref):
        @pl.when(pl.program_id(0) == 0)
        def _():
            carry_ref[...] = jnp.zeros_like(carry_ref)
            dbf_ref[...] = jnp.zeros_like(dbf_ref)

        lane = _lane((tq, 128))
        lo = lane < 64
        dqs = [dq_ref[h, 0].T for h in range(C_HEADS)]
        dc = jnp.zeros((tq, 128), F32)
        for h in range(C_HEADS):
            dc = dc + jnp.where(lane == h, dqs[h][:, 64:65] - dk_ref[:, 128 * h + 67:128 * h + 68], 0.0)
        utri = jnp.where(_lane((tq, tq)) >= _row((tq, tq)), 1.0, 0.0).astype(BF16)
        dlf = _dot3_left(utri, dc) + carry_ref[...]
        carry_ref[...] = dlf[0:1, :]
        dfl = jnp.where(lane < C_HEADS, dlf * _sigmoid(-(fl_ref[...] + bf_ref[...])), 0.0)
        ofl_ref[...] = dfl.astype(BF16)
        dbf_ref[...] += jnp.sum(dfl, axis=0, keepdims=True)
        for p in range(C_HEADS // 2):
            a, b = 128 * (2 * p), 128 * (2 * p + 1)
            oq_ref[:, 128 * p:128 * p + 128] = (
                jnp.where(lo, dqs[2 * p], pltpu.roll(dqs[2 * p + 1], 64, axis=1)) * Q_SCALE).astype(BF16)
            ok_ref[:, 128 * p:128 * p + 128] = jnp.where(
                lo, dk_ref[:, a:a + 128], pltpu.roll(dk_ref[:, b:b + 128], 64, axis=1)).astype(BF16)

    rev = lambda w: pl.BlockSpec((tq, w), lambda i: (n - 1 - i, 0))
    return pl.pallas_call(
        body, name=f"fox_bwd_post_{tag}", grid=(n,),
        in_specs=[pl.BlockSpec((C_HEADS, 1, 128, tq), lambda i: (0, n - 1 - i, 0, 0)), rev(C_HEADS * 128),
                  pl.BlockSpec((tq, 128), lambda i: (n - 1 - i, COL_CF // 128)), pl.BlockSpec((1, 128), lambda i: (0, 0))],
        out_specs=[rev(C_WIDTH), rev(C_WIDTH), rev(128), pl.BlockSpec((1, 128), lambda i: (0, 0))],
        out_shape=[SDS((T, C_WIDTH), BF16), SDS((T, C_WIDTH), BF16), SDS((T, 128), BF16), SDS((1, 128), F32)],
        scratch_shapes=[pltpu.VMEM((1, 128), F32)], compiler_params=_params("arbitrary"),
    )(dqt, dkt, proj, bf)


def _adamw_math(w, g, m, v):
    m = ADAM_B1 * m + (1.0 - ADAM_B1) * g
    v = ADAM_B2 * v + (1.0 - ADAM_B2) * (g * g)
    delta = -ADAM_LR * ((m / ADAM_C1) / (jnp.sqrt(v / ADAM_C2) + ADAM_EPS) + ADAM_WD * w)
    return delta, m, v


def _adamw_pair(w, m, v, ga, gb, name):
    n0 = w.shape[0]
    most = max(1, ADAMW_BLOCK_BYTES // (4 * math.prod(w.shape[1:])))
    t0 = max(t for t in range(1, min(n0, most) + 1) if n0 % t == 0)

    def body(w_ref, m_ref, v_ref, ga_ref, gb_ref, g_ref, d_ref, nm_ref, nv_ref):
        g = ga_ref[...] + gb_ref[...]
        g_ref[...] = g
        d_ref[...], nm_ref[...], nv_ref[...] = _adamw_math(w_ref[...], g, m_ref[...], v_ref[...])

    blk = pl.BlockSpec((t0,) + w.shape[1:], lambda i: (i, 0, 0))
    return pl.pallas_call(
        body, name=name, grid=(n0 // t0,), in_specs=[blk] * 5, out_specs=[blk] * 4,
        out_shape=[SDS(w.shape, F32)] * 4, compiler_params=_params("parallel"),
    )(w, m, v, ga, gb)


def _adamw_small(ws, ms, vs, gall):
    offs = _small_offsets()
    n = len(ws)

    def body(*refs):
        w_refs, m_refs, v_refs, g_ref = refs[:n], refs[n:2 * n], refs[2 * n:3 * n], refs[3 * n]
        outs = refs[3 * n + 1:]

        def total(off, rows):
            g = g_ref[0, off:off + rows, :]
            for dev in range(1, N_DEV):
                g = g + g_ref[dev, off:off + rows, :]
            return g

        for k in range(n):
            g = total(offs[k], ws[k].shape[0])
            go_ref, d_ref, nm_ref, nv_ref = outs[4 * k:4 * k + 4]
            go_ref[...] = g
            d_ref[...], nm_ref[...], nv_ref[...] = _adamw_math(w_refs[k][...], g, m_refs[k][...], v_refs[k][...])
        outs[4 * n][...] = total(offs[n], 1)

    shapes = [SDS(w.shape, F32) for w in ws for _ in range(4)] + [SDS((1, 128), F32)]
    res = pl.pallas_call(body, name="adamw_small", out_shape=shapes,
                         compiler_params=pltpu.CompilerParams(vmem_limit_bytes=VMEM_LIMIT))(*ws, *ms, *vs, gall)
    return [res[4 * k:4 * k + 4] for k in range(n)], res[4 * n]


def _pack_grads(dlng, dlnb, dwm, dbst, dlb, donorm, dbf, dfinal, loss_part):
    offs = _small_offsets()
    base = offs[1]
    L = len(dwm)
    assert L == 2

    def body(*refs):
        lng, lnb, wm, bst, on, bf = (refs[L * a:L * a + L] for a in range(6))
        lb_ref, fin_ref, loss_ref, o_ref = refs[6 * L:]
        o_ref[...] = jnp.zeros_like(o_ref)
        lane = _lane((1, 128))
        for l in range(L):
            for j in range(2):
                o_ref[offs[1] - base + 2 * l + j:offs[1] - base + 2 * l + j + 1, :] = lng[l][:, 128 * j:128 * j + 128]
                o_ref[offs[2] - base + 2 * l + j:offs[2] - base + 2 * l + j + 1, :] = lnb[l][:, 128 * j:128 * j + 128]
                o_ref[offs[5] - base + 2 * l + j:offs[5] - base + 2 * l + j + 1, :] = lb_ref[l:l + 1, 128 * j:128 * j + 128]
            for g in range(A_GROUPS):
                row = offs[3] - base + (A_GROUPS * l + g) * CHUNK
                o_ref[row:row + CHUNK, :] = wm[l][g]
            o_ref[offs[4] - base + A_GROUPS * l:offs[4] - base + A_GROUPS * (l + 1), :] = bst[l][...].T[0:A_GROUPS, :]
        o_ref[offs[6] - base:offs[6] - base + 1, :] = jnp.where(lane < 64, on[0][...], pltpu.roll(on[1][...], 64, axis=1))
        o_ref[offs[7] - base:offs[7] - base + 1, :] = jnp.where(
            lane < C_HEADS, bf[0][...], jnp.where(lane < 2 * C_HEADS, pltpu.roll(bf[1][...], C_HEADS, axis=1), 0.0))
        for j in range(D_MODEL // 128):
            o_ref[offs[8] - base + j:offs[8] - base + j + 1, :] = fin_ref[:, 128 * j:128 * j + 128]
        o_ref[offs[9] - base:offs[9] - base + 1, :] = loss_ref[...]

    rows = offs[9] + 8 - base
    return pl.pallas_call(body, name="pack_grads", out_shape=SDS((rows, 128), F32))(
        *dlng, *dlnb, *dwm, *dbst, *donorm, *dbf, dlb, dfinal, loss_part)


def _sum_chips(layers, name, layer_major):
    _, R, C = layers[0].shape
    L = len(layers)
    tc = _tile(C, 256)

    def body(*refs):
        o_ref = refs[-1]
        for l, p_ref in enumerate(refs[:-1]):
            p = [p_ref[k].astype(F32) for k in range(N_CHIPS)]
            s = ((p[0] + p[1]) + p[2]) + p[3]
            if layer_major:
                o_ref[l] = s
            else:
                o_ref[:, l, :] = s

    out = (L, R, C) if layer_major else (R, L, C)
    out_blk = (L, R, tc) if layer_major else (R, L, tc)
    return pl.pallas_call(
        body, name=name, grid=(C // tc,),
        in_specs=[pl.BlockSpec((N_CHIPS, R, tc), lambda i: (0, 0, i))] * L,
        out_specs=pl.BlockSpec(out_blk, lambda i: (0, 0, i)), out_shape=SDS(out, F32),
        compiler_params=_params("parallel"),
    )(*layers)


ANY = pl.BlockSpec(memory_space=pl.ANY)


def _mesh_pos():
    return lax.axis_index("x"), lax.axis_index("y"), lax.axis_index("c")


def _other_chips(x, y):
    return [(1 - x, y), (x, 1 - y), (1 - x, 1 - y)]


class _ChipExchange:
    def __init__(self, mode, sources):
        assert mode in ("gather", "scatter")
        self.mode, self.sources = mode, tuple(sources)
        self.n = len(self.sources)
        self.in_specs = [ANY] * self.n
        self.out_specs = [ANY] * self.n
        self.out_shape = [SDS(((N_CHIPS,) + s.shape) if mode == "gather" else s.shape, s.dtype) for s in self.sources]
        self.scratch = [pltpu.SemaphoreType.DMA((3 * self.n,)), pltpu.SemaphoreType.DMA((3 * self.n,)),
                        pltpu.SemaphoreType.DMA((self.n,))]

    def _copies(self, srcs, dsts, send_sems, recv_sems, local_sems):
        x, y, c = _mesh_pos()
        me = 2 * x + y
        view = (lambda r, chip: r) if self.mode == "gather" else (lambda r, chip: r.at[chip])
        local = [pltpu.make_async_copy(view(s, me), d.at[me], local_sems.at[a]) for a, (s, d) in enumerate(zip(srcs, dsts))]
        sends, recvs = [], []
        for j, (px, py) in enumerate(_other_chips(x, y)):
            peer = 2 * px + py
            for a, (s, d) in enumerate(zip(srcs, dsts)):
                sems = dict(send_sem=send_sems.at[self.n * j + a], recv_sem=recv_sems.at[self.n * j + a],
                            device_id=(px, py, c), device_id_type=MESH_ID)
                sends.append(pltpu.make_async_remote_copy(src_ref=view(s, peer), dst_ref=d.at[me], **sems))
                recvs.append(pltpu.make_async_remote_copy(src_ref=view(s, me), dst_ref=d.at[peer], **sems))
        return local, sends, recvs

    def start(self, srcs, dsts, sems):
        local, sends, _ = self._copies(srcs, dsts, *sems)
        for cp in local + sends:
            cp.start()

    def wait(self, srcs, dsts, sems):
        local, sends, recvs = self._copies(srcs, dsts, *sems)
        for cp in recvs:
            cp.wait_recv()
        for cp in sends:
            cp.wait_send()
        for cp in local:
            cp.wait()


def _gather_halves(w, tag):
    R, C = w.shape
    H = C // 2

    def body(w_ref, g_ref, send_sems, recv_sems, pass_send, pass_recv, local_sem):
        x, y, c = _mesh_pos()
        me = 2 * x + y
        mine, theirs = pl.ds(pl.multiple_of(c * H, H), H), pl.ds(pl.multiple_of((1 - c) * H, H), H)
        own = pltpu.make_async_copy(w_ref, g_ref.at[me], local_sem)
        own.start()

        def fetch(j, px, py, src, dst):
            return pltpu.make_async_remote_copy(src_ref=src, dst_ref=dst, send_sem=send_sems.at[j], recv_sem=recv_sems.at[j],
                                                device_id=(px, py, c), device_id_type=MESH_ID)

        def hand(j, cols, peer):
            return pltpu.make_async_remote_copy(src_ref=g_ref.at[peer, :, cols], dst_ref=g_ref.at[peer, :, cols],
                                                send_sem=pass_send.at[j], recv_sem=pass_recv.at[j],
                                                device_id=(x, y, 1 - c), device_id_type=MESH_ID)

        chips = _other_chips(x, y)
        sends = [fetch(j, px, py, w_ref.at[:, mine], g_ref.at[me, :, mine]) for j, (px, py) in enumerate(chips)]
        for cp in sends:
            cp.start()
        passed = []
        for j, (px, py) in enumerate(chips):
            peer = 2 * px + py
            fetch(j, px, py, w_ref.at[:, mine], g_ref.at[peer, :, mine]).wait_recv()
            passed.append(hand(j, mine, peer))
            passed[-1].start()
        for j, (px, py) in enumerate(chips):
            hand(j, theirs, 2 * px + py).wait_recv()
        for cp in sends + passed:
            cp.wait_send()
        own.wait()

    return pl.pallas_call(
        body, name=f"gather_halves_{tag}", in_specs=[ANY], out_specs=ANY, out_shape=SDS((N_CHIPS, R, C), w.dtype),
        scratch_shapes=[pltpu.SemaphoreType.DMA((3,)), pltpu.SemaphoreType.DMA((3,)), pltpu.SemaphoreType.DMA((3,)),
                        pltpu.SemaphoreType.DMA((3,)), pltpu.SemaphoreType.DMA],
        compiler_params=pltpu.CompilerParams(has_side_effects=True),
    )(w)


class _DeviceGather:
    def __init__(self, source):
        self.sources, self.n = (source,), 1
        self.in_specs, self.out_specs = [ANY], [ANY]
        self.out_shape = [SDS((N_DEV,) + source.shape, source.dtype)]
        self.scratch = [pltpu.SemaphoreType.DMA((N_DEV - 1,)), pltpu.SemaphoreType.DMA((N_DEV - 1,)),
                        pltpu.SemaphoreType.DMA((1,))]

    def _copies(self, srcs, dsts, send_sems, recv_sems, local_sems):
        (src,), (dst,) = srcs, dsts
        x, y, c = _mesh_pos()
        me = 4 * x + 2 * y + c
        local = [pltpu.make_async_copy(src, dst.at[me], local_sems.at[0])]
        sends, recvs = [], []
        for k in range(1, N_DEV):
            px, py, pc = (1 - x) if k & 4 else x, (1 - y) if k & 2 else y, (1 - c) if k & 1 else c
            sems = dict(send_sem=send_sems.at[k - 1], recv_sem=recv_sems.at[k - 1], device_id=(px, py, pc),
                        device_id_type=MESH_ID)
            sends.append(pltpu.make_async_remote_copy(src_ref=src, dst_ref=dst.at[me], **sems))
            recvs.append(pltpu.make_async_remote_copy(src_ref=src, dst_ref=dst.at[4 * px + 2 * py + pc], **sems))
        return local, sends, recvs

    start = _ChipExchange.start
    wait = _ChipExchange.wait


def _gather_devices(a, name):
    ex = _DeviceGather(a)

    def body(a_ref, g_ref, *sems):
        ex.start((a_ref,), (g_ref,), sems)
        ex.wait((a_ref,), (g_ref,), sems)

    return pl.pallas_call(
        body, name=name, in_specs=ex.in_specs, out_specs=ex.out_specs[0], out_shape=ex.out_shape[0],
        scratch_shapes=ex.scratch, compiler_params=pltpu.CompilerParams(has_side_effects=True),
    )(a)


def _swap_cores(pin, pout):
    def body(pin_ref, pout_ref, oin_ref, oout_ref, send_sems, recv_sems):
        x, y, c = _mesh_pos()
        cps = [pltpu.make_async_remote_copy(src_ref=src, dst_ref=dst, send_sem=send_sems.at[a], recv_sem=recv_sems.at[a],
                                            device_id=(x, y, 1 - c), device_id_type=MESH_ID)
               for a, (src, dst) in enumerate(((pin_ref, oin_ref), (pout_ref, oout_ref)))]
        for cp in cps:
            cp.start()
        for cp in cps:
            cp.wait()

    return pl.pallas_call(
        body, name="swap_cores", in_specs=[ANY, ANY], out_specs=[ANY, ANY],
        out_shape=[SDS(pin.shape, F32), SDS(pout.shape, F32)],
        scratch_shapes=[pltpu.SemaphoreType.DMA((2,)), pltpu.SemaphoreType.DMA((2,))],
        compiler_params=pltpu.CompilerParams(has_side_effects=True),
    )(pin, pout)


PACK_TILE = 8 * 128


def _pack_rows(size):
    return (size + PACK_TILE - 1) // PACK_TILE * 8


def _small_offsets():
    offs = [0]
    for _, shape in SMALL_PARAMS:
        offs.append(offs[-1] + _pack_rows(math.prod(shape)))
    return offs


def _rows_view(a):
    flat = a.reshape(-1)
    return jnp.pad(flat, (0, (-flat.size) % 128)).reshape(-1, 128)


def _from_rows(rows, shape):
    return rows.reshape(-1)[:math.prod(shape)].reshape(shape)


def _layer_consts(l, gmlp_ln_g, gmlp_ln_b, gmlp_w_s, gmlp_b_s, hgrn_onorm_g, fox_b_f):
    causal = jnp.tril(jnp.ones((CHUNK, CHUNK), bool))
    wm = jnp.where(causal[None], gmlp_w_s[l], 0.0)
    return dict(
        lng=gmlp_ln_g[l].reshape(1, A_WIDTH), lnb=gmlp_ln_b[l].reshape(1, A_WIDTH),
        wm=wm.astype(BF16), wmt=jnp.swapaxes(wm, 1, 2).astype(BF16),
        bst=jnp.pad(gmlp_b_s[l].T, ((0, 0), (0, 128 - A_GROUPS))),
        onorm=jnp.tile(hgrn_onorm_g[l], 4).reshape(1, B_WIDTH),
        bf=jnp.pad(fox_b_f[l], (0, 128 - C_HEADS)).reshape(1, 128),
    )


def kernel(x, norm_g, w_in, w_out, gmlp_ln_g, gmlp_ln_b, gmlp_w_s, gmlp_b_s, hgrn_lb, hgrn_onorm_g, fox_b_f, final_norm_g, loss_target, m_norm_g, m_w_in, m_w_out, m_gmlp_ln_g, m_gmlp_ln_b, m_gmlp_w_s, m_gmlp_b_s, m_hgrn_lb, m_hgrn_onorm_g, m_fox_b_f, m_final_norm_g, v_norm_g, v_w_in, v_w_out, v_gmlp_ln_g, v_gmlp_ln_b, v_gmlp_w_s, v_gmlp_b_s, v_hgrn_lb, v_hgrn_onorm_g, v_fox_b_f, v_final_norm_g):
    T = x.shape[1]
    shard_in = w_in.shape[2]
    shard_out = w_out.shape[1]
    xs = x.reshape(T, D_MODEL)
    tgt = loss_target.reshape(T, D_MODEL)

    w_in_b = [w_in[l].T.astype(BF16) for l in range(DEPTH)]
    w_out_b = w_out.astype(BF16)

    lb_all = _lb_fwd(hgrn_lb)
    consts = [_layer_consts(l, gmlp_ln_g, gmlp_ln_b, gmlp_w_s, gmlp_b_s, hgrn_onorm_g, fox_b_f) for l in range(DEPTH)]

    saved = []
    xl = xs
    w_in_l = _gather_halves(w_in_b[0], "w_in_l0")
    for l in range(DEPTH):
        cs = consts[l]
        tag = f"l{l}"
        h, proj = _inproj(xl, norm_g[l].reshape(1, D_MODEL), w_in_l, D_IN_PAD, tag)
        (ya,), (yb, ob, s0), (qt, kt, vt) = _run_parts(
            [_gmlp_fwd(proj, cs["lng"], cs["lnb"], cs["wm"], cs["bst"]),
             _hgrn_fwd(proj, lb_all[l].reshape(1, B_WIDTH), cs["onorm"]), _fox_prep(proj, cs["bf"])],
            (T // CHUNK,), f"mix_fwd_{tag}")
        ride = _ChipExchange("gather", (w_out_b[l],) + ((w_in_b[l + 1],) if l + 1 < DEPTH else ()))
        oc, lse, yc, *gathered = _fox_fwd(qt, kt, vt, proj, tag, ride)
        w_out_l = gathered[0].reshape(N_CHIPS * shard_out, D_MODEL)
        saved.append(dict(x=xl, h=h, proj=proj, ya=ya, yb=yb, yc=yc, ob=ob, s0=s0, qt=qt, kt=kt, oc=oc, lse=lse,
                          w_in=w_in_l, w_out=w_out_l))
        xl = _outproj(xl, ya, yb, yc, w_out_l, tag)
        if l + 1 < DEPTH:
            w_in_l = gathered[1]

    dx, loss_part, d_final = _loss_head(xl, final_norm_g.reshape(1, D_MODEL), tgt)

    g_small = {}
    dlb_rows, rin, rout = [None] * DEPTH, [None] * DEPTH, [None] * DEPTH
    slabs_in = None
    for l in reversed(range(DEPTH)):
        cs, sv = consts[l], saved[l]
        tag = f"l{l}"
        proj = sv["proj"]
        dy, dw_out = _outproj_bwd(dx, sv["ya"], sv["yb"], sv["yc"], sv["w_out"], tag)
        (da, dwm, dbst, dlng, dlnb), (db, dlb_rows[l], donorm) = _run_parts(
            [_gmlp_bwd(proj, dy, cs["lng"], cs["lnb"], cs["wm"], cs["wmt"], cs["bst"]),
             _hgrn_bwd(proj, dy, sv["ob"], sv["s0"], lb_all[l].reshape(1, B_WIDTH), cs["onorm"])],
            (T // CHUNK,), f"mix_bwd_{tag}")
        do, delta, dzc, dot, qtr = _fox_bwd_prep(proj, dy, sv["oc"], sv["qt"], tag)
        slabs_out = dw_out.reshape(N_CHIPS, shard_out, D_MODEL).astype(BF16)
        ride = _ChipExchange("scatter", (slabs_out,) + ((slabs_in,) if slabs_in is not None else ()))
        dqt, dkt, dvc, *received = _fox_bwd(sv["qt"], sv["kt"], proj, do, sv["lse"], delta, dot, qtr, tag, ride)
        rout[l] = received[0]
        if slabs_in is not None:
            rin[l + 1] = received[1]
        dqc, dkc, dflc, dbf = _fox_bwd_post(dqt, dkt, proj, cs["bf"], tag)
        g_small[l] = dict(ln_g=dlng, ln_b=dlnb, w_s=dwm, b_s=dbst, onorm=donorm, bf=dbf)
        dproj = [da, db, dqc, dkc, dvc, dzc, dflc]
        if l == 0:
            d_hgrn_lb = _lb_bwd(hgrn_lb, jnp.concatenate(dlb_rows, axis=0))
            per_layer = lambda key: [g_small[k][key] for k in range(DEPTH)]
            early = _pack_grads(per_layer("ln_g"), per_layer("ln_b"), per_layer("w_s"), per_layer("b_s"), d_hgrn_lb,
                                per_layer("onorm"), per_layer("bf"), d_final, loss_part)
            ride, parts = _DeviceGather(early), []
            for n, cols in enumerate(DW_IN_GROUPS):
                part, arrived = _dw_in(sv["h"], dproj, D_IN_PAD, shard_in, cols, f"{tag}_{n}", ride)
                if n == 0:
                    rearly = arrived
                else:
                    parts.append(arrived)
                ride = _ChipExchange("scatter", (part,))
        else:
            slabs_in = _dw_in(sv["h"], dproj, D_IN_PAD, shard_in, (0, D_MODEL), tag)
            ride = None
        dx, dng, *received = _dx_in(sv["x"], norm_g[l].reshape(1, D_MODEL), dx, dproj, sv["w_in"], tag, ride)
        if l == 0:
            rin[0] = jnp.concatenate(parts + received, axis=2)
        g_small[l]["norm_g"] = dng.reshape(D_MODEL // 128, 128)
    grad_x = dx.reshape(x.shape)
    rlate = _gather_devices(jnp.concatenate([g_small[l]["norm_g"] for l in range(DEPTH)]), "gather_norm_grads")
    rsmall = jnp.concatenate([rlate, rearly], axis=1)

    pin, pout = _sum_chips(rin, "sum_chips_w_in", False), _sum_chips(rout, "sum_chips_w_out", True)
    oin, oout = _swap_cores(pin, pout)
    to_view = lambda a: jnp.transpose(a, (2, 0, 1))
    g_w_in, d_w_in, nm_w_in, nv_w_in = [
        jnp.transpose(o, (1, 2, 0))
        for o in _adamw_pair(to_view(w_in), to_view(m_w_in), to_view(v_w_in), pin, oin, "adamw_w_in")]
    g_w_out, d_w_out, nm_w_out, nv_w_out = _adamw_pair(w_out, m_w_out, v_w_out, pout, oout, "adamw_w_out")

    small_w = [norm_g, gmlp_ln_g, gmlp_ln_b, gmlp_w_s, gmlp_b_s, hgrn_lb, hgrn_onorm_g, fox_b_f, final_norm_g]
    small_m = [m_norm_g, m_gmlp_ln_g, m_gmlp_ln_b, m_gmlp_w_s, m_gmlp_b_s, m_hgrn_lb, m_hgrn_onorm_g, m_fox_b_f, m_final_norm_g]
    small_v = [v_norm_g, v_gmlp_ln_g, v_gmlp_ln_b, v_gmlp_w_s, v_gmlp_b_s, v_hgrn_lb, v_hgrn_onorm_g, v_fox_b_f, v_final_norm_g]
    views = lambda ps: [_rows_view(p) for p in ps]
    per_param, loss_row = _adamw_small(views(small_w), views(small_m), views(small_v), rsmall)
    sg, sd, sm, sv_ = [[_from_rows(per_param[k][a], shape) for k, (_, shape) in enumerate(SMALL_PARAMS)] for a in range(4)]
    loss = loss_row[0, 0]

    def order(big_in, big_out, small):
        return [small[0], big_in, big_out] + small[1:]

    return (loss, grad_x, *order(g_w_in, g_w_out, sg), *order(d_w_in, d_w_out, sd), *order(nm_w_in, nm_w_out, sm),
            *order(nv_w_in, nv_w_out, sv_))
```

```python
import collections
import functools
import math

import jax
import jax.numpy as jnp
from jax import lax
from jax.experimental import pallas as pl
from jax.experimental.pallas import tpu as pltpu

F32 = jnp.float32
BF16 = jnp.bfloat16
SDS = jax.ShapeDtypeStruct
MESH_ID = pl.DeviceIdType.MESH

D_MODEL = 1024
DEPTH = 2
A_WIDTH = 256
A_GROUPS = 4
B_WIDTH = 256
C_WIDTH = 512
C_HEADS = 8
D_IN = 3848
D_IN_PAD = 4096
CHUNK = 128
SUB = 16
SUB_SHIFT = 4
NORM_EPS = 1e-6
F_FLOOR = 1e-30
COL_AU, COL_AV, COL_AZ = 0, 256, 512
COL_BQ, COL_BF, COL_BI, COL_BZ = 768, 1024, 1280, 1536
COL_CQ, COL_CK, COL_CV, COL_CZ, COL_CF = 1792, 2304, 2816, 3328, 3840
HEAD_LANES = 128
Q_SCALE = 0.125
ADAM_LR, ADAM_B1, ADAM_B2, ADAM_EPS, ADAM_WD, ADAM_STEP = 0.001, 0.9, 0.999, 1e-08, 0.01, 10
ADAM_C1 = 1.0 - ADAM_B1 ** ADAM_STEP
ADAM_C2 = 1.0 - ADAM_B2 ** ADAM_STEP
VMEM_LIMIT = 56 * 1024 * 1024
ADAMW_BLOCK_BYTES = 1 << 20
N_CHIPS = 4
N_DEV = 8

SMALL_PARAMS = (
    ("norm_g", (DEPTH, D_MODEL)), ("gmlp_ln_g", (DEPTH, 4, 64)), ("gmlp_ln_b", (DEPTH, 4, 64)),
    ("gmlp_w_s", (DEPTH, 4, 128, 128)), ("gmlp_b_s", (DEPTH, 4, 128)), ("hgrn_lb", (DEPTH, 256)),
    ("hgrn_onorm_g", (DEPTH, 64)), ("fox_b_f", (DEPTH, 8)), ("final_norm_g", (D_MODEL,)),
)


def _tile(n, pref):
    t = min(n, pref)
    assert n % t == 0, (n, pref)
    return t


def _params(*sem):
    return pltpu.CompilerParams(dimension_semantics=sem, vmem_limit_bytes=VMEM_LIMIT)


_Part = collections.namedtuple("_Part", "body operands in_specs out_specs out_shape scratch")


def _run_parts(parts, grid, name):
    counts = [(len(p.operands), len(p.out_shape), len(p.scratch)) for p in parts]

    def body(*refs):
        ins, outs, scr = [], [], []
        pos = 0
        for group, k in ((ins, 0), (outs, 1), (scr, 2)):
            for c in counts:
                group.append(refs[pos:pos + c[k]])
                pos += c[k]
        for p, i, o, s in zip(parts, ins, outs, scr):
            p.body(*i, *o, *s)

    flat = lambda key: [x for p in parts for x in getattr(p, key)]
    res = pl.pallas_call(
        body, name=name, grid=grid, in_specs=flat("in_specs"), out_specs=flat("out_specs"), out_shape=flat("out_shape"),
        scratch_shapes=flat("scratch"), compiler_params=_params(*(("arbitrary",) * len(grid))),
    )(*flat("operands"))
    out, pos = [], 0
    for c in counts:
        out.append(list(res[pos:pos + c[1]]))
        pos += c[1]
    return out


def _dot(a, b):
    return jnp.dot(a, b, preferred_element_type=F32)


def _dot_nt(a, b):
    return lax.dot_general(a, b, (((1,), (1,)), ((), ())), preferred_element_type=F32)


def _dot_tn(a, b):
    return lax.dot_general(a, b, (((0,), (0,)), ((), ())), preferred_element_type=F32)


def _split3(x):
    hi = x.astype(BF16)
    r = x - hi.astype(F32)
    mid = r.astype(BF16)
    lo = (r - mid.astype(F32)).astype(BF16)
    return hi, mid, lo


def _dot3_left(c, x):
    hi, mid, lo = _split3(x)
    return _dot(c, hi) + _dot(c, mid) + _dot(c, lo)


def _sigmoid(x):
    return jax.nn.sigmoid(x)


def _silu_and_grad(x):
    s = _sigmoid(x)
    return x * s, s * (1.0 + x * (1.0 - s))


_GELU_C = math.sqrt(2.0 / math.pi)


def _gelu_and_grad(x):
    inner = _GELU_C * (x + 0.044715 * x * x * x)
    t = jnp.tanh(inner)
    y = 0.5 * x * (1.0 + t)
    dy = 0.5 * (1.0 + t) + 0.5 * x * (1.0 - t * t) * _GELU_C * (1.0 + 3.0 * 0.044715 * x * x)
    return y, dy


def _lane(shape):
    return lax.broadcasted_iota(jnp.int32, shape, 1)


def _row(shape):
    return lax.broadcasted_iota(jnp.int32, shape, 0)


def _gsum64(x):
    lo = _lane(x.shape) < 64
    s0 = jnp.sum(jnp.where(lo, x, 0.0), axis=-1, keepdims=True)
    s1 = jnp.sum(jnp.where(lo, 0.0, x), axis=-1, keepdims=True)
    return jnp.where(lo, s0, s1)


def _colreduce(x, op):
    parts = [x[r:r + 8, :] for r in range(0, x.shape[0], 8)]
    while len(parts) > 1:
        pairs = [op(parts[k], parts[k + 1]) for k in range(0, len(parts) - 1, 2)]
        parts = pairs + ([parts[-1]] if len(parts) % 2 else [])
    red = jnp.max if op is jnp.maximum else jnp.sum
    return red(parts[0], axis=0, keepdims=True)


def _block_diag64(dtype=BF16):
    r, c = _row((128, 128)), _lane((128, 128))
    return jnp.where((r >> 6) == (c >> 6), 1.0, 0.0).astype(dtype)


def _assemble_w_in(slab_ref, wt_ref):
    shard = slab_ref.shape[1]
    top = N_CHIPS * shard // 16 * 16
    wt_ref[top:, :] = jnp.zeros((wt_ref.shape[0] - top, wt_ref.shape[1]), wt_ref.dtype)
    for k in range(N_CHIPS):
        wt_ref[shard * k:shard * (k + 1), :] = slab_ref[k]


def _inproj(x, g, w, dp_width, tag):
    T, D = x.shape
    tm = _tile(T, 512)

    def body(x_ref, g_ref, w_ref, h_ref, p_ref, wt_ref):
        pl.when(pl.program_id(0) == 0)(lambda: _assemble_w_in(w_ref, wt_ref))
        xv = x_ref[...]
        r = lax.rsqrt(jnp.mean(xv * xv, axis=-1, keepdims=True) + NORM_EPS)
        h = (xv * r * g_ref[...]).astype(BF16)
        h_ref[...] = h
        p_ref[...] = _dot_nt(h, wt_ref[...])

    return pl.pallas_call(
        body, name=f"inproj_{tag}", grid=(T // tm,),
        in_specs=[pl.BlockSpec((tm, D), lambda i: (i, 0)), pl.BlockSpec((1, D), lambda i: (0, 0)),
                  pl.BlockSpec(w.shape, lambda i: (0, 0, 0))],
        out_specs=[pl.BlockSpec((tm, D), lambda i: (i, 0)), pl.BlockSpec((tm, dp_width), lambda i: (i, 0))],
        out_shape=[SDS((T, D), BF16), SDS((T, dp_width), F32)],
        scratch_shapes=[pltpu.VMEM((dp_width, D), BF16)],
        compiler_params=_params("arbitrary"),
    )(x, g, w)


def _outproj(x, ya, yb, yc, wo, tag):
    T, D = x.shape
    tm = _tile(T, 512)

    def body(x_ref, ya_ref, yb_ref, yc_ref, wo_ref, o_ref):
        acc = x_ref[...] + _dot(ya_ref[...], wo_ref[0:A_WIDTH, :])
        acc = acc + _dot(yb_ref[...], wo_ref[A_WIDTH:A_WIDTH + B_WIDTH, :])
        o_ref[...] = acc + _dot(yc_ref[...], wo_ref[A_WIDTH + B_WIDTH:, :])

    row = lambda w: pl.BlockSpec((tm, w), lambda i: (i, 0))
    return pl.pallas_call(
        body, name=f"outproj_{tag}", grid=(T // tm,),
        in_specs=[row(D), row(A_WIDTH), row(B_WIDTH), row(C_WIDTH), pl.BlockSpec(wo.shape, lambda i: (0, 0))],
        out_specs=row(D), out_shape=SDS((T, D), F32), compiler_params=_params("parallel"),
    )(x, ya, yb, yc, wo)


def _outproj_bwd(dx, ya, yb, yc, wo, tag):
    T, D = dx.shape
    DM = wo.shape[0]
    tm = _tile(T, 512)

    def body(dx_ref, ya_ref, yb_ref, yc_ref, wo_ref, dy_ref, dwo_ref):
        @pl.when(pl.program_id(0) == 0)
        def _():
            dwo_ref[...] = jnp.zeros_like(dwo_ref)

        dxb = dx_ref[...].astype(BF16)
        dy_ref[...] = _dot_nt(dxb, wo_ref[...])
        dwo_ref[0:A_WIDTH, :] += _dot_tn(ya_ref[...], dxb)
        dwo_ref[A_WIDTH:A_WIDTH + B_WIDTH, :] += _dot_tn(yb_ref[...], dxb)
        dwo_ref[A_WIDTH + B_WIDTH:, :] += _dot_tn(yc_ref[...], dxb)

    row = lambda w: pl.BlockSpec((tm, w), lambda i: (i, 0))
    return pl.pallas_call(
        body, name=f"outproj_bwd_{tag}", grid=(T // tm,),
        in_specs=[row(D), row(A_WIDTH), row(B_WIDTH), row(C_WIDTH), pl.BlockSpec(wo.shape, lambda i: (0, 0))],
        out_specs=[row(DM), pl.BlockSpec((DM, D), lambda i: (0, 0))],
        out_shape=[SDS((T, DM), F32), SDS((DM, D), F32)], compiler_params=_params("arbitrary"),
    )(dx, ya, yb, yc, wo)


DW_IN_GROUPS = ((0, 256), (256, 256), (512, 512))


def _piece_offsets(pieces):
    offs = [0]
    for p in pieces:
        offs.append(offs[-1] + p.shape[1])
    return offs


def _dw_in(h, pieces, dp_width, shard, cols, tag, ride=None):
    T = h.shape[0]
    first, D = cols
    assert N_CHIPS * shard <= dp_width and first % D == 0
    tm = _tile(T, 512)
    grid = (T // tm,)
    offs = _piece_offsets(pieces)
    n = len(pieces)

    def body(h_ref, *rest):
        p_refs, rest = rest[:n], rest[n:]
        ride_srcs, (dw_ref,), ride_dsts, (acc_ref,), ride_sems = _ride_refs(ride, rest, 1, 1)
        i = pl.program_id(0)
        _ride_start(ride, grid, ride_srcs, ride_dsts, ride_sems)

        @pl.when(i == 0)
        def _():
            acc_ref[...] = jnp.zeros_like(acc_ref)

        hv = h_ref[...]
        for k, p_ref in enumerate(p_refs):
            acc_ref[offs[k]:offs[k + 1], :] += _dot_tn(p_ref[...], hv)

        @pl.when(i == grid[0] - 1)
        def _():
            for k in range(N_CHIPS):
                dw_ref[k] = acc_ref[shard * k:shard * (k + 1), :].astype(BF16)

        _ride_wait(ride, grid, ride_srcs, ride_dsts, ride_sems)

    extra = ride or _ChipExchange("gather", ())
    return pl.pallas_call(
        body, name=f"dw_in_{tag}", grid=grid,
        in_specs=[pl.BlockSpec((tm, D), lambda i: (i, first // D))]
        + [pl.BlockSpec((tm, p.shape[1]), lambda i: (i, 0)) for p in pieces] + extra.in_specs,
        out_specs=[pl.BlockSpec((N_CHIPS, shard, D), lambda i: (0, 0, 0))] + extra.out_specs,
        out_shape=[SDS((N_CHIPS, shard, D), BF16)] + extra.out_shape,
        scratch_shapes=[pltpu.VMEM((dp_width, D), F32)] + (extra.scratch if ride else []),
        compiler_params=pltpu.CompilerParams(dimension_semantics=("arbitrary",), vmem_limit_bytes=VMEM_LIMIT,
                                             has_side_effects=bool(ride)),
    )(h, *pieces, *extra.sources)


def _dx_in(x, g, dres, pieces, w, tag, ride=None):
    T, D = x.shape
    tm = _tile(T, 512)
    grid = (T // tm,)
    offs = _piece_offsets(pieces)
    n = len(pieces)

    def body(x_ref, g_ref, dres_ref, w_ref, *rest):
        p_refs, rest = rest[:n], rest[n:]
        ride_srcs, (dx_ref, dg_ref), ride_dsts, (wt_ref,), ride_sems = _ride_refs(ride, rest, 2, 1)
        _ride_start(ride, grid, ride_srcs, ride_dsts, ride_sems)

        @pl.when(pl.program_id(0) == 0)
        def _():
            dg_ref[...] = jnp.zeros_like(dg_ref)
            _assemble_w_in(w_ref, wt_ref)

        dh = _dot(p_refs[0][...], wt_ref[offs[0]:offs[1], :])
        for k in range(1, n):
            dh = dh + _dot(p_refs[k][...], wt_ref[offs[k]:offs[k + 1], :])
        xv = x_ref[...]
        r = lax.rsqrt(jnp.mean(xv * xv, axis=-1, keepdims=True) + NORM_EPS)
        xh = xv * r
        dg_ref[...] += jnp.sum(dh * xh, axis=0, keepdims=True)
        dxh = dh * g_ref[...]
        dx_ref[...] = dres_ref[...] + r * (dxh - xh * jnp.mean(dxh * xh, axis=-1, keepdims=True))
        _ride_wait(ride, grid, ride_srcs, ride_dsts, ride_sems)

    extra = ride or _ChipExchange("gather", ())
    row = pl.BlockSpec((tm, D), lambda i: (i, 0))
    return pl.pallas_call(
        body, name=f"dx_in_{tag}", grid=grid,
        in_specs=[row, pl.BlockSpec((1, D), lambda i: (0, 0)), row, pl.BlockSpec(w.shape, lambda i: (0, 0, 0))]
        + [pl.BlockSpec((tm, p.shape[1]), lambda i: (i, 0)) for p in pieces] + extra.in_specs,
        out_specs=[row, pl.BlockSpec((1, D), lambda i: (0, 0))] + extra.out_specs,
        out_shape=[SDS((T, D), F32), SDS((1, D), F32)] + extra.out_shape,
        scratch_shapes=[pltpu.VMEM((offs[-1], D), BF16)] + (extra.scratch if ride else []),
        compiler_params=pltpu.CompilerParams(dimension_semantics=("arbitrary",), vmem_limit_bytes=VMEM_LIMIT,
                                             has_side_effects=bool(ride)),
    )(x, g, dres, w, *pieces, *extra.sources)


def _loss_head(x, g, tgt):
    T, D = x.shape
    tm = _tile(T, 512)

    def body(x_ref, g_ref, t_ref, dx_ref, loss_ref, dg_ref):
        @pl.when(pl.program_id(0) == 0)
        def _():
            loss_ref[...] = jnp.zeros_like(loss_ref)
            dg_ref[...] = jnp.zeros_like(dg_ref)

        xv = x_ref[...]
        r = lax.rsqrt(jnp.mean(xv * xv, axis=-1, keepdims=True) + NORM_EPS)
        xh = xv * r
        gv = g_ref[...]
        err = xh * gv - t_ref[...]
        tok = jnp.mean(err * err, axis=-1, keepdims=True)
        loss_ref[...] += 0.5 * jnp.sum(tok, axis=0, keepdims=True)
        dy = err * (1.0 / D)
        dg_ref[...] += jnp.sum(dy * xh, axis=0, keepdims=True)
        dxh = dy * gv
        dx_ref[...] = r * (dxh - xh * jnp.mean(dxh * xh, axis=-1, keepdims=True))

    row = pl.BlockSpec((tm, D), lambda i: (i, 0))
    return pl.pallas_call(
        body, name="loss_head", grid=(T // tm,),
        in_specs=[row, pl.BlockSpec((1, D), lambda i: (0, 0)), row],
        out_specs=[row, pl.BlockSpec((1, 128), lambda i: (0, 0)), pl.BlockSpec((1, D), lambda i: (0, 0))],
        out_shape=[SDS((T, D), F32), SDS((1, 128), F32), SDS((1, D), F32)], compiler_params=_params("arbitrary"),
    )(x, g, tgt)


def _gmlp_core(u, v, lng, lnb, wm_ref, bst_ref, pair):
    ug, dug = _gelu_and_grad(u)
    vg, dvg = _gelu_and_grad(v)
    mu = _gsum64(vg) * (1.0 / 64)
    d = vg - mu
    var = _gsum64(d * d) * (1.0 / 64)
    rstd = lax.rsqrt(var + NORM_EPS)
    xh = d * rstd
    vn = xh * lng + lnb
    vnb = vn.astype(BF16)
    lo = _lane(u.shape) < 64
    g0, g1 = 2 * pair, 2 * pair + 1
    mixed = jnp.where(lo, _dot(wm_ref[g0], vnb) + bst_ref[:, g0:g0 + 1], _dot(wm_ref[g1], vnb) + bst_ref[:, g1:g1 + 1])
    return ug, dug, dvg, rstd, xh, vnb, mixed, lo


def _gmlp_fwd(proj, lng, lnb, wm, bst):
    T = proj.shape[0]

    def body(u_ref, v_ref, z_ref, lng_ref, lnb_ref, wm_ref, bst_ref, y_ref):
        for pair in range(2):
            sl = slice(128 * pair, 128 * pair + 128)
            ug, _, _, _, _, _, mixed, _ = _gmlp_core(u_ref[:, sl], v_ref[:, sl], lng_ref[:, sl], lnb_ref[:, sl],
                                                     wm_ref, bst_ref, pair)
            sz, _ = _silu_and_grad(z_ref[:, sl])
            y_ref[:, sl] = (ug * mixed * sz).astype(BF16)

    col = lambda c: pl.BlockSpec((CHUNK, A_WIDTH), lambda i, c=c: (i, c // A_WIDTH))
    full = lambda a: pl.BlockSpec(a.shape, lambda i, n=a.ndim: (0,) * n)
    return _Part(body, (proj, proj, proj, lng, lnb, wm, bst),
                 [col(COL_AU), col(COL_AV), col(COL_AZ), full(lng), full(lnb), full(wm), full(bst)],
                 [pl.BlockSpec((CHUNK, A_WIDTH), lambda i: (i, 0))], [SDS((T, A_WIDTH), BF16)], [])


def _gmlp_bwd(proj, dy, lng, lnb, wm, wmt, bst):
    T = proj.shape[0]
    n = T // CHUNK

    def body(u_ref, v_ref, z_ref, dy_ref, lng_ref, lnb_ref, wm_ref, wmt_ref, bst_ref,
             da_ref, dwm_ref, dbst_ref, dlng_ref, dlnb_ref):
        @pl.when(pl.program_id(0) == 0)
        def _():
            dwm_ref[...] = jnp.zeros_like(dwm_ref)
            dbst_ref[...] = jnp.zeros_like(dbst_ref)
            dlng_ref[...] = jnp.zeros_like(dlng_ref)
            dlnb_ref[...] = jnp.zeros_like(dlnb_ref)

        lane = _lane((CHUNK, 128))
        dbst = dbst_ref[...]
        for pair in range(2):
            sl = slice(128 * pair, 128 * pair + 128)
            lng_p = lng_ref[:, sl]
            ug, dug, dvg, rstd, xh, vnb, mixed, lo = _gmlp_core(u_ref[:, sl], v_ref[:, sl], lng_p, lnb_ref[:, sl],
                                                                wm_ref, bst_ref, pair)
            sz, dsz = _silu_and_grad(z_ref[:, sl])
            dyv = dy_ref[:, sl]
            out = ug * mixed
            dz = dyv * out * dsz
            dout = dyv * sz
            du = dout * mixed * dug
            dmix = dout * ug
            g0, g1 = 2 * pair, 2 * pair + 1
            dm0 = jnp.where(lo, dmix, 0.0)
            dm1 = jnp.where(lo, 0.0, dmix)
            dbst = dbst + jnp.where(lane == g0, jnp.sum(dm0, axis=-1, keepdims=True), 0.0)
            dbst = dbst + jnp.where(lane == g1, jnp.sum(dm1, axis=-1, keepdims=True), 0.0)
            dwm_ref[g0] += _dot_nt(dm0.astype(BF16), vnb)
            dwm_ref[g1] += _dot_nt(dm1.astype(BF16), vnb)
            dmb = dmix.astype(BF16)
            dvn = jnp.where(lo, _dot(wmt_ref[g0], dmb), _dot(wmt_ref[g1], dmb))
            dlng_ref[:, sl] += jnp.sum(dvn * xh, axis=0, keepdims=True)
            dlnb_ref[:, sl] += jnp.sum(dvn, axis=0, keepdims=True)
            dxh = dvn * lng_p
            m1 = _gsum64(dxh) * (1.0 / 64)
            m2 = _gsum64(dxh * xh) * (1.0 / 64)
            dv = rstd * (dxh - m1 - xh * m2) * dvg
            da_ref[:, COL_AU + 128 * pair:COL_AU + 128 * pair + 128] = du.astype(BF16)
            da_ref[:, COL_AV + 128 * pair:COL_AV + 128 * pair + 128] = dv.astype(BF16)
            da_ref[:, COL_AZ + 128 * pair:COL_AZ + 128 * pair + 128] = dz.astype(BF16)
        dbst_ref[...] = dbst

        @pl.when(pl.program_id(0) == n - 1)
        def _():
            causal = _lane((CHUNK, CHUNK)) <= _row((CHUNK, CHUNK))
            for g in range(A_GROUPS):
                dwm_ref[g] = jnp.where(causal, dwm_ref[g], 0.0)

    col = lambda c: pl.BlockSpec((CHUNK, A_WIDTH), lambda i, c=c: (i, c // A_WIDTH))
    full = lambda a: pl.BlockSpec(a.shape, lambda i, n=a.ndim: (0,) * n)
    acc = lambda s: pl.BlockSpec(s, lambda i, n=len(s): (0,) * n)
    return _Part(body, (proj, proj, proj, dy, lng, lnb, wm, wmt, bst),
                 [col(COL_AU), col(COL_AV), col(COL_AZ), pl.BlockSpec((CHUNK, A_WIDTH), lambda i: (i, 0)),
                  full(lng), full(lnb), full(wm), full(wmt), full(bst)],
                 [pl.BlockSpec((CHUNK, 3 * A_WIDTH), lambda i: (i, 0)), acc((A_GROUPS, CHUNK, CHUNK)),
                  acc((CHUNK, 128)), acc((1, A_WIDTH)), acc((1, A_WIDTH))],
                 [SDS((T, 3 * A_WIDTH), BF16), SDS((A_GROUPS, CHUNK, CHUNK), F32), SDS((CHUNK, 128), F32),
                  SDS((1, A_WIDTH), F32), SDS((1, A_WIDTH), F32)], [])


def _hgrn_consts():
    r, c = _row((CHUNK, CHUNK)), _lane((CHUNK, CHUNK))
    same = (r >> SUB_SHIFT) == (c >> SUB_SHIFT)
    lsub = jnp.where(same & (c <= r), 1.0, 0.0).astype(BF16)
    usub = jnp.where(same & (c >= r), 1.0, 0.0).astype(BF16)
    bsub = jnp.where(same, 1.0, 0.0).astype(BF16)
    return lsub, usub, bsub


def _hgrn_gates(qv, zf, lbp):
    sq, dsq = _silu_and_grad(qv)
    qt = sq * Q_SCALE
    sg = _sigmoid(zf)
    sgn = _sigmoid(-zf)
    f = lbp + (1.0 - lbp) * sg
    g = jnp.log(jnp.maximum(f, F_FLOOR))
    kf = (1.0 - lbp) * sgn
    return qt, dsq, sg, sgn, f, g, kf


def _hgrn_intra_scores(qt, kf, b, mbd):
    rid = _row((SUB, 128))
    parts = []
    for s in range(SUB):
        e = jnp.exp(b - b[s:s + 1, :])
        parts.append(jnp.where(rid >= s, qt * kf[s:s + 1, :] * e, 0.0))
    return _dot(jnp.concatenate(parts, axis=0).astype(BF16), mbd)


def _hgrn_intra_out(a, v):
    o = jnp.zeros((SUB, 128), F32)
    for s in range(SUB):
        o = o + a[SUB * s:SUB * s + SUB, :] * v[s:s + 1, :]
    return o


def _hgrn_intra_bwd_scores(qt, kf, b, v, do, mbd):
    rid = _row((SUB, 128))
    ps, das, kes, es = [], [], [], []
    for s in range(SUB):
        e = jnp.where(rid >= s, jnp.exp(b - b[s:s + 1, :]), 0.0)
        ke = kf[s:s + 1, :] * e
        es.append(e)
        kes.append(ke)
        ps.append(qt * ke)
        das.append(do * v[s:s + 1, :])
    a = _dot(jnp.concatenate(ps, axis=0).astype(BF16), mbd)
    da = _dot(jnp.concatenate(das, axis=0).astype(BF16), mbd)
    return a, da, kes, es


def _hgrn_intra_bwd_grads(scores, qt, do, rsum):
    a, da, kes, es = scores
    dqt = jnp.zeros((SUB, 128), F32)
    xs, ys = [], []
    for s in range(SUB):
        da_s = da[SUB * s:SUB * s + SUB, :]
        dqt = dqt + da_s * kes[s]
        xs.append(a[SUB * s:SUB * s + SUB, :] * do)
        ys.append(da_s * qt * es[s])
    dv = _dot(rsum, jnp.concatenate(xs, axis=0).astype(BF16))
    dkf = _dot(rsum, jnp.concatenate(ys, axis=0).astype(BF16))
    return dqt, dkf, dv


def _hgrn_norm_gate(o, z, onorm):
    ms = _gsum64(o * o) * (1.0 / 64)
    r = lax.rsqrt(ms + NORM_EPS)
    xh = o * r
    sz, dsz = _silu_and_grad(z)
    return xh, r, sz, dsz, xh * onorm


def _hgrn_fwd(proj, lb, onorm):
    T = proj.shape[0]
    n = T // CHUNK
    nsub = CHUNK // SUB

    def body(q_ref, f_ref, i_ref, z_ref, lb_ref, on_ref, y_ref, o_ref, s0_ref, st_ref):
        @pl.when(pl.program_id(0) == 0)
        def _():
            st_ref[...] = jnp.zeros_like(st_ref)

        lsub, _, bsub = _hgrn_consts()
        mbd = _block_diag64()
        bdmask = mbd > 0
        rid = _row((CHUNK, 128))
        subs = [slice(SUB * sub, SUB * sub + SUB) for sub in range(nsub)]
        work = []
        for pair in range(2):
            sl = slice(128 * pair, 128 * pair + 128)
            qt, _, _, _, _, g, kf = _hgrn_gates(q_ref[:, sl], f_ref[:, sl], lb_ref[:, sl])
            work.append(dict(sl=sl, qt=qt, kf=kf, v=i_ref[:, sl], b=_dot3_left(lsub, g), bl=_dot3_left(bsub, g)))
        for w in work:
            qt, kf, v, b, bl = w["qt"], w["kf"], w["v"], w["b"], w["bl"]
            w["qh"] = (qt * jnp.exp(b)).astype(BF16)
            kh = kf * jnp.exp(bl - b)
            w["dec"] = jnp.exp(bl)
            vtb = v.T.astype(BF16)
            w["scores"] = [_hgrn_intra_scores(qt[rs], kf[rs], b[rs], mbd) for rs in subs]
            w["adds"] = [_dot(vtb, jnp.where((rid >> SUB_SHIFT) == sub, kh, 0.0).astype(BF16)) for sub in range(nsub)]
        for pair, w in enumerate(work):
            w["st"] = st_ref[pair]
            s0_ref[0, pair] = w["st"]
            w["outs"] = []
        for sub, rs in enumerate(subs):
            for w in work:
                w["outs"].append(_dot_nt(w["qh"][rs], w["st"].astype(BF16)) + _hgrn_intra_out(w["scores"][sub], w["v"][rs]))
                w["st"] = jnp.where(bdmask, w["st"] * w["dec"][SUB * sub:SUB * sub + 1, :] + w["adds"][sub], 0.0)
        for pair, w in enumerate(work):
            sl = w["sl"]
            st_ref[pair] = w["st"]
            o = jnp.concatenate(w["outs"], axis=0)
            o_ref[:, sl] = o
            _, _, sz, _, on = _hgrn_norm_gate(o, z_ref[:, sl], on_ref[:, sl])
            y_ref[:, sl] = (on * sz).astype(BF16)

    col = lambda c: pl.BlockSpec((CHUNK, B_WIDTH), lambda i, c=c: (i, c // B_WIDTH))
    full = lambda a: pl.BlockSpec(a.shape, lambda i, n=a.ndim: (0,) * n)
    return _Part(body, (proj, proj, proj, proj, lb, onorm),
                 [col(COL_BQ), col(COL_BF), col(COL_BI), col(COL_BZ), full(lb), full(onorm)],
                 [pl.BlockSpec((CHUNK, B_WIDTH), lambda i: (i, 0)), pl.BlockSpec((CHUNK, B_WIDTH), lambda i: (i, 0)),
                  pl.BlockSpec((1, 2, 128, 128), lambda i: (i, 0, 0, 0))],
                 [SDS((T, B_WIDTH), BF16), SDS((T, B_WIDTH), F32), SDS((n, 2, 128, 128), F32)],
                 [pltpu.VMEM((2, 128, 128), F32)])


def _hgrn_bwd(proj, dy, o_saved, s0, lb, onorm):
    T = proj.shape[0]
    n = T // CHUNK
    nsub = CHUNK // SUB

    def body(q_ref, f_ref, i_ref, z_ref, dy_ref, o_ref, s0_ref, lb_ref, on_ref,
             db_ref, dlb_ref, don_ref, dst_ref, sts_ref):
        @pl.when(pl.program_id(0) == 0)
        def _():
            dst_ref[...] = jnp.zeros_like(dst_ref)
            dlb_ref[...] = jnp.zeros_like(dlb_ref)
            don_ref[...] = jnp.zeros_like(don_ref)

        lsub, usub, bsub = _hgrn_consts()
        mbd = _block_diag64()
        bdmask = mbd > 0
        rsum = jnp.where((_lane((SUB, SUB * SUB)) >> SUB_SHIFT) == _row((SUB, SUB * SUB)), 1.0, 0.0).astype(BF16)
        subs = [slice(SUB * sub, SUB * sub + SUB) for sub in range(nsub)]
        work = []
        for pair in range(2):
            sl = slice(128 * pair, 128 * pair + 128)
            lbp = lb_ref[:, sl]
            qt, dsq, sg, sgn, f, g, kf = _hgrn_gates(q_ref[:, sl], f_ref[:, sl], lbp)
            w = dict(sl=sl, lbp=lbp, qt=qt, dsq=dsq, sg=sg, sgn=sgn, f=f, kf=kf, v=i_ref[:, sl],
                     b=_dot3_left(lsub, g), bl=_dot3_left(bsub, g))
            onp = on_ref[:, sl]
            xh, r, sz, dsz, on = _hgrn_norm_gate(o_ref[:, sl], z_ref[:, sl], onp)
            dyv = dy_ref[:, sl]
            w["dz"] = dyv * on * dsz
            don = dyv * sz
            cn = jnp.sum(don * xh, axis=0, keepdims=True)
            don_ref[...] += cn + pltpu.roll(cn, 64, axis=1)
            dxo = don * onp
            w["do"] = r * (dxo - xh * (_gsum64(dxo * xh) * (1.0 / 64)))
            work.append(w)
        for w in work:
            qt, kf, v, b, bl, do = w["qt"], w["kf"], w["v"], w["b"], w["bl"], w["do"]
            w["eb"] = jnp.exp(b)
            w["ekb"] = jnp.exp(bl - b)
            w["qhb"] = (qt * w["eb"]).astype(BF16)
            w["khb"] = (kf * w["ekb"]).astype(BF16)
            w["dec"] = jnp.exp(bl)
            w["vb"] = v.astype(BF16)
            w["dob"] = do.astype(BF16)
            w["scores"] = [_hgrn_intra_bwd_scores(qt[rs], kf[rs], b[rs], v[rs], do[rs], mbd) for rs in subs]
            w["st_adds"] = [_dot_tn(w["vb"][rs], w["khb"][rs]) for rs in subs]
            w["gst_adds"] = [_dot_tn(w["dob"][rs], w["qhb"][rs]) for rs in subs]
        for pair, w in enumerate(work):
            w["st"] = s0_ref[0, pair]
        for sub in range(nsub):
            for pair, w in enumerate(work):
                sts_ref[pair, sub] = w["st"]
                w["st"] = jnp.where(bdmask, w["st"] * w["dec"][SUB * sub:SUB * sub + 1, :] + w["st_adds"][sub], 0.0)
        for pair, w in enumerate(work):
            w["gst"] = dst_ref[pair]
            w["dqt_p"], w["dkf_p"], w["dv_p"], w["dbl_p"] = ([None] * nsub for _ in range(4))
        for sub in reversed(range(nsub)):
            rs = subs[sub]
            for pair, w in enumerate(work):
                gst = w["gst"]
                st_in = sts_ref[pair, sub]
                gb = gst.astype(BF16)
                dqh = _dot(w["dob"][rs], st_in.astype(BF16))
                dkh = _dot(w["vb"][rs], gb)
                dv_inter = _dot_nt(w["khb"][rs], gb)
                ddec = jnp.sum(gst * st_in, axis=0, keepdims=True)
                dec_row = w["dec"][SUB * sub:SUB * sub + 1, :]
                w["gst"] = jnp.where(bdmask, gst * dec_row + w["gst_adds"][sub], 0.0)
                dqt_i, dkf_i, dv_i = _hgrn_intra_bwd_grads(w["scores"][sub], w["qt"][rs], w["do"][rs], rsum)
                dkf_inter = dkh * w["ekb"][rs]
                w["dqt_p"][sub] = dqh * w["eb"][rs] + dqt_i
                w["dkf_p"][sub] = dkf_inter + dkf_i
                w["dv_p"][sub] = dv_inter + dv_i
                row = jnp.sum(w["kf"][rs] * dkf_inter, axis=0, keepdims=True) + ddec * dec_row
                w["dbl_p"][sub] = jnp.broadcast_to(row, (SUB, 128))
        for pair, w in enumerate(work):
            sl, lbp, sg, sgn, f = w["sl"], w["lbp"], w["sg"], w["sgn"], w["f"]
            dst_ref[pair] = w["gst"]
            dqt = jnp.concatenate(w["dqt_p"], axis=0)
            dkf = jnp.concatenate(w["dkf_p"], axis=0)
            dv = jnp.concatenate(w["dv_p"], axis=0)
            dg = _dot3_left(usub, w["qt"] * dqt - w["kf"] * dkf) + jnp.concatenate(w["dbl_p"], axis=0)
            df = jnp.where(f > F_FLOOR, dg / f, 0.0)
            dlb_ref[:, sl] += jnp.sum(df * (1.0 - sg) - dkf * sgn, axis=0, keepdims=True)
            dfl = (1.0 - lbp) * sg * sgn * (df - dkf)
            dq = dqt * Q_SCALE * w["dsq"]
            db_ref[:, 0 * B_WIDTH + 128 * pair:0 * B_WIDTH + 128 * pair + 128] = dq.astype(BF16)
            db_ref[:, 1 * B_WIDTH + 128 * pair:1 * B_WIDTH + 128 * pair + 128] = dfl.astype(BF16)
            db_ref[:, 2 * B_WIDTH + 128 * pair:2 * B_WIDTH + 128 * pair + 128] = dv.astype(BF16)
            db_ref[:, 3 * B_WIDTH + 128 * pair:3 * B_WIDTH + 128 * pair + 128] = w["dz"].astype(BF16)

    rev = lambda c: pl.BlockSpec((CHUNK, B_WIDTH), lambda i, c=c: (n - 1 - i, c // B_WIDTH))
    full = lambda a: pl.BlockSpec(a.shape, lambda i, n_=a.ndim: (0,) * n_)
    acc = lambda s: pl.BlockSpec(s, lambda i, n_=len(s): (0,) * n_)
    return _Part(body, (proj, proj, proj, proj, dy, o_saved, s0, lb, onorm),
                 [rev(COL_BQ), rev(COL_BF), rev(COL_BI), rev(COL_BZ),
                  pl.BlockSpec((CHUNK, B_WIDTH), lambda i: (n - 1 - i, 1)),
                  pl.BlockSpec((CHUNK, B_WIDTH), lambda i: (n - 1 - i, 0)),
                  pl.BlockSpec((1, 2, 128, 128), lambda i: (n - 1 - i, 0, 0, 0)), full(lb), full(onorm)],
                 [pl.BlockSpec((CHUNK, 4 * B_WIDTH), lambda i: (n - 1 - i, 0)), acc((1, B_WIDTH)), acc((1, 128))],
                 [SDS((T, 4 * B_WIDTH), BF16), SDS((1, B_WIDTH), F32), SDS((1, 128), F32)],
                 [pltpu.VMEM((2, 128, 128), F32), pltpu.VMEM((2, nsub, 128, 128), F32)])


def _lb_fwd(hgrn_lb):
    assert hgrn_lb.shape[0] == 2

    def body(x_ref, o_ref):
        x0, x1 = x_ref[0:1, :], x_ref[1:2, :]
        m = jnp.maximum(x0, x1)
        e0, e1 = jnp.exp(x0 - m), jnp.exp(x1 - m)
        p0, p1 = e0 / (e0 + e1), e1 / (e0 + e1)
        o_ref[0:1, :] = jnp.clip(p0 - p0, 0.0, 1.0 - 1e-6)
        o_ref[1:2, :] = jnp.clip((p0 + p1) - p0, 0.0, 1.0 - 1e-6)

    return pl.pallas_call(body, name="lb_fwd", out_shape=SDS(hgrn_lb.shape, F32))(hgrn_lb)


def _lb_bwd(hgrn_lb, dlb):
    def body(x_ref, d_ref, o_ref):
        x0, x1 = x_ref[0:1, :], x_ref[1:2, :]
        m = jnp.maximum(x0, x1)
        e0, e1 = jnp.exp(x0 - m), jnp.exp(x1 - m)
        p0, p1 = e0 / (e0 + e1), e1 / (e0 + e1)
        val = (p0 + p1) - p0
        dp1 = jnp.where((val > 0.0) & (val < 1.0 - 1e-6), d_ref[1:2, :], 0.0)
        inner = p1 * dp1
        o_ref[0:1, :] = p0 * (0.0 - inner)
        o_ref[1:2, :] = p1 * (dp1 - inner)

    return pl.pallas_call(body, name="lb_bwd", out_shape=SDS(hgrn_lb.shape, F32))(hgrn_lb, dlb)


def _fox_prep(proj, bf):
    T = proj.shape[0]
    n = T // CHUNK

    def body(q0_ref, q1_ref, k0_ref, k1_ref, v0_ref, v1_ref, fl_ref, bf_ref, qo_ref, ko_ref, vt_ref, carry_ref):
        for p, v_ref in enumerate((v0_ref, v0_ref, v1_ref, v1_ref)):
            vt_ref[p, 0] = v_ref[:, 128 * (p % 2):128 * (p % 2) + 128].T.astype(BF16)

        @pl.when(pl.program_id(0) == 0)
        def _():
            carry_ref[...] = jnp.zeros_like(carry_ref)

        ltri = jnp.where(_lane((CHUNK, CHUNK)) <= _row((CHUNK, CHUNK)), 1.0, 0.0).astype(BF16)
        lf = jax.nn.log_sigmoid(fl_ref[...] + bf_ref[...])
        c = _dot3_left(ltri, lf) + carry_ref[...]
        carry_ref[...] = c[CHUNK - 1:CHUNK, :]
        lane = _lane((CHUNK, 128))
        feat = lane < 64
        ones_q = (lane >= 67) & (lane <= 69)
        ones_k = (lane >= 64) & (lane <= 66)
        qrefs, krefs = (q0_ref, q1_ref), (k0_ref, k1_ref)
        for h in range(C_HEADS):
            blk = slice(128 * ((h // 2) % 2), 128 * ((h // 2) % 2) + 128)
            qp, kp = qrefs[h // 4][:, blk], krefs[h // 4][:, blk]
            if h % 2:
                qp, kp = pltpu.roll(qp, 64, axis=1), pltpu.roll(kp, 64, axis=1)
            ch = jnp.broadcast_to(c[:, h:h + 1], (CHUNK, 128))
            hi = ch.astype(BF16).astype(F32)
            r1 = ch - hi
            mid = r1.astype(BF16).astype(F32)
            lo = r1 - mid
            aq = jnp.where(lane == 64, hi, jnp.where(lane == 65, mid, jnp.where(lane == 66, lo,
                           jnp.where(ones_q, 1.0, 0.0))))
            ak = jnp.where(lane == 67, -hi, jnp.where(lane == 68, -mid, jnp.where(lane == 69, -lo,
                           jnp.where(ones_k, 1.0, 0.0))))
            qo_ref[:, 128 * h:128 * h + 128] = jnp.where(feat, qp * Q_SCALE, aq).astype(BF16)
            ko_ref[:, 128 * h:128 * h + 128] = jnp.where(feat, kp, ak).astype(BF16)

    w = 256
    col = lambda c: pl.BlockSpec((CHUNK, w), lambda i, c=c: (i, c // w))
    return _Part(body, (proj, proj, proj, proj, proj, proj, proj, bf),
                 [col(COL_CQ), col(COL_CQ + w), col(COL_CK), col(COL_CK + w), col(COL_CV), col(COL_CV + w),
                  pl.BlockSpec((CHUNK, 128), lambda i: (i, COL_CF // 128)), pl.BlockSpec((1, 128), lambda i: (0, 0))],
                 [pl.BlockSpec((CHUNK, C_HEADS * 128), lambda i: (i, 0))] * 2
                 + [pl.BlockSpec((C_HEADS // 2, 1, 128, CHUNK), lambda i: (0, i, 0, 0))],
                 [SDS((T, C_HEADS * 128), BF16)] * 2 + [SDS((C_HEADS // 2, n, 128, CHUNK), BF16)],
                 [pltpu.VMEM((1, 128), F32)])


FOX_TILE = 512
FOX_KEYS = 512
FOX_STRIP = 16


def _fox_mask(tk, tq, k0, q0):
    return (_row((tk, tq)) + (k0 - q0)) <= _lane((tk, tq))


def _ride_refs(ride, rest, n_out, n_scratch):
    n = ride.n if ride else 0
    srcs, rest = rest[:n], rest[n:]
    outs, rest = rest[:n_out], rest[n_out:]
    dsts, rest = rest[:n], rest[n:]
    return srcs, outs, dsts, rest[:n_scratch], rest[n_scratch:]


def _ride_start(ride, grid, srcs, dsts, sems):
    if ride:
        first = functools.reduce(lambda a, b: a & b, [pl.program_id(d) == 0 for d in range(len(grid))])
        pl.when(first)(lambda: ride.start(srcs, dsts, sems))


def _ride_wait(ride, grid, srcs, dsts, sems):
    if ride:
        last = functools.reduce(lambda a, b: a & b, [pl.program_id(d) == n - 1 for d, n in enumerate(grid)])
        pl.when(last)(lambda: ride.wait(srcs, dsts, sems))


def _fox_fwd(qt, kt, vt, proj, tag, ride=None):
    T = proj.shape[0]
    tq, tk = _tile(T, FOX_TILE), _tile(T, FOX_KEYS)
    nq, nsub = T // tq, tk // CHUNK
    npair = C_HEADS // 2

    def body(q_ref, k_ref, vt_ref, z_ref, *rest):
        ride_srcs, (o_ref, lse_ref, y_ref), ride_dsts, (acc_ref, st_ref, pt_ref), ride_sems = _ride_refs(ride, rest, 3, 3)
        i = pl.program_id(1)
        _ride_start(ride, (npair, nq), ride_srcs, ride_dsts, ride_sems)

        qs = (q_ref[:, 0:128], q_ref[:, 128:256])
        acc_ref[...] = jnp.zeros_like(acc_ref)
        pt_ref[...] = jnp.zeros_like(pt_ref)
        nfull = (i * tq) // tk

        def scores(j):
            kb = k_ref[pl.ds(pl.multiple_of(j * tk, tk), tk), :]
            return tuple(_dot_nt(kb[:, 128 * h:128 * h + 128], qs[h]) for h in range(2))

        def weigh(j, h):
            rows = slice(64 * h, 64 * h + 64)
            vth = jnp.concatenate([vt_ref[0, nsub * j + c, rows, :] for c in range(nsub)], axis=1)
            return _dot(vth, pt_ref[h])

        def block(j, carry, diagonal):
            nxt = () if diagonal else scores(j + 1)
            pvs = [weigh(jnp.maximum(j - 1, 0), h) for h in range(2)]
            new = []
            for h in range(2):
                m, l, alpha_prev = carry[3 * h:3 * h + 3]
                st = st_ref[h]
                if diagonal:
                    st = jnp.where(_fox_mask(tk, tq, j * tk, i * tq), st, -jnp.inf)
                m_new = jnp.maximum(m, _colreduce(st, jnp.maximum))
                pt = jnp.exp(st - m_new)
                alpha = jnp.exp(m - m_new)
                rows = slice(64 * h, 64 * h + 64)
                acc_ref[rows, :] = alpha_prev * acc_ref[rows, :] + pvs[h]
                pt_ref[h] = pt.astype(BF16)
                new += [m_new, alpha * l + _colreduce(pt, jnp.add), alpha]
            for h, st in enumerate(nxt):
                st_ref[h] = st
            return tuple(new)

        for h, st in enumerate(scores(0)):
            st_ref[h] = st
        init = (jnp.full((1, tq), -jnp.inf, F32), jnp.zeros((1, tq), F32), jnp.ones((1, tq), F32)) * 2
        carry = lax.fori_loop(0, nfull, lambda j, c: block(j, c, False), init)
        m0, l0, a0, m1, l1, a1 = block(nfull, carry, True)
        for h, alpha in enumerate((a0, a1)):
            rows = slice(64 * h, 64 * h + 64)
            acc_ref[rows, :] = alpha * acc_ref[rows, :] + weigh(nfull, h)
        inv = jnp.where(_row((128, tq)) < 64, 1.0 / l0, 1.0 / l1)
        o = (acc_ref[...] * inv).T
        o_ref[...] = o
        r8 = _row((8, tq))
        lse_ref[0, 0] = jnp.where(r8 == 0, m0 + jnp.log(l0), jnp.where(r8 == 1, m1 + jnp.log(l1), 0.0))
        sz, _ = _silu_and_grad(z_ref[...])
        y_ref[...] = (o * sz).astype(BF16)
        _ride_wait(ride, (npair, nq), ride_srcs, ride_dsts, ride_sems)

    blk = pl.BlockSpec((tq, 128), lambda p, i: (i, p))
    extra = ride or _ChipExchange("gather", ())
    return pl.pallas_call(
        body, name=f"fox_fwd_{tag}", grid=(npair, nq),
        in_specs=[pl.BlockSpec((tq, 256), lambda p, i: (i, p)), pl.BlockSpec((T, 256), lambda p, i: (0, p)),
                  pl.BlockSpec((1, T // CHUNK, 128, CHUNK), lambda p, i: (p, 0, 0, 0)),
                  pl.BlockSpec((tq, 128), lambda p, i: (i, COL_CZ // 128 + p))] + extra.in_specs,
        out_specs=[blk, pl.BlockSpec((1, 1, 8, tq), lambda p, i: (p, i, 0, 0)), blk] + extra.out_specs,
        out_shape=[SDS((T, C_WIDTH), F32), SDS((npair, nq, 8, tq), F32), SDS((T, C_WIDTH), BF16)] + extra.out_shape,
        scratch_shapes=[pltpu.VMEM((128, tq), F32), pltpu.VMEM((2, tk, tq), F32), pltpu.VMEM((2, tk, tq), BF16)]
        + (extra.scratch if ride else []),
        compiler_params=pltpu.CompilerParams(dimension_semantics=("arbitrary", "arbitrary"), vmem_limit_bytes=VMEM_LIMIT,
                                             has_side_effects=bool(ride)),
    )(qt, kt, vt, proj, *extra.sources)


def _fox_bwd_prep(proj, dy, o, qt, tag):
    T = proj.shape[0]
    tq = _tile(T, FOX_TILE)
    nq = T // tq

    def body(z0_ref, z1_ref, dy_ref, o_ref, q_ref, do_ref, dl_ref, dz_ref, dot_ref, qt_ref):
        sel = jnp.where((_lane((16, 128)) >> 6) == _row((16, 128)), 1.0, 0.0).astype(BF16)
        for p, z_ref in enumerate((z0_ref, z0_ref, z1_ref, z1_ref)):
            sl = slice(128 * p, 128 * p + 128)
            sz, dsz = _silu_and_grad(z_ref[:, 128 * (p % 2):128 * (p % 2) + 128])
            dyv, ov = dy_ref[:, sl], o_ref[:, sl]
            do = dyv * sz
            do_ref[:, sl] = do.astype(BF16)
            dot_ref[p, 0] = do.T.astype(BF16)
            dz_ref[:, sl] = (dyv * ov * dsz).astype(BF16)
            hi, mid, lo = _split3(do * ov)
            dl_ref[p, 0] = (_dot_nt(sel, hi) + _dot_nt(sel, mid) + _dot_nt(sel, lo))[0:8, :]
        for h in range(C_HEADS):
            qt_ref[h, 0] = q_ref[:, 128 * h:128 * h + 128].astype(F32).T.astype(BF16)

    w = 256
    blk = pl.BlockSpec((tq, C_WIDTH), lambda i: (i, 0))
    return pl.pallas_call(
        body, name=f"fox_bwd_prep_{tag}", grid=(nq,),
        in_specs=[pl.BlockSpec((tq, w), lambda i: (i, COL_CZ // w)), pl.BlockSpec((tq, w), lambda i: (i, COL_CZ // w + 1)),
                  pl.BlockSpec((tq, C_WIDTH), lambda i: (i, (A_WIDTH + B_WIDTH) // C_WIDTH)), blk,
                  pl.BlockSpec((tq, C_HEADS * 128), lambda i: (i, 0))],
        out_specs=[blk, pl.BlockSpec((C_HEADS // 2, 1, 8, tq), lambda i: (0, i, 0, 0)), blk,
                   pl.BlockSpec((C_HEADS // 2, 1, 128, tq), lambda i: (0, i, 0, 0)),
                   pl.BlockSpec((C_HEADS, 1, 128, tq), lambda i: (0, i, 0, 0))],
        out_shape=[SDS((T, C_WIDTH), BF16), SDS((C_HEADS // 2, nq, 8, tq), F32), SDS((T, C_WIDTH), BF16),
                   SDS((C_HEADS // 2, nq, 128, tq), BF16), SDS((C_HEADS, nq, 128, tq), BF16)],
        compiler_params=_params("parallel"),
    )(proj, proj, dy, o, qt)


def _fox_bwd(qt, kt, proj, do, lse, delta, dot, qtr, tag, ride=None):
    T = proj.shape[0]
    tq, tk = _tile(T, FOX_TILE), _tile(T, FOX_KEYS)
    nq, nk = T // tq, T // tk
    assert tq == tk
    npair = C_HEADS // 2

    def body(q_ref, k_ref, v_ref, do_ref, lse_ref, dl_ref, dot_ref, qtr_ref, *rest):
        ride_srcs, (dq_ref, dk_ref, dv_ref), ride_dsts, scratch, ride_sems = _ride_refs(ride, rest, 3, 4)
        dvt_ref, dkt_ref, pt_ref, ds_ref = scratch
        j = pl.program_id(1)
        first = (j * tk) // tq
        _ride_start(ride, (npair, nk), ride_srcs, ride_dsts, ride_sems)

        @pl.when(j == 0)
        def _():
            dq_ref[...] = jnp.zeros_like(dq_ref)

        dkt_ref[...] = jnp.zeros_like(dkt_ref)
        dvt_ref[...] = jnp.zeros_like(dvt_ref)
        ks = (k_ref[:, 0:128], k_ref[:, 128:256])
        kts = tuple(k.astype(F32).T.astype(BF16) for k in ks)
        vb = v_ref[...].astype(BF16)
        lo = _lane((tq, 128)) < 64

        def operands(i):
            q0 = pl.multiple_of(i * tq, tq)
            qb = q_ref[pl.ds(q0, tq), :]
            dob = do_ref[pl.ds(q0, tq), :]
            qhs = (qb[:, 0:128], qb[:, 128:256])
            dohs = (jnp.where(lo, dob, jnp.zeros_like(dob)), jnp.where(lo, jnp.zeros_like(dob), dob))
            return qhs, dohs

        def scores(i):
            qhs, dohs = operands(i)
            return tuple((_dot_nt(ks[h], qhs[h]), _dot_nt(vb, dohs[h])) for h in range(2))

        def grads(i, slot):
            for h in range(2):
                rows = slice(64 * h, 64 * h + 64)
                dvt_ref[rows, :] += _dot_nt(dot_ref[0, i, rows, :], pt_ref[slot, h])
                dkt_ref[h] += _dot_nt(qtr_ref[h, i], ds_ref[slot, h])
                dq_ref[h, i] += _dot(kts[h], ds_ref[slot, h])

        def block(i, slot, diagonal, opening):
            sc = scores(i)
            if not opening:
                grads(i - 1, 1 - slot)
            lsev = lse_ref[0, i]
            dlv = dl_ref[0, i]
            for h in range(2):
                lseh = jnp.broadcast_to(lsev[h:h + 1, :], (FOX_STRIP, tq))
                dlh = jnp.broadcast_to(dlv[h:h + 1, :], (FOX_STRIP, tq))
                for r in range(0, tk, FOX_STRIP):
                    rows = slice(r, r + FOX_STRIP)
                    pt = jnp.exp(sc[h][0][rows, :] - lseh)
                    if diagonal:
                        pt = jnp.where(_fox_mask(FOX_STRIP, tq, r, 0), pt, 0.0)
                    ds_ref[slot, h, rows, :] = (pt * (sc[h][1][rows, :] - dlh)).astype(BF16)
                    pt_ref[slot, h, rows, :] = pt.astype(BF16)

        block(first, 0, True, True)
        rest = nq - 1 - first

        def two_steps(t, carry):
            block(first + 1 + 2 * t, 1, False, False)
            block(first + 2 + 2 * t, 0, False, False)
            return carry

        lax.fori_loop(0, rest // 2, two_steps, 0)
        pl.when(rest % 2 == 1)(lambda: block(nq - 1, 1, False, False))
        grads(nq - 1, rest % 2)
        dv_ref[...] = dvt_ref[...].T.astype(BF16)
        for h in range(2):
            dk_ref[:, 128 * h:128 * h + 128] = dkt_ref[h].T
        _ride_wait(ride, (npair, nk), ride_srcs, ride_dsts, ride_sems)

    full = lambda w: pl.BlockSpec((T, w), lambda p, j: (0, p))
    stat = pl.BlockSpec((1, nq, 8, tq), lambda p, j: (p, 0, 0, 0))
    extra = ride or _ChipExchange("gather", ())
    return pl.pallas_call(
        body, name=f"fox_bwd_{tag}", grid=(npair, nk),
        in_specs=[full(256), pl.BlockSpec((tk, 256), lambda p, j: (j, p)),
                  pl.BlockSpec((tk, 128), lambda p, j: (j, COL_CV // 128 + p)), full(128), stat, stat,
                  pl.BlockSpec((1, nq, 128, tq), lambda p, j: (p, 0, 0, 0)),
                  pl.BlockSpec((2, nq, 128, tq), lambda p, j: (p, 0, 0, 0))] + extra.in_specs,
        out_specs=[pl.BlockSpec((2, nq, 128, tq), lambda p, j: (p, 0, 0, 0)), pl.BlockSpec((tk, 256), lambda p, j: (j, p)),
                   pl.BlockSpec((tk, 128), lambda p, j: (j, p))] + extra.out_specs,
        out_shape=[SDS((C_HEADS, nq, 128, tq), F32), SDS((T, C_HEADS * 128), F32), SDS((T, C_WIDTH), BF16)]
        + extra.out_shape,
        scratch_shapes=[pltpu.VMEM((128, tk), F32), pltpu.VMEM((2, 128, tk), F32),
                        pltpu.VMEM((2, 2, tk, tq), BF16), pltpu.VMEM((2, 2, tk, tq), BF16)]
        + (extra.scratch if ride else []),
        compiler_params=pltpu.CompilerParams(dimension_semantics=("arbitrary", "arbitrary"), vmem_limit_bytes=VMEM_LIMIT,
                                             has_side_effects=bool(ride)),
    )(qt, kt, proj, do, lse, delta, dot, qtr, *extra.sources)


def _fox_bwd_post(dqt, dkt, proj, bf, tag):
    T = proj.shape[0]
    tq = _tile(T, FOX_TILE)
    n = T // tq

    def body(dq_ref, dk_ref, fl_ref, bf_ref, oq_ref, ok_ref, ofl_ref, dbf_ref, carry_ref):
        @pl.when(pl.program_id(0) == 0)
        def _():
            carry_ref[...] = jnp.zeros_like(carry_ref)
            dbf_ref[...] = jnp.zeros_like(dbf_ref)

        lane = _lane((tq, 128))
        lo = lane < 64
        dqs = [dq_ref[h, 0].T for h in range(C_HEADS)]
        dc = jnp.zeros((tq, 128), F32)
        for h in range(C_HEADS):
            dc = dc + jnp.where(lane == h, dqs[h][:, 64:65] - dk_ref[:, 128 * h + 67:128 * h + 68], 0.0)
        utri = jnp.where(_lane((tq, tq)) >= _row((tq, tq)), 1.0, 0.0).astype(BF16)
        dlf = _dot3_left(utri, dc) + carry_ref[...]
        carry_ref[...] = dlf[0:1, :]
        dfl = jnp.where(lane < C_HEADS, dlf * _sigmoid(-(fl_ref[...] + bf_ref[...])), 0.0)
        ofl_ref[...] = dfl.astype(BF16)
        dbf_ref[...] += jnp.sum(dfl, axis=0, keepdims=True)
        for p in range(C_HEADS // 2):
            a, b = 128 * (2 * p), 128 * (2 * p + 1)
            oq_ref[:, 128 * p:128 * p + 128] = (
                jnp.where(lo, dqs[2 * p], pltpu.roll(dqs[2 * p + 1], 64, axis=1)) * Q_SCALE).astype(BF16)
            ok_ref[:, 128 * p:128 * p + 128] = jnp.where(
                lo, dk_ref[:, a:a + 128], pltpu.roll(dk_ref[:, b:b + 128], 64, axis=1)).astype(BF16)

    rev = lambda w: pl.BlockSpec((tq, w), lambda i: (n - 1 - i, 0))
    return pl.pallas_call(
        body, name=f"fox_bwd_post_{tag}", grid=(n,),
        in_specs=[pl.BlockSpec((C_HEADS, 1, 128, tq), lambda i: (0, n - 1 - i, 0, 0)), rev(C_HEADS * 128),
                  pl.BlockSpec((tq, 128), lambda i: (n - 1 - i, COL_CF // 128)), pl.BlockSpec((1, 128), lambda i: (0, 0))],
        out_specs=[rev(C_WIDTH), rev(C_WIDTH), rev(128), pl.BlockSpec((1, 128), lambda i: (0, 0))],
        out_shape=[SDS((T, C_WIDTH), BF16), SDS((T, C_WIDTH), BF16), SDS((T, 128), BF16), SDS((1, 128), F32)],
        scratch_shapes=[pltpu.VMEM((1, 128), F32)], compiler_params=_params("arbitrary"),
    )(dqt, dkt, proj, bf)


def _adamw_math(w, g, m, v):
    m = ADAM_B1 * m + (1.0 - ADAM_B1) * g
    v = ADAM_B2 * v + (1.0 - ADAM_B2) * (g * g)
    delta = -ADAM_LR * ((m / ADAM_C1) / (jnp.sqrt(v / ADAM_C2) + ADAM_EPS) + ADAM_WD * w)
    return delta, m, v


def _adamw_pair(w, m, v, ga, gb, name):
    n0 = w.shape[0]
    most = max(1, ADAMW_BLOCK_BYTES // (4 * math.prod(w.shape[1:])))
    t0 = max(t for t in range(1, min(n0, most) + 1) if n0 % t == 0)

    def body(w_ref, m_ref, v_ref, ga_ref, gb_ref, g_ref, d_ref, nm_ref, nv_ref):
        g = ga_ref[...] + gb_ref[...]
        g_ref[...] = g
        d_ref[...], nm_ref[...], nv_ref[...] = _adamw_math(w_ref[...], g, m_ref[...], v_ref[...])

    blk = pl.BlockSpec((t0,) + w.shape[1:], lambda i: (i, 0, 0))
    return pl.pallas_call(
        body, name=name, grid=(n0 // t0,), in_specs=[blk] * 5, out_specs=[blk] * 4,
        out_shape=[SDS(w.shape, F32)] * 4, compiler_params=_params("parallel"),
    )(w, m, v, ga, gb)


def _adamw_small(ws, ms, vs, gall):
    offs = _small_offsets()
    n = len(ws)

    def body(*refs):
        w_refs, m_refs, v_refs, g_ref = refs[:n], refs[n:2 * n], refs[2 * n:3 * n], refs[3 * n]
        outs = refs[3 * n + 1:]

        def total(off, rows):
            g = g_ref[0, off:off + rows, :]
            for dev in range(1, N_DEV):
                g = g + g_ref[dev, off:off + rows, :]
            return g

        for k in range(n):
            g = total(offs[k], ws[k].shape[0])
            go_ref, d_ref, nm_ref, nv_ref = outs[4 * k:4 * k + 4]
            go_ref[...] = g
            d_ref[...], nm_ref[...], nv_ref[...] = _adamw_math(w_refs[k][...], g, m_refs[k][...], v_refs[k][...])
        outs[4 * n][...] = total(offs[n], 1)

    shapes = [SDS(w.shape, F32) for w in ws for _ in range(4)] + [SDS((1, 128), F32)]
    res = pl.pallas_call(body, name="adamw_small", out_shape=shapes,
                         compiler_params=pltpu.CompilerParams(vmem_limit_bytes=VMEM_LIMIT))(*ws, *ms, *vs, gall)
    return [res[4 * k:4 * k + 4] for k in range(n)], res[4 * n]


def _pack_grads(dlng, dlnb, dwm, dbst, dlb, donorm, dbf, dfinal, loss_part):
    offs = _small_offsets()
    base = offs[1]
    L = len(dwm)
    assert L == 2

    def body(*refs):
        lng, lnb, wm, bst, on, bf = (refs[L * a:L * a + L] for a in range(6))
        lb_ref, fin_ref, loss_ref, o_ref = refs[6 * L:]
        o_ref[...] = jnp.zeros_like(o_ref)
        lane = _lane((1, 128))
        for l in range(L):
            for j in range(2):
                o_ref[offs[1] - base + 2 * l + j:offs[1] - base + 2 * l + j + 1, :] = lng[l][:, 128 * j:128 * j + 128]
                o_ref[offs[2] - base + 2 * l + j:offs[2] - base + 2 * l + j + 1, :] = lnb[l][:, 128 * j:128 * j + 128]
                o_ref[offs[5] - base + 2 * l + j:offs[5] - base + 2 * l + j + 1, :] = lb_ref[l:l + 1, 128 * j:128 * j + 128]
            for g in range(A_GROUPS):
                row = offs[3] - base + (A_GROUPS * l + g) * CHUNK
                o_ref[row:row + CHUNK, :] = wm[l][g]
            o_ref[offs[4] - base + A_GROUPS * l:offs[4] - base + A_GROUPS * (l + 1), :] = bst[l][...].T[0:A_GROUPS, :]
        o_ref[offs[6] - base:offs[6] - base + 1, :] = jnp.where(lane < 64, on[0][...], pltpu.roll(on[1][...], 64, axis=1))
        o_ref[offs[7] - base:offs[7] - base + 1, :] = jnp.where(
            lane < C_HEADS, bf[0][...], jnp.where(lane < 2 * C_HEADS, pltpu.roll(bf[1][...], C_HEADS, axis=1), 0.0))
        for j in range(D_MODEL // 128):
            o_ref[offs[8] - base + j:offs[8] - base + j + 1, :] = fin_ref[:, 128 * j:128 * j + 128]
        o_ref[offs[9] - base:offs[9] - base + 1, :] = loss_ref[...]

    rows = offs[9] + 8 - base
    return pl.pallas_call(body, name="pack_grads", out_shape=SDS((rows, 128), F32))(
        *dlng, *dlnb, *dwm, *dbst, *donorm, *dbf, dlb, dfinal, loss_part)


def _sum_chips(layers, name, layer_major):
    _, R, C = layers[0].shape
    L = len(layers)
    tc = _tile(C, 256)

    def body(*refs):
        o_ref = refs[-1]
        for l, p_ref in enumerate(refs[:-1]):
            p = [p_ref[k].astype(F32) for k in range(N_CHIPS)]
            s = ((p[0] + p[1]) + p[2]) + p[3]
            if layer_major:
                o_ref[l] = s
            else:
                o_ref[:, l, :] = s

    out = (L, R, C) if layer_major else (R, L, C)
    out_blk = (L, R, tc) if layer_major else (R, L, tc)
    return pl.pallas_call(
        body, name=name, grid=(C // tc,),
        in_specs=[pl.BlockSpec((N_CHIPS, R, tc), lambda i: (0, 0, i))] * L,
        out_specs=pl.BlockSpec(out_blk, lambda i: (0, 0, i)), out_shape=SDS(out, F32),
        compiler_params=_params("parallel"),
    )(*layers)


ANY = pl.BlockSpec(memory_space=pl.ANY)


def _mesh_pos():
    return lax.axis_index("x"), lax.axis_index("y"), lax.axis_index("c")


def _other_chips(x, y):
    return [(1 - x, y), (x, 1 - y), (1 - x, 1 - y)]


class _ChipExchange:
    def __init__(self, mode, sources):
        assert mode in ("gather", "scatter")
        self.mode, self.sources = mode, tuple(sources)
        self.n = len(self.sources)
        self.in_specs = [ANY] * self.n
        self.out_specs = [ANY] * self.n
        self.out_shape = [SDS(((N_CHIPS,) + s.shape) if mode == "gather" else s.shape, s.dtype) for s in self.sources]
        self.scratch = [pltpu.SemaphoreType.DMA((3 * self.n,)), pltpu.SemaphoreType.DMA((3 * self.n,)),
                        pltpu.SemaphoreType.DMA((self.n,))]

    def _copies(self, srcs, dsts, send_sems, recv_sems, local_sems):
        x, y, c = _mesh_pos()
        me = 2 * x + y
        view = (lambda r, chip: r) if self.mode == "gather" else (lambda r, chip: r.at[chip])
        local = [pltpu.make_async_copy(view(s, me), d.at[me], local_sems.at[a]) for a, (s, d) in enumerate(zip(srcs, dsts))]
        sends, recvs = [], []
        for j, (px, py) in enumerate(_other_chips(x, y)):
            peer = 2 * px + py
            for a, (s, d) in enumerate(zip(srcs, dsts)):
                sems = dict(send_sem=send_sems.at[self.n * j + a], recv_sem=recv_sems.at[self.n * j + a],
                            device_id=(px, py, c), device_id_type=MESH_ID)
                sends.append(pltpu.make_async_remote_copy(src_ref=view(s, peer), dst_ref=d.at[me], **sems))
                recvs.append(pltpu.make_async_remote_copy(src_ref=view(s, me), dst_ref=d.at[peer], **sems))
        return local, sends, recvs

    def start(self, srcs, dsts, sems):
        local, sends, _ = self._copies(srcs, dsts, *sems)
        for cp in local + sends:
            cp.start()

    def wait(self, srcs, dsts, sems):
        local, sends, recvs = self._copies(srcs, dsts, *sems)
        for cp in recvs:
            cp.wait_recv()
        for cp in sends:
            cp.wait_send()
        for cp in local:
            cp.wait()


def _gather_halves(w, tag):
    R, C = w.shape
    H = C // 2

    def body(w_ref, g_ref, send_sems, recv_sems, pass_send, pass_recv, local_sem):
        x, y, c = _mesh_pos()
        me = 2 * x + y
        mine, theirs = pl.ds(pl.multiple_of(c * H, H), H), pl.ds(pl.multiple_of((1 - c) * H, H), H)
        own = pltpu.make_async_copy(w_ref, g_ref.at[me], local_sem)
        own.start()

        def fetch(j, px, py, src, dst):
            return pltpu.make_async_remote_copy(src_ref=src, dst_ref=dst, send_sem=send_sems.at[j], recv_sem=recv_sems.at[j],
                                                device_id=(px, py, c), device_id_type=MESH_ID)

        def hand(j, cols, peer):
            return pltpu.make_async_remote_copy(src_ref=g_ref.at[peer, :, cols], dst_ref=g_ref.at[peer, :, cols],
                                                send_sem=pass_send.at[j], recv_sem=pass_recv.at[j],
                                                device_id=(x, y, 1 - c), device_id_type=MESH_ID)

        chips = _other_chips(x, y)
        sends = [fetch(j, px, py, w_ref.at[:, mine], g_ref.at[me, :, mine]) for j, (px, py) in enumerate(chips)]
        for cp in sends:
            cp.start()
        passed = []
        for j, (px, py) in enumerate(chips):
            peer = 2 * px + py
            fetch(j, px, py, w_ref.at[:, mine], g_ref.at[peer, :, mine]).wait_recv()
            passed.append(hand(j, mine, peer))
            passed[-1].start()
        for j, (px, py) in enumerate(chips):
            hand(j, theirs, 2 * px + py).wait_recv()
        for cp in sends + passed:
            cp.wait_send()
        own.wait()

    return pl.pallas_call(
        body, name=f"gather_halves_{tag}", in_specs=[ANY], out_specs=ANY, out_shape=SDS((N_CHIPS, R, C), w.dtype),
        scratch_shapes=[pltpu.SemaphoreType.DMA((3,)), pltpu.SemaphoreType.DMA((3,)), pltpu.SemaphoreType.DMA((3,)),
                        pltpu.SemaphoreType.DMA((3,)), pltpu.SemaphoreType.DMA],
        compiler_params=pltpu.CompilerParams(has_side_effects=True),
    )(w)


class _DeviceGather:
    def __init__(self, source):
        self.sources, self.n = (source,), 1
        self.in_specs, self.out_specs = [ANY], [ANY]
        self.out_shape = [SDS((N_DEV,) + source.shape, source.dtype)]
        self.scratch = [pltpu.SemaphoreType.DMA((N_DEV - 1,)), pltpu.SemaphoreType.DMA((N_DEV - 1,)),
                        pltpu.SemaphoreType.DMA((1,))]

    def _copies(self, srcs, dsts, send_sems, recv_sems, local_sems):
        (src,), (dst,) = srcs, dsts
        x, y, c = _mesh_pos()
        me = 4 * x + 2 * y + c
        local = [pltpu.make_async_copy(src, dst.at[me], local_sems.at[0])]
        sends, recvs = [], []
        for k in range(1, N_DEV):
            px, py, pc = (1 - x) if k & 4 else x, (1 - y) if k & 2 else y, (1 - c) if k & 1 else c
            sems = dict(send_sem=send_sems.at[k - 1], recv_sem=recv_sems.at[k - 1], device_id=(px, py, pc),
                        device_id_type=MESH_ID)
            sends.append(pltpu.make_async_remote_copy(src_ref=src, dst_ref=dst.at[me], **sems))
            recvs.append(pltpu.make_async_remote_copy(src_ref=src, dst_ref=dst.at[4 * px + 2 * py + pc], **sems))
        return local, sends, recvs

    start = _ChipExchange.start
    wait = _ChipExchange.wait


class _Rides:
    def __init__(self, *rides):
        self.rides = rides
        self.n = sum(r.n for r in rides)
        self.sources = tuple(s for r in rides for s in r.sources)
        self.in_specs, self.out_specs = [ANY] * self.n, [ANY] * self.n
        self.out_shape = [s for r in rides for s in r.out_shape]
        self.scratch = [s for r in rides for s in r.scratch]

    def _each(self, srcs, dsts, sems):
        a = b = 0
        for r in self.rides:
            yield r, srcs[a:a + r.n], dsts[a:a + r.n], sems[b:b + len(r.scratch)]
            a, b = a + r.n, b + len(r.scratch)

    def start(self, srcs, dsts, sems):
        for r, s, d, m in self._each(srcs, dsts, sems):
            r.start(s, d, m)

    def wait(self, srcs, dsts, sems):
        for r, s, d, m in self._each(srcs, dsts, sems):
            r.wait(s, d, m)


def _gather_devices(a, name):
    ex = _DeviceGather(a)

    def body(a_ref, g_ref, *sems):
        ex.start((a_ref,), (g_ref,), sems)
        ex.wait((a_ref,), (g_ref,), sems)

    return pl.pallas_call(
        body, name=name, in_specs=ex.in_specs, out_specs=ex.out_specs[0], out_shape=ex.out_shape[0],
        scratch_shapes=ex.scratch, compiler_params=pltpu.CompilerParams(has_side_effects=True),
    )(a)


def _swap_cores(pin, pout):
    def body(pin_ref, pout_ref, oin_ref, oout_ref, send_sems, recv_sems):
        x, y, c = _mesh_pos()
        cps = [pltpu.make_async_remote_copy(src_ref=src, dst_ref=dst, send_sem=send_sems.at[a], recv_sem=recv_sems.at[a],
                                            device_id=(x, y, 1 - c), device_id_type=MESH_ID)
               for a, (src, dst) in enumerate(((pin_ref, oin_ref), (pout_ref, oout_ref)))]
        for cp in cps:
            cp.start()
        for cp in cps:
            cp.wait()

    return pl.pallas_call(
        body, name="swap_cores", in_specs=[ANY, ANY], out_specs=[ANY, ANY],
        out_shape=[SDS(pin.shape, F32), SDS(pout.shape, F32)],
        scratch_shapes=[pltpu.SemaphoreType.DMA((2,)), pltpu.SemaphoreType.DMA((2,))],
        compiler_params=pltpu.CompilerParams(has_side_effects=True),
    )(pin, pout)


PACK_TILE = 8 * 128


def _pack_rows(size):
    return (size + PACK_TILE - 1) // PACK_TILE * 8


def _small_offsets():
    offs = [0]
    for _, shape in SMALL_PARAMS:
        offs.append(offs[-1] + _pack_rows(math.prod(shape)))
    return offs


def _rows_view(a):
    flat = a.reshape(-1)
    return jnp.pad(flat, (0, (-flat.size) % 128)).reshape(-1, 128)


def _from_rows(rows, shape):
    return rows.reshape(-1)[:math.prod(shape)].reshape(shape)


def _layer_consts(l, gmlp_ln_g, gmlp_ln_b, gmlp_w_s, gmlp_b_s, hgrn_onorm_g, fox_b_f):
    causal = jnp.tril(jnp.ones((CHUNK, CHUNK), bool))
    wm = jnp.where(causal[None], gmlp_w_s[l], 0.0)
    return dict(
        lng=gmlp_ln_g[l].reshape(1, A_WIDTH), lnb=gmlp_ln_b[l].reshape(1, A_WIDTH),
        wm=wm.astype(BF16), wmt=jnp.swapaxes(wm, 1, 2).astype(BF16),
        bst=jnp.pad(gmlp_b_s[l].T, ((0, 0), (0, 128 - A_GROUPS))),
        onorm=jnp.tile(hgrn_onorm_g[l], 4).reshape(1, B_WIDTH),
        bf=jnp.pad(fox_b_f[l], (0, 128 - C_HEADS)).reshape(1, 128),
    )


def kernel(x, norm_g, w_in, w_out, gmlp_ln_g, gmlp_ln_b, gmlp_w_s, gmlp_b_s, hgrn_lb, hgrn_onorm_g, fox_b_f, final_norm_g, loss_target, m_norm_g, m_w_in, m_w_out, m_gmlp_ln_g, m_gmlp_ln_b, m_gmlp_w_s, m_gmlp_b_s, m_hgrn_lb, m_hgrn_onorm_g, m_fox_b_f, m_final_norm_g, v_norm_g, v_w_in, v_w_out, v_gmlp_ln_g, v_gmlp_ln_b, v_gmlp_w_s, v_gmlp_b_s, v_hgrn_lb, v_hgrn_onorm_g, v_fox_b_f, v_final_norm_g):
    T = x.shape[1]
    shard_in = w_in.shape[2]
    shard_out = w_out.shape[1]
    xs = x.reshape(T, D_MODEL)
    tgt = loss_target.reshape(T, D_MODEL)

    w_in_b = [w_in[l].T.astype(BF16) for l in range(DEPTH)]
    w_out_b = w_out.astype(BF16)

    lb_all = _lb_fwd(hgrn_lb)
    consts = [_layer_consts(l, gmlp_ln_g, gmlp_ln_b, gmlp_w_s, gmlp_b_s, hgrn_onorm_g, fox_b_f) for l in range(DEPTH)]

    saved = []
    xl = xs
    w_in_l = _gather_halves(w_in_b[0], "w_in_l0")
    for l in range(DEPTH):
        cs = consts[l]
        tag = f"l{l}"
        h, proj = _inproj(xl, norm_g[l].reshape(1, D_MODEL), w_in_l, D_IN_PAD, tag)
        (ya,), (yb, ob, s0), (qt, kt, vt) = _run_parts(
            [_gmlp_fwd(proj, cs["lng"], cs["lnb"], cs["wm"], cs["bst"]),
             _hgrn_fwd(proj, lb_all[l].reshape(1, B_WIDTH), cs["onorm"]), _fox_prep(proj, cs["bf"])],
            (T // CHUNK,), f"mix_fwd_{tag}")
        ride = _ChipExchange("gather", (w_out_b[l],) + ((w_in_b[l + 1],) if l + 1 < DEPTH else ()))
        oc, lse, yc, *gathered = _fox_fwd(qt, kt, vt, proj, tag, ride)
        w_out_l = gathered[0].reshape(N_CHIPS * shard_out, D_MODEL)
        saved.append(dict(x=xl, h=h, proj=proj, ya=ya, yb=yb, yc=yc, ob=ob, s0=s0, qt=qt, kt=kt, oc=oc, lse=lse,
                          w_in=w_in_l, w_out=w_out_l))
        xl = _outproj(xl, ya, yb, yc, w_out_l, tag)
        if l + 1 < DEPTH:
            w_in_l = gathered[1]

    dx, loss_part, d_final = _loss_head(xl, final_norm_g.reshape(1, D_MODEL), tgt)

    g_small = {}
    dlb_rows, rin, rout = [None] * DEPTH, [None] * DEPTH, [None] * DEPTH
    slabs_in = None
    for l in reversed(range(DEPTH)):
        cs, sv = consts[l], saved[l]
        tag = f"l{l}"
        proj = sv["proj"]
        dy, dw_out = _outproj_bwd(dx, sv["ya"], sv["yb"], sv["yc"], sv["w_out"], tag)
        (da, dwm, dbst, dlng, dlnb), (db, dlb_rows[l], donorm) = _run_parts(
            [_gmlp_bwd(proj, dy, cs["lng"], cs["lnb"], cs["wm"], cs["wmt"], cs["bst"]),
             _hgrn_bwd(proj, dy, sv["ob"], sv["s0"], lb_all[l].reshape(1, B_WIDTH), cs["onorm"])],
            (T // CHUNK,), f"mix_bwd_{tag}")
        do, delta, dzc, dot, qtr = _fox_bwd_prep(proj, dy, sv["oc"], sv["qt"], tag)
        slabs_out = dw_out.reshape(N_CHIPS, shard_out, D_MODEL).astype(BF16)
        ride = _ChipExchange("scatter", (slabs_out,) + ((slabs_in,) if slabs_in is not None else ()))
        g_small[l] = dict(ln_g=dlng, ln_b=dlnb, w_s=dwm, b_s=dbst, onorm=donorm)
        if l == 0:
            d_hgrn_lb = _lb_bwd(hgrn_lb, jnp.concatenate(dlb_rows, axis=0))
            per_layer = lambda key: [g_small[k][key] for k in range(DEPTH)]
            dbf_known = [jnp.zeros((1, 128), F32)] + [g_small[k]["bf"] for k in range(1, DEPTH)]
            early = _pack_grads(per_layer("ln_g"), per_layer("ln_b"), per_layer("w_s"), per_layer("b_s"), d_hgrn_lb,
                                per_layer("onorm"), dbf_known, d_final, loss_part)
            ride = _Rides(ride, _DeviceGather(early))
        dqt, dkt, dvc, *received = _fox_bwd(sv["qt"], sv["kt"], proj, do, sv["lse"], delta, dot, qtr, tag, ride)
        rout[l] = received[0]
        if slabs_in is not None:
            rin[l + 1] = received[1]
        if l == 0:
            rearly = received[-1]
        dqc, dkc, dflc, g_small[l]["bf"] = _fox_bwd_post(dqt, dkt, proj, cs["bf"], tag)
        dproj = [da, db, dqc, dkc, dvc, dzc, dflc]
        if l == 0:
            ride, parts = None, []
            for n, cols in enumerate(DW_IN_GROUPS):
                part, *arrived = _dw_in(sv["h"], dproj, D_IN_PAD, shard_in, cols, f"{tag}_{n}", ride)
                parts += arrived
                ride = _ChipExchange("scatter", (part,))
        else:
            slabs_in, = _dw_in(sv["h"], dproj, D_IN_PAD, shard_in, (0, D_MODEL), tag)
            ride = None
        dx, dng, *received = _dx_in(sv["x"], norm_g[l].reshape(1, D_MODEL), dx, dproj, sv["w_in"], tag, ride)
        if l == 0:
            rin[0] = jnp.concatenate(parts + received, axis=2)
        g_small[l]["norm_g"] = dng.reshape(D_MODEL // 128, 128)
    grad_x = dx.reshape(x.shape)
    dbf0 = jnp.where(_lane((1, 128)) < C_HEADS, g_small[0]["bf"], 0.0)
    late = jnp.concatenate([g_small[l]["norm_g"] for l in range(DEPTH)] + [jnp.pad(dbf0, ((0, 7), (0, 0)))])
    rlate = _gather_devices(late, "gather_late_grads")
    n_norm = DEPTH * D_MODEL // 128
    small_offs = _small_offsets()
    rearly = rearly.at[:, small_offs[7] - small_offs[1], :].add(rlate[:, n_norm, :])
    rsmall = jnp.concatenate([rlate[:, :n_norm], rearly], axis=1)

    pin, pout = _sum_chips(rin, "sum_chips_w_in", False), _sum_chips(rout, "sum_chips_w_out", True)
    oin, oout = _swap_cores(pin, pout)
    to_view = lambda a: jnp.transpose(a, (2, 0, 1))
    g_w_in, d_w_in, nm_w_in, nv_w_in = [
        jnp.transpose(o, (1, 2, 0))
        for o in _adamw_pair(to_view(w_in), to_view(m_w_in), to_view(v_w_in), pin, oin, "adamw_w_in")]
    g_w_out, d_w_out, nm_w_out, nv_w_out = _adamw_pair(w_out, m_w_out, v_w_out, pout, oout, "adamw_w_out")

    small_w = [norm_g, gmlp_ln_g, gmlp_ln_b, gmlp_w_s, gmlp_b_s, hgrn_lb, hgrn_onorm_g, fox_b_f, final_norm_g]
    small_m = [m_norm_g, m_gmlp_ln_g, m_gmlp_ln_b, m_gmlp_w_s, m_gmlp_b_s, m_hgrn_lb, m_hgrn_onorm_g, m_fox_b_f, m_final_norm_g]
    small_v = [v_norm_g, v_gmlp_ln_g, v_gmlp_ln_b, v_gmlp_w_s, v_gmlp_b_s, v_hgrn_lb, v_hgrn_onorm_g, v_fox_b_f, v_final_norm_g]
    views = lambda ps: [_rows_view(p) for p in ps]
    per_param, loss_row = _adamw_small(views(small_w), views(small_m), views(small_v), rsmall)
    sg, sd, sm, sv_ = [[_from_rows(per_param[k][a], shape) for k, (_, shape) in enumerate(SMALL_PARAMS)] for a in range(4)]
    loss = loss_row[0, 0]

    def order(big_in, big_out, small):
        return [small[0], big_in, big_out] + small[1:]

    return (loss, grad_x, *order(g_w_in, g_w_out, sg), *order(d_w_in, d_w_out, sd), *order(nm_w_in, nm_w_out, sm),
            *order(nv_w_in, nv_w_out, sv_))
```

```python
import collections
import functools
import math

import jax
import jax.numpy as jnp
from jax import lax
from jax.experimental import pallas as pl
from jax.experimental.pallas import tpu as pltpu

F32 = jnp.float32
BF16 = jnp.bfloat16
SDS = jax.ShapeDtypeStruct
MESH_ID = pl.DeviceIdType.MESH

D_MODEL = 1024
DEPTH = 2
A_WIDTH = 256
A_GROUPS = 4
B_WIDTH = 256
C_WIDTH = 512
C_HEADS = 8
D_IN = 3848
D_IN_PAD = 4096
CHUNK = 128
SUB = 16
SUB_SHIFT = 4
NORM_EPS = 1e-6
F_FLOOR = 1e-30
COL_AU, COL_AV, COL_AZ = 0, 256, 512
COL_BQ, COL_BF, COL_BI, COL_BZ = 768, 1024, 1280, 1536
COL_CQ, COL_CK, COL_CV, COL_CZ, COL_CF = 1792, 2304, 2816, 3328, 3840
HEAD_LANES = 128
Q_SCALE = 0.125
ADAM_LR, ADAM_B1, ADAM_B2, ADAM_EPS, ADAM_WD, ADAM_STEP = 0.001, 0.9, 0.999, 1e-08, 0.01, 10
ADAM_C1 = 1.0 - ADAM_B1 ** ADAM_STEP
ADAM_C2 = 1.0 - ADAM_B2 ** ADAM_STEP
VMEM_LIMIT = 56 * 1024 * 1024
ADAMW_BLOCK_BYTES = 1 << 20
N_CHIPS = 4
N_DEV = 8

SMALL_PARAMS = (
    ("norm_g", (DEPTH, D_MODEL)), ("gmlp_ln_g", (DEPTH, 4, 64)), ("gmlp_ln_b", (DEPTH, 4, 64)),
    ("gmlp_w_s", (DEPTH, 4, 128, 128)), ("gmlp_b_s", (DEPTH, 4, 128)), ("hgrn_lb", (DEPTH, 256)),
    ("hgrn_onorm_g", (DEPTH, 64)), ("fox_b_f", (DEPTH, 8)), ("final_norm_g", (D_MODEL,)),
)


def _tile(n, pref):
    t = min(n, pref)
    assert n % t == 0, (n, pref)
    return t


def _params(*sem):
    return pltpu.CompilerParams(dimension_semantics=sem, vmem_limit_bytes=VMEM_LIMIT)


_Part = collections.namedtuple("_Part", "body operands in_specs out_specs out_shape scratch")


def _run_parts(parts, grid, name):
    counts = [(len(p.operands), len(p.out_shape), len(p.scratch)) for p in parts]

    def body(*refs):
        ins, outs, scr = [], [], []
        pos = 0
        for group, k in ((ins, 0), (outs, 1), (scr, 2)):
            for c in counts:
                group.append(refs[pos:pos + c[k]])
                pos += c[k]
        for p, i, o, s in zip(parts, ins, outs, scr):
            p.body(*i, *o, *s)

    flat = lambda key: [x for p in parts for x in getattr(p, key)]
    res = pl.pallas_call(
        body, name=name, grid=grid, in_specs=flat("in_specs"), out_specs=flat("out_specs"), out_shape=flat("out_shape"),
        scratch_shapes=flat("scratch"), compiler_params=_params(*(("arbitrary",) * len(grid))),
    )(*flat("operands"))
    out, pos = [], 0
    for c in counts:
        out.append(list(res[pos:pos + c[1]]))
        pos += c[1]
    return out


def _dot(a, b):
    return jnp.dot(a, b, preferred_element_type=F32)


def _dot_nt(a, b):
    return lax.dot_general(a, b, (((1,), (1,)), ((), ())), preferred_element_type=F32)


def _dot_tn(a, b):
    return lax.dot_general(a, b, (((0,), (0,)), ((), ())), preferred_element_type=F32)


def _split3(x):
    hi = x.astype(BF16)
    r = x - hi.astype(F32)
    mid = r.astype(BF16)
    lo = (r - mid.astype(F32)).astype(BF16)
    return hi, mid, lo


def _dot3_left(c, x):
    hi, mid, lo = _split3(x)
    return _dot(c, hi) + _dot(c, mid) + _dot(c, lo)


def _sigmoid(x):
    return jax.nn.sigmoid(x)


def _silu_and_grad(x):
    s = _sigmoid(x)
    return x * s, s * (1.0 + x * (1.0 - s))


_GELU_C = math.sqrt(2.0 / math.pi)


def _gelu_and_grad(x):
    inner = _GELU_C * (x + 0.044715 * x * x * x)
    t = jnp.tanh(inner)
    y = 0.5 * x * (1.0 + t)
    dy = 0.5 * (1.0 + t) + 0.5 * x * (1.0 - t * t) * _GELU_C * (1.0 + 3.0 * 0.044715 * x * x)
    return y, dy


def _lane(shape):
    return lax.broadcasted_iota(jnp.int32, shape, 1)


def _row(shape):
    return lax.broadcasted_iota(jnp.int32, shape, 0)


def _gsum64(x):
    lo = _lane(x.shape) < 64
    s0 = jnp.sum(jnp.where(lo, x, 0.0), axis=-1, keepdims=True)
    s1 = jnp.sum(jnp.where(lo, 0.0, x), axis=-1, keepdims=True)
    return jnp.where(lo, s0, s1)


def _colreduce(x, op):
    parts = [x[r:r + 8, :] for r in range(0, x.shape[0], 8)]
    while len(parts) > 1:
        pairs = [op(parts[k], parts[k + 1]) for k in range(0, len(parts) - 1, 2)]
        parts = pairs + ([parts[-1]] if len(parts) % 2 else [])
    red = jnp.max if op is jnp.maximum else jnp.sum
    return red(parts[0], axis=0, keepdims=True)


def _block_diag64(dtype=BF16):
    r, c = _row((128, 128)), _lane((128, 128))
    return jnp.where((r >> 6) == (c >> 6), 1.0, 0.0).astype(dtype)


def _assemble_w_in(slab_ref, wt_ref):
    shard = slab_ref.shape[1]
    top = N_CHIPS * shard // 16 * 16
    wt_ref[top:, :] = jnp.zeros((wt_ref.shape[0] - top, wt_ref.shape[1]), wt_ref.dtype)
    for k in range(N_CHIPS):
        wt_ref[shard * k:shard * (k + 1), :] = slab_ref[k]


def _inproj(x, g, w, dp_width, tag):
    T, D = x.shape
    tm = _tile(T, 512)

    def body(x_ref, g_ref, w_ref, h_ref, p_ref, wt_ref):
        pl.when(pl.program_id(0) == 0)(lambda: _assemble_w_in(w_ref, wt_ref))
        xv = x_ref[...]
        r = lax.rsqrt(jnp.mean(xv * xv, axis=-1, keepdims=True) + NORM_EPS)
        h = (xv * r * g_ref[...]).astype(BF16)
        h_ref[...] = h
        p_ref[...] = _dot_nt(h, wt_ref[...])

    return pl.pallas_call(
        body, name=f"inproj_{tag}", grid=(T // tm,),
        in_specs=[pl.BlockSpec((tm, D), lambda i: (i, 0)), pl.BlockSpec((1, D), lambda i: (0, 0)),
                  pl.BlockSpec(w.shape, lambda i: (0, 0, 0))],
        out_specs=[pl.BlockSpec((tm, D), lambda i: (i, 0)), pl.BlockSpec((tm, dp_width), lambda i: (i, 0))],
        out_shape=[SDS((T, D), BF16), SDS((T, dp_width), F32)],
        scratch_shapes=[pltpu.VMEM((dp_width, D), BF16)],
        compiler_params=_params("arbitrary"),
    )(x, g, w)


def _outproj(x, ya, yb, yc, wo, tag):
    T, D = x.shape
    tm = _tile(T, 512)

    def body(x_ref, ya_ref, yb_ref, yc_ref, wo_ref, o_ref):
        acc = x_ref[...] + _dot(ya_ref[...], wo_ref[0:A_WIDTH, :])
        acc = acc + _dot(yb_ref[...], wo_ref[A_WIDTH:A_WIDTH + B_WIDTH, :])
        o_ref[...] = acc + _dot(yc_ref[...], wo_ref[A_WIDTH + B_WIDTH:, :])

    row = lambda w: pl.BlockSpec((tm, w), lambda i: (i, 0))
    return pl.pallas_call(
        body, name=f"outproj_{tag}", grid=(T // tm,),
        in_specs=[row(D), row(A_WIDTH), row(B_WIDTH), row(C_WIDTH), pl.BlockSpec(wo.shape, lambda i: (0, 0))],
        out_specs=row(D), out_shape=SDS((T, D), F32), compiler_params=_params("parallel"),
    )(x, ya, yb, yc, wo)


def _outproj_bwd(dx, ya, yb, yc, wo, tag):
    T, D = dx.shape
    DM = wo.shape[0]
    tm = _tile(T, 512)

    def body(dx_ref, ya_ref, yb_ref, yc_ref, wo_ref, dy_ref, dwo_ref):
        @pl.when(pl.program_id(0) == 0)
        def _():
            dwo_ref[...] = jnp.zeros_like(dwo_ref)

        dxb = dx_ref[...].astype(BF16)
        dy_ref[...] = _dot_nt(dxb, wo_ref[...])
        dwo_ref[0:A_WIDTH, :] += _dot_tn(ya_ref[...], dxb)
        dwo_ref[A_WIDTH:A_WIDTH + B_WIDTH, :] += _dot_tn(yb_ref[...], dxb)
        dwo_ref[A_WIDTH + B_WIDTH:, :] += _dot_tn(yc_ref[...], dxb)

    row = lambda w: pl.BlockSpec((tm, w), lambda i: (i, 0))
    return pl.pallas_call(
        body, name=f"outproj_bwd_{tag}", grid=(T // tm,),
        in_specs=[row(D), row(A_WIDTH), row(B_WIDTH), row(C_WIDTH), pl.BlockSpec(wo.shape, lambda i: (0, 0))],
        out_specs=[row(DM), pl.BlockSpec((DM, D), lambda i: (0, 0))],
        out_shape=[SDS((T, DM), F32), SDS((DM, D), F32)], compiler_params=_params("arbitrary"),
    )(dx, ya, yb, yc, wo)


DW_IN_GROUPS = ((0, 256), (256, 256), (512, 512))


def _piece_offsets(pieces):
    offs = [0]
    for p in pieces:
        offs.append(offs[-1] + p.shape[1])
    return offs


def _dw_in(h, pieces, dp_width, shard, cols, tag, ride=None):
    T = h.shape[0]
    first, D = cols
    assert N_CHIPS * shard <= dp_width and first % D == 0
    tm = _tile(T, 512)
    grid = (T // tm,)
    offs = _piece_offsets(pieces)
    n = len(pieces)

    def body(h_ref, *rest):
        p_refs, rest = rest[:n], rest[n:]
        ride_srcs, (dw_ref,), ride_dsts, (acc_ref,), ride_sems = _ride_refs(ride, rest, 1, 1)
        i = pl.program_id(0)
        _ride_start(ride, grid, ride_srcs, ride_dsts, ride_sems)

        @pl.when(i == 0)
        def _():
            acc_ref[...] = jnp.zeros_like(acc_ref)

        hv = h_ref[...]
        for k, p_ref in enumerate(p_refs):
            acc_ref[offs[k]:offs[k + 1], :] += _dot_tn(p_ref[...], hv)

        @pl.when(i == grid[0] - 1)
        def _():
            for k in range(N_CHIPS):
                dw_ref[k] = acc_ref[shard * k:shard * (k + 1), :].astype(BF16)

        _ride_wait(ride, grid, ride_srcs, ride_dsts, ride_sems)

    extra = ride or _ChipExchange("gather", ())
    return pl.pallas_call(
        body, name=f"dw_in_{tag}", grid=grid,
        in_specs=[pl.BlockSpec((tm, D), lambda i: (i, first // D))]
        + [pl.BlockSpec((tm, p.shape[1]), lambda i: (i, 0)) for p in pieces] + extra.in_specs,
        out_specs=[pl.BlockSpec((N_CHIPS, shard, D), lambda i: (0, 0, 0))] + extra.out_specs,
        out_shape=[SDS((N_CHIPS, shard, D), BF16)] + extra.out_shape,
        scratch_shapes=[pltpu.VMEM((dp_width, D), F32)] + (extra.scratch if ride else []),
        compiler_params=pltpu.CompilerParams(dimension_semantics=("arbitrary",), vmem_limit_bytes=VMEM_LIMIT,
                                             has_side_effects=bool(ride)),
    )(h, *pieces, *extra.sources)


def _dx_in(x, g, dres, pieces, w, tag, ride=None):
    T, D = x.shape
    tm = _tile(T, 512)
    grid = (T // tm,)
    offs = _piece_offsets(pieces)
    n = len(pieces)

    def body(x_ref, g_ref, dres_ref, w_ref, *rest):
        p_refs, rest = rest[:n], rest[n:]
        ride_srcs, (dx_ref, dg_ref), ride_dsts, (wt_ref,), ride_sems = _ride_refs(ride, rest, 2, 1)
        _ride_start(ride, grid, ride_srcs, ride_dsts, ride_sems)

        @pl.when(pl.program_id(0) == 0)
        def _():
            dg_ref[...] = jnp.zeros_like(dg_ref)
            _assemble_w_in(w_ref, wt_ref)

        dh = _dot(p_refs[0][...], wt_ref[offs[0]:offs[1], :])
        for k in range(1, n):
            dh = dh + _dot(p_refs[k][...], wt_ref[offs[k]:offs[k + 1], :])
        xv = x_ref[...]
        r = lax.rsqrt(jnp.mean(xv * xv, axis=-1, keepdims=True) + NORM_EPS)
        xh = xv * r
        dg_ref[...] += jnp.sum(dh * xh, axis=0, keepdims=True)
        dxh = dh * g_ref[...]
        dx_ref[...] = dres_ref[...] + r * (dxh - xh * jnp.mean(dxh * xh, axis=-1, keepdims=True))
        _ride_wait(ride, grid, ride_srcs, ride_dsts, ride_sems)

    extra = ride or _ChipExchange("gather", ())
    row = pl.BlockSpec((tm, D), lambda i: (i, 0))
    return pl.pallas_call(
        body, name=f"dx_in_{tag}", grid=grid,
        in_specs=[row, pl.BlockSpec((1, D), lambda i: (0, 0)), row, pl.BlockSpec(w.shape, lambda i: (0, 0, 0))]
        + [pl.BlockSpec((tm, p.shape[1]), lambda i: (i, 0)) for p in pieces] + extra.in_specs,
        out_specs=[row, pl.BlockSpec((1, D), lambda i: (0, 0))] + extra.out_specs,
        out_shape=[SDS((T, D), F32), SDS((1, D), F32)] + extra.out_shape,
        scratch_shapes=[pltpu.VMEM((offs[-1], D), BF16)] + (extra.scratch if ride else []),
        compiler_params=pltpu.CompilerParams(dimension_semantics=("arbitrary",), vmem_limit_bytes=VMEM_LIMIT,
                                             has_side_effects=bool(ride)),
    )(x, g, dres, w, *pieces, *extra.sources)


def _loss_head(x, g, tgt):
    T, D = x.shape
    tm = _tile(T, 512)

    def body(x_ref, g_ref, t_ref, dx_ref, loss_ref, dg_ref):
        @pl.when(pl.program_id(0) == 0)
        def _():
            loss_ref[...] = jnp.zeros_like(loss_ref)
            dg_ref[...] = jnp.zeros_like(dg_ref)

        xv = x_ref[...]
        r = lax.rsqrt(jnp.mean(xv * xv, axis=-1, keepdims=True) + NORM_EPS)
        xh = xv * r
        gv = g_ref[...]
        err = xh * gv - t_ref[...]
        tok = jnp.mean(err * err, axis=-1, keepdims=True)
        loss_ref[...] += 0.5 * jnp.sum(tok, axis=0, keepdims=True)
        dy = err * (1.0 / D)
        dg_ref[...] += jnp.sum(dy * xh, axis=0, keepdims=True)
        dxh = dy * gv
        dx_ref[...] = r * (dxh - xh * jnp.mean(dxh * xh, axis=-1, keepdims=True))

    row = pl.BlockSpec((tm, D), lambda i: (i, 0))
    return pl.pallas_call(
        body, name="loss_head", grid=(T // tm,),
        in_specs=[row, pl.BlockSpec((1, D), lambda i: (0, 0)), row],
        out_specs=[row, pl.BlockSpec((1, 128), lambda i: (0, 0)), pl.BlockSpec((1, D), lambda i: (0, 0))],
        out_shape=[SDS((T, D), F32), SDS((1, 128), F32), SDS((1, D), F32)], compiler_params=_params("arbitrary"),
    )(x, g, tgt)


def _gmlp_core(u, v, lng, lnb, wm_ref, bst_ref, pair):
    ug, dug = _gelu_and_grad(u)
    vg, dvg = _gelu_and_grad(v)
    mu = _gsum64(vg) * (1.0 / 64)
    d = vg - mu
    var = _gsum64(d * d) * (1.0 / 64)
    rstd = lax.rsqrt(var + NORM_EPS)
    xh = d * rstd
    vn = xh * lng + lnb
    vnb = vn.astype(BF16)
    lo = _lane(u.shape) < 64
    g0, g1 = 2 * pair, 2 * pair + 1
    mixed = jnp.where(lo, _dot(wm_ref[g0], vnb) + bst_ref[:, g0:g0 + 1], _dot(wm_ref[g1], vnb) + bst_ref[:, g1:g1 + 1])
    return ug, dug, dvg, rstd, xh, vnb, mixed, lo


def _gmlp_fwd(proj, lng, lnb, wm, bst):
    T = proj.shape[0]

    def body(u_ref, v_ref, z_ref, lng_ref, lnb_ref, wm_ref, bst_ref, y_ref):
        for pair in range(2):
            sl = slice(128 * pair, 128 * pair + 128)
            ug, _, _, _, _, _, mixed, _ = _gmlp_core(u_ref[:, sl], v_ref[:, sl], lng_ref[:, sl], lnb_ref[:, sl],
                                                     wm_ref, bst_ref, pair)
            sz, _ = _silu_and_grad(z_ref[:, sl])
            y_ref[:, sl] = (ug * mixed * sz).astype(BF16)

    col = lambda c: pl.BlockSpec((CHUNK, A_WIDTH), lambda i, c=c: (i, c // A_WIDTH))
    full = lambda a: pl.BlockSpec(a.shape, lambda i, n=a.ndim: (0,) * n)
    return _Part(body, (proj, proj, proj, lng, lnb, wm, bst),
                 [col(COL_AU), col(COL_AV), col(COL_AZ), full(lng), full(lnb), full(wm), full(bst)],
                 [pl.BlockSpec((CHUNK, A_WIDTH), lambda i: (i, 0))], [SDS((T, A_WIDTH), BF16)], [])


def _gmlp_bwd(proj, dy, lng, lnb, wm, wmt, bst):
    T = proj.shape[0]
    n = T // CHUNK

    def body(u_ref, v_ref, z_ref, dy_ref, lng_ref, lnb_ref, wm_ref, wmt_ref, bst_ref,
             da_ref, dwm_ref, dbst_ref, dlng_ref, dlnb_ref):
        @pl.when(pl.program_id(0) == 0)
        def _():
            dwm_ref[...] = jnp.zeros_like(dwm_ref)
            dbst_ref[...] = jnp.zeros_like(dbst_ref)
            dlng_ref[...] = jnp.zeros_like(dlng_ref)
            dlnb_ref[...] = jnp.zeros_like(dlnb_ref)

        lane = _lane((CHUNK, 128))
        dbst = dbst_ref[...]
        for pair in range(2):
            sl = slice(128 * pair, 128 * pair + 128)
            lng_p = lng_ref[:, sl]
            ug, dug, dvg, rstd, xh, vnb, mixed, lo = _gmlp_core(u_ref[:, sl], v_ref[:, sl], lng_p, lnb_ref[:, sl],
                                                                wm_ref, bst_ref, pair)
            sz, dsz = _silu_and_grad(z_ref[:, sl])
            dyv = dy_ref[:, sl]
            out = ug * mixed
            dz = dyv * out * dsz
            dout = dyv * sz
            du = dout * mixed * dug
            dmix = dout * ug
            g0, g1 = 2 * pair, 2 * pair + 1
            dm0 = jnp.where(lo, dmix, 0.0)
            dm1 = jnp.where(lo, 0.0, dmix)
            dbst = dbst + jnp.where(lane == g0, jnp.sum(dm0, axis=-1, keepdims=True), 0.0)
            dbst = dbst + jnp.where(lane == g1, jnp.sum(dm1, axis=-1, keepdims=True), 0.0)
            dwm_ref[g0] += _dot_nt(dm0.astype(BF16), vnb)
            dwm_ref[g1] += _dot_nt(dm1.astype(BF16), vnb)
            dmb = dmix.astype(BF16)
            dvn = jnp.where(lo, _dot(wmt_ref[g0], dmb), _dot(wmt_ref[g1], dmb))
            dlng_ref[:, sl] += jnp.sum(dvn * xh, axis=0, keepdims=True)
            dlnb_ref[:, sl] += jnp.sum(dvn, axis=0, keepdims=True)
            dxh = dvn * lng_p
            m1 = _gsum64(dxh) * (1.0 / 64)
            m2 = _gsum64(dxh * xh) * (1.0 / 64)
            dv = rstd * (dxh - m1 - xh * m2) * dvg
            da_ref[:, COL_AU + 128 * pair:COL_AU + 128 * pair + 128] = du.astype(BF16)
            da_ref[:, COL_AV + 128 * pair:COL_AV + 128 * pair + 128] = dv.astype(BF16)
            da_ref[:, COL_AZ + 128 * pair:COL_AZ + 128 * pair + 128] = dz.astype(BF16)
        dbst_ref[...] = dbst

        @pl.when(pl.program_id(0) == n - 1)
        def _():
            causal = _lane((CHUNK, CHUNK)) <= _row((CHUNK, CHUNK))
            for g in range(A_GROUPS):
                dwm_ref[g] = jnp.where(causal, dwm_ref[g], 0.0)

    col = lambda c: pl.BlockSpec((CHUNK, A_WIDTH), lambda i, c=c: (i, c // A_WIDTH))
    full = lambda a: pl.BlockSpec(a.shape, lambda i, n=a.ndim: (0,) * n)
    acc = lambda s: pl.BlockSpec(s, lambda i, n=len(s): (0,) * n)
    return _Part(body, (proj, proj, proj, dy, lng, lnb, wm, wmt, bst),
                 [col(COL_AU), col(COL_AV), col(COL_AZ), pl.BlockSpec((CHUNK, A_WIDTH), lambda i: (i, 0)),
                  full(lng), full(lnb), full(wm), full(wmt), full(bst)],
                 [pl.BlockSpec((CHUNK, 3 * A_WIDTH), lambda i: (i, 0)), acc((A_GROUPS, CHUNK, CHUNK)),
                  acc((CHUNK, 128)), acc((1, A_WIDTH)), acc((1, A_WIDTH))],
                 [SDS((T, 3 * A_WIDTH), BF16), SDS((A_GROUPS, CHUNK, CHUNK), F32), SDS((CHUNK, 128), F32),
                  SDS((1, A_WIDTH), F32), SDS((1, A_WIDTH), F32)], [])


def _hgrn_consts():
    r, c = _row((CHUNK, CHUNK)), _lane((CHUNK, CHUNK))
    same = (r >> SUB_SHIFT) == (c >> SUB_SHIFT)
    lsub = jnp.where(same & (c <= r), 1.0, 0.0).astype(BF16)
    usub = jnp.where(same & (c >= r), 1.0, 0.0).astype(BF16)
    bsub = jnp.where(same, 1.0, 0.0).astype(BF16)
    return lsub, usub, bsub


def _hgrn_gates(qv, zf, lbp):
    sq, dsq = _silu_and_grad(qv)
    qt = sq * Q_SCALE
    sg = _sigmoid(zf)
    sgn = _sigmoid(-zf)
    f = lbp + (1.0 - lbp) * sg
    g = jnp.log(jnp.maximum(f, F_FLOOR))
    kf = (1.0 - lbp) * sgn
    return qt, dsq, sg, sgn, f, g, kf


def _hgrn_intra_scores(qt, kf, b, mbd):
    rid = _row((SUB, 128))
    parts = []
    for s in range(SUB):
        e = jnp.exp(b - b[s:s + 1, :])
        parts.append(jnp.where(rid >= s, qt * kf[s:s + 1, :] * e, 0.0))
    return _dot(jnp.concatenate(parts, axis=0).astype(BF16), mbd)


def _hgrn_intra_out(a, v):
    o = jnp.zeros((SUB, 128), F32)
    for s in range(SUB):
        o = o + a[SUB * s:SUB * s + SUB, :] * v[s:s + 1, :]
    return o


def _hgrn_intra_bwd_scores(qt, kf, b, v, do, mbd):
    rid = _row((SUB, 128))
    ps, das, kes, es = [], [], [], []
    for s in range(SUB):
        e = jnp.where(rid >= s, jnp.exp(b - b[s:s + 1, :]), 0.0)
        ke = kf[s:s + 1, :] * e
        es.append(e)
        kes.append(ke)
        ps.append(qt * ke)
        das.append(do * v[s:s + 1, :])
    a = _dot(jnp.concatenate(ps, axis=0).astype(BF16), mbd)
    da = _dot(jnp.concatenate(das, axis=0).astype(BF16), mbd)
    return a, da, kes, es


def _hgrn_intra_bwd_grads(scores, qt, do, rsum):
    a, da, kes, es = scores
    dqt = jnp.zeros((SUB, 128), F32)
    xs, ys = [], []
    for s in range(SUB):
        da_s = da[SUB * s:SUB * s + SUB, :]
        dqt = dqt + da_s * kes[s]
        xs.append(a[SUB * s:SUB * s + SUB, :] * do)
        ys.append(da_s * qt * es[s])
    dv = _dot(rsum, jnp.concatenate(xs, axis=0).astype(BF16))
    dkf = _dot(rsum, jnp.concatenate(ys, axis=0).astype(BF16))
    return dqt, dkf, dv


def _hgrn_norm_gate(o, z, onorm):
    ms = _gsum64(o * o) * (1.0 / 64)
    r = lax.rsqrt(ms + NORM_EPS)
    xh = o * r
    sz, dsz = _silu_and_grad(z)
    return xh, r, sz, dsz, xh * onorm


def _hgrn_fwd(proj, lb, onorm):
    T = proj.shape[0]
    n = T // CHUNK
    nsub = CHUNK // SUB

    def body(q_ref, f_ref, i_ref, z_ref, lb_ref, on_ref, y_ref, o_ref, s0_ref, st_ref):
        @pl.when(pl.program_id(0) == 0)
        def _():
            st_ref[...] = jnp.zeros_like(st_ref)

        lsub, _, bsub = _hgrn_consts()
        mbd = _block_diag64()
        bdmask = mbd > 0
        rid = _row((CHUNK, 128))
        subs = [slice(SUB * sub, SUB * sub + SUB) for sub in range(nsub)]
        work = []
        for pair in range(2):
            sl = slice(128 * pair, 128 * pair + 128)
            qt, _, _, _, _, g, kf = _hgrn_gates(q_ref[:, sl], f_ref[:, sl], lb_ref[:, sl])
            work.append(dict(sl=sl, qt=qt, kf=kf, v=i_ref[:, sl], b=_dot3_left(lsub, g), bl=_dot3_left(bsub, g)))
        for w in work:
            qt, kf, v, b, bl = w["qt"], w["kf"], w["v"], w["b"], w["bl"]
            w["qh"] = (qt * jnp.exp(b)).astype(BF16)
            kh = kf * jnp.exp(bl - b)
            w["dec"] = jnp.exp(bl)
            vtb = v.T.astype(BF16)
            w["scores"] = [_hgrn_intra_scores(qt[rs], kf[rs], b[rs], mbd) for rs in subs]
            w["adds"] = [_dot(vtb, jnp.where((rid >> SUB_SHIFT) == sub, kh, 0.0).astype(BF16)) for sub in range(nsub)]
        for pair, w in enumerate(work):
            w["st"] = st_ref[pair]
            s0_ref[0, pair] = w["st"]
            w["outs"] = []
        for sub, rs in enumerate(subs):
            for w in work:
                w["outs"].append(_dot_nt(w["qh"][rs], w["st"].astype(BF16)) + _hgrn_intra_out(w["scores"][sub], w["v"][rs]))
                w["st"] = jnp.where(bdmask, w["st"] * w["dec"][SUB * sub:SUB * sub + 1, :] + w["adds"][sub], 0.0)
        for pair, w in enumerate(work):
            sl = w["sl"]
            st_ref[pair] = w["st"]
            o = jnp.concatenate(w["outs"], axis=0)
            o_ref[:, sl] = o
            _, _, sz, _, on = _hgrn_norm_gate(o, z_ref[:, sl], on_ref[:, sl])
            y_ref[:, sl] = (on * sz).astype(BF16)

    col = lambda c: pl.BlockSpec((CHUNK, B_WIDTH), lambda i, c=c: (i, c // B_WIDTH))
    full = lambda a: pl.BlockSpec(a.shape, lambda i, n=a.ndim: (0,) * n)
    return _Part(body, (proj, proj, proj, proj, lb, onorm),
                 [col(COL_BQ), col(COL_BF), col(COL_BI), col(COL_BZ), full(lb), full(onorm)],
                 [pl.BlockSpec((CHUNK, B_WIDTH), lambda i: (i, 0)), pl.BlockSpec((CHUNK, B_WIDTH), lambda i: (i, 0)),
                  pl.BlockSpec((1, 2, 128, 128), lambda i: (i, 0, 0, 0))],
                 [SDS((T, B_WIDTH), BF16), SDS((T, B_WIDTH), F32), SDS((n, 2, 128, 128), F32)],
                 [pltpu.VMEM((2, 128, 128), F32)])


def _hgrn_bwd(proj, dy, o_saved, s0, lb, onorm):
    T = proj.shape[0]
    n = T // CHUNK
    nsub = CHUNK // SUB

    def body(q_ref, f_ref, i_ref, z_ref, dy_ref, o_ref, s0_ref, lb_ref, on_ref,
             db_ref, dlb_ref, don_ref, dst_ref, sts_ref):
        @pl.when(pl.program_id(0) == 0)
        def _():
            dst_ref[...] = jnp.zeros_like(dst_ref)
            dlb_ref[...] = jnp.zeros_like(dlb_ref)
            don_ref[...] = jnp.zeros_like(don_ref)

        lsub, usub, bsub = _hgrn_consts()
        mbd = _block_diag64()
        bdmask = mbd > 0
        rsum = jnp.where((_lane((SUB, SUB * SUB)) >> SUB_SHIFT) == _row((SUB, SUB * SUB)), 1.0, 0.0).astype(BF16)
        subs = [slice(SUB * sub, SUB * sub + SUB) for sub in range(nsub)]
        work = []
        for pair in range(2):
            sl = slice(128 * pair, 128 * pair + 128)
            lbp = lb_ref[:, sl]
            qt, dsq, sg, sgn, f, g, kf = _hgrn_gates(q_ref[:, sl], f_ref[:, sl], lbp)
            w = dict(sl=sl, lbp=lbp, qt=qt, dsq=dsq, sg=sg, sgn=sgn, f=f, kf=kf, v=i_ref[:, sl],
                     b=_dot3_left(lsub, g), bl=_dot3_left(bsub, g))
            onp = on_ref[:, sl]
            xh, r, sz, dsz, on = _hgrn_norm_gate(o_ref[:, sl], z_ref[:, sl], onp)
            dyv = dy_ref[:, sl]
            w["dz"] = dyv * on * dsz
            don = dyv * sz
            cn = jnp.sum(don * xh, axis=0, keepdims=True)
            don_ref[...] += cn + pltpu.roll(cn, 64, axis=1)
            dxo = don * onp
            w["do"] = r * (dxo - xh * (_gsum64(dxo * xh) * (1.0 / 64)))
            work.append(w)
        for w in work:
            qt, kf, v, b, bl, do = w["qt"], w["kf"], w["v"], w["b"], w["bl"], w["do"]
            w["eb"] = jnp.exp(b)
            w["ekb"] = jnp.exp(bl - b)
            w["qhb"] = (qt * w["eb"]).astype(BF16)
            w["khb"] = (kf * w["ekb"]).astype(BF16)
            w["dec"] = jnp.exp(bl)
            w["vb"] = v.astype(BF16)
            w["dob"] = do.astype(BF16)
            w["scores"] = [_hgrn_intra_bwd_scores(qt[rs], kf[rs], b[rs], v[rs], do[rs], mbd) for rs in subs]
            w["st_adds"] = [_dot_tn(w["vb"][rs], w["khb"][rs]) for rs in subs]
            w["gst_adds"] = [_dot_tn(w["dob"][rs], w["qhb"][rs]) for rs in subs]
        for pair, w in enumerate(work):
            w["st"] = s0_ref[0, pair]
        for sub in range(nsub):
            for pair, w in enumerate(work):
                sts_ref[pair, sub] = w["st"]
                w["st"] = jnp.where(bdmask, w["st"] * w["dec"][SUB * sub:SUB * sub + 1, :] + w["st_adds"][sub], 0.0)
        for pair, w in enumerate(work):
            w["gst"] = dst_ref[pair]
            w["dqt_p"], w["dkf_p"], w["dv_p"], w["dbl_p"] = ([None] * nsub for _ in range(4))
        for sub in reversed(range(nsub)):
            rs = subs[sub]
            for pair, w in enumerate(work):
                gst = w["gst"]
                st_in = sts_ref[pair, sub]
                gb = gst.astype(BF16)
                dqh = _dot(w["dob"][rs], st_in.astype(BF16))
                dkh = _dot(w["vb"][rs], gb)
                dv_inter = _dot_nt(w["khb"][rs], gb)
                ddec = jnp.sum(gst * st_in, axis=0, keepdims=True)
                dec_row = w["dec"][SUB * sub:SUB * sub + 1, :]
                w["gst"] = jnp.where(bdmask, gst * dec_row + w["gst_adds"][sub], 0.0)
                dqt_i, dkf_i, dv_i = _hgrn_intra_bwd_grads(w["scores"][sub], w["qt"][rs], w["do"][rs], rsum)
                dkf_inter = dkh * w["ekb"][rs]
                w["dqt_p"][sub] = dqh * w["eb"][rs] + dqt_i
                w["dkf_p"][sub] = dkf_inter + dkf_i
                w["dv_p"][sub] = dv_inter + dv_i
                row = jnp.sum(w["kf"][rs] * dkf_inter, axis=0, keepdims=True) + ddec * dec_row
                w["dbl_p"][sub] = jnp.broadcast_to(row, (SUB, 128))
        for pair, w in enumerate(work):
            sl, lbp, sg, sgn, f = w["sl"], w["lbp"], w["sg"], w["sgn"], w["f"]
            dst_ref[pair] = w["gst"]
            dqt = jnp.concatenate(w["dqt_p"], axis=0)
            dkf = jnp.concatenate(w["dkf_p"], axis=0)
            dv = jnp.concatenate(w["dv_p"], axis=0)
            dg = _dot3_left(usub, w["qt"] * dqt - w["kf"] * dkf) + jnp.concatenate(w["dbl_p"], axis=0)
            df = jnp.where(f > F_FLOOR, dg / f, 0.0)
            dlb_ref[:, sl] += jnp.sum(df * (1.0 - sg) - dkf * sgn, axis=0, keepdims=True)
            dfl = (1.0 - lbp) * sg * sgn * (df - dkf)
            dq = dqt * Q_SCALE * w["dsq"]
            db_ref[:, 0 * B_WIDTH + 128 * pair:0 * B_WIDTH + 128 * pair + 128] = dq.astype(BF16)
            db_ref[:, 1 * B_WIDTH + 128 * pair:1 * B_WIDTH + 128 * pair + 128] = dfl.astype(BF16)
            db_ref[:, 2 * B_WIDTH + 128 * pair:2 * B_WIDTH + 128 * pair + 128] = dv.astype(BF16)
            db_ref[:, 3 * B_WIDTH + 128 * pair:3 * B_WIDTH + 128 * pair + 128] = w["dz"].astype(BF16)

    rev = lambda c: pl.BlockSpec((CHUNK, B_WIDTH), lambda i, c=c: (n - 1 - i, c // B_WIDTH))
    full = lambda a: pl.BlockSpec(a.shape, lambda i, n_=a.ndim: (0,) * n_)
    acc = lambda s: pl.BlockSpec(s, lambda i, n_=len(s): (0,) * n_)
    return _Part(body, (proj, proj, proj, proj, dy, o_saved, s0, lb, onorm),
                 [rev(COL_BQ), rev(COL_BF), rev(COL_BI), rev(COL_BZ),
                  pl.BlockSpec((CHUNK, B_WIDTH), lambda i: (n - 1 - i, 1)),
                  pl.BlockSpec((CHUNK, B_WIDTH), lambda i: (n - 1 - i, 0)),
                  pl.BlockSpec((1, 2, 128, 128), lambda i: (n - 1 - i, 0, 0, 0)), full(lb), full(onorm)],
                 [pl.BlockSpec((CHUNK, 4 * B_WIDTH), lambda i: (n - 1 - i, 0)), acc((1, B_WIDTH)), acc((1, 128))],
                 [SDS((T, 4 * B_WIDTH), BF16), SDS((1, B_WIDTH), F32), SDS((1, 128), F32)],
                 [pltpu.VMEM((2, 128, 128), F32), pltpu.VMEM((2, nsub, 128, 128), F32)])


def _lb_fwd(hgrn_lb):
    assert hgrn_lb.shape[0] == 2

    def body(x_ref, o_ref):
        x0, x1 = x_ref[0:1, :], x_ref[1:2, :]
        m = jnp.maximum(x0, x1)
        e0, e1 = jnp.exp(x0 - m), jnp.exp(x1 - m)
        p0, p1 = e0 / (e0 + e1), e1 / (e0 + e1)
        o_ref[0:1, :] = jnp.clip(p0 - p0, 0.0, 1.0 - 1e-6)
        o_ref[1:2, :] = jnp.clip((p0 + p1) - p0, 0.0, 1.0 - 1e-6)

    return pl.pallas_call(body, name="lb_fwd", out_shape=SDS(hgrn_lb.shape, F32))(hgrn_lb)


def _lb_bwd(hgrn_lb, dlb):
    def body(x_ref, d_ref, o_ref):
        x0, x1 = x_ref[0:1, :], x_ref[1:2, :]
        m = jnp.maximum(x0, x1)
        e0, e1 = jnp.exp(x0 - m), jnp.exp(x1 - m)
        p0, p1 = e0 / (e0 + e1), e1 / (e0 + e1)
        val = (p0 + p1) - p0
        dp1 = jnp.where((val > 0.0) & (val < 1.0 - 1e-6), d_ref[1:2, :], 0.0)
        inner = p1 * dp1
        o_ref[0:1, :] = p0 * (0.0 - inner)
        o_ref[1:2, :] = p1 * (dp1 - inner)

    return pl.pallas_call(body, name="lb_bwd", out_shape=SDS(hgrn_lb.shape, F32))(hgrn_lb, dlb)


def _fox_prep(proj, bf):
    T = proj.shape[0]
    n = T // CHUNK

    def body(q0_ref, q1_ref, k0_ref, k1_ref, v0_ref, v1_ref, fl_ref, bf_ref, qo_ref, ko_ref, vt_ref, carry_ref):
        for p, v_ref in enumerate((v0_ref, v0_ref, v1_ref, v1_ref)):
            vt_ref[p, 0] = v_ref[:, 128 * (p % 2):128 * (p % 2) + 128].T.astype(BF16)

        @pl.when(pl.program_id(0) == 0)
        def _():
            carry_ref[...] = jnp.zeros_like(carry_ref)

        ltri = jnp.where(_lane((CHUNK, CHUNK)) <= _row((CHUNK, CHUNK)), 1.0, 0.0).astype(BF16)
        lf = jax.nn.log_sigmoid(fl_ref[...] + bf_ref[...])
        c = _dot3_left(ltri, lf) + carry_ref[...]
        carry_ref[...] = c[CHUNK - 1:CHUNK, :]
        lane = _lane((CHUNK, 128))
        feat = lane < 64
        ones_q = (lane >= 67) & (lane <= 69)
        ones_k = (lane >= 64) & (lane <= 66)
        qrefs, krefs = (q0_ref, q1_ref), (k0_ref, k1_ref)
        for h in range(C_HEADS):
            blk = slice(128 * ((h // 2) % 2), 128 * ((h // 2) % 2) + 128)
            qp, kp = qrefs[h // 4][:, blk], krefs[h // 4][:, blk]
            if h % 2:
                qp, kp = pltpu.roll(qp, 64, axis=1), pltpu.roll(kp, 64, axis=1)
            ch = jnp.broadcast_to(c[:, h:h + 1], (CHUNK, 128))
            hi = ch.astype(BF16).astype(F32)
            r1 = ch - hi
            mid = r1.astype(BF16).astype(F32)
            lo = r1 - mid
            aq = jnp.where(lane == 64, hi, jnp.where(lane == 65, mid, jnp.where(lane == 66, lo,
                           jnp.where(ones_q, 1.0, 0.0))))
            ak = jnp.where(lane == 67, -hi, jnp.where(lane == 68, -mid, jnp.where(lane == 69, -lo,
                           jnp.where(ones_k, 1.0, 0.0))))
            qo_ref[:, 128 * h:128 * h + 128] = jnp.where(feat, qp * Q_SCALE, aq).astype(BF16)
            ko_ref[:, 128 * h:128 * h + 128] = jnp.where(feat, kp, ak).astype(BF16)

    w = 256
    col = lambda c: pl.BlockSpec((CHUNK, w), lambda i, c=c: (i, c // w))
    return _Part(body, (proj, proj, proj, proj, proj, proj, proj, bf),
                 [col(COL_CQ), col(COL_CQ + w), col(COL_CK), col(COL_CK + w), col(COL_CV), col(COL_CV + w),
                  pl.BlockSpec((CHUNK, 128), lambda i: (i, COL_CF // 128)), pl.BlockSpec((1, 128), lambda i: (0, 0))],
                 [pl.BlockSpec((CHUNK, C_HEADS * 128), lambda i: (i, 0))] * 2
                 + [pl.BlockSpec((C_HEADS // 2, 1, 128, CHUNK), lambda i: (0, i, 0, 0))],
                 [SDS((T, C_HEADS * 128), BF16)] * 2 + [SDS((C_HEADS // 2, n, 128, CHUNK), BF16)],
                 [pltpu.VMEM((1, 128), F32)])


FOX_TILE = 512
FOX_KEYS = 512
FOX_STRIP = 16


def _fox_mask(tk, tq, k0, q0):
    return (_row((tk, tq)) + (k0 - q0)) <= _lane((tk, tq))


def _ride_refs(ride, rest, n_out, n_scratch):
    n = ride.n if ride else 0
    srcs, rest = rest[:n], rest[n:]
    outs, rest = rest[:n_out], rest[n_out:]
    dsts, rest = rest[:n], rest[n:]
    return srcs, outs, dsts, rest[:n_scratch], rest[n_scratch:]


def _ride_start(ride, grid, srcs, dsts, sems):
    if ride:
        first = functools.reduce(lambda a, b: a & b, [pl.program_id(d) == 0 for d in range(len(grid))])
        pl.when(first)(lambda: ride.start(srcs, dsts, sems))


def _ride_wait(ride, grid, srcs, dsts, sems):
    if ride:
        last = functools.reduce(lambda a, b: a & b, [pl.program_id(d) == n - 1 for d, n in enumerate(grid)])
        pl.when(last)(lambda: ride.wait(srcs, dsts, sems))


def _fox_fwd(qt, kt, vt, proj, tag, ride=None):
    T = proj.shape[0]
    tq, tk = _tile(T, FOX_TILE), _tile(T, FOX_KEYS)
    nq, nsub = T // tq, tk // CHUNK
    npair = C_HEADS // 2

    def body(q_ref, k_ref, vt_ref, z_ref, *rest):
        ride_srcs, (o_ref, lse_ref, y_ref), ride_dsts, (acc_ref, st_ref, pt_ref), ride_sems = _ride_refs(ride, rest, 3, 3)
        i = pl.program_id(1)
        _ride_start(ride, (npair, nq), ride_srcs, ride_dsts, ride_sems)

        qs = (q_ref[:, 0:128], q_ref[:, 128:256])
        acc_ref[...] = jnp.zeros_like(acc_ref)
        pt_ref[...] = jnp.zeros_like(pt_ref)
        nfull = (i * tq) // tk

        def scores(j):
            kb = k_ref[pl.ds(pl.multiple_of(j * tk, tk), tk), :]
            return tuple(_dot_nt(kb[:, 128 * h:128 * h + 128], qs[h]) for h in range(2))

        def weigh(j, h):
            rows = slice(64 * h, 64 * h + 64)
            vth = jnp.concatenate([vt_ref[0, nsub * j + c, rows, :] for c in range(nsub)], axis=1)
            return _dot(vth, pt_ref[h])

        def block(j, carry, diagonal):
            nxt = () if diagonal else scores(j + 1)
            pvs = [weigh(jnp.maximum(j - 1, 0), h) for h in range(2)]
            new = []
            for h in range(2):
                m, l, alpha_prev = carry[3 * h:3 * h + 3]
                st = st_ref[h]
                if diagonal:
                    st = jnp.where(_fox_mask(tk, tq, j * tk, i * tq), st, -jnp.inf)
                m_new = jnp.maximum(m, _colreduce(st, jnp.maximum))
                pt = jnp.exp(st - m_new)
                alpha = jnp.exp(m - m_new)
                rows = slice(64 * h, 64 * h + 64)
                acc_ref[rows, :] = alpha_prev * acc_ref[rows, :] + pvs[h]
                pt_ref[h] = pt.astype(BF16)
                new += [m_new, alpha * l + _colreduce(pt, jnp.add), alpha]
            for h, st in enumerate(nxt):
                st_ref[h] = st
            return tuple(new)

        for h, st in enumerate(scores(0)):
            st_ref[h] = st
        init = (jnp.full((1, tq), -jnp.inf, F32), jnp.zeros((1, tq), F32), jnp.ones((1, tq), F32)) * 2
        carry = lax.fori_loop(0, nfull, lambda j, c: block(j, c, False), init)
        m0, l0, a0, m1, l1, a1 = block(nfull, carry, True)
        for h, alpha in enumerate((a0, a1)):
            rows = slice(64 * h, 64 * h + 64)
            acc_ref[rows, :] = alpha * acc_ref[rows, :] + weigh(nfull, h)
        inv = jnp.where(_row((128, tq)) < 64, 1.0 / l0, 1.0 / l1)
        o = (acc_ref[...] * inv).T
        o_ref[...] = o
        r8 = _row((8, tq))
        lse_ref[0, 0] = jnp.where(r8 == 0, m0 + jnp.log(l0), jnp.where(r8 == 1, m1 + jnp.log(l1), 0.0))
        sz, _ = _silu_and_grad(z_ref[...])
        y_ref[...] = (o * sz).astype(BF16)
        _ride_wait(ride, (npair, nq), ride_srcs, ride_dsts, ride_sems)

    blk = pl.BlockSpec((tq, 128), lambda p, i: (i, p))
    extra = ride or _ChipExchange("gather", ())
    return pl.pallas_call(
        body, name=f"fox_fwd_{tag}", grid=(npair, nq),
        in_specs=[pl.BlockSpec((tq, 256), lambda p, i: (i, p)), pl.BlockSpec((T, 256), lambda p, i: (0, p)),
                  pl.BlockSpec((1, T // CHUNK, 128, CHUNK), lambda p, i: (p, 0, 0, 0)),
                  pl.BlockSpec((tq, 128), lambda p, i: (i, COL_CZ // 128 + p))] + extra.in_specs,
        out_specs=[blk, pl.BlockSpec((1, 1, 8, tq), lambda p, i: (p, i, 0, 0)), blk] + extra.out_specs,
        out_shape=[SDS((T, C_WIDTH), F32), SDS((npair, nq, 8, tq), F32), SDS((T, C_WIDTH), BF16)] + extra.out_shape,
        scratch_shapes=[pltpu.VMEM((128, tq), F32), pltpu.VMEM((2, tk, tq), F32), pltpu.VMEM((2, tk, tq), BF16)]
        + (extra.scratch if ride else []),
        compiler_params=pltpu.CompilerParams(dimension_semantics=("arbitrary", "arbitrary"), vmem_limit_bytes=VMEM_LIMIT,
                                             has_side_effects=bool(ride)),
    )(qt, kt, vt, proj, *extra.sources)


def _fox_bwd_prep(proj, dy, o, qt, tag):
    T = proj.shape[0]
    tq = _tile(T, FOX_TILE)
    nq = T // tq

    def body(z0_ref, z1_ref, dy_ref, o_ref, q_ref, do_ref, dl_ref, dz_ref, dot_ref, qt_ref):
        sel = jnp.where((_lane((16, 128)) >> 6) == _row((16, 128)), 1.0, 0.0).astype(BF16)
        for p, z_ref in enumerate((z0_ref, z0_ref, z1_ref, z1_ref)):
            sl = slice(128 * p, 128 * p + 128)
            sz, dsz = _silu_and_grad(z_ref[:, 128 * (p % 2):128 * (p % 2) + 128])
            dyv, ov = dy_ref[:, sl], o_ref[:, sl]
            do = dyv * sz
            do_ref[:, sl] = do.astype(BF16)
            dot_ref[p, 0] = do.T.astype(BF16)
            dz_ref[:, sl] = (dyv * ov * dsz).astype(BF16)
            hi, mid, lo = _split3(do * ov)
            dl_ref[p, 0] = (_dot_nt(sel, hi) + _dot_nt(sel, mid) + _dot_nt(sel, lo))[0:8, :]
        for h in range(C_HEADS):
            qt_ref[h, 0] = q_ref[:, 128 * h:128 * h + 128].astype(F32).T.astype(BF16)

    w = 256
    blk = pl.BlockSpec((tq, C_WIDTH), lambda i: (i, 0))
    return pl.pallas_call(
        body, name=f"fox_bwd_prep_{tag}", grid=(nq,),
        in_specs=[pl.BlockSpec((tq, w), lambda i: (i, COL_CZ // w)), pl.BlockSpec((tq, w), lambda i: (i, COL_CZ // w + 1)),
                  pl.BlockSpec((tq, C_WIDTH), lambda i: (i, (A_WIDTH + B_WIDTH) // C_WIDTH)), blk,
                  pl.BlockSpec((tq, C_HEADS * 128), lambda i: (i, 0))],
        out_specs=[blk, pl.BlockSpec((C_HEADS // 2, 1, 8, tq), lambda i: (0, i, 0, 0)), blk,
                   pl.BlockSpec((C_HEADS // 2, 1, 128, tq), lambda i: (0, i, 0, 0)),
                   pl.BlockSpec((C_HEADS, 1, 128, tq), lambda i: (0, i, 0, 0))],
        out_shape=[SDS((T, C_WIDTH), BF16), SDS((C_HEADS // 2, nq, 8, tq), F32), SDS((T, C_WIDTH), BF16),
                   SDS((C_HEADS // 2, nq, 128, tq), BF16), SDS((C_HEADS, nq, 128, tq), BF16)],
        compiler_params=_params("parallel"),
    )(proj, proj, dy, o, qt)


def _fox_bwd(qt, kt, proj, do, lse, delta, dot, qtr, tag, ride=None):
    T = proj.shape[0]
    tq, tk = _tile(T, FOX_TILE), _tile(T, FOX_KEYS)
    nq, nk = T // tq, T // tk
    assert tq == tk
    npair = C_HEADS // 2

    def body(q_ref, k_ref, v_ref, do_ref, lse_ref, dl_ref, dot_ref, qtr_ref, *rest):
        ride_srcs, (dq_ref, dk_ref, dv_ref), ride_dsts, scratch, ride_sems = _ride_refs(ride, rest, 3, 4)
        dvt_ref, dkt_ref, pt_ref, ds_ref = scratch
        j = pl.program_id(1)
        first = (j * tk) // tq
        _ride_start(ride, (npair, nk), ride_srcs, ride_dsts, ride_sems)

        @pl.when(j == 0)
        def _():
            dq_ref[...] = jnp.zeros_like(dq_ref)

        dkt_ref[...] = jnp.zeros_like(dkt_ref)
        dvt_ref[...] = jnp.zeros_like(dvt_ref)
        ks = (k_ref[:, 0:128], k_ref[:, 128:256])
        kts = tuple(k.astype(F32).T.astype(BF16) for k in ks)
        vb = v_ref[...].astype(BF16)
        lo = _lane((tq, 128)) < 64

        def operands(i):
            q0 = pl.multiple_of(i * tq, tq)
            qb = q_ref[pl.ds(q0, tq), :]
            dob = do_ref[pl.ds(q0, tq), :]
            qhs = (qb[:, 0:128], qb[:, 128:256])
            dohs = (jnp.where(lo, dob, jnp.zeros_like(dob)), jnp.where(lo, jnp.zeros_like(dob), dob))
            return qhs, dohs

        def scores(i):
            qhs, dohs = operands(i)
            return tuple((_dot_nt(ks[h], qhs[h]), _dot_nt(vb, dohs[h])) for h in range(2))

        def grads(i, slot):
            for h in range(2):
                rows = slice(64 * h, 64 * h + 64)
                dvt_ref[rows, :] += _dot_nt(dot_ref[0, i, rows, :], pt_ref[slot, h])
                dkt_ref[h] += _dot_nt(qtr_ref[h, i], ds_ref[slot, h])
                dq_ref[h, i] += _dot(kts[h], ds_ref[slot, h])

        def block(i, slot, diagonal, opening):
            sc = scores(i)
            if not opening:
                grads(i - 1, 1 - slot)
            lsev = lse_ref[0, i]
            dlv = dl_ref[0, i]
            for h in range(2):
                lseh = jnp.broadcast_to(lsev[h:h + 1, :], (FOX_STRIP, tq))
                dlh = jnp.broadcast_to(dlv[h:h + 1, :], (FOX_STRIP, tq))
                for r in range(0, tk, FOX_STRIP):
                    rows = slice(r, r + FOX_STRIP)
                    pt = jnp.exp(sc[h][0][rows, :] - lseh)
                    if diagonal:
                        pt = jnp.where(_fox_mask(FOX_STRIP, tq, r, 0), pt, 0.0)
                    ds_ref[slot, h, rows, :] = (pt * (sc[h][1][rows, :] - dlh)).astype(BF16)
                    pt_ref[slot, h, rows, :] = pt.astype(BF16)

        block(first, 0, True, True)
        rest = nq - 1 - first

        def two_steps(t, carry):
            block(first + 1 + 2 * t, 1, False, False)
            block(first + 2 + 2 * t, 0, False, False)
            return carry

        lax.fori_loop(0, rest // 2, two_steps, 0)
        pl.when(rest % 2 == 1)(lambda: block(nq - 1, 1, False, False))
        grads(nq - 1, rest % 2)
        dv_ref[...] = dvt_ref[...].T.astype(BF16)
        for h in range(2):
            dk_ref[:, 128 * h:128 * h + 128] = dkt_ref[h].T
        _ride_wait(ride, (npair, nk), ride_srcs, ride_dsts, ride_sems)

    full = lambda w: pl.BlockSpec((T, w), lambda p, j: (0, p))
    stat = pl.BlockSpec((1, nq, 8, tq), lambda p, j: (p, 0, 0, 0))
    extra = ride or _ChipExchange("gather", ())
    return pl.pallas_call(
        body, name=f"fox_bwd_{tag}", grid=(npair, nk),
        in_specs=[full(256), pl.BlockSpec((tk, 256), lambda p, j: (j, p)),
                  pl.BlockSpec((tk, 128), lambda p, j: (j, COL_CV // 128 + p)), full(128), stat, stat,
                  pl.BlockSpec((1, nq, 128, tq), lambda p, j: (p, 0, 0, 0)),
                  pl.BlockSpec((2, nq, 128, tq), lambda p, j: (p, 0, 0, 0))] + extra.in_specs,
        out_specs=[pl.BlockSpec((2, nq, 128, tq), lambda p, j: (p, 0, 0, 0)), pl.BlockSpec((tk, 256), lambda p, j: (j, p)),
                   pl.BlockSpec((tk, 128), lambda p, j: (j, p))] + extra.out_specs,
        out_shape=[SDS((C_HEADS, nq, 128, tq), F32), SDS((T, C_HEADS * 128), F32), SDS((T, C_WIDTH), BF16)]
        + extra.out_shape,
        scratch_shapes=[pltpu.VMEM((128, tk), F32), pltpu.VMEM((2, 128, tk), F32),
                        pltpu.VMEM((2, 2, tk, tq), BF16), pltpu.VMEM((2, 2, tk, tq), BF16)]
        + (extra.scratch if ride else []),
        compiler_params=pltpu.CompilerParams(dimension_semantics=("arbitrary", "arbitrary"), vmem_limit_bytes=VMEM_LIMIT,
                                             has_side_effects=bool(ride)),
    )(qt, kt, proj, do, lse, delta, dot, qtr, *extra.sources)


def _fox_bwd_post(dqt, dkt, proj, bf, tag):
    T = proj.shape[0]
    tq = _tile(T, FOX_TILE)
    n = T // tq

    def body(dq_ref, dk_ref, fl_ref, bf_ref, oq_ref, ok_ref, ofl_ref, dbf_ref, carry_ref):
        @pl.when(pl.program_id(0) == 0)
        def _():
            carry_ref[...] = jnp.zeros_like(carry_ref)
            dbf_ref[...] = jnp.zeros_like(dbf_ref)

        lane = _lane((tq, 128))
        lo = lane < 64
        dqs = [dq_ref[h, 0].T for h in range(C_HEADS)]
        dc = jnp.zeros((tq, 128), F32)
        for h in range(C_HEADS):
            dc = dc + jnp.where(lane == h, dqs[h][:, 64:65] - dk_ref[:, 128 * h + 67:128 * h + 68], 0.0)
        utri = jnp.where(_lane((tq, tq)) >= _row((tq, tq)), 1.0, 0.0).astype(BF16)
        dlf = _dot3_left(utri, dc) + carry_ref[...]
        carry_ref[...] = dlf[0:1, :]
        dfl = jnp.where(lane < C_HEADS, dlf * _sigmoid(-(fl_ref[...] + bf_ref[...])), 0.0)
        ofl_ref[...] = dfl.astype(BF16)
        dbf_ref[...] += jnp.sum(dfl, axis=0, keepdims=True)
        for p in range(C_HEADS // 2):
            a, b = 128 * (2 * p), 128 * (2 * p + 1)
            oq_ref[:, 128 * p:128 * p + 128] = (
                jnp.where(lo, dqs[2 * p], pltpu.roll(dqs[2 * p + 1], 64, axis=1)) * Q_SCALE).astype(BF16)
            ok_ref[:, 128 * p:128 * p + 128] = jnp.where(
                lo, dk_ref[:, a:a + 128], pltpu.roll(dk_ref[:, b:b + 128], 64, axis=1)).astype(BF16)

    rev = lambda w: pl.BlockSpec((tq, w), lambda i: (n - 1 - i, 0))
    return pl.pallas_call(
        body, name=f"fox_bwd_post_{tag}", grid=(n,),
        in_specs=[pl.BlockSpec((C_HEADS, 1, 128, tq), lambda i: (0, n - 1 - i, 0, 0)), rev(C_HEADS * 128),
                  pl.BlockSpec((tq, 128), lambda i: (n - 1 - i, COL_CF // 128)), pl.BlockSpec((1, 128), lambda i: (0, 0))],
        out_specs=[rev(C_WIDTH), rev(C_WIDTH), rev(128), pl.BlockSpec((1, 128), lambda i: (0, 0))],
        out_shape=[SDS((T, C_WIDTH), BF16), SDS((T, C_WIDTH), BF16), SDS((T, 128), BF16), SDS((1, 128), F32)],
        scratch_shapes=[pltpu.VMEM((1, 128), F32)], compiler_params=_params("arbitrary"),
    )(dqt, dkt, proj, bf)


def _adamw_math(w, g, m, v):
    m = ADAM_B1 * m + (1.0 - ADAM_B1) * g
    v = ADAM_B2 * v + (1.0 - ADAM_B2) * (g * g)
    delta = -ADAM_LR * ((m / ADAM_C1) / (jnp.sqrt(v / ADAM_C2) + ADAM_EPS) + ADAM_WD * w)
    return delta, m, v


def _adamw_pair(w, m, v, ga, gb, name):
    n0 = w.shape[0]
    most = max(1, ADAMW_BLOCK_BYTES // (4 * math.prod(w.shape[1:])))
    t0 = max(t for t in range(1, min(n0, most) + 1) if n0 % t == 0)

    def body(w_ref, m_ref, v_ref, ga_ref, gb_ref, g_ref, d_ref, nm_ref, nv_ref):
        g = ga_ref[...] + gb_ref[...]
        g_ref[...] = g
        d_ref[...], nm_ref[...], nv_ref[...] = _adamw_math(w_ref[...], g, m_ref[...], v_ref[...])

    blk = pl.BlockSpec((t0,) + w.shape[1:], lambda i: (i, 0, 0))
    return pl.pallas_call(
        body, name=name, grid=(n0 // t0,), in_specs=[blk] * 5, out_specs=[blk] * 4,
        out_shape=[SDS(w.shape, F32)] * 4, compiler_params=_params("parallel"),
    )(w, m, v, ga, gb)


def _small_layout():
    L = DEPTH
    lanes = lambda j: slice(128 * j, 128 * j + 128)
    wide = lambda n: [((slice(l, l + 1), lanes(j)), n * l + j, 1, 0, 128) for l in range(L) for j in range(n)]
    halves = [((l, slice(g, g + 1)), 2 * l + g // 2, 1, 64 * (g % 2), 64) for l in range(L) for g in range(A_GROUPS)]
    side_by_side = lambda w: [((slice(l, l + 1),), 0, 1, w * l, w) for l in range(L)]
    return [
        ((L, D_MODEL), wide(D_MODEL // 128)), ((L, A_GROUPS, 64), halves), ((L, A_GROUPS, 64), halves),
        ((L * A_GROUPS * CHUNK, CHUNK), [((slice(None),), 0, L * A_GROUPS * CHUNK, 0, CHUNK)]),
        ((L, A_GROUPS, CHUNK), [((l,), A_GROUPS * l, A_GROUPS, 0, CHUNK) for l in range(L)]),
        ((L, B_WIDTH), wide(B_WIDTH // 128)), ((L, 64), side_by_side(64)), ((L, C_HEADS), side_by_side(C_HEADS)),
        ((1, D_MODEL), [((slice(None), lanes(j)), j, 1, 0, 128) for j in range(D_MODEL // 128)]),
    ]


def _adamw_small(ws, ms, vs, gall):
    offs = _small_offsets()
    layout = _small_layout()
    n = len(ws)
    assert [w.shape for w in ws] == [shape for shape, _ in layout]

    def body(*refs):
        w_refs, m_refs, v_refs, g_ref = refs[:n], refs[n:2 * n], refs[2 * n:3 * n], refs[3 * n]
        outs = refs[3 * n + 1:]

        def total(off, rows):
            g = g_ref[0, off:off + rows, :]
            for dev in range(1, N_DEV):
                g = g + g_ref[dev, off:off + rows, :]
            return g

        for k, (_, pieces) in enumerate(layout):
            g = total(offs[k], offs[k + 1] - offs[k])
            go_ref, d_ref, nm_ref, nv_ref = outs[4 * k:4 * k + 4]
            for idx, row, rows, lane, width in pieces:
                gp = g[row:row + rows, :]
                if lane:
                    gp = pltpu.roll(gp, 128 - lane, axis=1)
                gp = gp[:, :width]
                go_ref[idx] = gp
                d_ref[idx], nm_ref[idx], nv_ref[idx] = _adamw_math(w_refs[k][idx], gp, m_refs[k][idx], v_refs[k][idx])
        outs[4 * n][...] = total(offs[n], 1)

    shapes = [SDS(w.shape, F32) for w in ws for _ in range(4)] + [SDS((1, 128), F32)]
    res = pl.pallas_call(body, name="adamw_small", out_shape=shapes,
                         compiler_params=pltpu.CompilerParams(vmem_limit_bytes=VMEM_LIMIT))(*ws, *ms, *vs, gall)
    return [res[4 * k:4 * k + 4] for k in range(n)], res[4 * n]


def _pack_grads(dlng, dlnb, dwm, dbst, dlb, donorm, dbf, dfinal, loss_part):
    offs = _small_offsets()
    base = offs[1]
    L = len(dwm)
    assert L == 2

    def body(*refs):
        lng, lnb, wm, bst, on, bf = (refs[L * a:L * a + L] for a in range(6))
        lb_ref, fin_ref, loss_ref, o_ref = refs[6 * L:]
        o_ref[...] = jnp.zeros_like(o_ref)
        lane = _lane((1, 128))
        for l in range(L):
            for j in range(2):
                o_ref[offs[1] - base + 2 * l + j:offs[1] - base + 2 * l + j + 1, :] = lng[l][:, 128 * j:128 * j + 128]
                o_ref[offs[2] - base + 2 * l + j:offs[2] - base + 2 * l + j + 1, :] = lnb[l][:, 128 * j:128 * j + 128]
                o_ref[offs[5] - base + 2 * l + j:offs[5] - base + 2 * l + j + 1, :] = lb_ref[l:l + 1, 128 * j:128 * j + 128]
            for g in range(A_GROUPS):
                row = offs[3] - base + (A_GROUPS * l + g) * CHUNK
                o_ref[row:row + CHUNK, :] = wm[l][g]
            o_ref[offs[4] - base + A_GROUPS * l:offs[4] - base + A_GROUPS * (l + 1), :] = bst[l][...].T[0:A_GROUPS, :]
        o_ref[offs[6] - base:offs[6] - base + 1, :] = jnp.where(lane < 64, on[0][...], pltpu.roll(on[1][...], 64, axis=1))
        o_ref[offs[7] - base:offs[7] - base + 1, :] = jnp.where(
            lane < C_HEADS, bf[0][...], jnp.where(lane < 2 * C_HEADS, pltpu.roll(bf[1][...], C_HEADS, axis=1), 0.0))
        for j in range(D_MODEL // 128):
            o_ref[offs[8] - base + j:offs[8] - base + j + 1, :] = fin_ref[:, 128 * j:128 * j + 128]
        o_ref[offs[9] - base:offs[9] - base + 1, :] = loss_ref[...]

    rows = offs[9] + 8 - base
    return pl.pallas_call(body, name="pack_grads", out_shape=SDS((rows, 128), F32))(
        *dlng, *dlnb, *dwm, *dbst, *donorm, *dbf, dlb, dfinal, loss_part)


def _sum_chips(layers, name, layer_major):
    _, R, C = layers[0].shape
    L = len(layers)
    tc = _tile(C, 256)

    def body(*refs):
        o_ref = refs[-1]
        for l, p_ref in enumerate(refs[:-1]):
            p = [p_ref[k].astype(F32) for k in range(N_CHIPS)]
            s = ((p[0] + p[1]) + p[2]) + p[3]
            if layer_major:
                o_ref[l] = s
            else:
                o_ref[:, l, :] = s

    out = (L, R, C) if layer_major else (R, L, C)
    out_blk = (L, R, tc) if layer_major else (R, L, tc)
    return pl.pallas_call(
        body, name=name, grid=(C // tc,),
        in_specs=[pl.BlockSpec((N_CHIPS, R, tc), lambda i: (0, 0, i))] * L,
        out_specs=pl.BlockSpec(out_blk, lambda i: (0, 0, i)), out_shape=SDS(out, F32),
        compiler_params=_params("parallel"),
    )(*layers)


ANY = pl.BlockSpec(memory_space=pl.ANY)


def _mesh_pos():
    return lax.axis_index("x"), lax.axis_index("y"), lax.axis_index("c")


def _other_chips(x, y):
    return [(1 - x, y), (x, 1 - y), (1 - x, 1 - y)]


class _ChipExchange:
    def __init__(self, mode, sources):
        assert mode in ("gather", "scatter")
        self.mode, self.sources = mode, tuple(sources)
        self.n = len(self.sources)
        self.in_specs = [ANY] * self.n
        self.out_specs = [ANY] * self.n
        self.out_shape = [SDS(((N_CHIPS,) + s.shape) if mode == "gather" else s.shape, s.dtype) for s in self.sources]
        self.scratch = [pltpu.SemaphoreType.DMA((3 * self.n,)), pltpu.SemaphoreType.DMA((3 * self.n,)),
                        pltpu.SemaphoreType.DMA((self.n,))]

    def _copies(self, srcs, dsts, send_sems, recv_sems, local_sems):
        x, y, c = _mesh_pos()
        me = 2 * x + y
        view = (lambda r, chip: r) if self.mode == "gather" else (lambda r, chip: r.at[chip])
        local = [pltpu.make_async_copy(view(s, me), d.at[me], local_sems.at[a]) for a, (s, d) in enumerate(zip(srcs, dsts))]
        sends, recvs = [], []
        for j, (px, py) in enumerate(_other_chips(x, y)):
            peer = 2 * px + py
            for a, (s, d) in enumerate(zip(srcs, dsts)):
                sems = dict(send_sem=send_sems.at[self.n * j + a], recv_sem=recv_sems.at[self.n * j + a],
                            device_id=(px, py, c), device_id_type=MESH_ID)
                sends.append(pltpu.make_async_remote_copy(src_ref=view(s, peer), dst_ref=d.at[me], **sems))
                recvs.append(pltpu.make_async_remote_copy(src_ref=view(s, me), dst_ref=d.at[peer], **sems))
        return local, sends, recvs

    def start(self, srcs, dsts, sems):
        local, sends, _ = self._copies(srcs, dsts, *sems)
        for cp in local + sends:
            cp.start()

    def wait(self, srcs, dsts, sems):
        local, sends, recvs = self._copies(srcs, dsts, *sems)
        for cp in recvs:
            cp.wait_recv()
        for cp in sends:
            cp.wait_send()
        for cp in local:
            cp.wait()


def _gather_halves(w, tag):
    R, C = w.shape
    H = C // 2

    def body(w_ref, g_ref, send_sems, recv_sems, pass_send, pass_recv, local_sem):
        x, y, c = _mesh_pos()
        me = 2 * x + y
        mine, theirs = pl.ds(pl.multiple_of(c * H, H), H), pl.ds(pl.multiple_of((1 - c) * H, H), H)
        own = pltpu.make_async_copy(w_ref, g_ref.at[me], local_sem)
        own.start()

        def fetch(j, px, py, src, dst):
            return pltpu.make_async_remote_copy(src_ref=src, dst_ref=dst, send_sem=send_sems.at[j], recv_sem=recv_sems.at[j],
                                                device_id=(px, py, c), device_id_type=MESH_ID)

        def hand(j, cols, peer):
            return pltpu.make_async_remote_copy(src_ref=g_ref.at[peer, :, cols], dst_ref=g_ref.at[peer, :, cols],
                                                send_sem=pass_send.at[j], recv_sem=pass_recv.at[j],
                                                device_id=(x, y, 1 - c), device_id_type=MESH_ID)

        chips = _other_chips(x, y)
        sends = [fetch(j, px, py, w_ref.at[:, mine], g_ref.at[me, :, mine]) for j, (px, py) in enumerate(chips)]
        for cp in sends:
            cp.start()
        passed = []
        for j, (px, py) in enumerate(chips):
            peer = 2 * px + py
            fetch(j, px, py, w_ref.at[:, mine], g_ref.at[peer, :, mine]).wait_recv()
            passed.append(hand(j, mine, peer))
            passed[-1].start()
        for j, (px, py) in enumerate(chips):
            hand(j, theirs, 2 * px + py).wait_recv()
        for cp in sends + passed:
            cp.wait_send()
        own.wait()

    return pl.pallas_call(
        body, name=f"gather_halves_{tag}", in_specs=[ANY], out_specs=ANY, out_shape=SDS((N_CHIPS, R, C), w.dtype),
        scratch_shapes=[pltpu.SemaphoreType.DMA((3,)), pltpu.SemaphoreType.DMA((3,)), pltpu.SemaphoreType.DMA((3,)),
                        pltpu.SemaphoreType.DMA((3,)), pltpu.SemaphoreType.DMA],
        compiler_params=pltpu.CompilerParams(has_side_effects=True),
    )(w)


class _DeviceGather:
    def __init__(self, source):
        self.sources, self.n = (source,), 1
        self.in_specs, self.out_specs = [ANY], [ANY]
        self.out_shape = [SDS((N_DEV,) + source.shape, source.dtype)]
        self.scratch = [pltpu.SemaphoreType.DMA((N_DEV - 1,)), pltpu.SemaphoreType.DMA((N_DEV - 1,)),
                        pltpu.SemaphoreType.DMA((1,))]

    def _copies(self, srcs, dsts, send_sems, recv_sems, local_sems):
        (src,), (dst,) = srcs, dsts
        x, y, c = _mesh_pos()
        me = 4 * x + 2 * y + c
        local = [pltpu.make_async_copy(src, dst.at[me], local_sems.at[0])]
        sends, recvs = [], []
        for k in range(1, N_DEV):
            px, py, pc = (1 - x) if k & 4 else x, (1 - y) if k & 2 else y, (1 - c) if k & 1 else c
            sems = dict(send_sem=send_sems.at[k - 1], recv_sem=recv_sems.at[k - 1], device_id=(px, py, pc),
                        device_id_type=MESH_ID)
            sends.append(pltpu.make_async_remote_copy(src_ref=src, dst_ref=dst.at[me], **sems))
            recvs.append(pltpu.make_async_remote_copy(src_ref=src, dst_ref=dst.at[4 * px + 2 * py + pc], **sems))
        return local, sends, recvs

    start = _ChipExchange.start
    wait = _ChipExchange.wait


class _Rides:
    def __init__(self, *rides):
        self.rides = rides
        self.n = sum(r.n for r in rides)
        self.sources = tuple(s for r in rides for s in r.sources)
        self.in_specs, self.out_specs = [ANY] * self.n, [ANY] * self.n
        self.out_shape = [s for r in rides for s in r.out_shape]
        self.scratch = [s for r in rides for s in r.scratch]

    def _each(self, srcs, dsts, sems):
        a = b = 0
        for r in self.rides:
            yield r, srcs[a:a + r.n], dsts[a:a + r.n], sems[b:b + len(r.scratch)]
            a, b = a + r.n, b + len(r.scratch)

    def start(self, srcs, dsts, sems):
        for r, s, d, m in self._each(srcs, dsts, sems):
            r.start(s, d, m)

    def wait(self, srcs, dsts, sems):
        for r, s, d, m in self._each(srcs, dsts, sems):
            r.wait(s, d, m)


def _gather_devices(a, name):
    ex = _DeviceGather(a)

    def body(a_ref, g_ref, *sems):
        ex.start((a_ref,), (g_ref,), sems)
        ex.wait((a_ref,), (g_ref,), sems)

    return pl.pallas_call(
        body, name=name, in_specs=ex.in_specs, out_specs=ex.out_specs[0], out_shape=ex.out_shape[0],
        scratch_shapes=ex.scratch, compiler_params=pltpu.CompilerParams(has_side_effects=True),
    )(a)


def _swap_cores(pin, pout):
    def body(pin_ref, pout_ref, oin_ref, oout_ref, send_sems, recv_sems):
        x, y, c = _mesh_pos()
        cps = [pltpu.make_async_remote_copy(src_ref=src, dst_ref=dst, send_sem=send_sems.at[a], recv_sem=recv_sems.at[a],
                                            device_id=(x, y, 1 - c), device_id_type=MESH_ID)
               for a, (src, dst) in enumerate(((pin_ref, oin_ref), (pout_ref, oout_ref)))]
        for cp in cps:
            cp.start()
        for cp in cps:
            cp.wait()

    return pl.pallas_call(
        body, name="swap_cores", in_specs=[ANY, ANY], out_specs=[ANY, ANY],
        out_shape=[SDS(pin.shape, F32), SDS(pout.shape, F32)],
        scratch_shapes=[pltpu.SemaphoreType.DMA((2,)), pltpu.SemaphoreType.DMA((2,))],
        compiler_params=pltpu.CompilerParams(has_side_effects=True),
    )(pin, pout)


PACK_TILE = 8 * 128


def _pack_rows(size):
    return (size + PACK_TILE - 1) // PACK_TILE * 8


def _small_offsets():
    offs = [0]
    for _, shape in SMALL_PARAMS:
        offs.append(offs[-1] + _pack_rows(math.prod(shape)))
    return offs


def _layer_consts(l, gmlp_ln_g, gmlp_ln_b, gmlp_w_s, gmlp_b_s, hgrn_onorm_g, fox_b_f):
    causal = jnp.tril(jnp.ones((CHUNK, CHUNK), bool))
    wm = jnp.where(causal[None], gmlp_w_s[l], 0.0)
    return dict(
        lng=gmlp_ln_g[l].reshape(1, A_WIDTH), lnb=gmlp_ln_b[l].reshape(1, A_WIDTH),
        wm=wm.astype(BF16), wmt=jnp.swapaxes(wm, 1, 2).astype(BF16),
        bst=jnp.pad(gmlp_b_s[l].T, ((0, 0), (0, 128 - A_GROUPS))),
        onorm=jnp.tile(hgrn_onorm_g[l], 4).reshape(1, B_WIDTH),
        bf=jnp.pad(fox_b_f[l], (0, 128 - C_HEADS)).reshape(1, 128),
    )


def kernel(x, norm_g, w_in, w_out, gmlp_ln_g, gmlp_ln_b, gmlp_w_s, gmlp_b_s, hgrn_lb, hgrn_onorm_g, fox_b_f, final_norm_g, loss_target, m_norm_g, m_w_in, m_w_out, m_gmlp_ln_g, m_gmlp_ln_b, m_gmlp_w_s, m_gmlp_b_s, m_hgrn_lb, m_hgrn_onorm_g, m_fox_b_f, m_final_norm_g, v_norm_g, v_w_in, v_w_out, v_gmlp_ln_g, v_gmlp_ln_b, v_gmlp_w_s, v_gmlp_b_s, v_hgrn_lb, v_hgrn_onorm_g, v_fox_b_f, v_final_norm_g):
    T = x.shape[1]
    shard_in = w_in.shape[2]
    shard_out = w_out.shape[1]
    xs = x.reshape(T, D_MODEL)
    tgt = loss_target.reshape(T, D_MODEL)

    w_in_b = [w_in[l].T.astype(BF16) for l in range(DEPTH)]
    w_out_b = w_out.astype(BF16)

    lb_all = _lb_fwd(hgrn_lb)
    consts = [_layer_consts(l, gmlp_ln_g, gmlp_ln_b, gmlp_w_s, gmlp_b_s, hgrn_onorm_g, fox_b_f) for l in range(DEPTH)]

    saved = []
    xl = xs
    w_in_l = _gather_halves(w_in_b[0], "w_in_l0")
    for l in range(DEPTH):
        cs = consts[l]
        tag = f"l{l}"
        h, proj = _inproj(xl, norm_g[l].reshape(1, D_MODEL), w_in_l, D_IN_PAD, tag)
        (ya,), (yb, ob, s0), (qt, kt, vt) = _run_parts(
            [_gmlp_fwd(proj, cs["lng"], cs["lnb"], cs["wm"], cs["bst"]),
             _hgrn_fwd(proj, lb_all[l].reshape(1, B_WIDTH), cs["onorm"]), _fox_prep(proj, cs["bf"])],
            (T // CHUNK,), f"mix_fwd_{tag}")
        ride = _ChipExchange("gather", (w_out_b[l],) + ((w_in_b[l + 1],) if l + 1 < DEPTH else ()))
        oc, lse, yc, *gathered = _fox_fwd(qt, kt, vt, proj, tag, ride)
        w_out_l = gathered[0].reshape(N_CHIPS * shard_out, D_MODEL)
        saved.append(dict(x=xl, h=h, proj=proj, ya=ya, yb=yb, yc=yc, ob=ob, s0=s0, qt=qt, kt=kt, oc=oc, lse=lse,
                          w_in=w_in_l, w_out=w_out_l))
        xl = _outproj(xl, ya, yb, yc, w_out_l, tag)
        if l + 1 < DEPTH:
            w_in_l = gathered[1]

    dx, loss_part, d_final = _loss_head(xl, final_norm_g.reshape(1, D_MODEL), tgt)

    g_small = {}
    dlb_rows, rin, rout = [None] * DEPTH, [None] * DEPTH, [None] * DEPTH
    slabs_in = None
    for l in reversed(range(DEPTH)):
        cs, sv = consts[l], saved[l]
        tag = f"l{l}"
        proj = sv["proj"]
        dy, dw_out = _outproj_bwd(dx, sv["ya"], sv["yb"], sv["yc"], sv["w_out"], tag)
        (da, dwm, dbst, dlng, dlnb), (db, dlb_rows[l], donorm) = _run_parts(
            [_gmlp_bwd(proj, dy, cs["lng"], cs["lnb"], cs["wm"], cs["wmt"], cs["bst"]),
             _hgrn_bwd(proj, dy, sv["ob"], sv["s0"], lb_all[l].reshape(1, B_WIDTH), cs["onorm"])],
            (T // CHUNK,), f"mix_bwd_{tag}")
        do, delta, dzc, dot, qtr = _fox_bwd_prep(proj, dy, sv["oc"], sv["qt"], tag)
        slabs_out = dw_out.reshape(N_CHIPS, shard_out, D_MODEL).astype(BF16)
        ride = _ChipExchange("scatter", (slabs_out,) + ((slabs_in,) if slabs_in is not None else ()))
        g_small[l] = dict(ln_g=dlng, ln_b=dlnb, w_s=dwm, b_s=dbst, onorm=donorm)
        if l == 0:
            d_hgrn_lb = _lb_bwd(hgrn_lb, jnp.concatenate(dlb_rows, axis=0))
            per_layer = lambda key: [g_small[k][key] for k in range(DEPTH)]
            dbf_known = [jnp.zeros((1, 128), F32)] + [g_small[k]["bf"] for k in range(1, DEPTH)]
            early = _pack_grads(per_layer("ln_g"), per_layer("ln_b"), per_layer("w_s"), per_layer("b_s"), d_hgrn_lb,
                                per_layer("onorm"), dbf_known, d_final, loss_part)
            ride = _Rides(ride, _DeviceGather(early))
        dqt, dkt, dvc, *received = _fox_bwd(sv["qt"], sv["kt"], proj, do, sv["lse"], delta, dot, qtr, tag, ride)
        rout[l] = received[0]
        if slabs_in is not None:
            rin[l + 1] = received[1]
        if l == 0:
            rearly = received[-1]
        dqc, dkc, dflc, g_small[l]["bf"] = _fox_bwd_post(dqt, dkt, proj, cs["bf"], tag)
        dproj = [da, db, dqc, dkc, dvc, dzc, dflc]
        if l == 0:
            ride, parts = None, []
            for n, cols in enumerate(DW_IN_GROUPS):
                part, *arrived = _dw_in(sv["h"], dproj, D_IN_PAD, shard_in, cols, f"{tag}_{n}", ride)
                parts += arrived
                ride = _ChipExchange("scatter", (part,))
        else:
            slabs_in, = _dw_in(sv["h"], dproj, D_IN_PAD, shard_in, (0, D_MODEL), tag)
            ride = None
        dx, dng, *received = _dx_in(sv["x"], norm_g[l].reshape(1, D_MODEL), dx, dproj, sv["w_in"], tag, ride)
        if l == 0:
            rin[0] = jnp.concatenate(parts + received, axis=2)
        g_small[l]["norm_g"] = dng.reshape(D_MODEL // 128, 128)
    grad_x = dx.reshape(x.shape)
    dbf0 = jnp.where(_lane((1, 128)) < C_HEADS, g_small[0]["bf"], 0.0)
    late = jnp.concatenate([g_small[l]["norm_g"] for l in range(DEPTH)] + [jnp.pad(dbf0, ((0, 7), (0, 0)))])
    rlate = _gather_devices(late, "gather_late_grads")
    n_norm = DEPTH * D_MODEL // 128
    small_offs = _small_offsets()
    rearly = rearly.at[:, small_offs[7] - small_offs[1], :].add(rlate[:, n_norm, :])
    rsmall = jnp.concatenate([rlate[:, :n_norm], rearly], axis=1)

    pin, pout = _sum_chips(rin, "sum_chips_w_in", False), _sum_chips(rout, "sum_chips_w_out", True)
    oin, oout = _swap_cores(pin, pout)
    to_view = lambda a: jnp.transpose(a, (2, 0, 1))
    g_w_in, d_w_in, nm_w_in, nv_w_in = [
        jnp.transpose(o, (1, 2, 0))
        for o in _adamw_pair(to_view(w_in), to_view(m_w_in), to_view(v_w_in), pin, oin, "adamw_w_in")]
    g_w_out, d_w_out, nm_w_out, nv_w_out = _adamw_pair(w_out, m_w_out, v_w_out, pout, oout, "adamw_w_out")

    small_w = [norm_g, gmlp_ln_g, gmlp_ln_b, gmlp_w_s, gmlp_b_s, hgrn_lb, hgrn_onorm_g, fox_b_f, final_norm_g]
    small_m = [m_norm_g, m_gmlp_ln_g, m_gmlp_ln_b, m_gmlp_w_s, m_gmlp_b_s, m_hgrn_lb, m_hgrn_onorm_g, m_fox_b_f, m_final_norm_g]
    small_v = [v_norm_g, v_gmlp_ln_g, v_gmlp_ln_b, v_gmlp_w_s, v_gmlp_b_s, v_hgrn_lb, v_hgrn_onorm_g, v_fox_b_f, v_final_norm_g]
    views = lambda ps: [p.reshape(shape) for p, (shape, _) in zip(ps, _small_layout())]
    per_param, loss_row = _adamw_small(views(small_w), views(small_m), views(small_v), rsmall)
    sg, sd, sm, sv_ = [[per_param[k][a].reshape(shape) for k, (_, shape) in enumerate(SMALL_PARAMS)] for a in range(4)]
    loss = loss_row[0, 0]

    def order(big_in, big_out, small):
        return [small[0], big_in, big_out] + small[1:]

    return (loss, grad_x, *order(g_w_in, g_w_out, sg), *order(d_w_in, d_w_out, sd), *order(nm_w_in, nm_w_out, sm),
            *order(nv_w_in, nv_w_out, sv_))
```

```python
import collections
import functools
import math

import jax
import jax.numpy as jnp
from jax import lax
from jax.experimental import pallas as pl
from jax.experimental.pallas import tpu as pltpu

F32 = jnp.float32
BF16 = jnp.bfloat16
SDS = jax.ShapeDtypeStruct
MESH_ID = pl.DeviceIdType.MESH

D_MODEL = 1024
DEPTH = 2
A_WIDTH = 256
A_GROUPS = 4
B_WIDTH = 256
C_WIDTH = 512
C_HEADS = 8
D_IN = 3848
D_IN_PAD = 4096
CHUNK = 128
SUB = 16
SUB_SHIFT = 4
NORM_EPS = 1e-6
F_FLOOR = 1e-30
COL_AU, COL_AV, COL_AZ = 0, 256, 512
COL_BQ, COL_BF, COL_BI, COL_BZ = 768, 1024, 1280, 1536
COL_CQ, COL_CK, COL_CV, COL_CZ, COL_CF = 1792, 2304, 2816, 3328, 3840
HEAD_LANES = 128
Q_SCALE = 0.125
ADAM_LR, ADAM_B1, ADAM_B2, ADAM_EPS, ADAM_WD, ADAM_STEP = 0.001, 0.9, 0.999, 1e-08, 0.01, 10
ADAM_C1 = 1.0 - ADAM_B1 ** ADAM_STEP
ADAM_C2 = 1.0 - ADAM_B2 ** ADAM_STEP
VMEM_LIMIT = 56 * 1024 * 1024
ADAMW_BLOCK_BYTES = 1 << 20
N_CHIPS = 4
N_DEV = 8

SMALL_PARAMS = (
    ("norm_g", (DEPTH, D_MODEL)), ("gmlp_ln_g", (DEPTH, 4, 64)), ("gmlp_ln_b", (DEPTH, 4, 64)),
    ("gmlp_w_s", (DEPTH, 4, 128, 128)), ("gmlp_b_s", (DEPTH, 4, 128)), ("hgrn_lb", (DEPTH, 256)),
    ("hgrn_onorm_g", (DEPTH, 64)), ("fox_b_f", (DEPTH, 8)), ("final_norm_g", (D_MODEL,)),
)


def _tile(n, pref):
    t = min(n, pref)
    assert n % t == 0, (n, pref)
    return t


def _params(*sem):
    return pltpu.CompilerParams(dimension_semantics=sem, vmem_limit_bytes=VMEM_LIMIT)


_Part = collections.namedtuple("_Part", "body operands in_specs out_specs out_shape scratch")


def _run_parts(parts, grid, name):
    counts = [(len(p.operands), len(p.out_shape), len(p.scratch)) for p in parts]

    def body(*refs):
        ins, outs, scr = [], [], []
        pos = 0
        for group, k in ((ins, 0), (outs, 1), (scr, 2)):
            for c in counts:
                group.append(refs[pos:pos + c[k]])
                pos += c[k]
        for p, i, o, s in zip(parts, ins, outs, scr):
            p.body(*i, *o, *s)

    flat = lambda key: [x for p in parts for x in getattr(p, key)]
    res = pl.pallas_call(
        body, name=name, grid=grid, in_specs=flat("in_specs"), out_specs=flat("out_specs"), out_shape=flat("out_shape"),
        scratch_shapes=flat("scratch"), compiler_params=_params(*(("arbitrary",) * len(grid))),
    )(*flat("operands"))
    out, pos = [], 0
    for c in counts:
        out.append(list(res[pos:pos + c[1]]))
        pos += c[1]
    return out


def _dot(a, b):
    return jnp.dot(a, b, preferred_element_type=F32)


def _dot_nt(a, b):
    return lax.dot_general(a, b, (((1,), (1,)), ((), ())), preferred_element_type=F32)


def _dot_tn(a, b):
    return lax.dot_general(a, b, (((0,), (0,)), ((), ())), preferred_element_type=F32)


def _split3(x):
    hi = x.astype(BF16)
    r = x - hi.astype(F32)
    mid = r.astype(BF16)
    lo = (r - mid.astype(F32)).astype(BF16)
    return hi, mid, lo


def _dot3_left(c, x):
    hi, mid, lo = _split3(x)
    return _dot(c, hi) + _dot(c, mid) + _dot(c, lo)


def _sigmoid(x):
    return jax.nn.sigmoid(x)


def _silu_and_grad(x):
    s = _sigmoid(x)
    return x * s, s * (1.0 + x * (1.0 - s))


_GELU_C = math.sqrt(2.0 / math.pi)


def _gelu_and_grad(x):
    inner = _GELU_C * (x + 0.044715 * x * x * x)
    t = jnp.tanh(inner)
    y = 0.5 * x * (1.0 + t)
    dy = 0.5 * (1.0 + t) + 0.5 * x * (1.0 - t * t) * _GELU_C * (1.0 + 3.0 * 0.044715 * x * x)
    return y, dy


def _lane(shape):
    return lax.broadcasted_iota(jnp.int32, shape, 1)


def _row(shape):
    return lax.broadcasted_iota(jnp.int32, shape, 0)


def _gsum64(x):
    lo = _lane(x.shape) < 64
    s0 = jnp.sum(jnp.where(lo, x, 0.0), axis=-1, keepdims=True)
    s1 = jnp.sum(jnp.where(lo, 0.0, x), axis=-1, keepdims=True)
    return jnp.where(lo, s0, s1)


def _colreduce(x, op):
    parts = [x[r:r + 8, :] for r in range(0, x.shape[0], 8)]
    while len(parts) > 1:
        pairs = [op(parts[k], parts[k + 1]) for k in range(0, len(parts) - 1, 2)]
        parts = pairs + ([parts[-1]] if len(parts) % 2 else [])
    red = jnp.max if op is jnp.maximum else jnp.sum
    return red(parts[0], axis=0, keepdims=True)


def _block_diag64(dtype=BF16):
    r, c = _row((128, 128)), _lane((128, 128))
    return jnp.where((r >> 6) == (c >> 6), 1.0, 0.0).astype(dtype)


def _assemble_w_in(slab_ref, wt_ref):
    shard = slab_ref.shape[1]
    top = N_CHIPS * shard // 16 * 16
    wt_ref[top:, :] = jnp.zeros((wt_ref.shape[0] - top, wt_ref.shape[1]), wt_ref.dtype)
    for k in range(N_CHIPS):
        wt_ref[shard * k:shard * (k + 1), :] = slab_ref[k]


def _inproj(x, g, w, dp_width, tag):
    T, D = x.shape
    tm = _tile(T, 512)

    def body(x_ref, g_ref, w_ref, h_ref, p_ref, wt_ref):
        pl.when(pl.program_id(0) == 0)(lambda: _assemble_w_in(w_ref, wt_ref))
        xv = x_ref[...]
        r = lax.rsqrt(jnp.mean(xv * xv, axis=-1, keepdims=True) + NORM_EPS)
        h = (xv * r * g_ref[...]).astype(BF16)
        h_ref[...] = h
        p_ref[...] = _dot_nt(h, wt_ref[...])

    return pl.pallas_call(
        body, name=f"inproj_{tag}", grid=(T // tm,),
        in_specs=[pl.BlockSpec((tm, D), lambda i: (i, 0)), pl.BlockSpec((1, D), lambda i: (0, 0)),
                  pl.BlockSpec(w.shape, lambda i: (0, 0, 0))],
        out_specs=[pl.BlockSpec((tm, D), lambda i: (i, 0)), pl.BlockSpec((tm, dp_width), lambda i: (i, 0))],
        out_shape=[SDS((T, D), BF16), SDS((T, dp_width), F32)],
        scratch_shapes=[pltpu.VMEM((dp_width, D), BF16)],
        compiler_params=_params("arbitrary"),
    )(x, g, w)


def _outproj(x, ya, yb, yc, wo, tag):
    T, D = x.shape
    tm = _tile(T, 512)

    def body(x_ref, ya_ref, yb_ref, yc_ref, wo_ref, o_ref):
        acc = x_ref[...] + _dot(ya_ref[...], wo_ref[0:A_WIDTH, :])
        acc = acc + _dot(yb_ref[...], wo_ref[A_WIDTH:A_WIDTH + B_WIDTH, :])
        o_ref[...] = acc + _dot(yc_ref[...], wo_ref[A_WIDTH + B_WIDTH:, :])

    row = lambda w: pl.BlockSpec((tm, w), lambda i: (i, 0))
    return pl.pallas_call(
        body, name=f"outproj_{tag}", grid=(T // tm,),
        in_specs=[row(D), row(A_WIDTH), row(B_WIDTH), row(C_WIDTH), pl.BlockSpec(wo.shape, lambda i: (0, 0))],
        out_specs=row(D), out_shape=SDS((T, D), F32), compiler_params=_params("parallel"),
    )(x, ya, yb, yc, wo)


def _outproj_bwd(dx, ya, yb, yc, wo, tag):
    T, D = dx.shape
    DM = wo.shape[0]
    tm = _tile(T, 512)

    def body(dx_ref, ya_ref, yb_ref, yc_ref, wo_ref, dy_ref, dwo_ref):
        @pl.when(pl.program_id(0) == 0)
        def _():
            dwo_ref[...] = jnp.zeros_like(dwo_ref)

        dxb = dx_ref[...].astype(BF16)
        dy_ref[...] = _dot_nt(dxb, wo_ref[...])
        dwo_ref[0:A_WIDTH, :] += _dot_tn(ya_ref[...], dxb)
        dwo_ref[A_WIDTH:A_WIDTH + B_WIDTH, :] += _dot_tn(yb_ref[...], dxb)
        dwo_ref[A_WIDTH + B_WIDTH:, :] += _dot_tn(yc_ref[...], dxb)

    row = lambda w: pl.BlockSpec((tm, w), lambda i: (i, 0))
    return pl.pallas_call(
        body, name=f"outproj_bwd_{tag}", grid=(T // tm,),
        in_specs=[row(D), row(A_WIDTH), row(B_WIDTH), row(C_WIDTH), pl.BlockSpec(wo.shape, lambda i: (0, 0))],
        out_specs=[row(DM), pl.BlockSpec((DM, D), lambda i: (0, 0))],
        out_shape=[SDS((T, DM), F32), SDS((DM, D), F32)], compiler_params=_params("arbitrary"),
    )(dx, ya, yb, yc, wo)


DW_IN_GROUPS = ((0, 256), (256, 256), (512, 512))


def _piece_offsets(pieces):
    offs = [0]
    for p in pieces:
        offs.append(offs[-1] + p.shape[1])
    return offs


def _dw_in(h, pieces, dp_width, shard, cols, tag, ride=None):
    T = h.shape[0]
    first, D = cols
    assert N_CHIPS * shard <= dp_width and first % D == 0
    tm = _tile(T, 512)
    grid = (T // tm,)
    offs = _piece_offsets(pieces)
    n = len(pieces)

    def body(h_ref, *rest):
        p_refs, rest = rest[:n], rest[n:]
        ride_srcs, (dw_ref,), ride_dsts, (acc_ref,), ride_sems = _ride_refs(ride, rest, 1, 1)
        i = pl.program_id(0)
        _ride_start(ride, grid, ride_srcs, ride_dsts, ride_sems)

        @pl.when(i == 0)
        def _():
            acc_ref[...] = jnp.zeros_like(acc_ref)

        hv = h_ref[...]
        for k, p_ref in enumerate(p_refs):
            acc_ref[offs[k]:offs[k + 1], :] += _dot_tn(p_ref[...], hv)

        @pl.when(i == grid[0] - 1)
        def _():
            for k in range(N_CHIPS):
                dw_ref[k] = acc_ref[shard * k:shard * (k + 1), :].astype(BF16)

        _ride_wait(ride, grid, ride_srcs, ride_dsts, ride_sems)

    extra = ride or _ChipExchange("gather", ())
    return pl.pallas_call(
        body, name=f"dw_in_{tag}", grid=grid,
        in_specs=[pl.BlockSpec((tm, D), lambda i: (i, first // D))]
        + [pl.BlockSpec((tm, p.shape[1]), lambda i: (i, 0)) for p in pieces] + extra.in_specs,
        out_specs=[pl.BlockSpec((N_CHIPS, shard, D), lambda i: (0, 0, 0))] + extra.out_specs,
        out_shape=[SDS((N_CHIPS, shard, D), BF16)] + extra.out_shape,
        scratch_shapes=[pltpu.VMEM((dp_width, D), F32)] + (extra.scratch if ride else []),
        compiler_params=pltpu.CompilerParams(dimension_semantics=("arbitrary",), vmem_limit_bytes=VMEM_LIMIT,
                                             has_side_effects=bool(ride)),
    )(h, *pieces, *extra.sources)


def _dx_in(x, g, dres, pieces, w, tag, ride=None):
    T, D = x.shape
    tm = _tile(T, 512)
    grid = (T // tm,)
    offs = _piece_offsets(pieces)
    n = len(pieces)

    def body(x_ref, g_ref, dres_ref, w_ref, *rest):
        p_refs, rest = rest[:n], rest[n:]
        ride_srcs, (dx_ref, dg_ref), ride_dsts, (wt_ref,), ride_sems = _ride_refs(ride, rest, 2, 1)
        _ride_start(ride, grid, ride_srcs, ride_dsts, ride_sems)

        @pl.when(pl.program_id(0) == 0)
        def _():
            dg_ref[...] = jnp.zeros_like(dg_ref)
            _assemble_w_in(w_ref, wt_ref)

        dh = _dot(p_refs[0][...], wt_ref[offs[0]:offs[1], :])
        for k in range(1, n):
            dh = dh + _dot(p_refs[k][...], wt_ref[offs[k]:offs[k + 1], :])
        xv = x_ref[...]
        r = lax.rsqrt(jnp.mean(xv * xv, axis=-1, keepdims=True) + NORM_EPS)
        xh = xv * r
        dg_ref[...] += jnp.sum(dh * xh, axis=0, keepdims=True)
        dxh = dh * g_ref[...]
        dx_ref[...] = dres_ref[...] + r * (dxh - xh * jnp.mean(dxh * xh, axis=-1, keepdims=True))
        _ride_wait(ride, grid, ride_srcs, ride_dsts, ride_sems)

    extra = ride or _ChipExchange("gather", ())
    row = pl.BlockSpec((tm, D), lambda i: (i, 0))
    return pl.pallas_call(
        body, name=f"dx_in_{tag}", grid=grid,
        in_specs=[row, pl.BlockSpec((1, D), lambda i: (0, 0)), row, pl.BlockSpec(w.shape, lambda i: (0, 0, 0))]
        + [pl.BlockSpec((tm, p.shape[1]), lambda i: (i, 0)) for p in pieces] + extra.in_specs,
        out_specs=[row, pl.BlockSpec((1, D), lambda i: (0, 0))] + extra.out_specs,
        out_shape=[SDS((T, D), F32), SDS((1, D), F32)] + extra.out_shape,
        scratch_shapes=[pltpu.VMEM((offs[-1], D), BF16)] + (extra.scratch if ride else []),
        compiler_params=pltpu.CompilerParams(dimension_semantics=("arbitrary",), vmem_limit_bytes=VMEM_LIMIT,
                                             has_side_effects=bool(ride)),
    )(x, g, dres, w, *pieces, *extra.sources)


def _loss_head(x, g, tgt):
    T, D = x.shape
    tm = _tile(T, 512)

    def body(x_ref, g_ref, t_ref, dx_ref, loss_ref, dg_ref):
        @pl.when(pl.program_id(0) == 0)
        def _():
            loss_ref[...] = jnp.zeros_like(loss_ref)
            dg_ref[...] = jnp.zeros_like(dg_ref)

        xv = x_ref[...]
        r = lax.rsqrt(jnp.mean(xv * xv, axis=-1, keepdims=True) + NORM_EPS)
        xh = xv * r
        gv = g_ref[...]
        err = xh * gv - t_ref[...]
        tok = jnp.mean(err * err, axis=-1, keepdims=True)
        loss_ref[...] += 0.5 * jnp.sum(tok, axis=0, keepdims=True)
        dy = err * (1.0 / D)
        dg_ref[...] += jnp.sum(dy * xh, axis=0, keepdims=True)
        dxh = dy * gv
        dx_ref[...] = r * (dxh - xh * jnp.mean(dxh * xh, axis=-1, keepdims=True))

    row = pl.BlockSpec((tm, D), lambda i: (i, 0))
    return pl.pallas_call(
        body, name="loss_head", grid=(T // tm,),
        in_specs=[row, pl.BlockSpec((1, D), lambda i: (0, 0)), row],
        out_specs=[row, pl.BlockSpec((1, 128), lambda i: (0, 0)), pl.BlockSpec((1, D), lambda i: (0, 0))],
        out_shape=[SDS((T, D), F32), SDS((1, 128), F32), SDS((1, D), F32)], compiler_params=_params("arbitrary"),
    )(x, g, tgt)


def _gmlp_core(u, v, lng, lnb, wm_ref, bst_ref, pair):
    ug, dug = _gelu_and_grad(u)
    vg, dvg = _gelu_and_grad(v)
    mu = _gsum64(vg) * (1.0 / 64)
    d = vg - mu
    var = _gsum64(d * d) * (1.0 / 64)
    rstd = lax.rsqrt(var + NORM_EPS)
    xh = d * rstd
    vn = xh * lng + lnb
    vnb = vn.astype(BF16)
    lo = _lane(u.shape) < 64
    g0, g1 = 2 * pair, 2 * pair + 1
    mixed = jnp.where(lo, _dot(wm_ref[g0], vnb) + bst_ref[:, g0:g0 + 1], _dot(wm_ref[g1], vnb) + bst_ref[:, g1:g1 + 1])
    return ug, dug, dvg, rstd, xh, vnb, mixed, lo


def _gmlp_fwd(proj, lng, lnb, wm, bst):
    T = proj.shape[0]

    def body(u_ref, v_ref, z_ref, lng_ref, lnb_ref, wm_ref, bst_ref, y_ref):
        for pair in range(2):
            sl = slice(128 * pair, 128 * pair + 128)
            ug, _, _, _, _, _, mixed, _ = _gmlp_core(u_ref[:, sl], v_ref[:, sl], lng_ref[:, sl], lnb_ref[:, sl],
                                                     wm_ref, bst_ref, pair)
            sz, _ = _silu_and_grad(z_ref[:, sl])
            y_ref[:, sl] = (ug * mixed * sz).astype(BF16)

    col = lambda c: pl.BlockSpec((CHUNK, A_WIDTH), lambda i, c=c: (i, c // A_WIDTH))
    full = lambda a: pl.BlockSpec(a.shape, lambda i, n=a.ndim: (0,) * n)
    return _Part(body, (proj, proj, proj, lng, lnb, wm, bst),
                 [col(COL_AU), col(COL_AV), col(COL_AZ), full(lng), full(lnb), full(wm), full(bst)],
                 [pl.BlockSpec((CHUNK, A_WIDTH), lambda i: (i, 0))], [SDS((T, A_WIDTH), BF16)], [])


def _gmlp_bwd(proj, dy, lng, lnb, wm, wmt, bst):
    T = proj.shape[0]
    n = T // CHUNK

    def body(u_ref, v_ref, z_ref, dy_ref, lng_ref, lnb_ref, wm_ref, wmt_ref, bst_ref,
             da_ref, dwm_ref, dbst_ref, dlng_ref, dlnb_ref):
        @pl.when(pl.program_id(0) == 0)
        def _():
            dwm_ref[...] = jnp.zeros_like(dwm_ref)
            dbst_ref[...] = jnp.zeros_like(dbst_ref)
            dlng_ref[...] = jnp.zeros_like(dlng_ref)
            dlnb_ref[...] = jnp.zeros_like(dlnb_ref)

        lane = _lane((CHUNK, 128))
        dbst = dbst_ref[...]
        for pair in range(2):
            sl = slice(128 * pair, 128 * pair + 128)
            lng_p = lng_ref[:, sl]
            ug, dug, dvg, rstd, xh, vnb, mixed, lo = _gmlp_core(u_ref[:, sl], v_ref[:, sl], lng_p, lnb_ref[:, sl],
                                                                wm_ref, bst_ref, pair)
            sz, dsz = _silu_and_grad(z_ref[:, sl])
            dyv = dy_ref[:, sl]
            out = ug * mixed
            dz = dyv * out * dsz
            dout = dyv * sz
            du = dout * mixed * dug
            dmix = dout * ug
            g0, g1 = 2 * pair, 2 * pair + 1
            dm0 = jnp.where(lo, dmix, 0.0)
            dm1 = jnp.where(lo, 0.0, dmix)
            dbst = dbst + jnp.where(lane == g0, jnp.sum(dm0, axis=-1, keepdims=True), 0.0)
            dbst = dbst + jnp.where(lane == g1, jnp.sum(dm1, axis=-1, keepdims=True), 0.0)
            dwm_ref[g0] += _dot_nt(dm0.astype(BF16), vnb)
            dwm_ref[g1] += _dot_nt(dm1.astype(BF16), vnb)
            dmb = dmix.astype(BF16)
            dvn = jnp.where(lo, _dot(wmt_ref[g0], dmb), _dot(wmt_ref[g1], dmb))
            dlng_ref[:, sl] += jnp.sum(dvn * xh, axis=0, keepdims=True)
            dlnb_ref[:, sl] += jnp.sum(dvn, axis=0, keepdims=True)
            dxh = dvn * lng_p
            m1 = _gsum64(dxh) * (1.0 / 64)
            m2 = _gsum64(dxh * xh) * (1.0 / 64)
            dv = rstd * (dxh - m1 - xh * m2) * dvg
            da_ref[:, COL_AU + 128 * pair:COL_AU + 128 * pair + 128] = du.astype(BF16)
            da_ref[:, COL_AV + 128 * pair:COL_AV + 128 * pair + 128] = dv.astype(BF16)
            da_ref[:, COL_AZ + 128 * pair:COL_AZ + 128 * pair + 128] = dz.astype(BF16)
        dbst_ref[...] = dbst

        @pl.when(pl.program_id(0) == n - 1)
        def _():
            causal = _lane((CHUNK, CHUNK)) <= _row((CHUNK, CHUNK))
            for g in range(A_GROUPS):
                dwm_ref[g] = jnp.where(causal, dwm_ref[g], 0.0)

    col = lambda c: pl.BlockSpec((CHUNK, A_WIDTH), lambda i, c=c: (i, c // A_WIDTH))
    full = lambda a: pl.BlockSpec(a.shape, lambda i, n=a.ndim: (0,) * n)
    acc = lambda s: pl.BlockSpec(s, lambda i, n=len(s): (0,) * n)
    return _Part(body, (proj, proj, proj, dy, lng, lnb, wm, wmt, bst),
                 [col(COL_AU), col(COL_AV), col(COL_AZ), pl.BlockSpec((CHUNK, A_WIDTH), lambda i: (i, 0)),
                  full(lng), full(lnb), full(wm), full(wmt), full(bst)],
                 [pl.BlockSpec((CHUNK, 3 * A_WIDTH), lambda i: (i, 0)), acc((A_GROUPS, CHUNK, CHUNK)),
                  acc((CHUNK, 128)), acc((1, A_WIDTH)), acc((1, A_WIDTH))],
                 [SDS((T, 3 * A_WIDTH), BF16), SDS((A_GROUPS, CHUNK, CHUNK), F32), SDS((CHUNK, 128), F32),
                  SDS((1, A_WIDTH), F32), SDS((1, A_WIDTH), F32)], [])


def _hgrn_consts():
    r, c = _row((CHUNK, CHUNK)), _lane((CHUNK, CHUNK))
    same = (r >> SUB_SHIFT) == (c >> SUB_SHIFT)
    lsub = jnp.where(same & (c <= r), 1.0, 0.0).astype(BF16)
    usub = jnp.where(same & (c >= r), 1.0, 0.0).astype(BF16)
    bsub = jnp.where(same, 1.0, 0.0).astype(BF16)
    return lsub, usub, bsub


def _hgrn_gates(qv, zf, lbp):
    sq, dsq = _silu_and_grad(qv)
    qt = sq * Q_SCALE
    sg = _sigmoid(zf)
    sgn = _sigmoid(-zf)
    f = lbp + (1.0 - lbp) * sg
    g = jnp.log(jnp.maximum(f, F_FLOOR))
    kf = (1.0 - lbp) * sgn
    return qt, dsq, sg, sgn, f, g, kf


def _hgrn_intra_scores(qt, kf, b, mbd):
    rid = _row((SUB, 128))
    parts = []
    for s in range(SUB):
        e = jnp.exp(b - b[s:s + 1, :])
        parts.append(jnp.where(rid >= s, qt * kf[s:s + 1, :] * e, 0.0))
    return _dot(jnp.concatenate(parts, axis=0).astype(BF16), mbd)


def _hgrn_intra_out(a, v):
    o = jnp.zeros((SUB, 128), F32)
    for s in range(SUB):
        o = o + a[SUB * s:SUB * s + SUB, :] * v[s:s + 1, :]
    return o


def _hgrn_intra_bwd_scores(qt, kf, b, v, do, mbd):
    rid = _row((SUB, 128))
    ps, das, kes, es = [], [], [], []
    for s in range(SUB):
        e = jnp.where(rid >= s, jnp.exp(b - b[s:s + 1, :]), 0.0)
        ke = kf[s:s + 1, :] * e
        es.append(e)
        kes.append(ke)
        ps.append(qt * ke)
        das.append(do * v[s:s + 1, :])
    a = _dot(jnp.concatenate(ps, axis=0).astype(BF16), mbd)
    da = _dot(jnp.concatenate(das, axis=0).astype(BF16), mbd)
    return a, da, kes, es


def _hgrn_intra_bwd_grads(scores, qt, do, rsum):
    a, da, kes, es = scores
    dqt = jnp.zeros((SUB, 128), F32)
    xs, ys = [], []
    for s in range(SUB):
        da_s = da[SUB * s:SUB * s + SUB, :]
        dqt = dqt + da_s * kes[s]
        xs.append(a[SUB * s:SUB * s + SUB, :] * do)
        ys.append(da_s * qt * es[s])
    dv = _dot(rsum, jnp.concatenate(xs, axis=0).astype(BF16))
    dkf = _dot(rsum, jnp.concatenate(ys, axis=0).astype(BF16))
    return dqt, dkf, dv


def _hgrn_norm_gate(o, z, onorm):
    ms = _gsum64(o * o) * (1.0 / 64)
    r = lax.rsqrt(ms + NORM_EPS)
    xh = o * r
    sz, dsz = _silu_and_grad(z)
    return xh, r, sz, dsz, xh * onorm


def _hgrn_fwd(proj, lb, onorm):
    T = proj.shape[0]
    n = T // CHUNK
    nsub = CHUNK // SUB

    def body(q_ref, f_ref, i_ref, z_ref, lb_ref, on_ref, y_ref, o_ref, s0_ref, st_ref):
        @pl.when(pl.program_id(0) == 0)
        def _():
            st_ref[...] = jnp.zeros_like(st_ref)

        lsub, _, bsub = _hgrn_consts()
        mbd = _block_diag64()
        bdmask = mbd > 0
        rid = _row((CHUNK, 128))
        subs = [slice(SUB * sub, SUB * sub + SUB) for sub in range(nsub)]
        work = []
        for pair in range(2):
            sl = slice(128 * pair, 128 * pair + 128)
            qt, _, _, _, _, g, kf = _hgrn_gates(q_ref[:, sl], f_ref[:, sl], lb_ref[:, sl])
            work.append(dict(sl=sl, qt=qt, kf=kf, v=i_ref[:, sl], b=_dot3_left(lsub, g), bl=_dot3_left(bsub, g)))
        for w in work:
            qt, kf, v, b, bl = w["qt"], w["kf"], w["v"], w["b"], w["bl"]
            w["qh"] = (qt * jnp.exp(b)).astype(BF16)
            kh = kf * jnp.exp(bl - b)
            w["dec"] = jnp.exp(bl)
            vtb = v.T.astype(BF16)
            w["scores"] = [_hgrn_intra_scores(qt[rs], kf[rs], b[rs], mbd) for rs in subs]
            w["adds"] = [_dot(vtb, jnp.where((rid >> SUB_SHIFT) == sub, kh, 0.0).astype(BF16)) for sub in range(nsub)]
        for pair, w in enumerate(work):
            w["st"] = st_ref[pair]
            s0_ref[0, pair] = w["st"]
            w["outs"] = []
        for sub, rs in enumerate(subs):
            for w in work:
                w["outs"].append(_dot_nt(w["qh"][rs], w["st"].astype(BF16)) + _hgrn_intra_out(w["scores"][sub], w["v"][rs]))
                w["st"] = jnp.where(bdmask, w["st"] * w["dec"][SUB * sub:SUB * sub + 1, :] + w["adds"][sub], 0.0)
        for pair, w in enumerate(work):
            sl = w["sl"]
            st_ref[pair] = w["st"]
            o = jnp.concatenate(w["outs"], axis=0)
            o_ref[:, sl] = o
            _, _, sz, _, on = _hgrn_norm_gate(o, z_ref[:, sl], on_ref[:, sl])
            y_ref[:, sl] = (on * sz).astype(BF16)

    col = lambda c: pl.BlockSpec((CHUNK, B_WIDTH), lambda i, c=c: (i, c // B_WIDTH))
    full = lambda a: pl.BlockSpec(a.shape, lambda i, n=a.ndim: (0,) * n)
    return _Part(body, (proj, proj, proj, proj, lb, onorm),
                 [col(COL_BQ), col(COL_BF), col(COL_BI), col(COL_BZ), full(lb), full(onorm)],
                 [pl.BlockSpec((CHUNK, B_WIDTH), lambda i: (i, 0)), pl.BlockSpec((CHUNK, B_WIDTH), lambda i: (i, 0)),
                  pl.BlockSpec((1, 2, 128, 128), lambda i: (i, 0, 0, 0))],
                 [SDS((T, B_WIDTH), BF16), SDS((T, B_WIDTH), F32), SDS((n, 2, 128, 128), F32)],
                 [pltpu.VMEM((2, 128, 128), F32)])


def _hgrn_bwd(proj, dy, o_saved, s0, lb, onorm):
    T = proj.shape[0]
    n = T // CHUNK
    nsub = CHUNK // SUB

    def body(q_ref, f_ref, i_ref, z_ref, dy_ref, o_ref, s0_ref, lb_ref, on_ref,
             db_ref, dlb_ref, don_ref, dst_ref, sts_ref):
        @pl.when(pl.program_id(0) == 0)
        def _():
            dst_ref[...] = jnp.zeros_like(dst_ref)
            dlb_ref[...] = jnp.zeros_like(dlb_ref)
            don_ref[...] = jnp.zeros_like(don_ref)

        lsub, usub, bsub = _hgrn_consts()
        mbd = _block_diag64()
        bdmask = mbd > 0
        rsum = jnp.where((_lane((SUB, SUB * SUB)) >> SUB_SHIFT) == _row((SUB, SUB * SUB)), 1.0, 0.0).astype(BF16)
        subs = [slice(SUB * sub, SUB * sub + SUB) for sub in range(nsub)]
        work = []
        for pair in range(2):
            sl = slice(128 * pair, 128 * pair + 128)
            lbp = lb_ref[:, sl]
            qt, dsq, sg, sgn, f, g, kf = _hgrn_gates(q_ref[:, sl], f_ref[:, sl], lbp)
            w = dict(sl=sl, lbp=lbp, qt=qt, dsq=dsq, sg=sg, sgn=sgn, f=f, kf=kf, v=i_ref[:, sl],
                     b=_dot3_left(lsub, g), bl=_dot3_left(bsub, g))
            onp = on_ref[:, sl]
            xh, r, sz, dsz, on = _hgrn_norm_gate(o_ref[:, sl], z_ref[:, sl], onp)
            dyv = dy_ref[:, sl]
            w["dz"] = dyv * on * dsz
            don = dyv * sz
            cn = jnp.sum(don * xh, axis=0, keepdims=True)
            don_ref[...] += cn + pltpu.roll(cn, 64, axis=1)
            dxo = don * onp
            w["do"] = r * (dxo - xh * (_gsum64(dxo * xh) * (1.0 / 64)))
            work.append(w)
        for w in work:
            qt, kf, v, b, bl, do = w["qt"], w["kf"], w["v"], w["b"], w["bl"], w["do"]
            w["eb"] = jnp.exp(b)
            w["ekb"] = jnp.exp(bl - b)
            w["qhb"] = (qt * w["eb"]).astype(BF16)
            w["khb"] = (kf * w["ekb"]).astype(BF16)
            w["dec"] = jnp.exp(bl)
            w["vb"] = v.astype(BF16)
            w["dob"] = do.astype(BF16)
            w["scores"] = [_hgrn_intra_bwd_scores(qt[rs], kf[rs], b[rs], v[rs], do[rs], mbd) for rs in subs]
            w["st_adds"] = [_dot_tn(w["vb"][rs], w["khb"][rs]) for rs in subs]
            w["gst_adds"] = [_dot_tn(w["dob"][rs], w["qhb"][rs]) for rs in subs]
        for pair, w in enumerate(work):
            w["st"] = s0_ref[0, pair]
        for sub in range(nsub):
            for pair, w in enumerate(work):
                sts_ref[pair, sub] = w["st"]
                w["st"] = jnp.where(bdmask, w["st"] * w["dec"][SUB * sub:SUB * sub + 1, :] + w["st_adds"][sub], 0.0)
        for pair, w in enumerate(work):
            w["gst"] = dst_ref[pair]
            w["dqt_p"], w["dkf_p"], w["dv_p"], w["dbl_p"] = ([None] * nsub for _ in range(4))
        for sub in reversed(range(nsub)):
            rs = subs[sub]
            for pair, w in enumerate(work):
                gst = w["gst"]
                st_in = sts_ref[pair, sub]
                gb = gst.astype(BF16)
                dqh = _dot(w["dob"][rs], st_in.astype(BF16))
                dkh = _dot(w["vb"][rs], gb)
                dv_inter = _dot_nt(w["khb"][rs], gb)
                ddec = jnp.sum(gst * st_in, axis=0, keepdims=True)
                dec_row = w["dec"][SUB * sub:SUB * sub + 1, :]
                w["gst"] = jnp.where(bdmask, gst * dec_row + w["gst_adds"][sub], 0.0)
                dqt_i, dkf_i, dv_i = _hgrn_intra_bwd_grads(w["scores"][sub], w["qt"][rs], w["do"][rs], rsum)
                dkf_inter = dkh * w["ekb"][rs]
                w["dqt_p"][sub] = dqh * w["eb"][rs] + dqt_i
                w["dkf_p"][sub] = dkf_inter + dkf_i
                w["dv_p"][sub] = dv_inter + dv_i
                row = jnp.sum(w["kf"][rs] * dkf_inter, axis=0, keepdims=True) + ddec * dec_row
                w["dbl_p"][sub] = jnp.broadcast_to(row, (SUB, 128))
        for pair, w in enumerate(work):
            sl, lbp, sg, sgn, f = w["sl"], w["lbp"], w["sg"], w["sgn"], w["f"]
            dst_ref[pair] = w["gst"]
            dqt = jnp.concatenate(w["dqt_p"], axis=0)
            dkf = jnp.concatenate(w["dkf_p"], axis=0)
            dv = jnp.concatenate(w["dv_p"], axis=0)
            dg = _dot3_left(usub, w["qt"] * dqt - w["kf"] * dkf) + jnp.concatenate(w["dbl_p"], axis=0)
            df = jnp.where(f > F_FLOOR, dg / f, 0.0)
            dlb_ref[:, sl] += jnp.sum(df * (1.0 - sg) - dkf * sgn, axis=0, keepdims=True)
            dfl = (1.0 - lbp) * sg * sgn * (df - dkf)
            dq = dqt * Q_SCALE * w["dsq"]
            db_ref[:, 0 * B_WIDTH + 128 * pair:0 * B_WIDTH + 128 * pair + 128] = dq.astype(BF16)
            db_ref[:, 1 * B_WIDTH + 128 * pair:1 * B_WIDTH + 128 * pair + 128] = dfl.astype(BF16)
            db_ref[:, 2 * B_WIDTH + 128 * pair:2 * B_WIDTH + 128 * pair + 128] = dv.astype(BF16)
            db_ref[:, 3 * B_WIDTH + 128 * pair:3 * B_WIDTH + 128 * pair + 128] = w["dz"].astype(BF16)

    rev = lambda c: pl.BlockSpec((CHUNK, B_WIDTH), lambda i, c=c: (n - 1 - i, c // B_WIDTH))
    full = lambda a: pl.BlockSpec(a.shape, lambda i, n_=a.ndim: (0,) * n_)
    acc = lambda s: pl.BlockSpec(s, lambda i, n_=len(s): (0,) * n_)
    return _Part(body, (proj, proj, proj, proj, dy, o_saved, s0, lb, onorm),
                 [rev(COL_BQ), rev(COL_BF), rev(COL_BI), rev(COL_BZ),
                  pl.BlockSpec((CHUNK, B_WIDTH), lambda i: (n - 1 - i, 1)),
                  pl.BlockSpec((CHUNK, B_WIDTH), lambda i: (n - 1 - i, 0)),
                  pl.BlockSpec((1, 2, 128, 128), lambda i: (n - 1 - i, 0, 0, 0)), full(lb), full(onorm)],
                 [pl.BlockSpec((CHUNK, 4 * B_WIDTH), lambda i: (n - 1 - i, 0)), acc((1, B_WIDTH)), acc((1, 128))],
                 [SDS((T, 4 * B_WIDTH), BF16), SDS((1, B_WIDTH), F32), SDS((1, 128), F32)],
                 [pltpu.VMEM((2, 128, 128), F32), pltpu.VMEM((2, nsub, 128, 128), F32)])


def _lb_fwd(hgrn_lb):
    assert hgrn_lb.shape[0] == 2

    def body(x_ref, o_ref):
        x0, x1 = x_ref[0:1, :], x_ref[1:2, :]
        m = jnp.maximum(x0, x1)
        e0, e1 = jnp.exp(x0 - m), jnp.exp(x1 - m)
        p0, p1 = e0 / (e0 + e1), e1 / (e0 + e1)
        o_ref[0:1, :] = jnp.clip(p0 - p0, 0.0, 1.0 - 1e-6)
        o_ref[1:2, :] = jnp.clip((p0 + p1) - p0, 0.0, 1.0 - 1e-6)

    return pl.pallas_call(body, name="lb_fwd", out_shape=SDS(hgrn_lb.shape, F32))(hgrn_lb)


def _lb_bwd(hgrn_lb, dlb):
    def body(x_ref, d_ref, o_ref):
        x0, x1 = x_ref[0:1, :], x_ref[1:2, :]
        m = jnp.maximum(x0, x1)
        e0, e1 = jnp.exp(x0 - m), jnp.exp(x1 - m)
        p0, p1 = e0 / (e0 + e1), e1 / (e0 + e1)
        val = (p0 + p1) - p0
        dp1 = jnp.where((val > 0.0) & (val < 1.0 - 1e-6), d_ref[1:2, :], 0.0)
        inner = p1 * dp1
        o_ref[0:1, :] = p0 * (0.0 - inner)
        o_ref[1:2, :] = p1 * (dp1 - inner)

    return pl.pallas_call(body, name="lb_bwd", out_shape=SDS(hgrn_lb.shape, F32))(hgrn_lb, dlb)


def _fox_prep(proj, bf):
    T = proj.shape[0]
    n = T // CHUNK

    def body(q0_ref, q1_ref, k0_ref, k1_ref, v0_ref, v1_ref, fl_ref, bf_ref, qo_ref, ko_ref, vt_ref, carry_ref):
        for p, v_ref in enumerate((v0_ref, v0_ref, v1_ref, v1_ref)):
            vt_ref[p, 0] = v_ref[:, 128 * (p % 2):128 * (p % 2) + 128].T.astype(BF16)

        @pl.when(pl.program_id(0) == 0)
        def _():
            carry_ref[...] = jnp.zeros_like(carry_ref)

        ltri = jnp.where(_lane((CHUNK, CHUNK)) <= _row((CHUNK, CHUNK)), 1.0, 0.0).astype(BF16)
        lf = jax.nn.log_sigmoid(fl_ref[...] + bf_ref[...])
        c = _dot3_left(ltri, lf) + carry_ref[...]
        carry_ref[...] = c[CHUNK - 1:CHUNK, :]
        lane = _lane((CHUNK, 128))
        feat = lane < 64
        ones_q = (lane >= 67) & (lane <= 69)
        ones_k = (lane >= 64) & (lane <= 66)
        qrefs, krefs = (q0_ref, q1_ref), (k0_ref, k1_ref)
        for h in range(C_HEADS):
            blk = slice(128 * ((h // 2) % 2), 128 * ((h // 2) % 2) + 128)
            qp, kp = qrefs[h // 4][:, blk], krefs[h // 4][:, blk]
            if h % 2:
                qp, kp = pltpu.roll(qp, 64, axis=1), pltpu.roll(kp, 64, axis=1)
            ch = jnp.broadcast_to(c[:, h:h + 1], (CHUNK, 128))
            hi = ch.astype(BF16).astype(F32)
            r1 = ch - hi
            mid = r1.astype(BF16).astype(F32)
            lo = r1 - mid
            aq = jnp.where(lane == 64, hi, jnp.where(lane == 65, mid, jnp.where(lane == 66, lo,
                           jnp.where(ones_q, 1.0, 0.0))))
            ak = jnp.where(lane == 67, -hi, jnp.where(lane == 68, -mid, jnp.where(lane == 69, -lo,
                           jnp.where(ones_k, 1.0, 0.0))))
            qo_ref[:, 128 * h:128 * h + 128] = jnp.where(feat, qp * Q_SCALE, aq).astype(BF16)
            ko_ref[:, 128 * h:128 * h + 128] = jnp.where(feat, kp, ak).astype(BF16)

    w = 256
    col = lambda c: pl.BlockSpec((CHUNK, w), lambda i, c=c: (i, c // w))
    return _Part(body, (proj, proj, proj, proj, proj, proj, proj, bf),
                 [col(COL_CQ), col(COL_CQ + w), col(COL_CK), col(COL_CK + w), col(COL_CV), col(COL_CV + w),
                  pl.BlockSpec((CHUNK, 128), lambda i: (i, COL_CF // 128)), pl.BlockSpec((1, 128), lambda i: (0, 0))],
                 [pl.BlockSpec((CHUNK, C_HEADS * 128), lambda i: (i, 0))] * 2
                 + [pl.BlockSpec((C_HEADS // 2, 1, 128, CHUNK), lambda i: (0, i, 0, 0))],
                 [SDS((T, C_HEADS * 128), BF16)] * 2 + [SDS((C_HEADS // 2, n, 128, CHUNK), BF16)],
                 [pltpu.VMEM((1, 128), F32)])


FOX_TILE = 512
FOX_KEYS = 512
FOX_STRIP = 16


def _fox_mask(tk, tq, k0, q0):
    return (_row((tk, tq)) + (k0 - q0)) <= _lane((tk, tq))


def _ride_refs(ride, rest, n_out, n_scratch):
    n = ride.n if ride else 0
    srcs, rest = rest[:n], rest[n:]
    outs, rest = rest[:n_out], rest[n_out:]
    dsts, rest = rest[:n], rest[n:]
    return srcs, outs, dsts, rest[:n_scratch], rest[n_scratch:]


def _ride_start(ride, grid, srcs, dsts, sems):
    if ride:
        first = functools.reduce(lambda a, b: a & b, [pl.program_id(d) == 0 for d in range(len(grid))])
        pl.when(first)(lambda: ride.start(srcs, dsts, sems))


def _ride_wait(ride, grid, srcs, dsts, sems):
    if ride:
        last = functools.reduce(lambda a, b: a & b, [pl.program_id(d) == n - 1 for d, n in enumerate(grid)])
        pl.when(last)(lambda: ride.wait(srcs, dsts, sems))


def _fox_fwd(qt, kt, vt, proj, tag, ride=None):
    T = proj.shape[0]
    tq, tk = _tile(T, FOX_TILE), _tile(T, FOX_KEYS)
    nq, nsub = T // tq, tk // CHUNK
    npair = C_HEADS // 2

    def body(q_ref, k_ref, vt_ref, z_ref, *rest):
        ride_srcs, (o_ref, lse_ref, y_ref), ride_dsts, (acc_ref, st_ref, pt_ref), ride_sems = _ride_refs(ride, rest, 3, 3)
        i = pl.program_id(1)
        _ride_start(ride, (npair, nq), ride_srcs, ride_dsts, ride_sems)

        qs = (q_ref[:, 0:128], q_ref[:, 128:256])
        acc_ref[...] = jnp.zeros_like(acc_ref)
        pt_ref[...] = jnp.zeros_like(pt_ref)
        nfull = (i * tq) // tk

        def scores(j):
            kb = k_ref[pl.ds(pl.multiple_of(j * tk, tk), tk), :]
            return tuple(_dot_nt(kb[:, 128 * h:128 * h + 128], qs[h]) for h in range(2))

        def weigh(j, h):
            rows = slice(64 * h, 64 * h + 64)
            vth = jnp.concatenate([vt_ref[0, nsub * j + c, rows, :] for c in range(nsub)], axis=1)
            return _dot(vth, pt_ref[h])

        def block(j, carry, diagonal):
            nxt = () if diagonal else scores(j + 1)
            pvs = [weigh(jnp.maximum(j - 1, 0), h) for h in range(2)]
            new = []
            for h in range(2):
                m, l, alpha_prev = carry[3 * h:3 * h + 3]
                st = st_ref[h]
                if diagonal:
                    st = jnp.where(_fox_mask(tk, tq, j * tk, i * tq), st, -jnp.inf)
                m_new = jnp.maximum(m, _colreduce(st, jnp.maximum))
                pt = jnp.exp(st - m_new)
                alpha = jnp.exp(m - m_new)
                rows = slice(64 * h, 64 * h + 64)
                acc_ref[rows, :] = alpha_prev * acc_ref[rows, :] + pvs[h]
                pt_ref[h] = pt.astype(BF16)
                new += [m_new, alpha * l + _colreduce(pt, jnp.add), alpha]
            for h, st in enumerate(nxt):
                st_ref[h] = st
            return tuple(new)

        for h, st in enumerate(scores(0)):
            st_ref[h] = st
        init = (jnp.full((1, tq), -jnp.inf, F32), jnp.zeros((1, tq), F32), jnp.ones((1, tq), F32)) * 2
        carry = lax.fori_loop(0, nfull, lambda j, c: block(j, c, False), init)
        m0, l0, a0, m1, l1, a1 = block(nfull, carry, True)
        for h, alpha in enumerate((a0, a1)):
            rows = slice(64 * h, 64 * h + 64)
            acc_ref[rows, :] = alpha * acc_ref[rows, :] + weigh(nfull, h)
        inv = jnp.where(_row((128, tq)) < 64, 1.0 / l0, 1.0 / l1)
        o = (acc_ref[...] * inv).T
        o_ref[...] = o
        r8 = _row((8, tq))
        lse_ref[0, 0] = jnp.where(r8 == 0, m0 + jnp.log(l0), jnp.where(r8 == 1, m1 + jnp.log(l1), 0.0))
        sz, _ = _silu_and_grad(z_ref[...])
        y_ref[...] = (o * sz).astype(BF16)
        _ride_wait(ride, (npair, nq), ride_srcs, ride_dsts, ride_sems)

    blk = pl.BlockSpec((tq, 128), lambda p, i: (i, p))
    extra = ride or _ChipExchange("gather", ())
    return pl.pallas_call(
        body, name=f"fox_fwd_{tag}", grid=(npair, nq),
        in_specs=[pl.BlockSpec((tq, 256), lambda p, i: (i, p)), pl.BlockSpec((T, 256), lambda p, i: (0, p)),
                  pl.BlockSpec((1, T // CHUNK, 128, CHUNK), lambda p, i: (p, 0, 0, 0)),
                  pl.BlockSpec((tq, 128), lambda p, i: (i, COL_CZ // 128 + p))] + extra.in_specs,
        out_specs=[blk, pl.BlockSpec((1, 1, 8, tq), lambda p, i: (p, i, 0, 0)), blk] + extra.out_specs,
        out_shape=[SDS((T, C_WIDTH), F32), SDS((npair, nq, 8, tq), F32), SDS((T, C_WIDTH), BF16)] + extra.out_shape,
        scratch_shapes=[pltpu.VMEM((128, tq), F32), pltpu.VMEM((2, tk, tq), F32), pltpu.VMEM((2, tk, tq), BF16)]
        + (extra.scratch if ride else []),
        compiler_params=pltpu.CompilerParams(dimension_semantics=("arbitrary", "arbitrary"), vmem_limit_bytes=VMEM_LIMIT,
                                             has_side_effects=bool(ride)),
    )(qt, kt, vt, proj, *extra.sources)


def _fox_bwd_prep(proj, dy, o, qt, tag):
    T = proj.shape[0]
    tq = _tile(T, FOX_TILE)
    nq = T // tq

    def body(z0_ref, z1_ref, dy_ref, o_ref, q_ref, do_ref, dl_ref, dz_ref, dot_ref, qt_ref):
        sel = jnp.where((_lane((16, 128)) >> 6) == _row((16, 128)), 1.0, 0.0).astype(BF16)
        for p, z_ref in enumerate((z0_ref, z0_ref, z1_ref, z1_ref)):
            sl = slice(128 * p, 128 * p + 128)
            sz, dsz = _silu_and_grad(z_ref[:, 128 * (p % 2):128 * (p % 2) + 128])
            dyv, ov = dy_ref[:, sl], o_ref[:, sl]
            do = dyv * sz
            do_ref[:, sl] = do.astype(BF16)
            dot_ref[p, 0] = do.T.astype(BF16)
            dz_ref[:, sl] = (dyv * ov * dsz).astype(BF16)
            hi, mid, lo = _split3(do * ov)
            dl_ref[p, 0] = (_dot_nt(sel, hi) + _dot_nt(sel, mid) + _dot_nt(sel, lo))[0:8, :]
        for h in range(C_HEADS):
            qt_ref[h, 0] = q_ref[:, 128 * h:128 * h + 128].astype(F32).T.astype(BF16)

    w = 256
    blk = pl.BlockSpec((tq, C_WIDTH), lambda i: (i, 0))
    return pl.pallas_call(
        body, name=f"fox_bwd_prep_{tag}", grid=(nq,),
        in_specs=[pl.BlockSpec((tq, w), lambda i: (i, COL_CZ // w)), pl.BlockSpec((tq, w), lambda i: (i, COL_CZ // w + 1)),
                  pl.BlockSpec((tq, C_WIDTH), lambda i: (i, (A_WIDTH + B_WIDTH) // C_WIDTH)), blk,
                  pl.BlockSpec((tq, C_HEADS * 128), lambda i: (i, 0))],
        out_specs=[blk, pl.BlockSpec((C_HEADS // 2, 1, 8, tq), lambda i: (0, i, 0, 0)), blk,
                   pl.BlockSpec((C_HEADS // 2, 1, 128, tq), lambda i: (0, i, 0, 0)),
                   pl.BlockSpec((C_HEADS, 1, 128, tq), lambda i: (0, i, 0, 0))],
        out_shape=[SDS((T, C_WIDTH), BF16), SDS((C_HEADS // 2, nq, 8, tq), F32), SDS((T, C_WIDTH), BF16),
                   SDS((C_HEADS // 2, nq, 128, tq), BF16), SDS((C_HEADS, nq, 128, tq), BF16)],
        compiler_params=_params("parallel"),
    )(proj, proj, dy, o, qt)


def _fox_bwd(qt, kt, proj, do, lse, delta, dot, qtr, tag, ride=None):
    T = proj.shape[0]
    tq, tk = _tile(T, FOX_TILE), _tile(T, FOX_KEYS)
    nq, nk = T // tq, T // tk
    assert tq == tk
    npair = C_HEADS // 2

    def body(q_ref, k_ref, v_ref, do_ref, lse_ref, dl_ref, dot_ref, qtr_ref, *rest):
        ride_srcs, (dq_ref, dk_ref, dv_ref), ride_dsts, scratch, ride_sems = _ride_refs(ride, rest, 3, 4)
        dvt_ref, dkt_ref, pt_ref, ds_ref = scratch
        j = pl.program_id(1)
        first = (j * tk) // tq
        _ride_start(ride, (npair, nk), ride_srcs, ride_dsts, ride_sems)

        @pl.when(j == 0)
        def _():
            dq_ref[...] = jnp.zeros_like(dq_ref)

        dkt_ref[...] = jnp.zeros_like(dkt_ref)
        dvt_ref[...] = jnp.zeros_like(dvt_ref)
        ks = (k_ref[:, 0:128], k_ref[:, 128:256])
        kts = tuple(k.astype(F32).T.astype(BF16) for k in ks)
        vb = v_ref[...].astype(BF16)
        lo = _lane((tq, 128)) < 64

        def operands(i):
            q0 = pl.multiple_of(i * tq, tq)
            qb = q_ref[pl.ds(q0, tq), :]
            dob = do_ref[pl.ds(q0, tq), :]
            qhs = (qb[:, 0:128], qb[:, 128:256])
            dohs = (jnp.where(lo, dob, jnp.zeros_like(dob)), jnp.where(lo, jnp.zeros_like(dob), dob))
            return qhs, dohs

        def scores(i):
            qhs, dohs = operands(i)
            return tuple((_dot_nt(ks[h], qhs[h]), _dot_nt(vb, dohs[h])) for h in range(2))

        def grads(i, slot):
            for h in range(2):
                rows = slice(64 * h, 64 * h + 64)
                dvt_ref[rows, :] += _dot_nt(dot_ref[0, i, rows, :], pt_ref[slot, h])
                dkt_ref[h] += _dot_nt(qtr_ref[h, i], ds_ref[slot, h])
                dq_ref[h, i] += _dot(kts[h], ds_ref[slot, h])

        def block(i, slot, diagonal, opening):
            sc = scores(i)
            if not opening:
                grads(i - 1, 1 - slot)
            lsev = lse_ref[0, i]
            dlv = dl_ref[0, i]
            for h in range(2):
                lseh = jnp.broadcast_to(lsev[h:h + 1, :], (FOX_STRIP, tq))
                dlh = jnp.broadcast_to(dlv[h:h + 1, :], (FOX_STRIP, tq))
                for r in range(0, tk, FOX_STRIP):
                    rows = slice(r, r + FOX_STRIP)
                    pt = jnp.exp(sc[h][0][rows, :] - lseh)
                    if diagonal:
                        pt = jnp.where(_fox_mask(FOX_STRIP, tq, r, 0), pt, 0.0)
                    ds_ref[slot, h, rows, :] = (pt * (sc[h][1][rows, :] - dlh)).astype(BF16)
                    pt_ref[slot, h, rows, :] = pt.astype(BF16)

        block(first, 0, True, True)
        rest = nq - 1 - first

        def two_steps(t, carry):
            block(first + 1 + 2 * t, 1, False, False)
            block(first + 2 + 2 * t, 0, False, False)
            return carry

        lax.fori_loop(0, rest // 2, two_steps, 0)
        pl.when(rest % 2 == 1)(lambda: block(nq - 1, 1, False, False))
        grads(nq - 1, rest % 2)
        dv_ref[...] = dvt_ref[...].T.astype(BF16)
        for h in range(2):
            dk_ref[:, 128 * h:128 * h + 128] = dkt_ref[h].T
        _ride_wait(ride, (npair, nk), ride_srcs, ride_dsts, ride_sems)

    full = lambda w: pl.BlockSpec((T, w), lambda p, j: (0, p))
    stat = pl.BlockSpec((1, nq, 8, tq), lambda p, j: (p, 0, 0, 0))
    extra = ride or _ChipExchange("gather", ())
    return pl.pallas_call(
        body, name=f"fox_bwd_{tag}", grid=(npair, nk),
        in_specs=[full(256), pl.BlockSpec((tk, 256), lambda p, j: (j, p)),
                  pl.BlockSpec((tk, 128), lambda p, j: (j, COL_CV // 128 + p)), full(128), stat, stat,
                  pl.BlockSpec((1, nq, 128, tq), lambda p, j: (p, 0, 0, 0)),
                  pl.BlockSpec((2, nq, 128, tq), lambda p, j: (p, 0, 0, 0))] + extra.in_specs,
        out_specs=[pl.BlockSpec((2, nq, 128, tq), lambda p, j: (p, 0, 0, 0)), pl.BlockSpec((tk, 256), lambda p, j: (j, p)),
                   pl.BlockSpec((tk, 128), lambda p, j: (j, p))] + extra.out_specs,
        out_shape=[SDS((C_HEADS, nq, 128, tq), F32), SDS((T, C_HEADS * 128), F32), SDS((T, C_WIDTH), BF16)]
        + extra.out_shape,
        scratch_shapes=[pltpu.VMEM((128, tk), F32), pltpu.VMEM((2, 128, tk), F32),
                        pltpu.VMEM((2, 2, tk, tq), BF16), pltpu.VMEM((2, 2, tk, tq), BF16)]
        + (extra.scratch if ride else []),
        compiler_params=pltpu.CompilerParams(dimension_semantics=("arbitrary", "arbitrary"), vmem_limit_bytes=VMEM_LIMIT,
                                             has_side_effects=bool(ride)),
    )(qt, kt, proj, do, lse, delta, dot, qtr, *extra.sources)


def _fox_bwd_post(dqt, dkt, proj, bf, tag):
    T = proj.shape[0]
    tq = _tile(T, FOX_TILE)
    n = T // tq

    def body(dq_ref, dk_ref, fl_ref, bf_ref, oq_ref, ok_ref, ofl_ref, dbf_ref, carry_ref):
        @pl.when(pl.program_id(0) == 0)
        def _():
            carry_ref[...] = jnp.zeros_like(carry_ref)
            dbf_ref[...] = jnp.zeros_like(dbf_ref)

        lane = _lane((tq, 128))
        lo = lane < 64
        dqs = [dq_ref[h, 0].T for h in range(C_HEADS)]
        dc = jnp.zeros((tq, 128), F32)
        for h in range(C_HEADS):
            dc = dc + jnp.where(lane == h, dqs[h][:, 64:65] - dk_ref[:, 128 * h + 67:128 * h + 68], 0.0)
        utri = jnp.where(_lane((tq, tq)) >= _row((tq, tq)), 1.0, 0.0).astype(BF16)
        dlf = _dot3_left(utri, dc) + carry_ref[...]
        carry_ref[...] = dlf[0:1, :]
        dfl = jnp.where(lane < C_HEADS, dlf * _sigmoid(-(fl_ref[...] + bf_ref[...])), 0.0)
        ofl_ref[...] = dfl.astype(BF16)
        dbf_ref[...] += jnp.sum(dfl, axis=0, keepdims=True)
        for p in range(C_HEADS // 2):
            a, b = 128 * (2 * p), 128 * (2 * p + 1)
            oq_ref[:, 128 * p:128 * p + 128] = (
                jnp.where(lo, dqs[2 * p], pltpu.roll(dqs[2 * p + 1], 64, axis=1)) * Q_SCALE).astype(BF16)
            ok_ref[:, 128 * p:128 * p + 128] = jnp.where(
                lo, dk_ref[:, a:a + 128], pltpu.roll(dk_ref[:, b:b + 128], 64, axis=1)).astype(BF16)

    rev = lambda w: pl.BlockSpec((tq, w), lambda i: (n - 1 - i, 0))
    return pl.pallas_call(
        body, name=f"fox_bwd_post_{tag}", grid=(n,),
        in_specs=[pl.BlockSpec((C_HEADS, 1, 128, tq), lambda i: (0, n - 1 - i, 0, 0)), rev(C_HEADS * 128),
                  pl.BlockSpec((tq, 128), lambda i: (n - 1 - i, COL_CF // 128)), pl.BlockSpec((1, 128), lambda i: (0, 0))],
        out_specs=[rev(C_WIDTH), rev(C_WIDTH), rev(128), pl.BlockSpec((1, 128), lambda i: (0, 0))],
        out_shape=[SDS((T, C_WIDTH), BF16), SDS((T, C_WIDTH), BF16), SDS((T, 128), BF16), SDS((1, 128), F32)],
        scratch_shapes=[pltpu.VMEM((1, 128), F32)], compiler_params=_params("arbitrary"),
    )(dqt, dkt, proj, bf)


def _adamw_math(w, g, m, v):
    m = ADAM_B1 * m + (1.0 - ADAM_B1) * g
    v = ADAM_B2 * v + (1.0 - ADAM_B2) * (g * g)
    delta = -ADAM_LR * ((m / ADAM_C1) / (jnp.sqrt(v / ADAM_C2) + ADAM_EPS) + ADAM_WD * w)
    return delta, m, v


def _adamw_pair(w, m, v, ga, gb, name):
    n0 = w.shape[0]
    most = max(1, ADAMW_BLOCK_BYTES // (4 * math.prod(w.shape[1:])))
    t0 = max(t for t in range(1, min(n0, most) + 1) if n0 % t == 0)

    def body(w_ref, m_ref, v_ref, ga_ref, gb_ref, g_ref, d_ref, nm_ref, nv_ref):
        g = ga_ref[...] + gb_ref[...]
        g_ref[...] = g
        d_ref[...], nm_ref[...], nv_ref[...] = _adamw_math(w_ref[...], g, m_ref[...], v_ref[...])

    blk = pl.BlockSpec((t0,) + w.shape[1:], lambda i: (i, 0, 0))
    return pl.pallas_call(
        body, name=name, grid=(n0 // t0,), in_specs=[blk] * 5, out_specs=[blk] * 4,
        out_shape=[SDS(w.shape, F32)] * 4, compiler_params=_params("parallel"),
    )(w, m, v, ga, gb)


def _small_layout():
    L = DEPTH
    lanes = lambda j: slice(128 * j, 128 * j + 128)
    wide = lambda n: [((slice(l, l + 1), lanes(j)), n * l + j, 1, 0, 128) for l in range(L) for j in range(n)]
    halves = [((l, slice(g, g + 1)), 2 * l + g // 2, 1, 64 * (g % 2), 64) for l in range(L) for g in range(A_GROUPS)]
    side_by_side = lambda w: [((slice(l, l + 1),), 0, 1, w * l, w) for l in range(L)]
    return [
        ((L, D_MODEL), wide(D_MODEL // 128)), ((L, A_GROUPS, 64), halves), ((L, A_GROUPS, 64), halves),
        ((L * A_GROUPS * CHUNK, CHUNK), [((slice(None),), 0, L * A_GROUPS * CHUNK, 0, CHUNK)]),
        ((L, A_GROUPS, CHUNK), [((l,), A_GROUPS * l, A_GROUPS, 0, CHUNK) for l in range(L)]),
        ((L, B_WIDTH), wide(B_WIDTH // 128)), ((L, 64), side_by_side(64)), ((L, C_HEADS), side_by_side(C_HEADS)),
        ((1, D_MODEL), [((slice(None), lanes(j)), j, 1, 0, 128) for j in range(D_MODEL // 128)]),
    ]


def _adamw_small(ws, ms, vs, gearly, glate, late):
    offs = _small_offsets()
    layout = _small_layout()
    n = len(ws)
    assert [w.shape for w in ws] == [shape for shape, _ in layout] and 0 in late

    def body(*refs):
        w_refs, m_refs, v_refs = refs[:n], refs[n:2 * n], refs[2 * n:3 * n]
        early_ref, late_ref = refs[3 * n:3 * n + 2]
        outs = refs[3 * n + 2:]

        def total(k):
            rows = offs[k + 1] - offs[k] if k < n else 1

            def block(dev):
                parts = [early_ref[dev, offs[k] - offs[1]:offs[k] - offs[1] + rows, :]] if k else []
                if k in late:
                    parts.append(late_ref[dev, late[k]:late[k] + rows, :])
                return functools.reduce(jnp.add, parts)

            return functools.reduce(jnp.add, [block(dev) for dev in range(N_DEV)])

        for k, (_, pieces) in enumerate(layout):
            g = total(k)
            go_ref, d_ref, nm_ref, nv_ref = outs[4 * k:4 * k + 4]
            for idx, row, rows, lane, width in pieces:
                gp = g[row:row + rows, :]
                if lane:
                    gp = pltpu.roll(gp, 128 - lane, axis=1)
                gp = gp[:, :width]
                go_ref[idx] = gp
                d_ref[idx], nm_ref[idx], nv_ref[idx] = _adamw_math(w_refs[k][idx], gp, m_refs[k][idx], v_refs[k][idx])
        outs[4 * n][...] = total(n)

    shapes = [SDS(w.shape, F32) for w in ws for _ in range(4)] + [SDS((1, 128), F32)]
    res = pl.pallas_call(body, name="adamw_small", out_shape=shapes,
                         compiler_params=pltpu.CompilerParams(vmem_limit_bytes=VMEM_LIMIT))(*ws, *ms, *vs, gearly, glate)
    return [res[4 * k:4 * k + 4] for k in range(n)], res[4 * n]


def _pack_grads(dlng, dlnb, dwm, dbst, dlb, donorm, dbf, dfinal, loss_part):
    offs = _small_offsets()
    base = offs[1]
    L = len(dwm)
    assert L == 2

    def body(*refs):
        lng, lnb, wm, bst, on, bf = (refs[L * a:L * a + L] for a in range(6))
        lb_ref, fin_ref, loss_ref, o_ref = refs[6 * L:]
        o_ref[...] = jnp.zeros_like(o_ref)
        lane = _lane((1, 128))
        for l in range(L):
            for j in range(2):
                o_ref[offs[1] - base + 2 * l + j:offs[1] - base + 2 * l + j + 1, :] = lng[l][:, 128 * j:128 * j + 128]
                o_ref[offs[2] - base + 2 * l + j:offs[2] - base + 2 * l + j + 1, :] = lnb[l][:, 128 * j:128 * j + 128]
                o_ref[offs[5] - base + 2 * l + j:offs[5] - base + 2 * l + j + 1, :] = lb_ref[l:l + 1, 128 * j:128 * j + 128]
            for g in range(A_GROUPS):
                row = offs[3] - base + (A_GROUPS * l + g) * CHUNK
                o_ref[row:row + CHUNK, :] = wm[l][g]
            o_ref[offs[4] - base + A_GROUPS * l:offs[4] - base + A_GROUPS * (l + 1), :] = bst[l][...].T[0:A_GROUPS, :]
        o_ref[offs[6] - base:offs[6] - base + 1, :] = jnp.where(lane < 64, on[0][...], pltpu.roll(on[1][...], 64, axis=1))
        o_ref[offs[7] - base:offs[7] - base + 1, :] = jnp.where(
            lane < C_HEADS, bf[0][...], jnp.where(lane < 2 * C_HEADS, pltpu.roll(bf[1][...], C_HEADS, axis=1), 0.0))
        for j in range(D_MODEL // 128):
            o_ref[offs[8] - base + j:offs[8] - base + j + 1, :] = fin_ref[:, 128 * j:128 * j + 128]
        o_ref[offs[9] - base:offs[9] - base + 1, :] = loss_ref[...]

    rows = offs[9] + 8 - base
    return pl.pallas_call(body, name="pack_grads", out_shape=SDS((rows, 128), F32))(
        *dlng, *dlnb, *dwm, *dbst, *donorm, *dbf, dlb, dfinal, loss_part)


def _sum_chips(layers, name, layer_major):
    _, R, C = layers[0].shape
    L = len(layers)
    tc = _tile(C, 256)

    def body(*refs):
        o_ref = refs[-1]
        for l, p_ref in enumerate(refs[:-1]):
            p = [p_ref[k].astype(F32) for k in range(N_CHIPS)]
            s = ((p[0] + p[1]) + p[2]) + p[3]
            if layer_major:
                o_ref[l] = s
            else:
                o_ref[:, l, :] = s

    out = (L, R, C) if layer_major else (R, L, C)
    out_blk = (L, R, tc) if layer_major else (R, L, tc)
    return pl.pallas_call(
        body, name=name, grid=(C // tc,),
        in_specs=[pl.BlockSpec((N_CHIPS, R, tc), lambda i: (0, 0, i))] * L,
        out_specs=pl.BlockSpec(out_blk, lambda i: (0, 0, i)), out_shape=SDS(out, F32),
        compiler_params=_params("parallel"),
    )(*layers)


ANY = pl.BlockSpec(memory_space=pl.ANY)


def _mesh_pos():
    return lax.axis_index("x"), lax.axis_index("y"), lax.axis_index("c")


def _other_chips(x, y):
    return [(1 - x, y), (x, 1 - y), (1 - x, 1 - y)]


class _ChipExchange:
    def __init__(self, mode, sources):
        assert mode in ("gather", "scatter")
        self.mode, self.sources = mode, tuple(sources)
        self.n = len(self.sources)
        self.in_specs = [ANY] * self.n
        self.out_specs = [ANY] * self.n
        self.out_shape = [SDS(((N_CHIPS,) + s.shape) if mode == "gather" else s.shape, s.dtype) for s in self.sources]
        self.scratch = [pltpu.SemaphoreType.DMA((3 * self.n,)), pltpu.SemaphoreType.DMA((3 * self.n,)),
                        pltpu.SemaphoreType.DMA((self.n,))]

    def _copies(self, srcs, dsts, send_sems, recv_sems, local_sems):
        x, y, c = _mesh_pos()
        me = 2 * x + y
        view = (lambda r, chip: r) if self.mode == "gather" else (lambda r, chip: r.at[chip])
        local = [pltpu.make_async_copy(view(s, me), d.at[me], local_sems.at[a]) for a, (s, d) in enumerate(zip(srcs, dsts))]
        sends, recvs = [], []
        for j, (px, py) in enumerate(_other_chips(x, y)):
            peer = 2 * px + py
            for a, (s, d) in enumerate(zip(srcs, dsts)):
                sems = dict(send_sem=send_sems.at[self.n * j + a], recv_sem=recv_sems.at[self.n * j + a],
                            device_id=(px, py, c), device_id_type=MESH_ID)
                sends.append(pltpu.make_async_remote_copy(src_ref=view(s, peer), dst_ref=d.at[me], **sems))
                recvs.append(pltpu.make_async_remote_copy(src_ref=view(s, me), dst_ref=d.at[peer], **sems))
        return local, sends, recvs

    def start(self, srcs, dsts, sems):
        local, sends, _ = self._copies(srcs, dsts, *sems)
        for cp in local + sends:
            cp.start()

    def wait(self, srcs, dsts, sems):
        local, sends, recvs = self._copies(srcs, dsts, *sems)
        for cp in recvs:
            cp.wait_recv()
        for cp in sends:
            cp.wait_send()
        for cp in local:
            cp.wait()


def _gather_halves(w, tag):
    R, C = w.shape
    H = C // 2

    def body(w_ref, g_ref, send_sems, recv_sems, pass_send, pass_recv, local_sem):
        x, y, c = _mesh_pos()
        me = 2 * x + y
        mine, theirs = pl.ds(pl.multiple_of(c * H, H), H), pl.ds(pl.multiple_of((1 - c) * H, H), H)
        own = pltpu.make_async_copy(w_ref, g_ref.at[me], local_sem)
        own.start()

        def fetch(j, px, py, src, dst):
            return pltpu.make_async_remote_copy(src_ref=src, dst_ref=dst, send_sem=send_sems.at[j], recv_sem=recv_sems.at[j],
                                                device_id=(px, py, c), device_id_type=MESH_ID)

        def hand(j, cols, peer):
            return pltpu.make_async_remote_copy(src_ref=g_ref.at[peer, :, cols], dst_ref=g_ref.at[peer, :, cols],
                                                send_sem=pass_send.at[j], recv_sem=pass_recv.at[j],
                                                device_id=(x, y, 1 - c), device_id_type=MESH_ID)

        chips = _other_chips(x, y)
        sends = [fetch(j, px, py, w_ref.at[:, mine], g_ref.at[me, :, mine]) for j, (px, py) in enumerate(chips)]
        for cp in sends:
            cp.start()
        passed = []
        for j, (px, py) in enumerate(chips):
            peer = 2 * px + py
            fetch(j, px, py, w_ref.at[:, mine], g_ref.at[peer, :, mine]).wait_recv()
            passed.append(hand(j, mine, peer))
            passed[-1].start()
        for j, (px, py) in enumerate(chips):
            hand(j, theirs, 2 * px + py).wait_recv()
        for cp in sends + passed:
            cp.wait_send()
        own.wait()

    return pl.pallas_call(
        body, name=f"gather_halves_{tag}", in_specs=[ANY], out_specs=ANY, out_shape=SDS((N_CHIPS, R, C), w.dtype),
        scratch_shapes=[pltpu.SemaphoreType.DMA((3,)), pltpu.SemaphoreType.DMA((3,)), pltpu.SemaphoreType.DMA((3,)),
                        pltpu.SemaphoreType.DMA((3,)), pltpu.SemaphoreType.DMA],
        compiler_params=pltpu.CompilerParams(has_side_effects=True),
    )(w)


class _DeviceGather:
    def __init__(self, source):
        self.sources, self.n = (source,), 1
        self.in_specs, self.out_specs = [ANY], [ANY]
        self.out_shape = [SDS((N_DEV,) + source.shape, source.dtype)]
        self.scratch = [pltpu.SemaphoreType.DMA((N_DEV - 1,)), pltpu.SemaphoreType.DMA((N_DEV - 1,)),
                        pltpu.SemaphoreType.DMA((1,))]

    def _copies(self, srcs, dsts, send_sems, recv_sems, local_sems):
        (src,), (dst,) = srcs, dsts
        x, y, c = _mesh_pos()
        me = 4 * x + 2 * y + c
        local = [pltpu.make_async_copy(src, dst.at[me], local_sems.at[0])]
        sends, recvs = [], []
        for k in range(1, N_DEV):
            px, py, pc = (1 - x) if k & 4 else x, (1 - y) if k & 2 else y, (1 - c) if k & 1 else c
            sems = dict(send_sem=send_sems.at[k - 1], recv_sem=recv_sems.at[k - 1], device_id=(px, py, pc),
                        device_id_type=MESH_ID)
            sends.append(pltpu.make_async_remote_copy(src_ref=src, dst_ref=dst.at[me], **sems))
            recvs.append(pltpu.make_async_remote_copy(src_ref=src, dst_ref=dst.at[4 * px + 2 * py + pc], **sems))
        return local, sends, recvs

    start = _ChipExchange.start
    wait = _ChipExchange.wait


class _Rides:
    def __init__(self, *rides):
        self.rides = rides
        self.n = sum(r.n for r in rides)
        self.sources = tuple(s for r in rides for s in r.sources)
        self.in_specs, self.out_specs = [ANY] * self.n, [ANY] * self.n
        self.out_shape = [s for r in rides for s in r.out_shape]
        self.scratch = [s for r in rides for s in r.scratch]

    def _each(self, srcs, dsts, sems):
        a = b = 0
        for r in self.rides:
            yield r, srcs[a:a + r.n], dsts[a:a + r.n], sems[b:b + len(r.scratch)]
            a, b = a + r.n, b + len(r.scratch)

    def start(self, srcs, dsts, sems):
        for r, s, d, m in self._each(srcs, dsts, sems):
            r.start(s, d, m)

    def wait(self, srcs, dsts, sems):
        for r, s, d, m in self._each(srcs, dsts, sems):
            r.wait(s, d, m)


def _gather_devices(a, name):
    ex = _DeviceGather(a)

    def body(a_ref, g_ref, *sems):
        ex.start((a_ref,), (g_ref,), sems)
        ex.wait((a_ref,), (g_ref,), sems)

    return pl.pallas_call(
        body, name=name, in_specs=ex.in_specs, out_specs=ex.out_specs[0], out_shape=ex.out_shape[0],
        scratch_shapes=ex.scratch, compiler_params=pltpu.CompilerParams(has_side_effects=True),
    )(a)


def _swap_cores(pin, pout):
    def body(pin_ref, pout_ref, oin_ref, oout_ref, send_sems, recv_sems):
        x, y, c = _mesh_pos()
        cps = [pltpu.make_async_remote_copy(src_ref=src, dst_ref=dst, send_sem=send_sems.at[a], recv_sem=recv_sems.at[a],
                                            device_id=(x, y, 1 - c), device_id_type=MESH_ID)
               for a, (src, dst) in enumerate(((pin_ref, oin_ref), (pout_ref, oout_ref)))]
        for cp in cps:
            cp.start()
        for cp in cps:
            cp.wait()

    return pl.pallas_call(
        body, name="swap_cores", in_specs=[ANY, ANY], out_specs=[ANY, ANY],
        out_shape=[SDS(pin.shape, F32), SDS(pout.shape, F32)],
        scratch_shapes=[pltpu.SemaphoreType.DMA((2,)), pltpu.SemaphoreType.DMA((2,))],
        compiler_params=pltpu.CompilerParams(has_side_effects=True),
    )(pin, pout)


PACK_TILE = 8 * 128


def _pack_rows(size):
    return (size + PACK_TILE - 1) // PACK_TILE * 8


def _small_offsets():
    offs = [0]
    for _, shape in SMALL_PARAMS:
        offs.append(offs[-1] + _pack_rows(math.prod(shape)))
    return offs


def _layer_consts(l, gmlp_ln_g, gmlp_ln_b, gmlp_w_s, gmlp_b_s, hgrn_onorm_g, fox_b_f):
    causal = jnp.tril(jnp.ones((CHUNK, CHUNK), bool))
    wm = jnp.where(causal[None], gmlp_w_s[l], 0.0)
    return dict(
        lng=gmlp_ln_g[l].reshape(1, A_WIDTH), lnb=gmlp_ln_b[l].reshape(1, A_WIDTH),
        wm=wm.astype(BF16), wmt=jnp.swapaxes(wm, 1, 2).astype(BF16),
        bst=jnp.pad(gmlp_b_s[l].T, ((0, 0), (0, 128 - A_GROUPS))),
        onorm=jnp.tile(hgrn_onorm_g[l], 4).reshape(1, B_WIDTH),
        bf=jnp.pad(fox_b_f[l], (0, 128 - C_HEADS)).reshape(1, 128),
    )


def kernel(x, norm_g, w_in, w_out, gmlp_ln_g, gmlp_ln_b, gmlp_w_s, gmlp_b_s, hgrn_lb, hgrn_onorm_g, fox_b_f, final_norm_g, loss_target, m_norm_g, m_w_in, m_w_out, m_gmlp_ln_g, m_gmlp_ln_b, m_gmlp_w_s, m_gmlp_b_s, m_hgrn_lb, m_hgrn_onorm_g, m_fox_b_f, m_final_norm_g, v_norm_g, v_w_in, v_w_out, v_gmlp_ln_g, v_gmlp_ln_b, v_gmlp_w_s, v_gmlp_b_s, v_hgrn_lb, v_hgrn_onorm_g, v_fox_b_f, v_final_norm_g):
    T = x.shape[1]
    shard_in = w_in.shape[2]
    shard_out = w_out.shape[1]
    xs = x.reshape(T, D_MODEL)
    tgt = loss_target.reshape(T, D_MODEL)

    w_in_b = [w_in[l].T.astype(BF16) for l in range(DEPTH)]
    w_out_b = w_out.astype(BF16)

    lb_all = _lb_fwd(hgrn_lb)
    consts = [_layer_consts(l, gmlp_ln_g, gmlp_ln_b, gmlp_w_s, gmlp_b_s, hgrn_onorm_g, fox_b_f) for l in range(DEPTH)]

    saved = []
    xl = xs
    w_in_l = _gather_halves(w_in_b[0], "w_in_l0")
    for l in range(DEPTH):
        cs = consts[l]
        tag = f"l{l}"
        h, proj = _inproj(xl, norm_g[l].reshape(1, D_MODEL), w_in_l, D_IN_PAD, tag)
        (ya,), (yb, ob, s0), (qt, kt, vt) = _run_parts(
            [_gmlp_fwd(proj, cs["lng"], cs["lnb"], cs["wm"], cs["bst"]),
             _hgrn_fwd(proj, lb_all[l].reshape(1, B_WIDTH), cs["onorm"]), _fox_prep(proj, cs["bf"])],
            (T // CHUNK,), f"mix_fwd_{tag}")
        ride = _ChipExchange("gather", (w_out_b[l],) + ((w_in_b[l + 1],) if l + 1 < DEPTH else ()))
        oc, lse, yc, *gathered = _fox_fwd(qt, kt, vt, proj, tag, ride)
        w_out_l = gathered[0].reshape(N_CHIPS * shard_out, D_MODEL)
        saved.append(dict(x=xl, h=h, proj=proj, ya=ya, yb=yb, yc=yc, ob=ob, s0=s0, qt=qt, kt=kt, oc=oc, lse=lse,
                          w_in=w_in_l, w_out=w_out_l))
        xl = _outproj(xl, ya, yb, yc, w_out_l, tag)
        if l + 1 < DEPTH:
            w_in_l = gathered[1]

    dx, loss_part, d_final = _loss_head(xl, final_norm_g.reshape(1, D_MODEL), tgt)

    g_small = {}
    dlb_rows, rin, rout = [None] * DEPTH, [None] * DEPTH, [None] * DEPTH
    slabs_in = None
    for l in reversed(range(DEPTH)):
        cs, sv = consts[l], saved[l]
        tag = f"l{l}"
        proj = sv["proj"]
        dy, dw_out = _outproj_bwd(dx, sv["ya"], sv["yb"], sv["yc"], sv["w_out"], tag)
        (da, dwm, dbst, dlng, dlnb), (db, dlb_rows[l], donorm) = _run_parts(
            [_gmlp_bwd(proj, dy, cs["lng"], cs["lnb"], cs["wm"], cs["wmt"], cs["bst"]),
             _hgrn_bwd(proj, dy, sv["ob"], sv["s0"], lb_all[l].reshape(1, B_WIDTH), cs["onorm"])],
            (T // CHUNK,), f"mix_bwd_{tag}")
        do, delta, dzc, dot, qtr = _fox_bwd_prep(proj, dy, sv["oc"], sv["qt"], tag)
        slabs_out = dw_out.reshape(N_CHIPS, shard_out, D_MODEL).astype(BF16)
        ride = _ChipExchange("scatter", (slabs_out,) + ((slabs_in,) if slabs_in is not None else ()))
        g_small[l] = dict(ln_g=dlng, ln_b=dlnb, w_s=dwm, b_s=dbst, onorm=donorm)
        if l == 0:
            d_hgrn_lb = _lb_bwd(hgrn_lb, jnp.concatenate(dlb_rows, axis=0))
            per_layer = lambda key: [g_small[k][key] for k in range(DEPTH)]
            dbf_known = [jnp.zeros((1, 128), F32)] + [g_small[k]["bf"] for k in range(1, DEPTH)]
            early = _pack_grads(per_layer("ln_g"), per_layer("ln_b"), per_layer("w_s"), per_layer("b_s"), d_hgrn_lb,
                                per_layer("onorm"), dbf_known, d_final, loss_part)
            ride = _Rides(ride, _DeviceGather(early))
        dqt, dkt, dvc, *received = _fox_bwd(sv["qt"], sv["kt"], proj, do, sv["lse"], delta, dot, qtr, tag, ride)
        rout[l] = received[0]
        if slabs_in is not None:
            rin[l + 1] = received[1]
        if l == 0:
            rearly = received[-1]
        dqc, dkc, dflc, g_small[l]["bf"] = _fox_bwd_post(dqt, dkt, proj, cs["bf"], tag)
        dproj = [da, db, dqc, dkc, dvc, dzc, dflc]
        if l == 0:
            ride, parts = None, []
            for n, cols in enumerate(DW_IN_GROUPS):
                part, *arrived = _dw_in(sv["h"], dproj, D_IN_PAD, shard_in, cols, f"{tag}_{n}", ride)
                parts += arrived
                ride = _ChipExchange("scatter", (part,))
        else:
            slabs_in, = _dw_in(sv["h"], dproj, D_IN_PAD, shard_in, (0, D_MODEL), tag)
            ride = None
        dx, dng, *received = _dx_in(sv["x"], norm_g[l].reshape(1, D_MODEL), dx, dproj, sv["w_in"], tag, ride)
        if l == 0:
            rin[0] = jnp.concatenate(parts + received, axis=2)
        g_small[l]["norm_g"] = dng.reshape(D_MODEL // 128, 128)
    grad_x = dx.reshape(x.shape)
    dbf0 = jnp.where(_lane((1, 128)) < C_HEADS, g_small[0]["bf"], 0.0)
    late = jnp.concatenate([g_small[l]["norm_g"] for l in range(DEPTH)] + [jnp.pad(dbf0, ((0, 7), (0, 0)))])
    rlate = _gather_devices(late, "gather_late_grads")
    late_blocks = {0: 0, 7: DEPTH * D_MODEL // 128}

    pin, pout = _sum_chips(rin, "sum_chips_w_in", False), _sum_chips(rout, "sum_chips_w_out", True)
    oin, oout = _swap_cores(pin, pout)
    to_view = lambda a: jnp.transpose(a, (2, 0, 1))
    g_w_in, d_w_in, nm_w_in, nv_w_in = [
        jnp.transpose(o, (1, 2, 0))
        for o in _adamw_pair(to_view(w_in), to_view(m_w_in), to_view(v_w_in), pin, oin, "adamw_w_in")]
    g_w_out, d_w_out, nm_w_out, nv_w_out = _adamw_pair(w_out, m_w_out, v_w_out, pout, oout, "adamw_w_out")

    small_w = [norm_g, gmlp_ln_g, gmlp_ln_b, gmlp_w_s, gmlp_b_s, hgrn_lb, hgrn_onorm_g, fox_b_f, final_norm_g]
    small_m = [m_norm_g, m_gmlp_ln_g, m_gmlp_ln_b, m_gmlp_w_s, m_gmlp_b_s, m_hgrn_lb, m_hgrn_onorm_g, m_fox_b_f, m_final_norm_g]
    small_v = [v_norm_g, v_gmlp_ln_g, v_gmlp_ln_b, v_gmlp_w_s, v_gmlp_b_s, v_hgrn_lb, v_hgrn_onorm_g, v_fox_b_f, v_final_norm_g]
    views = lambda ps: [p.reshape(shape) for p, (shape, _) in zip(ps, _small_layout())]
    per_param, loss_row = _adamw_small(views(small_w), views(small_m), views(small_v), rearly, rlate, late_blocks)
    sg, sd, sm, sv_ = [[per_param[k][a].reshape(shape) for k, (_, shape) in enumerate(SMALL_PARAMS)] for a in range(4)]
    loss = loss_row[0, 0]

    def order(big_in, big_out, small):
        return [small[0], big_in, big_out] + small[1:]

    return (loss, grad_x, *order(g_w_in, g_w_out, sg), *order(d_w_in, d_w_out, sd), *order(nm_w_in, nm_w_out, sm),
            *order(nv_w_in, nv_w_out, sv_))
```

```python
import collections
import functools
import math

import jax
import jax.numpy as jnp
from jax import lax
from jax.experimental import pallas as pl
from jax.experimental.pallas import tpu as pltpu

F32 = jnp.float32
BF16 = jnp.bfloat16
SDS = jax.ShapeDtypeStruct
MESH_ID = pl.DeviceIdType.MESH

D_MODEL = 1024
DEPTH = 2
A_WIDTH = 256
A_GROUPS = 4
B_WIDTH = 256
C_WIDTH = 512
C_HEADS = 8
D_IN = 3848
D_IN_PAD = 4096
CHUNK = 128
SUB = 16
SUB_SHIFT = 4
NORM_EPS = 1e-6
F_FLOOR = 1e-30
COL_AU, COL_AV, COL_AZ = 0, 256, 512
COL_BQ, COL_BF, COL_BI, COL_BZ = 768, 1024, 1280, 1536
COL_CQ, COL_CK, COL_CV, COL_CZ, COL_CF = 1792, 2304, 2816, 3328, 3840
HEAD_LANES = 128
Q_SCALE = 0.125
ADAM_LR, ADAM_B1, ADAM_B2, ADAM_EPS, ADAM_WD, ADAM_STEP = 0.001, 0.9, 0.999, 1e-08, 0.01, 10
ADAM_C1 = 1.0 - ADAM_B1 ** ADAM_STEP
ADAM_C2 = 1.0 - ADAM_B2 ** ADAM_STEP
VMEM_LIMIT = 56 * 1024 * 1024
ADAMW_BLOCK_BYTES = 1 << 20
N_CHIPS = 4
N_DEV = 8

SMALL_PARAMS = (
    ("norm_g", (DEPTH, D_MODEL)), ("gmlp_ln_g", (DEPTH, 4, 64)), ("gmlp_ln_b", (DEPTH, 4, 64)),
    ("gmlp_w_s", (DEPTH, 4, 128, 128)), ("gmlp_b_s", (DEPTH, 4, 128)), ("hgrn_lb", (DEPTH, 256)),
    ("hgrn_onorm_g", (DEPTH, 64)), ("fox_b_f", (DEPTH, 8)), ("final_norm_g", (D_MODEL,)),
)


def _tile(n, pref):
    t = min(n, pref)
    assert n % t == 0, (n, pref)
    return t


def _params(*sem):
    return pltpu.CompilerParams(dimension_semantics=sem, vmem_limit_bytes=VMEM_LIMIT)


_Part = collections.namedtuple("_Part", "body operands in_specs out_specs out_shape scratch")


def _run_parts(parts, grid, name):
    counts = [(len(p.operands), len(p.out_shape), len(p.scratch)) for p in parts]

    def body(*refs):
        ins, outs, scr = [], [], []
        pos = 0
        for group, k in ((ins, 0), (outs, 1), (scr, 2)):
            for c in counts:
                group.append(refs[pos:pos + c[k]])
                pos += c[k]
        for p, i, o, s in zip(parts, ins, outs, scr):
            p.body(*i, *o, *s)

    flat = lambda key: [x for p in parts for x in getattr(p, key)]
    res = pl.pallas_call(
        body, name=name, grid=grid, in_specs=flat("in_specs"), out_specs=flat("out_specs"), out_shape=flat("out_shape"),
        scratch_shapes=flat("scratch"), compiler_params=_params(*(("arbitrary",) * len(grid))),
    )(*flat("operands"))
    out, pos = [], 0
    for c in counts:
        out.append(list(res[pos:pos + c[1]]))
        pos += c[1]
    return out


def _dot(a, b):
    return jnp.dot(a, b, preferred_element_type=F32)


def _dot_nt(a, b):
    return lax.dot_general(a, b, (((1,), (1,)), ((), ())), preferred_element_type=F32)


def _dot_tn(a, b):
    return lax.dot_general(a, b, (((0,), (0,)), ((), ())), preferred_element_type=F32)


def _split3(x):
    hi = x.astype(BF16)
    r = x - hi.astype(F32)
    mid = r.astype(BF16)
    lo = (r - mid.astype(F32)).astype(BF16)
    return hi, mid, lo


def _dot3_left(c, x):
    hi, mid, lo = _split3(x)
    return _dot(c, hi) + _dot(c, mid) + _dot(c, lo)


def _sigmoid(x):
    return jax.nn.sigmoid(x)


def _silu_and_grad(x):
    s = _sigmoid(x)
    return x * s, s * (1.0 + x * (1.0 - s))


_GELU_C = math.sqrt(2.0 / math.pi)


def _gelu_and_grad(x):
    inner = _GELU_C * (x + 0.044715 * x * x * x)
    t = jnp.tanh(inner)
    y = 0.5 * x * (1.0 + t)
    dy = 0.5 * (1.0 + t) + 0.5 * x * (1.0 - t * t) * _GELU_C * (1.0 + 3.0 * 0.044715 * x * x)
    return y, dy


def _lane(shape):
    return lax.broadcasted_iota(jnp.int32, shape, 1)


def _row(shape):
    return lax.broadcasted_iota(jnp.int32, shape, 0)


def _gsum64(x):
    lo = _lane(x.shape) < 64
    s0 = jnp.sum(jnp.where(lo, x, 0.0), axis=-1, keepdims=True)
    s1 = jnp.sum(jnp.where(lo, 0.0, x), axis=-1, keepdims=True)
    return jnp.where(lo, s0, s1)


def _colreduce(x, op):
    parts = [x[r:r + 8, :] for r in range(0, x.shape[0], 8)]
    while len(parts) > 1:
        pairs = [op(parts[k], parts[k + 1]) for k in range(0, len(parts) - 1, 2)]
        parts = pairs + ([parts[-1]] if len(parts) % 2 else [])
    red = jnp.max if op is jnp.maximum else jnp.sum
    return red(parts[0], axis=0, keepdims=True)


def _block_diag64(dtype=BF16):
    r, c = _row((128, 128)), _lane((128, 128))
    return jnp.where((r >> 6) == (c >> 6), 1.0, 0.0).astype(dtype)


def _assemble_w_in(slab_ref, wt_ref):
    shard = slab_ref.shape[1]
    top = N_CHIPS * shard // 16 * 16
    wt_ref[top:, :] = jnp.zeros((wt_ref.shape[0] - top, wt_ref.shape[1]), wt_ref.dtype)
    for k in range(N_CHIPS):
        wt_ref[shard * k:shard * (k + 1), :] = slab_ref[k]


def _inproj(x, g, w, dp_width, tag):
    T, D = x.shape
    tm = _tile(T, 512)

    def body(x_ref, g_ref, w_ref, h_ref, p_ref, wt_ref):
        pl.when(pl.program_id(0) == 0)(lambda: _assemble_w_in(w_ref, wt_ref))
        xv = x_ref[...]
        r = lax.rsqrt(jnp.mean(xv * xv, axis=-1, keepdims=True) + NORM_EPS)
        h = (xv * r * g_ref[...]).astype(BF16)
        h_ref[...] = h
        p_ref[...] = _dot_nt(h, wt_ref[...])

    return pl.pallas_call(
        body, name=f"inproj_{tag}", grid=(T // tm,),
        in_specs=[pl.BlockSpec((tm, D), lambda i: (i, 0)), pl.BlockSpec((1, D), lambda i: (0, 0)),
                  pl.BlockSpec(w.shape, lambda i: (0, 0, 0))],
        out_specs=[pl.BlockSpec((tm, D), lambda i: (i, 0)), pl.BlockSpec((tm, dp_width), lambda i: (i, 0))],
        out_shape=[SDS((T, D), BF16), SDS((T, dp_width), F32)],
        scratch_shapes=[pltpu.VMEM((dp_width, D), BF16)],
        compiler_params=_params("arbitrary"),
    )(x, g, w)


def _outproj(x, ya, yb, yc, wo, tag):
    T, D = x.shape
    tm = _tile(T, 512)

    def body(x_ref, ya_ref, yb_ref, yc_ref, wo_ref, o_ref):
        acc = x_ref[...] + _dot(ya_ref[...], wo_ref[0:A_WIDTH, :])
        acc = acc + _dot(yb_ref[...], wo_ref[A_WIDTH:A_WIDTH + B_WIDTH, :])
        o_ref[...] = acc + _dot(yc_ref[...], wo_ref[A_WIDTH + B_WIDTH:, :])

    row = lambda w: pl.BlockSpec((tm, w), lambda i: (i, 0))
    return pl.pallas_call(
        body, name=f"outproj_{tag}", grid=(T // tm,),
        in_specs=[row(D), row(A_WIDTH), row(B_WIDTH), row(C_WIDTH), pl.BlockSpec(wo.shape, lambda i: (0, 0))],
        out_specs=row(D), out_shape=SDS((T, D), F32), compiler_params=_params("parallel"),
    )(x, ya, yb, yc, wo)


def _outproj_bwd(dx, ya, yb, yc, wo, tag):
    T, D = dx.shape
    DM = wo.shape[0]
    tm = _tile(T, 512)

    def body(dx_ref, ya_ref, yb_ref, yc_ref, wo_ref, dy_ref, dwo_ref):
        @pl.when(pl.program_id(0) == 0)
        def _():
            dwo_ref[...] = jnp.zeros_like(dwo_ref)

        dxb = dx_ref[...].astype(BF16)
        dy_ref[...] = _dot_nt(dxb, wo_ref[...])
        dwo_ref[0:A_WIDTH, :] += _dot_tn(ya_ref[...], dxb)
        dwo_ref[A_WIDTH:A_WIDTH + B_WIDTH, :] += _dot_tn(yb_ref[...], dxb)
        dwo_ref[A_WIDTH + B_WIDTH:, :] += _dot_tn(yc_ref[...], dxb)

    row = lambda w: pl.BlockSpec((tm, w), lambda i: (i, 0))
    return pl.pallas_call(
        body, name=f"outproj_bwd_{tag}", grid=(T // tm,),
        in_specs=[row(D), row(A_WIDTH), row(B_WIDTH), row(C_WIDTH), pl.BlockSpec(wo.shape, lambda i: (0, 0))],
        out_specs=[row(DM), pl.BlockSpec((DM, D), lambda i: (0, 0))],
        out_shape=[SDS((T, DM), F32), SDS((DM, D), F32)], compiler_params=_params("arbitrary"),
    )(dx, ya, yb, yc, wo)


DW_IN_GROUPS = ((0, 256), (256, 256), (512, 512))


def _piece_offsets(pieces):
    offs = [0]
    for p in pieces:
        offs.append(offs[-1] + p.shape[1])
    return offs


def _dw_in(h, pieces, dp_width, shard, cols, tag, ride=None):
    T = h.shape[0]
    first, D = cols
    assert N_CHIPS * shard <= dp_width and first % D == 0
    tm = _tile(T, 512)
    grid = (T // tm,)
    offs = _piece_offsets(pieces)
    n = len(pieces)

    def body(h_ref, *rest):
        p_refs, rest = rest[:n], rest[n:]
        ride_srcs, (dw_ref,), ride_dsts, (acc_ref,), ride_sems = _ride_refs(ride, rest, 1, 1)
        i = pl.program_id(0)
        _ride_start(ride, grid, ride_srcs, ride_dsts, ride_sems)

        @pl.when(i == 0)
        def _():
            acc_ref[...] = jnp.zeros_like(acc_ref)

        hv = h_ref[...]
        for k, p_ref in enumerate(p_refs):
            acc_ref[offs[k]:offs[k + 1], :] += _dot_tn(p_ref[...], hv)

        @pl.when(i == grid[0] - 1)
        def _():
            for k in range(N_CHIPS):
                dw_ref[k] = acc_ref[shard * k:shard * (k + 1), :].astype(BF16)

        _ride_wait(ride, grid, ride_srcs, ride_dsts, ride_sems)

    extra = ride or _ChipExchange("gather", ())
    return pl.pallas_call(
        body, name=f"dw_in_{tag}", grid=grid,
        in_specs=[pl.BlockSpec((tm, D), lambda i: (i, first // D))]
        + [pl.BlockSpec((tm, p.shape[1]), lambda i: (i, 0)) for p in pieces] + extra.in_specs,
        out_specs=[pl.BlockSpec((N_CHIPS, shard, D), lambda i: (0, 0, 0))] + extra.out_specs,
        out_shape=[SDS((N_CHIPS, shard, D), BF16)] + extra.out_shape,
        scratch_shapes=[pltpu.VMEM((dp_width, D), F32)] + (extra.scratch if ride else []),
        compiler_params=pltpu.CompilerParams(dimension_semantics=("arbitrary",), vmem_limit_bytes=VMEM_LIMIT,
                                             has_side_effects=bool(ride)),
    )(h, *pieces, *extra.sources)


def _dx_in(x, g, dres, pieces, w, tag, ride=None):
    T, D = x.shape
    tm = _tile(T, 512)
    grid = (T // tm,)
    offs = _piece_offsets(pieces)
    n = len(pieces)

    def body(x_ref, g_ref, dres_ref, w_ref, *rest):
        p_refs, rest = rest[:n], rest[n:]
        ride_srcs, (dx_ref, dg_ref), ride_dsts, (wt_ref,), ride_sems = _ride_refs(ride, rest, 2, 1)
        _ride_start(ride, grid, ride_srcs, ride_dsts, ride_sems)

        @pl.when(pl.program_id(0) == 0)
        def _():
            dg_ref[...] = jnp.zeros_like(dg_ref)
            _assemble_w_in(w_ref, wt_ref)

        dh = _dot(p_refs[0][...], wt_ref[offs[0]:offs[1], :])
        for k in range(1, n):
            dh = dh + _dot(p_refs[k][...], wt_ref[offs[k]:offs[k + 1], :])
        xv = x_ref[...]
        r = lax.rsqrt(jnp.mean(xv * xv, axis=-1, keepdims=True) + NORM_EPS)
        xh = xv * r
        dg_ref[...] += jnp.sum(dh * xh, axis=0, keepdims=True)
        dxh = dh * g_ref[...]
        dx_ref[...] = dres_ref[...] + r * (dxh - xh * jnp.mean(dxh * xh, axis=-1, keepdims=True))
        _ride_wait(ride, grid, ride_srcs, ride_dsts, ride_sems)

    extra = ride or _ChipExchange("gather", ())
    row = pl.BlockSpec((tm, D), lambda i: (i, 0))
    return pl.pallas_call(
        body, name=f"dx_in_{tag}", grid=grid,
        in_specs=[row, pl.BlockSpec((1, D), lambda i: (0, 0)), row, pl.BlockSpec(w.shape, lambda i: (0, 0, 0))]
        + [pl.BlockSpec((tm, p.shape[1]), lambda i: (i, 0)) for p in pieces] + extra.in_specs,
        out_specs=[row, pl.BlockSpec((1, D), lambda i: (0, 0))] + extra.out_specs,
        out_shape=[SDS((T, D), F32), SDS((1, D), F32)] + extra.out_shape,
        scratch_shapes=[pltpu.VMEM((offs[-1], D), BF16)] + (extra.scratch if ride else []),
        compiler_params=pltpu.CompilerParams(dimension_semantics=("arbitrary",), vmem_limit_bytes=VMEM_LIMIT,
                                             has_side_effects=bool(ride)),
    )(x, g, dres, w, *pieces, *extra.sources)


def _loss_head(x, g, tgt):
    T, D = x.shape
    tm = _tile(T, 512)

    def body(x_ref, g_ref, t_ref, dx_ref, loss_ref, dg_ref):
        @pl.when(pl.program_id(0) == 0)
        def _():
            loss_ref[...] = jnp.zeros_like(loss_ref)
            dg_ref[...] = jnp.zeros_like(dg_ref)

        xv = x_ref[...]
        r = lax.rsqrt(jnp.mean(xv * xv, axis=-1, keepdims=True) + NORM_EPS)
        xh = xv * r
        gv = g_ref[...]
        err = xh * gv - t_ref[...]
        tok = jnp.mean(err * err, axis=-1, keepdims=True)
        loss_ref[...] += 0.5 * jnp.sum(tok, axis=0, keepdims=True)
        dy = err * (1.0 / D)
        dg_ref[...] += jnp.sum(dy * xh, axis=0, keepdims=True)
        dxh = dy * gv
        dx_ref[...] = r * (dxh - xh * jnp.mean(dxh * xh, axis=-1, keepdims=True))

    row = pl.BlockSpec((tm, D), lambda i: (i, 0))
    return pl.pallas_call(
        body, name="loss_head", grid=(T // tm,),
        in_specs=[row, pl.BlockSpec((1, D), lambda i: (0, 0)), row],
        out_specs=[row, pl.BlockSpec((1, 128), lambda i: (0, 0)), pl.BlockSpec((1, D), lambda i: (0, 0))],
        out_shape=[SDS((T, D), F32), SDS((1, 128), F32), SDS((1, D), F32)], compiler_params=_params("arbitrary"),
    )(x, g, tgt)


def _gmlp_core(u, v, lng, lnb, wm_ref, bst_ref, pair):
    ug, dug = _gelu_and_grad(u)
    vg, dvg = _gelu_and_grad(v)
    mu = _gsum64(vg) * (1.0 / 64)
    d = vg - mu
    var = _gsum64(d * d) * (1.0 / 64)
    rstd = lax.rsqrt(var + NORM_EPS)
    xh = d * rstd
    vn = xh * lng + lnb
    vnb = vn.astype(BF16)
    lo = _lane(u.shape) < 64
    g0, g1 = 2 * pair, 2 * pair + 1
    mixed = jnp.where(lo, _dot(wm_ref[g0], vnb) + bst_ref[:, g0:g0 + 1], _dot(wm_ref[g1], vnb) + bst_ref[:, g1:g1 + 1])
    return ug, dug, dvg, rstd, xh, vnb, mixed, lo


def _gmlp_fwd(proj, lng, lnb, wm, bst):
    T = proj.shape[0]

    def body(u_ref, v_ref, z_ref, lng_ref, lnb_ref, wm_ref, bst_ref, y_ref):
        for pair in range(2):
            sl = slice(128 * pair, 128 * pair + 128)
            ug, _, _, _, _, _, mixed, _ = _gmlp_core(u_ref[:, sl], v_ref[:, sl], lng_ref[:, sl], lnb_ref[:, sl],
                                                     wm_ref, bst_ref, pair)
            sz, _ = _silu_and_grad(z_ref[:, sl])
            y_ref[:, sl] = (ug * mixed * sz).astype(BF16)

    col = lambda c: pl.BlockSpec((CHUNK, A_WIDTH), lambda i, c=c: (i, c // A_WIDTH))
    full = lambda a: pl.BlockSpec(a.shape, lambda i, n=a.ndim: (0,) * n)
    return _Part(body, (proj, proj, proj, lng, lnb, wm, bst),
                 [col(COL_AU), col(COL_AV), col(COL_AZ), full(lng), full(lnb), full(wm), full(bst)],
                 [pl.BlockSpec((CHUNK, A_WIDTH), lambda i: (i, 0))], [SDS((T, A_WIDTH), BF16)], [])


def _gmlp_bwd(proj, dy, lng, lnb, wm, wmt, bst):
    T = proj.shape[0]
    n = T // CHUNK

    def body(u_ref, v_ref, z_ref, dy_ref, lng_ref, lnb_ref, wm_ref, wmt_ref, bst_ref,
             da_ref, dwm_ref, dbst_ref, dlng_ref, dlnb_ref):
        @pl.when(pl.program_id(0) == 0)
        def _():
            dwm_ref[...] = jnp.zeros_like(dwm_ref)
            dbst_ref[...] = jnp.zeros_like(dbst_ref)
            dlng_ref[...] = jnp.zeros_like(dlng_ref)
            dlnb_ref[...] = jnp.zeros_like(dlnb_ref)

        lane = _lane((CHUNK, 128))
        dbst = dbst_ref[...]
        for pair in range(2):
            sl = slice(128 * pair, 128 * pair + 128)
            lng_p = lng_ref[:, sl]
            ug, dug, dvg, rstd, xh, vnb, mixed, lo = _gmlp_core(u_ref[:, sl], v_ref[:, sl], lng_p, lnb_ref[:, sl],
                                                                wm_ref, bst_ref, pair)
            sz, dsz = _silu_and_grad(z_ref[:, sl])
            dyv = dy_ref[:, sl]
            out = ug * mixed
            dz = dyv * out * dsz
            dout = dyv * sz
            du = dout * mixed * dug
            dmix = dout * ug
            g0, g1 = 2 * pair, 2 * pair + 1
            dm0 = jnp.where(lo, dmix, 0.0)
            dm1 = jnp.where(lo, 0.0, dmix)
            dbst = dbst + jnp.where(lane == g0, jnp.sum(dm0, axis=-1, keepdims=True), 0.0)
            dbst = dbst + jnp.where(lane == g1, jnp.sum(dm1, axis=-1, keepdims=True), 0.0)
            dwm_ref[g0] += _dot_nt(dm0.astype(BF16), vnb)
            dwm_ref[g1] += _dot_nt(dm1.astype(BF16), vnb)
            dmb = dmix.astype(BF16)
            dvn = jnp.where(lo, _dot(wmt_ref[g0], dmb), _dot(wmt_ref[g1], dmb))
            dlng_ref[:, sl] += jnp.sum(dvn * xh, axis=0, keepdims=True)
            dlnb_ref[:, sl] += jnp.sum(dvn, axis=0, keepdims=True)
            dxh = dvn * lng_p
            m1 = _gsum64(dxh) * (1.0 / 64)
            m2 = _gsum64(dxh * xh) * (1.0 / 64)
            dv = rstd * (dxh - m1 - xh * m2) * dvg
            da_ref[:, COL_AU + 128 * pair:COL_AU + 128 * pair + 128] = du.astype(BF16)
            da_ref[:, COL_AV + 128 * pair:COL_AV + 128 * pair + 128] = dv.astype(BF16)
            da_ref[:, COL_AZ + 128 * pair:COL_AZ + 128 * pair + 128] = dz.astype(BF16)
        dbst_ref[...] = dbst

        @pl.when(pl.program_id(0) == n - 1)
        def _():
            causal = _lane((CHUNK, CHUNK)) <= _row((CHUNK, CHUNK))
            for g in range(A_GROUPS):
                dwm_ref[g] = jnp.where(causal, dwm_ref[g], 0.0)

    col = lambda c: pl.BlockSpec((CHUNK, A_WIDTH), lambda i, c=c: (i, c // A_WIDTH))
    full = lambda a: pl.BlockSpec(a.shape, lambda i, n=a.ndim: (0,) * n)
    acc = lambda s: pl.BlockSpec(s, lambda i, n=len(s): (0,) * n)
    return _Part(body, (proj, proj, proj, dy, lng, lnb, wm, wmt, bst),
                 [col(COL_AU), col(COL_AV), col(COL_AZ), pl.BlockSpec((CHUNK, A_WIDTH), lambda i: (i, 0)),
                  full(lng), full(lnb), full(wm), full(wmt), full(bst)],
                 [pl.BlockSpec((CHUNK, 3 * A_WIDTH), lambda i: (i, 0)), acc((A_GROUPS, CHUNK, CHUNK)),
                  acc((CHUNK, 128)), acc((1, A_WIDTH)), acc((1, A_WIDTH))],
                 [SDS((T, 3 * A_WIDTH), BF16), SDS((A_GROUPS, CHUNK, CHUNK), F32), SDS((CHUNK, 128), F32),
                  SDS((1, A_WIDTH), F32), SDS((1, A_WIDTH), F32)], [])


def _hgrn_consts():
    r, c = _row((CHUNK, CHUNK)), _lane((CHUNK, CHUNK))
    same = (r >> SUB_SHIFT) == (c >> SUB_SHIFT)
    lsub = jnp.where(same & (c <= r), 1.0, 0.0).astype(BF16)
    usub = jnp.where(same & (c >= r), 1.0, 0.0).astype(BF16)
    bsub = jnp.where(same, 1.0, 0.0).astype(BF16)
    return lsub, usub, bsub


def _hgrn_gates(qv, zf, lbp):
    sq, dsq = _silu_and_grad(qv)
    qt = sq * Q_SCALE
    sg = _sigmoid(zf)
    sgn = _sigmoid(-zf)
    f = lbp + (1.0 - lbp) * sg
    g = jnp.log(jnp.maximum(f, F_FLOOR))
    kf = (1.0 - lbp) * sgn
    return qt, dsq, sg, sgn, f, g, kf


def _hgrn_intra_scores(qt, kf, b, mbd):
    rid = _row((SUB, 128))
    parts = []
    for s in range(SUB):
        e = jnp.exp(b - b[s:s + 1, :])
        parts.append(jnp.where(rid >= s, qt * kf[s:s + 1, :] * e, 0.0))
    return _dot(jnp.concatenate(parts, axis=0).astype(BF16), mbd)


def _hgrn_intra_out(a, v):
    o = jnp.zeros((SUB, 128), F32)
    for s in range(SUB):
        o = o + a[SUB * s:SUB * s + SUB, :] * v[s:s + 1, :]
    return o


def _hgrn_intra_bwd_scores(qt, kf, b, v, do, mbd):
    rid = _row((SUB, 128))
    ps, das, kes, es = [], [], [], []
    for s in range(SUB):
        e = jnp.where(rid >= s, jnp.exp(b - b[s:s + 1, :]), 0.0)
        ke = kf[s:s + 1, :] * e
        es.append(e)
        kes.append(ke)
        ps.append(qt * ke)
        das.append(do * v[s:s + 1, :])
    a = _dot(jnp.concatenate(ps, axis=0).astype(BF16), mbd)
    da = _dot(jnp.concatenate(das, axis=0).astype(BF16), mbd)
    return a, da, kes, es


def _hgrn_intra_bwd_grads(scores, qt, do, rsum):
    a, da, kes, es = scores
    dqt = jnp.zeros((SUB, 128), F32)
    xs, ys = [], []
    for s in range(SUB):
        da_s = da[SUB * s:SUB * s + SUB, :]
        dqt = dqt + da_s * kes[s]
        xs.append(a[SUB * s:SUB * s + SUB, :] * do)
        ys.append(da_s * qt * es[s])
    dv = _dot(rsum, jnp.concatenate(xs, axis=0).astype(BF16))
    dkf = _dot(rsum, jnp.concatenate(ys, axis=0).astype(BF16))
    return dqt, dkf, dv


def _hgrn_norm_gate(o, z, onorm):
    ms = _gsum64(o * o) * (1.0 / 64)
    r = lax.rsqrt(ms + NORM_EPS)
    xh = o * r
    sz, dsz = _silu_and_grad(z)
    return xh, r, sz, dsz, xh * onorm


def _hgrn_fwd(proj, lb, onorm):
    T = proj.shape[0]
    n = T // CHUNK
    nsub = CHUNK // SUB

    def body(q_ref, f_ref, i_ref, z_ref, lb_ref, on_ref, y_ref, o_ref, s0_ref, st_ref):
        @pl.when(pl.program_id(0) == 0)
        def _():
            st_ref[...] = jnp.zeros_like(st_ref)

        lsub, _, bsub = _hgrn_consts()
        mbd = _block_diag64()
        bdmask = mbd > 0
        rid = _row((CHUNK, 128))
        subs = [slice(SUB * sub, SUB * sub + SUB) for sub in range(nsub)]
        work = []
        for pair in range(2):
            sl = slice(128 * pair, 128 * pair + 128)
            qt, _, _, _, _, g, kf = _hgrn_gates(q_ref[:, sl], f_ref[:, sl], lb_ref[:, sl])
            work.append(dict(sl=sl, qt=qt, kf=kf, v=i_ref[:, sl], b=_dot3_left(lsub, g), bl=_dot3_left(bsub, g)))
        for w in work:
            qt, kf, v, b, bl = w["qt"], w["kf"], w["v"], w["b"], w["bl"]
            w["qh"] = (qt * jnp.exp(b)).astype(BF16)
            kh = kf * jnp.exp(bl - b)
            w["dec"] = jnp.exp(bl)
            vtb = v.T.astype(BF16)
            w["scores"] = [_hgrn_intra_scores(qt[rs], kf[rs], b[rs], mbd) for rs in subs]
            w["adds"] = [_dot(vtb, jnp.where((rid >> SUB_SHIFT) == sub, kh, 0.0).astype(BF16)) for sub in range(nsub)]
        for pair, w in enumerate(work):
            w["st"] = st_ref[pair]
            s0_ref[0, pair] = w["st"]
            w["outs"] = []
        for sub, rs in enumerate(subs):
            for w in work:
                w["outs"].append(_dot_nt(w["qh"][rs], w["st"].astype(BF16)) + _hgrn_intra_out(w["scores"][sub], w["v"][rs]))
                w["st"] = jnp.where(bdmask, w["st"] * w["dec"][SUB * sub:SUB * sub + 1, :] + w["adds"][sub], 0.0)
        for pair, w in enumerate(work):
            sl = w["sl"]
            st_ref[pair] = w["st"]
            o = jnp.concatenate(w["outs"], axis=0)
            o_ref[:, sl] = o
            _, _, sz, _, on = _hgrn_norm_gate(o, z_ref[:, sl], on_ref[:, sl])
            y_ref[:, sl] = (on * sz).astype(BF16)

    col = lambda c: pl.BlockSpec((CHUNK, B_WIDTH), lambda i, c=c: (i, c // B_WIDTH))
    full = lambda a: pl.BlockSpec(a.shape, lambda i, n=a.ndim: (0,) * n)
    return _Part(body, (proj, proj, proj, proj, lb, onorm),
                 [col(COL_BQ), col(COL_BF), col(COL_BI), col(COL_BZ), full(lb), full(onorm)],
                 [pl.BlockSpec((CHUNK, B_WIDTH), lambda i: (i, 0)), pl.BlockSpec((CHUNK, B_WIDTH), lambda i: (i, 0)),
                  pl.BlockSpec((1, 2, 128, 128), lambda i: (i, 0, 0, 0))],
                 [SDS((T, B_WIDTH), BF16), SDS((T, B_WIDTH), F32), SDS((n, 2, 128, 128), F32)],
                 [pltpu.VMEM((2, 128, 128), F32)])


def _hgrn_bwd(proj, dy, o_saved, s0, lb, onorm):
    T = proj.shape[0]
    n = T // CHUNK
    nsub = CHUNK // SUB

    def body(q_ref, f_ref, i_ref, z_ref, dy_ref, o_ref, s0_ref, lb_ref, on_ref,
             db_ref, dlb_ref, don_ref, dst_ref, sts_ref):
        @pl.when(pl.program_id(0) == 0)
        def _():
            dst_ref[...] = jnp.zeros_like(dst_ref)
            dlb_ref[...] = jnp.zeros_like(dlb_ref)
            don_ref[...] = jnp.zeros_like(don_ref)

        lsub, usub, bsub = _hgrn_consts()
        mbd = _block_diag64()
        bdmask = mbd > 0
        rsum = jnp.where((_lane((SUB, SUB * SUB)) >> SUB_SHIFT) == _row((SUB, SUB * SUB)), 1.0, 0.0).astype(BF16)
        subs = [slice(SUB * sub, SUB * sub + SUB) for sub in range(nsub)]
        work = []
        for pair in range(2):
            sl = slice(128 * pair, 128 * pair + 128)
            lbp = lb_ref[:, sl]
            qt, dsq, sg, sgn, f, g, kf = _hgrn_gates(q_ref[:, sl], f_ref[:, sl], lbp)
            w = dict(sl=sl, lbp=lbp, qt=qt, dsq=dsq, sg=sg, sgn=sgn, f=f, kf=kf, v=i_ref[:, sl],
                     b=_dot3_left(lsub, g), bl=_dot3_left(bsub, g))
            onp = on_ref[:, sl]
            xh, r, sz, dsz, on = _hgrn_norm_gate(o_ref[:, sl], z_ref[:, sl], onp)
            dyv = dy_ref[:, sl]
            w["dz"] = dyv * on * dsz
            don = dyv * sz
            cn = jnp.sum(don * xh, axis=0, keepdims=True)
            don_ref[...] += cn + pltpu.roll(cn, 64, axis=1)
            dxo = don * onp
            w["do"] = r * (dxo - xh * (_gsum64(dxo * xh) * (1.0 / 64)))
            work.append(w)
        for w in work:
            qt, kf, v, b, bl, do = w["qt"], w["kf"], w["v"], w["b"], w["bl"], w["do"]
            w["eb"] = jnp.exp(b)
            w["ekb"] = jnp.exp(bl - b)
            w["qhb"] = (qt * w["eb"]).astype(BF16)
            w["khb"] = (kf * w["ekb"]).astype(BF16)
            w["dec"] = jnp.exp(bl)
            w["vb"] = v.astype(BF16)
            w["dob"] = do.astype(BF16)
            w["scores"] = [_hgrn_intra_bwd_scores(qt[rs], kf[rs], b[rs], v[rs], do[rs], mbd) for rs in subs]
            w["st_adds"] = [_dot_tn(w["vb"][rs], w["khb"][rs]) for rs in subs]
            w["gst_adds"] = [_dot_tn(w["dob"][rs], w["qhb"][rs]) for rs in subs]
        for pair, w in enumerate(work):
            w["st"] = s0_ref[0, pair]
        for sub in range(nsub):
            for pair, w in enumerate(work):
                sts_ref[pair, sub] = w["st"]
                w["st"] = jnp.where(bdmask, w["st"] * w["dec"][SUB * sub:SUB * sub + 1, :] + w["st_adds"][sub], 0.0)
        for pair, w in enumerate(work):
            w["gst"] = dst_ref[pair]
            w["dqt_p"], w["dkf_p"], w["dv_p"], w["dbl_p"] = ([None] * nsub for _ in range(4))
        for sub in reversed(range(nsub)):
            rs = subs[sub]
            for pair, w in enumerate(work):
                gst = w["gst"]
                st_in = sts_ref[pair, sub]
                gb = gst.astype(BF16)
                dqh = _dot(w["dob"][rs], st_in.astype(BF16))
                dkh = _dot(w["vb"][rs], gb)
                dv_inter = _dot_nt(w["khb"][rs], gb)
                ddec = jnp.sum(gst * st_in, axis=0, keepdims=True)
                dec_row = w["dec"][SUB * sub:SUB * sub + 1, :]
                w["gst"] = jnp.where(bdmask, gst * dec_row + w["gst_adds"][sub], 0.0)
                dqt_i, dkf_i, dv_i = _hgrn_intra_bwd_grads(w["scores"][sub], w["qt"][rs], w["do"][rs], rsum)
                dkf_inter = dkh * w["ekb"][rs]
                w["dqt_p"][sub] = dqh * w["eb"][rs] + dqt_i
                w["dkf_p"][sub] = dkf_inter + dkf_i
                w["dv_p"][sub] = dv_inter + dv_i
                row = jnp.sum(w["kf"][rs] * dkf_inter, axis=0, keepdims=True) + ddec * dec_row
                w["dbl_p"][sub] = jnp.broadcast_to(row, (SUB, 128))
        for pair, w in enumerate(work):
            sl, lbp, sg, sgn, f = w["sl"], w["lbp"], w["sg"], w["sgn"], w["f"]
            dst_ref[pair] = w["gst"]
            dqt = jnp.concatenate(w["dqt_p"], axis=0)
            dkf = jnp.concatenate(w["dkf_p"], axis=0)
            dv = jnp.concatenate(w["dv_p"], axis=0)
            dg = _dot3_left(usub, w["qt"] * dqt - w["kf"] * dkf) + jnp.concatenate(w["dbl_p"], axis=0)
            df = jnp.where(f > F_FLOOR, dg / f, 0.0)
            dlb_ref[:, sl] += jnp.sum(df * (1.0 - sg) - dkf * sgn, axis=0, keepdims=True)
            dfl = (1.0 - lbp) * sg * sgn * (df - dkf)
            dq = dqt * Q_SCALE * w["dsq"]
            db_ref[:, 0 * B_WIDTH + 128 * pair:0 * B_WIDTH + 128 * pair + 128] = dq.astype(BF16)
            db_ref[:, 1 * B_WIDTH + 128 * pair:1 * B_WIDTH + 128 * pair + 128] = dfl.astype(BF16)
            db_ref[:, 2 * B_WIDTH + 128 * pair:2 * B_WIDTH + 128 * pair + 128] = dv.astype(BF16)
            db_ref[:, 3 * B_WIDTH + 128 * pair:3 * B_WIDTH + 128 * pair + 128] = w["dz"].astype(BF16)

    rev = lambda c: pl.BlockSpec((CHUNK, B_WIDTH), lambda i, c=c: (n - 1 - i, c // B_WIDTH))
    full = lambda a: pl.BlockSpec(a.shape, lambda i, n_=a.ndim: (0,) * n_)
    acc = lambda s: pl.BlockSpec(s, lambda i, n_=len(s): (0,) * n_)
    return _Part(body, (proj, proj, proj, proj, dy, o_saved, s0, lb, onorm),
                 [rev(COL_BQ), rev(COL_BF), rev(COL_BI), rev(COL_BZ),
                  pl.BlockSpec((CHUNK, B_WIDTH), lambda i: (n - 1 - i, 1)),
                  pl.BlockSpec((CHUNK, B_WIDTH), lambda i: (n - 1 - i, 0)),
                  pl.BlockSpec((1, 2, 128, 128), lambda i: (n - 1 - i, 0, 0, 0)), full(lb), full(onorm)],
                 [pl.BlockSpec((CHUNK, 4 * B_WIDTH), lambda i: (n - 1 - i, 0)), acc((1, B_WIDTH)), acc((1, 128))],
                 [SDS((T, 4 * B_WIDTH), BF16), SDS((1, B_WIDTH), F32), SDS((1, 128), F32)],
                 [pltpu.VMEM((2, 128, 128), F32), pltpu.VMEM((2, nsub, 128, 128), F32)])


def _lb_fwd(hgrn_lb):
    assert hgrn_lb.shape[0] == 2

    def body(x_ref, o_ref):
        x0, x1 = x_ref[0:1, :], x_ref[1:2, :]
        m = jnp.maximum(x0, x1)
        e0, e1 = jnp.exp(x0 - m), jnp.exp(x1 - m)
        p0, p1 = e0 / (e0 + e1), e1 / (e0 + e1)
        o_ref[0:1, :] = jnp.clip(p0 - p0, 0.0, 1.0 - 1e-6)
        o_ref[1:2, :] = jnp.clip((p0 + p1) - p0, 0.0, 1.0 - 1e-6)

    return pl.pallas_call(body, name="lb_fwd", out_shape=SDS(hgrn_lb.shape, F32))(hgrn_lb)


def _lb_bwd(hgrn_lb, dlb):
    def body(x_ref, d_ref, o_ref):
        x0, x1 = x_ref[0:1, :], x_ref[1:2, :]
        m = jnp.maximum(x0, x1)
        e0, e1 = jnp.exp(x0 - m), jnp.exp(x1 - m)
        p0, p1 = e0 / (e0 + e1), e1 / (e0 + e1)
        val = (p0 + p1) - p0
        dp1 = jnp.where((val > 0.0) & (val < 1.0 - 1e-6), d_ref[1:2, :], 0.0)
        inner = p1 * dp1
        o_ref[0:1, :] = p0 * (0.0 - inner)
        o_ref[1:2, :] = p1 * (dp1 - inner)

    return pl.pallas_call(body, name="lb_bwd", out_shape=SDS(hgrn_lb.shape, F32))(hgrn_lb, dlb)


def _fox_prep(proj, bf):
    T = proj.shape[0]
    n = T // CHUNK

    def body(q0_ref, q1_ref, k0_ref, k1_ref, v0_ref, v1_ref, fl_ref, bf_ref, qo_ref, ko_ref, vt_ref, carry_ref):
        for p, v_ref in enumerate((v0_ref, v0_ref, v1_ref, v1_ref)):
            vt_ref[p, 0] = v_ref[:, 128 * (p % 2):128 * (p % 2) + 128].T.astype(BF16)

        @pl.when(pl.program_id(0) == 0)
        def _():
            carry_ref[...] = jnp.zeros_like(carry_ref)

        ltri = jnp.where(_lane((CHUNK, CHUNK)) <= _row((CHUNK, CHUNK)), 1.0, 0.0).astype(BF16)
        lf = jax.nn.log_sigmoid(fl_ref[...] + bf_ref[...])
        c = _dot3_left(ltri, lf) + carry_ref[...]
        carry_ref[...] = c[CHUNK - 1:CHUNK, :]
        lane = _lane((CHUNK, 128))
        feat = lane < 64
        ones_q = (lane >= 67) & (lane <= 69)
        ones_k = (lane >= 64) & (lane <= 66)
        qrefs, krefs = (q0_ref, q1_ref), (k0_ref, k1_ref)
        for h in range(C_HEADS):
            blk = slice(128 * ((h // 2) % 2), 128 * ((h // 2) % 2) + 128)
            qp, kp = qrefs[h // 4][:, blk], krefs[h // 4][:, blk]
            if h % 2:
                qp, kp = pltpu.roll(qp, 64, axis=1), pltpu.roll(kp, 64, axis=1)
            ch = jnp.broadcast_to(c[:, h:h + 1], (CHUNK, 128))
            hi = ch.astype(BF16).astype(F32)
            r1 = ch - hi
            mid = r1.astype(BF16).astype(F32)
            lo = r1 - mid
            aq = jnp.where(lane == 64, hi, jnp.where(lane == 65, mid, jnp.where(lane == 66, lo,
                           jnp.where(ones_q, 1.0, 0.0))))
            ak = jnp.where(lane == 67, -hi, jnp.where(lane == 68, -mid, jnp.where(lane == 69, -lo,
                           jnp.where(ones_k, 1.0, 0.0))))
            qo_ref[:, 128 * h:128 * h + 128] = jnp.where(feat, qp * Q_SCALE, aq).astype(BF16)
            ko_ref[:, 128 * h:128 * h + 128] = jnp.where(feat, kp, ak).astype(BF16)

    w = 256
    col = lambda c: pl.BlockSpec((CHUNK, w), lambda i, c=c: (i, c // w))
    return _Part(body, (proj, proj, proj, proj, proj, proj, proj, bf),
                 [col(COL_CQ), col(COL_CQ + w), col(COL_CK), col(COL_CK + w), col(COL_CV), col(COL_CV + w),
                  pl.BlockSpec((CHUNK, 128), lambda i: (i, COL_CF // 128)), pl.BlockSpec((1, 128), lambda i: (0, 0))],
                 [pl.BlockSpec((CHUNK, C_HEADS * 128), lambda i: (i, 0))] * 2
                 + [pl.BlockSpec((C_HEADS // 2, 1, 128, CHUNK), lambda i: (0, i, 0, 0))],
                 [SDS((T, C_HEADS * 128), BF16)] * 2 + [SDS((C_HEADS // 2, n, 128, CHUNK), BF16)],
                 [pltpu.VMEM((1, 128), F32)])


FOX_TILE = 512
FOX_KEYS = 512
FOX_STRIP = 16


def _fox_mask(tk, tq, k0, q0):
    return (_row((tk, tq)) + (k0 - q0)) <= _lane((tk, tq))


def _ride_refs(ride, rest, n_out, n_scratch):
    n = ride.n if ride else 0
    srcs, rest = rest[:n], rest[n:]
    outs, rest = rest[:n_out], rest[n_out:]
    dsts, rest = rest[:n], rest[n:]
    return srcs, outs, dsts, rest[:n_scratch], rest[n_scratch:]


def _ride_start(ride, grid, srcs, dsts, sems):
    if ride:
        first = functools.reduce(lambda a, b: a & b, [pl.program_id(d) == 0 for d in range(len(grid))])
        pl.when(first)(lambda: ride.start(srcs, dsts, sems))


def _ride_wait(ride, grid, srcs, dsts, sems):
    if ride:
        last = functools.reduce(lambda a, b: a & b, [pl.program_id(d) == n - 1 for d, n in enumerate(grid)])
        pl.when(last)(lambda: ride.wait(srcs, dsts, sems))


def _fox_fwd(qt, kt, vt, proj, tag, ride=None):
    T = proj.shape[0]
    tq, tk = _tile(T, FOX_TILE), _tile(T, FOX_KEYS)
    nq, nsub = T // tq, tk // CHUNK
    npair = C_HEADS // 2

    def body(q_ref, k_ref, vt_ref, z_ref, *rest):
        ride_srcs, (o_ref, lse_ref, y_ref), ride_dsts, (acc_ref, st_ref, pt_ref), ride_sems = _ride_refs(ride, rest, 3, 3)
        i = pl.program_id(1)
        _ride_start(ride, (npair, nq), ride_srcs, ride_dsts, ride_sems)

        qs = (q_ref[:, 0:128], q_ref[:, 128:256])
        acc_ref[...] = jnp.zeros_like(acc_ref)
        pt_ref[...] = jnp.zeros_like(pt_ref)
        nfull = (i * tq) // tk

        def scores(j):
            kb = k_ref[pl.ds(pl.multiple_of(j * tk, tk), tk), :]
            return tuple(_dot_nt(kb[:, 128 * h:128 * h + 128], qs[h]) for h in range(2))

        def weigh(j, h):
            rows = slice(64 * h, 64 * h + 64)
            vth = jnp.concatenate([vt_ref[0, nsub * j + c, rows, :] for c in range(nsub)], axis=1)
            return _dot(vth, pt_ref[h])

        def block(j, carry, diagonal):
            nxt = () if diagonal else scores(j + 1)
            pvs = [weigh(jnp.maximum(j - 1, 0), h) for h in range(2)]
            new = []
            for h in range(2):
                m, l, alpha_prev = carry[3 * h:3 * h + 3]
                st = st_ref[h]
                if diagonal:
                    st = jnp.where(_fox_mask(tk, tq, j * tk, i * tq), st, -jnp.inf)
                m_new = jnp.maximum(m, _colreduce(st, jnp.maximum))
                pt = jnp.exp(st - m_new)
                alpha = jnp.exp(m - m_new)
                rows = slice(64 * h, 64 * h + 64)
                acc_ref[rows, :] = alpha_prev * acc_ref[rows, :] + pvs[h]
                pt_ref[h] = pt.astype(BF16)
                new += [m_new, alpha * l + _colreduce(pt, jnp.add), alpha]
            for h, st in enumerate(nxt):
                st_ref[h] = st
            return tuple(new)

        for h, st in enumerate(scores(0)):
            st_ref[h] = st
        init = (jnp.full((1, tq), -jnp.inf, F32), jnp.zeros((1, tq), F32), jnp.ones((1, tq), F32)) * 2
        carry = lax.fori_loop(0, nfull, lambda j, c: block(j, c, False), init)
        m0, l0, a0, m1, l1, a1 = block(nfull, carry, True)
        for h, alpha in enumerate((a0, a1)):
            rows = slice(64 * h, 64 * h + 64)
            acc_ref[rows, :] = alpha * acc_ref[rows, :] + weigh(nfull, h)
        inv = jnp.where(_row((128, tq)) < 64, 1.0 / l0, 1.0 / l1)
        o = (acc_ref[...] * inv).T
        o_ref[...] = o
        r8 = _row((8, tq))
        lse_ref[0, 0] = jnp.where(r8 == 0, m0 + jnp.log(l0), jnp.where(r8 == 1, m1 + jnp.log(l1), 0.0))
        sz, _ = _silu_and_grad(z_ref[...])
        y_ref[...] = (o * sz).astype(BF16)
        _ride_wait(ride, (npair, nq), ride_srcs, ride_dsts, ride_sems)

    blk = pl.BlockSpec((tq, 128), lambda p, i: (i, p))
    extra = ride or _ChipExchange("gather", ())
    return pl.pallas_call(
        body, name=f"fox_fwd_{tag}", grid=(npair, nq),
        in_specs=[pl.BlockSpec((tq, 256), lambda p, i: (i, p)), pl.BlockSpec((T, 256), lambda p, i: (0, p)),
                  pl.BlockSpec((1, T // CHUNK, 128, CHUNK), lambda p, i: (p, 0, 0, 0)),
                  pl.BlockSpec((tq, 128), lambda p, i: (i, COL_CZ // 128 + p))] + extra.in_specs,
        out_specs=[blk, pl.BlockSpec((1, 1, 8, tq), lambda p, i: (p, i, 0, 0)), blk] + extra.out_specs,
        out_shape=[SDS((T, C_WIDTH), F32), SDS((npair, nq, 8, tq), F32), SDS((T, C_WIDTH), BF16)] + extra.out_shape,
        scratch_shapes=[pltpu.VMEM((128, tq), F32), pltpu.VMEM((2, tk, tq), F32), pltpu.VMEM((2, tk, tq), BF16)]
        + (extra.scratch if ride else []),
        compiler_params=pltpu.CompilerParams(dimension_semantics=("arbitrary", "arbitrary"), vmem_limit_bytes=VMEM_LIMIT,
                                             has_side_effects=bool(ride)),
    )(qt, kt, vt, proj, *extra.sources)


def _fox_bwd_prep(proj, dy, o, qt, tag):
    T = proj.shape[0]
    tq = _tile(T, FOX_TILE)
    nq = T // tq

    def body(z0_ref, z1_ref, dy_ref, o_ref, q_ref, do_ref, dl_ref, dz_ref, dot_ref, qt_ref):
        sel = jnp.where((_lane((16, 128)) >> 6) == _row((16, 128)), 1.0, 0.0).astype(BF16)
        for p, z_ref in enumerate((z0_ref, z0_ref, z1_ref, z1_ref)):
            sl = slice(128 * p, 128 * p + 128)
            sz, dsz = _silu_and_grad(z_ref[:, 128 * (p % 2):128 * (p % 2) + 128])
            dyv, ov = dy_ref[:, sl], o_ref[:, sl]
            do = dyv * sz
            do_ref[:, sl] = do.astype(BF16)
            dot_ref[p, 0] = do.T.astype(BF16)
            dz_ref[:, sl] = (dyv * ov * dsz).astype(BF16)
            hi, mid, lo = _split3(do * ov)
            dl_ref[p, 0] = (_dot_nt(sel, hi) + _dot_nt(sel, mid) + _dot_nt(sel, lo))[0:8, :]
        for h in range(C_HEADS):
            qt_ref[h, 0] = q_ref[:, 128 * h:128 * h + 128].astype(F32).T.astype(BF16)

    w = 256
    blk = pl.BlockSpec((tq, C_WIDTH), lambda i: (i, 0))
    return pl.pallas_call(
        body, name=f"fox_bwd_prep_{tag}", grid=(nq,),
        in_specs=[pl.BlockSpec((tq, w), lambda i: (i, COL_CZ // w)), pl.BlockSpec((tq, w), lambda i: (i, COL_CZ // w + 1)),
                  pl.BlockSpec((tq, C_WIDTH), lambda i: (i, (A_WIDTH + B_WIDTH) // C_WIDTH)), blk,
                  pl.BlockSpec((tq, C_HEADS * 128), lambda i: (i, 0))],
        out_specs=[blk, pl.BlockSpec((C_HEADS // 2, 1, 8, tq), lambda i: (0, i, 0, 0)), blk,
                   pl.BlockSpec((C_HEADS // 2, 1, 128, tq), lambda i: (0, i, 0, 0)),
                   pl.BlockSpec((C_HEADS, 1, 128, tq), lambda i: (0, i, 0, 0))],
        out_shape=[SDS((T, C_WIDTH), BF16), SDS((C_HEADS // 2, nq, 8, tq), F32), SDS((T, C_WIDTH), BF16),
                   SDS((C_HEADS // 2, nq, 128, tq), BF16), SDS((C_HEADS, nq, 128, tq), BF16)],
        compiler_params=_params("parallel"),
    )(proj, proj, dy, o, qt)


def _fox_bwd(qt, kt, proj, do, lse, delta, dot, qtr, tag, ride=None):
    T = proj.shape[0]
    tq, tk = _tile(T, FOX_TILE), _tile(T, FOX_KEYS)
    nq, nk = T // tq, T // tk
    assert tq == tk
    npair = C_HEADS // 2

    def body(q_ref, k_ref, v_ref, do_ref, lse_ref, dl_ref, dot_ref, qtr_ref, *rest):
        ride_srcs, (dq_ref, dk_ref, dv_ref), ride_dsts, scratch, ride_sems = _ride_refs(ride, rest, 3, 4)
        dvt_ref, dkt_ref, pt_ref, ds_ref = scratch
        j = pl.program_id(1)
        first = (j * tk) // tq
        _ride_start(ride, (npair, nk), ride_srcs, ride_dsts, ride_sems)

        @pl.when(j == 0)
        def _():
            dq_ref[...] = jnp.zeros_like(dq_ref)

        dkt_ref[...] = jnp.zeros_like(dkt_ref)
        dvt_ref[...] = jnp.zeros_like(dvt_ref)
        ks = (k_ref[:, 0:128], k_ref[:, 128:256])
        kts = tuple(k.astype(F32).T.astype(BF16) for k in ks)
        vb = v_ref[...].astype(BF16)
        lo = _lane((tq, 128)) < 64

        def operands(i):
            q0 = pl.multiple_of(i * tq, tq)
            qb = q_ref[pl.ds(q0, tq), :]
            dob = do_ref[pl.ds(q0, tq), :]
            qhs = (qb[:, 0:128], qb[:, 128:256])
            dohs = (jnp.where(lo, dob, jnp.zeros_like(dob)), jnp.where(lo, jnp.zeros_like(dob), dob))
            return qhs, dohs

        def scores(i):
            qhs, dohs = operands(i)
            return tuple((_dot_nt(ks[h], qhs[h]), _dot_nt(vb, dohs[h])) for h in range(2))

        def grads(i, slot):
            for h in range(2):
                rows = slice(64 * h, 64 * h + 64)
                dvt_ref[rows, :] += _dot_nt(dot_ref[0, i, rows, :], pt_ref[slot, h])
                dkt_ref[h] += _dot_nt(qtr_ref[h, i], ds_ref[slot, h])
                dq_ref[h, i] += _dot(kts[h], ds_ref[slot, h])

        def block(i, slot, diagonal, opening):
            sc = scores(i)
            if not opening:
                grads(i - 1, 1 - slot)
            lsev = lse_ref[0, i]
            dlv = dl_ref[0, i]
            for h in range(2):
                lseh = jnp.broadcast_to(lsev[h:h + 1, :], (FOX_STRIP, tq))
                dlh = jnp.broadcast_to(dlv[h:h + 1, :], (FOX_STRIP, tq))
                for r in range(0, tk, FOX_STRIP):
                    rows = slice(r, r + FOX_STRIP)
                    pt = jnp.exp(sc[h][0][rows, :] - lseh)
                    if diagonal:
                        pt = jnp.where(_fox_mask(FOX_STRIP, tq, r, 0), pt, 0.0)
                    ds_ref[slot, h, rows, :] = (pt * (sc[h][1][rows, :] - dlh)).astype(BF16)
                    pt_ref[slot, h, rows, :] = pt.astype(BF16)

        block(first, 0, True, True)
        rest = nq - 1 - first

        def two_steps(t, carry):
            block(first + 1 + 2 * t, 1, False, False)
            block(first + 2 + 2 * t, 0, False, False)
            return carry

        lax.fori_loop(0, rest // 2, two_steps, 0)
        pl.when(rest % 2 == 1)(lambda: block(nq - 1, 1, False, False))
        grads(nq - 1, rest % 2)
        dv_ref[...] = dvt_ref[...].T.astype(BF16)
        for h in range(2):
            dk_ref[:, 128 * h:128 * h + 128] = dkt_ref[h].T
        _ride_wait(ride, (npair, nk), ride_srcs, ride_dsts, ride_sems)

    full = lambda w: pl.BlockSpec((T, w), lambda p, j: (0, p))
    stat = pl.BlockSpec((1, nq, 8, tq), lambda p, j: (p, 0, 0, 0))
    extra = ride or _ChipExchange("gather", ())
    return pl.pallas_call(
        body, name=f"fox_bwd_{tag}", grid=(npair, nk),
        in_specs=[full(256), pl.BlockSpec((tk, 256), lambda p, j: (j, p)),
                  pl.BlockSpec((tk, 128), lambda p, j: (j, COL_CV // 128 + p)), full(128), stat, stat,
                  pl.BlockSpec((1, nq, 128, tq), lambda p, j: (p, 0, 0, 0)),
                  pl.BlockSpec((2, nq, 128, tq), lambda p, j: (p, 0, 0, 0))] + extra.in_specs,
        out_specs=[pl.BlockSpec((2, nq, 128, tq), lambda p, j: (p, 0, 0, 0)), pl.BlockSpec((tk, 256), lambda p, j: (j, p)),
                   pl.BlockSpec((tk, 128), lambda p, j: (j, p))] + extra.out_specs,
        out_shape=[SDS((C_HEADS, nq, 128, tq), F32), SDS((T, C_HEADS * 128), F32), SDS((T, C_WIDTH), BF16)]
        + extra.out_shape,
        scratch_shapes=[pltpu.VMEM((128, tk), F32), pltpu.VMEM((2, 128, tk), F32),
                        pltpu.VMEM((2, 2, tk, tq), BF16), pltpu.VMEM((2, 2, tk, tq), BF16)]
        + (extra.scratch if ride else []),
        compiler_params=pltpu.CompilerParams(dimension_semantics=("arbitrary", "arbitrary"), vmem_limit_bytes=VMEM_LIMIT,
                                             has_side_effects=bool(ride)),
    )(qt, kt, proj, do, lse, delta, dot, qtr, *extra.sources)


def _fox_bwd_post(dqt, dkt, proj, bf, tag):
    T = proj.shape[0]
    tq = _tile(T, FOX_TILE)
    n = T // tq

    def body(dq_ref, dk_ref, fl_ref, bf_ref, oq_ref, ok_ref, ofl_ref, dbf_ref, carry_ref):
        @pl.when(pl.program_id(0) == 0)
        def _():
            carry_ref[...] = jnp.zeros_like(carry_ref)
            dbf_ref[...] = jnp.zeros_like(dbf_ref)

        lane = _lane((tq, 128))
        lo = lane < 64
        dqs = [dq_ref[h, 0].T for h in range(C_HEADS)]
        dc = jnp.zeros((tq, 128), F32)
        for h in range(C_HEADS):
            dc = dc + jnp.where(lane == h, dqs[h][:, 64:65] - dk_ref[:, 128 * h + 67:128 * h + 68], 0.0)
        utri = jnp.where(_lane((tq, tq)) >= _row((tq, tq)), 1.0, 0.0).astype(BF16)
        dlf = _dot3_left(utri, dc) + carry_ref[...]
        carry_ref[...] = dlf[0:1, :]
        dfl = jnp.where(lane < C_HEADS, dlf * _sigmoid(-(fl_ref[...] + bf_ref[...])), 0.0)
        ofl_ref[...] = dfl.astype(BF16)
        dbf_ref[...] += jnp.sum(dfl, axis=0, keepdims=True)
        for p in range(C_HEADS // 2):
            a, b = 128 * (2 * p), 128 * (2 * p + 1)
            oq_ref[:, 128 * p:128 * p + 128] = (
                jnp.where(lo, dqs[2 * p], pltpu.roll(dqs[2 * p + 1], 64, axis=1)) * Q_SCALE).astype(BF16)
            ok_ref[:, 128 * p:128 * p + 128] = jnp.where(
                lo, dk_ref[:, a:a + 128], pltpu.roll(dk_ref[:, b:b + 128], 64, axis=1)).astype(BF16)

    rev = lambda w: pl.BlockSpec((tq, w), lambda i: (n - 1 - i, 0))
    return pl.pallas_call(
        body, name=f"fox_bwd_post_{tag}", grid=(n,),
        in_specs=[pl.BlockSpec((C_HEADS, 1, 128, tq), lambda i: (0, n - 1 - i, 0, 0)), rev(C_HEADS * 128),
                  pl.BlockSpec((tq, 128), lambda i: (n - 1 - i, COL_CF // 128)), pl.BlockSpec((1, 128), lambda i: (0, 0))],
        out_specs=[rev(C_WIDTH), rev(C_WIDTH), rev(128), pl.BlockSpec((1, 128), lambda i: (0, 0))],
        out_shape=[SDS((T, C_WIDTH), BF16), SDS((T, C_WIDTH), BF16), SDS((T, 128), BF16), SDS((1, 128), F32)],
        scratch_shapes=[pltpu.VMEM((1, 128), F32)], compiler_params=_params("arbitrary"),
    )(dqt, dkt, proj, bf)


def _adamw_math(w, g, m, v):
    m = ADAM_B1 * m + (1.0 - ADAM_B1) * g
    v = ADAM_B2 * v + (1.0 - ADAM_B2) * (g * g)
    delta = -ADAM_LR * ((m / ADAM_C1) / (jnp.sqrt(v / ADAM_C2) + ADAM_EPS) + ADAM_WD * w)
    return delta, m, v


def _adamw_pair(w, m, v, ga, gb, name):
    n0 = w.shape[0]
    most = max(1, ADAMW_BLOCK_BYTES // (4 * math.prod(w.shape[1:])))
    t0 = max(t for t in range(1, min(n0, most) + 1) if n0 % t == 0)

    def body(w_ref, m_ref, v_ref, ga_ref, gb_ref, g_ref, d_ref, nm_ref, nv_ref):
        g = ga_ref[...] + gb_ref[...]
        g_ref[...] = g
        d_ref[...], nm_ref[...], nv_ref[...] = _adamw_math(w_ref[...], g, m_ref[...], v_ref[...])

    blk = pl.BlockSpec((t0,) + w.shape[1:], lambda i: (i, 0, 0))
    return pl.pallas_call(
        body, name=name, grid=(n0 // t0,), in_specs=[blk] * 5, out_specs=[blk] * 4,
        out_shape=[SDS(w.shape, F32)] * 4, compiler_params=_params("parallel"),
    )(w, m, v, ga, gb)


def _small_layout():
    L = DEPTH
    lanes = lambda j: slice(128 * j, 128 * j + 128)
    wide = lambda n: [((slice(l, l + 1), lanes(j)), n * l + j, 1, 0, 128) for l in range(L) for j in range(n)]
    halves = [((l, slice(g, g + 1)), 2 * l + g // 2, 1, 64 * (g % 2), 64) for l in range(L) for g in range(A_GROUPS)]
    side_by_side = lambda w: [((slice(l, l + 1),), 0, 1, w * l, w) for l in range(L)]
    return [
        ((L, D_MODEL), wide(D_MODEL // 128)), ((L, A_GROUPS, 64), halves), ((L, A_GROUPS, 64), halves),
        ((L * A_GROUPS * CHUNK, CHUNK), [((slice(None),), 0, L * A_GROUPS * CHUNK, 0, CHUNK)]),
        ((L, A_GROUPS, CHUNK), [((l,), A_GROUPS * l, A_GROUPS, 0, CHUNK) for l in range(L)]),
        ((L, B_WIDTH), wide(B_WIDTH // 128)), ((L, 64), side_by_side(64)), ((L, C_HEADS), side_by_side(C_HEADS)),
        ((1, D_MODEL), [((slice(None), lanes(j)), j, 1, 0, 128) for j in range(D_MODEL // 128)]),
    ]


def _adamw_small(ws, ms, vs, gearly, glate, late):
    offs = _small_offsets()
    layout = _small_layout()
    n = len(ws)
    assert [w.shape for w in ws] == [shape for shape, _ in layout] and 0 in late

    def body(*refs):
        w_refs, m_refs, v_refs = refs[:n], refs[n:2 * n], refs[2 * n:3 * n]
        early_ref, late_ref = refs[3 * n:3 * n + 2]
        outs = refs[3 * n + 2:]

        def total(k):
            rows = offs[k + 1] - offs[k] if k < n else 1

            def block(dev):
                parts = [early_ref[dev, offs[k] - offs[1]:offs[k] - offs[1] + rows, :]] if k else []
                if k in late:
                    parts.append(late_ref[dev, late[k]:late[k] + rows, :])
                return functools.reduce(jnp.add, parts)

            return functools.reduce(jnp.add, [block(dev) for dev in range(N_DEV)])

        for k, (_, pieces) in enumerate(layout):
            g = total(k)
            go_ref, d_ref, nm_ref, nv_ref = outs[4 * k:4 * k + 4]
            for idx, row, rows, lane, width in pieces:
                gp = g[row:row + rows, :]
                if lane:
                    gp = pltpu.roll(gp, 128 - lane, axis=1)
                gp = gp[:, :width]
                go_ref[idx] = gp
                d_ref[idx], nm_ref[idx], nv_ref[idx] = _adamw_math(w_refs[k][idx], gp, m_refs[k][idx], v_refs[k][idx])
        outs[4 * n][...] = total(n)

    shapes = [SDS(w.shape, F32) for w in ws for _ in range(4)] + [SDS((1, 128), F32)]
    res = pl.pallas_call(body, name="adamw_small", out_shape=shapes,
                         compiler_params=pltpu.CompilerParams(vmem_limit_bytes=VMEM_LIMIT))(*ws, *ms, *vs, gearly, glate)
    return [res[4 * k:4 * k + 4] for k in range(n)], res[4 * n]


def _pack_grads(dlng, dlnb, dwm, dbst, dlb, donorm, dbf, dfinal, loss_part):
    offs = _small_offsets()
    base = offs[1]
    L = len(dwm)
    assert L == 2

    def body(*refs):
        lng, lnb, wm, bst, on, bf = (refs[L * a:L * a + L] for a in range(6))
        lb_ref, fin_ref, loss_ref, o_ref = refs[6 * L:]
        o_ref[...] = jnp.zeros_like(o_ref)
        lane = _lane((1, 128))
        for l in range(L):
            for j in range(2):
                o_ref[offs[1] - base + 2 * l + j:offs[1] - base + 2 * l + j + 1, :] = lng[l][:, 128 * j:128 * j + 128]
                o_ref[offs[2] - base + 2 * l + j:offs[2] - base + 2 * l + j + 1, :] = lnb[l][:, 128 * j:128 * j + 128]
                o_ref[offs[5] - base + 2 * l + j:offs[5] - base + 2 * l + j + 1, :] = lb_ref[l:l + 1, 128 * j:128 * j + 128]
            for g in range(A_GROUPS):
                row = offs[3] - base + (A_GROUPS * l + g) * CHUNK
                o_ref[row:row + CHUNK, :] = wm[l][g]
            o_ref[offs[4] - base + A_GROUPS * l:offs[4] - base + A_GROUPS * (l + 1), :] = bst[l][...].T[0:A_GROUPS, :]
        o_ref[offs[6] - base:offs[6] - base + 1, :] = jnp.where(lane < 64, on[0][...], pltpu.roll(on[1][...], 64, axis=1))
        o_ref[offs[7] - base:offs[7] - base + 1, :] = jnp.where(
            lane < C_HEADS, bf[0][...], jnp.where(lane < 2 * C_HEADS, pltpu.roll(bf[1][...], C_HEADS, axis=1), 0.0))
        for j in range(D_MODEL // 128):
            o_ref[offs[8] - base + j:offs[8] - base + j + 1, :] = fin_ref[:, 128 * j:128 * j + 128]
        o_ref[offs[9] - base:offs[9] - base + 1, :] = loss_ref[...]

    rows = offs[9] + 8 - base
    return pl.pallas_call(body, name="pack_grads", out_shape=SDS((rows, 128), F32))(
        *dlng, *dlnb, *dwm, *dbst, *donorm, *dbf, dlb, dfinal, loss_part)


def _sum_chips(layers, name, layer_major):
    groups = [list(layer) if isinstance(layer, (list, tuple)) else [layer] for layer in layers]
    R = groups[0][0].shape[1]
    C = sum(a.shape[2] for a in groups[0])
    L = len(groups)
    tc = _tile(C, 256)
    steps = C // tc
    plan = []
    for l, layer in enumerate(groups):
        assert all(a.shape[2] % tc == 0 for a in layer) and sum(a.shape[2] for a in layer) == C
        firsts = [sum(a.shape[2] for a in layer[:k]) // tc for k in range(len(layer))]
        plan += [(l, first, a.shape[2] // tc) for first, a in zip(firsts, layer)]

    def body(*refs):
        o_ref = refs[-1]
        i = pl.program_id(0)
        for (l, first, n), p_ref in zip(plan, refs[:-1]):
            def write(l=l, p_ref=p_ref):
                p = [p_ref[k].astype(F32) for k in range(N_CHIPS)]
                s = ((p[0] + p[1]) + p[2]) + p[3]
                if layer_major:
                    o_ref[l] = s
                else:
                    o_ref[:, l, :] = s

            if n == steps:
                write()
            else:
                pl.when((i >= first) & (i < first + n))(write)

    out = (L, R, C) if layer_major else (R, L, C)
    out_blk = (L, R, tc) if layer_major else (R, L, tc)
    return pl.pallas_call(
        body, name=name, grid=(steps,),
        in_specs=[pl.BlockSpec((N_CHIPS, R, tc), lambda i, first=first, n=n: (0, 0, jnp.clip(i - first, 0, n - 1)))
                  for _, first, n in plan],
        out_specs=pl.BlockSpec(out_blk, lambda i: (0, 0, i)), out_shape=SDS(out, F32),
        compiler_params=_params("parallel"),
    )(*[a for layer in groups for a in layer])


ANY = pl.BlockSpec(memory_space=pl.ANY)


def _mesh_pos():
    return lax.axis_index("x"), lax.axis_index("y"), lax.axis_index("c")


def _other_chips(x, y):
    return [(1 - x, y), (x, 1 - y), (1 - x, 1 - y)]


class _ChipExchange:
    def __init__(self, mode, sources):
        assert mode in ("gather", "scatter")
        self.mode, self.sources = mode, tuple(sources)
        self.n = len(self.sources)
        self.in_specs = [ANY] * self.n
        self.out_specs = [ANY] * self.n
        self.out_shape = [SDS(((N_CHIPS,) + s.shape) if mode == "gather" else s.shape, s.dtype) for s in self.sources]
        self.scratch = [pltpu.SemaphoreType.DMA((3 * self.n,)), pltpu.SemaphoreType.DMA((3 * self.n,)),
                        pltpu.SemaphoreType.DMA((self.n,))]

    def _copies(self, srcs, dsts, send_sems, recv_sems, local_sems):
        x, y, c = _mesh_pos()
        me = 2 * x + y
        view = (lambda r, chip: r) if self.mode == "gather" else (lambda r, chip: r.at[chip])
        local = [pltpu.make_async_copy(view(s, me), d.at[me], local_sems.at[a]) for a, (s, d) in enumerate(zip(srcs, dsts))]
        sends, recvs = [], []
        for j, (px, py) in enumerate(_other_chips(x, y)):
            peer = 2 * px + py
            for a, (s, d) in enumerate(zip(srcs, dsts)):
                sems = dict(send_sem=send_sems.at[self.n * j + a], recv_sem=recv_sems.at[self.n * j + a],
                            device_id=(px, py, c), device_id_type=MESH_ID)
                sends.append(pltpu.make_async_remote_copy(src_ref=view(s, peer), dst_ref=d.at[me], **sems))
                recvs.append(pltpu.make_async_remote_copy(src_ref=view(s, me), dst_ref=d.at[peer], **sems))
        return local, sends, recvs

    def start(self, srcs, dsts, sems):
        local, sends, _ = self._copies(srcs, dsts, *sems)
        for cp in local + sends:
            cp.start()

    def wait(self, srcs, dsts, sems):
        local, sends, recvs = self._copies(srcs, dsts, *sems)
        for cp in recvs:
            cp.wait_recv()
        for cp in sends:
            cp.wait_send()
        for cp in local:
            cp.wait()


def _gather_halves(w, tag):
    R, C = w.shape
    H = C // 2

    def body(w_ref, g_ref, send_sems, recv_sems, pass_send, pass_recv, local_sem):
        x, y, c = _mesh_pos()
        me = 2 * x + y
        mine, theirs = pl.ds(pl.multiple_of(c * H, H), H), pl.ds(pl.multiple_of((1 - c) * H, H), H)
        own = pltpu.make_async_copy(w_ref, g_ref.at[me], local_sem)
        own.start()

        def fetch(j, px, py, src, dst):
            return pltpu.make_async_remote_copy(src_ref=src, dst_ref=dst, send_sem=send_sems.at[j], recv_sem=recv_sems.at[j],
                                                device_id=(px, py, c), device_id_type=MESH_ID)

        def hand(j, cols, peer):
            return pltpu.make_async_remote_copy(src_ref=g_ref.at[peer, :, cols], dst_ref=g_ref.at[peer, :, cols],
                                                send_sem=pass_send.at[j], recv_sem=pass_recv.at[j],
                                                device_id=(x, y, 1 - c), device_id_type=MESH_ID)

        chips = _other_chips(x, y)
        sends = [fetch(j, px, py, w_ref.at[:, mine], g_ref.at[me, :, mine]) for j, (px, py) in enumerate(chips)]
        for cp in sends:
            cp.start()
        passed = []
        for j, (px, py) in enumerate(chips):
            peer = 2 * px + py
            fetch(j, px, py, w_ref.at[:, mine], g_ref.at[peer, :, mine]).wait_recv()
            passed.append(hand(j, mine, peer))
            passed[-1].start()
        for j, (px, py) in enumerate(chips):
            hand(j, theirs, 2 * px + py).wait_recv()
        for cp in sends + passed:
            cp.wait_send()
        own.wait()

    return pl.pallas_call(
        body, name=f"gather_halves_{tag}", in_specs=[ANY], out_specs=ANY, out_shape=SDS((N_CHIPS, R, C), w.dtype),
        scratch_shapes=[pltpu.SemaphoreType.DMA((3,)), pltpu.SemaphoreType.DMA((3,)), pltpu.SemaphoreType.DMA((3,)),
                        pltpu.SemaphoreType.DMA((3,)), pltpu.SemaphoreType.DMA],
        compiler_params=pltpu.CompilerParams(has_side_effects=True),
    )(w)


class _DeviceGather:
    def __init__(self, source):
        self.sources, self.n = (source,), 1
        self.in_specs, self.out_specs = [ANY], [ANY]
        self.out_shape = [SDS((N_DEV,) + source.shape, source.dtype)]
        self.scratch = [pltpu.SemaphoreType.DMA((N_DEV - 1,)), pltpu.SemaphoreType.DMA((N_DEV - 1,)),
                        pltpu.SemaphoreType.DMA((1,))]

    def _copies(self, srcs, dsts, send_sems, recv_sems, local_sems):
        (src,), (dst,) = srcs, dsts
        x, y, c = _mesh_pos()
        me = 4 * x + 2 * y + c
        local = [pltpu.make_async_copy(src, dst.at[me], local_sems.at[0])]
        sends, recvs = [], []
        for k in range(1, N_DEV):
            px, py, pc = (1 - x) if k & 4 else x, (1 - y) if k & 2 else y, (1 - c) if k & 1 else c
            sems = dict(send_sem=send_sems.at[k - 1], recv_sem=recv_sems.at[k - 1], device_id=(px, py, pc),
                        device_id_type=MESH_ID)
            sends.append(pltpu.make_async_remote_copy(src_ref=src, dst_ref=dst.at[me], **sems))
            recvs.append(pltpu.make_async_remote_copy(src_ref=src, dst_ref=dst.at[4 * px + 2 * py + pc], **sems))
        return local, sends, recvs

    start = _ChipExchange.start
    wait = _ChipExchange.wait


class _Rides:
    def __init__(self, *rides):
        self.rides = rides
        self.n = sum(r.n for r in rides)
        self.sources = tuple(s for r in rides for s in r.sources)
        self.in_specs, self.out_specs = [ANY] * self.n, [ANY] * self.n
        self.out_shape = [s for r in rides for s in r.out_shape]
        self.scratch = [s for r in rides for s in r.scratch]

    def _each(self, srcs, dsts, sems):
        a = b = 0
        for r in self.rides:
            yield r, srcs[a:a + r.n], dsts[a:a + r.n], sems[b:b + len(r.scratch)]
            a, b = a + r.n, b + len(r.scratch)

    def start(self, srcs, dsts, sems):
        for r, s, d, m in self._each(srcs, dsts, sems):
            r.start(s, d, m)

    def wait(self, srcs, dsts, sems):
        for r, s, d, m in self._each(srcs, dsts, sems):
            r.wait(s, d, m)


def _gather_devices(a, name):
    ex = _DeviceGather(a)

    def body(a_ref, g_ref, *sems):
        ex.start((a_ref,), (g_ref,), sems)
        ex.wait((a_ref,), (g_ref,), sems)

    return pl.pallas_call(
        body, name=name, in_specs=ex.in_specs, out_specs=ex.out_specs[0], out_shape=ex.out_shape[0],
        scratch_shapes=ex.scratch, compiler_params=pltpu.CompilerParams(has_side_effects=True),
    )(a)


def _swap_cores(pin, pout):
    def body(pin_ref, pout_ref, oin_ref, oout_ref, send_sems, recv_sems):
        x, y, c = _mesh_pos()
        cps = [pltpu.make_async_remote_copy(src_ref=src, dst_ref=dst, send_sem=send_sems.at[a], recv_sem=recv_sems.at[a],
                                            device_id=(x, y, 1 - c), device_id_type=MESH_ID)
               for a, (src, dst) in enumerate(((pin_ref, oin_ref), (pout_ref, oout_ref)))]
        for cp in cps:
            cp.start()
        for cp in cps:
            cp.wait()

    return pl.pallas_call(
        body, name="swap_cores", in_specs=[ANY, ANY], out_specs=[ANY, ANY],
        out_shape=[SDS(pin.shape, F32), SDS(pout.shape, F32)],
        scratch_shapes=[pltpu.SemaphoreType.DMA((2,)), pltpu.SemaphoreType.DMA((2,))],
        compiler_params=pltpu.CompilerParams(has_side_effects=True),
    )(pin, pout)


PACK_TILE = 8 * 128


def _pack_rows(size):
    return (size + PACK_TILE - 1) // PACK_TILE * 8


def _small_offsets():
    offs = [0]
    for _, shape in SMALL_PARAMS:
        offs.append(offs[-1] + _pack_rows(math.prod(shape)))
    return offs


def _layer_consts(l, gmlp_ln_g, gmlp_ln_b, gmlp_w_s, gmlp_b_s, hgrn_onorm_g, fox_b_f):
    causal = jnp.tril(jnp.ones((CHUNK, CHUNK), bool))
    wm = jnp.where(causal[None], gmlp_w_s[l], 0.0)
    return dict(
        lng=gmlp_ln_g[l].reshape(1, A_WIDTH), lnb=gmlp_ln_b[l].reshape(1, A_WIDTH),
        wm=wm.astype(BF16), wmt=jnp.swapaxes(wm, 1, 2).astype(BF16),
        bst=jnp.pad(gmlp_b_s[l].T, ((0, 0), (0, 128 - A_GROUPS))),
        onorm=jnp.tile(hgrn_onorm_g[l], 4).reshape(1, B_WIDTH),
        bf=jnp.pad(fox_b_f[l], (0, 128 - C_HEADS)).reshape(1, 128),
    )


def kernel(x, norm_g, w_in, w_out, gmlp_ln_g, gmlp_ln_b, gmlp_w_s, gmlp_b_s, hgrn_lb, hgrn_onorm_g, fox_b_f, final_norm_g, loss_target, m_norm_g, m_w_in, m_w_out, m_gmlp_ln_g, m_gmlp_ln_b, m_gmlp_w_s, m_gmlp_b_s, m_hgrn_lb, m_hgrn_onorm_g, m_fox_b_f, m_final_norm_g, v_norm_g, v_w_in, v_w_out, v_gmlp_ln_g, v_gmlp_ln_b, v_gmlp_w_s, v_gmlp_b_s, v_hgrn_lb, v_hgrn_onorm_g, v_fox_b_f, v_final_norm_g):
    T = x.shape[1]
    shard_in = w_in.shape[2]
    shard_out = w_out.shape[1]
    xs = x.reshape(T, D_MODEL)
    tgt = loss_target.reshape(T, D_MODEL)

    w_in_b = [w_in[l].T.astype(BF16) for l in range(DEPTH)]
    w_out_b = w_out.astype(BF16)

    lb_all = _lb_fwd(hgrn_lb)
    consts = [_layer_consts(l, gmlp_ln_g, gmlp_ln_b, gmlp_w_s, gmlp_b_s, hgrn_onorm_g, fox_b_f) for l in range(DEPTH)]

    saved = []
    xl = xs
    w_in_l = _gather_halves(w_in_b[0], "w_in_l0")
    for l in range(DEPTH):
        cs = consts[l]
        tag = f"l{l}"
        h, proj = _inproj(xl, norm_g[l].reshape(1, D_MODEL), w_in_l, D_IN_PAD, tag)
        (ya,), (yb, ob, s0), (qt, kt, vt) = _run_parts(
            [_gmlp_fwd(proj, cs["lng"], cs["lnb"], cs["wm"], cs["bst"]),
             _hgrn_fwd(proj, lb_all[l].reshape(1, B_WIDTH), cs["onorm"]), _fox_prep(proj, cs["bf"])],
            (T // CHUNK,), f"mix_fwd_{tag}")
        ride = _ChipExchange("gather", (w_out_b[l],) + ((w_in_b[l + 1],) if l + 1 < DEPTH else ()))
        oc, lse, yc, *gathered = _fox_fwd(qt, kt, vt, proj, tag, ride)
        w_out_l = gathered[0].reshape(N_CHIPS * shard_out, D_MODEL)
        saved.append(dict(x=xl, h=h, proj=proj, ya=ya, yb=yb, yc=yc, ob=ob, s0=s0, qt=qt, kt=kt, oc=oc, lse=lse,
                          w_in=w_in_l, w_out=w_out_l))
        xl = _outproj(xl, ya, yb, yc, w_out_l, tag)
        if l + 1 < DEPTH:
            w_in_l = gathered[1]

    dx, loss_part, d_final = _loss_head(xl, final_norm_g.reshape(1, D_MODEL), tgt)

    g_small = {}
    dlb_rows, rin, rout = [None] * DEPTH, [None] * DEPTH, [None] * DEPTH
    slabs_in = None
    for l in reversed(range(DEPTH)):
        cs, sv = consts[l], saved[l]
        tag = f"l{l}"
        proj = sv["proj"]
        dy, dw_out = _outproj_bwd(dx, sv["ya"], sv["yb"], sv["yc"], sv["w_out"], tag)
        (da, dwm, dbst, dlng, dlnb), (db, dlb_rows[l], donorm) = _run_parts(
            [_gmlp_bwd(proj, dy, cs["lng"], cs["lnb"], cs["wm"], cs["wmt"], cs["bst"]),
             _hgrn_bwd(proj, dy, sv["ob"], sv["s0"], lb_all[l].reshape(1, B_WIDTH), cs["onorm"])],
            (T // CHUNK,), f"mix_bwd_{tag}")
        do, delta, dzc, dot, qtr = _fox_bwd_prep(proj, dy, sv["oc"], sv["qt"], tag)
        slabs_out = dw_out.reshape(N_CHIPS, shard_out, D_MODEL).astype(BF16)
        ride = _ChipExchange("scatter", (slabs_out,) + ((slabs_in,) if slabs_in is not None else ()))
        g_small[l] = dict(ln_g=dlng, ln_b=dlnb, w_s=dwm, b_s=dbst, onorm=donorm)
        if l == 0:
            d_hgrn_lb = _lb_bwd(hgrn_lb, jnp.concatenate(dlb_rows, axis=0))
            per_layer = lambda key: [g_small[k][key] for k in range(DEPTH)]
            dbf_known = [jnp.zeros((1, 128), F32)] + [g_small[k]["bf"] for k in range(1, DEPTH)]
            early = _pack_grads(per_layer("ln_g"), per_layer("ln_b"), per_layer("w_s"), per_layer("b_s"), d_hgrn_lb,
                                per_layer("onorm"), dbf_known, d_final, loss_part)
            ride = _Rides(ride, _DeviceGather(early))
        dqt, dkt, dvc, *received = _fox_bwd(sv["qt"], sv["kt"], proj, do, sv["lse"], delta, dot, qtr, tag, ride)
        rout[l] = received[0]
        if slabs_in is not None:
            rin[l + 1] = received[1]
        if l == 0:
            rearly = received[-1]
        dqc, dkc, dflc, g_small[l]["bf"] = _fox_bwd_post(dqt, dkt, proj, cs["bf"], tag)
        dproj = [da, db, dqc, dkc, dvc, dzc, dflc]
        if l == 0:
            ride, parts = None, []
            for n, cols in enumerate(DW_IN_GROUPS):
                part, *arrived = _dw_in(sv["h"], dproj, D_IN_PAD, shard_in, cols, f"{tag}_{n}", ride)
                parts += arrived
                ride = _ChipExchange("scatter", (part,))
        else:
            slabs_in, = _dw_in(sv["h"], dproj, D_IN_PAD, shard_in, (0, D_MODEL), tag)
            ride = None
        dx, dng, *received = _dx_in(sv["x"], norm_g[l].reshape(1, D_MODEL), dx, dproj, sv["w_in"], tag, ride)
        if l == 0:
            rin[0] = parts + received
        g_small[l]["norm_g"] = dng.reshape(D_MODEL // 128, 128)
    grad_x = dx.reshape(x.shape)
    dbf0 = jnp.where(_lane((1, 128)) < C_HEADS, g_small[0]["bf"], 0.0)
    late = jnp.concatenate([g_small[l]["norm_g"] for l in range(DEPTH)] + [jnp.pad(dbf0, ((0, 7), (0, 0)))])
    rlate = _gather_devices(late, "gather_late_grads")
    late_blocks = {0: 0, 7: DEPTH * D_MODEL // 128}

    pin, pout = _sum_chips(rin, "sum_chips_w_in", False), _sum_chips(rout, "sum_chips_w_out", True)
    oin, oout = _swap_cores(pin, pout)
    to_view = lambda a: jnp.transpose(a, (2, 0, 1))
    g_w_in, d_w_in, nm_w_in, nv_w_in = [
        jnp.transpose(o, (1, 2, 0))
        for o in _adamw_pair(to_view(w_in), to_view(m_w_in), to_view(v_w_in), pin, oin, "adamw_w_in")]
    g_w_out, d_w_out, nm_w_out, nv_w_out = _adamw_pair(w_out, m_w_out, v_w_out, pout, oout, "adamw_w_out")

    small_w = [norm_g, gmlp_ln_g, gmlp_ln_b, gmlp_w_s, gmlp_b_s, hgrn_lb, hgrn_onorm_g, fox_b_f, final_norm_g]
    small_m = [m_norm_g, m_gmlp_ln_g, m_gmlp_ln_b, m_gmlp_w_s, m_gmlp_b_s, m_hgrn_lb, m_hgrn_onorm_g, m_fox_b_f, m_final_norm_g]
    small_v = [v_norm_g, v_gmlp_ln_g, v_gmlp_ln_b, v_gmlp_w_s, v_gmlp_b_s, v_hgrn_lb, v_hgrn_onorm_g, v_fox_b_f, v_final_norm_g]
    views = lambda ps: [p.reshape(shape) for p, (shape, _) in zip(ps, _small_layout())]
    per_param, loss_row = _adamw_small(views(small_w), views(small_m), views(small_v), rearly, rlate, late_blocks)
    sg, sd, sm, sv_ = [[per_param[k][a].reshape(shape) for k, (_, shape) in enumerate(SMALL_PARAMS)] for a in range(4)]
    loss = loss_row[0, 0]

    def order(big_in, big_out, small):
        return [small[0], big_in, big_out] + small[1:]

    return (loss, grad_x, *order(g_w_in, g_w_out, sg), *order(d_w_in, d_w_out, sd), *order(nm_w_in, nm_w_out, sm),
            *order(nv_w_in, nv_w_out, sv_))
```

```python
import collections
import functools
import math

import jax
import jax.numpy as jnp
from jax import lax
from jax.experimental import pallas as pl
from jax.experimental.pallas import tpu as pltpu

F32 = jnp.float32
BF16 = jnp.bfloat16
SDS = jax.ShapeDtypeStruct
MESH_ID = pl.DeviceIdType.MESH

D_MODEL = 1024
DEPTH = 2
A_WIDTH = 256
A_GROUPS = 4
B_WIDTH = 256
C_WIDTH = 512
C_HEADS = 8
D_IN = 3848
D_IN_PAD = 4096
CHUNK = 128
SUB = 16
SUB_SHIFT = 4
NORM_EPS = 1e-6
F_FLOOR = 1e-30
COL_AU, COL_AV, COL_AZ = 0, 256, 512
COL_BQ, COL_BF, COL_BI, COL_BZ = 768, 1024, 1280, 1536
COL_CQ, COL_CK, COL_CV, COL_CZ, COL_CF = 1792, 2304, 2816, 3328, 3840
HEAD_LANES = 128
Q_SCALE = 0.125
ADAM_LR, ADAM_B1, ADAM_B2, ADAM_EPS, ADAM_WD, ADAM_STEP = 0.001, 0.9, 0.999, 1e-08, 0.01, 10
ADAM_C1 = 1.0 - ADAM_B1 ** ADAM_STEP
ADAM_C2 = 1.0 - ADAM_B2 ** ADAM_STEP
VMEM_LIMIT = 56 * 1024 * 1024
ADAMW_BLOCK_BYTES = 1 << 20
N_CHIPS = 4
N_DEV = 8

SMALL_PARAMS = (
    ("norm_g", (DEPTH, D_MODEL)), ("gmlp_ln_g", (DEPTH, 4, 64)), ("gmlp_ln_b", (DEPTH, 4, 64)),
    ("gmlp_w_s", (DEPTH, 4, 128, 128)), ("gmlp_b_s", (DEPTH, 4, 128)), ("hgrn_lb", (DEPTH, 256)),
    ("hgrn_onorm_g", (DEPTH, 64)), ("fox_b_f", (DEPTH, 8)), ("final_norm_g", (D_MODEL,)),
)


def _tile(n, pref):
    t = min(n, pref)
    assert n % t == 0, (n, pref)
    return t


def _params(*sem):
    return pltpu.CompilerParams(dimension_semantics=sem, vmem_limit_bytes=VMEM_LIMIT)


_Part = collections.namedtuple("_Part", "body operands in_specs out_specs out_shape scratch")


def _run_parts(parts, grid, name):
    counts = [(len(p.operands), len(p.out_shape), len(p.scratch)) for p in parts]

    def body(*refs):
        ins, outs, scr = [], [], []
        pos = 0
        for group, k in ((ins, 0), (outs, 1), (scr, 2)):
            for c in counts:
                group.append(refs[pos:pos + c[k]])
                pos += c[k]
        for p, i, o, s in zip(parts, ins, outs, scr):
            p.body(*i, *o, *s)

    flat = lambda key: [x for p in parts for x in getattr(p, key)]
    res = pl.pallas_call(
        body, name=name, grid=grid, in_specs=flat("in_specs"), out_specs=flat("out_specs"), out_shape=flat("out_shape"),
        scratch_shapes=flat("scratch"), compiler_params=_params(*(("arbitrary",) * len(grid))),
    )(*flat("operands"))
    out, pos = [], 0
    for c in counts:
        out.append(list(res[pos:pos + c[1]]))
        pos += c[1]
    return out


def _dot(a, b):
    return jnp.dot(a, b, preferred_element_type=F32)


def _dot_nt(a, b):
    return lax.dot_general(a, b, (((1,), (1,)), ((), ())), preferred_element_type=F32)


def _dot_tn(a, b):
    return lax.dot_general(a, b, (((0,), (0,)), ((), ())), preferred_element_type=F32)


def _split3(x):
    hi = x.astype(BF16)
    r = x - hi.astype(F32)
    mid = r.astype(BF16)
    lo = (r - mid.astype(F32)).astype(BF16)
    return hi, mid, lo


def _dot3_left(c, x):
    hi, mid, lo = _split3(x)
    return _dot(c, hi) + _dot(c, mid) + _dot(c, lo)


def _sigmoid(x):
    return jax.nn.sigmoid(x)


def _silu_and_grad(x):
    s = _sigmoid(x)
    return x * s, s * (1.0 + x * (1.0 - s))


_GELU_C = math.sqrt(2.0 / math.pi)


def _gelu_and_grad(x):
    inner = _GELU_C * (x + 0.044715 * x * x * x)
    t = jnp.tanh(inner)
    y = 0.5 * x * (1.0 + t)
    dy = 0.5 * (1.0 + t) + 0.5 * x * (1.0 - t * t) * _GELU_C * (1.0 + 3.0 * 0.044715 * x * x)
    return y, dy


def _lane(shape):
    return lax.broadcasted_iota(jnp.int32, shape, 1)


def _row(shape):
    return lax.broadcasted_iota(jnp.int32, shape, 0)


def _gsum64(x):
    lo = _lane(x.shape) < 64
    s0 = jnp.sum(jnp.where(lo, x, 0.0), axis=-1, keepdims=True)
    s1 = jnp.sum(jnp.where(lo, 0.0, x), axis=-1, keepdims=True)
    return jnp.where(lo, s0, s1)


def _colreduce(x, op):
    parts = [x[r:r + 8, :] for r in range(0, x.shape[0], 8)]
    while len(parts) > 1:
        pairs = [op(parts[k], parts[k + 1]) for k in range(0, len(parts) - 1, 2)]
        parts = pairs + ([parts[-1]] if len(parts) % 2 else [])
    red = jnp.max if op is jnp.maximum else jnp.sum
    return red(parts[0], axis=0, keepdims=True)


def _block_diag64(dtype=BF16):
    r, c = _row((128, 128)), _lane((128, 128))
    return jnp.where((r >> 6) == (c >> 6), 1.0, 0.0).astype(dtype)


def _assemble_w_in(slab_ref, wt_ref):
    shard = slab_ref.shape[1]
    top = N_CHIPS * shard // 16 * 16
    wt_ref[top:, :] = jnp.zeros((wt_ref.shape[0] - top, wt_ref.shape[1]), wt_ref.dtype)
    for k in range(N_CHIPS):
        wt_ref[shard * k:shard * (k + 1), :] = slab_ref[k]


def _inproj(x, g, w, dp_width, tag):
    T, D = x.shape
    tm = _tile(T, 512)

    def body(x_ref, g_ref, w_ref, h_ref, p_ref, wt_ref):
        pl.when(pl.program_id(0) == 0)(lambda: _assemble_w_in(w_ref, wt_ref))
        xv = x_ref[...]
        r = lax.rsqrt(jnp.mean(xv * xv, axis=-1, keepdims=True) + NORM_EPS)
        h = (xv * r * g_ref[...]).astype(BF16)
        h_ref[...] = h
        p_ref[...] = _dot_nt(h, wt_ref[...])

    return pl.pallas_call(
        body, name=f"inproj_{tag}", grid=(T // tm,),
        in_specs=[pl.BlockSpec((tm, D), lambda i: (i, 0)), pl.BlockSpec((1, D), lambda i: (0, 0)),
                  pl.BlockSpec(w.shape, lambda i: (0, 0, 0))],
        out_specs=[pl.BlockSpec((tm, D), lambda i: (i, 0)), pl.BlockSpec((tm, dp_width), lambda i: (i, 0))],
        out_shape=[SDS((T, D), BF16), SDS((T, dp_width), F32)],
        scratch_shapes=[pltpu.VMEM((dp_width, D), BF16)],
        compiler_params=_params("arbitrary"),
    )(x, g, w)


def _outproj(x, ya, yb, yc, wo, tag, head=None):
    T, D = x.shape
    tm = _tile(T, 512)

    def residual(x_ref, ya_ref, yb_ref, yc_ref, wo_ref):
        acc = x_ref[...] + _dot(ya_ref[...], wo_ref[0:A_WIDTH, :])
        acc = acc + _dot(yb_ref[...], wo_ref[A_WIDTH:A_WIDTH + B_WIDTH, :])
        return acc + _dot(yc_ref[...], wo_ref[A_WIDTH + B_WIDTH:, :])

    def body(x_ref, ya_ref, yb_ref, yc_ref, wo_ref, o_ref):
        o_ref[...] = residual(x_ref, ya_ref, yb_ref, yc_ref, wo_ref)

    def body_head(x_ref, ya_ref, yb_ref, yc_ref, wo_ref, g_ref, t_ref, dx_ref, loss_ref, dg_ref):
        @pl.when(pl.program_id(0) == 0)
        def _():
            loss_ref[...] = jnp.zeros_like(loss_ref)
            dg_ref[...] = jnp.zeros_like(dg_ref)

        xv = residual(x_ref, ya_ref, yb_ref, yc_ref, wo_ref)
        r = lax.rsqrt(jnp.mean(xv * xv, axis=-1, keepdims=True) + NORM_EPS)
        xh = xv * r
        gv = g_ref[...]
        err = xh * gv - t_ref[...]
        tok = jnp.mean(err * err, axis=-1, keepdims=True)
        loss_ref[...] += 0.5 * jnp.sum(tok, axis=0, keepdims=True)
        dy = err * (1.0 / D)
        dg_ref[...] += jnp.sum(dy * xh, axis=0, keepdims=True)
        dxh = dy * gv
        dx_ref[...] = r * (dxh - xh * jnp.mean(dxh * xh, axis=-1, keepdims=True))

    row = lambda w: pl.BlockSpec((tm, w), lambda i: (i, 0))
    fixed = lambda shape: pl.BlockSpec(shape, lambda i: (0, 0))
    in_specs = [row(D), row(A_WIDTH), row(B_WIDTH), row(C_WIDTH), fixed(wo.shape)]
    if head is None:
        return pl.pallas_call(
            body, name=f"outproj_{tag}", grid=(T // tm,), in_specs=in_specs,
            out_specs=row(D), out_shape=SDS((T, D), F32), compiler_params=_params("parallel"),
        )(x, ya, yb, yc, wo)
    return pl.pallas_call(
        body_head, name=f"outproj_loss_{tag}", grid=(T // tm,), in_specs=in_specs + [fixed((1, D)), row(D)],
        out_specs=[row(D), fixed((1, 128)), fixed((1, D))],
        out_shape=[SDS((T, D), F32), SDS((1, 128), F32), SDS((1, D), F32)], compiler_params=_params("arbitrary"),
    )(x, ya, yb, yc, wo, *head)


def _outproj_bwd(dx, ya, yb, yc, wo, tag):
    T, D = dx.shape
    DM = wo.shape[0]
    tm = _tile(T, 512)

    def body(dx_ref, ya_ref, yb_ref, yc_ref, wo_ref, dy_ref, dwo_ref):
        @pl.when(pl.program_id(0) == 0)
        def _():
            dwo_ref[...] = jnp.zeros_like(dwo_ref)

        dxb = dx_ref[...].astype(BF16)
        dy_ref[...] = _dot_nt(dxb, wo_ref[...])
        dwo_ref[0:A_WIDTH, :] += _dot_tn(ya_ref[...], dxb)
        dwo_ref[A_WIDTH:A_WIDTH + B_WIDTH, :] += _dot_tn(yb_ref[...], dxb)
        dwo_ref[A_WIDTH + B_WIDTH:, :] += _dot_tn(yc_ref[...], dxb)

    row = lambda w: pl.BlockSpec((tm, w), lambda i: (i, 0))
    return pl.pallas_call(
        body, name=f"outproj_bwd_{tag}", grid=(T // tm,),
        in_specs=[row(D), row(A_WIDTH), row(B_WIDTH), row(C_WIDTH), pl.BlockSpec(wo.shape, lambda i: (0, 0))],
        out_specs=[row(DM), pl.BlockSpec((DM, D), lambda i: (0, 0))],
        out_shape=[SDS((T, DM), F32), SDS((DM, D), F32)], compiler_params=_params("arbitrary"),
    )(dx, ya, yb, yc, wo)


DW_IN_GROUPS = ((0, 256), (256, 256), (512, 512))


def _piece_offsets(pieces):
    offs = [0]
    for p in pieces:
        offs.append(offs[-1] + p.shape[1])
    return offs


def _dw_in(h, pieces, dp_width, shard, cols, tag, ride=None):
    T = h.shape[0]
    first, D = cols
    assert N_CHIPS * shard <= dp_width and first % D == 0
    tm = _tile(T, 512)
    grid = (T // tm,)
    offs = _piece_offsets(pieces)
    n = len(pieces)

    def body(h_ref, *rest):
        p_refs, rest = rest[:n], rest[n:]
        ride_srcs, (dw_ref,), ride_dsts, (acc_ref,), ride_sems = _ride_refs(ride, rest, 1, 1)
        i = pl.program_id(0)
        _ride_start(ride, grid, ride_srcs, ride_dsts, ride_sems)

        @pl.when(i == 0)
        def _():
            acc_ref[...] = jnp.zeros_like(acc_ref)

        hv = h_ref[...]
        for k, p_ref in enumerate(p_refs):
            acc_ref[offs[k]:offs[k + 1], :] += _dot_tn(p_ref[...], hv)

        @pl.when(i == grid[0] - 1)
        def _():
            for k in range(N_CHIPS):
                dw_ref[k] = acc_ref[shard * k:shard * (k + 1), :].astype(BF16)

        _ride_wait(ride, grid, ride_srcs, ride_dsts, ride_sems)

    extra = ride or _ChipExchange("gather", ())
    return pl.pallas_call(
        body, name=f"dw_in_{tag}", grid=grid,
        in_specs=[pl.BlockSpec((tm, D), lambda i: (i, first // D))]
        + [pl.BlockSpec((tm, p.shape[1]), lambda i: (i, 0)) for p in pieces] + extra.in_specs,
        out_specs=[pl.BlockSpec((N_CHIPS, shard, D), lambda i: (0, 0, 0))] + extra.out_specs,
        out_shape=[SDS((N_CHIPS, shard, D), BF16)] + extra.out_shape,
        scratch_shapes=[pltpu.VMEM((dp_width, D), F32)] + (extra.scratch if ride else []),
        compiler_params=pltpu.CompilerParams(dimension_semantics=("arbitrary",), vmem_limit_bytes=VMEM_LIMIT,
                                             has_side_effects=bool(ride)),
    )(h, *pieces, *extra.sources)


def _dx_in(x, g, dres, pieces, w, tag, ride=None):
    T, D = x.shape
    tm = _tile(T, 512)
    grid = (T // tm,)
    offs = _piece_offsets(pieces)
    n = len(pieces)

    def body(x_ref, g_ref, dres_ref, w_ref, *rest):
        p_refs, rest = rest[:n], rest[n:]
        ride_srcs, (dx_ref, dg_ref), ride_dsts, (wt_ref,), ride_sems = _ride_refs(ride, rest, 2, 1)
        _ride_start(ride, grid, ride_srcs, ride_dsts, ride_sems)

        @pl.when(pl.program_id(0) == 0)
        def _():
            dg_ref[...] = jnp.zeros_like(dg_ref)
            _assemble_w_in(w_ref, wt_ref)

        dh = _dot(p_refs[0][...], wt_ref[offs[0]:offs[1], :])
        for k in range(1, n):
            dh = dh + _dot(p_refs[k][...], wt_ref[offs[k]:offs[k + 1], :])
        xv = x_ref[...]
        r = lax.rsqrt(jnp.mean(xv * xv, axis=-1, keepdims=True) + NORM_EPS)
        xh = xv * r
        dg_ref[...] += jnp.sum(dh * xh, axis=0, keepdims=True)
        dxh = dh * g_ref[...]
        dx_ref[...] = dres_ref[...] + r * (dxh - xh * jnp.mean(dxh * xh, axis=-1, keepdims=True))
        _ride_wait(ride, grid, ride_srcs, ride_dsts, ride_sems)

    extra = ride or _ChipExchange("gather", ())
    row = pl.BlockSpec((tm, D), lambda i: (i, 0))
    return pl.pallas_call(
        body, name=f"dx_in_{tag}", grid=grid,
        in_specs=[row, pl.BlockSpec((1, D), lambda i: (0, 0)), row, pl.BlockSpec(w.shape, lambda i: (0, 0, 0))]
        + [pl.BlockSpec((tm, p.shape[1]), lambda i: (i, 0)) for p in pieces] + extra.in_specs,
        out_specs=[row, pl.BlockSpec((1, D), lambda i: (0, 0))] + extra.out_specs,
        out_shape=[SDS((T, D), F32), SDS((1, D), F32)] + extra.out_shape,
        scratch_shapes=[pltpu.VMEM((offs[-1], D), BF16)] + (extra.scratch if ride else []),
        compiler_params=pltpu.CompilerParams(dimension_semantics=("arbitrary",), vmem_limit_bytes=VMEM_LIMIT,
                                             has_side_effects=bool(ride)),
    )(x, g, dres, w, *pieces, *extra.sources)


def _gmlp_core(u, v, lng, lnb, wm_ref, bst_ref, pair):
    ug, dug = _gelu_and_grad(u)
    vg, dvg = _gelu_and_grad(v)
    mu = _gsum64(vg) * (1.0 / 64)
    d = vg - mu
    var = _gsum64(d * d) * (1.0 / 64)
    rstd = lax.rsqrt(var + NORM_EPS)
    xh = d * rstd
    vn = xh * lng + lnb
    vnb = vn.astype(BF16)
    lo = _lane(u.shape) < 64
    g0, g1 = 2 * pair, 2 * pair + 1
    mixed = jnp.where(lo, _dot(wm_ref[g0], vnb) + bst_ref[:, g0:g0 + 1], _dot(wm_ref[g1], vnb) + bst_ref[:, g1:g1 + 1])
    return ug, dug, dvg, rstd, xh, vnb, mixed, lo


def _gmlp_fwd(proj, lng, lnb, wm, bst):
    T = proj.shape[0]

    def body(u_ref, v_ref, z_ref, lng_ref, lnb_ref, wm_ref, bst_ref, y_ref):
        for pair in range(2):
            sl = slice(128 * pair, 128 * pair + 128)
            ug, _, _, _, _, _, mixed, _ = _gmlp_core(u_ref[:, sl], v_ref[:, sl], lng_ref[:, sl], lnb_ref[:, sl],
                                                     wm_ref, bst_ref, pair)
            sz, _ = _silu_and_grad(z_ref[:, sl])
            y_ref[:, sl] = (ug * mixed * sz).astype(BF16)

    col = lambda c: pl.BlockSpec((CHUNK, A_WIDTH), lambda i, c=c: (i, c // A_WIDTH))
    full = lambda a: pl.BlockSpec(a.shape, lambda i, n=a.ndim: (0,) * n)
    return _Part(body, (proj, proj, proj, lng, lnb, wm, bst),
                 [col(COL_AU), col(COL_AV), col(COL_AZ), full(lng), full(lnb), full(wm), full(bst)],
                 [pl.BlockSpec((CHUNK, A_WIDTH), lambda i: (i, 0))], [SDS((T, A_WIDTH), BF16)], [])


def _gmlp_bwd(proj, dy, lng, lnb, wm, wmt, bst):
    T = proj.shape[0]
    n = T // CHUNK

    def body(u_ref, v_ref, z_ref, dy_ref, lng_ref, lnb_ref, wm_ref, wmt_ref, bst_ref,
             da_ref, dwm_ref, dbst_ref, dlng_ref, dlnb_ref):
        @pl.when(pl.program_id(0) == 0)
        def _():
            dwm_ref[...] = jnp.zeros_like(dwm_ref)
            dbst_ref[...] = jnp.zeros_like(dbst_ref)
            dlng_ref[...] = jnp.zeros_like(dlng_ref)
            dlnb_ref[...] = jnp.zeros_like(dlnb_ref)

        lane = _lane((CHUNK, 128))
        dbst = dbst_ref[...]
        for pair in range(2):
            sl = slice(128 * pair, 128 * pair + 128)
            lng_p = lng_ref[:, sl]
            ug, dug, dvg, rstd, xh, vnb, mixed, lo = _gmlp_core(u_ref[:, sl], v_ref[:, sl], lng_p, lnb_ref[:, sl],
                                                                wm_ref, bst_ref, pair)
            sz, dsz = _silu_and_grad(z_ref[:, sl])
            dyv = dy_ref[:, sl]
            out = ug * mixed
            dz = dyv * out * dsz
            dout = dyv * sz
            du = dout * mixed * dug
            dmix = dout * ug
            g0, g1 = 2 * pair, 2 * pair + 1
            dm0 = jnp.where(lo, dmix, 0.0)
            dm1 = jnp.where(lo, 0.0, dmix)
            dbst = dbst + jnp.where(lane == g0, jnp.sum(dm0, axis=-1, keepdims=True), 0.0)
            dbst = dbst + jnp.where(lane == g1, jnp.sum(dm1, axis=-1, keepdims=True), 0.0)
            dwm_ref[g0] += _dot_nt(dm0.astype(BF16), vnb)
            dwm_ref[g1] += _dot_nt(dm1.astype(BF16), vnb)
            dmb = dmix.astype(BF16)
            dvn = jnp.where(lo, _dot(wmt_ref[g0], dmb), _dot(wmt_ref[g1], dmb))
            dlng_ref[:, sl] += jnp.sum(dvn * xh, axis=0, keepdims=True)
            dlnb_ref[:, sl] += jnp.sum(dvn, axis=0, keepdims=True)
            dxh = dvn * lng_p
            m1 = _gsum64(dxh) * (1.0 / 64)
            m2 = _gsum64(dxh * xh) * (1.0 / 64)
            dv = rstd * (dxh - m1 - xh * m2) * dvg
            da_ref[:, COL_AU + 128 * pair:COL_AU + 128 * pair + 128] = du.astype(BF16)
            da_ref[:, COL_AV + 128 * pair:COL_AV + 128 * pair + 128] = dv.astype(BF16)
            da_ref[:, COL_AZ + 128 * pair:COL_AZ + 128 * pair + 128] = dz.astype(BF16)
        dbst_ref[...] = dbst

        @pl.when(pl.program_id(0) == n - 1)
        def _():
            causal = _lane((CHUNK, CHUNK)) <= _row((CHUNK, CHUNK))
            for g in range(A_GROUPS):
                dwm_ref[g] = jnp.where(causal, dwm_ref[g], 0.0)

    col = lambda c: pl.BlockSpec((CHUNK, A_WIDTH), lambda i, c=c: (i, c // A_WIDTH))
    full = lambda a: pl.BlockSpec(a.shape, lambda i, n=a.ndim: (0,) * n)
    acc = lambda s: pl.BlockSpec(s, lambda i, n=len(s): (0,) * n)
    return _Part(body, (proj, proj, proj, dy, lng, lnb, wm, wmt, bst),
                 [col(COL_AU), col(COL_AV), col(COL_AZ), pl.BlockSpec((CHUNK, A_WIDTH), lambda i: (i, 0)),
                  full(lng), full(lnb), full(wm), full(wmt), full(bst)],
                 [pl.BlockSpec((CHUNK, 3 * A_WIDTH), lambda i: (i, 0)), acc((A_GROUPS, CHUNK, CHUNK)),
                  acc((CHUNK, 128)), acc((1, A_WIDTH)), acc((1, A_WIDTH))],
                 [SDS((T, 3 * A_WIDTH), BF16), SDS((A_GROUPS, CHUNK, CHUNK), F32), SDS((CHUNK, 128), F32),
                  SDS((1, A_WIDTH), F32), SDS((1, A_WIDTH), F32)], [])


def _hgrn_consts():
    r, c = _row((CHUNK, CHUNK)), _lane((CHUNK, CHUNK))
    same = (r >> SUB_SHIFT) == (c >> SUB_SHIFT)
    lsub = jnp.where(same & (c <= r), 1.0, 0.0).astype(BF16)
    usub = jnp.where(same & (c >= r), 1.0, 0.0).astype(BF16)
    bsub = jnp.where(same, 1.0, 0.0).astype(BF16)
    return lsub, usub, bsub


def _hgrn_gates(qv, zf, lbp):
    sq, dsq = _silu_and_grad(qv)
    qt = sq * Q_SCALE
    sg = _sigmoid(zf)
    sgn = _sigmoid(-zf)
    f = lbp + (1.0 - lbp) * sg
    g = jnp.log(jnp.maximum(f, F_FLOOR))
    kf = (1.0 - lbp) * sgn
    return qt, dsq, sg, sgn, f, g, kf


def _hgrn_intra_scores(qt, kf, b, mbd):
    rid = _row((SUB, 128))
    parts = []
    for s in range(SUB):
        e = jnp.exp(b - b[s:s + 1, :])
        parts.append(jnp.where(rid >= s, qt * kf[s:s + 1, :] * e, 0.0))
    return _dot(jnp.concatenate(parts, axis=0).astype(BF16), mbd)


def _hgrn_intra_out(a, v):
    o = jnp.zeros((SUB, 128), F32)
    for s in range(SUB):
        o = o + a[SUB * s:SUB * s + SUB, :] * v[s:s + 1, :]
    return o


def _hgrn_intra_bwd_scores(qt, kf, b, v, do, mbd):
    rid = _row((SUB, 128))
    ps, das, kes, es = [], [], [], []
    for s in range(SUB):
        e = jnp.where(rid >= s, jnp.exp(b - b[s:s + 1, :]), 0.0)
        ke = kf[s:s + 1, :] * e
        es.append(e)
        kes.append(ke)
        ps.append(qt * ke)
        das.append(do * v[s:s + 1, :])
    a = _dot(jnp.concatenate(ps, axis=0).astype(BF16), mbd)
    da = _dot(jnp.concatenate(das, axis=0).astype(BF16), mbd)
    return a, da, kes, es


def _hgrn_intra_bwd_grads(scores, qt, do, rsum):
    a, da, kes, es = scores
    dqt = jnp.zeros((SUB, 128), F32)
    xs, ys = [], []
    for s in range(SUB):
        da_s = da[SUB * s:SUB * s + SUB, :]
        dqt = dqt + da_s * kes[s]
        xs.append(a[SUB * s:SUB * s + SUB, :] * do)
        ys.append(da_s * qt * es[s])
    dv = _dot(rsum, jnp.concatenate(xs, axis=0).astype(BF16))
    dkf = _dot(rsum, jnp.concatenate(ys, axis=0).astype(BF16))
    return dqt, dkf, dv


def _hgrn_norm_gate(o, z, onorm):
    ms = _gsum64(o * o) * (1.0 / 64)
    r = lax.rsqrt(ms + NORM_EPS)
    xh = o * r
    sz, dsz = _silu_and_grad(z)
    return xh, r, sz, dsz, xh * onorm


def _hgrn_fwd(proj, lb, onorm):
    T = proj.shape[0]
    n = T // CHUNK
    nsub = CHUNK // SUB

    def body(q_ref, f_ref, i_ref, z_ref, lb_ref, on_ref, y_ref, o_ref, s0_ref, st_ref):
        @pl.when(pl.program_id(0) == 0)
        def _():
            st_ref[...] = jnp.zeros_like(st_ref)

        lsub, _, bsub = _hgrn_consts()
        mbd = _block_diag64()
        bdmask = mbd > 0
        rid = _row((CHUNK, 128))
        subs = [slice(SUB * sub, SUB * sub + SUB) for sub in range(nsub)]
        work = []
        for pair in range(2):
            sl = slice(128 * pair, 128 * pair + 128)
            qt, _, _, _, _, g, kf = _hgrn_gates(q_ref[:, sl], f_ref[:, sl], lb_ref[:, sl])
            work.append(dict(sl=sl, qt=qt, kf=kf, v=i_ref[:, sl], b=_dot3_left(lsub, g), bl=_dot3_left(bsub, g)))
        for w in work:
            qt, kf, v, b, bl = w["qt"], w["kf"], w["v"], w["b"], w["bl"]
            w["qh"] = (qt * jnp.exp(b)).astype(BF16)
            kh = kf * jnp.exp(bl - b)
            w["dec"] = jnp.exp(bl)
            vtb = v.T.astype(BF16)
            w["scores"] = [_hgrn_intra_scores(qt[rs], kf[rs], b[rs], mbd) for rs in subs]
            w["adds"] = [_dot(vtb, jnp.where((rid >> SUB_SHIFT) == sub, kh, 0.0).astype(BF16)) for sub in range(nsub)]
        for pair, w in enumerate(work):
            w["st"] = st_ref[pair]
            s0_ref[0, pair] = w["st"]
            w["outs"] = []
        for sub, rs in enumerate(subs):
            for w in work:
                w["outs"].append(_dot_nt(w["qh"][rs], w["st"].astype(BF16)) + _hgrn_intra_out(w["scores"][sub], w["v"][rs]))
                w["st"] = jnp.where(bdmask, w["st"] * w["dec"][SUB * sub:SUB * sub + 1, :] + w["adds"][sub], 0.0)
        for pair, w in enumerate(work):
            sl = w["sl"]
            st_ref[pair] = w["st"]
            o = jnp.concatenate(w["outs"], axis=0)
            o_ref[:, sl] = o
            _, _, sz, _, on = _hgrn_norm_gate(o, z_ref[:, sl], on_ref[:, sl])
            y_ref[:, sl] = (on * sz).astype(BF16)

    col = lambda c: pl.BlockSpec((CHUNK, B_WIDTH), lambda i, c=c: (i, c // B_WIDTH))
    full = lambda a: pl.BlockSpec(a.shape, lambda i, n=a.ndim: (0,) * n)
    return _Part(body, (proj, proj, proj, proj, lb, onorm),
                 [col(COL_BQ), col(COL_BF), col(COL_BI), col(COL_BZ), full(lb), full(onorm)],
                 [pl.BlockSpec((CHUNK, B_WIDTH), lambda i: (i, 0)), pl.BlockSpec((CHUNK, B_WIDTH), lambda i: (i, 0)),
                  pl.BlockSpec((1, 2, 128, 128), lambda i: (i, 0, 0, 0))],
                 [SDS((T, B_WIDTH), BF16), SDS((T, B_WIDTH), F32), SDS((n, 2, 128, 128), F32)],
                 [pltpu.VMEM((2, 128, 128), F32)])


def _hgrn_bwd(proj, dy, o_saved, s0, lb, onorm):
    T = proj.shape[0]
    n = T // CHUNK
    nsub = CHUNK // SUB

    def body(q_ref, f_ref, i_ref, z_ref, dy_ref, o_ref, s0_ref, lb_ref, on_ref,
             db_ref, dlb_ref, don_ref, dst_ref, sts_ref):
        @pl.when(pl.program_id(0) == 0)
        def _():
            dst_ref[...] = jnp.zeros_like(dst_ref)
            dlb_ref[...] = jnp.zeros_like(dlb_ref)
            don_ref[...] = jnp.zeros_like(don_ref)

        lsub, usub, bsub = _hgrn_consts()
        mbd = _block_diag64()
        bdmask = mbd > 0
        rsum = jnp.where((_lane((SUB, SUB * SUB)) >> SUB_SHIFT) == _row((SUB, SUB * SUB)), 1.0, 0.0).astype(BF16)
        subs = [slice(SUB * sub, SUB * sub + SUB) for sub in range(nsub)]
        work = []
        for pair in range(2):
            sl = slice(128 * pair, 128 * pair + 128)
            lbp = lb_ref[:, sl]
            qt, dsq, sg, sgn, f, g, kf = _hgrn_gates(q_ref[:, sl], f_ref[:, sl], lbp)
            w = dict(sl=sl, lbp=lbp, qt=qt, dsq=dsq, sg=sg, sgn=sgn, f=f, kf=kf, v=i_ref[:, sl],
                     b=_dot3_left(lsub, g), bl=_dot3_left(bsub, g))
            onp = on_ref[:, sl]
            xh, r, sz, dsz, on = _hgrn_norm_gate(o_ref[:, sl], z_ref[:, sl], onp)
            dyv = dy_ref[:, sl]
            w["dz"] = dyv * on * dsz
            don = dyv * sz
            cn = jnp.sum(don * xh, axis=0, keepdims=True)
            don_ref[...] += cn + pltpu.roll(cn, 64, axis=1)
            dxo = don * onp
            w["do"] = r * (dxo - xh * (_gsum64(dxo * xh) * (1.0 / 64)))
            work.append(w)
        for w in work:
            qt, kf, v, b, bl, do = w["qt"], w["kf"], w["v"], w["b"], w["bl"], w["do"]
            w["eb"] = jnp.exp(b)
            w["ekb"] = jnp.exp(bl - b)
            w["qhb"] = (qt * w["eb"]).astype(BF16)
            w["khb"] = (kf * w["ekb"]).astype(BF16)
            w["dec"] = jnp.exp(bl)
            w["vb"] = v.astype(BF16)
            w["dob"] = do.astype(BF16)
            w["scores"] = [_hgrn_intra_bwd_scores(qt[rs], kf[rs], b[rs], v[rs], do[rs], mbd) for rs in subs]
            w["st_adds"] = [_dot_tn(w["vb"][rs], w["khb"][rs]) for rs in subs]
            w["gst_adds"] = [_dot_tn(w["dob"][rs], w["qhb"][rs]) for rs in subs]
        for pair, w in enumerate(work):
            w["st"] = s0_ref[0, pair]
        for sub in range(nsub):
            for pair, w in enumerate(work):
                sts_ref[pair, sub] = w["st"]
                w["st"] = jnp.where(bdmask, w["st"] * w["dec"][SUB * sub:SUB * sub + 1, :] + w["st_adds"][sub], 0.0)
        for pair, w in enumerate(work):
            w["gst"] = dst_ref[pair]
            w["dqt_p"], w["dkf_p"], w["dv_p"], w["dbl_p"] = ([None] * nsub for _ in range(4))
        for sub in reversed(range(nsub)):
            rs = subs[sub]
            for pair, w in enumerate(work):
                gst = w["gst"]
                st_in = sts_ref[pair, sub]
                gb = gst.astype(BF16)
                dqh = _dot(w["dob"][rs], st_in.astype(BF16))
                dkh = _dot(w["vb"][rs], gb)
                dv_inter = _dot_nt(w["khb"][rs], gb)
                ddec = jnp.sum(gst * st_in, axis=0, keepdims=True)
                dec_row = w["dec"][SUB * sub:SUB * sub + 1, :]
                w["gst"] = jnp.where(bdmask, gst * dec_row + w["gst_adds"][sub], 0.0)
                dqt_i, dkf_i, dv_i = _hgrn_intra_bwd_grads(w["scores"][sub], w["qt"][rs], w["do"][rs], rsum)
                dkf_inter = dkh * w["ekb"][rs]
                w["dqt_p"][sub] = dqh * w["eb"][rs] + dqt_i
                w["dkf_p"][sub] = dkf_inter + dkf_i
                w["dv_p"][sub] = dv_inter + dv_i
                row = jnp.sum(w["kf"][rs] * dkf_inter, axis=0, keepdims=True) + ddec * dec_row
                w["dbl_p"][sub] = jnp.broadcast_to(row, (SUB, 128))
        for pair, w in enumerate(work):
            sl, lbp, sg, sgn, f = w["sl"], w["lbp"], w["sg"], w["sgn"], w["f"]
            dst_ref[pair] = w["gst"]
            dqt = jnp.concatenate(w["dqt_p"], axis=0)
            dkf = jnp.concatenate(w["dkf_p"], axis=0)
            dv = jnp.concatenate(w["dv_p"], axis=0)
            dg = _dot3_left(usub, w["qt"] * dqt - w["kf"] * dkf) + jnp.concatenate(w["dbl_p"], axis=0)
            df = jnp.where(f > F_FLOOR, dg / f, 0.0)
            dlb_ref[:, sl] += jnp.sum(df * (1.0 - sg) - dkf * sgn, axis=0, keepdims=True)
            dfl = (1.0 - lbp) * sg * sgn * (df - dkf)
            dq = dqt * Q_SCALE * w["dsq"]
            db_ref[:, 0 * B_WIDTH + 128 * pair:0 * B_WIDTH + 128 * pair + 128] = dq.astype(BF16)
            db_ref[:, 1 * B_WIDTH + 128 * pair:1 * B_WIDTH + 128 * pair + 128] = dfl.astype(BF16)
            db_ref[:, 2 * B_WIDTH + 128 * pair:2 * B_WIDTH + 128 * pair + 128] = dv.astype(BF16)
            db_ref[:, 3 * B_WIDTH + 128 * pair:3 * B_WIDTH + 128 * pair + 128] = w["dz"].astype(BF16)

    rev = lambda c: pl.BlockSpec((CHUNK, B_WIDTH), lambda i, c=c: (n - 1 - i, c // B_WIDTH))
    full = lambda a: pl.BlockSpec(a.shape, lambda i, n_=a.ndim: (0,) * n_)
    acc = lambda s: pl.BlockSpec(s, lambda i, n_=len(s): (0,) * n_)
    return _Part(body, (proj, proj, proj, proj, dy, o_saved, s0, lb, onorm),
                 [rev(COL_BQ), rev(COL_BF), rev(COL_BI), rev(COL_BZ),
                  pl.BlockSpec((CHUNK, B_WIDTH), lambda i: (n - 1 - i, 1)),
                  pl.BlockSpec((CHUNK, B_WIDTH), lambda i: (n - 1 - i, 0)),
                  pl.BlockSpec((1, 2, 128, 128), lambda i: (n - 1 - i, 0, 0, 0)), full(lb), full(onorm)],
                 [pl.BlockSpec((CHUNK, 4 * B_WIDTH), lambda i: (n - 1 - i, 0)), acc((1, B_WIDTH)), acc((1, 128))],
                 [SDS((T, 4 * B_WIDTH), BF16), SDS((1, B_WIDTH), F32), SDS((1, 128), F32)],
                 [pltpu.VMEM((2, 128, 128), F32), pltpu.VMEM((2, nsub, 128, 128), F32)])


def _lb_fwd(hgrn_lb):
    assert hgrn_lb.shape[0] == 2

    def body(x_ref, o_ref):
        x0, x1 = x_ref[0:1, :], x_ref[1:2, :]
        m = jnp.maximum(x0, x1)
        e0, e1 = jnp.exp(x0 - m), jnp.exp(x1 - m)
        p0, p1 = e0 / (e0 + e1), e1 / (e0 + e1)
        o_ref[0:1, :] = jnp.clip(p0 - p0, 0.0, 1.0 - 1e-6)
        o_ref[1:2, :] = jnp.clip((p0 + p1) - p0, 0.0, 1.0 - 1e-6)

    return pl.pallas_call(body, name="lb_fwd", out_shape=SDS(hgrn_lb.shape, F32))(hgrn_lb)


def _lb_bwd(hgrn_lb, dlb):
    def body(x_ref, d_ref, o_ref):
        x0, x1 = x_ref[0:1, :], x_ref[1:2, :]
        m = jnp.maximum(x0, x1)
        e0, e1 = jnp.exp(x0 - m), jnp.exp(x1 - m)
        p0, p1 = e0 / (e0 + e1), e1 / (e0 + e1)
        val = (p0 + p1) - p0
        dp1 = jnp.where((val > 0.0) & (val < 1.0 - 1e-6), d_ref[1:2, :], 0.0)
        inner = p1 * dp1
        o_ref[0:1, :] = p0 * (0.0 - inner)
        o_ref[1:2, :] = p1 * (dp1 - inner)

    return pl.pallas_call(body, name="lb_bwd", out_shape=SDS(hgrn_lb.shape, F32))(hgrn_lb, dlb)


def _fox_prep(proj, bf):
    T = proj.shape[0]
    n = T // CHUNK

    def body(q0_ref, q1_ref, k0_ref, k1_ref, v0_ref, v1_ref, fl_ref, bf_ref, qo_ref, ko_ref, vt_ref, carry_ref):
        for p, v_ref in enumerate((v0_ref, v0_ref, v1_ref, v1_ref)):
            vt_ref[p, 0] = v_ref[:, 128 * (p % 2):128 * (p % 2) + 128].T.astype(BF16)

        @pl.when(pl.program_id(0) == 0)
        def _():
            carry_ref[...] = jnp.zeros_like(carry_ref)

        ltri = jnp.where(_lane((CHUNK, CHUNK)) <= _row((CHUNK, CHUNK)), 1.0, 0.0).astype(BF16)
        lf = jax.nn.log_sigmoid(fl_ref[...] + bf_ref[...])
        c = _dot3_left(ltri, lf) + carry_ref[...]
        carry_ref[...] = c[CHUNK - 1:CHUNK, :]
        lane = _lane((CHUNK, 128))
        feat = lane < 64
        ones_q = (lane >= 67) & (lane <= 69)
        ones_k = (lane >= 64) & (lane <= 66)
        qrefs, krefs = (q0_ref, q1_ref), (k0_ref, k1_ref)
        for h in range(C_HEADS):
            blk = slice(128 * ((h // 2) % 2), 128 * ((h // 2) % 2) + 128)
            qp, kp = qrefs[h // 4][:, blk], krefs[h // 4][:, blk]
            if h % 2:
                qp, kp = pltpu.roll(qp, 64, axis=1), pltpu.roll(kp, 64, axis=1)
            ch = jnp.broadcast_to(c[:, h:h + 1], (CHUNK, 128))
            hi = ch.astype(BF16).astype(F32)
            r1 = ch - hi
            mid = r1.astype(BF16).astype(F32)
            lo = r1 - mid
            aq = jnp.where(lane == 64, hi, jnp.where(lane == 65, mid, jnp.where(lane == 66, lo,
                           jnp.where(ones_q, 1.0, 0.0))))
            ak = jnp.where(lane == 67, -hi, jnp.where(lane == 68, -mid, jnp.where(lane == 69, -lo,
                           jnp.where(ones_k, 1.0, 0.0))))
            qo_ref[:, 128 * h:128 * h + 128] = jnp.where(feat, qp * Q_SCALE, aq).astype(BF16)
            ko_ref[:, 128 * h:128 * h + 128] = jnp.where(feat, kp, ak).astype(BF16)

    w = 256
    col = lambda c: pl.BlockSpec((CHUNK, w), lambda i, c=c: (i, c // w))
    return _Part(body, (proj, proj, proj, proj, proj, proj, proj, bf),
                 [col(COL_CQ), col(COL_CQ + w), col(COL_CK), col(COL_CK + w), col(COL_CV), col(COL_CV + w),
                  pl.BlockSpec((CHUNK, 128), lambda i: (i, COL_CF // 128)), pl.BlockSpec((1, 128), lambda i: (0, 0))],
                 [pl.BlockSpec((CHUNK, C_HEADS * 128), lambda i: (i, 0))] * 2
                 + [pl.BlockSpec((C_HEADS // 2, 1, 128, CHUNK), lambda i: (0, i, 0, 0))],
                 [SDS((T, C_HEADS * 128), BF16)] * 2 + [SDS((C_HEADS // 2, n, 128, CHUNK), BF16)],
                 [pltpu.VMEM((1, 128), F32)])


FOX_TILE = 512
FOX_KEYS = 512
FOX_STRIP = 16


def _fox_mask(tk, tq, k0, q0):
    return (_row((tk, tq)) + (k0 - q0)) <= _lane((tk, tq))


def _ride_refs(ride, rest, n_out, n_scratch):
    n = ride.n if ride else 0
    srcs, rest = rest[:n], rest[n:]
    outs, rest = rest[:n_out], rest[n_out:]
    dsts, rest = rest[:n], rest[n:]
    return srcs, outs, dsts, rest[:n_scratch], rest[n_scratch:]


def _ride_start(ride, grid, srcs, dsts, sems):
    if ride:
        first = functools.reduce(lambda a, b: a & b, [pl.program_id(d) == 0 for d in range(len(grid))])
        pl.when(first)(lambda: ride.start(srcs, dsts, sems))


def _ride_wait(ride, grid, srcs, dsts, sems):
    if ride:
        last = functools.reduce(lambda a, b: a & b, [pl.program_id(d) == n - 1 for d, n in enumerate(grid)])
        pl.when(last)(lambda: ride.wait(srcs, dsts, sems))


def _fox_fwd(qt, kt, vt, proj, tag, ride=None):
    T = proj.shape[0]
    tq, tk = _tile(T, FOX_TILE), _tile(T, FOX_KEYS)
    nq, nsub = T // tq, tk // CHUNK
    npair = C_HEADS // 2

    def body(q_ref, k_ref, vt_ref, z_ref, *rest):
        ride_srcs, (o_ref, lse_ref, y_ref), ride_dsts, (acc_ref, st_ref, pt_ref), ride_sems = _ride_refs(ride, rest, 3, 3)
        i = pl.program_id(1)
        _ride_start(ride, (npair, nq), ride_srcs, ride_dsts, ride_sems)

        qs = (q_ref[:, 0:128], q_ref[:, 128:256])
        acc_ref[...] = jnp.zeros_like(acc_ref)
        pt_ref[...] = jnp.zeros_like(pt_ref)
        nfull = (i * tq) // tk

        def scores(j):
            kb = k_ref[pl.ds(pl.multiple_of(j * tk, tk), tk), :]
            return tuple(_dot_nt(kb[:, 128 * h:128 * h + 128], qs[h]) for h in range(2))

        def weigh(j, h):
            rows = slice(64 * h, 64 * h + 64)
            vth = jnp.concatenate([vt_ref[0, nsub * j + c, rows, :] for c in range(nsub)], axis=1)
            return _dot(vth, pt_ref[h])

        def block(j, carry, diagonal):
            nxt = () if diagonal else scores(j + 1)
            pvs = [weigh(jnp.maximum(j - 1, 0), h) for h in range(2)]
            new = []
            for h in range(2):
                m, l, alpha_prev = carry[3 * h:3 * h + 3]
                st = st_ref[h]
                if diagonal:
                    st = jnp.where(_fox_mask(tk, tq, j * tk, i * tq), st, -jnp.inf)
                m_new = jnp.maximum(m, _colreduce(st, jnp.maximum))
                pt = jnp.exp(st - m_new)
                alpha = jnp.exp(m - m_new)
                rows = slice(64 * h, 64 * h + 64)
                acc_ref[rows, :] = alpha_prev * acc_ref[rows, :] + pvs[h]
                pt_ref[h] = pt.astype(BF16)
                new += [m_new, alpha * l + _colreduce(pt, jnp.add), alpha]
            for h, st in enumerate(nxt):
                st_ref[h] = st
            return tuple(new)

        for h, st in enumerate(scores(0)):
            st_ref[h] = st
        init = (jnp.full((1, tq), -jnp.inf, F32), jnp.zeros((1, tq), F32), jnp.ones((1, tq), F32)) * 2
        carry = lax.fori_loop(0, nfull, lambda j, c: block(j, c, False), init)
        m0, l0, a0, m1, l1, a1 = block(nfull, carry, True)
        for h, alpha in enumerate((a0, a1)):
            rows = slice(64 * h, 64 * h + 64)
            acc_ref[rows, :] = alpha * acc_ref[rows, :] + weigh(nfull, h)
        inv = jnp.where(_row((128, tq)) < 64, 1.0 / l0, 1.0 / l1)
        o = (acc_ref[...] * inv).T
        o_ref[...] = o
        r8 = _row((8, tq))
        lse_ref[0, 0] = jnp.where(r8 == 0, m0 + jnp.log(l0), jnp.where(r8 == 1, m1 + jnp.log(l1), 0.0))
        sz, _ = _silu_and_grad(z_ref[...])
        y_ref[...] = (o * sz).astype(BF16)
        _ride_wait(ride, (npair, nq), ride_srcs, ride_dsts, ride_sems)

    blk = pl.BlockSpec((tq, 128), lambda p, i: (i, p))
    extra = ride or _ChipExchange("gather", ())
    return pl.pallas_call(
        body, name=f"fox_fwd_{tag}", grid=(npair, nq),
        in_specs=[pl.BlockSpec((tq, 256), lambda p, i: (i, p)), pl.BlockSpec((T, 256), lambda p, i: (0, p)),
                  pl.BlockSpec((1, T // CHUNK, 128, CHUNK), lambda p, i: (p, 0, 0, 0)),
                  pl.BlockSpec((tq, 128), lambda p, i: (i, COL_CZ // 128 + p))] + extra.in_specs,
        out_specs=[blk, pl.BlockSpec((1, 1, 8, tq), lambda p, i: (p, i, 0, 0)), blk] + extra.out_specs,
        out_shape=[SDS((T, C_WIDTH), F32), SDS((npair, nq, 8, tq), F32), SDS((T, C_WIDTH), BF16)] + extra.out_shape,
        scratch_shapes=[pltpu.VMEM((128, tq), F32), pltpu.VMEM((2, tk, tq), F32), pltpu.VMEM((2, tk, tq), BF16)]
        + (extra.scratch if ride else []),
        compiler_params=pltpu.CompilerParams(dimension_semantics=("arbitrary", "arbitrary"), vmem_limit_bytes=VMEM_LIMIT,
                                             has_side_effects=bool(ride)),
    )(qt, kt, vt, proj, *extra.sources)


def _fox_bwd_prep(proj, dy, o, qt, tag):
    T = proj.shape[0]
    tq = _tile(T, FOX_TILE)
    nq = T // tq

    def body(z0_ref, z1_ref, dy_ref, o_ref, q_ref, do_ref, dl_ref, dz_ref, dot_ref, qt_ref):
        sel = jnp.where((_lane((16, 128)) >> 6) == _row((16, 128)), 1.0, 0.0).astype(BF16)
        for p, z_ref in enumerate((z0_ref, z0_ref, z1_ref, z1_ref)):
            sl = slice(128 * p, 128 * p + 128)
            sz, dsz = _silu_and_grad(z_ref[:, 128 * (p % 2):128 * (p % 2) + 128])
            dyv, ov = dy_ref[:, sl], o_ref[:, sl]
            do = dyv * sz
            do_ref[:, sl] = do.astype(BF16)
            dot_ref[p, 0] = do.T.astype(BF16)
            dz_ref[:, sl] = (dyv * ov * dsz).astype(BF16)
            hi, mid, lo = _split3(do * ov)
            dl_ref[p, 0] = (_dot_nt(sel, hi) + _dot_nt(sel, mid) + _dot_nt(sel, lo))[0:8, :]
        for h in range(C_HEADS):
            qt_ref[h, 0] = q_ref[:, 128 * h:128 * h + 128].astype(F32).T.astype(BF16)

    w = 256
    blk = pl.BlockSpec((tq, C_WIDTH), lambda i: (i, 0))
    return pl.pallas_call(
        body, name=f"fox_bwd_prep_{tag}", grid=(nq,),
        in_specs=[pl.BlockSpec((tq, w), lambda i: (i, COL_CZ // w)), pl.BlockSpec((tq, w), lambda i: (i, COL_CZ // w + 1)),
                  pl.BlockSpec((tq, C_WIDTH), lambda i: (i, (A_WIDTH + B_WIDTH) // C_WIDTH)), blk,
                  pl.BlockSpec((tq, C_HEADS * 128), lambda i: (i, 0))],
        out_specs=[blk, pl.BlockSpec((C_HEADS // 2, 1, 8, tq), lambda i: (0, i, 0, 0)), blk,
                   pl.BlockSpec((C_HEADS // 2, 1, 128, tq), lambda i: (0, i, 0, 0)),
                   pl.BlockSpec((C_HEADS, 1, 128, tq), lambda i: (0, i, 0, 0))],
        out_shape=[SDS((T, C_WIDTH), BF16), SDS((C_HEADS // 2, nq, 8, tq), F32), SDS((T, C_WIDTH), BF16),
                   SDS((C_HEADS // 2, nq, 128, tq), BF16), SDS((C_HEADS, nq, 128, tq), BF16)],
        compiler_params=_params("parallel"),
    )(proj, proj, dy, o, qt)


def _fox_bwd(qt, kt, proj, do, lse, delta, dot, qtr, tag, ride=None):
    T = proj.shape[0]
    tq, tk = _tile(T, FOX_TILE), _tile(T, FOX_KEYS)
    nq, nk = T // tq, T // tk
    assert tq == tk
    npair = C_HEADS // 2

    def body(q_ref, k_ref, v_ref, do_ref, lse_ref, dl_ref, dot_ref, qtr_ref, *rest):
        ride_srcs, (dq_ref, dk_ref, dv_ref), ride_dsts, scratch, ride_sems = _ride_refs(ride, rest, 3, 4)
        dvt_ref, dkt_ref, pt_ref, ds_ref = scratch
        j = pl.program_id(1)
        first = (j * tk) // tq
        _ride_start(ride, (npair, nk), ride_srcs, ride_dsts, ride_sems)

        @pl.when(j == 0)
        def _():
            dq_ref[...] = jnp.zeros_like(dq_ref)

        dkt_ref[...] = jnp.zeros_like(dkt_ref)
        dvt_ref[...] = jnp.zeros_like(dvt_ref)
        ks = (k_ref[:, 0:128], k_ref[:, 128:256])
        kts = tuple(k.astype(F32).T.astype(BF16) for k in ks)
        vb = v_ref[...].astype(BF16)
        lo = _lane((tq, 128)) < 64

        def operands(i):
            q0 = pl.multiple_of(i * tq, tq)
            qb = q_ref[pl.ds(q0, tq), :]
            dob = do_ref[pl.ds(q0, tq), :]
            qhs = (qb[:, 0:128], qb[:, 128:256])
            dohs = (jnp.where(lo, dob, jnp.zeros_like(dob)), jnp.where(lo, jnp.zeros_like(dob), dob))
            return qhs, dohs

        def scores(i):
            qhs, dohs = operands(i)
            return tuple((_dot_nt(ks[h], qhs[h]), _dot_nt(vb, dohs[h])) for h in range(2))

        def grads(i, slot):
            for h in range(2):
                rows = slice(64 * h, 64 * h + 64)
                dvt_ref[rows, :] += _dot_nt(dot_ref[0, i, rows, :], pt_ref[slot, h])
                dkt_ref[h] += _dot_nt(qtr_ref[h, i], ds_ref[slot, h])
                dq_ref[h, i] += _dot(kts[h], ds_ref[slot, h])

        def block(i, slot, diagonal, opening):
            sc = scores(i)
            if not opening:
                grads(i - 1, 1 - slot)
            lsev = lse_ref[0, i]
            dlv = dl_ref[0, i]
            for h in range(2):
                lseh = jnp.broadcast_to(lsev[h:h + 1, :], (FOX_STRIP, tq))
                dlh = jnp.broadcast_to(dlv[h:h + 1, :], (FOX_STRIP, tq))
                for r in range(0, tk, FOX_STRIP):
                    rows = slice(r, r + FOX_STRIP)
                    pt = jnp.exp(sc[h][0][rows, :] - lseh)
                    if diagonal:
                        pt = jnp.where(_fox_mask(FOX_STRIP, tq, r, 0), pt, 0.0)
                    ds_ref[slot, h, rows, :] = (pt * (sc[h][1][rows, :] - dlh)).astype(BF16)
                    pt_ref[slot, h, rows, :] = pt.astype(BF16)

        block(first, 0, True, True)
        rest = nq - 1 - first

        def two_steps(t, carry):
            block(first + 1 + 2 * t, 1, False, False)
            block(first + 2 + 2 * t, 0, False, False)
            return carry

        lax.fori_loop(0, rest // 2, two_steps, 0)
        pl.when(rest % 2 == 1)(lambda: block(nq - 1, 1, False, False))
        grads(nq - 1, rest % 2)
        dv_ref[...] = dvt_ref[...].T.astype(BF16)
        for h in range(2):
            dk_ref[:, 128 * h:128 * h + 128] = dkt_ref[h].T
        _ride_wait(ride, (npair, nk), ride_srcs, ride_dsts, ride_sems)

    full = lambda w: pl.BlockSpec((T, w), lambda p, j: (0, p))
    stat = pl.BlockSpec((1, nq, 8, tq), lambda p, j: (p, 0, 0, 0))
    extra = ride or _ChipExchange("gather", ())
    return pl.pallas_call(
        body, name=f"fox_bwd_{tag}", grid=(npair, nk),
        in_specs=[full(256), pl.BlockSpec((tk, 256), lambda p, j: (j, p)),
                  pl.BlockSpec((tk, 128), lambda p, j: (j, COL_CV // 128 + p)), full(128), stat, stat,
                  pl.BlockSpec((1, nq, 128, tq), lambda p, j: (p, 0, 0, 0)),
                  pl.BlockSpec((2, nq, 128, tq), lambda p, j: (p, 0, 0, 0))] + extra.in_specs,
        out_specs=[pl.BlockSpec((2, nq, 128, tq), lambda p, j: (p, 0, 0, 0)), pl.BlockSpec((tk, 256), lambda p, j: (j, p)),
                   pl.BlockSpec((tk, 128), lambda p, j: (j, p))] + extra.out_specs,
        out_shape=[SDS((C_HEADS, nq, 128, tq), F32), SDS((T, C_HEADS * 128), F32), SDS((T, C_WIDTH), BF16)]
        + extra.out_shape,
        scratch_shapes=[pltpu.VMEM((128, tk), F32), pltpu.VMEM((2, 128, tk), F32),
                        pltpu.VMEM((2, 2, tk, tq), BF16), pltpu.VMEM((2, 2, tk, tq), BF16)]
        + (extra.scratch if ride else []),
        compiler_params=pltpu.CompilerParams(dimension_semantics=("arbitrary", "arbitrary"), vmem_limit_bytes=VMEM_LIMIT,
                                             has_side_effects=bool(ride)),
    )(qt, kt, proj, do, lse, delta, dot, qtr, *extra.sources)


def _fox_bwd_post(dqt, dkt, proj, bf, tag):
    T = proj.shape[0]
    tq = _tile(T, FOX_TILE)
    n = T // tq

    def body(dq_ref, dk_ref, fl_ref, bf_ref, oq_ref, ok_ref, ofl_ref, dbf_ref, carry_ref):
        @pl.when(pl.program_id(0) == 0)
        def _():
            carry_ref[...] = jnp.zeros_like(carry_ref)
            dbf_ref[...] = jnp.zeros_like(dbf_ref)

        lane = _lane((tq, 128))
        lo = lane < 64
        dqs = [dq_ref[h, 0].T for h in range(C_HEADS)]
        dc = jnp.zeros((tq, 128), F32)
        for h in range(C_HEADS):
            dc = dc + jnp.where(lane == h, dqs[h][:, 64:65] - dk_ref[:, 128 * h + 67:128 * h + 68], 0.0)
        utri = jnp.where(_lane((tq, tq)) >= _row((tq, tq)), 1.0, 0.0).astype(BF16)
        dlf = _dot3_left(utri, dc) + carry_ref[...]
        carry_ref[...] = dlf[0:1, :]
        dfl = jnp.where(lane < C_HEADS, dlf * _sigmoid(-(fl_ref[...] + bf_ref[...])), 0.0)
        ofl_ref[...] = dfl.astype(BF16)
        dbf_ref[...] += jnp.sum(dfl, axis=0, keepdims=True)
        for p in range(C_HEADS // 2):
            a, b = 128 * (2 * p), 128 * (2 * p + 1)
            oq_ref[:, 128 * p:128 * p + 128] = (
                jnp.where(lo, dqs[2 * p], pltpu.roll(dqs[2 * p + 1], 64, axis=1)) * Q_SCALE).astype(BF16)
            ok_ref[:, 128 * p:128 * p + 128] = jnp.where(
                lo, dk_ref[:, a:a + 128], pltpu.roll(dk_ref[:, b:b + 128], 64, axis=1)).astype(BF16)

    rev = lambda w: pl.BlockSpec((tq, w), lambda i: (n - 1 - i, 0))
    return pl.pallas_call(
        body, name=f"fox_bwd_post_{tag}", grid=(n,),
        in_specs=[pl.BlockSpec((C_HEADS, 1, 128, tq), lambda i: (0, n - 1 - i, 0, 0)), rev(C_HEADS * 128),
                  pl.BlockSpec((tq, 128), lambda i: (n - 1 - i, COL_CF // 128)), pl.BlockSpec((1, 128), lambda i: (0, 0))],
        out_specs=[rev(C_WIDTH), rev(C_WIDTH), rev(128), pl.BlockSpec((1, 128), lambda i: (0, 0))],
        out_shape=[SDS((T, C_WIDTH), BF16), SDS((T, C_WIDTH), BF16), SDS((T, 128), BF16), SDS((1, 128), F32)],
        scratch_shapes=[pltpu.VMEM((1, 128), F32)], compiler_params=_params("arbitrary"),
    )(dqt, dkt, proj, bf)


def _adamw_math(w, g, m, v):
    m = ADAM_B1 * m + (1.0 - ADAM_B1) * g
    v = ADAM_B2 * v + (1.0 - ADAM_B2) * (g * g)
    delta = -ADAM_LR * ((m / ADAM_C1) / (jnp.sqrt(v / ADAM_C2) + ADAM_EPS) + ADAM_WD * w)
    return delta, m, v


def _adamw_pair(w, m, v, ga, gb, name):
    n0 = w.shape[0]
    most = max(1, ADAMW_BLOCK_BYTES // (4 * math.prod(w.shape[1:])))
    t0 = max(t for t in range(1, min(n0, most) + 1) if n0 % t == 0)

    def body(w_ref, m_ref, v_ref, ga_ref, gb_ref, g_ref, d_ref, nm_ref, nv_ref):
        g = ga_ref[...] + gb_ref[...]
        g_ref[...] = g
        d_ref[...], nm_ref[...], nv_ref[...] = _adamw_math(w_ref[...], g, m_ref[...], v_ref[...])

    blk = pl.BlockSpec((t0,) + w.shape[1:], lambda i: (i, 0, 0))
    return pl.pallas_call(
        body, name=name, grid=(n0 // t0,), in_specs=[blk] * 5, out_specs=[blk] * 4,
        out_shape=[SDS(w.shape, F32)] * 4, compiler_params=_params("parallel"),
    )(w, m, v, ga, gb)


def _small_layout():
    L = DEPTH
    lanes = lambda j: slice(128 * j, 128 * j + 128)
    wide = lambda n: [((slice(l, l + 1), lanes(j)), n * l + j, 1, 0, 128) for l in range(L) for j in range(n)]
    halves = [((l, slice(g, g + 1)), 2 * l + g // 2, 1, 64 * (g % 2), 64) for l in range(L) for g in range(A_GROUPS)]
    side_by_side = lambda w: [((slice(l, l + 1),), 0, 1, w * l, w) for l in range(L)]
    return [
        ((L, D_MODEL), wide(D_MODEL // 128)), ((L, A_GROUPS, 64), halves), ((L, A_GROUPS, 64), halves),
        ((L * A_GROUPS * CHUNK, CHUNK), [((slice(None),), 0, L * A_GROUPS * CHUNK, 0, CHUNK)]),
        ((L, A_GROUPS, CHUNK), [((l,), A_GROUPS * l, A_GROUPS, 0, CHUNK) for l in range(L)]),
        ((L, B_WIDTH), wide(B_WIDTH // 128)), ((L, 64), side_by_side(64)), ((L, C_HEADS), side_by_side(C_HEADS)),
        ((1, D_MODEL), [((slice(None), lanes(j)), j, 1, 0, 128) for j in range(D_MODEL // 128)]),
    ]


def _adamw_small(ws, ms, vs, gearly, glate, late):
    offs = _small_offsets()
    layout = _small_layout()
    n = len(ws)
    assert [w.shape for w in ws] == [shape for shape, _ in layout] and 0 in late

    def body(*refs):
        w_refs, m_refs, v_refs = refs[:n], refs[n:2 * n], refs[2 * n:3 * n]
        early_ref, late_ref = refs[3 * n:3 * n + 2]
        outs = refs[3 * n + 2:]

        def total(k):
            rows = offs[k + 1] - offs[k] if k < n else 1

            def block(dev):
                parts = [early_ref[dev, offs[k] - offs[1]:offs[k] - offs[1] + rows, :]] if k else []
                if k in late:
                    parts.append(late_ref[dev, late[k]:late[k] + rows, :])
                return functools.reduce(jnp.add, parts)

            return functools.reduce(jnp.add, [block(dev) for dev in range(N_DEV)])

        for k, (_, pieces) in enumerate(layout):
            g = total(k)
            go_ref, d_ref, nm_ref, nv_ref = outs[4 * k:4 * k + 4]
            for idx, row, rows, lane, width in pieces:
                gp = g[row:row + rows, :]
                if lane:
                    gp = pltpu.roll(gp, 128 - lane, axis=1)
                gp = gp[:, :width]
                go_ref[idx] = gp
                d_ref[idx], nm_ref[idx], nv_ref[idx] = _adamw_math(w_refs[k][idx], gp, m_refs[k][idx], v_refs[k][idx])
        outs[4 * n][...] = total(n)

    shapes = [SDS(w.shape, F32) for w in ws for _ in range(4)] + [SDS((1, 128), F32)]
    res = pl.pallas_call(body, name="adamw_small", out_shape=shapes,
                         compiler_params=pltpu.CompilerParams(vmem_limit_bytes=VMEM_LIMIT))(*ws, *ms, *vs, gearly, glate)
    return [res[4 * k:4 * k + 4] for k in range(n)], res[4 * n]


def _pack_grads(dlng, dlnb, dwm, dbst, dlb, donorm, dbf, dfinal, loss_part):
    offs = _small_offsets()
    base = offs[1]
    L = len(dwm)
    assert L == 2

    def body(*refs):
        lng, lnb, wm, bst, on, bf = (refs[L * a:L * a + L] for a in range(6))
        lb_ref, fin_ref, loss_ref, o_ref = refs[6 * L:]
        o_ref[...] = jnp.zeros_like(o_ref)
        lane = _lane((1, 128))
        for l in range(L):
            for j in range(2):
                o_ref[offs[1] - base + 2 * l + j:offs[1] - base + 2 * l + j + 1, :] = lng[l][:, 128 * j:128 * j + 128]
                o_ref[offs[2] - base + 2 * l + j:offs[2] - base + 2 * l + j + 1, :] = lnb[l][:, 128 * j:128 * j + 128]
                o_ref[offs[5] - base + 2 * l + j:offs[5] - base + 2 * l + j + 1, :] = lb_ref[l:l + 1, 128 * j:128 * j + 128]
            for g in range(A_GROUPS):
                row = offs[3] - base + (A_GROUPS * l + g) * CHUNK
                o_ref[row:row + CHUNK, :] = wm[l][g]
            o_ref[offs[4] - base + A_GROUPS * l:offs[4] - base + A_GROUPS * (l + 1), :] = bst[l][...].T[0:A_GROUPS, :]
        o_ref[offs[6] - base:offs[6] - base + 1, :] = jnp.where(lane < 64, on[0][...], pltpu.roll(on[1][...], 64, axis=1))
        o_ref[offs[7] - base:offs[7] - base + 1, :] = jnp.where(
            lane < C_HEADS, bf[0][...], jnp.where(lane < 2 * C_HEADS, pltpu.roll(bf[1][...], C_HEADS, axis=1), 0.0))
        for j in range(D_MODEL // 128):
            o_ref[offs[8] - base + j:offs[8] - base + j + 1, :] = fin_ref[:, 128 * j:128 * j + 128]
        o_ref[offs[9] - base:offs[9] - base + 1, :] = loss_ref[...]

    rows = offs[9] + 8 - base
    return pl.pallas_call(body, name="pack_grads", out_shape=SDS((rows, 128), F32))(
        *dlng, *dlnb, *dwm, *dbst, *donorm, *dbf, dlb, dfinal, loss_part)


def _sum_chips(layers, name, layer_major):
    groups = [list(layer) if isinstance(layer, (list, tuple)) else [layer] for layer in layers]
    R = groups[0][0].shape[1]
    C = sum(a.shape[2] for a in groups[0])
    L = len(groups)
    tc = _tile(C, 256)
    steps = C // tc
    plan = []
    for l, layer in enumerate(groups):
        assert all(a.shape[2] % tc == 0 for a in layer) and sum(a.shape[2] for a in layer) == C
        firsts = [sum(a.shape[2] for a in layer[:k]) // tc for k in range(len(layer))]
        plan += [(l, first, a.shape[2] // tc) for first, a in zip(firsts, layer)]

    def body(*refs):
        o_ref = refs[-1]
        i = pl.program_id(0)
        for (l, first, n), p_ref in zip(plan, refs[:-1]):
            def write(l=l, p_ref=p_ref):
                p = [p_ref[k].astype(F32) for k in range(N_CHIPS)]
                s = ((p[0] + p[1]) + p[2]) + p[3]
                if layer_major:
                    o_ref[l] = s
                else:
                    o_ref[:, l, :] = s

            if n == steps:
                write()
            else:
                pl.when((i >= first) & (i < first + n))(write)

    out = (L, R, C) if layer_major else (R, L, C)
    out_blk = (L, R, tc) if layer_major else (R, L, tc)
    return pl.pallas_call(
        body, name=name, grid=(steps,),
        in_specs=[pl.BlockSpec((N_CHIPS, R, tc), lambda i, first=first, n=n: (0, 0, jnp.clip(i - first, 0, n - 1)))
                  for _, first, n in plan],
        out_specs=pl.BlockSpec(out_blk, lambda i: (0, 0, i)), out_shape=SDS(out, F32),
        compiler_params=_params("parallel"),
    )(*[a for layer in groups for a in layer])


ANY = pl.BlockSpec(memory_space=pl.ANY)


def _mesh_pos():
    return lax.axis_index("x"), lax.axis_index("y"), lax.axis_index("c")


def _other_chips(x, y):
    return [(1 - x, y), (x, 1 - y), (1 - x, 1 - y)]


class _ChipExchange:
    def __init__(self, mode, sources):
        assert mode in ("gather", "scatter")
        self.mode, self.sources = mode, tuple(sources)
        self.n = len(self.sources)
        self.in_specs = [ANY] * self.n
        self.out_specs = [ANY] * self.n
        self.out_shape = [SDS(((N_CHIPS,) + s.shape) if mode == "gather" else s.shape, s.dtype) for s in self.sources]
        self.scratch = [pltpu.SemaphoreType.DMA((3 * self.n,)), pltpu.SemaphoreType.DMA((3 * self.n,)),
                        pltpu.SemaphoreType.DMA((self.n,))]

    def _copies(self, srcs, dsts, send_sems, recv_sems, local_sems):
        x, y, c = _mesh_pos()
        me = 2 * x + y
        view = (lambda r, chip: r) if self.mode == "gather" else (lambda r, chip: r.at[chip])
        local = [pltpu.make_async_copy(view(s, me), d.at[me], local_sems.at[a]) for a, (s, d) in enumerate(zip(srcs, dsts))]
        sends, recvs = [], []
        for j, (px, py) in enumerate(_other_chips(x, y)):
            peer = 2 * px + py
            for a, (s, d) in enumerate(zip(srcs, dsts)):
                sems = dict(send_sem=send_sems.at[self.n * j + a], recv_sem=recv_sems.at[self.n * j + a],
                            device_id=(px, py, c), device_id_type=MESH_ID)
                sends.append(pltpu.make_async_remote_copy(src_ref=view(s, peer), dst_ref=d.at[me], **sems))
                recvs.append(pltpu.make_async_remote_copy(src_ref=view(s, me), dst_ref=d.at[peer], **sems))
        return local, sends, recvs

    def start(self, srcs, dsts, sems):
        local, sends, _ = self._copies(srcs, dsts, *sems)
        for cp in local + sends:
            cp.start()

    def wait(self, srcs, dsts, sems):
        local, sends, recvs = self._copies(srcs, dsts, *sems)
        for cp in recvs:
            cp.wait_recv()
        for cp in sends:
            cp.wait_send()
        for cp in local:
            cp.wait()


def _gather_halves(w, tag):
    R, C = w.shape
    H = C // 2

    def body(w_ref, g_ref, send_sems, recv_sems, pass_send, pass_recv, local_sem):
        x, y, c = _mesh_pos()
        me = 2 * x + y
        mine, theirs = pl.ds(pl.multiple_of(c * H, H), H), pl.ds(pl.multiple_of((1 - c) * H, H), H)
        own = pltpu.make_async_copy(w_ref, g_ref.at[me], local_sem)
        own.start()

        def fetch(j, px, py, src, dst):
            return pltpu.make_async_remote_copy(src_ref=src, dst_ref=dst, send_sem=send_sems.at[j], recv_sem=recv_sems.at[j],
                                                device_id=(px, py, c), device_id_type=MESH_ID)

        def hand(j, cols, peer):
            return pltpu.make_async_remote_copy(src_ref=g_ref.at[peer, :, cols], dst_ref=g_ref.at[peer, :, cols],
                                                send_sem=pass_send.at[j], recv_sem=pass_recv.at[j],
                                                device_id=(x, y, 1 - c), device_id_type=MESH_ID)

        chips = _other_chips(x, y)
        sends = [fetch(j, px, py, w_ref.at[:, mine], g_ref.at[me, :, mine]) for j, (px, py) in enumerate(chips)]
        for cp in sends:
            cp.start()
        passed = []
        for j, (px, py) in enumerate(chips):
            peer = 2 * px + py
            fetch(j, px, py, w_ref.at[:, mine], g_ref.at[peer, :, mine]).wait_recv()
            passed.append(hand(j, mine, peer))
            passed[-1].start()
        for j, (px, py) in enumerate(chips):
            hand(j, theirs, 2 * px + py).wait_recv()
        for cp in sends + passed:
            cp.wait_send()
        own.wait()

    return pl.pallas_call(
        body, name=f"gather_halves_{tag}", in_specs=[ANY], out_specs=ANY, out_shape=SDS((N_CHIPS, R, C), w.dtype),
        scratch_shapes=[pltpu.SemaphoreType.DMA((3,)), pltpu.SemaphoreType.DMA((3,)), pltpu.SemaphoreType.DMA((3,)),
                        pltpu.SemaphoreType.DMA((3,)), pltpu.SemaphoreType.DMA],
        compiler_params=pltpu.CompilerParams(has_side_effects=True),
    )(w)


class _DeviceGather:
    def __init__(self, source):
        self.sources, self.n = (source,), 1
        self.in_specs, self.out_specs = [ANY], [ANY]
        self.out_shape = [SDS((N_DEV,) + source.shape, source.dtype)]
        self.scratch = [pltpu.SemaphoreType.DMA((N_DEV - 1,)), pltpu.SemaphoreType.DMA((N_DEV - 1,)),
                        pltpu.SemaphoreType.DMA((1,))]

    def _copies(self, srcs, dsts, send_sems, recv_sems, local_sems):
        (src,), (dst,) = srcs, dsts
        x, y, c = _mesh_pos()
        me = 4 * x + 2 * y + c
        local = [pltpu.make_async_copy(src, dst.at[me], local_sems.at[0])]
        sends, recvs = [], []
        for k in range(1, N_DEV):
            px, py, pc = (1 - x) if k & 4 else x, (1 - y) if k & 2 else y, (1 - c) if k & 1 else c
            sems = dict(send_sem=send_sems.at[k - 1], recv_sem=recv_sems.at[k - 1], device_id=(px, py, pc),
                        device_id_type=MESH_ID)
            sends.append(pltpu.make_async_remote_copy(src_ref=src, dst_ref=dst.at[me], **sems))
            recvs.append(pltpu.make_async_remote_copy(src_ref=src, dst_ref=dst.at[4 * px + 2 * py + pc], **sems))
        return local, sends, recvs

    start = _ChipExchange.start
    wait = _ChipExchange.wait


class _Rides:
    def __init__(self, *rides):
        self.rides = rides
        self.n = sum(r.n for r in rides)
        self.sources = tuple(s for r in rides for s in r.sources)
        self.in_specs, self.out_specs = [ANY] * self.n, [ANY] * self.n
        self.out_shape = [s for r in rides for s in r.out_shape]
        self.scratch = [s for r in rides for s in r.scratch]

    def _each(self, srcs, dsts, sems):
        a = b = 0
        for r in self.rides:
            yield r, srcs[a:a + r.n], dsts[a:a + r.n], sems[b:b + len(r.scratch)]
            a, b = a + r.n, b + len(r.scratch)

    def start(self, srcs, dsts, sems):
        for r, s, d, m in self._each(srcs, dsts, sems):
            r.start(s, d, m)

    def wait(self, srcs, dsts, sems):
        for r, s, d, m in self._each(srcs, dsts, sems):
            r.wait(s, d, m)


def _gather_devices(a, name):
    ex = _DeviceGather(a)

    def body(a_ref, g_ref, *sems):
        ex.start((a_ref,), (g_ref,), sems)
        ex.wait((a_ref,), (g_ref,), sems)

    return pl.pallas_call(
        body, name=name, in_specs=ex.in_specs, out_specs=ex.out_specs[0], out_shape=ex.out_shape[0],
        scratch_shapes=ex.scratch, compiler_params=pltpu.CompilerParams(has_side_effects=True),
    )(a)


def _swap_cores(pin, pout):
    def body(pin_ref, pout_ref, oin_ref, oout_ref, send_sems, recv_sems):
        x, y, c = _mesh_pos()
        cps = [pltpu.make_async_remote_copy(src_ref=src, dst_ref=dst, send_sem=send_sems.at[a], recv_sem=recv_sems.at[a],
                                            device_id=(x, y, 1 - c), device_id_type=MESH_ID)
               for a, (src, dst) in enumerate(((pin_ref, oin_ref), (pout_ref, oout_ref)))]
        for cp in cps:
            cp.start()
        for cp in cps:
            cp.wait()

    return pl.pallas_call(
        body, name="swap_cores", in_specs=[ANY, ANY], out_specs=[ANY, ANY],
        out_shape=[SDS(pin.shape, F32), SDS(pout.shape, F32)],
        scratch_shapes=[pltpu.SemaphoreType.DMA((2,)), pltpu.SemaphoreType.DMA((2,))],
        compiler_params=pltpu.CompilerParams(has_side_effects=True),
    )(pin, pout)


PACK_TILE = 8 * 128


def _pack_rows(size):
    return (size + PACK_TILE - 1) // PACK_TILE * 8


def _small_offsets():
    offs = [0]
    for _, shape in SMALL_PARAMS:
        offs.append(offs[-1] + _pack_rows(math.prod(shape)))
    return offs


def _layer_consts(l, gmlp_ln_g, gmlp_ln_b, gmlp_w_s, gmlp_b_s, hgrn_onorm_g, fox_b_f):
    causal = jnp.tril(jnp.ones((CHUNK, CHUNK), bool))
    wm = jnp.where(causal[None], gmlp_w_s[l], 0.0)
    return dict(
        lng=gmlp_ln_g[l].reshape(1, A_WIDTH), lnb=gmlp_ln_b[l].reshape(1, A_WIDTH),
        wm=wm.astype(BF16), wmt=jnp.swapaxes(wm, 1, 2).astype(BF16),
        bst=jnp.pad(gmlp_b_s[l].T, ((0, 0), (0, 128 - A_GROUPS))),
        onorm=jnp.tile(hgrn_onorm_g[l], 4).reshape(1, B_WIDTH),
        bf=jnp.pad(fox_b_f[l], (0, 128 - C_HEADS)).reshape(1, 128),
    )


def kernel(x, norm_g, w_in, w_out, gmlp_ln_g, gmlp_ln_b, gmlp_w_s, gmlp_b_s, hgrn_lb, hgrn_onorm_g, fox_b_f, final_norm_g, loss_target, m_norm_g, m_w_in, m_w_out, m_gmlp_ln_g, m_gmlp_ln_b, m_gmlp_w_s, m_gmlp_b_s, m_hgrn_lb, m_hgrn_onorm_g, m_fox_b_f, m_final_norm_g, v_norm_g, v_w_in, v_w_out, v_gmlp_ln_g, v_gmlp_ln_b, v_gmlp_w_s, v_gmlp_b_s, v_hgrn_lb, v_hgrn_onorm_g, v_fox_b_f, v_final_norm_g):
    T = x.shape[1]
    shard_in = w_in.shape[2]
    shard_out = w_out.shape[1]
    xs = x.reshape(T, D_MODEL)
    tgt = loss_target.reshape(T, D_MODEL)

    w_in_b = [w_in[l].T.astype(BF16) for l in range(DEPTH)]
    w_out_b = w_out.astype(BF16)

    lb_all = _lb_fwd(hgrn_lb)
    consts = [_layer_consts(l, gmlp_ln_g, gmlp_ln_b, gmlp_w_s, gmlp_b_s, hgrn_onorm_g, fox_b_f) for l in range(DEPTH)]

    saved = []
    xl = xs
    w_in_l = _gather_halves(w_in_b[0], "w_in_l0")
    for l in range(DEPTH):
        cs = consts[l]
        tag = f"l{l}"
        h, proj = _inproj(xl, norm_g[l].reshape(1, D_MODEL), w_in_l, D_IN_PAD, tag)
        (ya,), (yb, ob, s0), (qt, kt, vt) = _run_parts(
            [_gmlp_fwd(proj, cs["lng"], cs["lnb"], cs["wm"], cs["bst"]),
             _hgrn_fwd(proj, lb_all[l].reshape(1, B_WIDTH), cs["onorm"]), _fox_prep(proj, cs["bf"])],
            (T // CHUNK,), f"mix_fwd_{tag}")
        ride = _ChipExchange("gather", (w_out_b[l],) + ((w_in_b[l + 1],) if l + 1 < DEPTH else ()))
        oc, lse, yc, *gathered = _fox_fwd(qt, kt, vt, proj, tag, ride)
        w_out_l = gathered[0].reshape(N_CHIPS * shard_out, D_MODEL)
        saved.append(dict(x=xl, h=h, proj=proj, ya=ya, yb=yb, yc=yc, ob=ob, s0=s0, qt=qt, kt=kt, oc=oc, lse=lse,
                          w_in=w_in_l, w_out=w_out_l))
        if l + 1 < DEPTH:
            xl = _outproj(xl, ya, yb, yc, w_out_l, tag)
            w_in_l = gathered[1]
    dx, loss_part, d_final = _outproj(xl, ya, yb, yc, w_out_l, tag, head=(final_norm_g.reshape(1, D_MODEL), tgt))

    g_small = {}
    dlb_rows, rin, rout = [None] * DEPTH, [None] * DEPTH, [None] * DEPTH
    slabs_in = None
    for l in reversed(range(DEPTH)):
        cs, sv = consts[l], saved[l]
        tag = f"l{l}"
        proj = sv["proj"]
        dy, dw_out = _outproj_bwd(dx, sv["ya"], sv["yb"], sv["yc"], sv["w_out"], tag)
        (da, dwm, dbst, dlng, dlnb), (db, dlb_rows[l], donorm) = _run_parts(
            [_gmlp_bwd(proj, dy, cs["lng"], cs["lnb"], cs["wm"], cs["wmt"], cs["bst"]),
             _hgrn_bwd(proj, dy, sv["ob"], sv["s0"], lb_all[l].reshape(1, B_WIDTH), cs["onorm"])],
            (T // CHUNK,), f"mix_bwd_{tag}")
        do, delta, dzc, dot, qtr = _fox_bwd_prep(proj, dy, sv["oc"], sv["qt"], tag)
        slabs_out = dw_out.reshape(N_CHIPS, shard_out, D_MODEL).astype(BF16)
        ride = _ChipExchange("scatter", (slabs_out,) + ((slabs_in,) if slabs_in is not None else ()))
        g_small[l] = dict(ln_g=dlng, ln_b=dlnb, w_s=dwm, b_s=dbst, onorm=donorm)
        if l == 0:
            d_hgrn_lb = _lb_bwd(hgrn_lb, jnp.concatenate(dlb_rows, axis=0))
            per_layer = lambda key: [g_small[k][key] for k in range(DEPTH)]
            dbf_known = [jnp.zeros((1, 128), F32)] + [g_small[k]["bf"] for k in range(1, DEPTH)]
            early = _pack_grads(per_layer("ln_g"), per_layer("ln_b"), per_layer("w_s"), per_layer("b_s"), d_hgrn_lb,
                                per_layer("onorm"), dbf_known, d_final, loss_part)
            ride = _Rides(ride, _DeviceGather(early))
        dqt, dkt, dvc, *received = _fox_bwd(sv["qt"], sv["kt"], proj, do, sv["lse"], delta, dot, qtr, tag, ride)
        rout[l] = received[0]
        if slabs_in is not None:
            rin[l + 1] = received[1]
        if l == 0:
            rearly = received[-1]
        dqc, dkc, dflc, g_small[l]["bf"] = _fox_bwd_post(dqt, dkt, proj, cs["bf"], tag)
        dproj = [da, db, dqc, dkc, dvc, dzc, dflc]
        if l == 0:
            ride, parts = None, []
            for n, cols in enumerate(DW_IN_GROUPS):
                part, *arrived = _dw_in(sv["h"], dproj, D_IN_PAD, shard_in, cols, f"{tag}_{n}", ride)
                parts += arrived
                ride = _ChipExchange("scatter", (part,))
        else:
            slabs_in, = _dw_in(sv["h"], dproj, D_IN_PAD, shard_in, (0, D_MODEL), tag)
            ride = None
        dx, dng, *received = _dx_in(sv["x"], norm_g[l].reshape(1, D_MODEL), dx, dproj, sv["w_in"], tag, ride)
        if l == 0:
            rin[0] = parts + received
        g_small[l]["norm_g"] = dng.reshape(D_MODEL // 128, 128)
    grad_x = dx.reshape(x.shape)
    dbf0 = jnp.where(_lane((1, 128)) < C_HEADS, g_small[0]["bf"], 0.0)
    late = jnp.concatenate([g_small[l]["norm_g"] for l in range(DEPTH)] + [jnp.pad(dbf0, ((0, 7), (0, 0)))])
    rlate = _gather_devices(late, "gather_late_grads")
    late_blocks = {0: 0, 7: DEPTH * D_MODEL // 128}

    pin, pout = _sum_chips(rin, "sum_chips_w_in", False), _sum_chips(rout, "sum_chips_w_out", True)
    oin, oout = _swap_cores(pin, pout)
    to_view = lambda a: jnp.transpose(a, (2, 0, 1))
    g_w_in, d_w_in, nm_w_in, nv_w_in = [
        jnp.transpose(o, (1, 2, 0))
        for o in _adamw_pair(to_view(w_in), to_view(m_w_in), to_view(v_w_in), pin, oin, "adamw_w_in")]
    g_w_out, d_w_out, nm_w_out, nv_w_out = _adamw_pair(w_out, m_w_out, v_w_out, pout, oout, "adamw_w_out")

    small_w = [norm_g, gmlp_ln_g, gmlp_ln_b, gmlp_w_s, gmlp_b_s, hgrn_lb, hgrn_onorm_g, fox_b_f, final_norm_g]
    small_m = [m_norm_g, m_gmlp_ln_g, m_gmlp_ln_b, m_gmlp_w_s, m_gmlp_b_s, m_hgrn_lb, m_hgrn_onorm_g, m_fox_b_f, m_final_norm_g]
    small_v = [v_norm_g, v_gmlp_ln_g, v_gmlp_ln_b, v_gmlp_w_s, v_gmlp_b_s, v_hgrn_lb, v_hgrn_onorm_g, v_fox_b_f, v_final_norm_g]
    views = lambda ps: [p.reshape(shape) for p, (shape, _) in zip(ps, _small_layout())]
    per_param, loss_row = _adamw_small(views(small_w), views(small_m), views(small_v), rearly, rlate, late_blocks)
    sg, sd, sm, sv_ = [[per_param[k][a].reshape(shape) for k, (_, shape) in enumerate(SMALL_PARAMS)] for a in range(4)]
    loss = loss_row[0, 0]

    def order(big_in, big_out, small):
        return [small[0], big_in, big_out] + small[1:]

    return (loss, grad_x, *order(g_w_in, g_w_out, sg), *order(d_w_in, d_w_out, sd), *order(nm_w_in, nm_w_out, sm),
            *order(nv_w_in, nv_w_out, sv_))
```

```python
import collections
import functools
import math

import jax
import jax.numpy as jnp
from jax import lax
from jax.experimental import pallas as pl
from jax.experimental.pallas import tpu as pltpu

F32 = jnp.float32
BF16 = jnp.bfloat16
SDS = jax.ShapeDtypeStruct
MESH_ID = pl.DeviceIdType.MESH

D_MODEL = 1024
DEPTH = 2
A_WIDTH = 256
A_GROUPS = 4
B_WIDTH = 256
C_WIDTH = 512
C_HEADS = 8
D_IN = 3848
D_IN_PAD = 4096
CHUNK = 128
SUB = 16
SUB_SHIFT = 4
NORM_EPS = 1e-6
F_FLOOR = 1e-30
COL_AU, COL_AV, COL_AZ = 0, 256, 512
COL_BQ, COL_BF, COL_BI, COL_BZ = 768, 1024, 1280, 1536
COL_CQ, COL_CK, COL_CV, COL_CZ, COL_CF = 1792, 2304, 2816, 3328, 3840
HEAD_LANES = 128
Q_SCALE = 0.125
ADAM_LR, ADAM_B1, ADAM_B2, ADAM_EPS, ADAM_WD, ADAM_STEP = 0.001, 0.9, 0.999, 1e-08, 0.01, 10
ADAM_C1 = 1.0 - ADAM_B1 ** ADAM_STEP
ADAM_C2 = 1.0 - ADAM_B2 ** ADAM_STEP
VMEM_LIMIT = 56 * 1024 * 1024
ADAMW_BLOCK_BYTES = 1 << 20
N_CHIPS = 4
N_DEV = 8

SMALL_PARAMS = (
    ("norm_g", (DEPTH, D_MODEL)), ("gmlp_ln_g", (DEPTH, 4, 64)), ("gmlp_ln_b", (DEPTH, 4, 64)),
    ("gmlp_w_s", (DEPTH, 4, 128, 128)), ("gmlp_b_s", (DEPTH, 4, 128)), ("hgrn_lb", (DEPTH, 256)),
    ("hgrn_onorm_g", (DEPTH, 64)), ("fox_b_f", (DEPTH, 8)), ("final_norm_g", (D_MODEL,)),
)


def _tile(n, pref):
    t = min(n, pref)
    assert n % t == 0, (n, pref)
    return t


def _params(*sem):
    return pltpu.CompilerParams(dimension_semantics=sem, vmem_limit_bytes=VMEM_LIMIT)


_Part = collections.namedtuple("_Part", "body operands in_specs out_specs out_shape scratch")


def _run_parts(parts, grid, name):
    counts = [(len(p.operands), len(p.out_shape), len(p.scratch)) for p in parts]

    def body(*refs):
        ins, outs, scr = [], [], []
        pos = 0
        for group, k in ((ins, 0), (outs, 1), (scr, 2)):
            for c in counts:
                group.append(refs[pos:pos + c[k]])
                pos += c[k]
        for p, i, o, s in zip(parts, ins, outs, scr):
            p.body(*i, *o, *s)

    flat = lambda key: [x for p in parts for x in getattr(p, key)]
    res = pl.pallas_call(
        body, name=name, grid=grid, in_specs=flat("in_specs"), out_specs=flat("out_specs"), out_shape=flat("out_shape"),
        scratch_shapes=flat("scratch"), compiler_params=_params(*(("arbitrary",) * len(grid))),
    )(*flat("operands"))
    out, pos = [], 0
    for c in counts:
        out.append(list(res[pos:pos + c[1]]))
        pos += c[1]
    return out


def _dot(a, b):
    return jnp.dot(a, b, preferred_element_type=F32)


def _dot_nt(a, b):
    return lax.dot_general(a, b, (((1,), (1,)), ((), ())), preferred_element_type=F32)


def _dot_tn(a, b):
    return lax.dot_general(a, b, (((0,), (0,)), ((), ())), preferred_element_type=F32)


def _split3(x):
    hi = x.astype(BF16)
    r = x - hi.astype(F32)
    mid = r.astype(BF16)
    lo = (r - mid.astype(F32)).astype(BF16)
    return hi, mid, lo


def _dot3_left(c, x):
    hi, mid, lo = _split3(x)
    return _dot(c, hi) + _dot(c, mid) + _dot(c, lo)


def _sigmoid(x):
    return jax.nn.sigmoid(x)


def _silu_and_grad(x):
    s = _sigmoid(x)
    return x * s, s * (1.0 + x * (1.0 - s))


_GELU_C = math.sqrt(2.0 / math.pi)


def _gelu_and_grad(x):
    inner = _GELU_C * (x + 0.044715 * x * x * x)
    t = jnp.tanh(inner)
    y = 0.5 * x * (1.0 + t)
    dy = 0.5 * (1.0 + t) + 0.5 * x * (1.0 - t * t) * _GELU_C * (1.0 + 3.0 * 0.044715 * x * x)
    return y, dy


def _lane(shape):
    return lax.broadcasted_iota(jnp.int32, shape, 1)


def _row(shape):
    return lax.broadcasted_iota(jnp.int32, shape, 0)


def _gsum64(x):
    lo = _lane(x.shape) < 64
    s0 = jnp.sum(jnp.where(lo, x, 0.0), axis=-1, keepdims=True)
    s1 = jnp.sum(jnp.where(lo, 0.0, x), axis=-1, keepdims=True)
    return jnp.where(lo, s0, s1)


def _colreduce(x, op):
    parts = [x[r:r + 8, :] for r in range(0, x.shape[0], 8)]
    while len(parts) > 1:
        pairs = [op(parts[k], parts[k + 1]) for k in range(0, len(parts) - 1, 2)]
        parts = pairs + ([parts[-1]] if len(parts) % 2 else [])
    red = jnp.max if op is jnp.maximum else jnp.sum
    return red(parts[0], axis=0, keepdims=True)


def _block_diag64(dtype=BF16):
    r, c = _row((128, 128)), _lane((128, 128))
    return jnp.where((r >> 6) == (c >> 6), 1.0, 0.0).astype(dtype)


def _assemble_w_in(slab_ref, wt_ref):
    shard = slab_ref.shape[1]
    top = N_CHIPS * shard // 16 * 16
    wt_ref[top:, :] = jnp.zeros((wt_ref.shape[0] - top, wt_ref.shape[1]), wt_ref.dtype)
    for k in range(N_CHIPS):
        wt_ref[shard * k:shard * (k + 1), :] = slab_ref[k]


def _inproj(x, g, w, dp_width, tag):
    T, D = x.shape
    tm = _tile(T, 512)

    def body(x_ref, g_ref, w_ref, h_ref, p_ref, wt_ref):
        pl.when(pl.program_id(0) == 0)(lambda: _assemble_w_in(w_ref, wt_ref))
        xv = x_ref[...]
        r = lax.rsqrt(jnp.mean(xv * xv, axis=-1, keepdims=True) + NORM_EPS)
        h = (xv * r * g_ref[...]).astype(BF16)
        h_ref[...] = h
        p_ref[...] = _dot_nt(h, wt_ref[...])

    return pl.pallas_call(
        body, name=f"inproj_{tag}", grid=(T // tm,),
        in_specs=[pl.BlockSpec((tm, D), lambda i: (i, 0)), pl.BlockSpec((1, D), lambda i: (0, 0)),
                  pl.BlockSpec(w.shape, lambda i: (0, 0, 0))],
        out_specs=[pl.BlockSpec((tm, D), lambda i: (i, 0)), pl.BlockSpec((tm, dp_width), lambda i: (i, 0))],
        out_shape=[SDS((T, D), BF16), SDS((T, dp_width), F32)],
        scratch_shapes=[pltpu.VMEM((dp_width, D), BF16)],
        compiler_params=_params("arbitrary"),
    )(x, g, w)


def _outproj(x, ya, yb, yc, wo, tag, head=None):
    T, D = x.shape
    tm = _tile(T, 512)

    def residual(x_ref, ya_ref, yb_ref, yc_ref, wo_ref):
        acc = x_ref[...] + _dot(ya_ref[...], wo_ref[0:A_WIDTH, :])
        acc = acc + _dot(yb_ref[...], wo_ref[A_WIDTH:A_WIDTH + B_WIDTH, :])
        return acc + _dot(yc_ref[...], wo_ref[A_WIDTH + B_WIDTH:, :])

    def body(x_ref, ya_ref, yb_ref, yc_ref, wo_ref, o_ref):
        o_ref[...] = residual(x_ref, ya_ref, yb_ref, yc_ref, wo_ref)

    def body_head(x_ref, ya_ref, yb_ref, yc_ref, wo_ref, g_ref, t_ref, dx_ref, loss_ref, dg_ref):
        @pl.when(pl.program_id(0) == 0)
        def _():
            loss_ref[...] = jnp.zeros_like(loss_ref)
            dg_ref[...] = jnp.zeros_like(dg_ref)

        xv = residual(x_ref, ya_ref, yb_ref, yc_ref, wo_ref)
        r = lax.rsqrt(jnp.mean(xv * xv, axis=-1, keepdims=True) + NORM_EPS)
        xh = xv * r
        gv = g_ref[...]
        err = xh * gv - t_ref[...]
        tok = jnp.mean(err * err, axis=-1, keepdims=True)
        loss_ref[...] += 0.5 * jnp.sum(tok, axis=0, keepdims=True)
        dy = err * (1.0 / D)
        dg_ref[...] += jnp.sum(dy * xh, axis=0, keepdims=True)
        dxh = dy * gv
        dx_ref[...] = r * (dxh - xh * jnp.mean(dxh * xh, axis=-1, keepdims=True))

    row = lambda w: pl.BlockSpec((tm, w), lambda i: (i, 0))
    fixed = lambda shape: pl.BlockSpec(shape, lambda i: (0, 0))
    in_specs = [row(D), row(A_WIDTH), row(B_WIDTH), row(C_WIDTH), fixed(wo.shape)]
    if head is None:
        return pl.pallas_call(
            body, name=f"outproj_{tag}", grid=(T // tm,), in_specs=in_specs,
            out_specs=row(D), out_shape=SDS((T, D), F32), compiler_params=_params("parallel"),
        )(x, ya, yb, yc, wo)
    return pl.pallas_call(
        body_head, name=f"outproj_loss_{tag}", grid=(T // tm,), in_specs=in_specs + [fixed((1, D)), row(D)],
        out_specs=[row(D), fixed((1, 128)), fixed((1, D))],
        out_shape=[SDS((T, D), F32), SDS((1, 128), F32), SDS((1, D), F32)], compiler_params=_params("arbitrary"),
    )(x, ya, yb, yc, wo, *head)


def _outproj_bwd(dx, ya, yb, yc, wo, proj, o, qt, tag):
    T, D = dx.shape
    DM = wo.shape[0]
    AB = A_WIDTH + B_WIDTH
    tq = _tile(T, FOX_TILE)
    nq = T // tq

    def body(dx_ref, ya_ref, yb_ref, yc_ref, wo_ref, z0_ref, z1_ref, o_ref, q_ref,
             dy_ref, dwo_ref, do_ref, dl_ref, dz_ref, dot_ref, qt_ref):
        @pl.when(pl.program_id(0) == 0)
        def _():
            dwo_ref[...] = jnp.zeros_like(dwo_ref)

        dxb = dx_ref[...].astype(BF16)
        dy = _dot_nt(dxb, wo_ref[...])
        dy_ref[...] = dy[:, 0:AB]
        dwo_ref[0:A_WIDTH, :] += _dot_tn(ya_ref[...], dxb)
        dwo_ref[A_WIDTH:AB, :] += _dot_tn(yb_ref[...], dxb)
        dwo_ref[AB:, :] += _dot_tn(yc_ref[...], dxb)

        sel = jnp.where((_lane((16, 128)) >> 6) == _row((16, 128)), 1.0, 0.0).astype(BF16)
        for p, z_ref in enumerate((z0_ref, z0_ref, z1_ref, z1_ref)):
            sl = slice(128 * p, 128 * p + 128)
            sz, dsz = _silu_and_grad(z_ref[:, 128 * (p % 2):128 * (p % 2) + 128])
            dyv, ov = dy[:, AB + 128 * p:AB + 128 * p + 128], o_ref[:, sl]
            do = dyv * sz
            do_ref[:, sl] = do.astype(BF16)
            dot_ref[p, 0] = do.T.astype(BF16)
            dz_ref[:, sl] = (dyv * ov * dsz).astype(BF16)
            hi, mid, lo = _split3(do * ov)
            dl_ref[p, 0] = (_dot_nt(sel, hi) + _dot_nt(sel, mid) + _dot_nt(sel, lo))[0:8, :]
        for h in range(C_HEADS):
            qt_ref[h, 0] = q_ref[:, 128 * h:128 * h + 128].astype(F32).T.astype(BF16)

    w = 256
    row = lambda width: pl.BlockSpec((tq, width), lambda i: (i, 0))
    return pl.pallas_call(
        body, name=f"outproj_bwd_{tag}", grid=(nq,),
        in_specs=[row(D), row(A_WIDTH), row(B_WIDTH), row(C_WIDTH), pl.BlockSpec(wo.shape, lambda i: (0, 0)),
                  pl.BlockSpec((tq, w), lambda i: (i, COL_CZ // w)), pl.BlockSpec((tq, w), lambda i: (i, COL_CZ // w + 1)),
                  row(C_WIDTH), row(C_HEADS * 128)],
        out_specs=[row(AB), pl.BlockSpec((DM, D), lambda i: (0, 0)), row(C_WIDTH),
                   pl.BlockSpec((C_HEADS // 2, 1, 8, tq), lambda i: (0, i, 0, 0)), row(C_WIDTH),
                   pl.BlockSpec((C_HEADS // 2, 1, 128, tq), lambda i: (0, i, 0, 0)),
                   pl.BlockSpec((C_HEADS, 1, 128, tq), lambda i: (0, i, 0, 0))],
        out_shape=[SDS((T, AB), F32), SDS((DM, D), F32), SDS((T, C_WIDTH), BF16), SDS((C_HEADS // 2, nq, 8, tq), F32),
                   SDS((T, C_WIDTH), BF16), SDS((C_HEADS // 2, nq, 128, tq), BF16), SDS((C_HEADS, nq, 128, tq), BF16)],
        compiler_params=_params("arbitrary"),
    )(dx, ya, yb, yc, wo, proj, proj, o, qt)


DW_IN_GROUPS = ((0, 256), (256, 256), (512, 512))


def _piece_offsets(pieces):
    offs = [0]
    for p in pieces:
        offs.append(offs[-1] + p.shape[1])
    return offs


def _dw_in(h, pieces, dp_width, shard, cols, tag, ride=None):
    T = h.shape[0]
    first, D = cols
    assert N_CHIPS * shard <= dp_width and first % D == 0
    tm = _tile(T, 512)
    grid = (T // tm,)
    offs = _piece_offsets(pieces)
    n = len(pieces)

    def body(h_ref, *rest):
        p_refs, rest = rest[:n], rest[n:]
        ride_srcs, (dw_ref,), ride_dsts, (acc_ref,), ride_sems = _ride_refs(ride, rest, 1, 1)
        i = pl.program_id(0)
        _ride_start(ride, grid, ride_srcs, ride_dsts, ride_sems)

        @pl.when(i == 0)
        def _():
            acc_ref[...] = jnp.zeros_like(acc_ref)

        hv = h_ref[...]
        for k, p_ref in enumerate(p_refs):
            acc_ref[offs[k]:offs[k + 1], :] += _dot_tn(p_ref[...], hv)

        @pl.when(i == grid[0] - 1)
        def _():
            for k in range(N_CHIPS):
                dw_ref[k] = acc_ref[shard * k:shard * (k + 1), :].astype(BF16)

        _ride_wait(ride, grid, ride_srcs, ride_dsts, ride_sems)

    extra = ride or _ChipExchange("gather", ())
    return pl.pallas_call(
        body, name=f"dw_in_{tag}", grid=grid,
        in_specs=[pl.BlockSpec((tm, D), lambda i: (i, first // D))]
        + [pl.BlockSpec((tm, p.shape[1]), lambda i: (i, 0)) for p in pieces] + extra.in_specs,
        out_specs=[pl.BlockSpec((N_CHIPS, shard, D), lambda i: (0, 0, 0))] + extra.out_specs,
        out_shape=[SDS((N_CHIPS, shard, D), BF16)] + extra.out_shape,
        scratch_shapes=[pltpu.VMEM((dp_width, D), F32)] + (extra.scratch if ride else []),
        compiler_params=pltpu.CompilerParams(dimension_semantics=("arbitrary",), vmem_limit_bytes=VMEM_LIMIT,
                                             has_side_effects=bool(ride)),
    )(h, *pieces, *extra.sources)


def _dx_in(x, g, dres, pieces, w, tag, ride=None):
    T, D = x.shape
    tm = _tile(T, 512)
    grid = (T // tm,)
    offs = _piece_offsets(pieces)
    n = len(pieces)

    def body(x_ref, g_ref, dres_ref, w_ref, *rest):
        p_refs, rest = rest[:n], rest[n:]
        ride_srcs, (dx_ref, dg_ref), ride_dsts, (wt_ref,), ride_sems = _ride_refs(ride, rest, 2, 1)
        _ride_start(ride, grid, ride_srcs, ride_dsts, ride_sems)

        @pl.when(pl.program_id(0) == 0)
        def _():
            dg_ref[...] = jnp.zeros_like(dg_ref)
            _assemble_w_in(w_ref, wt_ref)

        dh = _dot(p_refs[0][...], wt_ref[offs[0]:offs[1], :])
        for k in range(1, n):
            dh = dh + _dot(p_refs[k][...], wt_ref[offs[k]:offs[k + 1], :])
        xv = x_ref[...]
        r = lax.rsqrt(jnp.mean(xv * xv, axis=-1, keepdims=True) + NORM_EPS)
        xh = xv * r
        dg_ref[...] += jnp.sum(dh * xh, axis=0, keepdims=True)
        dxh = dh * g_ref[...]
        dx_ref[...] = dres_ref[...] + r * (dxh - xh * jnp.mean(dxh * xh, axis=-1, keepdims=True))
        _ride_wait(ride, grid, ride_srcs, ride_dsts, ride_sems)

    extra = ride or _ChipExchange("gather", ())
    row = pl.BlockSpec((tm, D), lambda i: (i, 0))
    return pl.pallas_call(
        body, name=f"dx_in_{tag}", grid=grid,
        in_specs=[row, pl.BlockSpec((1, D), lambda i: (0, 0)), row, pl.BlockSpec(w.shape, lambda i: (0, 0, 0))]
        + [pl.BlockSpec((tm, p.shape[1]), lambda i: (i, 0)) for p in pieces] + extra.in_specs,
        out_specs=[row, pl.BlockSpec((1, D), lambda i: (0, 0))] + extra.out_specs,
        out_shape=[SDS((T, D), F32), SDS((1, D), F32)] + extra.out_shape,
        scratch_shapes=[pltpu.VMEM((offs[-1], D), BF16)] + (extra.scratch if ride else []),
        compiler_params=pltpu.CompilerParams(dimension_semantics=("arbitrary",), vmem_limit_bytes=VMEM_LIMIT,
                                             has_side_effects=bool(ride)),
    )(x, g, dres, w, *pieces, *extra.sources)


def _gmlp_core(u, v, lng, lnb, wm_ref, bst_ref, pair):
    ug, dug = _gelu_and_grad(u)
    vg, dvg = _gelu_and_grad(v)
    mu = _gsum64(vg) * (1.0 / 64)
    d = vg - mu
    var = _gsum64(d * d) * (1.0 / 64)
    rstd = lax.rsqrt(var + NORM_EPS)
    xh = d * rstd
    vn = xh * lng + lnb
    vnb = vn.astype(BF16)
    lo = _lane(u.shape) < 64
    g0, g1 = 2 * pair, 2 * pair + 1
    mixed = jnp.where(lo, _dot(wm_ref[g0], vnb) + bst_ref[:, g0:g0 + 1], _dot(wm_ref[g1], vnb) + bst_ref[:, g1:g1 + 1])
    return ug, dug, dvg, rstd, xh, vnb, mixed, lo


def _gmlp_fwd(proj, lng, lnb, wm, bst):
    T = proj.shape[0]

    def body(u_ref, v_ref, z_ref, lng_ref, lnb_ref, wm_ref, bst_ref, y_ref):
        for pair in range(2):
            sl = slice(128 * pair, 128 * pair + 128)
            ug, _, _, _, _, _, mixed, _ = _gmlp_core(u_ref[:, sl], v_ref[:, sl], lng_ref[:, sl], lnb_ref[:, sl],
                                                     wm_ref, bst_ref, pair)
            sz, _ = _silu_and_grad(z_ref[:, sl])
            y_ref[:, sl] = (ug * mixed * sz).astype(BF16)

    col = lambda c: pl.BlockSpec((CHUNK, A_WIDTH), lambda i, c=c: (i, c // A_WIDTH))
    full = lambda a: pl.BlockSpec(a.shape, lambda i, n=a.ndim: (0,) * n)
    return _Part(body, (proj, proj, proj, lng, lnb, wm, bst),
                 [col(COL_AU), col(COL_AV), col(COL_AZ), full(lng), full(lnb), full(wm), full(bst)],
                 [pl.BlockSpec((CHUNK, A_WIDTH), lambda i: (i, 0))], [SDS((T, A_WIDTH), BF16)], [])


def _gmlp_bwd(proj, dy, lng, lnb, wm, wmt, bst):
    T = proj.shape[0]
    n = T // CHUNK

    def body(u_ref, v_ref, z_ref, dy_ref, lng_ref, lnb_ref, wm_ref, wmt_ref, bst_ref,
             da_ref, dwm_ref, dbst_ref, dlng_ref, dlnb_ref):
        @pl.when(pl.program_id(0) == 0)
        def _():
            dwm_ref[...] = jnp.zeros_like(dwm_ref)
            dbst_ref[...] = jnp.zeros_like(dbst_ref)
            dlng_ref[...] = jnp.zeros_like(dlng_ref)
            dlnb_ref[...] = jnp.zeros_like(dlnb_ref)

        lane = _lane((CHUNK, 128))
        dbst = dbst_ref[...]
        for pair in range(2):
            sl = slice(128 * pair, 128 * pair + 128)
            lng_p = lng_ref[:, sl]
            ug, dug, dvg, rstd, xh, vnb, mixed, lo = _gmlp_core(u_ref[:, sl], v_ref[:, sl], lng_p, lnb_ref[:, sl],
                                                                wm_ref, bst_ref, pair)
            sz, dsz = _silu_and_grad(z_ref[:, sl])
            dyv = dy_ref[:, sl]
            out = ug * mixed
            dz = dyv * out * dsz
            dout = dyv * sz
            du = dout * mixed * dug
            dmix = dout * ug
            g0, g1 = 2 * pair, 2 * pair + 1
            dm0 = jnp.where(lo, dmix, 0.0)
            dm1 = jnp.where(lo, 0.0, dmix)
            dbst = dbst + jnp.where(lane == g0, jnp.sum(dm0, axis=-1, keepdims=True), 0.0)
            dbst = dbst + jnp.where(lane == g1, jnp.sum(dm1, axis=-1, keepdims=True), 0.0)
            dwm_ref[g0] += _dot_nt(dm0.astype(BF16), vnb)
            dwm_ref[g1] += _dot_nt(dm1.astype(BF16), vnb)
            dmb = dmix.astype(BF16)
            dvn = jnp.where(lo, _dot(wmt_ref[g0], dmb), _dot(wmt_ref[g1], dmb))
            dlng_ref[:, sl] += jnp.sum(dvn * xh, axis=0, keepdims=True)
            dlnb_ref[:, sl] += jnp.sum(dvn, axis=0, keepdims=True)
            dxh = dvn * lng_p
            m1 = _gsum64(dxh) * (1.0 / 64)
            m2 = _gsum64(dxh * xh) * (1.0 / 64)
            dv = rstd * (dxh - m1 - xh * m2) * dvg
            da_ref[:, COL_AU + 128 * pair:COL_AU + 128 * pair + 128] = du.astype(BF16)
            da_ref[:, COL_AV + 128 * pair:COL_AV + 128 * pair + 128] = dv.astype(BF16)
            da_ref[:, COL_AZ + 128 * pair:COL_AZ + 128 * pair + 128] = dz.astype(BF16)
        dbst_ref[...] = dbst

        @pl.when(pl.program_id(0) == n - 1)
        def _():
            causal = _lane((CHUNK, CHUNK)) <= _row((CHUNK, CHUNK))
            for g in range(A_GROUPS):
                dwm_ref[g] = jnp.where(causal, dwm_ref[g], 0.0)

    col = lambda c: pl.BlockSpec((CHUNK, A_WIDTH), lambda i, c=c: (i, c // A_WIDTH))
    full = lambda a: pl.BlockSpec(a.shape, lambda i, n=a.ndim: (0,) * n)
    acc = lambda s: pl.BlockSpec(s, lambda i, n=len(s): (0,) * n)
    return _Part(body, (proj, proj, proj, dy, lng, lnb, wm, wmt, bst),
                 [col(COL_AU), col(COL_AV), col(COL_AZ), pl.BlockSpec((CHUNK, A_WIDTH), lambda i: (i, 0)),
                  full(lng), full(lnb), full(wm), full(wmt), full(bst)],
                 [pl.BlockSpec((CHUNK, 3 * A_WIDTH), lambda i: (i, 0)), acc((A_GROUPS, CHUNK, CHUNK)),
                  acc((CHUNK, 128)), acc((1, A_WIDTH)), acc((1, A_WIDTH))],
                 [SDS((T, 3 * A_WIDTH), BF16), SDS((A_GROUPS, CHUNK, CHUNK), F32), SDS((CHUNK, 128), F32),
                  SDS((1, A_WIDTH), F32), SDS((1, A_WIDTH), F32)], [])


def _hgrn_consts():
    r, c = _row((CHUNK, CHUNK)), _lane((CHUNK, CHUNK))
    same = (r >> SUB_SHIFT) == (c >> SUB_SHIFT)
    lsub = jnp.where(same & (c <= r), 1.0, 0.0).astype(BF16)
    usub = jnp.where(same & (c >= r), 1.0, 0.0).astype(BF16)
    bsub = jnp.where(same, 1.0, 0.0).astype(BF16)
    return lsub, usub, bsub


def _hgrn_gates(qv, zf, lbp):
    sq, dsq = _silu_and_grad(qv)
    qt = sq * Q_SCALE
    sg = _sigmoid(zf)
    sgn = _sigmoid(-zf)
    f = lbp + (1.0 - lbp) * sg
    g = jnp.log(jnp.maximum(f, F_FLOOR))
    kf = (1.0 - lbp) * sgn
    return qt, dsq, sg, sgn, f, g, kf


def _hgrn_intra_scores(qt, kf, b, mbd):
    rid = _row((SUB, 128))
    parts = []
    for s in range(SUB):
        e = jnp.exp(b - b[s:s + 1, :])
        parts.append(jnp.where(rid >= s, qt * kf[s:s + 1, :] * e, 0.0))
    return _dot(jnp.concatenate(parts, axis=0).astype(BF16), mbd)


def _hgrn_intra_out(a, v):
    o = jnp.zeros((SUB, 128), F32)
    for s in range(SUB):
        o = o + a[SUB * s:SUB * s + SUB, :] * v[s:s + 1, :]
    return o


def _hgrn_intra_bwd_scores(qt, kf, b, v, do, mbd):
    rid = _row((SUB, 128))
    ps, das, kes, es = [], [], [], []
    for s in range(SUB):
        e = jnp.where(rid >= s, jnp.exp(b - b[s:s + 1, :]), 0.0)
        ke = kf[s:s + 1, :] * e
        es.append(e)
        kes.append(ke)
        ps.append(qt * ke)
        das.append(do * v[s:s + 1, :])
    a = _dot(jnp.concatenate(ps, axis=0).astype(BF16), mbd)
    da = _dot(jnp.concatenate(das, axis=0).astype(BF16), mbd)
    return a, da, kes, es


def _hgrn_intra_bwd_grads(scores, qt, do, rsum):
    a, da, kes, es = scores
    dqt = jnp.zeros((SUB, 128), F32)
    xs, ys = [], []
    for s in range(SUB):
        da_s = da[SUB * s:SUB * s + SUB, :]
        dqt = dqt + da_s * kes[s]
        xs.append(a[SUB * s:SUB * s + SUB, :] * do)
        ys.append(da_s * qt * es[s])
    dv = _dot(rsum, jnp.concatenate(xs, axis=0).astype(BF16))
    dkf = _dot(rsum, jnp.concatenate(ys, axis=0).astype(BF16))
    return dqt, dkf, dv


def _hgrn_norm_gate(o, z, onorm):
    ms = _gsum64(o * o) * (1.0 / 64)
    r = lax.rsqrt(ms + NORM_EPS)
    xh = o * r
    sz, dsz = _silu_and_grad(z)
    return xh, r, sz, dsz, xh * onorm


def _hgrn_fwd(proj, lb, onorm):
    T = proj.shape[0]
    n = T // CHUNK
    nsub = CHUNK // SUB

    def body(q_ref, f_ref, i_ref, z_ref, lb_ref, on_ref, y_ref, o_ref, s0_ref, st_ref):
        @pl.when(pl.program_id(0) == 0)
        def _():
            st_ref[...] = jnp.zeros_like(st_ref)

        lsub, _, bsub = _hgrn_consts()
        mbd = _block_diag64()
        bdmask = mbd > 0
        rid = _row((CHUNK, 128))
        subs = [slice(SUB * sub, SUB * sub + SUB) for sub in range(nsub)]
        work = []
        for pair in range(2):
            sl = slice(128 * pair, 128 * pair + 128)
            qt, _, _, _, _, g, kf = _hgrn_gates(q_ref[:, sl], f_ref[:, sl], lb_ref[:, sl])
            work.append(dict(sl=sl, qt=qt, kf=kf, v=i_ref[:, sl], b=_dot3_left(lsub, g), bl=_dot3_left(bsub, g)))
        for w in work:
            qt, kf, v, b, bl = w["qt"], w["kf"], w["v"], w["b"], w["bl"]
            w["qh"] = (qt * jnp.exp(b)).astype(BF16)
            kh = kf * jnp.exp(bl - b)
            w["dec"] = jnp.exp(bl)
            vtb = v.T.astype(BF16)
            w["scores"] = [_hgrn_intra_scores(qt[rs], kf[rs], b[rs], mbd) for rs in subs]
            w["adds"] = [_dot(vtb, jnp.where((rid >> SUB_SHIFT) == sub, kh, 0.0).astype(BF16)) for sub in range(nsub)]
        for pair, w in enumerate(work):
            w["st"] = st_ref[pair]
            s0_ref[0, pair] = w["st"]
            w["outs"] = []
        for sub, rs in enumerate(subs):
            for w in work:
                w["outs"].append(_dot_nt(w["qh"][rs], w["st"].astype(BF16)) + _hgrn_intra_out(w["scores"][sub], w["v"][rs]))
                w["st"] = jnp.where(bdmask, w["st"] * w["dec"][SUB * sub:SUB * sub + 1, :] + w["adds"][sub], 0.0)
        for pair, w in enumerate(work):
            sl = w["sl"]
            st_ref[pair] = w["st"]
            o = jnp.concatenate(w["outs"], axis=0)
            o_ref[:, sl] = o
            _, _, sz, _, on = _hgrn_norm_gate(o, z_ref[:, sl], on_ref[:, sl])
            y_ref[:, sl] = (on * sz).astype(BF16)

    col = lambda c: pl.BlockSpec((CHUNK, B_WIDTH), lambda i, c=c: (i, c // B_WIDTH))
    full = lambda a: pl.BlockSpec(a.shape, lambda i, n=a.ndim: (0,) * n)
    return _Part(body, (proj, proj, proj, proj, lb, onorm),
                 [col(COL_BQ), col(COL_BF), col(COL_BI), col(COL_BZ), full(lb), full(onorm)],
                 [pl.BlockSpec((CHUNK, B_WIDTH), lambda i: (i, 0)), pl.BlockSpec((CHUNK, B_WIDTH), lambda i: (i, 0)),
                  pl.BlockSpec((1, 2, 128, 128), lambda i: (i, 0, 0, 0))],
                 [SDS((T, B_WIDTH), BF16), SDS((T, B_WIDTH), F32), SDS((n, 2, 128, 128), F32)],
                 [pltpu.VMEM((2, 128, 128), F32)])


def _hgrn_bwd(proj, dy, o_saved, s0, lb, onorm):
    T = proj.shape[0]
    n = T // CHUNK
    nsub = CHUNK // SUB

    def body(q_ref, f_ref, i_ref, z_ref, dy_ref, o_ref, s0_ref, lb_ref, on_ref,
             db_ref, dlb_ref, don_ref, dst_ref, sts_ref):
        @pl.when(pl.program_id(0) == 0)
        def _():
            dst_ref[...] = jnp.zeros_like(dst_ref)
            dlb_ref[...] = jnp.zeros_like(dlb_ref)
            don_ref[...] = jnp.zeros_like(don_ref)

        lsub, usub, bsub = _hgrn_consts()
        mbd = _block_diag64()
        bdmask = mbd > 0
        rsum = jnp.where((_lane((SUB, SUB * SUB)) >> SUB_SHIFT) == _row((SUB, SUB * SUB)), 1.0, 0.0).astype(BF16)
        subs = [slice(SUB * sub, SUB * sub + SUB) for sub in range(nsub)]
        work = []
        for pair in range(2):
            sl = slice(128 * pair, 128 * pair + 128)
            lbp = lb_ref[:, sl]
            qt, dsq, sg, sgn, f, g, kf = _hgrn_gates(q_ref[:, sl], f_ref[:, sl], lbp)
            w = dict(sl=sl, lbp=lbp, qt=qt, dsq=dsq, sg=sg, sgn=sgn, f=f, kf=kf, v=i_ref[:, sl],
                     b=_dot3_left(lsub, g), bl=_dot3_left(bsub, g))
            onp = on_ref[:, sl]
            xh, r, sz, dsz, on = _hgrn_norm_gate(o_ref[:, sl], z_ref[:, sl], onp)
            dyv = dy_ref[:, sl]
            w["dz"] = dyv * on * dsz
            don = dyv * sz
            cn = jnp.sum(don * xh, axis=0, keepdims=True)
            don_ref[...] += cn + pltpu.roll(cn, 64, axis=1)
            dxo = don * onp
            w["do"] = r * (dxo - xh * (_gsum64(dxo * xh) * (1.0 / 64)))
            work.append(w)
        for w in work:
            qt, kf, v, b, bl, do = w["qt"], w["kf"], w["v"], w["b"], w["bl"], w["do"]
            w["eb"] = jnp.exp(b)
            w["ekb"] = jnp.exp(bl - b)
            w["qhb"] = (qt * w["eb"]).astype(BF16)
            w["khb"] = (kf * w["ekb"]).astype(BF16)
            w["dec"] = jnp.exp(bl)
            w["vb"] = v.astype(BF16)
            w["dob"] = do.astype(BF16)
            w["scores"] = [_hgrn_intra_bwd_scores(qt[rs], kf[rs], b[rs], v[rs], do[rs], mbd) for rs in subs]
            w["st_adds"] = [_dot_tn(w["vb"][rs], w["khb"][rs]) for rs in subs]
            w["gst_adds"] = [_dot_tn(w["dob"][rs], w["qhb"][rs]) for rs in subs]
        for pair, w in enumerate(work):
            w["st"] = s0_ref[0, pair]
        for sub in range(nsub):
            for pair, w in enumerate(work):
                sts_ref[pair, sub] = w["st"]
                w["st"] = jnp.where(bdmask, w["st"] * w["dec"][SUB * sub:SUB * sub + 1, :] + w["st_adds"][sub], 0.0)
        for pair, w in enumerate(work):
            w["gst"] = dst_ref[pair]
            w["dqt_p"], w["dkf_p"], w["dv_p"], w["dbl_p"] = ([None] * nsub for _ in range(4))
        for sub in reversed(range(nsub)):
            rs = subs[sub]
            for pair, w in enumerate(work):
                gst = w["gst"]
                st_in = sts_ref[pair, sub]
                gb = gst.astype(BF16)
                dqh = _dot(w["dob"][rs], st_in.astype(BF16))
                dkh = _dot(w["vb"][rs], gb)
                dv_inter = _dot_nt(w["khb"][rs], gb)
                ddec = jnp.sum(gst * st_in, axis=0, keepdims=True)
                dec_row = w["dec"][SUB * sub:SUB * sub + 1, :]
                w["gst"] = jnp.where(bdmask, gst * dec_row + w["gst_adds"][sub], 0.0)
                dqt_i, dkf_i, dv_i = _hgrn_intra_bwd_grads(w["scores"][sub], w["qt"][rs], w["do"][rs], rsum)
                dkf_inter = dkh * w["ekb"][rs]
                w["dqt_p"][sub] = dqh * w["eb"][rs] + dqt_i
                w["dkf_p"][sub] = dkf_inter + dkf_i
                w["dv_p"][sub] = dv_inter + dv_i
                row = jnp.sum(w["kf"][rs] * dkf_inter, axis=0, keepdims=True) + ddec * dec_row
                w["dbl_p"][sub] = jnp.broadcast_to(row, (SUB, 128))
        for pair, w in enumerate(work):
            sl, lbp, sg, sgn, f = w["sl"], w["lbp"], w["sg"], w["sgn"], w["f"]
            dst_ref[pair] = w["gst"]
            dqt = jnp.concatenate(w["dqt_p"], axis=0)
            dkf = jnp.concatenate(w["dkf_p"], axis=0)
            dv = jnp.concatenate(w["dv_p"], axis=0)
            dg = _dot3_left(usub, w["qt"] * dqt - w["kf"] * dkf) + jnp.concatenate(w["dbl_p"], axis=0)
            df = jnp.where(f > F_FLOOR, dg / f, 0.0)
            dlb_ref[:, sl] += jnp.sum(df * (1.0 - sg) - dkf * sgn, axis=0, keepdims=True)
            dfl = (1.0 - lbp) * sg * sgn * (df - dkf)
            dq = dqt * Q_SCALE * w["dsq"]
            db_ref[:, 0 * B_WIDTH + 128 * pair:0 * B_WIDTH + 128 * pair + 128] = dq.astype(BF16)
            db_ref[:, 1 * B_WIDTH + 128 * pair:1 * B_WIDTH + 128 * pair + 128] = dfl.astype(BF16)
            db_ref[:, 2 * B_WIDTH + 128 * pair:2 * B_WIDTH + 128 * pair + 128] = dv.astype(BF16)
            db_ref[:, 3 * B_WIDTH + 128 * pair:3 * B_WIDTH + 128 * pair + 128] = w["dz"].astype(BF16)

    rev = lambda c: pl.BlockSpec((CHUNK, B_WIDTH), lambda i, c=c: (n - 1 - i, c // B_WIDTH))
    full = lambda a: pl.BlockSpec(a.shape, lambda i, n_=a.ndim: (0,) * n_)
    acc = lambda s: pl.BlockSpec(s, lambda i, n_=len(s): (0,) * n_)
    return _Part(body, (proj, proj, proj, proj, dy, o_saved, s0, lb, onorm),
                 [rev(COL_BQ), rev(COL_BF), rev(COL_BI), rev(COL_BZ),
                  pl.BlockSpec((CHUNK, B_WIDTH), lambda i: (n - 1 - i, 1)),
                  pl.BlockSpec((CHUNK, B_WIDTH), lambda i: (n - 1 - i, 0)),
                  pl.BlockSpec((1, 2, 128, 128), lambda i: (n - 1 - i, 0, 0, 0)), full(lb), full(onorm)],
                 [pl.BlockSpec((CHUNK, 4 * B_WIDTH), lambda i: (n - 1 - i, 0)), acc((1, B_WIDTH)), acc((1, 128))],
                 [SDS((T, 4 * B_WIDTH), BF16), SDS((1, B_WIDTH), F32), SDS((1, 128), F32)],
                 [pltpu.VMEM((2, 128, 128), F32), pltpu.VMEM((2, nsub, 128, 128), F32)])


def _lb_fwd(hgrn_lb):
    assert hgrn_lb.shape[0] == 2

    def body(x_ref, o_ref):
        x0, x1 = x_ref[0:1, :], x_ref[1:2, :]
        m = jnp.maximum(x0, x1)
        e0, e1 = jnp.exp(x0 - m), jnp.exp(x1 - m)
        p0, p1 = e0 / (e0 + e1), e1 / (e0 + e1)
        o_ref[0:1, :] = jnp.clip(p0 - p0, 0.0, 1.0 - 1e-6)
        o_ref[1:2, :] = jnp.clip((p0 + p1) - p0, 0.0, 1.0 - 1e-6)

    return pl.pallas_call(body, name="lb_fwd", out_shape=SDS(hgrn_lb.shape, F32))(hgrn_lb)


def _lb_bwd(hgrn_lb, dlb):
    def body(x_ref, d_ref, o_ref):
        x0, x1 = x_ref[0:1, :], x_ref[1:2, :]
        m = jnp.maximum(x0, x1)
        e0, e1 = jnp.exp(x0 - m), jnp.exp(x1 - m)
        p0, p1 = e0 / (e0 + e1), e1 / (e0 + e1)
        val = (p0 + p1) - p0
        dp1 = jnp.where((val > 0.0) & (val < 1.0 - 1e-6), d_ref[1:2, :], 0.0)
        inner = p1 * dp1
        o_ref[0:1, :] = p0 * (0.0 - inner)
        o_ref[1:2, :] = p1 * (dp1 - inner)

    return pl.pallas_call(body, name="lb_bwd", out_shape=SDS(hgrn_lb.shape, F32))(hgrn_lb, dlb)


def _fox_prep(proj, bf):
    T = proj.shape[0]
    n = T // CHUNK

    def body(q0_ref, q1_ref, k0_ref, k1_ref, v0_ref, v1_ref, fl_ref, bf_ref, qo_ref, ko_ref, vt_ref, carry_ref):
        for p, v_ref in enumerate((v0_ref, v0_ref, v1_ref, v1_ref)):
            vt_ref[p, 0] = v_ref[:, 128 * (p % 2):128 * (p % 2) + 128].T.astype(BF16)

        @pl.when(pl.program_id(0) == 0)
        def _():
            carry_ref[...] = jnp.zeros_like(carry_ref)

        ltri = jnp.where(_lane((CHUNK, CHUNK)) <= _row((CHUNK, CHUNK)), 1.0, 0.0).astype(BF16)
        lf = jax.nn.log_sigmoid(fl_ref[...] + bf_ref[...])
        c = _dot3_left(ltri, lf) + carry_ref[...]
        carry_ref[...] = c[CHUNK - 1:CHUNK, :]
        lane = _lane((CHUNK, 128))
        feat = lane < 64
        ones_q = (lane >= 67) & (lane <= 69)
        ones_k = (lane >= 64) & (lane <= 66)
        qrefs, krefs = (q0_ref, q1_ref), (k0_ref, k1_ref)
        for h in range(C_HEADS):
            blk = slice(128 * ((h // 2) % 2), 128 * ((h // 2) % 2) + 128)
            qp, kp = qrefs[h // 4][:, blk], krefs[h // 4][:, blk]
            if h % 2:
                qp, kp = pltpu.roll(qp, 64, axis=1), pltpu.roll(kp, 64, axis=1)
            ch = jnp.broadcast_to(c[:, h:h + 1], (CHUNK, 128))
            hi = ch.astype(BF16).astype(F32)
            r1 = ch - hi
            mid = r1.astype(BF16).astype(F32)
            lo = r1 - mid
            aq = jnp.where(lane == 64, hi, jnp.where(lane == 65, mid, jnp.where(lane == 66, lo,
                           jnp.where(ones_q, 1.0, 0.0))))
            ak = jnp.where(lane == 67, -hi, jnp.where(lane == 68, -mid, jnp.where(lane == 69, -lo,
                           jnp.where(ones_k, 1.0, 0.0))))
            qo_ref[:, 128 * h:128 * h + 128] = jnp.where(feat, qp * Q_SCALE, aq).astype(BF16)
            ko_ref[:, 128 * h:128 * h + 128] = jnp.where(feat, kp, ak).astype(BF16)

    w = 256
    col = lambda c: pl.BlockSpec((CHUNK, w), lambda i, c=c: (i, c // w))
    return _Part(body, (proj, proj, proj, proj, proj, proj, proj, bf),
                 [col(COL_CQ), col(COL_CQ + w), col(COL_CK), col(COL_CK + w), col(COL_CV), col(COL_CV + w),
                  pl.BlockSpec((CHUNK, 128), lambda i: (i, COL_CF // 128)), pl.BlockSpec((1, 128), lambda i: (0, 0))],
                 [pl.BlockSpec((CHUNK, C_HEADS * 128), lambda i: (i, 0))] * 2
                 + [pl.BlockSpec((C_HEADS // 2, 1, 128, CHUNK), lambda i: (0, i, 0, 0))],
                 [SDS((T, C_HEADS * 128), BF16)] * 2 + [SDS((C_HEADS // 2, n, 128, CHUNK), BF16)],
                 [pltpu.VMEM((1, 128), F32)])


FOX_TILE = 512
FOX_KEYS = 512
FOX_STRIP = 16


def _fox_mask(tk, tq, k0, q0):
    return (_row((tk, tq)) + (k0 - q0)) <= _lane((tk, tq))


def _ride_refs(ride, rest, n_out, n_scratch):
    n = ride.n if ride else 0
    srcs, rest = rest[:n], rest[n:]
    outs, rest = rest[:n_out], rest[n_out:]
    dsts, rest = rest[:n], rest[n:]
    return srcs, outs, dsts, rest[:n_scratch], rest[n_scratch:]


def _ride_start(ride, grid, srcs, dsts, sems):
    if ride:
        first = functools.reduce(lambda a, b: a & b, [pl.program_id(d) == 0 for d in range(len(grid))])
        pl.when(first)(lambda: ride.start(srcs, dsts, sems))


def _ride_wait(ride, grid, srcs, dsts, sems):
    if ride:
        last = functools.reduce(lambda a, b: a & b, [pl.program_id(d) == n - 1 for d, n in enumerate(grid)])
        pl.when(last)(lambda: ride.wait(srcs, dsts, sems))


def _fox_fwd(qt, kt, vt, proj, tag, ride=None):
    T = proj.shape[0]
    tq, tk = _tile(T, FOX_TILE), _tile(T, FOX_KEYS)
    nq, nsub = T // tq, tk // CHUNK
    npair = C_HEADS // 2

    def body(q_ref, k_ref, vt_ref, z_ref, *rest):
        ride_srcs, (o_ref, lse_ref, y_ref), ride_dsts, (acc_ref, st_ref, pt_ref), ride_sems = _ride_refs(ride, rest, 3, 3)
        i = pl.program_id(1)
        _ride_start(ride, (npair, nq), ride_srcs, ride_dsts, ride_sems)

        qs = (q_ref[:, 0:128], q_ref[:, 128:256])
        acc_ref[...] = jnp.zeros_like(acc_ref)
        pt_ref[...] = jnp.zeros_like(pt_ref)
        nfull = (i * tq) // tk

        def scores(j):
            kb = k_ref[pl.ds(pl.multiple_of(j * tk, tk), tk), :]
            return tuple(_dot_nt(kb[:, 128 * h:128 * h + 128], qs[h]) for h in range(2))

        def weigh(j, h):
            rows = slice(64 * h, 64 * h + 64)
            vth = jnp.concatenate([vt_ref[0, nsub * j + c, rows, :] for c in range(nsub)], axis=1)
            return _dot(vth, pt_ref[h])

        def block(j, carry, diagonal):
            nxt = () if diagonal else scores(j + 1)
            pvs = [weigh(jnp.maximum(j - 1, 0), h) for h in range(2)]
            new = []
            for h in range(2):
                m, l, alpha_prev = carry[3 * h:3 * h + 3]
                st = st_ref[h]
                if diagonal:
                    st = jnp.where(_fox_mask(tk, tq, j * tk, i * tq), st, -jnp.inf)
                m_new = jnp.maximum(m, _colreduce(st, jnp.maximum))
                pt = jnp.exp(st - m_new)
                alpha = jnp.exp(m - m_new)
                rows = slice(64 * h, 64 * h + 64)
                acc_ref[rows, :] = alpha_prev * acc_ref[rows, :] + pvs[h]
                pt_ref[h] = pt.astype(BF16)
                new += [m_new, alpha * l + _colreduce(pt, jnp.add), alpha]
            for h, st in enumerate(nxt):
                st_ref[h] = st
            return tuple(new)

        for h, st in enumerate(scores(0)):
            st_ref[h] = st
        init = (jnp.full((1, tq), -jnp.inf, F32), jnp.zeros((1, tq), F32), jnp.ones((1, tq), F32)) * 2
        carry = lax.fori_loop(0, nfull, lambda j, c: block(j, c, False), init)
        m0, l0, a0, m1, l1, a1 = block(nfull, carry, True)
        for h, alpha in enumerate((a0, a1)):
            rows = slice(64 * h, 64 * h + 64)
            acc_ref[rows, :] = alpha * acc_ref[rows, :] + weigh(nfull, h)
        inv = jnp.where(_row((128, tq)) < 64, 1.0 / l0, 1.0 / l1)
        o = (acc_ref[...] * inv).T
        o_ref[...] = o
        r8 = _row((8, tq))
        lse_ref[0, 0] = jnp.where(r8 == 0, m0 + jnp.log(l0), jnp.where(r8 == 1, m1 + jnp.log(l1), 0.0))
        sz, _ = _silu_and_grad(z_ref[...])
        y_ref[...] = (o * sz).astype(BF16)
        _ride_wait(ride, (npair, nq), ride_srcs, ride_dsts, ride_sems)

    blk = pl.BlockSpec((tq, 128), lambda p, i: (i, p))
    extra = ride or _ChipExchange("gather", ())
    return pl.pallas_call(
        body, name=f"fox_fwd_{tag}", grid=(npair, nq),
        in_specs=[pl.BlockSpec((tq, 256), lambda p, i: (i, p)), pl.BlockSpec((T, 256), lambda p, i: (0, p)),
                  pl.BlockSpec((1, T // CHUNK, 128, CHUNK), lambda p, i: (p, 0, 0, 0)),
                  pl.BlockSpec((tq, 128), lambda p, i: (i, COL_CZ // 128 + p))] + extra.in_specs,
        out_specs=[blk, pl.BlockSpec((1, 1, 8, tq), lambda p, i: (p, i, 0, 0)), blk] + extra.out_specs,
        out_shape=[SDS((T, C_WIDTH), F32), SDS((npair, nq, 8, tq), F32), SDS((T, C_WIDTH), BF16)] + extra.out_shape,
        scratch_shapes=[pltpu.VMEM((128, tq), F32), pltpu.VMEM((2, tk, tq), F32), pltpu.VMEM((2, tk, tq), BF16)]
        + (extra.scratch if ride else []),
        compiler_params=pltpu.CompilerParams(dimension_semantics=("arbitrary", "arbitrary"), vmem_limit_bytes=VMEM_LIMIT,
                                             has_side_effects=bool(ride)),
    )(qt, kt, vt, proj, *extra.sources)


def _fox_bwd(qt, kt, proj, do, lse, delta, dot, qtr, tag, ride=None):
    T = proj.shape[0]
    tq, tk = _tile(T, FOX_TILE), _tile(T, FOX_KEYS)
    nq, nk = T // tq, T // tk
    assert tq == tk
    npair = C_HEADS // 2

    def body(q_ref, k_ref, v_ref, do_ref, lse_ref, dl_ref, dot_ref, qtr_ref, *rest):
        ride_srcs, (dq_ref, dk_ref, dv_ref), ride_dsts, scratch, ride_sems = _ride_refs(ride, rest, 3, 4)
        dvt_ref, dkt_ref, pt_ref, ds_ref = scratch
        j = pl.program_id(1)
        first = (j * tk) // tq
        _ride_start(ride, (npair, nk), ride_srcs, ride_dsts, ride_sems)

        @pl.when(j == 0)
        def _():
            dq_ref[...] = jnp.zeros_like(dq_ref)

        dkt_ref[...] = jnp.zeros_like(dkt_ref)
        dvt_ref[...] = jnp.zeros_like(dvt_ref)
        ks = (k_ref[:, 0:128], k_ref[:, 128:256])
        kts = tuple(k.astype(F32).T.astype(BF16) for k in ks)
        vb = v_ref[...].astype(BF16)
        lo = _lane((tq, 128)) < 64

        def operands(i):
            q0 = pl.multiple_of(i * tq, tq)
            qb = q_ref[pl.ds(q0, tq), :]
            dob = do_ref[pl.ds(q0, tq), :]
            qhs = (qb[:, 0:128], qb[:, 128:256])
            dohs = (jnp.where(lo, dob, jnp.zeros_like(dob)), jnp.where(lo, jnp.zeros_like(dob), dob))
            return qhs, dohs

        def scores(i):
            qhs, dohs = operands(i)
            return tuple((_dot_nt(ks[h], qhs[h]), _dot_nt(vb, dohs[h])) for h in range(2))

        def grads(i, slot):
            for h in range(2):
                rows = slice(64 * h, 64 * h + 64)
                dvt_ref[rows, :] += _dot_nt(dot_ref[0, i, rows, :], pt_ref[slot, h])
                dkt_ref[h] += _dot_nt(qtr_ref[h, i], ds_ref[slot, h])
                dq_ref[h, i] += _dot(kts[h], ds_ref[slot, h])

        def block(i, slot, diagonal, opening):
            sc = scores(i)
            if not opening:
                grads(i - 1, 1 - slot)
            lsev = lse_ref[0, i]
            dlv = dl_ref[0, i]
            for h in range(2):
                lseh = jnp.broadcast_to(lsev[h:h + 1, :], (FOX_STRIP, tq))
                dlh = jnp.broadcast_to(dlv[h:h + 1, :], (FOX_STRIP, tq))
                for r in range(0, tk, FOX_STRIP):
                    rows = slice(r, r + FOX_STRIP)
                    pt = jnp.exp(sc[h][0][rows, :] - lseh)
                    if diagonal:
                        pt = jnp.where(_fox_mask(FOX_STRIP, tq, r, 0), pt, 0.0)
                    ds_ref[slot, h, rows, :] = (pt * (sc[h][1][rows, :] - dlh)).astype(BF16)
                    pt_ref[slot, h, rows, :] = pt.astype(BF16)

        block(first, 0, True, True)
        rest = nq - 1 - first

        def two_steps(t, carry):
            block(first + 1 + 2 * t, 1, False, False)
            block(first + 2 + 2 * t, 0, False, False)
            return carry

        lax.fori_loop(0, rest // 2, two_steps, 0)
        pl.when(rest % 2 == 1)(lambda: block(nq - 1, 1, False, False))
        grads(nq - 1, rest % 2)
        dv_ref[...] = dvt_ref[...].T.astype(BF16)
        for h in range(2):
            dk_ref[:, 128 * h:128 * h + 128] = dkt_ref[h].T
        _ride_wait(ride, (npair, nk), ride_srcs, ride_dsts, ride_sems)

    full = lambda w: pl.BlockSpec((T, w), lambda p, j: (0, p))
    stat = pl.BlockSpec((1, nq, 8, tq), lambda p, j: (p, 0, 0, 0))
    extra = ride or _ChipExchange("gather", ())
    return pl.pallas_call(
        body, name=f"fox_bwd_{tag}", grid=(npair, nk),
        in_specs=[full(256), pl.BlockSpec((tk, 256), lambda p, j: (j, p)),
                  pl.BlockSpec((tk, 128), lambda p, j: (j, COL_CV // 128 + p)), full(128), stat, stat,
                  pl.BlockSpec((1, nq, 128, tq), lambda p, j: (p, 0, 0, 0)),
                  pl.BlockSpec((2, nq, 128, tq), lambda p, j: (p, 0, 0, 0))] + extra.in_specs,
        out_specs=[pl.BlockSpec((2, nq, 128, tq), lambda p, j: (p, 0, 0, 0)), pl.BlockSpec((tk, 256), lambda p, j: (j, p)),
                   pl.BlockSpec((tk, 128), lambda p, j: (j, p))] + extra.out_specs,
        out_shape=[SDS((C_HEADS, nq, 128, tq), F32), SDS((T, C_HEADS * 128), F32), SDS((T, C_WIDTH), BF16)]
        + extra.out_shape,
        scratch_shapes=[pltpu.VMEM((128, tk), F32), pltpu.VMEM((2, 128, tk), F32),
                        pltpu.VMEM((2, 2, tk, tq), BF16), pltpu.VMEM((2, 2, tk, tq), BF16)]
        + (extra.scratch if ride else []),
        compiler_params=pltpu.CompilerParams(dimension_semantics=("arbitrary", "arbitrary"), vmem_limit_bytes=VMEM_LIMIT,
                                             has_side_effects=bool(ride)),
    )(qt, kt, proj, do, lse, delta, dot, qtr, *extra.sources)


def _fox_bwd_post(dqt, dkt, proj, bf, tag):
    T = proj.shape[0]
    tq = _tile(T, FOX_TILE)
    n = T // tq

    def body(dq_ref, dk_ref, fl_ref, bf_ref, oq_ref, ok_ref, ofl_ref, dbf_ref, carry_ref):
        @pl.when(pl.program_id(0) == 0)
        def _():
            carry_ref[...] = jnp.zeros_like(carry_ref)
            dbf_ref[...] = jnp.zeros_like(dbf_ref)

        lane = _lane((tq, 128))
        lo = lane < 64
        dqs = [dq_ref[h, 0].T for h in range(C_HEADS)]
        dc = jnp.zeros((tq, 128), F32)
        for h in range(C_HEADS):
            dc = dc + jnp.where(lane == h, dqs[h][:, 64:65] - dk_ref[:, 128 * h + 67:128 * h + 68], 0.0)
        utri = jnp.where(_lane((tq, tq)) >= _row((tq, tq)), 1.0, 0.0).astype(BF16)
        dlf = _dot3_left(utri, dc) + carry_ref[...]
        carry_ref[...] = dlf[0:1, :]
        dfl = jnp.where(lane < C_HEADS, dlf * _sigmoid(-(fl_ref[...] + bf_ref[...])), 0.0)
        ofl_ref[...] = dfl.astype(BF16)
        dbf_ref[...] += jnp.sum(dfl, axis=0, keepdims=True)
        for p in range(C_HEADS // 2):
            a, b = 128 * (2 * p), 128 * (2 * p + 1)
            oq_ref[:, 128 * p:128 * p + 128] = (
                jnp.where(lo, dqs[2 * p], pltpu.roll(dqs[2 * p + 1], 64, axis=1)) * Q_SCALE).astype(BF16)
            ok_ref[:, 128 * p:128 * p + 128] = jnp.where(
                lo, dk_ref[:, a:a + 128], pltpu.roll(dk_ref[:, b:b + 128], 64, axis=1)).astype(BF16)

    rev = lambda w: pl.BlockSpec((tq, w), lambda i: (n - 1 - i, 0))
    return pl.pallas_call(
        body, name=f"fox_bwd_post_{tag}", grid=(n,),
        in_specs=[pl.BlockSpec((C_HEADS, 1, 128, tq), lambda i: (0, n - 1 - i, 0, 0)), rev(C_HEADS * 128),
                  pl.BlockSpec((tq, 128), lambda i: (n - 1 - i, COL_CF // 128)), pl.BlockSpec((1, 128), lambda i: (0, 0))],
        out_specs=[rev(C_WIDTH), rev(C_WIDTH), rev(128), pl.BlockSpec((1, 128), lambda i: (0, 0))],
        out_shape=[SDS((T, C_WIDTH), BF16), SDS((T, C_WIDTH), BF16), SDS((T, 128), BF16), SDS((1, 128), F32)],
        scratch_shapes=[pltpu.VMEM((1, 128), F32)], compiler_params=_params("arbitrary"),
    )(dqt, dkt, proj, bf)


def _adamw_math(w, g, m, v):
    m = ADAM_B1 * m + (1.0 - ADAM_B1) * g
    v = ADAM_B2 * v + (1.0 - ADAM_B2) * (g * g)
    delta = -ADAM_LR * ((m / ADAM_C1) / (jnp.sqrt(v / ADAM_C2) + ADAM_EPS) + ADAM_WD * w)
    return delta, m, v


def _adamw_pair(w, m, v, ga, gb, name):
    n0 = w.shape[0]
    most = max(1, ADAMW_BLOCK_BYTES // (4 * math.prod(w.shape[1:])))
    t0 = max(t for t in range(1, min(n0, most) + 1) if n0 % t == 0)

    def body(w_ref, m_ref, v_ref, ga_ref, gb_ref, g_ref, d_ref, nm_ref, nv_ref):
        g = ga_ref[...] + gb_ref[...]
        g_ref[...] = g
        d_ref[...], nm_ref[...], nv_ref[...] = _adamw_math(w_ref[...], g, m_ref[...], v_ref[...])

    blk = pl.BlockSpec((t0,) + w.shape[1:], lambda i: (i, 0, 0))
    return pl.pallas_call(
        body, name=name, grid=(n0 // t0,), in_specs=[blk] * 5, out_specs=[blk] * 4,
        out_shape=[SDS(w.shape, F32)] * 4, compiler_params=_params("parallel"),
    )(w, m, v, ga, gb)


def _small_layout():
    L = DEPTH
    lanes = lambda j: slice(128 * j, 128 * j + 128)
    wide = lambda n: [((slice(l, l + 1), lanes(j)), n * l + j, 1, 0, 128) for l in range(L) for j in range(n)]
    halves = [((l, slice(g, g + 1)), 2 * l + g // 2, 1, 64 * (g % 2), 64) for l in range(L) for g in range(A_GROUPS)]
    side_by_side = lambda w: [((slice(l, l + 1),), 0, 1, w * l, w) for l in range(L)]
    return [
        ((L, D_MODEL), wide(D_MODEL // 128)), ((L, A_GROUPS, 64), halves), ((L, A_GROUPS, 64), halves),
        ((L * A_GROUPS * CHUNK, CHUNK), [((slice(None),), 0, L * A_GROUPS * CHUNK, 0, CHUNK)]),
        ((L, A_GROUPS, CHUNK), [((l,), A_GROUPS * l, A_GROUPS, 0, CHUNK) for l in range(L)]),
        ((L, B_WIDTH), wide(B_WIDTH // 128)), ((L, 64), side_by_side(64)), ((L, C_HEADS), side_by_side(C_HEADS)),
        ((1, D_MODEL), [((slice(None), lanes(j)), j, 1, 0, 128) for j in range(D_MODEL // 128)]),
    ]


def _adamw_small(ws, ms, vs, gearly, glate, late):
    offs = _small_offsets()
    layout = _small_layout()
    n = len(ws)
    assert [w.shape for w in ws] == [shape for shape, _ in layout] and 0 in late

    def body(*refs):
        w_refs, m_refs, v_refs = refs[:n], refs[n:2 * n], refs[2 * n:3 * n]
        early_ref, late_ref = refs[3 * n:3 * n + 2]
        outs = refs[3 * n + 2:]

        def total(k):
            rows = offs[k + 1] - offs[k] if k < n else 1

            def block(dev):
                parts = [early_ref[dev, offs[k] - offs[1]:offs[k] - offs[1] + rows, :]] if k else []
                if k in late:
                    parts.append(late_ref[dev, late[k]:late[k] + rows, :])
                return functools.reduce(jnp.add, parts)

            return functools.reduce(jnp.add, [block(dev) for dev in range(N_DEV)])

        for k, (_, pieces) in enumerate(layout):
            g = total(k)
            go_ref, d_ref, nm_ref, nv_ref = outs[4 * k:4 * k + 4]
            for idx, row, rows, lane, width in pieces:
                gp = g[row:row + rows, :]
                if lane:
                    gp = pltpu.roll(gp, 128 - lane, axis=1)
                gp = gp[:, :width]
                go_ref[idx] = gp
                d_ref[idx], nm_ref[idx], nv_ref[idx] = _adamw_math(w_refs[k][idx], gp, m_refs[k][idx], v_refs[k][idx])
        outs[4 * n][...] = total(n)

    shapes = [SDS(w.shape, F32) for w in ws for _ in range(4)] + [SDS((1, 128), F32)]
    res = pl.pallas_call(body, name="adamw_small", out_shape=shapes,
                         compiler_params=pltpu.CompilerParams(vmem_limit_bytes=VMEM_LIMIT))(*ws, *ms, *vs, gearly, glate)
    return [res[4 * k:4 * k + 4] for k in range(n)], res[4 * n]


def _pack_grads(dlng, dlnb, dwm, dbst, dlb, donorm, dbf, dfinal, loss_part):
    offs = _small_offsets()
    base = offs[1]
    L = len(dwm)
    assert L == 2

    def body(*refs):
        lng, lnb, wm, bst, on, bf = (refs[L * a:L * a + L] for a in range(6))
        lb_ref, fin_ref, loss_ref, o_ref = refs[6 * L:]
        o_ref[...] = jnp.zeros_like(o_ref)
        lane = _lane((1, 128))
        for l in range(L):
            for j in range(2):
                o_ref[offs[1] - base + 2 * l + j:offs[1] - base + 2 * l + j + 1, :] = lng[l][:, 128 * j:128 * j + 128]
                o_ref[offs[2] - base + 2 * l + j:offs[2] - base + 2 * l + j + 1, :] = lnb[l][:, 128 * j:128 * j + 128]
                o_ref[offs[5] - base + 2 * l + j:offs[5] - base + 2 * l + j + 1, :] = lb_ref[l:l + 1, 128 * j:128 * j + 128]
            for g in range(A_GROUPS):
                row = offs[3] - base + (A_GROUPS * l + g) * CHUNK
                o_ref[row:row + CHUNK, :] = wm[l][g]
            o_ref[offs[4] - base + A_GROUPS * l:offs[4] - base + A_GROUPS * (l + 1), :] = bst[l][...].T[0:A_GROUPS, :]
        o_ref[offs[6] - base:offs[6] - base + 1, :] = jnp.where(lane < 64, on[0][...], pltpu.roll(on[1][...], 64, axis=1))
        o_ref[offs[7] - base:offs[7] - base + 1, :] = jnp.where(
            lane < C_HEADS, bf[0][...], jnp.where(lane < 2 * C_HEADS, pltpu.roll(bf[1][...], C_HEADS, axis=1), 0.0))
        for j in range(D_MODEL // 128):
            o_ref[offs[8] - base + j:offs[8] - base + j + 1, :] = fin_ref[:, 128 * j:128 * j + 128]
        o_ref[offs[9] - base:offs[9] - base + 1, :] = loss_ref[...]

    rows = offs[9] + 8 - base
    return pl.pallas_call(body, name="pack_grads", out_shape=SDS((rows, 128), F32))(
        *dlng, *dlnb, *dwm, *dbst, *donorm, *dbf, dlb, dfinal, loss_part)


def _sum_chips(layers, name, layer_major):
    groups = [list(layer) if isinstance(layer, (list, tuple)) else [layer] for layer in layers]
    R = groups[0][0].shape[1]
    C = sum(a.shape[2] for a in groups[0])
    L = len(groups)
    tc = _tile(C, 256)
    steps = C // tc
    plan = []
    for l, layer in enumerate(groups):
        assert all(a.shape[2] % tc == 0 for a in layer) and sum(a.shape[2] for a in layer) == C
        firsts = [sum(a.shape[2] for a in layer[:k]) // tc for k in range(len(layer))]
        plan += [(l, first, a.shape[2] // tc) for first, a in zip(firsts, layer)]

    def body(*refs):
        o_ref = refs[-1]
        i = pl.program_id(0)
        for (l, first, n), p_ref in zip(plan, refs[:-1]):
            def write(l=l, p_ref=p_ref):
                p = [p_ref[k].astype(F32) for k in range(N_CHIPS)]
                s = ((p[0] + p[1]) + p[2]) + p[3]
                if layer_major:
                    o_ref[l] = s
                else:
                    o_ref[:, l, :] = s

            if n == steps:
                write()
            else:
                pl.when((i >= first) & (i < first + n))(write)

    out = (L, R, C) if layer_major else (R, L, C)
    out_blk = (L, R, tc) if layer_major else (R, L, tc)
    return pl.pallas_call(
        body, name=name, grid=(steps,),
        in_specs=[pl.BlockSpec((N_CHIPS, R, tc), lambda i, first=first, n=n: (0, 0, jnp.clip(i - first, 0, n - 1)))
                  for _, first, n in plan],
        out_specs=pl.BlockSpec(out_blk, lambda i: (0, 0, i)), out_shape=SDS(out, F32),
        compiler_params=_params("parallel"),
    )(*[a for layer in groups for a in layer])


ANY = pl.BlockSpec(memory_space=pl.ANY)


def _mesh_pos():
    return lax.axis_index("x"), lax.axis_index("y"), lax.axis_index("c")


def _other_chips(x, y):
    return [(1 - x, y), (x, 1 - y), (1 - x, 1 - y)]


class _ChipExchange:
    def __init__(self, mode, sources):
        assert mode in ("gather", "scatter")
        self.mode, self.sources = mode, tuple(sources)
        self.n = len(self.sources)
        self.in_specs = [ANY] * self.n
        self.out_specs = [ANY] * self.n
        self.out_shape = [SDS(((N_CHIPS,) + s.shape) if mode == "gather" else s.shape, s.dtype) for s in self.sources]
        self.scratch = [pltpu.SemaphoreType.DMA((3 * self.n,)), pltpu.SemaphoreType.DMA((3 * self.n,)),
                        pltpu.SemaphoreType.DMA((self.n,))]

    def _copies(self, srcs, dsts, send_sems, recv_sems, local_sems):
        x, y, c = _mesh_pos()
        me = 2 * x + y
        view = (lambda r, chip: r) if self.mode == "gather" else (lambda r, chip: r.at[chip])
        local = [pltpu.make_async_copy(view(s, me), d.at[me], local_sems.at[a]) for a, (s, d) in enumerate(zip(srcs, dsts))]
        sends, recvs = [], []
        for j, (px, py) in enumerate(_other_chips(x, y)):
            peer = 2 * px + py
            for a, (s, d) in enumerate(zip(srcs, dsts)):
                sems = dict(send_sem=send_sems.at[self.n * j + a], recv_sem=recv_sems.at[self.n * j + a],
                            device_id=(px, py, c), device_id_type=MESH_ID)
                sends.append(pltpu.make_async_remote_copy(src_ref=view(s, peer), dst_ref=d.at[me], **sems))
                recvs.append(pltpu.make_async_remote_copy(src_ref=view(s, me), dst_ref=d.at[peer], **sems))
        return local, sends, recvs

    def start(self, srcs, dsts, sems):
        local, sends, _ = self._copies(srcs, dsts, *sems)
        for cp in local + sends:
            cp.start()

    def wait(self, srcs, dsts, sems):
        local, sends, recvs = self._copies(srcs, dsts, *sems)
        for cp in recvs:
            cp.wait_recv()
        for cp in sends:
            cp.wait_send()
        for cp in local:
            cp.wait()


def _gather_halves(w, tag):
    R, C = w.shape
    H = C // 2

    def body(w_ref, g_ref, send_sems, recv_sems, pass_send, pass_recv, local_sem):
        x, y, c = _mesh_pos()
        me = 2 * x + y
        mine, theirs = pl.ds(pl.multiple_of(c * H, H), H), pl.ds(pl.multiple_of((1 - c) * H, H), H)
        own = pltpu.make_async_copy(w_ref, g_ref.at[me], local_sem)
        own.start()

        def fetch(j, px, py, src, dst):
            return pltpu.make_async_remote_copy(src_ref=src, dst_ref=dst, send_sem=send_sems.at[j], recv_sem=recv_sems.at[j],
                                                device_id=(px, py, c), device_id_type=MESH_ID)

        def hand(j, cols, peer):
            return pltpu.make_async_remote_copy(src_ref=g_ref.at[peer, :, cols], dst_ref=g_ref.at[peer, :, cols],
                                                send_sem=pass_send.at[j], recv_sem=pass_recv.at[j],
                                                device_id=(x, y, 1 - c), device_id_type=MESH_ID)

        chips = _other_chips(x, y)
        sends = [fetch(j, px, py, w_ref.at[:, mine], g_ref.at[me, :, mine]) for j, (px, py) in enumerate(chips)]
        for cp in sends:
            cp.start()
        passed = []
        for j, (px, py) in enumerate(chips):
            peer = 2 * px + py
            fetch(j, px, py, w_ref.at[:, mine], g_ref.at[peer, :, mine]).wait_recv()
            passed.append(hand(j, mine, peer))
            passed[-1].start()
        for j, (px, py) in enumerate(chips):
            hand(j, theirs, 2 * px + py).wait_recv()
        for cp in sends + passed:
            cp.wait_send()
        own.wait()

    return pl.pallas_call(
        body, name=f"gather_halves_{tag}", in_specs=[ANY], out_specs=ANY, out_shape=SDS((N_CHIPS, R, C), w.dtype),
        scratch_shapes=[pltpu.SemaphoreType.DMA((3,)), pltpu.SemaphoreType.DMA((3,)), pltpu.SemaphoreType.DMA((3,)),
                        pltpu.SemaphoreType.DMA((3,)), pltpu.SemaphoreType.DMA],
        compiler_params=pltpu.CompilerParams(has_side_effects=True),
    )(w)


class _DeviceGather:
    def __init__(self, source):
        self.sources, self.n = (source,), 1
        self.in_specs, self.out_specs = [ANY], [ANY]
        self.out_shape = [SDS((N_DEV,) + source.shape, source.dtype)]
        self.scratch = [pltpu.SemaphoreType.DMA((N_DEV - 1,)), pltpu.SemaphoreType.DMA((N_DEV - 1,)),
                        pltpu.SemaphoreType.DMA((1,))]

    def _copies(self, srcs, dsts, send_sems, recv_sems, local_sems):
        (src,), (dst,) = srcs, dsts
        x, y, c = _mesh_pos()
        me = 4 * x + 2 * y + c
        local = [pltpu.make_async_copy(src, dst.at[me], local_sems.at[0])]
        sends, recvs = [], []
        for k in range(1, N_DEV):
            px, py, pc = (1 - x) if k & 4 else x, (1 - y) if k & 2 else y, (1 - c) if k & 1 else c
            sems = dict(send_sem=send_sems.at[k - 1], recv_sem=recv_sems.at[k - 1], device_id=(px, py, pc),
                        device_id_type=MESH_ID)
            sends.append(pltpu.make_async_remote_copy(src_ref=src, dst_ref=dst.at[me], **sems))
            recvs.append(pltpu.make_async_remote_copy(src_ref=src, dst_ref=dst.at[4 * px + 2 * py + pc], **sems))
        return local, sends, recvs

    start = _ChipExchange.start
    wait = _ChipExchange.wait


class _Rides:
    def __init__(self, *rides):
        self.rides = rides
        self.n = sum(r.n for r in rides)
        self.sources = tuple(s for r in rides for s in r.sources)
        self.in_specs, self.out_specs = [ANY] * self.n, [ANY] * self.n
        self.out_shape = [s for r in rides for s in r.out_shape]
        self.scratch = [s for r in rides for s in r.scratch]

    def _each(self, srcs, dsts, sems):
        a = b = 0
        for r in self.rides:
            yield r, srcs[a:a + r.n], dsts[a:a + r.n], sems[b:b + len(r.scratch)]
            a, b = a + r.n, b + len(r.scratch)

    def start(self, srcs, dsts, sems):
        for r, s, d, m in self._each(srcs, dsts, sems):
            r.start(s, d, m)

    def wait(self, srcs, dsts, sems):
        for r, s, d, m in self._each(srcs, dsts, sems):
            r.wait(s, d, m)


def _gather_devices(a, name):
    ex = _DeviceGather(a)

    def body(a_ref, g_ref, *sems):
        ex.start((a_ref,), (g_ref,), sems)
        ex.wait((a_ref,), (g_ref,), sems)

    return pl.pallas_call(
        body, name=name, in_specs=ex.in_specs, out_specs=ex.out_specs[0], out_shape=ex.out_shape[0],
        scratch_shapes=ex.scratch, compiler_params=pltpu.CompilerParams(has_side_effects=True),
    )(a)


def _swap_cores(pin, pout):
    def body(pin_ref, pout_ref, oin_ref, oout_ref, send_sems, recv_sems):
        x, y, c = _mesh_pos()
        cps = [pltpu.make_async_remote_copy(src_ref=src, dst_ref=dst, send_sem=send_sems.at[a], recv_sem=recv_sems.at[a],
                                            device_id=(x, y, 1 - c), device_id_type=MESH_ID)
               for a, (src, dst) in enumerate(((pin_ref, oin_ref), (pout_ref, oout_ref)))]
        for cp in cps:
            cp.start()
        for cp in cps:
            cp.wait()

    return pl.pallas_call(
        body, name="swap_cores", in_specs=[ANY, ANY], out_specs=[ANY, ANY],
        out_shape=[SDS(pin.shape, F32), SDS(pout.shape, F32)],
        scratch_shapes=[pltpu.SemaphoreType.DMA((2,)), pltpu.SemaphoreType.DMA((2,))],
        compiler_params=pltpu.CompilerParams(has_side_effects=True),
    )(pin, pout)


PACK_TILE = 8 * 128


def _pack_rows(size):
    return (size + PACK_TILE - 1) // PACK_TILE * 8


def _small_offsets():
    offs = [0]
    for _, shape in SMALL_PARAMS:
        offs.append(offs[-1] + _pack_rows(math.prod(shape)))
    return offs


def _layer_consts(l, gmlp_ln_g, gmlp_ln_b, gmlp_w_s, gmlp_b_s, hgrn_onorm_g, fox_b_f):
    causal = jnp.tril(jnp.ones((CHUNK, CHUNK), bool))
    wm = jnp.where(causal[None], gmlp_w_s[l], 0.0)
    return dict(
        lng=gmlp_ln_g[l].reshape(1, A_WIDTH), lnb=gmlp_ln_b[l].reshape(1, A_WIDTH),
        wm=wm.astype(BF16), wmt=jnp.swapaxes(wm, 1, 2).astype(BF16),
        bst=jnp.pad(gmlp_b_s[l].T, ((0, 0), (0, 128 - A_GROUPS))),
        onorm=jnp.tile(hgrn_onorm_g[l], 4).reshape(1, B_WIDTH),
        bf=jnp.pad(fox_b_f[l], (0, 128 - C_HEADS)).reshape(1, 128),
    )


def kernel(x, norm_g, w_in, w_out, gmlp_ln_g, gmlp_ln_b, gmlp_w_s, gmlp_b_s, hgrn_lb, hgrn_onorm_g, fox_b_f, final_norm_g, loss_target, m_norm_g, m_w_in, m_w_out, m_gmlp_ln_g, m_gmlp_ln_b, m_gmlp_w_s, m_gmlp_b_s, m_hgrn_lb, m_hgrn_onorm_g, m_fox_b_f, m_final_norm_g, v_norm_g, v_w_in, v_w_out, v_gmlp_ln_g, v_gmlp_ln_b, v_gmlp_w_s, v_gmlp_b_s, v_hgrn_lb, v_hgrn_onorm_g, v_fox_b_f, v_final_norm_g):
    T = x.shape[1]
    shard_in = w_in.shape[2]
    shard_out = w_out.shape[1]
    xs = x.reshape(T, D_MODEL)
    tgt = loss_target.reshape(T, D_MODEL)

    w_in_b = [w_in[l].T.astype(BF16) for l in range(DEPTH)]
    w_out_b = w_out.astype(BF16)

    lb_all = _lb_fwd(hgrn_lb)
    consts = [_layer_consts(l, gmlp_ln_g, gmlp_ln_b, gmlp_w_s, gmlp_b_s, hgrn_onorm_g, fox_b_f) for l in range(DEPTH)]

    saved = []
    xl = xs
    w_in_l = _gather_halves(w_in_b[0], "w_in_l0")
    for l in range(DEPTH):
        cs = consts[l]
        tag = f"l{l}"
        h, proj = _inproj(xl, norm_g[l].reshape(1, D_MODEL), w_in_l, D_IN_PAD, tag)
        (ya,), (yb, ob, s0), (qt, kt, vt) = _run_parts(
            [_gmlp_fwd(proj, cs["lng"], cs["lnb"], cs["wm"], cs["bst"]),
             _hgrn_fwd(proj, lb_all[l].reshape(1, B_WIDTH), cs["onorm"]), _fox_prep(proj, cs["bf"])],
            (T // CHUNK,), f"mix_fwd_{tag}")
        ride = _ChipExchange("gather", (w_out_b[l],) + ((w_in_b[l + 1],) if l + 1 < DEPTH else ()))
        oc, lse, yc, *gathered = _fox_fwd(qt, kt, vt, proj, tag, ride)
        w_out_l = gathered[0].reshape(N_CHIPS * shard_out, D_MODEL)
        saved.append(dict(x=xl, h=h, proj=proj, ya=ya, yb=yb, yc=yc, ob=ob, s0=s0, qt=qt, kt=kt, oc=oc, lse=lse,
                          w_in=w_in_l, w_out=w_out_l))
        if l + 1 < DEPTH:
            xl = _outproj(xl, ya, yb, yc, w_out_l, tag)
            w_in_l = gathered[1]
    dx, loss_part, d_final = _outproj(xl, ya, yb, yc, w_out_l, tag, head=(final_norm_g.reshape(1, D_MODEL), tgt))

    g_small = {}
    dlb_rows, rin, rout = [None] * DEPTH, [None] * DEPTH, [None] * DEPTH
    slabs_in = None
    for l in reversed(range(DEPTH)):
        cs, sv = consts[l], saved[l]
        tag = f"l{l}"
        proj = sv["proj"]
        dy, dw_out, do, delta, dzc, dot, qtr = _outproj_bwd(dx, sv["ya"], sv["yb"], sv["yc"], sv["w_out"], proj, sv["oc"],
                                                            sv["qt"], tag)
        (da, dwm, dbst, dlng, dlnb), (db, dlb_rows[l], donorm) = _run_parts(
            [_gmlp_bwd(proj, dy, cs["lng"], cs["lnb"], cs["wm"], cs["wmt"], cs["bst"]),
             _hgrn_bwd(proj, dy, sv["ob"], sv["s0"], lb_all[l].reshape(1, B_WIDTH), cs["onorm"])],
            (T // CHUNK,), f"mix_bwd_{tag}")
        slabs_out = dw_out.reshape(N_CHIPS, shard_out, D_MODEL).astype(BF16)
        ride = _ChipExchange("scatter", (slabs_out,) + ((slabs_in,) if slabs_in is not None else ()))
        g_small[l] = dict(ln_g=dlng, ln_b=dlnb, w_s=dwm, b_s=dbst, onorm=donorm)
        if l == 0:
            d_hgrn_lb = _lb_bwd(hgrn_lb, jnp.concatenate(dlb_rows, axis=0))
            per_layer = lambda key: [g_small[k][key] for k in range(DEPTH)]
            dbf_known = [jnp.zeros((1, 128), F32)] + [g_small[k]["bf"] for k in range(1, DEPTH)]
            early = _pack_grads(per_layer("ln_g"), per_layer("ln_b"), per_layer("w_s"), per_layer("b_s"), d_hgrn_lb,
                                per_layer("onorm"), dbf_known, d_final, loss_part)
            ride = _Rides(ride, _DeviceGather(early))
        dqt, dkt, dvc, *received = _fox_bwd(sv["qt"], sv["kt"], proj, do, sv["lse"], delta, dot, qtr, tag, ride)
        rout[l] = received[0]
        if slabs_in is not None:
            rin[l + 1] = received[1]
        if l == 0:
            rearly = received[-1]
        dqc, dkc, dflc, g_small[l]["bf"] = _fox_bwd_post(dqt, dkt, proj, cs["bf"], tag)
        dproj = [da, db, dqc, dkc, dvc, dzc, dflc]
        if l == 0:
            ride, parts = None, []
            for n, cols in enumerate(DW_IN_GROUPS):
                part, *arrived = _dw_in(sv["h"], dproj, D_IN_PAD, shard_in, cols, f"{tag}_{n}", ride)
                parts += arrived
                ride = _ChipExchange("scatter", (part,))
        else:
            slabs_in, = _dw_in(sv["h"], dproj, D_IN_PAD, shard_in, (0, D_MODEL), tag)
            ride = None
        dx, dng, *received = _dx_in(sv["x"], norm_g[l].reshape(1, D_MODEL), dx, dproj, sv["w_in"], tag, ride)
        if l == 0:
            rin[0] = parts + received
        g_small[l]["norm_g"] = dng.reshape(D_MODEL // 128, 128)
    grad_x = dx.reshape(x.shape)
    dbf0 = jnp.where(_lane((1, 128)) < C_HEADS, g_small[0]["bf"], 0.0)
    late = jnp.concatenate([g_small[l]["norm_g"] for l in range(DEPTH)] + [jnp.pad(dbf0, ((0, 7), (0, 0)))])
    rlate = _gather_devices(late, "gather_late_grads")
    late_blocks = {0: 0, 7: DEPTH * D_MODEL // 128}

    pin, pout = _sum_chips(rin, "sum_chips_w_in", False), _sum_chips(rout, "sum_chips_w_out", True)
    oin, oout = _swap_cores(pin, pout)
    to_view = lambda a: jnp.transpose(a, (2, 0, 1))
    g_w_in, d_w_in, nm_w_in, nv_w_in = [
        jnp.transpose(o, (1, 2, 0))
        for o in _adamw_pair(to_view(w_in), to_view(m_w_in), to_view(v_w_in), pin, oin, "adamw_w_in")]
    g_w_out, d_w_out, nm_w_out, nv_w_out = _adamw_pair(w_out, m_w_out, v_w_out, pout, oout, "adamw_w_out")

    small_w = [norm_g, gmlp_ln_g, gmlp_ln_b, gmlp_w_s, gmlp_b_s, hgrn_lb, hgrn_onorm_g, fox_b_f, final_norm_g]
    small_m = [m_norm_g, m_gmlp_ln_g, m_gmlp_ln_b, m_gmlp_w_s, m_gmlp_b_s, m_hgrn_lb, m_hgrn_onorm_g, m_fox_b_f, m_final_norm_g]
    small_v = [v_norm_g, v_gmlp_ln_g, v_gmlp_ln_b, v_gmlp_w_s, v_gmlp_b_s, v_hgrn_lb, v_hgrn_onorm_g, v_fox_b_f, v_final_norm_g]
    views = lambda ps: [p.reshape(shape) for p, (shape, _) in zip(ps, _small_layout())]
    per_param, loss_row = _adamw_small(views(small_w), views(small_m), views(small_v), rearly, rlate, late_blocks)
    sg, sd, sm, sv_ = [[per_param[k][a].reshape(shape) for k, (_, shape) in enumerate(SMALL_PARAMS)] for a in range(4)]
    loss = loss_row[0, 0]

    def order(big_in, big_out, small):
        return [small[0], big_in, big_out] + small[1:]

    return (loss, grad_x, *order(g_w_in, g_w_out, sg), *order(d_w_in, d_w_out, sd), *order(nm_w_in, nm_w_out, sm),
            *order(nv_w_in, nv_w_out, sv_))
```

```python
import collections
import functools
import math

import jax
import jax.numpy as jnp
from jax import lax
from jax.experimental import pallas as pl
from jax.experimental.pallas import tpu as pltpu

F32 = jnp.float32
BF16 = jnp.bfloat16
SDS = jax.ShapeDtypeStruct
MESH_ID = pl.DeviceIdType.MESH

D_MODEL = 1024
DEPTH = 2
A_WIDTH = 256
A_GROUPS = 4
B_WIDTH = 256
C_WIDTH = 512
C_HEADS = 8
D_IN = 3848
D_IN_PAD = 4096
CHUNK = 128
SUB = 16
SUB_SHIFT = 4
NORM_EPS = 1e-6
F_FLOOR = 1e-30
COL_AU, COL_AV, COL_AZ = 0, 256, 512
COL_BQ, COL_BF, COL_BI, COL_BZ = 768, 1024, 1280, 1536
COL_CQ, COL_CK, COL_CV, COL_CZ, COL_CF = 1792, 2304, 2816, 3328, 3840
HEAD_LANES = 128
Q_SCALE = 0.125
ADAM_LR, ADAM_B1, ADAM_B2, ADAM_EPS, ADAM_WD, ADAM_STEP = 0.001, 0.9, 0.999, 1e-08, 0.01, 10
ADAM_C1 = 1.0 - ADAM_B1 ** ADAM_STEP
ADAM_C2 = 1.0 - ADAM_B2 ** ADAM_STEP
VMEM_LIMIT = 56 * 1024 * 1024
ADAMW_BLOCK_BYTES = 1 << 20
N_CHIPS = 4
N_DEV = 8

SMALL_PARAMS = (
    ("norm_g", (DEPTH, D_MODEL)), ("gmlp_ln_g", (DEPTH, 4, 64)), ("gmlp_ln_b", (DEPTH, 4, 64)),
    ("gmlp_w_s", (DEPTH, 4, 128, 128)), ("gmlp_b_s", (DEPTH, 4, 128)), ("hgrn_lb", (DEPTH, 256)),
    ("hgrn_onorm_g", (DEPTH, 64)), ("fox_b_f", (DEPTH, 8)), ("final_norm_g", (D_MODEL,)),
)


def _tile(n, pref):
    t = min(n, pref)
    assert n % t == 0, (n, pref)
    return t


def _params(*sem):
    return pltpu.CompilerParams(dimension_semantics=sem, vmem_limit_bytes=VMEM_LIMIT)


_Part = collections.namedtuple("_Part", "body operands in_specs out_specs out_shape scratch")


def _run_parts(parts, grid, name):
    counts = [(len(p.operands), len(p.out_shape), len(p.scratch)) for p in parts]

    def body(*refs):
        ins, outs, scr = [], [], []
        pos = 0
        for group, k in ((ins, 0), (outs, 1), (scr, 2)):
            for c in counts:
                group.append(refs[pos:pos + c[k]])
                pos += c[k]
        for p, i, o, s in zip(parts, ins, outs, scr):
            p.body(*i, *o, *s)

    flat = lambda key: [x for p in parts for x in getattr(p, key)]
    res = pl.pallas_call(
        body, name=name, grid=grid, in_specs=flat("in_specs"), out_specs=flat("out_specs"), out_shape=flat("out_shape"),
        scratch_shapes=flat("scratch"), compiler_params=_params(*(("arbitrary",) * len(grid))),
    )(*flat("operands"))
    out, pos = [], 0
    for c in counts:
        out.append(list(res[pos:pos + c[1]]))
        pos += c[1]
    return out


def _dot(a, b):
    return jnp.dot(a, b, preferred_element_type=F32)


def _dot_nt(a, b):
    return lax.dot_general(a, b, (((1,), (1,)), ((), ())), preferred_element_type=F32)


def _dot_tn(a, b):
    return lax.dot_general(a, b, (((0,), (0,)), ((), ())), preferred_element_type=F32)


def _split3(x):
    hi = x.astype(BF16)
    r = x - hi.astype(F32)
    mid = r.astype(BF16)
    lo = (r - mid.astype(F32)).astype(BF16)
    return hi, mid, lo


def _dot3_left(c, x):
    hi, mid, lo = _split3(x)
    return _dot(c, hi) + _dot(c, mid) + _dot(c, lo)


def _sigmoid(x):
    return jax.nn.sigmoid(x)


def _silu_and_grad(x):
    s = _sigmoid(x)
    return x * s, s * (1.0 + x * (1.0 - s))


_GELU_C = math.sqrt(2.0 / math.pi)


def _gelu_and_grad(x):
    inner = _GELU_C * (x + 0.044715 * x * x * x)
    t = jnp.tanh(inner)
    y = 0.5 * x * (1.0 + t)
    dy = 0.5 * (1.0 + t) + 0.5 * x * (1.0 - t * t) * _GELU_C * (1.0 + 3.0 * 0.044715 * x * x)
    return y, dy


def _lane(shape):
    return lax.broadcasted_iota(jnp.int32, shape, 1)


def _row(shape):
    return lax.broadcasted_iota(jnp.int32, shape, 0)


def _gsum64(x):
    lo = _lane(x.shape) < 64
    s0 = jnp.sum(jnp.where(lo, x, 0.0), axis=-1, keepdims=True)
    s1 = jnp.sum(jnp.where(lo, 0.0, x), axis=-1, keepdims=True)
    return jnp.where(lo, s0, s1)


def _colreduce(x, op):
    parts = [x[r:r + 8, :] for r in range(0, x.shape[0], 8)]
    while len(parts) > 1:
        pairs = [op(parts[k], parts[k + 1]) for k in range(0, len(parts) - 1, 2)]
        parts = pairs + ([parts[-1]] if len(parts) % 2 else [])
    red = jnp.max if op is jnp.maximum else jnp.sum
    return red(parts[0], axis=0, keepdims=True)


def _block_diag64(dtype=BF16):
    r, c = _row((128, 128)), _lane((128, 128))
    return jnp.where((r >> 6) == (c >> 6), 1.0, 0.0).astype(dtype)


def _assemble_w_in(slab_ref, wt_ref):
    shard = slab_ref.shape[1]
    top = N_CHIPS * shard // 16 * 16
    wt_ref[top:, :] = jnp.zeros((wt_ref.shape[0] - top, wt_ref.shape[1]), wt_ref.dtype)
    for k in range(N_CHIPS):
        wt_ref[shard * k:shard * (k + 1), :] = slab_ref[k]


def _residual(x_ref, ya_ref, yb_ref, yc_ref, wo_ref):
    acc = x_ref[...] + _dot(ya_ref[...], wo_ref[0:A_WIDTH, :])
    acc = acc + _dot(yb_ref[...], wo_ref[A_WIDTH:A_WIDTH + B_WIDTH, :])
    return acc + _dot(yc_ref[...], wo_ref[A_WIDTH + B_WIDTH:, :])


def _inproj(x, g, w, dp_width, tag, below=None):
    T, D = x.shape
    tm = _tile(T, 512)
    nb = len(below or ())

    def body(x_ref, g_ref, w_ref, *rest):
        h_ref, p_ref = rest[nb:nb + 2]
        wt_ref = rest[-1]
        pl.when(pl.program_id(0) == 0)(lambda: _assemble_w_in(w_ref, wt_ref))
        if below:
            xv = _residual(x_ref, *rest[:nb])
            rest[nb + 2][...] = xv
        else:
            xv = x_ref[...]
        r = lax.rsqrt(jnp.mean(xv * xv, axis=-1, keepdims=True) + NORM_EPS)
        h = (xv * r * g_ref[...]).astype(BF16)
        h_ref[...] = h
        p_ref[...] = _dot_nt(h, wt_ref[...])

    row = lambda width: pl.BlockSpec((tm, width), lambda i: (i, 0))
    below_specs = [row(A_WIDTH), row(B_WIDTH), row(C_WIDTH), pl.BlockSpec(below[3].shape, lambda i: (0, 0))] if below else []
    return pl.pallas_call(
        body, name=f"inproj_{tag}", grid=(T // tm,),
        in_specs=[row(D), pl.BlockSpec((1, D), lambda i: (0, 0)),
                  pl.BlockSpec(w.shape, lambda i: (0, 0, 0), pipeline_mode=pl.Buffered(1))] + below_specs,
        out_specs=[row(D), row(dp_width)] + ([row(D)] if below else []),
        out_shape=[SDS((T, D), BF16), SDS((T, dp_width), F32)] + ([SDS((T, D), F32)] if below else []),
        scratch_shapes=[pltpu.VMEM((dp_width, D), BF16)],
        compiler_params=_params("arbitrary"),
    )(x, g, w, *(below or ()))


def _outproj(x, ya, yb, yc, wo, tag, head=None):
    T, D = x.shape
    tm = _tile(T, 512)

    def body(x_ref, ya_ref, yb_ref, yc_ref, wo_ref, o_ref):
        o_ref[...] = _residual(x_ref, ya_ref, yb_ref, yc_ref, wo_ref)

    def body_head(x_ref, ya_ref, yb_ref, yc_ref, wo_ref, g_ref, t_ref, dx_ref, loss_ref, dg_ref):
        @pl.when(pl.program_id(0) == 0)
        def _():
            loss_ref[...] = jnp.zeros_like(loss_ref)
            dg_ref[...] = jnp.zeros_like(dg_ref)

        xv = _residual(x_ref, ya_ref, yb_ref, yc_ref, wo_ref)
        r = lax.rsqrt(jnp.mean(xv * xv, axis=-1, keepdims=True) + NORM_EPS)
        xh = xv * r
        gv = g_ref[...]
        err = xh * gv - t_ref[...]
        tok = jnp.mean(err * err, axis=-1, keepdims=True)
        loss_ref[...] += 0.5 * jnp.sum(tok, axis=0, keepdims=True)
        dy = err * (1.0 / D)
        dg_ref[...] += jnp.sum(dy * xh, axis=0, keepdims=True)
        dxh = dy * gv
        dx_ref[...] = r * (dxh - xh * jnp.mean(dxh * xh, axis=-1, keepdims=True))

    row = lambda w: pl.BlockSpec((tm, w), lambda i: (i, 0))
    fixed = lambda shape: pl.BlockSpec(shape, lambda i: (0, 0))
    in_specs = [row(D), row(A_WIDTH), row(B_WIDTH), row(C_WIDTH), fixed(wo.shape)]
    if head is None:
        return pl.pallas_call(
            body, name=f"outproj_{tag}", grid=(T // tm,), in_specs=in_specs,
            out_specs=row(D), out_shape=SDS((T, D), F32), compiler_params=_params("parallel"),
        )(x, ya, yb, yc, wo)
    return pl.pallas_call(
        body_head, name=f"outproj_loss_{tag}", grid=(T // tm,), in_specs=in_specs + [fixed((1, D)), row(D)],
        out_specs=[row(D), fixed((1, 128)), fixed((1, D))],
        out_shape=[SDS((T, D), F32), SDS((1, 128), F32), SDS((1, D), F32)], compiler_params=_params("arbitrary"),
    )(x, ya, yb, yc, wo, *head)


def _outproj_bwd(dx, ya, yb, yc, wo, proj, o, qt, tag):
    T, D = dx.shape
    DM = wo.shape[0]
    AB = A_WIDTH + B_WIDTH
    tq = _tile(T, FOX_TILE)
    nq = T // tq

    def body(dx_ref, ya_ref, yb_ref, yc_ref, wo_ref, z0_ref, z1_ref, o_ref, q_ref,
             dy_ref, dwo_ref, do_ref, dl_ref, dz_ref, dot_ref, qt_ref):
        @pl.when(pl.program_id(0) == 0)
        def _():
            dwo_ref[...] = jnp.zeros_like(dwo_ref)

        dxb = dx_ref[...].astype(BF16)
        dy = _dot_nt(dxb, wo_ref[...])
        dy_ref[...] = dy[:, 0:AB]
        dwo_ref[0:A_WIDTH, :] += _dot_tn(ya_ref[...], dxb)
        dwo_ref[A_WIDTH:AB, :] += _dot_tn(yb_ref[...], dxb)
        dwo_ref[AB:, :] += _dot_tn(yc_ref[...], dxb)

        sel = jnp.where((_lane((16, 128)) >> 6) == _row((16, 128)), 1.0, 0.0).astype(BF16)
        for p, z_ref in enumerate((z0_ref, z0_ref, z1_ref, z1_ref)):
            sl = slice(128 * p, 128 * p + 128)
            sz, dsz = _silu_and_grad(z_ref[:, 128 * (p % 2):128 * (p % 2) + 128])
            dyv, ov = dy[:, AB + 128 * p:AB + 128 * p + 128], o_ref[:, sl]
            do = dyv * sz
            do_ref[:, sl] = do.astype(BF16)
            dot_ref[p, 0] = do.T.astype(BF16)
            dz_ref[:, sl] = (dyv * ov * dsz).astype(BF16)
            hi, mid, lo = _split3(do * ov)
            dl_ref[p, 0] = (_dot_nt(sel, hi) + _dot_nt(sel, mid) + _dot_nt(sel, lo))[0:8, :]
        for h in range(C_HEADS):
            qt_ref[h, 0] = q_ref[:, 128 * h:128 * h + 128].astype(F32).T.astype(BF16)

    w = 256
    row = lambda width: pl.BlockSpec((tq, width), lambda i: (i, 0))
    return pl.pallas_call(
        body, name=f"outproj_bwd_{tag}", grid=(nq,),
        in_specs=[row(D), row(A_WIDTH), row(B_WIDTH), row(C_WIDTH), pl.BlockSpec(wo.shape, lambda i: (0, 0)),
                  pl.BlockSpec((tq, w), lambda i: (i, COL_CZ // w)), pl.BlockSpec((tq, w), lambda i: (i, COL_CZ // w + 1)),
                  row(C_WIDTH), row(C_HEADS * 128)],
        out_specs=[row(AB), pl.BlockSpec((DM, D), lambda i: (0, 0)), row(C_WIDTH),
                   pl.BlockSpec((C_HEADS // 2, 1, 8, tq), lambda i: (0, i, 0, 0)), row(C_WIDTH),
                   pl.BlockSpec((C_HEADS // 2, 1, 128, tq), lambda i: (0, i, 0, 0)),
                   pl.BlockSpec((C_HEADS, 1, 128, tq), lambda i: (0, i, 0, 0))],
        out_shape=[SDS((T, AB), F32), SDS((DM, D), F32), SDS((T, C_WIDTH), BF16), SDS((C_HEADS // 2, nq, 8, tq), F32),
                   SDS((T, C_WIDTH), BF16), SDS((C_HEADS // 2, nq, 128, tq), BF16), SDS((C_HEADS, nq, 128, tq), BF16)],
        compiler_params=_params("arbitrary"),
    )(dx, ya, yb, yc, wo, proj, proj, o, qt)


DW_IN_GROUPS = ((0, 256), (256, 256), (512, 512))


def _piece_offsets(pieces):
    offs = [0]
    for p in pieces:
        offs.append(offs[-1] + p.shape[1])
    return offs


def _dw_in(h, pieces, dp_width, shard, cols, tag, ride=None):
    T = h.shape[0]
    first, D = cols
    assert N_CHIPS * shard <= dp_width and first % D == 0
    tm = _tile(T, 512)
    grid = (T // tm,)
    offs = _piece_offsets(pieces)
    n = len(pieces)

    def body(h_ref, *rest):
        p_refs, rest = rest[:n], rest[n:]
        ride_srcs, (dw_ref,), ride_dsts, (acc_ref,), ride_sems = _ride_refs(ride, rest, 1, 1)
        i = pl.program_id(0)
        _ride_start(ride, grid, ride_srcs, ride_dsts, ride_sems)

        @pl.when(i == 0)
        def _():
            acc_ref[...] = jnp.zeros_like(acc_ref)

        hv = h_ref[...]
        for k, p_ref in enumerate(p_refs):
            acc_ref[offs[k]:offs[k + 1], :] += _dot_tn(p_ref[...], hv)

        @pl.when(i == grid[0] - 1)
        def _():
            for k in range(N_CHIPS):
                dw_ref[k] = acc_ref[shard * k:shard * (k + 1), :].astype(BF16)

        _ride_wait(ride, grid, ride_srcs, ride_dsts, ride_sems)

    extra = ride or _ChipExchange("gather", ())
    return pl.pallas_call(
        body, name=f"dw_in_{tag}", grid=grid,
        in_specs=[pl.BlockSpec((tm, D), lambda i: (i, first // D))]
        + [pl.BlockSpec((tm, p.shape[1]), lambda i: (i, 0)) for p in pieces] + extra.in_specs,
        out_specs=[pl.BlockSpec((N_CHIPS, shard, D), lambda i: (0, 0, 0))] + extra.out_specs,
        out_shape=[SDS((N_CHIPS, shard, D), BF16)] + extra.out_shape,
        scratch_shapes=[pltpu.VMEM((dp_width, D), F32)] + (extra.scratch if ride else []),
        compiler_params=pltpu.CompilerParams(dimension_semantics=("arbitrary",), vmem_limit_bytes=VMEM_LIMIT,
                                             has_side_effects=bool(ride)),
    )(h, *pieces, *extra.sources)


def _dx_in(x, g, dres, pieces, w, tag, ride=None):
    T, D = x.shape
    tm = _tile(T, 512)
    grid = (T // tm,)
    offs = _piece_offsets(pieces)
    n = len(pieces)

    def body(x_ref, g_ref, dres_ref, w_ref, *rest):
        p_refs, rest = rest[:n], rest[n:]
        ride_srcs, (dx_ref, dg_ref), ride_dsts, (wt_ref,), ride_sems = _ride_refs(ride, rest, 2, 1)
        _ride_start(ride, grid, ride_srcs, ride_dsts, ride_sems)

        @pl.when(pl.program_id(0) == 0)
        def _():
            dg_ref[...] = jnp.zeros_like(dg_ref)
            _assemble_w_in(w_ref, wt_ref)

        dh = _dot(p_refs[0][...], wt_ref[offs[0]:offs[1], :])
        for k in range(1, n):
            dh = dh + _dot(p_refs[k][...], wt_ref[offs[k]:offs[k + 1], :])
        xv = x_ref[...]
        r = lax.rsqrt(jnp.mean(xv * xv, axis=-1, keepdims=True) + NORM_EPS)
        xh = xv * r
        dg_ref[...] += jnp.sum(dh * xh, axis=0, keepdims=True)
        dxh = dh * g_ref[...]
        dx_ref[...] = dres_ref[...] + r * (dxh - xh * jnp.mean(dxh * xh, axis=-1, keepdims=True))
        _ride_wait(ride, grid, ride_srcs, ride_dsts, ride_sems)

    extra = ride or _ChipExchange("gather", ())
    row = pl.BlockSpec((tm, D), lambda i: (i, 0))
    return pl.pallas_call(
        body, name=f"dx_in_{tag}", grid=grid,
        in_specs=[row, pl.BlockSpec((1, D), lambda i: (0, 0)), row, pl.BlockSpec(w.shape, lambda i: (0, 0, 0))]
        + [pl.BlockSpec((tm, p.shape[1]), lambda i: (i, 0)) for p in pieces] + extra.in_specs,
        out_specs=[row, pl.BlockSpec((1, D), lambda i: (0, 0))] + extra.out_specs,
        out_shape=[SDS((T, D), F32), SDS((1, D), F32)] + extra.out_shape,
        scratch_shapes=[pltpu.VMEM((offs[-1], D), BF16)] + (extra.scratch if ride else []),
        compiler_params=pltpu.CompilerParams(dimension_semantics=("arbitrary",), vmem_limit_bytes=VMEM_LIMIT,
                                             has_side_effects=bool(ride)),
    )(x, g, dres, w, *pieces, *extra.sources)


def _gmlp_core(u, v, lng, lnb, wm_ref, bst_ref, pair):
    ug, dug = _gelu_and_grad(u)
    vg, dvg = _gelu_and_grad(v)
    mu = _gsum64(vg) * (1.0 / 64)
    d = vg - mu
    var = _gsum64(d * d) * (1.0 / 64)
    rstd = lax.rsqrt(var + NORM_EPS)
    xh = d * rstd
    vn = xh * lng + lnb
    vnb = vn.astype(BF16)
    lo = _lane(u.shape) < 64
    g0, g1 = 2 * pair, 2 * pair + 1
    mixed = jnp.where(lo, _dot(wm_ref[g0], vnb) + bst_ref[:, g0:g0 + 1], _dot(wm_ref[g1], vnb) + bst_ref[:, g1:g1 + 1])
    return ug, dug, dvg, rstd, xh, vnb, mixed, lo


def _gmlp_fwd(proj, lng, lnb, wm, bst):
    T = proj.shape[0]

    def body(u_ref, v_ref, z_ref, lng_ref, lnb_ref, wm_ref, bst_ref, y_ref):
        for pair in range(2):
            sl = slice(128 * pair, 128 * pair + 128)
            ug, _, _, _, _, _, mixed, _ = _gmlp_core(u_ref[:, sl], v_ref[:, sl], lng_ref[:, sl], lnb_ref[:, sl],
                                                     wm_ref, bst_ref, pair)
            sz, _ = _silu_and_grad(z_ref[:, sl])
            y_ref[:, sl] = (ug * mixed * sz).astype(BF16)

    col = lambda c: pl.BlockSpec((CHUNK, A_WIDTH), lambda i, c=c: (i, c // A_WIDTH))
    full = lambda a: pl.BlockSpec(a.shape, lambda i, n=a.ndim: (0,) * n)
    return _Part(body, (proj, proj, proj, lng, lnb, wm, bst),
                 [col(COL_AU), col(COL_AV), col(COL_AZ), full(lng), full(lnb), full(wm), full(bst)],
                 [pl.BlockSpec((CHUNK, A_WIDTH), lambda i: (i, 0))], [SDS((T, A_WIDTH), BF16)], [])


def _gmlp_bwd(proj, dy, lng, lnb, wm, wmt, bst):
    T = proj.shape[0]
    n = T // CHUNK

    def body(u_ref, v_ref, z_ref, dy_ref, lng_ref, lnb_ref, wm_ref, wmt_ref, bst_ref,
             da_ref, dwm_ref, dbst_ref, dlng_ref, dlnb_ref):
        @pl.when(pl.program_id(0) == 0)
        def _():
            dwm_ref[...] = jnp.zeros_like(dwm_ref)
            dbst_ref[...] = jnp.zeros_like(dbst_ref)
            dlng_ref[...] = jnp.zeros_like(dlng_ref)
            dlnb_ref[...] = jnp.zeros_like(dlnb_ref)

        lane = _lane((CHUNK, 128))
        dbst = dbst_ref[...]
        for pair in range(2):
            sl = slice(128 * pair, 128 * pair + 128)
            lng_p = lng_ref[:, sl]
            ug, dug, dvg, rstd, xh, vnb, mixed, lo = _gmlp_core(u_ref[:, sl], v_ref[:, sl], lng_p, lnb_ref[:, sl],
                                                                wm_ref, bst_ref, pair)
            sz, dsz = _silu_and_grad(z_ref[:, sl])
            dyv = dy_ref[:, sl]
            out = ug * mixed
            dz = dyv * out * dsz
            dout = dyv * sz
            du = dout * mixed * dug
            dmix = dout * ug
            g0, g1 = 2 * pair, 2 * pair + 1
            dm0 = jnp.where(lo, dmix, 0.0)
            dm1 = jnp.where(lo, 0.0, dmix)
            dbst = dbst + jnp.where(lane == g0, jnp.sum(dm0, axis=-1, keepdims=True), 0.0)
            dbst = dbst + jnp.where(lane == g1, jnp.sum(dm1, axis=-1, keepdims=True), 0.0)
            dwm_ref[g0] += _dot_nt(dm0.astype(BF16), vnb)
            dwm_ref[g1] += _dot_nt(dm1.astype(BF16), vnb)
            dmb = dmix.astype(BF16)
            dvn = jnp.where(lo, _dot(wmt_ref[g0], dmb), _dot(wmt_ref[g1], dmb))
            dlng_ref[:, sl] += jnp.sum(dvn * xh, axis=0, keepdims=True)
            dlnb_ref[:, sl] += jnp.sum(dvn, axis=0, keepdims=True)
            dxh = dvn * lng_p
            m1 = _gsum64(dxh) * (1.0 / 64)
            m2 = _gsum64(dxh * xh) * (1.0 / 64)
            dv = rstd * (dxh - m1 - xh * m2) * dvg
            da_ref[:, COL_AU + 128 * pair:COL_AU + 128 * pair + 128] = du.astype(BF16)
            da_ref[:, COL_AV + 128 * pair:COL_AV + 128 * pair + 128] = dv.astype(BF16)
            da_ref[:, COL_AZ + 128 * pair:COL_AZ + 128 * pair + 128] = dz.astype(BF16)
        dbst_ref[...] = dbst

        @pl.when(pl.program_id(0) == n - 1)
        def _():
            causal = _lane((CHUNK, CHUNK)) <= _row((CHUNK, CHUNK))
            for g in range(A_GROUPS):
                dwm_ref[g] = jnp.where(causal, dwm_ref[g], 0.0)

    col = lambda c: pl.BlockSpec((CHUNK, A_WIDTH), lambda i, c=c: (i, c // A_WIDTH))
    full = lambda a: pl.BlockSpec(a.shape, lambda i, n=a.ndim: (0,) * n)
    acc = lambda s: pl.BlockSpec(s, lambda i, n=len(s): (0,) * n)
    return _Part(body, (proj, proj, proj, dy, lng, lnb, wm, wmt, bst),
                 [col(COL_AU), col(COL_AV), col(COL_AZ), pl.BlockSpec((CHUNK, A_WIDTH), lambda i: (i, 0)),
                  full(lng), full(lnb), full(wm), full(wmt), full(bst)],
                 [pl.BlockSpec((CHUNK, 3 * A_WIDTH), lambda i: (i, 0)), acc((A_GROUPS, CHUNK, CHUNK)),
                  acc((CHUNK, 128)), acc((1, A_WIDTH)), acc((1, A_WIDTH))],
                 [SDS((T, 3 * A_WIDTH), BF16), SDS((A_GROUPS, CHUNK, CHUNK), F32), SDS((CHUNK, 128), F32),
                  SDS((1, A_WIDTH), F32), SDS((1, A_WIDTH), F32)], [])


def _hgrn_consts():
    r, c = _row((CHUNK, CHUNK)), _lane((CHUNK, CHUNK))
    same = (r >> SUB_SHIFT) == (c >> SUB_SHIFT)
    lsub = jnp.where(same & (c <= r), 1.0, 0.0).astype(BF16)
    usub = jnp.where(same & (c >= r), 1.0, 0.0).astype(BF16)
    bsub = jnp.where(same, 1.0, 0.0).astype(BF16)
    return lsub, usub, bsub


def _hgrn_gates(qv, zf, lbp):
    sq, dsq = _silu_and_grad(qv)
    qt = sq * Q_SCALE
    sg = _sigmoid(zf)
    sgn = _sigmoid(-zf)
    f = lbp + (1.0 - lbp) * sg
    g = jnp.log(jnp.maximum(f, F_FLOOR))
    kf = (1.0 - lbp) * sgn
    return qt, dsq, sg, sgn, f, g, kf


def _hgrn_intra_scores(qt, kf, b, mbd):
    rid = _row((SUB, 128))
    parts = []
    for s in range(SUB):
        e = jnp.exp(b - b[s:s + 1, :])
        parts.append(jnp.where(rid >= s, qt * kf[s:s + 1, :] * e, 0.0))
    return _dot(jnp.concatenate(parts, axis=0).astype(BF16), mbd)


def _hgrn_intra_out(a, v):
    o = jnp.zeros((SUB, 128), F32)
    for s in range(SUB):
        o = o + a[SUB * s:SUB * s + SUB, :] * v[s:s + 1, :]
    return o


def _hgrn_intra_bwd_scores(qt, kf, b, v, do, mbd):
    rid = _row((SUB, 128))
    ps, das, kes, es = [], [], [], []
    for s in range(SUB):
        e = jnp.where(rid >= s, jnp.exp(b - b[s:s + 1, :]), 0.0)
        ke = kf[s:s + 1, :] * e
        es.append(e)
        kes.append(ke)
        ps.append(qt * ke)
        das.append(do * v[s:s + 1, :])
    a = _dot(jnp.concatenate(ps, axis=0).astype(BF16), mbd)
    da = _dot(jnp.concatenate(das, axis=0).astype(BF16), mbd)
    return a, da, kes, es


def _hgrn_intra_bwd_grads(scores, qt, do, rsum):
    a, da, kes, es = scores
    dqt = jnp.zeros((SUB, 128), F32)
    xs, ys = [], []
    for s in range(SUB):
        da_s = da[SUB * s:SUB * s + SUB, :]
        dqt = dqt + da_s * kes[s]
        xs.append(a[SUB * s:SUB * s + SUB, :] * do)
        ys.append(da_s * qt * es[s])
    dv = _dot(rsum, jnp.concatenate(xs, axis=0).astype(BF16))
    dkf = _dot(rsum, jnp.concatenate(ys, axis=0).astype(BF16))
    return dqt, dkf, dv


def _hgrn_norm_gate(o, z, onorm):
    ms = _gsum64(o * o) * (1.0 / 64)
    r = lax.rsqrt(ms + NORM_EPS)
    xh = o * r
    sz, dsz = _silu_and_grad(z)
    return xh, r, sz, dsz, xh * onorm


def _hgrn_fwd(proj, lb, onorm):
    T = proj.shape[0]
    n = T // CHUNK
    nsub = CHUNK // SUB

    def body(q_ref, f_ref, i_ref, z_ref, lb_ref, on_ref, y_ref, o_ref, s0_ref, st_ref):
        @pl.when(pl.program_id(0) == 0)
        def _():
            st_ref[...] = jnp.zeros_like(st_ref)

        lsub, _, bsub = _hgrn_consts()
        mbd = _block_diag64()
        bdmask = mbd > 0
        rid = _row((CHUNK, 128))
        subs = [slice(SUB * sub, SUB * sub + SUB) for sub in range(nsub)]
        work = []
        for pair in range(2):
            sl = slice(128 * pair, 128 * pair + 128)
            qt, _, _, _, _, g, kf = _hgrn_gates(q_ref[:, sl], f_ref[:, sl], lb_ref[:, sl])
            work.append(dict(sl=sl, qt=qt, kf=kf, v=i_ref[:, sl], b=_dot3_left(lsub, g), bl=_dot3_left(bsub, g)))
        for w in work:
            qt, kf, v, b, bl = w["qt"], w["kf"], w["v"], w["b"], w["bl"]
            w["qh"] = (qt * jnp.exp(b)).astype(BF16)
            kh = kf * jnp.exp(bl - b)
            w["dec"] = jnp.exp(bl)
            vtb = v.T.astype(BF16)
            w["scores"] = [_hgrn_intra_scores(qt[rs], kf[rs], b[rs], mbd) for rs in subs]
            w["adds"] = [_dot(vtb, jnp.where((rid >> SUB_SHIFT) == sub, kh, 0.0).astype(BF16)) for sub in range(nsub)]
        for pair, w in enumerate(work):
            w["st"] = st_ref[pair]
            s0_ref[0, pair] = w["st"]
            w["outs"] = []
        for sub, rs in enumerate(subs):
            for w in work:
                w["outs"].append(_dot_nt(w["qh"][rs], w["st"].astype(BF16)) + _hgrn_intra_out(w["scores"][sub], w["v"][rs]))
                w["st"] = jnp.where(bdmask, w["st"] * w["dec"][SUB * sub:SUB * sub + 1, :] + w["adds"][sub], 0.0)
        for pair, w in enumerate(work):
            sl = w["sl"]
            st_ref[pair] = w["st"]
            o = jnp.concatenate(w["outs"], axis=0)
            o_ref[:, sl] = o
            _, _, sz, _, on = _hgrn_norm_gate(o, z_ref[:, sl], on_ref[:, sl])
            y_ref[:, sl] = (on * sz).astype(BF16)

    col = lambda c: pl.BlockSpec((CHUNK, B_WIDTH), lambda i, c=c: (i, c // B_WIDTH))
    full = lambda a: pl.BlockSpec(a.shape, lambda i, n=a.ndim: (0,) * n)
    return _Part(body, (proj, proj, proj, proj, lb, onorm),
                 [col(COL_BQ), col(COL_BF), col(COL_BI), col(COL_BZ), full(lb), full(onorm)],
                 [pl.BlockSpec((CHUNK, B_WIDTH), lambda i: (i, 0)), pl.BlockSpec((CHUNK, B_WIDTH), lambda i: (i, 0)),
                  pl.BlockSpec((1, 2, 128, 128), lambda i: (i, 0, 0, 0))],
                 [SDS((T, B_WIDTH), BF16), SDS((T, B_WIDTH), F32), SDS((n, 2, 128, 128), F32)],
                 [pltpu.VMEM((2, 128, 128), F32)])


def _hgrn_bwd(proj, dy, o_saved, s0, lb, onorm):
    T = proj.shape[0]
    n = T // CHUNK
    nsub = CHUNK // SUB

    def body(q_ref, f_ref, i_ref, z_ref, dy_ref, o_ref, s0_ref, lb_ref, on_ref,
             db_ref, dlb_ref, don_ref, dst_ref, sts_ref):
        @pl.when(pl.program_id(0) == 0)
        def _():
            dst_ref[...] = jnp.zeros_like(dst_ref)
            dlb_ref[...] = jnp.zeros_like(dlb_ref)
            don_ref[...] = jnp.zeros_like(don_ref)

        lsub, usub, bsub = _hgrn_consts()
        mbd = _block_diag64()
        bdmask = mbd > 0
        rsum = jnp.where((_lane((SUB, SUB * SUB)) >> SUB_SHIFT) == _row((SUB, SUB * SUB)), 1.0, 0.0).astype(BF16)
        subs = [slice(SUB * sub, SUB * sub + SUB) for sub in range(nsub)]
        work = []
        for pair in range(2):
            sl = slice(128 * pair, 128 * pair + 128)
            lbp = lb_ref[:, sl]
            qt, dsq, sg, sgn, f, g, kf = _hgrn_gates(q_ref[:, sl], f_ref[:, sl], lbp)
            w = dict(sl=sl, lbp=lbp, qt=qt, dsq=dsq, sg=sg, sgn=sgn, f=f, kf=kf, v=i_ref[:, sl],
                     b=_dot3_left(lsub, g), bl=_dot3_left(bsub, g))
            onp = on_ref[:, sl]
            xh, r, sz, dsz, on = _hgrn_norm_gate(o_ref[:, sl], z_ref[:, sl], onp)
            dyv = dy_ref[:, sl]
            w["dz"] = dyv * on * dsz
            don = dyv * sz
            cn = jnp.sum(don * xh, axis=0, keepdims=True)
            don_ref[...] += cn + pltpu.roll(cn, 64, axis=1)
            dxo = don * onp
            w["do"] = r * (dxo - xh * (_gsum64(dxo * xh) * (1.0 / 64)))
            work.append(w)
        for w in work:
            qt, kf, v, b, bl, do = w["qt"], w["kf"], w["v"], w["b"], w["bl"], w["do"]
            w["eb"] = jnp.exp(b)
            w["ekb"] = jnp.exp(bl - b)
            w["qhb"] = (qt * w["eb"]).astype(BF16)
            w["khb"] = (kf * w["ekb"]).astype(BF16)
            w["dec"] = jnp.exp(bl)
            w["vb"] = v.astype(BF16)
            w["dob"] = do.astype(BF16)
            w["scores"] = [_hgrn_intra_bwd_scores(qt[rs], kf[rs], b[rs], v[rs], do[rs], mbd) for rs in subs]
            w["st_adds"] = [_dot_tn(w["vb"][rs], w["khb"][rs]) for rs in subs]
            w["gst_adds"] = [_dot_tn(w["dob"][rs], w["qhb"][rs]) for rs in subs]
        for pair, w in enumerate(work):
            w["st"] = s0_ref[0, pair]
        for sub in range(nsub):
            for pair, w in enumerate(work):
                sts_ref[pair, sub] = w["st"]
                w["st"] = jnp.where(bdmask, w["st"] * w["dec"][SUB * sub:SUB * sub + 1, :] + w["st_adds"][sub], 0.0)
        for pair, w in enumerate(work):
            w["gst"] = dst_ref[pair]
            w["dqt_p"], w["dkf_p"], w["dv_p"], w["dbl_p"] = ([None] * nsub for _ in range(4))
        for sub in reversed(range(nsub)):
            rs = subs[sub]
            for pair, w in enumerate(work):
                gst = w["gst"]
                st_in = sts_ref[pair, sub]
                gb = gst.astype(BF16)
                dqh = _dot(w["dob"][rs], st_in.astype(BF16))
                dkh = _dot(w["vb"][rs], gb)
                dv_inter = _dot_nt(w["khb"][rs], gb)
                ddec = jnp.sum(gst * st_in, axis=0, keepdims=True)
                dec_row = w["dec"][SUB * sub:SUB * sub + 1, :]
                w["gst"] = jnp.where(bdmask, gst * dec_row + w["gst_adds"][sub], 0.0)
                dqt_i, dkf_i, dv_i = _hgrn_intra_bwd_grads(w["scores"][sub], w["qt"][rs], w["do"][rs], rsum)
                dkf_inter = dkh * w["ekb"][rs]
                w["dqt_p"][sub] = dqh * w["eb"][rs] + dqt_i
                w["dkf_p"][sub] = dkf_inter + dkf_i
                w["dv_p"][sub] = dv_inter + dv_i
                row = jnp.sum(w["kf"][rs] * dkf_inter, axis=0, keepdims=True) + ddec * dec_row
                w["dbl_p"][sub] = jnp.broadcast_to(row, (SUB, 128))
        for pair, w in enumerate(work):
            sl, lbp, sg, sgn, f = w["sl"], w["lbp"], w["sg"], w["sgn"], w["f"]
            dst_ref[pair] = w["gst"]
            dqt = jnp.concatenate(w["dqt_p"], axis=0)
            dkf = jnp.concatenate(w["dkf_p"], axis=0)
            dv = jnp.concatenate(w["dv_p"], axis=0)
            dg = _dot3_left(usub, w["qt"] * dqt - w["kf"] * dkf) + jnp.concatenate(w["dbl_p"], axis=0)
            df = jnp.where(f > F_FLOOR, dg / f, 0.0)
            dlb_ref[:, sl] += jnp.sum(df * (1.0 - sg) - dkf * sgn, axis=0, keepdims=True)
            dfl = (1.0 - lbp) * sg * sgn * (df - dkf)
            dq = dqt * Q_SCALE * w["dsq"]
            db_ref[:, 0 * B_WIDTH + 128 * pair:0 * B_WIDTH + 128 * pair + 128] = dq.astype(BF16)
            db_ref[:, 1 * B_WIDTH + 128 * pair:1 * B_WIDTH + 128 * pair + 128] = dfl.astype(BF16)
            db_ref[:, 2 * B_WIDTH + 128 * pair:2 * B_WIDTH + 128 * pair + 128] = dv.astype(BF16)
            db_ref[:, 3 * B_WIDTH + 128 * pair:3 * B_WIDTH + 128 * pair + 128] = w["dz"].astype(BF16)

    rev = lambda c: pl.BlockSpec((CHUNK, B_WIDTH), lambda i, c=c: (n - 1 - i, c // B_WIDTH))
    full = lambda a: pl.BlockSpec(a.shape, lambda i, n_=a.ndim: (0,) * n_)
    acc = lambda s: pl.BlockSpec(s, lambda i, n_=len(s): (0,) * n_)
    return _Part(body, (proj, proj, proj, proj, dy, o_saved, s0, lb, onorm),
                 [rev(COL_BQ), rev(COL_BF), rev(COL_BI), rev(COL_BZ),
                  pl.BlockSpec((CHUNK, B_WIDTH), lambda i: (n - 1 - i, 1)),
                  pl.BlockSpec((CHUNK, B_WIDTH), lambda i: (n - 1 - i, 0)),
                  pl.BlockSpec((1, 2, 128, 128), lambda i: (n - 1 - i, 0, 0, 0)), full(lb), full(onorm)],
                 [pl.BlockSpec((CHUNK, 4 * B_WIDTH), lambda i: (n - 1 - i, 0)), acc((1, B_WIDTH)), acc((1, 128))],
                 [SDS((T, 4 * B_WIDTH), BF16), SDS((1, B_WIDTH), F32), SDS((1, 128), F32)],
                 [pltpu.VMEM((2, 128, 128), F32), pltpu.VMEM((2, nsub, 128, 128), F32)])


def _lb_fwd(hgrn_lb):
    assert hgrn_lb.shape[0] == 2

    def body(x_ref, o_ref):
        x0, x1 = x_ref[0:1, :], x_ref[1:2, :]
        m = jnp.maximum(x0, x1)
        e0, e1 = jnp.exp(x0 - m), jnp.exp(x1 - m)
        p0, p1 = e0 / (e0 + e1), e1 / (e0 + e1)
        o_ref[0:1, :] = jnp.clip(p0 - p0, 0.0, 1.0 - 1e-6)
        o_ref[1:2, :] = jnp.clip((p0 + p1) - p0, 0.0, 1.0 - 1e-6)

    return pl.pallas_call(body, name="lb_fwd", out_shape=SDS(hgrn_lb.shape, F32))(hgrn_lb)


def _lb_bwd(hgrn_lb, dlb):
    def body(x_ref, d_ref, o_ref):
        x0, x1 = x_ref[0:1, :], x_ref[1:2, :]
        m = jnp.maximum(x0, x1)
        e0, e1 = jnp.exp(x0 - m), jnp.exp(x1 - m)
        p0, p1 = e0 / (e0 + e1), e1 / (e0 + e1)
        val = (p0 + p1) - p0
        dp1 = jnp.where((val > 0.0) & (val < 1.0 - 1e-6), d_ref[1:2, :], 0.0)
        inner = p1 * dp1
        o_ref[0:1, :] = p0 * (0.0 - inner)
        o_ref[1:2, :] = p1 * (dp1 - inner)

    return pl.pallas_call(body, name="lb_bwd", out_shape=SDS(hgrn_lb.shape, F32))(hgrn_lb, dlb)


def _fox_prep(proj, bf):
    T = proj.shape[0]
    n = T // CHUNK

    def body(q0_ref, q1_ref, k0_ref, k1_ref, v0_ref, v1_ref, fl_ref, bf_ref, qo_ref, ko_ref, vt_ref, carry_ref):
        for p, v_ref in enumerate((v0_ref, v0_ref, v1_ref, v1_ref)):
            vt_ref[p, 0] = v_ref[:, 128 * (p % 2):128 * (p % 2) + 128].T.astype(BF16)

        @pl.when(pl.program_id(0) == 0)
        def _():
            carry_ref[...] = jnp.zeros_like(carry_ref)

        ltri = jnp.where(_lane((CHUNK, CHUNK)) <= _row((CHUNK, CHUNK)), 1.0, 0.0).astype(BF16)
        lf = jax.nn.log_sigmoid(fl_ref[...] + bf_ref[...])
        c = _dot3_left(ltri, lf) + carry_ref[...]
        carry_ref[...] = c[CHUNK - 1:CHUNK, :]
        lane = _lane((CHUNK, 128))
        feat = lane < 64
        ones_q = (lane >= 67) & (lane <= 69)
        ones_k = (lane >= 64) & (lane <= 66)
        qrefs, krefs = (q0_ref, q1_ref), (k0_ref, k1_ref)
        for h in range(C_HEADS):
            blk = slice(128 * ((h // 2) % 2), 128 * ((h // 2) % 2) + 128)
            qp, kp = qrefs[h // 4][:, blk], krefs[h // 4][:, blk]
            if h % 2:
                qp, kp = pltpu.roll(qp, 64, axis=1), pltpu.roll(kp, 64, axis=1)
            ch = jnp.broadcast_to(c[:, h:h + 1], (CHUNK, 128))
            hi = ch.astype(BF16).astype(F32)
            r1 = ch - hi
            mid = r1.astype(BF16).astype(F32)
            lo = r1 - mid
            aq = jnp.where(lane == 64, hi, jnp.where(lane == 65, mid, jnp.where(lane == 66, lo,
                           jnp.where(ones_q, 1.0, 0.0))))
            ak = jnp.where(lane == 67, -hi, jnp.where(lane == 68, -mid, jnp.where(lane == 69, -lo,
                           jnp.where(ones_k, 1.0, 0.0))))
            qo_ref[:, 128 * h:128 * h + 128] = jnp.where(feat, qp * Q_SCALE, aq).astype(BF16)
            ko_ref[:, 128 * h:128 * h + 128] = jnp.where(feat, kp, ak).astype(BF16)

    w = 256
    col = lambda c: pl.BlockSpec((CHUNK, w), lambda i, c=c: (i, c // w))
    return _Part(body, (proj, proj, proj, proj, proj, proj, proj, bf),
                 [col(COL_CQ), col(COL_CQ + w), col(COL_CK), col(COL_CK + w), col(COL_CV), col(COL_CV + w),
                  pl.BlockSpec((CHUNK, 128), lambda i: (i, COL_CF // 128)), pl.BlockSpec((1, 128), lambda i: (0, 0))],
                 [pl.BlockSpec((CHUNK, C_HEADS * 128), lambda i: (i, 0))] * 2
                 + [pl.BlockSpec((C_HEADS // 2, 1, 128, CHUNK), lambda i: (0, i, 0, 0))],
                 [SDS((T, C_HEADS * 128), BF16)] * 2 + [SDS((C_HEADS // 2, n, 128, CHUNK), BF16)],
                 [pltpu.VMEM((1, 128), F32)])


FOX_TILE = 512
FOX_KEYS = 512
FOX_STRIP = 16


def _fox_mask(tk, tq, k0, q0):
    return (_row((tk, tq)) + (k0 - q0)) <= _lane((tk, tq))


def _ride_refs(ride, rest, n_out, n_scratch):
    n = ride.n if ride else 0
    srcs, rest = rest[:n], rest[n:]
    outs, rest = rest[:n_out], rest[n_out:]
    dsts, rest = rest[:n], rest[n:]
    return srcs, outs, dsts, rest[:n_scratch], rest[n_scratch:]


def _ride_start(ride, grid, srcs, dsts, sems):
    if ride:
        first = functools.reduce(lambda a, b: a & b, [pl.program_id(d) == 0 for d in range(len(grid))])
        pl.when(first)(lambda: ride.start(srcs, dsts, sems))


def _ride_wait(ride, grid, srcs, dsts, sems):
    if ride:
        last = functools.reduce(lambda a, b: a & b, [pl.program_id(d) == n - 1 for d, n in enumerate(grid)])
        pl.when(last)(lambda: ride.wait(srcs, dsts, sems))


def _fox_fwd(qt, kt, vt, proj, tag, ride=None):
    T = proj.shape[0]
    tq, tk = _tile(T, FOX_TILE), _tile(T, FOX_KEYS)
    nq, nsub = T // tq, tk // CHUNK
    npair = C_HEADS // 2

    def body(q_ref, k_ref, vt_ref, z_ref, *rest):
        ride_srcs, (o_ref, lse_ref, y_ref), ride_dsts, (acc_ref, st_ref, pt_ref), ride_sems = _ride_refs(ride, rest, 3, 3)
        i = pl.program_id(1)
        _ride_start(ride, (npair, nq), ride_srcs, ride_dsts, ride_sems)

        qs = (q_ref[:, 0:128], q_ref[:, 128:256])
        acc_ref[...] = jnp.zeros_like(acc_ref)
        pt_ref[...] = jnp.zeros_like(pt_ref)
        nfull = (i * tq) // tk

        def scores(j):
            kb = k_ref[pl.ds(pl.multiple_of(j * tk, tk), tk), :]
            return tuple(_dot_nt(kb[:, 128 * h:128 * h + 128], qs[h]) for h in range(2))

        def weigh(j, h):
            rows = slice(64 * h, 64 * h + 64)
            vth = jnp.concatenate([vt_ref[0, nsub * j + c, rows, :] for c in range(nsub)], axis=1)
            return _dot(vth, pt_ref[h])

        def block(j, carry, diagonal):
            nxt = () if diagonal else scores(j + 1)
            pvs = [weigh(jnp.maximum(j - 1, 0), h) for h in range(2)]
            new = []
            for h in range(2):
                m, l, alpha_prev = carry[3 * h:3 * h + 3]
                st = st_ref[h]
                if diagonal:
                    st = jnp.where(_fox_mask(tk, tq, j * tk, i * tq), st, -jnp.inf)
                m_new = jnp.maximum(m, _colreduce(st, jnp.maximum))
                pt = jnp.exp(st - m_new)
                alpha = jnp.exp(m - m_new)
                rows = slice(64 * h, 64 * h + 64)
                acc_ref[rows, :] = alpha_prev * acc_ref[rows, :] + pvs[h]
                pt_ref[h] = pt.astype(BF16)
                new += [m_new, alpha * l + _colreduce(pt, jnp.add), alpha]
            for h, st in enumerate(nxt):
                st_ref[h] = st
            return tuple(new)

        for h, st in enumerate(scores(0)):
            st_ref[h] = st
        init = (jnp.full((1, tq), -jnp.inf, F32), jnp.zeros((1, tq), F32), jnp.ones((1, tq), F32)) * 2
        carry = lax.fori_loop(0, nfull, lambda j, c: block(j, c, False), init)
        m0, l0, a0, m1, l1, a1 = block(nfull, carry, True)
        for h, alpha in enumerate((a0, a1)):
            rows = slice(64 * h, 64 * h + 64)
            acc_ref[rows, :] = alpha * acc_ref[rows, :] + weigh(nfull, h)
        inv = jnp.where(_row((128, tq)) < 64, 1.0 / l0, 1.0 / l1)
        o = (acc_ref[...] * inv).T
        o_ref[...] = o
        r8 = _row((8, tq))
        lse_ref[0, 0] = jnp.where(r8 == 0, m0 + jnp.log(l0), jnp.where(r8 == 1, m1 + jnp.log(l1), 0.0))
        sz, _ = _silu_and_grad(z_ref[...])
        y_ref[...] = (o * sz).astype(BF16)
        _ride_wait(ride, (npair, nq), ride_srcs, ride_dsts, ride_sems)

    blk = pl.BlockSpec((tq, 128), lambda p, i: (i, p))
    extra = ride or _ChipExchange("gather", ())
    return pl.pallas_call(
        body, name=f"fox_fwd_{tag}", grid=(npair, nq),
        in_specs=[pl.BlockSpec((tq, 256), lambda p, i: (i, p)), pl.BlockSpec((T, 256), lambda p, i: (0, p)),
                  pl.BlockSpec((1, T // CHUNK, 128, CHUNK), lambda p, i: (p, 0, 0, 0)),
                  pl.BlockSpec((tq, 128), lambda p, i: (i, COL_CZ // 128 + p))] + extra.in_specs,
        out_specs=[blk, pl.BlockSpec((1, 1, 8, tq), lambda p, i: (p, i, 0, 0)), blk] + extra.out_specs,
        out_shape=[SDS((T, C_WIDTH), F32), SDS((npair, nq, 8, tq), F32), SDS((T, C_WIDTH), BF16)] + extra.out_shape,
        scratch_shapes=[pltpu.VMEM((128, tq), F32), pltpu.VMEM((2, tk, tq), F32), pltpu.VMEM((2, tk, tq), BF16)]
        + (extra.scratch if ride else []),
        compiler_params=pltpu.CompilerParams(dimension_semantics=("arbitrary", "arbitrary"), vmem_limit_bytes=VMEM_LIMIT,
                                             has_side_effects=bool(ride)),
    )(qt, kt, vt, proj, *extra.sources)


def _fox_bwd(qt, kt, proj, do, lse, delta, dot, qtr, tag, ride=None):
    T = proj.shape[0]
    tq, tk = _tile(T, FOX_TILE), _tile(T, FOX_KEYS)
    nq, nk = T // tq, T // tk
    assert tq == tk
    npair = C_HEADS // 2

    def body(q_ref, k_ref, v_ref, do_ref, lse_ref, dl_ref, dot_ref, qtr_ref, *rest):
        ride_srcs, (dq_ref, dk_ref, dv_ref), ride_dsts, scratch, ride_sems = _ride_refs(ride, rest, 3, 4)
        dvt_ref, dkt_ref, pt_ref, ds_ref = scratch
        j = pl.program_id(1)
        first = (j * tk) // tq
        _ride_start(ride, (npair, nk), ride_srcs, ride_dsts, ride_sems)

        @pl.when(j == 0)
        def _():
            dq_ref[...] = jnp.zeros_like(dq_ref)

        dkt_ref[...] = jnp.zeros_like(dkt_ref)
        dvt_ref[...] = jnp.zeros_like(dvt_ref)
        ks = (k_ref[:, 0:128], k_ref[:, 128:256])
        kts = tuple(k.astype(F32).T.astype(BF16) for k in ks)
        vb = v_ref[...].astype(BF16)
        lo = _lane((tq, 128)) < 64

        def operands(i):
            q0 = pl.multiple_of(i * tq, tq)
            qb = q_ref[pl.ds(q0, tq), :]
            dob = do_ref[pl.ds(q0, tq), :]
            qhs = (qb[:, 0:128], qb[:, 128:256])
            dohs = (jnp.where(lo, dob, jnp.zeros_like(dob)), jnp.where(lo, jnp.zeros_like(dob), dob))
            return qhs, dohs

        def scores(i):
            qhs, dohs = operands(i)
            return tuple((_dot_nt(ks[h], qhs[h]), _dot_nt(vb, dohs[h])) for h in range(2))

        def grads(i, slot):
            for h in range(2):
                rows = slice(64 * h, 64 * h + 64)
                dvt_ref[rows, :] += _dot_nt(dot_ref[0, i, rows, :], pt_ref[slot, h])
                dkt_ref[h] += _dot_nt(qtr_ref[h, i], ds_ref[slot, h])
                dq_ref[h, i] += _dot(kts[h], ds_ref[slot, h])

        def block(i, slot, diagonal, opening):
            sc = scores(i)
            if not opening:
                grads(i - 1, 1 - slot)
            lsev = lse_ref[0, i]
            dlv = dl_ref[0, i]
            for h in range(2):
                lseh = jnp.broadcast_to(lsev[h:h + 1, :], (FOX_STRIP, tq))
                dlh = jnp.broadcast_to(dlv[h:h + 1, :], (FOX_STRIP, tq))
                for r in range(0, tk, FOX_STRIP):
                    rows = slice(r, r + FOX_STRIP)
                    pt = jnp.exp(sc[h][0][rows, :] - lseh)
                    if diagonal:
                        pt = jnp.where(_fox_mask(FOX_STRIP, tq, r, 0), pt, 0.0)
                    ds_ref[slot, h, rows, :] = (pt * (sc[h][1][rows, :] - dlh)).astype(BF16)
                    pt_ref[slot, h, rows, :] = pt.astype(BF16)

        block(first, 0, True, True)
        rest = nq - 1 - first

        def two_steps(t, carry):
            block(first + 1 + 2 * t, 1, False, False)
            block(first + 2 + 2 * t, 0, False, False)
            return carry

        lax.fori_loop(0, rest // 2, two_steps, 0)
        pl.when(rest % 2 == 1)(lambda: block(nq - 1, 1, False, False))
        grads(nq - 1, rest % 2)
        dv_ref[...] = dvt_ref[...].T.astype(BF16)
        for h in range(2):
            dk_ref[:, 128 * h:128 * h + 128] = dkt_ref[h].T
        _ride_wait(ride, (npair, nk), ride_srcs, ride_dsts, ride_sems)

    full = lambda w: pl.BlockSpec((T, w), lambda p, j: (0, p))
    stat = pl.BlockSpec((1, nq, 8, tq), lambda p, j: (p, 0, 0, 0))
    extra = ride or _ChipExchange("gather", ())
    return pl.pallas_call(
        body, name=f"fox_bwd_{tag}", grid=(npair, nk),
        in_specs=[full(256), pl.BlockSpec((tk, 256), lambda p, j: (j, p)),
                  pl.BlockSpec((tk, 128), lambda p, j: (j, COL_CV // 128 + p)), full(128), stat, stat,
                  pl.BlockSpec((1, nq, 128, tq), lambda p, j: (p, 0, 0, 0)),
                  pl.BlockSpec((2, nq, 128, tq), lambda p, j: (p, 0, 0, 0))] + extra.in_specs,
        out_specs=[pl.BlockSpec((2, nq, 128, tq), lambda p, j: (p, 0, 0, 0)), pl.BlockSpec((tk, 256), lambda p, j: (j, p)),
                   pl.BlockSpec((tk, 128), lambda p, j: (j, p))] + extra.out_specs,
        out_shape=[SDS((C_HEADS, nq, 128, tq), F32), SDS((T, C_HEADS * 128), F32), SDS((T, C_WIDTH), BF16)]
        + extra.out_shape,
        scratch_shapes=[pltpu.VMEM((128, tk), F32), pltpu.VMEM((2, 128, tk), F32),
                        pltpu.VMEM((2, 2, tk, tq), BF16), pltpu.VMEM((2, 2, tk, tq), BF16)]
        + (extra.scratch if ride else []),
        compiler_params=pltpu.CompilerParams(dimension_semantics=("arbitrary", "arbitrary"), vmem_limit_bytes=VMEM_LIMIT,
                                             has_side_effects=bool(ride)),
    )(qt, kt, proj, do, lse, delta, dot, qtr, *extra.sources)


def _fox_bwd_post(dqt, dkt, proj, bf, tag):
    T = proj.shape[0]
    tq = _tile(T, FOX_TILE)
    n = T // tq

    def body(dq_ref, dk_ref, fl_ref, bf_ref, oq_ref, ok_ref, ofl_ref, dbf_ref, carry_ref):
        @pl.when(pl.program_id(0) == 0)
        def _():
            carry_ref[...] = jnp.zeros_like(carry_ref)
            dbf_ref[...] = jnp.zeros_like(dbf_ref)

        lane = _lane((tq, 128))
        lo = lane < 64
        dqs = [dq_ref[h, 0].T for h in range(C_HEADS)]
        dc = jnp.zeros((tq, 128), F32)
        for h in range(C_HEADS):
            dc = dc + jnp.where(lane == h, dqs[h][:, 64:65] - dk_ref[:, 128 * h + 67:128 * h + 68], 0.0)
        utri = jnp.where(_lane((tq, tq)) >= _row((tq, tq)), 1.0, 0.0).astype(BF16)
        dlf = _dot3_left(utri, dc) + carry_ref[...]
        carry_ref[...] = dlf[0:1, :]
        dfl = jnp.where(lane < C_HEADS, dlf * _sigmoid(-(fl_ref[...] + bf_ref[...])), 0.0)
        ofl_ref[...] = dfl.astype(BF16)
        dbf_ref[...] += jnp.sum(dfl, axis=0, keepdims=True)
        for p in range(C_HEADS // 2):
            a, b = 128 * (2 * p), 128 * (2 * p + 1)
            oq_ref[:, 128 * p:128 * p + 128] = (
                jnp.where(lo, dqs[2 * p], pltpu.roll(dqs[2 * p + 1], 64, axis=1)) * Q_SCALE).astype(BF16)
            ok_ref[:, 128 * p:128 * p + 128] = jnp.where(
                lo, dk_ref[:, a:a + 128], pltpu.roll(dk_ref[:, b:b + 128], 64, axis=1)).astype(BF16)

    rev = lambda w: pl.BlockSpec((tq, w), lambda i: (n - 1 - i, 0))
    return pl.pallas_call(
        body, name=f"fox_bwd_post_{tag}", grid=(n,),
        in_specs=[pl.BlockSpec((C_HEADS, 1, 128, tq), lambda i: (0, n - 1 - i, 0, 0)), rev(C_HEADS * 128),
                  pl.BlockSpec((tq, 128), lambda i: (n - 1 - i, COL_CF // 128)), pl.BlockSpec((1, 128), lambda i: (0, 0))],
        out_specs=[rev(C_WIDTH), rev(C_WIDTH), rev(128), pl.BlockSpec((1, 128), lambda i: (0, 0))],
        out_shape=[SDS((T, C_WIDTH), BF16), SDS((T, C_WIDTH), BF16), SDS((T, 128), BF16), SDS((1, 128), F32)],
        scratch_shapes=[pltpu.VMEM((1, 128), F32)], compiler_params=_params("arbitrary"),
    )(dqt, dkt, proj, bf)


def _adamw_math(w, g, m, v):
    m = ADAM_B1 * m + (1.0 - ADAM_B1) * g
    v = ADAM_B2 * v + (1.0 - ADAM_B2) * (g * g)
    delta = -ADAM_LR * ((m / ADAM_C1) / (jnp.sqrt(v / ADAM_C2) + ADAM_EPS) + ADAM_WD * w)
    return delta, m, v


def _adamw_pair(w, m, v, ga, gb, name):
    n0 = w.shape[0]
    most = max(1, ADAMW_BLOCK_BYTES // (4 * math.prod(w.shape[1:])))
    t0 = max(t for t in range(1, min(n0, most) + 1) if n0 % t == 0)

    def body(w_ref, m_ref, v_ref, ga_ref, gb_ref, g_ref, d_ref, nm_ref, nv_ref):
        g = ga_ref[...] + gb_ref[...]
        g_ref[...] = g
        d_ref[...], nm_ref[...], nv_ref[...] = _adamw_math(w_ref[...], g, m_ref[...], v_ref[...])

    blk = pl.BlockSpec((t0,) + w.shape[1:], lambda i: (i, 0, 0))
    return pl.pallas_call(
        body, name=name, grid=(n0 // t0,), in_specs=[blk] * 5, out_specs=[blk] * 4,
        out_shape=[SDS(w.shape, F32)] * 4, compiler_params=_params("parallel"),
    )(w, m, v, ga, gb)


def _small_layout():
    L = DEPTH
    lanes = lambda j: slice(128 * j, 128 * j + 128)
    wide = lambda n: [((slice(l, l + 1), lanes(j)), n * l + j, 1, 0, 128) for l in range(L) for j in range(n)]
    halves = [((l, slice(g, g + 1)), 2 * l + g // 2, 1, 64 * (g % 2), 64) for l in range(L) for g in range(A_GROUPS)]
    side_by_side = lambda w: [((slice(l, l + 1),), 0, 1, w * l, w) for l in range(L)]
    return [
        ((L, D_MODEL), wide(D_MODEL // 128)), ((L, A_GROUPS, 64), halves), ((L, A_GROUPS, 64), halves),
        ((L * A_GROUPS * CHUNK, CHUNK), [((slice(None),), 0, L * A_GROUPS * CHUNK, 0, CHUNK)]),
        ((L, A_GROUPS, CHUNK), [((l,), A_GROUPS * l, A_GROUPS, 0, CHUNK) for l in range(L)]),
        ((L, B_WIDTH), wide(B_WIDTH // 128)), ((L, 64), side_by_side(64)), ((L, C_HEADS), side_by_side(C_HEADS)),
        ((1, D_MODEL), [((slice(None), lanes(j)), j, 1, 0, 128) for j in range(D_MODEL // 128)]),
    ]


def _adamw_small(ws, ms, vs, gearly, glate, late):
    offs = _small_offsets()
    layout = _small_layout()
    n = len(ws)
    assert [w.shape for w in ws] == [shape for shape, _ in layout] and 0 in late

    def body(*refs):
        w_refs, m_refs, v_refs = refs[:n], refs[n:2 * n], refs[2 * n:3 * n]
        early_ref, late_ref = refs[3 * n:3 * n + 2]
        outs = refs[3 * n + 2:]

        def total(k):
            rows = offs[k + 1] - offs[k] if k < n else 1

            def block(dev):
                parts = [early_ref[dev, offs[k] - offs[1]:offs[k] - offs[1] + rows, :]] if k else []
                if k in late:
                    parts.append(late_ref[dev, late[k]:late[k] + rows, :])
                return functools.reduce(jnp.add, parts)

            return functools.reduce(jnp.add, [block(dev) for dev in range(N_DEV)])

        for k, (_, pieces) in enumerate(layout):
            g = total(k)
            go_ref, d_ref, nm_ref, nv_ref = outs[4 * k:4 * k + 4]
            for idx, row, rows, lane, width in pieces:
                gp = g[row:row + rows, :]
                if lane:
                    gp = pltpu.roll(gp, 128 - lane, axis=1)
                gp = gp[:, :width]
                go_ref[idx] = gp
                d_ref[idx], nm_ref[idx], nv_ref[idx] = _adamw_math(w_refs[k][idx], gp, m_refs[k][idx], v_refs[k][idx])
        outs[4 * n][...] = total(n)

    shapes = [SDS(w.shape, F32) for w in ws for _ in range(4)] + [SDS((1, 128), F32)]
    res = pl.pallas_call(body, name="adamw_small", out_shape=shapes,
                         compiler_params=pltpu.CompilerParams(vmem_limit_bytes=VMEM_LIMIT))(*ws, *ms, *vs, gearly, glate)
    return [res[4 * k:4 * k + 4] for k in range(n)], res[4 * n]


def _pack_grads(dlng, dlnb, dwm, dbst, dlb, donorm, dbf, dfinal, loss_part):
    offs = _small_offsets()
    base = offs[1]
    L = len(dwm)
    assert L == 2

    def body(*refs):
        lng, lnb, wm, bst, on, bf = (refs[L * a:L * a + L] for a in range(6))
        lb_ref, fin_ref, loss_ref, o_ref = refs[6 * L:]
        o_ref[...] = jnp.zeros_like(o_ref)
        lane = _lane((1, 128))
        for l in range(L):
            for j in range(2):
                o_ref[offs[1] - base + 2 * l + j:offs[1] - base + 2 * l + j + 1, :] = lng[l][:, 128 * j:128 * j + 128]
                o_ref[offs[2] - base + 2 * l + j:offs[2] - base + 2 * l + j + 1, :] = lnb[l][:, 128 * j:128 * j + 128]
                o_ref[offs[5] - base + 2 * l + j:offs[5] - base + 2 * l + j + 1, :] = lb_ref[l:l + 1, 128 * j:128 * j + 128]
            for g in range(A_GROUPS):
                row = offs[3] - base + (A_GROUPS * l + g) * CHUNK
                o_ref[row:row + CHUNK, :] = wm[l][g]
            o_ref[offs[4] - base + A_GROUPS * l:offs[4] - base + A_GROUPS * (l + 1), :] = bst[l][...].T[0:A_GROUPS, :]
        o_ref[offs[6] - base:offs[6] - base + 1, :] = jnp.where(lane < 64, on[0][...], pltpu.roll(on[1][...], 64, axis=1))
        o_ref[offs[7] - base:offs[7] - base + 1, :] = jnp.where(
            lane < C_HEADS, bf[0][...], jnp.where(lane < 2 * C_HEADS, pltpu.roll(bf[1][...], C_HEADS, axis=1), 0.0))
        for j in range(D_MODEL // 128):
            o_ref[offs[8] - base + j:offs[8] - base + j + 1, :] = fin_ref[:, 128 * j:128 * j + 128]
        o_ref[offs[9] - base:offs[9] - base + 1, :] = loss_ref[...]

    rows = offs[9] + 8 - base
    return pl.pallas_call(body, name="pack_grads", out_shape=SDS((rows, 128), F32))(
        *dlng, *dlnb, *dwm, *dbst, *donorm, *dbf, dlb, dfinal, loss_part)


def _sum_chips(layers, name, layer_major):
    groups = [list(layer) if isinstance(layer, (list, tuple)) else [layer] for layer in layers]
    R = groups[0][0].shape[1]
    C = sum(a.shape[2] for a in groups[0])
    L = len(groups)
    tc = _tile(C, 256)
    steps = C // tc
    plan = []
    for l, layer in enumerate(groups):
        assert all(a.shape[2] % tc == 0 for a in layer) and sum(a.shape[2] for a in layer) == C
        firsts = [sum(a.shape[2] for a in layer[:k]) // tc for k in range(len(layer))]
        plan += [(l, first, a.shape[2] // tc) for first, a in zip(firsts, layer)]

    def body(*refs):
        o_ref = refs[-1]
        i = pl.program_id(0)
        for (l, first, n), p_ref in zip(plan, refs[:-1]):
            def write(l=l, p_ref=p_ref):
                p = [p_ref[k].astype(F32) for k in range(N_CHIPS)]
                s = ((p[0] + p[1]) + p[2]) + p[3]
                if layer_major:
                    o_ref[l] = s
                else:
                    o_ref[:, l, :] = s

            if n == steps:
                write()
            else:
                pl.when((i >= first) & (i < first + n))(write)

    out = (L, R, C) if layer_major else (R, L, C)
    out_blk = (L, R, tc) if layer_major else (R, L, tc)
    return pl.pallas_call(
        body, name=name, grid=(steps,),
        in_specs=[pl.BlockSpec((N_CHIPS, R, tc), lambda i, first=first, n=n: (0, 0, jnp.clip(i - first, 0, n - 1)))
                  for _, first, n in plan],
        out_specs=pl.BlockSpec(out_blk, lambda i: (0, 0, i)), out_shape=SDS(out, F32),
        compiler_params=_params("parallel"),
    )(*[a for layer in groups for a in layer])


ANY = pl.BlockSpec(memory_space=pl.ANY)


def _mesh_pos():
    return lax.axis_index("x"), lax.axis_index("y"), lax.axis_index("c")


def _other_chips(x, y):
    return [(1 - x, y), (x, 1 - y), (1 - x, 1 - y)]


class _ChipExchange:
    def __init__(self, mode, sources):
        assert mode in ("gather", "scatter")
        self.mode, self.sources = mode, tuple(sources)
        self.n = len(self.sources)
        self.in_specs = [ANY] * self.n
        self.out_specs = [ANY] * self.n
        self.out_shape = [SDS(((N_CHIPS,) + s.shape) if mode == "gather" else s.shape, s.dtype) for s in self.sources]
        self.scratch = [pltpu.SemaphoreType.DMA((3 * self.n,)), pltpu.SemaphoreType.DMA((3 * self.n,)),
                        pltpu.SemaphoreType.DMA((self.n,))]

    def _copies(self, srcs, dsts, send_sems, recv_sems, local_sems):
        x, y, c = _mesh_pos()
        me = 2 * x + y
        view = (lambda r, chip: r) if self.mode == "gather" else (lambda r, chip: r.at[chip])
        local = [pltpu.make_async_copy(view(s, me), d.at[me], local_sems.at[a]) for a, (s, d) in enumerate(zip(srcs, dsts))]
        sends, recvs = [], []
        for j, (px, py) in enumerate(_other_chips(x, y)):
            peer = 2 * px + py
            for a, (s, d) in enumerate(zip(srcs, dsts)):
                sems = dict(send_sem=send_sems.at[self.n * j + a], recv_sem=recv_sems.at[self.n * j + a],
                            device_id=(px, py, c), device_id_type=MESH_ID)
                sends.append(pltpu.make_async_remote_copy(src_ref=view(s, peer), dst_ref=d.at[me], **sems))
                recvs.append(pltpu.make_async_remote_copy(src_ref=view(s, me), dst_ref=d.at[peer], **sems))
        return local, sends, recvs

    def start(self, srcs, dsts, sems):
        local, sends, _ = self._copies(srcs, dsts, *sems)
        for cp in local + sends:
            cp.start()

    def wait(self, srcs, dsts, sems):
        local, sends, recvs = self._copies(srcs, dsts, *sems)
        for cp in recvs:
            cp.wait_recv()
        for cp in sends:
            cp.wait_send()
        for cp in local:
            cp.wait()


def _gather_halves(w, tag):
    R, C = w.shape
    H = C // 2

    def body(w_ref, g_ref, send_sems, recv_sems, pass_send, pass_recv, local_sem):
        x, y, c = _mesh_pos()
        me = 2 * x + y
        mine, theirs = pl.ds(pl.multiple_of(c * H, H), H), pl.ds(pl.multiple_of((1 - c) * H, H), H)
        own = pltpu.make_async_copy(w_ref, g_ref.at[me], local_sem)
        own.start()

        def fetch(j, px, py, src, dst):
            return pltpu.make_async_remote_copy(src_ref=src, dst_ref=dst, send_sem=send_sems.at[j], recv_sem=recv_sems.at[j],
                                                device_id=(px, py, c), device_id_type=MESH_ID)

        def hand(j, cols, peer):
            return pltpu.make_async_remote_copy(src_ref=g_ref.at[peer, :, cols], dst_ref=g_ref.at[peer, :, cols],
                                                send_sem=pass_send.at[j], recv_sem=pass_recv.at[j],
                                                device_id=(x, y, 1 - c), device_id_type=MESH_ID)

        chips = _other_chips(x, y)
        sends = [fetch(j, px, py, w_ref.at[:, mine], g_ref.at[me, :, mine]) for j, (px, py) in enumerate(chips)]
        for cp in sends:
            cp.start()
        passed = []
        for j, (px, py) in enumerate(chips):
            peer = 2 * px + py
            fetch(j, px, py, w_ref.at[:, mine], g_ref.at[peer, :, mine]).wait_recv()
            passed.append(hand(j, mine, peer))
            passed[-1].start()
        for j, (px, py) in enumerate(chips):
            hand(j, theirs, 2 * px + py).wait_recv()
        for cp in sends + passed:
            cp.wait_send()
        own.wait()

    return pl.pallas_call(
        body, name=f"gather_halves_{tag}", in_specs=[ANY], out_specs=ANY, out_shape=SDS((N_CHIPS, R, C), w.dtype),
        scratch_shapes=[pltpu.SemaphoreType.DMA((3,)), pltpu.SemaphoreType.DMA((3,)), pltpu.SemaphoreType.DMA((3,)),
                        pltpu.SemaphoreType.DMA((3,)), pltpu.SemaphoreType.DMA],
        compiler_params=pltpu.CompilerParams(has_side_effects=True),
    )(w)


class _DeviceGather:
    def __init__(self, source):
        self.sources, self.n = (source,), 1
        self.in_specs, self.out_specs = [ANY], [ANY]
        self.out_shape = [SDS((N_DEV,) + source.shape, source.dtype)]
        self.scratch = [pltpu.SemaphoreType.DMA((N_DEV - 1,)), pltpu.SemaphoreType.DMA((N_DEV - 1,)),
                        pltpu.SemaphoreType.DMA((1,))]

    def _copies(self, srcs, dsts, send_sems, recv_sems, local_sems):
        (src,), (dst,) = srcs, dsts
        x, y, c = _mesh_pos()
        me = 4 * x + 2 * y + c
        local = [pltpu.make_async_copy(src, dst.at[me], local_sems.at[0])]
        sends, recvs = [], []
        for k in range(1, N_DEV):
            px, py, pc = (1 - x) if k & 4 else x, (1 - y) if k & 2 else y, (1 - c) if k & 1 else c
            sems = dict(send_sem=send_sems.at[k - 1], recv_sem=recv_sems.at[k - 1], device_id=(px, py, pc),
                        device_id_type=MESH_ID)
            sends.append(pltpu.make_async_remote_copy(src_ref=src, dst_ref=dst.at[me], **sems))
            recvs.append(pltpu.make_async_remote_copy(src_ref=src, dst_ref=dst.at[4 * px + 2 * py + pc], **sems))
        return local, sends, recvs

    start = _ChipExchange.start
    wait = _ChipExchange.wait


class _Rides:
    def __init__(self, *rides):
        self.rides = rides
        self.n = sum(r.n for r in rides)
        self.sources = tuple(s for r in rides for s in r.sources)
        self.in_specs, self.out_specs = [ANY] * self.n, [ANY] * self.n
        self.out_shape = [s for r in rides for s in r.out_shape]
        self.scratch = [s for r in rides for s in r.scratch]

    def _each(self, srcs, dsts, sems):
        a = b = 0
        for r in self.rides:
            yield r, srcs[a:a + r.n], dsts[a:a + r.n], sems[b:b + len(r.scratch)]
            a, b = a + r.n, b + len(r.scratch)

    def start(self, srcs, dsts, sems):
        for r, s, d, m in self._each(srcs, dsts, sems):
            r.start(s, d, m)

    def wait(self, srcs, dsts, sems):
        for r, s, d, m in self._each(srcs, dsts, sems):
            r.wait(s, d, m)


def _gather_devices(a, name):
    ex = _DeviceGather(a)

    def body(a_ref, g_ref, *sems):
        ex.start((a_ref,), (g_ref,), sems)
        ex.wait((a_ref,), (g_ref,), sems)

    return pl.pallas_call(
        body, name=name, in_specs=ex.in_specs, out_specs=ex.out_specs[0], out_shape=ex.out_shape[0],
        scratch_shapes=ex.scratch, compiler_params=pltpu.CompilerParams(has_side_effects=True),
    )(a)


def _swap_cores(pin, pout):
    def body(pin_ref, pout_ref, oin_ref, oout_ref, send_sems, recv_sems):
        x, y, c = _mesh_pos()
        cps = [pltpu.make_async_remote_copy(src_ref=src, dst_ref=dst, send_sem=send_sems.at[a], recv_sem=recv_sems.at[a],
                                            device_id=(x, y, 1 - c), device_id_type=MESH_ID)
               for a, (src, dst) in enumerate(((pin_ref, oin_ref), (pout_ref, oout_ref)))]
        for cp in cps:
            cp.start()
        for cp in cps:
            cp.wait()

    return pl.pallas_call(
        body, name="swap_cores", in_specs=[ANY, ANY], out_specs=[ANY, ANY],
        out_shape=[SDS(pin.shape, F32), SDS(pout.shape, F32)],
        scratch_shapes=[pltpu.SemaphoreType.DMA((2,)), pltpu.SemaphoreType.DMA((2,))],
        compiler_params=pltpu.CompilerParams(has_side_effects=True),
    )(pin, pout)


PACK_TILE = 8 * 128


def _pack_rows(size):
    return (size + PACK_TILE - 1) // PACK_TILE * 8


def _small_offsets():
    offs = [0]
    for _, shape in SMALL_PARAMS:
        offs.append(offs[-1] + _pack_rows(math.prod(shape)))
    return offs


def _layer_consts(l, gmlp_ln_g, gmlp_ln_b, gmlp_w_s, gmlp_b_s, hgrn_onorm_g, fox_b_f):
    causal = jnp.tril(jnp.ones((CHUNK, CHUNK), bool))
    wm = jnp.where(causal[None], gmlp_w_s[l], 0.0)
    return dict(
        lng=gmlp_ln_g[l].reshape(1, A_WIDTH), lnb=gmlp_ln_b[l].reshape(1, A_WIDTH),
        wm=wm.astype(BF16), wmt=jnp.swapaxes(wm, 1, 2).astype(BF16),
        bst=jnp.pad(gmlp_b_s[l].T, ((0, 0), (0, 128 - A_GROUPS))),
        onorm=jnp.tile(hgrn_onorm_g[l], 4).reshape(1, B_WIDTH),
        bf=jnp.pad(fox_b_f[l], (0, 128 - C_HEADS)).reshape(1, 128),
    )


def kernel(x, norm_g, w_in, w_out, gmlp_ln_g, gmlp_ln_b, gmlp_w_s, gmlp_b_s, hgrn_lb, hgrn_onorm_g, fox_b_f, final_norm_g, loss_target, m_norm_g, m_w_in, m_w_out, m_gmlp_ln_g, m_gmlp_ln_b, m_gmlp_w_s, m_gmlp_b_s, m_hgrn_lb, m_hgrn_onorm_g, m_fox_b_f, m_final_norm_g, v_norm_g, v_w_in, v_w_out, v_gmlp_ln_g, v_gmlp_ln_b, v_gmlp_w_s, v_gmlp_b_s, v_hgrn_lb, v_hgrn_onorm_g, v_fox_b_f, v_final_norm_g):
    T = x.shape[1]
    shard_in = w_in.shape[2]
    shard_out = w_out.shape[1]
    xs = x.reshape(T, D_MODEL)
    tgt = loss_target.reshape(T, D_MODEL)

    w_in_b = [w_in[l].T.astype(BF16) for l in range(DEPTH)]
    w_out_b = w_out.astype(BF16)

    lb_all = _lb_fwd(hgrn_lb)
    consts = [_layer_consts(l, gmlp_ln_g, gmlp_ln_b, gmlp_w_s, gmlp_b_s, hgrn_onorm_g, fox_b_f) for l in range(DEPTH)]

    saved = []
    xl = xs
    w_in_l = _gather_halves(w_in_b[0], "w_in_l0")
    below = None
    for l in range(DEPTH):
        cs = consts[l]
        tag = f"l{l}"
        h, proj, *x_next = _inproj(xl, norm_g[l].reshape(1, D_MODEL), w_in_l, D_IN_PAD, tag, below)
        xl = x_next[0] if x_next else xl
        (ya,), (yb, ob, s0), (qt, kt, vt) = _run_parts(
            [_gmlp_fwd(proj, cs["lng"], cs["lnb"], cs["wm"], cs["bst"]),
             _hgrn_fwd(proj, lb_all[l].reshape(1, B_WIDTH), cs["onorm"]), _fox_prep(proj, cs["bf"])],
            (T // CHUNK,), f"mix_fwd_{tag}")
        ride = _ChipExchange("gather", (w_out_b[l],) + ((w_in_b[l + 1],) if l + 1 < DEPTH else ()))
        oc, lse, yc, *gathered = _fox_fwd(qt, kt, vt, proj, tag, ride)
        w_out_l = gathered[0].reshape(N_CHIPS * shard_out, D_MODEL)
        saved.append(dict(x=xl, h=h, proj=proj, ya=ya, yb=yb, yc=yc, ob=ob, s0=s0, qt=qt, kt=kt, oc=oc, lse=lse,
                          w_in=w_in_l, w_out=w_out_l))
        if l + 1 < DEPTH:
            below = (ya, yb, yc, w_out_l)
            w_in_l = gathered[1]
    dx, loss_part, d_final = _outproj(xl, ya, yb, yc, w_out_l, tag, head=(final_norm_g.reshape(1, D_MODEL), tgt))

    g_small = {}
    dlb_rows, rin, rout = [None] * DEPTH, [None] * DEPTH, [None] * DEPTH
    slabs_in = None
    for l in reversed(range(DEPTH)):
        cs, sv = consts[l], saved[l]
        tag = f"l{l}"
        proj = sv["proj"]
        dy, dw_out, do, delta, dzc, dot, qtr = _outproj_bwd(dx, sv["ya"], sv["yb"], sv["yc"], sv["w_out"], proj, sv["oc"],
                                                            sv["qt"], tag)
        (da, dwm, dbst, dlng, dlnb), (db, dlb_rows[l], donorm) = _run_parts(
            [_gmlp_bwd(proj, dy, cs["lng"], cs["lnb"], cs["wm"], cs["wmt"], cs["bst"]),
             _hgrn_bwd(proj, dy, sv["ob"], sv["s0"], lb_all[l].reshape(1, B_WIDTH), cs["onorm"])],
            (T // CHUNK,), f"mix_bwd_{tag}")
        slabs_out = dw_out.reshape(N_CHIPS, shard_out, D_MODEL).astype(BF16)
        ride = _ChipExchange("scatter", (slabs_out,) + ((slabs_in,) if slabs_in is not None else ()))
        g_small[l] = dict(ln_g=dlng, ln_b=dlnb, w_s=dwm, b_s=dbst, onorm=donorm)
        if l == 0:
            d_hgrn_lb = _lb_bwd(hgrn_lb, jnp.concatenate(dlb_rows, axis=0))
            per_layer = lambda key: [g_small[k][key] for k in range(DEPTH)]
            dbf_known = [jnp.zeros((1, 128), F32)] + [g_small[k]["bf"] for k in range(1, DEPTH)]
            early = _pack_grads(per_layer("ln_g"), per_layer("ln_b"), per_layer("w_s"), per_layer("b_s"), d_hgrn_lb,
                                per_layer("onorm"), dbf_known, d_final, loss_part)
            ride = _Rides(ride, _DeviceGather(early))
        dqt, dkt, dvc, *received = _fox_bwd(sv["qt"], sv["kt"], proj, do, sv["lse"], delta, dot, qtr, tag, ride)
        rout[l] = received[0]
        if slabs_in is not None:
            rin[l + 1] = received[1]
        if l == 0:
            rearly = received[-1]
        dqc, dkc, dflc, g_small[l]["bf"] = _fox_bwd_post(dqt, dkt, proj, cs["bf"], tag)
        dproj = [da, db, dqc, dkc, dvc, dzc, dflc]
        if l == 0:
            ride, parts = None, []
            for n, cols in enumerate(DW_IN_GROUPS):
                part, *arrived = _dw_in(sv["h"], dproj, D_IN_PAD, shard_in, cols, f"{tag}_{n}", ride)
                parts += arrived
                ride = _ChipExchange("scatter", (part,))
        else:
            slabs_in, = _dw_in(sv["h"], dproj, D_IN_PAD, shard_in, (0, D_MODEL), tag)
            ride = None
        dx, dng, *received = _dx_in(sv["x"], norm_g[l].reshape(1, D_MODEL), dx, dproj, sv["w_in"], tag, ride)
        if l == 0:
            rin[0] = parts + received
        g_small[l]["norm_g"] = dng.reshape(D_MODEL // 128, 128)
    grad_x = dx.reshape(x.shape)
    dbf0 = jnp.where(_lane((1, 128)) < C_HEADS, g_small[0]["bf"], 0.0)
    late = jnp.concatenate([g_small[l]["norm_g"] for l in range(DEPTH)] + [jnp.pad(dbf0, ((0, 7), (0, 0)))])
    rlate = _gather_devices(late, "gather_late_grads")
    late_blocks = {0: 0, 7: DEPTH * D_MODEL // 128}

    pin, pout = _sum_chips(rin, "sum_chips_w_in", False), _sum_chips(rout, "sum_chips_w_out", True)
    oin, oout = _swap_cores(pin, pout)
    to_view = lambda a: jnp.transpose(a, (2, 0, 1))
    g_w_in, d_w_in, nm_w_in, nv_w_in = [
        jnp.transpose(o, (1, 2, 0))
        for o in _adamw_pair(to_view(w_in), to_view(m_w_in), to_view(v_w_in), pin, oin, "adamw_w_in")]
    g_w_out, d_w_out, nm_w_out, nv_w_out = _adamw_pair(w_out, m_w_out, v_w_out, pout, oout, "adamw_w_out")

    small_w = [norm_g, gmlp_ln_g, gmlp_ln_b, gmlp_w_s, gmlp_b_s, hgrn_lb, hgrn_onorm_g, fox_b_f, final_norm_g]
    small_m = [m_norm_g, m_gmlp_ln_g, m_gmlp_ln_b, m_gmlp_w_s, m_gmlp_b_s, m_hgrn_lb, m_hgrn_onorm_g, m_fox_b_f, m_final_norm_g]
    small_v = [v_norm_g, v_gmlp_ln_g, v_gmlp_ln_b, v_gmlp_w_s, v_gmlp_b_s, v_hgrn_lb, v_hgrn_onorm_g, v_fox_b_f, v_final_norm_g]
    views = lambda ps: [p.reshape(shape) for p, (shape, _) in zip(ps, _small_layout())]
    per_param, loss_row = _adamw_small(views(small_w), views(small_m), views(small_v), rearly, rlate, late_blocks)
    sg, sd, sm, sv_ = [[per_param[k][a].reshape(shape) for k, (_, shape) in enumerate(SMALL_PARAMS)] for a in range(4)]
    loss = loss_row[0, 0]

    def order(big_in, big_out, small):
        return [small[0], big_in, big_out] + small[1:]

    return (loss, grad_x, *order(g_w_in, g_w_out, sg), *order(d_w_in, d_w_out, sd), *order(nm_w_in, nm_w_out, sm),
            *order(nv_w_in, nv_w_out, sv_))
```
